```python
import jax
import jax.numpy as jnp
from jax import lax
import numpy as np

D_MODEL = 1024
BATCH = 32
SEQ = 2048
DEPTH = 2

CHUNK = 64
D_PLE = 256
N_EVEN = (DEPTH + 1) // 2
N_ODD = DEPTH // 2
D_FF = 2816
A_HEADS = 8
A_KV_HEADS = 2
A_GROUP = A_HEADS // A_KV_HEADS
A_HEAD_DIM = 64
A_WIDTH = A_HEADS * A_HEAD_DIM
A_KV_WIDTH = A_KV_HEADS * A_HEAD_DIM
A_WINDOW = 128
A_PREV_CHUNKS = A_WINDOW // CHUNK
B_WIDTH = 512
B_BLOCKS = 8
B_BLOCK = B_WIDTH // B_BLOCKS
B_CONV = 4
RG_C = 8.0
AB_PROJ = A_WIDTH + 2 * A_KV_WIDTH + 2 * B_WIDTH
C_HEADS = 8
C_HEAD_DIM = 128
C_WIDTH = C_HEADS * C_HEAD_DIM
C_CONV = 4
C_PROJ = 4 * C_WIDTH + 2 * C_HEADS
DN_ALPHA = (2.0 * DEPTH) ** 0.25
DN_BETA = (8.0 * DEPTH) ** -0.25
LN_EPS = 1e-5
NORM_EPS = 1e-6
NEG = -1e30

kernel_name = 'hybrid_swa_rglru_gdn_deepnorm_macaron'


def layer_norm(x, g, b):
    xf = x.astype(jnp.float32)
    mu = jnp.mean(xf, -1, keepdims=True)
    var = jnp.mean(jnp.square(xf - mu), -1, keepdims=True)
    return ((xf - mu) * lax.rsqrt(var + LN_EPS) * g + b).astype(x.dtype)


def swiglu(x, wg, wu, wd):
    return (jax.nn.silu(x @ wg) * (x @ wu)) @ wd


def causal_dwconv(x, w):
    k, s = w.shape[0], x.shape[1]
    xp = jnp.pad(x, ((0, 0), (k - 1, 0), (0, 0)))
    y = xp[:, 0:s] * w[0]
    for j in range(1, k):
        y = y + xp[:, j:j + s] * w[j]
    return y


def chunk_band(t, n_prev):
    b, s = t.shape[:2]
    nc = s // CHUNK
    pad = n_prev * CHUNK
    tp = jnp.pad(t, ((0, 0), (pad, 0), (0, 0), (0, 0)))
    return jnp.concatenate(
        [tp[:, j * CHUNK:j * CHUNK + s].reshape(b, nc, CHUNK, *t.shape[2:]) for j in range(n_prev + 1)],
        axis=2)


def alibi_slopes(n):
    return 2.0 ** (-8.0 * jnp.arange(1, n + 1, dtype=jnp.float32) / n)


def sliding_window_sink_attention(q, k, v, sinks):
    b, s = q.shape[:2]
    nc = s // CHUNK
    pad = A_PREV_CHUNKS * CHUNK
    nk = pad + CHUNK
    qb = q.reshape(b, nc, CHUNK, A_KV_HEADS, A_GROUP, A_HEAD_DIM)
    kb = chunk_band(k, A_PREV_CHUNKS)
    vb = chunk_band(v, A_PREV_CHUNKS)
    sc = jnp.einsum('bnckgd,bnskd->bnkgcs', qb, kb).astype(jnp.float32) * (A_HEAD_DIM ** -0.5)
    dist = jnp.abs(jnp.arange(CHUNK)[:, None] + pad - jnp.arange(nk)[None, :]).astype(jnp.float32)
    slopes = alibi_slopes(A_HEADS).reshape(A_KV_HEADS, A_GROUP)
    valid = (jnp.arange(nc)[:, None] * CHUNK + jnp.arange(nk)[None, :] - pad) >= 0
    sc = sc - slopes[:, :, None, None] * dist
    sc = jnp.where(valid[None, :, None, None, None, :], sc, NEG)
    sink = sinks.astype(jnp.float32).reshape(A_KV_HEADS, A_GROUP)[:, :, None]
    m = jnp.maximum(sc.max(-1), sink)
    pr = jnp.exp(sc - m[..., None])
    den = pr.sum(-1) + jnp.exp(sink - m)
    o = jnp.einsum('bnkgcs,bnskd->bnckgd', pr / den[..., None], vb.astype(jnp.float32))
    return o.reshape(b, s, A_WIDTH).astype(q.dtype)


def rg_lru(x, w_a, b_a, w_x, b_x, lam):
    xb = x.reshape(*x.shape[:2], B_BLOCKS, B_BLOCK)
    r = jax.nn.sigmoid(jnp.einsum('bshi,hij->bshj', xb, w_a).reshape(x.shape) + b_a)
    i = jax.nn.sigmoid(jnp.einsum('bshi,hij->bshj', xb, w_x).reshape(x.shape) + b_x)
    log_a = (-RG_C * r * jax.nn.softplus(-lam)).astype(jnp.float32)
    a = jnp.exp(log_a)
    u = jnp.sqrt(-jnp.expm1(2.0 * log_a)) * (i * x).astype(jnp.float32)

    def combine(c1, c2):
        a1, b1 = c1
        a2, b2 = c2
        return a1 * a2, a2 * b1 + b2

    _, h = lax.associative_scan(combine, (a, u), axis=1)
    return h.astype(x.dtype)


def mixer_ab(x, w_in, sinks, conv_w, conv_b, w_a, b_a, w_x, b_x, lam, w_out):
    b, s = x.shape[:2]
    proj = x @ w_in
    o1 = A_WIDTH
    o2 = o1 + A_KV_WIDTH
    o3 = o2 + A_KV_WIDTH
    o4 = o3 + B_WIDTH
    q = proj[..., :o1].reshape(b, s, A_HEADS, A_HEAD_DIM)
    k = proj[..., o1:o2].reshape(b, s, A_KV_HEADS, A_HEAD_DIM)
    v = proj[..., o2:o3].reshape(b, s, A_KV_HEADS, A_HEAD_DIM)
    bx = proj[..., o3:o4]
    bg = proj[..., o4:]
    ya = sliding_window_sink_attention(q, k, v, sinks)
    bx = causal_dwconv(bx, conv_w) + conv_b
    yb = rg_lru(bx, w_a, b_a, w_x, b_x, lam) * jax.nn.gelu(bg)
    return jnp.concatenate([ya, yb], axis=-1) @ w_out


def gated_delta_rule(q, k, v, g, beta):
    f32 = jnp.float32
    b, s, h, dk = q.shape
    dv = v.shape[-1]
    nc = s // CHUNK

    def to_chunks(t):
        return t.astype(f32).reshape(b, nc, CHUNK, h, -1).transpose(1, 0, 3, 2, 4)

    q = to_chunks(q) * (dk ** -0.5)
    k = to_chunks(k)
    v = to_chunks(v)
    g = g.astype(f32).reshape(b, nc, CHUNK, h).transpose(1, 0, 3, 2)
    beta = beta.astype(f32).reshape(b, nc, CHUNK, h).transpose(1, 0, 3, 2)
    gc = jnp.cumsum(g, axis=-1)
    tril = jnp.tril(jnp.ones((CHUNK, CHUNK), bool))
    strict = jnp.tril(jnp.ones((CHUNK, CHUNK), bool), -1)
    diff = gc[..., :, None] - gc[..., None, :]
    decay = jnp.where(tril, jnp.exp(jnp.where(tril, diff, 0.0)), 0.0)
    kb = k * beta[..., None]
    lmat = jnp.where(strict, jnp.einsum('nbhid,nbhjd->nbhij', kb, k) * decay, 0.0)
    amat = lmat + jnp.eye(CHUNK, dtype=f32)
    u = lax.linalg.triangular_solve(amat, v * beta[..., None], left_side=True, lower=True, unit_diagonal=True)
    w = lax.linalg.triangular_solve(amat, kb * jnp.exp(gc)[..., None], left_side=True, lower=True, unit_diagonal=True)
    attn = jnp.einsum('nbhid,nbhjd->nbhij', q, k) * decay
    qg = q * jnp.exp(gc)[..., None]
    kdec = k * jnp.exp(gc[..., -1:] - gc)[..., None]
    glast = jnp.exp(gc[..., -1])

    def step(state, xs):
        qg_n, kdec_n, w_n, u_n, attn_n, gl_n = xs
        v_new = u_n - jnp.einsum('bhcd,bhde->bhce', w_n, state)
        o = jnp.einsum('bhcd,bhde->bhce', qg_n, state) + jnp.einsum('bhij,bhje->bhie', attn_n, v_new)
        state = state * gl_n[..., None, None] + jnp.einsum('bhcd,bhce->bhde', kdec_n, v_new)
        return state, o

    s0 = jnp.zeros((b, h, dk, dv), f32)
    _, o = lax.scan(step, s0, (qg, kdec, w, u, attn, glast))
    return o.transpose(1, 0, 3, 2, 4).reshape(b, s, h, dv)


def mixer_c(x, w_in, conv_w, a_log, dt_bias, norm_g, w_out):
    b, s = x.shape[:2]
    proj = x @ w_in
    qkv = jax.nn.silu(causal_dwconv(proj[..., :3 * C_WIDTH], conv_w))
    z = proj[..., 3 * C_WIDTH:4 * C_WIDTH].reshape(b, s, C_HEADS, C_HEAD_DIM)
    b_logit = proj[..., 4 * C_WIDTH:4 * C_WIDTH + C_HEADS]
    a_in = proj[..., 4 * C_WIDTH + C_HEADS:]
    q = qkv[..., :C_WIDTH].reshape(b, s, C_HEADS, C_HEAD_DIM).astype(jnp.float32)
    k = qkv[..., C_WIDTH:2 * C_WIDTH].reshape(b, s, C_HEADS, C_HEAD_DIM).astype(jnp.float32)
    v = qkv[..., 2 * C_WIDTH:].reshape(b, s, C_HEADS, C_HEAD_DIM)
    q = q * lax.rsqrt(jnp.sum(q * q, -1, keepdims=True) + NORM_EPS)
    k = k * lax.rsqrt(jnp.sum(k * k, -1, keepdims=True) + NORM_EPS)
    beta = jax.nn.sigmoid(b_logit.astype(jnp.float32))
    g = -jnp.exp(a_log.astype(jnp.float32)) * jax.nn.softplus((a_in + dt_bias).astype(jnp.float32))
    o = gated_delta_rule(q, k, v, g, beta)
    o = o * lax.rsqrt(jnp.mean(o * o, -1, keepdims=True) + NORM_EPS) * norm_g
    o = (o * jax.nn.silu(z.astype(jnp.float32))).astype(x.dtype)
    return o.reshape(b, s, C_WIDTH) @ w_out


def _fwd_setup_inputs(seed: int = 0) -> dict:
    key = jax.random.key(seed)
    ks = iter(jax.random.split(key, 48))
    f32 = jnp.float32

    def nrm(shape, scale):
        return jax.random.normal(next(ks), shape, f32) * scale

    d = D_MODEL
    x = nrm((BATCH, SEQ, d), 1.0)
    p = nrm((DEPTH, BATCH, SEQ, D_PLE), 1.0)
    ffn1_wg = nrm((DEPTH, d, D_FF), d ** -0.5)
    ffn1_wu = nrm((DEPTH, d, D_FF), d ** -0.5)
    ffn1_wd = nrm((DEPTH, D_FF, d), DN_BETA * D_FF ** -0.5)
    ffn2_wg = nrm((DEPTH, d, D_FF), d ** -0.5)
    ffn2_wu = nrm((DEPTH, d, D_FF), d ** -0.5)
    ffn2_wd = nrm((DEPTH, D_FF, d), DN_BETA * D_FF ** -0.5)
    ln_g = 1.0 + nrm((DEPTH, 3, d), 0.02)
    ln_b = nrm((DEPTH, 3, d), 0.02)
    ple_wg = nrm((DEPTH, d, d), d ** -0.5)
    ple_bg = nrm((DEPTH, d), 0.02)
    ple_wp = nrm((DEPTH, D_PLE, d), D_PLE ** -0.5)
    ab_w_in = nrm((N_EVEN, d, AB_PROJ), d ** -0.5)
    a_sinks = nrm((N_EVEN, A_HEADS), 0.5)
    b_conv_w = nrm((N_EVEN, B_CONV, B_WIDTH), B_CONV ** -0.5)
    b_conv_b = nrm((N_EVEN, B_WIDTH), 0.02)
    b_wa = nrm((N_EVEN, B_BLOCKS, B_BLOCK, B_BLOCK), B_BLOCK ** -0.5)
    b_ba = nrm((N_EVEN, B_WIDTH), 0.02)
    b_wx = nrm((N_EVEN, B_BLOCKS, B_BLOCK, B_BLOCK), B_BLOCK ** -0.5)
    b_bx = nrm((N_EVEN, B_WIDTH), 0.02)
    a_c = jax.random.uniform(next(ks), (N_EVEN, B_WIDTH), f32, 0.9, 0.999)
    a0 = a_c ** (1.0 / RG_C)
    b_lam = jnp.log(a0) - jnp.log1p(-a0)
    ab_w_out = nrm((N_EVEN, A_WIDTH + B_WIDTH, d), DN_BETA * (A_WIDTH + B_WIDTH) ** -0.5)
    c_w_in = nrm((N_ODD, d, C_PROJ), d ** -0.5)
    c_conv_w = nrm((N_ODD, C_CONV, 3 * C_WIDTH), C_CONV ** -0.5)
    c_a_log = jnp.log(jax.random.uniform(next(ks), (N_ODD, C_HEADS), f32, 1.0, 16.0))
    dt = jnp.exp(jax.random.uniform(next(ks), (N_ODD, C_HEADS), f32, np.log(1e-3), np.log(1e-1)))
    c_dt_bias = dt + jnp.log(-jnp.expm1(-dt))
    c_norm_g = 1.0 + nrm((N_ODD, C_HEAD_DIM), 0.02)
    c_w_out = nrm((N_ODD, C_WIDTH, d), DN_BETA * C_WIDTH ** -0.5)
    return {'x': x, 'p': p,
            'ffn1_wg': ffn1_wg, 'ffn1_wu': ffn1_wu, 'ffn1_wd': ffn1_wd,
            'ffn2_wg': ffn2_wg, 'ffn2_wu': ffn2_wu, 'ffn2_wd': ffn2_wd,
            'ln_g': ln_g, 'ln_b': ln_b,
            'ple_wg': ple_wg, 'ple_bg': ple_bg, 'ple_wp': ple_wp,
            'ab_w_in': ab_w_in, 'a_sinks': a_sinks,
            'b_conv_w': b_conv_w, 'b_conv_b': b_conv_b,
            'b_wa': b_wa, 'b_ba': b_ba, 'b_wx': b_wx, 'b_bx': b_bx, 'b_lam': b_lam,
            'ab_w_out': ab_w_out,
            'c_w_in': c_w_in, 'c_conv_w': c_conv_w, 'c_a_log': c_a_log, 'c_dt_bias': c_dt_bias,
            'c_norm_g': c_norm_g, 'c_w_out': c_w_out}


def _fwd_reference(x, p, ffn1_wg, ffn1_wu, ffn1_wd, ffn2_wg, ffn2_wu, ffn2_wd, ln_g, ln_b,
              ple_wg, ple_bg, ple_wp, ab_w_in, a_sinks, b_conv_w, b_conv_b,
              b_wa, b_ba, b_wx, b_bx, b_lam, ab_w_out,
              c_w_in, c_conv_w, c_a_log, c_dt_bias, c_norm_g, c_w_out):
    for i in range(DEPTH):
        j = i // 2
        x = layer_norm(DN_ALPHA * x + 0.5 * swiglu(x, ffn1_wg[i], ffn1_wu[i], ffn1_wd[i]), ln_g[i, 0], ln_b[i, 0])
        if i % 2 == 0:
            y = mixer_ab(x, ab_w_in[j], a_sinks[j], b_conv_w[j], b_conv_b[j],
                         b_wa[j], b_ba[j], b_wx[j], b_bx[j], b_lam[j], ab_w_out[j])
        else:
            y = mixer_c(x, c_w_in[j], c_conv_w[j], c_a_log[j], c_dt_bias[j], c_norm_g[j], c_w_out[j])
        x = layer_norm(DN_ALPHA * x + y, ln_g[i, 1], ln_b[i, 1])
        x = layer_norm(DN_ALPHA * x + 0.5 * swiglu(x, ffn2_wg[i], ffn2_wu[i], ffn2_wd[i]), ln_g[i, 2], ln_b[i, 2])
        x = x + jax.nn.sigmoid(x @ ple_wg[i] + ple_bg[i]) * (p[i] @ ple_wp[i])
    return x


import jax as _jax
import jax.numpy as _jnp

TWIN_FORMAT = 'train_step'
FWD_PARAMS = ['x', 'p', 'ffn1_wg', 'ffn1_wu', 'ffn1_wd', 'ffn2_wg', 'ffn2_wu', 'ffn2_wd', 'ln_g', 'ln_b', 'ple_wg', 'ple_bg', 'ple_wp', 'ab_w_in', 'a_sinks', 'b_conv_w', 'b_conv_b', 'b_wa', 'b_ba', 'b_wx', 'b_bx', 'b_lam', 'ab_w_out', 'c_w_in', 'c_conv_w', 'c_a_log', 'c_dt_bias', 'c_norm_g', 'c_w_out']
TWIN_WEIGHTS = ['ffn1_wg', 'ffn1_wu', 'ffn1_wd', 'ffn2_wg', 'ffn2_wu', 'ffn2_wd', 'ln_g', 'ln_b', 'ple_wg', 'ple_bg', 'ple_wp', 'ab_w_in', 'a_sinks', 'b_conv_w', 'b_conv_b', 'b_wa', 'b_ba', 'b_wx', 'b_bx', 'b_lam', 'ab_w_out', 'c_w_in', 'c_conv_w', 'c_a_log', 'c_dt_bias', 'c_norm_g', 'c_w_out']
TWIN_DIFF_INPUT = 'x'
TWIN_INPUTS = ['x', 'p', 'ffn1_wg', 'ffn1_wu', 'ffn1_wd', 'ffn2_wg', 'ffn2_wu', 'ffn2_wd', 'ln_g', 'ln_b', 'ple_wg', 'ple_bg', 'ple_wp', 'ab_w_in', 'a_sinks', 'b_conv_w', 'b_conv_b', 'b_wa', 'b_ba', 'b_wx', 'b_bx', 'b_lam', 'ab_w_out', 'c_w_in', 'c_conv_w', 'c_a_log', 'c_dt_bias', 'c_norm_g', 'c_w_out', 'loss_target', 'm_ffn1_wg', 'm_ffn1_wu', 'm_ffn1_wd', 'm_ffn2_wg', 'm_ffn2_wu', 'm_ffn2_wd', 'm_ln_g', 'm_ln_b', 'm_ple_wg', 'm_ple_bg', 'm_ple_wp', 'm_ab_w_in', 'm_a_sinks', 'm_b_conv_w', 'm_b_conv_b', 'm_b_wa', 'm_b_ba', 'm_b_wx', 'm_b_bx', 'm_b_lam', 'm_ab_w_out', 'm_c_w_in', 'm_c_conv_w', 'm_c_a_log', 'm_c_dt_bias', 'm_c_norm_g', 'm_c_w_out', 'v_ffn1_wg', 'v_ffn1_wu', 'v_ffn1_wd', 'v_ffn2_wg', 'v_ffn2_wu', 'v_ffn2_wd', 'v_ln_g', 'v_ln_b', 'v_ple_wg', 'v_ple_bg', 'v_ple_wp', 'v_ab_w_in', 'v_a_sinks', 'v_b_conv_w', 'v_b_conv_b', 'v_b_wa', 'v_b_ba', 'v_b_wx', 'v_b_bx', 'v_b_lam', 'v_ab_w_out', 'v_c_w_in', 'v_c_conv_w', 'v_c_a_log', 'v_c_dt_bias', 'v_c_norm_g', 'v_c_w_out']
TWIN_OUTPUTS = ['loss', 'grad_x', 'grad_ffn1_wg', 'grad_ffn1_wu', 'grad_ffn1_wd', 'grad_ffn2_wg', 'grad_ffn2_wu', 'grad_ffn2_wd', 'grad_ln_g', 'grad_ln_b', 'grad_ple_wg', 'grad_ple_bg', 'grad_ple_wp', 'grad_ab_w_in', 'grad_a_sinks', 'grad_b_conv_w', 'grad_b_conv_b', 'grad_b_wa', 'grad_b_ba', 'grad_b_wx', 'grad_b_bx', 'grad_b_lam', 'grad_ab_w_out', 'grad_c_w_in', 'grad_c_conv_w', 'grad_c_a_log', 'grad_c_dt_bias', 'grad_c_norm_g', 'grad_c_w_out', 'delta_ffn1_wg', 'delta_ffn1_wu', 'delta_ffn1_wd', 'delta_ffn2_wg', 'delta_ffn2_wu', 'delta_ffn2_wd', 'delta_ln_g', 'delta_ln_b', 'delta_ple_wg', 'delta_ple_bg', 'delta_ple_wp', 'delta_ab_w_in', 'delta_a_sinks', 'delta_b_conv_w', 'delta_b_conv_b', 'delta_b_wa', 'delta_b_ba', 'delta_b_wx', 'delta_b_bx', 'delta_b_lam', 'delta_ab_w_out', 'delta_c_w_in', 'delta_c_conv_w', 'delta_c_a_log', 'delta_c_dt_bias', 'delta_c_norm_g', 'delta_c_w_out', 'new_m_ffn1_wg', 'new_m_ffn1_wu', 'new_m_ffn1_wd', 'new_m_ffn2_wg', 'new_m_ffn2_wu', 'new_m_ffn2_wd', 'new_m_ln_g', 'new_m_ln_b', 'new_m_ple_wg', 'new_m_ple_bg', 'new_m_ple_wp', 'new_m_ab_w_in', 'new_m_a_sinks', 'new_m_b_conv_w', 'new_m_b_conv_b', 'new_m_b_wa', 'new_m_b_ba', 'new_m_b_wx', 'new_m_b_bx', 'new_m_b_lam', 'new_m_ab_w_out', 'new_m_c_w_in', 'new_m_c_conv_w', 'new_m_c_a_log', 'new_m_c_dt_bias', 'new_m_c_norm_g', 'new_m_c_w_out', 'new_v_ffn1_wg', 'new_v_ffn1_wu', 'new_v_ffn1_wd', 'new_v_ffn2_wg', 'new_v_ffn2_wu', 'new_v_ffn2_wd', 'new_v_ln_g', 'new_v_ln_b', 'new_v_ple_wg', 'new_v_ple_bg', 'new_v_ple_wp', 'new_v_ab_w_in', 'new_v_a_sinks', 'new_v_b_conv_w', 'new_v_b_conv_b', 'new_v_b_wa', 'new_v_b_ba', 'new_v_b_wx', 'new_v_b_bx', 'new_v_b_lam', 'new_v_ab_w_out', 'new_v_c_w_in', 'new_v_c_conv_w', 'new_v_c_a_log', 'new_v_c_dt_bias', 'new_v_c_norm_g', 'new_v_c_w_out']
TWIN_LEAF_KINDS = {'loss': 'loss', 'grad_x': 'grad_x', 'grad_ffn1_wg': 'grad_w', 'grad_ffn1_wu': 'grad_w', 'grad_ffn1_wd': 'grad_w', 'grad_ffn2_wg': 'grad_w', 'grad_ffn2_wu': 'grad_w', 'grad_ffn2_wd': 'grad_w', 'grad_ln_g': 'grad_w', 'grad_ln_b': 'grad_w', 'grad_ple_wg': 'grad_w', 'grad_ple_bg': 'grad_w', 'grad_ple_wp': 'grad_w', 'grad_ab_w_in': 'grad_w', 'grad_a_sinks': 'grad_w', 'grad_b_conv_w': 'grad_w', 'grad_b_conv_b': 'grad_w', 'grad_b_wa': 'grad_w', 'grad_b_ba': 'grad_w', 'grad_b_wx': 'grad_w', 'grad_b_bx': 'grad_w', 'grad_b_lam': 'grad_w', 'grad_ab_w_out': 'grad_w', 'grad_c_w_in': 'grad_w', 'grad_c_conv_w': 'grad_w', 'grad_c_a_log': 'grad_w', 'grad_c_dt_bias': 'grad_w', 'grad_c_norm_g': 'grad_w', 'grad_c_w_out': 'grad_w', 'delta_ffn1_wg': 'delta_w', 'delta_ffn1_wu': 'delta_w', 'delta_ffn1_wd': 'delta_w', 'delta_ffn2_wg': 'delta_w', 'delta_ffn2_wu': 'delta_w', 'delta_ffn2_wd': 'delta_w', 'delta_ln_g': 'delta_w', 'delta_ln_b': 'delta_w', 'delta_ple_wg': 'delta_w', 'delta_ple_bg': 'delta_w', 'delta_ple_wp': 'delta_w', 'delta_ab_w_in': 'delta_w', 'delta_a_sinks': 'delta_w', 'delta_b_conv_w': 'delta_w', 'delta_b_conv_b': 'delta_w', 'delta_b_wa': 'delta_w', 'delta_b_ba': 'delta_w', 'delta_b_wx': 'delta_w', 'delta_b_bx': 'delta_w', 'delta_b_lam': 'delta_w', 'delta_ab_w_out': 'delta_w', 'delta_c_w_in': 'delta_w', 'delta_c_conv_w': 'delta_w', 'delta_c_a_log': 'delta_w', 'delta_c_dt_bias': 'delta_w', 'delta_c_norm_g': 'delta_w', 'delta_c_w_out': 'delta_w', 'new_m_ffn1_wg': 'new_m', 'new_m_ffn1_wu': 'new_m', 'new_m_ffn1_wd': 'new_m', 'new_m_ffn2_wg': 'new_m', 'new_m_ffn2_wu': 'new_m', 'new_m_ffn2_wd': 'new_m', 'new_m_ln_g': 'new_m', 'new_m_ln_b': 'new_m', 'new_m_ple_wg': 'new_m', 'new_m_ple_bg': 'new_m', 'new_m_ple_wp': 'new_m', 'new_m_ab_w_in': 'new_m', 'new_m_a_sinks': 'new_m', 'new_m_b_conv_w': 'new_m', 'new_m_b_conv_b': 'new_m', 'new_m_b_wa': 'new_m', 'new_m_b_ba': 'new_m', 'new_m_b_wx': 'new_m', 'new_m_b_bx': 'new_m', 'new_m_b_lam': 'new_m', 'new_m_ab_w_out': 'new_m', 'new_m_c_w_in': 'new_m', 'new_m_c_conv_w': 'new_m', 'new_m_c_a_log': 'new_m', 'new_m_c_dt_bias': 'new_m', 'new_m_c_norm_g': 'new_m', 'new_m_c_w_out': 'new_m', 'new_v_ffn1_wg': 'new_v', 'new_v_ffn1_wu': 'new_v', 'new_v_ffn1_wd': 'new_v', 'new_v_ffn2_wg': 'new_v', 'new_v_ffn2_wu': 'new_v', 'new_v_ffn2_wd': 'new_v', 'new_v_ln_g': 'new_v', 'new_v_ln_b': 'new_v', 'new_v_ple_wg': 'new_v', 'new_v_ple_bg': 'new_v', 'new_v_ple_wp': 'new_v', 'new_v_ab_w_in': 'new_v', 'new_v_a_sinks': 'new_v', 'new_v_b_conv_w': 'new_v', 'new_v_b_conv_b': 'new_v', 'new_v_b_wa': 'new_v', 'new_v_b_ba': 'new_v', 'new_v_b_wx': 'new_v', 'new_v_b_bx': 'new_v', 'new_v_b_lam': 'new_v', 'new_v_ab_w_out': 'new_v', 'new_v_c_w_in': 'new_v', 'new_v_c_conv_w': 'new_v', 'new_v_c_a_log': 'new_v', 'new_v_c_dt_bias': 'new_v', 'new_v_c_norm_g': 'new_v', 'new_v_c_w_out': 'new_v'}


def _forward(args):
    return _fwd_reference(*[args[k] for k in FWD_PARAMS])


def _output_shape():
    out = _jax.eval_shape(lambda: _forward(_fwd_setup_inputs(0)))
    return out.shape, out.dtype

N_MICROBATCH = 1
ADAM_LR = 0.001
ADAM_B1 = 0.9
ADAM_B2 = 0.999
ADAM_EPS = 1e-08
ADAM_WD = 0.01
ADAM_STEP = 10
PER_EXAMPLE_BATCH_AXIS = {'x': 0, 'p': 1, 'loss_target': 0}
SHARED_INPUTS = []
_WEIGHT_DTYPES = {'ffn1_wg': _jnp.float32, 'ffn1_wu': _jnp.float32, 'ffn1_wd': _jnp.float32, 'ffn2_wg': _jnp.float32, 'ffn2_wu': _jnp.float32, 'ffn2_wd': _jnp.float32, 'ln_g': _jnp.float32, 'ln_b': _jnp.float32, 'ple_wg': _jnp.float32, 'ple_bg': _jnp.float32, 'ple_wp': _jnp.float32, 'ab_w_in': _jnp.float32, 'a_sinks': _jnp.float32, 'b_conv_w': _jnp.float32, 'b_conv_b': _jnp.float32, 'b_wa': _jnp.float32, 'b_ba': _jnp.float32, 'b_wx': _jnp.float32, 'b_bx': _jnp.float32, 'b_lam': _jnp.float32, 'ab_w_out': _jnp.float32, 'c_w_in': _jnp.float32, 'c_conv_w': _jnp.float32, 'c_a_log': _jnp.float32, 'c_dt_bias': _jnp.float32, 'c_norm_g': _jnp.float32, 'c_w_out': _jnp.float32}
MOMENT_SCALE = {'ffn1_wg': 2.161558e-02, 'ffn1_wu': 2.113348e-02, 'ffn1_wd': 7.008263e-02, 'ffn2_wg': 1.930700e-02, 'ffn2_wu': 1.895169e-02, 'ffn2_wd': 6.301907e-02, 'ln_g': 2.692305e+01, 'ln_b': 5.909878e+00, 'ple_wg': 1.787981e-01, 'ple_bg': 4.695113e+00, 'ple_wp': 4.795444e-01, 'ab_w_in': 4.778349e-02, 'a_sinks': 4.695004e-03, 'b_conv_w': 1.472183e-01, 'b_conv_b': 2.743576e+00, 'b_wa': 1.202910e-01, 'b_ba': 5.887605e-02, 'b_wx': 2.172445e-01, 'b_bx': 4.208542e-02, 'b_lam': 6.040696e-02, 'ab_w_out': 2.106016e-01, 'c_w_in': 5.148041e-02, 'c_conv_w': 6.882398e-02, 'c_a_log': 3.255042e-01, 'c_dt_bias': 2.987564e-01, 'c_norm_g': 4.463945e-01, 'c_w_out': 3.389311e-01}


def _to_microbatches(a, axis):
    t = _jnp.moveaxis(a, axis, 0)
    t = t.reshape((N_MICROBATCH, t.shape[0] // N_MICROBATCH) + t.shape[1:])
    return _jnp.moveaxis(t, 1, axis + 1)


def setup_inputs(seed: int = 0) -> dict:
    inp = _fwd_setup_inputs(seed)
    key = _jax.random.fold_in(_jax.random.key(seed), 7919)
    shape, _ = _output_shape()
    out = dict(inp)
    out["loss_target"] = _jax.random.normal(_jax.random.fold_in(key, 0), shape, _jnp.float32)
    for i, name in enumerate(TWIN_WEIGHTS):
        w = inp[name].astype(_jnp.float32)
        if MOMENT_SCALE is None:
            s = _jnp.sqrt(_jnp.mean(_jnp.square(w)) + 1e-30)
        else:
            s = MOMENT_SCALE[name]
        km, kv = _jax.random.split(_jax.random.fold_in(key, i + 1))
        out[name] = w
        out["m_" + name] = s * _jax.random.normal(km, w.shape, _jnp.float32)
        out["v_" + name] = (s * s) * _jax.random.uniform(kv, w.shape, _jnp.float32, 0.5, 1.5)
    if N_MICROBATCH > 1:
        for name, axis in PER_EXAMPLE_BATCH_AXIS.items():
            out[name] = _to_microbatches(out[name], axis)
    return {'x': out['x'], 'p': out['p'], 'ffn1_wg': out['ffn1_wg'], 'ffn1_wu': out['ffn1_wu'], 'ffn1_wd': out['ffn1_wd'], 'ffn2_wg': out['ffn2_wg'], 'ffn2_wu': out['ffn2_wu'], 'ffn2_wd': out['ffn2_wd'], 'ln_g': out['ln_g'], 'ln_b': out['ln_b'], 'ple_wg': out['ple_wg'], 'ple_bg': out['ple_bg'], 'ple_wp': out['ple_wp'], 'ab_w_in': out['ab_w_in'], 'a_sinks': out['a_sinks'], 'b_conv_w': out['b_conv_w'], 'b_conv_b': out['b_conv_b'], 'b_wa': out['b_wa'], 'b_ba': out['b_ba'], 'b_wx': out['b_wx'], 'b_bx': out['b_bx'], 'b_lam': out['b_lam'], 'ab_w_out': out['ab_w_out'], 'c_w_in': out['c_w_in'], 'c_conv_w': out['c_conv_w'], 'c_a_log': out['c_a_log'], 'c_dt_bias': out['c_dt_bias'], 'c_norm_g': out['c_norm_g'], 'c_w_out': out['c_w_out'], 'loss_target': out['loss_target'], 'm_ffn1_wg': out['m_ffn1_wg'], 'm_ffn1_wu': out['m_ffn1_wu'], 'm_ffn1_wd': out['m_ffn1_wd'], 'm_ffn2_wg': out['m_ffn2_wg'], 'm_ffn2_wu': out['m_ffn2_wu'], 'm_ffn2_wd': out['m_ffn2_wd'], 'm_ln_g': out['m_ln_g'], 'm_ln_b': out['m_ln_b'], 'm_ple_wg': out['m_ple_wg'], 'm_ple_bg': out['m_ple_bg'], 'm_ple_wp': out['m_ple_wp'], 'm_ab_w_in': out['m_ab_w_in'], 'm_a_sinks': out['m_a_sinks'], 'm_b_conv_w': out['m_b_conv_w'], 'm_b_conv_b': out['m_b_conv_b'], 'm_b_wa': out['m_b_wa'], 'm_b_ba': out['m_b_ba'], 'm_b_wx': out['m_b_wx'], 'm_b_bx': out['m_b_bx'], 'm_b_lam': out['m_b_lam'], 'm_ab_w_out': out['m_ab_w_out'], 'm_c_w_in': out['m_c_w_in'], 'm_c_conv_w': out['m_c_conv_w'], 'm_c_a_log': out['m_c_a_log'], 'm_c_dt_bias': out['m_c_dt_bias'], 'm_c_norm_g': out['m_c_norm_g'], 'm_c_w_out': out['m_c_w_out'], 'v_ffn1_wg': out['v_ffn1_wg'], 'v_ffn1_wu': out['v_ffn1_wu'], 'v_ffn1_wd': out['v_ffn1_wd'], 'v_ffn2_wg': out['v_ffn2_wg'], 'v_ffn2_wu': out['v_ffn2_wu'], 'v_ffn2_wd': out['v_ffn2_wd'], 'v_ln_g': out['v_ln_g'], 'v_ln_b': out['v_ln_b'], 'v_ple_wg': out['v_ple_wg'], 'v_ple_bg': out['v_ple_bg'], 'v_ple_wp': out['v_ple_wp'], 'v_ab_w_in': out['v_ab_w_in'], 'v_a_sinks': out['v_a_sinks'], 'v_b_conv_w': out['v_b_conv_w'], 'v_b_conv_b': out['v_b_conv_b'], 'v_b_wa': out['v_b_wa'], 'v_b_ba': out['v_b_ba'], 'v_b_wx': out['v_b_wx'], 'v_b_bx': out['v_b_bx'], 'v_b_lam': out['v_b_lam'], 'v_ab_w_out': out['v_ab_w_out'], 'v_c_w_in': out['v_c_w_in'], 'v_c_conv_w': out['v_c_conv_w'], 'v_c_a_log': out['v_c_a_log'], 'v_c_dt_bias': out['v_c_dt_bias'], 'v_c_norm_g': out['v_c_norm_g'], 'v_c_w_out': out['v_c_w_out']}


def _loss(weights, diff, rest, loss_target):
    with _jax.named_scope("forward"):
        args = {**rest, TWIN_DIFF_INPUT: diff, **{k: w.astype(_WEIGHT_DTYPES[k]) for k, w in weights.items()}}
        y = _forward(args)
    with _jax.named_scope("loss_head"):
        err = _jnp.square(y.astype(_jnp.float32) - loss_target)
        return 0.5 * _jnp.sum(_jnp.mean(err, axis=-1)) if err.ndim else 0.5 * err


def _adamw(w, g, m, v):
    m = ADAM_B1 * m + (1.0 - ADAM_B1) * g
    v = ADAM_B2 * v + (1.0 - ADAM_B2) * _jnp.square(g)
    m_hat = m / (1.0 - ADAM_B1 ** ADAM_STEP)
    v_hat = v / (1.0 - ADAM_B2 ** ADAM_STEP)
    delta = -ADAM_LR * (m_hat / (_jnp.sqrt(v_hat) + ADAM_EPS) + ADAM_WD * w)
    return delta, m, v


def reference(x, p, ffn1_wg, ffn1_wu, ffn1_wd, ffn2_wg, ffn2_wu, ffn2_wd, ln_g, ln_b, ple_wg, ple_bg, ple_wp, ab_w_in, a_sinks, b_conv_w, b_conv_b, b_wa, b_ba, b_wx, b_bx, b_lam, ab_w_out, c_w_in, c_conv_w, c_a_log, c_dt_bias, c_norm_g, c_w_out, loss_target, m_ffn1_wg, m_ffn1_wu, m_ffn1_wd, m_ffn2_wg, m_ffn2_wu, m_ffn2_wd, m_ln_g, m_ln_b, m_ple_wg, m_ple_bg, m_ple_wp, m_ab_w_in, m_a_sinks, m_b_conv_w, m_b_conv_b, m_b_wa, m_b_ba, m_b_wx, m_b_bx, m_b_lam, m_ab_w_out, m_c_w_in, m_c_conv_w, m_c_a_log, m_c_dt_bias, m_c_norm_g, m_c_w_out, v_ffn1_wg, v_ffn1_wu, v_ffn1_wd, v_ffn2_wg, v_ffn2_wu, v_ffn2_wd, v_ln_g, v_ln_b, v_ple_wg, v_ple_bg, v_ple_wp, v_ab_w_in, v_a_sinks, v_b_conv_w, v_b_conv_b, v_b_wa, v_b_ba, v_b_wx, v_b_bx, v_b_lam, v_ab_w_out, v_c_w_in, v_c_conv_w, v_c_a_log, v_c_dt_bias, v_c_norm_g, v_c_w_out):
    given = dict(x=x, p=p, ffn1_wg=ffn1_wg, ffn1_wu=ffn1_wu, ffn1_wd=ffn1_wd, ffn2_wg=ffn2_wg, ffn2_wu=ffn2_wu, ffn2_wd=ffn2_wd, ln_g=ln_g, ln_b=ln_b, ple_wg=ple_wg, ple_bg=ple_bg, ple_wp=ple_wp, ab_w_in=ab_w_in, a_sinks=a_sinks, b_conv_w=b_conv_w, b_conv_b=b_conv_b, b_wa=b_wa, b_ba=b_ba, b_wx=b_wx, b_bx=b_bx, b_lam=b_lam, ab_w_out=ab_w_out, c_w_in=c_w_in, c_conv_w=c_conv_w, c_a_log=c_a_log, c_dt_bias=c_dt_bias, c_norm_g=c_norm_g, c_w_out=c_w_out, loss_target=loss_target, m_ffn1_wg=m_ffn1_wg, m_ffn1_wu=m_ffn1_wu, m_ffn1_wd=m_ffn1_wd, m_ffn2_wg=m_ffn2_wg, m_ffn2_wu=m_ffn2_wu, m_ffn2_wd=m_ffn2_wd, m_ln_g=m_ln_g, m_ln_b=m_ln_b, m_ple_wg=m_ple_wg, m_ple_bg=m_ple_bg, m_ple_wp=m_ple_wp, m_ab_w_in=m_ab_w_in, m_a_sinks=m_a_sinks, m_b_conv_w=m_b_conv_w, m_b_conv_b=m_b_conv_b, m_b_wa=m_b_wa, m_b_ba=m_b_ba, m_b_wx=m_b_wx, m_b_bx=m_b_bx, m_b_lam=m_b_lam, m_ab_w_out=m_ab_w_out, m_c_w_in=m_c_w_in, m_c_conv_w=m_c_conv_w, m_c_a_log=m_c_a_log, m_c_dt_bias=m_c_dt_bias, m_c_norm_g=m_c_norm_g, m_c_w_out=m_c_w_out, v_ffn1_wg=v_ffn1_wg, v_ffn1_wu=v_ffn1_wu, v_ffn1_wd=v_ffn1_wd, v_ffn2_wg=v_ffn2_wg, v_ffn2_wu=v_ffn2_wu, v_ffn2_wd=v_ffn2_wd, v_ln_g=v_ln_g, v_ln_b=v_ln_b, v_ple_wg=v_ple_wg, v_ple_bg=v_ple_bg, v_ple_wp=v_ple_wp, v_ab_w_in=v_ab_w_in, v_a_sinks=v_a_sinks, v_b_conv_w=v_b_conv_w, v_b_conv_b=v_b_conv_b, v_b_wa=v_b_wa, v_b_ba=v_b_ba, v_b_wx=v_b_wx, v_b_bx=v_b_bx, v_b_lam=v_b_lam, v_ab_w_out=v_ab_w_out, v_c_w_in=v_c_w_in, v_c_conv_w=v_c_conv_w, v_c_a_log=v_c_a_log, v_c_dt_bias=v_c_dt_bias, v_c_norm_g=v_c_norm_g, v_c_w_out=v_c_w_out)
    weights = {n: given[n] for n in TWIN_WEIGHTS}
    shared = {n: given[n] for n in SHARED_INPUTS}
    per_example = {n: given[n] for n in ['x', 'p']}
    grad_fn = _jax.value_and_grad(_loss, argnums=(0, 1))

    def one_microbatch(ex, loss_target):
        ex = dict(ex)
        diff = ex.pop(TWIN_DIFF_INPUT)
        return grad_fn(weights, diff, {**shared, **ex}, loss_target)

    if N_MICROBATCH == 1:
        loss, (grad_w, grad_x) = one_microbatch(per_example, given["loss_target"])
    else:
        def body(carry, xs):
            loss_sum, grad_sum = carry
            l_k, (gw_k, gx_k) = one_microbatch(xs[0], xs[1])
            with _jax.named_scope("update"):
                return (loss_sum + l_k, _jax.tree.map(_jnp.add, grad_sum, gw_k)), gx_k

        init = (_jnp.zeros((), _jnp.float32), _jax.tree.map(_jnp.zeros_like, weights))
        (loss, grad_w), grad_x = _jax.lax.scan(body, init, (per_example, given["loss_target"]))
    with _jax.named_scope("update"):
        delta_w, new_m, new_v = {}, {}, {}
        for n in TWIN_WEIGHTS:
            delta_w[n], new_m[n], new_v[n] = _adamw(weights[n], grad_w[n], given["m_" + n], given["v_" + n])
    return (loss, grad_x, *[grad_w[n] for n in TWIN_WEIGHTS], *[delta_w[n] for n in TWIN_WEIGHTS],
            *[new_m[n] for n in TWIN_WEIGHTS], *[new_v[n] for n in TWIN_WEIGHTS])
```

```python
import functools

import jax
import jax.numpy as jnp
from jax import lax
from jax.experimental import pallas as pl
from jax.experimental.pallas import tpu as pltpu

f32 = jnp.float32
bf16 = jnp.bfloat16
HI = lax.Precision.HIGHEST

DEPTH = 2
CHUNK = 64
A_HEADS, A_KV_HEADS, A_GROUP, A_HEAD_DIM = 8, 2, 4, 64
A_WIDTH, A_KV_WIDTH, A_WINDOW = 512, 128, 128
B_WIDTH, B_BLOCKS, B_BLOCK, B_CONV = 512, 8, 64, 4
RG_C = 8.0
C_HEADS, C_HEAD_DIM, C_WIDTH, C_CONV = 8, 128, 1024, 4
DN_ALPHA = (2.0 * DEPTH) ** 0.25
LN_EPS = 1e-5
NORM_EPS = 1e-6
NEG = -1e30
ADAM_LR, ADAM_B1, ADAM_B2, ADAM_EPS, ADAM_WD, ADAM_STEP = 0.001, 0.9, 0.999, 1e-08, 0.01, 10

VMEM_LIMIT_BYTES = 56 * 1024 * 1024
LANES = 128
SUBLANES = 8
GROUP_W = 128
C_HEADS_PER_STEP_FWD = 2
C_HEADS_PER_STEP_BWD = 1

NN = ((1,), (0,))
NT = ((1,), (1,))
TN = ((0,), (0,))


def _params(sem):
    return pltpu.CompilerParams(dimension_semantics=sem, vmem_limit_bytes=VMEM_LIMIT_BYTES)


def _tile(n, cap, mult):
    best = None
    t = mult
    while t <= min(n, cap):
        if n % t == 0:
            best = t
        t += mult
    return best if best is not None else n


def _bdot(a, b, dims):
    return lax.dot_general(a.astype(bf16), b.astype(bf16), (dims, ((), ())), preferred_element_type=f32)


def _hdot(a, b):
    return lax.dot_general(a, b, (NN, ((), ())), precision=HI, preferred_element_type=f32)


@jax.custom_vjp
def _bnn(a, b):
    return _bdot(a, b, NN)


def _bnn_fwd(a, b):
    return _bdot(a, b, NN), (a, b)


def _bnn_bwd(res, g):
    a, b = res
    return _bdot(g, b, NT), _bdot(a, g, TN)


_bnn.defvjp(_bnn_fwd, _bnn_bwd)


@jax.custom_vjp
def _bnt(a, b):
    return _bdot(a, b, NT)


def _bnt_fwd(a, b):
    return _bdot(a, b, NT), (a, b)


def _bnt_bwd(res, g):
    a, b = res
    return _bdot(g, b, NN), _bdot(g, a, TN)


_bnt.defvjp(_bnt_fwd, _bnt_bwd)


@jax.custom_vjp
def _btn(a, b):
    return _bdot(a, b, TN)


def _btn_fwd(a, b):
    return _bdot(a, b, TN), (a, b)


def _btn_bwd(res, g):
    a, b = res
    return _bdot(b, g, NT), _bdot(a, g, NN)


_btn.defvjp(_btn_fwd, _btn_bwd)

RAW_DOTS = (lambda a, b: _bdot(a, b, NN), lambda a, b: _bdot(a, b, NT), lambda a, b: _bdot(a, b, TN))
VJP_DOTS = (_bnn, _bnt, _btn)


def _layer_norm(z, g, b):
    mu = jnp.mean(z, -1, keepdims=True)
    d = z - mu
    var = jnp.mean(d * d, -1, keepdims=True)
    return d * lax.rsqrt(var + LN_EPS) * g + b


def _silu(x):
    return x * jax.nn.sigmoid(x)


def mm_nn(a, w, add=None, add_scale=1.0, *, name):
    m, k = a.shape
    n = w.shape[1]
    tm = _tile(m, 512, SUBLANES)
    tn = _tile(n, 1024, LANES)

    def body(*refs):
        if add is None:
            a_ref, w_ref, o_ref = refs
            o_ref[...] = _bdot(a_ref[...], w_ref[...], NN)
        else:
            a_ref, w_ref, add_ref, o_ref = refs
            o_ref[...] = _bdot(a_ref[...], w_ref[...], NN) + add_scale * add_ref[...]

    in_specs = [pl.BlockSpec((tm, k), lambda i, j: (i, 0)), pl.BlockSpec((k, tn), lambda i, j: (0, j))]
    args = [a, w]
    if add is not None:
        in_specs.append(pl.BlockSpec((tm, tn), lambda i, j: (i, j)))
        args.append(add)
    return pl.pallas_call(
        body, grid=(m // tm, n // tn), in_specs=in_specs,
        out_specs=pl.BlockSpec((tm, tn), lambda i, j: (i, j)),
        out_shape=jax.ShapeDtypeStruct((m, n), f32),
        compiler_params=_params(("parallel", "parallel")), name=name,
    )(*args)


def mm_tn(a, b, *, name):
    m, k = a.shape
    n = b.shape[1]
    tm = _tile(m, 512, SUBLANES)
    tn = _tile(n, 512, LANES)

    def body(a_ref, b_ref, o_ref):
        part = _bdot(a_ref[...], b_ref[...], TN)

        @pl.when(pl.program_id(1) == 0)
        def _():
            o_ref[...] = part

        @pl.when(pl.program_id(1) > 0)
        def _():
            o_ref[...] += part

    return pl.pallas_call(
        body, grid=(n // tn, m // tm),
        in_specs=[pl.BlockSpec((tm, k), lambda j, i: (i, 0)), pl.BlockSpec((tm, tn), lambda j, i: (i, j))],
        out_specs=pl.BlockSpec((k, tn), lambda j, i: (0, j)),
        out_shape=jax.ShapeDtypeStruct((k, n), f32),
        compiler_params=_params(("parallel", "arbitrary")), name=name,
    )(a, b)


def proj_ln(a_list, w_list, xres, g, b, *, name):
    t, d = xres.shape
    tm = _tile(t, 256, SUBLANES)
    na = len(a_list)

    def body(*refs):
        a_refs, w_refs = refs[:na], refs[na:2 * na]
        x_ref, g_ref, b_ref, y_ref, z_ref = refs[2 * na:]
        z = DN_ALPHA * x_ref[...]
        for a_ref, w_ref in zip(a_refs, w_refs):
            z = z + _bdot(a_ref[...], w_ref[...], NN)
        z_ref[...] = z
        y_ref[...] = _layer_norm(z, g_ref[...], b_ref[...])

    in_specs = [pl.BlockSpec((tm, a.shape[1]), lambda i: (i, 0)) for a in a_list]
    in_specs += [pl.BlockSpec(w.shape, lambda i: (0, 0)) for w in w_list]
    in_specs += [pl.BlockSpec((tm, d), lambda i: (i, 0)), pl.BlockSpec((1, d), lambda i: (0, 0)),
                 pl.BlockSpec((1, d), lambda i: (0, 0))]
    return pl.pallas_call(
        body, grid=(t // tm,), in_specs=in_specs,
        out_specs=[pl.BlockSpec((tm, d), lambda i: (i, 0))] * 2,
        out_shape=[jax.ShapeDtypeStruct((t, d), f32)] * 2,
        compiler_params=_params(("parallel",)), name=name,
    )(*a_list, *w_list, xres, g, b)


def ln_bwd(z, dy, g, *, name):
    t, d = z.shape
    tm = _tile(t, 512, SUBLANES)

    def body(z_ref, dy_ref, g_ref, dz_ref, dzb_ref, dg_ref, db_ref):
        zz = z_ref[...]
        dy_ = dy_ref[...]
        mu = jnp.mean(zz, -1, keepdims=True)
        dd = zz - mu
        var = jnp.mean(dd * dd, -1, keepdims=True)
        rstd = lax.rsqrt(var + LN_EPS)
        xhat = dd * rstd
        dxh = dy_ * g_ref[...]
        dz = rstd * (dxh - jnp.mean(dxh, -1, keepdims=True) - xhat * jnp.mean(dxh * xhat, -1, keepdims=True))
        dz_ref[...] = dz
        dzb_ref[...] = dz.astype(bf16)
        pg = jnp.sum(dy_ * xhat, 0, keepdims=True)
        pb = jnp.sum(dy_, 0, keepdims=True)

        @pl.when(pl.program_id(0) == 0)
        def _():
            dg_ref[...] = pg
            db_ref[...] = pb

        @pl.when(pl.program_id(0) > 0)
        def _():
            dg_ref[...] += pg
            db_ref[...] += pb

    row = pl.BlockSpec((tm, d), lambda i: (i, 0))
    vec = pl.BlockSpec((1, d), lambda i: (0, 0))
    return pl.pallas_call(
        body, grid=(t // tm,), in_specs=[row, row, vec], out_specs=[row, row, vec, vec],
        out_shape=[jax.ShapeDtypeStruct((t, d), f32), jax.ShapeDtypeStruct((t, d), bf16),
                   jax.ShapeDtypeStruct((1, d), f32), jax.ShapeDtypeStruct((1, d), f32)],
        compiler_params=_params(("arbitrary",)), name=name,
    )(z, dy, g)


def loss_head(y, target, *, name):
    t, d = y.shape
    tm = _tile(t, 512, SUBLANES)

    def body(y_ref, t_ref, dy_ref, sq_ref):
        e = y_ref[...] - t_ref[...]
        dy_ref[...] = e * (1.0 / d)
        part = jnp.sum(e * e, 0, keepdims=True)

        @pl.when(pl.program_id(0) == 0)
        def _():
            sq_ref[...] = part

        @pl.when(pl.program_id(0) > 0)
        def _():
            sq_ref[...] += part

    row = pl.BlockSpec((tm, d), lambda i: (i, 0))
    vec = pl.BlockSpec((1, d), lambda i: (0, 0))
    return pl.pallas_call(
        body, grid=(t // tm,), in_specs=[row, row], out_specs=[row, vec],
        out_shape=[jax.ShapeDtypeStruct((t, d), f32), jax.ShapeDtypeStruct((1, d), f32)],
        compiler_params=_params(("arbitrary",)), name=name,
    )(y, target)


def ffn_fwd(x, wg, wu, wd, g, b, *, name):
    t, d = x.shape
    ff = wg.shape[1]
    tm = _tile(t, 512, SUBLANES)
    tf = _tile(ff, 256, LANES)
    nf = ff // tf

    def body(x_ref, wg_ref, wu_ref, wd_ref, g_ref, b_ref, y_ref, z_ref, acc_ref):
        f = pl.program_id(1)
        xb = x_ref[...].astype(bf16)
        gate = _bdot(xb, wg_ref[...], NN)
        up = _bdot(xb, wu_ref[...], NN)
        part = _bdot(_silu(gate) * up, wd_ref[...], NN)

        @pl.when(f == 0)
        def _():
            acc_ref[...] = part

        @pl.when(f > 0)
        def _():
            acc_ref[...] += part

        @pl.when(f == nf - 1)
        def _():
            z = DN_ALPHA * x_ref[...] + 0.5 * acc_ref[...]
            z_ref[...] = z
            y_ref[...] = _layer_norm(z, g_ref[...], b_ref[...])

    row = pl.BlockSpec((tm, d), lambda i, j: (i, 0))
    vec = pl.BlockSpec((1, d), lambda i, j: (0, 0))
    return pl.pallas_call(
        body, grid=(t // tm, nf),
        in_specs=[row, pl.BlockSpec((d, tf), lambda i, j: (0, j)), pl.BlockSpec((d, tf), lambda i, j: (0, j)),
                  pl.BlockSpec((tf, d), lambda i, j: (j, 0)), vec, vec],
        out_specs=[row, row],
        out_shape=[jax.ShapeDtypeStruct((t, d), f32)] * 2,
        scratch_shapes=[pltpu.VMEM((tm, d), f32)],
        compiler_params=_params(("parallel", "arbitrary")), name=name,
    )(x, wg, wu, wd, g, b)


def ffn_bwd_weights(xb, dzb, wg, wu, wdt, *, name):
    t, d = xb.shape
    ff = wg.shape[1]
    tm = _tile(t, 1024, SUBLANES)
    tf = _tile(ff, 256, LANES)

    def body(x_ref, dz_ref, wg_ref, wu_ref, wdt_ref, dgate_ref, dup_ref, dwg_ref, dwu_ref, dwd_ref):
        x = x_ref[...]
        dzh = dz_ref[...] * 0.5
        gate = _bdot(x, wg_ref[...], NN)
        up = _bdot(x, wu_ref[...], NN)
        sg = jax.nn.sigmoid(gate)
        s = gate * sg
        dh = _bdot(dzh, wdt_ref[...], NN)
        dup = (dh * s).astype(bf16)
        dgate = (dh * up * (sg * (1.0 + gate * (1.0 - sg)))).astype(bf16)
        dgate_ref[...] = dgate
        dup_ref[...] = dup
        pwg = _bdot(x, dgate, TN)
        pwu = _bdot(x, dup, TN)
        pwd = _bdot(s * up, dzh, TN)

        @pl.when(pl.program_id(1) == 0)
        def _():
            dwg_ref[...] = pwg
            dwu_ref[...] = pwu
            dwd_ref[...] = pwd

        @pl.when(pl.program_id(1) > 0)
        def _():
            dwg_ref[...] += pwg
            dwu_ref[...] += pwu
            dwd_ref[...] += pwd

    row = pl.BlockSpec((tm, d), lambda j, i: (i, 0))
    col = pl.BlockSpec((d, tf), lambda j, i: (0, j))
    act = pl.BlockSpec((tm, tf), lambda j, i: (i, j))
    return pl.pallas_call(
        body, grid=(ff // tf, t // tm),
        in_specs=[row, row, col, col, col],
        out_specs=[act, act, col, col, pl.BlockSpec((tf, d), lambda j, i: (j, 0))],
        out_shape=[jax.ShapeDtypeStruct((t, ff), bf16), jax.ShapeDtypeStruct((t, ff), bf16),
                   jax.ShapeDtypeStruct((d, ff), f32), jax.ShapeDtypeStruct((d, ff), f32),
                   jax.ShapeDtypeStruct((ff, d), f32)],
        compiler_params=_params(("parallel", "arbitrary")), name=name,
    )(xb, dzb, wg, wu, wdt)


def ffn_bwd_input(dgate, dup, wgt, wut, dz, *, name):
    t, ff = dgate.shape
    d = wgt.shape[1]
    tm = _tile(t, 256, SUBLANES)

    def body(dg_ref, du_ref, wgt_ref, wut_ref, dz_ref, dx_ref):
        dx_ref[...] = (DN_ALPHA * dz_ref[...] + _bdot(dg_ref[...], wgt_ref[...], NN)
                       + _bdot(du_ref[...], wut_ref[...], NN))

    act = pl.BlockSpec((tm, ff), lambda i: (i, 0))
    wsp = pl.BlockSpec((ff, d), lambda i: (0, 0))
    row = pl.BlockSpec((tm, d), lambda i: (i, 0))
    return pl.pallas_call(
        body, grid=(t // tm,), in_specs=[act, act, wsp, wsp, row], out_specs=row,
        out_shape=jax.ShapeDtypeStruct((t, d), f32),
        compiler_params=_params(("parallel",)), name=name,
    )(dgate, dup, wgt, wut, dz)


def ple_fwd(x, p, wg, bg, wp, *, name):
    t, d = x.shape
    dp = p.shape[1]
    tm = _tile(t, 512, SUBLANES)

    def body(x_ref, p_ref, wg_ref, bg_ref, wp_ref, o_ref):
        x_ = x_ref[...]
        gate = jax.nn.sigmoid(_bdot(x_, wg_ref[...], NN) + bg_ref[...])
        o_ref[...] = x_ + gate * _bdot(p_ref[...], wp_ref[...], NN)

    row = pl.BlockSpec((tm, d), lambda i: (i, 0))
    return pl.pallas_call(
        body, grid=(t // tm,),
        in_specs=[row, pl.BlockSpec((tm, dp), lambda i: (i, 0)), pl.BlockSpec((d, d), lambda i: (0, 0)),
                  pl.BlockSpec((1, d), lambda i: (0, 0)), pl.BlockSpec((dp, d), lambda i: (0, 0))],
        out_specs=row, out_shape=jax.ShapeDtypeStruct((t, d), f32),
        compiler_params=_params(("parallel",)), name=name,
    )(x, p, wg, bg, wp)


def ple_bwd(x, p, dy, wg, wgt, bg, wp, *, name):
    t, d = x.shape
    dp = p.shape[1]
    tm = _tile(t, 512, SUBLANES)

    def body(x_ref, p_ref, dy_ref, wg_ref, wgt_ref, bg_ref, wp_ref, dx_ref, dwg_ref, dbg_ref, dwp_ref):
        x_ = x_ref[...]
        dy_ = dy_ref[...]
        s = jax.nn.sigmoid(_bdot(x_, wg_ref[...], NN) + bg_ref[...])
        e = _bdot(p_ref[...], wp_ref[...], NN)
        da = dy_ * e * s * (1.0 - s)
        de = dy_ * s
        dx_ref[...] = dy_ + _bdot(da, wgt_ref[...], NN)
        pwg = _bdot(x_, da, TN)
        pbg = jnp.sum(da, 0, keepdims=True)
        pwp = _bdot(p_ref[...], de, TN)

        @pl.when(pl.program_id(0) == 0)
        def _():
            dwg_ref[...] = pwg
            dbg_ref[...] = pbg
            dwp_ref[...] = pwp

        @pl.when(pl.program_id(0) > 0)
        def _():
            dwg_ref[...] += pwg
            dbg_ref[...] += pbg
            dwp_ref[...] += pwp

    row = pl.BlockSpec((tm, d), lambda i: (i, 0))
    full = lambda shape: pl.BlockSpec(shape, lambda i: (0, 0))
    return pl.pallas_call(
        body, grid=(t // tm,),
        in_specs=[row, pl.BlockSpec((tm, dp), lambda i: (i, 0)), row, full((d, d)), full((d, d)), full((1, d)),
                  full((dp, d))],
        out_specs=[row, full((d, d)), full((1, d)), full((dp, d))],
        out_shape=[jax.ShapeDtypeStruct((t, d), f32), jax.ShapeDtypeStruct((d, d), f32),
                   jax.ShapeDtypeStruct((1, d), f32), jax.ShapeDtypeStruct((dp, d), f32)],
        compiler_params=_params(("arbitrary",)), name=name,
    )(x, p, dy, wg, wgt, bg, wp)


def _conv_taps(xpad_ref, w_ref, s):
    acc = w_ref[0:1, :] * xpad_ref[SUBLANES - 3:SUBLANES - 3 + s, :]
    for j in range(1, 4):
        acc = acc + w_ref[j:j + 1, :] * xpad_ref[SUBLANES - 3 + j:SUBLANES - 3 + j + s, :]
    return acc


def conv_fwd(x, w, bias, act, nb, *, name):
    t, c = x.shape
    s = t // nb
    cw = GROUP_W

    def body(x_ref, w_ref, b_ref, y_ref, xpad):
        xpad[0:SUBLANES, :] = jnp.zeros((SUBLANES, cw), f32)
        xpad[SUBLANES:, :] = x_ref[...]
        acc = _conv_taps(xpad, w_ref, s) + b_ref[...]
        y_ref[...] = _silu(acc) if act else acc

    slab = pl.BlockSpec((s, cw), lambda b, g: (b, g))
    return pl.pallas_call(
        body, grid=(nb, c // cw),
        in_specs=[slab, pl.BlockSpec((4, cw), lambda b, g: (0, g)), pl.BlockSpec((1, cw), lambda b, g: (0, g))],
        out_specs=slab, out_shape=jax.ShapeDtypeStruct((t, c), f32),
        scratch_shapes=[pltpu.VMEM((s + SUBLANES, cw), f32)],
        compiler_params=_params(("parallel", "parallel")), name=name,
    )(x, w, bias)


def conv_bwd(x, w, bias, dy, act, nb, *, name):
    t, c = x.shape
    s = t // nb
    cw = GROUP_W

    def body(x_ref, w_ref, b_ref, dy_ref, dx_ref, dw_ref, db_ref, xpad, dpad):
        xpad[0:SUBLANES, :] = jnp.zeros((SUBLANES, cw), f32)
        xpad[SUBLANES:, :] = x_ref[...]
        dacc = dy_ref[...]
        if act:
            acc = _conv_taps(xpad, w_ref, s) + b_ref[...]
            sg = jax.nn.sigmoid(acc)
            dacc = dacc * (sg * (1.0 + acc * (1.0 - sg)))
        dpad[0:s, :] = dacc
        dpad[s:, :] = jnp.zeros((SUBLANES, cw), f32)
        dx = w_ref[0:1, :] * dpad[3:3 + s, :]
        for j in range(1, 4):
            dx = dx + w_ref[j:j + 1, :] * dpad[3 - j:3 - j + s, :]
        dx_ref[...] = dx
        first = pl.program_id(1) == 0
        for j in range(4):
            pw = jnp.sum(dacc * xpad[SUBLANES - 3 + j:SUBLANES - 3 + j + s, :], 0, keepdims=True)

            @pl.when(first)
            def _():
                dw_ref[j:j + 1, :] = pw

            @pl.when(jnp.logical_not(first))
            def _():
                dw_ref[j:j + 1, :] += pw

        pb = jnp.sum(dacc, 0, keepdims=True)

        @pl.when(first)
        def _():
            db_ref[...] = pb

        @pl.when(jnp.logical_not(first))
        def _():
            db_ref[...] += pb

    slab = pl.BlockSpec((s, cw), lambda g, b: (b, g))
    wsp = pl.BlockSpec((4, cw), lambda g, b: (0, g))
    bsp = pl.BlockSpec((1, cw), lambda g, b: (0, g))
    return pl.pallas_call(
        body, grid=(c // cw, nb), in_specs=[slab, wsp, bsp, slab], out_specs=[slab, wsp, bsp],
        out_shape=[jax.ShapeDtypeStruct((t, c), f32), jax.ShapeDtypeStruct((4, c), f32),
                   jax.ShapeDtypeStruct((1, c), f32)],
        scratch_shapes=[pltpu.VMEM((s + SUBLANES, cw), f32), pltpu.VMEM((s + SUBLANES, cw), f32)],
        compiler_params=_params(("parallel", "arbitrary")), name=name,
    )(x, w, bias, dy)


def _attn_head(q, k, v, sink, slope, valid, dist, dots):
    nn, nt, _ = dots
    sc = nt(q, k) * (A_HEAD_DIM ** -0.5)
    sc = sc - slope * dist
    sc = jnp.where(valid, sc, NEG)
    m = jnp.maximum(jnp.max(sc, -1, keepdims=True), sink)
    pr = jnp.exp(sc - m)
    den = jnp.sum(pr, -1, keepdims=True) + jnp.exp(sink - m)
    return nn(pr / den, v)


def _attn_band_consts(r0):
    band = A_WINDOW + CHUNK
    qi = lax.broadcasted_iota(jnp.int32, (CHUNK, band), 0)
    kj = lax.broadcasted_iota(jnp.int32, (CHUNK, band), 1)
    dist = jnp.abs(qi + A_WINDOW - kj).astype(f32)
    valid = (kj + r0) >= A_WINDOW
    return dist, valid


def attn_fwd(qkv, sinks, nb, *, name):
    t = qkv.shape[0]
    s = t // nb
    band = A_WINDOW + CHUNK
    hd = A_HEAD_DIM

    def body(qkv_ref, sink_ref, o_ref, kvpad):
        kvpad[0:A_WINDOW, :] = jnp.zeros((A_WINDOW, 2 * A_KV_WIDTH), f32)
        kvpad[A_WINDOW:, :] = qkv_ref[:, A_WIDTH:]

        def chunk(n, carry):
            r0 = pl.multiple_of(n * CHUNK, CHUNK)
            dist, valid = _attn_band_consts(r0)
            for kvh in range(A_KV_HEADS):
                kb = kvpad[pl.ds(r0, band), kvh * hd:(kvh + 1) * hd]
                vb = kvpad[pl.ds(r0, band), A_KV_WIDTH + kvh * hd:A_KV_WIDTH + (kvh + 1) * hd]
                for gi in range(A_GROUP):
                    h = kvh * A_GROUP + gi
                    q = qkv_ref[pl.ds(r0, CHUNK), h * hd:(h + 1) * hd]
                    o = _attn_head(q, kb, vb, sink_ref[:, h:h + 1], 2.0 ** -(h + 1), valid, dist, RAW_DOTS)
                    o_ref[pl.ds(r0, CHUNK), h * hd:(h + 1) * hd] = o
            return carry

        lax.fori_loop(0, s // CHUNK, chunk, 0)

    return pl.pallas_call(
        body, grid=(nb,),
        in_specs=[pl.BlockSpec((s, A_WIDTH + 2 * A_KV_WIDTH), lambda b: (b, 0)),
                  pl.BlockSpec((1, A_HEADS), lambda b: (0, 0))],
        out_specs=pl.BlockSpec((s, A_WIDTH), lambda b: (b, 0)),
        out_shape=jax.ShapeDtypeStruct((t, A_WIDTH), f32),
        scratch_shapes=[pltpu.VMEM((s + A_WINDOW, 2 * A_KV_WIDTH), f32)],
        compiler_params=_params(("parallel",)), name=name,
    )(qkv, sinks)


def attn_bwd(qkv, sinks, do, nb, *, name):
    t = qkv.shape[0]
    s = t // nb
    band = A_WINDOW + CHUNK
    hd = A_HEAD_DIM
    kvw = 2 * A_KV_WIDTH

    def body(qkv_ref, sink_ref, do_ref, dqkv_ref, dsink_ref, kvpad, dkvpad):
        kvpad[0:A_WINDOW, :] = jnp.zeros((A_WINDOW, kvw), f32)
        kvpad[A_WINDOW:, :] = qkv_ref[:, A_WIDTH:]
        dkvpad[...] = jnp.zeros((s + A_WINDOW, kvw), f32)

        def chunk(n, dsinks):
            r0 = pl.multiple_of(n * CHUNK, CHUNK)
            dist, valid = _attn_band_consts(r0)
            dsinks = list(dsinks)
            for kvh in range(A_KV_HEADS):
                ksl = slice(kvh * hd, (kvh + 1) * hd)
                vsl = slice(A_KV_WIDTH + kvh * hd, A_KV_WIDTH + (kvh + 1) * hd)
                kb = kvpad[pl.ds(r0, band), ksl]
                vb = kvpad[pl.ds(r0, band), vsl]
                dk_acc = jnp.zeros((band, hd), f32)
                dv_acc = jnp.zeros((band, hd), f32)
                for gi in range(A_GROUP):
                    h = kvh * A_GROUP + gi
                    q = qkv_ref[pl.ds(r0, CHUNK), h * hd:(h + 1) * hd]
                    fn = functools.partial(_attn_head, slope=2.0 ** -(h + 1), valid=valid, dist=dist, dots=VJP_DOTS)
                    _, vjp = jax.vjp(fn, q, kb, vb, sink_ref[:, h:h + 1])
                    dq, dk, dv, ds = vjp(do_ref[pl.ds(r0, CHUNK), h * hd:(h + 1) * hd])
                    dqkv_ref[pl.ds(r0, CHUNK), h * hd:(h + 1) * hd] = dq
                    dk_acc = dk_acc + dk
                    dv_acc = dv_acc + dv
                    dsinks[h] = dsinks[h] + ds
                dkvpad[pl.ds(r0, band), ksl] += dk_acc
                dkvpad[pl.ds(r0, band), vsl] += dv_acc
            return tuple(dsinks)

        dsinks = lax.fori_loop(0, s // CHUNK, chunk, tuple(jnp.zeros((1, 1), f32) for _ in range(A_HEADS)))
        dqkv_ref[:, A_WIDTH:] = dkvpad[A_WINDOW:, :]
        first = pl.program_id(0) == 0
        for h in range(A_HEADS):
            @pl.when(first)
            def _():
                dsink_ref[:, h:h + 1] = dsinks[h]

            @pl.when(jnp.logical_not(first))
            def _():
                dsink_ref[:, h:h + 1] += dsinks[h]

    wq = A_WIDTH + kvw
    return pl.pallas_call(
        body, grid=(nb,),
        in_specs=[pl.BlockSpec((s, wq), lambda b: (b, 0)), pl.BlockSpec((1, A_HEADS), lambda b: (0, 0)),
                  pl.BlockSpec((s, A_WIDTH), lambda b: (b, 0))],
        out_specs=[pl.BlockSpec((s, wq), lambda b: (b, 0)), pl.BlockSpec((1, A_HEADS), lambda b: (0, 0))],
        out_shape=[jax.ShapeDtypeStruct((t, wq), f32), jax.ShapeDtypeStruct((1, A_HEADS), f32)],
        scratch_shapes=[pltpu.VMEM((s + A_WINDOW, kvw), f32), pltpu.VMEM((s + A_WINDOW, kvw), f32)],
        compiler_params=_params(("arbitrary",)), name=name,
    )(qkv, sinks, do)


def _rg_gates(xc, wa, wx, ba, bx, lam, nn):
    r = jax.nn.sigmoid(nn(xc, wa) + ba)
    i = jax.nn.sigmoid(nn(xc, wx) + bx)
    log_a = -RG_C * r * jax.nn.softplus(-lam)
    a = jnp.exp(log_a)
    mult = jnp.sqrt(-jnp.tanh(log_a) * (jnp.exp(2.0 * log_a) + 1.0))
    return a, mult * (i * xc)


def _linear_scan(a, u, reverse):
    s = a.shape[0]
    t = lax.broadcasted_iota(jnp.int32, a.shape, 0)
    d = 1
    while d < s:
        if reverse:
            keep = t < s - d
            shift = s - d
        else:
            keep = t >= d
            shift = d
        us = jnp.where(keep, pltpu.roll(u, shift, 0), 0.0)
        as_ = jnp.where(keep, pltpu.roll(a, shift, 0), 1.0)
        u = u + a * us
        a = a * as_
        d *= 2
    return u


def rglru_fwd(xc, bg, wa, wx, ba, bx, lam, nb, *, name):
    t, c = xc.shape
    s = t // nb
    cw = GROUP_W

    def body(xc_ref, bg_ref, wa_ref, wx_ref, ba_ref, bx_ref, lam_ref, y_ref, h_ref):
        a, u = _rg_gates(xc_ref[...], wa_ref[...], wx_ref[...], ba_ref[...], bx_ref[...], lam_ref[...], RAW_DOTS[0])
        h = _linear_scan(a, u, False)
        h_ref[...] = h
        y_ref[...] = h * jax.nn.gelu(bg_ref[...])

    slab = pl.BlockSpec((s, cw), lambda b, g: (b, g))
    wsp = pl.BlockSpec((None, cw, cw), lambda b, g: (g, 0, 0))
    vec = pl.BlockSpec((1, cw), lambda b, g: (0, g))
    return pl.pallas_call(
        body, grid=(nb, c // cw), in_specs=[slab, slab, wsp, wsp, vec, vec, vec], out_specs=[slab, slab],
        out_shape=[jax.ShapeDtypeStruct((t, c), f32)] * 2,
        compiler_params=_params(("parallel", "parallel")), name=name,
    )(xc, bg, wa, wx, ba, bx, lam)


def rglru_bwd(xc, bg, h, dy, wa, wx, ba, bx, lam, nb, *, name):
    t, c = xc.shape
    s = t // nb
    cw = GROUP_W

    def body(xc_ref, bg_ref, h_ref, dy_ref, wa_ref, wx_ref, ba_ref, bx_ref, lam_ref,
             dxc_ref, dbg_ref, dwa_ref, dwx_ref, dba_ref, dbx_ref, dlam_ref):
        h = h_ref[...]
        dy_ = dy_ref[...]
        gel, gel_vjp = jax.vjp(jax.nn.gelu, bg_ref[...])
        dbg_ref[...] = gel_vjp(dy_ * h)[0]
        dh = dy_ * gel
        gates = functools.partial(_rg_gates, nn=_bnn)
        (a, _), gates_vjp = jax.vjp(gates, xc_ref[...], wa_ref[...], wx_ref[...], ba_ref[...], bx_ref[...],
                                    lam_ref[...])
        ti = lax.broadcasted_iota(jnp.int32, a.shape, 0)
        a_next = jnp.where(ti < s - 1, pltpu.roll(a, s - 1, 0), 0.0)
        lam_t = _linear_scan(a_next, dh, True)
        h_prev = jnp.where(ti >= 1, pltpu.roll(h, 1, 0), 0.0)
        dxc, dwa, dwx, dba, dbx, dlam = gates_vjp((lam_t * h_prev, lam_t))
        dxc_ref[...] = dxc
        first = pl.program_id(1) == 0

        @pl.when(first)
        def _():
            dwa_ref[...] = dwa
            dwx_ref[...] = dwx
            dba_ref[...] = dba
            dbx_ref[...] = dbx
            dlam_ref[...] = dlam

        @pl.when(jnp.logical_not(first))
        def _():
            dwa_ref[...] += dwa
            dwx_ref[...] += dwx
            dba_ref[...] += dba
            dbx_ref[...] += dbx
            dlam_ref[...] += dlam

    slab = pl.BlockSpec((s, cw), lambda g, b: (b, g))
    wsp = pl.BlockSpec((None, cw, cw), lambda g, b: (g, 0, 0))
    vec = pl.BlockSpec((1, cw), lambda g, b: (0, g))
    ng = c // cw
    return pl.pallas_call(
        body, grid=(ng, nb), in_specs=[slab, slab, slab, slab, wsp, wsp, vec, vec, vec],
        out_specs=[slab, slab, wsp, wsp, vec, vec, vec],
        out_shape=[jax.ShapeDtypeStruct((t, c), f32), jax.ShapeDtypeStruct((t, c), f32),
                   jax.ShapeDtypeStruct((ng, cw, cw), f32), jax.ShapeDtypeStruct((ng, cw, cw), f32),
                   jax.ShapeDtypeStruct((1, c), f32), jax.ShapeDtypeStruct((1, c), f32),
                   jax.ShapeDtypeStruct((1, c), f32)],
        compiler_params=_params(("parallel", "arbitrary")), name=name,
    )(xc, bg, h, dy, wa, wx, ba, bx, lam)


def _gdn_chunk_prep(q, k, v, bl, al, a_log, dt_b, dots):
    _, nt, _ = dots
    hd = C_HEAD_DIM
    qn = q * lax.rsqrt(jnp.sum(q * q, -1, keepdims=True) + NORM_EPS) * (hd ** -0.5)
    kn = k * lax.rsqrt(jnp.sum(k * k, -1, keepdims=True) + NORM_EPS)
    beta = jax.nn.sigmoid(bl)
    g = -jnp.exp(a_log) * jax.nn.softplus(al + dt_b)
    ri = lax.broadcasted_iota(jnp.int32, (CHUNK, CHUNK), 0)
    ci = lax.broadcasted_iota(jnp.int32, (CHUNK, CHUNK), 1)
    tril = ri >= ci
    strict = ri > ci
    trilf = tril.astype(f32)
    eye = (ri == ci).astype(f32)
    gc_sq = _hdot(trilf, jnp.broadcast_to(g, (CHUNK, CHUNK)))
    gc = _hdot(trilf, jnp.broadcast_to(g, (CHUNK, hd)))
    gc_row = _hdot(jnp.ones((CHUNK, CHUNK), f32), eye * gc_sq)
    decay = jnp.where(tril, jnp.exp(jnp.where(tril, gc_sq - gc_row, 0.0)), 0.0)
    kb = kn * beta
    lmat = jnp.where(strict, nt(kb, kn) * decay, 0.0)
    pw = -lmat
    inv = eye + pw
    for _ in range(5):
        pw = _hdot(pw, pw)
        inv = inv + _hdot(inv, pw)
    egc = jnp.exp(gc)
    u = _hdot(inv, v * beta)
    w = _hdot(inv, kb * egc)
    attn = nt(qn, kn) * decay
    g_last = jnp.sum(jnp.broadcast_to(g, (CHUNK, hd)), 0, keepdims=True)
    return qn * egc, kn * jnp.exp(g_last - gc), w, u, attn, jnp.exp(g_last)


def _gdn_chunk_step(state, qg, kdec, w, u, attn, gl, z, ng, dots):
    nn, _, tn = dots
    v_new = u - nn(w, state)
    o = nn(qg, state) + nn(attn, v_new)
    state = state * gl + tn(kdec, v_new)
    o = o * lax.rsqrt(jnp.mean(o * o, -1, keepdims=True) + NORM_EPS) * ng
    return o * _silu(z), state


def _pick_lane(x, lane):
    li = lax.broadcasted_iota(jnp.int32, x.shape, 1)
    return jnp.sum(jnp.where(li == lane, x, 0.0), 1, keepdims=True)


def _put_lane(col, lane, width):
    li = lax.broadcasted_iota(jnp.int32, (col.shape[0], width), 1)
    return jnp.where(li == lane, col, 0.0)


def _gdn_specs(s, nc):
    hd = C_HEAD_DIM
    head = lambda off: pl.BlockSpec((s, hd), lambda b, h, off=off: (b, off + h))
    attn = pl.BlockSpec((None, s, CHUNK), lambda b, h: (h, b, 0))
    gl = pl.BlockSpec((None, nc * SUBLANES, hd), lambda b, h: (h, b, 0))
    ba = pl.BlockSpec((s, LANES), lambda b, h: (b, 0))
    sc8 = pl.BlockSpec((1, C_HEADS), lambda b, h: (0, 0))
    return head, attn, gl, ba, sc8


def gdn_prep_fwd(qkv, ba, a_log, dt_b, nb, *, name):
    t = qkv.shape[0]
    s = t // nb
    nc = s // CHUNK
    hd = C_HEAD_DIM
    head, attn_sp, gl_sp, ba_sp, sc8 = _gdn_specs(s, nc)

    def body(q_ref, k_ref, v_ref, ba_ref, alog_ref, dtb_ref, qg_ref, kd_ref, w_ref, u_ref, at_ref, gl_ref):
        h = pl.program_id(1)
        a_log_h = _pick_lane(alog_ref[...], h)
        dt_b_h = _pick_lane(dtb_ref[...], h)

        def chunk(n, carry):
            rows = pl.ds(pl.multiple_of(n * CHUNK, CHUNK), CHUNK)
            bav = ba_ref[rows, :]
            outs = _gdn_chunk_prep(q_ref[rows, :], k_ref[rows, :], v_ref[rows, :], _pick_lane(bav, h),
                                   _pick_lane(bav, C_HEADS + h), a_log_h, dt_b_h, RAW_DOTS)
            qg_ref[rows, :], kd_ref[rows, :], w_ref[rows, :], u_ref[rows, :], at_ref[rows, :] = outs[:5]
            gl_ref[pl.ds(pl.multiple_of(n * SUBLANES, SUBLANES), SUBLANES), :] = jnp.broadcast_to(outs[5], (SUBLANES, hd))
            return carry

        lax.fori_loop(0, nc, chunk, 0)

    big = jax.ShapeDtypeStruct((t, C_WIDTH), f32)
    return pl.pallas_call(
        body, grid=(nb, C_HEADS),
        in_specs=[head(0), head(C_HEADS), head(2 * C_HEADS), ba_sp, sc8, sc8],
        out_specs=[head(0)] * 4 + [attn_sp, gl_sp],
        out_shape=[big] * 4 + [jax.ShapeDtypeStruct((C_HEADS, t, CHUNK), f32),
                               jax.ShapeDtypeStruct((C_HEADS, nb * nc * SUBLANES, hd), f32)],
        compiler_params=_params(("parallel", "parallel")), name=name,
    )(qkv, qkv, qkv, ba, a_log, dt_b)


def gdn_prep_bwd(qkv, ba, a_log, dt_b, cts, nb, *, name):
    t = qkv.shape[0]
    s = t // nb
    nc = s // CHUNK
    hd = C_HEAD_DIM
    head, attn_sp, gl_sp, ba_sp, sc8 = _gdn_specs(s, nc)

    def body(q_ref, k_ref, v_ref, ba_ref, alog_ref, dtb_ref, cqg, ckd, cw_, cu, cat, cgl,
             dq_ref, dk_ref, dv_ref, dba_ref, dalog_ref, ddtb_ref):
        b = pl.program_id(0)
        h = pl.program_id(1)
        a_log_h = _pick_lane(alog_ref[...], h)
        dt_b_h = _pick_lane(dtb_ref[...], h)
        prep = functools.partial(_gdn_chunk_prep, dots=VJP_DOTS)

        @pl.when(h == 0)
        def _():
            dba_ref[...] = jnp.zeros((s, LANES), f32)

        def chunk(n, carry):
            da_log, ddt_b = carry
            rows = pl.ds(pl.multiple_of(n * CHUNK, CHUNK), CHUNK)
            bav = ba_ref[rows, :]
            _, vjp = jax.vjp(prep, q_ref[rows, :], k_ref[rows, :], v_ref[rows, :], _pick_lane(bav, h),
                             _pick_lane(bav, C_HEADS + h), a_log_h, dt_b_h)
            cgl_n = cgl[pl.ds(pl.multiple_of(n * SUBLANES, SUBLANES), SUBLANES), :][0:1, :]
            dq, dk, dv, dbl, dal, dalog_n, ddtb_n = vjp((cqg[rows, :], ckd[rows, :], cw_[rows, :], cu[rows, :],
                                                         cat[rows, :], cgl_n))
            dq_ref[rows, :] = dq
            dk_ref[rows, :] = dk
            dv_ref[rows, :] = dv
            dba_ref[rows, :] += _put_lane(dbl, h, LANES) + _put_lane(dal, C_HEADS + h, LANES)
            return da_log + dalog_n, ddt_b + ddtb_n

        da_log, ddt_b = lax.fori_loop(0, nc, chunk, (jnp.zeros((1, 1), f32), jnp.zeros((1, 1), f32)))
        first = jnp.logical_and(b == 0, h == 0)

        @pl.when(first)
        def _():
            dalog_ref[...] = _put_lane(da_log, h, LANES)
            ddtb_ref[...] = _put_lane(ddt_b, h, LANES)

        @pl.when(jnp.logical_not(first))
        def _():
            dalog_ref[...] += _put_lane(da_log, h, LANES)
            ddtb_ref[...] += _put_lane(ddt_b, h, LANES)

    big = jax.ShapeDtypeStruct((t, C_WIDTH), f32)
    vec = pl.BlockSpec((1, LANES), lambda b, h: (0, 0))
    return pl.pallas_call(
        body, grid=(nb, C_HEADS),
        in_specs=[head(0), head(C_HEADS), head(2 * C_HEADS), ba_sp, sc8, sc8] + [head(0)] * 4 + [attn_sp, gl_sp],
        out_specs=[head(0)] * 3 + [ba_sp, vec, vec],
        out_shape=[big] * 3 + [jax.ShapeDtypeStruct((t, LANES), f32), jax.ShapeDtypeStruct((1, LANES), f32),
                               jax.ShapeDtypeStruct((1, LANES), f32)],
        compiler_params=_params(("arbitrary", "arbitrary")), name=name,
    )(qkv, qkv, qkv, ba, a_log, dt_b, *cts)


def _gdn_rec_specs(s, nc, hp):
    hd = C_HEAD_DIM
    wide = pl.BlockSpec((s, hp * hd), lambda b, j: (b, j))
    attn = pl.BlockSpec((hp, s, CHUNK), lambda b, j: (j, b, 0))
    gl = pl.BlockSpec((hp, nc * SUBLANES, hd), lambda b, j: (j, b, 0))
    ng = pl.BlockSpec((1, hd), lambda b, j: (0, 0))
    return wide, attn, gl, ng


def gdn_rec_fwd(qg, kdec, w, u, attn, gl, z, ng, nb, *, name):
    t = qg.shape[0]
    s = t // nb
    nc = s // CHUNK
    hd = C_HEAD_DIM
    hp = C_HEADS_PER_STEP_FWD
    wide, attn_sp, gl_sp, ng_sp = _gdn_rec_specs(s, nc, hp)

    def body(qg_ref, kd_ref, w_ref, u_ref, at_ref, gl_ref, z_ref, ng_ref, y_ref):
        def chunk(n, states):
            rows = pl.ds(pl.multiple_of(n * CHUNK, CHUNK), CHUNK)
            grow = pl.ds(pl.multiple_of(n * SUBLANES, SUBLANES), SUBLANES)
            new = []
            for j in range(hp):
                cols = slice(j * hd, (j + 1) * hd)
                y, st = _gdn_chunk_step(states[j], qg_ref[rows, cols], kd_ref[rows, cols], w_ref[rows, cols],
                                        u_ref[rows, cols], at_ref[j, rows, :], gl_ref[j, grow, :][0:1, :],
                                        z_ref[rows, cols], ng_ref[...], RAW_DOTS)
                y_ref[rows, cols] = y
                new.append(st)
            return tuple(new)

        lax.fori_loop(0, nc, chunk, tuple(jnp.zeros((hd, hd), f32) for _ in range(hp)))

    return pl.pallas_call(
        body, grid=(nb, C_HEADS // hp),
        in_specs=[wide] * 4 + [attn_sp, gl_sp, wide, ng_sp], out_specs=wide,
        out_shape=jax.ShapeDtypeStruct((t, C_WIDTH), f32),
        compiler_params=_params(("parallel", "parallel")), name=name,
    )(qg, kdec, w, u, attn, gl, z, ng)


def gdn_rec_bwd(qg, kdec, w, u, attn, gl, z, ng, dy, nb, *, name):
    t = qg.shape[0]
    s = t // nb
    nc = s // CHUNK
    hd = C_HEAD_DIM
    hp = C_HEADS_PER_STEP_BWD
    wide, attn_sp, gl_sp, ng_sp = _gdn_rec_specs(s, nc, hp)

    def body(qg_ref, kd_ref, w_ref, u_ref, at_ref, gl_ref, z_ref, ng_ref, dy_ref,
             dqg_ref, dkd_ref, dw_ref, du_ref, dat_ref, dgl_ref, dz_ref, dng_ref, states):
        step = functools.partial(_gdn_chunk_step, dots=VJP_DOTS)

        def operands(n, j):
            rows = pl.ds(pl.multiple_of(n * CHUNK, CHUNK), CHUNK)
            grow = pl.ds(pl.multiple_of(n * SUBLANES, SUBLANES), SUBLANES)
            cols = slice(j * hd, (j + 1) * hd)
            return (qg_ref[rows, cols], kd_ref[rows, cols], w_ref[rows, cols], u_ref[rows, cols],
                    at_ref[j, rows, :], gl_ref[j, grow, :][0:1, :], z_ref[rows, cols], ng_ref[...])

        def fwd_chunk(n, sts):
            new = []
            for j in range(hp):
                states[j, n] = sts[j]
                _, st = _gdn_chunk_step(sts[j], *operands(n, j), RAW_DOTS)
                new.append(st)
            return tuple(new)

        lax.fori_loop(0, nc, fwd_chunk, tuple(jnp.zeros((hd, hd), f32) for _ in range(hp)))

        def bwd_chunk(i, carry):
            n = nc - 1 - i
            rows = pl.ds(pl.multiple_of(n * CHUNK, CHUNK), CHUNK)
            grow = pl.ds(pl.multiple_of(n * SUBLANES, SUBLANES), SUBLANES)
            dsts, dng = carry
            new = []
            for j in range(hp):
                cols = slice(j * hd, (j + 1) * hd)
                _, vjp = jax.vjp(step, states[j, n], *operands(n, j))
                dst, dqg, dkd, dw, du, dat, dgl, dz, dng_n = vjp((dy_ref[rows, cols], dsts[j]))
                dqg_ref[rows, cols] = dqg
                dkd_ref[rows, cols] = dkd
                dw_ref[rows, cols] = dw
                du_ref[rows, cols] = du
                dat_ref[j, rows, :] = dat
                dgl_ref[j, grow, :] = jnp.broadcast_to(dgl, (SUBLANES, hd))
                dz_ref[rows, cols] = dz
                dng = dng + dng_n
                new.append(dst)
            return tuple(new), dng

        _, dng = lax.fori_loop(0, nc, bwd_chunk,
                               (tuple(jnp.zeros((hd, hd), f32) for _ in range(hp)), jnp.zeros((1, hd), f32)))
        first = jnp.logical_and(pl.program_id(0) == 0, pl.program_id(1) == 0)

        @pl.when(first)
        def _():
            dng_ref[...] = dng

        @pl.when(jnp.logical_not(first))
        def _():
            dng_ref[...] += dng

    big = jax.ShapeDtypeStruct((t, C_WIDTH), f32)
    return pl.pallas_call(
        body, grid=(nb, C_HEADS // hp),
        in_specs=[wide] * 4 + [attn_sp, gl_sp, wide, ng_sp, wide],
        out_specs=[wide] * 4 + [attn_sp, gl_sp, wide, ng_sp],
        out_shape=[big] * 4 + [jax.ShapeDtypeStruct(attn.shape, f32), jax.ShapeDtypeStruct(gl.shape, f32), big,
                               jax.ShapeDtypeStruct((1, hd), f32)],
        scratch_shapes=[pltpu.VMEM((hp, nc, hd, hd), f32)],
        compiler_params=_params(("arbitrary", "arbitrary")), name=name,
    )(qg, kdec, w, u, attn, gl, z, ng, dy)


def _blockdiag_slabs(w):
    per = GROUP_W // B_BLOCK
    slabs = jnp.zeros((B_BLOCKS // per, GROUP_W, GROUP_W), w.dtype)
    for h in range(B_BLOCKS):
        o = (h % per) * B_BLOCK
        slabs = slabs.at[h // per, o:o + B_BLOCK, o:o + B_BLOCK].set(w[h])
    return slabs


def _slab_blocks(slabs):
    per = GROUP_W // B_BLOCK
    return jnp.stack([slabs[h // per, (h % per) * B_BLOCK:(h % per + 1) * B_BLOCK,
                            (h % per) * B_BLOCK:(h % per + 1) * B_BLOCK] for h in range(B_BLOCKS)])


def _mixer_ab_fwd(x1, W, g, b, nb, tag):
    w_in = W["ab_w_in"][0].astype(bf16)
    o1, o2 = A_WIDTH + 2 * A_KV_WIDTH, A_WIDTH + 2 * A_KV_WIDTH + B_WIDTH
    w_qkv, w_bx, w_bg = w_in[:, :o1], w_in[:, o1:o2], w_in[:, o2:]
    pqkv = mm_nn(x1, w_qkv, name=tag + "_in_qkv")
    pbx = mm_nn(x1, w_bx, name=tag + "_in_bx")
    pbg = mm_nn(x1, w_bg, name=tag + "_in_bg")
    ya = attn_fwd(pqkv, W["a_sinks"], nb, name=tag + "_attn_fwd")
    xc = conv_fwd(pbx, W["b_conv_w"][0], W["b_conv_b"], False, nb, name=tag + "_conv_fwd")
    wa_s, wx_s = _blockdiag_slabs(W["b_wa"][0]), _blockdiag_slabs(W["b_wx"][0])
    yb, hh = rglru_fwd(xc, pbg, wa_s, wx_s, W["b_ba"], W["b_bx"], W["b_lam"], nb, name=tag + "_rglru_fwd")
    w_out = W["ab_w_out"][0].astype(bf16)
    x2, z1 = proj_ln([ya, yb], [w_out[:A_WIDTH], w_out[A_WIDTH:]], x1, g, b, name=tag + "_out_ln")
    saved = (pqkv, pbx, pbg, ya, xc, yb, hh, wa_s, wx_s, w_qkv, w_bx, w_bg, w_out)
    return x2, z1, saved


def _mixer_ab_bwd(x1, dz1, dz1b, W, saved, nb, tag):
    pqkv, pbx, pbg, ya, xc, yb, hh, wa_s, wx_s, w_qkv, w_bx, w_bg, w_out = saved
    dya = mm_nn(dz1b, w_out[:A_WIDTH].T, name=tag + "_dya")
    dyb = mm_nn(dz1b, w_out[A_WIDTH:].T, name=tag + "_dyb")
    dwo = jnp.concatenate([mm_tn(ya, dz1b, name=tag + "_dwo_a"), mm_tn(yb, dz1b, name=tag + "_dwo_b")], 0)
    dpqkv, dsinks = attn_bwd(pqkv, W["a_sinks"], dya, nb, name=tag + "_attn_bwd")
    dxc, dpbg, dwa_s, dwx_s, dba, dbx, dlam = rglru_bwd(xc, pbg, hh, dyb, wa_s, wx_s, W["b_ba"], W["b_bx"],
                                                       W["b_lam"], nb, name=tag + "_rglru_bwd")
    dpbx, dconv_w, dconv_b = conv_bwd(pbx, W["b_conv_w"][0], W["b_conv_b"], dxc, False, nb, name=tag + "_conv_bwd")
    dw_in = jnp.concatenate([mm_tn(x1, dpqkv, name=tag + "_dwin_qkv"), mm_tn(x1, dpbx, name=tag + "_dwin_bx"),
                             mm_tn(x1, dpbg, name=tag + "_dwin_bg")], 1)
    dx1 = mm_nn(dpqkv, w_qkv.T, add=dz1, add_scale=DN_ALPHA, name=tag + "_dx_qkv")
    dx1 = mm_nn(dpbx, w_bx.T, add=dx1, name=tag + "_dx_bx")
    dx1 = mm_nn(dpbg, w_bg.T, add=dx1, name=tag + "_dx_bg")
    grads = {"ab_w_in": dw_in[None], "a_sinks": dsinks, "b_conv_w": dconv_w[None], "b_conv_b": dconv_b,
             "b_wa": _slab_blocks(dwa_s)[None], "b_ba": dba, "b_wx": _slab_blocks(dwx_s)[None], "b_bx": dbx,
             "b_lam": dlam, "ab_w_out": dwo[None]}
    return dx1, grads


def _mixer_c_fwd(x1, W, g, b, nb, tag):
    w_in = W["c_w_in"][0].astype(bf16)
    d = w_in.shape[0]
    o1, o2 = 3 * C_WIDTH, 4 * C_WIDTH
    w_qkv, w_z = w_in[:, :o1], w_in[:, o1:o2]
    w_ba = jnp.concatenate([w_in[:, o2:], jnp.zeros((d, LANES - 2 * C_HEADS), bf16)], 1)
    pqkv = mm_nn(x1, w_qkv, name=tag + "_in_qkv")
    pz = mm_nn(x1, w_z, name=tag + "_in_z")
    pba = mm_nn(x1, w_ba, name=tag + "_in_ba")
    zero_b = jnp.zeros((1, o1), f32)
    qkvc = conv_fwd(pqkv, W["c_conv_w"][0], zero_b, True, nb, name=tag + "_conv_fwd")
    prep = gdn_prep_fwd(qkvc, pba, W["c_a_log"], W["c_dt_bias"], nb, name=tag + "_prep_fwd")
    yc = gdn_rec_fwd(*prep, pz, W["c_norm_g"], nb, name=tag + "_rec_fwd")
    w_out = W["c_w_out"][0].astype(bf16)
    x2, z1 = proj_ln([yc], [w_out], x1, g, b, name=tag + "_out_ln")
    saved = (pqkv, pz, pba, qkvc, prep, yc, w_qkv, w_z, w_ba, w_out, zero_b)
    return x2, z1, saved


def _mixer_c_bwd(x1, dz1, dz1b, W, saved, nb, tag):
    pqkv, pz, pba, qkvc, prep, yc, w_qkv, w_z, w_ba, w_out, zero_b = saved
    dyc = mm_nn(dz1b, w_out.T, name=tag + "_dyc")
    dwo = mm_tn(yc, dz1b, name=tag + "_dwo")
    rec = gdn_rec_bwd(*prep, pz, W["c_norm_g"], dyc, nb, name=tag + "_rec_bwd")
    cts, dpz, dng = rec[:6], rec[6], rec[7]
    dq, dk, dv, dpba, dalog, ddtb = gdn_prep_bwd(qkvc, pba, W["c_a_log"], W["c_dt_bias"], cts, nb,
                                                 name=tag + "_prep_bwd")
    dqkvc = jnp.concatenate([dq, dk, dv], 1)
    dpqkv, dconv_w, _ = conv_bwd(pqkv, W["c_conv_w"][0], zero_b, dqkvc, True, nb, name=tag + "_conv_bwd")
    dw_in = jnp.concatenate([mm_tn(x1, dpqkv, name=tag + "_dwin_qkv"), mm_tn(x1, dpz, name=tag + "_dwin_z"),
                             mm_tn(x1, dpba, name=tag + "_dwin_ba")[:, :2 * C_HEADS]], 1)
    dx1 = mm_nn(dpqkv, w_qkv.T, add=dz1, add_scale=DN_ALPHA, name=tag + "_dx_qkv")
    dx1 = mm_nn(dpz, w_z.T, add=dx1, name=tag + "_dx_z")
    dx1 = mm_nn(dpba, w_ba.T, add=dx1, name=tag + "_dx_ba")
    grads = {"c_w_in": dw_in[None], "c_conv_w": dconv_w[None], "c_a_log": dalog[:, :C_HEADS],
             "c_dt_bias": ddtb[:, :C_HEADS], "c_norm_g": dng, "c_w_out": dwo[None]}
    return dx1, grads


def _local_step(x, p, target, W):
    nb, s, d = x.shape
    t = nb * s
    h = x.reshape(t, d)
    tape = []
    for i in range(DEPTH):
        tag = f"l{i}"
        f1 = [W[k][i].astype(bf16) for k in ("ffn1_wg", "ffn1_wu", "ffn1_wd")]
        f2 = [W[k][i].astype(bf16) for k in ("ffn2_wg", "ffn2_wu", "ffn2_wd")]
        lg = [W["ln_g"][i, k][None] for k in range(3)]
        lb = [W["ln_b"][i, k][None] for k in range(3)]
        x1, z0 = ffn_fwd(h, *f1, lg[0], lb[0], name=tag + "_ffn1_fwd")
        mixer = _mixer_ab_fwd if i % 2 == 0 else _mixer_c_fwd
        x2, z1, msaved = mixer(x1, W, lg[1], lb[1], nb, tag + "_mix")
        x3, z2 = ffn_fwd(x2, *f2, lg[2], lb[2], name=tag + "_ffn2_fwd")
        pi = p[i].reshape(t, -1)
        pw = (W["ple_wg"][i].astype(bf16), W["ple_bg"][i][None], W["ple_wp"][i].astype(bf16))
        x4 = ple_fwd(x3, pi, *pw, name=tag + "_ple_fwd")
        tape.append((h, z0, x1, msaved, z1, x2, z2, x3, pi, pw, f1, f2, lg))
        h = x4
    dh, sq = loss_head(h, target.reshape(t, d), name="loss_head")
    loss = 0.5 * jnp.sum(sq) / d
    per_layer = [None] * DEPTH
    grads = {}
    for i in reversed(range(DEPTH)):
        tag = f"l{i}"
        h_in, z0, x1, msaved, z1, x2, z2, x3, pi, pw, f1, f2, lg = tape[i]
        dx3, dple_wg, dple_bg, dple_wp = ple_bwd(x3, pi, dh, pw[0], pw[0].T, pw[1], pw[2], name=tag + "_ple_bwd")
        dz2, dz2b, dg2, db2 = ln_bwd(z2, dx3, lg[2], name=tag + "_ln2_bwd")
        dgate, dup, dwg2, dwu2, dwd2 = ffn_bwd_weights(x2.astype(bf16), dz2b, f2[0], f2[1], f2[2].T,
                                                       name=tag + "_ffn2_bwd_w")
        dx2 = ffn_bwd_input(dgate, dup, f2[0].T, f2[1].T, dz2, name=tag + "_ffn2_bwd_x")
        dz1, dz1b, dg1, db1 = ln_bwd(z1, dx2, lg[1], name=tag + "_ln1_bwd")
        mixer_bwd = _mixer_ab_bwd if i % 2 == 0 else _mixer_c_bwd
        dx1, mgrads = mixer_bwd(x1, dz1, dz1b, W, msaved, nb, tag + "_mix")
        grads.update(mgrads)
        dz0, dz0b, dg0, db0 = ln_bwd(z0, dx1, lg[0], name=tag + "_ln0_bwd")
        dgate, dup, dwg1, dwu1, dwd1 = ffn_bwd_weights(h_in.astype(bf16), dz0b, f1[0], f1[1], f1[2].T,
                                                       name=tag + "_ffn1_bwd_w")
        dh = ffn_bwd_input(dgate, dup, f1[0].T, f1[1].T, dz0, name=tag + "_ffn1_bwd_x")
        per_layer[i] = {"ffn1_wg": dwg1, "ffn1_wu": dwu1, "ffn1_wd": dwd1, "ffn2_wg": dwg2, "ffn2_wu": dwu2,
                        "ffn2_wd": dwd2, "ln_g": jnp.concatenate([dg0, dg1, dg2], 0),
                        "ln_b": jnp.concatenate([db0, db1, db2], 0), "ple_wg": dple_wg, "ple_bg": dple_bg[0],
                        "ple_wp": dple_wp}
    for k in per_layer[0]:
        grads[k] = jnp.stack([per_layer[i][k] for i in range(DEPTH)])
    return loss, dh.reshape(nb, s, d), grads


WEIGHT_NAMES = ("ffn1_wg", "ffn1_wu", "ffn1_wd", "ffn2_wg", "ffn2_wu", "ffn2_wd", "ln_g", "ln_b", "ple_wg", "ple_bg",
                "ple_wp", "ab_w_in", "a_sinks", "b_conv_w", "b_conv_b", "b_wa", "b_ba", "b_wx", "b_bx", "b_lam",
                "ab_w_out", "c_w_in", "c_conv_w", "c_a_log", "c_dt_bias", "c_norm_g", "c_w_out")
SHARD_AXIS = {"ffn1_wg": 2, "ffn1_wu": 2, "ffn1_wd": 1, "ffn2_wg": 2, "ffn2_wu": 2, "ffn2_wd": 1, "ln_g": 2, "ln_b": 2,
              "ple_wg": 1, "ple_wp": 2, "ab_w_in": 2, "b_conv_w": 2, "ab_w_out": 1, "c_w_in": 2, "c_conv_w": 2,
              "c_w_out": 1}
N_CHIPS = 4
PACK_COLS = 512
MESH = pl.DeviceIdType.MESH
ANY = pl.BlockSpec(memory_space=pl.ANY)


def _pack_rows(n):
    unit = 2 * SUBLANES * PACK_COLS
    return -(-n // unit) * 2 * SUBLANES


def _pack(pieces, lead=()):
    k = len(lead)
    flat = jnp.concatenate([a.reshape(lead + (-1,)) for a in pieces], axis=k)
    rows = _pack_rows(flat.shape[k])
    flat = jnp.pad(flat, [(0, 0)] * k + [(0, rows * PACK_COLS - flat.shape[k])])
    return flat.reshape(lead + (rows, PACK_COLS))


def _unpack(pack, shapes, lead=()):
    k = len(lead)
    flat = pack.reshape(lead + (-1,))
    out, o = [], 0
    for shp in shapes:
        n = 1
        for dim in shp:
            n *= dim
        out.append(lax.slice_in_dim(flat, o, o + n, axis=k).reshape(lead + tuple(shp)))
        o += n
    return out


def _mesh_position():
    x, y, c = lax.axis_index("x"), lax.axis_index("y"), lax.axis_index("c")
    chips = [(1 - x, y), (x, 1 - y), (1 - x, 1 - y)]
    return x, y, c, chips


def _remote(src, dst, send_sems, recv_sems, k, to):
    return pltpu.make_async_remote_copy(src_ref=src, dst_ref=dst, send_sem=send_sems.at[k], recv_sem=recv_sems.at[k],
                                        device_id=to, device_id_type=MESH)


def gather_shards(wpack, *, name):
    r, cols = wpack.shape
    rh = r // 2

    def body(w_ref, out_ref, send_sems, recv_sems, local_sem):
        x, y, c, chips = _mesh_position()
        me = 2 * x + y
        sibling = (x, y, 1 - c)

        def half(chip, hc):
            return out_ref.at[chip, pl.ds(hc * rh, rh), :]

        mine = pltpu.make_async_copy(w_ref, out_ref.at[me], local_sem)
        mine.start()
        first = [_remote(w_ref.at[pl.ds(c * rh, rh), :], half(me, c), send_sems, recv_sems, j, (cx, cy, c))
                 for j, (cx, cy) in enumerate(chips)]
        for cp in first:
            cp.start()
        passed = []
        for j, (cx, cy) in enumerate(chips):
            got = half(2 * cx + cy, c)
            _remote(got, got, send_sems, recv_sems, j, (cx, cy, c)).wait_recv()
            fw = _remote(got, got, send_sems, recv_sems, 3 + j, sibling)
            fw.start()
            passed.append(fw)
        for j, (cx, cy) in enumerate(chips):
            got = half(2 * cx + cy, 1 - c)
            _remote(got, got, send_sems, recv_sems, 3 + j, sibling).wait_recv()
        for cp in first + passed:
            cp.wait_send()
        mine.wait()

    return pl.pallas_call(
        body, out_shape=jax.ShapeDtypeStruct((N_CHIPS, r, cols), wpack.dtype), in_specs=[ANY], out_specs=ANY,
        scratch_shapes=[pltpu.SemaphoreType.DMA((6,)), pltpu.SemaphoreType.DMA((6,)), pltpu.SemaphoreType.DMA],
        name=name,
    )(wpack)


def sibling_exchange(gpack, *, name):
    n, r, cols = gpack.shape
    rh = r // 2

    def body(g_ref, out_ref, send_sems, recv_sems):
        x, y, c, _ = _mesh_position()
        cp = _remote(g_ref.at[:, pl.ds((1 - c) * rh, rh), :], out_ref, send_sems, recv_sems, 0, (x, y, 1 - c))
        cp.start()
        cp.wait()

    return pl.pallas_call(
        body, out_shape=jax.ShapeDtypeStruct((n, rh, cols), gpack.dtype), in_specs=[ANY], out_specs=ANY,
        scratch_shapes=[pltpu.SemaphoreType.DMA((1,)), pltpu.SemaphoreType.DMA((1,))], name=name,
    )(gpack)


def add_own_half(gpack, other, c_idx, *, name):
    n, r, cols = gpack.shape
    rh = r // 2
    tr = _tile(rh, 512, SUBLANES)
    nt = rh // tr

    def body(c_ref, g_ref, o_ref, out_ref):
        out_ref[...] = g_ref[...] + o_ref[...]

    return pl.pallas_call(
        body,
        grid_spec=pltpu.PrefetchScalarGridSpec(
            num_scalar_prefetch=1, grid=(n, nt),
            in_specs=[pl.BlockSpec((None, tr, cols), lambda s, i, c_ref: (s, c_ref[0] * nt + i, 0)),
                      pl.BlockSpec((None, tr, cols), lambda s, i, c_ref: (s, i, 0))],
            out_specs=pl.BlockSpec((None, tr, cols), lambda s, i, c_ref: (s, i, 0))),
        out_shape=jax.ShapeDtypeStruct((n, rh, cols), f32),
        compiler_params=_params(("parallel", "parallel")), name=name,
    )(c_idx, gpack, other)


def chip_exchange(pk, *, name):
    n, rh, cols = pk.shape

    def body(p_ref, q_ref, send_sems, recv_sems, local_sem):
        x, y, c, chips = _mesh_position()
        me = 2 * x + y
        mine = pltpu.make_async_copy(p_ref.at[me], q_ref.at[me], local_sem)
        mine.start()
        sends = [_remote(p_ref.at[2 * cx + cy], q_ref.at[me], send_sems, recv_sems, j, (cx, cy, c))
                 for j, (cx, cy) in enumerate(chips)]
        for cp in sends:
            cp.start()
        for j, (cx, cy) in enumerate(chips):
            got = q_ref.at[2 * cx + cy]
            _remote(got, got, send_sems, recv_sems, j, (cx, cy, c)).wait_recv()
        for cp in sends:
            cp.wait_send()
        mine.wait()

    return pl.pallas_call(
        body, out_shape=jax.ShapeDtypeStruct((n, rh, cols), pk.dtype), in_specs=[ANY], out_specs=ANY,
        scratch_shapes=[pltpu.SemaphoreType.DMA((3,)), pltpu.SemaphoreType.DMA((3,)), pltpu.SemaphoreType.DMA],
        name=name,
    )(pk)


def sum_slots(q, *, name):
    n, rh, cols = q.shape
    tr = _tile(rh, 512, SUBLANES)

    def body(q_ref, out_ref):
        acc = q_ref[0] + q_ref[1]
        for i in range(2, n):
            acc = acc + q_ref[i]
        out_ref[...] = acc

    return pl.pallas_call(
        body, grid=(rh // tr,), in_specs=[pl.BlockSpec((n, tr, cols), lambda i: (0, i, 0))],
        out_specs=pl.BlockSpec((tr, cols), lambda i: (i, 0)), out_shape=jax.ShapeDtypeStruct((rh, cols), f32),
        compiler_params=_params(("parallel",)), name=name,
    )(q)


def sibling_share(fh, *, name):
    rh, cols = fh.shape

    def body(f_ref, out_ref, send_sems, recv_sems, local_sem):
        x, y, c, _ = _mesh_position()
        sibling = (x, y, 1 - c)
        own = out_ref.at[pl.ds(c * rh, rh), :]
        mine = pltpu.make_async_copy(f_ref, own, local_sem)
        mine.start()
        cp = _remote(f_ref, own, send_sems, recv_sems, 0, sibling)
        cp.start()
        theirs = out_ref.at[pl.ds((1 - c) * rh, rh), :]
        _remote(theirs, theirs, send_sems, recv_sems, 0, sibling).wait_recv()
        cp.wait_send()
        mine.wait()

    return pl.pallas_call(
        body, out_shape=jax.ShapeDtypeStruct((2 * rh, cols), fh.dtype), in_specs=[ANY], out_specs=ANY,
        scratch_shapes=[pltpu.SemaphoreType.DMA((1,)), pltpu.SemaphoreType.DMA((1,)), pltpu.SemaphoreType.DMA],
        name=name,
    )(fh)


def adamw(w, g, m, v, *, name):
    r, cols = w.shape
    tr = _tile(r, 512, SUBLANES)

    def body(w_ref, g_ref, m_ref, v_ref, d_ref, m2_ref, v2_ref):
        g_ = g_ref[...]
        m2 = ADAM_B1 * m_ref[...] + (1.0 - ADAM_B1) * g_
        v2 = ADAM_B2 * v_ref[...] + (1.0 - ADAM_B2) * (g_ * g_)
        m_hat = m2 / (1.0 - ADAM_B1 ** ADAM_STEP)
        v_hat = v2 / (1.0 - ADAM_B2 ** ADAM_STEP)
        d_ref[...] = -ADAM_LR * (m_hat / (jnp.sqrt(v_hat) + ADAM_EPS) + ADAM_WD * w_ref[...])
        m2_ref[...] = m2
        v2_ref[...] = v2

    row = pl.BlockSpec((tr, cols), lambda i: (i, 0))
    return pl.pallas_call(
        body, grid=(r // tr,), in_specs=[row] * 4, out_specs=[row] * 3,
        out_shape=[jax.ShapeDtypeStruct((r, cols), f32)] * 3,
        compiler_params=_params(("parallel",)), name=name,
    )(w, g, m, v)


def _full_weights(gathered, local, shapes):
    pieces = _unpack(gathered, shapes, lead=(N_CHIPS,))
    full = {}
    for name, loc, pc in zip(WEIGHT_NAMES, local, pieces):
        ax = SHARD_AXIS.get(name)
        full[name] = loc if ax is None else jnp.concatenate([pc[s] for s in range(N_CHIPS)], axis=ax)
    return full


def _grad_pack(grads, shapes):
    pieces = []
    for name, shp in zip(WEIGHT_NAMES, shapes):
        g = grads[name]
        ax = SHARD_AXIS.get(name)
        if ax is None:
            pieces.append(jnp.broadcast_to(g.reshape(shp)[None], (N_CHIPS,) + tuple(shp)))
        else:
            pieces.append(jnp.stack(jnp.split(g, N_CHIPS, axis=ax)))
    return _pack(pieces, lead=(N_CHIPS,))


def _train_step(x, p, loss_target, weights, m, v):
    shapes = [w.shape for w in weights]
    wpack = _pack(weights)
    gathered = gather_shards(wpack, name="comm_gather_weights")
    full = _full_weights(gathered, weights, shapes)
    loss, grad_x, grads = _local_step(x, p, loss_target, full)
    gpack = _grad_pack(grads, shapes)
    c_idx = lax.axis_index("c").astype(jnp.int32).reshape(1)
    other = sibling_exchange(gpack, name="comm_grad_sibling")
    chip_sum = add_own_half(gpack, other, c_idx, name="grad_add_sibling")
    slots = chip_exchange(chip_sum, name="comm_grad_chips")
    half = sum_slots(slots, name="grad_sum_chips")
    gsum = sibling_share(half, name="comm_grad_share")
    delta, m2, v2 = adamw(wpack, gsum, _pack(m), _pack(v), name="adamw")
    loss = lax.psum(loss, ("x", "y", "c"))
    return (loss, grad_x, *_unpack(gsum, shapes), *_unpack(delta, shapes), *_unpack(m2, shapes),
            *_unpack(v2, shapes))


def kernel(x, p, ffn1_wg, ffn1_wu, ffn1_wd, ffn2_wg, ffn2_wu, ffn2_wd, ln_g, ln_b, ple_wg, ple_bg, ple_wp, ab_w_in, a_sinks, b_conv_w, b_conv_b, b_wa, b_ba, b_wx, b_bx, b_lam, ab_w_out, c_w_in, c_conv_w, c_a_log, c_dt_bias, c_norm_g, c_w_out, loss_target, m_ffn1_wg, m_ffn1_wu, m_ffn1_wd, m_ffn2_wg, m_ffn2_wu, m_ffn2_wd, m_ln_g, m_ln_b, m_ple_wg, m_ple_bg, m_ple_wp, m_ab_w_in, m_a_sinks, m_b_conv_w, m_b_conv_b, m_b_wa, m_b_ba, m_b_wx, m_b_bx, m_b_lam, m_ab_w_out, m_c_w_in, m_c_conv_w, m_c_a_log, m_c_dt_bias, m_c_norm_g, m_c_w_out, v_ffn1_wg, v_ffn1_wu, v_ffn1_wd, v_ffn2_wg, v_ffn2_wu, v_ffn2_wd, v_ln_g, v_ln_b, v_ple_wg, v_ple_bg, v_ple_wp, v_ab_w_in, v_a_sinks, v_b_conv_w, v_b_conv_b, v_b_wa, v_b_ba, v_b_wx, v_b_bx, v_b_lam, v_ab_w_out, v_c_w_in, v_c_conv_w, v_c_a_log, v_c_dt_bias, v_c_norm_g, v_c_w_out):
    weights = [ffn1_wg, ffn1_wu, ffn1_wd, ffn2_wg, ffn2_wu, ffn2_wd, ln_g, ln_b, ple_wg, ple_bg, ple_wp, ab_w_in, a_sinks,
               b_conv_w, b_conv_b, b_wa, b_ba, b_wx, b_bx, b_lam, ab_w_out, c_w_in, c_conv_w, c_a_log, c_dt_bias, c_norm_g,
               c_w_out]
    m = [m_ffn1_wg, m_ffn1_wu, m_ffn1_wd, m_ffn2_wg, m_ffn2_wu, m_ffn2_wd, m_ln_g, m_ln_b, m_ple_wg, m_ple_bg, m_ple_wp,
         m_ab_w_in, m_a_sinks, m_b_conv_w, m_b_conv_b, m_b_wa, m_b_ba, m_b_wx, m_b_bx, m_b_lam, m_ab_w_out, m_c_w_in,
         m_c_conv_w, m_c_a_log, m_c_dt_bias, m_c_norm_g, m_c_w_out]
    v = [v_ffn1_wg, v_ffn1_wu, v_ffn1_wd, v_ffn2_wg, v_ffn2_wu, v_ffn2_wd, v_ln_g, v_ln_b, v_ple_wg, v_ple_bg, v_ple_wp,
         v_ab_w_in, v_a_sinks, v_b_conv_w, v_b_conv_b, v_b_wa, v_b_ba, v_b_wx, v_b_bx, v_b_lam, v_ab_w_out, v_c_w_in,
         v_c_conv_w, v_c_a_log, v_c_dt_bias, v_c_norm_g, v_c_w_out]
    return _train_step(x, p, loss_target, weights, m, v)
```

```python
import functools

import jax
import jax.numpy as jnp
from jax import lax
from jax.experimental import pallas as pl
from jax.experimental.pallas import tpu as pltpu

f32 = jnp.float32
bf16 = jnp.bfloat16

DEPTH = 2
CHUNK = 64
A_HEADS, A_KV_HEADS, A_GROUP, A_HEAD_DIM = 8, 2, 4, 64
A_WIDTH, A_KV_WIDTH, A_WINDOW = 512, 128, 128
B_WIDTH, B_BLOCKS, B_BLOCK, B_CONV = 512, 8, 64, 4
RG_C = 8.0
C_HEADS, C_HEAD_DIM, C_WIDTH, C_CONV = 8, 128, 1024, 4
DN_ALPHA = (2.0 * DEPTH) ** 0.25
LN_EPS = 1e-5
NORM_EPS = 1e-6
NEG = -1e30
ADAM_LR, ADAM_B1, ADAM_B2, ADAM_EPS, ADAM_WD, ADAM_STEP = 0.001, 0.9, 0.999, 1e-08, 0.01, 10

VMEM_LIMIT_BYTES = 56 * 1024 * 1024
LANES = 128
SUBLANES = 8
GROUP_W = 128
CHUNK_UNROLL = 2
C_HEADS_PER_STEP_FWD = 2
C_HEADS_PER_STEP_BWD = 1

NN = ((1,), (0,))
NT = ((1,), (1,))
TN = ((0,), (0,))


def _params(sem):
    return pltpu.CompilerParams(dimension_semantics=sem, vmem_limit_bytes=VMEM_LIMIT_BYTES)


def _tile(n, cap, mult):
    best = None
    t = mult
    while t <= min(n, cap):
        if n % t == 0:
            best = t
        t += mult
    return best if best is not None else n


def _bdot(a, b, dims):
    return lax.dot_general(a.astype(bf16), b.astype(bf16), (dims, ((), ())), preferred_element_type=f32)


def _h3dot(a, b):
    return lax.dot_general(a, b, (NN, ((), ())), precision=lax.Precision.HIGH, preferred_element_type=f32)


def _running_sum(x, reverse):
    s = x.shape[0]
    t = lax.broadcasted_iota(jnp.int32, x.shape, 0)
    d = 1
    while d < s:
        if reverse:
            x = x + jnp.where(t < s - d, pltpu.roll(x, s - d, 0), 0.0)
        else:
            x = x + jnp.where(t >= d, pltpu.roll(x, d, 0), 0.0)
        d *= 2
    return x


@jax.custom_vjp
def _cumsum0(x):
    return _running_sum(x, False)


def _cumsum0_fwd(x):
    return _running_sum(x, False), None


def _cumsum0_bwd(_, g):
    return (_running_sum(g, True),)


_cumsum0.defvjp(_cumsum0_fwd, _cumsum0_bwd)


@jax.custom_vjp
def _bnn(a, b):
    return _bdot(a, b, NN)


def _bnn_fwd(a, b):
    return _bdot(a, b, NN), (a, b)


def _bnn_bwd(res, g):
    a, b = res
    return _bdot(g, b, NT), _bdot(a, g, TN)


_bnn.defvjp(_bnn_fwd, _bnn_bwd)


@jax.custom_vjp
def _bnt(a, b):
    return _bdot(a, b, NT)


def _bnt_fwd(a, b):
    return _bdot(a, b, NT), (a, b)


def _bnt_bwd(res, g):
    a, b = res
    return _bdot(g, b, NN), _bdot(g, a, TN)


_bnt.defvjp(_bnt_fwd, _bnt_bwd)


@jax.custom_vjp
def _btn(a, b):
    return _bdot(a, b, TN)


def _btn_fwd(a, b):
    return _bdot(a, b, TN), (a, b)


def _btn_bwd(res, g):
    a, b = res
    return _bdot(b, g, NT), _bdot(a, g, NN)


_btn.defvjp(_btn_fwd, _btn_bwd)

RAW_DOTS = (lambda a, b: _bdot(a, b, NN), lambda a, b: _bdot(a, b, NT), lambda a, b: _bdot(a, b, TN),
            lambda x: _running_sum(x, False))
VJP_DOTS = (_bnn, _bnt, _btn, _cumsum0)


def _layer_norm(z, g, b):
    mu = jnp.mean(z, -1, keepdims=True)
    d = z - mu
    var = jnp.mean(d * d, -1, keepdims=True)
    return d * lax.rsqrt(var + LN_EPS) * g + b


def _silu(x):
    return x * jax.nn.sigmoid(x)


def mm_nn(a, w, add=None, add_scale=1.0, *, name):
    m, k = a.shape
    n = w.shape[1]
    tm = _tile(m, 512, SUBLANES)
    tn = _tile(n, 1024, LANES)

    def body(*refs):
        if add is None:
            a_ref, w_ref, o_ref = refs
            o_ref[...] = _bdot(a_ref[...], w_ref[...], NN)
        else:
            a_ref, w_ref, add_ref, o_ref = refs
            o_ref[...] = _bdot(a_ref[...], w_ref[...], NN) + add_scale * add_ref[...]

    in_specs = [pl.BlockSpec((tm, k), lambda i, j: (i, 0)), pl.BlockSpec((k, tn), lambda i, j: (0, j))]
    args = [a, w]
    if add is not None:
        in_specs.append(pl.BlockSpec((tm, tn), lambda i, j: (i, j)))
        args.append(add)
    return pl.pallas_call(
        body, grid=(m // tm, n // tn), in_specs=in_specs,
        out_specs=pl.BlockSpec((tm, tn), lambda i, j: (i, j)),
        out_shape=jax.ShapeDtypeStruct((m, n), f32),
        compiler_params=_params(("parallel", "parallel")), name=name,
    )(*args)


def mm_tn(a, b, *, name):
    m, k = a.shape
    n = b.shape[1]
    tm = _tile(m, 512, SUBLANES)
    tn = _tile(n, 512, LANES)

    def body(a_ref, b_ref, o_ref):
        part = _bdot(a_ref[...], b_ref[...], TN)

        @pl.when(pl.program_id(1) == 0)
        def _():
            o_ref[...] = part

        @pl.when(pl.program_id(1) > 0)
        def _():
            o_ref[...] += part

    return pl.pallas_call(
        body, grid=(n // tn, m // tm),
        in_specs=[pl.BlockSpec((tm, k), lambda j, i: (i, 0)), pl.BlockSpec((tm, tn), lambda j, i: (i, j))],
        out_specs=pl.BlockSpec((k, tn), lambda j, i: (0, j)),
        out_shape=jax.ShapeDtypeStruct((k, n), f32),
        compiler_params=_params(("parallel", "arbitrary")), name=name,
    )(a, b)


def proj_ln(a_list, w_list, xres, g, b, *, name):
    t, d = xres.shape
    tm = _tile(t, 256, SUBLANES)
    na = len(a_list)

    def body(*refs):
        a_refs, w_refs = refs[:na], refs[na:2 * na]
        x_ref, g_ref, b_ref, y_ref, z_ref = refs[2 * na:]
        z = DN_ALPHA * x_ref[...]
        for a_ref, w_ref in zip(a_refs, w_refs):
            z = z + _bdot(a_ref[...], w_ref[...], NN)
        z_ref[...] = z
        y_ref[...] = _layer_norm(z, g_ref[...], b_ref[...])

    in_specs = [pl.BlockSpec((tm, a.shape[1]), lambda i: (i, 0)) for a in a_list]
    in_specs += [pl.BlockSpec(w.shape, lambda i: (0, 0)) for w in w_list]
    in_specs += [pl.BlockSpec((tm, d), lambda i: (i, 0)), pl.BlockSpec((1, d), lambda i: (0, 0)),
                 pl.BlockSpec((1, d), lambda i: (0, 0))]
    return pl.pallas_call(
        body, grid=(t // tm,), in_specs=in_specs,
        out_specs=[pl.BlockSpec((tm, d), lambda i: (i, 0))] * 2,
        out_shape=[jax.ShapeDtypeStruct((t, d), f32)] * 2,
        compiler_params=_params(("parallel",)), name=name,
    )(*a_list, *w_list, xres, g, b)


def ln_bwd(z, dy, g, *, name):
    t, d = z.shape
    tm = _tile(t, 512, SUBLANES)

    def body(z_ref, dy_ref, g_ref, dz_ref, dzb_ref, dg_ref, db_ref):
        zz = z_ref[...]
        dy_ = dy_ref[...]
        mu = jnp.mean(zz, -1, keepdims=True)
        dd = zz - mu
        var = jnp.mean(dd * dd, -1, keepdims=True)
        rstd = lax.rsqrt(var + LN_EPS)
        xhat = dd * rstd
        dxh = dy_ * g_ref[...]
        dz = rstd * (dxh - jnp.mean(dxh, -1, keepdims=True) - xhat * jnp.mean(dxh * xhat, -1, keepdims=True))
        dz_ref[...] = dz
        dzb_ref[...] = dz.astype(bf16)
        pg = jnp.sum(dy_ * xhat, 0, keepdims=True)
        pb = jnp.sum(dy_, 0, keepdims=True)

        @pl.when(pl.program_id(0) == 0)
        def _():
            dg_ref[...] = pg
            db_ref[...] = pb

        @pl.when(pl.program_id(0) > 0)
        def _():
            dg_ref[...] += pg
            db_ref[...] += pb

    row = pl.BlockSpec((tm, d), lambda i: (i, 0))
    vec = pl.BlockSpec((1, d), lambda i: (0, 0))
    return pl.pallas_call(
        body, grid=(t // tm,), in_specs=[row, row, vec], out_specs=[row, row, vec, vec],
        out_shape=[jax.ShapeDtypeStruct((t, d), f32), jax.ShapeDtypeStruct((t, d), bf16),
                   jax.ShapeDtypeStruct((1, d), f32), jax.ShapeDtypeStruct((1, d), f32)],
        compiler_params=_params(("arbitrary",)), name=name,
    )(z, dy, g)


def loss_head(y, target, *, name):
    t, d = y.shape
    tm = _tile(t, 512, SUBLANES)

    def body(y_ref, t_ref, dy_ref, sq_ref):
        e = y_ref[...] - t_ref[...]
        dy_ref[...] = e * (1.0 / d)
        part = jnp.sum(e * e, 0, keepdims=True)

        @pl.when(pl.program_id(0) == 0)
        def _():
            sq_ref[...] = part

        @pl.when(pl.program_id(0) > 0)
        def _():
            sq_ref[...] += part

    row = pl.BlockSpec((tm, d), lambda i: (i, 0))
    vec = pl.BlockSpec((1, d), lambda i: (0, 0))
    return pl.pallas_call(
        body, grid=(t // tm,), in_specs=[row, row], out_specs=[row, vec],
        out_shape=[jax.ShapeDtypeStruct((t, d), f32), jax.ShapeDtypeStruct((1, d), f32)],
        compiler_params=_params(("arbitrary",)), name=name,
    )(y, target)


def ffn_fwd(x, wg, wu, wd, layer, g, b, *, name):
    t, d = x.shape
    nf, _, _, tf = wg.shape
    tm = _tile(t, 512, SUBLANES)

    def body(x_ref, wg_ref, wu_ref, wd_ref, g_ref, b_ref, y_ref, z_ref, acc_ref):
        f = pl.program_id(1)
        xb = x_ref[...].astype(bf16)
        gate = _bdot(xb, wg_ref[...], NN)
        up = _bdot(xb, wu_ref[...], NN)
        part = _bdot(_silu(gate) * up, wd_ref[...], NN)

        @pl.when(f == 0)
        def _():
            acc_ref[...] = part

        @pl.when(f > 0)
        def _():
            acc_ref[...] += part

        @pl.when(f == nf - 1)
        def _():
            z = DN_ALPHA * x_ref[...] + 0.5 * acc_ref[...]
            z_ref[...] = z
            y_ref[...] = _layer_norm(z, g_ref[...], b_ref[...])

    row = pl.BlockSpec((tm, d), lambda i, j: (i, 0))
    vec = pl.BlockSpec((1, d), lambda i, j: (0, 0))
    wcol = pl.BlockSpec((None, None, d, tf), lambda i, j: (j, layer, 0, 0))
    wrow = pl.BlockSpec((None, None, tf, d), lambda i, j: (j, layer, 0, 0))
    return pl.pallas_call(
        body, grid=(t // tm, nf),
        in_specs=[row, wcol, wcol, wrow, vec, vec],
        out_specs=[row, row],
        out_shape=[jax.ShapeDtypeStruct((t, d), f32)] * 2,
        scratch_shapes=[pltpu.VMEM((tm, d), f32)],
        compiler_params=_params(("parallel", "arbitrary")), name=name,
    )(x, wg, wu, wd, g, b)


def ffn_bwd_weights(xb, dzb, wg, wu, wd, layer, acc, *, name):
    t, d = xb.shape
    nf, nl, _, tf = wg.shape
    tm = _tile(t, 512, SUBLANES)

    def body(x_ref, dz_ref, wg_ref, wu_ref, wd_ref, *rest):
        dgate_ref, dup_ref, dwg_ref, dwu_ref, dwd_ref = rest[-5:]
        x = x_ref[...]
        dzh = dz_ref[...] * 0.5
        gate = _bdot(x, wg_ref[...], NN)
        up = _bdot(x, wu_ref[...], NN)
        sg = jax.nn.sigmoid(gate)
        s = gate * sg
        dh = _bdot(dzh, wd_ref[...], NT)
        dup = (dh * s).astype(bf16)
        dgate = (dh * up * (sg * (1.0 + gate * (1.0 - sg)))).astype(bf16)
        dgate_ref[...] = dgate
        dup_ref[...] = dup
        pwg = _bdot(x, dgate, TN)
        pwu = _bdot(x, dup, TN)
        pwd = _bdot(s * up, dzh, TN)

        @pl.when(pl.program_id(1) == 0)
        def _():
            dwg_ref[...] = pwg
            dwu_ref[...] = pwu
            dwd_ref[...] = pwd

        @pl.when(pl.program_id(1) > 0)
        def _():
            dwg_ref[...] += pwg
            dwu_ref[...] += pwu
            dwd_ref[...] += pwd

    row = pl.BlockSpec((tm, d), lambda j, i: (i, 0))
    wcol = pl.BlockSpec((None, None, d, tf), lambda j, i: (j, layer, 0, 0))
    wrow = pl.BlockSpec((None, None, tf, d), lambda j, i: (j, layer, 0, 0))
    act = pl.BlockSpec((None, tm, tf), lambda j, i: (j, i, 0))
    in_specs = [row, row, wcol, wcol, wrow]
    args = [xb, dzb, wg, wu, wd]
    aliases = {}
    if acc is not None:
        in_specs += [pl.BlockSpec(memory_space=pl.ANY)] * 3
        args += list(acc)
        aliases = {5: 2, 6: 3, 7: 4}
    return pl.pallas_call(
        body, grid=(nf, t // tm), in_specs=in_specs, out_specs=[act, act, wcol, wcol, wrow],
        out_shape=[jax.ShapeDtypeStruct((nf, t, tf), bf16), jax.ShapeDtypeStruct((nf, t, tf), bf16),
                   jax.ShapeDtypeStruct((nf, nl, d, tf), f32), jax.ShapeDtypeStruct((nf, nl, d, tf), f32),
                   jax.ShapeDtypeStruct((nf, nl, tf, d), f32)],
        input_output_aliases=aliases,
        compiler_params=_params(("parallel", "arbitrary")), name=name,
    )(*args)


def ffn_bwd_input(dgate, dup, wg, wu, layer, dz, *, name):
    nf, t, tf = dgate.shape
    d = wg.shape[2]
    tm = _tile(t, 256, SUBLANES)

    def body(dg_ref, du_ref, wg_ref, wu_ref, dz_ref, dx_ref):
        acc = DN_ALPHA * dz_ref[...]
        for j in range(nf):
            acc = acc + _bdot(dg_ref[j], wg_ref[j], NT) + _bdot(du_ref[j], wu_ref[j], NT)
        dx_ref[...] = acc

    act = pl.BlockSpec((nf, tm, tf), lambda i: (0, i, 0))
    wsp = pl.BlockSpec((nf, None, d, tf), lambda i: (0, layer, 0, 0))
    row = pl.BlockSpec((tm, d), lambda i: (i, 0))
    return pl.pallas_call(
        body, grid=(t // tm,), in_specs=[act, act, wsp, wsp, row], out_specs=row,
        out_shape=jax.ShapeDtypeStruct((t, d), f32),
        compiler_params=_params(("parallel",)), name=name,
    )(dgate, dup, wg, wu, dz)


def ple_fwd(x, p, wg, bg, wp, *, name):
    t, d = x.shape
    dp = p.shape[1]
    tm = _tile(t, 512, SUBLANES)

    def body(x_ref, p_ref, wg_ref, bg_ref, wp_ref, o_ref):
        x_ = x_ref[...]
        gate = jax.nn.sigmoid(_bdot(x_, wg_ref[...], NN) + bg_ref[...])
        o_ref[...] = x_ + gate * _bdot(p_ref[...], wp_ref[...], NN)

    row = pl.BlockSpec((tm, d), lambda i: (i, 0))
    return pl.pallas_call(
        body, grid=(t // tm,),
        in_specs=[row, pl.BlockSpec((tm, dp), lambda i: (i, 0)), pl.BlockSpec((d, d), lambda i: (0, 0)),
                  pl.BlockSpec((1, d), lambda i: (0, 0)), pl.BlockSpec((dp, d), lambda i: (0, 0))],
        out_specs=row, out_shape=jax.ShapeDtypeStruct((t, d), f32),
        compiler_params=_params(("parallel",)), name=name,
    )(x, p, wg, bg, wp)


def ple_bwd(x, p, dy, wg, wgt, bg, wp, *, name):
    t, d = x.shape
    dp = p.shape[1]
    tm = _tile(t, 512, SUBLANES)

    def body(x_ref, p_ref, dy_ref, wg_ref, wgt_ref, bg_ref, wp_ref, dx_ref, dwg_ref, dbg_ref, dwp_ref):
        x_ = x_ref[...]
        dy_ = dy_ref[...]
        s = jax.nn.sigmoid(_bdot(x_, wg_ref[...], NN) + bg_ref[...])
        e = _bdot(p_ref[...], wp_ref[...], NN)
        da = dy_ * e * s * (1.0 - s)
        de = dy_ * s
        dx_ref[...] = dy_ + _bdot(da, wgt_ref[...], NN)
        pwg = _bdot(x_, da, TN)
        pbg = jnp.sum(da, 0, keepdims=True)
        pwp = _bdot(p_ref[...], de, TN)

        @pl.when(pl.program_id(0) == 0)
        def _():
            dwg_ref[...] = pwg
            dbg_ref[...] = pbg
            dwp_ref[...] = pwp

        @pl.when(pl.program_id(0) > 0)
        def _():
            dwg_ref[...] += pwg
            dbg_ref[...] += pbg
            dwp_ref[...] += pwp

    row = pl.BlockSpec((tm, d), lambda i: (i, 0))
    full = lambda shape: pl.BlockSpec(shape, lambda i: (0, 0))
    return pl.pallas_call(
        body, grid=(t // tm,),
        in_specs=[row, pl.BlockSpec((tm, dp), lambda i: (i, 0)), row, full((d, d)), full((d, d)), full((1, d)),
                  full((dp, d))],
        out_specs=[row, full((d, d)), full((1, d)), full((dp, d))],
        out_shape=[jax.ShapeDtypeStruct((t, d), f32), jax.ShapeDtypeStruct((d, d), f32),
                   jax.ShapeDtypeStruct((1, d), f32), jax.ShapeDtypeStruct((dp, d), f32)],
        compiler_params=_params(("arbitrary",)), name=name,
    )(x, p, dy, wg, wgt, bg, wp)


def _conv_taps(xpad_ref, w_ref, s):
    acc = w_ref[0:1, :] * xpad_ref[SUBLANES - 3:SUBLANES - 3 + s, :]
    for j in range(1, 4):
        acc = acc + w_ref[j:j + 1, :] * xpad_ref[SUBLANES - 3 + j:SUBLANES - 3 + j + s, :]
    return acc


def conv_fwd(x, w, bias, act, nb, *, name):
    t, c = x.shape
    s = t // nb
    cw = GROUP_W

    def body(x_ref, w_ref, b_ref, y_ref, xpad):
        xpad[0:SUBLANES, :] = jnp.zeros((SUBLANES, cw), f32)
        xpad[SUBLANES:, :] = x_ref[...]
        acc = _conv_taps(xpad, w_ref, s) + b_ref[...]
        y_ref[...] = _silu(acc) if act else acc

    slab = pl.BlockSpec((s, cw), lambda b, g: (b, g))
    return pl.pallas_call(
        body, grid=(nb, c // cw),
        in_specs=[slab, pl.BlockSpec((4, cw), lambda b, g: (0, g)), pl.BlockSpec((1, cw), lambda b, g: (0, g))],
        out_specs=slab, out_shape=jax.ShapeDtypeStruct((t, c), f32),
        scratch_shapes=[pltpu.VMEM((s + SUBLANES, cw), f32)],
        compiler_params=_params(("parallel", "parallel")), name=name,
    )(x, w, bias)


def conv_bwd(x, w, bias, dy, act, nb, *, name):
    t, c = x.shape
    s = t // nb
    cw = GROUP_W

    def body(x_ref, w_ref, b_ref, dy_ref, dx_ref, dw_ref, db_ref, xpad, dpad):
        xpad[0:SUBLANES, :] = jnp.zeros((SUBLANES, cw), f32)
        xpad[SUBLANES:, :] = x_ref[...]
        dacc = dy_ref[...]
        if act:
            acc = _conv_taps(xpad, w_ref, s) + b_ref[...]
            sg = jax.nn.sigmoid(acc)
            dacc = dacc * (sg * (1.0 + acc * (1.0 - sg)))
        dpad[0:s, :] = dacc
        dpad[s:, :] = jnp.zeros((SUBLANES, cw), f32)
        dx = w_ref[0:1, :] * dpad[3:3 + s, :]
        for j in range(1, 4):
            dx = dx + w_ref[j:j + 1, :] * dpad[3 - j:3 - j + s, :]
        dx_ref[...] = dx
        first = pl.program_id(1) == 0
        for j in range(4):
            pw = jnp.sum(dacc * xpad[SUBLANES - 3 + j:SUBLANES - 3 + j + s, :], 0, keepdims=True)

            @pl.when(first)
            def _():
                dw_ref[j:j + 1, :] = pw

            @pl.when(jnp.logical_not(first))
            def _():
                dw_ref[j:j + 1, :] += pw

        pb = jnp.sum(dacc, 0, keepdims=True)

        @pl.when(first)
        def _():
            db_ref[...] = pb

        @pl.when(jnp.logical_not(first))
        def _():
            db_ref[...] += pb

    slab = pl.BlockSpec((s, cw), lambda g, b: (b, g))
    wsp = pl.BlockSpec((4, cw), lambda g, b: (0, g))
    bsp = pl.BlockSpec((1, cw), lambda g, b: (0, g))
    return pl.pallas_call(
        body, grid=(c // cw, nb), in_specs=[slab, wsp, bsp, slab], out_specs=[slab, wsp, bsp],
        out_shape=[jax.ShapeDtypeStruct((t, c), f32), jax.ShapeDtypeStruct((4, c), f32),
                   jax.ShapeDtypeStruct((1, c), f32)],
        scratch_shapes=[pltpu.VMEM((s + SUBLANES, cw), f32), pltpu.VMEM((s + SUBLANES, cw), f32)],
        compiler_params=_params(("parallel", "arbitrary")), name=name,
    )(x, w, bias, dy)


def _attn_head(q, k, v, sink, slope, valid, dist, dots):
    nn, nt = dots[:2]
    sc = nt(q, k) * (A_HEAD_DIM ** -0.5)
    sc = sc - slope * dist
    sc = jnp.where(valid, sc, NEG)
    m = jnp.maximum(jnp.max(sc, -1, keepdims=True), sink)
    pr = jnp.exp(sc - m)
    den = jnp.sum(pr, -1, keepdims=True) + jnp.exp(sink - m)
    return nn(pr / den, v)


def _attn_band_consts(r0):
    band = A_WINDOW + CHUNK
    qi = lax.broadcasted_iota(jnp.int32, (CHUNK, band), 0)
    kj = lax.broadcasted_iota(jnp.int32, (CHUNK, band), 1)
    dist = jnp.abs(qi + A_WINDOW - kj).astype(f32)
    valid = (kj + r0) >= A_WINDOW
    return dist, valid


def attn_fwd(qkv, sinks, nb, *, name):
    t = qkv.shape[0]
    s = t // nb
    band = A_WINDOW + CHUNK
    hd = A_HEAD_DIM

    def body(qkv_ref, sink_ref, o_ref, kvpad):
        kvpad[0:A_WINDOW, :] = jnp.zeros((A_WINDOW, 2 * A_KV_WIDTH), f32)
        kvpad[A_WINDOW:, :] = qkv_ref[:, A_WIDTH:]

        def chunk(n, carry):
            r0 = pl.multiple_of(n * CHUNK, CHUNK)
            dist, valid = _attn_band_consts(r0)
            for kvh in range(A_KV_HEADS):
                kb = kvpad[pl.ds(r0, band), kvh * hd:(kvh + 1) * hd]
                vb = kvpad[pl.ds(r0, band), A_KV_WIDTH + kvh * hd:A_KV_WIDTH + (kvh + 1) * hd]
                for gi in range(A_GROUP):
                    h = kvh * A_GROUP + gi
                    q = qkv_ref[pl.ds(r0, CHUNK), h * hd:(h + 1) * hd]
                    o = _attn_head(q, kb, vb, sink_ref[:, h:h + 1], 2.0 ** -(h + 1), valid, dist, RAW_DOTS)
                    o_ref[pl.ds(r0, CHUNK), h * hd:(h + 1) * hd] = o
            return carry

        lax.fori_loop(0, s // CHUNK, chunk, 0)

    return pl.pallas_call(
        body, grid=(nb,),
        in_specs=[pl.BlockSpec((s, A_WIDTH + 2 * A_KV_WIDTH), lambda b: (b, 0)),
                  pl.BlockSpec((1, A_HEADS), lambda b: (0, 0))],
        out_specs=pl.BlockSpec((s, A_WIDTH), lambda b: (b, 0)),
        out_shape=jax.ShapeDtypeStruct((t, A_WIDTH), f32),
        scratch_shapes=[pltpu.VMEM((s + A_WINDOW, 2 * A_KV_WIDTH), f32)],
        compiler_params=_params(("parallel",)), name=name,
    )(qkv, sinks)


def attn_bwd(qkv, sinks, do, nb, *, name):
    t = qkv.shape[0]
    s = t // nb
    band = A_WINDOW + CHUNK
    hd = A_HEAD_DIM
    kvw = 2 * A_KV_WIDTH

    def body(qkv_ref, sink_ref, do_ref, dqkv_ref, dsink_ref, kvpad, dkvpad):
        kvpad[0:A_WINDOW, :] = jnp.zeros((A_WINDOW, kvw), f32)
        kvpad[A_WINDOW:, :] = qkv_ref[:, A_WIDTH:]
        dkvpad[...] = jnp.zeros((s + A_WINDOW, kvw), f32)

        def chunk(n, dsinks):
            r0 = pl.multiple_of(n * CHUNK, CHUNK)
            dist, valid = _attn_band_consts(r0)
            dsinks = list(dsinks)
            for kvh in range(A_KV_HEADS):
                ksl = slice(kvh * hd, (kvh + 1) * hd)
                vsl = slice(A_KV_WIDTH + kvh * hd, A_KV_WIDTH + (kvh + 1) * hd)
                kb = kvpad[pl.ds(r0, band), ksl]
                vb = kvpad[pl.ds(r0, band), vsl]
                dk_acc = jnp.zeros((band, hd), f32)
                dv_acc = jnp.zeros((band, hd), f32)
                for gi in range(A_GROUP):
                    h = kvh * A_GROUP + gi
                    q = qkv_ref[pl.ds(r0, CHUNK), h * hd:(h + 1) * hd]
                    fn = functools.partial(_attn_head, slope=2.0 ** -(h + 1), valid=valid, dist=dist, dots=VJP_DOTS)
                    _, vjp = jax.vjp(fn, q, kb, vb, sink_ref[:, h:h + 1])
                    dq, dk, dv, ds = vjp(do_ref[pl.ds(r0, CHUNK), h * hd:(h + 1) * hd])
                    dqkv_ref[pl.ds(r0, CHUNK), h * hd:(h + 1) * hd] = dq
                    dk_acc = dk_acc + dk
                    dv_acc = dv_acc + dv
                    dsinks[h] = dsinks[h] + ds
                dkvpad[pl.ds(r0, band), ksl] += dk_acc
                dkvpad[pl.ds(r0, band), vsl] += dv_acc
            return tuple(dsinks)

        dsinks = lax.fori_loop(0, s // CHUNK, chunk, tuple(jnp.zeros((1, 1), f32) for _ in range(A_HEADS)))
        dqkv_ref[:, A_WIDTH:] = dkvpad[A_WINDOW:, :]
        first = pl.program_id(0) == 0
        for h in range(A_HEADS):
            @pl.when(first)
            def _():
                dsink_ref[:, h:h + 1] = dsinks[h]

            @pl.when(jnp.logical_not(first))
            def _():
                dsink_ref[:, h:h + 1] += dsinks[h]

    wq = A_WIDTH + kvw
    return pl.pallas_call(
        body, grid=(nb,),
        in_specs=[pl.BlockSpec((s, wq), lambda b: (b, 0)), pl.BlockSpec((1, A_HEADS), lambda b: (0, 0)),
                  pl.BlockSpec((s, A_WIDTH), lambda b: (b, 0))],
        out_specs=[pl.BlockSpec((s, wq), lambda b: (b, 0)), pl.BlockSpec((1, A_HEADS), lambda b: (0, 0))],
        out_shape=[jax.ShapeDtypeStruct((t, wq), f32), jax.ShapeDtypeStruct((1, A_HEADS), f32)],
        scratch_shapes=[pltpu.VMEM((s + A_WINDOW, kvw), f32), pltpu.VMEM((s + A_WINDOW, kvw), f32)],
        compiler_params=_params(("arbitrary",)), name=name,
    )(qkv, sinks, do)


def _rg_gates(xc, wa, wx, ba, bx, lam, nn):
    r = jax.nn.sigmoid(nn(xc, wa) + ba)
    i = jax.nn.sigmoid(nn(xc, wx) + bx)
    log_a = -RG_C * r * jax.nn.softplus(-lam)
    a = jnp.exp(log_a)
    mult = jnp.sqrt(-jnp.tanh(log_a) * (jnp.exp(2.0 * log_a) + 1.0))
    return a, mult * (i * xc)


def _linear_scan(a, u, reverse):
    s = a.shape[0]
    t = lax.broadcasted_iota(jnp.int32, a.shape, 0)
    d = 1
    while d < s:
        if reverse:
            keep = t < s - d
            shift = s - d
        else:
            keep = t >= d
            shift = d
        us = jnp.where(keep, pltpu.roll(u, shift, 0), 0.0)
        as_ = jnp.where(keep, pltpu.roll(a, shift, 0), 1.0)
        u = u + a * us
        a = a * as_
        d *= 2
    return u


def rglru_fwd(xc, bg, wa, wx, ba, bx, lam, nb, *, name):
    t, c = xc.shape
    s = t // nb
    cw = GROUP_W

    def body(xc_ref, bg_ref, wa_ref, wx_ref, ba_ref, bx_ref, lam_ref, y_ref, h_ref):
        a, u = _rg_gates(xc_ref[...], wa_ref[...], wx_ref[...], ba_ref[...], bx_ref[...], lam_ref[...], RAW_DOTS[0])
        h = _linear_scan(a, u, False)
        h_ref[...] = h
        y_ref[...] = h * jax.nn.gelu(bg_ref[...])

    slab = pl.BlockSpec((s, cw), lambda b, g: (b, g))
    wsp = pl.BlockSpec((None, cw, cw), lambda b, g: (g, 0, 0))
    vec = pl.BlockSpec((1, cw), lambda b, g: (0, g))
    return pl.pallas_call(
        body, grid=(nb, c // cw), in_specs=[slab, slab, wsp, wsp, vec, vec, vec], out_specs=[slab, slab],
        out_shape=[jax.ShapeDtypeStruct((t, c), f32)] * 2,
        compiler_params=_params(("parallel", "parallel")), name=name,
    )(xc, bg, wa, wx, ba, bx, lam)


def rglru_bwd(xc, bg, h, dy, wa, wx, ba, bx, lam, nb, *, name):
    t, c = xc.shape
    s = t // nb
    cw = GROUP_W

    def body(xc_ref, bg_ref, h_ref, dy_ref, wa_ref, wx_ref, ba_ref, bx_ref, lam_ref,
             dxc_ref, dbg_ref, dwa_ref, dwx_ref, dba_ref, dbx_ref, dlam_ref):
        h = h_ref[...]
        dy_ = dy_ref[...]
        gel, gel_vjp = jax.vjp(jax.nn.gelu, bg_ref[...])
        dbg_ref[...] = gel_vjp(dy_ * h)[0]
        dh = dy_ * gel
        gates = functools.partial(_rg_gates, nn=_bnn)
        (a, _), gates_vjp = jax.vjp(gates, xc_ref[...], wa_ref[...], wx_ref[...], ba_ref[...], bx_ref[...],
                                    lam_ref[...])
        ti = lax.broadcasted_iota(jnp.int32, a.shape, 0)
        a_next = jnp.where(ti < s - 1, pltpu.roll(a, s - 1, 0), 0.0)
        lam_t = _linear_scan(a_next, dh, True)
        h_prev = jnp.where(ti >= 1, pltpu.roll(h, 1, 0), 0.0)
        dxc, dwa, dwx, dba, dbx, dlam = gates_vjp((lam_t * h_prev, lam_t))
        dxc_ref[...] = dxc
        first = pl.program_id(1) == 0

        @pl.when(first)
        def _():
            dwa_ref[...] = dwa
            dwx_ref[...] = dwx
            dba_ref[...] = dba
            dbx_ref[...] = dbx
            dlam_ref[...] = dlam

        @pl.when(jnp.logical_not(first))
        def _():
            dwa_ref[...] += dwa
            dwx_ref[...] += dwx
            dba_ref[...] += dba
            dbx_ref[...] += dbx
            dlam_ref[...] += dlam

    slab = pl.BlockSpec((s, cw), lambda g, b: (b, g))
    wsp = pl.BlockSpec((None, cw, cw), lambda g, b: (g, 0, 0))
    vec = pl.BlockSpec((1, cw), lambda g, b: (0, g))
    ng = c // cw
    return pl.pallas_call(
        body, grid=(ng, nb), in_specs=[slab, slab, slab, slab, wsp, wsp, vec, vec, vec],
        out_specs=[slab, slab, wsp, wsp, vec, vec, vec],
        out_shape=[jax.ShapeDtypeStruct((t, c), f32), jax.ShapeDtypeStruct((t, c), f32),
                   jax.ShapeDtypeStruct((ng, cw, cw), f32), jax.ShapeDtypeStruct((ng, cw, cw), f32),
                   jax.ShapeDtypeStruct((1, c), f32), jax.ShapeDtypeStruct((1, c), f32),
                   jax.ShapeDtypeStruct((1, c), f32)],
        compiler_params=_params(("parallel", "arbitrary")), name=name,
    )(xc, bg, h, dy, wa, wx, ba, bx, lam)


def _gdn_chunk_prep(q, k, v, bl, al, a_log, dt_b, dots):
    nt, csum = dots[1], dots[3]
    hd = C_HEAD_DIM
    qn = q * lax.rsqrt(jnp.sum(q * q, -1, keepdims=True) + NORM_EPS) * (hd ** -0.5)
    kn = k * lax.rsqrt(jnp.sum(k * k, -1, keepdims=True) + NORM_EPS)
    beta = jax.nn.sigmoid(bl)
    g = -jnp.exp(a_log) * jax.nn.softplus(al + dt_b)
    ri = lax.broadcasted_iota(jnp.int32, (CHUNK, CHUNK), 0)
    ci = lax.broadcasted_iota(jnp.int32, (CHUNK, CHUNK), 1)
    tril = ri >= ci
    strict = ri > ci
    eye = (ri == ci).astype(f32)
    gc_sq = csum(jnp.broadcast_to(g, (CHUNK, CHUNK)))
    gc = csum(jnp.broadcast_to(g, (CHUNK, hd)))
    gc_row = gc_sq.T
    decay = jnp.where(tril, jnp.exp(jnp.where(tril, gc_sq - gc_row, 0.0)), 0.0)
    kb = kn * beta
    lmat = jnp.where(strict, nt(kb, kn) * decay, 0.0)
    pw = -lmat
    inv = eye + pw
    for _ in range(5):
        pw = _h3dot(pw, pw)
        inv = inv + _h3dot(inv, pw)
    egc = jnp.exp(gc)
    u = _h3dot(inv, v * beta)
    w = _h3dot(inv, kb * egc)
    attn = nt(qn, kn) * decay
    g_last = jnp.sum(jnp.broadcast_to(g, (CHUNK, hd)), 0, keepdims=True)
    return qn * egc, kn * jnp.exp(g_last - gc), w, u, attn, jnp.exp(g_last)


def _gdn_chunk_step(state, qg, kdec, w, u, attn, gl, z, ng, dots):
    nn, tn = dots[0], dots[2]
    v_new = u - nn(w, state)
    o = nn(qg, state) + nn(attn, v_new)
    state = state * gl + tn(kdec, v_new)
    o = o * lax.rsqrt(jnp.mean(o * o, -1, keepdims=True) + NORM_EPS) * ng
    return o * _silu(z), state


def _loop_unrolled(n, step, init):
    u = CHUNK_UNROLL if n % CHUNK_UNROLL == 0 else 1

    def trip(i, carry):
        for j in range(u):
            carry = step(i * u + j, carry)
        return carry

    return lax.fori_loop(0, n // u, trip, init)


def _pick_lane(x, lane):
    li = lax.broadcasted_iota(jnp.int32, x.shape, 1)
    return jnp.sum(jnp.where(li == lane, x, 0.0), 1, keepdims=True)


def _put_lane(col, lane, width):
    li = lax.broadcasted_iota(jnp.int32, (col.shape[0], width), 1)
    return jnp.where(li == lane, col, 0.0)


def _gdn_specs(s, nc):
    hd = C_HEAD_DIM
    head = lambda off: pl.BlockSpec((s, hd), lambda b, h, off=off: (b, off + h))
    attn = pl.BlockSpec((None, s, CHUNK), lambda b, h: (h, b, 0))
    gl = pl.BlockSpec((None, nc * SUBLANES, hd), lambda b, h: (h, b, 0))
    ba = pl.BlockSpec((s, LANES), lambda b, h: (b, 0))
    sc8 = pl.BlockSpec((1, C_HEADS), lambda b, h: (0, 0))
    return head, attn, gl, ba, sc8


def gdn_prep_fwd(qkv, ba, a_log, dt_b, nb, *, name):
    t = qkv.shape[0]
    s = t // nb
    nc = s // CHUNK
    hd = C_HEAD_DIM
    head, attn_sp, gl_sp, ba_sp, sc8 = _gdn_specs(s, nc)

    def body(q_ref, k_ref, v_ref, ba_ref, alog_ref, dtb_ref, qg_ref, kd_ref, w_ref, u_ref, at_ref, gl_ref):
        h = pl.program_id(1)
        a_log_h = _pick_lane(alog_ref[...], h)
        dt_b_h = _pick_lane(dtb_ref[...], h)

        def chunk(n, carry):
            rows = pl.ds(pl.multiple_of(n * CHUNK, CHUNK), CHUNK)
            bav = ba_ref[rows, :]
            outs = _gdn_chunk_prep(q_ref[rows, :], k_ref[rows, :], v_ref[rows, :], _pick_lane(bav, h),
                                   _pick_lane(bav, C_HEADS + h), a_log_h, dt_b_h, RAW_DOTS)
            qg_ref[rows, :], kd_ref[rows, :], w_ref[rows, :], u_ref[rows, :], at_ref[rows, :] = outs[:5]
            gl_ref[pl.ds(pl.multiple_of(n * SUBLANES, SUBLANES), SUBLANES), :] = jnp.broadcast_to(outs[5], (SUBLANES, hd))
            return carry

        _loop_unrolled(nc, chunk, 0)

    big = jax.ShapeDtypeStruct((t, C_WIDTH), f32)
    return pl.pallas_call(
        body, grid=(nb, C_HEADS),
        in_specs=[head(0), head(C_HEADS), head(2 * C_HEADS), ba_sp, sc8, sc8],
        out_specs=[head(0)] * 4 + [attn_sp, gl_sp],
        out_shape=[big] * 4 + [jax.ShapeDtypeStruct((C_HEADS, t, CHUNK), f32),
                               jax.ShapeDtypeStruct((C_HEADS, nb * nc * SUBLANES, hd), f32)],
        compiler_params=_params(("parallel", "parallel")), name=name,
    )(qkv, qkv, qkv, ba, a_log, dt_b)


def gdn_prep_bwd(qkv, ba, a_log, dt_b, cts, nb, *, name):
    t = qkv.shape[0]
    s = t // nb
    nc = s // CHUNK
    hd = C_HEAD_DIM
    head, attn_sp, gl_sp, ba_sp, sc8 = _gdn_specs(s, nc)

    def body(q_ref, k_ref, v_ref, ba_ref, alog_ref, dtb_ref, cqg, ckd, cw_, cu, cat, cgl,
             dq_ref, dk_ref, dv_ref, dba_ref, dalog_ref, ddtb_ref):
        b = pl.program_id(0)
        h = pl.program_id(1)
        a_log_h = _pick_lane(alog_ref[...], h)
        dt_b_h = _pick_lane(dtb_ref[...], h)
        prep = functools.partial(_gdn_chunk_prep, dots=VJP_DOTS)

        @pl.when(h == 0)
        def _():
            dba_ref[...] = jnp.zeros((s, LANES), f32)

        def chunk(n, carry):
            da_log, ddt_b = carry
            rows = pl.ds(pl.multiple_of(n * CHUNK, CHUNK), CHUNK)
            bav = ba_ref[rows, :]
            _, vjp = jax.vjp(prep, q_ref[rows, :], k_ref[rows, :], v_ref[rows, :], _pick_lane(bav, h),
                             _pick_lane(bav, C_HEADS + h), a_log_h, dt_b_h)
            cgl_n = cgl[pl.ds(pl.multiple_of(n * SUBLANES, SUBLANES), SUBLANES), :][0:1, :]
            dq, dk, dv, dbl, dal, dalog_n, ddtb_n = vjp((cqg[rows, :], ckd[rows, :], cw_[rows, :], cu[rows, :],
                                                         cat[rows, :], cgl_n))
            dq_ref[rows, :] = dq
            dk_ref[rows, :] = dk
            dv_ref[rows, :] = dv
            dba_ref[rows, :] += _put_lane(dbl, h, LANES) + _put_lane(dal, C_HEADS + h, LANES)
            return da_log + dalog_n, ddt_b + ddtb_n

        da_log, ddt_b = _loop_unrolled(nc, chunk, (jnp.zeros((1, 1), f32), jnp.zeros((1, 1), f32)))
        first = jnp.logical_and(b == 0, h == 0)

        @pl.when(first)
        def _():
            dalog_ref[...] = _put_lane(da_log, h, LANES)
            ddtb_ref[...] = _put_lane(ddt_b, h, LANES)

        @pl.when(jnp.logical_not(first))
        def _():
            dalog_ref[...] += _put_lane(da_log, h, LANES)
            ddtb_ref[...] += _put_lane(ddt_b, h, LANES)

    big = jax.ShapeDtypeStruct((t, C_WIDTH), f32)
    vec = pl.BlockSpec((1, LANES), lambda b, h: (0, 0))
    return pl.pallas_call(
        body, grid=(nb, C_HEADS),
        in_specs=[head(0), head(C_HEADS), head(2 * C_HEADS), ba_sp, sc8, sc8] + [head(0)] * 4 + [attn_sp, gl_sp],
        out_specs=[head(0)] * 3 + [ba_sp, vec, vec],
        out_shape=[big] * 3 + [jax.ShapeDtypeStruct((t, LANES), f32), jax.ShapeDtypeStruct((1, LANES), f32),
                               jax.ShapeDtypeStruct((1, LANES), f32)],
        compiler_params=_params(("arbitrary", "arbitrary")), name=name,
    )(qkv, qkv, qkv, ba, a_log, dt_b, *cts)


def _gdn_rec_specs(s, nc, hp):
    hd = C_HEAD_DIM
    wide = pl.BlockSpec((s, hp * hd), lambda b, j: (b, j))
    attn = pl.BlockSpec((hp, s, CHUNK), lambda b, j: (j, b, 0))
    gl = pl.BlockSpec((hp, nc * SUBLANES, hd), lambda b, j: (j, b, 0))
    ng = pl.BlockSpec((1, hd), lambda b, j: (0, 0))
    return wide, attn, gl, ng


def gdn_rec_fwd(qg, kdec, w, u, attn, gl, z, ng, nb, *, name):
    t = qg.shape[0]
    s = t // nb
    nc = s // CHUNK
    hd = C_HEAD_DIM
    hp = C_HEADS_PER_STEP_FWD
    wide, attn_sp, gl_sp, ng_sp = _gdn_rec_specs(s, nc, hp)

    def body(qg_ref, kd_ref, w_ref, u_ref, at_ref, gl_ref, z_ref, ng_ref, y_ref):
        def chunk(n, states):
            rows = pl.ds(pl.multiple_of(n * CHUNK, CHUNK), CHUNK)
            grow = pl.ds(pl.multiple_of(n * SUBLANES, SUBLANES), SUBLANES)
            new = []
            for j in range(hp):
                cols = slice(j * hd, (j + 1) * hd)
                y, st = _gdn_chunk_step(states[j], qg_ref[rows, cols], kd_ref[rows, cols], w_ref[rows, cols],
                                        u_ref[rows, cols], at_ref[j, rows, :], gl_ref[j, grow, :][0:1, :],
                                        z_ref[rows, cols], ng_ref[...], RAW_DOTS)
                y_ref[rows, cols] = y
                new.append(st)
            return tuple(new)

        lax.fori_loop(0, nc, chunk, tuple(jnp.zeros((hd, hd), f32) for _ in range(hp)))

    return pl.pallas_call(
        body, grid=(nb, C_HEADS // hp),
        in_specs=[wide] * 4 + [attn_sp, gl_sp, wide, ng_sp], out_specs=wide,
        out_shape=jax.ShapeDtypeStruct((t, C_WIDTH), f32),
        compiler_params=_params(("parallel", "parallel")), name=name,
    )(qg, kdec, w, u, attn, gl, z, ng)


def gdn_rec_bwd(qg, kdec, w, u, attn, gl, z, ng, dy, nb, *, name):
    t = qg.shape[0]
    s = t // nb
    nc = s // CHUNK
    hd = C_HEAD_DIM
    hp = C_HEADS_PER_STEP_BWD
    wide, attn_sp, gl_sp, ng_sp = _gdn_rec_specs(s, nc, hp)

    def body(qg_ref, kd_ref, w_ref, u_ref, at_ref, gl_ref, z_ref, ng_ref, dy_ref,
             dqg_ref, dkd_ref, dw_ref, du_ref, dat_ref, dgl_ref, dz_ref, dng_ref, states):
        step = functools.partial(_gdn_chunk_step, dots=VJP_DOTS)

        def operands(n, j):
            rows = pl.ds(pl.multiple_of(n * CHUNK, CHUNK), CHUNK)
            grow = pl.ds(pl.multiple_of(n * SUBLANES, SUBLANES), SUBLANES)
            cols = slice(j * hd, (j + 1) * hd)
            return (qg_ref[rows, cols], kd_ref[rows, cols], w_ref[rows, cols], u_ref[rows, cols],
                    at_ref[j, rows, :], gl_ref[j, grow, :][0:1, :], z_ref[rows, cols], ng_ref[...])

        def fwd_chunk(n, sts):
            new = []
            for j in range(hp):
                states[j, n] = sts[j]
                _, st = _gdn_chunk_step(sts[j], *operands(n, j), RAW_DOTS)
                new.append(st)
            return tuple(new)

        lax.fori_loop(0, nc, fwd_chunk, tuple(jnp.zeros((hd, hd), f32) for _ in range(hp)))

        def bwd_chunk(i, carry):
            n = nc - 1 - i
            rows = pl.ds(pl.multiple_of(n * CHUNK, CHUNK), CHUNK)
            grow = pl.ds(pl.multiple_of(n * SUBLANES, SUBLANES), SUBLANES)
            dsts, dng = carry
            new = []
            for j in range(hp):
                cols = slice(j * hd, (j + 1) * hd)
                _, vjp = jax.vjp(step, states[j, n], *operands(n, j))
                dst, dqg, dkd, dw, du, dat, dgl, dz, dng_n = vjp((dy_ref[rows, cols], dsts[j]))
                dqg_ref[rows, cols] = dqg
                dkd_ref[rows, cols] = dkd
                dw_ref[rows, cols] = dw
                du_ref[rows, cols] = du
                dat_ref[j, rows, :] = dat
                dgl_ref[j, grow, :] = jnp.broadcast_to(dgl, (SUBLANES, hd))
                dz_ref[rows, cols] = dz
                dng = dng + dng_n
                new.append(dst)
            return tuple(new), dng

        _, dng = lax.fori_loop(0, nc, bwd_chunk,
                               (tuple(jnp.zeros((hd, hd), f32) for _ in range(hp)), jnp.zeros((1, hd), f32)))
        first = jnp.logical_and(pl.program_id(0) == 0, pl.program_id(1) == 0)

        @pl.when(first)
        def _():
            dng_ref[...] = dng

        @pl.when(jnp.logical_not(first))
        def _():
            dng_ref[...] += dng

    big = jax.ShapeDtypeStruct((t, C_WIDTH), f32)
    return pl.pallas_call(
        body, grid=(nb, C_HEADS // hp),
        in_specs=[wide] * 4 + [attn_sp, gl_sp, wide, ng_sp, wide],
        out_specs=[wide] * 4 + [attn_sp, gl_sp, wide, ng_sp],
        out_shape=[big] * 4 + [jax.ShapeDtypeStruct(attn.shape, f32), jax.ShapeDtypeStruct(gl.shape, f32), big,
                               jax.ShapeDtypeStruct((1, hd), f32)],
        scratch_shapes=[pltpu.VMEM((hp, nc, hd, hd), f32)],
        compiler_params=_params(("arbitrary", "arbitrary")), name=name,
    )(qg, kdec, w, u, attn, gl, z, ng, dy)


def _blockdiag_slabs(w):
    per = GROUP_W // B_BLOCK
    slabs = jnp.zeros((B_BLOCKS // per, GROUP_W, GROUP_W), w.dtype)
    for h in range(B_BLOCKS):
        o = (h % per) * B_BLOCK
        slabs = slabs.at[h // per, o:o + B_BLOCK, o:o + B_BLOCK].set(w[h])
    return slabs


def _slab_blocks(slabs):
    per = GROUP_W // B_BLOCK
    return jnp.stack([slabs[h // per, (h % per) * B_BLOCK:(h % per + 1) * B_BLOCK,
                            (h % per) * B_BLOCK:(h % per + 1) * B_BLOCK] for h in range(B_BLOCKS)])


def _mixer_ab_fwd(x1, W, g, b, nb, tag):
    w_in = W["ab_w_in"][0].astype(bf16)
    o1, o2 = A_WIDTH + 2 * A_KV_WIDTH, A_WIDTH + 2 * A_KV_WIDTH + B_WIDTH
    w_qkv, w_bx, w_bg = w_in[:, :o1], w_in[:, o1:o2], w_in[:, o2:]
    pqkv = mm_nn(x1, w_qkv, name=tag + "_in_qkv")
    pbx = mm_nn(x1, w_bx, name=tag + "_in_bx")
    pbg = mm_nn(x1, w_bg, name=tag + "_in_bg")
    ya = attn_fwd(pqkv, W["a_sinks"], nb, name=tag + "_attn_fwd")
    xc = conv_fwd(pbx, W["b_conv_w"][0], W["b_conv_b"], False, nb, name=tag + "_conv_fwd")
    wa_s, wx_s = _blockdiag_slabs(W["b_wa"][0]), _blockdiag_slabs(W["b_wx"][0])
    yb, hh = rglru_fwd(xc, pbg, wa_s, wx_s, W["b_ba"], W["b_bx"], W["b_lam"], nb, name=tag + "_rglru_fwd")
    w_out = W["ab_w_out"][0].astype(bf16)
    x2, z1 = proj_ln([ya, yb], [w_out[:A_WIDTH], w_out[A_WIDTH:]], x1, g, b, name=tag + "_out_ln")
    saved = (pqkv, pbx, pbg, ya, xc, yb, hh, wa_s, wx_s, w_qkv, w_bx, w_bg, w_out)
    return x2, z1, saved


def _mixer_ab_bwd(x1, dz1, dz1b, W, saved, nb, tag):
    pqkv, pbx, pbg, ya, xc, yb, hh, wa_s, wx_s, w_qkv, w_bx, w_bg, w_out = saved
    dya = mm_nn(dz1b, w_out[:A_WIDTH].T, name=tag + "_dya")
    dyb = mm_nn(dz1b, w_out[A_WIDTH:].T, name=tag + "_dyb")
    dwo = jnp.concatenate([mm_tn(ya, dz1b, name=tag + "_dwo_a"), mm_tn(yb, dz1b, name=tag + "_dwo_b")], 0)
    dpqkv, dsinks = attn_bwd(pqkv, W["a_sinks"], dya, nb, name=tag + "_attn_bwd")
    dxc, dpbg, dwa_s, dwx_s, dba, dbx, dlam = rglru_bwd(xc, pbg, hh, dyb, wa_s, wx_s, W["b_ba"], W["b_bx"],
                                                       W["b_lam"], nb, name=tag + "_rglru_bwd")
    dpbx, dconv_w, dconv_b = conv_bwd(pbx, W["b_conv_w"][0], W["b_conv_b"], dxc, False, nb, name=tag + "_conv_bwd")
    dw_in = jnp.concatenate([mm_tn(x1, dpqkv, name=tag + "_dwin_qkv"), mm_tn(x1, dpbx, name=tag + "_dwin_bx"),
                             mm_tn(x1, dpbg, name=tag + "_dwin_bg")], 1)
    dx1 = mm_nn(dpqkv, w_qkv.T, add=dz1, add_scale=DN_ALPHA, name=tag + "_dx_qkv")
    dx1 = mm_nn(dpbx, w_bx.T, add=dx1, name=tag + "_dx_bx")
    dx1 = mm_nn(dpbg, w_bg.T, add=dx1, name=tag + "_dx_bg")
    grads = {"ab_w_in": dw_in[None], "a_sinks": dsinks, "b_conv_w": dconv_w[None], "b_conv_b": dconv_b,
             "b_wa": _slab_blocks(dwa_s)[None], "b_ba": dba, "b_wx": _slab_blocks(dwx_s)[None], "b_bx": dbx,
             "b_lam": dlam, "ab_w_out": dwo[None]}
    return dx1, grads


def _mixer_c_fwd(x1, W, g, b, nb, tag):
    w_in = W["c_w_in"][0].astype(bf16)
    d = w_in.shape[0]
    o1, o2 = 3 * C_WIDTH, 4 * C_WIDTH
    w_qkv, w_z = w_in[:, :o1], w_in[:, o1:o2]
    w_ba = jnp.concatenate([w_in[:, o2:], jnp.zeros((d, LANES - 2 * C_HEADS), bf16)], 1)
    pqkv = mm_nn(x1, w_qkv, name=tag + "_in_qkv")
    pz = mm_nn(x1, w_z, name=tag + "_in_z")
    pba = mm_nn(x1, w_ba, name=tag + "_in_ba")
    zero_b = jnp.zeros((1, o1), f32)
    qkvc = conv_fwd(pqkv, W["c_conv_w"][0], zero_b, True, nb, name=tag + "_conv_fwd")
    prep = gdn_prep_fwd(qkvc, pba, W["c_a_log"], W["c_dt_bias"], nb, name=tag + "_prep_fwd")
    yc = gdn_rec_fwd(*prep, pz, W["c_norm_g"], nb, name=tag + "_rec_fwd")
    w_out = W["c_w_out"][0].astype(bf16)
    x2, z1 = proj_ln([yc], [w_out], x1, g, b, name=tag + "_out_ln")
    saved = (pqkv, pz, pba, qkvc, prep, yc, w_qkv, w_z, w_ba, w_out, zero_b)
    return x2, z1, saved


def _mixer_c_bwd(x1, dz1, dz1b, W, saved, nb, tag):
    pqkv, pz, pba, qkvc, prep, yc, w_qkv, w_z, w_ba, w_out, zero_b = saved
    dyc = mm_nn(dz1b, w_out.T, name=tag + "_dyc")
    dwo = mm_tn(yc, dz1b, name=tag + "_dwo")
    rec = gdn_rec_bwd(*prep, pz, W["c_norm_g"], dyc, nb, name=tag + "_rec_bwd")
    cts, dpz, dng = rec[:6], rec[6], rec[7]
    dq, dk, dv, dpba, dalog, ddtb = gdn_prep_bwd(qkvc, pba, W["c_a_log"], W["c_dt_bias"], cts, nb,
                                                 name=tag + "_prep_bwd")
    dqkvc = jnp.concatenate([dq, dk, dv], 1)
    dpqkv, dconv_w, _ = conv_bwd(pqkv, W["c_conv_w"][0], zero_b, dqkvc, True, nb, name=tag + "_conv_bwd")
    dw_in = jnp.concatenate([mm_tn(x1, dpqkv, name=tag + "_dwin_qkv"), mm_tn(x1, dpz, name=tag + "_dwin_z"),
                             mm_tn(x1, dpba, name=tag + "_dwin_ba")[:, :2 * C_HEADS]], 1)
    dx1 = mm_nn(dpqkv, w_qkv.T, add=dz1, add_scale=DN_ALPHA, name=tag + "_dx_qkv")
    dx1 = mm_nn(dpz, w_z.T, add=dx1, name=tag + "_dx_z")
    dx1 = mm_nn(dpba, w_ba.T, add=dx1, name=tag + "_dx_ba")
    grads = {"c_w_in": dw_in[None], "c_conv_w": dconv_w[None], "c_a_log": dalog[:, :C_HEADS],
             "c_dt_bias": ddtb[:, :C_HEADS], "c_norm_g": dng, "c_w_out": dwo[None]}
    return dx1, grads


def _local_step(x, p, target, W, F):
    nb, s, d = x.shape
    t = nb * s
    h = x.reshape(t, d)
    tape = []
    f1 = [F[k] for k in ("ffn1_wg", "ffn1_wu", "ffn1_wd")]
    f2 = [F[k] for k in ("ffn2_wg", "ffn2_wu", "ffn2_wd")]
    for i in range(DEPTH):
        tag = f"l{i}"
        lg = [W["ln_g"][i, k][None] for k in range(3)]
        lb = [W["ln_b"][i, k][None] for k in range(3)]
        x1, z0 = ffn_fwd(h, *f1, i, lg[0], lb[0], name=tag + "_ffn1_fwd")
        mixer = _mixer_ab_fwd if i % 2 == 0 else _mixer_c_fwd
        x2, z1, msaved = mixer(x1, W, lg[1], lb[1], nb, tag + "_mix")
        x3, z2 = ffn_fwd(x2, *f2, i, lg[2], lb[2], name=tag + "_ffn2_fwd")
        pi = p[i].reshape(t, -1)
        pw = (W["ple_wg"][i].astype(bf16), W["ple_bg"][i][None], W["ple_wp"][i].astype(bf16))
        x4 = ple_fwd(x3, pi, *pw, name=tag + "_ple_fwd")
        tape.append((h, z0, x1, msaved, z1, x2, z2, x3, pi, pw, lg))
        h = x4
    dh, sq = loss_head(h, target.reshape(t, d), name="loss_head")
    loss = 0.5 * jnp.sum(sq) / d
    per_layer = [None] * DEPTH
    grads = {}
    df1 = df2 = None
    for i in reversed(range(DEPTH)):
        tag = f"l{i}"
        h_in, z0, x1, msaved, z1, x2, z2, x3, pi, pw, lg = tape[i]
        dx3, dple_wg, dple_bg, dple_wp = ple_bwd(x3, pi, dh, pw[0], pw[0].T, pw[1], pw[2], name=tag + "_ple_bwd")
        dz2, dz2b, dg2, db2 = ln_bwd(z2, dx3, lg[2], name=tag + "_ln2_bwd")
        dgate, dup, *df2 = ffn_bwd_weights(x2.astype(bf16), dz2b, *f2, i, df2, name=tag + "_ffn2_bwd_w")
        dx2 = ffn_bwd_input(dgate, dup, f2[0], f2[1], i, dz2, name=tag + "_ffn2_bwd_x")
        dz1, dz1b, dg1, db1 = ln_bwd(z1, dx2, lg[1], name=tag + "_ln1_bwd")
        mixer_bwd = _mixer_ab_bwd if i % 2 == 0 else _mixer_c_bwd
        dx1, mgrads = mixer_bwd(x1, dz1, dz1b, W, msaved, nb, tag + "_mix")
        grads.update(mgrads)
        dz0, dz0b, dg0, db0 = ln_bwd(z0, dx1, lg[0], name=tag + "_ln0_bwd")
        dgate, dup, *df1 = ffn_bwd_weights(h_in.astype(bf16), dz0b, *f1, i, df1, name=tag + "_ffn1_bwd_w")
        dh = ffn_bwd_input(dgate, dup, f1[0], f1[1], i, dz0, name=tag + "_ffn1_bwd_x")
        per_layer[i] = {"ln_g": jnp.concatenate([dg0, dg1, dg2], 0), "ln_b": jnp.concatenate([db0, db1, db2], 0),
                        "ple_wg": dple_wg, "ple_bg": dple_bg[0], "ple_wp": dple_wp}
    for k in per_layer[0]:
        grads[k] = jnp.stack([per_layer[i][k] for i in range(DEPTH)])
    grads.update(zip(("ffn1_wg", "ffn1_wu", "ffn1_wd"), df1))
    grads.update(zip(("ffn2_wg", "ffn2_wu", "ffn2_wd"), df2))
    return loss, dh.reshape(nb, s, d), grads


WEIGHT_NAMES = ("ffn1_wg", "ffn1_wu", "ffn1_wd", "ffn2_wg", "ffn2_wu", "ffn2_wd", "ln_g", "ln_b", "ple_wg", "ple_bg",
                "ple_wp", "ab_w_in", "a_sinks", "b_conv_w", "b_conv_b", "b_wa", "b_ba", "b_wx", "b_bx", "b_lam",
                "ab_w_out", "c_w_in", "c_conv_w", "c_a_log", "c_dt_bias", "c_norm_g", "c_w_out")
NATIVE_NAMES = WEIGHT_NAMES[:6]
PACKED_NAMES = WEIGHT_NAMES[6:]
SHARD_AXIS = {"ffn1_wg": 2, "ffn1_wu": 2, "ffn1_wd": 1, "ffn2_wg": 2, "ffn2_wu": 2, "ffn2_wd": 1, "ln_g": 2, "ln_b": 2,
              "ple_wg": 1, "ple_wp": 2, "ab_w_in": 2, "b_conv_w": 2, "ab_w_out": 1, "c_w_in": 2, "c_conv_w": 2,
              "c_w_out": 1}
N_CHIPS = 4
PACK_COLS = 512
MESH = pl.DeviceIdType.MESH
ANY = pl.BlockSpec(memory_space=pl.ANY)


def _pack_rows(n):
    unit = 2 * SUBLANES * PACK_COLS
    return -(-n // unit) * 2 * SUBLANES


def _pack(pieces, lead=()):
    k = len(lead)
    flat = jnp.concatenate([a.reshape(lead + (-1,)) for a in pieces], axis=k)
    rows = _pack_rows(flat.shape[k])
    flat = jnp.pad(flat, [(0, 0)] * k + [(0, rows * PACK_COLS - flat.shape[k])])
    return flat.reshape(lead + (rows, PACK_COLS))


def _unpack(pack, shapes, lead=()):
    k = len(lead)
    flat = pack.reshape(lead + (-1,))
    out, o = [], 0
    for shp in shapes:
        n = 1
        for dim in shp:
            n *= dim
        out.append(lax.slice_in_dim(flat, o, o + n, axis=k).reshape(lead + tuple(shp)))
        o += n
    return out


def _mesh_position():
    x, y, c = lax.axis_index("x"), lax.axis_index("y"), lax.axis_index("c")
    chips = [(1 - x, y), (x, 1 - y), (1 - x, 1 - y)]
    return x, y, c, chips


def _remote(src, dst, send_sems, recv_sems, k, to):
    return pltpu.make_async_remote_copy(src_ref=src, dst_ref=dst, send_sem=send_sems.at[k], recv_sem=recv_sems.at[k],
                                        device_id=to, device_id_type=MESH)


def _sems(n):
    return pltpu.SemaphoreType.DMA((n,))


def gather_shards(parts, *, name):
    n = len(parts)

    def body(*refs):
        w_refs, out_refs = refs[:n], refs[n:2 * n]
        send_sems, recv_sems, local_sems = refs[2 * n:]
        x, y, c, chips = _mesh_position()
        me = 2 * x + y
        sibling = (x, y, 1 - c)
        waits = []
        for a in range(n):
            mine = pltpu.make_async_copy(w_refs[a], out_refs[a].at[me], local_sems.at[a])
            mine.start()
            waits.append(mine.wait)
        for j, (cx, cy) in enumerate(chips):
            for a in range(n):
                cp = _remote(w_refs[a].at[c], out_refs[a].at[me, c], send_sems, recv_sems, 6 * a + j, (cx, cy, c))
                cp.start()
                waits.append(cp.wait_send)
        for j, (cx, cy) in enumerate(chips):
            for a in range(n):
                got = out_refs[a].at[2 * cx + cy, c]
                _remote(got, got, send_sems, recv_sems, 6 * a + j, (cx, cy, c)).wait_recv()
                fw = _remote(got, got, send_sems, recv_sems, 6 * a + 3 + j, sibling)
                fw.start()
                waits.append(fw.wait_send)
        for j, (cx, cy) in enumerate(chips):
            for a in range(n):
                got = out_refs[a].at[2 * cx + cy, 1 - c]
                _remote(got, got, send_sems, recv_sems, 6 * a + 3 + j, sibling).wait_recv()
        for wait in waits:
            wait()

    return pl.pallas_call(
        body, out_shape=[jax.ShapeDtypeStruct((N_CHIPS,) + a.shape, a.dtype) for a in parts],
        in_specs=[ANY] * n, out_specs=[ANY] * n, scratch_shapes=[_sems(6 * n), _sems(6 * n), _sems(n)], name=name,
    )(*parts)


def sibling_exchange(gs, *, name):
    n = len(gs)

    def body(*refs):
        g_refs, out_refs = refs[:n], refs[n:2 * n]
        send_sems, recv_sems = refs[2 * n:]
        x, y, c, _ = _mesh_position()
        cps = [_remote(g_refs[a].at[:, 1 - c], out_refs[a], send_sems, recv_sems, a, (x, y, 1 - c)) for a in range(n)]
        for cp in cps:
            cp.start()
        for cp in cps:
            cp.wait()

    return pl.pallas_call(
        body, out_shape=[jax.ShapeDtypeStruct(g.shape[:1] + g.shape[2:], g.dtype) for g in gs],
        in_specs=[ANY] * n, out_specs=[ANY] * n, scratch_shapes=[_sems(n), _sems(n)], name=name,
    )(*gs)


def add_own_half(gs, others, c_idx, *, name):
    n = len(gs)
    ns, _, r, cols = gs[0].shape
    tr = _tile(r, 256, SUBLANES)

    def body(c_ref, *refs):
        for a in range(n):
            refs[2 * n + a][...] = refs[a][...] + refs[n + a][...]

    own = pl.BlockSpec((None, None, tr, cols), lambda s, i, c_ref: (s, c_ref[0], i, 0))
    oth = pl.BlockSpec((None, tr, cols), lambda s, i, c_ref: (s, i, 0))
    return pl.pallas_call(
        body,
        grid_spec=pltpu.PrefetchScalarGridSpec(num_scalar_prefetch=1, grid=(ns, r // tr),
                                               in_specs=[own] * n + [oth] * n, out_specs=[oth] * n),
        out_shape=[jax.ShapeDtypeStruct((ns, r, cols), f32)] * n,
        compiler_params=_params(("parallel", "parallel")), name=name,
    )(c_idx, *gs, *others)


def chip_exchange(ps, *, name):
    n = len(ps)

    def body(*refs):
        p_refs, q_refs = refs[:n], refs[n:2 * n]
        send_sems, recv_sems, local_sems = refs[2 * n:]
        x, y, c, chips = _mesh_position()
        me = 2 * x + y
        waits = []
        for a in range(n):
            mine = pltpu.make_async_copy(p_refs[a].at[me], q_refs[a].at[me], local_sems.at[a])
            mine.start()
            waits.append(mine.wait)
        for j, (cx, cy) in enumerate(chips):
            for a in range(n):
                cp = _remote(p_refs[a].at[2 * cx + cy], q_refs[a].at[me], send_sems, recv_sems, 3 * a + j, (cx, cy, c))
                cp.start()
                waits.append(cp.wait_send)
        for j, (cx, cy) in enumerate(chips):
            for a in range(n):
                got = q_refs[a].at[2 * cx + cy]
                _remote(got, got, send_sems, recv_sems, 3 * a + j, (cx, cy, c)).wait_recv()
        for wait in waits:
            wait()

    return pl.pallas_call(
        body, out_shape=[jax.ShapeDtypeStruct(p_.shape, p_.dtype) for p_ in ps], in_specs=[ANY] * n,
        out_specs=[ANY] * n, scratch_shapes=[_sems(3 * n), _sems(3 * n), _sems(n)], name=name,
    )(*ps)


def sum_slots(qs, *, name):
    n = len(qs)
    ns, r, cols = qs[0].shape
    tr = _tile(r, 128, SUBLANES)

    def body(*refs):
        for a in range(n):
            q_ref = refs[a]
            acc = q_ref[0] + q_ref[1]
            for i in range(2, ns):
                acc = acc + q_ref[i]
            refs[n + a][...] = acc

    return pl.pallas_call(
        body, grid=(r // tr,), in_specs=[pl.BlockSpec((ns, tr, cols), lambda i: (0, i, 0))] * n,
        out_specs=[pl.BlockSpec((tr, cols), lambda i: (i, 0))] * n,
        out_shape=[jax.ShapeDtypeStruct((r, cols), f32)] * n,
        compiler_params=_params(("parallel",)), name=name,
    )(*qs)


def sibling_share(fs, *, name):
    n = len(fs)

    def body(*refs):
        f_refs, out_refs = refs[:n], refs[n:2 * n]
        send_sems, recv_sems, local_sems = refs[2 * n:]
        x, y, c, _ = _mesh_position()
        sibling = (x, y, 1 - c)
        waits = []
        for a in range(n):
            own = out_refs[a].at[c]
            mine = pltpu.make_async_copy(f_refs[a], own, local_sems.at[a])
            mine.start()
            cp = _remote(f_refs[a], own, send_sems, recv_sems, a, sibling)
            cp.start()
            waits += [mine.wait, cp.wait_send]
        for a in range(n):
            theirs = out_refs[a].at[1 - c]
            _remote(theirs, theirs, send_sems, recv_sems, a, sibling).wait_recv()
        for wait in waits:
            wait()

    return pl.pallas_call(
        body, out_shape=[jax.ShapeDtypeStruct((2,) + f_.shape, f_.dtype) for f_ in fs], in_specs=[ANY] * n,
        out_specs=[ANY] * n, scratch_shapes=[_sems(n), _sems(n), _sems(n)], name=name,
    )(*fs)


def adamw(ws, gs, ms, vs, *, name):
    n = len(ws)
    r, cols = ws[0].shape
    tr = _tile(r, 128, SUBLANES)

    def body(*refs):
        for a in range(n):
            w_ref, g_ref, m_ref, v_ref = (refs[k * n + a] for k in range(4))
            d_ref, m2_ref, v2_ref = (refs[(4 + k) * n + a] for k in range(3))
            g_ = g_ref[...]
            m2 = ADAM_B1 * m_ref[...] + (1.0 - ADAM_B1) * g_
            v2 = ADAM_B2 * v_ref[...] + (1.0 - ADAM_B2) * (g_ * g_)
            m_hat = m2 / (1.0 - ADAM_B1 ** ADAM_STEP)
            v_hat = v2 / (1.0 - ADAM_B2 ** ADAM_STEP)
            d_ref[...] = -ADAM_LR * (m_hat / (jnp.sqrt(v_hat) + ADAM_EPS) + ADAM_WD * w_ref[...])
            m2_ref[...] = m2
            v2_ref[...] = v2

    row = pl.BlockSpec((tr, cols), lambda i: (i, 0))
    out = pl.pallas_call(
        body, grid=(r // tr,), in_specs=[row] * (4 * n), out_specs=[row] * (3 * n),
        out_shape=[jax.ShapeDtypeStruct((r, cols), f32)] * (3 * n),
        compiler_params=_params(("parallel",)), name=name,
    )(*ws, *gs, *ms, *vs)
    return out[:n], out[n:2 * n], out[2 * n:]


def _full_weights(gathered, local, shapes):
    pieces = _unpack(gathered, shapes, lead=(N_CHIPS,))
    full = {}
    for name, loc, pc in zip(PACKED_NAMES, local, pieces):
        ax = SHARD_AXIS.get(name)
        full[name] = loc if ax is None else jnp.concatenate([pc[s] for s in range(N_CHIPS)], axis=ax)
    return full


def _grad_pack(grads, shapes):
    pieces = []
    for name, shp in zip(PACKED_NAMES, shapes):
        g = grads[name]
        ax = SHARD_AXIS.get(name)
        if ax is None:
            pieces.append(jnp.broadcast_to(g.reshape(shp)[None], (N_CHIPS,) + tuple(shp)))
        else:
            pieces.append(jnp.stack(jnp.split(g, N_CHIPS, axis=ax)))
    return _pack(pieces, lead=(N_CHIPS,))


def _by_shape(arrays):
    groups = {}
    for i, a in enumerate(arrays):
        groups.setdefault(a.shape, []).append(i)
    return list(groups.values())


def _grouped(fn, lists, n_out, tag):
    outs = [[None] * len(lists[0]) for _ in range(n_out)]
    for gi, idx in enumerate(_by_shape(lists[0])):
        res = fn(*[[lst[i] for i in idx] for lst in lists], name=f"{tag}_{gi}")
        res = res if n_out > 1 else (res,)
        for k in range(n_out):
            for i, r in zip(idx, res[k]):
                outs[k][i] = r
    return outs if n_out > 1 else outs[0]


def _train_step(x, p, loss_target, weights, m, v):
    packed_w = [weights[k] for k in PACKED_NAMES]
    shapes = [w.shape for w in packed_w]
    halves = lambda a: a.reshape((2, a.shape[0] // 2) + a.shape[1:])
    local = [weights[k] for k in NATIVE_NAMES] + [halves(_pack(packed_w))]
    local_m = [m[k] for k in NATIVE_NAMES] + [halves(_pack([m[k] for k in PACKED_NAMES]))]
    local_v = [v[k] for k in NATIVE_NAMES] + [halves(_pack([v[k] for k in PACKED_NAMES]))]
    sent = [a.astype(bf16) for a in local[:-1]] + [local[-1]]
    gathered = gather_shards(sent, name="comm_gather_weights")
    full = _full_weights(gathered[-1], packed_w, shapes)
    loss, grad_x, grads = _local_step(x, p, loss_target, full, dict(zip(NATIVE_NAMES, gathered[:-1])))
    gpack = _grad_pack(grads, shapes)
    gs = [grads[k] for k in NATIVE_NAMES] + [gpack.reshape((N_CHIPS,) + local[-1].shape)]
    c_idx = lax.axis_index("c").astype(jnp.int32).reshape(1)
    others = sibling_exchange(gs, name="comm_grad_sibling")
    chip_sums = _grouped(lambda a, b, name: add_own_half(a, b, c_idx, name=name), [gs, others], 1, "grad_add_sibling")
    slots = chip_exchange(chip_sums, name="comm_grad_chips")
    mine = _grouped(sum_slots, [slots], 1, "grad_sum_chips")
    gsum = sibling_share(mine, name="comm_grad_share")
    flat = lambda lst: [a.reshape((-1, a.shape[-1])) for a in lst]
    delta, m2, v2 = _grouped(adamw, [flat(local), flat(gsum), flat(local_m), flat(local_v)], 3, "adamw")
    loss = lax.psum(loss, ("x", "y", "c"))
    outs = []
    for res in (gsum, delta, m2, v2):
        by_name = {k: a.reshape(weights[k].shape) for k, a in zip(NATIVE_NAMES, res[:-1])}
        by_name.update(zip(PACKED_NAMES, _unpack(res[-1], shapes)))
        outs += [by_name[k] for k in WEIGHT_NAMES]
    return (loss, grad_x, *outs)


def kernel(x, p, ffn1_wg, ffn1_wu, ffn1_wd, ffn2_wg, ffn2_wu, ffn2_wd, ln_g, ln_b, ple_wg, ple_bg, ple_wp, ab_w_in, a_sinks, b_conv_w, b_conv_b, b_wa, b_ba, b_wx, b_bx, b_lam, ab_w_out, c_w_in, c_conv_w, c_a_log, c_dt_bias, c_norm_g, c_w_out, loss_target, m_ffn1_wg, m_ffn1_wu, m_ffn1_wd, m_ffn2_wg, m_ffn2_wu, m_ffn2_wd, m_ln_g, m_ln_b, m_ple_wg, m_ple_bg, m_ple_wp, m_ab_w_in, m_a_sinks, m_b_conv_w, m_b_conv_b, m_b_wa, m_b_ba, m_b_wx, m_b_bx, m_b_lam, m_ab_w_out, m_c_w_in, m_c_conv_w, m_c_a_log, m_c_dt_bias, m_c_norm_g, m_c_w_out, v_ffn1_wg, v_ffn1_wu, v_ffn1_wd, v_ffn2_wg, v_ffn2_wu, v_ffn2_wd, v_ln_g, v_ln_b, v_ple_wg, v_ple_bg, v_ple_wp, v_ab_w_in, v_a_sinks, v_b_conv_w, v_b_conv_b, v_b_wa, v_b_ba, v_b_wx, v_b_bx, v_b_lam, v_ab_w_out, v_c_w_in, v_c_conv_w, v_c_a_log, v_c_dt_bias, v_c_norm_g, v_c_w_out):
    weights = [ffn1_wg, ffn1_wu, ffn1_wd, ffn2_wg, ffn2_wu, ffn2_wd, ln_g, ln_b, ple_wg, ple_bg, ple_wp, ab_w_in, a_sinks,
               b_conv_w, b_conv_b, b_wa, b_ba, b_wx, b_bx, b_lam, ab_w_out, c_w_in, c_conv_w, c_a_log, c_dt_bias, c_norm_g,
               c_w_out]
    m = [m_ffn1_wg, m_ffn1_wu, m_ffn1_wd, m_ffn2_wg, m_ffn2_wu, m_ffn2_wd, m_ln_g, m_ln_b, m_ple_wg, m_ple_bg, m_ple_wp,
         m_ab_w_in, m_a_sinks, m_b_conv_w, m_b_conv_b, m_b_wa, m_b_ba, m_b_wx, m_b_bx, m_b_lam, m_ab_w_out, m_c_w_in,
         m_c_conv_w, m_c_a_log, m_c_dt_bias, m_c_norm_g, m_c_w_out]
    v = [v_ffn1_wg, v_ffn1_wu, v_ffn1_wd, v_ffn2_wg, v_ffn2_wu, v_ffn2_wd, v_ln_g, v_ln_b, v_ple_wg, v_ple_bg, v_ple_wp,
         v_ab_w_in, v_a_sinks, v_b_conv_w, v_b_conv_b, v_b_wa, v_b_ba, v_b_wx, v_b_bx, v_b_lam, v_ab_w_out, v_c_w_in,
         v_c_conv_w, v_c_a_log, v_c_dt_bias, v_c_norm_g, v_c_w_out]
    return _train_step(x, p, loss_target, dict(zip(WEIGHT_NAMES, weights)), dict(zip(WEIGHT_NAMES, m)),
                       dict(zip(WEIGHT_NAMES, v)))
```

```python
import functools

import jax
import jax.numpy as jnp
from jax import lax
from jax.experimental import pallas as pl
from jax.experimental.pallas import tpu as pltpu

f32 = jnp.float32
bf16 = jnp.bfloat16

DEPTH = 2
CHUNK = 64
A_HEADS, A_KV_HEADS, A_GROUP, A_HEAD_DIM = 8, 2, 4, 64
A_WIDTH, A_KV_WIDTH, A_WINDOW = 512, 128, 128
B_WIDTH, B_BLOCKS, B_BLOCK, B_CONV = 512, 8, 64, 4
RG_C = 8.0
C_HEADS, C_HEAD_DIM, C_WIDTH, C_CONV = 8, 128, 1024, 4
DN_ALPHA = (2.0 * DEPTH) ** 0.25
LN_EPS = 1e-5
NORM_EPS = 1e-6
NEG = -1e30
ADAM_LR, ADAM_B1, ADAM_B2, ADAM_EPS, ADAM_WD, ADAM_STEP = 0.001, 0.9, 0.999, 1e-08, 0.01, 10

VMEM_LIMIT_BYTES = 56 * 1024 * 1024
LANES = 128
SUBLANES = 8
GROUP_W = 128
PREP_FWD_UNROLL = 4
PREP_BWD_UNROLL = 4
C_HEADS_PER_STEP_FWD = 4
C_HEADS_PER_STEP_BWD = 2

NN = ((1,), (0,))
NT = ((1,), (1,))
TN = ((0,), (0,))


def _params(sem):
    return pltpu.CompilerParams(dimension_semantics=sem, vmem_limit_bytes=VMEM_LIMIT_BYTES)


def _tile(n, cap, mult):
    best = None
    t = mult
    while t <= min(n, cap):
        if n % t == 0:
            best = t
        t += mult
    return best if best is not None else n


def _bdot(a, b, dims):
    return lax.dot_general(a.astype(bf16), b.astype(bf16), (dims, ((), ())), preferred_element_type=f32)


def _h3dot(a, b):
    return lax.dot_general(a, b, (NN, ((), ())), precision=lax.Precision.HIGH, preferred_element_type=f32)


def _running_sum(x, reverse):
    s = x.shape[0]
    t = lax.broadcasted_iota(jnp.int32, x.shape, 0)
    d = 1
    while d < s:
        if reverse:
            x = x + jnp.where(t < s - d, pltpu.roll(x, s - d, 0), 0.0)
        else:
            x = x + jnp.where(t >= d, pltpu.roll(x, d, 0), 0.0)
        d *= 2
    return x


@jax.custom_vjp
def _cumsum0(x):
    return _running_sum(x, False)


def _cumsum0_fwd(x):
    return _running_sum(x, False), None


def _cumsum0_bwd(_, g):
    return (_running_sum(g, True),)


_cumsum0.defvjp(_cumsum0_fwd, _cumsum0_bwd)


@jax.custom_vjp
def _bnn(a, b):
    return _bdot(a, b, NN)


def _bnn_fwd(a, b):
    return _bdot(a, b, NN), (a, b)


def _bnn_bwd(res, g):
    a, b = res
    return _bdot(g, b, NT), _bdot(a, g, TN)


_bnn.defvjp(_bnn_fwd, _bnn_bwd)


@jax.custom_vjp
def _bnt(a, b):
    return _bdot(a, b, NT)


def _bnt_fwd(a, b):
    return _bdot(a, b, NT), (a, b)


def _bnt_bwd(res, g):
    a, b = res
    return _bdot(g, b, NN), _bdot(g, a, TN)


_bnt.defvjp(_bnt_fwd, _bnt_bwd)


@jax.custom_vjp
def _btn(a, b):
    return _bdot(a, b, TN)


def _btn_fwd(a, b):
    return _bdot(a, b, TN), (a, b)


def _btn_bwd(res, g):
    a, b = res
    return _bdot(b, g, NT), _bdot(a, g, NN)


_btn.defvjp(_btn_fwd, _btn_bwd)

RAW_DOTS = (lambda a, b: _bdot(a, b, NN), lambda a, b: _bdot(a, b, NT), lambda a, b: _bdot(a, b, TN),
            lambda x: _running_sum(x, False))
VJP_DOTS = (_bnn, _bnt, _btn, _cumsum0)


def _layer_norm(z, g, b):
    mu = jnp.mean(z, -1, keepdims=True)
    d = z - mu
    var = jnp.mean(d * d, -1, keepdims=True)
    return d * lax.rsqrt(var + LN_EPS) * g + b


def _silu(x):
    return x * jax.nn.sigmoid(x)


def mm_nn(a, w, add=None, add_scale=1.0, *, name):
    m, k = a.shape
    n = w.shape[1]
    tm = _tile(m, 512, SUBLANES)
    tn = _tile(n, 1024, LANES)

    def body(*refs):
        if add is None:
            a_ref, w_ref, o_ref = refs
            o_ref[...] = _bdot(a_ref[...], w_ref[...], NN)
        else:
            a_ref, w_ref, add_ref, o_ref = refs
            o_ref[...] = _bdot(a_ref[...], w_ref[...], NN) + add_scale * add_ref[...]

    in_specs = [pl.BlockSpec((tm, k), lambda i, j: (i, 0)), pl.BlockSpec((k, tn), lambda i, j: (0, j))]
    args = [a, w]
    if add is not None:
        in_specs.append(pl.BlockSpec((tm, tn), lambda i, j: (i, j)))
        args.append(add)
    return pl.pallas_call(
        body, grid=(m // tm, n // tn), in_specs=in_specs,
        out_specs=pl.BlockSpec((tm, tn), lambda i, j: (i, j)),
        out_shape=jax.ShapeDtypeStruct((m, n), f32),
        compiler_params=_params(("parallel", "parallel")), name=name,
    )(*args)


def mm_tn(a, b, *, name):
    m, k = a.shape
    n = b.shape[1]
    tm = _tile(m, 512, SUBLANES)
    tn = _tile(n, 512, LANES)

    def body(a_ref, b_ref, o_ref):
        part = _bdot(a_ref[...], b_ref[...], TN)

        @pl.when(pl.program_id(1) == 0)
        def _():
            o_ref[...] = part

        @pl.when(pl.program_id(1) > 0)
        def _():
            o_ref[...] += part

    return pl.pallas_call(
        body, grid=(n // tn, m // tm),
        in_specs=[pl.BlockSpec((tm, k), lambda j, i: (i, 0)), pl.BlockSpec((tm, tn), lambda j, i: (i, j))],
        out_specs=pl.BlockSpec((k, tn), lambda j, i: (0, j)),
        out_shape=jax.ShapeDtypeStruct((k, n), f32),
        compiler_params=_params(("parallel", "arbitrary")), name=name,
    )(a, b)


def proj_ln(a_list, w_list, xres, g, b, *, name):
    t, d = xres.shape
    tm = _tile(t, 256, SUBLANES)
    na = len(a_list)

    def body(*refs):
        a_refs, w_refs = refs[:na], refs[na:2 * na]
        x_ref, g_ref, b_ref, y_ref, z_ref = refs[2 * na:]
        z = DN_ALPHA * x_ref[...]
        for a_ref, w_ref in zip(a_refs, w_refs):
            z = z + _bdot(a_ref[...], w_ref[...], NN)
        z_ref[...] = z
        y_ref[...] = _layer_norm(z, g_ref[...], b_ref[...])

    in_specs = [pl.BlockSpec((tm, a.shape[1]), lambda i: (i, 0)) for a in a_list]
    in_specs += [pl.BlockSpec(w.shape, lambda i: (0, 0)) for w in w_list]
    in_specs += [pl.BlockSpec((tm, d), lambda i: (i, 0)), pl.BlockSpec((1, d), lambda i: (0, 0)),
                 pl.BlockSpec((1, d), lambda i: (0, 0))]
    return pl.pallas_call(
        body, grid=(t // tm,), in_specs=in_specs,
        out_specs=[pl.BlockSpec((tm, d), lambda i: (i, 0))] * 2,
        out_shape=[jax.ShapeDtypeStruct((t, d), f32)] * 2,
        compiler_params=_params(("parallel",)), name=name,
    )(*a_list, *w_list, xres, g, b)


def ln_bwd(z, dy, g, *, name):
    t, d = z.shape
    tm = _tile(t, 512, SUBLANES)

    def body(z_ref, dy_ref, g_ref, dz_ref, dzb_ref, dg_ref, db_ref):
        zz = z_ref[...]
        dy_ = dy_ref[...]
        mu = jnp.mean(zz, -1, keepdims=True)
        dd = zz - mu
        var = jnp.mean(dd * dd, -1, keepdims=True)
        rstd = lax.rsqrt(var + LN_EPS)
        xhat = dd * rstd
        dxh = dy_ * g_ref[...]
        dz = rstd * (dxh - jnp.mean(dxh, -1, keepdims=True) - xhat * jnp.mean(dxh * xhat, -1, keepdims=True))
        dz_ref[...] = dz
        dzb_ref[...] = dz.astype(bf16)
        pg = jnp.sum(dy_ * xhat, 0, keepdims=True)
        pb = jnp.sum(dy_, 0, keepdims=True)

        @pl.when(pl.program_id(0) == 0)
        def _():
            dg_ref[...] = pg
            db_ref[...] = pb

        @pl.when(pl.program_id(0) > 0)
        def _():
            dg_ref[...] += pg
            db_ref[...] += pb

    row = pl.BlockSpec((tm, d), lambda i: (i, 0))
    vec = pl.BlockSpec((1, d), lambda i: (0, 0))
    return pl.pallas_call(
        body, grid=(t // tm,), in_specs=[row, row, vec], out_specs=[row, row, vec, vec],
        out_shape=[jax.ShapeDtypeStruct((t, d), f32), jax.ShapeDtypeStruct((t, d), bf16),
                   jax.ShapeDtypeStruct((1, d), f32), jax.ShapeDtypeStruct((1, d), f32)],
        compiler_params=_params(("arbitrary",)), name=name,
    )(z, dy, g)


def loss_head(y, target, *, name):
    t, d = y.shape
    tm = _tile(t, 512, SUBLANES)

    def body(y_ref, t_ref, dy_ref, sq_ref):
        e = y_ref[...] - t_ref[...]
        dy_ref[...] = e * (1.0 / d)
        part = jnp.sum(e * e, 0, keepdims=True)

        @pl.when(pl.program_id(0) == 0)
        def _():
            sq_ref[...] = part

        @pl.when(pl.program_id(0) > 0)
        def _():
            sq_ref[...] += part

    row = pl.BlockSpec((tm, d), lambda i: (i, 0))
    vec = pl.BlockSpec((1, d), lambda i: (0, 0))
    return pl.pallas_call(
        body, grid=(t // tm,), in_specs=[row, row], out_specs=[row, vec],
        out_shape=[jax.ShapeDtypeStruct((t, d), f32), jax.ShapeDtypeStruct((1, d), f32)],
        compiler_params=_params(("arbitrary",)), name=name,
    )(y, target)


def ffn_fwd(x, wg, wu, wd, layer, g, b, *, name):
    t, d = x.shape
    nf, _, _, tf = wg.shape
    tm = _tile(t, 512, SUBLANES)

    def body(x_ref, wg_ref, wu_ref, wd_ref, g_ref, b_ref, y_ref, z_ref, acc_ref):
        f = pl.program_id(1)
        xb = x_ref[...].astype(bf16)
        gate = _bdot(xb, wg_ref[...], NN)
        up = _bdot(xb, wu_ref[...], NN)
        part = _bdot(_silu(gate) * up, wd_ref[...], NN)

        @pl.when(f == 0)
        def _():
            acc_ref[...] = part

        @pl.when(f > 0)
        def _():
            acc_ref[...] += part

        @pl.when(f == nf - 1)
        def _():
            z = DN_ALPHA * x_ref[...] + 0.5 * acc_ref[...]
            z_ref[...] = z
            y_ref[...] = _layer_norm(z, g_ref[...], b_ref[...])

    row = pl.BlockSpec((tm, d), lambda i, j: (i, 0))
    vec = pl.BlockSpec((1, d), lambda i, j: (0, 0))
    wcol = pl.BlockSpec((None, None, d, tf), lambda i, j: (j, layer, 0, 0))
    wrow = pl.BlockSpec((None, None, tf, d), lambda i, j: (j, layer, 0, 0))
    return pl.pallas_call(
        body, grid=(t // tm, nf),
        in_specs=[row, wcol, wcol, wrow, vec, vec],
        out_specs=[row, row],
        out_shape=[jax.ShapeDtypeStruct((t, d), f32)] * 2,
        scratch_shapes=[pltpu.VMEM((tm, d), f32)],
        compiler_params=_params(("parallel", "arbitrary")), name=name,
    )(x, wg, wu, wd, g, b)


def ffn_bwd_weights(xb, dzb, wg, wu, wd, layer, acc, *, name):
    t, d = xb.shape
    nf, nl, _, tf = wg.shape
    tm = _tile(t, 512, SUBLANES)

    def body(x_ref, dz_ref, wg_ref, wu_ref, wd_ref, *rest):
        dgate_ref, dup_ref, dwg_ref, dwu_ref, dwd_ref = rest[-5:]
        x = x_ref[...]
        dzh = dz_ref[...] * 0.5
        gate = _bdot(x, wg_ref[...], NN)
        up = _bdot(x, wu_ref[...], NN)
        sg = jax.nn.sigmoid(gate)
        s = gate * sg
        dh = _bdot(dzh, wd_ref[...], NT)
        dup = (dh * s).astype(bf16)
        dgate = (dh * up * (sg * (1.0 + gate * (1.0 - sg)))).astype(bf16)
        dgate_ref[...] = dgate
        dup_ref[...] = dup
        pwg = _bdot(x, dgate, TN)
        pwu = _bdot(x, dup, TN)
        pwd = _bdot(s * up, dzh, TN)

        @pl.when(pl.program_id(1) == 0)
        def _():
            dwg_ref[...] = pwg
            dwu_ref[...] = pwu
            dwd_ref[...] = pwd

        @pl.when(pl.program_id(1) > 0)
        def _():
            dwg_ref[...] += pwg
            dwu_ref[...] += pwu
            dwd_ref[...] += pwd

    row = pl.BlockSpec((tm, d), lambda j, i: (i, 0))
    wcol = pl.BlockSpec((None, None, d, tf), lambda j, i: (j, layer, 0, 0))
    wrow = pl.BlockSpec((None, None, tf, d), lambda j, i: (j, layer, 0, 0))
    act = pl.BlockSpec((None, tm, tf), lambda j, i: (j, i, 0))
    in_specs = [row, row, wcol, wcol, wrow]
    args = [xb, dzb, wg, wu, wd]
    aliases = {}
    if acc is not None:
        in_specs += [pl.BlockSpec(memory_space=pl.ANY)] * 3
        args += list(acc)
        aliases = {5: 2, 6: 3, 7: 4}
    return pl.pallas_call(
        body, grid=(nf, t // tm), in_specs=in_specs, out_specs=[act, act, wcol, wcol, wrow],
        out_shape=[jax.ShapeDtypeStruct((nf, t, tf), bf16), jax.ShapeDtypeStruct((nf, t, tf), bf16),
                   jax.ShapeDtypeStruct((nf, nl, d, tf), f32), jax.ShapeDtypeStruct((nf, nl, d, tf), f32),
                   jax.ShapeDtypeStruct((nf, nl, tf, d), f32)],
        input_output_aliases=aliases,
        compiler_params=_params(("parallel", "arbitrary")), name=name,
    )(*args)


def ffn_bwd_input(dgate, dup, wg, wu, layer, dz, *, name):
    nf, t, tf = dgate.shape
    d = wg.shape[2]
    tm = _tile(t, 256, SUBLANES)

    def body(dg_ref, du_ref, wg_ref, wu_ref, dz_ref, dx_ref):
        acc = DN_ALPHA * dz_ref[...]
        for j in range(nf):
            acc = acc + _bdot(dg_ref[j], wg_ref[j], NT) + _bdot(du_ref[j], wu_ref[j], NT)
        dx_ref[...] = acc

    act = pl.BlockSpec((nf, tm, tf), lambda i: (0, i, 0))
    wsp = pl.BlockSpec((nf, None, d, tf), lambda i: (0, layer, 0, 0))
    row = pl.BlockSpec((tm, d), lambda i: (i, 0))
    return pl.pallas_call(
        body, grid=(t // tm,), in_specs=[act, act, wsp, wsp, row], out_specs=row,
        out_shape=jax.ShapeDtypeStruct((t, d), f32),
        compiler_params=_params(("parallel",)), name=name,
    )(dgate, dup, wg, wu, dz)


def ple_fwd(x, p, wg, bg, wp, *, name):
    t, d = x.shape
    dp = p.shape[1]
    tm = _tile(t, 512, SUBLANES)

    def body(x_ref, p_ref, wg_ref, bg_ref, wp_ref, o_ref):
        x_ = x_ref[...]
        gate = jax.nn.sigmoid(_bdot(x_, wg_ref[...], NN) + bg_ref[...])
        o_ref[...] = x_ + gate * _bdot(p_ref[...], wp_ref[...], NN)

    row = pl.BlockSpec((tm, d), lambda i: (i, 0))
    return pl.pallas_call(
        body, grid=(t // tm,),
        in_specs=[row, pl.BlockSpec((tm, dp), lambda i: (i, 0)), pl.BlockSpec((d, d), lambda i: (0, 0)),
                  pl.BlockSpec((1, d), lambda i: (0, 0)), pl.BlockSpec((dp, d), lambda i: (0, 0))],
        out_specs=row, out_shape=jax.ShapeDtypeStruct((t, d), f32),
        compiler_params=_params(("parallel",)), name=name,
    )(x, p, wg, bg, wp)


def ple_bwd(x, p, dy, wg, wgt, bg, wp, *, name):
    t, d = x.shape
    dp = p.shape[1]
    tm = _tile(t, 512, SUBLANES)

    def body(x_ref, p_ref, dy_ref, wg_ref, wgt_ref, bg_ref, wp_ref, dx_ref, dwg_ref, dbg_ref, dwp_ref):
        x_ = x_ref[...]
        dy_ = dy_ref[...]
        s = jax.nn.sigmoid(_bdot(x_, wg_ref[...], NN) + bg_ref[...])
        e = _bdot(p_ref[...], wp_ref[...], NN)
        da = dy_ * e * s * (1.0 - s)
        de = dy_ * s
        dx_ref[...] = dy_ + _bdot(da, wgt_ref[...], NN)
        pwg = _bdot(x_, da, TN)
        pbg = jnp.sum(da, 0, keepdims=True)
        pwp = _bdot(p_ref[...], de, TN)

        @pl.when(pl.program_id(0) == 0)
        def _():
            dwg_ref[...] = pwg
            dbg_ref[...] = pbg
            dwp_ref[...] = pwp

        @pl.when(pl.program_id(0) > 0)
        def _():
            dwg_ref[...] += pwg
            dbg_ref[...] += pbg
            dwp_ref[...] += pwp

    row = pl.BlockSpec((tm, d), lambda i: (i, 0))
    full = lambda shape: pl.BlockSpec(shape, lambda i: (0, 0))
    return pl.pallas_call(
        body, grid=(t // tm,),
        in_specs=[row, pl.BlockSpec((tm, dp), lambda i: (i, 0)), row, full((d, d)), full((d, d)), full((1, d)),
                  full((dp, d))],
        out_specs=[row, full((d, d)), full((1, d)), full((dp, d))],
        out_shape=[jax.ShapeDtypeStruct((t, d), f32), jax.ShapeDtypeStruct((d, d), f32),
                   jax.ShapeDtypeStruct((1, d), f32), jax.ShapeDtypeStruct((dp, d), f32)],
        compiler_params=_params(("arbitrary",)), name=name,
    )(x, p, dy, wg, wgt, bg, wp)


def _conv_taps(xpad_ref, w_ref, s):
    acc = w_ref[0:1, :] * xpad_ref[SUBLANES - 3:SUBLANES - 3 + s, :]
    for j in range(1, 4):
        acc = acc + w_ref[j:j + 1, :] * xpad_ref[SUBLANES - 3 + j:SUBLANES - 3 + j + s, :]
    return acc


def conv_fwd(x, w, bias, act, nb, *, name):
    t, c = x.shape
    s = t // nb
    cw = GROUP_W

    def body(x_ref, w_ref, b_ref, y_ref, xpad):
        xpad[0:SUBLANES, :] = jnp.zeros((SUBLANES, cw), f32)
        xpad[SUBLANES:, :] = x_ref[...]
        acc = _conv_taps(xpad, w_ref, s) + b_ref[...]
        y_ref[...] = _silu(acc) if act else acc

    slab = pl.BlockSpec((s, cw), lambda b, g: (b, g))
    return pl.pallas_call(
        body, grid=(nb, c // cw),
        in_specs=[slab, pl.BlockSpec((4, cw), lambda b, g: (0, g)), pl.BlockSpec((1, cw), lambda b, g: (0, g))],
        out_specs=slab, out_shape=jax.ShapeDtypeStruct((t, c), f32),
        scratch_shapes=[pltpu.VMEM((s + SUBLANES, cw), f32)],
        compiler_params=_params(("parallel", "parallel")), name=name,
    )(x, w, bias)


def conv_bwd(x, w, bias, dy, act, nb, *, name):
    t, c = x.shape
    s = t // nb
    cw = GROUP_W

    def body(x_ref, w_ref, b_ref, dy_ref, dx_ref, dw_ref, db_ref, xpad, dpad):
        xpad[0:SUBLANES, :] = jnp.zeros((SUBLANES, cw), f32)
        xpad[SUBLANES:, :] = x_ref[...]
        dacc = dy_ref[...]
        if act:
            acc = _conv_taps(xpad, w_ref, s) + b_ref[...]
            sg = jax.nn.sigmoid(acc)
            dacc = dacc * (sg * (1.0 + acc * (1.0 - sg)))
        dpad[0:s, :] = dacc
        dpad[s:, :] = jnp.zeros((SUBLANES, cw), f32)
        dx = w_ref[0:1, :] * dpad[3:3 + s, :]
        for j in range(1, 4):
            dx = dx + w_ref[j:j + 1, :] * dpad[3 - j:3 - j + s, :]
        dx_ref[...] = dx
        first = pl.program_id(1) == 0
        for j in range(4):
            pw = jnp.sum(dacc * xpad[SUBLANES - 3 + j:SUBLANES - 3 + j + s, :], 0, keepdims=True)

            @pl.when(first)
            def _():
                dw_ref[j:j + 1, :] = pw

            @pl.when(jnp.logical_not(first))
            def _():
                dw_ref[j:j + 1, :] += pw

        pb = jnp.sum(dacc, 0, keepdims=True)

        @pl.when(first)
        def _():
            db_ref[...] = pb

        @pl.when(jnp.logical_not(first))
        def _():
            db_ref[...] += pb

    slab = pl.BlockSpec((s, cw), lambda g, b: (b, g))
    wsp = pl.BlockSpec((4, cw), lambda g, b: (0, g))
    bsp = pl.BlockSpec((1, cw), lambda g, b: (0, g))
    return pl.pallas_call(
        body, grid=(c // cw, nb), in_specs=[slab, wsp, bsp, slab], out_specs=[slab, wsp, bsp],
        out_shape=[jax.ShapeDtypeStruct((t, c), f32), jax.ShapeDtypeStruct((4, c), f32),
                   jax.ShapeDtypeStruct((1, c), f32)],
        scratch_shapes=[pltpu.VMEM((s + SUBLANES, cw), f32), pltpu.VMEM((s + SUBLANES, cw), f32)],
        compiler_params=_params(("parallel", "arbitrary")), name=name,
    )(x, w, bias, dy)


def _each(f, *lists):
    return [f(*a) for a in zip(*lists)]


def _attn_heads(qs, kbs, vbs, sinks, valid, dist, dots):
    nn, nt = dots[:2]
    kv = [h // A_GROUP for h in range(A_HEADS)]
    scs = [nt(qs[h], kbs[kv[h]]) for h in range(A_HEADS)]
    prs = []
    for h in range(A_HEADS):
        sc = scs[h] * (A_HEAD_DIM ** -0.5) - 2.0 ** -(h + 1) * dist
        sc = jnp.where(valid, sc, NEG)
        m = jnp.maximum(jnp.max(sc, -1, keepdims=True), sinks[h])
        pr = jnp.exp(sc - m)
        den = jnp.sum(pr, -1, keepdims=True) + jnp.exp(sinks[h] - m)
        prs.append(pr / den)
    return [nn(prs[h], vbs[kv[h]]) for h in range(A_HEADS)]


def _attn_band_consts(r0):
    band = A_WINDOW + CHUNK
    qi = lax.broadcasted_iota(jnp.int32, (CHUNK, band), 0)
    kj = lax.broadcasted_iota(jnp.int32, (CHUNK, band), 1)
    dist = jnp.abs(qi + A_WINDOW - kj).astype(f32)
    valid = (kj + r0) >= A_WINDOW
    return dist, valid


def attn_fwd(qkv, sinks, nb, *, name):
    t = qkv.shape[0]
    s = t // nb
    band = A_WINDOW + CHUNK
    hd = A_HEAD_DIM

    def body(qkv_ref, sink_ref, o_ref, kvpad):
        kvpad[0:A_WINDOW, :] = jnp.zeros((A_WINDOW, 2 * A_KV_WIDTH), f32)
        kvpad[A_WINDOW:, :] = qkv_ref[:, A_WIDTH:]

        def chunk(n, carry):
            r0 = pl.multiple_of(n * CHUNK, CHUNK)
            dist, valid = _attn_band_consts(r0)
            kbs = [kvpad[pl.ds(r0, band), kvh * hd:(kvh + 1) * hd] for kvh in range(A_KV_HEADS)]
            vbs = [kvpad[pl.ds(r0, band), A_KV_WIDTH + kvh * hd:A_KV_WIDTH + (kvh + 1) * hd]
                   for kvh in range(A_KV_HEADS)]
            qs = [qkv_ref[pl.ds(r0, CHUNK), h * hd:(h + 1) * hd] for h in range(A_HEADS)]
            outs = _attn_heads(qs, kbs, vbs, [sink_ref[:, h:h + 1] for h in range(A_HEADS)], valid, dist, RAW_DOTS)
            for h in range(A_HEADS):
                o_ref[pl.ds(r0, CHUNK), h * hd:(h + 1) * hd] = outs[h]
            return carry

        lax.fori_loop(0, s // CHUNK, chunk, 0)

    return pl.pallas_call(
        body, grid=(nb,),
        in_specs=[pl.BlockSpec((s, A_WIDTH + 2 * A_KV_WIDTH), lambda b: (b, 0)),
                  pl.BlockSpec((1, A_HEADS), lambda b: (0, 0))],
        out_specs=pl.BlockSpec((s, A_WIDTH), lambda b: (b, 0)),
        out_shape=jax.ShapeDtypeStruct((t, A_WIDTH), f32),
        scratch_shapes=[pltpu.VMEM((s + A_WINDOW, 2 * A_KV_WIDTH), f32)],
        compiler_params=_params(("parallel",)), name=name,
    )(qkv, sinks)


def attn_bwd(qkv, sinks, do, nb, *, name):
    t = qkv.shape[0]
    s = t // nb
    band = A_WINDOW + CHUNK
    hd = A_HEAD_DIM
    kvw = 2 * A_KV_WIDTH

    def body(qkv_ref, sink_ref, do_ref, dqkv_ref, dsink_ref, kvpad, dkvpad):
        kvpad[0:A_WINDOW, :] = jnp.zeros((A_WINDOW, kvw), f32)
        kvpad[A_WINDOW:, :] = qkv_ref[:, A_WIDTH:]
        dkvpad[...] = jnp.zeros((s + A_WINDOW, kvw), f32)

        def chunk(n, dsinks):
            r0 = pl.multiple_of(n * CHUNK, CHUNK)
            dist, valid = _attn_band_consts(r0)
            ksl = [slice(kvh * hd, (kvh + 1) * hd) for kvh in range(A_KV_HEADS)]
            vsl = [slice(A_KV_WIDTH + kvh * hd, A_KV_WIDTH + (kvh + 1) * hd) for kvh in range(A_KV_HEADS)]
            kbs = [kvpad[pl.ds(r0, band), sl] for sl in ksl]
            vbs = [kvpad[pl.ds(r0, band), sl] for sl in vsl]
            dkbs = [dkvpad[pl.ds(r0, band), sl] for sl in ksl]
            dvbs = [dkvpad[pl.ds(r0, band), sl] for sl in vsl]
            qs = [qkv_ref[pl.ds(r0, CHUNK), h * hd:(h + 1) * hd] for h in range(A_HEADS)]
            dos = [do_ref[pl.ds(r0, CHUNK), h * hd:(h + 1) * hd] for h in range(A_HEADS)]
            fn = functools.partial(_attn_heads, valid=valid, dist=dist, dots=VJP_DOTS)
            _, vjp = jax.vjp(fn, qs, kbs, vbs, [sink_ref[:, h:h + 1] for h in range(A_HEADS)])
            dqs, dks, dvs, dss = vjp(dos)
            for h in range(A_HEADS):
                dqkv_ref[pl.ds(r0, CHUNK), h * hd:(h + 1) * hd] = dqs[h]
            for kvh in range(A_KV_HEADS):
                dkvpad[pl.ds(r0, band), ksl[kvh]] = dkbs[kvh] + dks[kvh]
                dkvpad[pl.ds(r0, band), vsl[kvh]] = dvbs[kvh] + dvs[kvh]
            return tuple(dsinks[h] + dss[h] for h in range(A_HEADS))

        dsinks = lax.fori_loop(0, s // CHUNK, chunk, tuple(jnp.zeros((1, 1), f32) for _ in range(A_HEADS)))
        dqkv_ref[:, A_WIDTH:] = dkvpad[A_WINDOW:, :]
        first = pl.program_id(0) == 0
        for h in range(A_HEADS):
            @pl.when(first)
            def _():
                dsink_ref[:, h:h + 1] = dsinks[h]

            @pl.when(jnp.logical_not(first))
            def _():
                dsink_ref[:, h:h + 1] += dsinks[h]

    wq = A_WIDTH + kvw
    return pl.pallas_call(
        body, grid=(nb,),
        in_specs=[pl.BlockSpec((s, wq), lambda b: (b, 0)), pl.BlockSpec((1, A_HEADS), lambda b: (0, 0)),
                  pl.BlockSpec((s, A_WIDTH), lambda b: (b, 0))],
        out_specs=[pl.BlockSpec((s, wq), lambda b: (b, 0)), pl.BlockSpec((1, A_HEADS), lambda b: (0, 0))],
        out_shape=[jax.ShapeDtypeStruct((t, wq), f32), jax.ShapeDtypeStruct((1, A_HEADS), f32)],
        scratch_shapes=[pltpu.VMEM((s + A_WINDOW, kvw), f32), pltpu.VMEM((s + A_WINDOW, kvw), f32)],
        compiler_params=_params(("arbitrary",)), name=name,
    )(qkv, sinks, do)


def _rg_gates(xc, wa, wx, ba, bx, lam, nn):
    r = jax.nn.sigmoid(nn(xc, wa) + ba)
    i = jax.nn.sigmoid(nn(xc, wx) + bx)
    log_a = -RG_C * r * jax.nn.softplus(-lam)
    a = jnp.exp(log_a)
    mult = jnp.sqrt(-jnp.tanh(log_a) * (jnp.exp(2.0 * log_a) + 1.0))
    return a, mult * (i * xc)


def _linear_scan(a, u, reverse):
    s = a.shape[0]
    t = lax.broadcasted_iota(jnp.int32, a.shape, 0)
    d = 1
    while d < s:
        if reverse:
            keep = t < s - d
            shift = s - d
        else:
            keep = t >= d
            shift = d
        us = jnp.where(keep, pltpu.roll(u, shift, 0), 0.0)
        as_ = jnp.where(keep, pltpu.roll(a, shift, 0), 1.0)
        u = u + a * us
        a = a * as_
        d *= 2
    return u


def rglru_fwd(xc, bg, wa, wx, ba, bx, lam, nb, *, name):
    t, c = xc.shape
    s = t // nb
    cw = GROUP_W

    def body(xc_ref, bg_ref, wa_ref, wx_ref, ba_ref, bx_ref, lam_ref, y_ref, h_ref):
        a, u = _rg_gates(xc_ref[...], wa_ref[...], wx_ref[...], ba_ref[...], bx_ref[...], lam_ref[...], RAW_DOTS[0])
        h = _linear_scan(a, u, False)
        h_ref[...] = h
        y_ref[...] = h * jax.nn.gelu(bg_ref[...])

    slab = pl.BlockSpec((s, cw), lambda b, g: (b, g))
    wsp = pl.BlockSpec((None, cw, cw), lambda b, g: (g, 0, 0))
    vec = pl.BlockSpec((1, cw), lambda b, g: (0, g))
    return pl.pallas_call(
        body, grid=(nb, c // cw), in_specs=[slab, slab, wsp, wsp, vec, vec, vec], out_specs=[slab, slab],
        out_shape=[jax.ShapeDtypeStruct((t, c), f32)] * 2,
        compiler_params=_params(("parallel", "parallel")), name=name,
    )(xc, bg, wa, wx, ba, bx, lam)


def rglru_bwd(xc, bg, h, dy, wa, wx, ba, bx, lam, nb, *, name):
    t, c = xc.shape
    s = t // nb
    cw = GROUP_W

    def body(xc_ref, bg_ref, h_ref, dy_ref, wa_ref, wx_ref, ba_ref, bx_ref, lam_ref,
             dxc_ref, dbg_ref, dwa_ref, dwx_ref, dba_ref, dbx_ref, dlam_ref):
        h = h_ref[...]
        dy_ = dy_ref[...]
        gel, gel_vjp = jax.vjp(jax.nn.gelu, bg_ref[...])
        dbg_ref[...] = gel_vjp(dy_ * h)[0]
        dh = dy_ * gel
        gates = functools.partial(_rg_gates, nn=_bnn)
        (a, _), gates_vjp = jax.vjp(gates, xc_ref[...], wa_ref[...], wx_ref[...], ba_ref[...], bx_ref[...],
                                    lam_ref[...])
        ti = lax.broadcasted_iota(jnp.int32, a.shape, 0)
        a_next = jnp.where(ti < s - 1, pltpu.roll(a, s - 1, 0), 0.0)
        lam_t = _linear_scan(a_next, dh, True)
        h_prev = jnp.where(ti >= 1, pltpu.roll(h, 1, 0), 0.0)
        dxc, dwa, dwx, dba, dbx, dlam = gates_vjp((lam_t * h_prev, lam_t))
        dxc_ref[...] = dxc
        first = pl.program_id(1) == 0

        @pl.when(first)
        def _():
            dwa_ref[...] = dwa
            dwx_ref[...] = dwx
            dba_ref[...] = dba
            dbx_ref[...] = dbx
            dlam_ref[...] = dlam

        @pl.when(jnp.logical_not(first))
        def _():
            dwa_ref[...] += dwa
            dwx_ref[...] += dwx
            dba_ref[...] += dba
            dbx_ref[...] += dbx
            dlam_ref[...] += dlam

    slab = pl.BlockSpec((s, cw), lambda g, b: (b, g))
    wsp = pl.BlockSpec((None, cw, cw), lambda g, b: (g, 0, 0))
    vec = pl.BlockSpec((1, cw), lambda g, b: (0, g))
    ng = c // cw
    return pl.pallas_call(
        body, grid=(ng, nb), in_specs=[slab, slab, slab, slab, wsp, wsp, vec, vec, vec],
        out_specs=[slab, slab, wsp, wsp, vec, vec, vec],
        out_shape=[jax.ShapeDtypeStruct((t, c), f32), jax.ShapeDtypeStruct((t, c), f32),
                   jax.ShapeDtypeStruct((ng, cw, cw), f32), jax.ShapeDtypeStruct((ng, cw, cw), f32),
                   jax.ShapeDtypeStruct((1, c), f32), jax.ShapeDtypeStruct((1, c), f32),
                   jax.ShapeDtypeStruct((1, c), f32)],
        compiler_params=_params(("parallel", "arbitrary")), name=name,
    )(xc, bg, h, dy, wa, wx, ba, bx, lam)


def _gdn_chunks_prep(qs, ks, vs, bls, als, a_log, dt_b, dots):
    nt, csum = dots[1], dots[3]
    hd = C_HEAD_DIM
    ri = lax.broadcasted_iota(jnp.int32, (CHUNK, CHUNK), 0)
    ci = lax.broadcasted_iota(jnp.int32, (CHUNK, CHUNK), 1)
    tril = ri >= ci
    strict = ri > ci
    eye = (ri == ci).astype(f32)
    qn = [q * lax.rsqrt(jnp.sum(q * q, -1, keepdims=True) + NORM_EPS) * (hd ** -0.5) for q in qs]
    kn = [k * lax.rsqrt(jnp.sum(k * k, -1, keepdims=True) + NORM_EPS) for k in ks]
    beta = [jax.nn.sigmoid(bl) for bl in bls]
    g = [-jnp.exp(a_log) * jax.nn.softplus(al + dt_b) for al in als]
    gc_sq = [csum(jnp.broadcast_to(g_, (CHUNK, CHUNK))) for g_ in g]
    gc = [csum(jnp.broadcast_to(g_, (CHUNK, hd))) for g_ in g]
    decay = [jnp.where(tril, jnp.exp(jnp.where(tril, s - s.T, 0.0)), 0.0) for s in gc_sq]
    kb = _each(jnp.multiply, kn, beta)
    kk = _each(nt, kb, kn)
    pw = [-jnp.where(strict, a * d, 0.0) for a, d in zip(kk, decay)]
    inv = [eye + p_ for p_ in pw]
    for _ in range(5):
        pw = _each(_h3dot, pw, pw)
        inv = _each(jnp.add, inv, _each(_h3dot, inv, pw))
    egc = [jnp.exp(c_) for c_ in gc]
    u = _each(_h3dot, inv, _each(jnp.multiply, vs, beta))
    w = _each(_h3dot, inv, _each(jnp.multiply, kb, egc))
    attn = _each(jnp.multiply, _each(nt, qn, kn), decay)
    g_last = [jnp.sum(jnp.broadcast_to(g_, (CHUNK, hd)), 0, keepdims=True) for g_ in g]
    qg = _each(jnp.multiply, qn, egc)
    kdec = [k_ * jnp.exp(gl_ - c_) for k_, gl_, c_ in zip(kn, g_last, gc)]
    return [(qg[i], kdec[i], w[i], u[i], attn[i], jnp.exp(g_last[i])) for i in range(len(qs))]


def _gdn_heads_step(states, qgs, kdecs, ws, us, attns, gls, zs, ng, dots):
    nn, tn = dots[0], dots[2]
    v_new = _each(jnp.subtract, us, _each(nn, ws, states))
    o = _each(jnp.add, _each(nn, qgs, states), _each(nn, attns, v_new))
    new = [s * gl for s, gl in zip(states, gls)]
    new = _each(jnp.add, new, _each(tn, kdecs, v_new))
    y = [o_ * lax.rsqrt(jnp.mean(o_ * o_, -1, keepdims=True) + NORM_EPS) * ng * _silu(z) for o_, z in zip(o, zs)]
    return y, new


def _loop_unrolled(n, unroll, load, compute, store, init):
    u = unroll if n % unroll == 0 else 1

    def trip(i, carry):
        idx = [i * u + j for j in range(u)]
        loaded = [load(k) for k in idx]
        results = compute(loaded)
        for k, r in zip(idx, results):
            carry = store(k, r, carry)
        return carry

    return lax.fori_loop(0, n // u, trip, init)


def _pick_lane(x, lane):
    li = lax.broadcasted_iota(jnp.int32, x.shape, 1)
    return jnp.sum(jnp.where(li == lane, x, 0.0), 1, keepdims=True)


def _put_lane(col, lane, width):
    li = lax.broadcasted_iota(jnp.int32, (col.shape[0], width), 1)
    return jnp.where(li == lane, col, 0.0)


def _gdn_specs(s, nc):
    hd = C_HEAD_DIM
    head = lambda off: pl.BlockSpec((s, hd), lambda b, h, off=off: (b, off + h))
    attn = pl.BlockSpec((None, s, CHUNK), lambda b, h: (h, b, 0))
    gl = pl.BlockSpec((None, nc * SUBLANES, hd), lambda b, h: (h, b, 0))
    ba = pl.BlockSpec((s, LANES), lambda b, h: (b, 0))
    sc8 = pl.BlockSpec((1, C_HEADS), lambda b, h: (0, 0))
    return head, attn, gl, ba, sc8


def gdn_prep_fwd(qkv, ba, a_log, dt_b, nb, *, name):
    t = qkv.shape[0]
    s = t // nb
    nc = s // CHUNK
    hd = C_HEAD_DIM
    head, attn_sp, gl_sp, ba_sp, sc8 = _gdn_specs(s, nc)

    def body(q_ref, k_ref, v_ref, ba_ref, alog_ref, dtb_ref, qg_ref, kd_ref, w_ref, u_ref, at_ref, gl_ref):
        h = pl.program_id(1)
        a_log_h = _pick_lane(alog_ref[...], h)
        dt_b_h = _pick_lane(dtb_ref[...], h)

        def load(n):
            rows = pl.ds(pl.multiple_of(n * CHUNK, CHUNK), CHUNK)
            bav = ba_ref[rows, :]
            return q_ref[rows, :], k_ref[rows, :], v_ref[rows, :], _pick_lane(bav, h), _pick_lane(bav, C_HEADS + h)

        def compute(loaded):
            return _gdn_chunks_prep(*[list(x) for x in zip(*loaded)], a_log_h, dt_b_h, RAW_DOTS)

        def store(n, outs, carry):
            rows = pl.ds(pl.multiple_of(n * CHUNK, CHUNK), CHUNK)
            qg_ref[rows, :] = outs[0].astype(bf16)
            kd_ref[rows, :] = outs[1].astype(bf16)
            w_ref[rows, :] = outs[2].astype(bf16)
            u_ref[rows, :] = outs[3]
            at_ref[rows, :] = outs[4].astype(bf16)
            gl_ref[pl.ds(pl.multiple_of(n * SUBLANES, SUBLANES), SUBLANES), :] = jnp.broadcast_to(outs[5], (SUBLANES, hd))
            return carry

        _loop_unrolled(nc, PREP_FWD_UNROLL, load, compute, store, 0)

    big = jax.ShapeDtypeStruct((t, C_WIDTH), f32)
    bigb = jax.ShapeDtypeStruct((t, C_WIDTH), bf16)
    return pl.pallas_call(
        body, grid=(nb, C_HEADS),
        in_specs=[head(0), head(C_HEADS), head(2 * C_HEADS), ba_sp, sc8, sc8],
        out_specs=[head(0)] * 4 + [attn_sp, gl_sp],
        out_shape=[bigb, bigb, bigb, big, jax.ShapeDtypeStruct((C_HEADS, t, CHUNK), bf16),
                               jax.ShapeDtypeStruct((C_HEADS, nb * nc * SUBLANES, hd), f32)],
        compiler_params=_params(("parallel", "parallel")), name=name,
    )(qkv, qkv, qkv, ba, a_log, dt_b)


def gdn_prep_bwd(qkv, ba, a_log, dt_b, cts, nb, *, name):
    t = qkv.shape[0]
    s = t // nb
    nc = s // CHUNK
    hd = C_HEAD_DIM
    head, attn_sp, gl_sp, ba_sp, sc8 = _gdn_specs(s, nc)

    def body(q_ref, k_ref, v_ref, ba_ref, alog_ref, dtb_ref, cqg, ckd, cw_, cu, cat, cgl,
             dq_ref, dk_ref, dv_ref, dba_ref, dalog_ref, ddtb_ref):
        b = pl.program_id(0)
        h = pl.program_id(1)
        a_log_h = _pick_lane(alog_ref[...], h)
        dt_b_h = _pick_lane(dtb_ref[...], h)
        prep = functools.partial(_gdn_chunks_prep, dots=VJP_DOTS)

        @pl.when(h == 0)
        def _():
            dba_ref[...] = jnp.zeros((s, LANES), f32)

        def load(n):
            rows = pl.ds(pl.multiple_of(n * CHUNK, CHUNK), CHUNK)
            bav = ba_ref[rows, :]
            cgl_n = cgl[pl.ds(pl.multiple_of(n * SUBLANES, SUBLANES), SUBLANES), :][0:1, :]
            primals = (q_ref[rows, :], k_ref[rows, :], v_ref[rows, :], _pick_lane(bav, h), _pick_lane(bav, C_HEADS + h))
            return primals, (cqg[rows, :], ckd[rows, :], cw_[rows, :], cu[rows, :], cat[rows, :], cgl_n), dba_ref[rows, :]

        def compute(loaded):
            primals = [list(x) for x in zip(*[item[0] for item in loaded])]
            _, vjp = jax.vjp(prep, *primals, a_log_h, dt_b_h)
            dqs, dks, dvs, dbls, dals, dalog, ddtb = vjp([item[1] for item in loaded])
            zero = jnp.zeros((1, 1), f32)
            return [((dqs[i], dks[i], dvs[i], dbls[i], dals[i], dalog if i == 0 else zero, ddtb if i == 0 else zero),
                     loaded[i][2]) for i in range(len(loaded))]

        def store(n, res, carry):
            (dq, dk, dv, dbl, dal, dalog_n, ddtb_n), dba_old = res
            rows = pl.ds(pl.multiple_of(n * CHUNK, CHUNK), CHUNK)
            dq_ref[rows, :] = dq
            dk_ref[rows, :] = dk
            dv_ref[rows, :] = dv
            dba_ref[rows, :] = dba_old + _put_lane(dbl, h, LANES) + _put_lane(dal, C_HEADS + h, LANES)
            return carry[0] + dalog_n, carry[1] + ddtb_n

        da_log, ddt_b = _loop_unrolled(nc, PREP_BWD_UNROLL, load, compute, store,
                                       (jnp.zeros((1, 1), f32), jnp.zeros((1, 1), f32)))
        first = jnp.logical_and(b == 0, h == 0)

        @pl.when(first)
        def _():
            dalog_ref[...] = _put_lane(da_log, h, LANES)
            ddtb_ref[...] = _put_lane(ddt_b, h, LANES)

        @pl.when(jnp.logical_not(first))
        def _():
            dalog_ref[...] += _put_lane(da_log, h, LANES)
            ddtb_ref[...] += _put_lane(ddt_b, h, LANES)

    big = jax.ShapeDtypeStruct((t, C_WIDTH), f32)
    vec = pl.BlockSpec((1, LANES), lambda b, h: (0, 0))
    return pl.pallas_call(
        body, grid=(nb, C_HEADS),
        in_specs=[head(0), head(C_HEADS), head(2 * C_HEADS), ba_sp, sc8, sc8] + [head(0)] * 4 + [attn_sp, gl_sp],
        out_specs=[head(0)] * 3 + [ba_sp, vec, vec],
        out_shape=[big] * 3 + [jax.ShapeDtypeStruct((t, LANES), f32), jax.ShapeDtypeStruct((1, LANES), f32),
                               jax.ShapeDtypeStruct((1, LANES), f32)],
        compiler_params=_params(("arbitrary", "arbitrary")), name=name,
    )(qkv, qkv, qkv, ba, a_log, dt_b, *cts)


def _gdn_rec_specs(s, nc, hp):
    hd = C_HEAD_DIM
    wide = pl.BlockSpec((s, hp * hd), lambda b, j: (b, j))
    attn = pl.BlockSpec((hp, s, CHUNK), lambda b, j: (j, b, 0))
    gl = pl.BlockSpec((hp, nc * SUBLANES, hd), lambda b, j: (j, b, 0))
    ng = pl.BlockSpec((1, hd), lambda b, j: (0, 0))
    return wide, attn, gl, ng


def gdn_rec_fwd(qg, kdec, w, u, attn, gl, z, ng, nb, *, name):
    t = qg.shape[0]
    s = t // nb
    nc = s // CHUNK
    hd = C_HEAD_DIM
    hp = C_HEADS_PER_STEP_FWD
    wide, attn_sp, gl_sp, ng_sp = _gdn_rec_specs(s, nc, hp)

    def body(qg_ref, kd_ref, w_ref, u_ref, at_ref, gl_ref, z_ref, ng_ref, y_ref):
        def chunk(n, states):
            rows = pl.ds(pl.multiple_of(n * CHUNK, CHUNK), CHUNK)
            grow = pl.ds(pl.multiple_of(n * SUBLANES, SUBLANES), SUBLANES)
            cols = [slice(j * hd, (j + 1) * hd) for j in range(hp)]
            ins = [(qg_ref[rows, c], kd_ref[rows, c], w_ref[rows, c], u_ref[rows, c], at_ref[j, rows, :],
                    gl_ref[j, grow, :][0:1, :], z_ref[rows, c]) for j, c in enumerate(cols)]
            ys, new = _gdn_heads_step(list(states), *[list(x) for x in zip(*ins)], ng_ref[...], RAW_DOTS)
            for j in range(hp):
                y_ref[rows, cols[j]] = ys[j]
            return tuple(new)

        lax.fori_loop(0, nc, chunk, tuple(jnp.zeros((hd, hd), f32) for _ in range(hp)))

    return pl.pallas_call(
        body, grid=(nb, C_HEADS // hp),
        in_specs=[wide] * 4 + [attn_sp, gl_sp, wide, ng_sp], out_specs=wide,
        out_shape=jax.ShapeDtypeStruct((t, C_WIDTH), f32),
        compiler_params=_params(("parallel", "parallel")), name=name,
    )(qg, kdec, w, u, attn, gl, z, ng)


def gdn_rec_bwd(qg, kdec, w, u, attn, gl, z, ng, dy, nb, *, name):
    t = qg.shape[0]
    s = t // nb
    nc = s // CHUNK
    hd = C_HEAD_DIM
    hp = C_HEADS_PER_STEP_BWD
    wide, attn_sp, gl_sp, ng_sp = _gdn_rec_specs(s, nc, hp)

    def body(qg_ref, kd_ref, w_ref, u_ref, at_ref, gl_ref, z_ref, ng_ref, dy_ref,
             dqg_ref, dkd_ref, dw_ref, du_ref, dat_ref, dgl_ref, dz_ref, dng_ref, states):
        step = functools.partial(_gdn_heads_step, dots=VJP_DOTS)

        def operands(n):
            rows = pl.ds(pl.multiple_of(n * CHUNK, CHUNK), CHUNK)
            grow = pl.ds(pl.multiple_of(n * SUBLANES, SUBLANES), SUBLANES)
            cols = [slice(j * hd, (j + 1) * hd) for j in range(hp)]
            return ([qg_ref[rows, c].astype(f32) for c in cols], [kd_ref[rows, c].astype(f32) for c in cols],
                    [w_ref[rows, c].astype(f32) for c in cols], [u_ref[rows, c] for c in cols],
                    [at_ref[j, rows, :].astype(f32) for j in range(hp)],
                    [gl_ref[j, grow, :][0:1, :] for j in range(hp)], [z_ref[rows, c] for c in cols])

        def fwd_chunk(n, sts):
            _, new = _gdn_heads_step(list(sts), *operands(n), ng_ref[...], RAW_DOTS)
            for j in range(hp):
                states[j, n] = sts[j]
            return tuple(new)

        lax.fori_loop(0, nc, fwd_chunk, tuple(jnp.zeros((hd, hd), f32) for _ in range(hp)))

        def bwd_chunk(i, carry):
            n = nc - 1 - i
            rows = pl.ds(pl.multiple_of(n * CHUNK, CHUNK), CHUNK)
            grow = pl.ds(pl.multiple_of(n * SUBLANES, SUBLANES), SUBLANES)
            dsts, dng = carry
            dys = [dy_ref[rows, j * hd:(j + 1) * hd] for j in range(hp)]
            _, vjp = jax.vjp(step, [states[j, n] for j in range(hp)], *operands(n), ng_ref[...])
            dst, dqg, dkd, dw, du, dat, dgl, dz, dng_n = vjp((dys, list(dsts)))
            for j in range(hp):
                cols = slice(j * hd, (j + 1) * hd)
                dqg_ref[rows, cols] = dqg[j]
                dkd_ref[rows, cols] = dkd[j]
                dw_ref[rows, cols] = dw[j]
                du_ref[rows, cols] = du[j]
                dat_ref[j, rows, :] = dat[j]
                dgl_ref[j, grow, :] = jnp.broadcast_to(dgl[j], (SUBLANES, hd))
                dz_ref[rows, cols] = dz[j]
            return tuple(dst), dng + dng_n

        _, dng = lax.fori_loop(0, nc, bwd_chunk,
                               (tuple(jnp.zeros((hd, hd), f32) for _ in range(hp)), jnp.zeros((1, hd), f32)))
        first = jnp.logical_and(pl.program_id(0) == 0, pl.program_id(1) == 0)

        @pl.when(first)
        def _():
            dng_ref[...] = dng

        @pl.when(jnp.logical_not(first))
        def _():
            dng_ref[...] += dng

    big = jax.ShapeDtypeStruct((t, C_WIDTH), f32)
    return pl.pallas_call(
        body, grid=(nb, C_HEADS // hp),
        in_specs=[wide] * 4 + [attn_sp, gl_sp, wide, ng_sp, wide],
        out_specs=[wide] * 4 + [attn_sp, gl_sp, wide, ng_sp],
        out_shape=[big] * 4 + [jax.ShapeDtypeStruct(attn.shape, f32), jax.ShapeDtypeStruct(gl.shape, f32), big,
                               jax.ShapeDtypeStruct((1, hd), f32)],
        scratch_shapes=[pltpu.VMEM((hp, nc, hd, hd), f32)],
        compiler_params=_params(("arbitrary", "arbitrary")), name=name,
    )(qg, kdec, w, u, attn, gl, z, ng, dy)


def _blockdiag_slabs(w):
    per = GROUP_W // B_BLOCK
    slabs = jnp.zeros((B_BLOCKS // per, GROUP_W, GROUP_W), w.dtype)
    for h in range(B_BLOCKS):
        o = (h % per) * B_BLOCK
        slabs = slabs.at[h // per, o:o + B_BLOCK, o:o + B_BLOCK].set(w[h])
    return slabs


def _slab_blocks(slabs):
    per = GROUP_W // B_BLOCK
    return jnp.stack([slabs[h // per, (h % per) * B_BLOCK:(h % per + 1) * B_BLOCK,
                            (h % per) * B_BLOCK:(h % per + 1) * B_BLOCK] for h in range(B_BLOCKS)])


def _mixer_ab_fwd(x1, W, g, b, nb, tag):
    w_in = W["ab_w_in"][0].astype(bf16)
    o1, o2 = A_WIDTH + 2 * A_KV_WIDTH, A_WIDTH + 2 * A_KV_WIDTH + B_WIDTH
    w_qkv, w_bx, w_bg = w_in[:, :o1], w_in[:, o1:o2], w_in[:, o2:]
    pqkv = mm_nn(x1, w_qkv, name=tag + "_in_qkv")
    pbx = mm_nn(x1, w_bx, name=tag + "_in_bx")
    pbg = mm_nn(x1, w_bg, name=tag + "_in_bg")
    ya = attn_fwd(pqkv, W["a_sinks"], nb, name=tag + "_attn_fwd")
    xc = conv_fwd(pbx, W["b_conv_w"][0], W["b_conv_b"], False, nb, name=tag + "_conv_fwd")
    wa_s, wx_s = _blockdiag_slabs(W["b_wa"][0]), _blockdiag_slabs(W["b_wx"][0])
    yb, hh = rglru_fwd(xc, pbg, wa_s, wx_s, W["b_ba"], W["b_bx"], W["b_lam"], nb, name=tag + "_rglru_fwd")
    w_out = W["ab_w_out"][0].astype(bf16)
    x2, z1 = proj_ln([ya, yb], [w_out[:A_WIDTH], w_out[A_WIDTH:]], x1, g, b, name=tag + "_out_ln")
    saved = (pqkv, pbx, pbg, ya, xc, yb, hh, wa_s, wx_s, w_qkv, w_bx, w_bg, w_out)
    return x2, z1, saved


def _mixer_ab_bwd(x1, dz1, dz1b, W, saved, nb, tag):
    pqkv, pbx, pbg, ya, xc, yb, hh, wa_s, wx_s, w_qkv, w_bx, w_bg, w_out = saved
    dya = mm_nn(dz1b, w_out[:A_WIDTH].T, name=tag + "_dya")
    dyb = mm_nn(dz1b, w_out[A_WIDTH:].T, name=tag + "_dyb")
    dwo = jnp.concatenate([mm_tn(ya, dz1b, name=tag + "_dwo_a"), mm_tn(yb, dz1b, name=tag + "_dwo_b")], 0)
    dpqkv, dsinks = attn_bwd(pqkv, W["a_sinks"], dya, nb, name=tag + "_attn_bwd")
    dxc, dpbg, dwa_s, dwx_s, dba, dbx, dlam = rglru_bwd(xc, pbg, hh, dyb, wa_s, wx_s, W["b_ba"], W["b_bx"],
                                                       W["b_lam"], nb, name=tag + "_rglru_bwd")
    dpbx, dconv_w, dconv_b = conv_bwd(pbx, W["b_conv_w"][0], W["b_conv_b"], dxc, False, nb, name=tag + "_conv_bwd")
    dw_in = jnp.concatenate([mm_tn(x1, dpqkv, name=tag + "_dwin_qkv"), mm_tn(x1, dpbx, name=tag + "_dwin_bx"),
                             mm_tn(x1, dpbg, name=tag + "_dwin_bg")], 1)
    dx1 = mm_nn(dpqkv, w_qkv.T, add=dz1, add_scale=DN_ALPHA, name=tag + "_dx_qkv")
    dx1 = mm_nn(dpbx, w_bx.T, add=dx1, name=tag + "_dx_bx")
    dx1 = mm_nn(dpbg, w_bg.T, add=dx1, name=tag + "_dx_bg")
    grads = {"ab_w_in": dw_in[None], "a_sinks": dsinks, "b_conv_w": dconv_w[None], "b_conv_b": dconv_b,
             "b_wa": _slab_blocks(dwa_s)[None], "b_ba": dba, "b_wx": _slab_blocks(dwx_s)[None], "b_bx": dbx,
             "b_lam": dlam, "ab_w_out": dwo[None]}
    return dx1, grads


def _mixer_c_fwd(x1, W, g, b, nb, tag):
    w_in = W["c_w_in"][0].astype(bf16)
    d = w_in.shape[0]
    o1, o2 = 3 * C_WIDTH, 4 * C_WIDTH
    w_qkv, w_z = w_in[:, :o1], w_in[:, o1:o2]
    w_ba = jnp.concatenate([w_in[:, o2:], jnp.zeros((d, LANES - 2 * C_HEADS), bf16)], 1)
    pqkv = mm_nn(x1, w_qkv, name=tag + "_in_qkv")
    pz = mm_nn(x1, w_z, name=tag + "_in_z")
    pba = mm_nn(x1, w_ba, name=tag + "_in_ba")
    zero_b = jnp.zeros((1, o1), f32)
    qkvc = conv_fwd(pqkv, W["c_conv_w"][0], zero_b, True, nb, name=tag + "_conv_fwd")
    prep = gdn_prep_fwd(qkvc, pba, W["c_a_log"], W["c_dt_bias"], nb, name=tag + "_prep_fwd")
    yc = gdn_rec_fwd(*prep, pz, W["c_norm_g"], nb, name=tag + "_rec_fwd")
    w_out = W["c_w_out"][0].astype(bf16)
    x2, z1 = proj_ln([yc], [w_out], x1, g, b, name=tag + "_out_ln")
    saved = (pqkv, pz, pba, qkvc, prep, yc, w_qkv, w_z, w_ba, w_out, zero_b)
    return x2, z1, saved


def _mixer_c_bwd(x1, dz1, dz1b, W, saved, nb, tag):
    pqkv, pz, pba, qkvc, prep, yc, w_qkv, w_z, w_ba, w_out, zero_b = saved
    dyc = mm_nn(dz1b, w_out.T, name=tag + "_dyc")
    dwo = mm_tn(yc, dz1b, name=tag + "_dwo")
    rec = gdn_rec_bwd(*prep, pz, W["c_norm_g"], dyc, nb, name=tag + "_rec_bwd")
    cts, dpz, dng = rec[:6], rec[6], rec[7]
    dq, dk, dv, dpba, dalog, ddtb = gdn_prep_bwd(qkvc, pba, W["c_a_log"], W["c_dt_bias"], cts, nb,
                                                 name=tag + "_prep_bwd")
    dqkvc = jnp.concatenate([dq, dk, dv], 1)
    dpqkv, dconv_w, _ = conv_bwd(pqkv, W["c_conv_w"][0], zero_b, dqkvc, True, nb, name=tag + "_conv_bwd")
    dw_in = jnp.concatenate([mm_tn(x1, dpqkv, name=tag + "_dwin_qkv"), mm_tn(x1, dpz, name=tag + "_dwin_z"),
                             mm_tn(x1, dpba, name=tag + "_dwin_ba")[:, :2 * C_HEADS]], 1)
    dx1 = mm_nn(dpqkv, w_qkv.T, add=dz1, add_scale=DN_ALPHA, name=tag + "_dx_qkv")
    dx1 = mm_nn(dpz, w_z.T, add=dx1, name=tag + "_dx_z")
    dx1 = mm_nn(dpba, w_ba.T, add=dx1, name=tag + "_dx_ba")
    grads = {"c_w_in": dw_in[None], "c_conv_w": dconv_w[None], "c_a_log": dalog[:, :C_HEADS],
             "c_dt_bias": ddtb[:, :C_HEADS], "c_norm_g": dng, "c_w_out": dwo[None]}
    return dx1, grads


def _local_step(x, p, target, W, F):
    nb, s, d = x.shape
    t = nb * s
    h = x.reshape(t, d)
    tape = []
    f1 = [F[k] for k in ("ffn1_wg", "ffn1_wu", "ffn1_wd")]
    f2 = [F[k] for k in ("ffn2_wg", "ffn2_wu", "ffn2_wd")]
    for i in range(DEPTH):
        tag = f"l{i}"
        lg = [W["ln_g"][i, k][None] for k in range(3)]
        lb = [W["ln_b"][i, k][None] for k in range(3)]
        x1, z0 = ffn_fwd(h, *f1, i, lg[0], lb[0], name=tag + "_ffn1_fwd")
        mixer = _mixer_ab_fwd if i % 2 == 0 else _mixer_c_fwd
        x2, z1, msaved = mixer(x1, W, lg[1], lb[1], nb, tag + "_mix")
        x3, z2 = ffn_fwd(x2, *f2, i, lg[2], lb[2], name=tag + "_ffn2_fwd")
        pi = p[i].reshape(t, -1)
        pw = (W["ple_wg"][i].astype(bf16), W["ple_bg"][i][None], W["ple_wp"][i].astype(bf16))
        x4 = ple_fwd(x3, pi, *pw, name=tag + "_ple_fwd")
        tape.append((h, z0, x1, msaved, z1, x2, z2, x3, pi, pw, lg))
        h = x4
    dh, sq = loss_head(h, target.reshape(t, d), name="loss_head")
    loss = 0.5 * jnp.sum(sq) / d
    per_layer = [None] * DEPTH
    grads = {}
    df1 = df2 = None
    for i in reversed(range(DEPTH)):
        tag = f"l{i}"
        h_in, z0, x1, msaved, z1, x2, z2, x3, pi, pw, lg = tape[i]
        dx3, dple_wg, dple_bg, dple_wp = ple_bwd(x3, pi, dh, pw[0], pw[0].T, pw[1], pw[2], name=tag + "_ple_bwd")
        dz2, dz2b, dg2, db2 = ln_bwd(z2, dx3, lg[2], name=tag + "_ln2_bwd")
        dgate, dup, *df2 = ffn_bwd_weights(x2.astype(bf16), dz2b, *f2, i, df2, name=tag + "_ffn2_bwd_w")
        dx2 = ffn_bwd_input(dgate, dup, f2[0], f2[1], i, dz2, name=tag + "_ffn2_bwd_x")
        dz1, dz1b, dg1, db1 = ln_bwd(z1, dx2, lg[1], name=tag + "_ln1_bwd")
        mixer_bwd = _mixer_ab_bwd if i % 2 == 0 else _mixer_c_bwd
        dx1, mgrads = mixer_bwd(x1, dz1, dz1b, W, msaved, nb, tag + "_mix")
        grads.update(mgrads)
        dz0, dz0b, dg0, db0 = ln_bwd(z0, dx1, lg[0], name=tag + "_ln0_bwd")
        dgate, dup, *df1 = ffn_bwd_weights(h_in.astype(bf16), dz0b, *f1, i, df1, name=tag + "_ffn1_bwd_w")
        dh = ffn_bwd_input(dgate, dup, f1[0], f1[1], i, dz0, name=tag + "_ffn1_bwd_x")
        per_layer[i] = {"ln_g": jnp.concatenate([dg0, dg1, dg2], 0), "ln_b": jnp.concatenate([db0, db1, db2], 0),
                        "ple_wg": dple_wg, "ple_bg": dple_bg[0], "ple_wp": dple_wp}
    for k in per_layer[0]:
        grads[k] = jnp.stack([per_layer[i][k] for i in range(DEPTH)])
    grads.update(zip(("ffn1_wg", "ffn1_wu", "ffn1_wd"), df1))
    grads.update(zip(("ffn2_wg", "ffn2_wu", "ffn2_wd"), df2))
    return loss, dh.reshape(nb, s, d), grads


WEIGHT_NAMES = ("ffn1_wg", "ffn1_wu", "ffn1_wd", "ffn2_wg", "ffn2_wu", "ffn2_wd", "ln_g", "ln_b", "ple_wg", "ple_bg",
                "ple_wp", "ab_w_in", "a_sinks", "b_conv_w", "b_conv_b", "b_wa", "b_ba", "b_wx", "b_bx", "b_lam",
                "ab_w_out", "c_w_in", "c_conv_w", "c_a_log", "c_dt_bias", "c_norm_g", "c_w_out")
NATIVE_NAMES = WEIGHT_NAMES[:6]
PACKED_NAMES = WEIGHT_NAMES[6:]
SHARD_AXIS = {"ffn1_wg": 2, "ffn1_wu": 2, "ffn1_wd": 1, "ffn2_wg": 2, "ffn2_wu": 2, "ffn2_wd": 1, "ln_g": 2, "ln_b": 2,
              "ple_wg": 1, "ple_wp": 2, "ab_w_in": 2, "b_conv_w": 2, "ab_w_out": 1, "c_w_in": 2, "c_conv_w": 2,
              "c_w_out": 1}
N_CHIPS = 4
PACK_COLS = 512
MESH = pl.DeviceIdType.MESH
ANY = pl.BlockSpec(memory_space=pl.ANY)


def _pack_rows(n):
    unit = 2 * SUBLANES * PACK_COLS
    return -(-n // unit) * 2 * SUBLANES


def _pack(pieces, lead=()):
    k = len(lead)
    flat = jnp.concatenate([a.reshape(lead + (-1,)) for a in pieces], axis=k)
    rows = _pack_rows(flat.shape[k])
    flat = jnp.pad(flat, [(0, 0)] * k + [(0, rows * PACK_COLS - flat.shape[k])])
    return flat.reshape(lead + (rows, PACK_COLS))


def _unpack(pack, shapes, lead=()):
    k = len(lead)
    flat = pack.reshape(lead + (-1,))
    out, o = [], 0
    for shp in shapes:
        n = 1
        for dim in shp:
            n *= dim
        out.append(lax.slice_in_dim(flat, o, o + n, axis=k).reshape(lead + tuple(shp)))
        o += n
    return out


def _mesh_position():
    x, y, c = lax.axis_index("x"), lax.axis_index("y"), lax.axis_index("c")
    chips = [(1 - x, y), (x, 1 - y), (1 - x, 1 - y)]
    return x, y, c, chips


def _remote(src, dst, send_sems, recv_sems, k, to):
    return pltpu.make_async_remote_copy(src_ref=src, dst_ref=dst, send_sem=send_sems.at[k], recv_sem=recv_sems.at[k],
                                        device_id=to, device_id_type=MESH)


def _sems(n):
    return pltpu.SemaphoreType.DMA((n,))


def place_slot(parts, slot, n_slots, dtype, from_slot, *, name):
    n = len(parts)
    r, cols = parts[0].shape[-2:]
    tr = _tile(r, 256, SUBLANES * (4 // jnp.dtype(dtype).itemsize))

    def body(s_ref, *refs):
        for a in range(n):
            refs[n + a][...] = refs[a][...].astype(dtype)

    dst = pl.BlockSpec((None, tr, cols), lambda i, s_ref: (s_ref[0], i, 0))
    src = dst if from_slot else pl.BlockSpec((tr, cols), lambda i, s_ref: (i, 0))
    return pl.pallas_call(
        body,
        grid_spec=pltpu.PrefetchScalarGridSpec(num_scalar_prefetch=1, grid=(r // tr,), in_specs=[src] * n,
                                               out_specs=[dst] * n),
        out_shape=[jax.ShapeDtypeStruct((n_slots, r, cols), dtype)] * n,
        compiler_params=_params(("parallel",)), name=name,
    )(slot, *parts)


def gather_shards(bufs, *, name):
    n = len(bufs)

    def body(*refs):
        out_refs = refs[n:2 * n]
        send_sems, recv_sems = refs[2 * n:]
        x, y, c, chips = _mesh_position()
        me = 2 * x + y
        sibling = (x, y, 1 - c)
        waits = []
        for j, (cx, cy) in enumerate(chips):
            for a in range(n):
                own = out_refs[a].at[me, c]
                cp = _remote(own, own, send_sems, recv_sems, 6 * a + j, (cx, cy, c))
                cp.start()
                waits.append(cp.wait_send)
        for j, (cx, cy) in enumerate(chips):
            for a in range(n):
                got = out_refs[a].at[2 * cx + cy, c]
                _remote(got, got, send_sems, recv_sems, 6 * a + j, (cx, cy, c)).wait_recv()
                fw = _remote(got, got, send_sems, recv_sems, 6 * a + 3 + j, sibling)
                fw.start()
                waits.append(fw.wait_send)
        for j, (cx, cy) in enumerate(chips):
            for a in range(n):
                got = out_refs[a].at[2 * cx + cy, 1 - c]
                _remote(got, got, send_sems, recv_sems, 6 * a + 3 + j, sibling).wait_recv()
        for wait in waits:
            wait()

    return pl.pallas_call(
        body, out_shape=[jax.ShapeDtypeStruct(b.shape, b.dtype) for b in bufs],
        in_specs=[ANY] * n, out_specs=[ANY] * n, scratch_shapes=[_sems(6 * n), _sems(6 * n)],
        input_output_aliases={a: a for a in range(n)}, name=name,
    )(*bufs)


def sibling_exchange(gs, *, name):
    n = len(gs)

    def body(*refs):
        g_refs, out_refs = refs[:n], refs[n:2 * n]
        send_sems, recv_sems = refs[2 * n:]
        x, y, c, _ = _mesh_position()
        cps = [_remote(g_refs[a].at[:, 1 - c], out_refs[a], send_sems, recv_sems, a, (x, y, 1 - c)) for a in range(n)]
        for cp in cps:
            cp.start()
        for cp in cps:
            cp.wait()

    return pl.pallas_call(
        body, out_shape=[jax.ShapeDtypeStruct(g.shape[:1] + g.shape[2:], g.dtype) for g in gs],
        in_specs=[ANY] * n, out_specs=[ANY] * n, scratch_shapes=[_sems(n), _sems(n)], name=name,
    )(*gs)


def add_own_half(gs, others, c_idx, *, name):
    n = len(gs)
    ns, _, r, cols = gs[0].shape
    tr = _tile(r, 256, SUBLANES)

    def body(c_ref, *refs):
        for a in range(n):
            refs[2 * n + a][...] = refs[a][...] + refs[n + a][...]

    own = pl.BlockSpec((None, None, tr, cols), lambda s, i, c_ref: (s, c_ref[0], i, 0))
    oth = pl.BlockSpec((None, tr, cols), lambda s, i, c_ref: (s, i, 0))
    return pl.pallas_call(
        body,
        grid_spec=pltpu.PrefetchScalarGridSpec(num_scalar_prefetch=1, grid=(ns, r // tr),
                                               in_specs=[own] * n + [oth] * n, out_specs=[oth] * n),
        out_shape=[jax.ShapeDtypeStruct((ns, r, cols), f32)] * n,
        compiler_params=_params(("parallel", "parallel")), name=name,
    )(c_idx, *gs, *others)


def chip_exchange(ps, qs, *, name):
    n = len(ps)

    def body(*refs):
        p_refs, q_refs = refs[:n], refs[2 * n:3 * n]
        send_sems, recv_sems = refs[3 * n:]
        x, y, c, chips = _mesh_position()
        me = 2 * x + y
        waits = []
        for j, (cx, cy) in enumerate(chips):
            for a in range(n):
                cp = _remote(p_refs[a].at[2 * cx + cy], q_refs[a].at[me], send_sems, recv_sems, 3 * a + j, (cx, cy, c))
                cp.start()
                waits.append(cp.wait_send)
        for j, (cx, cy) in enumerate(chips):
            for a in range(n):
                got = q_refs[a].at[2 * cx + cy]
                _remote(got, got, send_sems, recv_sems, 3 * a + j, (cx, cy, c)).wait_recv()
        for wait in waits:
            wait()

    return pl.pallas_call(
        body, out_shape=[jax.ShapeDtypeStruct(q_.shape, q_.dtype) for q_ in qs], in_specs=[ANY] * (2 * n),
        out_specs=[ANY] * n, scratch_shapes=[_sems(3 * n), _sems(3 * n)],
        input_output_aliases={n + a: a for a in range(n)}, name=name,
    )(*ps, *qs)


def sum_slots(qs, *, name):
    n = len(qs)
    ns, r, cols = qs[0].shape
    tr = _tile(r, 128, SUBLANES)

    def body(*refs):
        for a in range(n):
            q_ref = refs[a]
            acc = q_ref[0] + q_ref[1]
            for i in range(2, ns):
                acc = acc + q_ref[i]
            refs[n + a][...] = acc

    return pl.pallas_call(
        body, grid=(r // tr,), in_specs=[pl.BlockSpec((ns, tr, cols), lambda i: (0, i, 0))] * n,
        out_specs=[pl.BlockSpec((tr, cols), lambda i: (i, 0))] * n,
        out_shape=[jax.ShapeDtypeStruct((r, cols), f32)] * n,
        compiler_params=_params(("parallel",)), name=name,
    )(*qs)


def sibling_share(bufs, *, name):
    n = len(bufs)

    def body(*refs):
        out_refs = refs[n:2 * n]
        send_sems, recv_sems = refs[2 * n:]
        x, y, c, _ = _mesh_position()
        sibling = (x, y, 1 - c)
        cps = []
        for a in range(n):
            own = out_refs[a].at[c]
            cp = _remote(own, own, send_sems, recv_sems, a, sibling)
            cp.start()
            cps.append(cp)
        for a in range(n):
            theirs = out_refs[a].at[1 - c]
            _remote(theirs, theirs, send_sems, recv_sems, a, sibling).wait_recv()
        for cp in cps:
            cp.wait_send()

    return pl.pallas_call(
        body, out_shape=[jax.ShapeDtypeStruct(b.shape, b.dtype) for b in bufs], in_specs=[ANY] * n,
        out_specs=[ANY] * n, scratch_shapes=[_sems(n), _sems(n)],
        input_output_aliases={a: a for a in range(n)}, name=name,
    )(*bufs)


def adamw(ws, gs, ms, vs, *, name):
    n = len(ws)
    r, cols = ws[0].shape
    tr = _tile(r, 128, SUBLANES)

    def body(*refs):
        for a in range(n):
            w_ref, g_ref, m_ref, v_ref = (refs[k * n + a] for k in range(4))
            d_ref, m2_ref, v2_ref = (refs[(4 + k) * n + a] for k in range(3))
            g_ = g_ref[...]
            m2 = ADAM_B1 * m_ref[...] + (1.0 - ADAM_B1) * g_
            v2 = ADAM_B2 * v_ref[...] + (1.0 - ADAM_B2) * (g_ * g_)
            m_hat = m2 / (1.0 - ADAM_B1 ** ADAM_STEP)
            v_hat = v2 / (1.0 - ADAM_B2 ** ADAM_STEP)
            d_ref[...] = -ADAM_LR * (m_hat / (jnp.sqrt(v_hat) + ADAM_EPS) + ADAM_WD * w_ref[...])
            m2_ref[...] = m2
            v2_ref[...] = v2

    row = pl.BlockSpec((tr, cols), lambda i: (i, 0))
    out = pl.pallas_call(
        body, grid=(r // tr,), in_specs=[row] * (4 * n), out_specs=[row] * (3 * n),
        out_shape=[jax.ShapeDtypeStruct((r, cols), f32)] * (3 * n),
        compiler_params=_params(("parallel",)), name=name,
    )(*ws, *gs, *ms, *vs)
    return out[:n], out[n:2 * n], out[2 * n:]


def _full_weights(gathered, local, shapes):
    pieces = _unpack(gathered, shapes, lead=(N_CHIPS,))
    full = {}
    for name, loc, pc in zip(PACKED_NAMES, local, pieces):
        ax = SHARD_AXIS.get(name)
        full[name] = loc if ax is None else jnp.concatenate([pc[s] for s in range(N_CHIPS)], axis=ax)
    return full


def _grad_pack(grads, shapes):
    pieces = []
    for name, shp in zip(PACKED_NAMES, shapes):
        g = grads[name]
        ax = SHARD_AXIS.get(name)
        if ax is None:
            pieces.append(jnp.broadcast_to(g.reshape(shp)[None], (N_CHIPS,) + tuple(shp)))
        else:
            pieces.append(jnp.stack(jnp.split(g, N_CHIPS, axis=ax)))
    return _pack(pieces, lead=(N_CHIPS,))


def _by_shape(arrays):
    groups = {}
    for i, a in enumerate(arrays):
        groups.setdefault(a.shape, []).append(i)
    return list(groups.values())


def _grouped(fn, lists, n_out, tag):
    outs = [[None] * len(lists[0]) for _ in range(n_out)]
    for gi, idx in enumerate(_by_shape(lists[0])):
        res = fn(*[[lst[i] for i in idx] for lst in lists], name=f"{tag}_{gi}")
        res = res if n_out > 1 else (res,)
        for k in range(n_out):
            for i, r in zip(idx, res[k]):
                outs[k][i] = r
    return outs if n_out > 1 else outs[0]


def _train_step(x, p, loss_target, weights, m, v):
    packed_w = [weights[k] for k in PACKED_NAMES]
    shapes = [w.shape for w in packed_w]
    halves = lambda a: a.reshape((2, a.shape[0] // 2) + a.shape[1:])
    local = [weights[k] for k in NATIVE_NAMES] + [halves(_pack(packed_w))]
    local_m = [m[k] for k in NATIVE_NAMES] + [halves(_pack([m[k] for k in PACKED_NAMES]))]
    local_v = [v[k] for k in NATIVE_NAMES] + [halves(_pack([v[k] for k in PACKED_NAMES]))]
    flat = lambda lst: [a.reshape((-1, a.shape[-1])) for a in lst]
    c_idx = lax.axis_index("c").astype(jnp.int32).reshape(1)
    chip_idx = (2 * lax.axis_index("x") + lax.axis_index("y")).astype(jnp.int32).reshape(1)

    def placed(arrays, slot, n_slots, dtype, from_slot, tag):
        return _grouped(lambda a, name: place_slot(a, slot, n_slots, dtype, from_slot, name=name), [arrays], 1, tag)

    bufs = placed(flat(local[:-1]), chip_idx, N_CHIPS, bf16, False, "place_ffn_weights")
    bufs += placed(flat(local[-1:]), chip_idx, N_CHIPS, f32, False, "place_packed_weights")
    bufs = [b.reshape((N_CHIPS,) + a.shape) for b, a in zip(bufs, local)]
    gathered = gather_shards(bufs, name="comm_gather_weights")
    full = _full_weights(gathered[-1], packed_w, shapes)
    loss, grad_x, grads = _local_step(x, p, loss_target, full, dict(zip(NATIVE_NAMES, gathered[:-1])))
    gpack = _grad_pack(grads, shapes)
    gs = [grads[k] for k in NATIVE_NAMES] + [gpack.reshape((N_CHIPS,) + local[-1].shape)]
    others = sibling_exchange(gs, name="comm_grad_sibling")
    chip_sums = _grouped(lambda a, b, name: add_own_half(a, b, c_idx, name=name), [gs, others], 1, "grad_add_sibling")
    slots = chip_exchange(chip_sums, placed(chip_sums, chip_idx, N_CHIPS, f32, True, "place_own_partial"),
                          name="comm_grad_chips")
    mine = _grouped(sum_slots, [slots], 1, "grad_sum_chips")
    gsum = sibling_share(placed(mine, c_idx, 2, f32, False, "place_own_half"), name="comm_grad_share")
    delta, m2, v2 = _grouped(adamw, [flat(local), flat(gsum), flat(local_m), flat(local_v)], 3, "adamw")
    loss = lax.psum(loss, ("x", "y", "c"))
    outs = []
    for res in (gsum, delta, m2, v2):
        by_name = {k: a.reshape(weights[k].shape) for k, a in zip(NATIVE_NAMES, res[:-1])}
        by_name.update(zip(PACKED_NAMES, _unpack(res[-1], shapes)))
        outs += [by_name[k] for k in WEIGHT_NAMES]
    return (loss, grad_x, *outs)


def kernel(x, p, ffn1_wg, ffn1_wu, ffn1_wd, ffn2_wg, ffn2_wu, ffn2_wd, ln_g, ln_b, ple_wg, ple_bg, ple_wp, ab_w_in, a_sinks, b_conv_w, b_conv_b, b_wa, b_ba, b_wx, b_bx, b_lam, ab_w_out, c_w_in, c_conv_w, c_a_log, c_dt_bias, c_norm_g, c_w_out, loss_target, m_ffn1_wg, m_ffn1_wu, m_ffn1_wd, m_ffn2_wg, m_ffn2_wu, m_ffn2_wd, m_ln_g, m_ln_b, m_ple_wg, m_ple_bg, m_ple_wp, m_ab_w_in, m_a_sinks, m_b_conv_w, m_b_conv_b, m_b_wa, m_b_ba, m_b_wx, m_b_bx, m_b_lam, m_ab_w_out, m_c_w_in, m_c_conv_w, m_c_a_log, m_c_dt_bias, m_c_norm_g, m_c_w_out, v_ffn1_wg, v_ffn1_wu, v_ffn1_wd, v_ffn2_wg, v_ffn2_wu, v_ffn2_wd, v_ln_g, v_ln_b, v_ple_wg, v_ple_bg, v_ple_wp, v_ab_w_in, v_a_sinks, v_b_conv_w, v_b_conv_b, v_b_wa, v_b_ba, v_b_wx, v_b_bx, v_b_lam, v_ab_w_out, v_c_w_in, v_c_conv_w, v_c_a_log, v_c_dt_bias, v_c_norm_g, v_c_w_out):
    weights = [ffn1_wg, ffn1_wu, ffn1_wd, ffn2_wg, ffn2_wu, ffn2_wd, ln_g, ln_b, ple_wg, ple_bg, ple_wp, ab_w_in, a_sinks,
               b_conv_w, b_conv_b, b_wa, b_ba, b_wx, b_bx, b_lam, ab_w_out, c_w_in, c_conv_w, c_a_log, c_dt_bias, c_norm_g,
               c_w_out]
    m = [m_ffn1_wg, m_ffn1_wu, m_ffn1_wd, m_ffn2_wg, m_ffn2_wu, m_ffn2_wd, m_ln_g, m_ln_b, m_ple_wg, m_ple_bg, m_ple_wp,
         m_ab_w_in, m_a_sinks, m_b_conv_w, m_b_conv_b, m_b_wa, m_b_ba, m_b_wx, m_b_bx, m_b_lam, m_ab_w_out, m_c_w_in,
         m_c_conv_w, m_c_a_log, m_c_dt_bias, m_c_norm_g, m_c_w_out]
    v = [v_ffn1_wg, v_ffn1_wu, v_ffn1_wd, v_ffn2_wg, v_ffn2_wu, v_ffn2_wd, v_ln_g, v_ln_b, v_ple_wg, v_ple_bg, v_ple_wp,
         v_ab_w_in, v_a_sinks, v_b_conv_w, v_b_conv_b, v_b_wa, v_b_ba, v_b_wx, v_b_bx, v_b_lam, v_ab_w_out, v_c_w_in,
         v_c_conv_w, v_c_a_log, v_c_dt_bias, v_c_norm_g, v_c_w_out]
    return _train_step(x, p, loss_target, dict(zip(WEIGHT_NAMES, weights)), dict(zip(WEIGHT_NAMES, m)),
                       dict(zip(WEIGHT_NAMES, v)))
```

```python
import functools

import jax
import jax.numpy as jnp
from jax import lax
from jax.experimental import pallas as pl
from jax.experimental.pallas import tpu as pltpu

f32 = jnp.float32
bf16 = jnp.bfloat16

DEPTH = 2
CHUNK = 64
A_HEADS, A_KV_HEADS, A_GROUP, A_HEAD_DIM = 8, 2, 4, 64
A_WIDTH, A_KV_WIDTH, A_WINDOW = 512, 128, 128
B_WIDTH, B_BLOCKS, B_BLOCK, B_CONV = 512, 8, 64, 4
RG_C = 8.0
C_HEADS, C_HEAD_DIM, C_WIDTH, C_CONV = 8, 128, 1024, 4
DN_ALPHA = (2.0 * DEPTH) ** 0.25
LN_EPS = 1e-5
NORM_EPS = 1e-6
NEG = -1e30
ADAM_LR, ADAM_B1, ADAM_B2, ADAM_EPS, ADAM_WD, ADAM_STEP = 0.001, 0.9, 0.999, 1e-08, 0.01, 10

VMEM_LIMIT_BYTES = 56 * 1024 * 1024
LANES = 128
SUBLANES = 8
GROUP_W = 128
PREP_FWD_UNROLL = 4
PREP_BWD_UNROLL = 4
C_HEADS_PER_STEP_FWD = 4
C_HEADS_PER_STEP_BWD = 2

NN = ((1,), (0,))
NT = ((1,), (1,))
TN = ((0,), (0,))


def _params(sem):
    return pltpu.CompilerParams(dimension_semantics=sem, vmem_limit_bytes=VMEM_LIMIT_BYTES)


def _tile(n, cap, mult):
    best = None
    t = mult
    while t <= min(n, cap):
        if n % t == 0:
            best = t
        t += mult
    return best if best is not None else n


def _bdot(a, b, dims):
    return lax.dot_general(a.astype(bf16), b.astype(bf16), (dims, ((), ())), preferred_element_type=f32)


def _running_sum(x, reverse):
    s = x.shape[0]
    t = lax.broadcasted_iota(jnp.int32, x.shape, 0)
    d = 1
    while d < s:
        if reverse:
            x = x + jnp.where(t < s - d, pltpu.roll(x, s - d, 0), 0.0)
        else:
            x = x + jnp.where(t >= d, pltpu.roll(x, d, 0), 0.0)
        d *= 2
    return x


@jax.custom_vjp
def _cumsum0(x):
    return _running_sum(x, False)


def _cumsum0_fwd(x):
    return _running_sum(x, False), None


def _cumsum0_bwd(_, g):
    return (_running_sum(g, True),)


_cumsum0.defvjp(_cumsum0_fwd, _cumsum0_bwd)


@jax.custom_vjp
def _bnn(a, b):
    return _bdot(a, b, NN)


def _bnn_fwd(a, b):
    return _bdot(a, b, NN), (a, b)


def _bnn_bwd(res, g):
    a, b = res
    return _bdot(g, b, NT), _bdot(a, g, TN)


_bnn.defvjp(_bnn_fwd, _bnn_bwd)


@jax.custom_vjp
def _bnt(a, b):
    return _bdot(a, b, NT)


def _bnt_fwd(a, b):
    return _bdot(a, b, NT), (a, b)


def _bnt_bwd(res, g):
    a, b = res
    return _bdot(g, b, NN), _bdot(g, a, TN)


_bnt.defvjp(_bnt_fwd, _bnt_bwd)


@jax.custom_vjp
def _btn(a, b):
    return _bdot(a, b, TN)


def _btn_fwd(a, b):
    return _bdot(a, b, TN), (a, b)


def _btn_bwd(res, g):
    a, b = res
    return _bdot(b, g, NT), _bdot(a, g, NN)


_btn.defvjp(_btn_fwd, _btn_bwd)

RAW_DOTS = (lambda a, b: _bdot(a, b, NN), lambda a, b: _bdot(a, b, NT), lambda a, b: _bdot(a, b, TN),
            lambda x: _running_sum(x, False))
VJP_DOTS = (_bnn, _bnt, _btn, _cumsum0)


def _layer_norm(z, g, b):
    mu = jnp.mean(z, -1, keepdims=True)
    d = z - mu
    var = jnp.mean(d * d, -1, keepdims=True)
    return d * lax.rsqrt(var + LN_EPS) * g + b


def _silu(x):
    return x * jax.nn.sigmoid(x)


def mm_nn(a, w, add=None, add_scale=1.0, *, name):
    m, k = a.shape
    n = w.shape[1]
    tm = _tile(m, 512, SUBLANES)
    tn = _tile(n, 1024, LANES)

    def body(*refs):
        if add is None:
            a_ref, w_ref, o_ref = refs
            o_ref[...] = _bdot(a_ref[...], w_ref[...], NN)
        else:
            a_ref, w_ref, add_ref, o_ref = refs
            o_ref[...] = _bdot(a_ref[...], w_ref[...], NN) + add_scale * add_ref[...]

    in_specs = [pl.BlockSpec((tm, k), lambda i, j: (i, 0)), pl.BlockSpec((k, tn), lambda i, j: (0, j))]
    args = [a, w]
    if add is not None:
        in_specs.append(pl.BlockSpec((tm, tn), lambda i, j: (i, j)))
        args.append(add)
    return pl.pallas_call(
        body, grid=(m // tm, n // tn), in_specs=in_specs,
        out_specs=pl.BlockSpec((tm, tn), lambda i, j: (i, j)),
        out_shape=jax.ShapeDtypeStruct((m, n), f32),
        compiler_params=_params(("parallel", "parallel")), name=name,
    )(*args)


def mm_tn(a, b, *, name):
    m, k = a.shape
    n = b.shape[1]
    tm = _tile(m, 512, SUBLANES)
    tn = _tile(n, 512, LANES)

    def body(a_ref, b_ref, o_ref):
        part = _bdot(a_ref[...], b_ref[...], TN)

        @pl.when(pl.program_id(1) == 0)
        def _():
            o_ref[...] = part

        @pl.when(pl.program_id(1) > 0)
        def _():
            o_ref[...] += part

    return pl.pallas_call(
        body, grid=(n // tn, m // tm),
        in_specs=[pl.BlockSpec((tm, k), lambda j, i: (i, 0)), pl.BlockSpec((tm, tn), lambda j, i: (i, j))],
        out_specs=pl.BlockSpec((k, tn), lambda j, i: (0, j)),
        out_shape=jax.ShapeDtypeStruct((k, n), f32),
        compiler_params=_params(("parallel", "arbitrary")), name=name,
    )(a, b)


def proj_ln(a_list, w_list, xres, g, b, *, name):
    t, d = xres.shape
    tm = _tile(t, 256, SUBLANES)
    na = len(a_list)

    def body(*refs):
        a_refs, w_refs = refs[:na], refs[na:2 * na]
        x_ref, g_ref, b_ref, y_ref, z_ref = refs[2 * na:]
        z = DN_ALPHA * x_ref[...]
        for a_ref, w_ref in zip(a_refs, w_refs):
            z = z + _bdot(a_ref[...], w_ref[...], NN)
        z_ref[...] = z
        y_ref[...] = _layer_norm(z, g_ref[...], b_ref[...])

    in_specs = [pl.BlockSpec((tm, a.shape[1]), lambda i: (i, 0)) for a in a_list]
    in_specs += [pl.BlockSpec(w.shape, lambda i: (0, 0)) for w in w_list]
    in_specs += [pl.BlockSpec((tm, d), lambda i: (i, 0)), pl.BlockSpec((1, d), lambda i: (0, 0)),
                 pl.BlockSpec((1, d), lambda i: (0, 0))]
    return pl.pallas_call(
        body, grid=(t // tm,), in_specs=in_specs,
        out_specs=[pl.BlockSpec((tm, d), lambda i: (i, 0))] * 2,
        out_shape=[jax.ShapeDtypeStruct((t, d), f32)] * 2,
        compiler_params=_params(("parallel",)), name=name,
    )(*a_list, *w_list, xres, g, b)


def ln_bwd(z, dy, g, *, name):
    t, d = z.shape
    tm = _tile(t, 512, SUBLANES)

    def body(z_ref, dy_ref, g_ref, dz_ref, dzb_ref, dg_ref, db_ref):
        zz = z_ref[...]
        dy_ = dy_ref[...]
        mu = jnp.mean(zz, -1, keepdims=True)
        dd = zz - mu
        var = jnp.mean(dd * dd, -1, keepdims=True)
        rstd = lax.rsqrt(var + LN_EPS)
        xhat = dd * rstd
        dxh = dy_ * g_ref[...]
        dz = rstd * (dxh - jnp.mean(dxh, -1, keepdims=True) - xhat * jnp.mean(dxh * xhat, -1, keepdims=True))
        dz_ref[...] = dz
        dzb_ref[...] = dz.astype(bf16)
        pg = jnp.sum(dy_ * xhat, 0, keepdims=True)
        pb = jnp.sum(dy_, 0, keepdims=True)

        @pl.when(pl.program_id(0) == 0)
        def _():
            dg_ref[...] = pg
            db_ref[...] = pb

        @pl.when(pl.program_id(0) > 0)
        def _():
            dg_ref[...] += pg
            db_ref[...] += pb

    row = pl.BlockSpec((tm, d), lambda i: (i, 0))
    vec = pl.BlockSpec((1, d), lambda i: (0, 0))
    return pl.pallas_call(
        body, grid=(t // tm,), in_specs=[row, row, vec], out_specs=[row, row, vec, vec],
        out_shape=[jax.ShapeDtypeStruct((t, d), f32), jax.ShapeDtypeStruct((t, d), bf16),
                   jax.ShapeDtypeStruct((1, d), f32), jax.ShapeDtypeStruct((1, d), f32)],
        compiler_params=_params(("arbitrary",)), name=name,
    )(z, dy, g)


def loss_head(y, target, *, name):
    t, d = y.shape
    tm = _tile(t, 512, SUBLANES)

    def body(y_ref, t_ref, dy_ref, sq_ref):
        e = y_ref[...] - t_ref[...]
        dy_ref[...] = e * (1.0 / d)
        part = jnp.sum(e * e, 0, keepdims=True)

        @pl.when(pl.program_id(0) == 0)
        def _():
            sq_ref[...] = part

        @pl.when(pl.program_id(0) > 0)
        def _():
            sq_ref[...] += part

    row = pl.BlockSpec((tm, d), lambda i: (i, 0))
    vec = pl.BlockSpec((1, d), lambda i: (0, 0))
    return pl.pallas_call(
        body, grid=(t // tm,), in_specs=[row, row], out_specs=[row, vec],
        out_shape=[jax.ShapeDtypeStruct((t, d), f32), jax.ShapeDtypeStruct((1, d), f32)],
        compiler_params=_params(("arbitrary",)), name=name,
    )(y, target)


FFN_COL_BLOCK = 256


def _lane_blocks(n):
    return [slice(s, min(s + FFN_COL_BLOCK, n)) for s in range(0, n, FFN_COL_BLOCK)]


def ffn_fwd(x, wg, wu, wd, layer, g, b, *, name):
    t, d = x.shape
    nf, _, _, tf = wg.shape
    tm = _tile(t, 512, SUBLANES)

    def body(x_ref, wg_ref, wu_ref, wd_ref, g_ref, b_ref, y_ref, z_ref, acc_ref):
        f = pl.program_id(1)
        xb = x_ref[...].astype(bf16)
        part, pending = None, None
        for cols in _lane_blocks(tf):
            gate_up = (_bdot(xb, wg_ref[:, cols], NN), _bdot(xb, wu_ref[:, cols], NN), cols)
            if pending is not None:
                down = _bdot(_silu(pending[0]) * pending[1], wd_ref[pending[2], :], NN)
                part = down if part is None else part + down
            pending = gate_up
        down = _bdot(_silu(pending[0]) * pending[1], wd_ref[pending[2], :], NN)
        part = down if part is None else part + down

        @pl.when(f == 0)
        def _():
            acc_ref[...] = part

        @pl.when(f > 0)
        def _():
            acc_ref[...] += part

        @pl.when(f == nf - 1)
        def _():
            z = DN_ALPHA * x_ref[...] + 0.5 * acc_ref[...]
            z_ref[...] = z
            y_ref[...] = _layer_norm(z, g_ref[...], b_ref[...])

    row = pl.BlockSpec((tm, d), lambda i, j: (i, 0))
    vec = pl.BlockSpec((1, d), lambda i, j: (0, 0))
    wcol = pl.BlockSpec((None, None, d, tf), lambda i, j: (j, layer, 0, 0))
    wrow = pl.BlockSpec((None, None, tf, d), lambda i, j: (j, layer, 0, 0))
    return pl.pallas_call(
        body, grid=(t // tm, nf),
        in_specs=[row, wcol, wcol, wrow, vec, vec],
        out_specs=[row, row],
        out_shape=[jax.ShapeDtypeStruct((t, d), f32)] * 2,
        scratch_shapes=[pltpu.VMEM((tm, d), f32)],
        compiler_params=_params(("parallel", "arbitrary")), name=name,
    )(x, wg, wu, wd, g, b)


def ffn_bwd_weights(xb, dzb, wg, wu, wd, layer, acc, *, name):
    t, d = xb.shape
    nf, nl, _, tf = wg.shape
    tm = _tile(t, 512, SUBLANES)

    def body(x_ref, dz_ref, wg_ref, wu_ref, wd_ref, *rest):
        dgate_ref, dup_ref, dwg_ref, dwu_ref, dwd_ref = rest[-5:]
        x = x_ref[...]
        dzh = dz_ref[...] * 0.5

        def first_half(cols):
            return _bdot(x, wg_ref[:, cols], NN), _bdot(x, wu_ref[:, cols], NN), _bdot(dzh, wd_ref[cols, :], NT), cols

        def second_half(gate, up, dh, cols):
            sg = jax.nn.sigmoid(gate)
            s = gate * sg
            dup = (dh * s).astype(bf16)
            dgate = (dh * up * (sg * (1.0 + gate * (1.0 - sg)))).astype(bf16)
            dgate_ref[:, cols] = dgate
            dup_ref[:, cols] = dup
            return _bdot(x, dgate, TN), _bdot(x, dup, TN), _bdot(s * up, dzh, TN), cols

        parts, pending = [], None
        for cols in _lane_blocks(tf):
            nxt = first_half(cols)
            if pending is not None:
                parts.append(second_half(*pending))
            pending = nxt
        parts.append(second_half(*pending))

        @pl.when(pl.program_id(1) == 0)
        def _():
            for pwg, pwu, pwd, cols in parts:
                dwg_ref[:, cols] = pwg
                dwu_ref[:, cols] = pwu
                dwd_ref[cols, :] = pwd

        @pl.when(pl.program_id(1) > 0)
        def _():
            for pwg, pwu, pwd, cols in parts:
                dwg_ref[:, cols] += pwg
                dwu_ref[:, cols] += pwu
                dwd_ref[cols, :] += pwd

    row = pl.BlockSpec((tm, d), lambda j, i: (i, 0))
    wcol = pl.BlockSpec((None, None, d, tf), lambda j, i: (j, layer, 0, 0))
    wrow = pl.BlockSpec((None, None, tf, d), lambda j, i: (j, layer, 0, 0))
    act = pl.BlockSpec((None, tm, tf), lambda j, i: (j, i, 0))
    in_specs = [row, row, wcol, wcol, wrow]
    args = [xb, dzb, wg, wu, wd]
    aliases = {}
    if acc is not None:
        in_specs += [pl.BlockSpec(memory_space=pl.ANY)] * 3
        args += list(acc)
        aliases = {5: 2, 6: 3, 7: 4}
    return pl.pallas_call(
        body, grid=(nf, t // tm), in_specs=in_specs, out_specs=[act, act, wcol, wcol, wrow],
        out_shape=[jax.ShapeDtypeStruct((nf, t, tf), bf16), jax.ShapeDtypeStruct((nf, t, tf), bf16),
                   jax.ShapeDtypeStruct((nf, nl, d, tf), f32), jax.ShapeDtypeStruct((nf, nl, d, tf), f32),
                   jax.ShapeDtypeStruct((nf, nl, tf, d), f32)],
        input_output_aliases=aliases,
        compiler_params=_params(("parallel", "arbitrary")), name=name,
    )(*args)


def ffn_bwd_input(dgate, dup, wg, wu, layer, dz, *, name):
    nf, t, tf = dgate.shape
    d = wg.shape[2]
    tm = _tile(t, 256, SUBLANES)

    def body(dg_ref, du_ref, wg_ref, wu_ref, dz_ref, dx_ref):
        acc = DN_ALPHA * dz_ref[...]
        for j in range(nf):
            acc = acc + _bdot(dg_ref[j], wg_ref[j], NT) + _bdot(du_ref[j], wu_ref[j], NT)
        dx_ref[...] = acc

    act = pl.BlockSpec((nf, tm, tf), lambda i: (0, i, 0))
    wsp = pl.BlockSpec((nf, None, d, tf), lambda i: (0, layer, 0, 0))
    row = pl.BlockSpec((tm, d), lambda i: (i, 0))
    return pl.pallas_call(
        body, grid=(t // tm,), in_specs=[act, act, wsp, wsp, row], out_specs=row,
        out_shape=jax.ShapeDtypeStruct((t, d), f32),
        compiler_params=_params(("parallel",)), name=name,
    )(dgate, dup, wg, wu, dz)


def ple_fwd(x, p, wg, bg, wp, *, name):
    t, d = x.shape
    dp = p.shape[1]
    tm = _tile(t, 512, SUBLANES)

    def body(x_ref, p_ref, wg_ref, bg_ref, wp_ref, o_ref):
        x_ = x_ref[...]
        gate = jax.nn.sigmoid(_bdot(x_, wg_ref[...], NN) + bg_ref[...])
        o_ref[...] = x_ + gate * _bdot(p_ref[...], wp_ref[...], NN)

    row = pl.BlockSpec((tm, d), lambda i: (i, 0))
    return pl.pallas_call(
        body, grid=(t // tm,),
        in_specs=[row, pl.BlockSpec((tm, dp), lambda i: (i, 0)), pl.BlockSpec((d, d), lambda i: (0, 0)),
                  pl.BlockSpec((1, d), lambda i: (0, 0)), pl.BlockSpec((dp, d), lambda i: (0, 0))],
        out_specs=row, out_shape=jax.ShapeDtypeStruct((t, d), f32),
        compiler_params=_params(("parallel",)), name=name,
    )(x, p, wg, bg, wp)


def ple_bwd(x, p, dy, wg, wgt, bg, wp, *, name):
    t, d = x.shape
    dp = p.shape[1]
    tm = _tile(t, 512, SUBLANES)

    def body(x_ref, p_ref, dy_ref, wg_ref, wgt_ref, bg_ref, wp_ref, dx_ref, dwg_ref, dbg_ref, dwp_ref):
        x_ = x_ref[...]
        dy_ = dy_ref[...]
        s = jax.nn.sigmoid(_bdot(x_, wg_ref[...], NN) + bg_ref[...])
        e = _bdot(p_ref[...], wp_ref[...], NN)
        da = dy_ * e * s * (1.0 - s)
        de = dy_ * s
        dx_ref[...] = dy_ + _bdot(da, wgt_ref[...], NN)
        pwg = _bdot(x_, da, TN)
        pbg = jnp.sum(da, 0, keepdims=True)
        pwp = _bdot(p_ref[...], de, TN)

        @pl.when(pl.program_id(0) == 0)
        def _():
            dwg_ref[...] = pwg
            dbg_ref[...] = pbg
            dwp_ref[...] = pwp

        @pl.when(pl.program_id(0) > 0)
        def _():
            dwg_ref[...] += pwg
            dbg_ref[...] += pbg
            dwp_ref[...] += pwp

    row = pl.BlockSpec((tm, d), lambda i: (i, 0))
    full = lambda shape: pl.BlockSpec(shape, lambda i: (0, 0))
    return pl.pallas_call(
        body, grid=(t // tm,),
        in_specs=[row, pl.BlockSpec((tm, dp), lambda i: (i, 0)), row, full((d, d)), full((d, d)), full((1, d)),
                  full((dp, d))],
        out_specs=[row, full((d, d)), full((1, d)), full((dp, d))],
        out_shape=[jax.ShapeDtypeStruct((t, d), f32), jax.ShapeDtypeStruct((d, d), f32),
                   jax.ShapeDtypeStruct((1, d), f32), jax.ShapeDtypeStruct((dp, d), f32)],
        compiler_params=_params(("arbitrary",)), name=name,
    )(x, p, dy, wg, wgt, bg, wp)


def _conv_taps(xpad_ref, w_ref, s):
    acc = w_ref[0:1, :] * xpad_ref[SUBLANES - 3:SUBLANES - 3 + s, :]
    for j in range(1, 4):
        acc = acc + w_ref[j:j + 1, :] * xpad_ref[SUBLANES - 3 + j:SUBLANES - 3 + j + s, :]
    return acc


def conv_fwd(x, w, bias, act, nb, *, name):
    t, c = x.shape
    s = t // nb
    cw = GROUP_W

    def body(x_ref, w_ref, b_ref, y_ref, xpad):
        xpad[0:SUBLANES, :] = jnp.zeros((SUBLANES, cw), f32)
        xpad[SUBLANES:, :] = x_ref[...]
        acc = _conv_taps(xpad, w_ref, s) + b_ref[...]
        y_ref[...] = _silu(acc) if act else acc

    slab = pl.BlockSpec((s, cw), lambda b, g: (b, g))
    return pl.pallas_call(
        body, grid=(nb, c // cw),
        in_specs=[slab, pl.BlockSpec((4, cw), lambda b, g: (0, g)), pl.BlockSpec((1, cw), lambda b, g: (0, g))],
        out_specs=slab, out_shape=jax.ShapeDtypeStruct((t, c), f32),
        scratch_shapes=[pltpu.VMEM((s + SUBLANES, cw), f32)],
        compiler_params=_params(("parallel", "parallel")), name=name,
    )(x, w, bias)


def conv_bwd(x, w, bias, dy, act, nb, *, name):
    t, c = x.shape
    s = t // nb
    cw = GROUP_W

    def body(x_ref, w_ref, b_ref, dy_ref, dx_ref, dw_ref, db_ref, xpad, dpad):
        xpad[0:SUBLANES, :] = jnp.zeros((SUBLANES, cw), f32)
        xpad[SUBLANES:, :] = x_ref[...]
        dacc = dy_ref[...]
        if act:
            acc = _conv_taps(xpad, w_ref, s) + b_ref[...]
            sg = jax.nn.sigmoid(acc)
            dacc = dacc * (sg * (1.0 + acc * (1.0 - sg)))
        dpad[0:s, :] = dacc
        dpad[s:, :] = jnp.zeros((SUBLANES, cw), f32)
        dx = w_ref[0:1, :] * dpad[3:3 + s, :]
        for j in range(1, 4):
            dx = dx + w_ref[j:j + 1, :] * dpad[3 - j:3 - j + s, :]
        dx_ref[...] = dx
        first = pl.program_id(1) == 0
        for j in range(4):
            pw = jnp.sum(dacc * xpad[SUBLANES - 3 + j:SUBLANES - 3 + j + s, :], 0, keepdims=True)

            @pl.when(first)
            def _():
                dw_ref[j:j + 1, :] = pw

            @pl.when(jnp.logical_not(first))
            def _():
                dw_ref[j:j + 1, :] += pw

        pb = jnp.sum(dacc, 0, keepdims=True)

        @pl.when(first)
        def _():
            db_ref[...] = pb

        @pl.when(jnp.logical_not(first))
        def _():
            db_ref[...] += pb

    slab = pl.BlockSpec((s, cw), lambda g, b: (b, g))
    wsp = pl.BlockSpec((4, cw), lambda g, b: (0, g))
    bsp = pl.BlockSpec((1, cw), lambda g, b: (0, g))
    return pl.pallas_call(
        body, grid=(c // cw, nb), in_specs=[slab, wsp, bsp, slab], out_specs=[slab, wsp, bsp],
        out_shape=[jax.ShapeDtypeStruct((t, c), f32), jax.ShapeDtypeStruct((4, c), f32),
                   jax.ShapeDtypeStruct((1, c), f32)],
        scratch_shapes=[pltpu.VMEM((s + SUBLANES, cw), f32), pltpu.VMEM((s + SUBLANES, cw), f32)],
        compiler_params=_params(("parallel", "arbitrary")), name=name,
    )(x, w, bias, dy)


def _each(f, *lists):
    return [f(*a) for a in zip(*lists)]


def _attn_heads(qs, kbs, vbs, sinks, valid, dist, dots):
    nn, nt = dots[:2]
    kv = [h // A_GROUP for h in range(A_HEADS)]
    scs = [nt(qs[h], kbs[kv[h]]) for h in range(A_HEADS)]
    prs = []
    for h in range(A_HEADS):
        sc = scs[h] * (A_HEAD_DIM ** -0.5) - 2.0 ** -(h + 1) * dist
        sc = jnp.where(valid, sc, NEG)
        m = lax.stop_gradient(jnp.maximum(jnp.max(sc, -1, keepdims=True), sinks[h]))
        pr = jnp.exp(sc - m)
        den = jnp.sum(pr, -1, keepdims=True) + jnp.exp(sinks[h] - m)
        prs.append(pr / den)
    return [nn(prs[h], vbs[kv[h]]) for h in range(A_HEADS)]


def _attn_band_consts(r0):
    band = A_WINDOW + CHUNK
    qi = lax.broadcasted_iota(jnp.int32, (CHUNK, band), 0)
    kj = lax.broadcasted_iota(jnp.int32, (CHUNK, band), 1)
    dist = jnp.abs(qi + A_WINDOW - kj).astype(f32)
    valid = (kj + r0) >= A_WINDOW
    return dist, valid


def attn_fwd(qkv, sinks, nb, *, name):
    t = qkv.shape[0]
    s = t // nb
    band = A_WINDOW + CHUNK
    hd = A_HEAD_DIM

    def body(qkv_ref, sink_ref, o_ref, kvpad):
        kvpad[0:A_WINDOW, :] = jnp.zeros((A_WINDOW, 2 * A_KV_WIDTH), f32)
        kvpad[A_WINDOW:, :] = qkv_ref[:, A_WIDTH:]

        def chunk(n, carry):
            r0 = pl.multiple_of(n * CHUNK, CHUNK)
            dist, valid = _attn_band_consts(r0)
            kbs = [kvpad[pl.ds(r0, band), kvh * hd:(kvh + 1) * hd] for kvh in range(A_KV_HEADS)]
            vbs = [kvpad[pl.ds(r0, band), A_KV_WIDTH + kvh * hd:A_KV_WIDTH + (kvh + 1) * hd]
                   for kvh in range(A_KV_HEADS)]
            qs = [qkv_ref[pl.ds(r0, CHUNK), h * hd:(h + 1) * hd] for h in range(A_HEADS)]
            outs = _attn_heads(qs, kbs, vbs, [sink_ref[:, h:h + 1] for h in range(A_HEADS)], valid, dist, RAW_DOTS)
            for h in range(A_HEADS):
                o_ref[pl.ds(r0, CHUNK), h * hd:(h + 1) * hd] = outs[h]
            return carry

        lax.fori_loop(0, s // CHUNK, chunk, 0)

    return pl.pallas_call(
        body, grid=(nb,),
        in_specs=[pl.BlockSpec((s, A_WIDTH + 2 * A_KV_WIDTH), lambda b: (b, 0)),
                  pl.BlockSpec((1, A_HEADS), lambda b: (0, 0))],
        out_specs=pl.BlockSpec((s, A_WIDTH), lambda b: (b, 0)),
        out_shape=jax.ShapeDtypeStruct((t, A_WIDTH), f32),
        scratch_shapes=[pltpu.VMEM((s + A_WINDOW, 2 * A_KV_WIDTH), f32)],
        compiler_params=_params(("parallel",)), name=name,
    )(qkv, sinks)


def attn_bwd(qkv, sinks, do, nb, *, name):
    t = qkv.shape[0]
    s = t // nb
    band = A_WINDOW + CHUNK
    hd = A_HEAD_DIM
    kvw = 2 * A_KV_WIDTH

    def body(qkv_ref, sink_ref, do_ref, dqkv_ref, dsink_ref, kvpad, dkvpad):
        kvpad[0:A_WINDOW, :] = jnp.zeros((A_WINDOW, kvw), f32)
        kvpad[A_WINDOW:, :] = qkv_ref[:, A_WIDTH:]
        dkvpad[...] = jnp.zeros((s + A_WINDOW, kvw), f32)

        def chunk(n, dsinks):
            r0 = pl.multiple_of(n * CHUNK, CHUNK)
            dist, valid = _attn_band_consts(r0)
            ksl = [slice(kvh * hd, (kvh + 1) * hd) for kvh in range(A_KV_HEADS)]
            vsl = [slice(A_KV_WIDTH + kvh * hd, A_KV_WIDTH + (kvh + 1) * hd) for kvh in range(A_KV_HEADS)]
            kbs = [kvpad[pl.ds(r0, band), sl] for sl in ksl]
            vbs = [kvpad[pl.ds(r0, band), sl] for sl in vsl]
            dkbs = [dkvpad[pl.ds(r0, band), sl] for sl in ksl]
            dvbs = [dkvpad[pl.ds(r0, band), sl] for sl in vsl]
            qs = [qkv_ref[pl.ds(r0, CHUNK), h * hd:(h + 1) * hd] for h in range(A_HEADS)]
            dos = [do_ref[pl.ds(r0, CHUNK), h * hd:(h + 1) * hd] for h in range(A_HEADS)]
            fn = functools.partial(_attn_heads, valid=valid, dist=dist, dots=VJP_DOTS)
            _, vjp = jax.vjp(fn, qs, kbs, vbs, [sink_ref[:, h:h + 1] for h in range(A_HEADS)])
            dqs, dks, dvs, dss = vjp(dos)
            for h in range(A_HEADS):
                dqkv_ref[pl.ds(r0, CHUNK), h * hd:(h + 1) * hd] = dqs[h]
            for kvh in range(A_KV_HEADS):
                dkvpad[pl.ds(r0, band), ksl[kvh]] = dkbs[kvh] + dks[kvh]
                dkvpad[pl.ds(r0, band), vsl[kvh]] = dvbs[kvh] + dvs[kvh]
            return tuple(dsinks[h] + dss[h] for h in range(A_HEADS))

        dsinks = lax.fori_loop(0, s // CHUNK, chunk, tuple(jnp.zeros((1, 1), f32) for _ in range(A_HEADS)))
        dqkv_ref[:, A_WIDTH:] = dkvpad[A_WINDOW:, :]
        first = pl.program_id(0) == 0
        for h in range(A_HEADS):
            @pl.when(first)
            def _():
                dsink_ref[:, h:h + 1] = dsinks[h]

            @pl.when(jnp.logical_not(first))
            def _():
                dsink_ref[:, h:h + 1] += dsinks[h]

    wq = A_WIDTH + kvw
    return pl.pallas_call(
        body, grid=(nb,),
        in_specs=[pl.BlockSpec((s, wq), lambda b: (b, 0)), pl.BlockSpec((1, A_HEADS), lambda b: (0, 0)),
                  pl.BlockSpec((s, A_WIDTH), lambda b: (b, 0))],
        out_specs=[pl.BlockSpec((s, wq), lambda b: (b, 0)), pl.BlockSpec((1, A_HEADS), lambda b: (0, 0))],
        out_shape=[jax.ShapeDtypeStruct((t, wq), f32), jax.ShapeDtypeStruct((1, A_HEADS), f32)],
        scratch_shapes=[pltpu.VMEM((s + A_WINDOW, kvw), f32), pltpu.VMEM((s + A_WINDOW, kvw), f32)],
        compiler_params=_params(("arbitrary",)), name=name,
    )(qkv, sinks, do)


def _rg_gates(xc, wa, wx, ba, bx, lam, nn):
    r = jax.nn.sigmoid(nn(xc, wa) + ba)
    i = jax.nn.sigmoid(nn(xc, wx) + bx)
    log_a = -RG_C * r * jax.nn.softplus(-lam)
    a = jnp.exp(log_a)
    mult = jnp.sqrt(-jnp.tanh(log_a) * (jnp.exp(2.0 * log_a) + 1.0))
    return a, mult * (i * xc)


def _linear_scan(a, u, reverse):
    s = a.shape[0]
    t = lax.broadcasted_iota(jnp.int32, a.shape, 0)
    d = 1
    while d < s:
        if reverse:
            keep = t < s - d
            shift = s - d
        else:
            keep = t >= d
            shift = d
        us = jnp.where(keep, pltpu.roll(u, shift, 0), 0.0)
        as_ = jnp.where(keep, pltpu.roll(a, shift, 0), 1.0)
        u = u + a * us
        a = a * as_
        d *= 2
    return u


def rglru_fwd(xc, bg, wa, wx, ba, bx, lam, nb, *, name):
    t, c = xc.shape
    s = t // nb
    cw = GROUP_W

    def body(xc_ref, bg_ref, wa_ref, wx_ref, ba_ref, bx_ref, lam_ref, y_ref, h_ref):
        a, u = _rg_gates(xc_ref[...], wa_ref[...], wx_ref[...], ba_ref[...], bx_ref[...], lam_ref[...], RAW_DOTS[0])
        h = _linear_scan(a, u, False)
        h_ref[...] = h
        y_ref[...] = h * jax.nn.gelu(bg_ref[...])

    slab = pl.BlockSpec((s, cw), lambda b, g: (b, g))
    wsp = pl.BlockSpec((None, cw, cw), lambda b, g: (g, 0, 0))
    vec = pl.BlockSpec((1, cw), lambda b, g: (0, g))
    return pl.pallas_call(
        body, grid=(nb, c // cw), in_specs=[slab, slab, wsp, wsp, vec, vec, vec], out_specs=[slab, slab],
        out_shape=[jax.ShapeDtypeStruct((t, c), f32)] * 2,
        compiler_params=_params(("parallel", "parallel")), name=name,
    )(xc, bg, wa, wx, ba, bx, lam)


def rglru_bwd(xc, bg, h, dy, wa, wx, ba, bx, lam, nb, *, name):
    t, c = xc.shape
    s = t // nb
    cw = GROUP_W

    def body(xc_ref, bg_ref, h_ref, dy_ref, wa_ref, wx_ref, ba_ref, bx_ref, lam_ref,
             dxc_ref, dbg_ref, dwa_ref, dwx_ref, dba_ref, dbx_ref, dlam_ref):
        h = h_ref[...]
        dy_ = dy_ref[...]
        gel, gel_vjp = jax.vjp(jax.nn.gelu, bg_ref[...])
        dbg_ref[...] = gel_vjp(dy_ * h)[0]
        dh = dy_ * gel
        gates = functools.partial(_rg_gates, nn=_bnn)
        (a, _), gates_vjp = jax.vjp(gates, xc_ref[...], wa_ref[...], wx_ref[...], ba_ref[...], bx_ref[...],
                                    lam_ref[...])
        ti = lax.broadcasted_iota(jnp.int32, a.shape, 0)
        a_next = jnp.where(ti < s - 1, pltpu.roll(a, s - 1, 0), 0.0)
        lam_t = _linear_scan(a_next, dh, True)
        h_prev = jnp.where(ti >= 1, pltpu.roll(h, 1, 0), 0.0)
        dxc, dwa, dwx, dba, dbx, dlam = gates_vjp((lam_t * h_prev, lam_t))
        dxc_ref[...] = dxc
        first = pl.program_id(1) == 0

        @pl.when(first)
        def _():
            dwa_ref[...] = dwa
            dwx_ref[...] = dwx
            dba_ref[...] = dba
            dbx_ref[...] = dbx
            dlam_ref[...] = dlam

        @pl.when(jnp.logical_not(first))
        def _():
            dwa_ref[...] += dwa
            dwx_ref[...] += dwx
            dba_ref[...] += dba
            dbx_ref[...] += dbx
            dlam_ref[...] += dlam

    slab = pl.BlockSpec((s, cw), lambda g, b: (b, g))
    wsp = pl.BlockSpec((None, cw, cw), lambda g, b: (g, 0, 0))
    vec = pl.BlockSpec((1, cw), lambda g, b: (0, g))
    ng = c // cw
    return pl.pallas_call(
        body, grid=(ng, nb), in_specs=[slab, slab, slab, slab, wsp, wsp, vec, vec, vec],
        out_specs=[slab, slab, wsp, wsp, vec, vec, vec],
        out_shape=[jax.ShapeDtypeStruct((t, c), f32), jax.ShapeDtypeStruct((t, c), f32),
                   jax.ShapeDtypeStruct((ng, cw, cw), f32), jax.ShapeDtypeStruct((ng, cw, cw), f32),
                   jax.ShapeDtypeStruct((1, c), f32), jax.ShapeDtypeStruct((1, c), f32),
                   jax.ShapeDtypeStruct((1, c), f32)],
        compiler_params=_params(("parallel", "arbitrary")), name=name,
    )(xc, bg, h, dy, wa, wx, ba, bx, lam)


def _gdn_chunks_prep(qs, ks, vs, bls, als, a_log, dt_b, dots):
    nn, nt, csum = dots[0], dots[1], dots[3]
    hd = C_HEAD_DIM
    ri = lax.broadcasted_iota(jnp.int32, (CHUNK, CHUNK), 0)
    ci = lax.broadcasted_iota(jnp.int32, (CHUNK, CHUNK), 1)
    tril = ri >= ci
    strict = ri > ci
    eye = (ri == ci).astype(f32)
    qn = [q * lax.rsqrt(jnp.sum(q * q, -1, keepdims=True) + NORM_EPS) * (hd ** -0.5) for q in qs]
    kn = [k * lax.rsqrt(jnp.sum(k * k, -1, keepdims=True) + NORM_EPS) for k in ks]
    beta = [jax.nn.sigmoid(bl) for bl in bls]
    g = [-jnp.exp(a_log) * jax.nn.softplus(al + dt_b) for al in als]
    gc_sq = [csum(jnp.broadcast_to(g_, (CHUNK, CHUNK))) for g_ in g]
    gc = [csum(jnp.broadcast_to(g_, (CHUNK, hd))) for g_ in g]
    decay = [jnp.where(tril, jnp.exp(jnp.where(tril, s - s.T, 0.0)), 0.0) for s in gc_sq]
    kb = _each(jnp.multiply, kn, beta)
    kk = _each(nt, kb, kn)
    pw = [-jnp.where(strict, a * d, 0.0) for a, d in zip(kk, decay)]
    inv = [eye + p_ for p_ in pw]
    for _ in range(5):
        pw = _each(nn, pw, pw)
        inv = _each(jnp.add, inv, _each(nn, inv, pw))
    egc = [jnp.exp(c_) for c_ in gc]
    u = _each(nn, inv, _each(jnp.multiply, vs, beta))
    w = _each(nn, inv, _each(jnp.multiply, kb, egc))
    attn = _each(jnp.multiply, _each(nt, qn, kn), decay)
    g_last = [jnp.sum(jnp.broadcast_to(g_, (CHUNK, hd)), 0, keepdims=True) for g_ in g]
    qg = _each(jnp.multiply, qn, egc)
    kdec = [k_ * jnp.exp(gl_ - c_) for k_, gl_, c_ in zip(kn, g_last, gc)]
    return [(qg[i], kdec[i], w[i], u[i], attn[i], jnp.exp(g_last[i])) for i in range(len(qs))]


def _gdn_heads_step(states, qgs, kdecs, ws, us, attns, gls, zs, ng, dots):
    nn, tn = dots[0], dots[2]
    v_new = _each(jnp.subtract, us, _each(nn, ws, states))
    o = _each(jnp.add, _each(nn, qgs, states), _each(nn, attns, v_new))
    new = [s * gl for s, gl in zip(states, gls)]
    new = _each(jnp.add, new, _each(tn, kdecs, v_new))
    y = [o_ * lax.rsqrt(jnp.mean(o_ * o_, -1, keepdims=True) + NORM_EPS) * ng * _silu(z) for o_, z in zip(o, zs)]
    return y, new


def _loop_unrolled(n, unroll, load, compute, store, init):
    u = unroll if n % unroll == 0 else 1

    def trip(i, carry):
        idx = [i * u + j for j in range(u)]
        loaded = [load(k) for k in idx]
        results = compute(loaded)
        for k, r in zip(idx, results):
            carry = store(k, r, carry)
        return carry

    return lax.fori_loop(0, n // u, trip, init)


def _pick_lane(x, lane):
    li = lax.broadcasted_iota(jnp.int32, x.shape, 1)
    return jnp.sum(jnp.where(li == lane, x, 0.0), 1, keepdims=True)


def _put_lane(col, lane, width):
    li = lax.broadcasted_iota(jnp.int32, (col.shape[0], width), 1)
    return jnp.where(li == lane, col, 0.0)


def _gdn_specs(s, nc):
    hd = C_HEAD_DIM
    head = lambda off: pl.BlockSpec((s, hd), lambda b, h, off=off: (b, off + h))
    attn = pl.BlockSpec((None, s, CHUNK), lambda b, h: (h, b, 0))
    gl = pl.BlockSpec((None, nc * SUBLANES, hd), lambda b, h: (h, b, 0))
    ba = pl.BlockSpec((s, LANES), lambda b, h: (b, 0))
    sc8 = pl.BlockSpec((1, C_HEADS), lambda b, h: (0, 0))
    return head, attn, gl, ba, sc8


def gdn_prep_fwd(qkv, ba, a_log, dt_b, nb, *, name):
    t = qkv.shape[0]
    s = t // nb
    nc = s // CHUNK
    hd = C_HEAD_DIM
    head, attn_sp, gl_sp, ba_sp, sc8 = _gdn_specs(s, nc)

    def body(q_ref, k_ref, v_ref, ba_ref, alog_ref, dtb_ref, qg_ref, kd_ref, w_ref, u_ref, at_ref, gl_ref):
        h = pl.program_id(1)
        a_log_h = _pick_lane(alog_ref[...], h)
        dt_b_h = _pick_lane(dtb_ref[...], h)

        def load(n):
            rows = pl.ds(pl.multiple_of(n * CHUNK, CHUNK), CHUNK)
            bav = ba_ref[rows, :]
            return q_ref[rows, :], k_ref[rows, :], v_ref[rows, :], _pick_lane(bav, h), _pick_lane(bav, C_HEADS + h)

        def compute(loaded):
            return _gdn_chunks_prep(*[list(x) for x in zip(*loaded)], a_log_h, dt_b_h, RAW_DOTS)

        def store(n, outs, carry):
            rows = pl.ds(pl.multiple_of(n * CHUNK, CHUNK), CHUNK)
            qg_ref[rows, :] = outs[0].astype(bf16)
            kd_ref[rows, :] = outs[1].astype(bf16)
            w_ref[rows, :] = outs[2].astype(bf16)
            u_ref[rows, :] = outs[3]
            at_ref[rows, :] = outs[4].astype(bf16)
            gl_ref[pl.ds(pl.multiple_of(n * SUBLANES, SUBLANES), SUBLANES), :] = jnp.broadcast_to(outs[5], (SUBLANES, hd))
            return carry

        _loop_unrolled(nc, PREP_FWD_UNROLL, load, compute, store, 0)

    big = jax.ShapeDtypeStruct((t, C_WIDTH), f32)
    bigb = jax.ShapeDtypeStruct((t, C_WIDTH), bf16)
    return pl.pallas_call(
        body, grid=(nb, C_HEADS),
        in_specs=[head(0), head(C_HEADS), head(2 * C_HEADS), ba_sp, sc8, sc8],
        out_specs=[head(0)] * 4 + [attn_sp, gl_sp],
        out_shape=[bigb, bigb, bigb, big, jax.ShapeDtypeStruct((C_HEADS, t, CHUNK), bf16),
                               jax.ShapeDtypeStruct((C_HEADS, nb * nc * SUBLANES, hd), f32)],
        compiler_params=_params(("parallel", "parallel")), name=name,
    )(qkv, qkv, qkv, ba, a_log, dt_b)


def gdn_prep_bwd(qkv, ba, a_log, dt_b, cts, nb, *, name):
    t = qkv.shape[0]
    s = t // nb
    nc = s // CHUNK
    hd = C_HEAD_DIM
    head, attn_sp, gl_sp, ba_sp, sc8 = _gdn_specs(s, nc)

    def body(q_ref, k_ref, v_ref, ba_ref, alog_ref, dtb_ref, cqg, ckd, cw_, cu, cat, cgl,
             dq_ref, dk_ref, dv_ref, dba_ref, dalog_ref, ddtb_ref):
        b = pl.program_id(0)
        h = pl.program_id(1)
        a_log_h = _pick_lane(alog_ref[...], h)
        dt_b_h = _pick_lane(dtb_ref[...], h)
        prep = functools.partial(_gdn_chunks_prep, dots=VJP_DOTS)

        @pl.when(h == 0)
        def _():
            dba_ref[...] = jnp.zeros((s, LANES), f32)

        def load(n):
            rows = pl.ds(pl.multiple_of(n * CHUNK, CHUNK), CHUNK)
            bav = ba_ref[rows, :]
            cgl_n = cgl[pl.ds(pl.multiple_of(n * SUBLANES, SUBLANES), SUBLANES), :][0:1, :]
            primals = (q_ref[rows, :], k_ref[rows, :], v_ref[rows, :], _pick_lane(bav, h), _pick_lane(bav, C_HEADS + h))
            return primals, (cqg[rows, :], ckd[rows, :], cw_[rows, :], cu[rows, :], cat[rows, :], cgl_n), dba_ref[rows, :]

        def compute(loaded):
            primals = [list(x) for x in zip(*[item[0] for item in loaded])]
            _, vjp = jax.vjp(prep, *primals, a_log_h, dt_b_h)
            dqs, dks, dvs, dbls, dals, dalog, ddtb = vjp([item[1] for item in loaded])
            zero = jnp.zeros((1, 1), f32)
            return [((dqs[i], dks[i], dvs[i], dbls[i], dals[i], dalog if i == 0 else zero, ddtb if i == 0 else zero),
                     loaded[i][2]) for i in range(len(loaded))]

        def store(n, res, carry):
            (dq, dk, dv, dbl, dal, dalog_n, ddtb_n), dba_old = res
            rows = pl.ds(pl.multiple_of(n * CHUNK, CHUNK), CHUNK)
            dq_ref[rows, :] = dq
            dk_ref[rows, :] = dk
            dv_ref[rows, :] = dv
            dba_ref[rows, :] = dba_old + _put_lane(dbl, h, LANES) + _put_lane(dal, C_HEADS + h, LANES)
            return carry[0] + dalog_n, carry[1] + ddtb_n

        da_log, ddt_b = _loop_unrolled(nc, PREP_BWD_UNROLL, load, compute, store,
                                       (jnp.zeros((1, 1), f32), jnp.zeros((1, 1), f32)))
        first = jnp.logical_and(b == 0, h == 0)

        @pl.when(first)
        def _():
            dalog_ref[...] = _put_lane(da_log, h, LANES)
            ddtb_ref[...] = _put_lane(ddt_b, h, LANES)

        @pl.when(jnp.logical_not(first))
        def _():
            dalog_ref[...] += _put_lane(da_log, h, LANES)
            ddtb_ref[...] += _put_lane(ddt_b, h, LANES)

    big = jax.ShapeDtypeStruct((t, C_WIDTH), f32)
    vec = pl.BlockSpec((1, LANES), lambda b, h: (0, 0))
    return pl.pallas_call(
        body, grid=(nb, C_HEADS),
        in_specs=[head(0), head(C_HEADS), head(2 * C_HEADS), ba_sp, sc8, sc8] + [head(0)] * 4 + [attn_sp, gl_sp],
        out_specs=[head(0)] * 3 + [ba_sp, vec, vec],
        out_shape=[big] * 3 + [jax.ShapeDtypeStruct((t, LANES), f32), jax.ShapeDtypeStruct((1, LANES), f32),
                               jax.ShapeDtypeStruct((1, LANES), f32)],
        compiler_params=_params(("arbitrary", "arbitrary")), name=name,
    )(qkv, qkv, qkv, ba, a_log, dt_b, *cts)


def _gdn_rec_specs(s, nc, hp):
    hd = C_HEAD_DIM
    wide = pl.BlockSpec((s, hp * hd), lambda b, j: (b, j))
    attn = pl.BlockSpec((hp, s, CHUNK), lambda b, j: (j, b, 0))
    gl = pl.BlockSpec((hp, nc * SUBLANES, hd), lambda b, j: (j, b, 0))
    ng = pl.BlockSpec((1, hd), lambda b, j: (0, 0))
    return wide, attn, gl, ng


def gdn_rec_fwd(qg, kdec, w, u, attn, gl, z, ng, nb, *, name):
    t = qg.shape[0]
    s = t // nb
    nc = s // CHUNK
    hd = C_HEAD_DIM
    hp = C_HEADS_PER_STEP_FWD
    wide, attn_sp, gl_sp, ng_sp = _gdn_rec_specs(s, nc, hp)

    def body(qg_ref, kd_ref, w_ref, u_ref, at_ref, gl_ref, z_ref, ng_ref, y_ref):
        def chunk(n, states):
            rows = pl.ds(pl.multiple_of(n * CHUNK, CHUNK), CHUNK)
            grow = pl.ds(pl.multiple_of(n * SUBLANES, SUBLANES), SUBLANES)
            cols = [slice(j * hd, (j + 1) * hd) for j in range(hp)]
            ins = [(qg_ref[rows, c], kd_ref[rows, c], w_ref[rows, c], u_ref[rows, c], at_ref[j, rows, :],
                    gl_ref[j, grow, :][0:1, :], z_ref[rows, c]) for j, c in enumerate(cols)]
            ys, new = _gdn_heads_step(list(states), *[list(x) for x in zip(*ins)], ng_ref[...], RAW_DOTS)
            for j in range(hp):
                y_ref[rows, cols[j]] = ys[j]
            return tuple(new)

        lax.fori_loop(0, nc, chunk, tuple(jnp.zeros((hd, hd), f32) for _ in range(hp)))

    return pl.pallas_call(
        body, grid=(nb, C_HEADS // hp),
        in_specs=[wide] * 4 + [attn_sp, gl_sp, wide, ng_sp], out_specs=wide,
        out_shape=jax.ShapeDtypeStruct((t, C_WIDTH), f32),
        compiler_params=_params(("parallel", "parallel")), name=name,
    )(qg, kdec, w, u, attn, gl, z, ng)


def gdn_rec_bwd(qg, kdec, w, u, attn, gl, z, ng, dy, nb, *, name):
    t = qg.shape[0]
    s = t // nb
    nc = s // CHUNK
    hd = C_HEAD_DIM
    hp = C_HEADS_PER_STEP_BWD
    wide, attn_sp, gl_sp, ng_sp = _gdn_rec_specs(s, nc, hp)

    def body(qg_ref, kd_ref, w_ref, u_ref, at_ref, gl_ref, z_ref, ng_ref, dy_ref,
             dqg_ref, dkd_ref, dw_ref, du_ref, dat_ref, dgl_ref, dz_ref, dng_ref, states):
        step = functools.partial(_gdn_heads_step, dots=VJP_DOTS)

        def operands(n):
            rows = pl.ds(pl.multiple_of(n * CHUNK, CHUNK), CHUNK)
            grow = pl.ds(pl.multiple_of(n * SUBLANES, SUBLANES), SUBLANES)
            cols = [slice(j * hd, (j + 1) * hd) for j in range(hp)]
            return ([qg_ref[rows, c].astype(f32) for c in cols], [kd_ref[rows, c].astype(f32) for c in cols],
                    [w_ref[rows, c].astype(f32) for c in cols], [u_ref[rows, c] for c in cols],
                    [at_ref[j, rows, :].astype(f32) for j in range(hp)],
                    [gl_ref[j, grow, :][0:1, :] for j in range(hp)], [z_ref[rows, c] for c in cols])

        def fwd_chunk(n, sts):
            _, new = _gdn_heads_step(list(sts), *operands(n), ng_ref[...], RAW_DOTS)
            for j in range(hp):
                states[j, n] = sts[j]
            return tuple(new)

        lax.fori_loop(0, nc, fwd_chunk, tuple(jnp.zeros((hd, hd), f32) for _ in range(hp)))

        def bwd_chunk(i, carry):
            n = nc - 1 - i
            rows = pl.ds(pl.multiple_of(n * CHUNK, CHUNK), CHUNK)
            grow = pl.ds(pl.multiple_of(n * SUBLANES, SUBLANES), SUBLANES)
            dsts, dng = carry
            dys = [dy_ref[rows, j * hd:(j + 1) * hd] for j in range(hp)]
            _, vjp = jax.vjp(step, [states[j, n] for j in range(hp)], *operands(n), ng_ref[...])
            dst, dqg, dkd, dw, du, dat, dgl, dz, dng_n = vjp((dys, list(dsts)))
            for j in range(hp):
                cols = slice(j * hd, (j + 1) * hd)
                dqg_ref[rows, cols] = dqg[j]
                dkd_ref[rows, cols] = dkd[j]
                dw_ref[rows, cols] = dw[j]
                du_ref[rows, cols] = du[j]
                dat_ref[j, rows, :] = dat[j]
                dgl_ref[j, grow, :] = jnp.broadcast_to(dgl[j], (SUBLANES, hd))
                dz_ref[rows, cols] = dz[j]
            return tuple(dst), dng + dng_n

        _, dng = lax.fori_loop(0, nc, bwd_chunk,
                               (tuple(jnp.zeros((hd, hd), f32) for _ in range(hp)), jnp.zeros((1, hd), f32)))
        first = jnp.logical_and(pl.program_id(0) == 0, pl.program_id(1) == 0)

        @pl.when(first)
        def _():
            dng_ref[...] = dng

        @pl.when(jnp.logical_not(first))
        def _():
            dng_ref[...] += dng

    big = jax.ShapeDtypeStruct((t, C_WIDTH), f32)
    return pl.pallas_call(
        body, grid=(nb, C_HEADS // hp),
        in_specs=[wide] * 4 + [attn_sp, gl_sp, wide, ng_sp, wide],
        out_specs=[wide] * 4 + [attn_sp, gl_sp, wide, ng_sp],
        out_shape=[big] * 4 + [jax.ShapeDtypeStruct(attn.shape, f32), jax.ShapeDtypeStruct(gl.shape, f32), big,
                               jax.ShapeDtypeStruct((1, hd), f32)],
        scratch_shapes=[pltpu.VMEM((hp, nc, hd, hd), f32)],
        compiler_params=_params(("arbitrary", "arbitrary")), name=name,
    )(qg, kdec, w, u, attn, gl, z, ng, dy)


def _blockdiag_slabs(w):
    per = GROUP_W // B_BLOCK
    slabs = jnp.zeros((B_BLOCKS // per, GROUP_W, GROUP_W), w.dtype)
    for h in range(B_BLOCKS):
        o = (h % per) * B_BLOCK
        slabs = slabs.at[h // per, o:o + B_BLOCK, o:o + B_BLOCK].set(w[h])
    return slabs


def _slab_blocks(slabs):
    per = GROUP_W // B_BLOCK
    return jnp.stack([slabs[h // per, (h % per) * B_BLOCK:(h % per + 1) * B_BLOCK,
                            (h % per) * B_BLOCK:(h % per + 1) * B_BLOCK] for h in range(B_BLOCKS)])


def _mixer_ab_fwd(x1, W, g, b, nb, tag):
    w_in = W["ab_w_in"][0].astype(bf16)
    o1, o2 = A_WIDTH + 2 * A_KV_WIDTH, A_WIDTH + 2 * A_KV_WIDTH + B_WIDTH
    w_qkv, w_bx, w_bg = w_in[:, :o1], w_in[:, o1:o2], w_in[:, o2:]
    pqkv = mm_nn(x1, w_qkv, name=tag + "_in_qkv")
    pbx = mm_nn(x1, w_bx, name=tag + "_in_bx")
    pbg = mm_nn(x1, w_bg, name=tag + "_in_bg")
    ya = attn_fwd(pqkv, W["a_sinks"], nb, name=tag + "_attn_fwd")
    xc = conv_fwd(pbx, W["b_conv_w"][0], W["b_conv_b"], False, nb, name=tag + "_conv_fwd")
    wa_s, wx_s = _blockdiag_slabs(W["b_wa"][0]), _blockdiag_slabs(W["b_wx"][0])
    yb, hh = rglru_fwd(xc, pbg, wa_s, wx_s, W["b_ba"], W["b_bx"], W["b_lam"], nb, name=tag + "_rglru_fwd")
    w_out = W["ab_w_out"][0].astype(bf16)
    x2, z1 = proj_ln([ya, yb], [w_out[:A_WIDTH], w_out[A_WIDTH:]], x1, g, b, name=tag + "_out_ln")
    saved = (pqkv, pbx, pbg, ya, xc, yb, hh, wa_s, wx_s, w_qkv, w_bx, w_bg, w_out)
    return x2, z1, saved


def _mixer_ab_bwd(x1, dz1, dz1b, W, saved, nb, tag):
    pqkv, pbx, pbg, ya, xc, yb, hh, wa_s, wx_s, w_qkv, w_bx, w_bg, w_out = saved
    dya = mm_nn(dz1b, w_out[:A_WIDTH].T, name=tag + "_dya")
    dyb = mm_nn(dz1b, w_out[A_WIDTH:].T, name=tag + "_dyb")
    dwo = jnp.concatenate([mm_tn(ya, dz1b, name=tag + "_dwo_a"), mm_tn(yb, dz1b, name=tag + "_dwo_b")], 0)
    dpqkv, dsinks = attn_bwd(pqkv, W["a_sinks"], dya, nb, name=tag + "_attn_bwd")
    dxc, dpbg, dwa_s, dwx_s, dba, dbx, dlam = rglru_bwd(xc, pbg, hh, dyb, wa_s, wx_s, W["b_ba"], W["b_bx"],
                                                       W["b_lam"], nb, name=tag + "_rglru_bwd")
    dpbx, dconv_w, dconv_b = conv_bwd(pbx, W["b_conv_w"][0], W["b_conv_b"], dxc, False, nb, name=tag + "_conv_bwd")
    dw_in = jnp.concatenate([mm_tn(x1, dpqkv, name=tag + "_dwin_qkv"), mm_tn(x1, dpbx, name=tag + "_dwin_bx"),
                             mm_tn(x1, dpbg, name=tag + "_dwin_bg")], 1)
    dx1 = mm_nn(dpqkv, w_qkv.T, add=dz1, add_scale=DN_ALPHA, name=tag + "_dx_qkv")
    dx1 = mm_nn(dpbx, w_bx.T, add=dx1, name=tag + "_dx_bx")
    dx1 = mm_nn(dpbg, w_bg.T, add=dx1, name=tag + "_dx_bg")
    grads = {"ab_w_in": dw_in[None], "a_sinks": dsinks, "b_conv_w": dconv_w[None], "b_conv_b": dconv_b,
             "b_wa": _slab_blocks(dwa_s)[None], "b_ba": dba, "b_wx": _slab_blocks(dwx_s)[None], "b_bx": dbx,
             "b_lam": dlam, "ab_w_out": dwo[None]}
    return dx1, grads


def _mixer_c_fwd(x1, W, g, b, nb, tag):
    w_in = W["c_w_in"][0].astype(bf16)
    d = w_in.shape[0]
    o1, o2 = 3 * C_WIDTH, 4 * C_WIDTH
    w_qkv, w_z = w_in[:, :o1], w_in[:, o1:o2]
    w_ba = jnp.concatenate([w_in[:, o2:], jnp.zeros((d, LANES - 2 * C_HEADS), bf16)], 1)
    pqkv = mm_nn(x1, w_qkv, name=tag + "_in_qkv")
    pz = mm_nn(x1, w_z, name=tag + "_in_z")
    pba = mm_nn(x1, w_ba, name=tag + "_in_ba")
    zero_b = jnp.zeros((1, o1), f32)
    qkvc = conv_fwd(pqkv, W["c_conv_w"][0], zero_b, True, nb, name=tag + "_conv_fwd")
    prep = gdn_prep_fwd(qkvc, pba, W["c_a_log"], W["c_dt_bias"], nb, name=tag + "_prep_fwd")
    yc = gdn_rec_fwd(*prep, pz, W["c_norm_g"], nb, name=tag + "_rec_fwd")
    w_out = W["c_w_out"][0].astype(bf16)
    x2, z1 = proj_ln([yc], [w_out], x1, g, b, name=tag + "_out_ln")
    saved = (pqkv, pz, pba, qkvc, prep, yc, w_qkv, w_z, w_ba, w_out, zero_b)
    return x2, z1, saved


def _mixer_c_bwd(x1, dz1, dz1b, W, saved, nb, tag):
    pqkv, pz, pba, qkvc, prep, yc, w_qkv, w_z, w_ba, w_out, zero_b = saved
    dyc = mm_nn(dz1b, w_out.T, name=tag + "_dyc")
    dwo = mm_tn(yc, dz1b, name=tag + "_dwo")
    rec = gdn_rec_bwd(*prep, pz, W["c_norm_g"], dyc, nb, name=tag + "_rec_bwd")
    cts, dpz, dng = rec[:6], rec[6], rec[7]
    dq, dk, dv, dpba, dalog, ddtb = gdn_prep_bwd(qkvc, pba, W["c_a_log"], W["c_dt_bias"], cts, nb,
                                                 name=tag + "_prep_bwd")
    dqkvc = jnp.concatenate([dq, dk, dv], 1)
    dpqkv, dconv_w, _ = conv_bwd(pqkv, W["c_conv_w"][0], zero_b, dqkvc, True, nb, name=tag + "_conv_bwd")
    dw_in = jnp.concatenate([mm_tn(x1, dpqkv, name=tag + "_dwin_qkv"), mm_tn(x1, dpz, name=tag + "_dwin_z"),
                             mm_tn(x1, dpba, name=tag + "_dwin_ba")[:, :2 * C_HEADS]], 1)
    dx1 = mm_nn(dpqkv, w_qkv.T, add=dz1, add_scale=DN_ALPHA, name=tag + "_dx_qkv")
    dx1 = mm_nn(dpz, w_z.T, add=dx1, name=tag + "_dx_z")
    dx1 = mm_nn(dpba, w_ba.T, add=dx1, name=tag + "_dx_ba")
    grads = {"c_w_in": dw_in[None], "c_conv_w": dconv_w[None], "c_a_log": dalog[:, :C_HEADS],
             "c_dt_bias": ddtb[:, :C_HEADS], "c_norm_g": dng, "c_w_out": dwo[None]}
    return dx1, grads


def _local_step(x, p, target, W, F):
    nb, s, d = x.shape
    t = nb * s
    h = x.reshape(t, d)
    tape = []
    f1 = [F[k] for k in ("ffn1_wg", "ffn1_wu", "ffn1_wd")]
    f2 = [F[k] for k in ("ffn2_wg", "ffn2_wu", "ffn2_wd")]
    for i in range(DEPTH):
        tag = f"l{i}"
        lg = [W["ln_g"][i, k][None] for k in range(3)]
        lb = [W["ln_b"][i, k][None] for k in range(3)]
        x1, z0 = ffn_fwd(h, *f1, i, lg[0], lb[0], name=tag + "_ffn1_fwd")
        mixer = _mixer_ab_fwd if i % 2 == 0 else _mixer_c_fwd
        x2, z1, msaved = mixer(x1, W, lg[1], lb[1], nb, tag + "_mix")
        x3, z2 = ffn_fwd(x2, *f2, i, lg[2], lb[2], name=tag + "_ffn2_fwd")
        pi = p[i].reshape(t, -1)
        pw = (W["ple_wg"][i].astype(bf16), W["ple_bg"][i][None], W["ple_wp"][i].astype(bf16))
        x4 = ple_fwd(x3, pi, *pw, name=tag + "_ple_fwd")
        tape.append((h, z0, x1, msaved, z1, x2, z2, x3, pi, pw, lg))
        h = x4
    dh, sq = loss_head(h, target.reshape(t, d), name="loss_head")
    loss = 0.5 * jnp.sum(sq) / d
    per_layer = [None] * DEPTH
    grads = {}
    df1 = df2 = None
    for i in reversed(range(DEPTH)):
        tag = f"l{i}"
        h_in, z0, x1, msaved, z1, x2, z2, x3, pi, pw, lg = tape[i]
        dx3, dple_wg, dple_bg, dple_wp = ple_bwd(x3, pi, dh, pw[0], pw[0].T, pw[1], pw[2], name=tag + "_ple_bwd")
        dz2, dz2b, dg2, db2 = ln_bwd(z2, dx3, lg[2], name=tag + "_ln2_bwd")
        dgate, dup, *df2 = ffn_bwd_weights(x2.astype(bf16), dz2b, *f2, i, df2, name=tag + "_ffn2_bwd_w")
        dx2 = ffn_bwd_input(dgate, dup, f2[0], f2[1], i, dz2, name=tag + "_ffn2_bwd_x")
        dz1, dz1b, dg1, db1 = ln_bwd(z1, dx2, lg[1], name=tag + "_ln1_bwd")
        mixer_bwd = _mixer_ab_bwd if i % 2 == 0 else _mixer_c_bwd
        dx1, mgrads = mixer_bwd(x1, dz1, dz1b, W, msaved, nb, tag + "_mix")
        grads.update(mgrads)
        dz0, dz0b, dg0, db0 = ln_bwd(z0, dx1, lg[0], name=tag + "_ln0_bwd")
        dgate, dup, *df1 = ffn_bwd_weights(h_in.astype(bf16), dz0b, *f1, i, df1, name=tag + "_ffn1_bwd_w")
        dh = ffn_bwd_input(dgate, dup, f1[0], f1[1], i, dz0, name=tag + "_ffn1_bwd_x")
        per_layer[i] = {"ln_g": jnp.concatenate([dg0, dg1, dg2], 0), "ln_b": jnp.concatenate([db0, db1, db2], 0),
                        "ple_wg": dple_wg, "ple_bg": dple_bg[0], "ple_wp": dple_wp}
    for k in per_layer[0]:
        grads[k] = jnp.stack([per_layer[i][k] for i in range(DEPTH)])
    grads.update(zip(("ffn1_wg", "ffn1_wu", "ffn1_wd"), df1))
    grads.update(zip(("ffn2_wg", "ffn2_wu", "ffn2_wd"), df2))
    return loss, dh.reshape(nb, s, d), grads


WEIGHT_NAMES = ("ffn1_wg", "ffn1_wu", "ffn1_wd", "ffn2_wg", "ffn2_wu", "ffn2_wd", "ln_g", "ln_b", "ple_wg", "ple_bg",
                "ple_wp", "ab_w_in", "a_sinks", "b_conv_w", "b_conv_b", "b_wa", "b_ba", "b_wx", "b_bx", "b_lam",
                "ab_w_out", "c_w_in", "c_conv_w", "c_a_log", "c_dt_bias", "c_norm_g", "c_w_out")
NATIVE_NAMES = WEIGHT_NAMES[:6]
PACKED_NAMES = WEIGHT_NAMES[6:]
SHARD_AXIS = {"ffn1_wg": 2, "ffn1_wu": 2, "ffn1_wd": 1, "ffn2_wg": 2, "ffn2_wu": 2, "ffn2_wd": 1, "ln_g": 2, "ln_b": 2,
              "ple_wg": 1, "ple_wp": 2, "ab_w_in": 2, "b_conv_w": 2, "ab_w_out": 1, "c_w_in": 2, "c_conv_w": 2,
              "c_w_out": 1}
N_CHIPS = 4
PACK_COLS = 512
MESH = pl.DeviceIdType.MESH
ANY = pl.BlockSpec(memory_space=pl.ANY)


def _pack_rows(n):
    unit = 4 * SUBLANES * PACK_COLS
    return -(-n // unit) * 4 * SUBLANES


def _pack(pieces, lead=()):
    k = len(lead)
    flat = jnp.concatenate([a.reshape(lead + (-1,)) for a in pieces], axis=k)
    rows = _pack_rows(flat.shape[k])
    flat = jnp.pad(flat, [(0, 0)] * k + [(0, rows * PACK_COLS - flat.shape[k])])
    return flat.reshape(lead + (rows, PACK_COLS))


def _unpack(pack, shapes, lead=()):
    k = len(lead)
    flat = pack.reshape(lead + (-1,))
    out, o = [], 0
    for shp in shapes:
        n = 1
        for dim in shp:
            n *= dim
        out.append(lax.slice_in_dim(flat, o, o + n, axis=k).reshape(lead + tuple(shp)))
        o += n
    return out


def _mesh_position():
    x, y, c = lax.axis_index("x"), lax.axis_index("y"), lax.axis_index("c")
    chips = [(1 - x, y), (x, 1 - y), (1 - x, 1 - y)]
    return x, y, c, chips


def _remote(src, dst, send_sems, recv_sems, k, to):
    return pltpu.make_async_remote_copy(src_ref=src, dst_ref=dst, send_sem=send_sems.at[k], recv_sem=recv_sems.at[k],
                                        device_id=to, device_id_type=MESH)


def _sems(n):
    return pltpu.SemaphoreType.DMA((n,))


def place_slot(parts, slot, n_slots, dtype, from_slot, *, name):
    n = len(parts)
    r, cols = parts[0].shape[-2:]
    tr = _tile(r, 256, SUBLANES * (4 // jnp.dtype(dtype).itemsize))

    def body(s_ref, *refs):
        for a in range(n):
            refs[n + a][...] = refs[a][...].astype(dtype)

    dst = pl.BlockSpec((None, tr, cols), lambda i, s_ref: (s_ref[0], i, 0))
    src = dst if from_slot else pl.BlockSpec((tr, cols), lambda i, s_ref: (i, 0))
    return pl.pallas_call(
        body,
        grid_spec=pltpu.PrefetchScalarGridSpec(num_scalar_prefetch=1, grid=(r // tr,), in_specs=[src] * n,
                                               out_specs=[dst] * n),
        out_shape=[jax.ShapeDtypeStruct((n_slots, r, cols), dtype)] * n,
        compiler_params=_params(("parallel",)), name=name,
    )(slot, *parts)


def gather_shards(bufs, *, name):
    n = len(bufs)

    def body(*refs):
        out_refs = refs[n:2 * n]
        send_sems, recv_sems = refs[2 * n:]
        x, y, c, chips = _mesh_position()
        me = 2 * x + y
        sibling = (x, y, 1 - c)
        waits = []
        for j, (cx, cy) in enumerate(chips):
            for a in range(n):
                own = out_refs[a].at[me, c]
                cp = _remote(own, own, send_sems, recv_sems, 6 * a + j, (cx, cy, c))
                cp.start()
                waits.append(cp.wait_send)
        for j, (cx, cy) in enumerate(chips):
            for a in range(n):
                got = out_refs[a].at[2 * cx + cy, c]
                _remote(got, got, send_sems, recv_sems, 6 * a + j, (cx, cy, c)).wait_recv()
                fw = _remote(got, got, send_sems, recv_sems, 6 * a + 3 + j, sibling)
                fw.start()
                waits.append(fw.wait_send)
        for j, (cx, cy) in enumerate(chips):
            for a in range(n):
                got = out_refs[a].at[2 * cx + cy, 1 - c]
                _remote(got, got, send_sems, recv_sems, 6 * a + 3 + j, sibling).wait_recv()
        for wait in waits:
            wait()

    return pl.pallas_call(
        body, out_shape=[jax.ShapeDtypeStruct(b.shape, b.dtype) for b in bufs],
        in_specs=[ANY] * n, out_specs=[ANY] * n, scratch_shapes=[_sems(6 * n), _sems(6 * n)],
        input_output_aliases={a: a for a in range(n)}, name=name,
    )(*bufs)


def sibling_exchange(gs, *, name):
    n = len(gs)

    def body(*refs):
        g_refs, out_refs = refs[:n], refs[n:2 * n]
        send_sems, recv_sems = refs[2 * n:]
        x, y, c, _ = _mesh_position()
        cps = [_remote(g_refs[a].at[:, 1 - c], out_refs[a], send_sems, recv_sems, a, (x, y, 1 - c)) for a in range(n)]
        for cp in cps:
            cp.start()
        for cp in cps:
            cp.wait()

    return pl.pallas_call(
        body, out_shape=[jax.ShapeDtypeStruct(g.shape[:1] + g.shape[2:], g.dtype) for g in gs],
        in_specs=[ANY] * n, out_specs=[ANY] * n, scratch_shapes=[_sems(n), _sems(n)], name=name,
    )(*gs)


def add_own_half(gs, others, c_idx, dtype, *, name):
    n = len(gs)
    ns, _, r, cols = gs[0].shape
    tr = _tile(r, 256, 2 * SUBLANES)

    def body(c_ref, *refs):
        for a in range(n):
            refs[2 * n + a][...] = (refs[a][...] + refs[n + a][...]).astype(dtype)

    own = pl.BlockSpec((None, None, tr, cols), lambda s, i, c_ref: (s, c_ref[0], i, 0))
    oth = pl.BlockSpec((None, tr, cols), lambda s, i, c_ref: (s, i, 0))
    return pl.pallas_call(
        body,
        grid_spec=pltpu.PrefetchScalarGridSpec(num_scalar_prefetch=1, grid=(ns, r // tr),
                                               in_specs=[own] * n + [oth] * n, out_specs=[oth] * n),
        out_shape=[jax.ShapeDtypeStruct((ns, r, cols), dtype)] * n,
        compiler_params=_params(("parallel", "parallel")), name=name,
    )(c_idx, *gs, *others)


def chip_exchange(ps, qs, *, name):
    n = len(ps)

    def body(*refs):
        p_refs, q_refs = refs[:n], refs[2 * n:3 * n]
        send_sems, recv_sems = refs[3 * n:]
        x, y, c, chips = _mesh_position()
        me = 2 * x + y
        waits = []
        for j, (cx, cy) in enumerate(chips):
            for a in range(n):
                cp = _remote(p_refs[a].at[2 * cx + cy], q_refs[a].at[me], send_sems, recv_sems, 3 * a + j, (cx, cy, c))
                cp.start()
                waits.append(cp.wait_send)
        for j, (cx, cy) in enumerate(chips):
            for a in range(n):
                got = q_refs[a].at[2 * cx + cy]
                _remote(got, got, send_sems, recv_sems, 3 * a + j, (cx, cy, c)).wait_recv()
        for wait in waits:
            wait()

    return pl.pallas_call(
        body, out_shape=[jax.ShapeDtypeStruct(q_.shape, q_.dtype) for q_ in qs], in_specs=[ANY] * (2 * n),
        out_specs=[ANY] * n, scratch_shapes=[_sems(3 * n), _sems(3 * n)],
        input_output_aliases={n + a: a for a in range(n)}, name=name,
    )(*ps, *qs)


def sum_slots(qs, *, name):
    n = len(qs)
    ns, r, cols = qs[0].shape
    tr = _tile(r, 128, 2 * SUBLANES)

    def body(*refs):
        for a in range(n):
            q_ref = refs[a]
            acc = q_ref[0].astype(f32) + q_ref[1].astype(f32)
            for i in range(2, ns):
                acc = acc + q_ref[i].astype(f32)
            refs[n + a][...] = acc

    return pl.pallas_call(
        body, grid=(r // tr,), in_specs=[pl.BlockSpec((ns, tr, cols), lambda i: (0, i, 0))] * n,
        out_specs=[pl.BlockSpec((tr, cols), lambda i: (i, 0))] * n,
        out_shape=[jax.ShapeDtypeStruct((r, cols), f32)] * n,
        compiler_params=_params(("parallel",)), name=name,
    )(*qs)


def sibling_share(bufs, *, name):
    n = len(bufs)

    def body(*refs):
        out_refs = refs[n:2 * n]
        send_sems, recv_sems = refs[2 * n:]
        x, y, c, _ = _mesh_position()
        sibling = (x, y, 1 - c)
        cps = []
        for a in range(n):
            own = out_refs[a].at[c]
            cp = _remote(own, own, send_sems, recv_sems, a, sibling)
            cp.start()
            cps.append(cp)
        for a in range(n):
            theirs = out_refs[a].at[1 - c]
            _remote(theirs, theirs, send_sems, recv_sems, a, sibling).wait_recv()
        for cp in cps:
            cp.wait_send()

    return pl.pallas_call(
        body, out_shape=[jax.ShapeDtypeStruct(b.shape, b.dtype) for b in bufs], in_specs=[ANY] * n,
        out_specs=[ANY] * n, scratch_shapes=[_sems(n), _sems(n)],
        input_output_aliases={a: a for a in range(n)}, name=name,
    )(*bufs)


def adamw(ws, gs, ms, vs, *, name):
    n = len(ws)
    r, cols = ws[0].shape
    tr = _tile(r, 128, SUBLANES)

    def body(*refs):
        for a in range(n):
            w_ref, g_ref, m_ref, v_ref = (refs[k * n + a] for k in range(4))
            d_ref, m2_ref, v2_ref = (refs[(4 + k) * n + a] for k in range(3))
            g_ = g_ref[...]
            m2 = ADAM_B1 * m_ref[...] + (1.0 - ADAM_B1) * g_
            v2 = ADAM_B2 * v_ref[...] + (1.0 - ADAM_B2) * (g_ * g_)
            m_hat = m2 / (1.0 - ADAM_B1 ** ADAM_STEP)
            v_hat = v2 / (1.0 - ADAM_B2 ** ADAM_STEP)
            d_ref[...] = -ADAM_LR * (m_hat / (jnp.sqrt(v_hat) + ADAM_EPS) + ADAM_WD * w_ref[...])
            m2_ref[...] = m2
            v2_ref[...] = v2

    row = pl.BlockSpec((tr, cols), lambda i: (i, 0))
    out = pl.pallas_call(
        body, grid=(r // tr,), in_specs=[row] * (4 * n), out_specs=[row] * (3 * n),
        out_shape=[jax.ShapeDtypeStruct((r, cols), f32)] * (3 * n),
        compiler_params=_params(("parallel",)), name=name,
    )(*ws, *gs, *ms, *vs)
    return out[:n], out[n:2 * n], out[2 * n:]


def _full_weights(gathered, local, shapes):
    pieces = _unpack(gathered, shapes, lead=(N_CHIPS,))
    full = {}
    for name, loc, pc in zip(PACKED_NAMES, local, pieces):
        ax = SHARD_AXIS.get(name)
        full[name] = loc if ax is None else jnp.concatenate([pc[s] for s in range(N_CHIPS)], axis=ax)
    return full


def _grad_pack(grads, shapes):
    pieces = []
    for name, shp in zip(PACKED_NAMES, shapes):
        g = grads[name]
        ax = SHARD_AXIS.get(name)
        if ax is None:
            pieces.append(jnp.broadcast_to(g.reshape(shp)[None], (N_CHIPS,) + tuple(shp)))
        else:
            pieces.append(jnp.stack(jnp.split(g, N_CHIPS, axis=ax)))
    return _pack(pieces, lead=(N_CHIPS,))


def _by_shape(arrays):
    groups = {}
    for i, a in enumerate(arrays):
        groups.setdefault(a.shape, []).append(i)
    return list(groups.values())


def _grouped(fn, lists, n_out, tag):
    outs = [[None] * len(lists[0]) for _ in range(n_out)]
    for gi, idx in enumerate(_by_shape(lists[0])):
        res = fn(*[[lst[i] for i in idx] for lst in lists], name=f"{tag}_{gi}")
        res = res if n_out > 1 else (res,)
        for k in range(n_out):
            for i, r in zip(idx, res[k]):
                outs[k][i] = r
    return outs if n_out > 1 else outs[0]


def _train_step(x, p, loss_target, weights, m, v):
    packed_w = [weights[k] for k in PACKED_NAMES]
    shapes = [w.shape for w in packed_w]
    halves = lambda a: a.reshape((2, a.shape[0] // 2) + a.shape[1:])
    local = [weights[k] for k in NATIVE_NAMES] + [halves(_pack(packed_w))]
    local_m = [m[k] for k in NATIVE_NAMES] + [halves(_pack([m[k] for k in PACKED_NAMES]))]
    local_v = [v[k] for k in NATIVE_NAMES] + [halves(_pack([v[k] for k in PACKED_NAMES]))]
    flat = lambda lst: [a.reshape((-1, a.shape[-1])) for a in lst]
    c_idx = lax.axis_index("c").astype(jnp.int32).reshape(1)
    chip_idx = (2 * lax.axis_index("x") + lax.axis_index("y")).astype(jnp.int32).reshape(1)

    def placed(arrays, slot, n_slots, dtype, from_slot, tag):
        return _grouped(lambda a, name: place_slot(a, slot, n_slots, dtype, from_slot, name=name), [arrays], 1, tag)

    bufs = placed(flat(local[:-1]), chip_idx, N_CHIPS, bf16, False, "place_ffn_weights")
    bufs += placed(flat(local[-1:]), chip_idx, N_CHIPS, f32, False, "place_packed_weights")
    bufs = [b.reshape((N_CHIPS,) + a.shape) for b, a in zip(bufs, local)]
    gathered = gather_shards(bufs, name="comm_gather_weights")
    full = _full_weights(gathered[-1], packed_w, shapes)
    loss, grad_x, grads = _local_step(x, p, loss_target, full, dict(zip(NATIVE_NAMES, gathered[:-1])))
    gpack = _grad_pack(grads, shapes)
    gs = [grads[k] for k in NATIVE_NAMES] + [gpack.reshape((N_CHIPS,) + local[-1].shape)]
    others = sibling_exchange(gs, name="comm_grad_sibling")
    nn_ = len(NATIVE_NAMES)
    chip_sums = _grouped(lambda a, b, name: add_own_half(a, b, c_idx, bf16, name=name), [gs[:nn_], others[:nn_]], 1,
                         "grad_add_sibling_ffn")
    chip_sums += add_own_half(gs[nn_:], others[nn_:], c_idx, f32, name="grad_add_sibling_packed")
    own = placed(chip_sums[:nn_], chip_idx, N_CHIPS, bf16, True, "place_own_partial_ffn")
    own += placed(chip_sums[nn_:], chip_idx, N_CHIPS, f32, True, "place_own_partial_packed")
    slots = chip_exchange(chip_sums, own, name="comm_grad_chips")
    mine = _grouped(sum_slots, [slots], 1, "grad_sum_chips")
    gsum = sibling_share(placed(mine, c_idx, 2, f32, False, "place_own_half"), name="comm_grad_share")
    delta, m2, v2 = _grouped(adamw, [flat(local), flat(gsum), flat(local_m), flat(local_v)], 3, "adamw")
    loss = lax.psum(loss, ("x", "y", "c"))
    outs = []
    for res in (gsum, delta, m2, v2):
        by_name = {k: a.reshape(weights[k].shape) for k, a in zip(NATIVE_NAMES, res[:-1])}
        by_name.update(zip(PACKED_NAMES, _unpack(res[-1], shapes)))
        outs += [by_name[k] for k in WEIGHT_NAMES]
    return (loss, grad_x, *outs)


def kernel(x, p, ffn1_wg, ffn1_wu, ffn1_wd, ffn2_wg, ffn2_wu, ffn2_wd, ln_g, ln_b, ple_wg, ple_bg, ple_wp, ab_w_in, a_sinks, b_conv_w, b_conv_b, b_wa, b_ba, b_wx, b_bx, b_lam, ab_w_out, c_w_in, c_conv_w, c_a_log, c_dt_bias, c_norm_g, c_w_out, loss_target, m_ffn1_wg, m_ffn1_wu, m_ffn1_wd, m_ffn2_wg, m_ffn2_wu, m_ffn2_wd, m_ln_g, m_ln_b, m_ple_wg, m_ple_bg, m_ple_wp, m_ab_w_in, m_a_sinks, m_b_conv_w, m_b_conv_b, m_b_wa, m_b_ba, m_b_wx, m_b_bx, m_b_lam, m_ab_w_out, m_c_w_in, m_c_conv_w, m_c_a_log, m_c_dt_bias, m_c_norm_g, m_c_w_out, v_ffn1_wg, v_ffn1_wu, v_ffn1_wd, v_ffn2_wg, v_ffn2_wu, v_ffn2_wd, v_ln_g, v_ln_b, v_ple_wg, v_ple_bg, v_ple_wp, v_ab_w_in, v_a_sinks, v_b_conv_w, v_b_conv_b, v_b_wa, v_b_ba, v_b_wx, v_b_bx, v_b_lam, v_ab_w_out, v_c_w_in, v_c_conv_w, v_c_a_log, v_c_dt_bias, v_c_norm_g, v_c_w_out):
    weights = [ffn1_wg, ffn1_wu, ffn1_wd, ffn2_wg, ffn2_wu, ffn2_wd, ln_g, ln_b, ple_wg, ple_bg, ple_wp, ab_w_in, a_sinks,
               b_conv_w, b_conv_b, b_wa, b_ba, b_wx, b_bx, b_lam, ab_w_out, c_w_in, c_conv_w, c_a_log, c_dt_bias, c_norm_g,
               c_w_out]
    m = [m_ffn1_wg, m_ffn1_wu, m_ffn1_wd, m_ffn2_wg, m_ffn2_wu, m_ffn2_wd, m_ln_g, m_ln_b, m_ple_wg, m_ple_bg, m_ple_wp,
         m_ab_w_in, m_a_sinks, m_b_conv_w, m_b_conv_b, m_b_wa, m_b_ba, m_b_wx, m_b_bx, m_b_lam, m_ab_w_out, m_c_w_in,
         m_c_conv_w, m_c_a_log, m_c_dt_bias, m_c_norm_g, m_c_w_out]
    v = [v_ffn1_wg, v_ffn1_wu, v_ffn1_wd, v_ffn2_wg, v_ffn2_wu, v_ffn2_wd, v_ln_g, v_ln_b, v_ple_wg, v_ple_bg, v_ple_wp,
         v_ab_w_in, v_a_sinks, v_b_conv_w, v_b_conv_b, v_b_wa, v_b_ba, v_b_wx, v_b_bx, v_b_lam, v_ab_w_out, v_c_w_in,
         v_c_conv_w, v_c_a_log, v_c_dt_bias, v_c_norm_g, v_c_w_out]
    return _train_step(x, p, loss_target, dict(zip(WEIGHT_NAMES, weights)), dict(zip(WEIGHT_NAMES, m)),
                       dict(zip(WEIGHT_NAMES, v)))
```

```python
import functools

import jax
import jax.numpy as jnp
from jax import lax
from jax.experimental import pallas as pl
from jax.experimental.pallas import tpu as pltpu

f32 = jnp.float32
bf16 = jnp.bfloat16

DEPTH = 2
CHUNK = 64
A_HEADS, A_KV_HEADS, A_GROUP, A_HEAD_DIM = 8, 2, 4, 64
A_WIDTH, A_KV_WIDTH, A_WINDOW = 512, 128, 128
B_WIDTH, B_BLOCKS, B_BLOCK, B_CONV = 512, 8, 64, 4
RG_C = 8.0
C_HEADS, C_HEAD_DIM, C_WIDTH, C_CONV = 8, 128, 1024, 4
DN_ALPHA = (2.0 * DEPTH) ** 0.25
LN_EPS = 1e-5
NORM_EPS = 1e-6
NEG = -1e30
ADAM_LR, ADAM_B1, ADAM_B2, ADAM_EPS, ADAM_WD, ADAM_STEP = 0.001, 0.9, 0.999, 1e-08, 0.01, 10

VMEM_LIMIT_BYTES = 56 * 1024 * 1024
LANES = 128
SUBLANES = 8
GROUP_W = 128
PREP_FWD_UNROLL = 4
PREP_BWD_UNROLL = 4
C_HEADS_PER_STEP = 4
GDN_TIME_BLOCK = 512

NN = ((1,), (0,))
NT = ((1,), (1,))
TN = ((0,), (0,))


def _params(sem):
    return pltpu.CompilerParams(dimension_semantics=sem, vmem_limit_bytes=VMEM_LIMIT_BYTES)


def _tile(n, cap, mult):
    best = None
    t = mult
    while t <= min(n, cap):
        if n % t == 0:
            best = t
        t += mult
    return best if best is not None else n


def _bdot(a, b, dims):
    return lax.dot_general(a.astype(bf16), b.astype(bf16), (dims, ((), ())), preferred_element_type=f32)


def _running_sum(x, reverse):
    s = x.shape[0]
    t = lax.broadcasted_iota(jnp.int32, x.shape, 0)
    d = 1
    while d < s:
        if reverse:
            x = x + jnp.where(t < s - d, pltpu.roll(x, s - d, 0), 0.0)
        else:
            x = x + jnp.where(t >= d, pltpu.roll(x, d, 0), 0.0)
        d *= 2
    return x


@jax.custom_vjp
def _cumsum0(x):
    return _running_sum(x, False)


def _cumsum0_fwd(x):
    return _running_sum(x, False), None


def _cumsum0_bwd(_, g):
    return (_running_sum(g, True),)


_cumsum0.defvjp(_cumsum0_fwd, _cumsum0_bwd)


@jax.custom_vjp
def _bnn(a, b):
    return _bdot(a, b, NN)


def _bnn_fwd(a, b):
    return _bdot(a, b, NN), (a, b)


def _bnn_bwd(res, g):
    a, b = res
    return _bdot(g, b, NT), _bdot(a, g, TN)


_bnn.defvjp(_bnn_fwd, _bnn_bwd)


@jax.custom_vjp
def _bnt(a, b):
    return _bdot(a, b, NT)


def _bnt_fwd(a, b):
    return _bdot(a, b, NT), (a, b)


def _bnt_bwd(res, g):
    a, b = res
    return _bdot(g, b, NN), _bdot(g, a, TN)


_bnt.defvjp(_bnt_fwd, _bnt_bwd)


@jax.custom_vjp
def _btn(a, b):
    return _bdot(a, b, TN)


def _btn_fwd(a, b):
    return _bdot(a, b, TN), (a, b)


def _btn_bwd(res, g):
    a, b = res
    return _bdot(b, g, NT), _bdot(a, g, NN)


_btn.defvjp(_btn_fwd, _btn_bwd)

RAW_DOTS = (lambda a, b: _bdot(a, b, NN), lambda a, b: _bdot(a, b, NT), lambda a, b: _bdot(a, b, TN),
            lambda x: _running_sum(x, False))
VJP_DOTS = (_bnn, _bnt, _btn, _cumsum0)


def _layer_norm(z, g, b):
    mu = jnp.mean(z, -1, keepdims=True)
    d = z - mu
    var = jnp.mean(d * d, -1, keepdims=True)
    return d * lax.rsqrt(var + LN_EPS) * g + b


def _silu(x):
    return x * jax.nn.sigmoid(x)


def mm_nn(a, w, add=None, add_scale=1.0, *, name):
    m, k = a.shape
    n = w.shape[1]
    tm = _tile(m, 512, SUBLANES)
    tn = _tile(n, 1024, LANES)

    def body(*refs):
        if add is None:
            a_ref, w_ref, o_ref = refs
            o_ref[...] = _bdot(a_ref[...], w_ref[...], NN)
        else:
            a_ref, w_ref, add_ref, o_ref = refs
            o_ref[...] = _bdot(a_ref[...], w_ref[...], NN) + add_scale * add_ref[...]

    in_specs = [pl.BlockSpec((tm, k), lambda i, j: (i, 0)), pl.BlockSpec((k, tn), lambda i, j: (0, j))]
    args = [a, w]
    if add is not None:
        in_specs.append(pl.BlockSpec((tm, tn), lambda i, j: (i, j)))
        args.append(add)
    return pl.pallas_call(
        body, grid=(m // tm, n // tn), in_specs=in_specs,
        out_specs=pl.BlockSpec((tm, tn), lambda i, j: (i, j)),
        out_shape=jax.ShapeDtypeStruct((m, n), f32),
        compiler_params=_params(("parallel", "parallel")), name=name,
    )(*args)


def mm_tn(a, b, *, name):
    m, k = a.shape
    n = b.shape[1]
    tm = _tile(m, 512, SUBLANES)
    tn = _tile(n, 512, LANES)

    def body(a_ref, b_ref, o_ref):
        part = _bdot(a_ref[...], b_ref[...], TN)

        @pl.when(pl.program_id(1) == 0)
        def _():
            o_ref[...] = part

        @pl.when(pl.program_id(1) > 0)
        def _():
            o_ref[...] += part

    return pl.pallas_call(
        body, grid=(n // tn, m // tm),
        in_specs=[pl.BlockSpec((tm, k), lambda j, i: (i, 0)), pl.BlockSpec((tm, tn), lambda j, i: (i, j))],
        out_specs=pl.BlockSpec((k, tn), lambda j, i: (0, j)),
        out_shape=jax.ShapeDtypeStruct((k, n), f32),
        compiler_params=_params(("parallel", "arbitrary")), name=name,
    )(a, b)


def proj_ln(a_list, w_list, xres, g, b, *, name):
    t, d = xres.shape
    tm = _tile(t, 256, SUBLANES)
    na = len(a_list)

    def body(*refs):
        a_refs, w_refs = refs[:na], refs[na:2 * na]
        x_ref, g_ref, b_ref, y_ref, z_ref = refs[2 * na:]
        z = DN_ALPHA * x_ref[...]
        for a_ref, w_ref in zip(a_refs, w_refs):
            z = z + _bdot(a_ref[...], w_ref[...], NN)
        z_ref[...] = z
        y_ref[...] = _layer_norm(z, g_ref[...], b_ref[...])

    in_specs = [pl.BlockSpec((tm, a.shape[1]), lambda i: (i, 0)) for a in a_list]
    in_specs += [pl.BlockSpec(w.shape, lambda i: (0, 0)) for w in w_list]
    in_specs += [pl.BlockSpec((tm, d), lambda i: (i, 0)), pl.BlockSpec((1, d), lambda i: (0, 0)),
                 pl.BlockSpec((1, d), lambda i: (0, 0))]
    return pl.pallas_call(
        body, grid=(t // tm,), in_specs=in_specs,
        out_specs=[pl.BlockSpec((tm, d), lambda i: (i, 0))] * 2,
        out_shape=[jax.ShapeDtypeStruct((t, d), f32)] * 2,
        compiler_params=_params(("parallel",)), name=name,
    )(*a_list, *w_list, xres, g, b)


def ln_bwd(z, dy, g, *, name):
    t, d = z.shape
    tm = _tile(t, 512, SUBLANES)

    def body(z_ref, dy_ref, g_ref, dz_ref, dzb_ref, dg_ref, db_ref):
        zz = z_ref[...]
        dy_ = dy_ref[...]
        mu = jnp.mean(zz, -1, keepdims=True)
        dd = zz - mu
        var = jnp.mean(dd * dd, -1, keepdims=True)
        rstd = lax.rsqrt(var + LN_EPS)
        xhat = dd * rstd
        dxh = dy_ * g_ref[...]
        dz = rstd * (dxh - jnp.mean(dxh, -1, keepdims=True) - xhat * jnp.mean(dxh * xhat, -1, keepdims=True))
        dz_ref[...] = dz
        dzb_ref[...] = dz.astype(bf16)
        pg = jnp.sum(dy_ * xhat, 0, keepdims=True)
        pb = jnp.sum(dy_, 0, keepdims=True)

        @pl.when(pl.program_id(0) == 0)
        def _():
            dg_ref[...] = pg
            db_ref[...] = pb

        @pl.when(pl.program_id(0) > 0)
        def _():
            dg_ref[...] += pg
            db_ref[...] += pb

    row = pl.BlockSpec((tm, d), lambda i: (i, 0))
    vec = pl.BlockSpec((1, d), lambda i: (0, 0))
    return pl.pallas_call(
        body, grid=(t // tm,), in_specs=[row, row, vec], out_specs=[row, row, vec, vec],
        out_shape=[jax.ShapeDtypeStruct((t, d), f32), jax.ShapeDtypeStruct((t, d), bf16),
                   jax.ShapeDtypeStruct((1, d), f32), jax.ShapeDtypeStruct((1, d), f32)],
        compiler_params=_params(("arbitrary",)), name=name,
    )(z, dy, g)


def loss_head(y, target, *, name):
    t, d = y.shape
    tm = _tile(t, 512, SUBLANES)

    def body(y_ref, t_ref, dy_ref, sq_ref):
        e = y_ref[...] - t_ref[...]
        dy_ref[...] = e * (1.0 / d)
        part = jnp.sum(e * e, 0, keepdims=True)

        @pl.when(pl.program_id(0) == 0)
        def _():
            sq_ref[...] = part

        @pl.when(pl.program_id(0) > 0)
        def _():
            sq_ref[...] += part

    row = pl.BlockSpec((tm, d), lambda i: (i, 0))
    vec = pl.BlockSpec((1, d), lambda i: (0, 0))
    return pl.pallas_call(
        body, grid=(t // tm,), in_specs=[row, row], out_specs=[row, vec],
        out_shape=[jax.ShapeDtypeStruct((t, d), f32), jax.ShapeDtypeStruct((1, d), f32)],
        compiler_params=_params(("arbitrary",)), name=name,
    )(y, target)


FFN_COL_BLOCK = 256
FFN_ROWS = 1024


def _lane_blocks(n):
    return [slice(s, min(s + FFN_COL_BLOCK, n)) for s in range(0, n, FFN_COL_BLOCK)]


def ffn_fwd(x, wg, wu, wd, layer, g, b, *, name):
    t, d = x.shape
    nf, _, _, tf = wg.shape
    tm = _tile(t, FFN_ROWS, SUBLANES)

    def body(x_ref, wg_ref, wu_ref, wd_ref, g_ref, b_ref, y_ref, z_ref, acc_ref):
        f = pl.program_id(1)
        xb = x_ref[...].astype(bf16)
        part, pending = None, None
        for cols in _lane_blocks(tf):
            gate_up = (_bdot(xb, wg_ref[:, cols], NN), _bdot(xb, wu_ref[:, cols], NN), cols)
            if pending is not None:
                down = _bdot(_silu(pending[0]) * pending[1], wd_ref[pending[2], :], NN)
                part = down if part is None else part + down
            pending = gate_up
        down = _bdot(_silu(pending[0]) * pending[1], wd_ref[pending[2], :], NN)
        part = down if part is None else part + down

        @pl.when(f == 0)
        def _():
            acc_ref[...] = part

        @pl.when(f > 0)
        def _():
            acc_ref[...] += part

        @pl.when(f == nf - 1)
        def _():
            z = DN_ALPHA * x_ref[...] + 0.5 * acc_ref[...]
            z_ref[...] = z
            y_ref[...] = _layer_norm(z, g_ref[...], b_ref[...])

    row = pl.BlockSpec((tm, d), lambda i, j: (i, 0))
    vec = pl.BlockSpec((1, d), lambda i, j: (0, 0))
    wcol = pl.BlockSpec((None, None, d, tf), lambda i, j: (j, layer, 0, 0))
    wrow = pl.BlockSpec((None, None, tf, d), lambda i, j: (j, layer, 0, 0))
    return pl.pallas_call(
        body, grid=(t // tm, nf),
        in_specs=[row, wcol, wcol, wrow, vec, vec],
        out_specs=[row, row],
        out_shape=[jax.ShapeDtypeStruct((t, d), f32)] * 2,
        scratch_shapes=[pltpu.VMEM((tm, d), f32)],
        compiler_params=_params(("parallel", "arbitrary")), name=name,
    )(x, wg, wu, wd, g, b)


def ffn_bwd_weights(xb, dzb, wg, wu, wd, layer, acc, *, name):
    t, d = xb.shape
    nf, nl, _, tf = wg.shape
    tm = _tile(t, FFN_ROWS, SUBLANES)

    def body(x_ref, dz_ref, wg_ref, wu_ref, wd_ref, *rest):
        dgate_ref, dup_ref, dwg_ref, dwu_ref, dwd_ref = rest[-5:]
        x = x_ref[...]
        dzh = dz_ref[...] * 0.5

        def first_half(cols):
            return _bdot(x, wg_ref[:, cols], NN), _bdot(x, wu_ref[:, cols], NN), _bdot(dzh, wd_ref[cols, :], NT), cols

        def second_half(gate, up, dh, cols):
            sg = jax.nn.sigmoid(gate)
            s = gate * sg
            dup = (dh * s).astype(bf16)
            dgate = (dh * up * (sg * (1.0 + gate * (1.0 - sg)))).astype(bf16)
            dgate_ref[:, cols] = dgate
            dup_ref[:, cols] = dup
            return _bdot(x, dgate, TN), _bdot(x, dup, TN), _bdot(s * up, dzh, TN), cols

        parts, pending = [], None
        for cols in _lane_blocks(tf):
            nxt = first_half(cols)
            if pending is not None:
                parts.append(second_half(*pending))
            pending = nxt
        parts.append(second_half(*pending))

        @pl.when(pl.program_id(1) == 0)
        def _():
            for pwg, pwu, pwd, cols in parts:
                dwg_ref[:, cols] = pwg
                dwu_ref[:, cols] = pwu
                dwd_ref[cols, :] = pwd

        @pl.when(pl.program_id(1) > 0)
        def _():
            for pwg, pwu, pwd, cols in parts:
                dwg_ref[:, cols] += pwg
                dwu_ref[:, cols] += pwu
                dwd_ref[cols, :] += pwd

    row = pl.BlockSpec((tm, d), lambda j, i: (i, 0))
    wcol = pl.BlockSpec((None, None, d, tf), lambda j, i: (j, layer, 0, 0))
    wrow = pl.BlockSpec((None, None, tf, d), lambda j, i: (j, layer, 0, 0))
    act = pl.BlockSpec((None, tm, tf), lambda j, i: (j, i, 0))
    in_specs = [row, row, wcol, wcol, wrow]
    args = [xb, dzb, wg, wu, wd]
    aliases = {}
    if acc is not None:
        in_specs += [pl.BlockSpec(memory_space=pl.ANY)] * 3
        args += list(acc)
        aliases = {5: 2, 6: 3, 7: 4}
    return pl.pallas_call(
        body, grid=(nf, t // tm), in_specs=in_specs, out_specs=[act, act, wcol, wcol, wrow],
        out_shape=[jax.ShapeDtypeStruct((nf, t, tf), bf16), jax.ShapeDtypeStruct((nf, t, tf), bf16),
                   jax.ShapeDtypeStruct((nf, nl, d, tf), f32), jax.ShapeDtypeStruct((nf, nl, d, tf), f32),
                   jax.ShapeDtypeStruct((nf, nl, tf, d), f32)],
        input_output_aliases=aliases,
        compiler_params=_params(("parallel", "arbitrary")), name=name,
    )(*args)


def ffn_bwd_input(dgate, dup, wg, wu, layer, dz, *, name):
    nf, t, tf = dgate.shape
    d = wg.shape[2]
    tm = _tile(t, FFN_ROWS // 2, SUBLANES)

    def body(dg_ref, du_ref, wg_ref, wu_ref, dz_ref, dx_ref):
        acc = DN_ALPHA * dz_ref[...]
        for j in range(nf):
            acc = acc + _bdot(dg_ref[j], wg_ref[j], NT) + _bdot(du_ref[j], wu_ref[j], NT)
        dx_ref[...] = acc

    act = pl.BlockSpec((nf, tm, tf), lambda i: (0, i, 0))
    wsp = pl.BlockSpec((nf, None, d, tf), lambda i: (0, layer, 0, 0))
    row = pl.BlockSpec((tm, d), lambda i: (i, 0))
    return pl.pallas_call(
        body, grid=(t // tm,), in_specs=[act, act, wsp, wsp, row], out_specs=row,
        out_shape=jax.ShapeDtypeStruct((t, d), f32),
        compiler_params=_params(("parallel",)), name=name,
    )(dgate, dup, wg, wu, dz)


def ple_fwd(x, p, wg, bg, wp, *, name):
    t, d = x.shape
    dp = p.shape[1]
    tm = _tile(t, 512, SUBLANES)

    def body(x_ref, p_ref, wg_ref, bg_ref, wp_ref, o_ref):
        x_ = x_ref[...]
        gate = jax.nn.sigmoid(_bdot(x_, wg_ref[...], NN) + bg_ref[...])
        o_ref[...] = x_ + gate * _bdot(p_ref[...], wp_ref[...], NN)

    row = pl.BlockSpec((tm, d), lambda i: (i, 0))
    return pl.pallas_call(
        body, grid=(t // tm,),
        in_specs=[row, pl.BlockSpec((tm, dp), lambda i: (i, 0)), pl.BlockSpec((d, d), lambda i: (0, 0)),
                  pl.BlockSpec((1, d), lambda i: (0, 0)), pl.BlockSpec((dp, d), lambda i: (0, 0))],
        out_specs=row, out_shape=jax.ShapeDtypeStruct((t, d), f32),
        compiler_params=_params(("parallel",)), name=name,
    )(x, p, wg, bg, wp)


def ple_bwd(x, p, dy, wg, wgt, bg, wp, *, name):
    t, d = x.shape
    dp = p.shape[1]
    tm = _tile(t, 512, SUBLANES)

    def body(x_ref, p_ref, dy_ref, wg_ref, wgt_ref, bg_ref, wp_ref, dx_ref, dwg_ref, dbg_ref, dwp_ref):
        x_ = x_ref[...]
        dy_ = dy_ref[...]
        s = jax.nn.sigmoid(_bdot(x_, wg_ref[...], NN) + bg_ref[...])
        e = _bdot(p_ref[...], wp_ref[...], NN)
        da = dy_ * e * s * (1.0 - s)
        de = dy_ * s
        dx_ref[...] = dy_ + _bdot(da, wgt_ref[...], NN)
        pwg = _bdot(x_, da, TN)
        pbg = jnp.sum(da, 0, keepdims=True)
        pwp = _bdot(p_ref[...], de, TN)

        @pl.when(pl.program_id(0) == 0)
        def _():
            dwg_ref[...] = pwg
            dbg_ref[...] = pbg
            dwp_ref[...] = pwp

        @pl.when(pl.program_id(0) > 0)
        def _():
            dwg_ref[...] += pwg
            dbg_ref[...] += pbg
            dwp_ref[...] += pwp

    row = pl.BlockSpec((tm, d), lambda i: (i, 0))
    full = lambda shape: pl.BlockSpec(shape, lambda i: (0, 0))
    return pl.pallas_call(
        body, grid=(t // tm,),
        in_specs=[row, pl.BlockSpec((tm, dp), lambda i: (i, 0)), row, full((d, d)), full((d, d)), full((1, d)),
                  full((dp, d))],
        out_specs=[row, full((d, d)), full((1, d)), full((dp, d))],
        out_shape=[jax.ShapeDtypeStruct((t, d), f32), jax.ShapeDtypeStruct((d, d), f32),
                   jax.ShapeDtypeStruct((1, d), f32), jax.ShapeDtypeStruct((dp, d), f32)],
        compiler_params=_params(("arbitrary",)), name=name,
    )(x, p, dy, wg, wgt, bg, wp)


def _conv_taps(xpad_ref, w_ref, s):
    acc = w_ref[0:1, :] * xpad_ref[SUBLANES - 3:SUBLANES - 3 + s, :]
    for j in range(1, 4):
        acc = acc + w_ref[j:j + 1, :] * xpad_ref[SUBLANES - 3 + j:SUBLANES - 3 + j + s, :]
    return acc


def conv_fwd(x, w, bias, act, nb, *, name):
    t, c = x.shape
    s = t // nb
    cw = GROUP_W

    def body(x_ref, w_ref, b_ref, y_ref, xpad):
        xpad[0:SUBLANES, :] = jnp.zeros((SUBLANES, cw), f32)
        xpad[SUBLANES:, :] = x_ref[...]
        acc = _conv_taps(xpad, w_ref, s) + b_ref[...]
        y_ref[...] = _silu(acc) if act else acc

    slab = pl.BlockSpec((s, cw), lambda b, g: (b, g))
    return pl.pallas_call(
        body, grid=(nb, c // cw),
        in_specs=[slab, pl.BlockSpec((4, cw), lambda b, g: (0, g)), pl.BlockSpec((1, cw), lambda b, g: (0, g))],
        out_specs=slab, out_shape=jax.ShapeDtypeStruct((t, c), f32),
        scratch_shapes=[pltpu.VMEM((s + SUBLANES, cw), f32)],
        compiler_params=_params(("parallel", "parallel")), name=name,
    )(x, w, bias)


def conv_bwd(x, w, bias, dy, act, nb, *, name):
    t, c = x.shape
    s = t // nb
    cw = GROUP_W

    def body(x_ref, w_ref, b_ref, dy_ref, dx_ref, dw_ref, db_ref, xpad, dpad):
        xpad[0:SUBLANES, :] = jnp.zeros((SUBLANES, cw), f32)
        xpad[SUBLANES:, :] = x_ref[...]
        dacc = dy_ref[...]
        if act:
            acc = _conv_taps(xpad, w_ref, s) + b_ref[...]
            sg = jax.nn.sigmoid(acc)
            dacc = dacc * (sg * (1.0 + acc * (1.0 - sg)))
        dpad[0:s, :] = dacc
        dpad[s:, :] = jnp.zeros((SUBLANES, cw), f32)
        dx = w_ref[0:1, :] * dpad[3:3 + s, :]
        for j in range(1, 4):
            dx = dx + w_ref[j:j + 1, :] * dpad[3 - j:3 - j + s, :]
        dx_ref[...] = dx
        first = pl.program_id(1) == 0
        for j in range(4):
            pw = jnp.sum(dacc * xpad[SUBLANES - 3 + j:SUBLANES - 3 + j + s, :], 0, keepdims=True)

            @pl.when(first)
            def _():
                dw_ref[j:j + 1, :] = pw

            @pl.when(jnp.logical_not(first))
            def _():
                dw_ref[j:j + 1, :] += pw

        pb = jnp.sum(dacc, 0, keepdims=True)

        @pl.when(first)
        def _():
            db_ref[...] = pb

        @pl.when(jnp.logical_not(first))
        def _():
            db_ref[...] += pb

    slab = pl.BlockSpec((s, cw), lambda g, b: (b, g))
    wsp = pl.BlockSpec((4, cw), lambda g, b: (0, g))
    bsp = pl.BlockSpec((1, cw), lambda g, b: (0, g))
    return pl.pallas_call(
        body, grid=(c // cw, nb), in_specs=[slab, wsp, bsp, slab], out_specs=[slab, wsp, bsp],
        out_shape=[jax.ShapeDtypeStruct((t, c), f32), jax.ShapeDtypeStruct((4, c), f32),
                   jax.ShapeDtypeStruct((1, c), f32)],
        scratch_shapes=[pltpu.VMEM((s + SUBLANES, cw), f32), pltpu.VMEM((s + SUBLANES, cw), f32)],
        compiler_params=_params(("parallel", "arbitrary")), name=name,
    )(x, w, bias, dy)


def _each(f, *lists):
    return [f(*a) for a in zip(*lists)]


def _attn_heads(qs, kbs, vbs, sinks, valid, dist, dots):
    nn, nt = dots[:2]
    kv = [h // A_GROUP for h in range(A_HEADS)]
    scs = [nt(qs[h], kbs[kv[h]]) for h in range(A_HEADS)]
    prs = []
    for h in range(A_HEADS):
        sc = scs[h] * (A_HEAD_DIM ** -0.5) - 2.0 ** -(h + 1) * dist
        sc = jnp.where(valid, sc, NEG)
        m = lax.stop_gradient(jnp.maximum(jnp.max(sc, -1, keepdims=True), sinks[h]))
        pr = jnp.exp(sc - m)
        den = jnp.sum(pr, -1, keepdims=True) + jnp.exp(sinks[h] - m)
        prs.append(pr / den)
    return [nn(prs[h], vbs[kv[h]]) for h in range(A_HEADS)]


def _attn_band_consts(r0):
    band = A_WINDOW + CHUNK
    qi = lax.broadcasted_iota(jnp.int32, (CHUNK, band), 0)
    kj = lax.broadcasted_iota(jnp.int32, (CHUNK, band), 1)
    dist = jnp.abs(qi + A_WINDOW - kj).astype(f32)
    valid = (kj + r0) >= A_WINDOW
    return dist, valid


def attn_fwd(qkv, sinks, nb, *, name):
    t = qkv.shape[0]
    s = t // nb
    band = A_WINDOW + CHUNK
    hd = A_HEAD_DIM

    def body(qkv_ref, sink_ref, o_ref, kvpad):
        kvpad[0:A_WINDOW, :] = jnp.zeros((A_WINDOW, 2 * A_KV_WIDTH), f32)
        kvpad[A_WINDOW:, :] = qkv_ref[:, A_WIDTH:]

        def chunk(n, carry):
            r0 = pl.multiple_of(n * CHUNK, CHUNK)
            dist, valid = _attn_band_consts(r0)
            kbs = [kvpad[pl.ds(r0, band), kvh * hd:(kvh + 1) * hd] for kvh in range(A_KV_HEADS)]
            vbs = [kvpad[pl.ds(r0, band), A_KV_WIDTH + kvh * hd:A_KV_WIDTH + (kvh + 1) * hd]
                   for kvh in range(A_KV_HEADS)]
            qs = [qkv_ref[pl.ds(r0, CHUNK), h * hd:(h + 1) * hd] for h in range(A_HEADS)]
            outs = _attn_heads(qs, kbs, vbs, [sink_ref[:, h:h + 1] for h in range(A_HEADS)], valid, dist, RAW_DOTS)
            for h in range(A_HEADS):
                o_ref[pl.ds(r0, CHUNK), h * hd:(h + 1) * hd] = outs[h]
            return carry

        lax.fori_loop(0, s // CHUNK, chunk, 0)

    return pl.pallas_call(
        body, grid=(nb,),
        in_specs=[pl.BlockSpec((s, A_WIDTH + 2 * A_KV_WIDTH), lambda b: (b, 0)),
                  pl.BlockSpec((1, A_HEADS), lambda b: (0, 0))],
        out_specs=pl.BlockSpec((s, A_WIDTH), lambda b: (b, 0)),
        out_shape=jax.ShapeDtypeStruct((t, A_WIDTH), f32),
        scratch_shapes=[pltpu.VMEM((s + A_WINDOW, 2 * A_KV_WIDTH), f32)],
        compiler_params=_params(("parallel",)), name=name,
    )(qkv, sinks)


def attn_bwd(qkv, sinks, do, nb, *, name):
    t = qkv.shape[0]
    s = t // nb
    band = A_WINDOW + CHUNK
    hd = A_HEAD_DIM
    kvw = 2 * A_KV_WIDTH

    def body(qkv_ref, sink_ref, do_ref, dqkv_ref, dsink_ref, kvpad, dkvpad):
        kvpad[0:A_WINDOW, :] = jnp.zeros((A_WINDOW, kvw), f32)
        kvpad[A_WINDOW:, :] = qkv_ref[:, A_WIDTH:]
        dkvpad[...] = jnp.zeros((s + A_WINDOW, kvw), f32)

        def chunk(n, dsinks):
            r0 = pl.multiple_of(n * CHUNK, CHUNK)
            dist, valid = _attn_band_consts(r0)
            ksl = [slice(kvh * hd, (kvh + 1) * hd) for kvh in range(A_KV_HEADS)]
            vsl = [slice(A_KV_WIDTH + kvh * hd, A_KV_WIDTH + (kvh + 1) * hd) for kvh in range(A_KV_HEADS)]
            kbs = [kvpad[pl.ds(r0, band), sl] for sl in ksl]
            vbs = [kvpad[pl.ds(r0, band), sl] for sl in vsl]
            dkbs = [dkvpad[pl.ds(r0, band), sl] for sl in ksl]
            dvbs = [dkvpad[pl.ds(r0, band), sl] for sl in vsl]
            qs = [qkv_ref[pl.ds(r0, CHUNK), h * hd:(h + 1) * hd] for h in range(A_HEADS)]
            dos = [do_ref[pl.ds(r0, CHUNK), h * hd:(h + 1) * hd] for h in range(A_HEADS)]
            fn = functools.partial(_attn_heads, valid=valid, dist=dist, dots=VJP_DOTS)
            _, vjp = jax.vjp(fn, qs, kbs, vbs, [sink_ref[:, h:h + 1] for h in range(A_HEADS)])
            dqs, dks, dvs, dss = vjp(dos)
            for h in range(A_HEADS):
                dqkv_ref[pl.ds(r0, CHUNK), h * hd:(h + 1) * hd] = dqs[h]
            for kvh in range(A_KV_HEADS):
                dkvpad[pl.ds(r0, band), ksl[kvh]] = dkbs[kvh] + dks[kvh]
                dkvpad[pl.ds(r0, band), vsl[kvh]] = dvbs[kvh] + dvs[kvh]
            return tuple(dsinks[h] + dss[h] for h in range(A_HEADS))

        dsinks = lax.fori_loop(0, s // CHUNK, chunk, tuple(jnp.zeros((1, 1), f32) for _ in range(A_HEADS)))
        dqkv_ref[:, A_WIDTH:] = dkvpad[A_WINDOW:, :]
        first = pl.program_id(0) == 0
        for h in range(A_HEADS):
            @pl.when(first)
            def _():
                dsink_ref[:, h:h + 1] = dsinks[h]

            @pl.when(jnp.logical_not(first))
            def _():
                dsink_ref[:, h:h + 1] += dsinks[h]

    wq = A_WIDTH + kvw
    return pl.pallas_call(
        body, grid=(nb,),
        in_specs=[pl.BlockSpec((s, wq), lambda b: (b, 0)), pl.BlockSpec((1, A_HEADS), lambda b: (0, 0)),
                  pl.BlockSpec((s, A_WIDTH), lambda b: (b, 0))],
        out_specs=[pl.BlockSpec((s, wq), lambda b: (b, 0)), pl.BlockSpec((1, A_HEADS), lambda b: (0, 0))],
        out_shape=[jax.ShapeDtypeStruct((t, wq), f32), jax.ShapeDtypeStruct((1, A_HEADS), f32)],
        scratch_shapes=[pltpu.VMEM((s + A_WINDOW, kvw), f32), pltpu.VMEM((s + A_WINDOW, kvw), f32)],
        compiler_params=_params(("arbitrary",)), name=name,
    )(qkv, sinks, do)


def _rg_gates(xc, wa, wx, ba, bx, lam, nn):
    r = jax.nn.sigmoid(nn(xc, wa) + ba)
    i = jax.nn.sigmoid(nn(xc, wx) + bx)
    log_a = -RG_C * r * jax.nn.softplus(-lam)
    a = jnp.exp(log_a)
    mult = jnp.sqrt(-jnp.tanh(log_a) * (jnp.exp(2.0 * log_a) + 1.0))
    return a, mult * (i * xc)


def _linear_scan(a, u, reverse):
    s = a.shape[0]
    t = lax.broadcasted_iota(jnp.int32, a.shape, 0)
    d = 1
    while d < s:
        if reverse:
            keep = t < s - d
            shift = s - d
        else:
            keep = t >= d
            shift = d
        us = jnp.where(keep, pltpu.roll(u, shift, 0), 0.0)
        as_ = jnp.where(keep, pltpu.roll(a, shift, 0), 1.0)
        u = u + a * us
        a = a * as_
        d *= 2
    return u


def rglru_fwd(xc, bg, wa, wx, ba, bx, lam, nb, *, name):
    t, c = xc.shape
    s = t // nb
    cw = GROUP_W

    def body(xc_ref, bg_ref, wa_ref, wx_ref, ba_ref, bx_ref, lam_ref, y_ref, h_ref):
        a, u = _rg_gates(xc_ref[...], wa_ref[...], wx_ref[...], ba_ref[...], bx_ref[...], lam_ref[...], RAW_DOTS[0])
        h = _linear_scan(a, u, False)
        h_ref[...] = h
        y_ref[...] = h * jax.nn.gelu(bg_ref[...])

    slab = pl.BlockSpec((s, cw), lambda b, g: (b, g))
    wsp = pl.BlockSpec((None, cw, cw), lambda b, g: (g, 0, 0))
    vec = pl.BlockSpec((1, cw), lambda b, g: (0, g))
    return pl.pallas_call(
        body, grid=(nb, c // cw), in_specs=[slab, slab, wsp, wsp, vec, vec, vec], out_specs=[slab, slab],
        out_shape=[jax.ShapeDtypeStruct((t, c), f32)] * 2,
        compiler_params=_params(("parallel", "parallel")), name=name,
    )(xc, bg, wa, wx, ba, bx, lam)


def rglru_bwd(xc, bg, h, dy, wa, wx, ba, bx, lam, nb, *, name):
    t, c = xc.shape
    s = t // nb
    cw = GROUP_W

    def body(xc_ref, bg_ref, h_ref, dy_ref, wa_ref, wx_ref, ba_ref, bx_ref, lam_ref,
             dxc_ref, dbg_ref, dwa_ref, dwx_ref, dba_ref, dbx_ref, dlam_ref):
        h = h_ref[...]
        dy_ = dy_ref[...]
        gel, gel_vjp = jax.vjp(jax.nn.gelu, bg_ref[...])
        dbg_ref[...] = gel_vjp(dy_ * h)[0]
        dh = dy_ * gel
        gates = functools.partial(_rg_gates, nn=_bnn)
        (a, _), gates_vjp = jax.vjp(gates, xc_ref[...], wa_ref[...], wx_ref[...], ba_ref[...], bx_ref[...],
                                    lam_ref[...])
        ti = lax.broadcasted_iota(jnp.int32, a.shape, 0)
        a_next = jnp.where(ti < s - 1, pltpu.roll(a, s - 1, 0), 0.0)
        lam_t = _linear_scan(a_next, dh, True)
        h_prev = jnp.where(ti >= 1, pltpu.roll(h, 1, 0), 0.0)
        dxc, dwa, dwx, dba, dbx, dlam = gates_vjp((lam_t * h_prev, lam_t))
        dxc_ref[...] = dxc
        first = pl.program_id(1) == 0

        @pl.when(first)
        def _():
            dwa_ref[...] = dwa
            dwx_ref[...] = dwx
            dba_ref[...] = dba
            dbx_ref[...] = dbx
            dlam_ref[...] = dlam

        @pl.when(jnp.logical_not(first))
        def _():
            dwa_ref[...] += dwa
            dwx_ref[...] += dwx
            dba_ref[...] += dba
            dbx_ref[...] += dbx
            dlam_ref[...] += dlam

    slab = pl.BlockSpec((s, cw), lambda g, b: (b, g))
    wsp = pl.BlockSpec((None, cw, cw), lambda g, b: (g, 0, 0))
    vec = pl.BlockSpec((1, cw), lambda g, b: (0, g))
    ng = c // cw
    return pl.pallas_call(
        body, grid=(ng, nb), in_specs=[slab, slab, slab, slab, wsp, wsp, vec, vec, vec],
        out_specs=[slab, slab, wsp, wsp, vec, vec, vec],
        out_shape=[jax.ShapeDtypeStruct((t, c), f32), jax.ShapeDtypeStruct((t, c), f32),
                   jax.ShapeDtypeStruct((ng, cw, cw), f32), jax.ShapeDtypeStruct((ng, cw, cw), f32),
                   jax.ShapeDtypeStruct((1, c), f32), jax.ShapeDtypeStruct((1, c), f32),
                   jax.ShapeDtypeStruct((1, c), f32)],
        compiler_params=_params(("parallel", "arbitrary")), name=name,
    )(xc, bg, h, dy, wa, wx, ba, bx, lam)


def _gdn_chunks_prep(qs, ks, vs, bls, als, a_log, dt_b, dots):
    nn, nt, csum = dots[0], dots[1], dots[3]
    hd = C_HEAD_DIM
    ri = lax.broadcasted_iota(jnp.int32, (CHUNK, CHUNK), 0)
    ci = lax.broadcasted_iota(jnp.int32, (CHUNK, CHUNK), 1)
    tril = ri >= ci
    strict = ri > ci
    eye = (ri == ci).astype(f32)
    qn = [q * lax.rsqrt(jnp.sum(q * q, -1, keepdims=True) + NORM_EPS) * (hd ** -0.5) for q in qs]
    kn = [k * lax.rsqrt(jnp.sum(k * k, -1, keepdims=True) + NORM_EPS) for k in ks]
    beta = [jax.nn.sigmoid(bl) for bl in bls]
    g = [-jnp.exp(a_log) * jax.nn.softplus(al + dt_b) for al in als]
    gc_sq = [csum(jnp.broadcast_to(g_, (CHUNK, CHUNK))) for g_ in g]
    gc = [csum(jnp.broadcast_to(g_, (CHUNK, hd))) for g_ in g]
    decay = [jnp.where(tril, jnp.exp(jnp.where(tril, s - s.T, 0.0)), 0.0) for s in gc_sq]
    kb = _each(jnp.multiply, kn, beta)
    kk = _each(nt, kb, kn)
    pw = [-jnp.where(strict, a * d, 0.0) for a, d in zip(kk, decay)]
    inv = [eye + p_ for p_ in pw]
    for _ in range(5):
        pw = _each(nn, pw, pw)
        inv = _each(jnp.add, inv, _each(nn, inv, pw))
    egc = [jnp.exp(c_) for c_ in gc]
    u = _each(nn, inv, _each(jnp.multiply, vs, beta))
    w = _each(nn, inv, _each(jnp.multiply, kb, egc))
    attn = _each(jnp.multiply, _each(nt, qn, kn), decay)
    g_last = [jnp.sum(jnp.broadcast_to(g_, (CHUNK, hd)), 0, keepdims=True) for g_ in g]
    qg = _each(jnp.multiply, qn, egc)
    kdec = [k_ * jnp.exp(gl_ - c_) for k_, gl_, c_ in zip(kn, g_last, gc)]
    return [(qg[i], kdec[i], w[i], u[i], attn[i], jnp.exp(g_last[i])) for i in range(len(qs))]


def _gdn_heads_step(states, qgs, kdecs, ws, us, attns, gls, zs, ng, dots):
    nn, tn = dots[0], dots[2]
    v_new = _each(jnp.subtract, us, _each(nn, ws, states))
    o = _each(jnp.add, _each(nn, qgs, states), _each(nn, attns, v_new))
    new = [s * gl for s, gl in zip(states, gls)]
    new = _each(jnp.add, new, _each(tn, kdecs, v_new))
    y = [o_ * lax.rsqrt(jnp.mean(o_ * o_, -1, keepdims=True) + NORM_EPS) * ng * _silu(z) for o_, z in zip(o, zs)]
    return y, new


def _loop_unrolled(n, unroll, load, compute, store, init):
    u = unroll if n % unroll == 0 else 1

    def trip(i, carry):
        idx = [i * u + j for j in range(u)]
        loaded = [load(k) for k in idx]
        results = compute(loaded)
        for k, r in zip(idx, results):
            carry = store(k, r, carry)
        return carry

    return lax.fori_loop(0, n // u, trip, init)


def _pick_lane(x, lane):
    li = lax.broadcasted_iota(jnp.int32, x.shape, 1)
    return jnp.sum(jnp.where(li == lane, x, 0.0), 1, keepdims=True)


def _put_lane(col, lane, width):
    li = lax.broadcasted_iota(jnp.int32, (col.shape[0], width), 1)
    return jnp.where(li == lane, col, 0.0)


def _gdn_specs(s, nc):
    hd = C_HEAD_DIM
    head = lambda off: pl.BlockSpec((s, hd), lambda b, h, off=off: (b, off + h))
    attn = pl.BlockSpec((None, s, CHUNK), lambda b, h: (h, b, 0))
    gl = pl.BlockSpec((None, nc * SUBLANES, hd), lambda b, h: (h, b, 0))
    ba = pl.BlockSpec((s, LANES), lambda b, h: (b, 0))
    sc8 = pl.BlockSpec((1, C_HEADS), lambda b, h: (0, 0))
    return head, attn, gl, ba, sc8


def gdn_prep_fwd(qkv, ba, a_log, dt_b, nb, *, name):
    t = qkv.shape[0]
    s = t // nb
    nc = s // CHUNK
    hd = C_HEAD_DIM
    head, attn_sp, gl_sp, ba_sp, sc8 = _gdn_specs(s, nc)

    def body(q_ref, k_ref, v_ref, ba_ref, alog_ref, dtb_ref, qg_ref, kd_ref, w_ref, u_ref, at_ref, gl_ref):
        h = pl.program_id(1)
        a_log_h = _pick_lane(alog_ref[...], h)
        dt_b_h = _pick_lane(dtb_ref[...], h)

        def load(n):
            rows = pl.ds(pl.multiple_of(n * CHUNK, CHUNK), CHUNK)
            bav = ba_ref[rows, :]
            return q_ref[rows, :], k_ref[rows, :], v_ref[rows, :], _pick_lane(bav, h), _pick_lane(bav, C_HEADS + h)

        def compute(loaded):
            return _gdn_chunks_prep(*[list(x) for x in zip(*loaded)], a_log_h, dt_b_h, RAW_DOTS)

        def store(n, outs, carry):
            rows = pl.ds(pl.multiple_of(n * CHUNK, CHUNK), CHUNK)
            qg_ref[rows, :] = outs[0].astype(bf16)
            kd_ref[rows, :] = outs[1].astype(bf16)
            w_ref[rows, :] = outs[2].astype(bf16)
            u_ref[rows, :] = outs[3]
            at_ref[rows, :] = outs[4].astype(bf16)
            gl_ref[pl.ds(pl.multiple_of(n * SUBLANES, SUBLANES), SUBLANES), :] = jnp.broadcast_to(outs[5], (SUBLANES, hd))
            return carry

        _loop_unrolled(nc, PREP_FWD_UNROLL, load, compute, store, 0)

    big = jax.ShapeDtypeStruct((t, C_WIDTH), f32)
    bigb = jax.ShapeDtypeStruct((t, C_WIDTH), bf16)
    return pl.pallas_call(
        body, grid=(nb, C_HEADS),
        in_specs=[head(0), head(C_HEADS), head(2 * C_HEADS), ba_sp, sc8, sc8],
        out_specs=[head(0)] * 4 + [attn_sp, gl_sp],
        out_shape=[bigb, bigb, bigb, big, jax.ShapeDtypeStruct((C_HEADS, t, CHUNK), bf16),
                               jax.ShapeDtypeStruct((C_HEADS, nb * nc * SUBLANES, hd), f32)],
        compiler_params=_params(("parallel", "parallel")), name=name,
    )(qkv, qkv, qkv, ba, a_log, dt_b)


def gdn_prep_bwd(qkv, ba, a_log, dt_b, cts, nb, *, name):
    t = qkv.shape[0]
    s = t // nb
    nc = s // CHUNK
    hd = C_HEAD_DIM
    head, attn_sp, gl_sp, ba_sp, sc8 = _gdn_specs(s, nc)

    def body(q_ref, k_ref, v_ref, ba_ref, alog_ref, dtb_ref, cqg, ckd, cw_, cu, cat, cgl,
             dq_ref, dk_ref, dv_ref, dba_ref, dalog_ref, ddtb_ref):
        b = pl.program_id(0)
        h = pl.program_id(1)
        a_log_h = _pick_lane(alog_ref[...], h)
        dt_b_h = _pick_lane(dtb_ref[...], h)
        prep = functools.partial(_gdn_chunks_prep, dots=VJP_DOTS)

        @pl.when(h == 0)
        def _():
            dba_ref[...] = jnp.zeros((s, LANES), f32)

        def load(n):
            rows = pl.ds(pl.multiple_of(n * CHUNK, CHUNK), CHUNK)
            bav = ba_ref[rows, :]
            cgl_n = cgl[pl.ds(pl.multiple_of(n * SUBLANES, SUBLANES), SUBLANES), :][0:1, :]
            primals = (q_ref[rows, :], k_ref[rows, :], v_ref[rows, :], _pick_lane(bav, h), _pick_lane(bav, C_HEADS + h))
            return primals, (cqg[rows, :], ckd[rows, :], cw_[rows, :], cu[rows, :], cat[rows, :], cgl_n), dba_ref[rows, :]

        def compute(loaded):
            primals = [list(x) for x in zip(*[item[0] for item in loaded])]
            _, vjp = jax.vjp(prep, *primals, a_log_h, dt_b_h)
            dqs, dks, dvs, dbls, dals, dalog, ddtb = vjp([item[1] for item in loaded])
            zero = jnp.zeros((1, 1), f32)
            return [((dqs[i], dks[i], dvs[i], dbls[i], dals[i], dalog if i == 0 else zero, ddtb if i == 0 else zero),
                     loaded[i][2]) for i in range(len(loaded))]

        def store(n, res, carry):
            (dq, dk, dv, dbl, dal, dalog_n, ddtb_n), dba_old = res
            rows = pl.ds(pl.multiple_of(n * CHUNK, CHUNK), CHUNK)
            dq_ref[rows, :] = dq
            dk_ref[rows, :] = dk
            dv_ref[rows, :] = dv
            dba_ref[rows, :] = dba_old + _put_lane(dbl, h, LANES) + _put_lane(dal, C_HEADS + h, LANES)
            return carry[0] + dalog_n, carry[1] + ddtb_n

        da_log, ddt_b = _loop_unrolled(nc, PREP_BWD_UNROLL, load, compute, store,
                                       (jnp.zeros((1, 1), f32), jnp.zeros((1, 1), f32)))
        first = jnp.logical_and(b == 0, h == 0)

        @pl.when(first)
        def _():
            dalog_ref[...] = _put_lane(da_log, h, LANES)
            ddtb_ref[...] = _put_lane(ddt_b, h, LANES)

        @pl.when(jnp.logical_not(first))
        def _():
            dalog_ref[...] += _put_lane(da_log, h, LANES)
            ddtb_ref[...] += _put_lane(ddt_b, h, LANES)

    big = jax.ShapeDtypeStruct((t, C_WIDTH), f32)
    vec = pl.BlockSpec((1, LANES), lambda b, h: (0, 0))
    return pl.pallas_call(
        body, grid=(nb, C_HEADS),
        in_specs=[head(0), head(C_HEADS), head(2 * C_HEADS), ba_sp, sc8, sc8] + [head(0)] * 4 + [attn_sp, gl_sp],
        out_specs=[head(0)] * 3 + [ba_sp, vec, vec],
        out_shape=[big] * 3 + [jax.ShapeDtypeStruct((t, LANES), f32), jax.ShapeDtypeStruct((1, LANES), f32),
                               jax.ShapeDtypeStruct((1, LANES), f32)],
        compiler_params=_params(("arbitrary", "arbitrary")), name=name,
    )(qkv, qkv, qkv, ba, a_log, dt_b, *cts)


def _gdn_rec_specs(sb, nsb, hp, reverse):
    hd = C_HEAD_DIM
    ncb = sb // CHUNK
    blk = (lambda b, k: b * nsb + (nsb - 1 - k)) if reverse else (lambda b, k: b * nsb + k)
    wide = pl.BlockSpec((sb, hp * hd), lambda b, j, k: (blk(b, k), j))
    attn = pl.BlockSpec((hp, sb, CHUNK), lambda b, j, k: (j, blk(b, k), 0))
    gl = pl.BlockSpec((hp, ncb * SUBLANES, hd), lambda b, j, k: (j, blk(b, k), 0))
    ng = pl.BlockSpec((1, hd), lambda b, j, k: (0, 0))
    states = pl.BlockSpec((hp, ncb, hd, hd), lambda b, j, k: (j, blk(b, k), 0, 0))
    return wide, attn, gl, ng, states


def gdn_rec_fwd(qg, kdec, w, u, attn, gl, z, ng, nb, *, name):
    t = qg.shape[0]
    s = t // nb
    sb = min(s, GDN_TIME_BLOCK)
    nsb = s // sb
    hd = C_HEAD_DIM
    hp = C_HEADS_PER_STEP
    wide, attn_sp, gl_sp, ng_sp, st_sp = _gdn_rec_specs(sb, nsb, hp, False)

    def body(qg_ref, kd_ref, w_ref, u_ref, at_ref, gl_ref, z_ref, ng_ref, y_ref, st_ref, carry_ref):
        @pl.when(pl.program_id(2) == 0)
        def _():
            carry_ref[...] = jnp.zeros((hp, hd, hd), f32)

        def chunk(n, states):
            for j in range(hp):
                st_ref[j, n] = states[j]
            rows = pl.ds(pl.multiple_of(n * CHUNK, CHUNK), CHUNK)
            grow = pl.ds(pl.multiple_of(n * SUBLANES, SUBLANES), SUBLANES)
            cols = [slice(j * hd, (j + 1) * hd) for j in range(hp)]
            ins = [(qg_ref[rows, c], kd_ref[rows, c], w_ref[rows, c], u_ref[rows, c], at_ref[j, rows, :],
                    gl_ref[j, grow, :][0:1, :], z_ref[rows, c]) for j, c in enumerate(cols)]
            ys, new = _gdn_heads_step(list(states), *[list(x) for x in zip(*ins)], ng_ref[...], RAW_DOTS)
            for j in range(hp):
                y_ref[rows, cols[j]] = ys[j]
            return tuple(new)

        last = lax.fori_loop(0, sb // CHUNK, chunk, tuple(carry_ref[j] for j in range(hp)))
        for j in range(hp):
            carry_ref[j] = last[j]

    return pl.pallas_call(
        body, grid=(nb, C_HEADS // hp, nsb),
        in_specs=[wide] * 4 + [attn_sp, gl_sp, wide, ng_sp], out_specs=[wide, st_sp],
        out_shape=[jax.ShapeDtypeStruct((t, C_WIDTH), f32), jax.ShapeDtypeStruct((C_HEADS, t // CHUNK, hd, hd), f32)],
        scratch_shapes=[pltpu.VMEM((hp, hd, hd), f32)],
        compiler_params=_params(("parallel", "parallel", "arbitrary")), name=name,
    )(qg, kdec, w, u, attn, gl, z, ng)


def gdn_rec_bwd(qg, kdec, w, u, attn, gl, z, ng, states, dy, nb, *, name):
    t = qg.shape[0]
    s = t // nb
    sb = min(s, GDN_TIME_BLOCK)
    nsb = s // sb
    nc = sb // CHUNK
    hd = C_HEAD_DIM
    hp = C_HEADS_PER_STEP
    wide, attn_sp, gl_sp, ng_sp, st_sp = _gdn_rec_specs(sb, nsb, hp, True)

    def body(qg_ref, kd_ref, w_ref, u_ref, at_ref, gl_ref, z_ref, ng_ref, states, dy_ref,
             dqg_ref, dkd_ref, dw_ref, du_ref, dat_ref, dgl_ref, dz_ref, dng_ref, carry_ref):
        step = functools.partial(_gdn_heads_step, dots=VJP_DOTS)

        @pl.when(pl.program_id(2) == 0)
        def _():
            carry_ref[...] = jnp.zeros((hp, hd, hd), f32)

        def operands(n):
            rows = pl.ds(pl.multiple_of(n * CHUNK, CHUNK), CHUNK)
            grow = pl.ds(pl.multiple_of(n * SUBLANES, SUBLANES), SUBLANES)
            cols = [slice(j * hd, (j + 1) * hd) for j in range(hp)]
            return ([qg_ref[rows, c].astype(f32) for c in cols], [kd_ref[rows, c].astype(f32) for c in cols],
                    [w_ref[rows, c].astype(f32) for c in cols], [u_ref[rows, c] for c in cols],
                    [at_ref[j, rows, :].astype(f32) for j in range(hp)],
                    [gl_ref[j, grow, :][0:1, :] for j in range(hp)], [z_ref[rows, c] for c in cols])

        def bwd_chunk(i, carry):
            n = nc - 1 - i
            rows = pl.ds(pl.multiple_of(n * CHUNK, CHUNK), CHUNK)
            grow = pl.ds(pl.multiple_of(n * SUBLANES, SUBLANES), SUBLANES)
            dsts, dng = carry
            dys = [dy_ref[rows, j * hd:(j + 1) * hd] for j in range(hp)]
            _, vjp = jax.vjp(step, [states[j, n] for j in range(hp)], *operands(n), ng_ref[...])
            dst, dqg, dkd, dw, du, dat, dgl, dz, dng_n = vjp((dys, list(dsts)))
            for j in range(hp):
                cols = slice(j * hd, (j + 1) * hd)
                dqg_ref[rows, cols] = dqg[j]
                dkd_ref[rows, cols] = dkd[j]
                dw_ref[rows, cols] = dw[j]
                du_ref[rows, cols] = du[j]
                dat_ref[j, rows, :] = dat[j]
                dgl_ref[j, grow, :] = jnp.broadcast_to(dgl[j], (SUBLANES, hd))
                dz_ref[rows, cols] = dz[j]
            return tuple(dst), dng + dng_n

        dlast, dng = lax.fori_loop(0, nc, bwd_chunk,
                                   (tuple(carry_ref[j] for j in range(hp)), jnp.zeros((1, hd), f32)))
        for j in range(hp):
            carry_ref[j] = dlast[j]
        first = jnp.logical_and(jnp.logical_and(pl.program_id(0) == 0, pl.program_id(1) == 0), pl.program_id(2) == 0)

        @pl.when(first)
        def _():
            dng_ref[...] = dng

        @pl.when(jnp.logical_not(first))
        def _():
            dng_ref[...] += dng

    big = jax.ShapeDtypeStruct((t, C_WIDTH), f32)
    return pl.pallas_call(
        body, grid=(nb, C_HEADS // hp, nsb),
        in_specs=[wide] * 4 + [attn_sp, gl_sp, wide, ng_sp, st_sp, wide],
        out_specs=[wide] * 4 + [attn_sp, gl_sp, wide, ng_sp],
        out_shape=[big] * 4 + [jax.ShapeDtypeStruct(attn.shape, f32), jax.ShapeDtypeStruct(gl.shape, f32), big,
                               jax.ShapeDtypeStruct((1, hd), f32)],
        scratch_shapes=[pltpu.VMEM((hp, hd, hd), f32)],
        compiler_params=_params(("arbitrary", "arbitrary", "arbitrary")), name=name,
    )(qg, kdec, w, u, attn, gl, z, ng, states, dy)


def _blockdiag_slabs(w):
    per = GROUP_W // B_BLOCK
    slabs = jnp.zeros((B_BLOCKS // per, GROUP_W, GROUP_W), w.dtype)
    for h in range(B_BLOCKS):
        o = (h % per) * B_BLOCK
        slabs = slabs.at[h // per, o:o + B_BLOCK, o:o + B_BLOCK].set(w[h])
    return slabs


def _slab_blocks(slabs):
    per = GROUP_W // B_BLOCK
    return jnp.stack([slabs[h // per, (h % per) * B_BLOCK:(h % per + 1) * B_BLOCK,
                            (h % per) * B_BLOCK:(h % per + 1) * B_BLOCK] for h in range(B_BLOCKS)])


def _mixer_ab_fwd(x1, W, g, b, nb, tag):
    w_in = W["ab_w_in"][0].astype(bf16)
    o1, o2 = A_WIDTH + 2 * A_KV_WIDTH, A_WIDTH + 2 * A_KV_WIDTH + B_WIDTH
    w_qkv, w_bx, w_bg = w_in[:, :o1], w_in[:, o1:o2], w_in[:, o2:]
    pqkv = mm_nn(x1, w_qkv, name=tag + "_in_qkv")
    pbx = mm_nn(x1, w_bx, name=tag + "_in_bx")
    pbg = mm_nn(x1, w_bg, name=tag + "_in_bg")
    ya = attn_fwd(pqkv, W["a_sinks"], nb, name=tag + "_attn_fwd")
    xc = conv_fwd(pbx, W["b_conv_w"][0], W["b_conv_b"], False, nb, name=tag + "_conv_fwd")
    wa_s, wx_s = _blockdiag_slabs(W["b_wa"][0]), _blockdiag_slabs(W["b_wx"][0])
    yb, hh = rglru_fwd(xc, pbg, wa_s, wx_s, W["b_ba"], W["b_bx"], W["b_lam"], nb, name=tag + "_rglru_fwd")
    w_out = W["ab_w_out"][0].astype(bf16)
    x2, z1 = proj_ln([ya, yb], [w_out[:A_WIDTH], w_out[A_WIDTH:]], x1, g, b, name=tag + "_out_ln")
    saved = (pqkv, pbx, pbg, ya, xc, yb, hh, wa_s, wx_s, w_qkv, w_bx, w_bg, w_out)
    return x2, z1, saved


def _mixer_ab_bwd(x1, dz1, dz1b, W, saved, nb, tag):
    pqkv, pbx, pbg, ya, xc, yb, hh, wa_s, wx_s, w_qkv, w_bx, w_bg, w_out = saved
    dya = mm_nn(dz1b, w_out[:A_WIDTH].T, name=tag + "_dya")
    dyb = mm_nn(dz1b, w_out[A_WIDTH:].T, name=tag + "_dyb")
    dwo = jnp.concatenate([mm_tn(ya, dz1b, name=tag + "_dwo_a"), mm_tn(yb, dz1b, name=tag + "_dwo_b")], 0)
    dpqkv, dsinks = attn_bwd(pqkv, W["a_sinks"], dya, nb, name=tag + "_attn_bwd")
    dxc, dpbg, dwa_s, dwx_s, dba, dbx, dlam = rglru_bwd(xc, pbg, hh, dyb, wa_s, wx_s, W["b_ba"], W["b_bx"],
                                                       W["b_lam"], nb, name=tag + "_rglru_bwd")
    dpbx, dconv_w, dconv_b = conv_bwd(pbx, W["b_conv_w"][0], W["b_conv_b"], dxc, False, nb, name=tag + "_conv_bwd")
    dw_in = jnp.concatenate([mm_tn(x1, dpqkv, name=tag + "_dwin_qkv"), mm_tn(x1, dpbx, name=tag + "_dwin_bx"),
                             mm_tn(x1, dpbg, name=tag + "_dwin_bg")], 1)
    dx1 = mm_nn(dpqkv, w_qkv.T, add=dz1, add_scale=DN_ALPHA, name=tag + "_dx_qkv")
    dx1 = mm_nn(dpbx, w_bx.T, add=dx1, name=tag + "_dx_bx")
    dx1 = mm_nn(dpbg, w_bg.T, add=dx1, name=tag + "_dx_bg")
    grads = {"ab_w_in": dw_in[None], "a_sinks": dsinks, "b_conv_w": dconv_w[None], "b_conv_b": dconv_b,
             "b_wa": _slab_blocks(dwa_s)[None], "b_ba": dba, "b_wx": _slab_blocks(dwx_s)[None], "b_bx": dbx,
             "b_lam": dlam, "ab_w_out": dwo[None]}
    return dx1, grads


def _mixer_c_fwd(x1, W, g, b, nb, tag):
    w_in = W["c_w_in"][0].astype(bf16)
    d = w_in.shape[0]
    o1, o2 = 3 * C_WIDTH, 4 * C_WIDTH
    w_qkv, w_z = w_in[:, :o1], w_in[:, o1:o2]
    w_ba = jnp.concatenate([w_in[:, o2:], jnp.zeros((d, LANES - 2 * C_HEADS), bf16)], 1)
    pqkv = mm_nn(x1, w_qkv, name=tag + "_in_qkv")
    pz = mm_nn(x1, w_z, name=tag + "_in_z")
    pba = mm_nn(x1, w_ba, name=tag + "_in_ba")
    zero_b = jnp.zeros((1, o1), f32)
    qkvc = conv_fwd(pqkv, W["c_conv_w"][0], zero_b, True, nb, name=tag + "_conv_fwd")
    prep = gdn_prep_fwd(qkvc, pba, W["c_a_log"], W["c_dt_bias"], nb, name=tag + "_prep_fwd")
    yc, states = gdn_rec_fwd(*prep, pz, W["c_norm_g"], nb, name=tag + "_rec_fwd")
    w_out = W["c_w_out"][0].astype(bf16)
    x2, z1 = proj_ln([yc], [w_out], x1, g, b, name=tag + "_out_ln")
    saved = (pqkv, pz, pba, qkvc, prep, states, yc, w_qkv, w_z, w_ba, w_out, zero_b)
    return x2, z1, saved


def _mixer_c_bwd(x1, dz1, dz1b, W, saved, nb, tag):
    pqkv, pz, pba, qkvc, prep, states, yc, w_qkv, w_z, w_ba, w_out, zero_b = saved
    dyc = mm_nn(dz1b, w_out.T, name=tag + "_dyc")
    dwo = mm_tn(yc, dz1b, name=tag + "_dwo")
    rec = gdn_rec_bwd(*prep, pz, W["c_norm_g"], states, dyc, nb, name=tag + "_rec_bwd")
    cts, dpz, dng = rec[:6], rec[6], rec[7]
    dq, dk, dv, dpba, dalog, ddtb = gdn_prep_bwd(qkvc, pba, W["c_a_log"], W["c_dt_bias"], cts, nb,
                                                 name=tag + "_prep_bwd")
    dqkvc = jnp.concatenate([dq, dk, dv], 1)
    dpqkv, dconv_w, _ = conv_bwd(pqkv, W["c_conv_w"][0], zero_b, dqkvc, True, nb, name=tag + "_conv_bwd")
    dw_in = jnp.concatenate([mm_tn(x1, dpqkv, name=tag + "_dwin_qkv"), mm_tn(x1, dpz, name=tag + "_dwin_z"),
                             mm_tn(x1, dpba, name=tag + "_dwin_ba")[:, :2 * C_HEADS]], 1)
    dx1 = mm_nn(dpqkv, w_qkv.T, add=dz1, add_scale=DN_ALPHA, name=tag + "_dx_qkv")
    dx1 = mm_nn(dpz, w_z.T, add=dx1, name=tag + "_dx_z")
    dx1 = mm_nn(dpba, w_ba.T, add=dx1, name=tag + "_dx_ba")
    grads = {"c_w_in": dw_in[None], "c_conv_w": dconv_w[None], "c_a_log": dalog[:, :C_HEADS],
             "c_dt_bias": ddtb[:, :C_HEADS], "c_norm_g": dng, "c_w_out": dwo[None]}
    return dx1, grads


def _local_step(x, p, target, W, F):
    nb, s, d = x.shape
    t = nb * s
    h = x.reshape(t, d)
    tape = []
    f1 = [F[k] for k in ("ffn1_wg", "ffn1_wu", "ffn1_wd")]
    f2 = [F[k] for k in ("ffn2_wg", "ffn2_wu", "ffn2_wd")]
    for i in range(DEPTH):
        tag = f"l{i}"
        lg = [W["ln_g"][i, k][None] for k in range(3)]
        lb = [W["ln_b"][i, k][None] for k in range(3)]
        x1, z0 = ffn_fwd(h, *f1, i, lg[0], lb[0], name=tag + "_ffn1_fwd")
        mixer = _mixer_ab_fwd if i % 2 == 0 else _mixer_c_fwd
        x2, z1, msaved = mixer(x1, W, lg[1], lb[1], nb, tag + "_mix")
        x3, z2 = ffn_fwd(x2, *f2, i, lg[2], lb[2], name=tag + "_ffn2_fwd")
        pi = p[i].reshape(t, -1)
        pw = (W["ple_wg"][i].astype(bf16), W["ple_bg"][i][None], W["ple_wp"][i].astype(bf16))
        x4 = ple_fwd(x3, pi, *pw, name=tag + "_ple_fwd")
        tape.append((h, z0, x1, msaved, z1, x2, z2, x3, pi, pw, lg))
        h = x4
    dh, sq = loss_head(h, target.reshape(t, d), name="loss_head")
    loss = 0.5 * jnp.sum(sq) / d
    per_layer = [None] * DEPTH
    grads = {}
    df1 = df2 = None
    for i in reversed(range(DEPTH)):
        tag = f"l{i}"
        h_in, z0, x1, msaved, z1, x2, z2, x3, pi, pw, lg = tape[i]
        dx3, dple_wg, dple_bg, dple_wp = ple_bwd(x3, pi, dh, pw[0], pw[0].T, pw[1], pw[2], name=tag + "_ple_bwd")
        dz2, dz2b, dg2, db2 = ln_bwd(z2, dx3, lg[2], name=tag + "_ln2_bwd")
        dgate, dup, *df2 = ffn_bwd_weights(x2.astype(bf16), dz2b, *f2, i, df2, name=tag + "_ffn2_bwd_w")
        dx2 = ffn_bwd_input(dgate, dup, f2[0], f2[1], i, dz2, name=tag + "_ffn2_bwd_x")
        dz1, dz1b, dg1, db1 = ln_bwd(z1, dx2, lg[1], name=tag + "_ln1_bwd")
        mixer_bwd = _mixer_ab_bwd if i % 2 == 0 else _mixer_c_bwd
        dx1, mgrads = mixer_bwd(x1, dz1, dz1b, W, msaved, nb, tag + "_mix")
        grads.update(mgrads)
        dz0, dz0b, dg0, db0 = ln_bwd(z0, dx1, lg[0], name=tag + "_ln0_bwd")
        dgate, dup, *df1 = ffn_bwd_weights(h_in.astype(bf16), dz0b, *f1, i, df1, name=tag + "_ffn1_bwd_w")
        dh = ffn_bwd_input(dgate, dup, f1[0], f1[1], i, dz0, name=tag + "_ffn1_bwd_x")
        per_layer[i] = {"ln_g": jnp.concatenate([dg0, dg1, dg2], 0), "ln_b": jnp.concatenate([db0, db1, db2], 0),
                        "ple_wg": dple_wg, "ple_bg": dple_bg[0], "ple_wp": dple_wp}
    for k in per_layer[0]:
        grads[k] = jnp.stack([per_layer[i][k] for i in range(DEPTH)])
    grads.update(zip(("ffn1_wg", "ffn1_wu", "ffn1_wd"), df1))
    grads.update(zip(("ffn2_wg", "ffn2_wu", "ffn2_wd"), df2))
    return loss, dh.reshape(nb, s, d), grads


WEIGHT_NAMES = ("ffn1_wg", "ffn1_wu", "ffn1_wd", "ffn2_wg", "ffn2_wu", "ffn2_wd", "ln_g", "ln_b", "ple_wg", "ple_bg",
                "ple_wp", "ab_w_in", "a_sinks", "b_conv_w", "b_conv_b", "b_wa", "b_ba", "b_wx", "b_bx", "b_lam",
                "ab_w_out", "c_w_in", "c_conv_w", "c_a_log", "c_dt_bias", "c_norm_g", "c_w_out")
NATIVE_NAMES = WEIGHT_NAMES[:6]
PACKED_NAMES = WEIGHT_NAMES[6:]
SHARD_AXIS = {"ffn1_wg": 2, "ffn1_wu": 2, "ffn1_wd": 1, "ffn2_wg": 2, "ffn2_wu": 2, "ffn2_wd": 1, "ln_g": 2, "ln_b": 2,
              "ple_wg": 1, "ple_wp": 2, "ab_w_in": 2, "b_conv_w": 2, "ab_w_out": 1, "c_w_in": 2, "c_conv_w": 2,
              "c_w_out": 1}
N_CHIPS = 4
PACK_COLS = LANES
PACK_TILE_MULTIPLE = 256
ELEMENTWISE_BLOCK_ELEMS = 128 * 1024


def _row_tile(r, cols):
    return _tile(r, max(2 * SUBLANES, ELEMENTWISE_BLOCK_ELEMS // cols), 2 * SUBLANES)
MESH = pl.DeviceIdType.MESH
ANY = pl.BlockSpec(memory_space=pl.ANY)


def _tiled_dims(shape):
    w = shape[-1]
    r = 1
    for dim in shape[:-1]:
        r *= dim
    return r, w, -(-r // SUBLANES) * SUBLANES, -(-w // LANES) * LANES


def _pack(pieces, lead=()):
    k = len(lead)
    tiles = []
    for a in pieces:
        r, w, rp, wp = _tiled_dims(a.shape[k:])
        a2 = jnp.pad(a.reshape(lead + (r, w)), [(0, 0)] * k + [(0, rp - r), (0, wp - w)])
        a2 = a2.reshape(lead + (rp // SUBLANES, SUBLANES, wp // LANES, LANES))
        a2 = jnp.swapaxes(a2, k + 1, k + 2)
        tiles.append(a2.reshape(lead + (-1, SUBLANES, LANES)))
    flat = jnp.concatenate(tiles, axis=k)
    n = flat.shape[k]
    n_pad = -(-n // PACK_TILE_MULTIPLE) * PACK_TILE_MULTIPLE
    flat = jnp.pad(flat, [(0, 0)] * k + [(0, n_pad - n), (0, 0), (0, 0)])
    return flat.reshape(lead + (n_pad * SUBLANES, PACK_COLS))


def _unpack(pack, shapes, lead=()):
    k = len(lead)
    flat = pack.reshape(lead + (-1, SUBLANES, LANES))
    out, o = [], 0
    for shp in shapes:
        r, w, rp, wp = _tiled_dims(shp)
        n = (rp // SUBLANES) * (wp // LANES)
        a2 = lax.slice_in_dim(flat, o, o + n, axis=k).reshape(lead + (rp // SUBLANES, wp // LANES, SUBLANES, LANES))
        a2 = jnp.swapaxes(a2, k + 1, k + 2).reshape(lead + (rp, wp))
        a2 = lax.slice_in_dim(lax.slice_in_dim(a2, 0, r, axis=k), 0, w, axis=k + 1)
        out.append(a2.reshape(lead + tuple(shp)))
        o += n
    return out


def _mesh_position():
    x, y, c = lax.axis_index("x"), lax.axis_index("y"), lax.axis_index("c")
    chips = [(1 - x, y), (x, 1 - y), (1 - x, 1 - y)]
    return x, y, c, chips


def _remote(src, dst, send_sems, recv_sems, k, to):
    return pltpu.make_async_remote_copy(src_ref=src, dst_ref=dst, send_sem=send_sems.at[k], recv_sem=recv_sems.at[k],
                                        device_id=to, device_id_type=MESH)


def _sems(n):
    return pltpu.SemaphoreType.DMA((n,))


def place_slot(parts, slot, n_slots, dtype, from_slot, *, name):
    n = len(parts)
    r, cols = parts[0].shape[-2:]
    tr = _row_tile(r, cols)

    def body(s_ref, *refs):
        for a in range(n):
            refs[n + a][...] = refs[a][...].astype(dtype)

    dst = pl.BlockSpec((None, tr, cols), lambda i, s_ref: (s_ref[0], i, 0))
    src = dst if from_slot else pl.BlockSpec((tr, cols), lambda i, s_ref: (i, 0))
    return pl.pallas_call(
        body,
        grid_spec=pltpu.PrefetchScalarGridSpec(num_scalar_prefetch=1, grid=(r // tr,), in_specs=[src] * n,
                                               out_specs=[dst] * n),
        out_shape=[jax.ShapeDtypeStruct((n_slots, r, cols), dtype)] * n,
        compiler_params=_params(("parallel",)), name=name,
    )(slot, *parts)


def gather_shards(bufs, *, name):
    n = len(bufs)

    def body(*refs):
        out_refs = refs[n:2 * n]
        send_sems, recv_sems = refs[2 * n:]
        x, y, c, chips = _mesh_position()
        me = 2 * x + y
        sibling = (x, y, 1 - c)
        waits = []
        for j, (cx, cy) in enumerate(chips):
            for a in range(n):
                own = out_refs[a].at[me, c]
                cp = _remote(own, own, send_sems, recv_sems, 6 * a + j, (cx, cy, c))
                cp.start()
                waits.append(cp.wait_send)
        for j, (cx, cy) in enumerate(chips):
            for a in range(n):
                got = out_refs[a].at[2 * cx + cy, c]
                _remote(got, got, send_sems, recv_sems, 6 * a + j, (cx, cy, c)).wait_recv()
                fw = _remote(got, got, send_sems, recv_sems, 6 * a + 3 + j, sibling)
                fw.start()
                waits.append(fw.wait_send)
        for j, (cx, cy) in enumerate(chips):
            for a in range(n):
                got = out_refs[a].at[2 * cx + cy, 1 - c]
                _remote(got, got, send_sems, recv_sems, 6 * a + 3 + j, sibling).wait_recv()
        for wait in waits:
            wait()

    return pl.pallas_call(
        body, out_shape=[jax.ShapeDtypeStruct(b.shape, b.dtype) for b in bufs],
        in_specs=[ANY] * n, out_specs=[ANY] * n, scratch_shapes=[_sems(6 * n), _sems(6 * n)],
        input_output_aliases={a: a for a in range(n)}, name=name,
    )(*bufs)


def sibling_exchange(gs, *, name):
    n = len(gs)

    def body(*refs):
        g_refs, out_refs = refs[:n], refs[n:2 * n]
        send_sems, recv_sems = refs[2 * n:]
        x, y, c, _ = _mesh_position()
        cps = [_remote(g_refs[a].at[:, 1 - c], out_refs[a], send_sems, recv_sems, a, (x, y, 1 - c)) for a in range(n)]
        for cp in cps:
            cp.start()
        for cp in cps:
            cp.wait()

    return pl.pallas_call(
        body, out_shape=[jax.ShapeDtypeStruct(g.shape[:1] + g.shape[2:], g.dtype) for g in gs],
        in_specs=[ANY] * n, out_specs=[ANY] * n, scratch_shapes=[_sems(n), _sems(n)], name=name,
    )(*gs)


def add_own_half(gs, others, c_idx, dtype, *, name):
    n = len(gs)
    ns, _, r, cols = gs[0].shape
    tr = _row_tile(r, cols)

    def body(c_ref, *refs):
        for a in range(n):
            refs[2 * n + a][...] = (refs[a][...] + refs[n + a][...]).astype(dtype)

    own = pl.BlockSpec((None, None, tr, cols), lambda s, i, c_ref: (s, c_ref[0], i, 0))
    oth = pl.BlockSpec((None, tr, cols), lambda s, i, c_ref: (s, i, 0))
    return pl.pallas_call(
        body,
        grid_spec=pltpu.PrefetchScalarGridSpec(num_scalar_prefetch=1, grid=(ns, r // tr),
                                               in_specs=[own] * n + [oth] * n, out_specs=[oth] * n),
        out_shape=[jax.ShapeDtypeStruct((ns, r, cols), dtype)] * n,
        compiler_params=_params(("parallel", "parallel")), name=name,
    )(c_idx, *gs, *others)


def chip_exchange(ps, qs, *, name):
    n = len(ps)

    def body(*refs):
        p_refs, q_refs = refs[:n], refs[2 * n:3 * n]
        send_sems, recv_sems = refs[3 * n:]
        x, y, c, chips = _mesh_position()
        me = 2 * x + y
        waits = []
        for j, (cx, cy) in enumerate(chips):
            for a in range(n):
                cp = _remote(p_refs[a].at[2 * cx + cy], q_refs[a].at[me], send_sems, recv_sems, 3 * a + j, (cx, cy, c))
                cp.start()
                waits.append(cp.wait_send)
        for j, (cx, cy) in enumerate(chips):
            for a in range(n):
                got = q_refs[a].at[2 * cx + cy]
                _remote(got, got, send_sems, recv_sems, 3 * a + j, (cx, cy, c)).wait_recv()
        for wait in waits:
            wait()

    return pl.pallas_call(
        body, out_shape=[jax.ShapeDtypeStruct(q_.shape, q_.dtype) for q_ in qs], in_specs=[ANY] * (2 * n),
        out_specs=[ANY] * n, scratch_shapes=[_sems(3 * n), _sems(3 * n)],
        input_output_aliases={n + a: a for a in range(n)}, name=name,
    )(*ps, *qs)


def sum_slots(qs, *, name):
    n = len(qs)
    ns, r, cols = qs[0].shape
    tr = _row_tile(r, cols * ns)

    def body(*refs):
        for a in range(n):
            q_ref = refs[a]
            acc = q_ref[0].astype(f32) + q_ref[1].astype(f32)
            for i in range(2, ns):
                acc = acc + q_ref[i].astype(f32)
            refs[n + a][...] = acc

    return pl.pallas_call(
        body, grid=(r // tr,), in_specs=[pl.BlockSpec((ns, tr, cols), lambda i: (0, i, 0))] * n,
        out_specs=[pl.BlockSpec((tr, cols), lambda i: (i, 0))] * n,
        out_shape=[jax.ShapeDtypeStruct((r, cols), f32)] * n,
        compiler_params=_params(("parallel",)), name=name,
    )(*qs)


def sibling_share(bufs, *, name):
    n = len(bufs)

    def body(*refs):
        out_refs = refs[n:2 * n]
        send_sems, recv_sems = refs[2 * n:]
        x, y, c, _ = _mesh_position()
        sibling = (x, y, 1 - c)
        cps = []
        for a in range(n):
            own = out_refs[a].at[c]
            cp = _remote(own, own, send_sems, recv_sems, a, sibling)
            cp.start()
            cps.append(cp)
        for a in range(n):
            theirs = out_refs[a].at[1 - c]
            _remote(theirs, theirs, send_sems, recv_sems, a, sibling).wait_recv()
        for cp in cps:
            cp.wait_send()

    return pl.pallas_call(
        body, out_shape=[jax.ShapeDtypeStruct(b.shape, b.dtype) for b in bufs], in_specs=[ANY] * n,
        out_specs=[ANY] * n, scratch_shapes=[_sems(n), _sems(n)],
        input_output_aliases={a: a for a in range(n)}, name=name,
    )(*bufs)


def adamw(ws, gs, ms, vs, *, name):
    n = len(ws)
    r, cols = ws[0].shape
    tr = _row_tile(r, cols)

    def body(*refs):
        for a in range(n):
            w_ref, g_ref, m_ref, v_ref = (refs[k * n + a] for k in range(4))
            d_ref, m2_ref, v2_ref = (refs[(4 + k) * n + a] for k in range(3))
            g_ = g_ref[...]
            m2 = ADAM_B1 * m_ref[...] + (1.0 - ADAM_B1) * g_
            v2 = ADAM_B2 * v_ref[...] + (1.0 - ADAM_B2) * (g_ * g_)
            m_hat = m2 / (1.0 - ADAM_B1 ** ADAM_STEP)
            v_hat = v2 / (1.0 - ADAM_B2 ** ADAM_STEP)
            d_ref[...] = -ADAM_LR * (m_hat / (jnp.sqrt(v_hat) + ADAM_EPS) + ADAM_WD * w_ref[...])
            m2_ref[...] = m2
            v2_ref[...] = v2

    row = pl.BlockSpec((tr, cols), lambda i: (i, 0))
    out = pl.pallas_call(
        body, grid=(r // tr,), in_specs=[row] * (4 * n), out_specs=[row] * (3 * n),
        out_shape=[jax.ShapeDtypeStruct((r, cols), f32)] * (3 * n),
        compiler_params=_params(("parallel",)), name=name,
    )(*ws, *gs, *ms, *vs)
    return out[:n], out[n:2 * n], out[2 * n:]


def _full_weights(gathered, local, shapes):
    pieces = _unpack(gathered, shapes, lead=(N_CHIPS,))
    full = {}
    for name, loc, pc in zip(PACKED_NAMES, local, pieces):
        ax = SHARD_AXIS.get(name)
        full[name] = loc if ax is None else jnp.concatenate([pc[s] for s in range(N_CHIPS)], axis=ax)
    return full


def _grad_pack(grads, shapes):
    pieces = []
    for name, shp in zip(PACKED_NAMES, shapes):
        g = grads[name]
        ax = SHARD_AXIS.get(name)
        if ax is None:
            pieces.append(jnp.broadcast_to(g.reshape(shp)[None], (N_CHIPS,) + tuple(shp)))
        else:
            pieces.append(jnp.stack(jnp.split(g, N_CHIPS, axis=ax)))
    return _pack(pieces, lead=(N_CHIPS,))


def _by_shape(arrays):
    groups = {}
    for i, a in enumerate(arrays):
        groups.setdefault(a.shape, []).append(i)
    return list(groups.values())


def _grouped(fn, lists, n_out, tag):
    outs = [[None] * len(lists[0]) for _ in range(n_out)]
    for gi, idx in enumerate(_by_shape(lists[0])):
        res = fn(*[[lst[i] for i in idx] for lst in lists], name=f"{tag}_{gi}")
        res = res if n_out > 1 else (res,)
        for k in range(n_out):
            for i, r in zip(idx, res[k]):
                outs[k][i] = r
    return outs if n_out > 1 else outs[0]


def _train_step(x, p, loss_target, weights, m, v):
    packed_w = [weights[k] for k in PACKED_NAMES]
    shapes = [w.shape for w in packed_w]
    halves = lambda a: a.reshape((2, a.shape[0] // 2) + a.shape[1:])
    local = [weights[k] for k in NATIVE_NAMES] + [halves(_pack(packed_w))]
    local_m = [m[k] for k in NATIVE_NAMES] + [halves(_pack([m[k] for k in PACKED_NAMES]))]
    local_v = [v[k] for k in NATIVE_NAMES] + [halves(_pack([v[k] for k in PACKED_NAMES]))]
    flat = lambda lst: [a.reshape((-1, a.shape[-1])) for a in lst]
    c_idx = lax.axis_index("c").astype(jnp.int32).reshape(1)
    chip_idx = (2 * lax.axis_index("x") + lax.axis_index("y")).astype(jnp.int32).reshape(1)

    def placed(arrays, slot, n_slots, dtype, from_slot, tag):
        return _grouped(lambda a, name: place_slot(a, slot, n_slots, dtype, from_slot, name=name), [arrays], 1, tag)

    bufs = placed(flat(local[:-1]), chip_idx, N_CHIPS, bf16, False, "place_ffn_weights")
    bufs += placed(flat(local[-1:]), chip_idx, N_CHIPS, f32, False, "place_packed_weights")
    bufs = [b.reshape((N_CHIPS,) + a.shape) for b, a in zip(bufs, local)]
    gathered = gather_shards(bufs, name="comm_gather_weights")
    full = _full_weights(gathered[-1], packed_w, shapes)
    loss, grad_x, grads = _local_step(x, p, loss_target, full, dict(zip(NATIVE_NAMES, gathered[:-1])))
    gpack = _grad_pack(grads, shapes)
    gs = [grads[k] for k in NATIVE_NAMES] + [gpack.reshape((N_CHIPS,) + local[-1].shape)]
    others = sibling_exchange(gs, name="comm_grad_sibling")
    nn_ = len(NATIVE_NAMES)
    chip_sums = _grouped(lambda a, b, name: add_own_half(a, b, c_idx, bf16, name=name), [gs[:nn_], others[:nn_]], 1,
                         "grad_add_sibling_ffn")
    chip_sums += add_own_half(gs[nn_:], others[nn_:], c_idx, f32, name="grad_add_sibling_packed")
    own = placed(chip_sums[:nn_], chip_idx, N_CHIPS, bf16, True, "place_own_partial_ffn")
    own += placed(chip_sums[nn_:], chip_idx, N_CHIPS, f32, True, "place_own_partial_packed")
    slots = chip_exchange(chip_sums, own, name="comm_grad_chips")
    mine = _grouped(sum_slots, [slots], 1, "grad_sum_chips")
    gsum = sibling_share(placed(mine, c_idx, 2, f32, False, "place_own_half"), name="comm_grad_share")
    delta, m2, v2 = _grouped(adamw, [flat(local), flat(gsum), flat(local_m), flat(local_v)], 3, "adamw")
    loss = lax.psum(loss, ("x", "y", "c"))
    outs = []
    for res in (gsum, delta, m2, v2):
        by_name = {k: a.reshape(weights[k].shape) for k, a in zip(NATIVE_NAMES, res[:-1])}
        by_name.update(zip(PACKED_NAMES, _unpack(res[-1], shapes)))
        outs += [by_name[k] for k in WEIGHT_NAMES]
    return (loss, grad_x, *outs)


def kernel(x, p, ffn1_wg, ffn1_wu, ffn1_wd, ffn2_wg, ffn2_wu, ffn2_wd, ln_g, ln_b, ple_wg, ple_bg, ple_wp, ab_w_in, a_sinks, b_conv_w, b_conv_b, b_wa, b_ba, b_wx, b_bx, b_lam, ab_w_out, c_w_in, c_conv_w, c_a_log, c_dt_bias, c_norm_g, c_w_out, loss_target, m_ffn1_wg, m_ffn1_wu, m_ffn1_wd, m_ffn2_wg, m_ffn2_wu, m_ffn2_wd, m_ln_g, m_ln_b, m_ple_wg, m_ple_bg, m_ple_wp, m_ab_w_in, m_a_sinks, m_b_conv_w, m_b_conv_b, m_b_wa, m_b_ba, m_b_wx, m_b_bx, m_b_lam, m_ab_w_out, m_c_w_in, m_c_conv_w, m_c_a_log, m_c_dt_bias, m_c_norm_g, m_c_w_out, v_ffn1_wg, v_ffn1_wu, v_ffn1_wd, v_ffn2_wg, v_ffn2_wu, v_ffn2_wd, v_ln_g, v_ln_b, v_ple_wg, v_ple_bg, v_ple_wp, v_ab_w_in, v_a_sinks, v_b_conv_w, v_b_conv_b, v_b_wa, v_b_ba, v_b_wx, v_b_bx, v_b_lam, v_ab_w_out, v_c_w_in, v_c_conv_w, v_c_a_log, v_c_dt_bias, v_c_norm_g, v_c_w_out):
    weights = [ffn1_wg, ffn1_wu, ffn1_wd, ffn2_wg, ffn2_wu, ffn2_wd, ln_g, ln_b, ple_wg, ple_bg, ple_wp, ab_w_in, a_sinks,
               b_conv_w, b_conv_b, b_wa, b_ba, b_wx, b_bx, b_lam, ab_w_out, c_w_in, c_conv_w, c_a_log, c_dt_bias, c_norm_g,
               c_w_out]
    m = [m_ffn1_wg, m_ffn1_wu, m_ffn1_wd, m_ffn2_wg, m_ffn2_wu, m_ffn2_wd, m_ln_g, m_ln_b, m_ple_wg, m_ple_bg, m_ple_wp,
         m_ab_w_in, m_a_sinks, m_b_conv_w, m_b_conv_b, m_b_wa, m_b_ba, m_b_wx, m_b_bx, m_b_lam, m_ab_w_out, m_c_w_in,
         m_c_conv_w, m_c_a_log, m_c_dt_bias, m_c_norm_g, m_c_w_out]
    v = [v_ffn1_wg, v_ffn1_wu, v_ffn1_wd, v_ffn2_wg, v_ffn2_wu, v_ffn2_wd, v_ln_g, v_ln_b, v_ple_wg, v_ple_bg, v_ple_wp,
         v_ab_w_in, v_a_sinks, v_b_conv_w, v_b_conv_b, v_b_wa, v_b_ba, v_b_wx, v_b_bx, v_b_lam, v_ab_w_out, v_c_w_in,
         v_c_conv_w, v_c_a_log, v_c_dt_bias, v_c_norm_g, v_c_w_out]
    return _train_step(x, p, loss_target, dict(zip(WEIGHT_NAMES, weights)), dict(zip(WEIGHT_NAMES, m)),
                       dict(zip(WEIGHT_NAMES, v)))
```

```python
import functools

import jax
import jax.numpy as jnp
from jax import lax
from jax.experimental import pallas as pl
from jax.experimental.pallas import tpu as pltpu

f32 = jnp.float32
bf16 = jnp.bfloat16

DEPTH = 2
CHUNK = 64
A_HEADS, A_KV_HEADS, A_GROUP, A_HEAD_DIM = 8, 2, 4, 64
A_WIDTH, A_KV_WIDTH, A_WINDOW = 512, 128, 128
B_WIDTH, B_BLOCKS, B_BLOCK, B_CONV = 512, 8, 64, 4
RG_C = 8.0
C_HEADS, C_HEAD_DIM, C_WIDTH, C_CONV = 8, 128, 1024, 4
DN_ALPHA = (2.0 * DEPTH) ** 0.25
LN_EPS = 1e-5
NORM_EPS = 1e-6
NEG = -1e30
ADAM_LR, ADAM_B1, ADAM_B2, ADAM_EPS, ADAM_WD, ADAM_STEP = 0.001, 0.9, 0.999, 1e-08, 0.01, 10

VMEM_LIMIT_BYTES = 56 * 1024 * 1024
LANES = 128
SUBLANES = 8
GROUP_W = 128
PREP_FWD_UNROLL = 8
PREP_BWD_UNROLL = 8
C_HEADS_PER_STEP = 4
GDN_TIME_BLOCK = 512

NN = ((1,), (0,))
NT = ((1,), (1,))
TN = ((0,), (0,))


def _params(sem):
    return pltpu.CompilerParams(dimension_semantics=sem, vmem_limit_bytes=VMEM_LIMIT_BYTES)


def _tile(n, cap, mult):
    best = None
    t = mult
    while t <= min(n, cap):
        if n % t == 0:
            best = t
        t += mult
    return best if best is not None else n


def _bdot(a, b, dims):
    return lax.dot_general(a.astype(bf16), b.astype(bf16), (dims, ((), ())), preferred_element_type=f32)


def _running_sum(x, reverse):
    s = x.shape[0]
    t = lax.broadcasted_iota(jnp.int32, x.shape, 0)
    d = 1
    while d < s:
        if reverse:
            x = x + jnp.where(t < s - d, pltpu.roll(x, s - d, 0), 0.0)
        else:
            x = x + jnp.where(t >= d, pltpu.roll(x, d, 0), 0.0)
        d *= 2
    return x


@jax.custom_vjp
def _cumsum0(x):
    return _running_sum(x, False)


def _cumsum0_fwd(x):
    return _running_sum(x, False), None


def _cumsum0_bwd(_, g):
    return (_running_sum(g, True),)


_cumsum0.defvjp(_cumsum0_fwd, _cumsum0_bwd)


@jax.custom_vjp
def _bnn(a, b):
    return _bdot(a, b, NN)


def _bnn_fwd(a, b):
    return _bdot(a, b, NN), (a, b)


def _bnn_bwd(res, g):
    a, b = res
    return _bdot(g, b, NT), _bdot(a, g, TN)


_bnn.defvjp(_bnn_fwd, _bnn_bwd)


@jax.custom_vjp
def _bnt(a, b):
    return _bdot(a, b, NT)


def _bnt_fwd(a, b):
    return _bdot(a, b, NT), (a, b)


def _bnt_bwd(res, g):
    a, b = res
    return _bdot(g, b, NN), _bdot(g, a, TN)


_bnt.defvjp(_bnt_fwd, _bnt_bwd)


@jax.custom_vjp
def _btn(a, b):
    return _bdot(a, b, TN)


def _btn_fwd(a, b):
    return _bdot(a, b, TN), (a, b)


def _btn_bwd(res, g):
    a, b = res
    return _bdot(b, g, NT), _bdot(a, g, NN)


_btn.defvjp(_btn_fwd, _btn_bwd)

RAW_DOTS = (lambda a, b: _bdot(a, b, NN), lambda a, b: _bdot(a, b, NT), lambda a, b: _bdot(a, b, TN),
            lambda x: _running_sum(x, False))
VJP_DOTS = (_bnn, _bnt, _btn, _cumsum0)


def _layer_norm(z, g, b):
    mu = jnp.mean(z, -1, keepdims=True)
    d = z - mu
    var = jnp.mean(d * d, -1, keepdims=True)
    return d * lax.rsqrt(var + LN_EPS) * g + b


def _silu(x):
    return x * jax.nn.sigmoid(x)


def mm_nn(a, w, add=None, add_scale=1.0, *, name):
    m, k = a.shape
    n = w.shape[1]
    tm = _tile(m, 512, SUBLANES)
    tn = _tile(n, 1024, LANES)

    def body(*refs):
        if add is None:
            a_ref, w_ref, o_ref = refs
            o_ref[...] = _bdot(a_ref[...], w_ref[...], NN)
        else:
            a_ref, w_ref, add_ref, o_ref = refs
            o_ref[...] = _bdot(a_ref[...], w_ref[...], NN) + add_scale * add_ref[...]

    in_specs = [pl.BlockSpec((tm, k), lambda i, j: (i, 0)), pl.BlockSpec((k, tn), lambda i, j: (0, j))]
    args = [a, w]
    if add is not None:
        in_specs.append(pl.BlockSpec((tm, tn), lambda i, j: (i, j)))
        args.append(add)
    return pl.pallas_call(
        body, grid=(m // tm, n // tn), in_specs=in_specs,
        out_specs=pl.BlockSpec((tm, tn), lambda i, j: (i, j)),
        out_shape=jax.ShapeDtypeStruct((m, n), f32),
        compiler_params=_params(("parallel", "parallel")), name=name,
    )(*args)


def mm_tn(a, b, *, name):
    m, k = a.shape
    n = b.shape[1]
    tm = _tile(m, 1024, 2 * SUBLANES)
    tn = _tile(n, 1024, LANES)

    def body(a_ref, b_ref, o_ref):
        part = _bdot(a_ref[...], b_ref[...], TN)

        @pl.when(pl.program_id(1) == 0)
        def _():
            o_ref[...] = part

        @pl.when(pl.program_id(1) > 0)
        def _():
            o_ref[...] += part

    return pl.pallas_call(
        body, grid=(n // tn, m // tm),
        in_specs=[pl.BlockSpec((tm, k), lambda j, i: (i, 0)), pl.BlockSpec((tm, tn), lambda j, i: (i, j))],
        out_specs=pl.BlockSpec((k, tn), lambda j, i: (0, j)),
        out_shape=jax.ShapeDtypeStruct((k, n), f32),
        compiler_params=_params(("parallel", "arbitrary")), name=name,
    )(a, b)


def proj_ln(a_list, w_list, xres, g, b, *, name):
    t, d = xres.shape
    tm = _tile(t, 256, SUBLANES)
    na = len(a_list)

    def body(*refs):
        a_refs, w_refs = refs[:na], refs[na:2 * na]
        x_ref, g_ref, b_ref, y_ref, z_ref = refs[2 * na:]
        z = DN_ALPHA * x_ref[...]
        for a_ref, w_ref in zip(a_refs, w_refs):
            z = z + _bdot(a_ref[...], w_ref[...], NN)
        z_ref[...] = z
        y_ref[...] = _layer_norm(z, g_ref[...], b_ref[...])

    in_specs = [pl.BlockSpec((tm, a.shape[1]), lambda i: (i, 0)) for a in a_list]
    in_specs += [pl.BlockSpec(w.shape, lambda i: (0, 0)) for w in w_list]
    in_specs += [pl.BlockSpec((tm, d), lambda i: (i, 0)), pl.BlockSpec((1, d), lambda i: (0, 0)),
                 pl.BlockSpec((1, d), lambda i: (0, 0))]
    return pl.pallas_call(
        body, grid=(t // tm,), in_specs=in_specs,
        out_specs=[pl.BlockSpec((tm, d), lambda i: (i, 0))] * 2,
        out_shape=[jax.ShapeDtypeStruct((t, d), f32)] * 2,
        compiler_params=_params(("parallel",)), name=name,
    )(*a_list, *w_list, xres, g, b)


def ln_bwd(z, dy, g, *, name):
    t, d = z.shape
    tm = _tile(t, 512, SUBLANES)

    def body(z_ref, dy_ref, g_ref, dz_ref, dzb_ref, dg_ref, db_ref):
        zz = z_ref[...]
        dy_ = dy_ref[...]
        mu = jnp.mean(zz, -1, keepdims=True)
        dd = zz - mu
        var = jnp.mean(dd * dd, -1, keepdims=True)
        rstd = lax.rsqrt(var + LN_EPS)
        xhat = dd * rstd
        dxh = dy_ * g_ref[...]
        dz = rstd * (dxh - jnp.mean(dxh, -1, keepdims=True) - xhat * jnp.mean(dxh * xhat, -1, keepdims=True))
        dz_ref[...] = dz
        dzb_ref[...] = dz.astype(bf16)
        pg = jnp.sum(dy_ * xhat, 0, keepdims=True)
        pb = jnp.sum(dy_, 0, keepdims=True)

        @pl.when(pl.program_id(0) == 0)
        def _():
            dg_ref[...] = pg
            db_ref[...] = pb

        @pl.when(pl.program_id(0) > 0)
        def _():
            dg_ref[...] += pg
            db_ref[...] += pb

    row = pl.BlockSpec((tm, d), lambda i: (i, 0))
    vec = pl.BlockSpec((1, d), lambda i: (0, 0))
    return pl.pallas_call(
        body, grid=(t // tm,), in_specs=[row, row, vec], out_specs=[row, row, vec, vec],
        out_shape=[jax.ShapeDtypeStruct((t, d), f32), jax.ShapeDtypeStruct((t, d), bf16),
                   jax.ShapeDtypeStruct((1, d), f32), jax.ShapeDtypeStruct((1, d), f32)],
        compiler_params=_params(("arbitrary",)), name=name,
    )(z, dy, g)


def loss_head(y, target, *, name):
    t, d = y.shape
    tm = _tile(t, 512, SUBLANES)

    def body(y_ref, t_ref, dy_ref, sq_ref):
        e = y_ref[...] - t_ref[...]
        dy_ref[...] = e * (1.0 / d)
        part = jnp.sum(e * e, 0, keepdims=True)

        @pl.when(pl.program_id(0) == 0)
        def _():
            sq_ref[...] = part

        @pl.when(pl.program_id(0) > 0)
        def _():
            sq_ref[...] += part

    row = pl.BlockSpec((tm, d), lambda i: (i, 0))
    vec = pl.BlockSpec((1, d), lambda i: (0, 0))
    return pl.pallas_call(
        body, grid=(t // tm,), in_specs=[row, row], out_specs=[row, vec],
        out_shape=[jax.ShapeDtypeStruct((t, d), f32), jax.ShapeDtypeStruct((1, d), f32)],
        compiler_params=_params(("arbitrary",)), name=name,
    )(y, target)


FFN_COL_BLOCK = 256
FFN_ROWS = 1024


def _lane_blocks(n):
    return [slice(s, min(s + FFN_COL_BLOCK, n)) for s in range(0, n, FFN_COL_BLOCK)]


def ffn_fwd(x, wg, wu, wd, layer, g, b, *, name):
    t, d = x.shape
    nf, _, _, tf = wg.shape
    tm = _tile(t, FFN_ROWS, SUBLANES)

    def body(x_ref, wg_ref, wu_ref, wd_ref, g_ref, b_ref, y_ref, z_ref, yb_ref, acc_ref):
        f = pl.program_id(1)
        xb = x_ref[...].astype(bf16)
        part, pending = None, None
        for cols in _lane_blocks(tf):
            gate_up = (_bdot(xb, wg_ref[:, cols], NN), _bdot(xb, wu_ref[:, cols], NN), cols)
            if pending is not None:
                down = _bdot(_silu(pending[0]) * pending[1], wd_ref[pending[2], :], NN)
                part = down if part is None else part + down
            pending = gate_up
        down = _bdot(_silu(pending[0]) * pending[1], wd_ref[pending[2], :], NN)
        part = down if part is None else part + down

        @pl.when(f == 0)
        def _():
            acc_ref[...] = part

        @pl.when(f > 0)
        def _():
            acc_ref[...] += part

        @pl.when(f == nf - 1)
        def _():
            z = DN_ALPHA * x_ref[...] + 0.5 * acc_ref[...]
            z_ref[...] = z
            y = _layer_norm(z, g_ref[...], b_ref[...])
            y_ref[...] = y
            yb_ref[...] = y.astype(bf16)

    row = pl.BlockSpec((tm, d), lambda i, j: (i, 0))
    vec = pl.BlockSpec((1, d), lambda i, j: (0, 0))
    wcol = pl.BlockSpec((None, None, d, tf), lambda i, j: (j, layer, 0, 0))
    wrow = pl.BlockSpec((None, None, tf, d), lambda i, j: (j, layer, 0, 0))
    return pl.pallas_call(
        body, grid=(t // tm, nf),
        in_specs=[row, wcol, wcol, wrow, vec, vec],
        out_specs=[row, row, row],
        out_shape=[jax.ShapeDtypeStruct((t, d), f32)] * 2 + [jax.ShapeDtypeStruct((t, d), bf16)],
        scratch_shapes=[pltpu.VMEM((tm, d), f32)],
        compiler_params=_params(("parallel", "arbitrary")), name=name,
    )(x, wg, wu, wd, g, b)


def ffn_bwd_weights(xb, dzb, wg, wu, wd, layer, acc, *, name):
    t, d = xb.shape
    nf, nl, _, tf = wg.shape
    tm = _tile(t, FFN_ROWS, SUBLANES)

    def body(x_ref, dz_ref, wg_ref, wu_ref, wd_ref, *rest):
        dgate_ref, dup_ref, dwg_ref, dwu_ref, dwd_ref = rest[-5:]
        x = x_ref[...]
        dzh = dz_ref[...] * 0.5

        def first_half(cols):
            return _bdot(x, wg_ref[:, cols], NN), _bdot(x, wu_ref[:, cols], NN), _bdot(dzh, wd_ref[cols, :], NT), cols

        def second_half(gate, up, dh, cols):
            sg = jax.nn.sigmoid(gate)
            s = gate * sg
            dup = (dh * s).astype(bf16)
            dgate = (dh * up * (sg * (1.0 + gate * (1.0 - sg)))).astype(bf16)
            dgate_ref[:, cols] = dgate
            dup_ref[:, cols] = dup
            return _bdot(x, dgate, TN), _bdot(x, dup, TN), _bdot(s * up, dzh, TN), cols

        parts, pending = [], None
        for cols in _lane_blocks(tf):
            nxt = first_half(cols)
            if pending is not None:
                parts.append(second_half(*pending))
            pending = nxt
        parts.append(second_half(*pending))

        @pl.when(pl.program_id(1) == 0)
        def _():
            for pwg, pwu, pwd, cols in parts:
                dwg_ref[:, cols] = pwg
                dwu_ref[:, cols] = pwu
                dwd_ref[cols, :] = pwd

        @pl.when(pl.program_id(1) > 0)
        def _():
            for pwg, pwu, pwd, cols in parts:
                dwg_ref[:, cols] += pwg
                dwu_ref[:, cols] += pwu
                dwd_ref[cols, :] += pwd

    row = pl.BlockSpec((tm, d), lambda j, i: (i, 0))
    wcol = pl.BlockSpec((None, None, d, tf), lambda j, i: (j, layer, 0, 0))
    wrow = pl.BlockSpec((None, None, tf, d), lambda j, i: (j, layer, 0, 0))
    act = pl.BlockSpec((None, tm, tf), lambda j, i: (j, i, 0))
    in_specs = [row, row, wcol, wcol, wrow]
    args = [xb, dzb, wg, wu, wd]
    aliases = {}
    if acc is not None:
        in_specs += [pl.BlockSpec(memory_space=pl.ANY)] * 3
        args += list(acc)
        aliases = {5: 2, 6: 3, 7: 4}
    return pl.pallas_call(
        body, grid=(nf, t // tm), in_specs=in_specs, out_specs=[act, act, wcol, wcol, wrow],
        out_shape=[jax.ShapeDtypeStruct((nf, t, tf), bf16), jax.ShapeDtypeStruct((nf, t, tf), bf16),
                   jax.ShapeDtypeStruct((nf, nl, d, tf), f32), jax.ShapeDtypeStruct((nf, nl, d, tf), f32),
                   jax.ShapeDtypeStruct((nf, nl, tf, d), f32)],
        input_output_aliases=aliases,
        compiler_params=_params(("parallel", "arbitrary")), name=name,
    )(*args)


def ffn_bwd_input(dgate, dup, wg, wu, layer, dz, *, name):
    nf, t, tf = dgate.shape
    d = wg.shape[2]
    tm = _tile(t, FFN_ROWS // 2, SUBLANES)

    def body(dg_ref, du_ref, wg_ref, wu_ref, dz_ref, dx_ref):
        acc = DN_ALPHA * dz_ref[...]
        for j in range(nf):
            acc = acc + _bdot(dg_ref[j], wg_ref[j], NT) + _bdot(du_ref[j], wu_ref[j], NT)
        dx_ref[...] = acc

    act = pl.BlockSpec((nf, tm, tf), lambda i: (0, i, 0))
    wsp = pl.BlockSpec((nf, None, d, tf), lambda i: (0, layer, 0, 0))
    row = pl.BlockSpec((tm, d), lambda i: (i, 0))
    return pl.pallas_call(
        body, grid=(t // tm,), in_specs=[act, act, wsp, wsp, row], out_specs=row,
        out_shape=jax.ShapeDtypeStruct((t, d), f32),
        compiler_params=_params(("parallel",)), name=name,
    )(dgate, dup, wg, wu, dz)


def ple_fwd(x, p, wg, bg, wp, *, name):
    t, d = x.shape
    dp = p.shape[1]
    tm = _tile(t, 512, SUBLANES)

    def body(x_ref, p_ref, wg_ref, bg_ref, wp_ref, o_ref):
        x_ = x_ref[...]
        gate = jax.nn.sigmoid(_bdot(x_, wg_ref[...], NN) + bg_ref[...])
        o_ref[...] = x_ + gate * _bdot(p_ref[...], wp_ref[...], NN)

    row = pl.BlockSpec((tm, d), lambda i: (i, 0))
    return pl.pallas_call(
        body, grid=(t // tm,),
        in_specs=[row, pl.BlockSpec((tm, dp), lambda i: (i, 0)), pl.BlockSpec((d, d), lambda i: (0, 0)),
                  pl.BlockSpec((1, d), lambda i: (0, 0)), pl.BlockSpec((dp, d), lambda i: (0, 0))],
        out_specs=row, out_shape=jax.ShapeDtypeStruct((t, d), f32),
        compiler_params=_params(("parallel",)), name=name,
    )(x, p, wg, bg, wp)


def ple_bwd(x, p, dy, wg, wgt, bg, wp, *, name):
    t, d = x.shape
    dp = p.shape[1]
    tm = _tile(t, 512, SUBLANES)

    def body(x_ref, p_ref, dy_ref, wg_ref, wgt_ref, bg_ref, wp_ref, dx_ref, dwg_ref, dbg_ref, dwp_ref):
        x_ = x_ref[...]
        dy_ = dy_ref[...]
        s = jax.nn.sigmoid(_bdot(x_, wg_ref[...], NN) + bg_ref[...])
        e = _bdot(p_ref[...], wp_ref[...], NN)
        da = dy_ * e * s * (1.0 - s)
        de = dy_ * s
        dx_ref[...] = dy_ + _bdot(da, wgt_ref[...], NN)
        pwg = _bdot(x_, da, TN)
        pbg = jnp.sum(da, 0, keepdims=True)
        pwp = _bdot(p_ref[...], de, TN)

        @pl.when(pl.program_id(0) == 0)
        def _():
            dwg_ref[...] = pwg
            dbg_ref[...] = pbg
            dwp_ref[...] = pwp

        @pl.when(pl.program_id(0) > 0)
        def _():
            dwg_ref[...] += pwg
            dbg_ref[...] += pbg
            dwp_ref[...] += pwp

    row = pl.BlockSpec((tm, d), lambda i: (i, 0))
    full = lambda shape: pl.BlockSpec(shape, lambda i: (0, 0))
    return pl.pallas_call(
        body, grid=(t // tm,),
        in_specs=[row, pl.BlockSpec((tm, dp), lambda i: (i, 0)), row, full((d, d)), full((d, d)), full((1, d)),
                  full((dp, d))],
        out_specs=[row, full((d, d)), full((1, d)), full((dp, d))],
        out_shape=[jax.ShapeDtypeStruct((t, d), f32), jax.ShapeDtypeStruct((d, d), f32),
                   jax.ShapeDtypeStruct((1, d), f32), jax.ShapeDtypeStruct((dp, d), f32)],
        compiler_params=_params(("arbitrary",)), name=name,
    )(x, p, dy, wg, wgt, bg, wp)


def _conv_taps(xpad_ref, w_ref, s):
    acc = w_ref[0:1, :] * xpad_ref[SUBLANES - 3:SUBLANES - 3 + s, :]
    for j in range(1, 4):
        acc = acc + w_ref[j:j + 1, :] * xpad_ref[SUBLANES - 3 + j:SUBLANES - 3 + j + s, :]
    return acc


def conv_fwd(x, w, bias, act, nb, *, name):
    t, c = x.shape
    s = t // nb
    cw = GROUP_W

    def body(x_ref, w_ref, b_ref, y_ref, xpad):
        xpad[0:SUBLANES, :] = jnp.zeros((SUBLANES, cw), f32)
        xpad[SUBLANES:, :] = x_ref[...]
        acc = _conv_taps(xpad, w_ref, s) + b_ref[...]
        y_ref[...] = _silu(acc) if act else acc

    slab = pl.BlockSpec((s, cw), lambda b, g: (b, g))
    return pl.pallas_call(
        body, grid=(nb, c // cw),
        in_specs=[slab, pl.BlockSpec((4, cw), lambda b, g: (0, g)), pl.BlockSpec((1, cw), lambda b, g: (0, g))],
        out_specs=slab, out_shape=jax.ShapeDtypeStruct((t, c), f32),
        scratch_shapes=[pltpu.VMEM((s + SUBLANES, cw), f32)],
        compiler_params=_params(("parallel", "parallel")), name=name,
    )(x, w, bias)


def conv_bwd(x, w, bias, dy, act, nb, *, name):
    t, c = x.shape
    s = t // nb
    cw = GROUP_W

    def body(x_ref, w_ref, b_ref, dy_ref, dx_ref, dw_ref, db_ref, xpad, dpad):
        xpad[0:SUBLANES, :] = jnp.zeros((SUBLANES, cw), f32)
        xpad[SUBLANES:, :] = x_ref[...]
        dacc = dy_ref[...]
        if act:
            acc = _conv_taps(xpad, w_ref, s) + b_ref[...]
            sg = jax.nn.sigmoid(acc)
            dacc = dacc * (sg * (1.0 + acc * (1.0 - sg)))
        dpad[0:s, :] = dacc
        dpad[s:, :] = jnp.zeros((SUBLANES, cw), f32)
        dx = w_ref[0:1, :] * dpad[3:3 + s, :]
        for j in range(1, 4):
            dx = dx + w_ref[j:j + 1, :] * dpad[3 - j:3 - j + s, :]
        dx_ref[...] = dx
        first = pl.program_id(1) == 0
        for j in range(4):
            pw = jnp.sum(dacc * xpad[SUBLANES - 3 + j:SUBLANES - 3 + j + s, :], 0, keepdims=True)

            @pl.when(first)
            def _():
                dw_ref[j:j + 1, :] = pw

            @pl.when(jnp.logical_not(first))
            def _():
                dw_ref[j:j + 1, :] += pw

        pb = jnp.sum(dacc, 0, keepdims=True)

        @pl.when(first)
        def _():
            db_ref[...] = pb

        @pl.when(jnp.logical_not(first))
        def _():
            db_ref[...] += pb

    slab = pl.BlockSpec((s, cw), lambda g, b: (b, g))
    wsp = pl.BlockSpec((4, cw), lambda g, b: (0, g))
    bsp = pl.BlockSpec((1, cw), lambda g, b: (0, g))
    return pl.pallas_call(
        body, grid=(c // cw, nb), in_specs=[slab, wsp, bsp, slab], out_specs=[slab, wsp, bsp],
        out_shape=[jax.ShapeDtypeStruct((t, c), f32), jax.ShapeDtypeStruct((4, c), f32),
                   jax.ShapeDtypeStruct((1, c), f32)],
        scratch_shapes=[pltpu.VMEM((s + SUBLANES, cw), f32), pltpu.VMEM((s + SUBLANES, cw), f32)],
        compiler_params=_params(("parallel", "arbitrary")), name=name,
    )(x, w, bias, dy)


def _each(f, *lists):
    return [f(*a) for a in zip(*lists)]


def _attn_heads(qs, kbs, vbs, sinks, valid, dist, dots):
    nn, nt = dots[:2]
    kv = [h // A_GROUP for h in range(A_HEADS)]
    scs = [nt(qs[h], kbs[kv[h]]) for h in range(A_HEADS)]
    prs = []
    for h in range(A_HEADS):
        sc = scs[h] * (A_HEAD_DIM ** -0.5) - 2.0 ** -(h + 1) * dist
        sc = jnp.where(valid, sc, NEG)
        m = lax.stop_gradient(jnp.maximum(jnp.max(sc, -1, keepdims=True), sinks[h]))
        pr = jnp.exp(sc - m)
        den = jnp.sum(pr, -1, keepdims=True) + jnp.exp(sinks[h] - m)
        prs.append(pr / den)
    return [nn(prs[h], vbs[kv[h]]) for h in range(A_HEADS)]


A_Q_ROWS = 2 * CHUNK


def _attn_band_consts(r0):
    band = A_WINDOW + A_Q_ROWS
    qi = lax.broadcasted_iota(jnp.int32, (A_Q_ROWS, band), 0)
    kj = lax.broadcasted_iota(jnp.int32, (A_Q_ROWS, band), 1)
    dist = jnp.abs(qi + A_WINDOW - kj).astype(f32)
    qc, kc = qi // CHUNK, kj // CHUNK
    valid = ((kj + r0) >= A_WINDOW) & (kc >= qc) & (kc <= qc + A_WINDOW // CHUNK)
    return dist, valid


def attn_fwd(qkv, sinks, nb, *, name):
    t = qkv.shape[0]
    s = t // nb
    band = A_WINDOW + A_Q_ROWS
    hd = A_HEAD_DIM

    def body(qkv_ref, sink_ref, o_ref, kvpad):
        kvpad[0:A_WINDOW, :] = jnp.zeros((A_WINDOW, 2 * A_KV_WIDTH), f32)
        kvpad[A_WINDOW:, :] = qkv_ref[:, A_WIDTH:]

        def chunk(n, carry):
            r0 = pl.multiple_of(n * A_Q_ROWS, A_Q_ROWS)
            dist, valid = _attn_band_consts(r0)
            kbs = [kvpad[pl.ds(r0, band), kvh * hd:(kvh + 1) * hd] for kvh in range(A_KV_HEADS)]
            vbs = [kvpad[pl.ds(r0, band), A_KV_WIDTH + kvh * hd:A_KV_WIDTH + (kvh + 1) * hd]
                   for kvh in range(A_KV_HEADS)]
            qs = [qkv_ref[pl.ds(r0, A_Q_ROWS), h * hd:(h + 1) * hd] for h in range(A_HEADS)]
            outs = _attn_heads(qs, kbs, vbs, [sink_ref[:, h:h + 1] for h in range(A_HEADS)], valid, dist, RAW_DOTS)
            for h in range(A_HEADS):
                o_ref[pl.ds(r0, A_Q_ROWS), h * hd:(h + 1) * hd] = outs[h]
            return carry

        lax.fori_loop(0, s // A_Q_ROWS, chunk, 0)

    return pl.pallas_call(
        body, grid=(nb,),
        in_specs=[pl.BlockSpec((s, A_WIDTH + 2 * A_KV_WIDTH), lambda b: (b, 0)),
                  pl.BlockSpec((1, A_HEADS), lambda b: (0, 0))],
        out_specs=pl.BlockSpec((s, A_WIDTH), lambda b: (b, 0)),
        out_shape=jax.ShapeDtypeStruct((t, A_WIDTH), f32),
        scratch_shapes=[pltpu.VMEM((s + A_WINDOW, 2 * A_KV_WIDTH), f32)],
        compiler_params=_params(("parallel",)), name=name,
    )(qkv, sinks)


def attn_bwd(qkv, sinks, do, nb, *, name):
    t = qkv.shape[0]
    s = t // nb
    band = A_WINDOW + A_Q_ROWS
    hd = A_HEAD_DIM
    kvw = 2 * A_KV_WIDTH

    def body(qkv_ref, sink_ref, do_ref, dqkv_ref, dsink_ref, kvpad, dkvpad):
        kvpad[0:A_WINDOW, :] = jnp.zeros((A_WINDOW, kvw), f32)
        kvpad[A_WINDOW:, :] = qkv_ref[:, A_WIDTH:]
        dkvpad[...] = jnp.zeros((s + A_WINDOW, kvw), f32)

        def chunk(n, dsinks):
            r0 = pl.multiple_of(n * A_Q_ROWS, A_Q_ROWS)
            dist, valid = _attn_band_consts(r0)
            ksl = [slice(kvh * hd, (kvh + 1) * hd) for kvh in range(A_KV_HEADS)]
            vsl = [slice(A_KV_WIDTH + kvh * hd, A_KV_WIDTH + (kvh + 1) * hd) for kvh in range(A_KV_HEADS)]
            kbs = [kvpad[pl.ds(r0, band), sl] for sl in ksl]
            vbs = [kvpad[pl.ds(r0, band), sl] for sl in vsl]
            dkbs = [dkvpad[pl.ds(r0, band), sl] for sl in ksl]
            dvbs = [dkvpad[pl.ds(r0, band), sl] for sl in vsl]
            qs = [qkv_ref[pl.ds(r0, A_Q_ROWS), h * hd:(h + 1) * hd] for h in range(A_HEADS)]
            dos = [do_ref[pl.ds(r0, A_Q_ROWS), h * hd:(h + 1) * hd] for h in range(A_HEADS)]
            fn = functools.partial(_attn_heads, valid=valid, dist=dist, dots=VJP_DOTS)
            _, vjp = jax.vjp(fn, qs, kbs, vbs, [sink_ref[:, h:h + 1] for h in range(A_HEADS)])
            dqs, dks, dvs, dss = vjp(dos)
            for h in range(A_HEADS):
                dqkv_ref[pl.ds(r0, A_Q_ROWS), h * hd:(h + 1) * hd] = dqs[h]
            for kvh in range(A_KV_HEADS):
                dkvpad[pl.ds(r0, band), ksl[kvh]] = dkbs[kvh] + dks[kvh]
                dkvpad[pl.ds(r0, band), vsl[kvh]] = dvbs[kvh] + dvs[kvh]
            return tuple(dsinks[h] + dss[h] for h in range(A_HEADS))

        dsinks = lax.fori_loop(0, s // A_Q_ROWS, chunk, tuple(jnp.zeros((1, 1), f32) for _ in range(A_HEADS)))
        dqkv_ref[:, A_WIDTH:] = dkvpad[A_WINDOW:, :]
        first = pl.program_id(0) == 0
        for h in range(A_HEADS):
            @pl.when(first)
            def _():
                dsink_ref[:, h:h + 1] = dsinks[h]

            @pl.when(jnp.logical_not(first))
            def _():
                dsink_ref[:, h:h + 1] += dsinks[h]

    wq = A_WIDTH + kvw
    return pl.pallas_call(
        body, grid=(nb,),
        in_specs=[pl.BlockSpec((s, wq), lambda b: (b, 0)), pl.BlockSpec((1, A_HEADS), lambda b: (0, 0)),
                  pl.BlockSpec((s, A_WIDTH), lambda b: (b, 0))],
        out_specs=[pl.BlockSpec((s, wq), lambda b: (b, 0)), pl.BlockSpec((1, A_HEADS), lambda b: (0, 0))],
        out_shape=[jax.ShapeDtypeStruct((t, wq), f32), jax.ShapeDtypeStruct((1, A_HEADS), f32)],
        scratch_shapes=[pltpu.VMEM((s + A_WINDOW, kvw), f32), pltpu.VMEM((s + A_WINDOW, kvw), f32)],
        compiler_params=_params(("arbitrary",)), name=name,
    )(qkv, sinks, do)


def _rg_gates(xc, wa, wx, ba, bx, lam, nn):
    r = jax.nn.sigmoid(nn(xc, wa) + ba)
    i = jax.nn.sigmoid(nn(xc, wx) + bx)
    log_a = -RG_C * r * jax.nn.softplus(-lam)
    a = jnp.exp(log_a)
    mult = jnp.sqrt(-jnp.tanh(log_a) * (jnp.exp(2.0 * log_a) + 1.0))
    return a, mult * (i * xc)


def _linear_scan(a, u, reverse):
    s = a.shape[0]
    t = lax.broadcasted_iota(jnp.int32, a.shape, 0)
    d = 1
    while d < s:
        if reverse:
            keep = t < s - d
            shift = s - d
        else:
            keep = t >= d
            shift = d
        us = jnp.where(keep, pltpu.roll(u, shift, 0), 0.0)
        as_ = jnp.where(keep, pltpu.roll(a, shift, 0), 1.0)
        u = u + a * us
        a = a * as_
        d *= 2
    return u


def rglru_fwd(xc, bg, wa, wx, ba, bx, lam, nb, *, name):
    t, c = xc.shape
    s = t // nb
    cw = GROUP_W

    def body(xc_ref, bg_ref, wa_ref, wx_ref, ba_ref, bx_ref, lam_ref, y_ref, h_ref):
        a, u = _rg_gates(xc_ref[...], wa_ref[...], wx_ref[...], ba_ref[...], bx_ref[...], lam_ref[...], RAW_DOTS[0])
        h = _linear_scan(a, u, False)
        h_ref[...] = h
        y_ref[...] = h * jax.nn.gelu(bg_ref[...])

    slab = pl.BlockSpec((s, cw), lambda b, g: (b, g))
    wsp = pl.BlockSpec((None, cw, cw), lambda b, g: (g, 0, 0))
    vec = pl.BlockSpec((1, cw), lambda b, g: (0, g))
    return pl.pallas_call(
        body, grid=(nb, c // cw), in_specs=[slab, slab, wsp, wsp, vec, vec, vec], out_specs=[slab, slab],
        out_shape=[jax.ShapeDtypeStruct((t, c), f32)] * 2,
        compiler_params=_params(("parallel", "parallel")), name=name,
    )(xc, bg, wa, wx, ba, bx, lam)


def rglru_bwd(xc, bg, h, dy, wa, wx, ba, bx, lam, nb, *, name):
    t, c = xc.shape
    s = t // nb
    cw = GROUP_W

    def body(xc_ref, bg_ref, h_ref, dy_ref, wa_ref, wx_ref, ba_ref, bx_ref, lam_ref,
             dxc_ref, dbg_ref, dwa_ref, dwx_ref, dba_ref, dbx_ref, dlam_ref):
        h = h_ref[...]
        dy_ = dy_ref[...]
        gel, gel_vjp = jax.vjp(jax.nn.gelu, bg_ref[...])
        dbg_ref[...] = gel_vjp(dy_ * h)[0]
        dh = dy_ * gel
        gates = functools.partial(_rg_gates, nn=_bnn)
        (a, _), gates_vjp = jax.vjp(gates, xc_ref[...], wa_ref[...], wx_ref[...], ba_ref[...], bx_ref[...],
                                    lam_ref[...])
        ti = lax.broadcasted_iota(jnp.int32, a.shape, 0)
        a_next = jnp.where(ti < s - 1, pltpu.roll(a, s - 1, 0), 0.0)
        lam_t = _linear_scan(a_next, dh, True)
        h_prev = jnp.where(ti >= 1, pltpu.roll(h, 1, 0), 0.0)
        dxc, dwa, dwx, dba, dbx, dlam = gates_vjp((lam_t * h_prev, lam_t))
        dxc_ref[...] = dxc
        first = pl.program_id(1) == 0

        @pl.when(first)
        def _():
            dwa_ref[...] = dwa
            dwx_ref[...] = dwx
            dba_ref[...] = dba
            dbx_ref[...] = dbx
            dlam_ref[...] = dlam

        @pl.when(jnp.logical_not(first))
        def _():
            dwa_ref[...] += dwa
            dwx_ref[...] += dwx
            dba_ref[...] += dba
            dbx_ref[...] += dbx
            dlam_ref[...] += dlam

    slab = pl.BlockSpec((s, cw), lambda g, b: (b, g))
    wsp = pl.BlockSpec((None, cw, cw), lambda g, b: (g, 0, 0))
    vec = pl.BlockSpec((1, cw), lambda g, b: (0, g))
    ng = c // cw
    return pl.pallas_call(
        body, grid=(ng, nb), in_specs=[slab, slab, slab, slab, wsp, wsp, vec, vec, vec],
        out_specs=[slab, slab, wsp, wsp, vec, vec, vec],
        out_shape=[jax.ShapeDtypeStruct((t, c), f32), jax.ShapeDtypeStruct((t, c), f32),
                   jax.ShapeDtypeStruct((ng, cw, cw), f32), jax.ShapeDtypeStruct((ng, cw, cw), f32),
                   jax.ShapeDtypeStruct((1, c), f32), jax.ShapeDtypeStruct((1, c), f32),
                   jax.ShapeDtypeStruct((1, c), f32)],
        compiler_params=_params(("parallel", "arbitrary")), name=name,
    )(xc, bg, h, dy, wa, wx, ba, bx, lam)


def _gdn_chunks_prep(qs, ks, vs, bls, als, a_log, dt_b, dots):
    nn, nt, csum = dots[0], dots[1], dots[3]
    hd = C_HEAD_DIM
    ri = lax.broadcasted_iota(jnp.int32, (CHUNK, CHUNK), 0)
    ci = lax.broadcasted_iota(jnp.int32, (CHUNK, CHUNK), 1)
    tril = ri >= ci
    strict = ri > ci
    eye = (ri == ci).astype(f32)
    qn = [q * lax.rsqrt(jnp.sum(q * q, -1, keepdims=True) + NORM_EPS) * (hd ** -0.5) for q in qs]
    kn = [k * lax.rsqrt(jnp.sum(k * k, -1, keepdims=True) + NORM_EPS) for k in ks]
    beta = [jax.nn.sigmoid(bl) for bl in bls]
    g = [-jnp.exp(a_log) * jax.nn.softplus(al + dt_b) for al in als]
    gc_sq = [csum(jnp.broadcast_to(g_, (CHUNK, CHUNK))) for g_ in g]
    gc = [csum(jnp.broadcast_to(g_, (CHUNK, hd))) for g_ in g]
    decay = [jnp.where(tril, jnp.exp(jnp.where(tril, s - s.T, 0.0)), 0.0) for s in gc_sq]
    kb = _each(jnp.multiply, kn, beta)
    kk = _each(nt, kb, kn)
    pw = [-jnp.where(strict, a * d, 0.0) for a, d in zip(kk, decay)]
    inv = [eye + p_ for p_ in pw]
    for _ in range(5):
        pw = _each(nn, pw, pw)
        inv = _each(jnp.add, inv, _each(nn, inv, pw))
    egc = [jnp.exp(c_) for c_ in gc]
    u = _each(nn, inv, _each(jnp.multiply, vs, beta))
    w = _each(nn, inv, _each(jnp.multiply, kb, egc))
    attn = _each(jnp.multiply, _each(nt, qn, kn), decay)
    g_last = [jnp.sum(jnp.broadcast_to(g_, (CHUNK, hd)), 0, keepdims=True) for g_ in g]
    qg = _each(jnp.multiply, qn, egc)
    kdec = [k_ * jnp.exp(gl_ - c_) for k_, gl_, c_ in zip(kn, g_last, gc)]
    return [(qg[i], kdec[i], w[i], u[i], attn[i], jnp.exp(g_last[i])) for i in range(len(qs))]


def _gdn_heads_step(states, qgs, kdecs, ws, us, attns, gls, zs, ng, dots):
    nn, tn = dots[0], dots[2]
    v_new = _each(jnp.subtract, us, _each(nn, ws, states))
    o = _each(jnp.add, _each(nn, qgs, states), _each(nn, attns, v_new))
    new = [s * gl for s, gl in zip(states, gls)]
    new = _each(jnp.add, new, _each(tn, kdecs, v_new))
    y = [o_ * lax.rsqrt(jnp.mean(o_ * o_, -1, keepdims=True) + NORM_EPS) * ng * _silu(z) for o_, z in zip(o, zs)]
    return y, new


def _loop_unrolled(n, unroll, load, compute, store, init):
    u = unroll if n % unroll == 0 else 1

    def trip(i, carry):
        idx = [i * u + j for j in range(u)]
        loaded = [load(k) for k in idx]
        results = compute(loaded)
        for k, r in zip(idx, results):
            carry = store(k, r, carry)
        return carry

    return lax.fori_loop(0, n // u, trip, init)


def _pick_lane(x, lane):
    li = lax.broadcasted_iota(jnp.int32, x.shape, 1)
    return jnp.sum(jnp.where(li == lane, x, 0.0), 1, keepdims=True)


def _put_lane(col, lane, width):
    li = lax.broadcasted_iota(jnp.int32, (col.shape[0], width), 1)
    return jnp.where(li == lane, col, 0.0)


def _gdn_specs(s, nc):
    hd = C_HEAD_DIM
    head = lambda off: pl.BlockSpec((s, hd), lambda b, h, off=off: (b, off + h))
    attn = pl.BlockSpec((None, s, CHUNK), lambda b, h: (h, b, 0))
    gl = pl.BlockSpec((None, nc * SUBLANES, hd), lambda b, h: (h, b, 0))
    ba = pl.BlockSpec((s, LANES), lambda b, h: (b, 0))
    sc8 = pl.BlockSpec((1, C_HEADS), lambda b, h: (0, 0))
    return head, attn, gl, ba, sc8


def gdn_prep_fwd(qkv, ba, a_log, dt_b, nb, *, name):
    t = qkv.shape[0]
    s = t // nb
    nc = s // CHUNK
    hd = C_HEAD_DIM
    head, attn_sp, gl_sp, ba_sp, sc8 = _gdn_specs(s, nc)

    def body(q_ref, k_ref, v_ref, ba_ref, alog_ref, dtb_ref, qg_ref, kd_ref, w_ref, u_ref, at_ref, gl_ref):
        h = pl.program_id(1)
        a_log_h = _pick_lane(alog_ref[...], h)
        dt_b_h = _pick_lane(dtb_ref[...], h)

        def load(n):
            rows = pl.ds(pl.multiple_of(n * CHUNK, CHUNK), CHUNK)
            bav = ba_ref[rows, :]
            return q_ref[rows, :], k_ref[rows, :], v_ref[rows, :], _pick_lane(bav, h), _pick_lane(bav, C_HEADS + h)

        def compute(loaded):
            return _gdn_chunks_prep(*[list(x) for x in zip(*loaded)], a_log_h, dt_b_h, RAW_DOTS)

        def store(n, outs, carry):
            rows = pl.ds(pl.multiple_of(n * CHUNK, CHUNK), CHUNK)
            qg_ref[rows, :] = outs[0].astype(bf16)
            kd_ref[rows, :] = outs[1].astype(bf16)
            w_ref[rows, :] = outs[2].astype(bf16)
            u_ref[rows, :] = outs[3]
            at_ref[rows, :] = outs[4].astype(bf16)
            gl_ref[pl.ds(pl.multiple_of(n * SUBLANES, SUBLANES), SUBLANES), :] = jnp.broadcast_to(outs[5], (SUBLANES, hd))
            return carry

        _loop_unrolled(nc, PREP_FWD_UNROLL, load, compute, store, 0)

    big = jax.ShapeDtypeStruct((t, C_WIDTH), f32)
    bigb = jax.ShapeDtypeStruct((t, C_WIDTH), bf16)
    return pl.pallas_call(
        body, grid=(nb, C_HEADS),
        in_specs=[head(0), head(C_HEADS), head(2 * C_HEADS), ba_sp, sc8, sc8],
        out_specs=[head(0)] * 4 + [attn_sp, gl_sp],
        out_shape=[bigb, bigb, bigb, big, jax.ShapeDtypeStruct((C_HEADS, t, CHUNK), bf16),
                               jax.ShapeDtypeStruct((C_HEADS, nb * nc * SUBLANES, hd), f32)],
        compiler_params=_params(("parallel", "parallel")), name=name,
    )(qkv, qkv, qkv, ba, a_log, dt_b)


def gdn_prep_bwd(qkv, ba, a_log, dt_b, cts, nb, *, name):
    t = qkv.shape[0]
    s = t // nb
    nc = s // CHUNK
    hd = C_HEAD_DIM
    head, attn_sp, gl_sp, ba_sp, sc8 = _gdn_specs(s, nc)

    def body(q_ref, k_ref, v_ref, ba_ref, alog_ref, dtb_ref, cqg, ckd, cw_, cu, cat, cgl,
             dq_ref, dk_ref, dv_ref, dba_ref, dalog_ref, ddtb_ref):
        b = pl.program_id(0)
        h = pl.program_id(1)
        a_log_h = _pick_lane(alog_ref[...], h)
        dt_b_h = _pick_lane(dtb_ref[...], h)
        prep = functools.partial(_gdn_chunks_prep, dots=VJP_DOTS)

        @pl.when(h == 0)
        def _():
            dba_ref[...] = jnp.zeros((s, LANES), f32)

        def load(n):
            rows = pl.ds(pl.multiple_of(n * CHUNK, CHUNK), CHUNK)
            bav = ba_ref[rows, :]
            cgl_n = cgl[pl.ds(pl.multiple_of(n * SUBLANES, SUBLANES), SUBLANES), :][0:1, :]
            primals = (q_ref[rows, :], k_ref[rows, :], v_ref[rows, :], _pick_lane(bav, h), _pick_lane(bav, C_HEADS + h))
            return primals, (cqg[rows, :], ckd[rows, :], cw_[rows, :], cu[rows, :], cat[rows, :], cgl_n), dba_ref[rows, :]

        def compute(loaded):
            primals = [list(x) for x in zip(*[item[0] for item in loaded])]
            _, vjp = jax.vjp(prep, *primals, a_log_h, dt_b_h)
            dqs, dks, dvs, dbls, dals, dalog, ddtb = vjp([item[1] for item in loaded])
            zero = jnp.zeros((1, 1), f32)
            return [((dqs[i], dks[i], dvs[i], dbls[i], dals[i], dalog if i == 0 else zero, ddtb if i == 0 else zero),
                     loaded[i][2]) for i in range(len(loaded))]

        def store(n, res, carry):
            (dq, dk, dv, dbl, dal, dalog_n, ddtb_n), dba_old = res
            rows = pl.ds(pl.multiple_of(n * CHUNK, CHUNK), CHUNK)
            dq_ref[rows, :] = dq
            dk_ref[rows, :] = dk
            dv_ref[rows, :] = dv
            dba_ref[rows, :] = dba_old + _put_lane(dbl, h, LANES) + _put_lane(dal, C_HEADS + h, LANES)
            return carry[0] + dalog_n, carry[1] + ddtb_n

        da_log, ddt_b = _loop_unrolled(nc, PREP_BWD_UNROLL, load, compute, store,
                                       (jnp.zeros((1, 1), f32), jnp.zeros((1, 1), f32)))
        first = jnp.logical_and(b == 0, h == 0)

        @pl.when(first)
        def _():
            dalog_ref[...] = _put_lane(da_log, h, LANES)
            ddtb_ref[...] = _put_lane(ddt_b, h, LANES)

        @pl.when(jnp.logical_not(first))
        def _():
            dalog_ref[...] += _put_lane(da_log, h, LANES)
            ddtb_ref[...] += _put_lane(ddt_b, h, LANES)

    big = jax.ShapeDtypeStruct((t, C_WIDTH), f32)
    vec = pl.BlockSpec((1, LANES), lambda b, h: (0, 0))
    return pl.pallas_call(
        body, grid=(nb, C_HEADS),
        in_specs=[head(0), head(C_HEADS), head(2 * C_HEADS), ba_sp, sc8, sc8] + [head(0)] * 4 + [attn_sp, gl_sp],
        out_specs=[head(0)] * 3 + [ba_sp, vec, vec],
        out_shape=[big] * 3 + [jax.ShapeDtypeStruct((t, LANES), f32), jax.ShapeDtypeStruct((1, LANES), f32),
                               jax.ShapeDtypeStruct((1, LANES), f32)],
        compiler_params=_params(("arbitrary", "arbitrary")), name=name,
    )(qkv, qkv, qkv, ba, a_log, dt_b, *cts)


def _gdn_rec_specs(sb, nsb, hp, reverse):
    hd = C_HEAD_DIM
    ncb = sb // CHUNK
    blk = (lambda b, k: b * nsb + (nsb - 1 - k)) if reverse else (lambda b, k: b * nsb + k)
    wide = pl.BlockSpec((sb, hp * hd), lambda b, j, k: (blk(b, k), j))
    attn = pl.BlockSpec((hp, sb, CHUNK), lambda b, j, k: (j, blk(b, k), 0))
    gl = pl.BlockSpec((hp, ncb * SUBLANES, hd), lambda b, j, k: (j, blk(b, k), 0))
    ng = pl.BlockSpec((1, hd), lambda b, j, k: (0, 0))
    states = pl.BlockSpec((hp, ncb, hd, hd), lambda b, j, k: (j, blk(b, k), 0, 0))
    return wide, attn, gl, ng, states


def gdn_rec_fwd(qg, kdec, w, u, attn, gl, z, ng, nb, *, name):
    t = qg.shape[0]
    s = t // nb
    sb = min(s, GDN_TIME_BLOCK)
    nsb = s // sb
    hd = C_HEAD_DIM
    hp = C_HEADS_PER_STEP
    wide, attn_sp, gl_sp, ng_sp, st_sp = _gdn_rec_specs(sb, nsb, hp, False)

    def body(qg_ref, kd_ref, w_ref, u_ref, at_ref, gl_ref, z_ref, ng_ref, y_ref, st_ref, carry_ref):
        @pl.when(pl.program_id(2) == 0)
        def _():
            carry_ref[...] = jnp.zeros((hp, hd, hd), f32)

        def chunk(n, states):
            for j in range(hp):
                st_ref[j, n] = states[j]
            rows = pl.ds(pl.multiple_of(n * CHUNK, CHUNK), CHUNK)
            grow = pl.ds(pl.multiple_of(n * SUBLANES, SUBLANES), SUBLANES)
            cols = [slice(j * hd, (j + 1) * hd) for j in range(hp)]
            ins = [(qg_ref[rows, c], kd_ref[rows, c], w_ref[rows, c], u_ref[rows, c], at_ref[j, rows, :],
                    gl_ref[j, grow, :][0:1, :], z_ref[rows, c]) for j, c in enumerate(cols)]
            ys, new = _gdn_heads_step(list(states), *[list(x) for x in zip(*ins)], ng_ref[...], RAW_DOTS)
            for j in range(hp):
                y_ref[rows, cols[j]] = ys[j]
            return tuple(new)

        last = lax.fori_loop(0, sb // CHUNK, chunk, tuple(carry_ref[j] for j in range(hp)))
        for j in range(hp):
            carry_ref[j] = last[j]

    return pl.pallas_call(
        body, grid=(nb, C_HEADS // hp, nsb),
        in_specs=[wide] * 4 + [attn_sp, gl_sp, wide, ng_sp], out_specs=[wide, st_sp],
        out_shape=[jax.ShapeDtypeStruct((t, C_WIDTH), f32), jax.ShapeDtypeStruct((C_HEADS, t // CHUNK, hd, hd), f32)],
        scratch_shapes=[pltpu.VMEM((hp, hd, hd), f32)],
        compiler_params=_params(("parallel", "parallel", "arbitrary")), name=name,
    )(qg, kdec, w, u, attn, gl, z, ng)


def gdn_rec_bwd(qg, kdec, w, u, attn, gl, z, ng, states, dy, nb, *, name):
    t = qg.shape[0]
    s = t // nb
    sb = min(s, GDN_TIME_BLOCK)
    nsb = s // sb
    nc = sb // CHUNK
    hd = C_HEAD_DIM
    hp = C_HEADS_PER_STEP
    wide, attn_sp, gl_sp, ng_sp, st_sp = _gdn_rec_specs(sb, nsb, hp, True)

    def body(qg_ref, kd_ref, w_ref, u_ref, at_ref, gl_ref, z_ref, ng_ref, states, dy_ref,
             dqg_ref, dkd_ref, dw_ref, du_ref, dat_ref, dgl_ref, dz_ref, dng_ref, carry_ref):
        step = functools.partial(_gdn_heads_step, dots=VJP_DOTS)

        @pl.when(pl.program_id(2) == 0)
        def _():
            carry_ref[...] = jnp.zeros((hp, hd, hd), f32)

        def operands(n):
            rows = pl.ds(pl.multiple_of(n * CHUNK, CHUNK), CHUNK)
            grow = pl.ds(pl.multiple_of(n * SUBLANES, SUBLANES), SUBLANES)
            cols = [slice(j * hd, (j + 1) * hd) for j in range(hp)]
            return ([qg_ref[rows, c].astype(f32) for c in cols], [kd_ref[rows, c].astype(f32) for c in cols],
                    [w_ref[rows, c].astype(f32) for c in cols], [u_ref[rows, c] for c in cols],
                    [at_ref[j, rows, :].astype(f32) for j in range(hp)],
                    [gl_ref[j, grow, :][0:1, :] for j in range(hp)], [z_ref[rows, c] for c in cols])

        def bwd_chunk(i, carry):
            n = nc - 1 - i
            rows = pl.ds(pl.multiple_of(n * CHUNK, CHUNK), CHUNK)
            grow = pl.ds(pl.multiple_of(n * SUBLANES, SUBLANES), SUBLANES)
            dsts, dng = carry
            dys = [dy_ref[rows, j * hd:(j + 1) * hd] for j in range(hp)]
            _, vjp = jax.vjp(step, [states[j, n] for j in range(hp)], *operands(n), ng_ref[...])
            dst, dqg, dkd, dw, du, dat, dgl, dz, dng_n = vjp((dys, list(dsts)))
            for j in range(hp):
                cols = slice(j * hd, (j + 1) * hd)
                dqg_ref[rows, cols] = dqg[j]
                dkd_ref[rows, cols] = dkd[j]
                dw_ref[rows, cols] = dw[j]
                du_ref[rows, cols] = du[j]
                dat_ref[j, rows, :] = dat[j]
                dgl_ref[j, grow, :] = jnp.broadcast_to(dgl[j], (SUBLANES, hd))
                dz_ref[rows, cols] = dz[j]
            return tuple(dst), dng + dng_n

        dlast, dng = lax.fori_loop(0, nc, bwd_chunk,
                                   (tuple(carry_ref[j] for j in range(hp)), jnp.zeros((1, hd), f32)))
        for j in range(hp):
            carry_ref[j] = dlast[j]
        first = jnp.logical_and(jnp.logical_and(pl.program_id(0) == 0, pl.program_id(1) == 0), pl.program_id(2) == 0)

        @pl.when(first)
        def _():
            dng_ref[...] = dng

        @pl.when(jnp.logical_not(first))
        def _():
            dng_ref[...] += dng

    big = jax.ShapeDtypeStruct((t, C_WIDTH), f32)
    return pl.pallas_call(
        body, grid=(nb, C_HEADS // hp, nsb),
        in_specs=[wide] * 4 + [attn_sp, gl_sp, wide, ng_sp, st_sp, wide],
        out_specs=[wide] * 4 + [attn_sp, gl_sp, wide, ng_sp],
        out_shape=[big] * 4 + [jax.ShapeDtypeStruct(attn.shape, f32), jax.ShapeDtypeStruct(gl.shape, f32), big,
                               jax.ShapeDtypeStruct((1, hd), f32)],
        scratch_shapes=[pltpu.VMEM((hp, hd, hd), f32)],
        compiler_params=_params(("arbitrary", "arbitrary", "arbitrary")), name=name,
    )(qg, kdec, w, u, attn, gl, z, ng, states, dy)


def _blockdiag_slabs(w):
    per = GROUP_W // B_BLOCK
    slabs = jnp.zeros((B_BLOCKS // per, GROUP_W, GROUP_W), w.dtype)
    for h in range(B_BLOCKS):
        o = (h % per) * B_BLOCK
        slabs = slabs.at[h // per, o:o + B_BLOCK, o:o + B_BLOCK].set(w[h])
    return slabs


def _slab_blocks(slabs):
    per = GROUP_W // B_BLOCK
    return jnp.stack([slabs[h // per, (h % per) * B_BLOCK:(h % per + 1) * B_BLOCK,
                            (h % per) * B_BLOCK:(h % per + 1) * B_BLOCK] for h in range(B_BLOCKS)])


def _mixer_ab_fwd(x1, x1b, W, g, b, nb, tag):
    w_in = W["ab_w_in"][0].astype(bf16)
    o1, o2 = A_WIDTH + 2 * A_KV_WIDTH, A_WIDTH + 2 * A_KV_WIDTH + B_WIDTH
    w_qkv, w_bx, w_bg = w_in[:, :o1], w_in[:, o1:o2], w_in[:, o2:]
    pqkv = mm_nn(x1b,w_qkv, name=tag + "_in_qkv")
    pbx = mm_nn(x1b,w_bx, name=tag + "_in_bx")
    pbg = mm_nn(x1b,w_bg, name=tag + "_in_bg")
    ya = attn_fwd(pqkv, W["a_sinks"], nb, name=tag + "_attn_fwd")
    xc = conv_fwd(pbx, W["b_conv_w"][0], W["b_conv_b"], False, nb, name=tag + "_conv_fwd")
    wa_s, wx_s = _blockdiag_slabs(W["b_wa"][0]), _blockdiag_slabs(W["b_wx"][0])
    yb, hh = rglru_fwd(xc, pbg, wa_s, wx_s, W["b_ba"], W["b_bx"], W["b_lam"], nb, name=tag + "_rglru_fwd")
    w_out = W["ab_w_out"][0].astype(bf16)
    x2, z1 = proj_ln([ya, yb], [w_out[:A_WIDTH], w_out[A_WIDTH:]], x1, g, b, name=tag + "_out_ln")
    saved = (pqkv, pbx, pbg, ya, xc, yb, hh, wa_s, wx_s, w_qkv, w_bx, w_bg, w_out)
    return x2, z1, saved


def _mixer_ab_bwd(x1b, dz1, dz1b, W, saved, nb, tag):
    pqkv, pbx, pbg, ya, xc, yb, hh, wa_s, wx_s, w_qkv, w_bx, w_bg, w_out = saved
    dya = mm_nn(dz1b, w_out[:A_WIDTH].T, name=tag + "_dya")
    dyb = mm_nn(dz1b, w_out[A_WIDTH:].T, name=tag + "_dyb")
    dwo = jnp.concatenate([mm_tn(ya, dz1b, name=tag + "_dwo_a"), mm_tn(yb, dz1b, name=tag + "_dwo_b")], 0)
    dpqkv, dsinks = attn_bwd(pqkv, W["a_sinks"], dya, nb, name=tag + "_attn_bwd")
    dxc, dpbg, dwa_s, dwx_s, dba, dbx, dlam = rglru_bwd(xc, pbg, hh, dyb, wa_s, wx_s, W["b_ba"], W["b_bx"],
                                                       W["b_lam"], nb, name=tag + "_rglru_bwd")
    dpbx, dconv_w, dconv_b = conv_bwd(pbx, W["b_conv_w"][0], W["b_conv_b"], dxc, False, nb, name=tag + "_conv_bwd")
    dw_in = jnp.concatenate([mm_tn(x1b,dpqkv, name=tag + "_dwin_qkv"), mm_tn(x1b,dpbx, name=tag + "_dwin_bx"),
                             mm_tn(x1b,dpbg, name=tag + "_dwin_bg")], 1)
    dx1 = mm_nn(dpqkv, w_qkv.T, add=dz1, add_scale=DN_ALPHA, name=tag + "_dx_qkv")
    dx1 = mm_nn(dpbx, w_bx.T, add=dx1, name=tag + "_dx_bx")
    dx1 = mm_nn(dpbg, w_bg.T, add=dx1, name=tag + "_dx_bg")
    grads = {"ab_w_in": dw_in[None], "a_sinks": dsinks, "b_conv_w": dconv_w[None], "b_conv_b": dconv_b,
             "b_wa": _slab_blocks(dwa_s)[None], "b_ba": dba, "b_wx": _slab_blocks(dwx_s)[None], "b_bx": dbx,
             "b_lam": dlam, "ab_w_out": dwo[None]}
    return dx1, grads


def _mixer_c_fwd(x1, x1b, W, g, b, nb, tag):
    w_in = W["c_w_in"][0].astype(bf16)
    d = w_in.shape[0]
    o1, o2 = 3 * C_WIDTH, 4 * C_WIDTH
    w_qkv, w_z = w_in[:, :o1], w_in[:, o1:o2]
    w_ba = jnp.concatenate([w_in[:, o2:], jnp.zeros((d, LANES - 2 * C_HEADS), bf16)], 1)
    pqkv = mm_nn(x1b,w_qkv, name=tag + "_in_qkv")
    pz = mm_nn(x1b,w_z, name=tag + "_in_z")
    pba = mm_nn(x1b,w_ba, name=tag + "_in_ba")
    zero_b = jnp.zeros((1, o1), f32)
    qkvc = conv_fwd(pqkv, W["c_conv_w"][0], zero_b, True, nb, name=tag + "_conv_fwd")
    prep = gdn_prep_fwd(qkvc, pba, W["c_a_log"], W["c_dt_bias"], nb, name=tag + "_prep_fwd")
    yc, states = gdn_rec_fwd(*prep, pz, W["c_norm_g"], nb, name=tag + "_rec_fwd")
    w_out = W["c_w_out"][0].astype(bf16)
    x2, z1 = proj_ln([yc], [w_out], x1, g, b, name=tag + "_out_ln")
    saved = (pqkv, pz, pba, qkvc, prep, states, yc, w_qkv, w_z, w_ba, w_out, zero_b)
    return x2, z1, saved


def _mixer_c_bwd(x1b, dz1, dz1b, W, saved, nb, tag):
    pqkv, pz, pba, qkvc, prep, states, yc, w_qkv, w_z, w_ba, w_out, zero_b = saved
    dyc = mm_nn(dz1b, w_out.T, name=tag + "_dyc")
    dwo = mm_tn(yc, dz1b, name=tag + "_dwo")
    rec = gdn_rec_bwd(*prep, pz, W["c_norm_g"], states, dyc, nb, name=tag + "_rec_bwd")
    cts, dpz, dng = rec[:6], rec[6], rec[7]
    dq, dk, dv, dpba, dalog, ddtb = gdn_prep_bwd(qkvc, pba, W["c_a_log"], W["c_dt_bias"], cts, nb,
                                                 name=tag + "_prep_bwd")
    dqkvc = jnp.concatenate([dq, dk, dv], 1)
    dpqkv, dconv_w, _ = conv_bwd(pqkv, W["c_conv_w"][0], zero_b, dqkvc, True, nb, name=tag + "_conv_bwd")
    dw_in = jnp.concatenate([mm_tn(x1b,dpqkv, name=tag + "_dwin_qkv"), mm_tn(x1b,dpz, name=tag + "_dwin_z"),
                             mm_tn(x1b,dpba, name=tag + "_dwin_ba")[:, :2 * C_HEADS]], 1)
    dx1 = mm_nn(dpqkv, w_qkv.T, add=dz1, add_scale=DN_ALPHA, name=tag + "_dx_qkv")
    dx1 = mm_nn(dpz, w_z.T, add=dx1, name=tag + "_dx_z")
    dx1 = mm_nn(dpba, w_ba.T, add=dx1, name=tag + "_dx_ba")
    grads = {"c_w_in": dw_in[None], "c_conv_w": dconv_w[None], "c_a_log": dalog[:, :C_HEADS],
             "c_dt_bias": ddtb[:, :C_HEADS], "c_norm_g": dng, "c_w_out": dwo[None]}
    return dx1, grads


def _local_step(x, p, target, W, F):
    nb, s, d = x.shape
    t = nb * s
    h = x.reshape(t, d)
    tape = []
    f1 = [F[k] for k in ("ffn1_wg", "ffn1_wu", "ffn1_wd")]
    f2 = [F[k] for k in ("ffn2_wg", "ffn2_wu", "ffn2_wd")]
    for i in range(DEPTH):
        tag = f"l{i}"
        lg = [W["ln_g"][i, k][None] for k in range(3)]
        lb = [W["ln_b"][i, k][None] for k in range(3)]
        x1, z0, x1b = ffn_fwd(h, *f1, i, lg[0], lb[0], name=tag + "_ffn1_fwd")
        mixer = _mixer_ab_fwd if i % 2 == 0 else _mixer_c_fwd
        x2, z1, msaved = mixer(x1, x1b, W, lg[1], lb[1], nb, tag + "_mix")
        x3, z2, _ = ffn_fwd(x2, *f2, i, lg[2], lb[2], name=tag + "_ffn2_fwd")
        pi = p[i].reshape(t, -1)
        pw = (W["ple_wg"][i].astype(bf16), W["ple_bg"][i][None], W["ple_wp"][i].astype(bf16))
        x4 = ple_fwd(x3, pi, *pw, name=tag + "_ple_fwd")
        tape.append((h, z0, x1b, msaved, z1, x2, z2, x3, pi, pw, lg))
        h = x4
    dh, sq = loss_head(h, target.reshape(t, d), name="loss_head")
    loss = 0.5 * jnp.sum(sq) / d
    per_layer = [None] * DEPTH
    grads = {}
    df1 = df2 = None
    for i in reversed(range(DEPTH)):
        tag = f"l{i}"
        h_in, z0, x1b, msaved, z1, x2, z2, x3, pi, pw, lg = tape[i]
        dx3, dple_wg, dple_bg, dple_wp = ple_bwd(x3, pi, dh, pw[0], pw[0].T, pw[1], pw[2], name=tag + "_ple_bwd")
        dz2, dz2b, dg2, db2 = ln_bwd(z2, dx3, lg[2], name=tag + "_ln2_bwd")
        dgate, dup, *df2 = ffn_bwd_weights(x2.astype(bf16), dz2b, *f2, i, df2, name=tag + "_ffn2_bwd_w")
        dx2 = ffn_bwd_input(dgate, dup, f2[0], f2[1], i, dz2, name=tag + "_ffn2_bwd_x")
        dz1, dz1b, dg1, db1 = ln_bwd(z1, dx2, lg[1], name=tag + "_ln1_bwd")
        mixer_bwd = _mixer_ab_bwd if i % 2 == 0 else _mixer_c_bwd
        dx1, mgrads = mixer_bwd(x1b, dz1, dz1b, W, msaved, nb, tag + "_mix")
        grads.update(mgrads)
        dz0, dz0b, dg0, db0 = ln_bwd(z0, dx1, lg[0], name=tag + "_ln0_bwd")
        dgate, dup, *df1 = ffn_bwd_weights(h_in.astype(bf16), dz0b, *f1, i, df1, name=tag + "_ffn1_bwd_w")
        dh = ffn_bwd_input(dgate, dup, f1[0], f1[1], i, dz0, name=tag + "_ffn1_bwd_x")
        per_layer[i] = {"ln_g": jnp.concatenate([dg0, dg1, dg2], 0), "ln_b": jnp.concatenate([db0, db1, db2], 0),
                        "ple_wg": dple_wg, "ple_bg": dple_bg[0], "ple_wp": dple_wp}
    for k in per_layer[0]:
        grads[k] = jnp.stack([per_layer[i][k] for i in range(DEPTH)])
    grads.update(zip(("ffn1_wg", "ffn1_wu", "ffn1_wd"), df1))
    grads.update(zip(("ffn2_wg", "ffn2_wu", "ffn2_wd"), df2))
    return loss, dh.reshape(nb, s, d), grads


WEIGHT_NAMES = ("ffn1_wg", "ffn1_wu", "ffn1_wd", "ffn2_wg", "ffn2_wu", "ffn2_wd", "ln_g", "ln_b", "ple_wg", "ple_bg",
                "ple_wp", "ab_w_in", "a_sinks", "b_conv_w", "b_conv_b", "b_wa", "b_ba", "b_wx", "b_bx", "b_lam",
                "ab_w_out", "c_w_in", "c_conv_w", "c_a_log", "c_dt_bias", "c_norm_g", "c_w_out")
NATIVE_NAMES = WEIGHT_NAMES[:6]
PACKED_NAMES = WEIGHT_NAMES[6:]
SHARD_AXIS = {"ffn1_wg": 2, "ffn1_wu": 2, "ffn1_wd": 1, "ffn2_wg": 2, "ffn2_wu": 2, "ffn2_wd": 1, "ln_g": 2, "ln_b": 2,
              "ple_wg": 1, "ple_wp": 2, "ab_w_in": 2, "b_conv_w": 2, "ab_w_out": 1, "c_w_in": 2, "c_conv_w": 2,
              "c_w_out": 1}
N_CHIPS = 4
PACK_COLS = LANES
PACK_TILE_MULTIPLE = 256
ELEMENTWISE_BLOCK_ELEMS = 128 * 1024


def _row_tile(r, cols):
    return _tile(r, max(2 * SUBLANES, ELEMENTWISE_BLOCK_ELEMS // cols), 2 * SUBLANES)
MESH = pl.DeviceIdType.MESH
ANY = pl.BlockSpec(memory_space=pl.ANY)


def _tiled_dims(shape):
    w = shape[-1]
    r = 1
    for dim in shape[:-1]:
        r *= dim
    return r, w, -(-r // SUBLANES) * SUBLANES, -(-w // LANES) * LANES


def _pack(pieces, lead=()):
    k = len(lead)
    tiles = []
    for a in pieces:
        r, w, rp, wp = _tiled_dims(a.shape[k:])
        a2 = jnp.pad(a.reshape(lead + (r, w)), [(0, 0)] * k + [(0, rp - r), (0, wp - w)])
        a2 = a2.reshape(lead + (rp // SUBLANES, SUBLANES, wp // LANES, LANES))
        a2 = jnp.swapaxes(a2, k + 1, k + 2)
        tiles.append(a2.reshape(lead + (-1, SUBLANES, LANES)))
    flat = jnp.concatenate(tiles, axis=k)
    n = flat.shape[k]
    n_pad = -(-n // PACK_TILE_MULTIPLE) * PACK_TILE_MULTIPLE
    flat = jnp.pad(flat, [(0, 0)] * k + [(0, n_pad - n), (0, 0), (0, 0)])
    return flat.reshape(lead + (n_pad * SUBLANES, PACK_COLS))


def _unpack(pack, shapes, lead=()):
    k = len(lead)
    flat = pack.reshape(lead + (-1, SUBLANES, LANES))
    out, o = [], 0
    for shp in shapes:
        r, w, rp, wp = _tiled_dims(shp)
        n = (rp // SUBLANES) * (wp // LANES)
        a2 = lax.slice_in_dim(flat, o, o + n, axis=k).reshape(lead + (rp // SUBLANES, wp // LANES, SUBLANES, LANES))
        a2 = jnp.swapaxes(a2, k + 1, k + 2).reshape(lead + (rp, wp))
        a2 = lax.slice_in_dim(lax.slice_in_dim(a2, 0, r, axis=k), 0, w, axis=k + 1)
        out.append(a2.reshape(lead + tuple(shp)))
        o += n
    return out


def _mesh_position():
    x, y, c = lax.axis_index("x"), lax.axis_index("y"), lax.axis_index("c")
    chips = [(1 - x, y), (x, 1 - y), (1 - x, 1 - y)]
    return x, y, c, chips


def _remote(src, dst, send_sems, recv_sems, k, to):
    return pltpu.make_async_remote_copy(src_ref=src, dst_ref=dst, send_sem=send_sems.at[k], recv_sem=recv_sems.at[k],
                                        device_id=to, device_id_type=MESH)


def _sems(n):
    return pltpu.SemaphoreType.DMA((n,))


def place_slot(parts, slot, n_slots, dtype, from_slot, *, name):
    n = len(parts)
    r, cols = parts[0].shape[-2:]
    tr = _row_tile(r, cols)

    def body(s_ref, *refs):
        for a in range(n):
            refs[n + a][...] = refs[a][...].astype(dtype)

    dst = pl.BlockSpec((None, tr, cols), lambda i, s_ref: (s_ref[0], i, 0))
    src = dst if from_slot else pl.BlockSpec((tr, cols), lambda i, s_ref: (i, 0))
    return pl.pallas_call(
        body,
        grid_spec=pltpu.PrefetchScalarGridSpec(num_scalar_prefetch=1, grid=(r // tr,), in_specs=[src] * n,
                                               out_specs=[dst] * n),
        out_shape=[jax.ShapeDtypeStruct((n_slots, r, cols), dtype)] * n,
        compiler_params=_params(("parallel",)), name=name,
    )(slot, *parts)


def gather_shards(bufs, *, name):
    n = len(bufs)

    def body(*refs):
        out_refs = refs[n:2 * n]
        send_sems, recv_sems = refs[2 * n:]
        x, y, c, chips = _mesh_position()
        me = 2 * x + y
        sibling = (x, y, 1 - c)
        waits = []
        for j, (cx, cy) in enumerate(chips):
            for a in range(n):
                own = out_refs[a].at[me, c]
                cp = _remote(own, own, send_sems, recv_sems, 6 * a + j, (cx, cy, c))
                cp.start()
                waits.append(cp.wait_send)
        for j, (cx, cy) in enumerate(chips):
            for a in range(n):
                got = out_refs[a].at[2 * cx + cy, c]
                _remote(got, got, send_sems, recv_sems, 6 * a + j, (cx, cy, c)).wait_recv()
                fw = _remote(got, got, send_sems, recv_sems, 6 * a + 3 + j, sibling)
                fw.start()
                waits.append(fw.wait_send)
        for j, (cx, cy) in enumerate(chips):
            for a in range(n):
                got = out_refs[a].at[2 * cx + cy, 1 - c]
                _remote(got, got, send_sems, recv_sems, 6 * a + 3 + j, sibling).wait_recv()
        for wait in waits:
            wait()

    return pl.pallas_call(
        body, out_shape=[jax.ShapeDtypeStruct(b.shape, b.dtype) for b in bufs],
        in_specs=[ANY] * n, out_specs=[ANY] * n, scratch_shapes=[_sems(6 * n), _sems(6 * n)],
        input_output_aliases={a: a for a in range(n)}, name=name,
    )(*bufs)


def sibling_exchange(gs, *, name):
    n = len(gs)

    def body(*refs):
        g_refs, out_refs = refs[:n], refs[n:2 * n]
        send_sems, recv_sems = refs[2 * n:]
        x, y, c, _ = _mesh_position()
        cps = [_remote(g_refs[a].at[:, 1 - c], out_refs[a], send_sems, recv_sems, a, (x, y, 1 - c)) for a in range(n)]
        for cp in cps:
            cp.start()
        for cp in cps:
            cp.wait()

    return pl.pallas_call(
        body, out_shape=[jax.ShapeDtypeStruct(g.shape[:1] + g.shape[2:], g.dtype) for g in gs],
        in_specs=[ANY] * n, out_specs=[ANY] * n, scratch_shapes=[_sems(n), _sems(n)], name=name,
    )(*gs)


def add_own_half(gs, others, c_idx, dtype, *, name):
    n = len(gs)
    ns, _, r, cols = gs[0].shape
    tr = _row_tile(r, cols)

    def body(c_ref, *refs):
        for a in range(n):
            refs[2 * n + a][...] = (refs[a][...] + refs[n + a][...]).astype(dtype)

    own = pl.BlockSpec((None, None, tr, cols), lambda s, i, c_ref: (s, c_ref[0], i, 0))
    oth = pl.BlockSpec((None, tr, cols), lambda s, i, c_ref: (s, i, 0))
    return pl.pallas_call(
        body,
        grid_spec=pltpu.PrefetchScalarGridSpec(num_scalar_prefetch=1, grid=(ns, r // tr),
                                               in_specs=[own] * n + [oth] * n, out_specs=[oth] * n),
        out_shape=[jax.ShapeDtypeStruct((ns, r, cols), dtype)] * n,
        compiler_params=_params(("parallel", "parallel")), name=name,
    )(c_idx, *gs, *others)


def chip_exchange(ps, qs, *, name):
    n = len(ps)

    def body(*refs):
        p_refs, q_refs = refs[:n], refs[2 * n:3 * n]
        send_sems, recv_sems = refs[3 * n:]
        x, y, c, chips = _mesh_position()
        me = 2 * x + y
        waits = []
        for j, (cx, cy) in enumerate(chips):
            for a in range(n):
                cp = _remote(p_refs[a].at[2 * cx + cy], q_refs[a].at[me], send_sems, recv_sems, 3 * a + j, (cx, cy, c))
                cp.start()
                waits.append(cp.wait_send)
        for j, (cx, cy) in enumerate(chips):
            for a in range(n):
                got = q_refs[a].at[2 * cx + cy]
                _remote(got, got, send_sems, recv_sems, 3 * a + j, (cx, cy, c)).wait_recv()
        for wait in waits:
            wait()

    return pl.pallas_call(
        body, out_shape=[jax.ShapeDtypeStruct(q_.shape, q_.dtype) for q_ in qs], in_specs=[ANY] * (2 * n),
        out_specs=[ANY] * n, scratch_shapes=[_sems(3 * n), _sems(3 * n)],
        input_output_aliases={n + a: a for a in range(n)}, name=name,
    )(*ps, *qs)


def sum_slots(qs, *, name):
    n = len(qs)
    ns, r, cols = qs[0].shape
    tr = _row_tile(r, cols * ns)

    def body(*refs):
        for a in range(n):
            q_ref = refs[a]
            acc = q_ref[0].astype(f32) + q_ref[1].astype(f32)
            for i in range(2, ns):
                acc = acc + q_ref[i].astype(f32)
            refs[n + a][...] = acc

    return pl.pallas_call(
        body, grid=(r // tr,), in_specs=[pl.BlockSpec((ns, tr, cols), lambda i: (0, i, 0))] * n,
        out_specs=[pl.BlockSpec((tr, cols), lambda i: (i, 0))] * n,
        out_shape=[jax.ShapeDtypeStruct((r, cols), f32)] * n,
        compiler_params=_params(("parallel",)), name=name,
    )(*qs)


def sibling_share(bufs, *, name):
    n = len(bufs)

    def body(*refs):
        out_refs = refs[n:2 * n]
        send_sems, recv_sems = refs[2 * n:]
        x, y, c, _ = _mesh_position()
        sibling = (x, y, 1 - c)
        cps = []
        for a in range(n):
            own = out_refs[a].at[c]
            cp = _remote(own, own, send_sems, recv_sems, a, sibling)
            cp.start()
            cps.append(cp)
        for a in range(n):
            theirs = out_refs[a].at[1 - c]
            _remote(theirs, theirs, send_sems, recv_sems, a, sibling).wait_recv()
        for cp in cps:
            cp.wait_send()

    return pl.pallas_call(
        body, out_shape=[jax.ShapeDtypeStruct(b.shape, b.dtype) for b in bufs], in_specs=[ANY] * n,
        out_specs=[ANY] * n, scratch_shapes=[_sems(n), _sems(n)],
        input_output_aliases={a: a for a in range(n)}, name=name,
    )(*bufs)


def adamw(ws, gs, ms, vs, *, name):
    n = len(ws)
    r, cols = ws[0].shape
    tr = _row_tile(r, cols)

    def body(*refs):
        for a in range(n):
            w_ref, g_ref, m_ref, v_ref = (refs[k * n + a] for k in range(4))
            d_ref, m2_ref, v2_ref = (refs[(4 + k) * n + a] for k in range(3))
            g_ = g_ref[...]
            m2 = ADAM_B1 * m_ref[...] + (1.0 - ADAM_B1) * g_
            v2 = ADAM_B2 * v_ref[...] + (1.0 - ADAM_B2) * (g_ * g_)
            m_hat = m2 / (1.0 - ADAM_B1 ** ADAM_STEP)
            v_hat = v2 / (1.0 - ADAM_B2 ** ADAM_STEP)
            d_ref[...] = -ADAM_LR * (m_hat / (jnp.sqrt(v_hat) + ADAM_EPS) + ADAM_WD * w_ref[...])
            m2_ref[...] = m2
            v2_ref[...] = v2

    row = pl.BlockSpec((tr, cols), lambda i: (i, 0))
    out = pl.pallas_call(
        body, grid=(r // tr,), in_specs=[row] * (4 * n), out_specs=[row] * (3 * n),
        out_shape=[jax.ShapeDtypeStruct((r, cols), f32)] * (3 * n),
        compiler_params=_params(("parallel",)), name=name,
    )(*ws, *gs, *ms, *vs)
    return out[:n], out[n:2 * n], out[2 * n:]


def _full_weights(gathered, local, shapes):
    pieces = _unpack(gathered, shapes, lead=(N_CHIPS,))
    full = {}
    for name, loc, pc in zip(PACKED_NAMES, local, pieces):
        ax = SHARD_AXIS.get(name)
        full[name] = loc if ax is None else jnp.concatenate([pc[s] for s in range(N_CHIPS)], axis=ax)
    return full


def _grad_pack(grads, shapes):
    pieces = []
    for name, shp in zip(PACKED_NAMES, shapes):
        g = grads[name]
        ax = SHARD_AXIS.get(name)
        if ax is None:
            pieces.append(jnp.broadcast_to(g.reshape(shp)[None], (N_CHIPS,) + tuple(shp)))
        else:
            pieces.append(jnp.stack(jnp.split(g, N_CHIPS, axis=ax)))
    return _pack(pieces, lead=(N_CHIPS,))


def _by_shape(arrays):
    groups = {}
    for i, a in enumerate(arrays):
        groups.setdefault(a.shape, []).append(i)
    return list(groups.values())


def _grouped(fn, lists, n_out, tag):
    outs = [[None] * len(lists[0]) for _ in range(n_out)]
    for gi, idx in enumerate(_by_shape(lists[0])):
        res = fn(*[[lst[i] for i in idx] for lst in lists], name=f"{tag}_{gi}")
        res = res if n_out > 1 else (res,)
        for k in range(n_out):
            for i, r in zip(idx, res[k]):
                outs[k][i] = r
    return outs if n_out > 1 else outs[0]


def _train_step(x, p, loss_target, weights, m, v):
    packed_w = [weights[k] for k in PACKED_NAMES]
    shapes = [w.shape for w in packed_w]
    halves = lambda a: a.reshape((2, a.shape[0] // 2) + a.shape[1:])
    local = [weights[k] for k in NATIVE_NAMES] + [halves(_pack(packed_w))]
    local_m = [m[k] for k in NATIVE_NAMES] + [halves(_pack([m[k] for k in PACKED_NAMES]))]
    local_v = [v[k] for k in NATIVE_NAMES] + [halves(_pack([v[k] for k in PACKED_NAMES]))]
    flat = lambda lst: [a.reshape((-1, a.shape[-1])) for a in lst]
    c_idx = lax.axis_index("c").astype(jnp.int32).reshape(1)
    chip_idx = (2 * lax.axis_index("x") + lax.axis_index("y")).astype(jnp.int32).reshape(1)

    def placed(arrays, slot, n_slots, dtype, from_slot, tag):
        return _grouped(lambda a, name: place_slot(a, slot, n_slots, dtype, from_slot, name=name), [arrays], 1, tag)

    bufs = placed(flat(local[:-1]), chip_idx, N_CHIPS, bf16, False, "place_ffn_weights")
    bufs += placed(flat(local[-1:]), chip_idx, N_CHIPS, f32, False, "place_packed_weights")
    bufs = [b.reshape((N_CHIPS,) + a.shape) for b, a in zip(bufs, local)]
    gathered = gather_shards(bufs, name="comm_gather_weights")
    full = _full_weights(gathered[-1], packed_w, shapes)
    loss, grad_x, grads = _local_step(x, p, loss_target, full, dict(zip(NATIVE_NAMES, gathered[:-1])))
    gpack = _grad_pack(grads, shapes)
    gs = [grads[k] for k in NATIVE_NAMES] + [gpack.reshape((N_CHIPS,) + local[-1].shape)]
    others = sibling_exchange(gs, name="comm_grad_sibling")
    nn_ = len(NATIVE_NAMES)
    chip_sums = _grouped(lambda a, b, name: add_own_half(a, b, c_idx, bf16, name=name), [gs[:nn_], others[:nn_]], 1,
                         "grad_add_sibling_ffn")
    chip_sums += add_own_half(gs[nn_:], others[nn_:], c_idx, f32, name="grad_add_sibling_packed")
    own = placed(chip_sums[:nn_], chip_idx, N_CHIPS, bf16, True, "place_own_partial_ffn")
    own += placed(chip_sums[nn_:], chip_idx, N_CHIPS, f32, True, "place_own_partial_packed")
    slots = chip_exchange(chip_sums, own, name="comm_grad_chips")
    mine = _grouped(sum_slots, [slots], 1, "grad_sum_chips")
    gsum = sibling_share(placed(mine, c_idx, 2, f32, False, "place_own_half"), name="comm_grad_share")
    delta, m2, v2 = _grouped(adamw, [flat(local), flat(gsum), flat(local_m), flat(local_v)], 3, "adamw")
    loss = lax.psum(loss, ("x", "y", "c"))
    outs = []
    for res in (gsum, delta, m2, v2):
        by_name = {k: a.reshape(weights[k].shape) for k, a in zip(NATIVE_NAMES, res[:-1])}
        by_name.update(zip(PACKED_NAMES, _unpack(res[-1], shapes)))
        outs += [by_name[k] for k in WEIGHT_NAMES]
    return (loss, grad_x, *outs)


def kernel(x, p, ffn1_wg, ffn1_wu, ffn1_wd, ffn2_wg, ffn2_wu, ffn2_wd, ln_g, ln_b, ple_wg, ple_bg, ple_wp, ab_w_in, a_sinks, b_conv_w, b_conv_b, b_wa, b_ba, b_wx, b_bx, b_lam, ab_w_out, c_w_in, c_conv_w, c_a_log, c_dt_bias, c_norm_g, c_w_out, loss_target, m_ffn1_wg, m_ffn1_wu, m_ffn1_wd, m_ffn2_wg, m_ffn2_wu, m_ffn2_wd, m_ln_g, m_ln_b, m_ple_wg, m_ple_bg, m_ple_wp, m_ab_w_in, m_a_sinks, m_b_conv_w, m_b_conv_b, m_b_wa, m_b_ba, m_b_wx, m_b_bx, m_b_lam, m_ab_w_out, m_c_w_in, m_c_conv_w, m_c_a_log, m_c_dt_bias, m_c_norm_g, m_c_w_out, v_ffn1_wg, v_ffn1_wu, v_ffn1_wd, v_ffn2_wg, v_ffn2_wu, v_ffn2_wd, v_ln_g, v_ln_b, v_ple_wg, v_ple_bg, v_ple_wp, v_ab_w_in, v_a_sinks, v_b_conv_w, v_b_conv_b, v_b_wa, v_b_ba, v_b_wx, v_b_bx, v_b_lam, v_ab_w_out, v_c_w_in, v_c_conv_w, v_c_a_log, v_c_dt_bias, v_c_norm_g, v_c_w_out):
    weights = [ffn1_wg, ffn1_wu, ffn1_wd, ffn2_wg, ffn2_wu, ffn2_wd, ln_g, ln_b, ple_wg, ple_bg, ple_wp, ab_w_in, a_sinks,
               b_conv_w, b_conv_b, b_wa, b_ba, b_wx, b_bx, b_lam, ab_w_out, c_w_in, c_conv_w, c_a_log, c_dt_bias, c_norm_g,
               c_w_out]
    m = [m_ffn1_wg, m_ffn1_wu, m_ffn1_wd, m_ffn2_wg, m_ffn2_wu, m_ffn2_wd, m_ln_g, m_ln_b, m_ple_wg, m_ple_bg, m_ple_wp,
         m_ab_w_in, m_a_sinks, m_b_conv_w, m_b_conv_b, m_b_wa, m_b_ba, m_b_wx, m_b_bx, m_b_lam, m_ab_w_out, m_c_w_in,
         m_c_conv_w, m_c_a_log, m_c_dt_bias, m_c_norm_g, m_c_w_out]
    v = [v_ffn1_wg, v_ffn1_wu, v_ffn1_wd, v_ffn2_wg, v_ffn2_wu, v_ffn2_wd, v_ln_g, v_ln_b, v_ple_wg, v_ple_bg, v_ple_wp,
         v_ab_w_in, v_a_sinks, v_b_conv_w, v_b_conv_b, v_b_wa, v_b_ba, v_b_wx, v_b_bx, v_b_lam, v_ab_w_out, v_c_w_in,
         v_c_conv_w, v_c_a_log, v_c_dt_bias, v_c_norm_g, v_c_w_out]
    return _train_step(x, p, loss_target, dict(zip(WEIGHT_NAMES, weights)), dict(zip(WEIGHT_NAMES, m)),
                       dict(zip(WEIGHT_NAMES, v)))
```

```python
import functools

import jax
import jax.numpy as jnp
from jax import lax
from jax.experimental import pallas as pl
from jax.experimental.pallas import tpu as pltpu
from jax.experimental.pallas import tpu_sc as plsc

f32 = jnp.float32
bf16 = jnp.bfloat16

DEPTH = 2
CHUNK = 64
A_HEADS, A_KV_HEADS, A_GROUP, A_HEAD_DIM = 8, 2, 4, 64
A_WIDTH, A_KV_WIDTH, A_WINDOW = 512, 128, 128
B_WIDTH, B_BLOCKS, B_BLOCK, B_CONV = 512, 8, 64, 4
RG_C = 8.0
C_HEADS, C_HEAD_DIM, C_WIDTH, C_CONV = 8, 128, 1024, 4
DN_ALPHA = (2.0 * DEPTH) ** 0.25
LN_EPS = 1e-5
NORM_EPS = 1e-6
NEG = -1e30
ADAM_LR, ADAM_B1, ADAM_B2, ADAM_EPS, ADAM_WD, ADAM_STEP = 0.001, 0.9, 0.999, 1e-08, 0.01, 10

VMEM_LIMIT_BYTES = 56 * 1024 * 1024
LANES = 128
SUBLANES = 8
GROUP_W = 128
PREP_FWD_UNROLL = 8
PREP_BWD_UNROLL = 8
C_HEADS_PER_STEP = 4
GDN_TIME_BLOCK = 512

NN = ((1,), (0,))
NT = ((1,), (1,))
TN = ((0,), (0,))


def _params(sem):
    return pltpu.CompilerParams(dimension_semantics=sem, vmem_limit_bytes=VMEM_LIMIT_BYTES)


def _tile(n, cap, mult):
    best = None
    t = mult
    while t <= min(n, cap):
        if n % t == 0:
            best = t
        t += mult
    return best if best is not None else n


def _bdot(a, b, dims):
    return lax.dot_general(a.astype(bf16), b.astype(bf16), (dims, ((), ())), preferred_element_type=f32)


def _running_sum(x, reverse):
    s = x.shape[0]
    t = lax.broadcasted_iota(jnp.int32, x.shape, 0)
    d = 1
    while d < s:
        if reverse:
            x = x + jnp.where(t < s - d, pltpu.roll(x, s - d, 0), 0.0)
        else:
            x = x + jnp.where(t >= d, pltpu.roll(x, d, 0), 0.0)
        d *= 2
    return x


@jax.custom_vjp
def _cumsum0(x):
    return _running_sum(x, False)


def _cumsum0_fwd(x):
    return _running_sum(x, False), None


def _cumsum0_bwd(_, g):
    return (_running_sum(g, True),)


_cumsum0.defvjp(_cumsum0_fwd, _cumsum0_bwd)


@jax.custom_vjp
def _bnn(a, b):
    return _bdot(a, b, NN)


def _bnn_fwd(a, b):
    return _bdot(a, b, NN), (a, b)


def _bnn_bwd(res, g):
    a, b = res
    return _bdot(g, b, NT), _bdot(a, g, TN)


_bnn.defvjp(_bnn_fwd, _bnn_bwd)


@jax.custom_vjp
def _bnt(a, b):
    return _bdot(a, b, NT)


def _bnt_fwd(a, b):
    return _bdot(a, b, NT), (a, b)


def _bnt_bwd(res, g):
    a, b = res
    return _bdot(g, b, NN), _bdot(g, a, TN)


_bnt.defvjp(_bnt_fwd, _bnt_bwd)


@jax.custom_vjp
def _btn(a, b):
    return _bdot(a, b, TN)


def _btn_fwd(a, b):
    return _bdot(a, b, TN), (a, b)


def _btn_bwd(res, g):
    a, b = res
    return _bdot(b, g, NT), _bdot(a, g, NN)


_btn.defvjp(_btn_fwd, _btn_bwd)

RAW_DOTS = (lambda a, b: _bdot(a, b, NN), lambda a, b: _bdot(a, b, NT), lambda a, b: _bdot(a, b, TN),
            lambda x: _running_sum(x, False))
VJP_DOTS = (_bnn, _bnt, _btn, _cumsum0)


def _layer_norm(z, g, b):
    mu = jnp.mean(z, -1, keepdims=True)
    d = z - mu
    var = jnp.mean(d * d, -1, keepdims=True)
    return d * lax.rsqrt(var + LN_EPS) * g + b


def _silu(x):
    return x * jax.nn.sigmoid(x)


def mm_nn(a, w, add=None, add_scale=1.0, *, name):
    m, k = a.shape
    n = w.shape[1]
    tm = _tile(m, 512, SUBLANES)
    tn = _tile(n, 1024, LANES)

    def body(*refs):
        if add is None:
            a_ref, w_ref, o_ref = refs
            o_ref[...] = _bdot(a_ref[...], w_ref[...], NN)
        else:
            a_ref, w_ref, add_ref, o_ref = refs
            o_ref[...] = _bdot(a_ref[...], w_ref[...], NN) + add_scale * add_ref[...]

    in_specs = [pl.BlockSpec((tm, k), lambda i, j: (i, 0)), pl.BlockSpec((k, tn), lambda i, j: (0, j))]
    args = [a, w]
    if add is not None:
        in_specs.append(pl.BlockSpec((tm, tn), lambda i, j: (i, j)))
        args.append(add)
    return pl.pallas_call(
        body, grid=(m // tm, n // tn), in_specs=in_specs,
        out_specs=pl.BlockSpec((tm, tn), lambda i, j: (i, j)),
        out_shape=jax.ShapeDtypeStruct((m, n), f32),
        compiler_params=_params(("parallel", "parallel")), name=name,
    )(*args)


def mm_tn(a, b, *, name):
    m, k = a.shape
    n = b.shape[1]
    tm = _tile(m, 1024, 2 * SUBLANES)
    tn = _tile(n, 1024, LANES)

    def body(a_ref, b_ref, o_ref):
        part = _bdot(a_ref[...], b_ref[...], TN)

        @pl.when(pl.program_id(1) == 0)
        def _():
            o_ref[...] = part

        @pl.when(pl.program_id(1) > 0)
        def _():
            o_ref[...] += part

    return pl.pallas_call(
        body, grid=(n // tn, m // tm),
        in_specs=[pl.BlockSpec((tm, k), lambda j, i: (i, 0)), pl.BlockSpec((tm, tn), lambda j, i: (i, j))],
        out_specs=pl.BlockSpec((k, tn), lambda j, i: (0, j)),
        out_shape=jax.ShapeDtypeStruct((k, n), f32),
        compiler_params=_params(("parallel", "arbitrary")), name=name,
    )(a, b)


def proj_ln(a_list, w_list, xres, g, b, *, name):
    t, d = xres.shape
    tm = _tile(t, 256, SUBLANES)
    na = len(a_list)

    def body(*refs):
        a_refs, w_refs = refs[:na], refs[na:2 * na]
        x_ref, g_ref, b_ref, y_ref, z_ref = refs[2 * na:]
        z = DN_ALPHA * x_ref[...]
        for a_ref, w_ref in zip(a_refs, w_refs):
            z = z + _bdot(a_ref[...], w_ref[...], NN)
        z_ref[...] = z
        y_ref[...] = _layer_norm(z, g_ref[...], b_ref[...])

    in_specs = [pl.BlockSpec((tm, a.shape[1]), lambda i: (i, 0)) for a in a_list]
    in_specs += [pl.BlockSpec(w.shape, lambda i: (0, 0)) for w in w_list]
    in_specs += [pl.BlockSpec((tm, d), lambda i: (i, 0)), pl.BlockSpec((1, d), lambda i: (0, 0)),
                 pl.BlockSpec((1, d), lambda i: (0, 0))]
    return pl.pallas_call(
        body, grid=(t // tm,), in_specs=in_specs,
        out_specs=[pl.BlockSpec((tm, d), lambda i: (i, 0))] * 2,
        out_shape=[jax.ShapeDtypeStruct((t, d), f32)] * 2,
        compiler_params=_params(("parallel",)), name=name,
    )(*a_list, *w_list, xres, g, b)


def ln_bwd(z, dy, g, *, name):
    t, d = z.shape
    tm = _tile(t, 512, SUBLANES)

    def body(z_ref, dy_ref, g_ref, dz_ref, dzb_ref, dg_ref, db_ref):
        zz = z_ref[...]
        dy_ = dy_ref[...]
        mu = jnp.mean(zz, -1, keepdims=True)
        dd = zz - mu
        var = jnp.mean(dd * dd, -1, keepdims=True)
        rstd = lax.rsqrt(var + LN_EPS)
        xhat = dd * rstd
        dxh = dy_ * g_ref[...]
        dz = rstd * (dxh - jnp.mean(dxh, -1, keepdims=True) - xhat * jnp.mean(dxh * xhat, -1, keepdims=True))
        dz_ref[...] = dz
        dzb_ref[...] = dz.astype(bf16)
        pg = jnp.sum(dy_ * xhat, 0, keepdims=True)
        pb = jnp.sum(dy_, 0, keepdims=True)

        @pl.when(pl.program_id(0) == 0)
        def _():
            dg_ref[...] = pg
            db_ref[...] = pb

        @pl.when(pl.program_id(0) > 0)
        def _():
            dg_ref[...] += pg
            db_ref[...] += pb

    row = pl.BlockSpec((tm, d), lambda i: (i, 0))
    vec = pl.BlockSpec((1, d), lambda i: (0, 0))
    return pl.pallas_call(
        body, grid=(t // tm,), in_specs=[row, row, vec], out_specs=[row, row, vec, vec],
        out_shape=[jax.ShapeDtypeStruct((t, d), f32), jax.ShapeDtypeStruct((t, d), bf16),
                   jax.ShapeDtypeStruct((1, d), f32), jax.ShapeDtypeStruct((1, d), f32)],
        compiler_params=_params(("arbitrary",)), name=name,
    )(z, dy, g)


def loss_head(y, target, *, name):
    t, d = y.shape
    tm = _tile(t, 512, SUBLANES)

    def body(y_ref, t_ref, dy_ref, sq_ref):
        e = y_ref[...] - t_ref[...]
        dy_ref[...] = e * (1.0 / d)
        part = jnp.sum(e * e, 0, keepdims=True)

        @pl.when(pl.program_id(0) == 0)
        def _():
            sq_ref[...] = part

        @pl.when(pl.program_id(0) > 0)
        def _():
            sq_ref[...] += part

    row = pl.BlockSpec((tm, d), lambda i: (i, 0))
    vec = pl.BlockSpec((1, d), lambda i: (0, 0))
    return pl.pallas_call(
        body, grid=(t // tm,), in_specs=[row, row], out_specs=[row, vec],
        out_shape=[jax.ShapeDtypeStruct((t, d), f32), jax.ShapeDtypeStruct((1, d), f32)],
        compiler_params=_params(("arbitrary",)), name=name,
    )(y, target)


FFN_COL_BLOCK = 256
FFN_ROWS = 1024


def _lane_blocks(n):
    return [slice(s, min(s + FFN_COL_BLOCK, n)) for s in range(0, n, FFN_COL_BLOCK)]


def ffn_fwd(x, wg, wu, wd, g, b, *, name):
    t, d = x.shape
    nf, _, tf = wg.shape
    tm = _tile(t, FFN_ROWS, SUBLANES)

    def body(x_ref, wg_ref, wu_ref, wd_ref, g_ref, b_ref, y_ref, z_ref, yb_ref, acc_ref):
        f = pl.program_id(1)
        xb = x_ref[...].astype(bf16)
        part, pending = None, None
        for cols in _lane_blocks(tf):
            gate_up = (_bdot(xb, wg_ref[:, cols], NN), _bdot(xb, wu_ref[:, cols], NN), cols)
            if pending is not None:
                down = _bdot(_silu(pending[0]) * pending[1], wd_ref[pending[2], :], NN)
                part = down if part is None else part + down
            pending = gate_up
        down = _bdot(_silu(pending[0]) * pending[1], wd_ref[pending[2], :], NN)
        part = down if part is None else part + down

        @pl.when(f == 0)
        def _():
            acc_ref[...] = part

        @pl.when(f > 0)
        def _():
            acc_ref[...] += part

        @pl.when(f == nf - 1)
        def _():
            z = DN_ALPHA * x_ref[...] + 0.5 * acc_ref[...]
            z_ref[...] = z
            y = _layer_norm(z, g_ref[...], b_ref[...])
            y_ref[...] = y
            yb_ref[...] = y.astype(bf16)

    row = pl.BlockSpec((tm, d), lambda i, j: (i, 0))
    vec = pl.BlockSpec((1, d), lambda i, j: (0, 0))
    wcol = pl.BlockSpec((None, d, tf), lambda i, j: (j, 0, 0))
    wrow = pl.BlockSpec((None, tf, d), lambda i, j: (j, 0, 0))
    return pl.pallas_call(
        body, grid=(t // tm, nf),
        in_specs=[row, wcol, wcol, wrow, vec, vec],
        out_specs=[row, row, row],
        out_shape=[jax.ShapeDtypeStruct((t, d), f32)] * 2 + [jax.ShapeDtypeStruct((t, d), bf16)],
        scratch_shapes=[pltpu.VMEM((tm, d), f32)],
        compiler_params=_params(("parallel", "arbitrary")), name=name,
    )(x, wg, wu, wd, g, b)


def ffn_bwd_weights(xb, dzb, wg, wu, wd, layer, nl, acc, *, name):
    t, d = xb.shape
    nf, _, tf = wg.shape
    tm = _tile(t, FFN_ROWS, SUBLANES)

    def body(x_ref, dz_ref, wg_ref, wu_ref, wd_ref, *rest):
        dgate_ref, dup_ref, dwg_ref, dwu_ref, dwd_ref = rest[-5:]
        x = x_ref[...]
        dzh = dz_ref[...] * 0.5

        def first_half(cols):
            return _bdot(x, wg_ref[:, cols], NN), _bdot(x, wu_ref[:, cols], NN), _bdot(dzh, wd_ref[cols, :], NT), cols

        def second_half(gate, up, dh, cols):
            sg = jax.nn.sigmoid(gate)
            s = gate * sg
            dup = (dh * s).astype(bf16)
            dgate = (dh * up * (sg * (1.0 + gate * (1.0 - sg)))).astype(bf16)
            dgate_ref[:, cols] = dgate
            dup_ref[:, cols] = dup
            return _bdot(x, dgate, TN), _bdot(x, dup, TN), _bdot(s * up, dzh, TN), cols

        parts, pending = [], None
        for cols in _lane_blocks(tf):
            nxt = first_half(cols)
            if pending is not None:
                parts.append(second_half(*pending))
            pending = nxt
        parts.append(second_half(*pending))

        @pl.when(pl.program_id(1) == 0)
        def _():
            for pwg, pwu, pwd, cols in parts:
                dwg_ref[:, cols] = pwg
                dwu_ref[:, cols] = pwu
                dwd_ref[cols, :] = pwd

        @pl.when(pl.program_id(1) > 0)
        def _():
            for pwg, pwu, pwd, cols in parts:
                dwg_ref[:, cols] += pwg
                dwu_ref[:, cols] += pwu
                dwd_ref[cols, :] += pwd

    row = pl.BlockSpec((tm, d), lambda j, i: (i, 0))
    wcol = pl.BlockSpec((None, None, d, tf), lambda j, i: (j, layer, 0, 0))
    wrow = pl.BlockSpec((None, None, tf, d), lambda j, i: (j, layer, 0, 0))
    act = pl.BlockSpec((None, tm, tf), lambda j, i: (j, i, 0))
    in_specs = [row, row, pl.BlockSpec((None, d, tf), lambda j, i: (j, 0, 0)),
                pl.BlockSpec((None, d, tf), lambda j, i: (j, 0, 0)), pl.BlockSpec((None, tf, d), lambda j, i: (j, 0, 0))]
    args = [xb, dzb, wg, wu, wd]
    aliases = {}
    if acc is not None:
        in_specs += [pl.BlockSpec(memory_space=pl.ANY)] * 3
        args += list(acc)
        aliases = {5: 2, 6: 3, 7: 4}
    return pl.pallas_call(
        body, grid=(nf, t // tm), in_specs=in_specs, out_specs=[act, act, wcol, wcol, wrow],
        out_shape=[jax.ShapeDtypeStruct((nf, t, tf), bf16), jax.ShapeDtypeStruct((nf, t, tf), bf16),
                   jax.ShapeDtypeStruct((nf, nl, d, tf), f32), jax.ShapeDtypeStruct((nf, nl, d, tf), f32),
                   jax.ShapeDtypeStruct((nf, nl, tf, d), f32)],
        input_output_aliases=aliases,
        compiler_params=_params(("parallel", "arbitrary")), name=name,
    )(*args)


def ffn_bwd_input(dgate, dup, wg, wu, dz, *, name):
    nf, t, tf = dgate.shape
    d = wg.shape[1]
    tm = _tile(t, FFN_ROWS // 2, SUBLANES)

    def body(dg_ref, du_ref, wg_ref, wu_ref, dz_ref, dx_ref):
        acc = DN_ALPHA * dz_ref[...]
        for j in range(nf):
            acc = acc + _bdot(dg_ref[j], wg_ref[j], NT) + _bdot(du_ref[j], wu_ref[j], NT)
        dx_ref[...] = acc

    act = pl.BlockSpec((nf, tm, tf), lambda i: (0, i, 0))
    wsp = pl.BlockSpec((nf, d, tf), lambda i: (0, 0, 0))
    row = pl.BlockSpec((tm, d), lambda i: (i, 0))
    return pl.pallas_call(
        body, grid=(t // tm,), in_specs=[act, act, wsp, wsp, row], out_specs=row,
        out_shape=jax.ShapeDtypeStruct((t, d), f32),
        compiler_params=_params(("parallel",)), name=name,
    )(dgate, dup, wg, wu, dz)


def ple_fwd(x, p, wg, bg, wp, *, name):
    t, d = x.shape
    dp = p.shape[1]
    tm = _tile(t, 512, SUBLANES)

    def body(x_ref, p_ref, wg_ref, bg_ref, wp_ref, o_ref):
        x_ = x_ref[...]
        gate = jax.nn.sigmoid(_bdot(x_, wg_ref[...], NN) + bg_ref[...])
        o_ref[...] = x_ + gate * _bdot(p_ref[...], wp_ref[...], NN)

    row = pl.BlockSpec((tm, d), lambda i: (i, 0))
    return pl.pallas_call(
        body, grid=(t // tm,),
        in_specs=[row, pl.BlockSpec((tm, dp), lambda i: (i, 0)), pl.BlockSpec((d, d), lambda i: (0, 0)),
                  pl.BlockSpec((1, d), lambda i: (0, 0)), pl.BlockSpec((dp, d), lambda i: (0, 0))],
        out_specs=row, out_shape=jax.ShapeDtypeStruct((t, d), f32),
        compiler_params=_params(("parallel",)), name=name,
    )(x, p, wg, bg, wp)


def ple_bwd(x, p, dy, wg, wgt, bg, wp, *, name):
    t, d = x.shape
    dp = p.shape[1]
    tm = _tile(t, 512, SUBLANES)

    def body(x_ref, p_ref, dy_ref, wg_ref, wgt_ref, bg_ref, wp_ref, dx_ref, dwg_ref, dbg_ref, dwp_ref):
        x_ = x_ref[...]
        dy_ = dy_ref[...]
        s = jax.nn.sigmoid(_bdot(x_, wg_ref[...], NN) + bg_ref[...])
        e = _bdot(p_ref[...], wp_ref[...], NN)
        da = dy_ * e * s * (1.0 - s)
        de = dy_ * s
        dx_ref[...] = dy_ + _bdot(da, wgt_ref[...], NN)
        pwg = _bdot(x_, da, TN)
        pbg = jnp.sum(da, 0, keepdims=True)
        pwp = _bdot(p_ref[...], de, TN)

        @pl.when(pl.program_id(0) == 0)
        def _():
            dwg_ref[...] = pwg
            dbg_ref[...] = pbg
            dwp_ref[...] = pwp

        @pl.when(pl.program_id(0) > 0)
        def _():
            dwg_ref[...] += pwg
            dbg_ref[...] += pbg
            dwp_ref[...] += pwp

    row = pl.BlockSpec((tm, d), lambda i: (i, 0))
    full = lambda shape: pl.BlockSpec(shape, lambda i: (0, 0))
    return pl.pallas_call(
        body, grid=(t // tm,),
        in_specs=[row, pl.BlockSpec((tm, dp), lambda i: (i, 0)), row, full((d, d)), full((d, d)), full((1, d)),
                  full((dp, d))],
        out_specs=[row, full((d, d)), full((1, d)), full((dp, d))],
        out_shape=[jax.ShapeDtypeStruct((t, d), f32), jax.ShapeDtypeStruct((d, d), f32),
                   jax.ShapeDtypeStruct((1, d), f32), jax.ShapeDtypeStruct((dp, d), f32)],
        compiler_params=_params(("arbitrary",)), name=name,
    )(x, p, dy, wg, wgt, bg, wp)


def _conv_taps(xpad_ref, w_ref, s):
    acc = w_ref[0:1, :] * xpad_ref[SUBLANES - 3:SUBLANES - 3 + s, :]
    for j in range(1, 4):
        acc = acc + w_ref[j:j + 1, :] * xpad_ref[SUBLANES - 3 + j:SUBLANES - 3 + j + s, :]
    return acc


def conv_fwd(x, w, bias, act, nb, *, name):
    t, c = x.shape
    s = t // nb
    cw = GROUP_W

    def body(x_ref, w_ref, b_ref, y_ref, xpad):
        xpad[0:SUBLANES, :] = jnp.zeros((SUBLANES, cw), f32)
        xpad[SUBLANES:, :] = x_ref[...]
        acc = _conv_taps(xpad, w_ref, s) + b_ref[...]
        y_ref[...] = _silu(acc) if act else acc

    slab = pl.BlockSpec((s, cw), lambda b, g: (b, g))
    return pl.pallas_call(
        body, grid=(nb, c // cw),
        in_specs=[slab, pl.BlockSpec((4, cw), lambda b, g: (0, g)), pl.BlockSpec((1, cw), lambda b, g: (0, g))],
        out_specs=slab, out_shape=jax.ShapeDtypeStruct((t, c), f32),
        scratch_shapes=[pltpu.VMEM((s + SUBLANES, cw), f32)],
        compiler_params=_params(("parallel", "parallel")), name=name,
    )(x, w, bias)


def conv_bwd(x, w, bias, dy, act, nb, *, name):
    t, c = x.shape
    s = t // nb
    cw = GROUP_W

    def body(x_ref, w_ref, b_ref, dy_ref, dx_ref, dw_ref, db_ref, xpad, dpad):
        xpad[0:SUBLANES, :] = jnp.zeros((SUBLANES, cw), f32)
        xpad[SUBLANES:, :] = x_ref[...]
        dacc = dy_ref[...]
        if act:
            acc = _conv_taps(xpad, w_ref, s) + b_ref[...]
            sg = jax.nn.sigmoid(acc)
            dacc = dacc * (sg * (1.0 + acc * (1.0 - sg)))
        dpad[0:s, :] = dacc
        dpad[s:, :] = jnp.zeros((SUBLANES, cw), f32)
        dx = w_ref[0:1, :] * dpad[3:3 + s, :]
        for j in range(1, 4):
            dx = dx + w_ref[j:j + 1, :] * dpad[3 - j:3 - j + s, :]
        dx_ref[...] = dx
        first = pl.program_id(1) == 0
        for j in range(4):
            pw = jnp.sum(dacc * xpad[SUBLANES - 3 + j:SUBLANES - 3 + j + s, :], 0, keepdims=True)

            @pl.when(first)
            def _():
                dw_ref[j:j + 1, :] = pw

            @pl.when(jnp.logical_not(first))
            def _():
                dw_ref[j:j + 1, :] += pw

        pb = jnp.sum(dacc, 0, keepdims=True)

        @pl.when(first)
        def _():
            db_ref[...] = pb

        @pl.when(jnp.logical_not(first))
        def _():
            db_ref[...] += pb

    slab = pl.BlockSpec((s, cw), lambda g, b: (b, g))
    wsp = pl.BlockSpec((4, cw), lambda g, b: (0, g))
    bsp = pl.BlockSpec((1, cw), lambda g, b: (0, g))
    return pl.pallas_call(
        body, grid=(c // cw, nb), in_specs=[slab, wsp, bsp, slab], out_specs=[slab, wsp, bsp],
        out_shape=[jax.ShapeDtypeStruct((t, c), f32), jax.ShapeDtypeStruct((4, c), f32),
                   jax.ShapeDtypeStruct((1, c), f32)],
        scratch_shapes=[pltpu.VMEM((s + SUBLANES, cw), f32), pltpu.VMEM((s + SUBLANES, cw), f32)],
        compiler_params=_params(("parallel", "arbitrary")), name=name,
    )(x, w, bias, dy)


def _each(f, *lists):
    return [f(*a) for a in zip(*lists)]


def _attn_heads(qs, kbs, vbs, sinks, valid, dist, dots):
    nn, nt = dots[:2]
    kv = [h // A_GROUP for h in range(A_HEADS)]
    scs = [nt(qs[h], kbs[kv[h]]) for h in range(A_HEADS)]
    prs = []
    for h in range(A_HEADS):
        sc = scs[h] * (A_HEAD_DIM ** -0.5) - 2.0 ** -(h + 1) * dist
        sc = jnp.where(valid, sc, NEG)
        m = lax.stop_gradient(jnp.maximum(jnp.max(sc, -1, keepdims=True), sinks[h]))
        pr = jnp.exp(sc - m)
        den = jnp.sum(pr, -1, keepdims=True) + jnp.exp(sinks[h] - m)
        prs.append(pr / den)
    return [nn(prs[h], vbs[kv[h]]) for h in range(A_HEADS)]


A_Q_ROWS = 2 * CHUNK


def _attn_band_consts(r0):
    band = A_WINDOW + A_Q_ROWS
    qi = lax.broadcasted_iota(jnp.int32, (A_Q_ROWS, band), 0)
    kj = lax.broadcasted_iota(jnp.int32, (A_Q_ROWS, band), 1)
    dist = jnp.abs(qi + A_WINDOW - kj).astype(f32)
    qc, kc = qi // CHUNK, kj // CHUNK
    valid = ((kj + r0) >= A_WINDOW) & (kc >= qc) & (kc <= qc + A_WINDOW // CHUNK)
    return dist, valid


def attn_fwd(qkv, sinks, nb, *, name):
    t = qkv.shape[0]
    s = t // nb
    band = A_WINDOW + A_Q_ROWS
    hd = A_HEAD_DIM

    def body(qkv_ref, sink_ref, o_ref, kvpad):
        kvpad[0:A_WINDOW, :] = jnp.zeros((A_WINDOW, 2 * A_KV_WIDTH), f32)
        kvpad[A_WINDOW:, :] = qkv_ref[:, A_WIDTH:]

        def chunk(n, carry):
            r0 = pl.multiple_of(n * A_Q_ROWS, A_Q_ROWS)
            dist, valid = _attn_band_consts(r0)
            kbs = [kvpad[pl.ds(r0, band), kvh * hd:(kvh + 1) * hd] for kvh in range(A_KV_HEADS)]
            vbs = [kvpad[pl.ds(r0, band), A_KV_WIDTH + kvh * hd:A_KV_WIDTH + (kvh + 1) * hd]
                   for kvh in range(A_KV_HEADS)]
            qs = [qkv_ref[pl.ds(r0, A_Q_ROWS), h * hd:(h + 1) * hd] for h in range(A_HEADS)]
            outs = _attn_heads(qs, kbs, vbs, [sink_ref[:, h:h + 1] for h in range(A_HEADS)], valid, dist, RAW_DOTS)
            for h in range(A_HEADS):
                o_ref[pl.ds(r0, A_Q_ROWS), h * hd:(h + 1) * hd] = outs[h]
            return carry

        lax.fori_loop(0, s // A_Q_ROWS, chunk, 0)

    return pl.pallas_call(
        body, grid=(nb,),
        in_specs=[pl.BlockSpec((s, A_WIDTH + 2 * A_KV_WIDTH), lambda b: (b, 0)),
                  pl.BlockSpec((1, A_HEADS), lambda b: (0, 0))],
        out_specs=pl.BlockSpec((s, A_WIDTH), lambda b: (b, 0)),
        out_shape=jax.ShapeDtypeStruct((t, A_WIDTH), f32),
        scratch_shapes=[pltpu.VMEM((s + A_WINDOW, 2 * A_KV_WIDTH), f32)],
        compiler_params=_params(("parallel",)), name=name,
    )(qkv, sinks)


def attn_bwd(qkv, sinks, do, nb, *, name):
    t = qkv.shape[0]
    s = t // nb
    band = A_WINDOW + A_Q_ROWS
    hd = A_HEAD_DIM
    kvw = 2 * A_KV_WIDTH

    def body(qkv_ref, sink_ref, do_ref, dqkv_ref, dsink_ref, kvpad, dkvpad):
        kvpad[0:A_WINDOW, :] = jnp.zeros((A_WINDOW, kvw), f32)
        kvpad[A_WINDOW:, :] = qkv_ref[:, A_WIDTH:]
        dkvpad[...] = jnp.zeros((s + A_WINDOW, kvw), f32)

        def chunk(n, dsinks):
            r0 = pl.multiple_of(n * A_Q_ROWS, A_Q_ROWS)
            dist, valid = _attn_band_consts(r0)
            ksl = [slice(kvh * hd, (kvh + 1) * hd) for kvh in range(A_KV_HEADS)]
            vsl = [slice(A_KV_WIDTH + kvh * hd, A_KV_WIDTH + (kvh + 1) * hd) for kvh in range(A_KV_HEADS)]
            kbs = [kvpad[pl.ds(r0, band), sl] for sl in ksl]
            vbs = [kvpad[pl.ds(r0, band), sl] for sl in vsl]
            dkbs = [dkvpad[pl.ds(r0, band), sl] for sl in ksl]
            dvbs = [dkvpad[pl.ds(r0, band), sl] for sl in vsl]
            qs = [qkv_ref[pl.ds(r0, A_Q_ROWS), h * hd:(h + 1) * hd] for h in range(A_HEADS)]
            dos = [do_ref[pl.ds(r0, A_Q_ROWS), h * hd:(h + 1) * hd] for h in range(A_HEADS)]
            fn = functools.partial(_attn_heads, valid=valid, dist=dist, dots=VJP_DOTS)
            _, vjp = jax.vjp(fn, qs, kbs, vbs, [sink_ref[:, h:h + 1] for h in range(A_HEADS)])
            dqs, dks, dvs, dss = vjp(dos)
            for h in range(A_HEADS):
                dqkv_ref[pl.ds(r0, A_Q_ROWS), h * hd:(h + 1) * hd] = dqs[h]
            for kvh in range(A_KV_HEADS):
                dkvpad[pl.ds(r0, band), ksl[kvh]] = dkbs[kvh] + dks[kvh]
                dkvpad[pl.ds(r0, band), vsl[kvh]] = dvbs[kvh] + dvs[kvh]
            return tuple(dsinks[h] + dss[h] for h in range(A_HEADS))

        dsinks = lax.fori_loop(0, s // A_Q_ROWS, chunk, tuple(jnp.zeros((1, 1), f32) for _ in range(A_HEADS)))
        dqkv_ref[:, A_WIDTH:] = dkvpad[A_WINDOW:, :]
        first = pl.program_id(0) == 0
        for h in range(A_HEADS):
            @pl.when(first)
            def _():
                dsink_ref[:, h:h + 1] = dsinks[h]

            @pl.when(jnp.logical_not(first))
            def _():
                dsink_ref[:, h:h + 1] += dsinks[h]

    wq = A_WIDTH + kvw
    return pl.pallas_call(
        body, grid=(nb,),
        in_specs=[pl.BlockSpec((s, wq), lambda b: (b, 0)), pl.BlockSpec((1, A_HEADS), lambda b: (0, 0)),
                  pl.BlockSpec((s, A_WIDTH), lambda b: (b, 0))],
        out_specs=[pl.BlockSpec((s, wq), lambda b: (b, 0)), pl.BlockSpec((1, A_HEADS), lambda b: (0, 0))],
        out_shape=[jax.ShapeDtypeStruct((t, wq), f32), jax.ShapeDtypeStruct((1, A_HEADS), f32)],
        scratch_shapes=[pltpu.VMEM((s + A_WINDOW, kvw), f32), pltpu.VMEM((s + A_WINDOW, kvw), f32)],
        compiler_params=_params(("arbitrary",)), name=name,
    )(qkv, sinks, do)


def _rg_gates(xc, wa, wx, ba, bx, lam, nn):
    r = jax.nn.sigmoid(nn(xc, wa) + ba)
    i = jax.nn.sigmoid(nn(xc, wx) + bx)
    log_a = -RG_C * r * jax.nn.softplus(-lam)
    a = jnp.exp(log_a)
    mult = jnp.sqrt(-jnp.tanh(log_a) * (jnp.exp(2.0 * log_a) + 1.0))
    return a, mult * (i * xc)


def _linear_scan(a, u, reverse):
    s = a.shape[0]
    t = lax.broadcasted_iota(jnp.int32, a.shape, 0)
    d = 1
    while d < s:
        if reverse:
            keep = t < s - d
            shift = s - d
        else:
            keep = t >= d
            shift = d
        us = jnp.where(keep, pltpu.roll(u, shift, 0), 0.0)
        as_ = jnp.where(keep, pltpu.roll(a, shift, 0), 1.0)
        u = u + a * us
        a = a * as_
        d *= 2
    return u


def rglru_fwd(xc, bg, wa, wx, ba, bx, lam, nb, *, name):
    t, c = xc.shape
    s = t // nb
    cw = GROUP_W

    def body(xc_ref, bg_ref, wa_ref, wx_ref, ba_ref, bx_ref, lam_ref, y_ref, h_ref):
        a, u = _rg_gates(xc_ref[...], wa_ref[...], wx_ref[...], ba_ref[...], bx_ref[...], lam_ref[...], RAW_DOTS[0])
        h = _linear_scan(a, u, False)
        h_ref[...] = h
        y_ref[...] = h * jax.nn.gelu(bg_ref[...])

    slab = pl.BlockSpec((s, cw), lambda b, g: (b, g))
    wsp = pl.BlockSpec((None, cw, cw), lambda b, g: (g, 0, 0))
    vec = pl.BlockSpec((1, cw), lambda b, g: (0, g))
    return pl.pallas_call(
        body, grid=(nb, c // cw), in_specs=[slab, slab, wsp, wsp, vec, vec, vec], out_specs=[slab, slab],
        out_shape=[jax.ShapeDtypeStruct((t, c), f32)] * 2,
        compiler_params=_params(("parallel", "parallel")), name=name,
    )(xc, bg, wa, wx, ba, bx, lam)


def rglru_bwd(xc, bg, h, dy, wa, wx, ba, bx, lam, nb, *, name):
    t, c = xc.shape
    s = t // nb
    cw = GROUP_W

    def body(xc_ref, bg_ref, h_ref, dy_ref, wa_ref, wx_ref, ba_ref, bx_ref, lam_ref,
             dxc_ref, dbg_ref, dwa_ref, dwx_ref, dba_ref, dbx_ref, dlam_ref):
        h = h_ref[...]
        dy_ = dy_ref[...]
        gel, gel_vjp = jax.vjp(jax.nn.gelu, bg_ref[...])
        dbg_ref[...] = gel_vjp(dy_ * h)[0]
        dh = dy_ * gel
        gates = functools.partial(_rg_gates, nn=_bnn)
        (a, _), gates_vjp = jax.vjp(gates, xc_ref[...], wa_ref[...], wx_ref[...], ba_ref[...], bx_ref[...],
                                    lam_ref[...])
        ti = lax.broadcasted_iota(jnp.int32, a.shape, 0)
        a_next = jnp.where(ti < s - 1, pltpu.roll(a, s - 1, 0), 0.0)
        lam_t = _linear_scan(a_next, dh, True)
        h_prev = jnp.where(ti >= 1, pltpu.roll(h, 1, 0), 0.0)
        dxc, dwa, dwx, dba, dbx, dlam = gates_vjp((lam_t * h_prev, lam_t))
        dxc_ref[...] = dxc
        first = pl.program_id(1) == 0

        @pl.when(first)
        def _():
            dwa_ref[...] = dwa
            dwx_ref[...] = dwx
            dba_ref[...] = dba
            dbx_ref[...] = dbx
            dlam_ref[...] = dlam

        @pl.when(jnp.logical_not(first))
        def _():
            dwa_ref[...] += dwa
            dwx_ref[...] += dwx
            dba_ref[...] += dba
            dbx_ref[...] += dbx
            dlam_ref[...] += dlam

    slab = pl.BlockSpec((s, cw), lambda g, b: (b, g))
    wsp = pl.BlockSpec((None, cw, cw), lambda g, b: (g, 0, 0))
    vec = pl.BlockSpec((1, cw), lambda g, b: (0, g))
    ng = c // cw
    return pl.pallas_call(
        body, grid=(ng, nb), in_specs=[slab, slab, slab, slab, wsp, wsp, vec, vec, vec],
        out_specs=[slab, slab, wsp, wsp, vec, vec, vec],
        out_shape=[jax.ShapeDtypeStruct((t, c), f32), jax.ShapeDtypeStruct((t, c), f32),
                   jax.ShapeDtypeStruct((ng, cw, cw), f32), jax.ShapeDtypeStruct((ng, cw, cw), f32),
                   jax.ShapeDtypeStruct((1, c), f32), jax.ShapeDtypeStruct((1, c), f32),
                   jax.ShapeDtypeStruct((1, c), f32)],
        compiler_params=_params(("parallel", "arbitrary")), name=name,
    )(xc, bg, h, dy, wa, wx, ba, bx, lam)


def _gdn_chunks_prep(qs, ks, vs, bls, als, a_log, dt_b, dots):
    nn, nt, csum = dots[0], dots[1], dots[3]
    hd = C_HEAD_DIM
    ri = lax.broadcasted_iota(jnp.int32, (CHUNK, CHUNK), 0)
    ci = lax.broadcasted_iota(jnp.int32, (CHUNK, CHUNK), 1)
    tril = ri >= ci
    strict = ri > ci
    eye = (ri == ci).astype(f32)
    qn = [q * lax.rsqrt(jnp.sum(q * q, -1, keepdims=True) + NORM_EPS) * (hd ** -0.5) for q in qs]
    kn = [k * lax.rsqrt(jnp.sum(k * k, -1, keepdims=True) + NORM_EPS) for k in ks]
    beta = [jax.nn.sigmoid(bl) for bl in bls]
    g = [-jnp.exp(a_log) * jax.nn.softplus(al + dt_b) for al in als]
    gc_sq = [csum(jnp.broadcast_to(g_, (CHUNK, CHUNK))) for g_ in g]
    gc = [csum(jnp.broadcast_to(g_, (CHUNK, hd))) for g_ in g]
    decay = [jnp.where(tril, jnp.exp(jnp.where(tril, s - s.T, 0.0)), 0.0) for s in gc_sq]
    kb = _each(jnp.multiply, kn, beta)
    kk = _each(nt, kb, kn)
    pw = [-jnp.where(strict, a * d, 0.0) for a, d in zip(kk, decay)]
    inv = [eye + p_ for p_ in pw]
    for _ in range(5):
        pw = _each(nn, pw, pw)
        inv = _each(jnp.add, inv, _each(nn, inv, pw))
    egc = [jnp.exp(c_) for c_ in gc]
    u = _each(nn, inv, _each(jnp.multiply, vs, beta))
    w = _each(nn, inv, _each(jnp.multiply, kb, egc))
    attn = _each(jnp.multiply, _each(nt, qn, kn), decay)
    g_last = [jnp.sum(jnp.broadcast_to(g_, (CHUNK, hd)), 0, keepdims=True) for g_ in g]
    qg = _each(jnp.multiply, qn, egc)
    kdec = [k_ * jnp.exp(gl_ - c_) for k_, gl_, c_ in zip(kn, g_last, gc)]
    return [(qg[i], kdec[i], w[i], u[i], attn[i], jnp.exp(g_last[i])) for i in range(len(qs))]


def _gdn_heads_step(states, qgs, kdecs, ws, us, attns, gls, zs, ng, dots):
    nn, tn = dots[0], dots[2]
    v_new = _each(jnp.subtract, us, _each(nn, ws, states))
    o = _each(jnp.add, _each(nn, qgs, states), _each(nn, attns, v_new))
    new = [s * gl for s, gl in zip(states, gls)]
    new = _each(jnp.add, new, _each(tn, kdecs, v_new))
    y = [o_ * lax.rsqrt(jnp.mean(o_ * o_, -1, keepdims=True) + NORM_EPS) * ng * _silu(z) for o_, z in zip(o, zs)]
    return y, new


def _loop_unrolled(n, unroll, load, compute, store, init):
    u = unroll if n % unroll == 0 else 1

    def trip(i, carry):
        idx = [i * u + j for j in range(u)]
        loaded = [load(k) for k in idx]
        results = compute(loaded)
        for k, r in zip(idx, results):
            carry = store(k, r, carry)
        return carry

    return lax.fori_loop(0, n // u, trip, init)


def _pick_lane(x, lane):
    li = lax.broadcasted_iota(jnp.int32, x.shape, 1)
    return jnp.sum(jnp.where(li == lane, x, 0.0), 1, keepdims=True)


def _put_lane(col, lane, width):
    li = lax.broadcasted_iota(jnp.int32, (col.shape[0], width), 1)
    return jnp.where(li == lane, col, 0.0)


def _gdn_specs(s, nc):
    hd = C_HEAD_DIM
    head = lambda off: pl.BlockSpec((s, hd), lambda b, h, off=off: (b, off + h))
    attn = pl.BlockSpec((None, s, CHUNK), lambda b, h: (h, b, 0))
    gl = pl.BlockSpec((None, nc * SUBLANES, hd), lambda b, h: (h, b, 0))
    ba = pl.BlockSpec((s, LANES), lambda b, h: (b, 0))
    sc8 = pl.BlockSpec((1, C_HEADS), lambda b, h: (0, 0))
    return head, attn, gl, ba, sc8


def gdn_prep_fwd(qkv, ba, a_log, dt_b, nb, *, name):
    t = qkv.shape[0]
    s = t // nb
    nc = s // CHUNK
    hd = C_HEAD_DIM
    head, attn_sp, gl_sp, ba_sp, sc8 = _gdn_specs(s, nc)

    def body(q_ref, k_ref, v_ref, ba_ref, alog_ref, dtb_ref, qg_ref, kd_ref, w_ref, u_ref, at_ref, gl_ref):
        h = pl.program_id(1)
        a_log_h = _pick_lane(alog_ref[...], h)
        dt_b_h = _pick_lane(dtb_ref[...], h)

        def load(n):
            rows = pl.ds(pl.multiple_of(n * CHUNK, CHUNK), CHUNK)
            bav = ba_ref[rows, :]
            return q_ref[rows, :], k_ref[rows, :], v_ref[rows, :], _pick_lane(bav, h), _pick_lane(bav, C_HEADS + h)

        def compute(loaded):
            return _gdn_chunks_prep(*[list(x) for x in zip(*loaded)], a_log_h, dt_b_h, RAW_DOTS)

        def store(n, outs, carry):
            rows = pl.ds(pl.multiple_of(n * CHUNK, CHUNK), CHUNK)
            qg_ref[rows, :] = outs[0].astype(bf16)
            kd_ref[rows, :] = outs[1].astype(bf16)
            w_ref[rows, :] = outs[2].astype(bf16)
            u_ref[rows, :] = outs[3]
            at_ref[rows, :] = outs[4].astype(bf16)
            gl_ref[pl.ds(pl.multiple_of(n * SUBLANES, SUBLANES), SUBLANES), :] = jnp.broadcast_to(outs[5], (SUBLANES, hd))
            return carry

        _loop_unrolled(nc, PREP_FWD_UNROLL, load, compute, store, 0)

    big = jax.ShapeDtypeStruct((t, C_WIDTH), f32)
    bigb = jax.ShapeDtypeStruct((t, C_WIDTH), bf16)
    return pl.pallas_call(
        body, grid=(nb, C_HEADS),
        in_specs=[head(0), head(C_HEADS), head(2 * C_HEADS), ba_sp, sc8, sc8],
        out_specs=[head(0)] * 4 + [attn_sp, gl_sp],
        out_shape=[bigb, bigb, bigb, big, jax.ShapeDtypeStruct((C_HEADS, t, CHUNK), bf16),
                               jax.ShapeDtypeStruct((C_HEADS, nb * nc * SUBLANES, hd), f32)],
        compiler_params=_params(("parallel", "parallel")), name=name,
    )(qkv, qkv, qkv, ba, a_log, dt_b)


def gdn_prep_bwd(qkv, ba, a_log, dt_b, cts, nb, *, name):
    t = qkv.shape[0]
    s = t // nb
    nc = s // CHUNK
    hd = C_HEAD_DIM
    head, attn_sp, gl_sp, ba_sp, sc8 = _gdn_specs(s, nc)

    def body(q_ref, k_ref, v_ref, ba_ref, alog_ref, dtb_ref, cqg, ckd, cw_, cu, cat, cgl,
             dq_ref, dk_ref, dv_ref, dba_ref, dalog_ref, ddtb_ref):
        b = pl.program_id(0)
        h = pl.program_id(1)
        a_log_h = _pick_lane(alog_ref[...], h)
        dt_b_h = _pick_lane(dtb_ref[...], h)
        prep = functools.partial(_gdn_chunks_prep, dots=VJP_DOTS)

        @pl.when(h == 0)
        def _():
            dba_ref[...] = jnp.zeros((s, LANES), f32)

        def load(n):
            rows = pl.ds(pl.multiple_of(n * CHUNK, CHUNK), CHUNK)
            bav = ba_ref[rows, :]
            cgl_n = cgl[pl.ds(pl.multiple_of(n * SUBLANES, SUBLANES), SUBLANES), :][0:1, :]
            primals = (q_ref[rows, :], k_ref[rows, :], v_ref[rows, :], _pick_lane(bav, h), _pick_lane(bav, C_HEADS + h))
            return primals, (cqg[rows, :], ckd[rows, :], cw_[rows, :], cu[rows, :], cat[rows, :], cgl_n), dba_ref[rows, :]

        def compute(loaded):
            primals = [list(x) for x in zip(*[item[0] for item in loaded])]
            _, vjp = jax.vjp(prep, *primals, a_log_h, dt_b_h)
            dqs, dks, dvs, dbls, dals, dalog, ddtb = vjp([item[1] for item in loaded])
            zero = jnp.zeros((1, 1), f32)
            return [((dqs[i], dks[i], dvs[i], dbls[i], dals[i], dalog if i == 0 else zero, ddtb if i == 0 else zero),
                     loaded[i][2]) for i in range(len(loaded))]

        def store(n, res, carry):
            (dq, dk, dv, dbl, dal, dalog_n, ddtb_n), dba_old = res
            rows = pl.ds(pl.multiple_of(n * CHUNK, CHUNK), CHUNK)
            dq_ref[rows, :] = dq
            dk_ref[rows, :] = dk
            dv_ref[rows, :] = dv
            dba_ref[rows, :] = dba_old + _put_lane(dbl, h, LANES) + _put_lane(dal, C_HEADS + h, LANES)
            return carry[0] + dalog_n, carry[1] + ddtb_n

        da_log, ddt_b = _loop_unrolled(nc, PREP_BWD_UNROLL, load, compute, store,
                                       (jnp.zeros((1, 1), f32), jnp.zeros((1, 1), f32)))
        first = jnp.logical_and(b == 0, h == 0)

        @pl.when(first)
        def _():
            dalog_ref[...] = _put_lane(da_log, h, LANES)
            ddtb_ref[...] = _put_lane(ddt_b, h, LANES)

        @pl.when(jnp.logical_not(first))
        def _():
            dalog_ref[...] += _put_lane(da_log, h, LANES)
            ddtb_ref[...] += _put_lane(ddt_b, h, LANES)

    big = jax.ShapeDtypeStruct((t, C_WIDTH), f32)
    vec = pl.BlockSpec((1, LANES), lambda b, h: (0, 0))
    return pl.pallas_call(
        body, grid=(nb, C_HEADS),
        in_specs=[head(0), head(C_HEADS), head(2 * C_HEADS), ba_sp, sc8, sc8] + [head(0)] * 4 + [attn_sp, gl_sp],
        out_specs=[head(0)] * 3 + [ba_sp, vec, vec],
        out_shape=[big] * 3 + [jax.ShapeDtypeStruct((t, LANES), f32), jax.ShapeDtypeStruct((1, LANES), f32),
                               jax.ShapeDtypeStruct((1, LANES), f32)],
        compiler_params=_params(("arbitrary", "arbitrary")), name=name,
    )(qkv, qkv, qkv, ba, a_log, dt_b, *cts)


def _gdn_rec_specs(sb, nsb, hp, reverse):
    hd = C_HEAD_DIM
    ncb = sb // CHUNK
    blk = (lambda b, k: b * nsb + (nsb - 1 - k)) if reverse else (lambda b, k: b * nsb + k)
    wide = pl.BlockSpec((sb, hp * hd), lambda b, j, k: (blk(b, k), j))
    attn = pl.BlockSpec((hp, sb, CHUNK), lambda b, j, k: (j, blk(b, k), 0))
    gl = pl.BlockSpec((hp, ncb * SUBLANES, hd), lambda b, j, k: (j, blk(b, k), 0))
    ng = pl.BlockSpec((1, hd), lambda b, j, k: (0, 0))
    states = pl.BlockSpec((hp, ncb, hd, hd), lambda b, j, k: (j, blk(b, k), 0, 0))
    return wide, attn, gl, ng, states


def gdn_rec_fwd(qg, kdec, w, u, attn, gl, z, ng, nb, *, name):
    t = qg.shape[0]
    s = t // nb
    sb = min(s, GDN_TIME_BLOCK)
    nsb = s // sb
    hd = C_HEAD_DIM
    hp = C_HEADS_PER_STEP
    wide, attn_sp, gl_sp, ng_sp, st_sp = _gdn_rec_specs(sb, nsb, hp, False)

    def body(qg_ref, kd_ref, w_ref, u_ref, at_ref, gl_ref, z_ref, ng_ref, y_ref, st_ref, carry_ref):
        @pl.when(pl.program_id(2) == 0)
        def _():
            carry_ref[...] = jnp.zeros((hp, hd, hd), f32)

        def chunk(n, states):
            for j in range(hp):
                st_ref[j, n] = states[j]
            rows = pl.ds(pl.multiple_of(n * CHUNK, CHUNK), CHUNK)
            grow = pl.ds(pl.multiple_of(n * SUBLANES, SUBLANES), SUBLANES)
            cols = [slice(j * hd, (j + 1) * hd) for j in range(hp)]
            ins = [(qg_ref[rows, c], kd_ref[rows, c], w_ref[rows, c], u_ref[rows, c], at_ref[j, rows, :],
                    gl_ref[j, grow, :][0:1, :], z_ref[rows, c]) for j, c in enumerate(cols)]
            ys, new = _gdn_heads_step(list(states), *[list(x) for x in zip(*ins)], ng_ref[...], RAW_DOTS)
            for j in range(hp):
                y_ref[rows, cols[j]] = ys[j]
            return tuple(new)

        last = lax.fori_loop(0, sb // CHUNK, chunk, tuple(carry_ref[j] for j in range(hp)))
        for j in range(hp):
            carry_ref[j] = last[j]

    return pl.pallas_call(
        body, grid=(nb, C_HEADS // hp, nsb),
        in_specs=[wide] * 4 + [attn_sp, gl_sp, wide, ng_sp], out_specs=[wide, st_sp],
        out_shape=[jax.ShapeDtypeStruct((t, C_WIDTH), f32), jax.ShapeDtypeStruct((C_HEADS, t // CHUNK, hd, hd), f32)],
        scratch_shapes=[pltpu.VMEM((hp, hd, hd), f32)],
        compiler_params=_params(("parallel", "parallel", "arbitrary")), name=name,
    )(qg, kdec, w, u, attn, gl, z, ng)


def gdn_rec_bwd(qg, kdec, w, u, attn, gl, z, ng, states, dy, nb, *, name):
    t = qg.shape[0]
    s = t // nb
    sb = min(s, GDN_TIME_BLOCK)
    nsb = s // sb
    nc = sb // CHUNK
    hd = C_HEAD_DIM
    hp = C_HEADS_PER_STEP
    wide, attn_sp, gl_sp, ng_sp, st_sp = _gdn_rec_specs(sb, nsb, hp, True)

    def body(qg_ref, kd_ref, w_ref, u_ref, at_ref, gl_ref, z_ref, ng_ref, states, dy_ref,
             dqg_ref, dkd_ref, dw_ref, du_ref, dat_ref, dgl_ref, dz_ref, dng_ref, carry_ref):
        step = functools.partial(_gdn_heads_step, dots=VJP_DOTS)

        @pl.when(pl.program_id(2) == 0)
        def _():
            carry_ref[...] = jnp.zeros((hp, hd, hd), f32)

        def operands(n):
            rows = pl.ds(pl.multiple_of(n * CHUNK, CHUNK), CHUNK)
            grow = pl.ds(pl.multiple_of(n * SUBLANES, SUBLANES), SUBLANES)
            cols = [slice(j * hd, (j + 1) * hd) for j in range(hp)]
            return ([qg_ref[rows, c].astype(f32) for c in cols], [kd_ref[rows, c].astype(f32) for c in cols],
                    [w_ref[rows, c].astype(f32) for c in cols], [u_ref[rows, c] for c in cols],
                    [at_ref[j, rows, :].astype(f32) for j in range(hp)],
                    [gl_ref[j, grow, :][0:1, :] for j in range(hp)], [z_ref[rows, c] for c in cols])

        def bwd_chunk(i, carry):
            n = nc - 1 - i
            rows = pl.ds(pl.multiple_of(n * CHUNK, CHUNK), CHUNK)
            grow = pl.ds(pl.multiple_of(n * SUBLANES, SUBLANES), SUBLANES)
            dsts, dng = carry
            dys = [dy_ref[rows, j * hd:(j + 1) * hd] for j in range(hp)]
            _, vjp = jax.vjp(step, [states[j, n] for j in range(hp)], *operands(n), ng_ref[...])
            dst, dqg, dkd, dw, du, dat, dgl, dz, dng_n = vjp((dys, list(dsts)))
            for j in range(hp):
                cols = slice(j * hd, (j + 1) * hd)
                dqg_ref[rows, cols] = dqg[j]
                dkd_ref[rows, cols] = dkd[j]
                dw_ref[rows, cols] = dw[j]
                du_ref[rows, cols] = du[j]
                dat_ref[j, rows, :] = dat[j]
                dgl_ref[j, grow, :] = jnp.broadcast_to(dgl[j], (SUBLANES, hd))
                dz_ref[rows, cols] = dz[j]
            return tuple(dst), dng + dng_n

        dlast, dng = lax.fori_loop(0, nc, bwd_chunk,
                                   (tuple(carry_ref[j] for j in range(hp)), jnp.zeros((1, hd), f32)))
        for j in range(hp):
            carry_ref[j] = dlast[j]
        first = jnp.logical_and(jnp.logical_and(pl.program_id(0) == 0, pl.program_id(1) == 0), pl.program_id(2) == 0)

        @pl.when(first)
        def _():
            dng_ref[...] = dng

        @pl.when(jnp.logical_not(first))
        def _():
            dng_ref[...] += dng

    big = jax.ShapeDtypeStruct((t, C_WIDTH), f32)
    return pl.pallas_call(
        body, grid=(nb, C_HEADS // hp, nsb),
        in_specs=[wide] * 4 + [attn_sp, gl_sp, wide, ng_sp, st_sp, wide],
        out_specs=[wide] * 4 + [attn_sp, gl_sp, wide, ng_sp],
        out_shape=[big] * 4 + [jax.ShapeDtypeStruct(attn.shape, f32), jax.ShapeDtypeStruct(gl.shape, f32), big,
                               jax.ShapeDtypeStruct((1, hd), f32)],
        scratch_shapes=[pltpu.VMEM((hp, hd, hd), f32)],
        compiler_params=_params(("arbitrary", "arbitrary", "arbitrary")), name=name,
    )(qg, kdec, w, u, attn, gl, z, ng, states, dy)


def _blockdiag_slabs(w):
    per = GROUP_W // B_BLOCK
    slabs = jnp.zeros((B_BLOCKS // per, GROUP_W, GROUP_W), w.dtype)
    for h in range(B_BLOCKS):
        o = (h % per) * B_BLOCK
        slabs = slabs.at[h // per, o:o + B_BLOCK, o:o + B_BLOCK].set(w[h])
    return slabs


def _slab_blocks(slabs):
    per = GROUP_W // B_BLOCK
    return jnp.stack([slabs[h // per, (h % per) * B_BLOCK:(h % per + 1) * B_BLOCK,
                            (h % per) * B_BLOCK:(h % per + 1) * B_BLOCK] for h in range(B_BLOCKS)])


def _mixer_ab_fwd(x1, x1b, W, g, b, nb, tag):
    w_in = W["ab_w_in"][0].astype(bf16)
    o1, o2 = A_WIDTH + 2 * A_KV_WIDTH, A_WIDTH + 2 * A_KV_WIDTH + B_WIDTH
    w_qkv, w_bx, w_bg = w_in[:, :o1], w_in[:, o1:o2], w_in[:, o2:]
    pqkv = mm_nn(x1b,w_qkv, name=tag + "_in_qkv")
    pbx = mm_nn(x1b,w_bx, name=tag + "_in_bx")
    pbg = mm_nn(x1b,w_bg, name=tag + "_in_bg")
    ya = attn_fwd(pqkv, W["a_sinks"], nb, name=tag + "_attn_fwd")
    xc = conv_fwd(pbx, W["b_conv_w"][0], W["b_conv_b"], False, nb, name=tag + "_conv_fwd")
    wa_s, wx_s = _blockdiag_slabs(W["b_wa"][0]), _blockdiag_slabs(W["b_wx"][0])
    yb, hh = rglru_fwd(xc, pbg, wa_s, wx_s, W["b_ba"], W["b_bx"], W["b_lam"], nb, name=tag + "_rglru_fwd")
    w_out = W["ab_w_out"][0].astype(bf16)
    x2, z1 = proj_ln([ya, yb], [w_out[:A_WIDTH], w_out[A_WIDTH:]], x1, g, b, name=tag + "_out_ln")
    saved = (pqkv, pbx, pbg, ya, xc, yb, hh, wa_s, wx_s, w_qkv, w_bx, w_bg, w_out)
    return x2, z1, saved


def _mixer_ab_bwd(x1b, dz1, dz1b, W, saved, nb, tag):
    pqkv, pbx, pbg, ya, xc, yb, hh, wa_s, wx_s, w_qkv, w_bx, w_bg, w_out = saved
    dya = mm_nn(dz1b, w_out[:A_WIDTH].T, name=tag + "_dya")
    dyb = mm_nn(dz1b, w_out[A_WIDTH:].T, name=tag + "_dyb")
    dwo = jnp.concatenate([mm_tn(ya, dz1b, name=tag + "_dwo_a"), mm_tn(yb, dz1b, name=tag + "_dwo_b")], 0)
    dpqkv, dsinks = attn_bwd(pqkv, W["a_sinks"], dya, nb, name=tag + "_attn_bwd")
    dxc, dpbg, dwa_s, dwx_s, dba, dbx, dlam = rglru_bwd(xc, pbg, hh, dyb, wa_s, wx_s, W["b_ba"], W["b_bx"],
                                                       W["b_lam"], nb, name=tag + "_rglru_bwd")
    dpbx, dconv_w, dconv_b = conv_bwd(pbx, W["b_conv_w"][0], W["b_conv_b"], dxc, False, nb, name=tag + "_conv_bwd")
    dw_in = jnp.concatenate([mm_tn(x1b,dpqkv, name=tag + "_dwin_qkv"), mm_tn(x1b,dpbx, name=tag + "_dwin_bx"),
                             mm_tn(x1b,dpbg, name=tag + "_dwin_bg")], 1)
    dx1 = mm_nn(dpqkv, w_qkv.T, add=dz1, add_scale=DN_ALPHA, name=tag + "_dx_qkv")
    dx1 = mm_nn(dpbx, w_bx.T, add=dx1, name=tag + "_dx_bx")
    dx1 = mm_nn(dpbg, w_bg.T, add=dx1, name=tag + "_dx_bg")
    grads = {"ab_w_in": dw_in[None], "a_sinks": dsinks, "b_conv_w": dconv_w[None], "b_conv_b": dconv_b,
             "b_wa": _slab_blocks(dwa_s)[None], "b_ba": dba, "b_wx": _slab_blocks(dwx_s)[None], "b_bx": dbx,
             "b_lam": dlam, "ab_w_out": dwo[None]}
    return dx1, grads


def _mixer_c_fwd(x1, x1b, W, g, b, nb, tag):
    w_in = W["c_w_in"][0].astype(bf16)
    d = w_in.shape[0]
    o1, o2 = 3 * C_WIDTH, 4 * C_WIDTH
    w_qkv, w_z = w_in[:, :o1], w_in[:, o1:o2]
    w_ba = jnp.concatenate([w_in[:, o2:], jnp.zeros((d, LANES - 2 * C_HEADS), bf16)], 1)
    pqkv = mm_nn(x1b,w_qkv, name=tag + "_in_qkv")
    pz = mm_nn(x1b,w_z, name=tag + "_in_z")
    pba = mm_nn(x1b,w_ba, name=tag + "_in_ba")
    zero_b = jnp.zeros((1, o1), f32)
    qkvc = conv_fwd(pqkv, W["c_conv_w"][0], zero_b, True, nb, name=tag + "_conv_fwd")
    prep = gdn_prep_fwd(qkvc, pba, W["c_a_log"], W["c_dt_bias"], nb, name=tag + "_prep_fwd")
    yc, states = gdn_rec_fwd(*prep, pz, W["c_norm_g"], nb, name=tag + "_rec_fwd")
    w_out = W["c_w_out"][0].astype(bf16)
    x2, z1 = proj_ln([yc], [w_out], x1, g, b, name=tag + "_out_ln")
    saved = (pqkv, pz, pba, qkvc, prep, states, yc, w_qkv, w_z, w_ba, w_out, zero_b)
    return x2, z1, saved


def _mixer_c_bwd(x1b, dz1, dz1b, W, saved, nb, tag):
    pqkv, pz, pba, qkvc, prep, states, yc, w_qkv, w_z, w_ba, w_out, zero_b = saved
    dyc = mm_nn(dz1b, w_out.T, name=tag + "_dyc")
    dwo = mm_tn(yc, dz1b, name=tag + "_dwo")
    rec = gdn_rec_bwd(*prep, pz, W["c_norm_g"], states, dyc, nb, name=tag + "_rec_bwd")
    cts, dpz, dng = rec[:6], rec[6], rec[7]
    dq, dk, dv, dpba, dalog, ddtb = gdn_prep_bwd(qkvc, pba, W["c_a_log"], W["c_dt_bias"], cts, nb,
                                                 name=tag + "_prep_bwd")
    dqkvc = jnp.concatenate([dq, dk, dv], 1)
    dpqkv, dconv_w, _ = conv_bwd(pqkv, W["c_conv_w"][0], zero_b, dqkvc, True, nb, name=tag + "_conv_bwd")
    dw_in = jnp.concatenate([mm_tn(x1b,dpqkv, name=tag + "_dwin_qkv"), mm_tn(x1b,dpz, name=tag + "_dwin_z"),
                             mm_tn(x1b,dpba, name=tag + "_dwin_ba")[:, :2 * C_HEADS]], 1)
    dx1 = mm_nn(dpqkv, w_qkv.T, add=dz1, add_scale=DN_ALPHA, name=tag + "_dx_qkv")
    dx1 = mm_nn(dpz, w_z.T, add=dx1, name=tag + "_dx_z")
    dx1 = mm_nn(dpba, w_ba.T, add=dx1, name=tag + "_dx_ba")
    grads = {"c_w_in": dw_in[None], "c_conv_w": dconv_w[None], "c_a_log": dalog[:, :C_HEADS],
             "c_dt_bias": ddtb[:, :C_HEADS], "c_norm_g": dng, "c_w_out": dwo[None]}
    return dx1, grads


def _local_step(x, p, target, W, F):
    nb, s, d = x.shape
    t = nb * s
    h = x.reshape(t, d)
    tape = []
    for i in range(DEPTH):
        tag = f"l{i}"
        f1 = [F[k][i] for k in ("ffn1_wg", "ffn1_wu", "ffn1_wd")]
        f2 = [F[k][i] for k in ("ffn2_wg", "ffn2_wu", "ffn2_wd")]
        lg = [W["ln_g"][i, k][None] for k in range(3)]
        lb = [W["ln_b"][i, k][None] for k in range(3)]
        x1, z0, x1b = ffn_fwd(h, *f1, lg[0], lb[0], name=tag + "_ffn1_fwd")
        mixer = _mixer_ab_fwd if i % 2 == 0 else _mixer_c_fwd
        x2, z1, msaved = mixer(x1, x1b, W, lg[1], lb[1], nb, tag + "_mix")
        x3, z2, _ = ffn_fwd(x2, *f2, lg[2], lb[2], name=tag + "_ffn2_fwd")
        pi = p[i].reshape(t, -1)
        pw = (W["ple_wg"][i].astype(bf16), W["ple_bg"][i][None], W["ple_wp"][i].astype(bf16))
        x4 = ple_fwd(x3, pi, *pw, name=tag + "_ple_fwd")
        tape.append((h, z0, x1b, msaved, z1, x2, z2, x3, pi, pw, lg))
        h = x4
    dh, sq = loss_head(h, target.reshape(t, d), name="loss_head")
    loss = 0.5 * jnp.sum(sq) / d
    per_layer = [None] * DEPTH
    grads = {}
    df1 = df2 = None
    for i in reversed(range(DEPTH)):
        tag = f"l{i}"
        h_in, z0, x1b, msaved, z1, x2, z2, x3, pi, pw, lg = tape[i]
        dx3, dple_wg, dple_bg, dple_wp = ple_bwd(x3, pi, dh, pw[0], pw[0].T, pw[1], pw[2], name=tag + "_ple_bwd")
        dz2, dz2b, dg2, db2 = ln_bwd(z2, dx3, lg[2], name=tag + "_ln2_bwd")
        f1 = [F[k][i] for k in ("ffn1_wg", "ffn1_wu", "ffn1_wd")]
        f2 = [F[k][i] for k in ("ffn2_wg", "ffn2_wu", "ffn2_wd")]
        dgate, dup, *df2 = ffn_bwd_weights(x2.astype(bf16), dz2b, *f2, i, DEPTH, df2, name=tag + "_ffn2_bwd_w")
        dx2 = ffn_bwd_input(dgate, dup, f2[0], f2[1], dz2, name=tag + "_ffn2_bwd_x")
        dz1, dz1b, dg1, db1 = ln_bwd(z1, dx2, lg[1], name=tag + "_ln1_bwd")
        mixer_bwd = _mixer_ab_bwd if i % 2 == 0 else _mixer_c_bwd
        dx1, mgrads = mixer_bwd(x1b, dz1, dz1b, W, msaved, nb, tag + "_mix")
        grads.update(mgrads)
        dz0, dz0b, dg0, db0 = ln_bwd(z0, dx1, lg[0], name=tag + "_ln0_bwd")
        dgate, dup, *df1 = ffn_bwd_weights(h_in.astype(bf16), dz0b, *f1, i, DEPTH, df1, name=tag + "_ffn1_bwd_w")
        dh = ffn_bwd_input(dgate, dup, f1[0], f1[1], dz0, name=tag + "_ffn1_bwd_x")
        per_layer[i] = {"ln_g": jnp.concatenate([dg0, dg1, dg2], 0), "ln_b": jnp.concatenate([db0, db1, db2], 0),
                        "ple_wg": dple_wg, "ple_bg": dple_bg[0], "ple_wp": dple_wp}
    for k in per_layer[0]:
        grads[k] = jnp.stack([per_layer[i][k] for i in range(DEPTH)])
    grads.update(zip(("ffn1_wg", "ffn1_wu", "ffn1_wd"), df1))
    grads.update(zip(("ffn2_wg", "ffn2_wu", "ffn2_wd"), df2))
    return loss, dh.reshape(nb, s, d), grads


WEIGHT_NAMES = ("ffn1_wg", "ffn1_wu", "ffn1_wd", "ffn2_wg", "ffn2_wu", "ffn2_wd", "ln_g", "ln_b", "ple_wg", "ple_bg",
                "ple_wp", "ab_w_in", "a_sinks", "b_conv_w", "b_conv_b", "b_wa", "b_ba", "b_wx", "b_bx", "b_lam",
                "ab_w_out", "c_w_in", "c_conv_w", "c_a_log", "c_dt_bias", "c_norm_g", "c_w_out")
NATIVE_NAMES = WEIGHT_NAMES[:6]
PACKED_NAMES = WEIGHT_NAMES[6:]
SHARD_AXIS = {"ffn1_wg": 2, "ffn1_wu": 2, "ffn1_wd": 1, "ffn2_wg": 2, "ffn2_wu": 2, "ffn2_wd": 1, "ln_g": 2, "ln_b": 2,
              "ple_wg": 1, "ple_wp": 2, "ab_w_in": 2, "b_conv_w": 2, "ab_w_out": 1, "c_w_in": 2, "c_conv_w": 2,
              "c_w_out": 1}
N_CHIPS = 4
PACK_COLS = LANES
PACK_TILE_MULTIPLE = 256
ELEMENTWISE_BLOCK_ELEMS = 128 * 1024


def _row_tile(r, cols):
    return _tile(r, max(2 * SUBLANES, ELEMENTWISE_BLOCK_ELEMS // cols), 2 * SUBLANES)
MESH = pl.DeviceIdType.MESH
ANY = pl.BlockSpec(memory_space=pl.ANY)


def _tiled_dims(shape):
    w = shape[-1]
    r = 1
    for dim in shape[:-1]:
        r *= dim
    return r, w, -(-r // SUBLANES) * SUBLANES, -(-w // LANES) * LANES


def _pack(pieces, lead=()):
    k = len(lead)
    tiles = []
    for a in pieces:
        r, w, rp, wp = _tiled_dims(a.shape[k:])
        a2 = jnp.pad(a.reshape(lead + (r, w)), [(0, 0)] * k + [(0, rp - r), (0, wp - w)])
        a2 = a2.reshape(lead + (rp // SUBLANES, SUBLANES, wp // LANES, LANES))
        a2 = jnp.swapaxes(a2, k + 1, k + 2)
        tiles.append(a2.reshape(lead + (-1, SUBLANES, LANES)))
    flat = jnp.concatenate(tiles, axis=k)
    n = flat.shape[k]
    n_pad = -(-n // PACK_TILE_MULTIPLE) * PACK_TILE_MULTIPLE
    flat = jnp.pad(flat, [(0, 0)] * k + [(0, n_pad - n), (0, 0), (0, 0)])
    return flat.reshape(lead + (n_pad * SUBLANES, PACK_COLS))


def _unpack(pack, shapes, lead=()):
    k = len(lead)
    flat = pack.reshape(lead + (-1, SUBLANES, LANES))
    out, o = [], 0
    for shp in shapes:
        r, w, rp, wp = _tiled_dims(shp)
        n = (rp // SUBLANES) * (wp // LANES)
        a2 = lax.slice_in_dim(flat, o, o + n, axis=k).reshape(lead + (rp // SUBLANES, wp // LANES, SUBLANES, LANES))
        a2 = jnp.swapaxes(a2, k + 1, k + 2).reshape(lead + (rp, wp))
        a2 = lax.slice_in_dim(lax.slice_in_dim(a2, 0, r, axis=k), 0, w, axis=k + 1)
        out.append(a2.reshape(lead + tuple(shp)))
        o += n
    return out


def _mesh_position():
    x, y, c = lax.axis_index("x"), lax.axis_index("y"), lax.axis_index("c")
    chips = [(1 - x, y), (x, 1 - y), (1 - x, 1 - y)]
    return x, y, c, chips


def _remote(src, dst, send_sems, recv_sems, k, to):
    return pltpu.make_async_remote_copy(src_ref=src, dst_ref=dst, send_sem=send_sems.at[k], recv_sem=recv_sems.at[k],
                                        device_id=to, device_id_type=MESH)


def _sems(n):
    return pltpu.SemaphoreType.DMA((n,))


def place_slot(parts, slot, n_slots, dtype, from_slot, *, name):
    n = len(parts)
    r, cols = parts[0].shape[-2:]
    tr = _row_tile(r, cols)

    def body(s_ref, *refs):
        for a in range(n):
            refs[n + a][...] = refs[a][...].astype(dtype)

    dst = pl.BlockSpec((None, tr, cols), lambda i, s_ref: (s_ref[0], i, 0))
    src = dst if from_slot else pl.BlockSpec((tr, cols), lambda i, s_ref: (i, 0))
    return pl.pallas_call(
        body,
        grid_spec=pltpu.PrefetchScalarGridSpec(num_scalar_prefetch=1, grid=(r // tr,), in_specs=[src] * n,
                                               out_specs=[dst] * n),
        out_shape=[jax.ShapeDtypeStruct((n_slots, r, cols), dtype)] * n,
        compiler_params=_params(("parallel",)), name=name,
    )(slot, *parts)


def gather_shards(bufs, *, name):
    n = len(bufs)

    def body(*refs):
        out_refs = refs[n:2 * n]
        send_sems, recv_sems = refs[2 * n:]
        x, y, c, chips = _mesh_position()
        me = 2 * x + y
        sibling = (x, y, 1 - c)
        waits = []
        for j, (cx, cy) in enumerate(chips):
            for a in range(n):
                own = out_refs[a].at[me, c]
                cp = _remote(own, own, send_sems, recv_sems, 6 * a + j, (cx, cy, c))
                cp.start()
                waits.append(cp.wait_send)
        for j, (cx, cy) in enumerate(chips):
            for a in range(n):
                got = out_refs[a].at[2 * cx + cy, c]
                _remote(got, got, send_sems, recv_sems, 6 * a + j, (cx, cy, c)).wait_recv()
                fw = _remote(got, got, send_sems, recv_sems, 6 * a + 3 + j, sibling)
                fw.start()
                waits.append(fw.wait_send)
        for j, (cx, cy) in enumerate(chips):
            for a in range(n):
                got = out_refs[a].at[2 * cx + cy, 1 - c]
                _remote(got, got, send_sems, recv_sems, 6 * a + 3 + j, sibling).wait_recv()
        for wait in waits:
            wait()

    return pl.pallas_call(
        body, out_shape=[jax.ShapeDtypeStruct(b.shape, b.dtype) for b in bufs],
        in_specs=[ANY] * n, out_specs=[ANY] * n, scratch_shapes=[_sems(6 * n), _sems(6 * n)],
        input_output_aliases={a: a for a in range(n)}, name=name,
    )(*bufs)


def gather_slots_async(bufs, collective_id, *, name):
    n = len(bufs)
    refs = [jax.new_ref(b, memory_space=pltpu.MemorySpace.HBM) for b in bufs]

    @pl.kernel(mesh=plsc.ScalarSubcoreMesh(axis_name="sequencer", num_cores=1), name=name,
               scratch_types=(_sems(3 * n), _sems(3 * n)),
               compiler_params=pltpu.CompilerParams(collective_id=collective_id))
    def launch(send_sems, recv_sems):
        x, y, c, chips = _mesh_position()
        me = 2 * x + y
        barrier = pltpu.get_barrier_semaphore()
        for cx, cy in chips:
            pl.semaphore_signal(barrier, inc=1, device_id=(cx, cy, c), device_id_type=MESH)
        pl.semaphore_wait(barrier, len(chips))
        sends = []
        for j, (cx, cy) in enumerate(chips):
            for a in range(n):
                own = refs[a].at[me]
                cp = _remote(own, own, send_sems, recv_sems, 3 * a + j, (cx, cy, c))
                cp.start()
                sends.append(cp)
        for j, (cx, cy) in enumerate(chips):
            for a in range(n):
                got = refs[a].at[2 * cx + cy]
                _remote(got, got, send_sems, recv_sems, 3 * a + j, (cx, cy, c)).wait_recv()
        for cp in sends:
            cp.wait_send()

    launch()
    return [r[...] for r in refs]


def sibling_exchange(gs, *, name):
    n = len(gs)

    def body(*refs):
        g_refs, out_refs = refs[:n], refs[n:2 * n]
        send_sems, recv_sems = refs[2 * n:]
        x, y, c, _ = _mesh_position()
        cps = [_remote(g_refs[a].at[:, 1 - c], out_refs[a], send_sems, recv_sems, a, (x, y, 1 - c)) for a in range(n)]
        for cp in cps:
            cp.start()
        for cp in cps:
            cp.wait()

    return pl.pallas_call(
        body, out_shape=[jax.ShapeDtypeStruct(g.shape[:1] + g.shape[2:], g.dtype) for g in gs],
        in_specs=[ANY] * n, out_specs=[ANY] * n, scratch_shapes=[_sems(n), _sems(n)], name=name,
    )(*gs)


def add_own_half(gs, others, c_idx, dtype, *, name):
    n = len(gs)
    ns, _, r, cols = gs[0].shape
    tr = _row_tile(r, cols)

    def body(c_ref, *refs):
        for a in range(n):
            refs[2 * n + a][...] = (refs[a][...] + refs[n + a][...]).astype(dtype)

    own = pl.BlockSpec((None, None, tr, cols), lambda s, i, c_ref: (s, c_ref[0], i, 0))
    oth = pl.BlockSpec((None, tr, cols), lambda s, i, c_ref: (s, i, 0))
    return pl.pallas_call(
        body,
        grid_spec=pltpu.PrefetchScalarGridSpec(num_scalar_prefetch=1, grid=(ns, r // tr),
                                               in_specs=[own] * n + [oth] * n, out_specs=[oth] * n),
        out_shape=[jax.ShapeDtypeStruct((ns, r, cols), dtype)] * n,
        compiler_params=_params(("parallel", "parallel")), name=name,
    )(c_idx, *gs, *others)


def chip_exchange(ps, qs, *, name):
    n = len(ps)

    def body(*refs):
        p_refs, q_refs = refs[:n], refs[2 * n:3 * n]
        send_sems, recv_sems = refs[3 * n:]
        x, y, c, chips = _mesh_position()
        me = 2 * x + y
        waits = []
        for j, (cx, cy) in enumerate(chips):
            for a in range(n):
                cp = _remote(p_refs[a].at[2 * cx + cy], q_refs[a].at[me], send_sems, recv_sems, 3 * a + j, (cx, cy, c))
                cp.start()
                waits.append(cp.wait_send)
        for j, (cx, cy) in enumerate(chips):
            for a in range(n):
                got = q_refs[a].at[2 * cx + cy]
                _remote(got, got, send_sems, recv_sems, 3 * a + j, (cx, cy, c)).wait_recv()
        for wait in waits:
            wait()

    return pl.pallas_call(
        body, out_shape=[jax.ShapeDtypeStruct(q_.shape, q_.dtype) for q_ in qs], in_specs=[ANY] * (2 * n),
        out_specs=[ANY] * n, scratch_shapes=[_sems(3 * n), _sems(3 * n)],
        input_output_aliases={n + a: a for a in range(n)}, name=name,
    )(*ps, *qs)


def sum_slots(qs, *, name):
    n = len(qs)
    ns, r, cols = qs[0].shape
    tr = _row_tile(r, cols * ns)

    def body(*refs):
        for a in range(n):
            q_ref = refs[a]
            acc = q_ref[0].astype(f32) + q_ref[1].astype(f32)
            for i in range(2, ns):
                acc = acc + q_ref[i].astype(f32)
            refs[n + a][...] = acc

    return pl.pallas_call(
        body, grid=(r // tr,), in_specs=[pl.BlockSpec((ns, tr, cols), lambda i: (0, i, 0))] * n,
        out_specs=[pl.BlockSpec((tr, cols), lambda i: (i, 0))] * n,
        out_shape=[jax.ShapeDtypeStruct((r, cols), f32)] * n,
        compiler_params=_params(("parallel",)), name=name,
    )(*qs)


def sibling_share(bufs, *, name):
    n = len(bufs)

    def body(*refs):
        out_refs = refs[n:2 * n]
        send_sems, recv_sems = refs[2 * n:]
        x, y, c, _ = _mesh_position()
        sibling = (x, y, 1 - c)
        cps = []
        for a in range(n):
            own = out_refs[a].at[c]
            cp = _remote(own, own, send_sems, recv_sems, a, sibling)
            cp.start()
            cps.append(cp)
        for a in range(n):
            theirs = out_refs[a].at[1 - c]
            _remote(theirs, theirs, send_sems, recv_sems, a, sibling).wait_recv()
        for cp in cps:
            cp.wait_send()

    return pl.pallas_call(
        body, out_shape=[jax.ShapeDtypeStruct(b.shape, b.dtype) for b in bufs], in_specs=[ANY] * n,
        out_specs=[ANY] * n, scratch_shapes=[_sems(n), _sems(n)],
        input_output_aliases={a: a for a in range(n)}, name=name,
    )(*bufs)


def adamw(ws, gs, ms, vs, *, name):
    n = len(ws)
    r, cols = ws[0].shape
    tr = _row_tile(r, cols)

    def body(*refs):
        for a in range(n):
            w_ref, g_ref, m_ref, v_ref = (refs[k * n + a] for k in range(4))
            d_ref, m2_ref, v2_ref = (refs[(4 + k) * n + a] for k in range(3))
            g_ = g_ref[...]
            m2 = ADAM_B1 * m_ref[...] + (1.0 - ADAM_B1) * g_
            v2 = ADAM_B2 * v_ref[...] + (1.0 - ADAM_B2) * (g_ * g_)
            m_hat = m2 / (1.0 - ADAM_B1 ** ADAM_STEP)
            v_hat = v2 / (1.0 - ADAM_B2 ** ADAM_STEP)
            d_ref[...] = -ADAM_LR * (m_hat / (jnp.sqrt(v_hat) + ADAM_EPS) + ADAM_WD * w_ref[...])
            m2_ref[...] = m2
            v2_ref[...] = v2

    row = pl.BlockSpec((tr, cols), lambda i: (i, 0))
    out = pl.pallas_call(
        body, grid=(r // tr,), in_specs=[row] * (4 * n), out_specs=[row] * (3 * n),
        out_shape=[jax.ShapeDtypeStruct((r, cols), f32)] * (3 * n),
        compiler_params=_params(("parallel",)), name=name,
    )(*ws, *gs, *ms, *vs)
    return out[:n], out[n:2 * n], out[2 * n:]


def _full_weights(gathered, local, shapes):
    pieces = _unpack(gathered, shapes, lead=(N_CHIPS,))
    full = {}
    for name, loc, pc in zip(PACKED_NAMES, local, pieces):
        ax = SHARD_AXIS.get(name)
        if ax is None:
            full[name] = loc
        else:
            shp = loc.shape
            full[name] = jnp.moveaxis(pc, 0, ax).reshape(shp[:ax] + (N_CHIPS * shp[ax],) + shp[ax + 1:])
    return full


def _grad_pack(grads, shapes):
    pieces = []
    for name, shp in zip(PACKED_NAMES, shapes):
        g = grads[name]
        ax = SHARD_AXIS.get(name)
        if ax is None:
            pieces.append(jnp.broadcast_to(g.reshape(shp)[None], (N_CHIPS,) + tuple(shp)))
        else:
            pieces.append(jnp.stack(jnp.split(g, N_CHIPS, axis=ax)))
    return _pack(pieces, lead=(N_CHIPS,))


def _by_shape(arrays):
    groups = {}
    for i, a in enumerate(arrays):
        groups.setdefault(a.shape, []).append(i)
    return list(groups.values())


def _grouped(fn, lists, n_out, tag):
    outs = [[None] * len(lists[0]) for _ in range(n_out)]
    for gi, idx in enumerate(_by_shape(lists[0])):
        res = fn(*[[lst[i] for i in idx] for lst in lists], name=f"{tag}_{gi}")
        res = res if n_out > 1 else (res,)
        for k in range(n_out):
            for i, r in zip(idx, res[k]):
                outs[k][i] = r
    return outs if n_out > 1 else outs[0]


def _train_step(x, p, loss_target, weights, m, v):
    packed_w = [weights[k] for k in PACKED_NAMES]
    shapes = [w.shape for w in packed_w]
    halves = lambda a: a.reshape((2, a.shape[0] // 2) + a.shape[1:])
    local = [weights[k] for k in NATIVE_NAMES] + [halves(_pack(packed_w))]
    local_m = [m[k] for k in NATIVE_NAMES] + [halves(_pack([m[k] for k in PACKED_NAMES]))]
    local_v = [v[k] for k in NATIVE_NAMES] + [halves(_pack([v[k] for k in PACKED_NAMES]))]
    flat = lambda lst: [a.reshape((-1, a.shape[-1])) for a in lst]
    c_idx = lax.axis_index("c").astype(jnp.int32).reshape(1)
    chip_idx = (2 * lax.axis_index("x") + lax.axis_index("y")).astype(jnp.int32).reshape(1)

    def placed(arrays, slot, n_slots, dtype, from_slot, tag):
        return _grouped(lambda a, name: place_slot(a, slot, n_slots, dtype, from_slot, name=name), [arrays], 1, tag)

    ffn_own = [weights[k][i] for i in range(DEPTH) for k in NATIVE_NAMES]
    ffn_bufs = placed(ffn_own, chip_idx, N_CHIPS, bf16, False, "place_ffn_weights")
    group = len(NATIVE_NAMES) // 2
    ffn_gathered = []
    for gi in range(0, len(ffn_bufs), group):
        ffn_gathered += gather_slots_async(ffn_bufs[gi:gi + group], collective_id=1 + gi // group,
                                           name=f"comm_gather_ffn_{gi // group}")
    ffn_weights = {k: [ffn_gathered[i * len(NATIVE_NAMES) + j] for i in range(DEPTH)] for j, k in enumerate(NATIVE_NAMES)}
    pack_buf = placed(flat(local[-1:]), chip_idx, N_CHIPS, f32, False, "place_packed_weights")[0]
    gathered_pack = gather_shards([pack_buf.reshape((N_CHIPS,) + local[-1].shape)], name="comm_gather_weights")[0]
    full = _full_weights(gathered_pack, packed_w, shapes)
    loss, grad_x, grads = _local_step(x, p, loss_target, full, ffn_weights)
    gpack = _grad_pack(grads, shapes)
    gs = [grads[k] for k in NATIVE_NAMES] + [gpack.reshape((N_CHIPS,) + local[-1].shape)]
    others = sibling_exchange(gs, name="comm_grad_sibling")
    nn_ = len(NATIVE_NAMES)
    chip_sums = _grouped(lambda a, b, name: add_own_half(a, b, c_idx, bf16, name=name), [gs[:nn_], others[:nn_]], 1,
                         "grad_add_sibling_ffn")
    chip_sums += add_own_half(gs[nn_:], others[nn_:], c_idx, f32, name="grad_add_sibling_packed")
    own = placed(chip_sums[:nn_], chip_idx, N_CHIPS, bf16, True, "place_own_partial_ffn")
    own += placed(chip_sums[nn_:], chip_idx, N_CHIPS, f32, True, "place_own_partial_packed")
    slots = chip_exchange(chip_sums, own, name="comm_grad_chips")
    mine = _grouped(sum_slots, [slots], 1, "grad_sum_chips")
    gsum = sibling_share(placed(mine, c_idx, 2, f32, False, "place_own_half"), name="comm_grad_share")
    delta, m2, v2 = _grouped(adamw, [flat(local), flat(gsum), flat(local_m), flat(local_v)], 3, "adamw")
    loss = lax.psum(loss, ("x", "y", "c"))
    outs = []
    for res in (gsum, delta, m2, v2):
        by_name = {k: a.reshape(weights[k].shape) for k, a in zip(NATIVE_NAMES, res[:-1])}
        by_name.update(zip(PACKED_NAMES, _unpack(res[-1], shapes)))
        outs += [by_name[k] for k in WEIGHT_NAMES]
    return (loss, grad_x, *outs)


def kernel(x, p, ffn1_wg, ffn1_wu, ffn1_wd, ffn2_wg, ffn2_wu, ffn2_wd, ln_g, ln_b, ple_wg, ple_bg, ple_wp, ab_w_in, a_sinks, b_conv_w, b_conv_b, b_wa, b_ba, b_wx, b_bx, b_lam, ab_w_out, c_w_in, c_conv_w, c_a_log, c_dt_bias, c_norm_g, c_w_out, loss_target, m_ffn1_wg, m_ffn1_wu, m_ffn1_wd, m_ffn2_wg, m_ffn2_wu, m_ffn2_wd, m_ln_g, m_ln_b, m_ple_wg, m_ple_bg, m_ple_wp, m_ab_w_in, m_a_sinks, m_b_conv_w, m_b_conv_b, m_b_wa, m_b_ba, m_b_wx, m_b_bx, m_b_lam, m_ab_w_out, m_c_w_in, m_c_conv_w, m_c_a_log, m_c_dt_bias, m_c_norm_g, m_c_w_out, v_ffn1_wg, v_ffn1_wu, v_ffn1_wd, v_ffn2_wg, v_ffn2_wu, v_ffn2_wd, v_ln_g, v_ln_b, v_ple_wg, v_ple_bg, v_ple_wp, v_ab_w_in, v_a_sinks, v_b_conv_w, v_b_conv_b, v_b_wa, v_b_ba, v_b_wx, v_b_bx, v_b_lam, v_ab_w_out, v_c_w_in, v_c_conv_w, v_c_a_log, v_c_dt_bias, v_c_norm_g, v_c_w_out):
    weights = [ffn1_wg, ffn1_wu, ffn1_wd, ffn2_wg, ffn2_wu, ffn2_wd, ln_g, ln_b, ple_wg, ple_bg, ple_wp, ab_w_in, a_sinks,
               b_conv_w, b_conv_b, b_wa, b_ba, b_wx, b_bx, b_lam, ab_w_out, c_w_in, c_conv_w, c_a_log, c_dt_bias, c_norm_g,
               c_w_out]
    m = [m_ffn1_wg, m_ffn1_wu, m_ffn1_wd, m_ffn2_wg, m_ffn2_wu, m_ffn2_wd, m_ln_g, m_ln_b, m_ple_wg, m_ple_bg, m_ple_wp,
         m_ab_w_in, m_a_sinks, m_b_conv_w, m_b_conv_b, m_b_wa, m_b_ba, m_b_wx, m_b_bx, m_b_lam, m_ab_w_out, m_c_w_in,
         m_c_conv_w, m_c_a_log, m_c_dt_bias, m_c_norm_g, m_c_w_out]
    v = [v_ffn1_wg, v_ffn1_wu, v_ffn1_wd, v_ffn2_wg, v_ffn2_wu, v_ffn2_wd, v_ln_g, v_ln_b, v_ple_wg, v_ple_bg, v_ple_wp,
         v_ab_w_in, v_a_sinks, v_b_conv_w, v_b_conv_b, v_b_wa, v_b_ba, v_b_wx, v_b_bx, v_b_lam, v_ab_w_out, v_c_w_in,
         v_c_conv_w, v_c_a_log, v_c_dt_bias, v_c_norm_g, v_c_w_out]
    return _train_step(x, p, loss_target, dict(zip(WEIGHT_NAMES, weights)), dict(zip(WEIGHT_NAMES, m)),
                       dict(zip(WEIGHT_NAMES, v)))
```

```python
import functools

import jax
import jax.numpy as jnp
from jax import lax
from jax.experimental import pallas as pl
from jax.experimental.pallas import tpu as pltpu
from jax.experimental.pallas import tpu_sc as plsc

f32 = jnp.float32
bf16 = jnp.bfloat16

DEPTH = 2
CHUNK = 64
A_HEADS, A_KV_HEADS, A_GROUP, A_HEAD_DIM = 8, 2, 4, 64
A_WIDTH, A_KV_WIDTH, A_WINDOW = 512, 128, 128
B_WIDTH, B_BLOCKS, B_BLOCK, B_CONV = 512, 8, 64, 4
RG_C = 8.0
C_HEADS, C_HEAD_DIM, C_WIDTH, C_CONV = 8, 128, 1024, 4
DN_ALPHA = (2.0 * DEPTH) ** 0.25
LN_EPS = 1e-5
NORM_EPS = 1e-6
NEG = -1e30
ADAM_LR, ADAM_B1, ADAM_B2, ADAM_EPS, ADAM_WD, ADAM_STEP = 0.001, 0.9, 0.999, 1e-08, 0.01, 10

VMEM_LIMIT_BYTES = 56 * 1024 * 1024
LANES = 128
SUBLANES = 8
GROUP_W = 128
PREP_FWD_UNROLL = 8
PREP_BWD_UNROLL = 8
C_HEADS_PER_STEP = 4
GDN_TIME_BLOCK = 512

NN = ((1,), (0,))
NT = ((1,), (1,))
TN = ((0,), (0,))


def _params(sem):
    return pltpu.CompilerParams(dimension_semantics=sem, vmem_limit_bytes=VMEM_LIMIT_BYTES)


def _tile(n, cap, mult):
    best = None
    t = mult
    while t <= min(n, cap):
        if n % t == 0:
            best = t
        t += mult
    return best if best is not None else n


def _bdot(a, b, dims):
    return lax.dot_general(a.astype(bf16), b.astype(bf16), (dims, ((), ())), preferred_element_type=f32)


def _running_sum(x, reverse):
    s = x.shape[0]
    t = lax.broadcasted_iota(jnp.int32, x.shape, 0)
    d = 1
    while d < s:
        if reverse:
            x = x + jnp.where(t < s - d, pltpu.roll(x, s - d, 0), 0.0)
        else:
            x = x + jnp.where(t >= d, pltpu.roll(x, d, 0), 0.0)
        d *= 2
    return x


@jax.custom_vjp
def _cumsum0(x):
    return _running_sum(x, False)


def _cumsum0_fwd(x):
    return _running_sum(x, False), None


def _cumsum0_bwd(_, g):
    return (_running_sum(g, True),)


_cumsum0.defvjp(_cumsum0_fwd, _cumsum0_bwd)


@jax.custom_vjp
def _bnn(a, b):
    return _bdot(a, b, NN)


def _bnn_fwd(a, b):
    return _bdot(a, b, NN), (a, b)


def _bnn_bwd(res, g):
    a, b = res
    return _bdot(g, b, NT), _bdot(a, g, TN)


_bnn.defvjp(_bnn_fwd, _bnn_bwd)


@jax.custom_vjp
def _bnt(a, b):
    return _bdot(a, b, NT)


def _bnt_fwd(a, b):
    return _bdot(a, b, NT), (a, b)


def _bnt_bwd(res, g):
    a, b = res
    return _bdot(g, b, NN), _bdot(g, a, TN)


_bnt.defvjp(_bnt_fwd, _bnt_bwd)


@jax.custom_vjp
def _btn(a, b):
    return _bdot(a, b, TN)


def _btn_fwd(a, b):
    return _bdot(a, b, TN), (a, b)


def _btn_bwd(res, g):
    a, b = res
    return _bdot(b, g, NT), _bdot(a, g, NN)


_btn.defvjp(_btn_fwd, _btn_bwd)

RAW_DOTS = (lambda a, b: _bdot(a, b, NN), lambda a, b: _bdot(a, b, NT), lambda a, b: _bdot(a, b, TN),
            lambda x: _running_sum(x, False))
VJP_DOTS = (_bnn, _bnt, _btn, _cumsum0)


def _layer_norm(z, g, b):
    mu = jnp.mean(z, -1, keepdims=True)
    d = z - mu
    var = jnp.mean(d * d, -1, keepdims=True)
    return d * lax.rsqrt(var + LN_EPS) * g + b


def _silu(x):
    return x * jax.nn.sigmoid(x)


def mm_nn(a, w, add=None, add_scale=1.0, *, name):
    m, k = a.shape
    n = w.shape[1]
    tm = _tile(m, 512, SUBLANES)
    tn = _tile(n, 1024, LANES)

    def body(*refs):
        if add is None:
            a_ref, w_ref, o_ref = refs
            o_ref[...] = _bdot(a_ref[...], w_ref[...], NN)
        else:
            a_ref, w_ref, add_ref, o_ref = refs
            o_ref[...] = _bdot(a_ref[...], w_ref[...], NN) + add_scale * add_ref[...]

    in_specs = [pl.BlockSpec((tm, k), lambda i, j: (i, 0)), pl.BlockSpec((k, tn), lambda i, j: (0, j))]
    args = [a, w]
    if add is not None:
        in_specs.append(pl.BlockSpec((tm, tn), lambda i, j: (i, j)))
        args.append(add)
    return pl.pallas_call(
        body, grid=(m // tm, n // tn), in_specs=in_specs,
        out_specs=pl.BlockSpec((tm, tn), lambda i, j: (i, j)),
        out_shape=jax.ShapeDtypeStruct((m, n), f32),
        compiler_params=_params(("parallel", "parallel")), name=name,
    )(*args)


def mm_tn(a, b, *, name):
    m, k = a.shape
    n = b.shape[1]
    tm = _tile(m, 1024, 2 * SUBLANES)
    tn = _tile(n, 1024, LANES)

    def body(a_ref, b_ref, o_ref):
        part = _bdot(a_ref[...], b_ref[...], TN)

        @pl.when(pl.program_id(1) == 0)
        def _():
            o_ref[...] = part

        @pl.when(pl.program_id(1) > 0)
        def _():
            o_ref[...] += part

    return pl.pallas_call(
        body, grid=(n // tn, m // tm),
        in_specs=[pl.BlockSpec((tm, k), lambda j, i: (i, 0)), pl.BlockSpec((tm, tn), lambda j, i: (i, j))],
        out_specs=pl.BlockSpec((k, tn), lambda j, i: (0, j)),
        out_shape=jax.ShapeDtypeStruct((k, n), f32),
        compiler_params=_params(("parallel", "arbitrary")), name=name,
    )(a, b)


def proj_ln(a_list, w_list, xres, g, b, *, name):
    t, d = xres.shape
    tm = _tile(t, 256, SUBLANES)
    na = len(a_list)

    def body(*refs):
        a_refs, w_refs = refs[:na], refs[na:2 * na]
        x_ref, g_ref, b_ref, y_ref, z_ref = refs[2 * na:]
        z = DN_ALPHA * x_ref[...]
        for a_ref, w_ref in zip(a_refs, w_refs):
            z = z + _bdot(a_ref[...], w_ref[...], NN)
        z_ref[...] = z
        y_ref[...] = _layer_norm(z, g_ref[...], b_ref[...])

    in_specs = [pl.BlockSpec((tm, a.shape[1]), lambda i: (i, 0)) for a in a_list]
    in_specs += [pl.BlockSpec(w.shape, lambda i: (0, 0)) for w in w_list]
    in_specs += [pl.BlockSpec((tm, d), lambda i: (i, 0)), pl.BlockSpec((1, d), lambda i: (0, 0)),
                 pl.BlockSpec((1, d), lambda i: (0, 0))]
    return pl.pallas_call(
        body, grid=(t // tm,), in_specs=in_specs,
        out_specs=[pl.BlockSpec((tm, d), lambda i: (i, 0))] * 2,
        out_shape=[jax.ShapeDtypeStruct((t, d), f32)] * 2,
        compiler_params=_params(("parallel",)), name=name,
    )(*a_list, *w_list, xres, g, b)


def ln_bwd(z, dy, g, *, name):
    t, d = z.shape
    tm = _tile(t, 512, SUBLANES)

    def body(z_ref, dy_ref, g_ref, dz_ref, dzb_ref, dg_ref, db_ref):
        zz = z_ref[...]
        dy_ = dy_ref[...]
        mu = jnp.mean(zz, -1, keepdims=True)
        dd = zz - mu
        var = jnp.mean(dd * dd, -1, keepdims=True)
        rstd = lax.rsqrt(var + LN_EPS)
        xhat = dd * rstd
        dxh = dy_ * g_ref[...]
        dz = rstd * (dxh - jnp.mean(dxh, -1, keepdims=True) - xhat * jnp.mean(dxh * xhat, -1, keepdims=True))
        dz_ref[...] = dz
        dzb_ref[...] = dz.astype(bf16)
        pg = jnp.sum(dy_ * xhat, 0, keepdims=True)
        pb = jnp.sum(dy_, 0, keepdims=True)

        @pl.when(pl.program_id(0) == 0)
        def _():
            dg_ref[...] = pg
            db_ref[...] = pb

        @pl.when(pl.program_id(0) > 0)
        def _():
            dg_ref[...] += pg
            db_ref[...] += pb

    row = pl.BlockSpec((tm, d), lambda i: (i, 0))
    vec = pl.BlockSpec((1, d), lambda i: (0, 0))
    return pl.pallas_call(
        body, grid=(t // tm,), in_specs=[row, row, vec], out_specs=[row, row, vec, vec],
        out_shape=[jax.ShapeDtypeStruct((t, d), f32), jax.ShapeDtypeStruct((t, d), bf16),
                   jax.ShapeDtypeStruct((1, d), f32), jax.ShapeDtypeStruct((1, d), f32)],
        compiler_params=_params(("arbitrary",)), name=name,
    )(z, dy, g)


def loss_head(y, target, *, name):
    t, d = y.shape
    tm = _tile(t, 512, SUBLANES)

    def body(y_ref, t_ref, dy_ref, sq_ref):
        e = y_ref[...] - t_ref[...]
        dy_ref[...] = e * (1.0 / d)
        part = jnp.sum(e * e, 0, keepdims=True)

        @pl.when(pl.program_id(0) == 0)
        def _():
            sq_ref[...] = part

        @pl.when(pl.program_id(0) > 0)
        def _():
            sq_ref[...] += part

    row = pl.BlockSpec((tm, d), lambda i: (i, 0))
    vec = pl.BlockSpec((1, d), lambda i: (0, 0))
    return pl.pallas_call(
        body, grid=(t // tm,), in_specs=[row, row], out_specs=[row, vec],
        out_shape=[jax.ShapeDtypeStruct((t, d), f32), jax.ShapeDtypeStruct((1, d), f32)],
        compiler_params=_params(("arbitrary",)), name=name,
    )(y, target)


FFN_COL_BLOCK = 256
FFN_ROWS = 1024


def _lane_blocks(n):
    return [slice(s, min(s + FFN_COL_BLOCK, n)) for s in range(0, n, FFN_COL_BLOCK)]


def ffn_fwd(x, wg, wu, wd, g, b, *, name):
    t, d = x.shape
    nf, _, tf = wg.shape
    tm = _tile(t, FFN_ROWS, SUBLANES)

    def body(x_ref, wg_ref, wu_ref, wd_ref, g_ref, b_ref, y_ref, z_ref, yb_ref, acc_ref):
        f = pl.program_id(1)
        xb = x_ref[...].astype(bf16)
        part, pending = None, None
        for cols in _lane_blocks(tf):
            gate_up = (_bdot(xb, wg_ref[:, cols], NN), _bdot(xb, wu_ref[:, cols], NN), cols)
            if pending is not None:
                down = _bdot(_silu(pending[0]) * pending[1], wd_ref[pending[2], :], NN)
                part = down if part is None else part + down
            pending = gate_up
        down = _bdot(_silu(pending[0]) * pending[1], wd_ref[pending[2], :], NN)
        part = down if part is None else part + down

        @pl.when(f == 0)
        def _():
            acc_ref[...] = part

        @pl.when(f > 0)
        def _():
            acc_ref[...] += part

        @pl.when(f == nf - 1)
        def _():
            z = DN_ALPHA * x_ref[...] + 0.5 * acc_ref[...]
            z_ref[...] = z
            y = _layer_norm(z, g_ref[...], b_ref[...])
            y_ref[...] = y
            yb_ref[...] = y.astype(bf16)

    row = pl.BlockSpec((tm, d), lambda i, j: (i, 0))
    vec = pl.BlockSpec((1, d), lambda i, j: (0, 0))
    wcol = pl.BlockSpec((None, d, tf), lambda i, j: (j, 0, 0))
    wrow = pl.BlockSpec((None, tf, d), lambda i, j: (j, 0, 0))
    return pl.pallas_call(
        body, grid=(t // tm, nf),
        in_specs=[row, wcol, wcol, wrow, vec, vec],
        out_specs=[row, row, row],
        out_shape=[jax.ShapeDtypeStruct((t, d), f32)] * 2 + [jax.ShapeDtypeStruct((t, d), bf16)],
        scratch_shapes=[pltpu.VMEM((tm, d), f32)],
        compiler_params=_params(("parallel", "arbitrary")), name=name,
    )(x, wg, wu, wd, g, b)


def ffn_bwd_weights(xb, dzb, wg, wu, wd, *, name):
    t, d = xb.shape
    nf, _, tf = wg.shape
    tm = _tile(t, FFN_ROWS, SUBLANES)

    def body(x_ref, dz_ref, wg_ref, wu_ref, wd_ref, *rest):
        dgate_ref, dup_ref, dwg_ref, dwu_ref, dwd_ref = rest[-5:]
        x = x_ref[...]
        dzh = dz_ref[...] * 0.5

        def first_half(cols):
            return _bdot(x, wg_ref[:, cols], NN), _bdot(x, wu_ref[:, cols], NN), _bdot(dzh, wd_ref[cols, :], NT), cols

        def second_half(gate, up, dh, cols):
            sg = jax.nn.sigmoid(gate)
            s = gate * sg
            dup = (dh * s).astype(bf16)
            dgate = (dh * up * (sg * (1.0 + gate * (1.0 - sg)))).astype(bf16)
            dgate_ref[:, cols] = dgate
            dup_ref[:, cols] = dup
            return _bdot(x, dgate, TN), _bdot(x, dup, TN), _bdot(s * up, dzh, TN), cols

        parts, pending = [], None
        for cols in _lane_blocks(tf):
            nxt = first_half(cols)
            if pending is not None:
                parts.append(second_half(*pending))
            pending = nxt
        parts.append(second_half(*pending))

        @pl.when(pl.program_id(1) == 0)
        def _():
            for pwg, pwu, pwd, cols in parts:
                dwg_ref[:, cols] = pwg
                dwu_ref[:, cols] = pwu
                dwd_ref[cols, :] = pwd

        @pl.when(pl.program_id(1) > 0)
        def _():
            for pwg, pwu, pwd, cols in parts:
                dwg_ref[:, cols] += pwg
                dwu_ref[:, cols] += pwu
                dwd_ref[cols, :] += pwd

    row = pl.BlockSpec((tm, d), lambda j, i: (i, 0))
    wcol = pl.BlockSpec((None, d, tf), lambda j, i: (j, 0, 0))
    wrow = pl.BlockSpec((None, tf, d), lambda j, i: (j, 0, 0))
    act = pl.BlockSpec((None, tm, tf), lambda j, i: (j, i, 0))
    return pl.pallas_call(
        body, grid=(nf, t // tm), in_specs=[row, row, wcol, wcol, wrow], out_specs=[act, act, wcol, wcol, wrow],
        out_shape=[jax.ShapeDtypeStruct((nf, t, tf), bf16), jax.ShapeDtypeStruct((nf, t, tf), bf16),
                   jax.ShapeDtypeStruct((nf, d, tf), f32), jax.ShapeDtypeStruct((nf, d, tf), f32),
                   jax.ShapeDtypeStruct((nf, tf, d), f32)],
        compiler_params=_params(("parallel", "arbitrary")), name=name,
    )(xb, dzb, wg, wu, wd)


def ffn_bwd_input(dgate, dup, wg, wu, dz, *, name):
    nf, t, tf = dgate.shape
    d = wg.shape[1]
    tm = _tile(t, FFN_ROWS // 2, SUBLANES)

    def body(dg_ref, du_ref, wg_ref, wu_ref, dz_ref, dx_ref):
        acc = DN_ALPHA * dz_ref[...]
        for j in range(nf):
            acc = acc + _bdot(dg_ref[j], wg_ref[j], NT) + _bdot(du_ref[j], wu_ref[j], NT)
        dx_ref[...] = acc

    act = pl.BlockSpec((nf, tm, tf), lambda i: (0, i, 0))
    wsp = pl.BlockSpec((nf, d, tf), lambda i: (0, 0, 0))
    row = pl.BlockSpec((tm, d), lambda i: (i, 0))
    return pl.pallas_call(
        body, grid=(t // tm,), in_specs=[act, act, wsp, wsp, row], out_specs=row,
        out_shape=jax.ShapeDtypeStruct((t, d), f32),
        compiler_params=_params(("parallel",)), name=name,
    )(dgate, dup, wg, wu, dz)


def ple_fwd(x, p, wg, bg, wp, *, name):
    t, d = x.shape
    dp = p.shape[1]
    tm = _tile(t, 512, SUBLANES)

    def body(x_ref, p_ref, wg_ref, bg_ref, wp_ref, o_ref):
        x_ = x_ref[...]
        gate = jax.nn.sigmoid(_bdot(x_, wg_ref[...], NN) + bg_ref[...])
        o_ref[...] = x_ + gate * _bdot(p_ref[...], wp_ref[...], NN)

    row = pl.BlockSpec((tm, d), lambda i: (i, 0))
    return pl.pallas_call(
        body, grid=(t // tm,),
        in_specs=[row, pl.BlockSpec((tm, dp), lambda i: (i, 0)), pl.BlockSpec((d, d), lambda i: (0, 0)),
                  pl.BlockSpec((1, d), lambda i: (0, 0)), pl.BlockSpec((dp, d), lambda i: (0, 0))],
        out_specs=row, out_shape=jax.ShapeDtypeStruct((t, d), f32),
        compiler_params=_params(("parallel",)), name=name,
    )(x, p, wg, bg, wp)


def ple_bwd(x, p, dy, wg, wgt, bg, wp, *, name):
    t, d = x.shape
    dp = p.shape[1]
    tm = _tile(t, 512, SUBLANES)

    def body(x_ref, p_ref, dy_ref, wg_ref, wgt_ref, bg_ref, wp_ref, dx_ref, dwg_ref, dbg_ref, dwp_ref):
        x_ = x_ref[...]
        dy_ = dy_ref[...]
        s = jax.nn.sigmoid(_bdot(x_, wg_ref[...], NN) + bg_ref[...])
        e = _bdot(p_ref[...], wp_ref[...], NN)
        da = dy_ * e * s * (1.0 - s)
        de = dy_ * s
        dx_ref[...] = dy_ + _bdot(da, wgt_ref[...], NN)
        pwg = _bdot(x_, da, TN)
        pbg = jnp.sum(da, 0, keepdims=True)
        pwp = _bdot(p_ref[...], de, TN)

        @pl.when(pl.program_id(0) == 0)
        def _():
            dwg_ref[...] = pwg
            dbg_ref[...] = pbg
            dwp_ref[...] = pwp

        @pl.when(pl.program_id(0) > 0)
        def _():
            dwg_ref[...] += pwg
            dbg_ref[...] += pbg
            dwp_ref[...] += pwp

    row = pl.BlockSpec((tm, d), lambda i: (i, 0))
    full = lambda shape: pl.BlockSpec(shape, lambda i: (0, 0))
    return pl.pallas_call(
        body, grid=(t // tm,),
        in_specs=[row, pl.BlockSpec((tm, dp), lambda i: (i, 0)), row, full((d, d)), full((d, d)), full((1, d)),
                  full((dp, d))],
        out_specs=[row, full((d, d)), full((1, d)), full((dp, d))],
        out_shape=[jax.ShapeDtypeStruct((t, d), f32), jax.ShapeDtypeStruct((d, d), f32),
                   jax.ShapeDtypeStruct((1, d), f32), jax.ShapeDtypeStruct((dp, d), f32)],
        compiler_params=_params(("arbitrary",)), name=name,
    )(x, p, dy, wg, wgt, bg, wp)


def _conv_taps(xpad_ref, w_ref, s):
    acc = w_ref[0:1, :] * xpad_ref[SUBLANES - 3:SUBLANES - 3 + s, :]
    for j in range(1, 4):
        acc = acc + w_ref[j:j + 1, :] * xpad_ref[SUBLANES - 3 + j:SUBLANES - 3 + j + s, :]
    return acc


def conv_fwd(x, w, bias, act, nb, *, name):
    t, c = x.shape
    s = t // nb
    cw = GROUP_W

    def body(x_ref, w_ref, b_ref, y_ref, xpad):
        xpad[0:SUBLANES, :] = jnp.zeros((SUBLANES, cw), f32)
        xpad[SUBLANES:, :] = x_ref[...]
        acc = _conv_taps(xpad, w_ref, s) + b_ref[...]
        y_ref[...] = _silu(acc) if act else acc

    slab = pl.BlockSpec((s, cw), lambda b, g: (b, g))
    return pl.pallas_call(
        body, grid=(nb, c // cw),
        in_specs=[slab, pl.BlockSpec((4, cw), lambda b, g: (0, g)), pl.BlockSpec((1, cw), lambda b, g: (0, g))],
        out_specs=slab, out_shape=jax.ShapeDtypeStruct((t, c), f32),
        scratch_shapes=[pltpu.VMEM((s + SUBLANES, cw), f32)],
        compiler_params=_params(("parallel", "parallel")), name=name,
    )(x, w, bias)


def conv_bwd(x, w, bias, dy, act, nb, *, name):
    t, c = x.shape
    s = t // nb
    cw = GROUP_W

    def body(x_ref, w_ref, b_ref, dy_ref, dx_ref, dw_ref, db_ref, xpad, dpad):
        xpad[0:SUBLANES, :] = jnp.zeros((SUBLANES, cw), f32)
        xpad[SUBLANES:, :] = x_ref[...]
        dacc = dy_ref[...]
        if act:
            acc = _conv_taps(xpad, w_ref, s) + b_ref[...]
            sg = jax.nn.sigmoid(acc)
            dacc = dacc * (sg * (1.0 + acc * (1.0 - sg)))
        dpad[0:s, :] = dacc
        dpad[s:, :] = jnp.zeros((SUBLANES, cw), f32)
        dx = w_ref[0:1, :] * dpad[3:3 + s, :]
        for j in range(1, 4):
            dx = dx + w_ref[j:j + 1, :] * dpad[3 - j:3 - j + s, :]
        dx_ref[...] = dx
        first = pl.program_id(1) == 0
        for j in range(4):
            pw = jnp.sum(dacc * xpad[SUBLANES - 3 + j:SUBLANES - 3 + j + s, :], 0, keepdims=True)

            @pl.when(first)
            def _():
                dw_ref[j:j + 1, :] = pw

            @pl.when(jnp.logical_not(first))
            def _():
                dw_ref[j:j + 1, :] += pw

        pb = jnp.sum(dacc, 0, keepdims=True)

        @pl.when(first)
        def _():
            db_ref[...] = pb

        @pl.when(jnp.logical_not(first))
        def _():
            db_ref[...] += pb

    slab = pl.BlockSpec((s, cw), lambda g, b: (b, g))
    wsp = pl.BlockSpec((4, cw), lambda g, b: (0, g))
    bsp = pl.BlockSpec((1, cw), lambda g, b: (0, g))
    return pl.pallas_call(
        body, grid=(c // cw, nb), in_specs=[slab, wsp, bsp, slab], out_specs=[slab, wsp, bsp],
        out_shape=[jax.ShapeDtypeStruct((t, c), f32), jax.ShapeDtypeStruct((4, c), f32),
                   jax.ShapeDtypeStruct((1, c), f32)],
        scratch_shapes=[pltpu.VMEM((s + SUBLANES, cw), f32), pltpu.VMEM((s + SUBLANES, cw), f32)],
        compiler_params=_params(("parallel", "arbitrary")), name=name,
    )(x, w, bias, dy)


def _each(f, *lists):
    return [f(*a) for a in zip(*lists)]


def _attn_heads(qs, kbs, vbs, sinks, valid, dist, dots):
    nn, nt = dots[:2]
    kv = [h // A_GROUP for h in range(A_HEADS)]
    scs = [nt(qs[h], kbs[kv[h]]) for h in range(A_HEADS)]
    prs = []
    for h in range(A_HEADS):
        sc = scs[h] * (A_HEAD_DIM ** -0.5) - 2.0 ** -(h + 1) * dist
        sc = jnp.where(valid, sc, NEG)
        m = lax.stop_gradient(jnp.maximum(jnp.max(sc, -1, keepdims=True), sinks[h]))
        pr = jnp.exp(sc - m)
        den = jnp.sum(pr, -1, keepdims=True) + jnp.exp(sinks[h] - m)
        prs.append(pr / den)
    return [nn(prs[h], vbs[kv[h]]) for h in range(A_HEADS)]


A_Q_ROWS = 2 * CHUNK


def _attn_band_consts(r0):
    band = A_WINDOW + A_Q_ROWS
    qi = lax.broadcasted_iota(jnp.int32, (A_Q_ROWS, band), 0)
    kj = lax.broadcasted_iota(jnp.int32, (A_Q_ROWS, band), 1)
    dist = jnp.abs(qi + A_WINDOW - kj).astype(f32)
    qc, kc = qi // CHUNK, kj // CHUNK
    valid = ((kj + r0) >= A_WINDOW) & (kc >= qc) & (kc <= qc + A_WINDOW // CHUNK)
    return dist, valid


def attn_fwd(qkv, sinks, nb, *, name):
    t = qkv.shape[0]
    s = t // nb
    band = A_WINDOW + A_Q_ROWS
    hd = A_HEAD_DIM

    def body(qkv_ref, sink_ref, o_ref, kvpad):
        kvpad[0:A_WINDOW, :] = jnp.zeros((A_WINDOW, 2 * A_KV_WIDTH), f32)
        kvpad[A_WINDOW:, :] = qkv_ref[:, A_WIDTH:]

        def chunk(n, carry):
            r0 = pl.multiple_of(n * A_Q_ROWS, A_Q_ROWS)
            dist, valid = _attn_band_consts(r0)
            kbs = [kvpad[pl.ds(r0, band), kvh * hd:(kvh + 1) * hd] for kvh in range(A_KV_HEADS)]
            vbs = [kvpad[pl.ds(r0, band), A_KV_WIDTH + kvh * hd:A_KV_WIDTH + (kvh + 1) * hd]
                   for kvh in range(A_KV_HEADS)]
            qs = [qkv_ref[pl.ds(r0, A_Q_ROWS), h * hd:(h + 1) * hd] for h in range(A_HEADS)]
            outs = _attn_heads(qs, kbs, vbs, [sink_ref[:, h:h + 1] for h in range(A_HEADS)], valid, dist, RAW_DOTS)
            for h in range(A_HEADS):
                o_ref[pl.ds(r0, A_Q_ROWS), h * hd:(h + 1) * hd] = outs[h]
            return carry

        lax.fori_loop(0, s // A_Q_ROWS, chunk, 0)

    return pl.pallas_call(
        body, grid=(nb,),
        in_specs=[pl.BlockSpec((s, A_WIDTH + 2 * A_KV_WIDTH), lambda b: (b, 0)),
                  pl.BlockSpec((1, A_HEADS), lambda b: (0, 0))],
        out_specs=pl.BlockSpec((s, A_WIDTH), lambda b: (b, 0)),
        out_shape=jax.ShapeDtypeStruct((t, A_WIDTH), f32),
        scratch_shapes=[pltpu.VMEM((s + A_WINDOW, 2 * A_KV_WIDTH), f32)],
        compiler_params=_params(("parallel",)), name=name,
    )(qkv, sinks)


def attn_bwd(qkv, sinks, do, nb, *, name):
    t = qkv.shape[0]
    s = t // nb
    band = A_WINDOW + A_Q_ROWS
    hd = A_HEAD_DIM
    kvw = 2 * A_KV_WIDTH

    def body(qkv_ref, sink_ref, do_ref, dqkv_ref, dsink_ref, kvpad, dkvpad):
        kvpad[0:A_WINDOW, :] = jnp.zeros((A_WINDOW, kvw), f32)
        kvpad[A_WINDOW:, :] = qkv_ref[:, A_WIDTH:]
        dkvpad[...] = jnp.zeros((s + A_WINDOW, kvw), f32)

        def chunk(n, dsinks):
            r0 = pl.multiple_of(n * A_Q_ROWS, A_Q_ROWS)
            dist, valid = _attn_band_consts(r0)
            ksl = [slice(kvh * hd, (kvh + 1) * hd) for kvh in range(A_KV_HEADS)]
            vsl = [slice(A_KV_WIDTH + kvh * hd, A_KV_WIDTH + (kvh + 1) * hd) for kvh in range(A_KV_HEADS)]
            kbs = [kvpad[pl.ds(r0, band), sl] for sl in ksl]
            vbs = [kvpad[pl.ds(r0, band), sl] for sl in vsl]
            dkbs = [dkvpad[pl.ds(r0, band), sl] for sl in ksl]
            dvbs = [dkvpad[pl.ds(r0, band), sl] for sl in vsl]
            qs = [qkv_ref[pl.ds(r0, A_Q_ROWS), h * hd:(h + 1) * hd] for h in range(A_HEADS)]
            dos = [do_ref[pl.ds(r0, A_Q_ROWS), h * hd:(h + 1) * hd] for h in range(A_HEADS)]
            fn = functools.partial(_attn_heads, valid=valid, dist=dist, dots=VJP_DOTS)
            _, vjp = jax.vjp(fn, qs, kbs, vbs, [sink_ref[:, h:h + 1] for h in range(A_HEADS)])
            dqs, dks, dvs, dss = vjp(dos)
            for h in range(A_HEADS):
                dqkv_ref[pl.ds(r0, A_Q_ROWS), h * hd:(h + 1) * hd] = dqs[h]
            for kvh in range(A_KV_HEADS):
                dkvpad[pl.ds(r0, band), ksl[kvh]] = dkbs[kvh] + dks[kvh]
                dkvpad[pl.ds(r0, band), vsl[kvh]] = dvbs[kvh] + dvs[kvh]
            return tuple(dsinks[h] + dss[h] for h in range(A_HEADS))

        dsinks = lax.fori_loop(0, s // A_Q_ROWS, chunk, tuple(jnp.zeros((1, 1), f32) for _ in range(A_HEADS)))
        dqkv_ref[:, A_WIDTH:] = dkvpad[A_WINDOW:, :]
        first = pl.program_id(0) == 0
        for h in range(A_HEADS):
            @pl.when(first)
            def _():
                dsink_ref[:, h:h + 1] = dsinks[h]

            @pl.when(jnp.logical_not(first))
            def _():
                dsink_ref[:, h:h + 1] += dsinks[h]

    wq = A_WIDTH + kvw
    return pl.pallas_call(
        body, grid=(nb,),
        in_specs=[pl.BlockSpec((s, wq), lambda b: (b, 0)), pl.BlockSpec((1, A_HEADS), lambda b: (0, 0)),
                  pl.BlockSpec((s, A_WIDTH), lambda b: (b, 0))],
        out_specs=[pl.BlockSpec((s, wq), lambda b: (b, 0)), pl.BlockSpec((1, A_HEADS), lambda b: (0, 0))],
        out_shape=[jax.ShapeDtypeStruct((t, wq), f32), jax.ShapeDtypeStruct((1, A_HEADS), f32)],
        scratch_shapes=[pltpu.VMEM((s + A_WINDOW, kvw), f32), pltpu.VMEM((s + A_WINDOW, kvw), f32)],
        compiler_params=_params(("arbitrary",)), name=name,
    )(qkv, sinks, do)


def _rg_gates(xc, wa, wx, ba, bx, lam, nn):
    r = jax.nn.sigmoid(nn(xc, wa) + ba)
    i = jax.nn.sigmoid(nn(xc, wx) + bx)
    log_a = -RG_C * r * jax.nn.softplus(-lam)
    a = jnp.exp(log_a)
    mult = jnp.sqrt(-jnp.tanh(log_a) * (jnp.exp(2.0 * log_a) + 1.0))
    return a, mult * (i * xc)


def _linear_scan(a, u, reverse):
    s = a.shape[0]
    t = lax.broadcasted_iota(jnp.int32, a.shape, 0)
    d = 1
    while d < s:
        if reverse:
            keep = t < s - d
            shift = s - d
        else:
            keep = t >= d
            shift = d
        us = jnp.where(keep, pltpu.roll(u, shift, 0), 0.0)
        as_ = jnp.where(keep, pltpu.roll(a, shift, 0), 1.0)
        u = u + a * us
        a = a * as_
        d *= 2
    return u


def rglru_fwd(xc, bg, wa, wx, ba, bx, lam, nb, *, name):
    t, c = xc.shape
    s = t // nb
    cw = GROUP_W

    def body(xc_ref, bg_ref, wa_ref, wx_ref, ba_ref, bx_ref, lam_ref, y_ref, h_ref):
        a, u = _rg_gates(xc_ref[...], wa_ref[...], wx_ref[...], ba_ref[...], bx_ref[...], lam_ref[...], RAW_DOTS[0])
        h = _linear_scan(a, u, False)
        h_ref[...] = h
        y_ref[...] = h * jax.nn.gelu(bg_ref[...])

    slab = pl.BlockSpec((s, cw), lambda b, g: (b, g))
    wsp = pl.BlockSpec((None, cw, cw), lambda b, g: (g, 0, 0))
    vec = pl.BlockSpec((1, cw), lambda b, g: (0, g))
    return pl.pallas_call(
        body, grid=(nb, c // cw), in_specs=[slab, slab, wsp, wsp, vec, vec, vec], out_specs=[slab, slab],
        out_shape=[jax.ShapeDtypeStruct((t, c), f32)] * 2,
        compiler_params=_params(("parallel", "parallel")), name=name,
    )(xc, bg, wa, wx, ba, bx, lam)


def rglru_bwd(xc, bg, h, dy, wa, wx, ba, bx, lam, nb, *, name):
    t, c = xc.shape
    s = t // nb
    cw = GROUP_W

    def body(xc_ref, bg_ref, h_ref, dy_ref, wa_ref, wx_ref, ba_ref, bx_ref, lam_ref,
             dxc_ref, dbg_ref, dwa_ref, dwx_ref, dba_ref, dbx_ref, dlam_ref):
        h = h_ref[...]
        dy_ = dy_ref[...]
        gel, gel_vjp = jax.vjp(jax.nn.gelu, bg_ref[...])
        dbg_ref[...] = gel_vjp(dy_ * h)[0]
        dh = dy_ * gel
        gates = functools.partial(_rg_gates, nn=_bnn)
        (a, _), gates_vjp = jax.vjp(gates, xc_ref[...], wa_ref[...], wx_ref[...], ba_ref[...], bx_ref[...],
                                    lam_ref[...])
        ti = lax.broadcasted_iota(jnp.int32, a.shape, 0)
        a_next = jnp.where(ti < s - 1, pltpu.roll(a, s - 1, 0), 0.0)
        lam_t = _linear_scan(a_next, dh, True)
        h_prev = jnp.where(ti >= 1, pltpu.roll(h, 1, 0), 0.0)
        dxc, dwa, dwx, dba, dbx, dlam = gates_vjp((lam_t * h_prev, lam_t))
        dxc_ref[...] = dxc
        first = pl.program_id(1) == 0

        @pl.when(first)
        def _():
            dwa_ref[...] = dwa
            dwx_ref[...] = dwx
            dba_ref[...] = dba
            dbx_ref[...] = dbx
            dlam_ref[...] = dlam

        @pl.when(jnp.logical_not(first))
        def _():
            dwa_ref[...] += dwa
            dwx_ref[...] += dwx
            dba_ref[...] += dba
            dbx_ref[...] += dbx
            dlam_ref[...] += dlam

    slab = pl.BlockSpec((s, cw), lambda g, b: (b, g))
    wsp = pl.BlockSpec((None, cw, cw), lambda g, b: (g, 0, 0))
    vec = pl.BlockSpec((1, cw), lambda g, b: (0, g))
    ng = c // cw
    return pl.pallas_call(
        body, grid=(ng, nb), in_specs=[slab, slab, slab, slab, wsp, wsp, vec, vec, vec],
        out_specs=[slab, slab, wsp, wsp, vec, vec, vec],
        out_shape=[jax.ShapeDtypeStruct((t, c), f32), jax.ShapeDtypeStruct((t, c), f32),
                   jax.ShapeDtypeStruct((ng, cw, cw), f32), jax.ShapeDtypeStruct((ng, cw, cw), f32),
                   jax.ShapeDtypeStruct((1, c), f32), jax.ShapeDtypeStruct((1, c), f32),
                   jax.ShapeDtypeStruct((1, c), f32)],
        compiler_params=_params(("parallel", "arbitrary")), name=name,
    )(xc, bg, h, dy, wa, wx, ba, bx, lam)


def _gdn_chunks_prep(qs, ks, vs, bls, als, a_log, dt_b, dots):
    nn, nt, csum = dots[0], dots[1], dots[3]
    hd = C_HEAD_DIM
    ri = lax.broadcasted_iota(jnp.int32, (CHUNK, CHUNK), 0)
    ci = lax.broadcasted_iota(jnp.int32, (CHUNK, CHUNK), 1)
    tril = ri >= ci
    strict = ri > ci
    eye = (ri == ci).astype(f32)
    qn = [q * lax.rsqrt(jnp.sum(q * q, -1, keepdims=True) + NORM_EPS) * (hd ** -0.5) for q in qs]
    kn = [k * lax.rsqrt(jnp.sum(k * k, -1, keepdims=True) + NORM_EPS) for k in ks]
    beta = [jax.nn.sigmoid(bl) for bl in bls]
    g = [-jnp.exp(a_log) * jax.nn.softplus(al + dt_b) for al in als]
    gc_sq = [csum(jnp.broadcast_to(g_, (CHUNK, CHUNK))) for g_ in g]
    gc = [csum(jnp.broadcast_to(g_, (CHUNK, hd))) for g_ in g]
    decay = [jnp.where(tril, jnp.exp(jnp.where(tril, s - s.T, 0.0)), 0.0) for s in gc_sq]
    kb = _each(jnp.multiply, kn, beta)
    kk = _each(nt, kb, kn)
    pw = [-jnp.where(strict, a * d, 0.0) for a, d in zip(kk, decay)]
    inv = [eye + p_ for p_ in pw]
    for _ in range(5):
        pw = _each(nn, pw, pw)
        inv = _each(jnp.add, inv, _each(nn, inv, pw))
    egc = [jnp.exp(c_) for c_ in gc]
    u = _each(nn, inv, _each(jnp.multiply, vs, beta))
    w = _each(nn, inv, _each(jnp.multiply, kb, egc))
    attn = _each(jnp.multiply, _each(nt, qn, kn), decay)
    g_last = [jnp.sum(jnp.broadcast_to(g_, (CHUNK, hd)), 0, keepdims=True) for g_ in g]
    qg = _each(jnp.multiply, qn, egc)
    kdec = [k_ * jnp.exp(gl_ - c_) for k_, gl_, c_ in zip(kn, g_last, gc)]
    return [(qg[i], kdec[i], w[i], u[i], attn[i], jnp.exp(g_last[i])) for i in range(len(qs))]


def _gdn_heads_step(states, qgs, kdecs, ws, us, attns, gls, zs, ng, dots):
    nn, tn = dots[0], dots[2]
    v_new = _each(jnp.subtract, us, _each(nn, ws, states))
    o = _each(jnp.add, _each(nn, qgs, states), _each(nn, attns, v_new))
    new = [s * gl for s, gl in zip(states, gls)]
    new = _each(jnp.add, new, _each(tn, kdecs, v_new))
    y = [o_ * lax.rsqrt(jnp.mean(o_ * o_, -1, keepdims=True) + NORM_EPS) * ng * _silu(z) for o_, z in zip(o, zs)]
    return y, new


def _loop_unrolled(n, unroll, load, compute, store, init):
    u = unroll if n % unroll == 0 else 1

    def trip(i, carry):
        idx = [i * u + j for j in range(u)]
        loaded = [load(k) for k in idx]
        results = compute(loaded)
        for k, r in zip(idx, results):
            carry = store(k, r, carry)
        return carry

    return lax.fori_loop(0, n // u, trip, init)


def _pick_lane(x, lane):
    li = lax.broadcasted_iota(jnp.int32, x.shape, 1)
    return jnp.sum(jnp.where(li == lane, x, 0.0), 1, keepdims=True)


def _put_lane(col, lane, width):
    li = lax.broadcasted_iota(jnp.int32, (col.shape[0], width), 1)
    return jnp.where(li == lane, col, 0.0)


def _gdn_specs(s, nc):
    hd = C_HEAD_DIM
    head = lambda off: pl.BlockSpec((s, hd), lambda b, h, off=off: (b, off + h))
    attn = pl.BlockSpec((None, s, CHUNK), lambda b, h: (h, b, 0))
    gl = pl.BlockSpec((None, nc * SUBLANES, hd), lambda b, h: (h, b, 0))
    ba = pl.BlockSpec((s, LANES), lambda b, h: (b, 0))
    sc8 = pl.BlockSpec((1, C_HEADS), lambda b, h: (0, 0))
    return head, attn, gl, ba, sc8


def gdn_prep_fwd(qkv, ba, a_log, dt_b, nb, *, name):
    t = qkv.shape[0]
    s = t // nb
    nc = s // CHUNK
    hd = C_HEAD_DIM
    head, attn_sp, gl_sp, ba_sp, sc8 = _gdn_specs(s, nc)

    def body(q_ref, k_ref, v_ref, ba_ref, alog_ref, dtb_ref, qg_ref, kd_ref, w_ref, u_ref, at_ref, gl_ref):
        h = pl.program_id(1)
        a_log_h = _pick_lane(alog_ref[...], h)
        dt_b_h = _pick_lane(dtb_ref[...], h)

        def load(n):
            rows = pl.ds(pl.multiple_of(n * CHUNK, CHUNK), CHUNK)
            bav = ba_ref[rows, :]
            return q_ref[rows, :], k_ref[rows, :], v_ref[rows, :], _pick_lane(bav, h), _pick_lane(bav, C_HEADS + h)

        def compute(loaded):
            return _gdn_chunks_prep(*[list(x) for x in zip(*loaded)], a_log_h, dt_b_h, RAW_DOTS)

        def store(n, outs, carry):
            rows = pl.ds(pl.multiple_of(n * CHUNK, CHUNK), CHUNK)
            qg_ref[rows, :] = outs[0].astype(bf16)
            kd_ref[rows, :] = outs[1].astype(bf16)
            w_ref[rows, :] = outs[2].astype(bf16)
            u_ref[rows, :] = outs[3]
            at_ref[rows, :] = outs[4].astype(bf16)
            gl_ref[pl.ds(pl.multiple_of(n * SUBLANES, SUBLANES), SUBLANES), :] = jnp.broadcast_to(outs[5], (SUBLANES, hd))
            return carry

        _loop_unrolled(nc, PREP_FWD_UNROLL, load, compute, store, 0)

    big = jax.ShapeDtypeStruct((t, C_WIDTH), f32)
    bigb = jax.ShapeDtypeStruct((t, C_WIDTH), bf16)
    return pl.pallas_call(
        body, grid=(nb, C_HEADS),
        in_specs=[head(0), head(C_HEADS), head(2 * C_HEADS), ba_sp, sc8, sc8],
        out_specs=[head(0)] * 4 + [attn_sp, gl_sp],
        out_shape=[bigb, bigb, bigb, big, jax.ShapeDtypeStruct((C_HEADS, t, CHUNK), bf16),
                               jax.ShapeDtypeStruct((C_HEADS, nb * nc * SUBLANES, hd), f32)],
        compiler_params=_params(("parallel", "parallel")), name=name,
    )(qkv, qkv, qkv, ba, a_log, dt_b)


def gdn_prep_bwd(qkv, ba, a_log, dt_b, cts, nb, *, name):
    t = qkv.shape[0]
    s = t // nb
    nc = s // CHUNK
    hd = C_HEAD_DIM
    head, attn_sp, gl_sp, ba_sp, sc8 = _gdn_specs(s, nc)

    def body(q_ref, k_ref, v_ref, ba_ref, alog_ref, dtb_ref, cqg, ckd, cw_, cu, cat, cgl,
             dq_ref, dk_ref, dv_ref, dba_ref, dalog_ref, ddtb_ref):
        b = pl.program_id(0)
        h = pl.program_id(1)
        a_log_h = _pick_lane(alog_ref[...], h)
        dt_b_h = _pick_lane(dtb_ref[...], h)
        prep = functools.partial(_gdn_chunks_prep, dots=VJP_DOTS)

        @pl.when(h == 0)
        def _():
            dba_ref[...] = jnp.zeros((s, LANES), f32)

        def load(n):
            rows = pl.ds(pl.multiple_of(n * CHUNK, CHUNK), CHUNK)
            bav = ba_ref[rows, :]
            cgl_n = cgl[pl.ds(pl.multiple_of(n * SUBLANES, SUBLANES), SUBLANES), :][0:1, :]
            primals = (q_ref[rows, :], k_ref[rows, :], v_ref[rows, :], _pick_lane(bav, h), _pick_lane(bav, C_HEADS + h))
            return primals, (cqg[rows, :], ckd[rows, :], cw_[rows, :], cu[rows, :], cat[rows, :], cgl_n), dba_ref[rows, :]

        def compute(loaded):
            primals = [list(x) for x in zip(*[item[0] for item in loaded])]
            _, vjp = jax.vjp(prep, *primals, a_log_h, dt_b_h)
            dqs, dks, dvs, dbls, dals, dalog, ddtb = vjp([item[1] for item in loaded])
            zero = jnp.zeros((1, 1), f32)
            return [((dqs[i], dks[i], dvs[i], dbls[i], dals[i], dalog if i == 0 else zero, ddtb if i == 0 else zero),
                     loaded[i][2]) for i in range(len(loaded))]

        def store(n, res, carry):
            (dq, dk, dv, dbl, dal, dalog_n, ddtb_n), dba_old = res
            rows = pl.ds(pl.multiple_of(n * CHUNK, CHUNK), CHUNK)
            dq_ref[rows, :] = dq
            dk_ref[rows, :] = dk
            dv_ref[rows, :] = dv
            dba_ref[rows, :] = dba_old + _put_lane(dbl, h, LANES) + _put_lane(dal, C_HEADS + h, LANES)
            return carry[0] + dalog_n, carry[1] + ddtb_n

        da_log, ddt_b = _loop_unrolled(nc, PREP_BWD_UNROLL, load, compute, store,
                                       (jnp.zeros((1, 1), f32), jnp.zeros((1, 1), f32)))
        first = jnp.logical_and(b == 0, h == 0)

        @pl.when(first)
        def _():
            dalog_ref[...] = _put_lane(da_log, h, LANES)
            ddtb_ref[...] = _put_lane(ddt_b, h, LANES)

        @pl.when(jnp.logical_not(first))
        def _():
            dalog_ref[...] += _put_lane(da_log, h, LANES)
            ddtb_ref[...] += _put_lane(ddt_b, h, LANES)

    big = jax.ShapeDtypeStruct((t, C_WIDTH), f32)
    vec = pl.BlockSpec((1, LANES), lambda b, h: (0, 0))
    return pl.pallas_call(
        body, grid=(nb, C_HEADS),
        in_specs=[head(0), head(C_HEADS), head(2 * C_HEADS), ba_sp, sc8, sc8] + [head(0)] * 4 + [attn_sp, gl_sp],
        out_specs=[head(0)] * 3 + [ba_sp, vec, vec],
        out_shape=[big] * 3 + [jax.ShapeDtypeStruct((t, LANES), f32), jax.ShapeDtypeStruct((1, LANES), f32),
                               jax.ShapeDtypeStruct((1, LANES), f32)],
        compiler_params=_params(("arbitrary", "arbitrary")), name=name,
    )(qkv, qkv, qkv, ba, a_log, dt_b, *cts)


def _gdn_rec_specs(sb, nsb, hp, reverse):
    hd = C_HEAD_DIM
    ncb = sb // CHUNK
    blk = (lambda b, k: b * nsb + (nsb - 1 - k)) if reverse else (lambda b, k: b * nsb + k)
    wide = pl.BlockSpec((sb, hp * hd), lambda b, j, k: (blk(b, k), j))
    attn = pl.BlockSpec((hp, sb, CHUNK), lambda b, j, k: (j, blk(b, k), 0))
    gl = pl.BlockSpec((hp, ncb * SUBLANES, hd), lambda b, j, k: (j, blk(b, k), 0))
    ng = pl.BlockSpec((1, hd), lambda b, j, k: (0, 0))
    states = pl.BlockSpec((hp, ncb, hd, hd), lambda b, j, k: (j, blk(b, k), 0, 0))
    return wide, attn, gl, ng, states


def gdn_rec_fwd(qg, kdec, w, u, attn, gl, z, ng, nb, *, name):
    t = qg.shape[0]
    s = t // nb
    sb = min(s, GDN_TIME_BLOCK)
    nsb = s // sb
    hd = C_HEAD_DIM
    hp = C_HEADS_PER_STEP
    wide, attn_sp, gl_sp, ng_sp, st_sp = _gdn_rec_specs(sb, nsb, hp, False)

    def body(qg_ref, kd_ref, w_ref, u_ref, at_ref, gl_ref, z_ref, ng_ref, y_ref, st_ref, carry_ref):
        @pl.when(pl.program_id(2) == 0)
        def _():
            carry_ref[...] = jnp.zeros((hp, hd, hd), f32)

        def chunk(n, states):
            for j in range(hp):
                st_ref[j, n] = states[j]
            rows = pl.ds(pl.multiple_of(n * CHUNK, CHUNK), CHUNK)
            grow = pl.ds(pl.multiple_of(n * SUBLANES, SUBLANES), SUBLANES)
            cols = [slice(j * hd, (j + 1) * hd) for j in range(hp)]
            ins = [(qg_ref[rows, c], kd_ref[rows, c], w_ref[rows, c], u_ref[rows, c], at_ref[j, rows, :],
                    gl_ref[j, grow, :][0:1, :], z_ref[rows, c]) for j, c in enumerate(cols)]
            ys, new = _gdn_heads_step(list(states), *[list(x) for x in zip(*ins)], ng_ref[...], RAW_DOTS)
            for j in range(hp):
                y_ref[rows, cols[j]] = ys[j]
            return tuple(new)

        last = lax.fori_loop(0, sb // CHUNK, chunk, tuple(carry_ref[j] for j in range(hp)))
        for j in range(hp):
            carry_ref[j] = last[j]

    return pl.pallas_call(
        body, grid=(nb, C_HEADS // hp, nsb),
        in_specs=[wide] * 4 + [attn_sp, gl_sp, wide, ng_sp], out_specs=[wide, st_sp],
        out_shape=[jax.ShapeDtypeStruct((t, C_WIDTH), f32), jax.ShapeDtypeStruct((C_HEADS, t // CHUNK, hd, hd), f32)],
        scratch_shapes=[pltpu.VMEM((hp, hd, hd), f32)],
        compiler_params=_params(("parallel", "parallel", "arbitrary")), name=name,
    )(qg, kdec, w, u, attn, gl, z, ng)


def gdn_rec_bwd(qg, kdec, w, u, attn, gl, z, ng, states, dy, nb, *, name):
    t = qg.shape[0]
    s = t // nb
    sb = min(s, GDN_TIME_BLOCK)
    nsb = s // sb
    nc = sb // CHUNK
    hd = C_HEAD_DIM
    hp = C_HEADS_PER_STEP
    wide, attn_sp, gl_sp, ng_sp, st_sp = _gdn_rec_specs(sb, nsb, hp, True)

    def body(qg_ref, kd_ref, w_ref, u_ref, at_ref, gl_ref, z_ref, ng_ref, states, dy_ref,
             dqg_ref, dkd_ref, dw_ref, du_ref, dat_ref, dgl_ref, dz_ref, dng_ref, carry_ref):
        step = functools.partial(_gdn_heads_step, dots=VJP_DOTS)

        @pl.when(pl.program_id(2) == 0)
        def _():
            carry_ref[...] = jnp.zeros((hp, hd, hd), f32)

        def operands(n):
            rows = pl.ds(pl.multiple_of(n * CHUNK, CHUNK), CHUNK)
            grow = pl.ds(pl.multiple_of(n * SUBLANES, SUBLANES), SUBLANES)
            cols = [slice(j * hd, (j + 1) * hd) for j in range(hp)]
            return ([qg_ref[rows, c].astype(f32) for c in cols], [kd_ref[rows, c].astype(f32) for c in cols],
                    [w_ref[rows, c].astype(f32) for c in cols], [u_ref[rows, c] for c in cols],
                    [at_ref[j, rows, :].astype(f32) for j in range(hp)],
                    [gl_ref[j, grow, :][0:1, :] for j in range(hp)], [z_ref[rows, c] for c in cols])

        def bwd_chunk(i, carry):
            n = nc - 1 - i
            rows = pl.ds(pl.multiple_of(n * CHUNK, CHUNK), CHUNK)
            grow = pl.ds(pl.multiple_of(n * SUBLANES, SUBLANES), SUBLANES)
            dsts, dng = carry
            dys = [dy_ref[rows, j * hd:(j + 1) * hd] for j in range(hp)]
            _, vjp = jax.vjp(step, [states[j, n] for j in range(hp)], *operands(n), ng_ref[...])
            dst, dqg, dkd, dw, du, dat, dgl, dz, dng_n = vjp((dys, list(dsts)))
            for j in range(hp):
                cols = slice(j * hd, (j + 1) * hd)
                dqg_ref[rows, cols] = dqg[j]
                dkd_ref[rows, cols] = dkd[j]
                dw_ref[rows, cols] = dw[j]
                du_ref[rows, cols] = du[j]
                dat_ref[j, rows, :] = dat[j]
                dgl_ref[j, grow, :] = jnp.broadcast_to(dgl[j], (SUBLANES, hd))
                dz_ref[rows, cols] = dz[j]
            return tuple(dst), dng + dng_n

        dlast, dng = lax.fori_loop(0, nc, bwd_chunk,
                                   (tuple(carry_ref[j] for j in range(hp)), jnp.zeros((1, hd), f32)))
        for j in range(hp):
            carry_ref[j] = dlast[j]
        first = jnp.logical_and(jnp.logical_and(pl.program_id(0) == 0, pl.program_id(1) == 0), pl.program_id(2) == 0)

        @pl.when(first)
        def _():
            dng_ref[...] = dng

        @pl.when(jnp.logical_not(first))
        def _():
            dng_ref[...] += dng

    big = jax.ShapeDtypeStruct((t, C_WIDTH), f32)
    return pl.pallas_call(
        body, grid=(nb, C_HEADS // hp, nsb),
        in_specs=[wide] * 4 + [attn_sp, gl_sp, wide, ng_sp, st_sp, wide],
        out_specs=[wide] * 4 + [attn_sp, gl_sp, wide, ng_sp],
        out_shape=[big] * 4 + [jax.ShapeDtypeStruct(attn.shape, f32), jax.ShapeDtypeStruct(gl.shape, f32), big,
                               jax.ShapeDtypeStruct((1, hd), f32)],
        scratch_shapes=[pltpu.VMEM((hp, hd, hd), f32)],
        compiler_params=_params(("arbitrary", "arbitrary", "arbitrary")), name=name,
    )(qg, kdec, w, u, attn, gl, z, ng, states, dy)


def _blockdiag_slabs(w):
    per = GROUP_W // B_BLOCK
    slabs = jnp.zeros((B_BLOCKS // per, GROUP_W, GROUP_W), w.dtype)
    for h in range(B_BLOCKS):
        o = (h % per) * B_BLOCK
        slabs = slabs.at[h // per, o:o + B_BLOCK, o:o + B_BLOCK].set(w[h])
    return slabs


def _slab_blocks(slabs):
    per = GROUP_W // B_BLOCK
    return jnp.stack([slabs[h // per, (h % per) * B_BLOCK:(h % per + 1) * B_BLOCK,
                            (h % per) * B_BLOCK:(h % per + 1) * B_BLOCK] for h in range(B_BLOCKS)])


def _mixer_ab_fwd(x1, x1b, W, g, b, nb, tag):
    w_in = W["ab_w_in"][0].astype(bf16)
    o1, o2 = A_WIDTH + 2 * A_KV_WIDTH, A_WIDTH + 2 * A_KV_WIDTH + B_WIDTH
    w_qkv, w_bx, w_bg = w_in[:, :o1], w_in[:, o1:o2], w_in[:, o2:]
    pqkv = mm_nn(x1b,w_qkv, name=tag + "_in_qkv")
    pbx = mm_nn(x1b,w_bx, name=tag + "_in_bx")
    pbg = mm_nn(x1b,w_bg, name=tag + "_in_bg")
    ya = attn_fwd(pqkv, W["a_sinks"], nb, name=tag + "_attn_fwd")
    xc = conv_fwd(pbx, W["b_conv_w"][0], W["b_conv_b"], False, nb, name=tag + "_conv_fwd")
    wa_s, wx_s = _blockdiag_slabs(W["b_wa"][0]), _blockdiag_slabs(W["b_wx"][0])
    yb, hh = rglru_fwd(xc, pbg, wa_s, wx_s, W["b_ba"], W["b_bx"], W["b_lam"], nb, name=tag + "_rglru_fwd")
    w_out = W["ab_w_out"][0].astype(bf16)
    x2, z1 = proj_ln([ya, yb], [w_out[:A_WIDTH], w_out[A_WIDTH:]], x1, g, b, name=tag + "_out_ln")
    saved = (pqkv, pbx, pbg, ya, xc, yb, hh, wa_s, wx_s, w_qkv, w_bx, w_bg, w_out)
    return x2, z1, saved


def _mixer_ab_bwd(x1b, dz1, dz1b, W, saved, nb, tag):
    pqkv, pbx, pbg, ya, xc, yb, hh, wa_s, wx_s, w_qkv, w_bx, w_bg, w_out = saved
    dya = mm_nn(dz1b, w_out[:A_WIDTH].T, name=tag + "_dya")
    dyb = mm_nn(dz1b, w_out[A_WIDTH:].T, name=tag + "_dyb")
    dwo = jnp.concatenate([mm_tn(ya, dz1b, name=tag + "_dwo_a"), mm_tn(yb, dz1b, name=tag + "_dwo_b")], 0)
    dpqkv, dsinks = attn_bwd(pqkv, W["a_sinks"], dya, nb, name=tag + "_attn_bwd")
    dxc, dpbg, dwa_s, dwx_s, dba, dbx, dlam = rglru_bwd(xc, pbg, hh, dyb, wa_s, wx_s, W["b_ba"], W["b_bx"],
                                                       W["b_lam"], nb, name=tag + "_rglru_bwd")
    dpbx, dconv_w, dconv_b = conv_bwd(pbx, W["b_conv_w"][0], W["b_conv_b"], dxc, False, nb, name=tag + "_conv_bwd")
    dw_in = jnp.concatenate([mm_tn(x1b,dpqkv, name=tag + "_dwin_qkv"), mm_tn(x1b,dpbx, name=tag + "_dwin_bx"),
                             mm_tn(x1b,dpbg, name=tag + "_dwin_bg")], 1)
    dx1 = mm_nn(dpqkv, w_qkv.T, add=dz1, add_scale=DN_ALPHA, name=tag + "_dx_qkv")
    dx1 = mm_nn(dpbx, w_bx.T, add=dx1, name=tag + "_dx_bx")
    dx1 = mm_nn(dpbg, w_bg.T, add=dx1, name=tag + "_dx_bg")
    grads = {"ab_w_in": dw_in[None], "a_sinks": dsinks, "b_conv_w": dconv_w[None], "b_conv_b": dconv_b,
             "b_wa": _slab_blocks(dwa_s)[None], "b_ba": dba, "b_wx": _slab_blocks(dwx_s)[None], "b_bx": dbx,
             "b_lam": dlam, "ab_w_out": dwo[None]}
    return dx1, grads


def _mixer_c_fwd(x1, x1b, W, g, b, nb, tag):
    w_in = W["c_w_in"][0].astype(bf16)
    d = w_in.shape[0]
    o1, o2 = 3 * C_WIDTH, 4 * C_WIDTH
    w_qkv, w_z = w_in[:, :o1], w_in[:, o1:o2]
    w_ba = jnp.concatenate([w_in[:, o2:], jnp.zeros((d, LANES - 2 * C_HEADS), bf16)], 1)
    pqkv = mm_nn(x1b,w_qkv, name=tag + "_in_qkv")
    pz = mm_nn(x1b,w_z, name=tag + "_in_z")
    pba = mm_nn(x1b,w_ba, name=tag + "_in_ba")
    zero_b = jnp.zeros((1, o1), f32)
    qkvc = conv_fwd(pqkv, W["c_conv_w"][0], zero_b, True, nb, name=tag + "_conv_fwd")
    prep = gdn_prep_fwd(qkvc, pba, W["c_a_log"], W["c_dt_bias"], nb, name=tag + "_prep_fwd")
    yc, states = gdn_rec_fwd(*prep, pz, W["c_norm_g"], nb, name=tag + "_rec_fwd")
    w_out = W["c_w_out"][0].astype(bf16)
    x2, z1 = proj_ln([yc], [w_out], x1, g, b, name=tag + "_out_ln")
    saved = (pqkv, pz, pba, qkvc, prep, states, yc, w_qkv, w_z, w_ba, w_out, zero_b)
    return x2, z1, saved


def _mixer_c_bwd(x1b, dz1, dz1b, W, saved, nb, tag):
    pqkv, pz, pba, qkvc, prep, states, yc, w_qkv, w_z, w_ba, w_out, zero_b = saved
    dyc = mm_nn(dz1b, w_out.T, name=tag + "_dyc")
    dwo = mm_tn(yc, dz1b, name=tag + "_dwo")
    rec = gdn_rec_bwd(*prep, pz, W["c_norm_g"], states, dyc, nb, name=tag + "_rec_bwd")
    cts, dpz, dng = rec[:6], rec[6], rec[7]
    dq, dk, dv, dpba, dalog, ddtb = gdn_prep_bwd(qkvc, pba, W["c_a_log"], W["c_dt_bias"], cts, nb,
                                                 name=tag + "_prep_bwd")
    dqkvc = jnp.concatenate([dq, dk, dv], 1)
    dpqkv, dconv_w, _ = conv_bwd(pqkv, W["c_conv_w"][0], zero_b, dqkvc, True, nb, name=tag + "_conv_bwd")
    dw_in = jnp.concatenate([mm_tn(x1b,dpqkv, name=tag + "_dwin_qkv"), mm_tn(x1b,dpz, name=tag + "_dwin_z"),
                             mm_tn(x1b,dpba, name=tag + "_dwin_ba")[:, :2 * C_HEADS]], 1)
    dx1 = mm_nn(dpqkv, w_qkv.T, add=dz1, add_scale=DN_ALPHA, name=tag + "_dx_qkv")
    dx1 = mm_nn(dpz, w_z.T, add=dx1, name=tag + "_dx_z")
    dx1 = mm_nn(dpba, w_ba.T, add=dx1, name=tag + "_dx_ba")
    grads = {"c_w_in": dw_in[None], "c_conv_w": dconv_w[None], "c_a_log": dalog[:, :C_HEADS],
             "c_dt_bias": ddtb[:, :C_HEADS], "c_norm_g": dng, "c_w_out": dwo[None]}
    return dx1, grads


def _local_step(x, p, target, W, F, on_ffn_grads):
    nb, s, d = x.shape
    t = nb * s
    h = x.reshape(t, d)
    tape = []
    for i in range(DEPTH):
        tag = f"l{i}"
        f1 = [F[k][i] for k in ("ffn1_wg", "ffn1_wu", "ffn1_wd")]
        f2 = [F[k][i] for k in ("ffn2_wg", "ffn2_wu", "ffn2_wd")]
        lg = [W["ln_g"][i, k][None] for k in range(3)]
        lb = [W["ln_b"][i, k][None] for k in range(3)]
        x1, z0, x1b = ffn_fwd(h, *f1, lg[0], lb[0], name=tag + "_ffn1_fwd")
        mixer = _mixer_ab_fwd if i % 2 == 0 else _mixer_c_fwd
        x2, z1, msaved = mixer(x1, x1b, W, lg[1], lb[1], nb, tag + "_mix")
        x3, z2, _ = ffn_fwd(x2, *f2, lg[2], lb[2], name=tag + "_ffn2_fwd")
        pi = p[i].reshape(t, -1)
        pw = (W["ple_wg"][i].astype(bf16), W["ple_bg"][i][None], W["ple_wp"][i].astype(bf16))
        x4 = ple_fwd(x3, pi, *pw, name=tag + "_ple_fwd")
        tape.append((h, z0, x1b, msaved, z1, x2, z2, x3, pi, pw, lg))
        h = x4
    dh, sq = loss_head(h, target.reshape(t, d), name="loss_head")
    loss = 0.5 * jnp.sum(sq) / d
    per_layer = [None] * DEPTH
    grads = {}
    for i in reversed(range(DEPTH)):
        tag = f"l{i}"
        h_in, z0, x1b, msaved, z1, x2, z2, x3, pi, pw, lg = tape[i]
        dx3, dple_wg, dple_bg, dple_wp = ple_bwd(x3, pi, dh, pw[0], pw[0].T, pw[1], pw[2], name=tag + "_ple_bwd")
        dz2, dz2b, dg2, db2 = ln_bwd(z2, dx3, lg[2], name=tag + "_ln2_bwd")
        f1 = [F[k][i] for k in ("ffn1_wg", "ffn1_wu", "ffn1_wd")]
        f2 = [F[k][i] for k in ("ffn2_wg", "ffn2_wu", "ffn2_wd")]
        dgate, dup, *df2 = ffn_bwd_weights(x2.astype(bf16), dz2b, *f2, name=tag + "_ffn2_bwd_w")
        on_ffn_grads(i, 3, df2)
        dx2 = ffn_bwd_input(dgate, dup, f2[0], f2[1], dz2, name=tag + "_ffn2_bwd_x")
        dz1, dz1b, dg1, db1 = ln_bwd(z1, dx2, lg[1], name=tag + "_ln1_bwd")
        mixer_bwd = _mixer_ab_bwd if i % 2 == 0 else _mixer_c_bwd
        dx1, mgrads = mixer_bwd(x1b, dz1, dz1b, W, msaved, nb, tag + "_mix")
        grads.update(mgrads)
        dz0, dz0b, dg0, db0 = ln_bwd(z0, dx1, lg[0], name=tag + "_ln0_bwd")
        dgate, dup, *df1 = ffn_bwd_weights(h_in.astype(bf16), dz0b, *f1, name=tag + "_ffn1_bwd_w")
        on_ffn_grads(i, 0, df1)
        dh = ffn_bwd_input(dgate, dup, f1[0], f1[1], dz0, name=tag + "_ffn1_bwd_x")
        per_layer[i] = {"ln_g": jnp.concatenate([dg0, dg1, dg2], 0), "ln_b": jnp.concatenate([db0, db1, db2], 0),
                        "ple_wg": dple_wg, "ple_bg": dple_bg[0], "ple_wp": dple_wp}
    for k in per_layer[0]:
        grads[k] = jnp.stack([per_layer[i][k] for i in range(DEPTH)])
    return loss, dh.reshape(nb, s, d), grads


WEIGHT_NAMES = ("ffn1_wg", "ffn1_wu", "ffn1_wd", "ffn2_wg", "ffn2_wu", "ffn2_wd", "ln_g", "ln_b", "ple_wg", "ple_bg",
                "ple_wp", "ab_w_in", "a_sinks", "b_conv_w", "b_conv_b", "b_wa", "b_ba", "b_wx", "b_bx", "b_lam",
                "ab_w_out", "c_w_in", "c_conv_w", "c_a_log", "c_dt_bias", "c_norm_g", "c_w_out")
NATIVE_NAMES = WEIGHT_NAMES[:6]
PACKED_NAMES = WEIGHT_NAMES[6:]
SHARD_AXIS = {"ffn1_wg": 2, "ffn1_wu": 2, "ffn1_wd": 1, "ffn2_wg": 2, "ffn2_wu": 2, "ffn2_wd": 1, "ln_g": 2, "ln_b": 2,
              "ple_wg": 1, "ple_wp": 2, "ab_w_in": 2, "b_conv_w": 2, "ab_w_out": 1, "c_w_in": 2, "c_conv_w": 2,
              "c_w_out": 1}
N_CHIPS = 4
PACK_COLS = LANES
PACK_TILE_MULTIPLE = 256
ELEMENTWISE_BLOCK_ELEMS = 128 * 1024


def _row_tile(r, cols):
    return _tile(r, max(2 * SUBLANES, ELEMENTWISE_BLOCK_ELEMS // cols), 2 * SUBLANES)
MESH = pl.DeviceIdType.MESH
ANY = pl.BlockSpec(memory_space=pl.ANY)


def _tiled_dims(shape):
    w = shape[-1]
    r = 1
    for dim in shape[:-1]:
        r *= dim
    return r, w, -(-r // SUBLANES) * SUBLANES, -(-w // LANES) * LANES


def _pack(pieces, lead=()):
    k = len(lead)
    tiles = []
    for a in pieces:
        r, w, rp, wp = _tiled_dims(a.shape[k:])
        a2 = jnp.pad(a.reshape(lead + (r, w)), [(0, 0)] * k + [(0, rp - r), (0, wp - w)])
        a2 = a2.reshape(lead + (rp // SUBLANES, SUBLANES, wp // LANES, LANES))
        a2 = jnp.swapaxes(a2, k + 1, k + 2)
        tiles.append(a2.reshape(lead + (-1, SUBLANES, LANES)))
    flat = jnp.concatenate(tiles, axis=k)
    n = flat.shape[k]
    n_pad = -(-n // PACK_TILE_MULTIPLE) * PACK_TILE_MULTIPLE
    flat = jnp.pad(flat, [(0, 0)] * k + [(0, n_pad - n), (0, 0), (0, 0)])
    return flat.reshape(lead + (n_pad * SUBLANES, PACK_COLS))


def _unpack(pack, shapes, lead=()):
    k = len(lead)
    flat = pack.reshape(lead + (-1, SUBLANES, LANES))
    out, o = [], 0
    for shp in shapes:
        r, w, rp, wp = _tiled_dims(shp)
        n = (rp // SUBLANES) * (wp // LANES)
        a2 = lax.slice_in_dim(flat, o, o + n, axis=k).reshape(lead + (rp // SUBLANES, wp // LANES, SUBLANES, LANES))
        a2 = jnp.swapaxes(a2, k + 1, k + 2).reshape(lead + (rp, wp))
        a2 = lax.slice_in_dim(lax.slice_in_dim(a2, 0, r, axis=k), 0, w, axis=k + 1)
        out.append(a2.reshape(lead + tuple(shp)))
        o += n
    return out


def _mesh_position():
    x, y, c = lax.axis_index("x"), lax.axis_index("y"), lax.axis_index("c")
    chips = [(1 - x, y), (x, 1 - y), (1 - x, 1 - y)]
    return x, y, c, chips


def _remote(src, dst, send_sems, recv_sems, k, to):
    return pltpu.make_async_remote_copy(src_ref=src, dst_ref=dst, send_sem=send_sems.at[k], recv_sem=recv_sems.at[k],
                                        device_id=to, device_id_type=MESH)


def _sems(n):
    return pltpu.SemaphoreType.DMA((n,))


def place_slot(parts, slots, n_slots, dtype, from_slot, *, name):
    n = len(parts)
    r, cols = parts[0].shape[-2:]
    tr = _row_tile(r, cols)

    def body(src_ref, dst_ref, *refs):
        for a in range(n):
            refs[n + a][...] = refs[a][...].astype(dtype)

    dst = pl.BlockSpec((None, tr, cols), lambda i, src_ref, dst_ref: (dst_ref[0], i, 0))
    src = (pl.BlockSpec((None, tr, cols), lambda i, src_ref, dst_ref: (src_ref[0], i, 0)) if from_slot
           else pl.BlockSpec((tr, cols), lambda i, src_ref, dst_ref: (i, 0)))
    return pl.pallas_call(
        body,
        grid_spec=pltpu.PrefetchScalarGridSpec(num_scalar_prefetch=2, grid=(r // tr,), in_specs=[src] * n,
                                               out_specs=[dst] * n),
        out_shape=[jax.ShapeDtypeStruct((n_slots, r, cols), dtype)] * n,
        compiler_params=_params(("parallel",)), name=name,
    )(*slots, *parts)


def cast_slots(parts, dtype, *, name):
    n = len(parts)
    k, r, cols = parts[0].shape
    tr = _row_tile(r, cols)

    def body(*refs):
        for a in range(n):
            refs[n + a][...] = refs[a][...].astype(dtype)

    blk = pl.BlockSpec((None, tr, cols), lambda s, i: (s, i, 0))
    return pl.pallas_call(
        body, grid=(k, r // tr), in_specs=[blk] * n, out_specs=[blk] * n,
        out_shape=[jax.ShapeDtypeStruct((k, r, cols), dtype)] * n,
        compiler_params=_params(("parallel", "parallel")), name=name,
    )(*parts)


def gather_shards(bufs, *, name):
    n = len(bufs)

    def body(*refs):
        out_refs = refs[n:2 * n]
        send_sems, recv_sems = refs[2 * n:]
        x, y, c, chips = _mesh_position()
        me = 2 * x + y
        sibling = (x, y, 1 - c)
        waits = []
        for j, (cx, cy) in enumerate(chips):
            for a in range(n):
                own = out_refs[a].at[me, c]
                cp = _remote(own, own, send_sems, recv_sems, 6 * a + j, (cx, cy, c))
                cp.start()
                waits.append(cp.wait_send)
        for j, (cx, cy) in enumerate(chips):
            for a in range(n):
                got = out_refs[a].at[2 * cx + cy, c]
                _remote(got, got, send_sems, recv_sems, 6 * a + j, (cx, cy, c)).wait_recv()
                fw = _remote(got, got, send_sems, recv_sems, 6 * a + 3 + j, sibling)
                fw.start()
                waits.append(fw.wait_send)
        for j, (cx, cy) in enumerate(chips):
            for a in range(n):
                got = out_refs[a].at[2 * cx + cy, 1 - c]
                _remote(got, got, send_sems, recv_sems, 6 * a + 3 + j, sibling).wait_recv()
        for wait in waits:
            wait()

    return pl.pallas_call(
        body, out_shape=[jax.ShapeDtypeStruct(b.shape, b.dtype) for b in bufs],
        in_specs=[ANY] * n, out_specs=[ANY] * n, scratch_shapes=[_sems(6 * n), _sems(6 * n)],
        input_output_aliases={a: a for a in range(n)}, name=name,
    )(*bufs)


def gather_slots_async(bufs, collective_id, *, name):
    n = len(bufs)
    refs = [jax.new_ref(b, memory_space=pltpu.MemorySpace.HBM) for b in bufs]

    @pl.kernel(mesh=plsc.ScalarSubcoreMesh(axis_name="sequencer", num_cores=1), name=name,
               scratch_types=(_sems(3 * n), _sems(3 * n)),
               compiler_params=pltpu.CompilerParams(collective_id=collective_id))
    def launch(send_sems, recv_sems):
        x, y, c, chips = _mesh_position()
        me = 2 * x + y
        barrier = pltpu.get_barrier_semaphore()
        for cx, cy in chips:
            pl.semaphore_signal(barrier, inc=1, device_id=(cx, cy, c), device_id_type=MESH)
        pl.semaphore_wait(barrier, len(chips))
        sends = []
        for j, (cx, cy) in enumerate(chips):
            for a in range(n):
                own = refs[a].at[me]
                cp = _remote(own, own, send_sems, recv_sems, 3 * a + j, (cx, cy, c))
                cp.start()
                sends.append(cp)
        for j, (cx, cy) in enumerate(chips):
            for a in range(n):
                got = refs[a].at[2 * cx + cy]
                _remote(got, got, send_sems, recv_sems, 3 * a + j, (cx, cy, c)).wait_recv()
        for cp in sends:
            cp.wait_send()

    launch()
    return [r[...] for r in refs]


N_DEVICES = 8
PEER_FLIPS = tuple((dx, dy, dc) for dx in (0, 1) for dy in (0, 1) for dc in (0, 1) if dx or dy or dc)


def exchange_partials_async(sends, recvs, collective_id, *, name):
    n = len(sends)
    s_refs = [jax.new_ref(a, memory_space=pltpu.MemorySpace.HBM) for a in sends]
    r_refs = [jax.new_ref(a, memory_space=pltpu.MemorySpace.HBM) for a in recvs]
    k = len(PEER_FLIPS)

    @pl.kernel(mesh=plsc.ScalarSubcoreMesh(axis_name="sequencer", num_cores=1), name=name,
               scratch_types=(_sems(k), _sems(k)), compiler_params=pltpu.CompilerParams(collective_id=collective_id))
    def launch(send_sems, recv_sems):
        x, y, c, _ = _mesh_position()
        me = 4 * x + 2 * y + c
        peers = [(1 - x if dx else x, 1 - y if dy else y, 1 - c if dc else c) for dx, dy, dc in PEER_FLIPS]
        barrier = pltpu.get_barrier_semaphore()
        for peer in peers:
            pl.semaphore_signal(barrier, inc=1, device_id=peer, device_id_type=MESH)
        pl.semaphore_wait(barrier, len(peers))
        sends_started = []
        for j, (px, py, pc) in enumerate(peers):
            for a in range(n):
                cp = _remote(s_refs[a].at[2 * px + py], r_refs[a].at[me], send_sems, recv_sems, j, (px, py, pc))
                cp.start()
                sends_started.append(cp)
        for j, (px, py, pc) in enumerate(peers):
            for a in range(n):
                got = r_refs[a].at[4 * px + 2 * py + pc]
                _remote(got, got, send_sems, recv_sems, j, (px, py, pc)).wait_recv()
        for cp in sends_started:
            cp.wait_send()

    launch()
    return [r[...] for r in r_refs]


def sibling_exchange(gs, *, name):
    n = len(gs)

    def body(*refs):
        g_refs, out_refs = refs[:n], refs[n:2 * n]
        send_sems, recv_sems = refs[2 * n:]
        x, y, c, _ = _mesh_position()
        cps = [_remote(g_refs[a].at[:, 1 - c], out_refs[a], send_sems, recv_sems, a, (x, y, 1 - c)) for a in range(n)]
        for cp in cps:
            cp.start()
        for cp in cps:
            cp.wait()

    return pl.pallas_call(
        body, out_shape=[jax.ShapeDtypeStruct(g.shape[:1] + g.shape[2:], g.dtype) for g in gs],
        in_specs=[ANY] * n, out_specs=[ANY] * n, scratch_shapes=[_sems(n), _sems(n)], name=name,
    )(*gs)


def add_own_half(gs, others, c_idx, dtype, *, name):
    n = len(gs)
    ns, _, r, cols = gs[0].shape
    tr = _row_tile(r, cols)

    def body(c_ref, *refs):
        for a in range(n):
            refs[2 * n + a][...] = (refs[a][...] + refs[n + a][...]).astype(dtype)

    own = pl.BlockSpec((None, None, tr, cols), lambda s, i, c_ref: (s, c_ref[0], i, 0))
    oth = pl.BlockSpec((None, tr, cols), lambda s, i, c_ref: (s, i, 0))
    return pl.pallas_call(
        body,
        grid_spec=pltpu.PrefetchScalarGridSpec(num_scalar_prefetch=1, grid=(ns, r // tr),
                                               in_specs=[own] * n + [oth] * n, out_specs=[oth] * n),
        out_shape=[jax.ShapeDtypeStruct((ns, r, cols), dtype)] * n,
        compiler_params=_params(("parallel", "parallel")), name=name,
    )(c_idx, *gs, *others)


def chip_exchange(ps, qs, *, name):
    n = len(ps)

    def body(*refs):
        p_refs, q_refs = refs[:n], refs[2 * n:3 * n]
        send_sems, recv_sems = refs[3 * n:]
        x, y, c, chips = _mesh_position()
        me = 2 * x + y
        waits = []
        for j, (cx, cy) in enumerate(chips):
            for a in range(n):
                cp = _remote(p_refs[a].at[2 * cx + cy], q_refs[a].at[me], send_sems, recv_sems, 3 * a + j, (cx, cy, c))
                cp.start()
                waits.append(cp.wait_send)
        for j, (cx, cy) in enumerate(chips):
            for a in range(n):
                got = q_refs[a].at[2 * cx + cy]
                _remote(got, got, send_sems, recv_sems, 3 * a + j, (cx, cy, c)).wait_recv()
        for wait in waits:
            wait()

    return pl.pallas_call(
        body, out_shape=[jax.ShapeDtypeStruct(q_.shape, q_.dtype) for q_ in qs], in_specs=[ANY] * (2 * n),
        out_specs=[ANY] * n, scratch_shapes=[_sems(3 * n), _sems(3 * n)],
        input_output_aliases={n + a: a for a in range(n)}, name=name,
    )(*ps, *qs)


def sum_slots(qs, *, name):
    n = len(qs)
    ns, r, cols = qs[0].shape
    tr = _row_tile(r, cols * ns)

    def body(*refs):
        for a in range(n):
            q_ref = refs[a]
            acc = q_ref[0].astype(f32) + q_ref[1].astype(f32)
            for i in range(2, ns):
                acc = acc + q_ref[i].astype(f32)
            refs[n + a][...] = acc

    return pl.pallas_call(
        body, grid=(r // tr,), in_specs=[pl.BlockSpec((ns, tr, cols), lambda i: (0, i, 0))] * n,
        out_specs=[pl.BlockSpec((tr, cols), lambda i: (i, 0))] * n,
        out_shape=[jax.ShapeDtypeStruct((r, cols), f32)] * n,
        compiler_params=_params(("parallel",)), name=name,
    )(*qs)


def sibling_share(bufs, *, name):
    n = len(bufs)

    def body(*refs):
        out_refs = refs[n:2 * n]
        send_sems, recv_sems = refs[2 * n:]
        x, y, c, _ = _mesh_position()
        sibling = (x, y, 1 - c)
        cps = []
        for a in range(n):
            own = out_refs[a].at[c]
            cp = _remote(own, own, send_sems, recv_sems, a, sibling)
            cp.start()
            cps.append(cp)
        for a in range(n):
            theirs = out_refs[a].at[1 - c]
            _remote(theirs, theirs, send_sems, recv_sems, a, sibling).wait_recv()
        for cp in cps:
            cp.wait_send()

    return pl.pallas_call(
        body, out_shape=[jax.ShapeDtypeStruct(b.shape, b.dtype) for b in bufs], in_specs=[ANY] * n,
        out_specs=[ANY] * n, scratch_shapes=[_sems(n), _sems(n)],
        input_output_aliases={a: a for a in range(n)}, name=name,
    )(*bufs)


def adamw(ws, gs, ms, vs, *, name):
    n = len(ws)
    r, cols = ws[0].shape
    tr = _row_tile(r, cols)

    def body(*refs):
        for a in range(n):
            w_ref, g_ref, m_ref, v_ref = (refs[k * n + a] for k in range(4))
            d_ref, m2_ref, v2_ref = (refs[(4 + k) * n + a] for k in range(3))
            g_ = g_ref[...]
            m2 = ADAM_B1 * m_ref[...] + (1.0 - ADAM_B1) * g_
            v2 = ADAM_B2 * v_ref[...] + (1.0 - ADAM_B2) * (g_ * g_)
            m_hat = m2 / (1.0 - ADAM_B1 ** ADAM_STEP)
            v_hat = v2 / (1.0 - ADAM_B2 ** ADAM_STEP)
            d_ref[...] = -ADAM_LR * (m_hat / (jnp.sqrt(v_hat) + ADAM_EPS) + ADAM_WD * w_ref[...])
            m2_ref[...] = m2
            v2_ref[...] = v2

    row = pl.BlockSpec((tr, cols), lambda i: (i, 0))
    out = pl.pallas_call(
        body, grid=(r // tr,), in_specs=[row] * (4 * n), out_specs=[row] * (3 * n),
        out_shape=[jax.ShapeDtypeStruct((r, cols), f32)] * (3 * n),
        compiler_params=_params(("parallel",)), name=name,
    )(*ws, *gs, *ms, *vs)
    return out[:n], out[n:2 * n], out[2 * n:]


def _full_weights(gathered, local, shapes):
    pieces = _unpack(gathered, shapes, lead=(N_CHIPS,))
    full = {}
    for name, loc, pc in zip(PACKED_NAMES, local, pieces):
        ax = SHARD_AXIS.get(name)
        if ax is None:
            full[name] = loc
        else:
            shp = loc.shape
            full[name] = jnp.moveaxis(pc, 0, ax).reshape(shp[:ax] + (N_CHIPS * shp[ax],) + shp[ax + 1:])
    return full


def _grad_pack(grads, shapes):
    pieces = []
    for name, shp in zip(PACKED_NAMES, shapes):
        g = grads[name]
        ax = SHARD_AXIS.get(name)
        if ax is None:
            pieces.append(jnp.broadcast_to(g.reshape(shp)[None], (N_CHIPS,) + tuple(shp)))
        else:
            pieces.append(jnp.stack(jnp.split(g, N_CHIPS, axis=ax)))
    return _pack(pieces, lead=(N_CHIPS,))


def _by_shape(arrays):
    groups = {}
    for i, a in enumerate(arrays):
        groups.setdefault(a.shape, []).append(i)
    return list(groups.values())


def _grouped(fn, lists, n_out, tag):
    outs = [[None] * len(lists[0]) for _ in range(n_out)]
    for gi, idx in enumerate(_by_shape(lists[0])):
        res = fn(*[[lst[i] for i in idx] for lst in lists], name=f"{tag}_{gi}")
        res = res if n_out > 1 else (res,)
        for k in range(n_out):
            for i, r in zip(idx, res[k]):
                outs[k][i] = r
    return outs if n_out > 1 else outs[0]


def _train_step(x, p, loss_target, weights, m, v):
    packed_w = [weights[k] for k in PACKED_NAMES]
    shapes = [w.shape for w in packed_w]
    halves = lambda a: a.reshape((2, a.shape[0] // 2) + a.shape[1:])
    local = [weights[k] for k in NATIVE_NAMES] + [halves(_pack(packed_w))]
    local_m = [m[k] for k in NATIVE_NAMES] + [halves(_pack([m[k] for k in PACKED_NAMES]))]
    local_v = [v[k] for k in NATIVE_NAMES] + [halves(_pack([v[k] for k in PACKED_NAMES]))]
    flat = lambda lst: [a.reshape((-1, a.shape[-1])) for a in lst]
    c_idx = lax.axis_index("c").astype(jnp.int32).reshape(1)
    chip_idx = (2 * lax.axis_index("x") + lax.axis_index("y")).astype(jnp.int32).reshape(1)
    c2 = (c_idx, c_idx)
    chip2 = (chip_idx, chip_idx)
    chip_dev = (chip_idx, 2 * chip_idx + c_idx)

    def placed(arrays, slot, n_slots, dtype, from_slot, tag):
        return _grouped(lambda a, name: place_slot(a, slot, n_slots, dtype, from_slot, name=name), [arrays], 1, tag)

    ffn_own = [weights[k][i] for i in range(DEPTH) for k in NATIVE_NAMES]
    ffn_bufs = placed(ffn_own, chip2, N_CHIPS, bf16, False, "place_ffn_weights")
    group = len(NATIVE_NAMES) // 2
    ffn_gathered = []
    for gi in range(0, len(ffn_bufs), group):
        ffn_gathered += gather_slots_async(ffn_bufs[gi:gi + group], collective_id=1 + gi // group,
                                           name=f"comm_gather_ffn_{gi // group}")
    ffn_weights = {k: [ffn_gathered[i * len(NATIVE_NAMES) + j] for i in range(DEPTH)] for j, k in enumerate(NATIVE_NAMES)}
    pack_buf = placed(flat(local[-1:]), chip2, N_CHIPS, f32, False, "place_packed_weights")[0]
    gathered_pack = gather_shards([pack_buf.reshape((N_CHIPS,) + local[-1].shape)], name="comm_gather_weights")[0]
    full = _full_weights(gathered_pack, packed_w, shapes)
    in_flight = {}

    def on_ffn_grads(layer, first, partials):
        tag = f"ffn_grads_l{layer}_{first}"
        sends = _grouped(lambda a, name: cast_slots(a, bf16, name=name), [partials], 1, "cast_" + tag)
        recvs = placed(sends, chip_dev, N_DEVICES, bf16, True, "place_" + tag)
        got = exchange_partials_async(sends, recvs, collective_id=5 + len(in_flight), name="comm_" + tag)
        in_flight[(layer, first)] = got

    loss, grad_x, grads = _local_step(x, p, loss_target, full, ffn_weights, on_ffn_grads)
    gs = [_grad_pack(grads, shapes).reshape((N_CHIPS,) + local[-1].shape)]
    others = sibling_exchange(gs, name="comm_grad_sibling")
    chip_sums = add_own_half(gs, others, c_idx, f32, name="grad_add_sibling")
    own = placed(chip_sums, chip2, N_CHIPS, f32, True, "place_own_partial")
    slots = chip_exchange(chip_sums, own, name="comm_grad_chips")
    mine = sum_slots(slots, name="grad_sum_chips")
    pack_sum = sibling_share(placed(mine, c2, 2, f32, False, "place_own_half"), name="comm_grad_share")
    ffn_sums = {}
    for (layer, first), got in in_flight.items():
        sums = _grouped(sum_slots, [got], 1, f"grad_sum_ffn_l{layer}_{first}")
        for j, g in enumerate(sums):
            ffn_sums[(NATIVE_NAMES[first + j], layer)] = g
    gsum = [jnp.stack([ffn_sums[(k, i)] for i in range(DEPTH)]) for k in NATIVE_NAMES] + list(pack_sum)
    delta, m2, v2 = _grouped(adamw, [flat(local), flat(gsum), flat(local_m), flat(local_v)], 3, "adamw")
    loss = lax.psum(loss, ("x", "y", "c"))
    outs = []
    for res in (gsum, delta, m2, v2):
        by_name = {k: a.reshape(weights[k].shape) for k, a in zip(NATIVE_NAMES, res[:-1])}
        by_name.update(zip(PACKED_NAMES, _unpack(res[-1], shapes)))
        outs += [by_name[k] for k in WEIGHT_NAMES]
    return (loss, grad_x, *outs)


def kernel(x, p, ffn1_wg, ffn1_wu, ffn1_wd, ffn2_wg, ffn2_wu, ffn2_wd, ln_g, ln_b, ple_wg, ple_bg, ple_wp, ab_w_in, a_sinks, b_conv_w, b_conv_b, b_wa, b_ba, b_wx, b_bx, b_lam, ab_w_out, c_w_in, c_conv_w, c_a_log, c_dt_bias, c_norm_g, c_w_out, loss_target, m_ffn1_wg, m_ffn1_wu, m_ffn1_wd, m_ffn2_wg, m_ffn2_wu, m_ffn2_wd, m_ln_g, m_ln_b, m_ple_wg, m_ple_bg, m_ple_wp, m_ab_w_in, m_a_sinks, m_b_conv_w, m_b_conv_b, m_b_wa, m_b_ba, m_b_wx, m_b_bx, m_b_lam, m_ab_w_out, m_c_w_in, m_c_conv_w, m_c_a_log, m_c_dt_bias, m_c_norm_g, m_c_w_out, v_ffn1_wg, v_ffn1_wu, v_ffn1_wd, v_ffn2_wg, v_ffn2_wu, v_ffn2_wd, v_ln_g, v_ln_b, v_ple_wg, v_ple_bg, v_ple_wp, v_ab_w_in, v_a_sinks, v_b_conv_w, v_b_conv_b, v_b_wa, v_b_ba, v_b_wx, v_b_bx, v_b_lam, v_ab_w_out, v_c_w_in, v_c_conv_w, v_c_a_log, v_c_dt_bias, v_c_norm_g, v_c_w_out):
    weights = [ffn1_wg, ffn1_wu, ffn1_wd, ffn2_wg, ffn2_wu, ffn2_wd, ln_g, ln_b, ple_wg, ple_bg, ple_wp, ab_w_in, a_sinks,
               b_conv_w, b_conv_b, b_wa, b_ba, b_wx, b_bx, b_lam, ab_w_out, c_w_in, c_conv_w, c_a_log, c_dt_bias, c_norm_g,
               c_w_out]
    m = [m_ffn1_wg, m_ffn1_wu, m_ffn1_wd, m_ffn2_wg, m_ffn2_wu, m_ffn2_wd, m_ln_g, m_ln_b, m_ple_wg, m_ple_bg, m_ple_wp,
         m_ab_w_in, m_a_sinks, m_b_conv_w, m_b_conv_b, m_b_wa, m_b_ba, m_b_wx, m_b_bx, m_b_lam, m_ab_w_out, m_c_w_in,
         m_c_conv_w, m_c_a_log, m_c_dt_bias, m_c_norm_g, m_c_w_out]
    v = [v_ffn1_wg, v_ffn1_wu, v_ffn1_wd, v_ffn2_wg, v_ffn2_wu, v_ffn2_wd, v_ln_g, v_ln_b, v_ple_wg, v_ple_bg, v_ple_wp,
         v_ab_w_in, v_a_sinks, v_b_conv_w, v_b_conv_b, v_b_wa, v_b_ba, v_b_wx, v_b_bx, v_b_lam, v_ab_w_out, v_c_w_in,
         v_c_conv_w, v_c_a_log, v_c_dt_bias, v_c_norm_g, v_c_w_out]
    return _train_step(x, p, loss_target, dict(zip(WEIGHT_NAMES, weights)), dict(zip(WEIGHT_NAMES, m)),
                       dict(zip(WEIGHT_NAMES, v)))
```

```python
import functools

import jax
import jax.numpy as jnp
from jax import lax
from jax.experimental import pallas as pl
from jax.experimental.pallas import tpu as pltpu
from jax.experimental.pallas import tpu_sc as plsc

f32 = jnp.float32
bf16 = jnp.bfloat16

DEPTH = 2
CHUNK = 64
A_HEADS, A_KV_HEADS, A_GROUP, A_HEAD_DIM = 8, 2, 4, 64
A_WIDTH, A_KV_WIDTH, A_WINDOW = 512, 128, 128
B_WIDTH, B_BLOCKS, B_BLOCK, B_CONV = 512, 8, 64, 4
RG_C = 8.0
C_HEADS, C_HEAD_DIM, C_WIDTH, C_CONV = 8, 128, 1024, 4
DN_ALPHA = (2.0 * DEPTH) ** 0.25
LN_EPS = 1e-5
NORM_EPS = 1e-6
NEG = -1e30
ADAM_LR, ADAM_B1, ADAM_B2, ADAM_EPS, ADAM_WD, ADAM_STEP = 0.001, 0.9, 0.999, 1e-08, 0.01, 10

VMEM_LIMIT_BYTES = 56 * 1024 * 1024
LANES = 128
SUBLANES = 8
GROUP_W = 128
PREP_FWD_UNROLL = 8
PREP_BWD_UNROLL = 8
C_HEADS_PER_STEP = 4
GDN_TIME_BLOCK = 512

NN = ((1,), (0,))
NT = ((1,), (1,))
TN = ((0,), (0,))


def _params(sem):
    return pltpu.CompilerParams(dimension_semantics=sem, vmem_limit_bytes=VMEM_LIMIT_BYTES)


def _tile(n, cap, mult):
    best = None
    t = mult
    while t <= min(n, cap):
        if n % t == 0:
            best = t
        t += mult
    return best if best is not None else n


def _bdot(a, b, dims):
    return lax.dot_general(a.astype(bf16), b.astype(bf16), (dims, ((), ())), preferred_element_type=f32)


def _running_sum(x, reverse):
    s = x.shape[0]
    t = lax.broadcasted_iota(jnp.int32, x.shape, 0)
    d = 1
    while d < s:
        if reverse:
            x = x + jnp.where(t < s - d, pltpu.roll(x, s - d, 0), 0.0)
        else:
            x = x + jnp.where(t >= d, pltpu.roll(x, d, 0), 0.0)
        d *= 2
    return x


@jax.custom_vjp
def _cumsum0(x):
    return _running_sum(x, False)


def _cumsum0_fwd(x):
    return _running_sum(x, False), None


def _cumsum0_bwd(_, g):
    return (_running_sum(g, True),)


_cumsum0.defvjp(_cumsum0_fwd, _cumsum0_bwd)


@jax.custom_vjp
def _bnn(a, b):
    return _bdot(a, b, NN)


def _bnn_fwd(a, b):
    return _bdot(a, b, NN), (a, b)


def _bnn_bwd(res, g):
    a, b = res
    return _bdot(g, b, NT), _bdot(a, g, TN)


_bnn.defvjp(_bnn_fwd, _bnn_bwd)


@jax.custom_vjp
def _bnt(a, b):
    return _bdot(a, b, NT)


def _bnt_fwd(a, b):
    return _bdot(a, b, NT), (a, b)


def _bnt_bwd(res, g):
    a, b = res
    return _bdot(g, b, NN), _bdot(g, a, TN)


_bnt.defvjp(_bnt_fwd, _bnt_bwd)


@jax.custom_vjp
def _btn(a, b):
    return _bdot(a, b, TN)


def _btn_fwd(a, b):
    return _bdot(a, b, TN), (a, b)


def _btn_bwd(res, g):
    a, b = res
    return _bdot(b, g, NT), _bdot(a, g, NN)


_btn.defvjp(_btn_fwd, _btn_bwd)

RAW_DOTS = (lambda a, b: _bdot(a, b, NN), lambda a, b: _bdot(a, b, NT), lambda a, b: _bdot(a, b, TN),
            lambda x: _running_sum(x, False))
VJP_DOTS = (_bnn, _bnt, _btn, _cumsum0)


def _layer_norm(z, g, b):
    mu = jnp.mean(z, -1, keepdims=True)
    d = z - mu
    var = jnp.mean(d * d, -1, keepdims=True)
    return d * lax.rsqrt(var + LN_EPS) * g + b


def _silu(x):
    return x * jax.nn.sigmoid(x)


def mm_nn(a, w, add=None, add_scale=1.0, *, name):
    m, k = a.shape
    n = w.shape[1]
    tm = _tile(m, 512, SUBLANES)
    tn = _tile(n, 1024, LANES)

    def body(*refs):
        if add is None:
            a_ref, w_ref, o_ref = refs
            o_ref[...] = _bdot(a_ref[...], w_ref[...], NN)
        else:
            a_ref, w_ref, add_ref, o_ref = refs
            o_ref[...] = _bdot(a_ref[...], w_ref[...], NN) + add_scale * add_ref[...]

    in_specs = [pl.BlockSpec((tm, k), lambda i, j: (i, 0)), pl.BlockSpec((k, tn), lambda i, j: (0, j))]
    args = [a, w]
    if add is not None:
        in_specs.append(pl.BlockSpec((tm, tn), lambda i, j: (i, j)))
        args.append(add)
    return pl.pallas_call(
        body, grid=(m // tm, n // tn), in_specs=in_specs,
        out_specs=pl.BlockSpec((tm, tn), lambda i, j: (i, j)),
        out_shape=jax.ShapeDtypeStruct((m, n), f32),
        compiler_params=_params(("parallel", "parallel")), name=name,
    )(*args)


def mm_tn(a, b, *, name):
    m, k = a.shape
    n = b.shape[1]
    tm = _tile(m, 1024, 2 * SUBLANES)
    tn = _tile(n, 1024, LANES)

    def body(a_ref, b_ref, o_ref):
        part = _bdot(a_ref[...], b_ref[...], TN)

        @pl.when(pl.program_id(1) == 0)
        def _():
            o_ref[...] = part

        @pl.when(pl.program_id(1) > 0)
        def _():
            o_ref[...] += part

    return pl.pallas_call(
        body, grid=(n // tn, m // tm),
        in_specs=[pl.BlockSpec((tm, k), lambda j, i: (i, 0)), pl.BlockSpec((tm, tn), lambda j, i: (i, j))],
        out_specs=pl.BlockSpec((k, tn), lambda j, i: (0, j)),
        out_shape=jax.ShapeDtypeStruct((k, n), f32),
        compiler_params=_params(("parallel", "arbitrary")), name=name,
    )(a, b)


def proj_ln(a_list, w_list, xres, g, b, *, name):
    t, d = xres.shape
    tm = _tile(t, 256, SUBLANES)
    na = len(a_list)

    def body(*refs):
        a_refs, w_refs = refs[:na], refs[na:2 * na]
        x_ref, g_ref, b_ref, y_ref, z_ref = refs[2 * na:]
        z = DN_ALPHA * x_ref[...]
        for a_ref, w_ref in zip(a_refs, w_refs):
            z = z + _bdot(a_ref[...], w_ref[...], NN)
        z_ref[...] = z
        y_ref[...] = _layer_norm(z, g_ref[...], b_ref[...])

    in_specs = [pl.BlockSpec((tm, a.shape[1]), lambda i: (i, 0)) for a in a_list]
    in_specs += [pl.BlockSpec(w.shape, lambda i: (0, 0)) for w in w_list]
    in_specs += [pl.BlockSpec((tm, d), lambda i: (i, 0)), pl.BlockSpec((1, d), lambda i: (0, 0)),
                 pl.BlockSpec((1, d), lambda i: (0, 0))]
    return pl.pallas_call(
        body, grid=(t // tm,), in_specs=in_specs,
        out_specs=[pl.BlockSpec((tm, d), lambda i: (i, 0))] * 2,
        out_shape=[jax.ShapeDtypeStruct((t, d), f32)] * 2,
        compiler_params=_params(("parallel",)), name=name,
    )(*a_list, *w_list, xres, g, b)


def ln_bwd(z, dy, g, *, name):
    t, d = z.shape
    tm = _tile(t, 512, SUBLANES)

    def body(z_ref, dy_ref, g_ref, dz_ref, dzb_ref, dg_ref, db_ref):
        zz = z_ref[...]
        dy_ = dy_ref[...]
        mu = jnp.mean(zz, -1, keepdims=True)
        dd = zz - mu
        var = jnp.mean(dd * dd, -1, keepdims=True)
        rstd = lax.rsqrt(var + LN_EPS)
        xhat = dd * rstd
        dxh = dy_ * g_ref[...]
        dz = rstd * (dxh - jnp.mean(dxh, -1, keepdims=True) - xhat * jnp.mean(dxh * xhat, -1, keepdims=True))
        dz_ref[...] = dz
        dzb_ref[...] = dz.astype(bf16)
        pg = jnp.sum(dy_ * xhat, 0, keepdims=True)
        pb = jnp.sum(dy_, 0, keepdims=True)

        @pl.when(pl.program_id(0) == 0)
        def _():
            dg_ref[...] = pg
            db_ref[...] = pb

        @pl.when(pl.program_id(0) > 0)
        def _():
            dg_ref[...] += pg
            db_ref[...] += pb

    row = pl.BlockSpec((tm, d), lambda i: (i, 0))
    vec = pl.BlockSpec((1, d), lambda i: (0, 0))
    return pl.pallas_call(
        body, grid=(t // tm,), in_specs=[row, row, vec], out_specs=[row, row, vec, vec],
        out_shape=[jax.ShapeDtypeStruct((t, d), f32), jax.ShapeDtypeStruct((t, d), bf16),
                   jax.ShapeDtypeStruct((1, d), f32), jax.ShapeDtypeStruct((1, d), f32)],
        compiler_params=_params(("arbitrary",)), name=name,
    )(z, dy, g)


def loss_head(y, target, *, name):
    t, d = y.shape
    tm = _tile(t, 512, SUBLANES)

    def body(y_ref, t_ref, dy_ref, sq_ref):
        e = y_ref[...] - t_ref[...]
        dy_ref[...] = e * (1.0 / d)
        part = jnp.sum(e * e, 0, keepdims=True)

        @pl.when(pl.program_id(0) == 0)
        def _():
            sq_ref[...] = part

        @pl.when(pl.program_id(0) > 0)
        def _():
            sq_ref[...] += part

    row = pl.BlockSpec((tm, d), lambda i: (i, 0))
    vec = pl.BlockSpec((1, d), lambda i: (0, 0))
    return pl.pallas_call(
        body, grid=(t // tm,), in_specs=[row, row], out_specs=[row, vec],
        out_shape=[jax.ShapeDtypeStruct((t, d), f32), jax.ShapeDtypeStruct((1, d), f32)],
        compiler_params=_params(("arbitrary",)), name=name,
    )(y, target)


FFN_COL_BLOCK = 256
FFN_ROWS = 1024


def _lane_blocks(n):
    return [slice(s, min(s + FFN_COL_BLOCK, n)) for s in range(0, n, FFN_COL_BLOCK)]


def ffn_fwd(x, wg, wu, wd, g, b, *, name):
    t, d = x.shape
    nf, _, tf = wg.shape
    tm = _tile(t, FFN_ROWS, SUBLANES)

    def body(x_ref, wg_ref, wu_ref, wd_ref, g_ref, b_ref, y_ref, z_ref, yb_ref, acc_ref):
        f = pl.program_id(1)
        xb = x_ref[...].astype(bf16)
        part, pending = None, None
        for cols in _lane_blocks(tf):
            gate_up = (_bdot(xb, wg_ref[:, cols], NN), _bdot(xb, wu_ref[:, cols], NN), cols)
            if pending is not None:
                down = _bdot(_silu(pending[0]) * pending[1], wd_ref[pending[2], :], NN)
                part = down if part is None else part + down
            pending = gate_up
        down = _bdot(_silu(pending[0]) * pending[1], wd_ref[pending[2], :], NN)
        part = down if part is None else part + down

        @pl.when(f == 0)
        def _():
            acc_ref[...] = part

        @pl.when(f > 0)
        def _():
            acc_ref[...] += part

        @pl.when(f == nf - 1)
        def _():
            z = DN_ALPHA * x_ref[...] + 0.5 * acc_ref[...]
            z_ref[...] = z
            y = _layer_norm(z, g_ref[...], b_ref[...])
            y_ref[...] = y
            yb_ref[...] = y.astype(bf16)

    row = pl.BlockSpec((tm, d), lambda i, j: (i, 0))
    vec = pl.BlockSpec((1, d), lambda i, j: (0, 0))
    wcol = pl.BlockSpec((None, d, tf), lambda i, j: (j, 0, 0))
    wrow = pl.BlockSpec((None, tf, d), lambda i, j: (j, 0, 0))
    return pl.pallas_call(
        body, grid=(t // tm, nf),
        in_specs=[row, wcol, wcol, wrow, vec, vec],
        out_specs=[row, row, row],
        out_shape=[jax.ShapeDtypeStruct((t, d), f32)] * 2 + [jax.ShapeDtypeStruct((t, d), bf16)],
        scratch_shapes=[pltpu.VMEM((tm, d), f32)],
        compiler_params=_params(("parallel", "arbitrary")), name=name,
    )(x, wg, wu, wd, g, b)


def ffn_bwd_weights(xb, dzb, wg, wu, wd, *, name):
    t, d = xb.shape
    nf, _, tf = wg.shape
    tm = _tile(t, FFN_ROWS, SUBLANES)
    nt = t // tm

    def body(x_ref, dz_ref, wg_ref, wu_ref, wd_ref, dgate_ref, dup_ref, owg_ref, owu_ref, owd_ref,
             dwg_ref, dwu_ref, dwd_ref):
        x = x_ref[...]
        dzh = dz_ref[...] * 0.5

        def first_half(cols):
            return _bdot(x, wg_ref[:, cols], NN), _bdot(x, wu_ref[:, cols], NN), _bdot(dzh, wd_ref[cols, :], NT), cols

        def second_half(gate, up, dh, cols):
            sg = jax.nn.sigmoid(gate)
            s = gate * sg
            dup = (dh * s).astype(bf16)
            dgate = (dh * up * (sg * (1.0 + gate * (1.0 - sg)))).astype(bf16)
            dgate_ref[:, cols] = dgate
            dup_ref[:, cols] = dup
            return _bdot(x, dgate, TN), _bdot(x, dup, TN), _bdot(s * up, dzh, TN), cols

        parts, pending = [], None
        for cols in _lane_blocks(tf):
            nxt = first_half(cols)
            if pending is not None:
                parts.append(second_half(*pending))
            pending = nxt
        parts.append(second_half(*pending))

        @pl.when(pl.program_id(1) == 0)
        def _():
            for pwg, pwu, pwd, cols in parts:
                dwg_ref[:, cols] = pwg
                dwu_ref[:, cols] = pwu
                dwd_ref[cols, :] = pwd

        @pl.when(pl.program_id(1) > 0)
        def _():
            for pwg, pwu, pwd, cols in parts:
                dwg_ref[:, cols] += pwg
                dwu_ref[:, cols] += pwu
                dwd_ref[cols, :] += pwd

        @pl.when(pl.program_id(1) == nt - 1)
        def _():
            owg_ref[...] = dwg_ref[...].astype(bf16)
            owu_ref[...] = dwu_ref[...].astype(bf16)
            owd_ref[...] = dwd_ref[...].astype(bf16)

    row = pl.BlockSpec((tm, d), lambda j, i: (i, 0))
    wcol = pl.BlockSpec((None, d, tf), lambda j, i: (j, 0, 0))
    wrow = pl.BlockSpec((None, tf, d), lambda j, i: (j, 0, 0))
    act = pl.BlockSpec((None, tm, tf), lambda j, i: (j, i, 0))
    return pl.pallas_call(
        body, grid=(nf, nt), in_specs=[row, row, wcol, wcol, wrow], out_specs=[act, act, wcol, wcol, wrow],
        out_shape=[jax.ShapeDtypeStruct((nf, t, tf), bf16), jax.ShapeDtypeStruct((nf, t, tf), bf16),
                   jax.ShapeDtypeStruct((nf, d, tf), bf16), jax.ShapeDtypeStruct((nf, d, tf), bf16),
                   jax.ShapeDtypeStruct((nf, tf, d), bf16)],
        scratch_shapes=[pltpu.VMEM((d, tf), f32), pltpu.VMEM((d, tf), f32), pltpu.VMEM((tf, d), f32)],
        compiler_params=_params(("parallel", "arbitrary")), name=name,
    )(xb, dzb, wg, wu, wd)


def ffn_bwd_input(dgate, dup, wg, wu, dz, *, name):
    nf, t, tf = dgate.shape
    d = wg.shape[1]
    tm = _tile(t, FFN_ROWS // 2, SUBLANES)

    def body(dg_ref, du_ref, wg_ref, wu_ref, dz_ref, dx_ref):
        acc = DN_ALPHA * dz_ref[...]
        for j in range(nf):
            acc = acc + _bdot(dg_ref[j], wg_ref[j], NT) + _bdot(du_ref[j], wu_ref[j], NT)
        dx_ref[...] = acc

    act = pl.BlockSpec((nf, tm, tf), lambda i: (0, i, 0))
    wsp = pl.BlockSpec((nf, d, tf), lambda i: (0, 0, 0))
    row = pl.BlockSpec((tm, d), lambda i: (i, 0))
    return pl.pallas_call(
        body, grid=(t // tm,), in_specs=[act, act, wsp, wsp, row], out_specs=row,
        out_shape=jax.ShapeDtypeStruct((t, d), f32),
        compiler_params=_params(("parallel",)), name=name,
    )(dgate, dup, wg, wu, dz)


def ple_fwd(x, p, wg, bg, wp, *, name):
    t, d = x.shape
    dp = p.shape[1]
    tm = _tile(t, 512, SUBLANES)

    def body(x_ref, p_ref, wg_ref, bg_ref, wp_ref, o_ref):
        x_ = x_ref[...]
        gate = jax.nn.sigmoid(_bdot(x_, wg_ref[...], NN) + bg_ref[...])
        o_ref[...] = x_ + gate * _bdot(p_ref[...], wp_ref[...], NN)

    row = pl.BlockSpec((tm, d), lambda i: (i, 0))
    return pl.pallas_call(
        body, grid=(t // tm,),
        in_specs=[row, pl.BlockSpec((tm, dp), lambda i: (i, 0)), pl.BlockSpec((d, d), lambda i: (0, 0)),
                  pl.BlockSpec((1, d), lambda i: (0, 0)), pl.BlockSpec((dp, d), lambda i: (0, 0))],
        out_specs=row, out_shape=jax.ShapeDtypeStruct((t, d), f32),
        compiler_params=_params(("parallel",)), name=name,
    )(x, p, wg, bg, wp)


def ple_bwd(x, p, dy, wg, wgt, bg, wp, *, name):
    t, d = x.shape
    dp = p.shape[1]
    tm = _tile(t, 512, SUBLANES)

    def body(x_ref, p_ref, dy_ref, wg_ref, wgt_ref, bg_ref, wp_ref, dx_ref, dwg_ref, dbg_ref, dwp_ref):
        x_ = x_ref[...]
        dy_ = dy_ref[...]
        s = jax.nn.sigmoid(_bdot(x_, wg_ref[...], NN) + bg_ref[...])
        e = _bdot(p_ref[...], wp_ref[...], NN)
        da = dy_ * e * s * (1.0 - s)
        de = dy_ * s
        dx_ref[...] = dy_ + _bdot(da, wgt_ref[...], NN)
        pwg = _bdot(x_, da, TN)
        pbg = jnp.sum(da, 0, keepdims=True)
        pwp = _bdot(p_ref[...], de, TN)

        @pl.when(pl.program_id(0) == 0)
        def _():
            dwg_ref[...] = pwg
            dbg_ref[...] = pbg
            dwp_ref[...] = pwp

        @pl.when(pl.program_id(0) > 0)
        def _():
            dwg_ref[...] += pwg
            dbg_ref[...] += pbg
            dwp_ref[...] += pwp

    row = pl.BlockSpec((tm, d), lambda i: (i, 0))
    full = lambda shape: pl.BlockSpec(shape, lambda i: (0, 0))
    return pl.pallas_call(
        body, grid=(t // tm,),
        in_specs=[row, pl.BlockSpec((tm, dp), lambda i: (i, 0)), row, full((d, d)), full((d, d)), full((1, d)),
                  full((dp, d))],
        out_specs=[row, full((d, d)), full((1, d)), full((dp, d))],
        out_shape=[jax.ShapeDtypeStruct((t, d), f32), jax.ShapeDtypeStruct((d, d), f32),
                   jax.ShapeDtypeStruct((1, d), f32), jax.ShapeDtypeStruct((dp, d), f32)],
        compiler_params=_params(("arbitrary",)), name=name,
    )(x, p, dy, wg, wgt, bg, wp)


def _conv_taps(xpad_ref, w_ref, s):
    acc = w_ref[0:1, :] * xpad_ref[SUBLANES - 3:SUBLANES - 3 + s, :]
    for j in range(1, 4):
        acc = acc + w_ref[j:j + 1, :] * xpad_ref[SUBLANES - 3 + j:SUBLANES - 3 + j + s, :]
    return acc


def conv_fwd(x, w, bias, act, nb, *, name):
    t, c = x.shape
    s = t // nb
    cw = GROUP_W

    def body(x_ref, w_ref, b_ref, y_ref, xpad):
        xpad[0:SUBLANES, :] = jnp.zeros((SUBLANES, cw), f32)
        xpad[SUBLANES:, :] = x_ref[...]
        acc = _conv_taps(xpad, w_ref, s) + b_ref[...]
        y_ref[...] = _silu(acc) if act else acc

    slab = pl.BlockSpec((s, cw), lambda b, g: (b, g))
    return pl.pallas_call(
        body, grid=(nb, c // cw),
        in_specs=[slab, pl.BlockSpec((4, cw), lambda b, g: (0, g)), pl.BlockSpec((1, cw), lambda b, g: (0, g))],
        out_specs=slab, out_shape=jax.ShapeDtypeStruct((t, c), f32),
        scratch_shapes=[pltpu.VMEM((s + SUBLANES, cw), f32)],
        compiler_params=_params(("parallel", "parallel")), name=name,
    )(x, w, bias)


def conv_bwd(x, w, bias, dy, act, nb, *, name):
    t, c = x.shape
    s = t // nb
    cw = GROUP_W

    def body(x_ref, w_ref, b_ref, dy_ref, dx_ref, dw_ref, db_ref, xpad, dpad):
        xpad[0:SUBLANES, :] = jnp.zeros((SUBLANES, cw), f32)
        xpad[SUBLANES:, :] = x_ref[...]
        dacc = dy_ref[...]
        if act:
            acc = _conv_taps(xpad, w_ref, s) + b_ref[...]
            sg = jax.nn.sigmoid(acc)
            dacc = dacc * (sg * (1.0 + acc * (1.0 - sg)))
        dpad[0:s, :] = dacc
        dpad[s:, :] = jnp.zeros((SUBLANES, cw), f32)
        dx = w_ref[0:1, :] * dpad[3:3 + s, :]
        for j in range(1, 4):
            dx = dx + w_ref[j:j + 1, :] * dpad[3 - j:3 - j + s, :]
        dx_ref[...] = dx
        first = pl.program_id(1) == 0
        for j in range(4):
            pw = jnp.sum(dacc * xpad[SUBLANES - 3 + j:SUBLANES - 3 + j + s, :], 0, keepdims=True)

            @pl.when(first)
            def _():
                dw_ref[j:j + 1, :] = pw

            @pl.when(jnp.logical_not(first))
            def _():
                dw_ref[j:j + 1, :] += pw

        pb = jnp.sum(dacc, 0, keepdims=True)

        @pl.when(first)
        def _():
            db_ref[...] = pb

        @pl.when(jnp.logical_not(first))
        def _():
            db_ref[...] += pb

    slab = pl.BlockSpec((s, cw), lambda g, b: (b, g))
    wsp = pl.BlockSpec((4, cw), lambda g, b: (0, g))
    bsp = pl.BlockSpec((1, cw), lambda g, b: (0, g))
    return pl.pallas_call(
        body, grid=(c // cw, nb), in_specs=[slab, wsp, bsp, slab], out_specs=[slab, wsp, bsp],
        out_shape=[jax.ShapeDtypeStruct((t, c), f32), jax.ShapeDtypeStruct((4, c), f32),
                   jax.ShapeDtypeStruct((1, c), f32)],
        scratch_shapes=[pltpu.VMEM((s + SUBLANES, cw), f32), pltpu.VMEM((s + SUBLANES, cw), f32)],
        compiler_params=_params(("parallel", "arbitrary")), name=name,
    )(x, w, bias, dy)


def _each(f, *lists):
    return [f(*a) for a in zip(*lists)]


def _attn_heads(qs, kbs, vbs, sinks, valid, dist, dots):
    nn, nt = dots[:2]
    kv = [h // A_GROUP for h in range(A_HEADS)]
    scs = [nt(qs[h], kbs[kv[h]]) for h in range(A_HEADS)]
    prs = []
    for h in range(A_HEADS):
        sc = scs[h] * (A_HEAD_DIM ** -0.5) - 2.0 ** -(h + 1) * dist
        sc = jnp.where(valid, sc, NEG)
        m = lax.stop_gradient(jnp.maximum(jnp.max(sc, -1, keepdims=True), sinks[h]))
        pr = jnp.exp(sc - m)
        den = jnp.sum(pr, -1, keepdims=True) + jnp.exp(sinks[h] - m)
        prs.append(pr / den)
    return [nn(prs[h], vbs[kv[h]]) for h in range(A_HEADS)]


A_Q_ROWS = 2 * CHUNK


def _attn_band_consts(r0):
    band = A_WINDOW + A_Q_ROWS
    qi = lax.broadcasted_iota(jnp.int32, (A_Q_ROWS, band), 0)
    kj = lax.broadcasted_iota(jnp.int32, (A_Q_ROWS, band), 1)
    dist = jnp.abs(qi + A_WINDOW - kj).astype(f32)
    qc, kc = qi // CHUNK, kj // CHUNK
    valid = ((kj + r0) >= A_WINDOW) & (kc >= qc) & (kc <= qc + A_WINDOW // CHUNK)
    return dist, valid


def attn_fwd(qkv, sinks, nb, *, name):
    t = qkv.shape[0]
    s = t // nb
    band = A_WINDOW + A_Q_ROWS
    hd = A_HEAD_DIM

    def body(qkv_ref, sink_ref, o_ref, kvpad):
        kvpad[0:A_WINDOW, :] = jnp.zeros((A_WINDOW, 2 * A_KV_WIDTH), f32)
        kvpad[A_WINDOW:, :] = qkv_ref[:, A_WIDTH:]

        def chunk(n, carry):
            r0 = pl.multiple_of(n * A_Q_ROWS, A_Q_ROWS)
            dist, valid = _attn_band_consts(r0)
            kbs = [kvpad[pl.ds(r0, band), kvh * hd:(kvh + 1) * hd] for kvh in range(A_KV_HEADS)]
            vbs = [kvpad[pl.ds(r0, band), A_KV_WIDTH + kvh * hd:A_KV_WIDTH + (kvh + 1) * hd]
                   for kvh in range(A_KV_HEADS)]
            qs = [qkv_ref[pl.ds(r0, A_Q_ROWS), h * hd:(h + 1) * hd] for h in range(A_HEADS)]
            outs = _attn_heads(qs, kbs, vbs, [sink_ref[:, h:h + 1] for h in range(A_HEADS)], valid, dist, RAW_DOTS)
            for h in range(A_HEADS):
                o_ref[pl.ds(r0, A_Q_ROWS), h * hd:(h + 1) * hd] = outs[h]
            return carry

        lax.fori_loop(0, s // A_Q_ROWS, chunk, 0)

    return pl.pallas_call(
        body, grid=(nb,),
        in_specs=[pl.BlockSpec((s, A_WIDTH + 2 * A_KV_WIDTH), lambda b: (b, 0)),
                  pl.BlockSpec((1, A_HEADS), lambda b: (0, 0))],
        out_specs=pl.BlockSpec((s, A_WIDTH), lambda b: (b, 0)),
        out_shape=jax.ShapeDtypeStruct((t, A_WIDTH), f32),
        scratch_shapes=[pltpu.VMEM((s + A_WINDOW, 2 * A_KV_WIDTH), f32)],
        compiler_params=_params(("parallel",)), name=name,
    )(qkv, sinks)


def attn_bwd(qkv, sinks, do, nb, *, name):
    t = qkv.shape[0]
    s = t // nb
    band = A_WINDOW + A_Q_ROWS
    hd = A_HEAD_DIM
    kvw = 2 * A_KV_WIDTH

    def body(qkv_ref, sink_ref, do_ref, dqkv_ref, dsink_ref, kvpad, dkvpad):
        kvpad[0:A_WINDOW, :] = jnp.zeros((A_WINDOW, kvw), f32)
        kvpad[A_WINDOW:, :] = qkv_ref[:, A_WIDTH:]
        dkvpad[...] = jnp.zeros((s + A_WINDOW, kvw), f32)

        def chunk(n, dsinks):
            r0 = pl.multiple_of(n * A_Q_ROWS, A_Q_ROWS)
            dist, valid = _attn_band_consts(r0)
            ksl = [slice(kvh * hd, (kvh + 1) * hd) for kvh in range(A_KV_HEADS)]
            vsl = [slice(A_KV_WIDTH + kvh * hd, A_KV_WIDTH + (kvh + 1) * hd) for kvh in range(A_KV_HEADS)]
            kbs = [kvpad[pl.ds(r0, band), sl] for sl in ksl]
            vbs = [kvpad[pl.ds(r0, band), sl] for sl in vsl]
            dkbs = [dkvpad[pl.ds(r0, band), sl] for sl in ksl]
            dvbs = [dkvpad[pl.ds(r0, band), sl] for sl in vsl]
            qs = [qkv_ref[pl.ds(r0, A_Q_ROWS), h * hd:(h + 1) * hd] for h in range(A_HEADS)]
            dos = [do_ref[pl.ds(r0, A_Q_ROWS), h * hd:(h + 1) * hd] for h in range(A_HEADS)]
            fn = functools.partial(_attn_heads, valid=valid, dist=dist, dots=VJP_DOTS)
            _, vjp = jax.vjp(fn, qs, kbs, vbs, [sink_ref[:, h:h + 1] for h in range(A_HEADS)])
            dqs, dks, dvs, dss = vjp(dos)
            for h in range(A_HEADS):
                dqkv_ref[pl.ds(r0, A_Q_ROWS), h * hd:(h + 1) * hd] = dqs[h]
            for kvh in range(A_KV_HEADS):
                dkvpad[pl.ds(r0, band), ksl[kvh]] = dkbs[kvh] + dks[kvh]
                dkvpad[pl.ds(r0, band), vsl[kvh]] = dvbs[kvh] + dvs[kvh]
            return tuple(dsinks[h] + dss[h] for h in range(A_HEADS))

        dsinks = lax.fori_loop(0, s // A_Q_ROWS, chunk, tuple(jnp.zeros((1, 1), f32) for _ in range(A_HEADS)))
        dqkv_ref[:, A_WIDTH:] = dkvpad[A_WINDOW:, :]
        first = pl.program_id(0) == 0
        for h in range(A_HEADS):
            @pl.when(first)
            def _():
                dsink_ref[:, h:h + 1] = dsinks[h]

            @pl.when(jnp.logical_not(first))
            def _():
                dsink_ref[:, h:h + 1] += dsinks[h]

    wq = A_WIDTH + kvw
    return pl.pallas_call(
        body, grid=(nb,),
        in_specs=[pl.BlockSpec((s, wq), lambda b: (b, 0)), pl.BlockSpec((1, A_HEADS), lambda b: (0, 0)),
                  pl.BlockSpec((s, A_WIDTH), lambda b: (b, 0))],
        out_specs=[pl.BlockSpec((s, wq), lambda b: (b, 0)), pl.BlockSpec((1, A_HEADS), lambda b: (0, 0))],
        out_shape=[jax.ShapeDtypeStruct((t, wq), f32), jax.ShapeDtypeStruct((1, A_HEADS), f32)],
        scratch_shapes=[pltpu.VMEM((s + A_WINDOW, kvw), f32), pltpu.VMEM((s + A_WINDOW, kvw), f32)],
        compiler_params=_params(("arbitrary",)), name=name,
    )(qkv, sinks, do)


def _rg_gates(xc, wa, wx, ba, bx, lam, nn):
    r = jax.nn.sigmoid(nn(xc, wa) + ba)
    i = jax.nn.sigmoid(nn(xc, wx) + bx)
    log_a = -RG_C * r * jax.nn.softplus(-lam)
    a = jnp.exp(log_a)
    mult = jnp.sqrt(-jnp.tanh(log_a) * (jnp.exp(2.0 * log_a) + 1.0))
    return a, mult * (i * xc)


def _linear_scan(a, u, reverse):
    s = a.shape[0]
    t = lax.broadcasted_iota(jnp.int32, a.shape, 0)
    d = 1
    while d < s:
        if reverse:
            keep = t < s - d
            shift = s - d
        else:
            keep = t >= d
            shift = d
        us = jnp.where(keep, pltpu.roll(u, shift, 0), 0.0)
        as_ = jnp.where(keep, pltpu.roll(a, shift, 0), 1.0)
        u = u + a * us
        a = a * as_
        d *= 2
    return u


def rglru_fwd(xc, bg, wa, wx, ba, bx, lam, nb, *, name):
    t, c = xc.shape
    s = t // nb
    cw = GROUP_W

    def body(xc_ref, bg_ref, wa_ref, wx_ref, ba_ref, bx_ref, lam_ref, y_ref, h_ref):
        a, u = _rg_gates(xc_ref[...], wa_ref[...], wx_ref[...], ba_ref[...], bx_ref[...], lam_ref[...], RAW_DOTS[0])
        h = _linear_scan(a, u, False)
        h_ref[...] = h
        y_ref[...] = h * jax.nn.gelu(bg_ref[...])

    slab = pl.BlockSpec((s, cw), lambda b, g: (b, g))
    wsp = pl.BlockSpec((None, cw, cw), lambda b, g: (g, 0, 0))
    vec = pl.BlockSpec((1, cw), lambda b, g: (0, g))
    return pl.pallas_call(
        body, grid=(nb, c // cw), in_specs=[slab, slab, wsp, wsp, vec, vec, vec], out_specs=[slab, slab],
        out_shape=[jax.ShapeDtypeStruct((t, c), f32)] * 2,
        compiler_params=_params(("parallel", "parallel")), name=name,
    )(xc, bg, wa, wx, ba, bx, lam)


def rglru_bwd(xc, bg, h, dy, wa, wx, ba, bx, lam, nb, *, name):
    t, c = xc.shape
    s = t // nb
    cw = GROUP_W

    def body(xc_ref, bg_ref, h_ref, dy_ref, wa_ref, wx_ref, ba_ref, bx_ref, lam_ref,
             dxc_ref, dbg_ref, dwa_ref, dwx_ref, dba_ref, dbx_ref, dlam_ref):
        h = h_ref[...]
        dy_ = dy_ref[...]
        gel, gel_vjp = jax.vjp(jax.nn.gelu, bg_ref[...])
        dbg_ref[...] = gel_vjp(dy_ * h)[0]
        dh = dy_ * gel
        gates = functools.partial(_rg_gates, nn=_bnn)
        (a, _), gates_vjp = jax.vjp(gates, xc_ref[...], wa_ref[...], wx_ref[...], ba_ref[...], bx_ref[...],
                                    lam_ref[...])
        ti = lax.broadcasted_iota(jnp.int32, a.shape, 0)
        a_next = jnp.where(ti < s - 1, pltpu.roll(a, s - 1, 0), 0.0)
        lam_t = _linear_scan(a_next, dh, True)
        h_prev = jnp.where(ti >= 1, pltpu.roll(h, 1, 0), 0.0)
        dxc, dwa, dwx, dba, dbx, dlam = gates_vjp((lam_t * h_prev, lam_t))
        dxc_ref[...] = dxc
        first = pl.program_id(1) == 0

        @pl.when(first)
        def _():
            dwa_ref[...] = dwa
            dwx_ref[...] = dwx
            dba_ref[...] = dba
            dbx_ref[...] = dbx
            dlam_ref[...] = dlam

        @pl.when(jnp.logical_not(first))
        def _():
            dwa_ref[...] += dwa
            dwx_ref[...] += dwx
            dba_ref[...] += dba
            dbx_ref[...] += dbx
            dlam_ref[...] += dlam

    slab = pl.BlockSpec((s, cw), lambda g, b: (b, g))
    wsp = pl.BlockSpec((None, cw, cw), lambda g, b: (g, 0, 0))
    vec = pl.BlockSpec((1, cw), lambda g, b: (0, g))
    ng = c // cw
    return pl.pallas_call(
        body, grid=(ng, nb), in_specs=[slab, slab, slab, slab, wsp, wsp, vec, vec, vec],
        out_specs=[slab, slab, wsp, wsp, vec, vec, vec],
        out_shape=[jax.ShapeDtypeStruct((t, c), f32), jax.ShapeDtypeStruct((t, c), f32),
                   jax.ShapeDtypeStruct((ng, cw, cw), f32), jax.ShapeDtypeStruct((ng, cw, cw), f32),
                   jax.ShapeDtypeStruct((1, c), f32), jax.ShapeDtypeStruct((1, c), f32),
                   jax.ShapeDtypeStruct((1, c), f32)],
        compiler_params=_params(("parallel", "arbitrary")), name=name,
    )(xc, bg, h, dy, wa, wx, ba, bx, lam)


def _gdn_chunks_prep(qs, ks, vs, bls, als, a_log, dt_b, dots):
    nn, nt, csum = dots[0], dots[1], dots[3]
    hd = C_HEAD_DIM
    ri = lax.broadcasted_iota(jnp.int32, (CHUNK, CHUNK), 0)
    ci = lax.broadcasted_iota(jnp.int32, (CHUNK, CHUNK), 1)
    tril = ri >= ci
    strict = ri > ci
    eye = (ri == ci).astype(f32)
    qn = [q * lax.rsqrt(jnp.sum(q * q, -1, keepdims=True) + NORM_EPS) * (hd ** -0.5) for q in qs]
    kn = [k * lax.rsqrt(jnp.sum(k * k, -1, keepdims=True) + NORM_EPS) for k in ks]
    beta = [jax.nn.sigmoid(bl) for bl in bls]
    g = [-jnp.exp(a_log) * jax.nn.softplus(al + dt_b) for al in als]
    gc_sq = [csum(jnp.broadcast_to(g_, (CHUNK, CHUNK))) for g_ in g]
    gc = [csum(jnp.broadcast_to(g_, (CHUNK, hd))) for g_ in g]
    decay = [jnp.where(tril, jnp.exp(jnp.where(tril, s - s.T, 0.0)), 0.0) for s in gc_sq]
    kb = _each(jnp.multiply, kn, beta)
    kk = _each(nt, kb, kn)
    pw = [-jnp.where(strict, a * d, 0.0) for a, d in zip(kk, decay)]
    inv = [eye + p_ for p_ in pw]
    for _ in range(5):
        pw = _each(nn, pw, pw)
        inv = _each(jnp.add, inv, _each(nn, inv, pw))
    egc = [jnp.exp(c_) for c_ in gc]
    u = _each(nn, inv, _each(jnp.multiply, vs, beta))
    w = _each(nn, inv, _each(jnp.multiply, kb, egc))
    attn = _each(jnp.multiply, _each(nt, qn, kn), decay)
    g_last = [jnp.sum(jnp.broadcast_to(g_, (CHUNK, hd)), 0, keepdims=True) for g_ in g]
    qg = _each(jnp.multiply, qn, egc)
    kdec = [k_ * jnp.exp(gl_ - c_) for k_, gl_, c_ in zip(kn, g_last, gc)]
    return [(qg[i], kdec[i], w[i], u[i], attn[i], jnp.exp(g_last[i])) for i in range(len(qs))]


def _gdn_heads_step(states, qgs, kdecs, ws, us, attns, gls, zs, ng, dots):
    nn, tn = dots[0], dots[2]
    v_new = _each(jnp.subtract, us, _each(nn, ws, states))
    o = _each(jnp.add, _each(nn, qgs, states), _each(nn, attns, v_new))
    new = [s * gl for s, gl in zip(states, gls)]
    new = _each(jnp.add, new, _each(tn, kdecs, v_new))
    y = [o_ * lax.rsqrt(jnp.mean(o_ * o_, -1, keepdims=True) + NORM_EPS) * ng * _silu(z) for o_, z in zip(o, zs)]
    return y, new


def _loop_unrolled(n, unroll, load, compute, store, init):
    u = unroll if n % unroll == 0 else 1

    def trip(i, carry):
        idx = [i * u + j for j in range(u)]
        loaded = [load(k) for k in idx]
        results = compute(loaded)
        for k, r in zip(idx, results):
            carry = store(k, r, carry)
        return carry

    return lax.fori_loop(0, n // u, trip, init)


def _pick_lane(x, lane):
    li = lax.broadcasted_iota(jnp.int32, x.shape, 1)
    return jnp.sum(jnp.where(li == lane, x, 0.0), 1, keepdims=True)


def _put_lane(col, lane, width):
    li = lax.broadcasted_iota(jnp.int32, (col.shape[0], width), 1)
    return jnp.where(li == lane, col, 0.0)


def _gdn_specs(s, nc):
    hd = C_HEAD_DIM
    head = lambda off: pl.BlockSpec((s, hd), lambda b, h, off=off: (b, off + h))
    attn = pl.BlockSpec((None, s, CHUNK), lambda b, h: (h, b, 0))
    gl = pl.BlockSpec((None, nc * SUBLANES, hd), lambda b, h: (h, b, 0))
    ba = pl.BlockSpec((s, LANES), lambda b, h: (b, 0))
    sc8 = pl.BlockSpec((1, C_HEADS), lambda b, h: (0, 0))
    return head, attn, gl, ba, sc8


def gdn_prep_fwd(qkv, ba, a_log, dt_b, nb, *, name):
    t = qkv.shape[0]
    s = t // nb
    nc = s // CHUNK
    hd = C_HEAD_DIM
    head, attn_sp, gl_sp, ba_sp, sc8 = _gdn_specs(s, nc)

    def body(q_ref, k_ref, v_ref, ba_ref, alog_ref, dtb_ref, qg_ref, kd_ref, w_ref, u_ref, at_ref, gl_ref):
        h = pl.program_id(1)
        a_log_h = _pick_lane(alog_ref[...], h)
        dt_b_h = _pick_lane(dtb_ref[...], h)

        def load(n):
            rows = pl.ds(pl.multiple_of(n * CHUNK, CHUNK), CHUNK)
            bav = ba_ref[rows, :]
            return q_ref[rows, :], k_ref[rows, :], v_ref[rows, :], _pick_lane(bav, h), _pick_lane(bav, C_HEADS + h)

        def compute(loaded):
            return _gdn_chunks_prep(*[list(x) for x in zip(*loaded)], a_log_h, dt_b_h, RAW_DOTS)

        def store(n, outs, carry):
            rows = pl.ds(pl.multiple_of(n * CHUNK, CHUNK), CHUNK)
            qg_ref[rows, :] = outs[0].astype(bf16)
            kd_ref[rows, :] = outs[1].astype(bf16)
            w_ref[rows, :] = outs[2].astype(bf16)
            u_ref[rows, :] = outs[3]
            at_ref[rows, :] = outs[4].astype(bf16)
            gl_ref[pl.ds(pl.multiple_of(n * SUBLANES, SUBLANES), SUBLANES), :] = jnp.broadcast_to(outs[5], (SUBLANES, hd))
            return carry

        _loop_unrolled(nc, PREP_FWD_UNROLL, load, compute, store, 0)

    big = jax.ShapeDtypeStruct((t, C_WIDTH), f32)
    bigb = jax.ShapeDtypeStruct((t, C_WIDTH), bf16)
    return pl.pallas_call(
        body, grid=(nb, C_HEADS),
        in_specs=[head(0), head(C_HEADS), head(2 * C_HEADS), ba_sp, sc8, sc8],
        out_specs=[head(0)] * 4 + [attn_sp, gl_sp],
        out_shape=[bigb, bigb, bigb, big, jax.ShapeDtypeStruct((C_HEADS, t, CHUNK), bf16),
                               jax.ShapeDtypeStruct((C_HEADS, nb * nc * SUBLANES, hd), f32)],
        compiler_params=_params(("parallel", "parallel")), name=name,
    )(qkv, qkv, qkv, ba, a_log, dt_b)


def gdn_prep_bwd(qkv, ba, a_log, dt_b, cts, nb, *, name):
    t = qkv.shape[0]
    s = t // nb
    nc = s // CHUNK
    hd = C_HEAD_DIM
    head, attn_sp, gl_sp, ba_sp, sc8 = _gdn_specs(s, nc)

    def body(q_ref, k_ref, v_ref, ba_ref, alog_ref, dtb_ref, cqg, ckd, cw_, cu, cat, cgl,
             dq_ref, dk_ref, dv_ref, dba_ref, dalog_ref, ddtb_ref):
        b = pl.program_id(0)
        h = pl.program_id(1)
        a_log_h = _pick_lane(alog_ref[...], h)
        dt_b_h = _pick_lane(dtb_ref[...], h)
        prep = functools.partial(_gdn_chunks_prep, dots=VJP_DOTS)

        @pl.when(h == 0)
        def _():
            dba_ref[...] = jnp.zeros((s, LANES), f32)

        def load(n):
            rows = pl.ds(pl.multiple_of(n * CHUNK, CHUNK), CHUNK)
            bav = ba_ref[rows, :]
            cgl_n = cgl[pl.ds(pl.multiple_of(n * SUBLANES, SUBLANES), SUBLANES), :][0:1, :]
            primals = (q_ref[rows, :], k_ref[rows, :], v_ref[rows, :], _pick_lane(bav, h), _pick_lane(bav, C_HEADS + h))
            return primals, (cqg[rows, :], ckd[rows, :], cw_[rows, :], cu[rows, :], cat[rows, :], cgl_n), dba_ref[rows, :]

        def compute(loaded):
            primals = [list(x) for x in zip(*[item[0] for item in loaded])]
            _, vjp = jax.vjp(prep, *primals, a_log_h, dt_b_h)
            dqs, dks, dvs, dbls, dals, dalog, ddtb = vjp([item[1] for item in loaded])
            zero = jnp.zeros((1, 1), f32)
            return [((dqs[i], dks[i], dvs[i], dbls[i], dals[i], dalog if i == 0 else zero, ddtb if i == 0 else zero),
                     loaded[i][2]) for i in range(len(loaded))]

        def store(n, res, carry):
            (dq, dk, dv, dbl, dal, dalog_n, ddtb_n), dba_old = res
            rows = pl.ds(pl.multiple_of(n * CHUNK, CHUNK), CHUNK)
            dq_ref[rows, :] = dq
            dk_ref[rows, :] = dk
            dv_ref[rows, :] = dv
            dba_ref[rows, :] = dba_old + _put_lane(dbl, h, LANES) + _put_lane(dal, C_HEADS + h, LANES)
            return carry[0] + dalog_n, carry[1] + ddtb_n

        da_log, ddt_b = _loop_unrolled(nc, PREP_BWD_UNROLL, load, compute, store,
                                       (jnp.zeros((1, 1), f32), jnp.zeros((1, 1), f32)))
        first = jnp.logical_and(b == 0, h == 0)

        @pl.when(first)
        def _():
            dalog_ref[...] = _put_lane(da_log, h, LANES)
            ddtb_ref[...] = _put_lane(ddt_b, h, LANES)

        @pl.when(jnp.logical_not(first))
        def _():
            dalog_ref[...] += _put_lane(da_log, h, LANES)
            ddtb_ref[...] += _put_lane(ddt_b, h, LANES)

    big = jax.ShapeDtypeStruct((t, C_WIDTH), f32)
    vec = pl.BlockSpec((1, LANES), lambda b, h: (0, 0))
    return pl.pallas_call(
        body, grid=(nb, C_HEADS),
        in_specs=[head(0), head(C_HEADS), head(2 * C_HEADS), ba_sp, sc8, sc8] + [head(0)] * 4 + [attn_sp, gl_sp],
        out_specs=[head(0)] * 3 + [ba_sp, vec, vec],
        out_shape=[big] * 3 + [jax.ShapeDtypeStruct((t, LANES), f32), jax.ShapeDtypeStruct((1, LANES), f32),
                               jax.ShapeDtypeStruct((1, LANES), f32)],
        compiler_params=_params(("arbitrary", "arbitrary")), name=name,
    )(qkv, qkv, qkv, ba, a_log, dt_b, *cts)


def _gdn_rec_specs(sb, nsb, hp, reverse):
    hd = C_HEAD_DIM
    ncb = sb // CHUNK
    blk = (lambda b, k: b * nsb + (nsb - 1 - k)) if reverse else (lambda b, k: b * nsb + k)
    wide = pl.BlockSpec((sb, hp * hd), lambda b, j, k: (blk(b, k), j))
    attn = pl.BlockSpec((hp, sb, CHUNK), lambda b, j, k: (j, blk(b, k), 0))
    gl = pl.BlockSpec((hp, ncb * SUBLANES, hd), lambda b, j, k: (j, blk(b, k), 0))
    ng = pl.BlockSpec((1, hd), lambda b, j, k: (0, 0))
    states = pl.BlockSpec((hp, ncb, hd, hd), lambda b, j, k: (j, blk(b, k), 0, 0))
    return wide, attn, gl, ng, states


def gdn_rec_fwd(qg, kdec, w, u, attn, gl, z, ng, nb, *, name):
    t = qg.shape[0]
    s = t // nb
    sb = min(s, GDN_TIME_BLOCK)
    nsb = s // sb
    hd = C_HEAD_DIM
    hp = C_HEADS_PER_STEP
    wide, attn_sp, gl_sp, ng_sp, st_sp = _gdn_rec_specs(sb, nsb, hp, False)

    def body(qg_ref, kd_ref, w_ref, u_ref, at_ref, gl_ref, z_ref, ng_ref, y_ref, st_ref, carry_ref):
        @pl.when(pl.program_id(2) == 0)
        def _():
            carry_ref[...] = jnp.zeros((hp, hd, hd), f32)

        def chunk(n, states):
            for j in range(hp):
                st_ref[j, n] = states[j]
            rows = pl.ds(pl.multiple_of(n * CHUNK, CHUNK), CHUNK)
            grow = pl.ds(pl.multiple_of(n * SUBLANES, SUBLANES), SUBLANES)
            cols = [slice(j * hd, (j + 1) * hd) for j in range(hp)]
            ins = [(qg_ref[rows, c], kd_ref[rows, c], w_ref[rows, c], u_ref[rows, c], at_ref[j, rows, :],
                    gl_ref[j, grow, :][0:1, :], z_ref[rows, c]) for j, c in enumerate(cols)]
            ys, new = _gdn_heads_step(list(states), *[list(x) for x in zip(*ins)], ng_ref[...], RAW_DOTS)
            for j in range(hp):
                y_ref[rows, cols[j]] = ys[j]
            return tuple(new)

        last = lax.fori_loop(0, sb // CHUNK, chunk, tuple(carry_ref[j] for j in range(hp)))
        for j in range(hp):
            carry_ref[j] = last[j]

    return pl.pallas_call(
        body, grid=(nb, C_HEADS // hp, nsb),
        in_specs=[wide] * 4 + [attn_sp, gl_sp, wide, ng_sp], out_specs=[wide, st_sp],
        out_shape=[jax.ShapeDtypeStruct((t, C_WIDTH), f32), jax.ShapeDtypeStruct((C_HEADS, t // CHUNK, hd, hd), f32)],
        scratch_shapes=[pltpu.VMEM((hp, hd, hd), f32)],
        compiler_params=_params(("parallel", "parallel", "arbitrary")), name=name,
    )(qg, kdec, w, u, attn, gl, z, ng)


def gdn_rec_bwd(qg, kdec, w, u, attn, gl, z, ng, states, dy, nb, *, name):
    t = qg.shape[0]
    s = t // nb
    sb = min(s, GDN_TIME_BLOCK)
    nsb = s // sb
    nc = sb // CHUNK
    hd = C_HEAD_DIM
    hp = C_HEADS_PER_STEP
    wide, attn_sp, gl_sp, ng_sp, st_sp = _gdn_rec_specs(sb, nsb, hp, True)

    def body(qg_ref, kd_ref, w_ref, u_ref, at_ref, gl_ref, z_ref, ng_ref, states, dy_ref,
             dqg_ref, dkd_ref, dw_ref, du_ref, dat_ref, dgl_ref, dz_ref, dng_ref, carry_ref):
        step = functools.partial(_gdn_heads_step, dots=VJP_DOTS)

        @pl.when(pl.program_id(2) == 0)
        def _():
            carry_ref[...] = jnp.zeros((hp, hd, hd), f32)

        def operands(n):
            rows = pl.ds(pl.multiple_of(n * CHUNK, CHUNK), CHUNK)
            grow = pl.ds(pl.multiple_of(n * SUBLANES, SUBLANES), SUBLANES)
            cols = [slice(j * hd, (j + 1) * hd) for j in range(hp)]
            return ([qg_ref[rows, c].astype(f32) for c in cols], [kd_ref[rows, c].astype(f32) for c in cols],
                    [w_ref[rows, c].astype(f32) for c in cols], [u_ref[rows, c] for c in cols],
                    [at_ref[j, rows, :].astype(f32) for j in range(hp)],
                    [gl_ref[j, grow, :][0:1, :] for j in range(hp)], [z_ref[rows, c] for c in cols])

        def bwd_chunk(i, carry):
            n = nc - 1 - i
            rows = pl.ds(pl.multiple_of(n * CHUNK, CHUNK), CHUNK)
            grow = pl.ds(pl.multiple_of(n * SUBLANES, SUBLANES), SUBLANES)
            dsts, dng = carry
            dys = [dy_ref[rows, j * hd:(j + 1) * hd] for j in range(hp)]
            _, vjp = jax.vjp(step, [states[j, n] for j in range(hp)], *operands(n), ng_ref[...])
            dst, dqg, dkd, dw, du, dat, dgl, dz, dng_n = vjp((dys, list(dsts)))
            for j in range(hp):
                cols = slice(j * hd, (j + 1) * hd)
                dqg_ref[rows, cols] = dqg[j]
                dkd_ref[rows, cols] = dkd[j]
                dw_ref[rows, cols] = dw[j]
                du_ref[rows, cols] = du[j]
                dat_ref[j, rows, :] = dat[j]
                dgl_ref[j, grow, :] = jnp.broadcast_to(dgl[j], (SUBLANES, hd))
                dz_ref[rows, cols] = dz[j]
            return tuple(dst), dng + dng_n

        dlast, dng = lax.fori_loop(0, nc, bwd_chunk,
                                   (tuple(carry_ref[j] for j in range(hp)), jnp.zeros((1, hd), f32)))
        for j in range(hp):
            carry_ref[j] = dlast[j]
        first = jnp.logical_and(jnp.logical_and(pl.program_id(0) == 0, pl.program_id(1) == 0), pl.program_id(2) == 0)

        @pl.when(first)
        def _():
            dng_ref[...] = dng

        @pl.when(jnp.logical_not(first))
        def _():
            dng_ref[...] += dng

    big = jax.ShapeDtypeStruct((t, C_WIDTH), f32)
    return pl.pallas_call(
        body, grid=(nb, C_HEADS // hp, nsb),
        in_specs=[wide] * 4 + [attn_sp, gl_sp, wide, ng_sp, st_sp, wide],
        out_specs=[wide] * 4 + [attn_sp, gl_sp, wide, ng_sp],
        out_shape=[big] * 4 + [jax.ShapeDtypeStruct(attn.shape, f32), jax.ShapeDtypeStruct(gl.shape, f32), big,
                               jax.ShapeDtypeStruct((1, hd), f32)],
        scratch_shapes=[pltpu.VMEM((hp, hd, hd), f32)],
        compiler_params=_params(("arbitrary", "arbitrary", "arbitrary")), name=name,
    )(qg, kdec, w, u, attn, gl, z, ng, states, dy)


def _blockdiag_slabs(w):
    per = GROUP_W // B_BLOCK
    slabs = jnp.zeros((B_BLOCKS // per, GROUP_W, GROUP_W), w.dtype)
    for h in range(B_BLOCKS):
        o = (h % per) * B_BLOCK
        slabs = slabs.at[h // per, o:o + B_BLOCK, o:o + B_BLOCK].set(w[h])
    return slabs


def _slab_blocks(slabs):
    per = GROUP_W // B_BLOCK
    return jnp.stack([slabs[h // per, (h % per) * B_BLOCK:(h % per + 1) * B_BLOCK,
                            (h % per) * B_BLOCK:(h % per + 1) * B_BLOCK] for h in range(B_BLOCKS)])


def _mixer_ab_fwd(x1, x1b, W, g, b, nb, tag):
    w_in = W["ab_w_in"][0].astype(bf16)
    o1, o2 = A_WIDTH + 2 * A_KV_WIDTH, A_WIDTH + 2 * A_KV_WIDTH + B_WIDTH
    w_qkv, w_bx, w_bg = w_in[:, :o1], w_in[:, o1:o2], w_in[:, o2:]
    pqkv = mm_nn(x1b,w_qkv, name=tag + "_in_qkv")
    pbx = mm_nn(x1b,w_bx, name=tag + "_in_bx")
    pbg = mm_nn(x1b,w_bg, name=tag + "_in_bg")
    ya = attn_fwd(pqkv, W["a_sinks"], nb, name=tag + "_attn_fwd")
    xc = conv_fwd(pbx, W["b_conv_w"][0], W["b_conv_b"], False, nb, name=tag + "_conv_fwd")
    wa_s, wx_s = _blockdiag_slabs(W["b_wa"][0]), _blockdiag_slabs(W["b_wx"][0])
    yb, hh = rglru_fwd(xc, pbg, wa_s, wx_s, W["b_ba"], W["b_bx"], W["b_lam"], nb, name=tag + "_rglru_fwd")
    w_out = W["ab_w_out"][0].astype(bf16)
    x2, z1 = proj_ln([ya, yb], [w_out[:A_WIDTH], w_out[A_WIDTH:]], x1, g, b, name=tag + "_out_ln")
    saved = (pqkv, pbx, pbg, ya, xc, yb, hh, wa_s, wx_s, w_qkv, w_bx, w_bg, w_out)
    return x2, z1, saved


def _mixer_ab_bwd(x1b, dz1, dz1b, W, saved, nb, tag):
    pqkv, pbx, pbg, ya, xc, yb, hh, wa_s, wx_s, w_qkv, w_bx, w_bg, w_out = saved
    dya = mm_nn(dz1b, w_out[:A_WIDTH].T, name=tag + "_dya")
    dyb = mm_nn(dz1b, w_out[A_WIDTH:].T, name=tag + "_dyb")
    dwo = jnp.concatenate([mm_tn(ya, dz1b, name=tag + "_dwo_a"), mm_tn(yb, dz1b, name=tag + "_dwo_b")], 0)
    dpqkv, dsinks = attn_bwd(pqkv, W["a_sinks"], dya, nb, name=tag + "_attn_bwd")
    dxc, dpbg, dwa_s, dwx_s, dba, dbx, dlam = rglru_bwd(xc, pbg, hh, dyb, wa_s, wx_s, W["b_ba"], W["b_bx"],
                                                       W["b_lam"], nb, name=tag + "_rglru_bwd")
    dpbx, dconv_w, dconv_b = conv_bwd(pbx, W["b_conv_w"][0], W["b_conv_b"], dxc, False, nb, name=tag + "_conv_bwd")
    dw_in = jnp.concatenate([mm_tn(x1b,dpqkv, name=tag + "_dwin_qkv"), mm_tn(x1b,dpbx, name=tag + "_dwin_bx"),
                             mm_tn(x1b,dpbg, name=tag + "_dwin_bg")], 1)
    dx1 = mm_nn(dpqkv, w_qkv.T, add=dz1, add_scale=DN_ALPHA, name=tag + "_dx_qkv")
    dx1 = mm_nn(dpbx, w_bx.T, add=dx1, name=tag + "_dx_bx")
    dx1 = mm_nn(dpbg, w_bg.T, add=dx1, name=tag + "_dx_bg")
    grads = {"ab_w_in": dw_in[None], "a_sinks": dsinks, "b_conv_w": dconv_w[None], "b_conv_b": dconv_b,
             "b_wa": _slab_blocks(dwa_s)[None], "b_ba": dba, "b_wx": _slab_blocks(dwx_s)[None], "b_bx": dbx,
             "b_lam": dlam, "ab_w_out": dwo[None]}
    return dx1, grads


def _mixer_c_fwd(x1, x1b, W, g, b, nb, tag):
    w_in = W["c_w_in"][0].astype(bf16)
    d = w_in.shape[0]
    o1, o2 = 3 * C_WIDTH, 4 * C_WIDTH
    w_qkv, w_z = w_in[:, :o1], w_in[:, o1:o2]
    w_ba = jnp.concatenate([w_in[:, o2:], jnp.zeros((d, LANES - 2 * C_HEADS), bf16)], 1)
    pqkv = mm_nn(x1b,w_qkv, name=tag + "_in_qkv")
    pz = mm_nn(x1b,w_z, name=tag + "_in_z")
    pba = mm_nn(x1b,w_ba, name=tag + "_in_ba")
    zero_b = jnp.zeros((1, o1), f32)
    qkvc = conv_fwd(pqkv, W["c_conv_w"][0], zero_b, True, nb, name=tag + "_conv_fwd")
    prep = gdn_prep_fwd(qkvc, pba, W["c_a_log"], W["c_dt_bias"], nb, name=tag + "_prep_fwd")
    yc, states = gdn_rec_fwd(*prep, pz, W["c_norm_g"], nb, name=tag + "_rec_fwd")
    w_out = W["c_w_out"][0].astype(bf16)
    x2, z1 = proj_ln([yc], [w_out], x1, g, b, name=tag + "_out_ln")
    saved = (pqkv, pz, pba, qkvc, prep, states, yc, w_qkv, w_z, w_ba, w_out, zero_b)
    return x2, z1, saved


def _mixer_c_bwd(x1b, dz1, dz1b, W, saved, nb, tag):
    pqkv, pz, pba, qkvc, prep, states, yc, w_qkv, w_z, w_ba, w_out, zero_b = saved
    dyc = mm_nn(dz1b, w_out.T, name=tag + "_dyc")
    dwo = mm_tn(yc, dz1b, name=tag + "_dwo")
    rec = gdn_rec_bwd(*prep, pz, W["c_norm_g"], states, dyc, nb, name=tag + "_rec_bwd")
    cts, dpz, dng = rec[:6], rec[6], rec[7]
    dq, dk, dv, dpba, dalog, ddtb = gdn_prep_bwd(qkvc, pba, W["c_a_log"], W["c_dt_bias"], cts, nb,
                                                 name=tag + "_prep_bwd")
    dqkvc = jnp.concatenate([dq, dk, dv], 1)
    dpqkv, dconv_w, _ = conv_bwd(pqkv, W["c_conv_w"][0], zero_b, dqkvc, True, nb, name=tag + "_conv_bwd")
    dw_in = jnp.concatenate([mm_tn(x1b,dpqkv, name=tag + "_dwin_qkv"), mm_tn(x1b,dpz, name=tag + "_dwin_z"),
                             mm_tn(x1b,dpba, name=tag + "_dwin_ba")[:, :2 * C_HEADS]], 1)
    dx1 = mm_nn(dpqkv, w_qkv.T, add=dz1, add_scale=DN_ALPHA, name=tag + "_dx_qkv")
    dx1 = mm_nn(dpz, w_z.T, add=dx1, name=tag + "_dx_z")
    dx1 = mm_nn(dpba, w_ba.T, add=dx1, name=tag + "_dx_ba")
    grads = {"c_w_in": dw_in[None], "c_conv_w": dconv_w[None], "c_a_log": dalog[:, :C_HEADS],
             "c_dt_bias": ddtb[:, :C_HEADS], "c_norm_g": dng, "c_w_out": dwo[None]}
    return dx1, grads


def _local_step(x, p, target, W, F, on_ffn_grads):
    nb, s, d = x.shape
    t = nb * s
    h = x.reshape(t, d)
    tape = []
    for i in range(DEPTH):
        tag = f"l{i}"
        f1 = [F[k][i] for k in ("ffn1_wg", "ffn1_wu", "ffn1_wd")]
        f2 = [F[k][i] for k in ("ffn2_wg", "ffn2_wu", "ffn2_wd")]
        lg = [W["ln_g"][i, k][None] for k in range(3)]
        lb = [W["ln_b"][i, k][None] for k in range(3)]
        x1, z0, x1b = ffn_fwd(h, *f1, lg[0], lb[0], name=tag + "_ffn1_fwd")
        mixer = _mixer_ab_fwd if i % 2 == 0 else _mixer_c_fwd
        x2, z1, msaved = mixer(x1, x1b, W, lg[1], lb[1], nb, tag + "_mix")
        x3, z2, _ = ffn_fwd(x2, *f2, lg[2], lb[2], name=tag + "_ffn2_fwd")
        pi = p[i].reshape(t, -1)
        pw = (W["ple_wg"][i].astype(bf16), W["ple_bg"][i][None], W["ple_wp"][i].astype(bf16))
        x4 = ple_fwd(x3, pi, *pw, name=tag + "_ple_fwd")
        tape.append((h, z0, x1b, msaved, z1, x2, z2, x3, pi, pw, lg))
        h = x4
    dh, sq = loss_head(h, target.reshape(t, d), name="loss_head")
    loss = 0.5 * jnp.sum(sq) / d
    per_layer = [None] * DEPTH
    grads = {}
    for i in reversed(range(DEPTH)):
        tag = f"l{i}"
        h_in, z0, x1b, msaved, z1, x2, z2, x3, pi, pw, lg = tape[i]
        dx3, dple_wg, dple_bg, dple_wp = ple_bwd(x3, pi, dh, pw[0], pw[0].T, pw[1], pw[2], name=tag + "_ple_bwd")
        dz2, dz2b, dg2, db2 = ln_bwd(z2, dx3, lg[2], name=tag + "_ln2_bwd")
        f1 = [F[k][i] for k in ("ffn1_wg", "ffn1_wu", "ffn1_wd")]
        f2 = [F[k][i] for k in ("ffn2_wg", "ffn2_wu", "ffn2_wd")]
        dgate, dup, *df2 = ffn_bwd_weights(x2.astype(bf16), dz2b, *f2, name=tag + "_ffn2_bwd_w")
        on_ffn_grads(i, 3, df2)
        dx2 = ffn_bwd_input(dgate, dup, f2[0], f2[1], dz2, name=tag + "_ffn2_bwd_x")
        dz1, dz1b, dg1, db1 = ln_bwd(z1, dx2, lg[1], name=tag + "_ln1_bwd")
        mixer_bwd = _mixer_ab_bwd if i % 2 == 0 else _mixer_c_bwd
        dx1, mgrads = mixer_bwd(x1b, dz1, dz1b, W, msaved, nb, tag + "_mix")
        grads.update(mgrads)
        dz0, dz0b, dg0, db0 = ln_bwd(z0, dx1, lg[0], name=tag + "_ln0_bwd")
        dgate, dup, *df1 = ffn_bwd_weights(h_in.astype(bf16), dz0b, *f1, name=tag + "_ffn1_bwd_w")
        on_ffn_grads(i, 0, df1)
        dh = ffn_bwd_input(dgate, dup, f1[0], f1[1], dz0, name=tag + "_ffn1_bwd_x")
        per_layer[i] = {"ln_g": jnp.concatenate([dg0, dg1, dg2], 0), "ln_b": jnp.concatenate([db0, db1, db2], 0),
                        "ple_wg": dple_wg, "ple_bg": dple_bg[0], "ple_wp": dple_wp}
    for k in per_layer[0]:
        grads[k] = jnp.stack([per_layer[i][k] for i in range(DEPTH)])
    return loss, dh.reshape(nb, s, d), grads


WEIGHT_NAMES = ("ffn1_wg", "ffn1_wu", "ffn1_wd", "ffn2_wg", "ffn2_wu", "ffn2_wd", "ln_g", "ln_b", "ple_wg", "ple_bg",
                "ple_wp", "ab_w_in", "a_sinks", "b_conv_w", "b_conv_b", "b_wa", "b_ba", "b_wx", "b_bx", "b_lam",
                "ab_w_out", "c_w_in", "c_conv_w", "c_a_log", "c_dt_bias", "c_norm_g", "c_w_out")
NATIVE_NAMES = WEIGHT_NAMES[:6]
PACKED_NAMES = WEIGHT_NAMES[6:]
GATHER_SMALL = ("ln_g", "ln_b", "b_conv_w", "c_conv_w")
GATHER_BIG = ("ple_wg", "ple_wp", "ab_w_in", "ab_w_out", "c_w_in", "c_w_out")
SHARD_AXIS = {"ffn1_wg": 2, "ffn1_wu": 2, "ffn1_wd": 1, "ffn2_wg": 2, "ffn2_wu": 2, "ffn2_wd": 1, "ln_g": 2, "ln_b": 2,
              "ple_wg": 1, "ple_wp": 2, "ab_w_in": 2, "b_conv_w": 2, "ab_w_out": 1, "c_w_in": 2, "c_conv_w": 2,
              "c_w_out": 1}
N_CHIPS = 4
PACK_COLS = LANES
PACK_TILE_MULTIPLE = 256
ELEMENTWISE_BLOCK_ELEMS = 128 * 1024


def _row_tile(r, cols):
    return _tile(r, max(2 * SUBLANES, ELEMENTWISE_BLOCK_ELEMS // cols), 2 * SUBLANES)
MESH = pl.DeviceIdType.MESH
ANY = pl.BlockSpec(memory_space=pl.ANY)


def _tiled_dims(shape):
    w = shape[-1]
    r = 1
    for dim in shape[:-1]:
        r *= dim
    return r, w, -(-r // SUBLANES) * SUBLANES, -(-w // LANES) * LANES


def _pack(pieces, lead=()):
    k = len(lead)
    tiles = []
    for a in pieces:
        r, w, rp, wp = _tiled_dims(a.shape[k:])
        a2 = jnp.pad(a.reshape(lead + (r, w)), [(0, 0)] * k + [(0, rp - r), (0, wp - w)])
        a2 = a2.reshape(lead + (rp // SUBLANES, SUBLANES, wp // LANES, LANES))
        a2 = jnp.swapaxes(a2, k + 1, k + 2)
        tiles.append(a2.reshape(lead + (-1, SUBLANES, LANES)))
    flat = jnp.concatenate(tiles, axis=k)
    n = flat.shape[k]
    n_pad = -(-n // PACK_TILE_MULTIPLE) * PACK_TILE_MULTIPLE
    flat = jnp.pad(flat, [(0, 0)] * k + [(0, n_pad - n), (0, 0), (0, 0)])
    return flat.reshape(lead + (n_pad * SUBLANES, PACK_COLS))


def _unpack(pack, shapes, lead=()):
    k = len(lead)
    flat = pack.reshape(lead + (-1, SUBLANES, LANES))
    out, o = [], 0
    for shp in shapes:
        r, w, rp, wp = _tiled_dims(shp)
        n = (rp // SUBLANES) * (wp // LANES)
        a2 = lax.slice_in_dim(flat, o, o + n, axis=k).reshape(lead + (rp // SUBLANES, wp // LANES, SUBLANES, LANES))
        a2 = jnp.swapaxes(a2, k + 1, k + 2).reshape(lead + (rp, wp))
        a2 = lax.slice_in_dim(lax.slice_in_dim(a2, 0, r, axis=k), 0, w, axis=k + 1)
        out.append(a2.reshape(lead + tuple(shp)))
        o += n
    return out


def _mesh_position():
    x, y, c = lax.axis_index("x"), lax.axis_index("y"), lax.axis_index("c")
    chips = [(1 - x, y), (x, 1 - y), (1 - x, 1 - y)]
    return x, y, c, chips


def _remote(src, dst, send_sems, recv_sems, k, to):
    return pltpu.make_async_remote_copy(src_ref=src, dst_ref=dst, send_sem=send_sems.at[k], recv_sem=recv_sems.at[k],
                                        device_id=to, device_id_type=MESH)


def _sems(n):
    return pltpu.SemaphoreType.DMA((n,))


def place_slot(parts, slots, n_slots, dtype, from_slot, *, name):
    n = len(parts)
    r, cols = parts[0].shape[-2:]
    tr = _row_tile(r, cols)

    def body(src_ref, dst_ref, *refs):
        for a in range(n):
            refs[n + a][...] = refs[a][...].astype(dtype)

    dst = pl.BlockSpec((None, tr, cols), lambda i, src_ref, dst_ref: (dst_ref[0], i, 0))
    src = (pl.BlockSpec((None, tr, cols), lambda i, src_ref, dst_ref: (src_ref[0], i, 0)) if from_slot
           else pl.BlockSpec((tr, cols), lambda i, src_ref, dst_ref: (i, 0)))
    return pl.pallas_call(
        body,
        grid_spec=pltpu.PrefetchScalarGridSpec(num_scalar_prefetch=2, grid=(r // tr,), in_specs=[src] * n,
                                               out_specs=[dst] * n),
        out_shape=[jax.ShapeDtypeStruct((n_slots, r, cols), dtype)] * n,
        compiler_params=_params(("parallel",)), name=name,
    )(*slots, *parts)


def gather_slots_async(bufs, collective_id, sources=None, *, name):
    n = len(bufs)
    refs = [jax.new_ref(b, memory_space=pltpu.MemorySpace.HBM) for b in bufs]
    src_refs = None if sources is None else [jax.new_ref(s_, memory_space=pltpu.MemorySpace.HBM) for s_ in sources]

    @pl.kernel(mesh=plsc.ScalarSubcoreMesh(axis_name="sequencer", num_cores=1), name=name,
               scratch_types=(_sems(3 * n), _sems(3 * n)),
               compiler_params=pltpu.CompilerParams(collective_id=collective_id))
    def launch(send_sems, recv_sems):
        x, y, c, chips = _mesh_position()
        me = 2 * x + y
        barrier = pltpu.get_barrier_semaphore()
        for cx, cy in chips:
            pl.semaphore_signal(barrier, inc=1, device_id=(cx, cy, c), device_id_type=MESH)
        pl.semaphore_wait(barrier, len(chips))
        sends = []
        for j, (cx, cy) in enumerate(chips):
            for a in range(n):
                own = refs[a].at[me]
                src = own if src_refs is None else src_refs[a].at[2 * cx + cy]
                cp = _remote(src, own, send_sems, recv_sems, 3 * a + j, (cx, cy, c))
                cp.start()
                sends.append(cp)
        for j, (cx, cy) in enumerate(chips):
            for a in range(n):
                got = refs[a].at[2 * cx + cy]
                _remote(got, got, send_sems, recv_sems, 3 * a + j, (cx, cy, c)).wait_recv()
        for cp in sends:
            cp.wait_send()

    launch()
    return [r[...] for r in refs]


N_DEVICES = 8
PEER_FLIPS = tuple((dx, dy, dc) for dx in (0, 1) for dy in (0, 1) for dc in (0, 1) if dx or dy or dc)


def exchange_partials_async(sends, recvs, collective_id, *, name):
    n = len(sends)
    s_refs = [jax.new_ref(a, memory_space=pltpu.MemorySpace.HBM) for a in sends]
    r_refs = [jax.new_ref(a, memory_space=pltpu.MemorySpace.HBM) for a in recvs]
    k = len(PEER_FLIPS)

    @pl.kernel(mesh=plsc.ScalarSubcoreMesh(axis_name="sequencer", num_cores=1), name=name,
               scratch_types=(_sems(k), _sems(k)), compiler_params=pltpu.CompilerParams(collective_id=collective_id))
    def launch(send_sems, recv_sems):
        x, y, c, _ = _mesh_position()
        me = 4 * x + 2 * y + c
        peers = [(1 - x if dx else x, 1 - y if dy else y, 1 - c if dc else c) for dx, dy, dc in PEER_FLIPS]
        barrier = pltpu.get_barrier_semaphore()
        for peer in peers:
            pl.semaphore_signal(barrier, inc=1, device_id=peer, device_id_type=MESH)
        pl.semaphore_wait(barrier, len(peers))
        sends_started = []
        for j, (px, py, pc) in enumerate(peers):
            for a in range(n):
                cp = _remote(s_refs[a].at[2 * px + py], r_refs[a].at[me], send_sems, recv_sems, j, (px, py, pc))
                cp.start()
                sends_started.append(cp)
        for j, (px, py, pc) in enumerate(peers):
            for a in range(n):
                got = r_refs[a].at[4 * px + 2 * py + pc]
                _remote(got, got, send_sems, recv_sems, j, (px, py, pc)).wait_recv()
        for cp in sends_started:
            cp.wait_send()

    launch()
    return [r[...] for r in r_refs]


def sibling_exchange(gs, *, name):
    n = len(gs)

    def body(*refs):
        g_refs, out_refs = refs[:n], refs[n:2 * n]
        send_sems, recv_sems = refs[2 * n:]
        x, y, c, _ = _mesh_position()
        cps = [_remote(g_refs[a].at[:, 1 - c], out_refs[a], send_sems, recv_sems, a, (x, y, 1 - c)) for a in range(n)]
        for cp in cps:
            cp.start()
        for cp in cps:
            cp.wait()

    return pl.pallas_call(
        body, out_shape=[jax.ShapeDtypeStruct(g.shape[:1] + g.shape[2:], g.dtype) for g in gs],
        in_specs=[ANY] * n, out_specs=[ANY] * n, scratch_shapes=[_sems(n), _sems(n)], name=name,
    )(*gs)


def add_own_half(gs, others, c_idx, dtype, *, name):
    n = len(gs)
    ns, _, r, cols = gs[0].shape
    tr = _row_tile(r, cols)

    def body(c_ref, *refs):
        for a in range(n):
            refs[2 * n + a][...] = (refs[a][...] + refs[n + a][...]).astype(dtype)

    own = pl.BlockSpec((None, None, tr, cols), lambda s, i, c_ref: (s, c_ref[0], i, 0))
    oth = pl.BlockSpec((None, tr, cols), lambda s, i, c_ref: (s, i, 0))
    return pl.pallas_call(
        body,
        grid_spec=pltpu.PrefetchScalarGridSpec(num_scalar_prefetch=1, grid=(ns, r // tr),
                                               in_specs=[own] * n + [oth] * n, out_specs=[oth] * n),
        out_shape=[jax.ShapeDtypeStruct((ns, r, cols), dtype)] * n,
        compiler_params=_params(("parallel", "parallel")), name=name,
    )(c_idx, *gs, *others)


def sum_slots(qs, *, name):
    n = len(qs)
    ns, r, cols = qs[0].shape
    tr = _row_tile(r, cols * ns)

    def body(*refs):
        for a in range(n):
            q_ref = refs[a]
            acc = q_ref[0].astype(f32) + q_ref[1].astype(f32)
            for i in range(2, ns):
                acc = acc + q_ref[i].astype(f32)
            refs[n + a][...] = acc

    return pl.pallas_call(
        body, grid=(r // tr,), in_specs=[pl.BlockSpec((ns, tr, cols), lambda i: (0, i, 0))] * n,
        out_specs=[pl.BlockSpec((tr, cols), lambda i: (i, 0))] * n,
        out_shape=[jax.ShapeDtypeStruct((r, cols), f32)] * n,
        compiler_params=_params(("parallel",)), name=name,
    )(*qs)


def sibling_share(bufs, *, name):
    n = len(bufs)

    def body(*refs):
        out_refs = refs[n:2 * n]
        send_sems, recv_sems = refs[2 * n:]
        x, y, c, _ = _mesh_position()
        sibling = (x, y, 1 - c)
        cps = []
        for a in range(n):
            own = out_refs[a].at[c]
            cp = _remote(own, own, send_sems, recv_sems, a, sibling)
            cp.start()
            cps.append(cp)
        for a in range(n):
            theirs = out_refs[a].at[1 - c]
            _remote(theirs, theirs, send_sems, recv_sems, a, sibling).wait_recv()
        for cp in cps:
            cp.wait_send()

    return pl.pallas_call(
        body, out_shape=[jax.ShapeDtypeStruct(b.shape, b.dtype) for b in bufs], in_specs=[ANY] * n,
        out_specs=[ANY] * n, scratch_shapes=[_sems(n), _sems(n)],
        input_output_aliases={a: a for a in range(n)}, name=name,
    )(*bufs)


def adamw(ws, gs, ms, vs, *, name):
    n = len(ws)
    r, cols = ws[0].shape
    tr = _row_tile(r, cols)

    def body(*refs):
        for a in range(n):
            w_ref, g_ref, m_ref, v_ref = (refs[k * n + a] for k in range(4))
            d_ref, m2_ref, v2_ref = (refs[(4 + k) * n + a] for k in range(3))
            g_ = g_ref[...]
            m2 = ADAM_B1 * m_ref[...] + (1.0 - ADAM_B1) * g_
            v2 = ADAM_B2 * v_ref[...] + (1.0 - ADAM_B2) * (g_ * g_)
            m_hat = m2 / (1.0 - ADAM_B1 ** ADAM_STEP)
            v_hat = v2 / (1.0 - ADAM_B2 ** ADAM_STEP)
            d_ref[...] = -ADAM_LR * (m_hat / (jnp.sqrt(v_hat) + ADAM_EPS) + ADAM_WD * w_ref[...])
            m2_ref[...] = m2
            v2_ref[...] = v2

    row = pl.BlockSpec((tr, cols), lambda i: (i, 0))
    out = pl.pallas_call(
        body, grid=(r // tr,), in_specs=[row] * (4 * n), out_specs=[row] * (3 * n),
        out_shape=[jax.ShapeDtypeStruct((r, cols), f32)] * (3 * n),
        compiler_params=_params(("parallel",)), name=name,
    )(*ws, *gs, *ms, *vs)
    return out[:n], out[n:2 * n], out[2 * n:]


def _full_weights(gathered, names, weights):
    pieces = _unpack(gathered, [weights[k].shape for k in names], lead=(N_CHIPS,))
    full = {}
    for name, pc in zip(names, pieces):
        ax = SHARD_AXIS[name]
        shp = weights[name].shape
        full[name] = jnp.moveaxis(pc, 0, ax).reshape(shp[:ax] + (N_CHIPS * shp[ax],) + shp[ax + 1:])
    return full


def _grad_pack(grads, shapes):
    pieces = []
    for name, shp in zip(PACKED_NAMES, shapes):
        g = grads[name]
        ax = SHARD_AXIS.get(name)
        if ax is None:
            pieces.append(jnp.broadcast_to(g.reshape(shp)[None], (N_CHIPS,) + tuple(shp)))
        else:
            pieces.append(jnp.stack(jnp.split(g, N_CHIPS, axis=ax)))
    return _pack(pieces, lead=(N_CHIPS,))


def _by_shape(arrays):
    groups = {}
    for i, a in enumerate(arrays):
        groups.setdefault(a.shape, []).append(i)
    return list(groups.values())


def _grouped(fn, lists, n_out, tag):
    outs = [[None] * len(lists[0]) for _ in range(n_out)]
    for gi, idx in enumerate(_by_shape(lists[0])):
        res = fn(*[[lst[i] for i in idx] for lst in lists], name=f"{tag}_{gi}")
        res = res if n_out > 1 else (res,)
        for k in range(n_out):
            for i, r in zip(idx, res[k]):
                outs[k][i] = r
    return outs if n_out > 1 else outs[0]


def _train_step(x, p, loss_target, weights, m, v):
    packed_w = [weights[k] for k in PACKED_NAMES]
    shapes = [w.shape for w in packed_w]
    halves = lambda a: a.reshape((2, a.shape[0] // 2) + a.shape[1:])
    local = [weights[k] for k in NATIVE_NAMES] + [halves(_pack(packed_w))]
    local_m = [m[k] for k in NATIVE_NAMES] + [halves(_pack([m[k] for k in PACKED_NAMES]))]
    local_v = [v[k] for k in NATIVE_NAMES] + [halves(_pack([v[k] for k in PACKED_NAMES]))]
    flat = lambda lst: [a.reshape((-1, a.shape[-1])) for a in lst]
    c_idx = lax.axis_index("c").astype(jnp.int32).reshape(1)
    chip_idx = (2 * lax.axis_index("x") + lax.axis_index("y")).astype(jnp.int32).reshape(1)
    c2 = (c_idx, c_idx)
    chip2 = (chip_idx, chip_idx)
    chip_dev = (chip_idx, 2 * chip_idx + c_idx)

    def placed(arrays, slot, n_slots, dtype, from_slot, tag):
        return _grouped(lambda a, name: place_slot(a, slot, n_slots, dtype, from_slot, name=name), [arrays], 1, tag)

    ffn_own = [weights[k][i] for i in range(DEPTH) for k in NATIVE_NAMES]
    ffn_bufs = placed(ffn_own, chip2, N_CHIPS, bf16, False, "place_ffn_weights")
    small_buf = placed([_pack([weights[k] for k in GATHER_SMALL])], chip2, N_CHIPS, f32, False, "place_small_weights")
    big_buf = placed([_pack([weights[k] for k in GATHER_BIG])], chip2, N_CHIPS, bf16, False, "place_big_weights")
    group = len(NATIVE_NAMES) // 2
    n_ffn_groups = len(ffn_bufs) // group
    full = {k: weights[k] for k in PACKED_NAMES if k not in SHARD_AXIS}
    full.update(_full_weights(gather_slots_async(small_buf, collective_id=n_ffn_groups + 1,
                                                 name="comm_gather_small")[0], GATHER_SMALL, weights))
    ffn_gathered = []
    for gi in range(n_ffn_groups):
        ffn_gathered += gather_slots_async(ffn_bufs[gi * group:(gi + 1) * group], collective_id=1 + gi,
                                           name=f"comm_gather_ffn_{gi}")
        if gi == 0:
            full.update(_full_weights(gather_slots_async(big_buf, collective_id=n_ffn_groups + 2,
                                                         name="comm_gather_big")[0], GATHER_BIG, weights))
    ffn_weights = {k: [ffn_gathered[i * len(NATIVE_NAMES) + j] for i in range(DEPTH)] for j, k in enumerate(NATIVE_NAMES)}
    first_grad_id = n_ffn_groups + 3
    in_flight = {}

    def on_ffn_grads(layer, first, partials):
        tag = f"ffn_grads_l{layer}_{first}"
        recvs = placed(partials, chip_dev, N_DEVICES, bf16, True, "place_" + tag)
        got = exchange_partials_async(partials, recvs, collective_id=first_grad_id + len(in_flight), name="comm_" + tag)
        in_flight[(layer, first)] = got

    loss, grad_x, grads = _local_step(x, p, loss_target, full, ffn_weights, on_ffn_grads)
    gs = [_grad_pack(grads, shapes).reshape((N_CHIPS,) + local[-1].shape)]
    others = sibling_exchange(gs, name="comm_grad_sibling")
    chip_sums = add_own_half(gs, others, c_idx, f32, name="grad_add_sibling")
    own = placed(chip_sums, chip2, N_CHIPS, f32, True, "place_own_partial")
    slots = gather_slots_async(own, collective_id=first_grad_id + len(in_flight), sources=chip_sums, name="comm_grad_chips")
    ffn_sums = {}
    for (layer, first), got in in_flight.items():
        sums = _grouped(sum_slots, [got], 1, f"grad_sum_ffn_l{layer}_{first}")
        for j, g in enumerate(sums):
            ffn_sums[(NATIVE_NAMES[first + j], layer)] = g
    nn_ = len(NATIVE_NAMES)
    gsum = [jnp.stack([ffn_sums[(k, i)] for i in range(DEPTH)]) for k in NATIVE_NAMES]
    delta, m2, v2 = _grouped(adamw, [flat(local[:nn_]), flat(gsum), flat(local_m[:nn_]), flat(local_v[:nn_])], 3,
                             "adamw_ffn")
    mine = sum_slots(slots, name="grad_sum_chips")
    gsum += list(sibling_share(placed(mine, c2, 2, f32, False, "place_own_half"), name="comm_grad_share"))
    pack_upd = adamw(flat(local[nn_:]), flat(gsum[nn_:]), flat(local_m[nn_:]), flat(local_v[nn_:]), name="adamw_packed")
    delta, m2, v2 = (a + list(b_) for a, b_ in zip((delta, m2, v2), pack_upd))
    loss = lax.psum(loss, ("x", "y", "c"))
    outs = []
    for res in (gsum, delta, m2, v2):
        by_name = {k: a.reshape(weights[k].shape) for k, a in zip(NATIVE_NAMES, res[:-1])}
        by_name.update(zip(PACKED_NAMES, _unpack(res[-1], shapes)))
        outs += [by_name[k] for k in WEIGHT_NAMES]
    return (loss, grad_x, *outs)


def kernel(x, p, ffn1_wg, ffn1_wu, ffn1_wd, ffn2_wg, ffn2_wu, ffn2_wd, ln_g, ln_b, ple_wg, ple_bg, ple_wp, ab_w_in, a_sinks, b_conv_w, b_conv_b, b_wa, b_ba, b_wx, b_bx, b_lam, ab_w_out, c_w_in, c_conv_w, c_a_log, c_dt_bias, c_norm_g, c_w_out, loss_target, m_ffn1_wg, m_ffn1_wu, m_ffn1_wd, m_ffn2_wg, m_ffn2_wu, m_ffn2_wd, m_ln_g, m_ln_b, m_ple_wg, m_ple_bg, m_ple_wp, m_ab_w_in, m_a_sinks, m_b_conv_w, m_b_conv_b, m_b_wa, m_b_ba, m_b_wx, m_b_bx, m_b_lam, m_ab_w_out, m_c_w_in, m_c_conv_w, m_c_a_log, m_c_dt_bias, m_c_norm_g, m_c_w_out, v_ffn1_wg, v_ffn1_wu, v_ffn1_wd, v_ffn2_wg, v_ffn2_wu, v_ffn2_wd, v_ln_g, v_ln_b, v_ple_wg, v_ple_bg, v_ple_wp, v_ab_w_in, v_a_sinks, v_b_conv_w, v_b_conv_b, v_b_wa, v_b_ba, v_b_wx, v_b_bx, v_b_lam, v_ab_w_out, v_c_w_in, v_c_conv_w, v_c_a_log, v_c_dt_bias, v_c_norm_g, v_c_w_out):
    weights = [ffn1_wg, ffn1_wu, ffn1_wd, ffn2_wg, ffn2_wu, ffn2_wd, ln_g, ln_b, ple_wg, ple_bg, ple_wp, ab_w_in, a_sinks,
               b_conv_w, b_conv_b, b_wa, b_ba, b_wx, b_bx, b_lam, ab_w_out, c_w_in, c_conv_w, c_a_log, c_dt_bias, c_norm_g,
               c_w_out]
    m = [m_ffn1_wg, m_ffn1_wu, m_ffn1_wd, m_ffn2_wg, m_ffn2_wu, m_ffn2_wd, m_ln_g, m_ln_b, m_ple_wg, m_ple_bg, m_ple_wp,
         m_ab_w_in, m_a_sinks, m_b_conv_w, m_b_conv_b, m_b_wa, m_b_ba, m_b_wx, m_b_bx, m_b_lam, m_ab_w_out, m_c_w_in,
         m_c_conv_w, m_c_a_log, m_c_dt_bias, m_c_norm_g, m_c_w_out]
    v = [v_ffn1_wg, v_ffn1_wu, v_ffn1_wd, v_ffn2_wg, v_ffn2_wu, v_ffn2_wd, v_ln_g, v_ln_b, v_ple_wg, v_ple_bg, v_ple_wp,
         v_ab_w_in, v_a_sinks, v_b_conv_w, v_b_conv_b, v_b_wa, v_b_ba, v_b_wx, v_b_bx, v_b_lam, v_ab_w_out, v_c_w_in,
         v_c_conv_w, v_c_a_log, v_c_dt_bias, v_c_norm_g, v_c_w_out]
    return _train_step(x, p, loss_target, dict(zip(WEIGHT_NAMES, weights)), dict(zip(WEIGHT_NAMES, m)),
                       dict(zip(WEIGHT_NAMES, v)))
```

```python
import functools

import jax
import jax.numpy as jnp
from jax import lax
from jax.experimental import pallas as pl
from jax.experimental.pallas import tpu as pltpu
from jax.experimental.pallas import tpu_sc as plsc

f32 = jnp.float32
bf16 = jnp.bfloat16

DEPTH = 2
CHUNK = 64
A_HEADS, A_KV_HEADS, A_GROUP, A_HEAD_DIM = 8, 2, 4, 64
A_WIDTH, A_KV_WIDTH, A_WINDOW = 512, 128, 128
B_WIDTH, B_BLOCKS, B_BLOCK, B_CONV = 512, 8, 64, 4
RG_C = 8.0
C_HEADS, C_HEAD_DIM, C_WIDTH, C_CONV = 8, 128, 1024, 4
DN_ALPHA = (2.0 * DEPTH) ** 0.25
LN_EPS = 1e-5
NORM_EPS = 1e-6
NEG = -1e30
ADAM_LR, ADAM_B1, ADAM_B2, ADAM_EPS, ADAM_WD, ADAM_STEP = 0.001, 0.9, 0.999, 1e-08, 0.01, 10

VMEM_LIMIT_BYTES = 56 * 1024 * 1024
LANES = 128
SUBLANES = 8
GROUP_W = 128
PREP_FWD_UNROLL = 8
PREP_BWD_UNROLL = 8
C_HEADS_PER_STEP = 4
GDN_TIME_BLOCK = 512

NN = ((1,), (0,))
NT = ((1,), (1,))
TN = ((0,), (0,))


def _params(sem):
    return pltpu.CompilerParams(dimension_semantics=sem, vmem_limit_bytes=VMEM_LIMIT_BYTES)


def _tile(n, cap, mult):
    best = None
    t = mult
    while t <= min(n, cap):
        if n % t == 0:
            best = t
        t += mult
    return best if best is not None else n


def _bdot(a, b, dims):
    return lax.dot_general(a.astype(bf16), b.astype(bf16), (dims, ((), ())), preferred_element_type=f32)


def _running_sum(x, reverse):
    s = x.shape[0]
    t = lax.broadcasted_iota(jnp.int32, x.shape, 0)
    d = 1
    while d < s:
        if reverse:
            x = x + jnp.where(t < s - d, pltpu.roll(x, s - d, 0), 0.0)
        else:
            x = x + jnp.where(t >= d, pltpu.roll(x, d, 0), 0.0)
        d *= 2
    return x


@jax.custom_vjp
def _cumsum0(x):
    return _running_sum(x, False)


def _cumsum0_fwd(x):
    return _running_sum(x, False), None


def _cumsum0_bwd(_, g):
    return (_running_sum(g, True),)


_cumsum0.defvjp(_cumsum0_fwd, _cumsum0_bwd)


@jax.custom_vjp
def _bnn(a, b):
    return _bdot(a, b, NN)


def _bnn_fwd(a, b):
    return _bdot(a, b, NN), (a, b)


def _bnn_bwd(res, g):
    a, b = res
    return _bdot(g, b, NT), _bdot(a, g, TN)


_bnn.defvjp(_bnn_fwd, _bnn_bwd)


@jax.custom_vjp
def _bnt(a, b):
    return _bdot(a, b, NT)


def _bnt_fwd(a, b):
    return _bdot(a, b, NT), (a, b)


def _bnt_bwd(res, g):
    a, b = res
    return _bdot(g, b, NN), _bdot(g, a, TN)


_bnt.defvjp(_bnt_fwd, _bnt_bwd)


@jax.custom_vjp
def _btn(a, b):
    return _bdot(a, b, TN)


def _btn_fwd(a, b):
    return _bdot(a, b, TN), (a, b)


def _btn_bwd(res, g):
    a, b = res
    return _bdot(b, g, NT), _bdot(a, g, NN)


_btn.defvjp(_btn_fwd, _btn_bwd)

RAW_DOTS = (lambda a, b: _bdot(a, b, NN), lambda a, b: _bdot(a, b, NT), lambda a, b: _bdot(a, b, TN),
            lambda x: _running_sum(x, False))
VJP_DOTS = (_bnn, _bnt, _btn, _cumsum0)


def _layer_norm(z, g, b):
    mu = jnp.mean(z, -1, keepdims=True)
    d = z - mu
    var = jnp.mean(d * d, -1, keepdims=True)
    return d * lax.rsqrt(var + LN_EPS) * g + b


def _silu(x):
    return x * jax.nn.sigmoid(x)


def mm_nn(a, w, add=None, add_scale=1.0, *, name):
    m, k = a.shape
    n = w.shape[1]
    tm = _tile(m, 512, SUBLANES)
    tn = _tile(n, 1024, LANES)

    def body(*refs):
        if add is None:
            a_ref, w_ref, o_ref = refs
            o_ref[...] = _bdot(a_ref[...], w_ref[...], NN)
        else:
            a_ref, w_ref, add_ref, o_ref = refs
            o_ref[...] = _bdot(a_ref[...], w_ref[...], NN) + add_scale * add_ref[...]

    in_specs = [pl.BlockSpec((tm, k), lambda i, j: (i, 0)), pl.BlockSpec((k, tn), lambda i, j: (0, j))]
    args = [a, w]
    if add is not None:
        in_specs.append(pl.BlockSpec((tm, tn), lambda i, j: (i, j)))
        args.append(add)
    return pl.pallas_call(
        body, grid=(m // tm, n // tn), in_specs=in_specs,
        out_specs=pl.BlockSpec((tm, tn), lambda i, j: (i, j)),
        out_shape=jax.ShapeDtypeStruct((m, n), f32),
        compiler_params=_params(("parallel", "parallel")), name=name,
    )(*args)


def mm_tn(a, b, *, name):
    m, k = a.shape
    n = b.shape[1]
    tm = _tile(m, 1024, 2 * SUBLANES)
    tn = _tile(n, 1024, LANES)

    def body(a_ref, b_ref, o_ref):
        part = _bdot(a_ref[...], b_ref[...], TN)

        @pl.when(pl.program_id(1) == 0)
        def _():
            o_ref[...] = part

        @pl.when(pl.program_id(1) > 0)
        def _():
            o_ref[...] += part

    return pl.pallas_call(
        body, grid=(n // tn, m // tm),
        in_specs=[pl.BlockSpec((tm, k), lambda j, i: (i, 0)), pl.BlockSpec((tm, tn), lambda j, i: (i, j))],
        out_specs=pl.BlockSpec((k, tn), lambda j, i: (0, j)),
        out_shape=jax.ShapeDtypeStruct((k, n), f32),
        compiler_params=_params(("parallel", "arbitrary")), name=name,
    )(a, b)


def proj_ln(a_list, w_list, xres, g, b, *, name):
    t, d = xres.shape
    tm = _tile(t, 256, SUBLANES)
    na = len(a_list)

    def body(*refs):
        a_refs, w_refs = refs[:na], refs[na:2 * na]
        x_ref, g_ref, b_ref, y_ref, z_ref = refs[2 * na:]
        z = DN_ALPHA * x_ref[...]
        for a_ref, w_ref in zip(a_refs, w_refs):
            z = z + _bdot(a_ref[...], w_ref[...], NN)
        z_ref[...] = z
        y_ref[...] = _layer_norm(z, g_ref[...], b_ref[...])

    in_specs = [pl.BlockSpec((tm, a.shape[1]), lambda i: (i, 0)) for a in a_list]
    in_specs += [pl.BlockSpec(w.shape, lambda i: (0, 0)) for w in w_list]
    in_specs += [pl.BlockSpec((tm, d), lambda i: (i, 0)), pl.BlockSpec((1, d), lambda i: (0, 0)),
                 pl.BlockSpec((1, d), lambda i: (0, 0))]
    return pl.pallas_call(
        body, grid=(t // tm,), in_specs=in_specs,
        out_specs=[pl.BlockSpec((tm, d), lambda i: (i, 0))] * 2,
        out_shape=[jax.ShapeDtypeStruct((t, d), f32)] * 2,
        compiler_params=_params(("parallel",)), name=name,
    )(*a_list, *w_list, xres, g, b)


def ln_bwd(z, dy, g, *, name):
    t, d = z.shape
    tm = _tile(t, 512, SUBLANES)

    def body(z_ref, dy_ref, g_ref, dz_ref, dzb_ref, dg_ref, db_ref):
        zz = z_ref[...]
        dy_ = dy_ref[...]
        mu = jnp.mean(zz, -1, keepdims=True)
        dd = zz - mu
        var = jnp.mean(dd * dd, -1, keepdims=True)
        rstd = lax.rsqrt(var + LN_EPS)
        xhat = dd * rstd
        dxh = dy_ * g_ref[...]
        dz = rstd * (dxh - jnp.mean(dxh, -1, keepdims=True) - xhat * jnp.mean(dxh * xhat, -1, keepdims=True))
        dz_ref[...] = dz
        dzb_ref[...] = dz.astype(bf16)
        pg = jnp.sum(dy_ * xhat, 0, keepdims=True)
        pb = jnp.sum(dy_, 0, keepdims=True)

        @pl.when(pl.program_id(0) == 0)
        def _():
            dg_ref[...] = pg
            db_ref[...] = pb

        @pl.when(pl.program_id(0) > 0)
        def _():
            dg_ref[...] += pg
            db_ref[...] += pb

    row = pl.BlockSpec((tm, d), lambda i: (i, 0))
    vec = pl.BlockSpec((1, d), lambda i: (0, 0))
    return pl.pallas_call(
        body, grid=(t // tm,), in_specs=[row, row, vec], out_specs=[row, row, vec, vec],
        out_shape=[jax.ShapeDtypeStruct((t, d), f32), jax.ShapeDtypeStruct((t, d), bf16),
                   jax.ShapeDtypeStruct((1, d), f32), jax.ShapeDtypeStruct((1, d), f32)],
        compiler_params=_params(("arbitrary",)), name=name,
    )(z, dy, g)


def loss_head(y, target, *, name):
    t, d = y.shape
    tm = _tile(t, 512, SUBLANES)

    def body(y_ref, t_ref, dy_ref, sq_ref):
        e = y_ref[...] - t_ref[...]
        dy_ref[...] = e * (1.0 / d)
        part = jnp.sum(e * e, 0, keepdims=True)

        @pl.when(pl.program_id(0) == 0)
        def _():
            sq_ref[...] = part

        @pl.when(pl.program_id(0) > 0)
        def _():
            sq_ref[...] += part

    row = pl.BlockSpec((tm, d), lambda i: (i, 0))
    vec = pl.BlockSpec((1, d), lambda i: (0, 0))
    return pl.pallas_call(
        body, grid=(t // tm,), in_specs=[row, row], out_specs=[row, vec],
        out_shape=[jax.ShapeDtypeStruct((t, d), f32), jax.ShapeDtypeStruct((1, d), f32)],
        compiler_params=_params(("arbitrary",)), name=name,
    )(y, target)


FFN_COL_BLOCK = 256
FFN_ROWS = 1024


def _lane_blocks(n):
    return [slice(s, min(s + FFN_COL_BLOCK, n)) for s in range(0, n, FFN_COL_BLOCK)]


def ffn_fwd(x, wg, wu, wd, g, b, *, name):
    t, d = x.shape
    nf, _, tf = wg.shape
    tm = _tile(t, FFN_ROWS, SUBLANES)

    def body(x_ref, wg_ref, wu_ref, wd_ref, g_ref, b_ref, y_ref, z_ref, yb_ref, acc_ref):
        f = pl.program_id(1)
        xb = x_ref[...].astype(bf16)
        part, pending = None, None
        for cols in _lane_blocks(tf):
            gate_up = (_bdot(xb, wg_ref[:, cols], NN), _bdot(xb, wu_ref[:, cols], NN), cols)
            if pending is not None:
                down = _bdot(_silu(pending[0]) * pending[1], wd_ref[pending[2], :], NN)
                part = down if part is None else part + down
            pending = gate_up
        down = _bdot(_silu(pending[0]) * pending[1], wd_ref[pending[2], :], NN)
        part = down if part is None else part + down

        @pl.when(f == 0)
        def _():
            acc_ref[...] = part

        @pl.when(f > 0)
        def _():
            acc_ref[...] += part

        @pl.when(f == nf - 1)
        def _():
            z = DN_ALPHA * x_ref[...] + 0.5 * acc_ref[...]
            z_ref[...] = z
            y = _layer_norm(z, g_ref[...], b_ref[...])
            y_ref[...] = y
            yb_ref[...] = y.astype(bf16)

    row = pl.BlockSpec((tm, d), lambda i, j: (i, 0))
    vec = pl.BlockSpec((1, d), lambda i, j: (0, 0))
    wcol = pl.BlockSpec((None, d, tf), lambda i, j: (j, 0, 0))
    wrow = pl.BlockSpec((None, tf, d), lambda i, j: (j, 0, 0))
    return pl.pallas_call(
        body, grid=(t // tm, nf),
        in_specs=[row, wcol, wcol, wrow, vec, vec],
        out_specs=[row, row, row],
        out_shape=[jax.ShapeDtypeStruct((t, d), f32)] * 2 + [jax.ShapeDtypeStruct((t, d), bf16)],
        scratch_shapes=[pltpu.VMEM((tm, d), f32)],
        compiler_params=_params(("parallel", "arbitrary")), name=name,
    )(x, wg, wu, wd, g, b)


def ffn_bwd_weights(xb, dzb, wg, wu, wd, *, name):
    t, d = xb.shape
    nf, _, tf = wg.shape
    tm = _tile(t, FFN_ROWS, SUBLANES)
    nt = t // tm

    def body(x_ref, dz_ref, wg_ref, wu_ref, wd_ref, dgate_ref, dup_ref, owg_ref, owu_ref, owd_ref,
             dwg_ref, dwu_ref, dwd_ref):
        x = x_ref[...]
        dzh = dz_ref[...] * 0.5

        def first_half(cols):
            return _bdot(x, wg_ref[:, cols], NN), _bdot(x, wu_ref[:, cols], NN), _bdot(dzh, wd_ref[cols, :], NT), cols

        def second_half(gate, up, dh, cols):
            sg = jax.nn.sigmoid(gate)
            s = gate * sg
            dup = (dh * s).astype(bf16)
            dgate = (dh * up * (sg * (1.0 + gate * (1.0 - sg)))).astype(bf16)
            dgate_ref[:, cols] = dgate
            dup_ref[:, cols] = dup
            return _bdot(x, dgate, TN), _bdot(x, dup, TN), _bdot(s * up, dzh, TN), cols

        parts, pending = [], None
        for cols in _lane_blocks(tf):
            nxt = first_half(cols)
            if pending is not None:
                parts.append(second_half(*pending))
            pending = nxt
        parts.append(second_half(*pending))

        @pl.when(pl.program_id(1) == 0)
        def _():
            for pwg, pwu, pwd, cols in parts:
                dwg_ref[:, cols] = pwg
                dwu_ref[:, cols] = pwu
                dwd_ref[cols, :] = pwd

        @pl.when(pl.program_id(1) > 0)
        def _():
            for pwg, pwu, pwd, cols in parts:
                dwg_ref[:, cols] += pwg
                dwu_ref[:, cols] += pwu
                dwd_ref[cols, :] += pwd

        @pl.when(pl.program_id(1) == nt - 1)
        def _():
            owg_ref[...] = dwg_ref[...].astype(bf16)
            owu_ref[...] = dwu_ref[...].astype(bf16)
            owd_ref[...] = dwd_ref[...].astype(bf16)

    row = pl.BlockSpec((tm, d), lambda j, i: (i, 0))
    wcol = pl.BlockSpec((None, d, tf), lambda j, i: (j, 0, 0))
    wrow = pl.BlockSpec((None, tf, d), lambda j, i: (j, 0, 0))
    act = pl.BlockSpec((None, tm, tf), lambda j, i: (j, i, 0))
    return pl.pallas_call(
        body, grid=(nf, nt), in_specs=[row, row, wcol, wcol, wrow], out_specs=[act, act, wcol, wcol, wrow],
        out_shape=[jax.ShapeDtypeStruct((nf, t, tf), bf16), jax.ShapeDtypeStruct((nf, t, tf), bf16),
                   jax.ShapeDtypeStruct((nf, d, tf), bf16), jax.ShapeDtypeStruct((nf, d, tf), bf16),
                   jax.ShapeDtypeStruct((nf, tf, d), bf16)],
        scratch_shapes=[pltpu.VMEM((d, tf), f32), pltpu.VMEM((d, tf), f32), pltpu.VMEM((tf, d), f32)],
        compiler_params=_params(("parallel", "arbitrary")), name=name,
    )(xb, dzb, wg, wu, wd)


def ffn_bwd_input(dgate, dup, wg, wu, dz, *, name):
    nf, t, tf = dgate.shape
    d = wg.shape[1]
    tm = _tile(t, FFN_ROWS // 2, SUBLANES)

    def body(dg_ref, du_ref, wg_ref, wu_ref, dz_ref, dx_ref):
        acc = DN_ALPHA * dz_ref[...]
        for j in range(nf):
            acc = acc + _bdot(dg_ref[j], wg_ref[j], NT) + _bdot(du_ref[j], wu_ref[j], NT)
        dx_ref[...] = acc

    act = pl.BlockSpec((nf, tm, tf), lambda i: (0, i, 0))
    wsp = pl.BlockSpec((nf, d, tf), lambda i: (0, 0, 0))
    row = pl.BlockSpec((tm, d), lambda i: (i, 0))
    return pl.pallas_call(
        body, grid=(t // tm,), in_specs=[act, act, wsp, wsp, row], out_specs=row,
        out_shape=jax.ShapeDtypeStruct((t, d), f32),
        compiler_params=_params(("parallel",)), name=name,
    )(dgate, dup, wg, wu, dz)


def ple_fwd(x, p, wg, bg, wp, *, name):
    t, d = x.shape
    dp = p.shape[1]
    tm = _tile(t, 512, SUBLANES)

    def body(x_ref, p_ref, wg_ref, bg_ref, wp_ref, o_ref):
        x_ = x_ref[...]
        gate = jax.nn.sigmoid(_bdot(x_, wg_ref[...], NN) + bg_ref[...])
        o_ref[...] = x_ + gate * _bdot(p_ref[...], wp_ref[...], NN)

    row = pl.BlockSpec((tm, d), lambda i: (i, 0))
    return pl.pallas_call(
        body, grid=(t // tm,),
        in_specs=[row, pl.BlockSpec((tm, dp), lambda i: (i, 0)), pl.BlockSpec((d, d), lambda i: (0, 0)),
                  pl.BlockSpec((1, d), lambda i: (0, 0)), pl.BlockSpec((dp, d), lambda i: (0, 0))],
        out_specs=row, out_shape=jax.ShapeDtypeStruct((t, d), f32),
        compiler_params=_params(("parallel",)), name=name,
    )(x, p, wg, bg, wp)


def ple_bwd(x, p, dy, wg, wgt, bg, wp, *, name):
    t, d = x.shape
    dp = p.shape[1]
    tm = _tile(t, 512, SUBLANES)

    def body(x_ref, p_ref, dy_ref, wg_ref, wgt_ref, bg_ref, wp_ref, dx_ref, dwg_ref, dbg_ref, dwp_ref):
        x_ = x_ref[...]
        dy_ = dy_ref[...]
        s = jax.nn.sigmoid(_bdot(x_, wg_ref[...], NN) + bg_ref[...])
        e = _bdot(p_ref[...], wp_ref[...], NN)
        da = dy_ * e * s * (1.0 - s)
        de = dy_ * s
        dx_ref[...] = dy_ + _bdot(da, wgt_ref[...], NN)
        pwg = _bdot(x_, da, TN)
        pbg = jnp.sum(da, 0, keepdims=True)
        pwp = _bdot(p_ref[...], de, TN)

        @pl.when(pl.program_id(0) == 0)
        def _():
            dwg_ref[...] = pwg
            dbg_ref[...] = pbg
            dwp_ref[...] = pwp

        @pl.when(pl.program_id(0) > 0)
        def _():
            dwg_ref[...] += pwg
            dbg_ref[...] += pbg
            dwp_ref[...] += pwp

    row = pl.BlockSpec((tm, d), lambda i: (i, 0))
    full = lambda shape: pl.BlockSpec(shape, lambda i: (0, 0))
    return pl.pallas_call(
        body, grid=(t // tm,),
        in_specs=[row, pl.BlockSpec((tm, dp), lambda i: (i, 0)), row, full((d, d)), full((d, d)), full((1, d)),
                  full((dp, d))],
        out_specs=[row, full((d, d)), full((1, d)), full((dp, d))],
        out_shape=[jax.ShapeDtypeStruct((t, d), f32), jax.ShapeDtypeStruct((d, d), f32),
                   jax.ShapeDtypeStruct((1, d), f32), jax.ShapeDtypeStruct((dp, d), f32)],
        compiler_params=_params(("arbitrary",)), name=name,
    )(x, p, dy, wg, wgt, bg, wp)


def _conv_taps(xpad_ref, w_ref, s):
    acc = w_ref[0:1, :] * xpad_ref[SUBLANES - 3:SUBLANES - 3 + s, :]
    for j in range(1, 4):
        acc = acc + w_ref[j:j + 1, :] * xpad_ref[SUBLANES - 3 + j:SUBLANES - 3 + j + s, :]
    return acc


def conv_fwd(x, w, bias, act, nb, *, name):
    t, c = x.shape
    s = t // nb
    cw = GROUP_W

    def body(x_ref, w_ref, b_ref, y_ref, xpad):
        xpad[0:SUBLANES, :] = jnp.zeros((SUBLANES, cw), f32)
        xpad[SUBLANES:, :] = x_ref[...]
        acc = _conv_taps(xpad, w_ref, s) + b_ref[...]
        y_ref[...] = _silu(acc) if act else acc

    slab = pl.BlockSpec((s, cw), lambda b, g: (b, g))
    return pl.pallas_call(
        body, grid=(nb, c // cw),
        in_specs=[slab, pl.BlockSpec((4, cw), lambda b, g: (0, g)), pl.BlockSpec((1, cw), lambda b, g: (0, g))],
        out_specs=slab, out_shape=jax.ShapeDtypeStruct((t, c), f32),
        scratch_shapes=[pltpu.VMEM((s + SUBLANES, cw), f32)],
        compiler_params=_params(("parallel", "parallel")), name=name,
    )(x, w, bias)


def conv_bwd(x, w, bias, dy, act, nb, *, name):
    t, c = x.shape
    s = t // nb
    cw = GROUP_W

    def body(x_ref, w_ref, b_ref, dy_ref, dx_ref, dw_ref, db_ref, xpad, dpad):
        xpad[0:SUBLANES, :] = jnp.zeros((SUBLANES, cw), f32)
        xpad[SUBLANES:, :] = x_ref[...]
        dacc = dy_ref[...]
        if act:
            acc = _conv_taps(xpad, w_ref, s) + b_ref[...]
            sg = jax.nn.sigmoid(acc)
            dacc = dacc * (sg * (1.0 + acc * (1.0 - sg)))
        dpad[0:s, :] = dacc
        dpad[s:, :] = jnp.zeros((SUBLANES, cw), f32)
        dx = w_ref[0:1, :] * dpad[3:3 + s, :]
        for j in range(1, 4):
            dx = dx + w_ref[j:j + 1, :] * dpad[3 - j:3 - j + s, :]
        dx_ref[...] = dx
        first = pl.program_id(1) == 0
        for j in range(4):
            pw = jnp.sum(dacc * xpad[SUBLANES - 3 + j:SUBLANES - 3 + j + s, :], 0, keepdims=True)

            @pl.when(first)
            def _():
                dw_ref[j:j + 1, :] = pw

            @pl.when(jnp.logical_not(first))
            def _():
                dw_ref[j:j + 1, :] += pw

        pb = jnp.sum(dacc, 0, keepdims=True)

        @pl.when(first)
        def _():
            db_ref[...] = pb

        @pl.when(jnp.logical_not(first))
        def _():
            db_ref[...] += pb

    slab = pl.BlockSpec((s, cw), lambda g, b: (b, g))
    wsp = pl.BlockSpec((4, cw), lambda g, b: (0, g))
    bsp = pl.BlockSpec((1, cw), lambda g, b: (0, g))
    return pl.pallas_call(
        body, grid=(c // cw, nb), in_specs=[slab, wsp, bsp, slab], out_specs=[slab, wsp, bsp],
        out_shape=[jax.ShapeDtypeStruct((t, c), f32), jax.ShapeDtypeStruct((4, c), f32),
                   jax.ShapeDtypeStruct((1, c), f32)],
        scratch_shapes=[pltpu.VMEM((s + SUBLANES, cw), f32), pltpu.VMEM((s + SUBLANES, cw), f32)],
        compiler_params=_params(("parallel", "arbitrary")), name=name,
    )(x, w, bias, dy)


def _each(f, *lists):
    return [f(*a) for a in zip(*lists)]


def _attn_heads(qs, kbs, vbs, sinks, valid, dist, dots):
    nn, nt = dots[:2]
    kv = [h // A_GROUP for h in range(A_HEADS)]
    scs = [nt(qs[h], kbs[kv[h]]) for h in range(A_HEADS)]
    prs = []
    for h in range(A_HEADS):
        sc = scs[h] * (A_HEAD_DIM ** -0.5) - 2.0 ** -(h + 1) * dist
        sc = jnp.where(valid, sc, NEG)
        m = lax.stop_gradient(jnp.maximum(jnp.max(sc, -1, keepdims=True), sinks[h]))
        pr = jnp.exp(sc - m)
        den = jnp.sum(pr, -1, keepdims=True) + jnp.exp(sinks[h] - m)
        prs.append(pr / den)
    return [nn(prs[h], vbs[kv[h]]) for h in range(A_HEADS)]


A_Q_ROWS = 2 * CHUNK


def _attn_band_consts(r0):
    band = A_WINDOW + A_Q_ROWS
    qi = lax.broadcasted_iota(jnp.int32, (A_Q_ROWS, band), 0)
    kj = lax.broadcasted_iota(jnp.int32, (A_Q_ROWS, band), 1)
    dist = jnp.abs(qi + A_WINDOW - kj).astype(f32)
    qc, kc = qi // CHUNK, kj // CHUNK
    valid = ((kj + r0) >= A_WINDOW) & (kc >= qc) & (kc <= qc + A_WINDOW // CHUNK)
    return dist, valid


def attn_fwd(qkv, sinks, nb, *, name):
    t = qkv.shape[0]
    s = t // nb
    band = A_WINDOW + A_Q_ROWS
    hd = A_HEAD_DIM

    def body(qkv_ref, sink_ref, o_ref, kvpad):
        kvpad[0:A_WINDOW, :] = jnp.zeros((A_WINDOW, 2 * A_KV_WIDTH), f32)
        kvpad[A_WINDOW:, :] = qkv_ref[:, A_WIDTH:]

        def chunk(n, carry):
            r0 = pl.multiple_of(n * A_Q_ROWS, A_Q_ROWS)
            dist, valid = _attn_band_consts(r0)
            kbs = [kvpad[pl.ds(r0, band), kvh * hd:(kvh + 1) * hd] for kvh in range(A_KV_HEADS)]
            vbs = [kvpad[pl.ds(r0, band), A_KV_WIDTH + kvh * hd:A_KV_WIDTH + (kvh + 1) * hd]
                   for kvh in range(A_KV_HEADS)]
            qs = [qkv_ref[pl.ds(r0, A_Q_ROWS), h * hd:(h + 1) * hd] for h in range(A_HEADS)]
            outs = _attn_heads(qs, kbs, vbs, [sink_ref[:, h:h + 1] for h in range(A_HEADS)], valid, dist, RAW_DOTS)
            for h in range(A_HEADS):
                o_ref[pl.ds(r0, A_Q_ROWS), h * hd:(h + 1) * hd] = outs[h]
            return carry

        lax.fori_loop(0, s // A_Q_ROWS, chunk, 0)

    return pl.pallas_call(
        body, grid=(nb,),
        in_specs=[pl.BlockSpec((s, A_WIDTH + 2 * A_KV_WIDTH), lambda b: (b, 0)),
                  pl.BlockSpec((1, A_HEADS), lambda b: (0, 0))],
        out_specs=pl.BlockSpec((s, A_WIDTH), lambda b: (b, 0)),
        out_shape=jax.ShapeDtypeStruct((t, A_WIDTH), f32),
        scratch_shapes=[pltpu.VMEM((s + A_WINDOW, 2 * A_KV_WIDTH), f32)],
        compiler_params=_params(("parallel",)), name=name,
    )(qkv, sinks)


def attn_bwd(qkv, sinks, do, nb, *, name):
    t = qkv.shape[0]
    s = t // nb
    band = A_WINDOW + A_Q_ROWS
    hd = A_HEAD_DIM
    kvw = 2 * A_KV_WIDTH

    def body(qkv_ref, sink_ref, do_ref, dqkv_ref, dsink_ref, kvpad, dkvpad):
        kvpad[0:A_WINDOW, :] = jnp.zeros((A_WINDOW, kvw), f32)
        kvpad[A_WINDOW:, :] = qkv_ref[:, A_WIDTH:]
        dkvpad[...] = jnp.zeros((s + A_WINDOW, kvw), f32)

        def chunk(n, dsinks):
            r0 = pl.multiple_of(n * A_Q_ROWS, A_Q_ROWS)
            dist, valid = _attn_band_consts(r0)
            ksl = [slice(kvh * hd, (kvh + 1) * hd) for kvh in range(A_KV_HEADS)]
            vsl = [slice(A_KV_WIDTH + kvh * hd, A_KV_WIDTH + (kvh + 1) * hd) for kvh in range(A_KV_HEADS)]
            kbs = [kvpad[pl.ds(r0, band), sl] for sl in ksl]
            vbs = [kvpad[pl.ds(r0, band), sl] for sl in vsl]
            dkbs = [dkvpad[pl.ds(r0, band), sl] for sl in ksl]
            dvbs = [dkvpad[pl.ds(r0, band), sl] for sl in vsl]
            qs = [qkv_ref[pl.ds(r0, A_Q_ROWS), h * hd:(h + 1) * hd] for h in range(A_HEADS)]
            dos = [do_ref[pl.ds(r0, A_Q_ROWS), h * hd:(h + 1) * hd] for h in range(A_HEADS)]
            fn = functools.partial(_attn_heads, valid=valid, dist=dist, dots=VJP_DOTS)
            _, vjp = jax.vjp(fn, qs, kbs, vbs, [sink_ref[:, h:h + 1] for h in range(A_HEADS)])
            dqs, dks, dvs, dss = vjp(dos)
            for h in range(A_HEADS):
                dqkv_ref[pl.ds(r0, A_Q_ROWS), h * hd:(h + 1) * hd] = dqs[h]
            for kvh in range(A_KV_HEADS):
                dkvpad[pl.ds(r0, band), ksl[kvh]] = dkbs[kvh] + dks[kvh]
                dkvpad[pl.ds(r0, band), vsl[kvh]] = dvbs[kvh] + dvs[kvh]
            return tuple(dsinks[h] + dss[h] for h in range(A_HEADS))

        dsinks = lax.fori_loop(0, s // A_Q_ROWS, chunk, tuple(jnp.zeros((1, 1), f32) for _ in range(A_HEADS)))
        dqkv_ref[:, A_WIDTH:] = dkvpad[A_WINDOW:, :]
        first = pl.program_id(0) == 0
        for h in range(A_HEADS):
            @pl.when(first)
            def _():
                dsink_ref[:, h:h + 1] = dsinks[h]

            @pl.when(jnp.logical_not(first))
            def _():
                dsink_ref[:, h:h + 1] += dsinks[h]

    wq = A_WIDTH + kvw
    return pl.pallas_call(
        body, grid=(nb,),
        in_specs=[pl.BlockSpec((s, wq), lambda b: (b, 0)), pl.BlockSpec((1, A_HEADS), lambda b: (0, 0)),
                  pl.BlockSpec((s, A_WIDTH), lambda b: (b, 0))],
        out_specs=[pl.BlockSpec((s, wq), lambda b: (b, 0)), pl.BlockSpec((1, A_HEADS), lambda b: (0, 0))],
        out_shape=[jax.ShapeDtypeStruct((t, wq), f32), jax.ShapeDtypeStruct((1, A_HEADS), f32)],
        scratch_shapes=[pltpu.VMEM((s + A_WINDOW, kvw), f32), pltpu.VMEM((s + A_WINDOW, kvw), f32)],
        compiler_params=_params(("arbitrary",)), name=name,
    )(qkv, sinks, do)


def _rg_gates(xc, wa, wx, ba, bx, lam, nn):
    r = jax.nn.sigmoid(nn(xc, wa) + ba)
    i = jax.nn.sigmoid(nn(xc, wx) + bx)
    log_a = -RG_C * r * jax.nn.softplus(-lam)
    a = jnp.exp(log_a)
    mult = jnp.sqrt(-jnp.tanh(log_a) * (jnp.exp(2.0 * log_a) + 1.0))
    return a, mult * (i * xc)


def _linear_scan(a, u, reverse):
    s = a.shape[0]
    t = lax.broadcasted_iota(jnp.int32, a.shape, 0)
    d = 1
    while d < s:
        if reverse:
            keep = t < s - d
            shift = s - d
        else:
            keep = t >= d
            shift = d
        us = jnp.where(keep, pltpu.roll(u, shift, 0), 0.0)
        as_ = jnp.where(keep, pltpu.roll(a, shift, 0), 1.0)
        u = u + a * us
        a = a * as_
        d *= 2
    return u


def rglru_fwd(xc, bg, wa, wx, ba, bx, lam, nb, *, name):
    t, c = xc.shape
    s = t // nb
    cw = GROUP_W

    def body(xc_ref, bg_ref, wa_ref, wx_ref, ba_ref, bx_ref, lam_ref, y_ref, h_ref):
        a, u = _rg_gates(xc_ref[...], wa_ref[...], wx_ref[...], ba_ref[...], bx_ref[...], lam_ref[...], RAW_DOTS[0])
        h = _linear_scan(a, u, False)
        h_ref[...] = h
        y_ref[...] = h * jax.nn.gelu(bg_ref[...])

    slab = pl.BlockSpec((s, cw), lambda b, g: (b, g))
    wsp = pl.BlockSpec((None, cw, cw), lambda b, g: (g, 0, 0))
    vec = pl.BlockSpec((1, cw), lambda b, g: (0, g))
    return pl.pallas_call(
        body, grid=(nb, c // cw), in_specs=[slab, slab, wsp, wsp, vec, vec, vec], out_specs=[slab, slab],
        out_shape=[jax.ShapeDtypeStruct((t, c), f32)] * 2,
        compiler_params=_params(("parallel", "parallel")), name=name,
    )(xc, bg, wa, wx, ba, bx, lam)


def rglru_bwd(xc, bg, h, dy, wa, wx, ba, bx, lam, nb, *, name):
    t, c = xc.shape
    s = t // nb
    cw = GROUP_W

    def body(xc_ref, bg_ref, h_ref, dy_ref, wa_ref, wx_ref, ba_ref, bx_ref, lam_ref,
             dxc_ref, dbg_ref, dwa_ref, dwx_ref, dba_ref, dbx_ref, dlam_ref):
        h = h_ref[...]
        dy_ = dy_ref[...]
        gel, gel_vjp = jax.vjp(jax.nn.gelu, bg_ref[...])
        dbg_ref[...] = gel_vjp(dy_ * h)[0]
        dh = dy_ * gel
        gates = functools.partial(_rg_gates, nn=_bnn)
        (a, _), gates_vjp = jax.vjp(gates, xc_ref[...], wa_ref[...], wx_ref[...], ba_ref[...], bx_ref[...],
                                    lam_ref[...])
        ti = lax.broadcasted_iota(jnp.int32, a.shape, 0)
        a_next = jnp.where(ti < s - 1, pltpu.roll(a, s - 1, 0), 0.0)
        lam_t = _linear_scan(a_next, dh, True)
        h_prev = jnp.where(ti >= 1, pltpu.roll(h, 1, 0), 0.0)
        dxc, dwa, dwx, dba, dbx, dlam = gates_vjp((lam_t * h_prev, lam_t))
        dxc_ref[...] = dxc
        first = pl.program_id(1) == 0

        @pl.when(first)
        def _():
            dwa_ref[...] = dwa
            dwx_ref[...] = dwx
            dba_ref[...] = dba
            dbx_ref[...] = dbx
            dlam_ref[...] = dlam

        @pl.when(jnp.logical_not(first))
        def _():
            dwa_ref[...] += dwa
            dwx_ref[...] += dwx
            dba_ref[...] += dba
            dbx_ref[...] += dbx
            dlam_ref[...] += dlam

    slab = pl.BlockSpec((s, cw), lambda g, b: (b, g))
    wsp = pl.BlockSpec((None, cw, cw), lambda g, b: (g, 0, 0))
    vec = pl.BlockSpec((1, cw), lambda g, b: (0, g))
    ng = c // cw
    return pl.pallas_call(
        body, grid=(ng, nb), in_specs=[slab, slab, slab, slab, wsp, wsp, vec, vec, vec],
        out_specs=[slab, slab, wsp, wsp, vec, vec, vec],
        out_shape=[jax.ShapeDtypeStruct((t, c), f32), jax.ShapeDtypeStruct((t, c), f32),
                   jax.ShapeDtypeStruct((ng, cw, cw), f32), jax.ShapeDtypeStruct((ng, cw, cw), f32),
                   jax.ShapeDtypeStruct((1, c), f32), jax.ShapeDtypeStruct((1, c), f32),
                   jax.ShapeDtypeStruct((1, c), f32)],
        compiler_params=_params(("parallel", "arbitrary")), name=name,
    )(xc, bg, h, dy, wa, wx, ba, bx, lam)


def _gdn_chunks_prep(qs, ks, vs, bls, als, a_log, dt_b, dots):
    nn, nt, csum = dots[0], dots[1], dots[3]
    hd = C_HEAD_DIM
    ri = lax.broadcasted_iota(jnp.int32, (CHUNK, CHUNK), 0)
    ci = lax.broadcasted_iota(jnp.int32, (CHUNK, CHUNK), 1)
    tril = ri >= ci
    strict = ri > ci
    eye = (ri == ci).astype(f32)
    qn = [q * lax.rsqrt(jnp.sum(q * q, -1, keepdims=True) + NORM_EPS) * (hd ** -0.5) for q in qs]
    kn = [k * lax.rsqrt(jnp.sum(k * k, -1, keepdims=True) + NORM_EPS) for k in ks]
    beta = [jax.nn.sigmoid(bl) for bl in bls]
    g = [-jnp.exp(a_log) * jax.nn.softplus(al + dt_b) for al in als]
    gc_sq = [csum(jnp.broadcast_to(g_, (CHUNK, CHUNK))) for g_ in g]
    gc = [csum(jnp.broadcast_to(g_, (CHUNK, hd))) for g_ in g]
    decay = [jnp.where(tril, jnp.exp(jnp.where(tril, s - s.T, 0.0)), 0.0) for s in gc_sq]
    kb = _each(jnp.multiply, kn, beta)
    kk = _each(nt, kb, kn)
    pw = [-jnp.where(strict, a * d, 0.0) for a, d in zip(kk, decay)]
    inv = [eye + p_ for p_ in pw]
    for _ in range(5):
        pw = _each(nn, pw, pw)
        inv = _each(jnp.add, inv, _each(nn, inv, pw))
    egc = [jnp.exp(c_) for c_ in gc]
    u = _each(nn, inv, _each(jnp.multiply, vs, beta))
    w = _each(nn, inv, _each(jnp.multiply, kb, egc))
    attn = _each(jnp.multiply, _each(nt, qn, kn), decay)
    g_last = [jnp.sum(jnp.broadcast_to(g_, (CHUNK, hd)), 0, keepdims=True) for g_ in g]
    qg = _each(jnp.multiply, qn, egc)
    kdec = [k_ * jnp.exp(gl_ - c_) for k_, gl_, c_ in zip(kn, g_last, gc)]
    return [(qg[i], kdec[i], w[i], u[i], attn[i], jnp.exp(g_last[i])) for i in range(len(qs))]


def _gdn_heads_step(states, qgs, kdecs, ws, us, attns, gls, zs, ng, dots):
    nn, tn = dots[0], dots[2]
    v_new = _each(jnp.subtract, us, _each(nn, ws, states))
    o = _each(jnp.add, _each(nn, qgs, states), _each(nn, attns, v_new))
    new = [s * gl for s, gl in zip(states, gls)]
    new = _each(jnp.add, new, _each(tn, kdecs, v_new))
    y = [o_ * lax.rsqrt(jnp.mean(o_ * o_, -1, keepdims=True) + NORM_EPS) * ng * _silu(z) for o_, z in zip(o, zs)]
    return y, new


def _loop_unrolled(n, unroll, load, compute, store, init):
    u = unroll if n % unroll == 0 else 1

    def trip(i, carry):
        idx = [i * u + j for j in range(u)]
        loaded = [load(k) for k in idx]
        results = compute(loaded)
        for k, r in zip(idx, results):
            carry = store(k, r, carry)
        return carry

    return lax.fori_loop(0, n // u, trip, init)


def _pick_lane(x, lane):
    li = lax.broadcasted_iota(jnp.int32, x.shape, 1)
    return jnp.sum(jnp.where(li == lane, x, 0.0), 1, keepdims=True)


def _put_lane(col, lane, width):
    li = lax.broadcasted_iota(jnp.int32, (col.shape[0], width), 1)
    return jnp.where(li == lane, col, 0.0)


def _gdn_specs(s, nc):
    hd = C_HEAD_DIM
    head = lambda off: pl.BlockSpec((s, hd), lambda b, h, off=off: (b, off + h))
    attn = pl.BlockSpec((None, s, CHUNK), lambda b, h: (h, b, 0))
    gl = pl.BlockSpec((None, nc * SUBLANES, hd), lambda b, h: (h, b, 0))
    ba = pl.BlockSpec((s, LANES), lambda b, h: (b, 0))
    sc8 = pl.BlockSpec((1, C_HEADS), lambda b, h: (0, 0))
    return head, attn, gl, ba, sc8


def gdn_prep_fwd(qkv, ba, a_log, dt_b, nb, *, name):
    t = qkv.shape[0]
    s = t // nb
    nc = s // CHUNK
    hd = C_HEAD_DIM
    head, attn_sp, gl_sp, ba_sp, sc8 = _gdn_specs(s, nc)

    def body(q_ref, k_ref, v_ref, ba_ref, alog_ref, dtb_ref, qg_ref, kd_ref, w_ref, u_ref, at_ref, gl_ref):
        h = pl.program_id(1)
        a_log_h = _pick_lane(alog_ref[...], h)
        dt_b_h = _pick_lane(dtb_ref[...], h)

        def load(n):
            rows = pl.ds(pl.multiple_of(n * CHUNK, CHUNK), CHUNK)
            bav = ba_ref[rows, :]
            return q_ref[rows, :], k_ref[rows, :], v_ref[rows, :], _pick_lane(bav, h), _pick_lane(bav, C_HEADS + h)

        def compute(loaded):
            return _gdn_chunks_prep(*[list(x) for x in zip(*loaded)], a_log_h, dt_b_h, RAW_DOTS)

        def store(n, outs, carry):
            rows = pl.ds(pl.multiple_of(n * CHUNK, CHUNK), CHUNK)
            qg_ref[rows, :] = outs[0].astype(bf16)
            kd_ref[rows, :] = outs[1].astype(bf16)
            w_ref[rows, :] = outs[2].astype(bf16)
            u_ref[rows, :] = outs[3]
            at_ref[rows, :] = outs[4].astype(bf16)
            gl_ref[pl.ds(pl.multiple_of(n * SUBLANES, SUBLANES), SUBLANES), :] = jnp.broadcast_to(outs[5], (SUBLANES, hd))
            return carry

        _loop_unrolled(nc, PREP_FWD_UNROLL, load, compute, store, 0)

    big = jax.ShapeDtypeStruct((t, C_WIDTH), f32)
    bigb = jax.ShapeDtypeStruct((t, C_WIDTH), bf16)
    return pl.pallas_call(
        body, grid=(nb, C_HEADS),
        in_specs=[head(0), head(C_HEADS), head(2 * C_HEADS), ba_sp, sc8, sc8],
        out_specs=[head(0)] * 4 + [attn_sp, gl_sp],
        out_shape=[bigb, bigb, bigb, big, jax.ShapeDtypeStruct((C_HEADS, t, CHUNK), bf16),
                               jax.ShapeDtypeStruct((C_HEADS, nb * nc * SUBLANES, hd), f32)],
        compiler_params=_params(("parallel", "parallel")), name=name,
    )(qkv, qkv, qkv, ba, a_log, dt_b)


def gdn_prep_bwd(qkv, ba, a_log, dt_b, cts, nb, *, name):
    t = qkv.shape[0]
    s = t // nb
    nc = s // CHUNK
    hd = C_HEAD_DIM
    head, attn_sp, gl_sp, ba_sp, sc8 = _gdn_specs(s, nc)

    def body(q_ref, k_ref, v_ref, ba_ref, alog_ref, dtb_ref, cqg, ckd, cw_, cu, cat, cgl,
             dq_ref, dk_ref, dv_ref, dba_ref, dalog_ref, ddtb_ref):
        b = pl.program_id(0)
        h = pl.program_id(1)
        a_log_h = _pick_lane(alog_ref[...], h)
        dt_b_h = _pick_lane(dtb_ref[...], h)
        prep = functools.partial(_gdn_chunks_prep, dots=VJP_DOTS)

        @pl.when(h == 0)
        def _():
            dba_ref[...] = jnp.zeros((s, LANES), f32)

        def load(n):
            rows = pl.ds(pl.multiple_of(n * CHUNK, CHUNK), CHUNK)
            bav = ba_ref[rows, :]
            cgl_n = cgl[pl.ds(pl.multiple_of(n * SUBLANES, SUBLANES), SUBLANES), :][0:1, :]
            primals = (q_ref[rows, :], k_ref[rows, :], v_ref[rows, :], _pick_lane(bav, h), _pick_lane(bav, C_HEADS + h))
            return primals, (cqg[rows, :], ckd[rows, :], cw_[rows, :], cu[rows, :], cat[rows, :], cgl_n), dba_ref[rows, :]

        def compute(loaded):
            primals = [list(x) for x in zip(*[item[0] for item in loaded])]
            _, vjp = jax.vjp(prep, *primals, a_log_h, dt_b_h)
            dqs, dks, dvs, dbls, dals, dalog, ddtb = vjp([item[1] for item in loaded])
            zero = jnp.zeros((1, 1), f32)
            return [((dqs[i], dks[i], dvs[i], dbls[i], dals[i], dalog if i == 0 else zero, ddtb if i == 0 else zero),
                     loaded[i][2]) for i in range(len(loaded))]

        def store(n, res, carry):
            (dq, dk, dv, dbl, dal, dalog_n, ddtb_n), dba_old = res
            rows = pl.ds(pl.multiple_of(n * CHUNK, CHUNK), CHUNK)
            dq_ref[rows, :] = dq
            dk_ref[rows, :] = dk
            dv_ref[rows, :] = dv
            dba_ref[rows, :] = dba_old + _put_lane(dbl, h, LANES) + _put_lane(dal, C_HEADS + h, LANES)
            return carry[0] + dalog_n, carry[1] + ddtb_n

        da_log, ddt_b = _loop_unrolled(nc, PREP_BWD_UNROLL, load, compute, store,
                                       (jnp.zeros((1, 1), f32), jnp.zeros((1, 1), f32)))
        first = jnp.logical_and(b == 0, h == 0)

        @pl.when(first)
        def _():
            dalog_ref[...] = _put_lane(da_log, h, LANES)
            ddtb_ref[...] = _put_lane(ddt_b, h, LANES)

        @pl.when(jnp.logical_not(first))
        def _():
            dalog_ref[...] += _put_lane(da_log, h, LANES)
            ddtb_ref[...] += _put_lane(ddt_b, h, LANES)

    big = jax.ShapeDtypeStruct((t, C_WIDTH), f32)
    vec = pl.BlockSpec((1, LANES), lambda b, h: (0, 0))
    return pl.pallas_call(
        body, grid=(nb, C_HEADS),
        in_specs=[head(0), head(C_HEADS), head(2 * C_HEADS), ba_sp, sc8, sc8] + [head(0)] * 4 + [attn_sp, gl_sp],
        out_specs=[head(0)] * 3 + [ba_sp, vec, vec],
        out_shape=[big] * 3 + [jax.ShapeDtypeStruct((t, LANES), f32), jax.ShapeDtypeStruct((1, LANES), f32),
                               jax.ShapeDtypeStruct((1, LANES), f32)],
        compiler_params=_params(("arbitrary", "arbitrary")), name=name,
    )(qkv, qkv, qkv, ba, a_log, dt_b, *cts)


def _gdn_rec_specs(sb, nsb, hp, reverse):
    hd = C_HEAD_DIM
    ncb = sb // CHUNK
    blk = (lambda b, k: b * nsb + (nsb - 1 - k)) if reverse else (lambda b, k: b * nsb + k)
    wide = pl.BlockSpec((sb, hp * hd), lambda b, j, k: (blk(b, k), j))
    attn = pl.BlockSpec((hp, sb, CHUNK), lambda b, j, k: (j, blk(b, k), 0))
    gl = pl.BlockSpec((hp, ncb * SUBLANES, hd), lambda b, j, k: (j, blk(b, k), 0))
    ng = pl.BlockSpec((1, hd), lambda b, j, k: (0, 0))
    states = pl.BlockSpec((hp, ncb, hd, hd), lambda b, j, k: (j, blk(b, k), 0, 0))
    return wide, attn, gl, ng, states


def gdn_rec_fwd(qg, kdec, w, u, attn, gl, z, ng, nb, *, name):
    t = qg.shape[0]
    s = t // nb
    sb = min(s, GDN_TIME_BLOCK)
    nsb = s // sb
    hd = C_HEAD_DIM
    hp = C_HEADS_PER_STEP
    wide, attn_sp, gl_sp, ng_sp, st_sp = _gdn_rec_specs(sb, nsb, hp, False)

    def body(qg_ref, kd_ref, w_ref, u_ref, at_ref, gl_ref, z_ref, ng_ref, y_ref, st_ref, carry_ref):
        @pl.when(pl.program_id(2) == 0)
        def _():
            carry_ref[...] = jnp.zeros((hp, hd, hd), f32)

        def chunk(n, states):
            for j in range(hp):
                st_ref[j, n] = states[j]
            rows = pl.ds(pl.multiple_of(n * CHUNK, CHUNK), CHUNK)
            grow = pl.ds(pl.multiple_of(n * SUBLANES, SUBLANES), SUBLANES)
            cols = [slice(j * hd, (j + 1) * hd) for j in range(hp)]
            ins = [(qg_ref[rows, c], kd_ref[rows, c], w_ref[rows, c], u_ref[rows, c], at_ref[j, rows, :],
                    gl_ref[j, grow, :][0:1, :], z_ref[rows, c]) for j, c in enumerate(cols)]
            ys, new = _gdn_heads_step(list(states), *[list(x) for x in zip(*ins)], ng_ref[...], RAW_DOTS)
            for j in range(hp):
                y_ref[rows, cols[j]] = ys[j]
            return tuple(new)

        last = lax.fori_loop(0, sb // CHUNK, chunk, tuple(carry_ref[j] for j in range(hp)))
        for j in range(hp):
            carry_ref[j] = last[j]

    return pl.pallas_call(
        body, grid=(nb, C_HEADS // hp, nsb),
        in_specs=[wide] * 4 + [attn_sp, gl_sp, wide, ng_sp], out_specs=[wide, st_sp],
        out_shape=[jax.ShapeDtypeStruct((t, C_WIDTH), f32), jax.ShapeDtypeStruct((C_HEADS, t // CHUNK, hd, hd), f32)],
        scratch_shapes=[pltpu.VMEM((hp, hd, hd), f32)],
        compiler_params=_params(("parallel", "parallel", "arbitrary")), name=name,
    )(qg, kdec, w, u, attn, gl, z, ng)


def gdn_rec_bwd(qg, kdec, w, u, attn, gl, z, ng, states, dy, nb, *, name):
    t = qg.shape[0]
    s = t // nb
    sb = min(s, GDN_TIME_BLOCK)
    nsb = s // sb
    nc = sb // CHUNK
    hd = C_HEAD_DIM
    hp = C_HEADS_PER_STEP
    wide, attn_sp, gl_sp, ng_sp, st_sp = _gdn_rec_specs(sb, nsb, hp, True)

    def body(qg_ref, kd_ref, w_ref, u_ref, at_ref, gl_ref, z_ref, ng_ref, states, dy_ref,
             dqg_ref, dkd_ref, dw_ref, du_ref, dat_ref, dgl_ref, dz_ref, dng_ref, carry_ref):
        step = functools.partial(_gdn_heads_step, dots=VJP_DOTS)

        @pl.when(pl.program_id(2) == 0)
        def _():
            carry_ref[...] = jnp.zeros((hp, hd, hd), f32)

        def operands(n):
            rows = pl.ds(pl.multiple_of(n * CHUNK, CHUNK), CHUNK)
            grow = pl.ds(pl.multiple_of(n * SUBLANES, SUBLANES), SUBLANES)
            cols = [slice(j * hd, (j + 1) * hd) for j in range(hp)]
            return ([qg_ref[rows, c].astype(f32) for c in cols], [kd_ref[rows, c].astype(f32) for c in cols],
                    [w_ref[rows, c].astype(f32) for c in cols], [u_ref[rows, c] for c in cols],
                    [at_ref[j, rows, :].astype(f32) for j in range(hp)],
                    [gl_ref[j, grow, :][0:1, :] for j in range(hp)], [z_ref[rows, c] for c in cols])

        def bwd_chunk(i, carry):
            n = nc - 1 - i
            rows = pl.ds(pl.multiple_of(n * CHUNK, CHUNK), CHUNK)
            grow = pl.ds(pl.multiple_of(n * SUBLANES, SUBLANES), SUBLANES)
            dsts, dng = carry
            dys = [dy_ref[rows, j * hd:(j + 1) * hd] for j in range(hp)]
            _, vjp = jax.vjp(step, [states[j, n] for j in range(hp)], *operands(n), ng_ref[...])
            dst, dqg, dkd, dw, du, dat, dgl, dz, dng_n = vjp((dys, list(dsts)))
            for j in range(hp):
                cols = slice(j * hd, (j + 1) * hd)
                dqg_ref[rows, cols] = dqg[j]
                dkd_ref[rows, cols] = dkd[j]
                dw_ref[rows, cols] = dw[j]
                du_ref[rows, cols] = du[j]
                dat_ref[j, rows, :] = dat[j]
                dgl_ref[j, grow, :] = jnp.broadcast_to(dgl[j], (SUBLANES, hd))
                dz_ref[rows, cols] = dz[j]
            return tuple(dst), dng + dng_n

        dlast, dng = lax.fori_loop(0, nc, bwd_chunk,
                                   (tuple(carry_ref[j] for j in range(hp)), jnp.zeros((1, hd), f32)))
        for j in range(hp):
            carry_ref[j] = dlast[j]
        first = jnp.logical_and(jnp.logical_and(pl.program_id(0) == 0, pl.program_id(1) == 0), pl.program_id(2) == 0)

        @pl.when(first)
        def _():
            dng_ref[...] = dng

        @pl.when(jnp.logical_not(first))
        def _():
            dng_ref[...] += dng

    big = jax.ShapeDtypeStruct((t, C_WIDTH), f32)
    return pl.pallas_call(
        body, grid=(nb, C_HEADS // hp, nsb),
        in_specs=[wide] * 4 + [attn_sp, gl_sp, wide, ng_sp, st_sp, wide],
        out_specs=[wide] * 4 + [attn_sp, gl_sp, wide, ng_sp],
        out_shape=[big] * 4 + [jax.ShapeDtypeStruct(attn.shape, f32), jax.ShapeDtypeStruct(gl.shape, f32), big,
                               jax.ShapeDtypeStruct((1, hd), f32)],
        scratch_shapes=[pltpu.VMEM((hp, hd, hd), f32)],
        compiler_params=_params(("arbitrary", "arbitrary", "arbitrary")), name=name,
    )(qg, kdec, w, u, attn, gl, z, ng, states, dy)


def _blockdiag_slabs(w):
    per = GROUP_W // B_BLOCK
    slabs = jnp.zeros((B_BLOCKS // per, GROUP_W, GROUP_W), w.dtype)
    for h in range(B_BLOCKS):
        o = (h % per) * B_BLOCK
        slabs = slabs.at[h // per, o:o + B_BLOCK, o:o + B_BLOCK].set(w[h])
    return slabs


def _slab_blocks(slabs):
    per = GROUP_W // B_BLOCK
    return jnp.stack([slabs[h // per, (h % per) * B_BLOCK:(h % per + 1) * B_BLOCK,
                            (h % per) * B_BLOCK:(h % per + 1) * B_BLOCK] for h in range(B_BLOCKS)])


def _mixer_ab_fwd(x1, x1b, W, g, b, nb, tag):
    w_in = W["ab_w_in"][0].astype(bf16)
    o1, o2 = A_WIDTH + 2 * A_KV_WIDTH, A_WIDTH + 2 * A_KV_WIDTH + B_WIDTH
    w_qkv, w_bx, w_bg = w_in[:, :o1], w_in[:, o1:o2], w_in[:, o2:]
    pqkv = mm_nn(x1b,w_qkv, name=tag + "_in_qkv")
    pbx = mm_nn(x1b,w_bx, name=tag + "_in_bx")
    pbg = mm_nn(x1b,w_bg, name=tag + "_in_bg")
    ya = attn_fwd(pqkv, W["a_sinks"], nb, name=tag + "_attn_fwd")
    xc = conv_fwd(pbx, W["b_conv_w"][0], W["b_conv_b"], False, nb, name=tag + "_conv_fwd")
    wa_s, wx_s = _blockdiag_slabs(W["b_wa"][0]), _blockdiag_slabs(W["b_wx"][0])
    yb, hh = rglru_fwd(xc, pbg, wa_s, wx_s, W["b_ba"], W["b_bx"], W["b_lam"], nb, name=tag + "_rglru_fwd")
    w_out = W["ab_w_out"][0].astype(bf16)
    x2, z1 = proj_ln([ya, yb], [w_out[:A_WIDTH], w_out[A_WIDTH:]], x1, g, b, name=tag + "_out_ln")
    saved = (pqkv, pbx, pbg, ya, xc, yb, hh, wa_s, wx_s, w_qkv, w_bx, w_bg, w_out)
    return x2, z1, saved


def _mixer_ab_bwd(x1b, dz1, dz1b, W, saved, nb, tag):
    pqkv, pbx, pbg, ya, xc, yb, hh, wa_s, wx_s, w_qkv, w_bx, w_bg, w_out = saved
    dya = mm_nn(dz1b, w_out[:A_WIDTH].T, name=tag + "_dya")
    dyb = mm_nn(dz1b, w_out[A_WIDTH:].T, name=tag + "_dyb")
    dwo = jnp.concatenate([mm_tn(ya, dz1b, name=tag + "_dwo_a"), mm_tn(yb, dz1b, name=tag + "_dwo_b")], 0)
    dpqkv, dsinks = attn_bwd(pqkv, W["a_sinks"], dya, nb, name=tag + "_attn_bwd")
    dxc, dpbg, dwa_s, dwx_s, dba, dbx, dlam = rglru_bwd(xc, pbg, hh, dyb, wa_s, wx_s, W["b_ba"], W["b_bx"],
                                                       W["b_lam"], nb, name=tag + "_rglru_bwd")
    dpbx, dconv_w, dconv_b = conv_bwd(pbx, W["b_conv_w"][0], W["b_conv_b"], dxc, False, nb, name=tag + "_conv_bwd")
    dw_in = jnp.concatenate([mm_tn(x1b,dpqkv, name=tag + "_dwin_qkv"), mm_tn(x1b,dpbx, name=tag + "_dwin_bx"),
                             mm_tn(x1b,dpbg, name=tag + "_dwin_bg")], 1)
    dx1 = mm_nn(dpqkv, w_qkv.T, add=dz1, add_scale=DN_ALPHA, name=tag + "_dx_qkv")
    dx1 = mm_nn(dpbx, w_bx.T, add=dx1, name=tag + "_dx_bx")
    dx1 = mm_nn(dpbg, w_bg.T, add=dx1, name=tag + "_dx_bg")
    grads = {"ab_w_in": dw_in[None], "a_sinks": dsinks, "b_conv_w": dconv_w[None], "b_conv_b": dconv_b,
             "b_wa": _slab_blocks(dwa_s)[None], "b_ba": dba, "b_wx": _slab_blocks(dwx_s)[None], "b_bx": dbx,
             "b_lam": dlam, "ab_w_out": dwo[None]}
    return dx1, grads


def _mixer_c_fwd(x1, x1b, W, g, b, nb, tag):
    w_in = W["c_w_in"][0].astype(bf16)
    d = w_in.shape[0]
    o1, o2 = 3 * C_WIDTH, 4 * C_WIDTH
    w_qkv, w_z = w_in[:, :o1], w_in[:, o1:o2]
    w_ba = jnp.concatenate([w_in[:, o2:], jnp.zeros((d, LANES - 2 * C_HEADS), bf16)], 1)
    pqkv = mm_nn(x1b,w_qkv, name=tag + "_in_qkv")
    pz = mm_nn(x1b,w_z, name=tag + "_in_z")
    pba = mm_nn(x1b,w_ba, name=tag + "_in_ba")
    zero_b = jnp.zeros((1, o1), f32)
    qkvc = conv_fwd(pqkv, W["c_conv_w"][0], zero_b, True, nb, name=tag + "_conv_fwd")
    prep = gdn_prep_fwd(qkvc, pba, W["c_a_log"], W["c_dt_bias"], nb, name=tag + "_prep_fwd")
    yc, states = gdn_rec_fwd(*prep, pz, W["c_norm_g"], nb, name=tag + "_rec_fwd")
    w_out = W["c_w_out"][0].astype(bf16)
    x2, z1 = proj_ln([yc], [w_out], x1, g, b, name=tag + "_out_ln")
    saved = (pqkv, pz, pba, qkvc, prep, states, yc, w_qkv, w_z, w_ba, w_out, zero_b)
    return x2, z1, saved


def _mixer_c_bwd(x1b, dz1, dz1b, W, saved, nb, tag):
    pqkv, pz, pba, qkvc, prep, states, yc, w_qkv, w_z, w_ba, w_out, zero_b = saved
    dyc = mm_nn(dz1b, w_out.T, name=tag + "_dyc")
    dwo = mm_tn(yc, dz1b, name=tag + "_dwo")
    rec = gdn_rec_bwd(*prep, pz, W["c_norm_g"], states, dyc, nb, name=tag + "_rec_bwd")
    cts, dpz, dng = rec[:6], rec[6], rec[7]
    dq, dk, dv, dpba, dalog, ddtb = gdn_prep_bwd(qkvc, pba, W["c_a_log"], W["c_dt_bias"], cts, nb,
                                                 name=tag + "_prep_bwd")
    dqkvc = jnp.concatenate([dq, dk, dv], 1)
    dpqkv, dconv_w, _ = conv_bwd(pqkv, W["c_conv_w"][0], zero_b, dqkvc, True, nb, name=tag + "_conv_bwd")
    dw_in = jnp.concatenate([mm_tn(x1b,dpqkv, name=tag + "_dwin_qkv"), mm_tn(x1b,dpz, name=tag + "_dwin_z"),
                             mm_tn(x1b,dpba, name=tag + "_dwin_ba")[:, :2 * C_HEADS]], 1)
    dx1 = mm_nn(dpqkv, w_qkv.T, add=dz1, add_scale=DN_ALPHA, name=tag + "_dx_qkv")
    dx1 = mm_nn(dpz, w_z.T, add=dx1, name=tag + "_dx_z")
    dx1 = mm_nn(dpba, w_ba.T, add=dx1, name=tag + "_dx_ba")
    grads = {"c_w_in": dw_in[None], "c_conv_w": dconv_w[None], "c_a_log": dalog[:, :C_HEADS],
             "c_dt_bias": ddtb[:, :C_HEADS], "c_norm_g": dng, "c_w_out": dwo[None]}
    return dx1, grads


def _local_step(x, p, target, W, F, on_ffn_grads, late_weights):
    nb, s, d = x.shape
    t = nb * s
    h = x.reshape(t, d)
    tape = []
    for i in range(DEPTH):
        tag = f"l{i}"
        f1 = [F[k][i] for k in ("ffn1_wg", "ffn1_wu", "ffn1_wd")]
        f2 = [F[k][i] for k in ("ffn2_wg", "ffn2_wu", "ffn2_wd")]
        lg = [W["ln_g"][i, k][None] for k in range(3)]
        lb = [W["ln_b"][i, k][None] for k in range(3)]
        x1, z0, x1b = ffn_fwd(h, *f1, lg[0], lb[0], name=tag + "_ffn1_fwd")
        if i == 0:
            x1, more = late_weights(x1)
            W = {**W, **more}
        mixer = _mixer_ab_fwd if i % 2 == 0 else _mixer_c_fwd
        x2, z1, msaved = mixer(x1, x1b, W, lg[1], lb[1], nb, tag + "_mix")
        x3, z2, _ = ffn_fwd(x2, *f2, lg[2], lb[2], name=tag + "_ffn2_fwd")
        pi = p[i].reshape(t, -1)
        pw = (W["ple_wg"][i].astype(bf16), W["ple_bg"][i][None], W["ple_wp"][i].astype(bf16))
        x4 = ple_fwd(x3, pi, *pw, name=tag + "_ple_fwd")
        tape.append((h, z0, x1b, msaved, z1, x2, z2, x3, pi, pw, lg))
        h = x4
    dh, sq = loss_head(h, target.reshape(t, d), name="loss_head")
    loss = 0.5 * jnp.sum(sq) / d
    per_layer = [None] * DEPTH
    grads = {}
    for i in reversed(range(DEPTH)):
        tag = f"l{i}"
        h_in, z0, x1b, msaved, z1, x2, z2, x3, pi, pw, lg = tape[i]
        dx3, dple_wg, dple_bg, dple_wp = ple_bwd(x3, pi, dh, pw[0], pw[0].T, pw[1], pw[2], name=tag + "_ple_bwd")
        dz2, dz2b, dg2, db2 = ln_bwd(z2, dx3, lg[2], name=tag + "_ln2_bwd")
        f1 = [F[k][i] for k in ("ffn1_wg", "ffn1_wu", "ffn1_wd")]
        f2 = [F[k][i] for k in ("ffn2_wg", "ffn2_wu", "ffn2_wd")]
        dgate, dup, *df2 = ffn_bwd_weights(x2.astype(bf16), dz2b, *f2, name=tag + "_ffn2_bwd_w")
        on_ffn_grads(i, 3, df2)
        dx2 = ffn_bwd_input(dgate, dup, f2[0], f2[1], dz2, name=tag + "_ffn2_bwd_x")
        dz1, dz1b, dg1, db1 = ln_bwd(z1, dx2, lg[1], name=tag + "_ln1_bwd")
        mixer_bwd = _mixer_ab_bwd if i % 2 == 0 else _mixer_c_bwd
        dx1, mgrads = mixer_bwd(x1b, dz1, dz1b, W, msaved, nb, tag + "_mix")
        grads.update(mgrads)
        dz0, dz0b, dg0, db0 = ln_bwd(z0, dx1, lg[0], name=tag + "_ln0_bwd")
        dgate, dup, *df1 = ffn_bwd_weights(h_in.astype(bf16), dz0b, *f1, name=tag + "_ffn1_bwd_w")
        on_ffn_grads(i, 0, df1)
        dh = ffn_bwd_input(dgate, dup, f1[0], f1[1], dz0, name=tag + "_ffn1_bwd_x")
        per_layer[i] = {"ln_g": jnp.concatenate([dg0, dg1, dg2], 0), "ln_b": jnp.concatenate([db0, db1, db2], 0),
                        "ple_wg": dple_wg, "ple_bg": dple_bg[0], "ple_wp": dple_wp}
    for k in per_layer[0]:
        grads[k] = jnp.stack([per_layer[i][k] for i in range(DEPTH)])
    return loss, dh.reshape(nb, s, d), grads


WEIGHT_NAMES = ("ffn1_wg", "ffn1_wu", "ffn1_wd", "ffn2_wg", "ffn2_wu", "ffn2_wd", "ln_g", "ln_b", "ple_wg", "ple_bg",
                "ple_wp", "ab_w_in", "a_sinks", "b_conv_w", "b_conv_b", "b_wa", "b_ba", "b_wx", "b_bx", "b_lam",
                "ab_w_out", "c_w_in", "c_conv_w", "c_a_log", "c_dt_bias", "c_norm_g", "c_w_out")
NATIVE_NAMES = WEIGHT_NAMES[:6]
PACKED_NAMES = WEIGHT_NAMES[6:]
GATHER_SMALL = ("ln_g", "ln_b", "b_conv_w", "c_conv_w")
GATHER_BIG = ("ple_wg", "ple_wp", "ab_w_in", "ab_w_out", "c_w_in", "c_w_out")
SHARD_AXIS = {"ffn1_wg": 2, "ffn1_wu": 2, "ffn1_wd": 1, "ffn2_wg": 2, "ffn2_wu": 2, "ffn2_wd": 1, "ln_g": 2, "ln_b": 2,
              "ple_wg": 1, "ple_wp": 2, "ab_w_in": 2, "b_conv_w": 2, "ab_w_out": 1, "c_w_in": 2, "c_conv_w": 2,
              "c_w_out": 1}
N_CHIPS = 4
PACK_COLS = LANES
PACK_TILE_MULTIPLE = 256
ELEMENTWISE_BLOCK_ELEMS = 128 * 1024


def _row_tile(r, cols):
    return _tile(r, max(2 * SUBLANES, ELEMENTWISE_BLOCK_ELEMS // cols), 2 * SUBLANES)
MESH = pl.DeviceIdType.MESH
ANY = pl.BlockSpec(memory_space=pl.ANY)


def _tiled_dims(shape):
    w = shape[-1]
    r = 1
    for dim in shape[:-1]:
        r *= dim
    return r, w, -(-r // SUBLANES) * SUBLANES, -(-w // LANES) * LANES


def _pack(pieces, lead=()):
    k = len(lead)
    tiles = []
    for a in pieces:
        r, w, rp, wp = _tiled_dims(a.shape[k:])
        a2 = jnp.pad(a.reshape(lead + (r, w)), [(0, 0)] * k + [(0, rp - r), (0, wp - w)])
        a2 = a2.reshape(lead + (rp // SUBLANES, SUBLANES, wp // LANES, LANES))
        a2 = jnp.swapaxes(a2, k + 1, k + 2)
        tiles.append(a2.reshape(lead + (-1, SUBLANES, LANES)))
    flat = jnp.concatenate(tiles, axis=k)
    n = flat.shape[k]
    n_pad = -(-n // PACK_TILE_MULTIPLE) * PACK_TILE_MULTIPLE
    flat = jnp.pad(flat, [(0, 0)] * k + [(0, n_pad - n), (0, 0), (0, 0)])
    return flat.reshape(lead + (n_pad * SUBLANES, PACK_COLS))


def _unpack(pack, shapes, lead=()):
    k = len(lead)
    flat = pack.reshape(lead + (-1, SUBLANES, LANES))
    out, o = [], 0
    for shp in shapes:
        r, w, rp, wp = _tiled_dims(shp)
        n = (rp // SUBLANES) * (wp // LANES)
        a2 = lax.slice_in_dim(flat, o, o + n, axis=k).reshape(lead + (rp // SUBLANES, wp // LANES, SUBLANES, LANES))
        a2 = jnp.swapaxes(a2, k + 1, k + 2).reshape(lead + (rp, wp))
        a2 = lax.slice_in_dim(lax.slice_in_dim(a2, 0, r, axis=k), 0, w, axis=k + 1)
        out.append(a2.reshape(lead + tuple(shp)))
        o += n
    return out


def _mesh_position():
    x, y, c = lax.axis_index("x"), lax.axis_index("y"), lax.axis_index("c")
    chips = [(1 - x, y), (x, 1 - y), (1 - x, 1 - y)]
    return x, y, c, chips


def _remote(src, dst, send_sems, recv_sems, k, to):
    return pltpu.make_async_remote_copy(src_ref=src, dst_ref=dst, send_sem=send_sems.at[k], recv_sem=recv_sems.at[k],
                                        device_id=to, device_id_type=MESH)


def _sems(n):
    return pltpu.SemaphoreType.DMA((n,))


def place_slot(parts, slots, n_slots, dtype, from_slot, *, name):
    n = len(parts)
    r, cols = parts[0].shape[-2:]
    tr = _row_tile(r, cols)

    def body(src_ref, dst_ref, *refs):
        for a in range(n):
            refs[n + a][...] = refs[a][...].astype(dtype)

    dst = pl.BlockSpec((None, tr, cols), lambda i, src_ref, dst_ref: (dst_ref[0], i, 0))
    src = (pl.BlockSpec((None, tr, cols), lambda i, src_ref, dst_ref: (src_ref[0], i, 0)) if from_slot
           else pl.BlockSpec((tr, cols), lambda i, src_ref, dst_ref: (i, 0)))
    return pl.pallas_call(
        body,
        grid_spec=pltpu.PrefetchScalarGridSpec(num_scalar_prefetch=2, grid=(r // tr,), in_specs=[src] * n,
                                               out_specs=[dst] * n),
        out_shape=[jax.ShapeDtypeStruct((n_slots, r, cols), dtype)] * n,
        compiler_params=_params(("parallel",)), name=name,
    )(*slots, *parts)


def gather_slots_async(bufs, collective_id, sources=None, *, name):
    n = len(bufs)
    refs = [jax.new_ref(b, memory_space=pltpu.MemorySpace.HBM) for b in bufs]
    src_refs = None if sources is None else [jax.new_ref(s_, memory_space=pltpu.MemorySpace.HBM) for s_ in sources]

    @pl.kernel(mesh=plsc.ScalarSubcoreMesh(axis_name="sequencer", num_cores=1), name=name,
               scratch_types=(_sems(3 * n), _sems(3 * n)),
               compiler_params=pltpu.CompilerParams(collective_id=collective_id))
    def launch(send_sems, recv_sems):
        x, y, c, chips = _mesh_position()
        me = 2 * x + y
        barrier = pltpu.get_barrier_semaphore()
        for cx, cy in chips:
            pl.semaphore_signal(barrier, inc=1, device_id=(cx, cy, c), device_id_type=MESH)
        pl.semaphore_wait(barrier, len(chips))
        sends = []
        for j, (cx, cy) in enumerate(chips):
            for a in range(n):
                own = refs[a].at[me]
                src = own if src_refs is None else src_refs[a].at[2 * cx + cy]
                cp = _remote(src, own, send_sems, recv_sems, 3 * a + j, (cx, cy, c))
                cp.start()
                sends.append(cp)
        for j, (cx, cy) in enumerate(chips):
            for a in range(n):
                got = refs[a].at[2 * cx + cy]
                _remote(got, got, send_sems, recv_sems, 3 * a + j, (cx, cy, c)).wait_recv()
        for cp in sends:
            cp.wait_send()

    launch()
    return [r[...] for r in refs]


N_DEVICES = 8
PEER_FLIPS = tuple((dx, dy, dc) for dx in (0, 1) for dy in (0, 1) for dc in (0, 1) if dx or dy or dc)


def exchange_partials_async(sends, recvs, collective_id, *, name):
    n = len(sends)
    s_refs = [jax.new_ref(a, memory_space=pltpu.MemorySpace.HBM) for a in sends]
    r_refs = [jax.new_ref(a, memory_space=pltpu.MemorySpace.HBM) for a in recvs]
    k = len(PEER_FLIPS)

    @pl.kernel(mesh=plsc.ScalarSubcoreMesh(axis_name="sequencer", num_cores=1), name=name,
               scratch_types=(_sems(k), _sems(k)), compiler_params=pltpu.CompilerParams(collective_id=collective_id))
    def launch(send_sems, recv_sems):
        x, y, c, _ = _mesh_position()
        me = 4 * x + 2 * y + c
        peers = [(1 - x if dx else x, 1 - y if dy else y, 1 - c if dc else c) for dx, dy, dc in PEER_FLIPS]
        barrier = pltpu.get_barrier_semaphore()
        for peer in peers:
            pl.semaphore_signal(barrier, inc=1, device_id=peer, device_id_type=MESH)
        pl.semaphore_wait(barrier, len(peers))
        sends_started = []
        for j, (px, py, pc) in enumerate(peers):
            for a in range(n):
                cp = _remote(s_refs[a].at[2 * px + py], r_refs[a].at[me], send_sems, recv_sems, j, (px, py, pc))
                cp.start()
                sends_started.append(cp)
        for j, (px, py, pc) in enumerate(peers):
            for a in range(n):
                got = r_refs[a].at[4 * px + 2 * py + pc]
                _remote(got, got, send_sems, recv_sems, j, (px, py, pc)).wait_recv()
        for cp in sends_started:
            cp.wait_send()

    launch()
    return [r[...] for r in r_refs]


def sibling_exchange(gs, *, name):
    n = len(gs)

    def body(*refs):
        g_refs, out_refs = refs[:n], refs[n:2 * n]
        send_sems, recv_sems = refs[2 * n:]
        x, y, c, _ = _mesh_position()
        cps = [_remote(g_refs[a].at[:, 1 - c], out_refs[a], send_sems, recv_sems, a, (x, y, 1 - c)) for a in range(n)]
        for cp in cps:
            cp.start()
        for cp in cps:
            cp.wait()

    return pl.pallas_call(
        body, out_shape=[jax.ShapeDtypeStruct(g.shape[:1] + g.shape[2:], g.dtype) for g in gs],
        in_specs=[ANY] * n, out_specs=[ANY] * n, scratch_shapes=[_sems(n), _sems(n)], name=name,
    )(*gs)


def add_own_half(gs, others, c_idx, dtype, *, name):
    n = len(gs)
    ns, _, r, cols = gs[0].shape
    tr = _row_tile(r, cols)

    def body(c_ref, *refs):
        for a in range(n):
            refs[2 * n + a][...] = (refs[a][...] + refs[n + a][...]).astype(dtype)

    own = pl.BlockSpec((None, None, tr, cols), lambda s, i, c_ref: (s, c_ref[0], i, 0))
    oth = pl.BlockSpec((None, tr, cols), lambda s, i, c_ref: (s, i, 0))
    return pl.pallas_call(
        body,
        grid_spec=pltpu.PrefetchScalarGridSpec(num_scalar_prefetch=1, grid=(ns, r // tr),
                                               in_specs=[own] * n + [oth] * n, out_specs=[oth] * n),
        out_shape=[jax.ShapeDtypeStruct((ns, r, cols), dtype)] * n,
        compiler_params=_params(("parallel", "parallel")), name=name,
    )(c_idx, *gs, *others)


def sum_slots(qs, *, name):
    n = len(qs)
    ns, r, cols = qs[0].shape
    tr = _row_tile(r, cols * ns)

    def body(*refs):
        for a in range(n):
            q_ref = refs[a]
            acc = q_ref[0].astype(f32) + q_ref[1].astype(f32)
            for i in range(2, ns):
                acc = acc + q_ref[i].astype(f32)
            refs[n + a][...] = acc

    return pl.pallas_call(
        body, grid=(r // tr,), in_specs=[pl.BlockSpec((ns, tr, cols), lambda i: (0, i, 0))] * n,
        out_specs=[pl.BlockSpec((tr, cols), lambda i: (i, 0))] * n,
        out_shape=[jax.ShapeDtypeStruct((r, cols), f32)] * n,
        compiler_params=_params(("parallel",)), name=name,
    )(*qs)


def sibling_share(bufs, *, name):
    n = len(bufs)

    def body(*refs):
        out_refs = refs[n:2 * n]
        send_sems, recv_sems = refs[2 * n:]
        x, y, c, _ = _mesh_position()
        sibling = (x, y, 1 - c)
        cps = []
        for a in range(n):
            own = out_refs[a].at[c]
            cp = _remote(own, own, send_sems, recv_sems, a, sibling)
            cp.start()
            cps.append(cp)
        for a in range(n):
            theirs = out_refs[a].at[1 - c]
            _remote(theirs, theirs, send_sems, recv_sems, a, sibling).wait_recv()
        for cp in cps:
            cp.wait_send()

    return pl.pallas_call(
        body, out_shape=[jax.ShapeDtypeStruct(b.shape, b.dtype) for b in bufs], in_specs=[ANY] * n,
        out_specs=[ANY] * n, scratch_shapes=[_sems(n), _sems(n)],
        input_output_aliases={a: a for a in range(n)}, name=name,
    )(*bufs)


def adamw(ws, gs, ms, vs, *, name):
    n = len(ws)
    r, cols = ws[0].shape
    tr = _row_tile(r, cols)

    def body(*refs):
        for a in range(n):
            w_ref, g_ref, m_ref, v_ref = (refs[k * n + a] for k in range(4))
            d_ref, m2_ref, v2_ref = (refs[(4 + k) * n + a] for k in range(3))
            g_ = g_ref[...]
            m2 = ADAM_B1 * m_ref[...] + (1.0 - ADAM_B1) * g_
            v2 = ADAM_B2 * v_ref[...] + (1.0 - ADAM_B2) * (g_ * g_)
            m_hat = m2 / (1.0 - ADAM_B1 ** ADAM_STEP)
            v_hat = v2 / (1.0 - ADAM_B2 ** ADAM_STEP)
            d_ref[...] = -ADAM_LR * (m_hat / (jnp.sqrt(v_hat) + ADAM_EPS) + ADAM_WD * w_ref[...])
            m2_ref[...] = m2
            v2_ref[...] = v2

    row = pl.BlockSpec((tr, cols), lambda i: (i, 0))
    out = pl.pallas_call(
        body, grid=(r // tr,), in_specs=[row] * (4 * n), out_specs=[row] * (3 * n),
        out_shape=[jax.ShapeDtypeStruct((r, cols), f32)] * (3 * n),
        compiler_params=_params(("parallel",)), name=name,
    )(*ws, *gs, *ms, *vs)
    return out[:n], out[n:2 * n], out[2 * n:]


def _full_weights(gathered, names, weights):
    pieces = _unpack(gathered, [weights[k].shape for k in names], lead=(N_CHIPS,))
    full = {}
    for name, pc in zip(names, pieces):
        ax = SHARD_AXIS[name]
        shp = weights[name].shape
        full[name] = jnp.moveaxis(pc, 0, ax).reshape(shp[:ax] + (N_CHIPS * shp[ax],) + shp[ax + 1:])
    return full


def _grad_pack(grads, shapes):
    pieces = []
    for name, shp in zip(PACKED_NAMES, shapes):
        g = grads[name]
        ax = SHARD_AXIS.get(name)
        if ax is None:
            pieces.append(jnp.broadcast_to(g.reshape(shp)[None], (N_CHIPS,) + tuple(shp)))
        else:
            pieces.append(jnp.stack(jnp.split(g, N_CHIPS, axis=ax)))
    return _pack(pieces, lead=(N_CHIPS,))


def _by_shape(arrays):
    groups = {}
    for i, a in enumerate(arrays):
        groups.setdefault(a.shape, []).append(i)
    return list(groups.values())


def _grouped(fn, lists, n_out, tag):
    outs = [[None] * len(lists[0]) for _ in range(n_out)]
    for gi, idx in enumerate(_by_shape(lists[0])):
        res = fn(*[[lst[i] for i in idx] for lst in lists], name=f"{tag}_{gi}")
        res = res if n_out > 1 else (res,)
        for k in range(n_out):
            for i, r in zip(idx, res[k]):
                outs[k][i] = r
    return outs if n_out > 1 else outs[0]


def _train_step(x, p, loss_target, weights, m, v):
    packed_w = [weights[k] for k in PACKED_NAMES]
    shapes = [w.shape for w in packed_w]
    halves = lambda a: a.reshape((2, a.shape[0] // 2) + a.shape[1:])
    local = [weights[k] for k in NATIVE_NAMES] + [halves(_pack(packed_w))]
    local_m = [m[k] for k in NATIVE_NAMES] + [halves(_pack([m[k] for k in PACKED_NAMES]))]
    local_v = [v[k] for k in NATIVE_NAMES] + [halves(_pack([v[k] for k in PACKED_NAMES]))]
    flat = lambda lst: [a.reshape((-1, a.shape[-1])) for a in lst]
    c_idx = lax.axis_index("c").astype(jnp.int32).reshape(1)
    chip_idx = (2 * lax.axis_index("x") + lax.axis_index("y")).astype(jnp.int32).reshape(1)
    c2 = (c_idx, c_idx)
    chip2 = (chip_idx, chip_idx)
    chip_dev = (chip_idx, 2 * chip_idx + c_idx)

    def placed(arrays, slot, n_slots, dtype, from_slot, tag):
        return _grouped(lambda a, name: place_slot(a, slot, n_slots, dtype, from_slot, name=name), [arrays], 1, tag)

    ffn_own = [weights[k][i] for i in range(DEPTH) for k in NATIVE_NAMES]
    ffn_bufs = placed(ffn_own, chip2, N_CHIPS, bf16, False, "place_ffn_weights")
    small_buf = placed([_pack([weights[k] for k in GATHER_SMALL])], chip2, N_CHIPS, f32, False, "place_small_weights")
    big_buf = placed([_pack([weights[k] for k in GATHER_BIG])], chip2, N_CHIPS, bf16, False, "place_big_weights")
    group = len(NATIVE_NAMES) // 2
    n_ffn_groups = len(ffn_bufs) // group
    full = {k: weights[k] for k in PACKED_NAMES if k not in SHARD_AXIS}
    full.update(_full_weights(gather_slots_async(small_buf, collective_id=n_ffn_groups + 1,
                                                 name="comm_gather_small")[0], GATHER_SMALL, weights))
    ffn_gathered = []
    for gi in range(n_ffn_groups):
        ffn_gathered += gather_slots_async(ffn_bufs[gi * group:(gi + 1) * group], collective_id=1 + gi,
                                           name=f"comm_gather_ffn_{gi}")
        if gi == 0:
            big_gathered = gather_slots_async(big_buf, collective_id=n_ffn_groups + 2, name="comm_gather_big")[0]

    def late_weights(x1):
        big, x1 = lax.optimization_barrier((big_gathered, x1))
        return x1, _full_weights(big, GATHER_BIG, weights)

    ffn_weights ={k: [ffn_gathered[i * len(NATIVE_NAMES) + j] for i in range(DEPTH)] for j, k in enumerate(NATIVE_NAMES)}
    first_grad_id = n_ffn_groups + 3
    in_flight = {}

    def on_ffn_grads(layer, first, partials):
        tag = f"ffn_grads_l{layer}_{first}"
        recvs = placed(partials, chip_dev, N_DEVICES, bf16, True, "place_" + tag)
        got = exchange_partials_async(partials, recvs, collective_id=first_grad_id + len(in_flight), name="comm_" + tag)
        in_flight[(layer, first)] = got

    loss, grad_x, grads = _local_step(x, p, loss_target, full, ffn_weights, on_ffn_grads, late_weights)
    order = list(in_flight)
    tied, grad_x = lax.optimization_barrier(([in_flight[k] for k in order], grad_x))
    in_flight = dict(zip(order, tied))
    gs = [_grad_pack(grads, shapes).reshape((N_CHIPS,) + local[-1].shape)]
    others = sibling_exchange(gs, name="comm_grad_sibling")
    chip_sums = add_own_half(gs, others, c_idx, f32, name="grad_add_sibling")
    own = placed(chip_sums, chip2, N_CHIPS, f32, True, "place_own_partial")
    slots = gather_slots_async(own, collective_id=first_grad_id + len(in_flight), sources=chip_sums, name="comm_grad_chips")
    ffn_sums = {}
    for (layer, first), got in in_flight.items():
        sums = _grouped(sum_slots, [got], 1, f"grad_sum_ffn_l{layer}_{first}")
        for j, g in enumerate(sums):
            ffn_sums[(NATIVE_NAMES[first + j], layer)] = g
    nn_ = len(NATIVE_NAMES)
    gsum = [jnp.stack([ffn_sums[(k, i)] for i in range(DEPTH)]) for k in NATIVE_NAMES]
    delta, m2, v2 = _grouped(adamw, [flat(local[:nn_]), flat(gsum), flat(local_m[:nn_]), flat(local_v[:nn_])], 3,
                             "adamw_ffn")
    mine = sum_slots(slots, name="grad_sum_chips")
    gsum += list(sibling_share(placed(mine, c2, 2, f32, False, "place_own_half"), name="comm_grad_share"))
    pack_upd = adamw(flat(local[nn_:]), flat(gsum[nn_:]), flat(local_m[nn_:]), flat(local_v[nn_:]), name="adamw_packed")
    delta, m2, v2 = (a + list(b_) for a, b_ in zip((delta, m2, v2), pack_upd))
    loss = lax.psum(loss, ("x", "y", "c"))
    outs = []
    for res in (gsum, delta, m2, v2):
        by_name = {k: a.reshape(weights[k].shape) for k, a in zip(NATIVE_NAMES, res[:-1])}
        by_name.update(zip(PACKED_NAMES, _unpack(res[-1], shapes)))
        outs += [by_name[k] for k in WEIGHT_NAMES]
    return (loss, grad_x, *outs)


def kernel(x, p, ffn1_wg, ffn1_wu, ffn1_wd, ffn2_wg, ffn2_wu, ffn2_wd, ln_g, ln_b, ple_wg, ple_bg, ple_wp, ab_w_in, a_sinks, b_conv_w, b_conv_b, b_wa, b_ba, b_wx, b_bx, b_lam, ab_w_out, c_w_in, c_conv_w, c_a_log, c_dt_bias, c_norm_g, c_w_out, loss_target, m_ffn1_wg, m_ffn1_wu, m_ffn1_wd, m_ffn2_wg, m_ffn2_wu, m_ffn2_wd, m_ln_g, m_ln_b, m_ple_wg, m_ple_bg, m_ple_wp, m_ab_w_in, m_a_sinks, m_b_conv_w, m_b_conv_b, m_b_wa, m_b_ba, m_b_wx, m_b_bx, m_b_lam, m_ab_w_out, m_c_w_in, m_c_conv_w, m_c_a_log, m_c_dt_bias, m_c_norm_g, m_c_w_out, v_ffn1_wg, v_ffn1_wu, v_ffn1_wd, v_ffn2_wg, v_ffn2_wu, v_ffn2_wd, v_ln_g, v_ln_b, v_ple_wg, v_ple_bg, v_ple_wp, v_ab_w_in, v_a_sinks, v_b_conv_w, v_b_conv_b, v_b_wa, v_b_ba, v_b_wx, v_b_bx, v_b_lam, v_ab_w_out, v_c_w_in, v_c_conv_w, v_c_a_log, v_c_dt_bias, v_c_norm_g, v_c_w_out):
    weights = [ffn1_wg, ffn1_wu, ffn1_wd, ffn2_wg, ffn2_wu, ffn2_wd, ln_g, ln_b, ple_wg, ple_bg, ple_wp, ab_w_in, a_sinks,
               b_conv_w, b_conv_b, b_wa, b_ba, b_wx, b_bx, b_lam, ab_w_out, c_w_in, c_conv_w, c_a_log, c_dt_bias, c_norm_g,
               c_w_out]
    m = [m_ffn1_wg, m_ffn1_wu, m_ffn1_wd, m_ffn2_wg, m_ffn2_wu, m_ffn2_wd, m_ln_g, m_ln_b, m_ple_wg, m_ple_bg, m_ple_wp,
         m_ab_w_in, m_a_sinks, m_b_conv_w, m_b_conv_b, m_b_wa, m_b_ba, m_b_wx, m_b_bx, m_b_lam, m_ab_w_out, m_c_w_in,
         m_c_conv_w, m_c_a_log, m_c_dt_bias, m_c_norm_g, m_c_w_out]
    v = [v_ffn1_wg, v_ffn1_wu, v_ffn1_wd, v_ffn2_wg, v_ffn2_wu, v_ffn2_wd, v_ln_g, v_ln_b, v_ple_wg, v_ple_bg, v_ple_wp,
         v_ab_w_in, v_a_sinks, v_b_conv_w, v_b_conv_b, v_b_wa, v_b_ba, v_b_wx, v_b_bx, v_b_lam, v_ab_w_out, v_c_w_in,
         v_c_conv_w, v_c_a_log, v_c_dt_bias, v_c_norm_g, v_c_w_out]
    return _train_step(x, p, loss_target, dict(zip(WEIGHT_NAMES, weights)), dict(zip(WEIGHT_NAMES, m)),
                       dict(zip(WEIGHT_NAMES, v)))
```

```python
import functools

import jax
import jax.numpy as jnp
from jax import lax
from jax.experimental import pallas as pl
from jax.experimental.pallas import tpu as pltpu
from jax.experimental.pallas import tpu_sc as plsc

f32 = jnp.float32
bf16 = jnp.bfloat16

DEPTH = 2
CHUNK = 64
A_HEADS, A_KV_HEADS, A_GROUP, A_HEAD_DIM = 8, 2, 4, 64
A_WIDTH, A_KV_WIDTH, A_WINDOW = 512, 128, 128
B_WIDTH, B_BLOCKS, B_BLOCK, B_CONV = 512, 8, 64, 4
RG_C = 8.0
C_HEADS, C_HEAD_DIM, C_WIDTH, C_CONV = 8, 128, 1024, 4
DN_ALPHA = (2.0 * DEPTH) ** 0.25
LN_EPS = 1e-5
NORM_EPS = 1e-6
NEG = -1e30
ADAM_LR, ADAM_B1, ADAM_B2, ADAM_EPS, ADAM_WD, ADAM_STEP = 0.001, 0.9, 0.999, 1e-08, 0.01, 10

VMEM_LIMIT_BYTES = 56 * 1024 * 1024
LANES = 128
SUBLANES = 8
GROUP_W = 128
PREP_FWD_UNROLL = 8
PREP_BWD_UNROLL = 8
C_HEADS_PER_STEP = 4
GDN_TIME_BLOCK = 512

NN = ((1,), (0,))
NT = ((1,), (1,))
TN = ((0,), (0,))


def _params(sem):
    return pltpu.CompilerParams(dimension_semantics=sem, vmem_limit_bytes=VMEM_LIMIT_BYTES)


def _tile(n, cap, mult):
    best = None
    t = mult
    while t <= min(n, cap):
        if n % t == 0:
            best = t
        t += mult
    return best if best is not None else n


def _bdot(a, b, dims):
    return lax.dot_general(a.astype(bf16), b.astype(bf16), (dims, ((), ())), preferred_element_type=f32)


def _running_sum(x, reverse):
    s = x.shape[0]
    t = lax.broadcasted_iota(jnp.int32, x.shape, 0)
    d = 1
    while d < s:
        if reverse:
            x = x + jnp.where(t < s - d, pltpu.roll(x, s - d, 0), 0.0)
        else:
            x = x + jnp.where(t >= d, pltpu.roll(x, d, 0), 0.0)
        d *= 2
    return x


@jax.custom_vjp
def _cumsum0(x):
    return _running_sum(x, False)


def _cumsum0_fwd(x):
    return _running_sum(x, False), None


def _cumsum0_bwd(_, g):
    return (_running_sum(g, True),)


_cumsum0.defvjp(_cumsum0_fwd, _cumsum0_bwd)


@jax.custom_vjp
def _bnn(a, b):
    return _bdot(a, b, NN)


def _bnn_fwd(a, b):
    return _bdot(a, b, NN), (a, b)


def _bnn_bwd(res, g):
    a, b = res
    return _bdot(g, b, NT), _bdot(a, g, TN)


_bnn.defvjp(_bnn_fwd, _bnn_bwd)


@jax.custom_vjp
def _bnt(a, b):
    return _bdot(a, b, NT)


def _bnt_fwd(a, b):
    return _bdot(a, b, NT), (a, b)


def _bnt_bwd(res, g):
    a, b = res
    return _bdot(g, b, NN), _bdot(g, a, TN)


_bnt.defvjp(_bnt_fwd, _bnt_bwd)


@jax.custom_vjp
def _btn(a, b):
    return _bdot(a, b, TN)


def _btn_fwd(a, b):
    return _bdot(a, b, TN), (a, b)


def _btn_bwd(res, g):
    a, b = res
    return _bdot(b, g, NT), _bdot(a, g, NN)


_btn.defvjp(_btn_fwd, _btn_bwd)

RAW_DOTS = (lambda a, b: _bdot(a, b, NN), lambda a, b: _bdot(a, b, NT), lambda a, b: _bdot(a, b, TN),
            lambda x: _running_sum(x, False))
VJP_DOTS = (_bnn, _bnt, _btn, _cumsum0)


def _layer_norm(z, g, b):
    mu = jnp.mean(z, -1, keepdims=True)
    d = z - mu
    var = jnp.mean(d * d, -1, keepdims=True)
    return d * lax.rsqrt(var + LN_EPS) * g + b


def _silu(x):
    return x * jax.nn.sigmoid(x)


def mm_nn(a, w, add=None, add_scale=1.0, *, name):
    m, k = a.shape
    n = w.shape[1]
    tm = _tile(m, 512, SUBLANES)
    tn = _tile(n, 1024, LANES)

    def body(*refs):
        if add is None:
            a_ref, w_ref, o_ref = refs
            o_ref[...] = _bdot(a_ref[...], w_ref[...], NN)
        else:
            a_ref, w_ref, add_ref, o_ref = refs
            o_ref[...] = _bdot(a_ref[...], w_ref[...], NN) + add_scale * add_ref[...]

    in_specs = [pl.BlockSpec((tm, k), lambda i, j: (i, 0)), pl.BlockSpec((k, tn), lambda i, j: (0, j))]
    args = [a, w]
    if add is not None:
        in_specs.append(pl.BlockSpec((tm, tn), lambda i, j: (i, j)))
        args.append(add)
    return pl.pallas_call(
        body, grid=(m // tm, n // tn), in_specs=in_specs,
        out_specs=pl.BlockSpec((tm, tn), lambda i, j: (i, j)),
        out_shape=jax.ShapeDtypeStruct((m, n), f32),
        compiler_params=_params(("parallel", "parallel")), name=name,
    )(*args)


def mm_tn(a, b, *, name):
    m, k = a.shape
    n = b.shape[1]
    tm = _tile(m, 1024, 2 * SUBLANES)
    tn = _tile(n, 1024, LANES)

    def body(a_ref, b_ref, o_ref):
        part = _bdot(a_ref[...], b_ref[...], TN)

        @pl.when(pl.program_id(1) == 0)
        def _():
            o_ref[...] = part

        @pl.when(pl.program_id(1) > 0)
        def _():
            o_ref[...] += part

    return pl.pallas_call(
        body, grid=(n // tn, m // tm),
        in_specs=[pl.BlockSpec((tm, k), lambda j, i: (i, 0)), pl.BlockSpec((tm, tn), lambda j, i: (i, j))],
        out_specs=pl.BlockSpec((k, tn), lambda j, i: (0, j)),
        out_shape=jax.ShapeDtypeStruct((k, n), f32),
        compiler_params=_params(("parallel", "arbitrary")), name=name,
    )(a, b)


def proj_ln(a_list, w_list, xres, g, b, *, name):
    t, d = xres.shape
    tm = _tile(t, 256, SUBLANES)
    na = len(a_list)

    def body(*refs):
        a_refs, w_refs = refs[:na], refs[na:2 * na]
        x_ref, g_ref, b_ref, y_ref, z_ref = refs[2 * na:]
        z = DN_ALPHA * x_ref[...]
        for a_ref, w_ref in zip(a_refs, w_refs):
            z = z + _bdot(a_ref[...], w_ref[...], NN)
        z_ref[...] = z
        y_ref[...] = _layer_norm(z, g_ref[...], b_ref[...])

    in_specs = [pl.BlockSpec((tm, a.shape[1]), lambda i: (i, 0)) for a in a_list]
    in_specs += [pl.BlockSpec(w.shape, lambda i: (0, 0)) for w in w_list]
    in_specs += [pl.BlockSpec((tm, d), lambda i: (i, 0)), pl.BlockSpec((1, d), lambda i: (0, 0)),
                 pl.BlockSpec((1, d), lambda i: (0, 0))]
    return pl.pallas_call(
        body, grid=(t // tm,), in_specs=in_specs,
        out_specs=[pl.BlockSpec((tm, d), lambda i: (i, 0))] * 2,
        out_shape=[jax.ShapeDtypeStruct((t, d), f32)] * 2,
        compiler_params=_params(("parallel",)), name=name,
    )(*a_list, *w_list, xres, g, b)


def ln_bwd(z, dy, g, *, name):
    t, d = z.shape
    tm = _tile(t, 512, SUBLANES)

    def body(z_ref, dy_ref, g_ref, dz_ref, dzb_ref, dg_ref, db_ref):
        zz = z_ref[...]
        dy_ = dy_ref[...]
        mu = jnp.mean(zz, -1, keepdims=True)
        dd = zz - mu
        var = jnp.mean(dd * dd, -1, keepdims=True)
        rstd = lax.rsqrt(var + LN_EPS)
        xhat = dd * rstd
        dxh = dy_ * g_ref[...]
        dz = rstd * (dxh - jnp.mean(dxh, -1, keepdims=True) - xhat * jnp.mean(dxh * xhat, -1, keepdims=True))
        dz_ref[...] = dz
        dzb_ref[...] = dz.astype(bf16)
        pg = jnp.sum(dy_ * xhat, 0, keepdims=True)
        pb = jnp.sum(dy_, 0, keepdims=True)

        @pl.when(pl.program_id(0) == 0)
        def _():
            dg_ref[...] = pg
            db_ref[...] = pb

        @pl.when(pl.program_id(0) > 0)
        def _():
            dg_ref[...] += pg
            db_ref[...] += pb

    row = pl.BlockSpec((tm, d), lambda i: (i, 0))
    vec = pl.BlockSpec((1, d), lambda i: (0, 0))
    return pl.pallas_call(
        body, grid=(t // tm,), in_specs=[row, row, vec], out_specs=[row, row, vec, vec],
        out_shape=[jax.ShapeDtypeStruct((t, d), f32), jax.ShapeDtypeStruct((t, d), bf16),
                   jax.ShapeDtypeStruct((1, d), f32), jax.ShapeDtypeStruct((1, d), f32)],
        compiler_params=_params(("arbitrary",)), name=name,
    )(z, dy, g)


def loss_head(y, target, *, name):
    t, d = y.shape
    tm = _tile(t, 512, SUBLANES)

    def body(y_ref, t_ref, dy_ref, sq_ref):
        e = y_ref[...] - t_ref[...]
        dy_ref[...] = e * (1.0 / d)
        part = jnp.sum(e * e, 0, keepdims=True)

        @pl.when(pl.program_id(0) == 0)
        def _():
            sq_ref[...] = part

        @pl.when(pl.program_id(0) > 0)
        def _():
            sq_ref[...] += part

    row = pl.BlockSpec((tm, d), lambda i: (i, 0))
    vec = pl.BlockSpec((1, d), lambda i: (0, 0))
    return pl.pallas_call(
        body, grid=(t // tm,), in_specs=[row, row], out_specs=[row, vec],
        out_shape=[jax.ShapeDtypeStruct((t, d), f32), jax.ShapeDtypeStruct((1, d), f32)],
        compiler_params=_params(("arbitrary",)), name=name,
    )(y, target)


FFN_COL_BLOCK = 256
FFN_ROWS = 1024


def _lane_blocks(n):
    return [slice(s, min(s + FFN_COL_BLOCK, n)) for s in range(0, n, FFN_COL_BLOCK)]


def ffn_fwd(x, wg, wu, wd, g, b, *, name):
    t, d = x.shape
    nf, _, tf = wg.shape
    tm = _tile(t, FFN_ROWS, SUBLANES)

    def body(x_ref, wg_ref, wu_ref, wd_ref, g_ref, b_ref, y_ref, z_ref, yb_ref, acc_ref):
        f = pl.program_id(1)
        xb = x_ref[...].astype(bf16)
        part, pending = None, None
        for cols in _lane_blocks(tf):
            gate_up = (_bdot(xb, wg_ref[:, cols], NN), _bdot(xb, wu_ref[:, cols], NN), cols)
            if pending is not None:
                down = _bdot(_silu(pending[0]) * pending[1], wd_ref[pending[2], :], NN)
                part = down if part is None else part + down
            pending = gate_up
        down = _bdot(_silu(pending[0]) * pending[1], wd_ref[pending[2], :], NN)
        part = down if part is None else part + down

        @pl.when(f == 0)
        def _():
            acc_ref[...] = part

        @pl.when(f > 0)
        def _():
            acc_ref[...] += part

        @pl.when(f == nf - 1)
        def _():
            z = DN_ALPHA * x_ref[...] + 0.5 * acc_ref[...]
            z_ref[...] = z
            y = _layer_norm(z, g_ref[...], b_ref[...])
            y_ref[...] = y
            yb_ref[...] = y.astype(bf16)

    row = pl.BlockSpec((tm, d), lambda i, j: (i, 0))
    vec = pl.BlockSpec((1, d), lambda i, j: (0, 0))
    wcol = pl.BlockSpec((None, d, tf), lambda i, j: (j, 0, 0))
    wrow = pl.BlockSpec((None, tf, d), lambda i, j: (j, 0, 0))
    return pl.pallas_call(
        body, grid=(t // tm, nf),
        in_specs=[row, wcol, wcol, wrow, vec, vec],
        out_specs=[row, row, row],
        out_shape=[jax.ShapeDtypeStruct((t, d), f32)] * 2 + [jax.ShapeDtypeStruct((t, d), bf16)],
        scratch_shapes=[pltpu.VMEM((tm, d), f32)],
        compiler_params=_params(("parallel", "arbitrary")), name=name,
    )(x, wg, wu, wd, g, b)


def ffn_bwd_weights(xb, dzb, wg, wu, wd, *, name):
    t, d = xb.shape
    nf, _, tf = wg.shape
    tm = _tile(t, FFN_ROWS, SUBLANES)
    nt = t // tm

    def body(x_ref, dz_ref, wg_ref, wu_ref, wd_ref, dgate_ref, dup_ref, owg_ref, owu_ref, owd_ref,
             dwg_ref, dwu_ref, dwd_ref):
        x = x_ref[...]
        dzh = dz_ref[...] * 0.5

        def first_half(cols):
            return _bdot(x, wg_ref[:, cols], NN), _bdot(x, wu_ref[:, cols], NN), _bdot(dzh, wd_ref[cols, :], NT), cols

        def second_half(gate, up, dh, cols):
            sg = jax.nn.sigmoid(gate)
            s = gate * sg
            dup = (dh * s).astype(bf16)
            dgate = (dh * up * (sg * (1.0 + gate * (1.0 - sg)))).astype(bf16)
            dgate_ref[:, cols] = dgate
            dup_ref[:, cols] = dup
            return _bdot(x, dgate, TN), _bdot(x, dup, TN), _bdot(s * up, dzh, TN), cols

        parts, pending = [], None
        for cols in _lane_blocks(tf):
            nxt = first_half(cols)
            if pending is not None:
                parts.append(second_half(*pending))
            pending = nxt
        parts.append(second_half(*pending))

        @pl.when(pl.program_id(1) == 0)
        def _():
            for pwg, pwu, pwd, cols in parts:
                dwg_ref[:, cols] = pwg
                dwu_ref[:, cols] = pwu
                dwd_ref[cols, :] = pwd

        @pl.when(pl.program_id(1) > 0)
        def _():
            for pwg, pwu, pwd, cols in parts:
                dwg_ref[:, cols] += pwg
                dwu_ref[:, cols] += pwu
                dwd_ref[cols, :] += pwd

        @pl.when(pl.program_id(1) == nt - 1)
        def _():
            owg_ref[...] = dwg_ref[...].astype(bf16)
            owu_ref[...] = dwu_ref[...].astype(bf16)
            owd_ref[...] = dwd_ref[...].astype(bf16)

    row = pl.BlockSpec((tm, d), lambda j, i: (i, 0))
    wcol = pl.BlockSpec((None, d, tf), lambda j, i: (j, 0, 0))
    wrow = pl.BlockSpec((None, tf, d), lambda j, i: (j, 0, 0))
    act = pl.BlockSpec((None, tm, tf), lambda j, i: (j, i, 0))
    return pl.pallas_call(
        body, grid=(nf, nt), in_specs=[row, row, wcol, wcol, wrow], out_specs=[act, act, wcol, wcol, wrow],
        out_shape=[jax.ShapeDtypeStruct((nf, t, tf), bf16), jax.ShapeDtypeStruct((nf, t, tf), bf16),
                   jax.ShapeDtypeStruct((nf, d, tf), bf16), jax.ShapeDtypeStruct((nf, d, tf), bf16),
                   jax.ShapeDtypeStruct((nf, tf, d), bf16)],
        scratch_shapes=[pltpu.VMEM((d, tf), f32), pltpu.VMEM((d, tf), f32), pltpu.VMEM((tf, d), f32)],
        compiler_params=_params(("parallel", "arbitrary")), name=name,
    )(xb, dzb, wg, wu, wd)


def ffn_bwd_input(dgate, dup, wg, wu, dz, *, name):
    nf, t, tf = dgate.shape
    d = wg.shape[1]
    tm = _tile(t, FFN_ROWS // 2, SUBLANES)

    def body(dg_ref, du_ref, wg_ref, wu_ref, dz_ref, dx_ref):
        acc = DN_ALPHA * dz_ref[...]
        for j in range(nf):
            acc = acc + _bdot(dg_ref[j], wg_ref[j], NT) + _bdot(du_ref[j], wu_ref[j], NT)
        dx_ref[...] = acc

    act = pl.BlockSpec((nf, tm, tf), lambda i: (0, i, 0))
    wsp = pl.BlockSpec((nf, d, tf), lambda i: (0, 0, 0))
    row = pl.BlockSpec((tm, d), lambda i: (i, 0))
    return pl.pallas_call(
        body, grid=(t // tm,), in_specs=[act, act, wsp, wsp, row], out_specs=row,
        out_shape=jax.ShapeDtypeStruct((t, d), f32),
        compiler_params=_params(("parallel",)), name=name,
    )(dgate, dup, wg, wu, dz)


def ple_fwd(x, p, wg, bg, wp, *, name):
    t, d = x.shape
    dp = p.shape[1]
    tm = _tile(t, 512, SUBLANES)

    def body(x_ref, p_ref, wg_ref, bg_ref, wp_ref, o_ref):
        x_ = x_ref[...]
        gate = jax.nn.sigmoid(_bdot(x_, wg_ref[...], NN) + bg_ref[...])
        o_ref[...] = x_ + gate * _bdot(p_ref[...], wp_ref[...], NN)

    row = pl.BlockSpec((tm, d), lambda i: (i, 0))
    return pl.pallas_call(
        body, grid=(t // tm,),
        in_specs=[row, pl.BlockSpec((tm, dp), lambda i: (i, 0)), pl.BlockSpec((d, d), lambda i: (0, 0)),
                  pl.BlockSpec((1, d), lambda i: (0, 0)), pl.BlockSpec((dp, d), lambda i: (0, 0))],
        out_specs=row, out_shape=jax.ShapeDtypeStruct((t, d), f32),
        compiler_params=_params(("parallel",)), name=name,
    )(x, p, wg, bg, wp)


def ple_bwd(x, p, dy, wg, wgt, bg, wp, *, name):
    t, d = x.shape
    dp = p.shape[1]
    tm = _tile(t, 512, SUBLANES)

    def body(x_ref, p_ref, dy_ref, wg_ref, wgt_ref, bg_ref, wp_ref, dx_ref, dwg_ref, dbg_ref, dwp_ref):
        x_ = x_ref[...]
        dy_ = dy_ref[...]
        s = jax.nn.sigmoid(_bdot(x_, wg_ref[...], NN) + bg_ref[...])
        e = _bdot(p_ref[...], wp_ref[...], NN)
        da = dy_ * e * s * (1.0 - s)
        de = dy_ * s
        dx_ref[...] = dy_ + _bdot(da, wgt_ref[...], NN)
        pwg = _bdot(x_, da, TN)
        pbg = jnp.sum(da, 0, keepdims=True)
        pwp = _bdot(p_ref[...], de, TN)

        @pl.when(pl.program_id(0) == 0)
        def _():
            dwg_ref[...] = pwg
            dbg_ref[...] = pbg
            dwp_ref[...] = pwp

        @pl.when(pl.program_id(0) > 0)
        def _():
            dwg_ref[...] += pwg
            dbg_ref[...] += pbg
            dwp_ref[...] += pwp

    row = pl.BlockSpec((tm, d), lambda i: (i, 0))
    full = lambda shape: pl.BlockSpec(shape, lambda i: (0, 0))
    return pl.pallas_call(
        body, grid=(t // tm,),
        in_specs=[row, pl.BlockSpec((tm, dp), lambda i: (i, 0)), row, full((d, d)), full((d, d)), full((1, d)),
                  full((dp, d))],
        out_specs=[row, full((d, d)), full((1, d)), full((dp, d))],
        out_shape=[jax.ShapeDtypeStruct((t, d), f32), jax.ShapeDtypeStruct((d, d), f32),
                   jax.ShapeDtypeStruct((1, d), f32), jax.ShapeDtypeStruct((dp, d), f32)],
        compiler_params=_params(("arbitrary",)), name=name,
    )(x, p, dy, wg, wgt, bg, wp)


def _conv_taps(xpad_ref, w_ref, s):
    acc = w_ref[0:1, :] * xpad_ref[SUBLANES - 3:SUBLANES - 3 + s, :]
    for j in range(1, 4):
        acc = acc + w_ref[j:j + 1, :] * xpad_ref[SUBLANES - 3 + j:SUBLANES - 3 + j + s, :]
    return acc


def conv_fwd(x, w, bias, act, nb, *, name):
    t, c = x.shape
    s = t // nb
    cw = GROUP_W

    def body(x_ref, w_ref, b_ref, y_ref, xpad):
        xpad[0:SUBLANES, :] = jnp.zeros((SUBLANES, cw), f32)
        xpad[SUBLANES:, :] = x_ref[...]
        acc = _conv_taps(xpad, w_ref, s) + b_ref[...]
        y_ref[...] = _silu(acc) if act else acc

    slab = pl.BlockSpec((s, cw), lambda b, g: (b, g))
    return pl.pallas_call(
        body, grid=(nb, c // cw),
        in_specs=[slab, pl.BlockSpec((4, cw), lambda b, g: (0, g)), pl.BlockSpec((1, cw), lambda b, g: (0, g))],
        out_specs=slab, out_shape=jax.ShapeDtypeStruct((t, c), f32),
        scratch_shapes=[pltpu.VMEM((s + SUBLANES, cw), f32)],
        compiler_params=_params(("parallel", "parallel")), name=name,
    )(x, w, bias)


def conv_bwd(x, w, bias, dy, act, nb, *, name):
    t, c = x.shape
    s = t // nb
    cw = GROUP_W

    def body(x_ref, w_ref, b_ref, dy_ref, dx_ref, dw_ref, db_ref, xpad, dpad):
        xpad[0:SUBLANES, :] = jnp.zeros((SUBLANES, cw), f32)
        xpad[SUBLANES:, :] = x_ref[...]
        dacc = dy_ref[...]
        if act:
            acc = _conv_taps(xpad, w_ref, s) + b_ref[...]
            sg = jax.nn.sigmoid(acc)
            dacc = dacc * (sg * (1.0 + acc * (1.0 - sg)))
        dpad[0:s, :] = dacc
        dpad[s:, :] = jnp.zeros((SUBLANES, cw), f32)
        dx = w_ref[0:1, :] * dpad[3:3 + s, :]
        for j in range(1, 4):
            dx = dx + w_ref[j:j + 1, :] * dpad[3 - j:3 - j + s, :]
        dx_ref[...] = dx
        first = pl.program_id(1) == 0
        for j in range(4):
            pw = jnp.sum(dacc * xpad[SUBLANES - 3 + j:SUBLANES - 3 + j + s, :], 0, keepdims=True)

            @pl.when(first)
            def _():
                dw_ref[j:j + 1, :] = pw

            @pl.when(jnp.logical_not(first))
            def _():
                dw_ref[j:j + 1, :] += pw

        pb = jnp.sum(dacc, 0, keepdims=True)

        @pl.when(first)
        def _():
            db_ref[...] = pb

        @pl.when(jnp.logical_not(first))
        def _():
            db_ref[...] += pb

    slab = pl.BlockSpec((s, cw), lambda g, b: (b, g))
    wsp = pl.BlockSpec((4, cw), lambda g, b: (0, g))
    bsp = pl.BlockSpec((1, cw), lambda g, b: (0, g))
    return pl.pallas_call(
        body, grid=(c // cw, nb), in_specs=[slab, wsp, bsp, slab], out_specs=[slab, wsp, bsp],
        out_shape=[jax.ShapeDtypeStruct((t, c), f32), jax.ShapeDtypeStruct((4, c), f32),
                   jax.ShapeDtypeStruct((1, c), f32)],
        scratch_shapes=[pltpu.VMEM((s + SUBLANES, cw), f32), pltpu.VMEM((s + SUBLANES, cw), f32)],
        compiler_params=_params(("parallel", "arbitrary")), name=name,
    )(x, w, bias, dy)


def _each(f, *lists):
    return [f(*a) for a in zip(*lists)]


def _attn_heads(qs, kbs, vbs, sinks, valid, dist, dots):
    nn, nt = dots[:2]
    kv = [h // A_GROUP for h in range(A_HEADS)]
    scs = [nt(qs[h], kbs[kv[h]]) for h in range(A_HEADS)]
    prs = []
    for h in range(A_HEADS):
        sc = scs[h] * (A_HEAD_DIM ** -0.5) - 2.0 ** -(h + 1) * dist
        sc = jnp.where(valid, sc, NEG)
        m = lax.stop_gradient(jnp.maximum(jnp.max(sc, -1, keepdims=True), sinks[h]))
        pr = jnp.exp(sc - m)
        den = jnp.sum(pr, -1, keepdims=True) + jnp.exp(sinks[h] - m)
        prs.append(pr / den)
    return [nn(prs[h], vbs[kv[h]]) for h in range(A_HEADS)]


A_Q_ROWS = 2 * CHUNK


def _attn_band_consts(r0):
    band = A_WINDOW + A_Q_ROWS
    qi = lax.broadcasted_iota(jnp.int32, (A_Q_ROWS, band), 0)
    kj = lax.broadcasted_iota(jnp.int32, (A_Q_ROWS, band), 1)
    dist = jnp.abs(qi + A_WINDOW - kj).astype(f32)
    qc, kc = qi // CHUNK, kj // CHUNK
    valid = ((kj + r0) >= A_WINDOW) & (kc >= qc) & (kc <= qc + A_WINDOW // CHUNK)
    return dist, valid


def attn_fwd(qkv, sinks, nb, *, name):
    t = qkv.shape[0]
    s = t // nb
    band = A_WINDOW + A_Q_ROWS
    hd = A_HEAD_DIM

    def body(qkv_ref, sink_ref, o_ref, kvpad):
        kvpad[0:A_WINDOW, :] = jnp.zeros((A_WINDOW, 2 * A_KV_WIDTH), f32)
        kvpad[A_WINDOW:, :] = qkv_ref[:, A_WIDTH:]

        def chunk(n, carry):
            r0 = pl.multiple_of(n * A_Q_ROWS, A_Q_ROWS)
            dist, valid = _attn_band_consts(r0)
            kbs = [kvpad[pl.ds(r0, band), kvh * hd:(kvh + 1) * hd] for kvh in range(A_KV_HEADS)]
            vbs = [kvpad[pl.ds(r0, band), A_KV_WIDTH + kvh * hd:A_KV_WIDTH + (kvh + 1) * hd]
                   for kvh in range(A_KV_HEADS)]
            qs = [qkv_ref[pl.ds(r0, A_Q_ROWS), h * hd:(h + 1) * hd] for h in range(A_HEADS)]
            outs = _attn_heads(qs, kbs, vbs, [sink_ref[:, h:h + 1] for h in range(A_HEADS)], valid, dist, RAW_DOTS)
            for h in range(A_HEADS):
                o_ref[pl.ds(r0, A_Q_ROWS), h * hd:(h + 1) * hd] = outs[h]
            return carry

        lax.fori_loop(0, s // A_Q_ROWS, chunk, 0)

    return pl.pallas_call(
        body, grid=(nb,),
        in_specs=[pl.BlockSpec((s, A_WIDTH + 2 * A_KV_WIDTH), lambda b: (b, 0)),
                  pl.BlockSpec((1, A_HEADS), lambda b: (0, 0))],
        out_specs=pl.BlockSpec((s, A_WIDTH), lambda b: (b, 0)),
        out_shape=jax.ShapeDtypeStruct((t, A_WIDTH), f32),
        scratch_shapes=[pltpu.VMEM((s + A_WINDOW, 2 * A_KV_WIDTH), f32)],
        compiler_params=_params(("parallel",)), name=name,
    )(qkv, sinks)


def attn_bwd(qkv, sinks, do, nb, *, name):
    t = qkv.shape[0]
    s = t // nb
    band = A_WINDOW + A_Q_ROWS
    hd = A_HEAD_DIM
    kvw = 2 * A_KV_WIDTH

    def body(qkv_ref, sink_ref, do_ref, dqkv_ref, dsink_ref, kvpad, dkvpad):
        kvpad[0:A_WINDOW, :] = jnp.zeros((A_WINDOW, kvw), f32)
        kvpad[A_WINDOW:, :] = qkv_ref[:, A_WIDTH:]
        dkvpad[...] = jnp.zeros((s + A_WINDOW, kvw), f32)

        def chunk(n, dsinks):
            r0 = pl.multiple_of(n * A_Q_ROWS, A_Q_ROWS)
            dist, valid = _attn_band_consts(r0)
            ksl = [slice(kvh * hd, (kvh + 1) * hd) for kvh in range(A_KV_HEADS)]
            vsl = [slice(A_KV_WIDTH + kvh * hd, A_KV_WIDTH + (kvh + 1) * hd) for kvh in range(A_KV_HEADS)]
            kbs = [kvpad[pl.ds(r0, band), sl] for sl in ksl]
            vbs = [kvpad[pl.ds(r0, band), sl] for sl in vsl]
            dkbs = [dkvpad[pl.ds(r0, band), sl] for sl in ksl]
            dvbs = [dkvpad[pl.ds(r0, band), sl] for sl in vsl]
            qs = [qkv_ref[pl.ds(r0, A_Q_ROWS), h * hd:(h + 1) * hd] for h in range(A_HEADS)]
            dos = [do_ref[pl.ds(r0, A_Q_ROWS), h * hd:(h + 1) * hd] for h in range(A_HEADS)]
            fn = functools.partial(_attn_heads, valid=valid, dist=dist, dots=VJP_DOTS)
            _, vjp = jax.vjp(fn, qs, kbs, vbs, [sink_ref[:, h:h + 1] for h in range(A_HEADS)])
            dqs, dks, dvs, dss = vjp(dos)
            for h in range(A_HEADS):
                dqkv_ref[pl.ds(r0, A_Q_ROWS), h * hd:(h + 1) * hd] = dqs[h]
            for kvh in range(A_KV_HEADS):
                dkvpad[pl.ds(r0, band), ksl[kvh]] = dkbs[kvh] + dks[kvh]
                dkvpad[pl.ds(r0, band), vsl[kvh]] = dvbs[kvh] + dvs[kvh]
            return tuple(dsinks[h] + dss[h] for h in range(A_HEADS))

        dsinks = lax.fori_loop(0, s // A_Q_ROWS, chunk, tuple(jnp.zeros((1, 1), f32) for _ in range(A_HEADS)))
        dqkv_ref[:, A_WIDTH:] = dkvpad[A_WINDOW:, :]
        first = pl.program_id(0) == 0
        for h in range(A_HEADS):
            @pl.when(first)
            def _():
                dsink_ref[:, h:h + 1] = dsinks[h]

            @pl.when(jnp.logical_not(first))
            def _():
                dsink_ref[:, h:h + 1] += dsinks[h]

    wq = A_WIDTH + kvw
    return pl.pallas_call(
        body, grid=(nb,),
        in_specs=[pl.BlockSpec((s, wq), lambda b: (b, 0)), pl.BlockSpec((1, A_HEADS), lambda b: (0, 0)),
                  pl.BlockSpec((s, A_WIDTH), lambda b: (b, 0))],
        out_specs=[pl.BlockSpec((s, wq), lambda b: (b, 0)), pl.BlockSpec((1, A_HEADS), lambda b: (0, 0))],
        out_shape=[jax.ShapeDtypeStruct((t, wq), f32), jax.ShapeDtypeStruct((1, A_HEADS), f32)],
        scratch_shapes=[pltpu.VMEM((s + A_WINDOW, kvw), f32), pltpu.VMEM((s + A_WINDOW, kvw), f32)],
        compiler_params=_params(("arbitrary",)), name=name,
    )(qkv, sinks, do)


def _rg_gates(xc, wa, wx, ba, bx, lam, nn):
    r = jax.nn.sigmoid(nn(xc, wa) + ba)
    i = jax.nn.sigmoid(nn(xc, wx) + bx)
    log_a = -RG_C * r * jax.nn.softplus(-lam)
    a = jnp.exp(log_a)
    mult = jnp.sqrt(-jnp.tanh(log_a) * (jnp.exp(2.0 * log_a) + 1.0))
    return a, mult * (i * xc)


def _linear_scan(a, u, reverse):
    s = a.shape[0]
    t = lax.broadcasted_iota(jnp.int32, a.shape, 0)
    d = 1
    while d < s:
        if reverse:
            keep = t < s - d
            shift = s - d
        else:
            keep = t >= d
            shift = d
        us = jnp.where(keep, pltpu.roll(u, shift, 0), 0.0)
        as_ = jnp.where(keep, pltpu.roll(a, shift, 0), 1.0)
        u = u + a * us
        a = a * as_
        d *= 2
    return u


def rglru_fwd(xc, bg, wa, wx, ba, bx, lam, nb, *, name):
    t, c = xc.shape
    s = t // nb
    cw = GROUP_W

    def body(xc_ref, bg_ref, wa_ref, wx_ref, ba_ref, bx_ref, lam_ref, y_ref, h_ref):
        a, u = _rg_gates(xc_ref[...], wa_ref[...], wx_ref[...], ba_ref[...], bx_ref[...], lam_ref[...], RAW_DOTS[0])
        h = _linear_scan(a, u, False)
        h_ref[...] = h
        y_ref[...] = h * jax.nn.gelu(bg_ref[...])

    slab = pl.BlockSpec((s, cw), lambda b, g: (b, g))
    wsp = pl.BlockSpec((None, cw, cw), lambda b, g: (g, 0, 0))
    vec = pl.BlockSpec((1, cw), lambda b, g: (0, g))
    return pl.pallas_call(
        body, grid=(nb, c // cw), in_specs=[slab, slab, wsp, wsp, vec, vec, vec], out_specs=[slab, slab],
        out_shape=[jax.ShapeDtypeStruct((t, c), f32)] * 2,
        compiler_params=_params(("parallel", "parallel")), name=name,
    )(xc, bg, wa, wx, ba, bx, lam)


def rglru_bwd(xc, bg, h, dy, wa, wx, ba, bx, lam, nb, *, name):
    t, c = xc.shape
    s = t // nb
    cw = GROUP_W

    def body(xc_ref, bg_ref, h_ref, dy_ref, wa_ref, wx_ref, ba_ref, bx_ref, lam_ref,
             dxc_ref, dbg_ref, dwa_ref, dwx_ref, dba_ref, dbx_ref, dlam_ref):
        h = h_ref[...]
        dy_ = dy_ref[...]
        gel, gel_vjp = jax.vjp(jax.nn.gelu, bg_ref[...])
        dbg_ref[...] = gel_vjp(dy_ * h)[0]
        dh = dy_ * gel
        gates = functools.partial(_rg_gates, nn=_bnn)
        (a, _), gates_vjp = jax.vjp(gates, xc_ref[...], wa_ref[...], wx_ref[...], ba_ref[...], bx_ref[...],
                                    lam_ref[...])
        ti = lax.broadcasted_iota(jnp.int32, a.shape, 0)
        a_next = jnp.where(ti < s - 1, pltpu.roll(a, s - 1, 0), 0.0)
        lam_t = _linear_scan(a_next, dh, True)
        h_prev = jnp.where(ti >= 1, pltpu.roll(h, 1, 0), 0.0)
        dxc, dwa, dwx, dba, dbx, dlam = gates_vjp((lam_t * h_prev, lam_t))
        dxc_ref[...] = dxc
        first = pl.program_id(1) == 0

        @pl.when(first)
        def _():
            dwa_ref[...] = dwa
            dwx_ref[...] = dwx
            dba_ref[...] = dba
            dbx_ref[...] = dbx
            dlam_ref[...] = dlam

        @pl.when(jnp.logical_not(first))
        def _():
            dwa_ref[...] += dwa
            dwx_ref[...] += dwx
            dba_ref[...] += dba
            dbx_ref[...] += dbx
            dlam_ref[...] += dlam

    slab = pl.BlockSpec((s, cw), lambda g, b: (b, g))
    wsp = pl.BlockSpec((None, cw, cw), lambda g, b: (g, 0, 0))
    vec = pl.BlockSpec((1, cw), lambda g, b: (0, g))
    ng = c // cw
    return pl.pallas_call(
        body, grid=(ng, nb), in_specs=[slab, slab, slab, slab, wsp, wsp, vec, vec, vec],
        out_specs=[slab, slab, wsp, wsp, vec, vec, vec],
        out_shape=[jax.ShapeDtypeStruct((t, c), f32), jax.ShapeDtypeStruct((t, c), f32),
                   jax.ShapeDtypeStruct((ng, cw, cw), f32), jax.ShapeDtypeStruct((ng, cw, cw), f32),
                   jax.ShapeDtypeStruct((1, c), f32), jax.ShapeDtypeStruct((1, c), f32),
                   jax.ShapeDtypeStruct((1, c), f32)],
        compiler_params=_params(("parallel", "arbitrary")), name=name,
    )(xc, bg, h, dy, wa, wx, ba, bx, lam)


def _gdn_chunks_prep(qs, ks, vs, bls, als, a_log, dt_b, dots):
    nn, nt, csum = dots[0], dots[1], dots[3]
    hd = C_HEAD_DIM
    ri = lax.broadcasted_iota(jnp.int32, (CHUNK, CHUNK), 0)
    ci = lax.broadcasted_iota(jnp.int32, (CHUNK, CHUNK), 1)
    tril = ri >= ci
    strict = ri > ci
    eye = (ri == ci).astype(f32)
    qn = [q * lax.rsqrt(jnp.sum(q * q, -1, keepdims=True) + NORM_EPS) * (hd ** -0.5) for q in qs]
    kn = [k * lax.rsqrt(jnp.sum(k * k, -1, keepdims=True) + NORM_EPS) for k in ks]
    beta = [jax.nn.sigmoid(bl) for bl in bls]
    g = [-jnp.exp(a_log) * jax.nn.softplus(al + dt_b) for al in als]
    gc_sq = [csum(jnp.broadcast_to(g_, (CHUNK, CHUNK))) for g_ in g]
    gc = [csum(jnp.broadcast_to(g_, (CHUNK, hd))) for g_ in g]
    decay = [jnp.where(tril, jnp.exp(jnp.where(tril, s - s.T, 0.0)), 0.0) for s in gc_sq]
    kb = _each(jnp.multiply, kn, beta)
    kk = _each(nt, kb, kn)
    pw = [-jnp.where(strict, a * d, 0.0) for a, d in zip(kk, decay)]
    inv = [eye + p_ for p_ in pw]
    for _ in range(5):
        pw = _each(nn, pw, pw)
        inv = _each(jnp.add, inv, _each(nn, inv, pw))
    egc = [jnp.exp(c_) for c_ in gc]
    u = _each(nn, inv, _each(jnp.multiply, vs, beta))
    w = _each(nn, inv, _each(jnp.multiply, kb, egc))
    attn = _each(jnp.multiply, _each(nt, qn, kn), decay)
    g_last = [jnp.sum(jnp.broadcast_to(g_, (CHUNK, hd)), 0, keepdims=True) for g_ in g]
    qg = _each(jnp.multiply, qn, egc)
    kdec = [k_ * jnp.exp(gl_ - c_) for k_, gl_, c_ in zip(kn, g_last, gc)]
    return [(qg[i], kdec[i], w[i], u[i], attn[i], jnp.exp(g_last[i])) for i in range(len(qs))]


def _gdn_heads_step(states, qgs, kdecs, ws, us, attns, gls, zs, ng, dots):
    nn, tn = dots[0], dots[2]
    v_new = _each(jnp.subtract, us, _each(nn, ws, states))
    o = _each(jnp.add, _each(nn, qgs, states), _each(nn, attns, v_new))
    new = [s * gl for s, gl in zip(states, gls)]
    new = _each(jnp.add, new, _each(tn, kdecs, v_new))
    y = [o_ * lax.rsqrt(jnp.mean(o_ * o_, -1, keepdims=True) + NORM_EPS) * ng * _silu(z) for o_, z in zip(o, zs)]
    return y, new


def _loop_unrolled(n, unroll, load, compute, store, init):
    u = unroll if n % unroll == 0 else 1

    def trip(i, carry):
        idx = [i * u + j for j in range(u)]
        loaded = [load(k) for k in idx]
        results = compute(loaded)
        for k, r in zip(idx, results):
            carry = store(k, r, carry)
        return carry

    return lax.fori_loop(0, n // u, trip, init)


def _pick_lane(x, lane):
    li = lax.broadcasted_iota(jnp.int32, x.shape, 1)
    return jnp.sum(jnp.where(li == lane, x, 0.0), 1, keepdims=True)


def _put_lane(col, lane, width):
    li = lax.broadcasted_iota(jnp.int32, (col.shape[0], width), 1)
    return jnp.where(li == lane, col, 0.0)


def _gdn_specs(s, nc):
    hd = C_HEAD_DIM
    head = lambda off: pl.BlockSpec((s, hd), lambda b, h, off=off: (b, off + h))
    attn = pl.BlockSpec((None, s, CHUNK), lambda b, h: (h, b, 0))
    gl = pl.BlockSpec((None, nc * SUBLANES, hd), lambda b, h: (h, b, 0))
    ba = pl.BlockSpec((s, LANES), lambda b, h: (b, 0))
    sc8 = pl.BlockSpec((1, C_HEADS), lambda b, h: (0, 0))
    return head, attn, gl, ba, sc8


def gdn_prep_fwd(qkv, ba, a_log, dt_b, nb, *, name):
    t = qkv.shape[0]
    s = t // nb
    nc = s // CHUNK
    hd = C_HEAD_DIM
    head, attn_sp, gl_sp, ba_sp, sc8 = _gdn_specs(s, nc)

    def body(q_ref, k_ref, v_ref, ba_ref, alog_ref, dtb_ref, qg_ref, kd_ref, w_ref, u_ref, at_ref, gl_ref):
        h = pl.program_id(1)
        a_log_h = _pick_lane(alog_ref[...], h)
        dt_b_h = _pick_lane(dtb_ref[...], h)

        def load(n):
            rows = pl.ds(pl.multiple_of(n * CHUNK, CHUNK), CHUNK)
            bav = ba_ref[rows, :]
            return q_ref[rows, :], k_ref[rows, :], v_ref[rows, :], _pick_lane(bav, h), _pick_lane(bav, C_HEADS + h)

        def compute(loaded):
            return _gdn_chunks_prep(*[list(x) for x in zip(*loaded)], a_log_h, dt_b_h, RAW_DOTS)

        def store(n, outs, carry):
            rows = pl.ds(pl.multiple_of(n * CHUNK, CHUNK), CHUNK)
            qg_ref[rows, :] = outs[0].astype(bf16)
            kd_ref[rows, :] = outs[1].astype(bf16)
            w_ref[rows, :] = outs[2].astype(bf16)
            u_ref[rows, :] = outs[3]
            at_ref[rows, :] = outs[4].astype(bf16)
            gl_ref[pl.ds(pl.multiple_of(n * SUBLANES, SUBLANES), SUBLANES), :] = jnp.broadcast_to(outs[5], (SUBLANES, hd))
            return carry

        _loop_unrolled(nc, PREP_FWD_UNROLL, load, compute, store, 0)

    big = jax.ShapeDtypeStruct((t, C_WIDTH), f32)
    bigb = jax.ShapeDtypeStruct((t, C_WIDTH), bf16)
    return pl.pallas_call(
        body, grid=(nb, C_HEADS),
        in_specs=[head(0), head(C_HEADS), head(2 * C_HEADS), ba_sp, sc8, sc8],
        out_specs=[head(0)] * 4 + [attn_sp, gl_sp],
        out_shape=[bigb, bigb, bigb, big, jax.ShapeDtypeStruct((C_HEADS, t, CHUNK), bf16),
                               jax.ShapeDtypeStruct((C_HEADS, nb * nc * SUBLANES, hd), f32)],
        compiler_params=_params(("parallel", "parallel")), name=name,
    )(qkv, qkv, qkv, ba, a_log, dt_b)


def gdn_prep_bwd(qkv, ba, a_log, dt_b, cts, nb, *, name):
    t = qkv.shape[0]
    s = t // nb
    nc = s // CHUNK
    hd = C_HEAD_DIM
    head, attn_sp, gl_sp, ba_sp, sc8 = _gdn_specs(s, nc)

    def body(q_ref, k_ref, v_ref, ba_ref, alog_ref, dtb_ref, cqg, ckd, cw_, cu, cat, cgl,
             dq_ref, dk_ref, dv_ref, dba_ref, dalog_ref, ddtb_ref):
        b = pl.program_id(0)
        h = pl.program_id(1)
        a_log_h = _pick_lane(alog_ref[...], h)
        dt_b_h = _pick_lane(dtb_ref[...], h)
        prep = functools.partial(_gdn_chunks_prep, dots=VJP_DOTS)

        @pl.when(h == 0)
        def _():
            dba_ref[...] = jnp.zeros((s, LANES), f32)

        def load(n):
            rows = pl.ds(pl.multiple_of(n * CHUNK, CHUNK), CHUNK)
            bav = ba_ref[rows, :]
            cgl_n = cgl[pl.ds(pl.multiple_of(n * SUBLANES, SUBLANES), SUBLANES), :][0:1, :]
            primals = (q_ref[rows, :], k_ref[rows, :], v_ref[rows, :], _pick_lane(bav, h), _pick_lane(bav, C_HEADS + h))
            return primals, (cqg[rows, :], ckd[rows, :], cw_[rows, :], cu[rows, :], cat[rows, :], cgl_n), dba_ref[rows, :]

        def compute(loaded):
            primals = [list(x) for x in zip(*[item[0] for item in loaded])]
            _, vjp = jax.vjp(prep, *primals, a_log_h, dt_b_h)
            dqs, dks, dvs, dbls, dals, dalog, ddtb = vjp([item[1] for item in loaded])
            zero = jnp.zeros((1, 1), f32)
            return [((dqs[i], dks[i], dvs[i], dbls[i], dals[i], dalog if i == 0 else zero, ddtb if i == 0 else zero),
                     loaded[i][2]) for i in range(len(loaded))]

        def store(n, res, carry):
            (dq, dk, dv, dbl, dal, dalog_n, ddtb_n), dba_old = res
            rows = pl.ds(pl.multiple_of(n * CHUNK, CHUNK), CHUNK)
            dq_ref[rows, :] = dq
            dk_ref[rows, :] = dk
            dv_ref[rows, :] = dv
            dba_ref[rows, :] = dba_old + _put_lane(dbl, h, LANES) + _put_lane(dal, C_HEADS + h, LANES)
            return carry[0] + dalog_n, carry[1] + ddtb_n

        da_log, ddt_b = _loop_unrolled(nc, PREP_BWD_UNROLL, load, compute, store,
                                       (jnp.zeros((1, 1), f32), jnp.zeros((1, 1), f32)))
        first = jnp.logical_and(b == 0, h == 0)

        @pl.when(first)
        def _():
            dalog_ref[...] = _put_lane(da_log, h, LANES)
            ddtb_ref[...] = _put_lane(ddt_b, h, LANES)

        @pl.when(jnp.logical_not(first))
        def _():
            dalog_ref[...] += _put_lane(da_log, h, LANES)
            ddtb_ref[...] += _put_lane(ddt_b, h, LANES)

    big = jax.ShapeDtypeStruct((t, C_WIDTH), f32)
    vec = pl.BlockSpec((1, LANES), lambda b, h: (0, 0))
    return pl.pallas_call(
        body, grid=(nb, C_HEADS),
        in_specs=[head(0), head(C_HEADS), head(2 * C_HEADS), ba_sp, sc8, sc8] + [head(0)] * 4 + [attn_sp, gl_sp],
        out_specs=[head(0)] * 3 + [ba_sp, vec, vec],
        out_shape=[big] * 3 + [jax.ShapeDtypeStruct((t, LANES), f32), jax.ShapeDtypeStruct((1, LANES), f32),
                               jax.ShapeDtypeStruct((1, LANES), f32)],
        compiler_params=_params(("arbitrary", "arbitrary")), name=name,
    )(qkv, qkv, qkv, ba, a_log, dt_b, *cts)


def _gdn_rec_specs(sb, nsb, hp, reverse):
    hd = C_HEAD_DIM
    ncb = sb // CHUNK
    blk = (lambda b, k: b * nsb + (nsb - 1 - k)) if reverse else (lambda b, k: b * nsb + k)
    wide = pl.BlockSpec((sb, hp * hd), lambda b, j, k: (blk(b, k), j))
    attn = pl.BlockSpec((hp, sb, CHUNK), lambda b, j, k: (j, blk(b, k), 0))
    gl = pl.BlockSpec((hp, ncb * SUBLANES, hd), lambda b, j, k: (j, blk(b, k), 0))
    ng = pl.BlockSpec((1, hd), lambda b, j, k: (0, 0))
    states = pl.BlockSpec((hp, ncb, hd, hd), lambda b, j, k: (j, blk(b, k), 0, 0))
    return wide, attn, gl, ng, states


def gdn_rec_fwd(qg, kdec, w, u, attn, gl, z, ng, nb, *, name):
    t = qg.shape[0]
    s = t // nb
    sb = min(s, GDN_TIME_BLOCK)
    nsb = s // sb
    hd = C_HEAD_DIM
    hp = C_HEADS_PER_STEP
    wide, attn_sp, gl_sp, ng_sp, st_sp = _gdn_rec_specs(sb, nsb, hp, False)

    def body(qg_ref, kd_ref, w_ref, u_ref, at_ref, gl_ref, z_ref, ng_ref, y_ref, st_ref, carry_ref):
        @pl.when(pl.program_id(2) == 0)
        def _():
            carry_ref[...] = jnp.zeros((hp, hd, hd), f32)

        def chunk(n, states):
            for j in range(hp):
                st_ref[j, n] = states[j]
            rows = pl.ds(pl.multiple_of(n * CHUNK, CHUNK), CHUNK)
            grow = pl.ds(pl.multiple_of(n * SUBLANES, SUBLANES), SUBLANES)
            cols = [slice(j * hd, (j + 1) * hd) for j in range(hp)]
            ins = [(qg_ref[rows, c], kd_ref[rows, c], w_ref[rows, c], u_ref[rows, c], at_ref[j, rows, :],
                    gl_ref[j, grow, :][0:1, :], z_ref[rows, c]) for j, c in enumerate(cols)]
            ys, new = _gdn_heads_step(list(states), *[list(x) for x in zip(*ins)], ng_ref[...], RAW_DOTS)
            for j in range(hp):
                y_ref[rows, cols[j]] = ys[j]
            return tuple(new)

        last = lax.fori_loop(0, sb // CHUNK, chunk, tuple(carry_ref[j] for j in range(hp)))
        for j in range(hp):
            carry_ref[j] = last[j]

    return pl.pallas_call(
        body, grid=(nb, C_HEADS // hp, nsb),
        in_specs=[wide] * 4 + [attn_sp, gl_sp, wide, ng_sp], out_specs=[wide, st_sp],
        out_shape=[jax.ShapeDtypeStruct((t, C_WIDTH), f32), jax.ShapeDtypeStruct((C_HEADS, t // CHUNK, hd, hd), f32)],
        scratch_shapes=[pltpu.VMEM((hp, hd, hd), f32)],
        compiler_params=_params(("parallel", "parallel", "arbitrary")), name=name,
    )(qg, kdec, w, u, attn, gl, z, ng)


def gdn_rec_bwd(qg, kdec, w, u, attn, gl, z, ng, states, dy, nb, *, name):
    t = qg.shape[0]
    s = t // nb
    sb = min(s, GDN_TIME_BLOCK)
    nsb = s // sb
    nc = sb // CHUNK
    hd = C_HEAD_DIM
    hp = C_HEADS_PER_STEP
    wide, attn_sp, gl_sp, ng_sp, st_sp = _gdn_rec_specs(sb, nsb, hp, True)

    def body(qg_ref, kd_ref, w_ref, u_ref, at_ref, gl_ref, z_ref, ng_ref, states, dy_ref,
             dqg_ref, dkd_ref, dw_ref, du_ref, dat_ref, dgl_ref, dz_ref, dng_ref, carry_ref):
        step = functools.partial(_gdn_heads_step, dots=VJP_DOTS)

        @pl.when(pl.program_id(2) == 0)
        def _():
            carry_ref[...] = jnp.zeros((hp, hd, hd), f32)

        def operands(n):
            rows = pl.ds(pl.multiple_of(n * CHUNK, CHUNK), CHUNK)
            grow = pl.ds(pl.multiple_of(n * SUBLANES, SUBLANES), SUBLANES)
            cols = [slice(j * hd, (j + 1) * hd) for j in range(hp)]
            return ([qg_ref[rows, c].astype(f32) for c in cols], [kd_ref[rows, c].astype(f32) for c in cols],
                    [w_ref[rows, c].astype(f32) for c in cols], [u_ref[rows, c] for c in cols],
                    [at_ref[j, rows, :].astype(f32) for j in range(hp)],
                    [gl_ref[j, grow, :][0:1, :] for j in range(hp)], [z_ref[rows, c] for c in cols])

        def bwd_chunk(i, carry):
            n = nc - 1 - i
            rows = pl.ds(pl.multiple_of(n * CHUNK, CHUNK), CHUNK)
            grow = pl.ds(pl.multiple_of(n * SUBLANES, SUBLANES), SUBLANES)
            dsts, dng = carry
            dys = [dy_ref[rows, j * hd:(j + 1) * hd] for j in range(hp)]
            _, vjp = jax.vjp(step, [states[j, n] for j in range(hp)], *operands(n), ng_ref[...])
            dst, dqg, dkd, dw, du, dat, dgl, dz, dng_n = vjp((dys, list(dsts)))
            for j in range(hp):
                cols = slice(j * hd, (j + 1) * hd)
                dqg_ref[rows, cols] = dqg[j]
                dkd_ref[rows, cols] = dkd[j]
                dw_ref[rows, cols] = dw[j]
                du_ref[rows, cols] = du[j]
                dat_ref[j, rows, :] = dat[j]
                dgl_ref[j, grow, :] = jnp.broadcast_to(dgl[j], (SUBLANES, hd))
                dz_ref[rows, cols] = dz[j]
            return tuple(dst), dng + dng_n

        dlast, dng = lax.fori_loop(0, nc, bwd_chunk,
                                   (tuple(carry_ref[j] for j in range(hp)), jnp.zeros((1, hd), f32)))
        for j in range(hp):
            carry_ref[j] = dlast[j]
        first = jnp.logical_and(jnp.logical_and(pl.program_id(0) == 0, pl.program_id(1) == 0), pl.program_id(2) == 0)

        @pl.when(first)
        def _():
            dng_ref[...] = dng

        @pl.when(jnp.logical_not(first))
        def _():
            dng_ref[...] += dng

    big = jax.ShapeDtypeStruct((t, C_WIDTH), f32)
    return pl.pallas_call(
        body, grid=(nb, C_HEADS // hp, nsb),
        in_specs=[wide] * 4 + [attn_sp, gl_sp, wide, ng_sp, st_sp, wide],
        out_specs=[wide] * 4 + [attn_sp, gl_sp, wide, ng_sp],
        out_shape=[big] * 4 + [jax.ShapeDtypeStruct(attn.shape, f32), jax.ShapeDtypeStruct(gl.shape, f32), big,
                               jax.ShapeDtypeStruct((1, hd), f32)],
        scratch_shapes=[pltpu.VMEM((hp, hd, hd), f32)],
        compiler_params=_params(("arbitrary", "arbitrary", "arbitrary")), name=name,
    )(qg, kdec, w, u, attn, gl, z, ng, states, dy)


def _blockdiag_slabs(w):
    per = GROUP_W // B_BLOCK
    slabs = jnp.zeros((B_BLOCKS // per, GROUP_W, GROUP_W), w.dtype)
    for h in range(B_BLOCKS):
        o = (h % per) * B_BLOCK
        slabs = slabs.at[h // per, o:o + B_BLOCK, o:o + B_BLOCK].set(w[h])
    return slabs


def _slab_blocks(slabs):
    per = GROUP_W // B_BLOCK
    return jnp.stack([slabs[h // per, (h % per) * B_BLOCK:(h % per + 1) * B_BLOCK,
                            (h % per) * B_BLOCK:(h % per + 1) * B_BLOCK] for h in range(B_BLOCKS)])


def _mixer_ab_fwd(x1, x1b, W, g, b, nb, tag):
    w_in = W["ab_w_in"][0].astype(bf16)
    o1, o2 = A_WIDTH + 2 * A_KV_WIDTH, A_WIDTH + 2 * A_KV_WIDTH + B_WIDTH
    w_qkv, w_bx, w_bg = w_in[:, :o1], w_in[:, o1:o2], w_in[:, o2:]
    pqkv = mm_nn(x1b,w_qkv, name=tag + "_in_qkv")
    pbx = mm_nn(x1b,w_bx, name=tag + "_in_bx")
    pbg = mm_nn(x1b,w_bg, name=tag + "_in_bg")
    ya = attn_fwd(pqkv, W["a_sinks"], nb, name=tag + "_attn_fwd")
    xc = conv_fwd(pbx, W["b_conv_w"][0], W["b_conv_b"], False, nb, name=tag + "_conv_fwd")
    wa_s, wx_s = _blockdiag_slabs(W["b_wa"][0]), _blockdiag_slabs(W["b_wx"][0])
    yb, hh = rglru_fwd(xc, pbg, wa_s, wx_s, W["b_ba"], W["b_bx"], W["b_lam"], nb, name=tag + "_rglru_fwd")
    w_out = W["ab_w_out"][0].astype(bf16)
    x2, z1 = proj_ln([ya, yb], [w_out[:A_WIDTH], w_out[A_WIDTH:]], x1, g, b, name=tag + "_out_ln")
    saved = (pqkv, pbx, pbg, ya, xc, yb, hh, wa_s, wx_s, w_qkv, w_bx, w_bg, w_out)
    return x2, z1, saved


def _mixer_ab_bwd(x1b, dz1, dz1b, W, saved, nb, tag):
    pqkv, pbx, pbg, ya, xc, yb, hh, wa_s, wx_s, w_qkv, w_bx, w_bg, w_out = saved
    dya = mm_nn(dz1b, w_out[:A_WIDTH].T, name=tag + "_dya")
    dyb = mm_nn(dz1b, w_out[A_WIDTH:].T, name=tag + "_dyb")
    dwo = jnp.concatenate([mm_tn(ya, dz1b, name=tag + "_dwo_a"), mm_tn(yb, dz1b, name=tag + "_dwo_b")], 0)
    dpqkv, dsinks = attn_bwd(pqkv, W["a_sinks"], dya, nb, name=tag + "_attn_bwd")
    dxc, dpbg, dwa_s, dwx_s, dba, dbx, dlam = rglru_bwd(xc, pbg, hh, dyb, wa_s, wx_s, W["b_ba"], W["b_bx"],
                                                       W["b_lam"], nb, name=tag + "_rglru_bwd")
    dpbx, dconv_w, dconv_b = conv_bwd(pbx, W["b_conv_w"][0], W["b_conv_b"], dxc, False, nb, name=tag + "_conv_bwd")
    dw_in = jnp.concatenate([mm_tn(x1b,dpqkv, name=tag + "_dwin_qkv"), mm_tn(x1b,dpbx, name=tag + "_dwin_bx"),
                             mm_tn(x1b,dpbg, name=tag + "_dwin_bg")], 1)
    dx1 = mm_nn(dpqkv, w_qkv.T, add=dz1, add_scale=DN_ALPHA, name=tag + "_dx_qkv")
    dx1 = mm_nn(dpbx, w_bx.T, add=dx1, name=tag + "_dx_bx")
    dx1 = mm_nn(dpbg, w_bg.T, add=dx1, name=tag + "_dx_bg")
    grads = {"ab_w_in": dw_in[None], "a_sinks": dsinks, "b_conv_w": dconv_w[None], "b_conv_b": dconv_b,
             "b_wa": _slab_blocks(dwa_s)[None], "b_ba": dba, "b_wx": _slab_blocks(dwx_s)[None], "b_bx": dbx,
             "b_lam": dlam, "ab_w_out": dwo[None]}
    return dx1, grads


def _mixer_c_fwd(x1, x1b, W, g, b, nb, tag):
    w_in = W["c_w_in"][0].astype(bf16)
    d = w_in.shape[0]
    o1, o2 = 3 * C_WIDTH, 4 * C_WIDTH
    w_qkv, w_z = w_in[:, :o1], w_in[:, o1:o2]
    w_ba = jnp.concatenate([w_in[:, o2:], jnp.zeros((d, LANES - 2 * C_HEADS), bf16)], 1)
    pqkv = mm_nn(x1b,w_qkv, name=tag + "_in_qkv")
    pz = mm_nn(x1b,w_z, name=tag + "_in_z")
    pba = mm_nn(x1b,w_ba, name=tag + "_in_ba")
    zero_b = jnp.zeros((1, o1), f32)
    qkvc = conv_fwd(pqkv, W["c_conv_w"][0], zero_b, True, nb, name=tag + "_conv_fwd")
    prep = gdn_prep_fwd(qkvc, pba, W["c_a_log"], W["c_dt_bias"], nb, name=tag + "_prep_fwd")
    yc, states = gdn_rec_fwd(*prep, pz, W["c_norm_g"], nb, name=tag + "_rec_fwd")
    w_out = W["c_w_out"][0].astype(bf16)
    x2, z1 = proj_ln([yc], [w_out], x1, g, b, name=tag + "_out_ln")
    saved = (pqkv, pz, pba, qkvc, prep, states, yc, w_qkv, w_z, w_ba, w_out, zero_b)
    return x2, z1, saved


def _mixer_c_bwd(x1b, dz1, dz1b, W, saved, nb, tag):
    pqkv, pz, pba, qkvc, prep, states, yc, w_qkv, w_z, w_ba, w_out, zero_b = saved
    dyc = mm_nn(dz1b, w_out.T, name=tag + "_dyc")
    dwo = mm_tn(yc, dz1b, name=tag + "_dwo")
    rec = gdn_rec_bwd(*prep, pz, W["c_norm_g"], states, dyc, nb, name=tag + "_rec_bwd")
    cts, dpz, dng = rec[:6], rec[6], rec[7]
    dq, dk, dv, dpba, dalog, ddtb = gdn_prep_bwd(qkvc, pba, W["c_a_log"], W["c_dt_bias"], cts, nb,
                                                 name=tag + "_prep_bwd")
    dqkvc = jnp.concatenate([dq, dk, dv], 1)
    dpqkv, dconv_w, _ = conv_bwd(pqkv, W["c_conv_w"][0], zero_b, dqkvc, True, nb, name=tag + "_conv_bwd")
    dw_in = jnp.concatenate([mm_tn(x1b,dpqkv, name=tag + "_dwin_qkv"), mm_tn(x1b,dpz, name=tag + "_dwin_z"),
                             mm_tn(x1b,dpba, name=tag + "_dwin_ba")[:, :2 * C_HEADS]], 1)
    dx1 = mm_nn(dpqkv, w_qkv.T, add=dz1, add_scale=DN_ALPHA, name=tag + "_dx_qkv")
    dx1 = mm_nn(dpz, w_z.T, add=dx1, name=tag + "_dx_z")
    dx1 = mm_nn(dpba, w_ba.T, add=dx1, name=tag + "_dx_ba")
    grads = {"c_w_in": dw_in[None], "c_conv_w": dconv_w[None], "c_a_log": dalog[:, :C_HEADS],
             "c_dt_bias": ddtb[:, :C_HEADS], "c_norm_g": dng, "c_w_out": dwo[None]}
    return dx1, grads


def _local_step(x, p, target, W, F, on_ffn_grads):
    nb, s, d = x.shape
    t = nb * s
    h = x.reshape(t, d)
    tape = []
    for i in range(DEPTH):
        tag = f"l{i}"
        f1 = [F[k][i] for k in ("ffn1_wg", "ffn1_wu", "ffn1_wd")]
        f2 = [F[k][i] for k in ("ffn2_wg", "ffn2_wu", "ffn2_wd")]
        lg = [W["ln_g"][i, k][None] for k in range(3)]
        lb = [W["ln_b"][i, k][None] for k in range(3)]
        x1, z0, x1b = ffn_fwd(h, *f1, lg[0], lb[0], name=tag + "_ffn1_fwd")
        mixer = _mixer_ab_fwd if i % 2 == 0 else _mixer_c_fwd
        x2, z1, msaved = mixer(x1, x1b, W, lg[1], lb[1], nb, tag + "_mix")
        x3, z2, _ = ffn_fwd(x2, *f2, lg[2], lb[2], name=tag + "_ffn2_fwd")
        pi = p[i].reshape(t, -1)
        pw = (W["ple_wg"][i].astype(bf16), W["ple_bg"][i][None], W["ple_wp"][i].astype(bf16))
        x4 = ple_fwd(x3, pi, *pw, name=tag + "_ple_fwd")
        tape.append((h, z0, x1b, msaved, z1, x2, z2, x3, pi, pw, lg))
        h = x4
    dh, sq = loss_head(h, target.reshape(t, d), name="loss_head")
    loss = 0.5 * jnp.sum(sq) / d
    per_layer = [None] * DEPTH
    grads = {}
    for i in reversed(range(DEPTH)):
        tag = f"l{i}"
        h_in, z0, x1b, msaved, z1, x2, z2, x3, pi, pw, lg = tape[i]
        dx3, dple_wg, dple_bg, dple_wp = ple_bwd(x3, pi, dh, pw[0], pw[0].T, pw[1], pw[2], name=tag + "_ple_bwd")
        dz2, dz2b, dg2, db2 = ln_bwd(z2, dx3, lg[2], name=tag + "_ln2_bwd")
        f1 = [F[k][i] for k in ("ffn1_wg", "ffn1_wu", "ffn1_wd")]
        f2 = [F[k][i] for k in ("ffn2_wg", "ffn2_wu", "ffn2_wd")]
        dgate, dup, *df2 = ffn_bwd_weights(x2.astype(bf16), dz2b, *f2, name=tag + "_ffn2_bwd_w")
        on_ffn_grads(i, 3, df2)
        dx2 = ffn_bwd_input(dgate, dup, f2[0], f2[1], dz2, name=tag + "_ffn2_bwd_x")
        dz1, dz1b, dg1, db1 = ln_bwd(z1, dx2, lg[1], name=tag + "_ln1_bwd")
        mixer_bwd = _mixer_ab_bwd if i % 2 == 0 else _mixer_c_bwd
        dx1, mgrads = mixer_bwd(x1b, dz1, dz1b, W, msaved, nb, tag + "_mix")
        grads.update(mgrads)
        dz0, dz0b, dg0, db0 = ln_bwd(z0, dx1, lg[0], name=tag + "_ln0_bwd")
        dgate, dup, *df1 = ffn_bwd_weights(h_in.astype(bf16), dz0b, *f1, name=tag + "_ffn1_bwd_w")
        on_ffn_grads(i, 0, df1)
        dh = ffn_bwd_input(dgate, dup, f1[0], f1[1], dz0, name=tag + "_ffn1_bwd_x")
        per_layer[i] = {"ln_g": jnp.concatenate([dg0, dg1, dg2], 0), "ln_b": jnp.concatenate([db0, db1, db2], 0),
                        "ple_wg": dple_wg, "ple_bg": dple_bg[0], "ple_wp": dple_wp}
    for k in per_layer[0]:
        grads[k] = jnp.stack([per_layer[i][k] for i in range(DEPTH)])
    return loss, dh.reshape(nb, s, d), grads


WEIGHT_NAMES = ("ffn1_wg", "ffn1_wu", "ffn1_wd", "ffn2_wg", "ffn2_wu", "ffn2_wd", "ln_g", "ln_b", "ple_wg", "ple_bg",
                "ple_wp", "ab_w_in", "a_sinks", "b_conv_w", "b_conv_b", "b_wa", "b_ba", "b_wx", "b_bx", "b_lam",
                "ab_w_out", "c_w_in", "c_conv_w", "c_a_log", "c_dt_bias", "c_norm_g", "c_w_out")
NATIVE_NAMES = WEIGHT_NAMES[:6]
PACKED_NAMES = WEIGHT_NAMES[6:]
SHARD_AXIS = {"ffn1_wg": 2, "ffn1_wu": 2, "ffn1_wd": 1, "ffn2_wg": 2, "ffn2_wu": 2, "ffn2_wd": 1, "ln_g": 2, "ln_b": 2,
              "ple_wg": 1, "ple_wp": 2, "ab_w_in": 2, "b_conv_w": 2, "ab_w_out": 1, "c_w_in": 2, "c_conv_w": 2,
              "c_w_out": 1}
N_CHIPS = 4
PACK_COLS = LANES
PACK_TILE_MULTIPLE = 256
ELEMENTWISE_BLOCK_ELEMS = 128 * 1024


def _row_tile(r, cols):
    return _tile(r, max(2 * SUBLANES, ELEMENTWISE_BLOCK_ELEMS // cols), 2 * SUBLANES)
MESH = pl.DeviceIdType.MESH
ANY = pl.BlockSpec(memory_space=pl.ANY)


def _tiled_dims(shape):
    w = shape[-1]
    r = 1
    for dim in shape[:-1]:
        r *= dim
    return r, w, -(-r // SUBLANES) * SUBLANES, -(-w // LANES) * LANES


def _pack(pieces, lead=()):
    k = len(lead)
    tiles = []
    for a in pieces:
        r, w, rp, wp = _tiled_dims(a.shape[k:])
        a2 = jnp.pad(a.reshape(lead + (r, w)), [(0, 0)] * k + [(0, rp - r), (0, wp - w)])
        a2 = a2.reshape(lead + (rp // SUBLANES, SUBLANES, wp // LANES, LANES))
        a2 = jnp.swapaxes(a2, k + 1, k + 2)
        tiles.append(a2.reshape(lead + (-1, SUBLANES, LANES)))
    flat = jnp.concatenate(tiles, axis=k)
    n = flat.shape[k]
    n_pad = -(-n // PACK_TILE_MULTIPLE) * PACK_TILE_MULTIPLE
    flat = jnp.pad(flat, [(0, 0)] * k + [(0, n_pad - n), (0, 0), (0, 0)])
    return flat.reshape(lead + (n_pad * SUBLANES, PACK_COLS))


def _unpack(pack, shapes, lead=()):
    k = len(lead)
    flat = pack.reshape(lead + (-1, SUBLANES, LANES))
    out, o = [], 0
    for shp in shapes:
        r, w, rp, wp = _tiled_dims(shp)
        n = (rp // SUBLANES) * (wp // LANES)
        a2 = lax.slice_in_dim(flat, o, o + n, axis=k).reshape(lead + (rp // SUBLANES, wp // LANES, SUBLANES, LANES))
        a2 = jnp.swapaxes(a2, k + 1, k + 2).reshape(lead + (rp, wp))
        a2 = lax.slice_in_dim(lax.slice_in_dim(a2, 0, r, axis=k), 0, w, axis=k + 1)
        out.append(a2.reshape(lead + tuple(shp)))
        o += n
    return out


def _mesh_position():
    x, y, c = lax.axis_index("x"), lax.axis_index("y"), lax.axis_index("c")
    chips = [(1 - x, y), (x, 1 - y), (1 - x, 1 - y)]
    return x, y, c, chips


def _remote(src, dst, send_sems, recv_sems, k, to):
    return pltpu.make_async_remote_copy(src_ref=src, dst_ref=dst, send_sem=send_sems.at[k], recv_sem=recv_sems.at[k],
                                        device_id=to, device_id_type=MESH)


def _sems(n):
    return pltpu.SemaphoreType.DMA((n,))


def place_slot(parts, slots, n_slots, dtype, from_slot, *, name):
    n = len(parts)
    r, cols = parts[0].shape[-2:]
    tr = _row_tile(r, cols)

    def body(src_ref, dst_ref, *refs):
        for a in range(n):
            refs[n + a][...] = refs[a][...].astype(dtype)

    dst = pl.BlockSpec((None, tr, cols), lambda i, src_ref, dst_ref: (dst_ref[0], i, 0))
    src = (pl.BlockSpec((None, tr, cols), lambda i, src_ref, dst_ref: (src_ref[0], i, 0)) if from_slot
           else pl.BlockSpec((tr, cols), lambda i, src_ref, dst_ref: (i, 0)))
    return pl.pallas_call(
        body,
        grid_spec=pltpu.PrefetchScalarGridSpec(num_scalar_prefetch=2, grid=(r // tr,), in_specs=[src] * n,
                                               out_specs=[dst] * n),
        out_shape=[jax.ShapeDtypeStruct((n_slots, r, cols), dtype)] * n,
        compiler_params=_params(("parallel",)), name=name,
    )(*slots, *parts)


def gather_shards(bufs, *, name):
    n = len(bufs)

    def body(*refs):
        out_refs = refs[n:2 * n]
        send_sems, recv_sems = refs[2 * n:]
        x, y, c, chips = _mesh_position()
        me = 2 * x + y
        sibling = (x, y, 1 - c)
        waits = []
        for j, (cx, cy) in enumerate(chips):
            for a in range(n):
                own = out_refs[a].at[me, c]
                cp = _remote(own, own, send_sems, recv_sems, 6 * a + j, (cx, cy, c))
                cp.start()
                waits.append(cp.wait_send)
        for j, (cx, cy) in enumerate(chips):
            for a in range(n):
                got = out_refs[a].at[2 * cx + cy, c]
                _remote(got, got, send_sems, recv_sems, 6 * a + j, (cx, cy, c)).wait_recv()
                fw = _remote(got, got, send_sems, recv_sems, 6 * a + 3 + j, sibling)
                fw.start()
                waits.append(fw.wait_send)
        for j, (cx, cy) in enumerate(chips):
            for a in range(n):
                got = out_refs[a].at[2 * cx + cy, 1 - c]
                _remote(got, got, send_sems, recv_sems, 6 * a + 3 + j, sibling).wait_recv()
        for wait in waits:
            wait()

    return pl.pallas_call(
        body, out_shape=[jax.ShapeDtypeStruct(b.shape, b.dtype) for b in bufs],
        in_specs=[ANY] * n, out_specs=[ANY] * n, scratch_shapes=[_sems(6 * n), _sems(6 * n)],
        input_output_aliases={a: a for a in range(n)}, name=name,
    )(*bufs)


def chip_exchange(ps, qs, *, name):
    n = len(ps)

    def body(*refs):
        p_refs, q_refs = refs[:n], refs[2 * n:3 * n]
        send_sems, recv_sems = refs[3 * n:]
        x, y, c, chips = _mesh_position()
        me = 2 * x + y
        waits = []
        for j, (cx, cy) in enumerate(chips):
            for a in range(n):
                cp = _remote(p_refs[a].at[2 * cx + cy], q_refs[a].at[me], send_sems, recv_sems, 3 * a + j, (cx, cy, c))
                cp.start()
                waits.append(cp.wait_send)
        for j, (cx, cy) in enumerate(chips):
            for a in range(n):
                got = q_refs[a].at[2 * cx + cy]
                _remote(got, got, send_sems, recv_sems, 3 * a + j, (cx, cy, c)).wait_recv()
        for wait in waits:
            wait()

    return pl.pallas_call(
        body, out_shape=[jax.ShapeDtypeStruct(q_.shape, q_.dtype) for q_ in qs], in_specs=[ANY] * (2 * n),
        out_specs=[ANY] * n, scratch_shapes=[_sems(3 * n), _sems(3 * n)],
        input_output_aliases={n + a: a for a in range(n)}, name=name,
    )(*ps, *qs)


def gather_slots_async(bufs, collective_id, sources=None, *, name):
    n = len(bufs)
    refs = [jax.new_ref(b, memory_space=pltpu.MemorySpace.HBM) for b in bufs]
    src_refs = None if sources is None else [jax.new_ref(s_, memory_space=pltpu.MemorySpace.HBM) for s_ in sources]

    @pl.kernel(mesh=plsc.ScalarSubcoreMesh(axis_name="sequencer", num_cores=1), name=name,
               scratch_types=(_sems(3 * n), _sems(3 * n)),
               compiler_params=pltpu.CompilerParams(collective_id=collective_id))
    def launch(send_sems, recv_sems):
        x, y, c, chips = _mesh_position()
        me = 2 * x + y
        barrier = pltpu.get_barrier_semaphore()
        for cx, cy in chips:
            pl.semaphore_signal(barrier, inc=1, device_id=(cx, cy, c), device_id_type=MESH)
        pl.semaphore_wait(barrier, len(chips))
        sends = []
        for j, (cx, cy) in enumerate(chips):
            for a in range(n):
                own = refs[a].at[me]
                src = own if src_refs is None else src_refs[a].at[2 * cx + cy]
                cp = _remote(src, own, send_sems, recv_sems, 3 * a + j, (cx, cy, c))
                cp.start()
                sends.append(cp)
        for j, (cx, cy) in enumerate(chips):
            for a in range(n):
                got = refs[a].at[2 * cx + cy]
                _remote(got, got, send_sems, recv_sems, 3 * a + j, (cx, cy, c)).wait_recv()
        for cp in sends:
            cp.wait_send()

    launch()
    return [r[...] for r in refs]


N_DEVICES = 8
PEER_FLIPS = tuple((dx, dy, dc) for dx in (0, 1) for dy in (0, 1) for dc in (0, 1) if dx or dy or dc)


def exchange_partials_async(sends, recvs, collective_id, *, name):
    n = len(sends)
    s_refs = [jax.new_ref(a, memory_space=pltpu.MemorySpace.HBM) for a in sends]
    r_refs = [jax.new_ref(a, memory_space=pltpu.MemorySpace.HBM) for a in recvs]
    k = len(PEER_FLIPS)

    @pl.kernel(mesh=plsc.ScalarSubcoreMesh(axis_name="sequencer", num_cores=1), name=name,
               scratch_types=(_sems(k), _sems(k)), compiler_params=pltpu.CompilerParams(collective_id=collective_id))
    def launch(send_sems, recv_sems):
        x, y, c, _ = _mesh_position()
        me = 4 * x + 2 * y + c
        peers = [(1 - x if dx else x, 1 - y if dy else y, 1 - c if dc else c) for dx, dy, dc in PEER_FLIPS]
        barrier = pltpu.get_barrier_semaphore()
        for peer in peers:
            pl.semaphore_signal(barrier, inc=1, device_id=peer, device_id_type=MESH)
        pl.semaphore_wait(barrier, len(peers))
        sends_started = []
        for j, (px, py, pc) in enumerate(peers):
            for a in range(n):
                cp = _remote(s_refs[a].at[2 * px + py], r_refs[a].at[me], send_sems, recv_sems, j, (px, py, pc))
                cp.start()
                sends_started.append(cp)
        for j, (px, py, pc) in enumerate(peers):
            for a in range(n):
                got = r_refs[a].at[4 * px + 2 * py + pc]
                _remote(got, got, send_sems, recv_sems, j, (px, py, pc)).wait_recv()
        for cp in sends_started:
            cp.wait_send()

    launch()
    return [r[...] for r in r_refs]


def sibling_exchange(gs, *, name):
    n = len(gs)

    def body(*refs):
        g_refs, out_refs = refs[:n], refs[n:2 * n]
        send_sems, recv_sems = refs[2 * n:]
        x, y, c, _ = _mesh_position()
        cps = [_remote(g_refs[a].at[:, 1 - c], out_refs[a], send_sems, recv_sems, a, (x, y, 1 - c)) for a in range(n)]
        for cp in cps:
            cp.start()
        for cp in cps:
            cp.wait()

    return pl.pallas_call(
        body, out_shape=[jax.ShapeDtypeStruct(g.shape[:1] + g.shape[2:], g.dtype) for g in gs],
        in_specs=[ANY] * n, out_specs=[ANY] * n, scratch_shapes=[_sems(n), _sems(n)], name=name,
    )(*gs)


def add_own_half(gs, others, c_idx, dtype, *, name):
    n = len(gs)
    ns, _, r, cols = gs[0].shape
    tr = _row_tile(r, cols)

    def body(c_ref, *refs):
        for a in range(n):
            refs[2 * n + a][...] = (refs[a][...] + refs[n + a][...]).astype(dtype)

    own = pl.BlockSpec((None, None, tr, cols), lambda s, i, c_ref: (s, c_ref[0], i, 0))
    oth = pl.BlockSpec((None, tr, cols), lambda s, i, c_ref: (s, i, 0))
    return pl.pallas_call(
        body,
        grid_spec=pltpu.PrefetchScalarGridSpec(num_scalar_prefetch=1, grid=(ns, r // tr),
                                               in_specs=[own] * n + [oth] * n, out_specs=[oth] * n),
        out_shape=[jax.ShapeDtypeStruct((ns, r, cols), dtype)] * n,
        compiler_params=_params(("parallel", "parallel")), name=name,
    )(c_idx, *gs, *others)


def sum_slots(qs, *, name):
    n = len(qs)
    ns, r, cols = qs[0].shape
    tr = _row_tile(r, cols * ns)

    def body(*refs):
        for a in range(n):
            q_ref = refs[a]
            acc = q_ref[0].astype(f32) + q_ref[1].astype(f32)
            for i in range(2, ns):
                acc = acc + q_ref[i].astype(f32)
            refs[n + a][...] = acc

    return pl.pallas_call(
        body, grid=(r // tr,), in_specs=[pl.BlockSpec((ns, tr, cols), lambda i: (0, i, 0))] * n,
        out_specs=[pl.BlockSpec((tr, cols), lambda i: (i, 0))] * n,
        out_shape=[jax.ShapeDtypeStruct((r, cols), f32)] * n,
        compiler_params=_params(("parallel",)), name=name,
    )(*qs)


def sibling_share(bufs, *, name):
    n = len(bufs)

    def body(*refs):
        out_refs = refs[n:2 * n]
        send_sems, recv_sems = refs[2 * n:]
        x, y, c, _ = _mesh_position()
        sibling = (x, y, 1 - c)
        cps = []
        for a in range(n):
            own = out_refs[a].at[c]
            cp = _remote(own, own, send_sems, recv_sems, a, sibling)
            cp.start()
            cps.append(cp)
        for a in range(n):
            theirs = out_refs[a].at[1 - c]
            _remote(theirs, theirs, send_sems, recv_sems, a, sibling).wait_recv()
        for cp in cps:
            cp.wait_send()

    return pl.pallas_call(
        body, out_shape=[jax.ShapeDtypeStruct(b.shape, b.dtype) for b in bufs], in_specs=[ANY] * n,
        out_specs=[ANY] * n, scratch_shapes=[_sems(n), _sems(n)],
        input_output_aliases={a: a for a in range(n)}, name=name,
    )(*bufs)


def adamw(ws, gs, ms, vs, *, name):
    n = len(ws)
    r, cols = ws[0].shape
    tr = _row_tile(r, cols)

    def body(*refs):
        for a in range(n):
            w_ref, g_ref, m_ref, v_ref = (refs[k * n + a] for k in range(4))
            d_ref, m2_ref, v2_ref = (refs[(4 + k) * n + a] for k in range(3))
            g_ = g_ref[...]
            m2 = ADAM_B1 * m_ref[...] + (1.0 - ADAM_B1) * g_
            v2 = ADAM_B2 * v_ref[...] + (1.0 - ADAM_B2) * (g_ * g_)
            m_hat = m2 / (1.0 - ADAM_B1 ** ADAM_STEP)
            v_hat = v2 / (1.0 - ADAM_B2 ** ADAM_STEP)
            d_ref[...] = -ADAM_LR * (m_hat / (jnp.sqrt(v_hat) + ADAM_EPS) + ADAM_WD * w_ref[...])
            m2_ref[...] = m2
            v2_ref[...] = v2

    row = pl.BlockSpec((tr, cols), lambda i: (i, 0))
    out = pl.pallas_call(
        body, grid=(r // tr,), in_specs=[row] * (4 * n), out_specs=[row] * (3 * n),
        out_shape=[jax.ShapeDtypeStruct((r, cols), f32)] * (3 * n),
        compiler_params=_params(("parallel",)), name=name,
    )(*ws, *gs, *ms, *vs)
    return out[:n], out[n:2 * n], out[2 * n:]


def _full_weights(gathered, local, shapes):
    pieces = _unpack(gathered, shapes, lead=(N_CHIPS,))
    full = {}
    for name, loc, pc in zip(PACKED_NAMES, local, pieces):
        ax = SHARD_AXIS.get(name)
        if ax is None:
            full[name] = loc
        else:
            shp = loc.shape
            full[name] = jnp.moveaxis(pc, 0, ax).reshape(shp[:ax] + (N_CHIPS * shp[ax],) + shp[ax + 1:])
    return full


def _grad_pack(grads, shapes):
    pieces = []
    for name, shp in zip(PACKED_NAMES, shapes):
        g = grads[name]
        ax = SHARD_AXIS.get(name)
        if ax is None:
            pieces.append(jnp.broadcast_to(g.reshape(shp)[None], (N_CHIPS,) + tuple(shp)))
        else:
            pieces.append(jnp.stack(jnp.split(g, N_CHIPS, axis=ax)))
    return _pack(pieces, lead=(N_CHIPS,))


def _by_shape(arrays):
    groups = {}
    for i, a in enumerate(arrays):
        groups.setdefault(a.shape, []).append(i)
    return list(groups.values())


def _grouped(fn, lists, n_out, tag):
    outs = [[None] * len(lists[0]) for _ in range(n_out)]
    for gi, idx in enumerate(_by_shape(lists[0])):
        res = fn(*[[lst[i] for i in idx] for lst in lists], name=f"{tag}_{gi}")
        res = res if n_out > 1 else (res,)
        for k in range(n_out):
            for i, r in zip(idx, res[k]):
                outs[k][i] = r
    return outs if n_out > 1 else outs[0]


def _train_step(x, p, loss_target, weights, m, v):
    packed_w = [weights[k] for k in PACKED_NAMES]
    shapes = [w.shape for w in packed_w]
    halves = lambda a: a.reshape((2, a.shape[0] // 2) + a.shape[1:])
    local = [weights[k] for k in NATIVE_NAMES] + [halves(_pack(packed_w))]
    local_m = [m[k] for k in NATIVE_NAMES] + [halves(_pack([m[k] for k in PACKED_NAMES]))]
    local_v = [v[k] for k in NATIVE_NAMES] + [halves(_pack([v[k] for k in PACKED_NAMES]))]
    flat = lambda lst: [a.reshape((-1, a.shape[-1])) for a in lst]
    c_idx = lax.axis_index("c").astype(jnp.int32).reshape(1)
    chip_idx = (2 * lax.axis_index("x") + lax.axis_index("y")).astype(jnp.int32).reshape(1)
    c2 = (c_idx, c_idx)
    chip2 = (chip_idx, chip_idx)
    chip_dev = (chip_idx, 2 * chip_idx + c_idx)

    def placed(arrays, slot, n_slots, dtype, from_slot, tag):
        return _grouped(lambda a, name: place_slot(a, slot, n_slots, dtype, from_slot, name=name), [arrays], 1, tag)

    ffn_own = [weights[k][i] for i in range(DEPTH) for k in NATIVE_NAMES]
    ffn_bufs = placed(ffn_own, chip2, N_CHIPS, bf16, False, "place_ffn_weights")
    group = len(NATIVE_NAMES) // 2
    n_ffn_groups = len(ffn_bufs) // group
    ffn_gathered = []
    for gi in range(n_ffn_groups):
        ffn_gathered += gather_slots_async(ffn_bufs[gi * group:(gi + 1) * group], collective_id=1 + gi,
                                           name=f"comm_gather_ffn_{gi}")
    ffn_weights = {k: [ffn_gathered[i * len(NATIVE_NAMES) + j] for i in range(DEPTH)] for j, k in enumerate(NATIVE_NAMES)}
    pack_buf = placed(flat(local[-1:]), chip2, N_CHIPS, f32, False, "place_packed_weights")[0]
    gathered_pack = gather_shards([pack_buf.reshape((N_CHIPS,) + local[-1].shape)], name="comm_gather_weights")[0]
    full = _full_weights(gathered_pack, packed_w, shapes)
    first_grad_id = n_ffn_groups + 1
    in_flight = {}

    def on_ffn_grads(layer, first, partials):
        tag = f"ffn_grads_l{layer}_{first}"
        recvs = placed(partials, chip_dev, N_DEVICES, bf16, True, "place_" + tag)
        got = exchange_partials_async(partials, recvs, collective_id=first_grad_id + len(in_flight), name="comm_" + tag)
        in_flight[(layer, first)] = got

    loss, grad_x, grads = _local_step(x, p, loss_target, full, ffn_weights, on_ffn_grads)
    gs = [_grad_pack(grads, shapes).reshape((N_CHIPS,) + local[-1].shape)]
    others = sibling_exchange(gs, name="comm_grad_sibling")
    chip_sums = add_own_half(gs, others, c_idx, f32, name="grad_add_sibling")
    own = placed(chip_sums, chip2, N_CHIPS, f32, True, "place_own_partial")
    slots = chip_exchange(chip_sums, own, name="comm_grad_chips")
    mine = sum_slots(slots, name="grad_sum_chips")
    pack_sum = sibling_share(placed(mine, c2, 2, f32, False, "place_own_half"), name="comm_grad_share")
    ffn_sums = {}
    for (layer, first), got in in_flight.items():
        sums = _grouped(sum_slots, [got], 1, f"grad_sum_ffn_l{layer}_{first}")
        for j, g in enumerate(sums):
            ffn_sums[(NATIVE_NAMES[first + j], layer)] = g
    gsum = [jnp.stack([ffn_sums[(k, i)] for i in range(DEPTH)]) for k in NATIVE_NAMES] + list(pack_sum)
    delta, m2, v2 = _grouped(adamw, [flat(local), flat(gsum), flat(local_m), flat(local_v)], 3, "adamw")
    loss = lax.psum(loss, ("x", "y", "c"))
    outs = []
    for res in (gsum, delta, m2, v2):
        by_name = {k: a.reshape(weights[k].shape) for k, a in zip(NATIVE_NAMES, res[:-1])}
        by_name.update(zip(PACKED_NAMES, _unpack(res[-1], shapes)))
        outs += [by_name[k] for k in WEIGHT_NAMES]
    return (loss, grad_x, *outs)


def kernel(x, p, ffn1_wg, ffn1_wu, ffn1_wd, ffn2_wg, ffn2_wu, ffn2_wd, ln_g, ln_b, ple_wg, ple_bg, ple_wp, ab_w_in, a_sinks, b_conv_w, b_conv_b, b_wa, b_ba, b_wx, b_bx, b_lam, ab_w_out, c_w_in, c_conv_w, c_a_log, c_dt_bias, c_norm_g, c_w_out, loss_target, m_ffn1_wg, m_ffn1_wu, m_ffn1_wd, m_ffn2_wg, m_ffn2_wu, m_ffn2_wd, m_ln_g, m_ln_b, m_ple_wg, m_ple_bg, m_ple_wp, m_ab_w_in, m_a_sinks, m_b_conv_w, m_b_conv_b, m_b_wa, m_b_ba, m_b_wx, m_b_bx, m_b_lam, m_ab_w_out, m_c_w_in, m_c_conv_w, m_c_a_log, m_c_dt_bias, m_c_norm_g, m_c_w_out, v_ffn1_wg, v_ffn1_wu, v_ffn1_wd, v_ffn2_wg, v_ffn2_wu, v_ffn2_wd, v_ln_g, v_ln_b, v_ple_wg, v_ple_bg, v_ple_wp, v_ab_w_in, v_a_sinks, v_b_conv_w, v_b_conv_b, v_b_wa, v_b_ba, v_b_wx, v_b_bx, v_b_lam, v_ab_w_out, v_c_w_in, v_c_conv_w, v_c_a_log, v_c_dt_bias, v_c_norm_g, v_c_w_out):
    weights = [ffn1_wg, ffn1_wu, ffn1_wd, ffn2_wg, ffn2_wu, ffn2_wd, ln_g, ln_b, ple_wg, ple_bg, ple_wp, ab_w_in, a_sinks,
               b_conv_w, b_conv_b, b_wa, b_ba, b_wx, b_bx, b_lam, ab_w_out, c_w_in, c_conv_w, c_a_log, c_dt_bias, c_norm_g,
               c_w_out]
    m = [m_ffn1_wg, m_ffn1_wu, m_ffn1_wd, m_ffn2_wg, m_ffn2_wu, m_ffn2_wd, m_ln_g, m_ln_b, m_ple_wg, m_ple_bg, m_ple_wp,
         m_ab_w_in, m_a_sinks, m_b_conv_w, m_b_conv_b, m_b_wa, m_b_ba, m_b_wx, m_b_bx, m_b_lam, m_ab_w_out, m_c_w_in,
         m_c_conv_w, m_c_a_log, m_c_dt_bias, m_c_norm_g, m_c_w_out]
    v = [v_ffn1_wg, v_ffn1_wu, v_ffn1_wd, v_ffn2_wg, v_ffn2_wu, v_ffn2_wd, v_ln_g, v_ln_b, v_ple_wg, v_ple_bg, v_ple_wp,
         v_ab_w_in, v_a_sinks, v_b_conv_w, v_b_conv_b, v_b_wa, v_b_ba, v_b_wx, v_b_bx, v_b_lam, v_ab_w_out, v_c_w_in,
         v_c_conv_w, v_c_a_log, v_c_dt_bias, v_c_norm_g, v_c_w_out]
    return _train_step(x, p, loss_target, dict(zip(WEIGHT_NAMES, weights)), dict(zip(WEIGHT_NAMES, m)),
                       dict(zip(WEIGHT_NAMES, v)))
```

```python
import functools

import jax
import jax.numpy as jnp
from jax import lax
from jax.experimental import pallas as pl
from jax.experimental.pallas import tpu as pltpu
from jax.experimental.pallas import tpu_sc as plsc

f32 = jnp.float32
bf16 = jnp.bfloat16

DEPTH = 2
CHUNK = 64
A_HEADS, A_KV_HEADS, A_GROUP, A_HEAD_DIM = 8, 2, 4, 64
A_WIDTH, A_KV_WIDTH, A_WINDOW = 512, 128, 128
B_WIDTH, B_BLOCKS, B_BLOCK, B_CONV = 512, 8, 64, 4
RG_C = 8.0
C_HEADS, C_HEAD_DIM, C_WIDTH, C_CONV = 8, 128, 1024, 4
DN_ALPHA = (2.0 * DEPTH) ** 0.25
LN_EPS = 1e-5
NORM_EPS = 1e-6
NEG = -1e30
ADAM_LR, ADAM_B1, ADAM_B2, ADAM_EPS, ADAM_WD, ADAM_STEP = 0.001, 0.9, 0.999, 1e-08, 0.01, 10

VMEM_LIMIT_BYTES = 56 * 1024 * 1024
LANES = 128
SUBLANES = 8
GROUP_W = 128
PREP_FWD_UNROLL = 8
PREP_BWD_UNROLL = 8
C_HEADS_PER_STEP = 4
GDN_TIME_BLOCK = 512

NN = ((1,), (0,))
NT = ((1,), (1,))
TN = ((0,), (0,))


def _params(sem):
    return pltpu.CompilerParams(dimension_semantics=sem, vmem_limit_bytes=VMEM_LIMIT_BYTES)


def _tile(n, cap, mult):
    best = None
    t = mult
    while t <= min(n, cap):
        if n % t == 0:
            best = t
        t += mult
    return best if best is not None else n


def _bdot(a, b, dims):
    return lax.dot_general(a.astype(bf16), b.astype(bf16), (dims, ((), ())), preferred_element_type=f32)


def _running_sum(x, reverse):
    s = x.shape[0]
    t = lax.broadcasted_iota(jnp.int32, x.shape, 0)
    d = 1
    while d < s:
        if reverse:
            x = x + jnp.where(t < s - d, pltpu.roll(x, s - d, 0), 0.0)
        else:
            x = x + jnp.where(t >= d, pltpu.roll(x, d, 0), 0.0)
        d *= 2
    return x


@jax.custom_vjp
def _cumsum0(x):
    return _running_sum(x, False)


def _cumsum0_fwd(x):
    return _running_sum(x, False), None


def _cumsum0_bwd(_, g):
    return (_running_sum(g, True),)


_cumsum0.defvjp(_cumsum0_fwd, _cumsum0_bwd)


@jax.custom_vjp
def _bnn(a, b):
    return _bdot(a, b, NN)


def _bnn_fwd(a, b):
    return _bdot(a, b, NN), (a, b)


def _bnn_bwd(res, g):
    a, b = res
    return _bdot(g, b, NT), _bdot(a, g, TN)


_bnn.defvjp(_bnn_fwd, _bnn_bwd)


@jax.custom_vjp
def _bnt(a, b):
    return _bdot(a, b, NT)


def _bnt_fwd(a, b):
    return _bdot(a, b, NT), (a, b)


def _bnt_bwd(res, g):
    a, b = res
    return _bdot(g, b, NN), _bdot(g, a, TN)


_bnt.defvjp(_bnt_fwd, _bnt_bwd)


@jax.custom_vjp
def _btn(a, b):
    return _bdot(a, b, TN)


def _btn_fwd(a, b):
    return _bdot(a, b, TN), (a, b)


def _btn_bwd(res, g):
    a, b = res
    return _bdot(b, g, NT), _bdot(a, g, NN)


_btn.defvjp(_btn_fwd, _btn_bwd)

RAW_DOTS = (lambda a, b: _bdot(a, b, NN), lambda a, b: _bdot(a, b, NT), lambda a, b: _bdot(a, b, TN),
            lambda x: _running_sum(x, False))
VJP_DOTS = (_bnn, _bnt, _btn, _cumsum0)


def _layer_norm(z, g, b):
    mu = jnp.mean(z, -1, keepdims=True)
    d = z - mu
    var = jnp.mean(d * d, -1, keepdims=True)
    return d * lax.rsqrt(var + LN_EPS) * g + b


def _silu(x):
    return x * jax.nn.sigmoid(x)


def mm_nn(a, w, add=None, add_scale=1.0, *, name):
    m, k = a.shape
    n = w.shape[1]
    tm = _tile(m, 512, SUBLANES)
    tn = _tile(n, 1024, LANES)

    def body(*refs):
        if add is None:
            a_ref, w_ref, o_ref = refs
            o_ref[...] = _bdot(a_ref[...], w_ref[...], NN)
        else:
            a_ref, w_ref, add_ref, o_ref = refs
            o_ref[...] = _bdot(a_ref[...], w_ref[...], NN) + add_scale * add_ref[...]

    in_specs = [pl.BlockSpec((tm, k), lambda i, j: (i, 0)), pl.BlockSpec((k, tn), lambda i, j: (0, j))]
    args = [a, w]
    if add is not None:
        in_specs.append(pl.BlockSpec((tm, tn), lambda i, j: (i, j)))
        args.append(add)
    return pl.pallas_call(
        body, grid=(m // tm, n // tn), in_specs=in_specs,
        out_specs=pl.BlockSpec((tm, tn), lambda i, j: (i, j)),
        out_shape=jax.ShapeDtypeStruct((m, n), f32),
        compiler_params=_params(("parallel", "parallel")), name=name,
    )(*args)


def mm_tn(a, b, *, name):
    m, k = a.shape
    n = b.shape[1]
    tm = _tile(m, 1024, 2 * SUBLANES)
    tn = _tile(n, 1024, LANES)

    def body(a_ref, b_ref, o_ref):
        part = _bdot(a_ref[...], b_ref[...], TN)

        @pl.when(pl.program_id(1) == 0)
        def _():
            o_ref[...] = part

        @pl.when(pl.program_id(1) > 0)
        def _():
            o_ref[...] += part

    return pl.pallas_call(
        body, grid=(n // tn, m // tm),
        in_specs=[pl.BlockSpec((tm, k), lambda j, i: (i, 0)), pl.BlockSpec((tm, tn), lambda j, i: (i, j))],
        out_specs=pl.BlockSpec((k, tn), lambda j, i: (0, j)),
        out_shape=jax.ShapeDtypeStruct((k, n), f32),
        compiler_params=_params(("parallel", "arbitrary")), name=name,
    )(a, b)


def proj_ln(a_list, w_list, xres, g, b, *, name):
    t, d = xres.shape
    tm = _tile(t, 256, SUBLANES)
    na = len(a_list)

    def body(*refs):
        a_refs, w_refs = refs[:na], refs[na:2 * na]
        x_ref, g_ref, b_ref, y_ref, z_ref = refs[2 * na:]
        z = DN_ALPHA * x_ref[...]
        for a_ref, w_ref in zip(a_refs, w_refs):
            z = z + _bdot(a_ref[...], w_ref[...], NN)
        z_ref[...] = z
        y_ref[...] = _layer_norm(z, g_ref[...], b_ref[...])

    in_specs = [pl.BlockSpec((tm, a.shape[1]), lambda i: (i, 0)) for a in a_list]
    in_specs += [pl.BlockSpec(w.shape, lambda i: (0, 0)) for w in w_list]
    in_specs += [pl.BlockSpec((tm, d), lambda i: (i, 0)), pl.BlockSpec((1, d), lambda i: (0, 0)),
                 pl.BlockSpec((1, d), lambda i: (0, 0))]
    return pl.pallas_call(
        body, grid=(t // tm,), in_specs=in_specs,
        out_specs=[pl.BlockSpec((tm, d), lambda i: (i, 0))] * 2,
        out_shape=[jax.ShapeDtypeStruct((t, d), f32)] * 2,
        compiler_params=_params(("parallel",)), name=name,
    )(*a_list, *w_list, xres, g, b)


def ln_bwd(z, dy, g, *, name):
    t, d = z.shape
    tm = _tile(t, 512, SUBLANES)

    def body(z_ref, dy_ref, g_ref, dz_ref, dzb_ref, dg_ref, db_ref):
        zz = z_ref[...]
        dy_ = dy_ref[...]
        mu = jnp.mean(zz, -1, keepdims=True)
        dd = zz - mu
        var = jnp.mean(dd * dd, -1, keepdims=True)
        rstd = lax.rsqrt(var + LN_EPS)
        xhat = dd * rstd
        dxh = dy_ * g_ref[...]
        dz = rstd * (dxh - jnp.mean(dxh, -1, keepdims=True) - xhat * jnp.mean(dxh * xhat, -1, keepdims=True))
        dz_ref[...] = dz
        dzb_ref[...] = dz.astype(bf16)
        pg = jnp.sum(dy_ * xhat, 0, keepdims=True)
        pb = jnp.sum(dy_, 0, keepdims=True)

        @pl.when(pl.program_id(0) == 0)
        def _():
            dg_ref[...] = pg
            db_ref[...] = pb

        @pl.when(pl.program_id(0) > 0)
        def _():
            dg_ref[...] += pg
            db_ref[...] += pb

    row = pl.BlockSpec((tm, d), lambda i: (i, 0))
    vec = pl.BlockSpec((1, d), lambda i: (0, 0))
    return pl.pallas_call(
        body, grid=(t // tm,), in_specs=[row, row, vec], out_specs=[row, row, vec, vec],
        out_shape=[jax.ShapeDtypeStruct((t, d), f32), jax.ShapeDtypeStruct((t, d), bf16),
                   jax.ShapeDtypeStruct((1, d), f32), jax.ShapeDtypeStruct((1, d), f32)],
        compiler_params=_params(("arbitrary",)), name=name,
    )(z, dy, g)


def loss_head(y, target, *, name):
    t, d = y.shape
    tm = _tile(t, 512, SUBLANES)

    def body(y_ref, t_ref, dy_ref, sq_ref):
        e = y_ref[...] - t_ref[...]
        dy_ref[...] = e * (1.0 / d)
        part = jnp.sum(e * e, 0, keepdims=True)

        @pl.when(pl.program_id(0) == 0)
        def _():
            sq_ref[...] = part

        @pl.when(pl.program_id(0) > 0)
        def _():
            sq_ref[...] += part

    row = pl.BlockSpec((tm, d), lambda i: (i, 0))
    vec = pl.BlockSpec((1, d), lambda i: (0, 0))
    return pl.pallas_call(
        body, grid=(t // tm,), in_specs=[row, row], out_specs=[row, vec],
        out_shape=[jax.ShapeDtypeStruct((t, d), f32), jax.ShapeDtypeStruct((1, d), f32)],
        compiler_params=_params(("arbitrary",)), name=name,
    )(y, target)


FFN_COL_BLOCK = 256
FFN_ROWS = 1024


def _lane_blocks(n):
    return [slice(s, min(s + FFN_COL_BLOCK, n)) for s in range(0, n, FFN_COL_BLOCK)]


def ffn_fwd(x, wg, wu, wd, g, b, *, name):
    t, d = x.shape
    nf, _, tf = wg.shape
    tm = _tile(t, FFN_ROWS, SUBLANES)

    def body(x_ref, wg_ref, wu_ref, wd_ref, g_ref, b_ref, y_ref, z_ref, yb_ref, acc_ref):
        f = pl.program_id(1)
        xb = x_ref[...].astype(bf16)
        part, pending = None, None
        for cols in _lane_blocks(tf):
            gate_up = (_bdot(xb, wg_ref[:, cols], NN), _bdot(xb, wu_ref[:, cols], NN), cols)
            if pending is not None:
                down = _bdot(_silu(pending[0]) * pending[1], wd_ref[pending[2], :], NN)
                part = down if part is None else part + down
            pending = gate_up
        down = _bdot(_silu(pending[0]) * pending[1], wd_ref[pending[2], :], NN)
        part = down if part is None else part + down

        @pl.when(f == 0)
        def _():
            acc_ref[...] = part

        @pl.when(f > 0)
        def _():
            acc_ref[...] += part

        @pl.when(f == nf - 1)
        def _():
            z = DN_ALPHA * x_ref[...] + 0.5 * acc_ref[...]
            z_ref[...] = z
            y = _layer_norm(z, g_ref[...], b_ref[...])
            y_ref[...] = y
            yb_ref[...] = y.astype(bf16)

    row = pl.BlockSpec((tm, d), lambda i, j: (i, 0))
    vec = pl.BlockSpec((1, d), lambda i, j: (0, 0))
    wcol = pl.BlockSpec((None, d, tf), lambda i, j: (j, 0, 0))
    wrow = pl.BlockSpec((None, tf, d), lambda i, j: (j, 0, 0))
    return pl.pallas_call(
        body, grid=(t // tm, nf),
        in_specs=[row, wcol, wcol, wrow, vec, vec],
        out_specs=[row, row, row],
        out_shape=[jax.ShapeDtypeStruct((t, d), f32)] * 2 + [jax.ShapeDtypeStruct((t, d), bf16)],
        scratch_shapes=[pltpu.VMEM((tm, d), f32)],
        compiler_params=_params(("parallel", "arbitrary")), name=name,
    )(x, wg, wu, wd, g, b)


def ffn_bwd_weights(xb, dzb, wg, wu, wd, *, name):
    t, d = xb.shape
    nf, _, tf = wg.shape
    tm = _tile(t, FFN_ROWS, SUBLANES)
    nt = t // tm

    def body(x_ref, dz_ref, wg_ref, wu_ref, wd_ref, dgate_ref, dup_ref, owg_ref, owu_ref, owd_ref,
             dwg_ref, dwu_ref, dwd_ref):
        x = x_ref[...]
        dzh = dz_ref[...] * 0.5

        def first_half(cols):
            return _bdot(x, wg_ref[:, cols], NN), _bdot(x, wu_ref[:, cols], NN), _bdot(dzh, wd_ref[cols, :], NT), cols

        def second_half(gate, up, dh, cols):
            sg = jax.nn.sigmoid(gate)
            s = gate * sg
            dup = (dh * s).astype(bf16)
            dgate = (dh * up * (sg * (1.0 + gate * (1.0 - sg)))).astype(bf16)
            dgate_ref[:, cols] = dgate
            dup_ref[:, cols] = dup
            return _bdot(x, dgate, TN), _bdot(x, dup, TN), _bdot(s * up, dzh, TN), cols

        parts, pending = [], None
        for cols in _lane_blocks(tf):
            nxt = first_half(cols)
            if pending is not None:
                parts.append(second_half(*pending))
            pending = nxt
        parts.append(second_half(*pending))

        @pl.when(pl.program_id(1) == 0)
        def _():
            for pwg, pwu, pwd, cols in parts:
                dwg_ref[:, cols] = pwg
                dwu_ref[:, cols] = pwu
                dwd_ref[cols, :] = pwd

        @pl.when(pl.program_id(1) > 0)
        def _():
            for pwg, pwu, pwd, cols in parts:
                dwg_ref[:, cols] += pwg
                dwu_ref[:, cols] += pwu
                dwd_ref[cols, :] += pwd

        @pl.when(pl.program_id(1) == nt - 1)
        def _():
            owg_ref[...] = dwg_ref[...].astype(bf16)
            owu_ref[...] = dwu_ref[...].astype(bf16)
            owd_ref[...] = dwd_ref[...].astype(bf16)

    row = pl.BlockSpec((tm, d), lambda j, i: (i, 0))
    wcol = pl.BlockSpec((None, d, tf), lambda j, i: (j, 0, 0))
    wrow = pl.BlockSpec((None, tf, d), lambda j, i: (j, 0, 0))
    act = pl.BlockSpec((None, tm, tf), lambda j, i: (j, i, 0))
    return pl.pallas_call(
        body, grid=(nf, nt), in_specs=[row, row, wcol, wcol, wrow], out_specs=[act, act, wcol, wcol, wrow],
        out_shape=[jax.ShapeDtypeStruct((nf, t, tf), bf16), jax.ShapeDtypeStruct((nf, t, tf), bf16),
                   jax.ShapeDtypeStruct((nf, d, tf), bf16), jax.ShapeDtypeStruct((nf, d, tf), bf16),
                   jax.ShapeDtypeStruct((nf, tf, d), bf16)],
        scratch_shapes=[pltpu.VMEM((d, tf), f32), pltpu.VMEM((d, tf), f32), pltpu.VMEM((tf, d), f32)],
        compiler_params=_params(("parallel", "arbitrary")), name=name,
    )(xb, dzb, wg, wu, wd)


def ffn_bwd_input(dgate, dup, wg, wu, dz, *, name):
    nf, t, tf = dgate.shape
    d = wg.shape[1]
    tm = _tile(t, FFN_ROWS // 2, SUBLANES)

    def body(dg_ref, du_ref, wg_ref, wu_ref, dz_ref, dx_ref):
        acc = DN_ALPHA * dz_ref[...]
        for j in range(nf):
            acc = acc + _bdot(dg_ref[j], wg_ref[j], NT) + _bdot(du_ref[j], wu_ref[j], NT)
        dx_ref[...] = acc

    act = pl.BlockSpec((nf, tm, tf), lambda i: (0, i, 0))
    wsp = pl.BlockSpec((nf, d, tf), lambda i: (0, 0, 0))
    row = pl.BlockSpec((tm, d), lambda i: (i, 0))
    return pl.pallas_call(
        body, grid=(t // tm,), in_specs=[act, act, wsp, wsp, row], out_specs=row,
        out_shape=jax.ShapeDtypeStruct((t, d), f32),
        compiler_params=_params(("parallel",)), name=name,
    )(dgate, dup, wg, wu, dz)


def ple_fwd(x, p, wg, bg, wp, *, name):
    t, d = x.shape
    dp = p.shape[1]
    tm = _tile(t, 512, SUBLANES)

    def body(x_ref, p_ref, wg_ref, bg_ref, wp_ref, o_ref):
        x_ = x_ref[...]
        gate = jax.nn.sigmoid(_bdot(x_, wg_ref[...], NN) + bg_ref[...])
        o_ref[...] = x_ + gate * _bdot(p_ref[...], wp_ref[...], NN)

    row = pl.BlockSpec((tm, d), lambda i: (i, 0))
    return pl.pallas_call(
        body, grid=(t // tm,),
        in_specs=[row, pl.BlockSpec((tm, dp), lambda i: (i, 0)), pl.BlockSpec((d, d), lambda i: (0, 0)),
                  pl.BlockSpec((1, d), lambda i: (0, 0)), pl.BlockSpec((dp, d), lambda i: (0, 0))],
        out_specs=row, out_shape=jax.ShapeDtypeStruct((t, d), f32),
        compiler_params=_params(("parallel",)), name=name,
    )(x, p, wg, bg, wp)


def ple_bwd(x, p, dy, wg, wgt, bg, wp, *, name):
    t, d = x.shape
    dp = p.shape[1]
    tm = _tile(t, 512, SUBLANES)

    def body(x_ref, p_ref, dy_ref, wg_ref, wgt_ref, bg_ref, wp_ref, dx_ref, dwg_ref, dbg_ref, dwp_ref):
        x_ = x_ref[...]
        dy_ = dy_ref[...]
        s = jax.nn.sigmoid(_bdot(x_, wg_ref[...], NN) + bg_ref[...])
        e = _bdot(p_ref[...], wp_ref[...], NN)
        da = dy_ * e * s * (1.0 - s)
        de = dy_ * s
        dx_ref[...] = dy_ + _bdot(da, wgt_ref[...], NN)
        pwg = _bdot(x_, da, TN)
        pbg = jnp.sum(da, 0, keepdims=True)
        pwp = _bdot(p_ref[...], de, TN)

        @pl.when(pl.program_id(0) == 0)
        def _():
            dwg_ref[...] = pwg
            dbg_ref[...] = pbg
            dwp_ref[...] = pwp

        @pl.when(pl.program_id(0) > 0)
        def _():
            dwg_ref[...] += pwg
            dbg_ref[...] += pbg
            dwp_ref[...] += pwp

    row = pl.BlockSpec((tm, d), lambda i: (i, 0))
    full = lambda shape: pl.BlockSpec(shape, lambda i: (0, 0))
    return pl.pallas_call(
        body, grid=(t // tm,),
        in_specs=[row, pl.BlockSpec((tm, dp), lambda i: (i, 0)), row, full((d, d)), full((d, d)), full((1, d)),
                  full((dp, d))],
        out_specs=[row, full((d, d)), full((1, d)), full((dp, d))],
        out_shape=[jax.ShapeDtypeStruct((t, d), f32), jax.ShapeDtypeStruct((d, d), f32),
                   jax.ShapeDtypeStruct((1, d), f32), jax.ShapeDtypeStruct((dp, d), f32)],
        compiler_params=_params(("arbitrary",)), name=name,
    )(x, p, dy, wg, wgt, bg, wp)


def _conv_taps(xpad_ref, w_ref, s):
    acc = w_ref[0:1, :] * xpad_ref[SUBLANES - 3:SUBLANES - 3 + s, :]
    for j in range(1, 4):
        acc = acc + w_ref[j:j + 1, :] * xpad_ref[SUBLANES - 3 + j:SUBLANES - 3 + j + s, :]
    return acc


def conv_fwd(x, w, bias, act, nb, *, name):
    t, c = x.shape
    s = t // nb
    cw = GROUP_W

    def body(x_ref, w_ref, b_ref, y_ref, xpad):
        xpad[0:SUBLANES, :] = jnp.zeros((SUBLANES, cw), f32)
        xpad[SUBLANES:, :] = x_ref[...]
        acc = _conv_taps(xpad, w_ref, s) + b_ref[...]
        y_ref[...] = _silu(acc) if act else acc

    slab = pl.BlockSpec((s, cw), lambda b, g: (b, g))
    return pl.pallas_call(
        body, grid=(nb, c // cw),
        in_specs=[slab, pl.BlockSpec((4, cw), lambda b, g: (0, g)), pl.BlockSpec((1, cw), lambda b, g: (0, g))],
        out_specs=slab, out_shape=jax.ShapeDtypeStruct((t, c), f32),
        scratch_shapes=[pltpu.VMEM((s + SUBLANES, cw), f32)],
        compiler_params=_params(("parallel", "parallel")), name=name,
    )(x, w, bias)


def conv_bwd(x, w, bias, dy, act, nb, *, name):
    t, c = x.shape
    s = t // nb
    cw = GROUP_W

    def body(x_ref, w_ref, b_ref, dy_ref, dx_ref, dw_ref, db_ref, xpad, dpad):
        xpad[0:SUBLANES, :] = jnp.zeros((SUBLANES, cw), f32)
        xpad[SUBLANES:, :] = x_ref[...]
        dacc = dy_ref[...]
        if act:
            acc = _conv_taps(xpad, w_ref, s) + b_ref[...]
            sg = jax.nn.sigmoid(acc)
            dacc = dacc * (sg * (1.0 + acc * (1.0 - sg)))
        dpad[0:s, :] = dacc
        dpad[s:, :] = jnp.zeros((SUBLANES, cw), f32)
        dx = w_ref[0:1, :] * dpad[3:3 + s, :]
        for j in range(1, 4):
            dx = dx + w_ref[j:j + 1, :] * dpad[3 - j:3 - j + s, :]
        dx_ref[...] = dx
        first = pl.program_id(1) == 0
        for j in range(4):
            pw = jnp.sum(dacc * xpad[SUBLANES - 3 + j:SUBLANES - 3 + j + s, :], 0, keepdims=True)

            @pl.when(first)
            def _():
                dw_ref[j:j + 1, :] = pw

            @pl.when(jnp.logical_not(first))
            def _():
                dw_ref[j:j + 1, :] += pw

        pb = jnp.sum(dacc, 0, keepdims=True)

        @pl.when(first)
        def _():
            db_ref[...] = pb

        @pl.when(jnp.logical_not(first))
        def _():
            db_ref[...] += pb

    slab = pl.BlockSpec((s, cw), lambda g, b: (b, g))
    wsp = pl.BlockSpec((4, cw), lambda g, b: (0, g))
    bsp = pl.BlockSpec((1, cw), lambda g, b: (0, g))
    return pl.pallas_call(
        body, grid=(c // cw, nb), in_specs=[slab, wsp, bsp, slab], out_specs=[slab, wsp, bsp],
        out_shape=[jax.ShapeDtypeStruct((t, c), f32), jax.ShapeDtypeStruct((4, c), f32),
                   jax.ShapeDtypeStruct((1, c), f32)],
        scratch_shapes=[pltpu.VMEM((s + SUBLANES, cw), f32), pltpu.VMEM((s + SUBLANES, cw), f32)],
        compiler_params=_params(("parallel", "arbitrary")), name=name,
    )(x, w, bias, dy)


def _each(f, *lists):
    return [f(*a) for a in zip(*lists)]


def _attn_heads(qs, kbs, vbs, sinks, valid, dist, dots):
    nn, nt = dots[:2]
    kv = [h // A_GROUP for h in range(A_HEADS)]
    scs = [nt(qs[h], kbs[kv[h]]) for h in range(A_HEADS)]
    prs = []
    for h in range(A_HEADS):
        sc = scs[h] * (A_HEAD_DIM ** -0.5) - 2.0 ** -(h + 1) * dist
        sc = jnp.where(valid, sc, NEG)
        m = lax.stop_gradient(jnp.maximum(jnp.max(sc, -1, keepdims=True), sinks[h]))
        pr = jnp.exp(sc - m)
        den = jnp.sum(pr, -1, keepdims=True) + jnp.exp(sinks[h] - m)
        prs.append(pr / den)
    return [nn(prs[h], vbs[kv[h]]) for h in range(A_HEADS)]


A_Q_ROWS = 2 * CHUNK


def _attn_band_consts(r0):
    band = A_WINDOW + A_Q_ROWS
    qi = lax.broadcasted_iota(jnp.int32, (A_Q_ROWS, band), 0)
    kj = lax.broadcasted_iota(jnp.int32, (A_Q_ROWS, band), 1)
    dist = jnp.abs(qi + A_WINDOW - kj).astype(f32)
    qc, kc = qi // CHUNK, kj // CHUNK
    valid = ((kj + r0) >= A_WINDOW) & (kc >= qc) & (kc <= qc + A_WINDOW // CHUNK)
    return dist, valid


def attn_fwd(qkv, sinks, nb, *, name):
    t = qkv.shape[0]
    s = t // nb
    band = A_WINDOW + A_Q_ROWS
    hd = A_HEAD_DIM

    def body(qkv_ref, sink_ref, o_ref, kvpad):
        kvpad[0:A_WINDOW, :] = jnp.zeros((A_WINDOW, 2 * A_KV_WIDTH), f32)
        kvpad[A_WINDOW:, :] = qkv_ref[:, A_WIDTH:]

        def chunk(n, carry):
            r0 = pl.multiple_of(n * A_Q_ROWS, A_Q_ROWS)
            dist, valid = _attn_band_consts(r0)
            kbs = [kvpad[pl.ds(r0, band), kvh * hd:(kvh + 1) * hd] for kvh in range(A_KV_HEADS)]
            vbs = [kvpad[pl.ds(r0, band), A_KV_WIDTH + kvh * hd:A_KV_WIDTH + (kvh + 1) * hd]
                   for kvh in range(A_KV_HEADS)]
            qs = [qkv_ref[pl.ds(r0, A_Q_ROWS), h * hd:(h + 1) * hd] for h in range(A_HEADS)]
            outs = _attn_heads(qs, kbs, vbs, [sink_ref[:, h:h + 1] for h in range(A_HEADS)], valid, dist, RAW_DOTS)
            for h in range(A_HEADS):
                o_ref[pl.ds(r0, A_Q_ROWS), h * hd:(h + 1) * hd] = outs[h]
            return carry

        lax.fori_loop(0, s // A_Q_ROWS, chunk, 0)

    return pl.pallas_call(
        body, grid=(nb,),
        in_specs=[pl.BlockSpec((s, A_WIDTH + 2 * A_KV_WIDTH), lambda b: (b, 0)),
                  pl.BlockSpec((1, A_HEADS), lambda b: (0, 0))],
        out_specs=pl.BlockSpec((s, A_WIDTH), lambda b: (b, 0)),
        out_shape=jax.ShapeDtypeStruct((t, A_WIDTH), f32),
        scratch_shapes=[pltpu.VMEM((s + A_WINDOW, 2 * A_KV_WIDTH), f32)],
        compiler_params=_params(("parallel",)), name=name,
    )(qkv, sinks)


def attn_bwd(qkv, sinks, do, nb, *, name):
    t = qkv.shape[0]
    s = t // nb
    band = A_WINDOW + A_Q_ROWS
    hd = A_HEAD_DIM
    kvw = 2 * A_KV_WIDTH

    def body(qkv_ref, sink_ref, do_ref, dqkv_ref, dsink_ref, kvpad, dkvpad):
        kvpad[0:A_WINDOW, :] = jnp.zeros((A_WINDOW, kvw), f32)
        kvpad[A_WINDOW:, :] = qkv_ref[:, A_WIDTH:]
        dkvpad[...] = jnp.zeros((s + A_WINDOW, kvw), f32)

        def chunk(n, dsinks):
            r0 = pl.multiple_of(n * A_Q_ROWS, A_Q_ROWS)
            dist, valid = _attn_band_consts(r0)
            ksl = [slice(kvh * hd, (kvh + 1) * hd) for kvh in range(A_KV_HEADS)]
            vsl = [slice(A_KV_WIDTH + kvh * hd, A_KV_WIDTH + (kvh + 1) * hd) for kvh in range(A_KV_HEADS)]
            kbs = [kvpad[pl.ds(r0, band), sl] for sl in ksl]
            vbs = [kvpad[pl.ds(r0, band), sl] for sl in vsl]
            dkbs = [dkvpad[pl.ds(r0, band), sl] for sl in ksl]
            dvbs = [dkvpad[pl.ds(r0, band), sl] for sl in vsl]
            qs = [qkv_ref[pl.ds(r0, A_Q_ROWS), h * hd:(h + 1) * hd] for h in range(A_HEADS)]
            dos = [do_ref[pl.ds(r0, A_Q_ROWS), h * hd:(h + 1) * hd] for h in range(A_HEADS)]
            fn = functools.partial(_attn_heads, valid=valid, dist=dist, dots=VJP_DOTS)
            _, vjp = jax.vjp(fn, qs, kbs, vbs, [sink_ref[:, h:h + 1] for h in range(A_HEADS)])
            dqs, dks, dvs, dss = vjp(dos)
            for h in range(A_HEADS):
                dqkv_ref[pl.ds(r0, A_Q_ROWS), h * hd:(h + 1) * hd] = dqs[h]
            for kvh in range(A_KV_HEADS):
                dkvpad[pl.ds(r0, band), ksl[kvh]] = dkbs[kvh] + dks[kvh]
                dkvpad[pl.ds(r0, band), vsl[kvh]] = dvbs[kvh] + dvs[kvh]
            return tuple(dsinks[h] + dss[h] for h in range(A_HEADS))

        dsinks = lax.fori_loop(0, s // A_Q_ROWS, chunk, tuple(jnp.zeros((1, 1), f32) for _ in range(A_HEADS)))
        dqkv_ref[:, A_WIDTH:] = dkvpad[A_WINDOW:, :]
        first = pl.program_id(0) == 0
        for h in range(A_HEADS):
            @pl.when(first)
            def _():
                dsink_ref[:, h:h + 1] = dsinks[h]

            @pl.when(jnp.logical_not(first))
            def _():
                dsink_ref[:, h:h + 1] += dsinks[h]

    wq = A_WIDTH + kvw
    return pl.pallas_call(
        body, grid=(nb,),
        in_specs=[pl.BlockSpec((s, wq), lambda b: (b, 0)), pl.BlockSpec((1, A_HEADS), lambda b: (0, 0)),
                  pl.BlockSpec((s, A_WIDTH), lambda b: (b, 0))],
        out_specs=[pl.BlockSpec((s, wq), lambda b: (b, 0)), pl.BlockSpec((1, A_HEADS), lambda b: (0, 0))],
        out_shape=[jax.ShapeDtypeStruct((t, wq), f32), jax.ShapeDtypeStruct((1, A_HEADS), f32)],
        scratch_shapes=[pltpu.VMEM((s + A_WINDOW, kvw), f32), pltpu.VMEM((s + A_WINDOW, kvw), f32)],
        compiler_params=_params(("arbitrary",)), name=name,
    )(qkv, sinks, do)


def _rg_gates(xc, wa, wx, ba, bx, lam, nn):
    r = jax.nn.sigmoid(nn(xc, wa) + ba)
    i = jax.nn.sigmoid(nn(xc, wx) + bx)
    log_a = -RG_C * r * jax.nn.softplus(-lam)
    a = jnp.exp(log_a)
    mult = jnp.sqrt(-jnp.tanh(log_a) * (jnp.exp(2.0 * log_a) + 1.0))
    return a, mult * (i * xc)


def _linear_scan(a, u, reverse):
    s = a.shape[0]
    t = lax.broadcasted_iota(jnp.int32, a.shape, 0)
    d = 1
    while d < s:
        if reverse:
            keep = t < s - d
            shift = s - d
        else:
            keep = t >= d
            shift = d
        us = jnp.where(keep, pltpu.roll(u, shift, 0), 0.0)
        as_ = jnp.where(keep, pltpu.roll(a, shift, 0), 1.0)
        u = u + a * us
        a = a * as_
        d *= 2
    return u


def rglru_fwd(xc, bg, wa, wx, ba, bx, lam, nb, *, name):
    t, c = xc.shape
    s = t // nb
    cw = GROUP_W

    def body(xc_ref, bg_ref, wa_ref, wx_ref, ba_ref, bx_ref, lam_ref, y_ref, h_ref):
        a, u = _rg_gates(xc_ref[...], wa_ref[...], wx_ref[...], ba_ref[...], bx_ref[...], lam_ref[...], RAW_DOTS[0])
        h = _linear_scan(a, u, False)
        h_ref[...] = h
        y_ref[...] = h * jax.nn.gelu(bg_ref[...])

    slab = pl.BlockSpec((s, cw), lambda b, g: (b, g))
    wsp = pl.BlockSpec((None, cw, cw), lambda b, g: (g, 0, 0))
    vec = pl.BlockSpec((1, cw), lambda b, g: (0, g))
    return pl.pallas_call(
        body, grid=(nb, c // cw), in_specs=[slab, slab, wsp, wsp, vec, vec, vec], out_specs=[slab, slab],
        out_shape=[jax.ShapeDtypeStruct((t, c), f32)] * 2,
        compiler_params=_params(("parallel", "parallel")), name=name,
    )(xc, bg, wa, wx, ba, bx, lam)


def rglru_bwd(xc, bg, h, dy, wa, wx, ba, bx, lam, nb, *, name):
    t, c = xc.shape
    s = t // nb
    cw = GROUP_W

    def body(xc_ref, bg_ref, h_ref, dy_ref, wa_ref, wx_ref, ba_ref, bx_ref, lam_ref,
             dxc_ref, dbg_ref, dwa_ref, dwx_ref, dba_ref, dbx_ref, dlam_ref):
        h = h_ref[...]
        dy_ = dy_ref[...]
        gel, gel_vjp = jax.vjp(jax.nn.gelu, bg_ref[...])
        dbg_ref[...] = gel_vjp(dy_ * h)[0]
        dh = dy_ * gel
        gates = functools.partial(_rg_gates, nn=_bnn)
        (a, _), gates_vjp = jax.vjp(gates, xc_ref[...], wa_ref[...], wx_ref[...], ba_ref[...], bx_ref[...],
                                    lam_ref[...])
        ti = lax.broadcasted_iota(jnp.int32, a.shape, 0)
        a_next = jnp.where(ti < s - 1, pltpu.roll(a, s - 1, 0), 0.0)
        lam_t = _linear_scan(a_next, dh, True)
        h_prev = jnp.where(ti >= 1, pltpu.roll(h, 1, 0), 0.0)
        dxc, dwa, dwx, dba, dbx, dlam = gates_vjp((lam_t * h_prev, lam_t))
        dxc_ref[...] = dxc
        first = pl.program_id(1) == 0

        @pl.when(first)
        def _():
            dwa_ref[...] = dwa
            dwx_ref[...] = dwx
            dba_ref[...] = dba
            dbx_ref[...] = dbx
            dlam_ref[...] = dlam

        @pl.when(jnp.logical_not(first))
        def _():
            dwa_ref[...] += dwa
            dwx_ref[...] += dwx
            dba_ref[...] += dba
            dbx_ref[...] += dbx
            dlam_ref[...] += dlam

    slab = pl.BlockSpec((s, cw), lambda g, b: (b, g))
    wsp = pl.BlockSpec((None, cw, cw), lambda g, b: (g, 0, 0))
    vec = pl.BlockSpec((1, cw), lambda g, b: (0, g))
    ng = c // cw
    return pl.pallas_call(
        body, grid=(ng, nb), in_specs=[slab, slab, slab, slab, wsp, wsp, vec, vec, vec],
        out_specs=[slab, slab, wsp, wsp, vec, vec, vec],
        out_shape=[jax.ShapeDtypeStruct((t, c), f32), jax.ShapeDtypeStruct((t, c), f32),
                   jax.ShapeDtypeStruct((ng, cw, cw), f32), jax.ShapeDtypeStruct((ng, cw, cw), f32),
                   jax.ShapeDtypeStruct((1, c), f32), jax.ShapeDtypeStruct((1, c), f32),
                   jax.ShapeDtypeStruct((1, c), f32)],
        compiler_params=_params(("parallel", "arbitrary")), name=name,
    )(xc, bg, h, dy, wa, wx, ba, bx, lam)


def _gdn_chunks_prep(qs, ks, vs, bls, als, a_log, dt_b, dots):
    nn, nt, csum = dots[0], dots[1], dots[3]
    hd = C_HEAD_DIM
    ri = lax.broadcasted_iota(jnp.int32, (CHUNK, CHUNK), 0)
    ci = lax.broadcasted_iota(jnp.int32, (CHUNK, CHUNK), 1)
    tril = ri >= ci
    strict = ri > ci
    eye = (ri == ci).astype(f32)
    qn = [q * lax.rsqrt(jnp.sum(q * q, -1, keepdims=True) + NORM_EPS) * (hd ** -0.5) for q in qs]
    kn = [k * lax.rsqrt(jnp.sum(k * k, -1, keepdims=True) + NORM_EPS) for k in ks]
    beta = [jax.nn.sigmoid(bl) for bl in bls]
    g = [-jnp.exp(a_log) * jax.nn.softplus(al + dt_b) for al in als]
    gc_sq = [csum(jnp.broadcast_to(g_, (CHUNK, CHUNK))) for g_ in g]
    gc = [csum(jnp.broadcast_to(g_, (CHUNK, hd))) for g_ in g]
    decay = [jnp.where(tril, jnp.exp(jnp.where(tril, s - s.T, 0.0)), 0.0) for s in gc_sq]
    kb = _each(jnp.multiply, kn, beta)
    kk = _each(nt, kb, kn)
    pw = [-jnp.where(strict, a * d, 0.0) for a, d in zip(kk, decay)]
    inv = [eye + p_ for p_ in pw]
    for _ in range(5):
        pw = _each(nn, pw, pw)
        inv = _each(jnp.add, inv, _each(nn, inv, pw))
    egc = [jnp.exp(c_) for c_ in gc]
    u = _each(nn, inv, _each(jnp.multiply, vs, beta))
    w = _each(nn, inv, _each(jnp.multiply, kb, egc))
    attn = _each(jnp.multiply, _each(nt, qn, kn), decay)
    g_last = [jnp.sum(jnp.broadcast_to(g_, (CHUNK, hd)), 0, keepdims=True) for g_ in g]
    qg = _each(jnp.multiply, qn, egc)
    kdec = [k_ * jnp.exp(gl_ - c_) for k_, gl_, c_ in zip(kn, g_last, gc)]
    return [(qg[i], kdec[i], w[i], u[i], attn[i], jnp.exp(g_last[i])) for i in range(len(qs))]


def _gdn_heads_step(states, qgs, kdecs, ws, us, attns, gls, zs, ng, dots):
    nn, tn = dots[0], dots[2]
    v_new = _each(jnp.subtract, us, _each(nn, ws, states))
    o = _each(jnp.add, _each(nn, qgs, states), _each(nn, attns, v_new))
    new = [s * gl for s, gl in zip(states, gls)]
    new = _each(jnp.add, new, _each(tn, kdecs, v_new))
    y = [o_ * lax.rsqrt(jnp.mean(o_ * o_, -1, keepdims=True) + NORM_EPS) * ng * _silu(z) for o_, z in zip(o, zs)]
    return y, new


def _loop_unrolled(n, unroll, load, compute, store, init):
    u = unroll if n % unroll == 0 else 1

    def trip(i, carry):
        idx = [i * u + j for j in range(u)]
        loaded = [load(k) for k in idx]
        results = compute(loaded)
        for k, r in zip(idx, results):
            carry = store(k, r, carry)
        return carry

    return lax.fori_loop(0, n // u, trip, init)


def _pick_lane(x, lane):
    li = lax.broadcasted_iota(jnp.int32, x.shape, 1)
    return jnp.sum(jnp.where(li == lane, x, 0.0), 1, keepdims=True)


def _put_lane(col, lane, width):
    li = lax.broadcasted_iota(jnp.int32, (col.shape[0], width), 1)
    return jnp.where(li == lane, col, 0.0)


def _gdn_specs(s, nc):
    hd = C_HEAD_DIM
    head = lambda off: pl.BlockSpec((s, hd), lambda b, h, off=off: (b, off + h))
    attn = pl.BlockSpec((None, s, CHUNK), lambda b, h: (h, b, 0))
    gl = pl.BlockSpec((None, nc * SUBLANES, hd), lambda b, h: (h, b, 0))
    ba = pl.BlockSpec((s, LANES), lambda b, h: (b, 0))
    sc8 = pl.BlockSpec((1, C_HEADS), lambda b, h: (0, 0))
    return head, attn, gl, ba, sc8


def gdn_prep_fwd(qkv, ba, a_log, dt_b, nb, *, name):
    t = qkv.shape[0]
    s = t // nb
    nc = s // CHUNK
    hd = C_HEAD_DIM
    head, attn_sp, gl_sp, ba_sp, sc8 = _gdn_specs(s, nc)

    def body(q_ref, k_ref, v_ref, ba_ref, alog_ref, dtb_ref, qg_ref, kd_ref, w_ref, u_ref, at_ref, gl_ref):
        h = pl.program_id(1)
        a_log_h = _pick_lane(alog_ref[...], h)
        dt_b_h = _pick_lane(dtb_ref[...], h)

        def load(n):
            rows = pl.ds(pl.multiple_of(n * CHUNK, CHUNK), CHUNK)
            bav = ba_ref[rows, :]
            return q_ref[rows, :], k_ref[rows, :], v_ref[rows, :], _pick_lane(bav, h), _pick_lane(bav, C_HEADS + h)

        def compute(loaded):
            return _gdn_chunks_prep(*[list(x) for x in zip(*loaded)], a_log_h, dt_b_h, RAW_DOTS)

        def store(n, outs, carry):
            rows = pl.ds(pl.multiple_of(n * CHUNK, CHUNK), CHUNK)
            qg_ref[rows, :] = outs[0].astype(bf16)
            kd_ref[rows, :] = outs[1].astype(bf16)
            w_ref[rows, :] = outs[2].astype(bf16)
            u_ref[rows, :] = outs[3]
            at_ref[rows, :] = outs[4].astype(bf16)
            gl_ref[pl.ds(pl.multiple_of(n * SUBLANES, SUBLANES), SUBLANES), :] = jnp.broadcast_to(outs[5], (SUBLANES, hd))
            return carry

        _loop_unrolled(nc, PREP_FWD_UNROLL, load, compute, store, 0)

    big = jax.ShapeDtypeStruct((t, C_WIDTH), f32)
    bigb = jax.ShapeDtypeStruct((t, C_WIDTH), bf16)
    return pl.pallas_call(
        body, grid=(nb, C_HEADS),
        in_specs=[head(0), head(C_HEADS), head(2 * C_HEADS), ba_sp, sc8, sc8],
        out_specs=[head(0)] * 4 + [attn_sp, gl_sp],
        out_shape=[bigb, bigb, bigb, big, jax.ShapeDtypeStruct((C_HEADS, t, CHUNK), bf16),
                               jax.ShapeDtypeStruct((C_HEADS, nb * nc * SUBLANES, hd), f32)],
        compiler_params=_params(("parallel", "parallel")), name=name,
    )(qkv, qkv, qkv, ba, a_log, dt_b)


def gdn_prep_bwd(qkv, ba, a_log, dt_b, cts, nb, *, name):
    t = qkv.shape[0]
    s = t // nb
    nc = s // CHUNK
    hd = C_HEAD_DIM
    head, attn_sp, gl_sp, ba_sp, sc8 = _gdn_specs(s, nc)

    def body(q_ref, k_ref, v_ref, ba_ref, alog_ref, dtb_ref, cqg, ckd, cw_, cu, cat, cgl,
             dq_ref, dk_ref, dv_ref, dba_ref, dalog_ref, ddtb_ref):
        b = pl.program_id(0)
        h = pl.program_id(1)
        a_log_h = _pick_lane(alog_ref[...], h)
        dt_b_h = _pick_lane(dtb_ref[...], h)
        prep = functools.partial(_gdn_chunks_prep, dots=VJP_DOTS)

        @pl.when(h == 0)
        def _():
            dba_ref[...] = jnp.zeros((s, LANES), f32)

        def load(n):
            rows = pl.ds(pl.multiple_of(n * CHUNK, CHUNK), CHUNK)
            bav = ba_ref[rows, :]
            cgl_n = cgl[pl.ds(pl.multiple_of(n * SUBLANES, SUBLANES), SUBLANES), :][0:1, :]
            primals = (q_ref[rows, :], k_ref[rows, :], v_ref[rows, :], _pick_lane(bav, h), _pick_lane(bav, C_HEADS + h))
            return primals, (cqg[rows, :], ckd[rows, :], cw_[rows, :], cu[rows, :], cat[rows, :], cgl_n), dba_ref[rows, :]

        def compute(loaded):
            primals = [list(x) for x in zip(*[item[0] for item in loaded])]
            _, vjp = jax.vjp(prep, *primals, a_log_h, dt_b_h)
            dqs, dks, dvs, dbls, dals, dalog, ddtb = vjp([item[1] for item in loaded])
            zero = jnp.zeros((1, 1), f32)
            return [((dqs[i], dks[i], dvs[i], dbls[i], dals[i], dalog if i == 0 else zero, ddtb if i == 0 else zero),
                     loaded[i][2]) for i in range(len(loaded))]

        def store(n, res, carry):
            (dq, dk, dv, dbl, dal, dalog_n, ddtb_n), dba_old = res
            rows = pl.ds(pl.multiple_of(n * CHUNK, CHUNK), CHUNK)
            dq_ref[rows, :] = dq
            dk_ref[rows, :] = dk
            dv_ref[rows, :] = dv
            dba_ref[rows, :] = dba_old + _put_lane(dbl, h, LANES) + _put_lane(dal, C_HEADS + h, LANES)
            return carry[0] + dalog_n, carry[1] + ddtb_n

        da_log, ddt_b = _loop_unrolled(nc, PREP_BWD_UNROLL, load, compute, store,
                                       (jnp.zeros((1, 1), f32), jnp.zeros((1, 1), f32)))
        first = jnp.logical_and(b == 0, h == 0)

        @pl.when(first)
        def _():
            dalog_ref[...] = _put_lane(da_log, h, LANES)
            ddtb_ref[...] = _put_lane(ddt_b, h, LANES)

        @pl.when(jnp.logical_not(first))
        def _():
            dalog_ref[...] += _put_lane(da_log, h, LANES)
            ddtb_ref[...] += _put_lane(ddt_b, h, LANES)

    big = jax.ShapeDtypeStruct((t, C_WIDTH), f32)
    vec = pl.BlockSpec((1, LANES), lambda b, h: (0, 0))
    return pl.pallas_call(
        body, grid=(nb, C_HEADS),
        in_specs=[head(0), head(C_HEADS), head(2 * C_HEADS), ba_sp, sc8, sc8] + [head(0)] * 4 + [attn_sp, gl_sp],
        out_specs=[head(0)] * 3 + [ba_sp, vec, vec],
        out_shape=[big] * 3 + [jax.ShapeDtypeStruct((t, LANES), f32), jax.ShapeDtypeStruct((1, LANES), f32),
                               jax.ShapeDtypeStruct((1, LANES), f32)],
        compiler_params=_params(("arbitrary", "arbitrary")), name=name,
    )(qkv, qkv, qkv, ba, a_log, dt_b, *cts)


def _gdn_rec_specs(sb, nsb, hp, reverse):
    hd = C_HEAD_DIM
    ncb = sb // CHUNK
    blk = (lambda b, k: b * nsb + (nsb - 1 - k)) if reverse else (lambda b, k: b * nsb + k)
    wide = pl.BlockSpec((sb, hp * hd), lambda b, j, k: (blk(b, k), j))
    attn = pl.BlockSpec((hp, sb, CHUNK), lambda b, j, k: (j, blk(b, k), 0))
    gl = pl.BlockSpec((hp, ncb * SUBLANES, hd), lambda b, j, k: (j, blk(b, k), 0))
    ng = pl.BlockSpec((1, hd), lambda b, j, k: (0, 0))
    states = pl.BlockSpec((hp, ncb, hd, hd), lambda b, j, k: (j, blk(b, k), 0, 0))
    return wide, attn, gl, ng, states


def gdn_rec_fwd(qg, kdec, w, u, attn, gl, z, ng, nb, *, name):
    t = qg.shape[0]
    s = t // nb
    sb = min(s, GDN_TIME_BLOCK)
    nsb = s // sb
    hd = C_HEAD_DIM
    hp = C_HEADS_PER_STEP
    wide, attn_sp, gl_sp, ng_sp, st_sp = _gdn_rec_specs(sb, nsb, hp, False)

    def body(qg_ref, kd_ref, w_ref, u_ref, at_ref, gl_ref, z_ref, ng_ref, y_ref, st_ref, carry_ref):
        @pl.when(pl.program_id(2) == 0)
        def _():
            carry_ref[...] = jnp.zeros((hp, hd, hd), f32)

        def chunk(n, states):
            for j in range(hp):
                st_ref[j, n] = states[j]
            rows = pl.ds(pl.multiple_of(n * CHUNK, CHUNK), CHUNK)
            grow = pl.ds(pl.multiple_of(n * SUBLANES, SUBLANES), SUBLANES)
            cols = [slice(j * hd, (j + 1) * hd) for j in range(hp)]
            ins = [(qg_ref[rows, c], kd_ref[rows, c], w_ref[rows, c], u_ref[rows, c], at_ref[j, rows, :],
                    gl_ref[j, grow, :][0:1, :], z_ref[rows, c]) for j, c in enumerate(cols)]
            ys, new = _gdn_heads_step(list(states), *[list(x) for x in zip(*ins)], ng_ref[...], RAW_DOTS)
            for j in range(hp):
                y_ref[rows, cols[j]] = ys[j]
            return tuple(new)

        last = lax.fori_loop(0, sb // CHUNK, chunk, tuple(carry_ref[j] for j in range(hp)))
        for j in range(hp):
            carry_ref[j] = last[j]

    return pl.pallas_call(
        body, grid=(nb, C_HEADS // hp, nsb),
        in_specs=[wide] * 4 + [attn_sp, gl_sp, wide, ng_sp], out_specs=[wide, st_sp],
        out_shape=[jax.ShapeDtypeStruct((t, C_WIDTH), f32), jax.ShapeDtypeStruct((C_HEADS, t // CHUNK, hd, hd), f32)],
        scratch_shapes=[pltpu.VMEM((hp, hd, hd), f32)],
        compiler_params=_params(("parallel", "parallel", "arbitrary")), name=name,
    )(qg, kdec, w, u, attn, gl, z, ng)


def gdn_rec_bwd(qg, kdec, w, u, attn, gl, z, ng, states, dy, nb, *, name):
    t = qg.shape[0]
    s = t // nb
    sb = min(s, GDN_TIME_BLOCK)
    nsb = s // sb
    nc = sb // CHUNK
    hd = C_HEAD_DIM
    hp = C_HEADS_PER_STEP
    wide, attn_sp, gl_sp, ng_sp, st_sp = _gdn_rec_specs(sb, nsb, hp, True)

    def body(qg_ref, kd_ref, w_ref, u_ref, at_ref, gl_ref, z_ref, ng_ref, states, dy_ref,
             dqg_ref, dkd_ref, dw_ref, du_ref, dat_ref, dgl_ref, dz_ref, dng_ref, carry_ref):
        step = functools.partial(_gdn_heads_step, dots=VJP_DOTS)

        @pl.when(pl.program_id(2) == 0)
        def _():
            carry_ref[...] = jnp.zeros((hp, hd, hd), f32)

        def operands(n):
            rows = pl.ds(pl.multiple_of(n * CHUNK, CHUNK), CHUNK)
            grow = pl.ds(pl.multiple_of(n * SUBLANES, SUBLANES), SUBLANES)
            cols = [slice(j * hd, (j + 1) * hd) for j in range(hp)]
            return ([qg_ref[rows, c].astype(f32) for c in cols], [kd_ref[rows, c].astype(f32) for c in cols],
                    [w_ref[rows, c].astype(f32) for c in cols], [u_ref[rows, c] for c in cols],
                    [at_ref[j, rows, :].astype(f32) for j in range(hp)],
                    [gl_ref[j, grow, :][0:1, :] for j in range(hp)], [z_ref[rows, c] for c in cols])

        def bwd_chunk(i, carry):
            n = nc - 1 - i
            rows = pl.ds(pl.multiple_of(n * CHUNK, CHUNK), CHUNK)
            grow = pl.ds(pl.multiple_of(n * SUBLANES, SUBLANES), SUBLANES)
            dsts, dng = carry
            dys = [dy_ref[rows, j * hd:(j + 1) * hd] for j in range(hp)]
            _, vjp = jax.vjp(step, [states[j, n] for j in range(hp)], *operands(n), ng_ref[...])
            dst, dqg, dkd, dw, du, dat, dgl, dz, dng_n = vjp((dys, list(dsts)))
            for j in range(hp):
                cols = slice(j * hd, (j + 1) * hd)
                dqg_ref[rows, cols] = dqg[j]
                dkd_ref[rows, cols] = dkd[j]
                dw_ref[rows, cols] = dw[j]
                du_ref[rows, cols] = du[j]
                dat_ref[j, rows, :] = dat[j]
                dgl_ref[j, grow, :] = jnp.broadcast_to(dgl[j], (SUBLANES, hd))
                dz_ref[rows, cols] = dz[j]
            return tuple(dst), dng + dng_n

        dlast, dng = lax.fori_loop(0, nc, bwd_chunk,
                                   (tuple(carry_ref[j] for j in range(hp)), jnp.zeros((1, hd), f32)))
        for j in range(hp):
            carry_ref[j] = dlast[j]
        first = jnp.logical_and(jnp.logical_and(pl.program_id(0) == 0, pl.program_id(1) == 0), pl.program_id(2) == 0)

        @pl.when(first)
        def _():
            dng_ref[...] = dng

        @pl.when(jnp.logical_not(first))
        def _():
            dng_ref[...] += dng

    big = jax.ShapeDtypeStruct((t, C_WIDTH), f32)
    return pl.pallas_call(
        body, grid=(nb, C_HEADS // hp, nsb),
        in_specs=[wide] * 4 + [attn_sp, gl_sp, wide, ng_sp, st_sp, wide],
        out_specs=[wide] * 4 + [attn_sp, gl_sp, wide, ng_sp],
        out_shape=[big] * 4 + [jax.ShapeDtypeStruct(attn.shape, f32), jax.ShapeDtypeStruct(gl.shape, f32), big,
                               jax.ShapeDtypeStruct((1, hd), f32)],
        scratch_shapes=[pltpu.VMEM((hp, hd, hd), f32)],
        compiler_params=_params(("arbitrary", "arbitrary", "arbitrary")), name=name,
    )(qg, kdec, w, u, attn, gl, z, ng, states, dy)


def _blockdiag_slabs(w):
    per = GROUP_W // B_BLOCK
    slabs = jnp.zeros((B_BLOCKS // per, GROUP_W, GROUP_W), w.dtype)
    for h in range(B_BLOCKS):
        o = (h % per) * B_BLOCK
        slabs = slabs.at[h // per, o:o + B_BLOCK, o:o + B_BLOCK].set(w[h])
    return slabs


def _slab_blocks(slabs):
    per = GROUP_W // B_BLOCK
    return jnp.stack([slabs[h // per, (h % per) * B_BLOCK:(h % per + 1) * B_BLOCK,
                            (h % per) * B_BLOCK:(h % per + 1) * B_BLOCK] for h in range(B_BLOCKS)])


def _mixer_ab_fwd(x1, x1b, W, g, b, nb, tag):
    w_in = W["ab_w_in"][0].astype(bf16)
    o1, o2 = A_WIDTH + 2 * A_KV_WIDTH, A_WIDTH + 2 * A_KV_WIDTH + B_WIDTH
    w_qkv, w_bx, w_bg = w_in[:, :o1], w_in[:, o1:o2], w_in[:, o2:]
    pqkv = mm_nn(x1b,w_qkv, name=tag + "_in_qkv")
    pbx = mm_nn(x1b,w_bx, name=tag + "_in_bx")
    pbg = mm_nn(x1b,w_bg, name=tag + "_in_bg")
    ya = attn_fwd(pqkv, W["a_sinks"], nb, name=tag + "_attn_fwd")
    xc = conv_fwd(pbx, W["b_conv_w"][0], W["b_conv_b"], False, nb, name=tag + "_conv_fwd")
    wa_s, wx_s = _blockdiag_slabs(W["b_wa"][0]), _blockdiag_slabs(W["b_wx"][0])
    yb, hh = rglru_fwd(xc, pbg, wa_s, wx_s, W["b_ba"], W["b_bx"], W["b_lam"], nb, name=tag + "_rglru_fwd")
    w_out = W["ab_w_out"][0].astype(bf16)
    x2, z1 = proj_ln([ya, yb], [w_out[:A_WIDTH], w_out[A_WIDTH:]], x1, g, b, name=tag + "_out_ln")
    saved = (pqkv, pbx, pbg, ya, xc, yb, hh, wa_s, wx_s, w_qkv, w_bx, w_bg, w_out)
    return x2, z1, saved


def _mixer_ab_bwd(x1b, dz1, dz1b, W, saved, nb, tag):
    pqkv, pbx, pbg, ya, xc, yb, hh, wa_s, wx_s, w_qkv, w_bx, w_bg, w_out = saved
    dya = mm_nn(dz1b, w_out[:A_WIDTH].T, name=tag + "_dya")
    dyb = mm_nn(dz1b, w_out[A_WIDTH:].T, name=tag + "_dyb")
    dwo = jnp.concatenate([mm_tn(ya, dz1b, name=tag + "_dwo_a"), mm_tn(yb, dz1b, name=tag + "_dwo_b")], 0)
    dpqkv, dsinks = attn_bwd(pqkv, W["a_sinks"], dya, nb, name=tag + "_attn_bwd")
    dxc, dpbg, dwa_s, dwx_s, dba, dbx, dlam = rglru_bwd(xc, pbg, hh, dyb, wa_s, wx_s, W["b_ba"], W["b_bx"],
                                                       W["b_lam"], nb, name=tag + "_rglru_bwd")
    dpbx, dconv_w, dconv_b = conv_bwd(pbx, W["b_conv_w"][0], W["b_conv_b"], dxc, False, nb, name=tag + "_conv_bwd")
    dw_in = jnp.concatenate([mm_tn(x1b,dpqkv, name=tag + "_dwin_qkv"), mm_tn(x1b,dpbx, name=tag + "_dwin_bx"),
                             mm_tn(x1b,dpbg, name=tag + "_dwin_bg")], 1)
    dx1 = mm_nn(dpqkv, w_qkv.T, add=dz1, add_scale=DN_ALPHA, name=tag + "_dx_qkv")
    dx1 = mm_nn(dpbx, w_bx.T, add=dx1, name=tag + "_dx_bx")
    dx1 = mm_nn(dpbg, w_bg.T, add=dx1, name=tag + "_dx_bg")
    grads = {"ab_w_in": dw_in[None], "a_sinks": dsinks, "b_conv_w": dconv_w[None], "b_conv_b": dconv_b,
             "b_wa": _slab_blocks(dwa_s)[None], "b_ba": dba, "b_wx": _slab_blocks(dwx_s)[None], "b_bx": dbx,
             "b_lam": dlam, "ab_w_out": dwo[None]}
    return dx1, grads


def _mixer_c_fwd(x1, x1b, W, g, b, nb, tag):
    w_in = W["c_w_in"][0].astype(bf16)
    d = w_in.shape[0]
    o1, o2 = 3 * C_WIDTH, 4 * C_WIDTH
    w_qkv, w_z = w_in[:, :o1], w_in[:, o1:o2]
    w_ba = jnp.concatenate([w_in[:, o2:], jnp.zeros((d, LANES - 2 * C_HEADS), bf16)], 1)
    pqkv = mm_nn(x1b,w_qkv, name=tag + "_in_qkv")
    pz = mm_nn(x1b,w_z, name=tag + "_in_z")
    pba = mm_nn(x1b,w_ba, name=tag + "_in_ba")
    zero_b = jnp.zeros((1, o1), f32)
    qkvc = conv_fwd(pqkv, W["c_conv_w"][0], zero_b, True, nb, name=tag + "_conv_fwd")
    prep = gdn_prep_fwd(qkvc, pba, W["c_a_log"], W["c_dt_bias"], nb, name=tag + "_prep_fwd")
    yc, states = gdn_rec_fwd(*prep, pz, W["c_norm_g"], nb, name=tag + "_rec_fwd")
    w_out = W["c_w_out"][0].astype(bf16)
    x2, z1 = proj_ln([yc], [w_out], x1, g, b, name=tag + "_out_ln")
    saved = (pqkv, pz, pba, qkvc, prep, states, yc, w_qkv, w_z, w_ba, w_out, zero_b)
    return x2, z1, saved


def _mixer_c_bwd(x1b, dz1, dz1b, W, saved, nb, tag):
    pqkv, pz, pba, qkvc, prep, states, yc, w_qkv, w_z, w_ba, w_out, zero_b = saved
    dyc = mm_nn(dz1b, w_out.T, name=tag + "_dyc")
    dwo = mm_tn(yc, dz1b, name=tag + "_dwo")
    rec = gdn_rec_bwd(*prep, pz, W["c_norm_g"], states, dyc, nb, name=tag + "_rec_bwd")
    cts, dpz, dng = rec[:6], rec[6], rec[7]
    dq, dk, dv, dpba, dalog, ddtb = gdn_prep_bwd(qkvc, pba, W["c_a_log"], W["c_dt_bias"], cts, nb,
                                                 name=tag + "_prep_bwd")
    dqkvc = jnp.concatenate([dq, dk, dv], 1)
    dpqkv, dconv_w, _ = conv_bwd(pqkv, W["c_conv_w"][0], zero_b, dqkvc, True, nb, name=tag + "_conv_bwd")
    dw_in = jnp.concatenate([mm_tn(x1b,dpqkv, name=tag + "_dwin_qkv"), mm_tn(x1b,dpz, name=tag + "_dwin_z"),
                             mm_tn(x1b,dpba, name=tag + "_dwin_ba")[:, :2 * C_HEADS]], 1)
    dx1 = mm_nn(dpqkv, w_qkv.T, add=dz1, add_scale=DN_ALPHA, name=tag + "_dx_qkv")
    dx1 = mm_nn(dpz, w_z.T, add=dx1, name=tag + "_dx_z")
    dx1 = mm_nn(dpba, w_ba.T, add=dx1, name=tag + "_dx_ba")
    grads = {"c_w_in": dw_in[None], "c_conv_w": dconv_w[None], "c_a_log": dalog[:, :C_HEADS],
             "c_dt_bias": ddtb[:, :C_HEADS], "c_norm_g": dng, "c_w_out": dwo[None]}
    return dx1, grads


def _local_step(x, p, target, W, F, on_ffn_grads):
    nb, s, d = x.shape
    t = nb * s
    h = x.reshape(t, d)
    tape = []
    for i in range(DEPTH):
        tag = f"l{i}"
        f1 = [F[k][i] for k in ("ffn1_wg", "ffn1_wu", "ffn1_wd")]
        f2 = [F[k][i] for k in ("ffn2_wg", "ffn2_wu", "ffn2_wd")]
        lg = [W["ln_g"][i, k][None] for k in range(3)]
        lb = [W["ln_b"][i, k][None] for k in range(3)]
        x1, z0, x1b = ffn_fwd(h, *f1, lg[0], lb[0], name=tag + "_ffn1_fwd")
        mixer = _mixer_ab_fwd if i % 2 == 0 else _mixer_c_fwd
        x2, z1, msaved = mixer(x1, x1b, W, lg[1], lb[1], nb, tag + "_mix")
        x3, z2, _ = ffn_fwd(x2, *f2, lg[2], lb[2], name=tag + "_ffn2_fwd")
        pi = p[i].reshape(t, -1)
        pw = (W["ple_wg"][i].astype(bf16), W["ple_bg"][i][None], W["ple_wp"][i].astype(bf16))
        x4 = ple_fwd(x3, pi, *pw, name=tag + "_ple_fwd")
        tape.append((h, z0, x1b, msaved, z1, x2, z2, x3, pi, pw, lg))
        h = x4
    dh, sq = loss_head(h, target.reshape(t, d), name="loss_head")
    loss = 0.5 * jnp.sum(sq) / d
    per_layer = [None] * DEPTH
    grads = {}
    for i in reversed(range(DEPTH)):
        tag = f"l{i}"
        h_in, z0, x1b, msaved, z1, x2, z2, x3, pi, pw, lg = tape[i]
        dx3, dple_wg, dple_bg, dple_wp = ple_bwd(x3, pi, dh, pw[0], pw[0].T, pw[1], pw[2], name=tag + "_ple_bwd")
        dz2, dz2b, dg2, db2 = ln_bwd(z2, dx3, lg[2], name=tag + "_ln2_bwd")
        f1 = [F[k][i] for k in ("ffn1_wg", "ffn1_wu", "ffn1_wd")]
        f2 = [F[k][i] for k in ("ffn2_wg", "ffn2_wu", "ffn2_wd")]
        dgate, dup, *df2 = ffn_bwd_weights(x2.astype(bf16), dz2b, *f2, name=tag + "_ffn2_bwd_w")
        on_ffn_grads(i, 3, df2)
        dx2 = ffn_bwd_input(dgate, dup, f2[0], f2[1], dz2, name=tag + "_ffn2_bwd_x")
        dz1, dz1b, dg1, db1 = ln_bwd(z1, dx2, lg[1], name=tag + "_ln1_bwd")
        mixer_bwd = _mixer_ab_bwd if i % 2 == 0 else _mixer_c_bwd
        dx1, mgrads = mixer_bwd(x1b, dz1, dz1b, W, msaved, nb, tag + "_mix")
        grads.update(mgrads)
        dz0, dz0b, dg0, db0 = ln_bwd(z0, dx1, lg[0], name=tag + "_ln0_bwd")
        dgate, dup, *df1 = ffn_bwd_weights(h_in.astype(bf16), dz0b, *f1, name=tag + "_ffn1_bwd_w")
        on_ffn_grads(i, 0, df1)
        dh = ffn_bwd_input(dgate, dup, f1[0], f1[1], dz0, name=tag + "_ffn1_bwd_x")
        per_layer[i] = {"ln_g": jnp.concatenate([dg0, dg1, dg2], 0), "ln_b": jnp.concatenate([db0, db1, db2], 0),
                        "ple_wg": dple_wg, "ple_bg": dple_bg[0], "ple_wp": dple_wp}
    for k in per_layer[0]:
        grads[k] = jnp.stack([per_layer[i][k] for i in range(DEPTH)])
    return loss, dh.reshape(nb, s, d), grads


WEIGHT_NAMES = ("ffn1_wg", "ffn1_wu", "ffn1_wd", "ffn2_wg", "ffn2_wu", "ffn2_wd", "ln_g", "ln_b", "ple_wg", "ple_bg",
                "ple_wp", "ab_w_in", "a_sinks", "b_conv_w", "b_conv_b", "b_wa", "b_ba", "b_wx", "b_bx", "b_lam",
                "ab_w_out", "c_w_in", "c_conv_w", "c_a_log", "c_dt_bias", "c_norm_g", "c_w_out")
NATIVE_NAMES = WEIGHT_NAMES[:6]
PACKED_NAMES = WEIGHT_NAMES[6:]
PACK_MATRICES = ("ple_wg", "ple_wp", "ab_w_in", "ab_w_out", "c_w_in", "c_w_out")
PACK_GROUPS = (tuple(k for k in PACKED_NAMES if k not in PACK_MATRICES), PACK_MATRICES)
PACK_TRANSIT = (f32, bf16)
SHARD_AXIS = {"ffn1_wg": 2, "ffn1_wu": 2, "ffn1_wd": 1, "ffn2_wg": 2, "ffn2_wu": 2, "ffn2_wd": 1, "ln_g": 2, "ln_b": 2,
              "ple_wg": 1, "ple_wp": 2, "ab_w_in": 2, "b_conv_w": 2, "ab_w_out": 1, "c_w_in": 2, "c_conv_w": 2,
              "c_w_out": 1}
N_CHIPS = 4
PACK_COLS = LANES
PACK_TILE_MULTIPLE = 256
ELEMENTWISE_BLOCK_ELEMS = 128 * 1024


def _row_tile(r, cols):
    return _tile(r, max(2 * SUBLANES, ELEMENTWISE_BLOCK_ELEMS // cols), 2 * SUBLANES)
MESH = pl.DeviceIdType.MESH
ANY = pl.BlockSpec(memory_space=pl.ANY)


def _tiled_dims(shape):
    w = shape[-1]
    r = 1
    for dim in shape[:-1]:
        r *= dim
    return r, w, -(-r // SUBLANES) * SUBLANES, -(-w // LANES) * LANES


def _pack(pieces, lead=()):
    k = len(lead)
    tiles = []
    for a in pieces:
        r, w, rp, wp = _tiled_dims(a.shape[k:])
        a2 = jnp.pad(a.reshape(lead + (r, w)), [(0, 0)] * k + [(0, rp - r), (0, wp - w)])
        a2 = a2.reshape(lead + (rp // SUBLANES, SUBLANES, wp // LANES, LANES))
        a2 = jnp.swapaxes(a2, k + 1, k + 2)
        tiles.append(a2.reshape(lead + (-1, SUBLANES, LANES)))
    flat = jnp.concatenate(tiles, axis=k)
    n = flat.shape[k]
    n_pad = -(-n // PACK_TILE_MULTIPLE) * PACK_TILE_MULTIPLE
    flat = jnp.pad(flat, [(0, 0)] * k + [(0, n_pad - n), (0, 0), (0, 0)])
    return flat.reshape(lead + (n_pad * SUBLANES, PACK_COLS))


def _unpack(pack, shapes, lead=()):
    k = len(lead)
    flat = pack.reshape(lead + (-1, SUBLANES, LANES))
    out, o = [], 0
    for shp in shapes:
        r, w, rp, wp = _tiled_dims(shp)
        n = (rp // SUBLANES) * (wp // LANES)
        a2 = lax.slice_in_dim(flat, o, o + n, axis=k).reshape(lead + (rp // SUBLANES, wp // LANES, SUBLANES, LANES))
        a2 = jnp.swapaxes(a2, k + 1, k + 2).reshape(lead + (rp, wp))
        a2 = lax.slice_in_dim(lax.slice_in_dim(a2, 0, r, axis=k), 0, w, axis=k + 1)
        out.append(a2.reshape(lead + tuple(shp)))
        o += n
    return out


def _mesh_position():
    x, y, c = lax.axis_index("x"), lax.axis_index("y"), lax.axis_index("c")
    chips = [(1 - x, y), (x, 1 - y), (1 - x, 1 - y)]
    return x, y, c, chips


def _remote(src, dst, send_sems, recv_sems, k, to):
    return pltpu.make_async_remote_copy(src_ref=src, dst_ref=dst, send_sem=send_sems.at[k], recv_sem=recv_sems.at[k],
                                        device_id=to, device_id_type=MESH)


def _sems(n):
    return pltpu.SemaphoreType.DMA((n,))


def place_slot(parts, slots, n_slots, dtype, from_slot, *, name):
    n = len(parts)
    r, cols = parts[0].shape[-2:]
    tr = _row_tile(r, cols)

    def body(src_ref, dst_ref, *refs):
        for a in range(n):
            refs[n + a][...] = refs[a][...].astype(dtype)

    dst = pl.BlockSpec((None, tr, cols), lambda i, src_ref, dst_ref: (dst_ref[0], i, 0))
    src = (pl.BlockSpec((None, tr, cols), lambda i, src_ref, dst_ref: (src_ref[0], i, 0)) if from_slot
           else pl.BlockSpec((tr, cols), lambda i, src_ref, dst_ref: (i, 0)))
    return pl.pallas_call(
        body,
        grid_spec=pltpu.PrefetchScalarGridSpec(num_scalar_prefetch=2, grid=(r // tr,), in_specs=[src] * n,
                                               out_specs=[dst] * n),
        out_shape=[jax.ShapeDtypeStruct((n_slots, r, cols), dtype)] * n,
        compiler_params=_params(("parallel",)), name=name,
    )(*slots, *parts)


def gather_shards(bufs, *, name):
    n = len(bufs)

    def body(*refs):
        out_refs = refs[n:2 * n]
        send_sems, recv_sems = refs[2 * n:]
        x, y, c, chips = _mesh_position()
        me = 2 * x + y
        sibling = (x, y, 1 - c)
        waits = []
        for j, (cx, cy) in enumerate(chips):
            for a in range(n):
                own = out_refs[a].at[me, c]
                cp = _remote(own, own, send_sems, recv_sems, 6 * a + j, (cx, cy, c))
                cp.start()
                waits.append(cp.wait_send)
        for j, (cx, cy) in enumerate(chips):
            for a in range(n):
                got = out_refs[a].at[2 * cx + cy, c]
                _remote(got, got, send_sems, recv_sems, 6 * a + j, (cx, cy, c)).wait_recv()
                fw = _remote(got, got, send_sems, recv_sems, 6 * a + 3 + j, sibling)
                fw.start()
                waits.append(fw.wait_send)
        for j, (cx, cy) in enumerate(chips):
            for a in range(n):
                got = out_refs[a].at[2 * cx + cy, 1 - c]
                _remote(got, got, send_sems, recv_sems, 6 * a + 3 + j, sibling).wait_recv()
        for wait in waits:
            wait()

    return pl.pallas_call(
        body, out_shape=[jax.ShapeDtypeStruct(b.shape, b.dtype) for b in bufs],
        in_specs=[ANY] * n, out_specs=[ANY] * n, scratch_shapes=[_sems(6 * n), _sems(6 * n)],
        input_output_aliases={a: a for a in range(n)}, name=name,
    )(*bufs)


def chip_exchange(ps, qs, *, name):
    n = len(ps)

    def body(*refs):
        p_refs, q_refs = refs[:n], refs[2 * n:3 * n]
        send_sems, recv_sems = refs[3 * n:]
        x, y, c, chips = _mesh_position()
        me = 2 * x + y
        waits = []
        for j, (cx, cy) in enumerate(chips):
            for a in range(n):
                cp = _remote(p_refs[a].at[2 * cx + cy], q_refs[a].at[me], send_sems, recv_sems, 3 * a + j, (cx, cy, c))
                cp.start()
                waits.append(cp.wait_send)
        for j, (cx, cy) in enumerate(chips):
            for a in range(n):
                got = q_refs[a].at[2 * cx + cy]
                _remote(got, got, send_sems, recv_sems, 3 * a + j, (cx, cy, c)).wait_recv()
        for wait in waits:
            wait()

    return pl.pallas_call(
        body, out_shape=[jax.ShapeDtypeStruct(q_.shape, q_.dtype) for q_ in qs], in_specs=[ANY] * (2 * n),
        out_specs=[ANY] * n, scratch_shapes=[_sems(3 * n), _sems(3 * n)],
        input_output_aliases={n + a: a for a in range(n)}, name=name,
    )(*ps, *qs)


def gather_slots_async(bufs, collective_id, sources=None, *, name):
    n = len(bufs)
    refs = [jax.new_ref(b, memory_space=pltpu.MemorySpace.HBM) for b in bufs]
    src_refs = None if sources is None else [jax.new_ref(s_, memory_space=pltpu.MemorySpace.HBM) for s_ in sources]

    @pl.kernel(mesh=plsc.ScalarSubcoreMesh(axis_name="sequencer", num_cores=1), name=name,
               scratch_types=(_sems(3 * n), _sems(3 * n)),
               compiler_params=pltpu.CompilerParams(collective_id=collective_id))
    def launch(send_sems, recv_sems):
        x, y, c, chips = _mesh_position()
        me = 2 * x + y
        barrier = pltpu.get_barrier_semaphore()
        for cx, cy in chips:
            pl.semaphore_signal(barrier, inc=1, device_id=(cx, cy, c), device_id_type=MESH)
        pl.semaphore_wait(barrier, len(chips))
        sends = []
        for j, (cx, cy) in enumerate(chips):
            for a in range(n):
                own = refs[a].at[me]
                src = own if src_refs is None else src_refs[a].at[2 * cx + cy]
                cp = _remote(src, own, send_sems, recv_sems, 3 * a + j, (cx, cy, c))
                cp.start()
                sends.append(cp)
        for j, (cx, cy) in enumerate(chips):
            for a in range(n):
                got = refs[a].at[2 * cx + cy]
                _remote(got, got, send_sems, recv_sems, 3 * a + j, (cx, cy, c)).wait_recv()
        for cp in sends:
            cp.wait_send()

    launch()
    return [r[...] for r in refs]


N_DEVICES = 8
PEER_FLIPS = tuple((dx, dy, dc) for dx in (0, 1) for dy in (0, 1) for dc in (0, 1) if dx or dy or dc)


def exchange_partials_async(sends, recvs, collective_id, *, name):
    n = len(sends)
    s_refs = [jax.new_ref(a, memory_space=pltpu.MemorySpace.HBM) for a in sends]
    r_refs = [jax.new_ref(a, memory_space=pltpu.MemorySpace.HBM) for a in recvs]
    k = len(PEER_FLIPS)

    @pl.kernel(mesh=plsc.ScalarSubcoreMesh(axis_name="sequencer", num_cores=1), name=name,
               scratch_types=(_sems(k), _sems(k)), compiler_params=pltpu.CompilerParams(collective_id=collective_id))
    def launch(send_sems, recv_sems):
        x, y, c, _ = _mesh_position()
        me = 4 * x + 2 * y + c
        peers = [(1 - x if dx else x, 1 - y if dy else y, 1 - c if dc else c) for dx, dy, dc in PEER_FLIPS]
        barrier = pltpu.get_barrier_semaphore()
        for peer in peers:
            pl.semaphore_signal(barrier, inc=1, device_id=peer, device_id_type=MESH)
        pl.semaphore_wait(barrier, len(peers))
        sends_started = []
        for j, (px, py, pc) in enumerate(peers):
            for a in range(n):
                cp = _remote(s_refs[a].at[2 * px + py], r_refs[a].at[me], send_sems, recv_sems, j, (px, py, pc))
                cp.start()
                sends_started.append(cp)
        for j, (px, py, pc) in enumerate(peers):
            for a in range(n):
                got = r_refs[a].at[4 * px + 2 * py + pc]
                _remote(got, got, send_sems, recv_sems, j, (px, py, pc)).wait_recv()
        for cp in sends_started:
            cp.wait_send()

    launch()
    return [r[...] for r in r_refs]


def sibling_exchange(gs, *, name):
    n = len(gs)

    def body(*refs):
        g_refs, out_refs = refs[:n], refs[n:2 * n]
        send_sems, recv_sems = refs[2 * n:]
        x, y, c, _ = _mesh_position()
        cps = [_remote(g_refs[a].at[:, 1 - c], out_refs[a], send_sems, recv_sems, a, (x, y, 1 - c)) for a in range(n)]
        for cp in cps:
            cp.start()
        for cp in cps:
            cp.wait()

    return pl.pallas_call(
        body, out_shape=[jax.ShapeDtypeStruct(g.shape[:1] + g.shape[2:], g.dtype) for g in gs],
        in_specs=[ANY] * n, out_specs=[ANY] * n, scratch_shapes=[_sems(n), _sems(n)], name=name,
    )(*gs)


def add_own_half(gs, others, c_idx, dtype, *, name):
    n = len(gs)
    ns, _, r, cols = gs[0].shape
    tr = _row_tile(r, cols)

    def body(c_ref, *refs):
        for a in range(n):
            refs[2 * n + a][...] = (refs[a][...] + refs[n + a][...]).astype(dtype)

    own = pl.BlockSpec((None, None, tr, cols), lambda s, i, c_ref: (s, c_ref[0], i, 0))
    oth = pl.BlockSpec((None, tr, cols), lambda s, i, c_ref: (s, i, 0))
    return pl.pallas_call(
        body,
        grid_spec=pltpu.PrefetchScalarGridSpec(num_scalar_prefetch=1, grid=(ns, r // tr),
                                               in_specs=[own] * n + [oth] * n, out_specs=[oth] * n),
        out_shape=[jax.ShapeDtypeStruct((ns, r, cols), dtype)] * n,
        compiler_params=_params(("parallel", "parallel")), name=name,
    )(c_idx, *gs, *others)


def sum_slots(qs, *, name):
    n = len(qs)
    ns, r, cols = qs[0].shape
    tr = _row_tile(r, cols * ns)

    def body(*refs):
        for a in range(n):
            q_ref = refs[a]
            acc = q_ref[0].astype(f32) + q_ref[1].astype(f32)
            for i in range(2, ns):
                acc = acc + q_ref[i].astype(f32)
            refs[n + a][...] = acc

    return pl.pallas_call(
        body, grid=(r // tr,), in_specs=[pl.BlockSpec((ns, tr, cols), lambda i: (0, i, 0))] * n,
        out_specs=[pl.BlockSpec((tr, cols), lambda i: (i, 0))] * n,
        out_shape=[jax.ShapeDtypeStruct((r, cols), f32)] * n,
        compiler_params=_params(("parallel",)), name=name,
    )(*qs)


def sibling_share(bufs, *, name):
    n = len(bufs)

    def body(*refs):
        out_refs = refs[n:2 * n]
        send_sems, recv_sems = refs[2 * n:]
        x, y, c, _ = _mesh_position()
        sibling = (x, y, 1 - c)
        cps = []
        for a in range(n):
            own = out_refs[a].at[c]
            cp = _remote(own, own, send_sems, recv_sems, a, sibling)
            cp.start()
            cps.append(cp)
        for a in range(n):
            theirs = out_refs[a].at[1 - c]
            _remote(theirs, theirs, send_sems, recv_sems, a, sibling).wait_recv()
        for cp in cps:
            cp.wait_send()

    return pl.pallas_call(
        body, out_shape=[jax.ShapeDtypeStruct(b.shape, b.dtype) for b in bufs], in_specs=[ANY] * n,
        out_specs=[ANY] * n, scratch_shapes=[_sems(n), _sems(n)],
        input_output_aliases={a: a for a in range(n)}, name=name,
    )(*bufs)


def adamw(ws, gs, ms, vs, *, name):
    n = len(ws)
    r, cols = ws[0].shape
    tr = _row_tile(r, cols)

    def body(*refs):
        for a in range(n):
            w_ref, g_ref, m_ref, v_ref = (refs[k * n + a] for k in range(4))
            d_ref, m2_ref, v2_ref = (refs[(4 + k) * n + a] for k in range(3))
            g_ = g_ref[...]
            m2 = ADAM_B1 * m_ref[...] + (1.0 - ADAM_B1) * g_
            v2 = ADAM_B2 * v_ref[...] + (1.0 - ADAM_B2) * (g_ * g_)
            m_hat = m2 / (1.0 - ADAM_B1 ** ADAM_STEP)
            v_hat = v2 / (1.0 - ADAM_B2 ** ADAM_STEP)
            d_ref[...] = -ADAM_LR * (m_hat / (jnp.sqrt(v_hat) + ADAM_EPS) + ADAM_WD * w_ref[...])
            m2_ref[...] = m2
            v2_ref[...] = v2

    row = pl.BlockSpec((tr, cols), lambda i: (i, 0))
    out = pl.pallas_call(
        body, grid=(r // tr,), in_specs=[row] * (4 * n), out_specs=[row] * (3 * n),
        out_shape=[jax.ShapeDtypeStruct((r, cols), f32)] * (3 * n),
        compiler_params=_params(("parallel",)), name=name,
    )(*ws, *gs, *ms, *vs)
    return out[:n], out[n:2 * n], out[2 * n:]


def _full_weights(gathered, names, weights):
    pieces = _unpack(gathered, [weights[k].shape for k in names], lead=(N_CHIPS,))
    full = {}
    for name, pc in zip(names, pieces):
        ax = SHARD_AXIS.get(name)
        if ax is None:
            full[name] = weights[name]
        else:
            shp = weights[name].shape
            full[name] = jnp.moveaxis(pc, 0, ax).reshape(shp[:ax] + (N_CHIPS * shp[ax],) + shp[ax + 1:])
    return full


def _grad_pack(grads, names, shapes):
    pieces = []
    for name, shp in zip(names, shapes):
        g = grads[name]
        ax = SHARD_AXIS.get(name)
        if ax is None:
            pieces.append(jnp.broadcast_to(g.reshape(shp)[None], (N_CHIPS,) + tuple(shp)))
        else:
            pieces.append(jnp.stack(jnp.split(g, N_CHIPS, axis=ax)))
    return _pack(pieces, lead=(N_CHIPS,))


def _by_shape(arrays):
    groups = {}
    for i, a in enumerate(arrays):
        groups.setdefault(a.shape, []).append(i)
    return list(groups.values())


def _grouped(fn, lists, n_out, tag):
    outs = [[None] * len(lists[0]) for _ in range(n_out)]
    for gi, idx in enumerate(_by_shape(lists[0])):
        res = fn(*[[lst[i] for i in idx] for lst in lists], name=f"{tag}_{gi}")
        res = res if n_out > 1 else (res,)
        for k in range(n_out):
            for i, r in zip(idx, res[k]):
                outs[k][i] = r
    return outs if n_out > 1 else outs[0]


def _train_step(x, p, loss_target, weights, m, v):
    shapes = [[weights[k].shape for k in names] for names in PACK_GROUPS]
    halves = lambda a: a.reshape((2, a.shape[0] // 2) + a.shape[1:])
    packs = lambda d_: [halves(_pack([d_[k] for k in names])) for names in PACK_GROUPS]
    nn_ = len(NATIVE_NAMES)
    local = [weights[k] for k in NATIVE_NAMES] + packs(weights)
    local_m = [m[k] for k in NATIVE_NAMES] + packs(m)
    local_v = [v[k] for k in NATIVE_NAMES] + packs(v)
    flat = lambda lst: [a.reshape((-1, a.shape[-1])) for a in lst]
    c_idx = lax.axis_index("c").astype(jnp.int32).reshape(1)
    chip_idx = (2 * lax.axis_index("x") + lax.axis_index("y")).astype(jnp.int32).reshape(1)
    c2 = (c_idx, c_idx)
    chip2 = (chip_idx, chip_idx)
    chip_dev = (chip_idx, 2 * chip_idx + c_idx)

    def placed(arrays, slot, n_slots, dtype, from_slot, tag):
        return _grouped(lambda a, name: place_slot(a, slot, n_slots, dtype, from_slot, name=name), [arrays], 1, tag)

    ffn_own = [weights[k][i] for i in range(DEPTH) for k in NATIVE_NAMES]
    ffn_bufs = placed(ffn_own, chip2, N_CHIPS, bf16, False, "place_ffn_weights")
    group = len(NATIVE_NAMES) // 2
    n_ffn_groups = len(ffn_bufs) // group
    ffn_gathered = []
    for gi in range(n_ffn_groups):
        ffn_gathered += gather_slots_async(ffn_bufs[gi * group:(gi + 1) * group], collective_id=1 + gi,
                                           name=f"comm_gather_ffn_{gi}")
    ffn_weights = {k: [ffn_gathered[i * len(NATIVE_NAMES) + j] for i in range(DEPTH)] for j, k in enumerate(NATIVE_NAMES)}
    pack_bufs = [placed(flat([a]), chip2, N_CHIPS, dt, False, f"place_packed_weights_{gi}")[0].reshape((N_CHIPS,) + a.shape)
                 for gi, (a, dt) in enumerate(zip(local[nn_:], PACK_TRANSIT))]
    full = {}
    for names, gathered in zip(PACK_GROUPS, gather_shards(pack_bufs, name="comm_gather_weights")):
        full.update(_full_weights(gathered, names, weights))
    first_grad_id = n_ffn_groups + 1
    in_flight = {}

    def on_ffn_grads(layer, first, partials):
        tag = f"ffn_grads_l{layer}_{first}"
        recvs = placed(partials, chip_dev, N_DEVICES, bf16, True, "place_" + tag)
        got = exchange_partials_async(partials, recvs, collective_id=first_grad_id + len(in_flight), name="comm_" + tag)
        in_flight[(layer, first)] = got

    loss, grad_x, grads = _local_step(x, p, loss_target, full, ffn_weights, on_ffn_grads)
    gs = [_grad_pack(grads, names, shp).reshape((N_CHIPS,) + a.shape)
          for names, shp, a in zip(PACK_GROUPS, shapes, local[nn_:])]
    others = sibling_exchange(gs, name="comm_grad_sibling")
    chip_sums = [add_own_half([g], [o], c_idx, dt, name=f"grad_add_sibling_{gi}")[0]
                 for gi, (g, o, dt) in enumerate(zip(gs, others, PACK_TRANSIT))]
    own = [placed([cs], chip2, N_CHIPS, dt, True, f"place_own_partial_{gi}")[0]
           for gi, (cs, dt) in enumerate(zip(chip_sums, PACK_TRANSIT))]
    slots = chip_exchange(chip_sums, own, name="comm_grad_chips")
    mine = _grouped(sum_slots, [list(slots)], 1, "grad_sum_chips")
    pack_sum = sibling_share(placed(mine, c2, 2, f32, False, "place_own_half"), name="comm_grad_share")
    ffn_sums = {}
    for (layer, first), got in in_flight.items():
        sums = _grouped(sum_slots, [got], 1, f"grad_sum_ffn_l{layer}_{first}")
        for j, g in enumerate(sums):
            ffn_sums[(NATIVE_NAMES[first + j], layer)] = g
    gsum = [jnp.stack([ffn_sums[(k, i)] for i in range(DEPTH)]) for k in NATIVE_NAMES] + list(pack_sum)
    delta, m2, v2 = _grouped(adamw, [flat(local), flat(gsum), flat(local_m), flat(local_v)], 3, "adamw")
    loss = lax.psum(loss, ("x", "y", "c"))
    outs = []
    for res in (gsum, delta, m2, v2):
        by_name = {k: a.reshape(weights[k].shape) for k, a in zip(NATIVE_NAMES, res[:nn_])}
        for names, shp, pk in zip(PACK_GROUPS, shapes, res[nn_:]):
            by_name.update(zip(names, _unpack(pk, shp)))
        outs += [by_name[k] for k in WEIGHT_NAMES]
    return (loss, grad_x, *outs)


def kernel(x, p, ffn1_wg, ffn1_wu, ffn1_wd, ffn2_wg, ffn2_wu, ffn2_wd, ln_g, ln_b, ple_wg, ple_bg, ple_wp, ab_w_in, a_sinks, b_conv_w, b_conv_b, b_wa, b_ba, b_wx, b_bx, b_lam, ab_w_out, c_w_in, c_conv_w, c_a_log, c_dt_bias, c_norm_g, c_w_out, loss_target, m_ffn1_wg, m_ffn1_wu, m_ffn1_wd, m_ffn2_wg, m_ffn2_wu, m_ffn2_wd, m_ln_g, m_ln_b, m_ple_wg, m_ple_bg, m_ple_wp, m_ab_w_in, m_a_sinks, m_b_conv_w, m_b_conv_b, m_b_wa, m_b_ba, m_b_wx, m_b_bx, m_b_lam, m_ab_w_out, m_c_w_in, m_c_conv_w, m_c_a_log, m_c_dt_bias, m_c_norm_g, m_c_w_out, v_ffn1_wg, v_ffn1_wu, v_ffn1_wd, v_ffn2_wg, v_ffn2_wu, v_ffn2_wd, v_ln_g, v_ln_b, v_ple_wg, v_ple_bg, v_ple_wp, v_ab_w_in, v_a_sinks, v_b_conv_w, v_b_conv_b, v_b_wa, v_b_ba, v_b_wx, v_b_bx, v_b_lam, v_ab_w_out, v_c_w_in, v_c_conv_w, v_c_a_log, v_c_dt_bias, v_c_norm_g, v_c_w_out):
    weights = [ffn1_wg, ffn1_wu, ffn1_wd, ffn2_wg, ffn2_wu, ffn2_wd, ln_g, ln_b, ple_wg, ple_bg, ple_wp, ab_w_in, a_sinks,
               b_conv_w, b_conv_b, b_wa, b_ba, b_wx, b_bx, b_lam, ab_w_out, c_w_in, c_conv_w, c_a_log, c_dt_bias, c_norm_g,
               c_w_out]
    m = [m_ffn1_wg, m_ffn1_wu, m_ffn1_wd, m_ffn2_wg, m_ffn2_wu, m_ffn2_wd, m_ln_g, m_ln_b, m_ple_wg, m_ple_bg, m_ple_wp,
         m_ab_w_in, m_a_sinks, m_b_conv_w, m_b_conv_b, m_b_wa, m_b_ba, m_b_wx, m_b_bx, m_b_lam, m_ab_w_out, m_c_w_in,
         m_c_conv_w, m_c_a_log, m_c_dt_bias, m_c_norm_g, m_c_w_out]
    v = [v_ffn1_wg, v_ffn1_wu, v_ffn1_wd, v_ffn2_wg, v_ffn2_wu, v_ffn2_wd, v_ln_g, v_ln_b, v_ple_wg, v_ple_bg, v_ple_wp,
         v_ab_w_in, v_a_sinks, v_b_conv_w, v_b_conv_b, v_b_wa, v_b_ba, v_b_wx, v_b_bx, v_b_lam, v_ab_w_out, v_c_w_in,
         v_c_conv_w, v_c_a_log, v_c_dt_bias, v_c_norm_g, v_c_w_out]
    return _train_step(x, p, loss_target, dict(zip(WEIGHT_NAMES, weights)), dict(zip(WEIGHT_NAMES, m)),
                       dict(zip(WEIGHT_NAMES, v)))
```

```python
import functools

import jax
import jax.numpy as jnp
from jax import lax
from jax.experimental import pallas as pl
from jax.experimental.pallas import tpu as pltpu
from jax.experimental.pallas import tpu_sc as plsc

f32 = jnp.float32
bf16 = jnp.bfloat16

DEPTH = 2
CHUNK = 64
A_HEADS, A_KV_HEADS, A_GROUP, A_HEAD_DIM = 8, 2, 4, 64
A_WIDTH, A_KV_WIDTH, A_WINDOW = 512, 128, 128
B_WIDTH, B_BLOCKS, B_BLOCK, B_CONV = 512, 8, 64, 4
RG_C = 8.0
C_HEADS, C_HEAD_DIM, C_WIDTH, C_CONV = 8, 128, 1024, 4
DN_ALPHA = (2.0 * DEPTH) ** 0.25
LN_EPS = 1e-5
NORM_EPS = 1e-6
NEG = -1e30
ADAM_LR, ADAM_B1, ADAM_B2, ADAM_EPS, ADAM_WD, ADAM_STEP = 0.001, 0.9, 0.999, 1e-08, 0.01, 10

VMEM_LIMIT_BYTES = 56 * 1024 * 1024
LANES = 128
SUBLANES = 8
GROUP_W = 128
PREP_FWD_UNROLL = 8
PREP_BWD_UNROLL = 8
C_HEADS_PER_STEP = 4
GDN_TIME_BLOCK = 512

NN = ((1,), (0,))
NT = ((1,), (1,))
TN = ((0,), (0,))


def _params(sem):
    return pltpu.CompilerParams(dimension_semantics=sem, vmem_limit_bytes=VMEM_LIMIT_BYTES)


def _tile(n, cap, mult):
    best = None
    t = mult
    while t <= min(n, cap):
        if n % t == 0:
            best = t
        t += mult
    return best if best is not None else n


def _bdot(a, b, dims):
    return lax.dot_general(a.astype(bf16), b.astype(bf16), (dims, ((), ())), preferred_element_type=f32)


def _running_sum(x, reverse):
    s = x.shape[0]
    t = lax.broadcasted_iota(jnp.int32, x.shape, 0)
    d = 1
    while d < s:
        if reverse:
            x = x + jnp.where(t < s - d, pltpu.roll(x, s - d, 0), 0.0)
        else:
            x = x + jnp.where(t >= d, pltpu.roll(x, d, 0), 0.0)
        d *= 2
    return x


@jax.custom_vjp
def _cumsum0(x):
    return _running_sum(x, False)


def _cumsum0_fwd(x):
    return _running_sum(x, False), None


def _cumsum0_bwd(_, g):
    return (_running_sum(g, True),)


_cumsum0.defvjp(_cumsum0_fwd, _cumsum0_bwd)


@jax.custom_vjp
def _bnn(a, b):
    return _bdot(a, b, NN)


def _bnn_fwd(a, b):
    return _bdot(a, b, NN), (a, b)


def _bnn_bwd(res, g):
    a, b = res
    return _bdot(g, b, NT), _bdot(a, g, TN)


_bnn.defvjp(_bnn_fwd, _bnn_bwd)


@jax.custom_vjp
def _bnt(a, b):
    return _bdot(a, b, NT)


def _bnt_fwd(a, b):
    return _bdot(a, b, NT), (a, b)


def _bnt_bwd(res, g):
    a, b = res
    return _bdot(g, b, NN), _bdot(g, a, TN)


_bnt.defvjp(_bnt_fwd, _bnt_bwd)


@jax.custom_vjp
def _btn(a, b):
    return _bdot(a, b, TN)


def _btn_fwd(a, b):
    return _bdot(a, b, TN), (a, b)


def _btn_bwd(res, g):
    a, b = res
    return _bdot(b, g, NT), _bdot(a, g, NN)


_btn.defvjp(_btn_fwd, _btn_bwd)

RAW_DOTS = (lambda a, b: _bdot(a, b, NN), lambda a, b: _bdot(a, b, NT), lambda a, b: _bdot(a, b, TN),
            lambda x: _running_sum(x, False))
VJP_DOTS = (_bnn, _bnt, _btn, _cumsum0)


def _layer_norm(z, g, b):
    mu = jnp.mean(z, -1, keepdims=True)
    d = z - mu
    var = jnp.mean(d * d, -1, keepdims=True)
    return d * lax.rsqrt(var + LN_EPS) * g + b


def _silu(x):
    return x * jax.nn.sigmoid(x)


def mm_nn(a, w, add=None, add_scale=1.0, *, name):
    m, k = a.shape
    n = w.shape[1]
    tm = _tile(m, 512, SUBLANES)
    tn = _tile(n, 1024, LANES)

    def body(*refs):
        if add is None:
            a_ref, w_ref, o_ref = refs
            o_ref[...] = _bdot(a_ref[...], w_ref[...], NN)
        else:
            a_ref, w_ref, add_ref, o_ref = refs
            o_ref[...] = _bdot(a_ref[...], w_ref[...], NN) + add_scale * add_ref[...]

    in_specs = [pl.BlockSpec((tm, k), lambda i, j: (i, 0)), pl.BlockSpec((k, tn), lambda i, j: (0, j))]
    args = [a, w]
    if add is not None:
        in_specs.append(pl.BlockSpec((tm, tn), lambda i, j: (i, j)))
        args.append(add)
    return pl.pallas_call(
        body, grid=(m // tm, n // tn), in_specs=in_specs,
        out_specs=pl.BlockSpec((tm, tn), lambda i, j: (i, j)),
        out_shape=jax.ShapeDtypeStruct((m, n), f32),
        compiler_params=_params(("parallel", "parallel")), name=name,
    )(*args)


def mm_tn(a, b, *, name):
    m, k = a.shape
    n = b.shape[1]
    tm = _tile(m, 1024, 2 * SUBLANES)
    tn = _tile(n, 1024, LANES)

    def body(a_ref, b_ref, o_ref):
        part = _bdot(a_ref[...], b_ref[...], TN)

        @pl.when(pl.program_id(1) == 0)
        def _():
            o_ref[...] = part

        @pl.when(pl.program_id(1) > 0)
        def _():
            o_ref[...] += part

    return pl.pallas_call(
        body, grid=(n // tn, m // tm),
        in_specs=[pl.BlockSpec((tm, k), lambda j, i: (i, 0)), pl.BlockSpec((tm, tn), lambda j, i: (i, j))],
        out_specs=pl.BlockSpec((k, tn), lambda j, i: (0, j)),
        out_shape=jax.ShapeDtypeStruct((k, n), f32),
        compiler_params=_params(("parallel", "arbitrary")), name=name,
    )(a, b)


def proj_ln(a_list, w_list, xres, g, b, *, name):
    t, d = xres.shape
    tm = _tile(t, 256, SUBLANES)
    na = len(a_list)

    def body(*refs):
        a_refs, w_refs = refs[:na], refs[na:2 * na]
        x_ref, g_ref, b_ref, y_ref, z_ref = refs[2 * na:]
        z = DN_ALPHA * x_ref[...]
        for a_ref, w_ref in zip(a_refs, w_refs):
            z = z + _bdot(a_ref[...], w_ref[...], NN)
        z_ref[...] = z
        y_ref[...] = _layer_norm(z, g_ref[...], b_ref[...])

    in_specs = [pl.BlockSpec((tm, a.shape[1]), lambda i: (i, 0)) for a in a_list]
    in_specs += [pl.BlockSpec(w.shape, lambda i: (0, 0)) for w in w_list]
    in_specs += [pl.BlockSpec((tm, d), lambda i: (i, 0)), pl.BlockSpec((1, d), lambda i: (0, 0)),
                 pl.BlockSpec((1, d), lambda i: (0, 0))]
    return pl.pallas_call(
        body, grid=(t // tm,), in_specs=in_specs,
        out_specs=[pl.BlockSpec((tm, d), lambda i: (i, 0))] * 2,
        out_shape=[jax.ShapeDtypeStruct((t, d), f32)] * 2,
        compiler_params=_params(("parallel",)), name=name,
    )(*a_list, *w_list, xres, g, b)


def ln_bwd(z, dy, g, *, name):
    t, d = z.shape
    tm = _tile(t, 512, SUBLANES)

    def body(z_ref, dy_ref, g_ref, dz_ref, dzb_ref, dg_ref, db_ref):
        zz = z_ref[...]
        dy_ = dy_ref[...]
        mu = jnp.mean(zz, -1, keepdims=True)
        dd = zz - mu
        var = jnp.mean(dd * dd, -1, keepdims=True)
        rstd = lax.rsqrt(var + LN_EPS)
        xhat = dd * rstd
        dxh = dy_ * g_ref[...]
        dz = rstd * (dxh - jnp.mean(dxh, -1, keepdims=True) - xhat * jnp.mean(dxh * xhat, -1, keepdims=True))
        dz_ref[...] = dz
        dzb_ref[...] = dz.astype(bf16)
        pg = jnp.sum(dy_ * xhat, 0, keepdims=True)
        pb = jnp.sum(dy_, 0, keepdims=True)

        @pl.when(pl.program_id(0) == 0)
        def _():
            dg_ref[...] = pg
            db_ref[...] = pb

        @pl.when(pl.program_id(0) > 0)
        def _():
            dg_ref[...] += pg
            db_ref[...] += pb

    row = pl.BlockSpec((tm, d), lambda i: (i, 0))
    vec = pl.BlockSpec((1, d), lambda i: (0, 0))
    return pl.pallas_call(
        body, grid=(t // tm,), in_specs=[row, row, vec], out_specs=[row, row, vec, vec],
        out_shape=[jax.ShapeDtypeStruct((t, d), f32), jax.ShapeDtypeStruct((t, d), bf16),
                   jax.ShapeDtypeStruct((1, d), f32), jax.ShapeDtypeStruct((1, d), f32)],
        compiler_params=_params(("arbitrary",)), name=name,
    )(z, dy, g)


def loss_head(y, target, *, name):
    t, d = y.shape
    tm = _tile(t, 512, SUBLANES)

    def body(y_ref, t_ref, dy_ref, sq_ref):
        e = y_ref[...] - t_ref[...]
        dy_ref[...] = e * (1.0 / d)
        part = jnp.sum(e * e, 0, keepdims=True)

        @pl.when(pl.program_id(0) == 0)
        def _():
            sq_ref[...] = part

        @pl.when(pl.program_id(0) > 0)
        def _():
            sq_ref[...] += part

    row = pl.BlockSpec((tm, d), lambda i: (i, 0))
    vec = pl.BlockSpec((1, d), lambda i: (0, 0))
    return pl.pallas_call(
        body, grid=(t // tm,), in_specs=[row, row], out_specs=[row, vec],
        out_shape=[jax.ShapeDtypeStruct((t, d), f32), jax.ShapeDtypeStruct((1, d), f32)],
        compiler_params=_params(("arbitrary",)), name=name,
    )(y, target)


FFN_COL_BLOCK = 256
FFN_ROWS = 1024


def _lane_blocks(n):
    return [slice(s, min(s + FFN_COL_BLOCK, n)) for s in range(0, n, FFN_COL_BLOCK)]


def ffn_fwd(x, wg, wu, wd, g, b, *, name):
    t, d = x.shape
    nf, _, tf = wg.shape
    tm = _tile(t, FFN_ROWS, SUBLANES)

    def body(x_ref, wg_ref, wu_ref, wd_ref, g_ref, b_ref, y_ref, z_ref, yb_ref, acc_ref):
        f = pl.program_id(1)
        xb = x_ref[...].astype(bf16)
        part, pending = None, None
        for cols in _lane_blocks(tf):
            gate_up = (_bdot(xb, wg_ref[:, cols], NN), _bdot(xb, wu_ref[:, cols], NN), cols)
            if pending is not None:
                down = _bdot(_silu(pending[0]) * pending[1], wd_ref[pending[2], :], NN)
                part = down if part is None else part + down
            pending = gate_up
        down = _bdot(_silu(pending[0]) * pending[1], wd_ref[pending[2], :], NN)
        part = down if part is None else part + down

        @pl.when(f == 0)
        def _():
            acc_ref[...] = part

        @pl.when(f > 0)
        def _():
            acc_ref[...] += part

        @pl.when(f == nf - 1)
        def _():
            z = DN_ALPHA * x_ref[...] + 0.5 * acc_ref[...]
            z_ref[...] = z
            y = _layer_norm(z, g_ref[...], b_ref[...])
            y_ref[...] = y
            yb_ref[...] = y.astype(bf16)

    row = pl.BlockSpec((tm, d), lambda i, j: (i, 0))
    vec = pl.BlockSpec((1, d), lambda i, j: (0, 0))
    wcol = pl.BlockSpec((None, d, tf), lambda i, j: (j, 0, 0))
    wrow = pl.BlockSpec((None, tf, d), lambda i, j: (j, 0, 0))
    return pl.pallas_call(
        body, grid=(t // tm, nf),
        in_specs=[row, wcol, wcol, wrow, vec, vec],
        out_specs=[row, row, row],
        out_shape=[jax.ShapeDtypeStruct((t, d), f32)] * 2 + [jax.ShapeDtypeStruct((t, d), bf16)],
        scratch_shapes=[pltpu.VMEM((tm, d), f32)],
        compiler_params=_params(("parallel", "arbitrary")), name=name,
    )(x, wg, wu, wd, g, b)


def ffn_bwd_weights(xb, dzb, wg, wu, wd, *, name):
    t, d = xb.shape
    nf, _, tf = wg.shape
    tm = _tile(t, FFN_ROWS, SUBLANES)
    nt = t // tm

    def body(x_ref, dz_ref, wg_ref, wu_ref, wd_ref, dgate_ref, dup_ref, owg_ref, owu_ref, owd_ref,
             dwg_ref, dwu_ref, dwd_ref):
        x = x_ref[...]
        dzh = dz_ref[...] * 0.5

        def first_half(cols):
            return _bdot(x, wg_ref[:, cols], NN), _bdot(x, wu_ref[:, cols], NN), _bdot(dzh, wd_ref[cols, :], NT), cols

        def second_half(gate, up, dh, cols):
            sg = jax.nn.sigmoid(gate)
            s = gate * sg
            dup = (dh * s).astype(bf16)
            dgate = (dh * up * (sg * (1.0 + gate * (1.0 - sg)))).astype(bf16)
            dgate_ref[:, cols] = dgate
            dup_ref[:, cols] = dup
            return _bdot(x, dgate, TN), _bdot(x, dup, TN), _bdot(s * up, dzh, TN), cols

        parts, pending = [], None
        for cols in _lane_blocks(tf):
            nxt = first_half(cols)
            if pending is not None:
                parts.append(second_half(*pending))
            pending = nxt
        parts.append(second_half(*pending))

        @pl.when(pl.program_id(1) == 0)
        def _():
            for pwg, pwu, pwd, cols in parts:
                dwg_ref[:, cols] = pwg
                dwu_ref[:, cols] = pwu
                dwd_ref[cols, :] = pwd

        @pl.when(pl.program_id(1) > 0)
        def _():
            for pwg, pwu, pwd, cols in parts:
                dwg_ref[:, cols] += pwg
                dwu_ref[:, cols] += pwu
                dwd_ref[cols, :] += pwd

        @pl.when(pl.program_id(1) == nt - 1)
        def _():
            owg_ref[...] = dwg_ref[...].astype(bf16)
            owu_ref[...] = dwu_ref[...].astype(bf16)
            owd_ref[...] = dwd_ref[...].astype(bf16)

    row = pl.BlockSpec((tm, d), lambda j, i: (i, 0))
    wcol = pl.BlockSpec((None, d, tf), lambda j, i: (j, 0, 0))
    wrow = pl.BlockSpec((None, tf, d), lambda j, i: (j, 0, 0))
    act = pl.BlockSpec((None, tm, tf), lambda j, i: (j, i, 0))
    return pl.pallas_call(
        body, grid=(nf, nt), in_specs=[row, row, wcol, wcol, wrow], out_specs=[act, act, wcol, wcol, wrow],
        out_shape=[jax.ShapeDtypeStruct((nf, t, tf), bf16), jax.ShapeDtypeStruct((nf, t, tf), bf16),
                   jax.ShapeDtypeStruct((nf, d, tf), bf16), jax.ShapeDtypeStruct((nf, d, tf), bf16),
                   jax.ShapeDtypeStruct((nf, tf, d), bf16)],
        scratch_shapes=[pltpu.VMEM((d, tf), f32), pltpu.VMEM((d, tf), f32), pltpu.VMEM((tf, d), f32)],
        compiler_params=_params(("parallel", "arbitrary")), name=name,
    )(xb, dzb, wg, wu, wd)


def ffn_bwd_input(dgate, dup, wg, wu, dz, *, name):
    nf, t, tf = dgate.shape
    d = wg.shape[1]
    tm = _tile(t, FFN_ROWS // 2, SUBLANES)

    def body(dg_ref, du_ref, wg_ref, wu_ref, dz_ref, dx_ref):
        acc = DN_ALPHA * dz_ref[...]
        for j in range(nf):
            acc = acc + _bdot(dg_ref[j], wg_ref[j], NT) + _bdot(du_ref[j], wu_ref[j], NT)
        dx_ref[...] = acc

    act = pl.BlockSpec((nf, tm, tf), lambda i: (0, i, 0))
    wsp = pl.BlockSpec((nf, d, tf), lambda i: (0, 0, 0))
    row = pl.BlockSpec((tm, d), lambda i: (i, 0))
    return pl.pallas_call(
        body, grid=(t // tm,), in_specs=[act, act, wsp, wsp, row], out_specs=row,
        out_shape=jax.ShapeDtypeStruct((t, d), f32),
        compiler_params=_params(("parallel",)), name=name,
    )(dgate, dup, wg, wu, dz)


def ple_fwd(x, p, wg, bg, wp, *, name):
    t, d = x.shape
    dp = p.shape[1]
    tm = _tile(t, 512, SUBLANES)

    def body(x_ref, p_ref, wg_ref, bg_ref, wp_ref, o_ref):
        x_ = x_ref[...]
        gate = jax.nn.sigmoid(_bdot(x_, wg_ref[...], NN) + bg_ref[...])
        o_ref[...] = x_ + gate * _bdot(p_ref[...], wp_ref[...], NN)

    row = pl.BlockSpec((tm, d), lambda i: (i, 0))
    return pl.pallas_call(
        body, grid=(t // tm,),
        in_specs=[row, pl.BlockSpec((tm, dp), lambda i: (i, 0)), pl.BlockSpec((d, d), lambda i: (0, 0)),
                  pl.BlockSpec((1, d), lambda i: (0, 0)), pl.BlockSpec((dp, d), lambda i: (0, 0))],
        out_specs=row, out_shape=jax.ShapeDtypeStruct((t, d), f32),
        compiler_params=_params(("parallel",)), name=name,
    )(x, p, wg, bg, wp)


def ple_bwd(x, p, dy, wg, wgt, bg, wp, *, name):
    t, d = x.shape
    dp = p.shape[1]
    tm = _tile(t, 512, SUBLANES)

    def body(x_ref, p_ref, dy_ref, wg_ref, wgt_ref, bg_ref, wp_ref, dx_ref, dwg_ref, dbg_ref, dwp_ref):
        x_ = x_ref[...]
        dy_ = dy_ref[...]
        s = jax.nn.sigmoid(_bdot(x_, wg_ref[...], NN) + bg_ref[...])
        e = _bdot(p_ref[...], wp_ref[...], NN)
        da = dy_ * e * s * (1.0 - s)
        de = dy_ * s
        dx_ref[...] = dy_ + _bdot(da, wgt_ref[...], NN)
        pwg = _bdot(x_, da, TN)
        pbg = jnp.sum(da, 0, keepdims=True)
        pwp = _bdot(p_ref[...], de, TN)

        @pl.when(pl.program_id(0) == 0)
        def _():
            dwg_ref[...] = pwg
            dbg_ref[...] = pbg
            dwp_ref[...] = pwp

        @pl.when(pl.program_id(0) > 0)
        def _():
            dwg_ref[...] += pwg
            dbg_ref[...] += pbg
            dwp_ref[...] += pwp

    row = pl.BlockSpec((tm, d), lambda i: (i, 0))
    full = lambda shape: pl.BlockSpec(shape, lambda i: (0, 0))
    return pl.pallas_call(
        body, grid=(t // tm,),
        in_specs=[row, pl.BlockSpec((tm, dp), lambda i: (i, 0)), row, full((d, d)), full((d, d)), full((1, d)),
                  full((dp, d))],
        out_specs=[row, full((d, d)), full((1, d)), full((dp, d))],
        out_shape=[jax.ShapeDtypeStruct((t, d), f32), jax.ShapeDtypeStruct((d, d), f32),
                   jax.ShapeDtypeStruct((1, d), f32), jax.ShapeDtypeStruct((dp, d), f32)],
        compiler_params=_params(("arbitrary",)), name=name,
    )(x, p, dy, wg, wgt, bg, wp)


def _conv_taps(xpad_ref, w_ref, s):
    acc = w_ref[0:1, :] * xpad_ref[SUBLANES - 3:SUBLANES - 3 + s, :]
    for j in range(1, 4):
        acc = acc + w_ref[j:j + 1, :] * xpad_ref[SUBLANES - 3 + j:SUBLANES - 3 + j + s, :]
    return acc


def conv_fwd(x, w, bias, act, nb, *, name):
    t, c = x.shape
    s = t // nb
    cw = GROUP_W

    def body(x_ref, w_ref, b_ref, y_ref, xpad):
        xpad[0:SUBLANES, :] = jnp.zeros((SUBLANES, cw), f32)
        xpad[SUBLANES:, :] = x_ref[...]
        acc = _conv_taps(xpad, w_ref, s) + b_ref[...]
        y_ref[...] = _silu(acc) if act else acc

    slab = pl.BlockSpec((s, cw), lambda b, g: (b, g))
    return pl.pallas_call(
        body, grid=(nb, c // cw),
        in_specs=[slab, pl.BlockSpec((4, cw), lambda b, g: (0, g)), pl.BlockSpec((1, cw), lambda b, g: (0, g))],
        out_specs=slab, out_shape=jax.ShapeDtypeStruct((t, c), f32),
        scratch_shapes=[pltpu.VMEM((s + SUBLANES, cw), f32)],
        compiler_params=_params(("parallel", "parallel")), name=name,
    )(x, w, bias)


def conv_bwd(x, w, bias, dy, act, nb, *, name):
    t, c = x.shape
    s = t // nb
    cw = GROUP_W

    def body(x_ref, w_ref, b_ref, dy_ref, dx_ref, dw_ref, db_ref, xpad, dpad):
        xpad[0:SUBLANES, :] = jnp.zeros((SUBLANES, cw), f32)
        xpad[SUBLANES:, :] = x_ref[...]
        dacc = dy_ref[...]
        if act:
            acc = _conv_taps(xpad, w_ref, s) + b_ref[...]
            sg = jax.nn.sigmoid(acc)
            dacc = dacc * (sg * (1.0 + acc * (1.0 - sg)))
        dpad[0:s, :] = dacc
        dpad[s:, :] = jnp.zeros((SUBLANES, cw), f32)
        dx = w_ref[0:1, :] * dpad[3:3 + s, :]
        for j in range(1, 4):
            dx = dx + w_ref[j:j + 1, :] * dpad[3 - j:3 - j + s, :]
        dx_ref[...] = dx
        first = pl.program_id(1) == 0
        for j in range(4):
            pw = jnp.sum(dacc * xpad[SUBLANES - 3 + j:SUBLANES - 3 + j + s, :], 0, keepdims=True)

            @pl.when(first)
            def _():
                dw_ref[j:j + 1, :] = pw

            @pl.when(jnp.logical_not(first))
            def _():
                dw_ref[j:j + 1, :] += pw

        pb = jnp.sum(dacc, 0, keepdims=True)

        @pl.when(first)
        def _():
            db_ref[...] = pb

        @pl.when(jnp.logical_not(first))
        def _():
            db_ref[...] += pb

    slab = pl.BlockSpec((s, cw), lambda g, b: (b, g))
    wsp = pl.BlockSpec((4, cw), lambda g, b: (0, g))
    bsp = pl.BlockSpec((1, cw), lambda g, b: (0, g))
    return pl.pallas_call(
        body, grid=(c // cw, nb), in_specs=[slab, wsp, bsp, slab], out_specs=[slab, wsp, bsp],
        out_shape=[jax.ShapeDtypeStruct((t, c), f32), jax.ShapeDtypeStruct((4, c), f32),
                   jax.ShapeDtypeStruct((1, c), f32)],
        scratch_shapes=[pltpu.VMEM((s + SUBLANES, cw), f32), pltpu.VMEM((s + SUBLANES, cw), f32)],
        compiler_params=_params(("parallel", "arbitrary")), name=name,
    )(x, w, bias, dy)


def _each(f, *lists):
    return [f(*a) for a in zip(*lists)]


def _attn_heads(qs, kbs, vbs, sinks, valid, dist, dots):
    nn, nt = dots[:2]
    kv = [h // A_GROUP for h in range(A_HEADS)]
    scs = [nt(qs[h], kbs[kv[h]]) for h in range(A_HEADS)]
    prs = []
    for h in range(A_HEADS):
        sc = scs[h] * (A_HEAD_DIM ** -0.5) - 2.0 ** -(h + 1) * dist
        sc = jnp.where(valid, sc, NEG)
        m = lax.stop_gradient(jnp.maximum(jnp.max(sc, -1, keepdims=True), sinks[h]))
        pr = jnp.exp(sc - m)
        den = jnp.sum(pr, -1, keepdims=True) + jnp.exp(sinks[h] - m)
        prs.append(pr / den)
    return [nn(prs[h], vbs[kv[h]]) for h in range(A_HEADS)]


A_Q_ROWS = 2 * CHUNK


def _attn_band_consts(r0):
    band = A_WINDOW + A_Q_ROWS
    qi = lax.broadcasted_iota(jnp.int32, (A_Q_ROWS, band), 0)
    kj = lax.broadcasted_iota(jnp.int32, (A_Q_ROWS, band), 1)
    dist = jnp.abs(qi + A_WINDOW - kj).astype(f32)
    qc, kc = qi // CHUNK, kj // CHUNK
    valid = ((kj + r0) >= A_WINDOW) & (kc >= qc) & (kc <= qc + A_WINDOW // CHUNK)
    return dist, valid


def attn_fwd(qkv, sinks, nb, *, name):
    t = qkv.shape[0]
    s = t // nb
    band = A_WINDOW + A_Q_ROWS
    hd = A_HEAD_DIM

    def body(qkv_ref, sink_ref, o_ref, kvpad):
        kvpad[0:A_WINDOW, :] = jnp.zeros((A_WINDOW, 2 * A_KV_WIDTH), f32)
        kvpad[A_WINDOW:, :] = qkv_ref[:, A_WIDTH:]

        def chunk(n, carry):
            r0 = pl.multiple_of(n * A_Q_ROWS, A_Q_ROWS)
            dist, valid = _attn_band_consts(r0)
            kbs = [kvpad[pl.ds(r0, band), kvh * hd:(kvh + 1) * hd] for kvh in range(A_KV_HEADS)]
            vbs = [kvpad[pl.ds(r0, band), A_KV_WIDTH + kvh * hd:A_KV_WIDTH + (kvh + 1) * hd]
                   for kvh in range(A_KV_HEADS)]
            qs = [qkv_ref[pl.ds(r0, A_Q_ROWS), h * hd:(h + 1) * hd] for h in range(A_HEADS)]
            outs = _attn_heads(qs, kbs, vbs, [sink_ref[:, h:h + 1] for h in range(A_HEADS)], valid, dist, RAW_DOTS)
            for h in range(A_HEADS):
                o_ref[pl.ds(r0, A_Q_ROWS), h * hd:(h + 1) * hd] = outs[h]
            return carry

        lax.fori_loop(0, s // A_Q_ROWS, chunk, 0)

    return pl.pallas_call(
        body, grid=(nb,),
        in_specs=[pl.BlockSpec((s, A_WIDTH + 2 * A_KV_WIDTH), lambda b: (b, 0)),
                  pl.BlockSpec((1, A_HEADS), lambda b: (0, 0))],
        out_specs=pl.BlockSpec((s, A_WIDTH), lambda b: (b, 0)),
        out_shape=jax.ShapeDtypeStruct((t, A_WIDTH), f32),
        scratch_shapes=[pltpu.VMEM((s + A_WINDOW, 2 * A_KV_WIDTH), f32)],
        compiler_params=_params(("parallel",)), name=name,
    )(qkv, sinks)


def attn_bwd(qkv, sinks, do, nb, *, name):
    t = qkv.shape[0]
    s = t // nb
    band = A_WINDOW + A_Q_ROWS
    hd = A_HEAD_DIM
    kvw = 2 * A_KV_WIDTH

    def body(qkv_ref, sink_ref, do_ref, dqkv_ref, dsink_ref, kvpad, dkvpad):
        kvpad[0:A_WINDOW, :] = jnp.zeros((A_WINDOW, kvw), f32)
        kvpad[A_WINDOW:, :] = qkv_ref[:, A_WIDTH:]
        dkvpad[...] = jnp.zeros((s + A_WINDOW, kvw), f32)

        def chunk(n, dsinks):
            r0 = pl.multiple_of(n * A_Q_ROWS, A_Q_ROWS)
            dist, valid = _attn_band_consts(r0)
            ksl = [slice(kvh * hd, (kvh + 1) * hd) for kvh in range(A_KV_HEADS)]
            vsl = [slice(A_KV_WIDTH + kvh * hd, A_KV_WIDTH + (kvh + 1) * hd) for kvh in range(A_KV_HEADS)]
            kbs = [kvpad[pl.ds(r0, band), sl] for sl in ksl]
            vbs = [kvpad[pl.ds(r0, band), sl] for sl in vsl]
            dkbs = [dkvpad[pl.ds(r0, band), sl] for sl in ksl]
            dvbs = [dkvpad[pl.ds(r0, band), sl] for sl in vsl]
            qs = [qkv_ref[pl.ds(r0, A_Q_ROWS), h * hd:(h + 1) * hd] for h in range(A_HEADS)]
            dos = [do_ref[pl.ds(r0, A_Q_ROWS), h * hd:(h + 1) * hd] for h in range(A_HEADS)]
            fn = functools.partial(_attn_heads, valid=valid, dist=dist, dots=VJP_DOTS)
            _, vjp = jax.vjp(fn, qs, kbs, vbs, [sink_ref[:, h:h + 1] for h in range(A_HEADS)])
            dqs, dks, dvs, dss = vjp(dos)
            for h in range(A_HEADS):
                dqkv_ref[pl.ds(r0, A_Q_ROWS), h * hd:(h + 1) * hd] = dqs[h]
            for kvh in range(A_KV_HEADS):
                dkvpad[pl.ds(r0, band), ksl[kvh]] = dkbs[kvh] + dks[kvh]
                dkvpad[pl.ds(r0, band), vsl[kvh]] = dvbs[kvh] + dvs[kvh]
            return tuple(dsinks[h] + dss[h] for h in range(A_HEADS))

        dsinks = lax.fori_loop(0, s // A_Q_ROWS, chunk, tuple(jnp.zeros((1, 1), f32) for _ in range(A_HEADS)))
        dqkv_ref[:, A_WIDTH:] = dkvpad[A_WINDOW:, :]
        first = pl.program_id(0) == 0
        for h in range(A_HEADS):
            @pl.when(first)
            def _():
                dsink_ref[:, h:h + 1] = dsinks[h]

            @pl.when(jnp.logical_not(first))
            def _():
                dsink_ref[:, h:h + 1] += dsinks[h]

    wq = A_WIDTH + kvw
    return pl.pallas_call(
        body, grid=(nb,),
        in_specs=[pl.BlockSpec((s, wq), lambda b: (b, 0)), pl.BlockSpec((1, A_HEADS), lambda b: (0, 0)),
                  pl.BlockSpec((s, A_WIDTH), lambda b: (b, 0))],
        out_specs=[pl.BlockSpec((s, wq), lambda b: (b, 0)), pl.BlockSpec((1, A_HEADS), lambda b: (0, 0))],
        out_shape=[jax.ShapeDtypeStruct((t, wq), f32), jax.ShapeDtypeStruct((1, A_HEADS), f32)],
        scratch_shapes=[pltpu.VMEM((s + A_WINDOW, kvw), f32), pltpu.VMEM((s + A_WINDOW, kvw), f32)],
        compiler_params=_params(("arbitrary",)), name=name,
    )(qkv, sinks, do)


def _rg_gates(xc, wa, wx, ba, bx, lam, nn):
    r = jax.nn.sigmoid(nn(xc, wa) + ba)
    i = jax.nn.sigmoid(nn(xc, wx) + bx)
    log_a = -RG_C * r * jax.nn.softplus(-lam)
    a = jnp.exp(log_a)
    mult = jnp.sqrt(-jnp.tanh(log_a) * (jnp.exp(2.0 * log_a) + 1.0))
    return a, mult * (i * xc)


def _linear_scan(a, u, reverse):
    s = a.shape[0]
    t = lax.broadcasted_iota(jnp.int32, a.shape, 0)
    d = 1
    while d < s:
        if reverse:
            keep = t < s - d
            shift = s - d
        else:
            keep = t >= d
            shift = d
        us = jnp.where(keep, pltpu.roll(u, shift, 0), 0.0)
        as_ = jnp.where(keep, pltpu.roll(a, shift, 0), 1.0)
        u = u + a * us
        a = a * as_
        d *= 2
    return u


def rglru_fwd(xc, bg, wa, wx, ba, bx, lam, nb, *, name):
    t, c = xc.shape
    s = t // nb
    cw = GROUP_W

    def body(xc_ref, bg_ref, wa_ref, wx_ref, ba_ref, bx_ref, lam_ref, y_ref, h_ref):
        a, u = _rg_gates(xc_ref[...], wa_ref[...], wx_ref[...], ba_ref[...], bx_ref[...], lam_ref[...], RAW_DOTS[0])
        h = _linear_scan(a, u, False)
        h_ref[...] = h
        y_ref[...] = h * jax.nn.gelu(bg_ref[...])

    slab = pl.BlockSpec((s, cw), lambda b, g: (b, g))
    wsp = pl.BlockSpec((None, cw, cw), lambda b, g: (g, 0, 0))
    vec = pl.BlockSpec((1, cw), lambda b, g: (0, g))
    return pl.pallas_call(
        body, grid=(nb, c // cw), in_specs=[slab, slab, wsp, wsp, vec, vec, vec], out_specs=[slab, slab],
        out_shape=[jax.ShapeDtypeStruct((t, c), f32)] * 2,
        compiler_params=_params(("parallel", "parallel")), name=name,
    )(xc, bg, wa, wx, ba, bx, lam)


def rglru_bwd(xc, bg, h, dy, wa, wx, ba, bx, lam, nb, *, name):
    t, c = xc.shape
    s = t // nb
    cw = GROUP_W

    def body(xc_ref, bg_ref, h_ref, dy_ref, wa_ref, wx_ref, ba_ref, bx_ref, lam_ref,
             dxc_ref, dbg_ref, dwa_ref, dwx_ref, dba_ref, dbx_ref, dlam_ref):
        h = h_ref[...]
        dy_ = dy_ref[...]
        gel, gel_vjp = jax.vjp(jax.nn.gelu, bg_ref[...])
        dbg_ref[...] = gel_vjp(dy_ * h)[0]
        dh = dy_ * gel
        gates = functools.partial(_rg_gates, nn=_bnn)
        (a, _), gates_vjp = jax.vjp(gates, xc_ref[...], wa_ref[...], wx_ref[...], ba_ref[...], bx_ref[...],
                                    lam_ref[...])
        ti = lax.broadcasted_iota(jnp.int32, a.shape, 0)
        a_next = jnp.where(ti < s - 1, pltpu.roll(a, s - 1, 0), 0.0)
        lam_t = _linear_scan(a_next, dh, True)
        h_prev = jnp.where(ti >= 1, pltpu.roll(h, 1, 0), 0.0)
        dxc, dwa, dwx, dba, dbx, dlam = gates_vjp((lam_t * h_prev, lam_t))
        dxc_ref[...] = dxc
        first = pl.program_id(1) == 0

        @pl.when(first)
        def _():
            dwa_ref[...] = dwa
            dwx_ref[...] = dwx
            dba_ref[...] = dba
            dbx_ref[...] = dbx
            dlam_ref[...] = dlam

        @pl.when(jnp.logical_not(first))
        def _():
            dwa_ref[...] += dwa
            dwx_ref[...] += dwx
            dba_ref[...] += dba
            dbx_ref[...] += dbx
            dlam_ref[...] += dlam

    slab = pl.BlockSpec((s, cw), lambda g, b: (b, g))
    wsp = pl.BlockSpec((None, cw, cw), lambda g, b: (g, 0, 0))
    vec = pl.BlockSpec((1, cw), lambda g, b: (0, g))
    ng = c // cw
    return pl.pallas_call(
        body, grid=(ng, nb), in_specs=[slab, slab, slab, slab, wsp, wsp, vec, vec, vec],
        out_specs=[slab, slab, wsp, wsp, vec, vec, vec],
        out_shape=[jax.ShapeDtypeStruct((t, c), f32), jax.ShapeDtypeStruct((t, c), f32),
                   jax.ShapeDtypeStruct((ng, cw, cw), f32), jax.ShapeDtypeStruct((ng, cw, cw), f32),
                   jax.ShapeDtypeStruct((1, c), f32), jax.ShapeDtypeStruct((1, c), f32),
                   jax.ShapeDtypeStruct((1, c), f32)],
        compiler_params=_params(("parallel", "arbitrary")), name=name,
    )(xc, bg, h, dy, wa, wx, ba, bx, lam)


def _gdn_chunks_prep(qs, ks, vs, bls, als, a_log, dt_b, dots):
    nn, nt, csum = dots[0], dots[1], dots[3]
    hd = C_HEAD_DIM
    ri = lax.broadcasted_iota(jnp.int32, (CHUNK, CHUNK), 0)
    ci = lax.broadcasted_iota(jnp.int32, (CHUNK, CHUNK), 1)
    tril = ri >= ci
    strict = ri > ci
    eye = (ri == ci).astype(f32)
    qn = [q * lax.rsqrt(jnp.sum(q * q, -1, keepdims=True) + NORM_EPS) * (hd ** -0.5) for q in qs]
    kn = [k * lax.rsqrt(jnp.sum(k * k, -1, keepdims=True) + NORM_EPS) for k in ks]
    beta = [jax.nn.sigmoid(bl) for bl in bls]
    g = [-jnp.exp(a_log) * jax.nn.softplus(al + dt_b) for al in als]
    gc_sq = [csum(jnp.broadcast_to(g_, (CHUNK, CHUNK))) for g_ in g]
    gc = [csum(jnp.broadcast_to(g_, (CHUNK, hd))) for g_ in g]
    decay = [jnp.where(tril, jnp.exp(jnp.where(tril, s - s.T, 0.0)), 0.0) for s in gc_sq]
    kb = _each(jnp.multiply, kn, beta)
    kk = _each(nt, kb, kn)
    pw = [-jnp.where(strict, a * d, 0.0) for a, d in zip(kk, decay)]
    inv = [eye + p_ for p_ in pw]
    for _ in range(5):
        pw = _each(nn, pw, pw)
        inv = _each(jnp.add, inv, _each(nn, inv, pw))
    egc = [jnp.exp(c_) for c_ in gc]
    u = _each(nn, inv, _each(jnp.multiply, vs, beta))
    w = _each(nn, inv, _each(jnp.multiply, kb, egc))
    attn = _each(jnp.multiply, _each(nt, qn, kn), decay)
    g_last = [jnp.sum(jnp.broadcast_to(g_, (CHUNK, hd)), 0, keepdims=True) for g_ in g]
    qg = _each(jnp.multiply, qn, egc)
    kdec = [k_ * jnp.exp(gl_ - c_) for k_, gl_, c_ in zip(kn, g_last, gc)]
    return [(qg[i], kdec[i], w[i], u[i], attn[i], jnp.exp(g_last[i])) for i in range(len(qs))]


def _gdn_heads_step(states, qgs, kdecs, ws, us, attns, gls, zs, ng, dots):
    nn, tn = dots[0], dots[2]
    v_new = _each(jnp.subtract, us, _each(nn, ws, states))
    o = _each(jnp.add, _each(nn, qgs, states), _each(nn, attns, v_new))
    new = [s * gl for s, gl in zip(states, gls)]
    new = _each(jnp.add, new, _each(tn, kdecs, v_new))
    y = [o_ * lax.rsqrt(jnp.mean(o_ * o_, -1, keepdims=True) + NORM_EPS) * ng * _silu(z) for o_, z in zip(o, zs)]
    return y, new


def _loop_unrolled(n, unroll, load, compute, store, init):
    u = unroll if n % unroll == 0 else 1

    def trip(i, carry):
        idx = [i * u + j for j in range(u)]
        loaded = [load(k) for k in idx]
        results = compute(loaded)
        for k, r in zip(idx, results):
            carry = store(k, r, carry)
        return carry

    return lax.fori_loop(0, n // u, trip, init)


def _pick_lane(x, lane):
    li = lax.broadcasted_iota(jnp.int32, x.shape, 1)
    return jnp.sum(jnp.where(li == lane, x, 0.0), 1, keepdims=True)


def _put_lane(col, lane, width):
    li = lax.broadcasted_iota(jnp.int32, (col.shape[0], width), 1)
    return jnp.where(li == lane, col, 0.0)


def _gdn_specs(s, nc):
    hd = C_HEAD_DIM
    head = lambda off: pl.BlockSpec((s, hd), lambda b, h, off=off: (b, off + h))
    attn = pl.BlockSpec((None, s, CHUNK), lambda b, h: (h, b, 0))
    gl = pl.BlockSpec((None, nc * SUBLANES, hd), lambda b, h: (h, b, 0))
    ba = pl.BlockSpec((s, LANES), lambda b, h: (b, 0))
    sc8 = pl.BlockSpec((1, C_HEADS), lambda b, h: (0, 0))
    return head, attn, gl, ba, sc8


def gdn_prep_fwd(qkv, ba, a_log, dt_b, nb, *, name):
    t = qkv.shape[0]
    s = t // nb
    nc = s // CHUNK
    hd = C_HEAD_DIM
    head, attn_sp, gl_sp, ba_sp, sc8 = _gdn_specs(s, nc)

    def body(q_ref, k_ref, v_ref, ba_ref, alog_ref, dtb_ref, qg_ref, kd_ref, w_ref, u_ref, at_ref, gl_ref):
        h = pl.program_id(1)
        a_log_h = _pick_lane(alog_ref[...], h)
        dt_b_h = _pick_lane(dtb_ref[...], h)

        def load(n):
            rows = pl.ds(pl.multiple_of(n * CHUNK, CHUNK), CHUNK)
            bav = ba_ref[rows, :]
            return q_ref[rows, :], k_ref[rows, :], v_ref[rows, :], _pick_lane(bav, h), _pick_lane(bav, C_HEADS + h)

        def compute(loaded):
            return _gdn_chunks_prep(*[list(x) for x in zip(*loaded)], a_log_h, dt_b_h, RAW_DOTS)

        def store(n, outs, carry):
            rows = pl.ds(pl.multiple_of(n * CHUNK, CHUNK), CHUNK)
            qg_ref[rows, :] = outs[0].astype(bf16)
            kd_ref[rows, :] = outs[1].astype(bf16)
            w_ref[rows, :] = outs[2].astype(bf16)
            u_ref[rows, :] = outs[3]
            at_ref[rows, :] = outs[4].astype(bf16)
            gl_ref[pl.ds(pl.multiple_of(n * SUBLANES, SUBLANES), SUBLANES), :] = jnp.broadcast_to(outs[5], (SUBLANES, hd))
            return carry

        _loop_unrolled(nc, PREP_FWD_UNROLL, load, compute, store, 0)

    big = jax.ShapeDtypeStruct((t, C_WIDTH), f32)
    bigb = jax.ShapeDtypeStruct((t, C_WIDTH), bf16)
    return pl.pallas_call(
        body, grid=(nb, C_HEADS),
        in_specs=[head(0), head(C_HEADS), head(2 * C_HEADS), ba_sp, sc8, sc8],
        out_specs=[head(0)] * 4 + [attn_sp, gl_sp],
        out_shape=[bigb, bigb, bigb, big, jax.ShapeDtypeStruct((C_HEADS, t, CHUNK), bf16),
                               jax.ShapeDtypeStruct((C_HEADS, nb * nc * SUBLANES, hd), f32)],
        compiler_params=_params(("parallel", "parallel")), name=name,
    )(qkv, qkv, qkv, ba, a_log, dt_b)


def gdn_prep_bwd(qkv, ba, a_log, dt_b, cts, nb, *, name):
    t = qkv.shape[0]
    s = t // nb
    nc = s // CHUNK
    hd = C_HEAD_DIM
    head, attn_sp, gl_sp, ba_sp, sc8 = _gdn_specs(s, nc)

    def body(q_ref, k_ref, v_ref, ba_ref, alog_ref, dtb_ref, cqg, ckd, cw_, cu, cat, cgl,
             dq_ref, dk_ref, dv_ref, dba_ref, dalog_ref, ddtb_ref):
        b = pl.program_id(0)
        h = pl.program_id(1)
        a_log_h = _pick_lane(alog_ref[...], h)
        dt_b_h = _pick_lane(dtb_ref[...], h)
        prep = functools.partial(_gdn_chunks_prep, dots=VJP_DOTS)

        @pl.when(h == 0)
        def _():
            dba_ref[...] = jnp.zeros((s, LANES), f32)

        def load(n):
            rows = pl.ds(pl.multiple_of(n * CHUNK, CHUNK), CHUNK)
            bav = ba_ref[rows, :]
            cgl_n = cgl[pl.ds(pl.multiple_of(n * SUBLANES, SUBLANES), SUBLANES), :][0:1, :]
            primals = (q_ref[rows, :], k_ref[rows, :], v_ref[rows, :], _pick_lane(bav, h), _pick_lane(bav, C_HEADS + h))
            return primals, (cqg[rows, :], ckd[rows, :], cw_[rows, :], cu[rows, :], cat[rows, :], cgl_n), dba_ref[rows, :]

        def compute(loaded):
            primals = [list(x) for x in zip(*[item[0] for item in loaded])]
            _, vjp = jax.vjp(prep, *primals, a_log_h, dt_b_h)
            dqs, dks, dvs, dbls, dals, dalog, ddtb = vjp([item[1] for item in loaded])
            zero = jnp.zeros((1, 1), f32)
            return [((dqs[i], dks[i], dvs[i], dbls[i], dals[i], dalog if i == 0 else zero, ddtb if i == 0 else zero),
                     loaded[i][2]) for i in range(len(loaded))]

        def store(n, res, carry):
            (dq, dk, dv, dbl, dal, dalog_n, ddtb_n), dba_old = res
            rows = pl.ds(pl.multiple_of(n * CHUNK, CHUNK), CHUNK)
            dq_ref[rows, :] = dq
            dk_ref[rows, :] = dk
            dv_ref[rows, :] = dv
            dba_ref[rows, :] = dba_old + _put_lane(dbl, h, LANES) + _put_lane(dal, C_HEADS + h, LANES)
            return carry[0] + dalog_n, carry[1] + ddtb_n

        da_log, ddt_b = _loop_unrolled(nc, PREP_BWD_UNROLL, load, compute, store,
                                       (jnp.zeros((1, 1), f32), jnp.zeros((1, 1), f32)))
        first = jnp.logical_and(b == 0, h == 0)

        @pl.when(first)
        def _():
            dalog_ref[...] = _put_lane(da_log, h, LANES)
            ddtb_ref[...] = _put_lane(ddt_b, h, LANES)

        @pl.when(jnp.logical_not(first))
        def _():
            dalog_ref[...] += _put_lane(da_log, h, LANES)
            ddtb_ref[...] += _put_lane(ddt_b, h, LANES)

    big = jax.ShapeDtypeStruct((t, C_WIDTH), f32)
    vec = pl.BlockSpec((1, LANES), lambda b, h: (0, 0))
    return pl.pallas_call(
        body, grid=(nb, C_HEADS),
        in_specs=[head(0), head(C_HEADS), head(2 * C_HEADS), ba_sp, sc8, sc8] + [head(0)] * 4 + [attn_sp, gl_sp],
        out_specs=[head(0)] * 3 + [ba_sp, vec, vec],
        out_shape=[big] * 3 + [jax.ShapeDtypeStruct((t, LANES), f32), jax.ShapeDtypeStruct((1, LANES), f32),
                               jax.ShapeDtypeStruct((1, LANES), f32)],
        compiler_params=_params(("arbitrary", "arbitrary")), name=name,
    )(qkv, qkv, qkv, ba, a_log, dt_b, *cts)


def _gdn_rec_specs(sb, nsb, hp, reverse):
    hd = C_HEAD_DIM
    ncb = sb // CHUNK
    blk = (lambda b, k: b * nsb + (nsb - 1 - k)) if reverse else (lambda b, k: b * nsb + k)
    wide = pl.BlockSpec((sb, hp * hd), lambda b, j, k: (blk(b, k), j))
    attn = pl.BlockSpec((hp, sb, CHUNK), lambda b, j, k: (j, blk(b, k), 0))
    gl = pl.BlockSpec((hp, ncb * SUBLANES, hd), lambda b, j, k: (j, blk(b, k), 0))
    ng = pl.BlockSpec((1, hd), lambda b, j, k: (0, 0))
    states = pl.BlockSpec((hp, ncb, hd, hd), lambda b, j, k: (j, blk(b, k), 0, 0))
    return wide, attn, gl, ng, states


def gdn_rec_fwd(qg, kdec, w, u, attn, gl, z, ng, nb, *, name):
    t = qg.shape[0]
    s = t // nb
    sb = min(s, GDN_TIME_BLOCK)
    nsb = s // sb
    hd = C_HEAD_DIM
    hp = C_HEADS_PER_STEP
    wide, attn_sp, gl_sp, ng_sp, st_sp = _gdn_rec_specs(sb, nsb, hp, False)

    def body(qg_ref, kd_ref, w_ref, u_ref, at_ref, gl_ref, z_ref, ng_ref, y_ref, st_ref, carry_ref):
        @pl.when(pl.program_id(2) == 0)
        def _():
            carry_ref[...] = jnp.zeros((hp, hd, hd), f32)

        def chunk(n, states):
            for j in range(hp):
                st_ref[j, n] = states[j]
            rows = pl.ds(pl.multiple_of(n * CHUNK, CHUNK), CHUNK)
            grow = pl.ds(pl.multiple_of(n * SUBLANES, SUBLANES), SUBLANES)
            cols = [slice(j * hd, (j + 1) * hd) for j in range(hp)]
            ins = [(qg_ref[rows, c], kd_ref[rows, c], w_ref[rows, c], u_ref[rows, c], at_ref[j, rows, :],
                    gl_ref[j, grow, :][0:1, :], z_ref[rows, c]) for j, c in enumerate(cols)]
            ys, new = _gdn_heads_step(list(states), *[list(x) for x in zip(*ins)], ng_ref[...], RAW_DOTS)
            for j in range(hp):
                y_ref[rows, cols[j]] = ys[j]
            return tuple(new)

        last = lax.fori_loop(0, sb // CHUNK, chunk, tuple(carry_ref[j] for j in range(hp)))
        for j in range(hp):
            carry_ref[j] = last[j]

    return pl.pallas_call(
        body, grid=(nb, C_HEADS // hp, nsb),
        in_specs=[wide] * 4 + [attn_sp, gl_sp, wide, ng_sp], out_specs=[wide, st_sp],
        out_shape=[jax.ShapeDtypeStruct((t, C_WIDTH), f32), jax.ShapeDtypeStruct((C_HEADS, t // CHUNK, hd, hd), f32)],
        scratch_shapes=[pltpu.VMEM((hp, hd, hd), f32)],
        compiler_params=_params(("parallel", "parallel", "arbitrary")), name=name,
    )(qg, kdec, w, u, attn, gl, z, ng)


def gdn_rec_bwd(qg, kdec, w, u, attn, gl, z, ng, states, dy, nb, *, name):
    t = qg.shape[0]
    s = t // nb
    sb = min(s, GDN_TIME_BLOCK)
    nsb = s // sb
    nc = sb // CHUNK
    hd = C_HEAD_DIM
    hp = C_HEADS_PER_STEP
    wide, attn_sp, gl_sp, ng_sp, st_sp = _gdn_rec_specs(sb, nsb, hp, True)

    def body(qg_ref, kd_ref, w_ref, u_ref, at_ref, gl_ref, z_ref, ng_ref, states, dy_ref,
             dqg_ref, dkd_ref, dw_ref, du_ref, dat_ref, dgl_ref, dz_ref, dng_ref, carry_ref):
        step = functools.partial(_gdn_heads_step, dots=VJP_DOTS)

        @pl.when(pl.program_id(2) == 0)
        def _():
            carry_ref[...] = jnp.zeros((hp, hd, hd), f32)

        def operands(n):
            rows = pl.ds(pl.multiple_of(n * CHUNK, CHUNK), CHUNK)
            grow = pl.ds(pl.multiple_of(n * SUBLANES, SUBLANES), SUBLANES)
            cols = [slice(j * hd, (j + 1) * hd) for j in range(hp)]
            return ([qg_ref[rows, c].astype(f32) for c in cols], [kd_ref[rows, c].astype(f32) for c in cols],
                    [w_ref[rows, c].astype(f32) for c in cols], [u_ref[rows, c] for c in cols],
                    [at_ref[j, rows, :].astype(f32) for j in range(hp)],
                    [gl_ref[j, grow, :][0:1, :] for j in range(hp)], [z_ref[rows, c] for c in cols])

        def bwd_chunk(i, carry):
            n = nc - 1 - i
            rows = pl.ds(pl.multiple_of(n * CHUNK, CHUNK), CHUNK)
            grow = pl.ds(pl.multiple_of(n * SUBLANES, SUBLANES), SUBLANES)
            dsts, dng = carry
            dys = [dy_ref[rows, j * hd:(j + 1) * hd] for j in range(hp)]
            _, vjp = jax.vjp(step, [states[j, n] for j in range(hp)], *operands(n), ng_ref[...])
            dst, dqg, dkd, dw, du, dat, dgl, dz, dng_n = vjp((dys, list(dsts)))
            for j in range(hp):
                cols = slice(j * hd, (j + 1) * hd)
                dqg_ref[rows, cols] = dqg[j]
                dkd_ref[rows, cols] = dkd[j]
                dw_ref[rows, cols] = dw[j]
                du_ref[rows, cols] = du[j]
                dat_ref[j, rows, :] = dat[j]
                dgl_ref[j, grow, :] = jnp.broadcast_to(dgl[j], (SUBLANES, hd))
                dz_ref[rows, cols] = dz[j]
            return tuple(dst), dng + dng_n

        dlast, dng = lax.fori_loop(0, nc, bwd_chunk,
                                   (tuple(carry_ref[j] for j in range(hp)), jnp.zeros((1, hd), f32)))
        for j in range(hp):
            carry_ref[j] = dlast[j]
        first = jnp.logical_and(jnp.logical_and(pl.program_id(0) == 0, pl.program_id(1) == 0), pl.program_id(2) == 0)

        @pl.when(first)
        def _():
            dng_ref[...] = dng

        @pl.when(jnp.logical_not(first))
        def _():
            dng_ref[...] += dng

    big = jax.ShapeDtypeStruct((t, C_WIDTH), f32)
    return pl.pallas_call(
        body, grid=(nb, C_HEADS // hp, nsb),
        in_specs=[wide] * 4 + [attn_sp, gl_sp, wide, ng_sp, st_sp, wide],
        out_specs=[wide] * 4 + [attn_sp, gl_sp, wide, ng_sp],
        out_shape=[big] * 4 + [jax.ShapeDtypeStruct(attn.shape, f32), jax.ShapeDtypeStruct(gl.shape, f32), big,
                               jax.ShapeDtypeStruct((1, hd), f32)],
        scratch_shapes=[pltpu.VMEM((hp, hd, hd), f32)],
        compiler_params=_params(("arbitrary", "arbitrary", "arbitrary")), name=name,
    )(qg, kdec, w, u, attn, gl, z, ng, states, dy)


def _blockdiag_slabs(w):
    per = GROUP_W // B_BLOCK
    slabs = jnp.zeros((B_BLOCKS // per, GROUP_W, GROUP_W), w.dtype)
    for h in range(B_BLOCKS):
        o = (h % per) * B_BLOCK
        slabs = slabs.at[h // per, o:o + B_BLOCK, o:o + B_BLOCK].set(w[h])
    return slabs


def _slab_blocks(slabs):
    per = GROUP_W // B_BLOCK
    return jnp.stack([slabs[h // per, (h % per) * B_BLOCK:(h % per + 1) * B_BLOCK,
                            (h % per) * B_BLOCK:(h % per + 1) * B_BLOCK] for h in range(B_BLOCKS)])


def _mixer_ab_fwd(x1, x1b, W, g, b, nb, tag):
    w_in = W["ab_w_in"][0].astype(bf16)
    o1, o2 = A_WIDTH + 2 * A_KV_WIDTH, A_WIDTH + 2 * A_KV_WIDTH + B_WIDTH
    w_qkv, w_bx, w_bg = w_in[:, :o1], w_in[:, o1:o2], w_in[:, o2:]
    pqkv = mm_nn(x1b,w_qkv, name=tag + "_in_qkv")
    pbx = mm_nn(x1b,w_bx, name=tag + "_in_bx")
    pbg = mm_nn(x1b,w_bg, name=tag + "_in_bg")
    ya = attn_fwd(pqkv, W["a_sinks"], nb, name=tag + "_attn_fwd")
    xc = conv_fwd(pbx, W["b_conv_w"][0], W["b_conv_b"], False, nb, name=tag + "_conv_fwd")
    wa_s, wx_s = _blockdiag_slabs(W["b_wa"][0]), _blockdiag_slabs(W["b_wx"][0])
    yb, hh = rglru_fwd(xc, pbg, wa_s, wx_s, W["b_ba"], W["b_bx"], W["b_lam"], nb, name=tag + "_rglru_fwd")
    w_out = W["ab_w_out"][0].astype(bf16)
    x2, z1 = proj_ln([ya, yb], [w_out[:A_WIDTH], w_out[A_WIDTH:]], x1, g, b, name=tag + "_out_ln")
    saved = (pqkv, pbx, pbg, ya, xc, yb, hh, wa_s, wx_s, w_qkv, w_bx, w_bg, w_out)
    return x2, z1, saved


def _mixer_ab_bwd(x1b, dz1, dz1b, W, saved, nb, tag):
    pqkv, pbx, pbg, ya, xc, yb, hh, wa_s, wx_s, w_qkv, w_bx, w_bg, w_out = saved
    dya = mm_nn(dz1b, w_out[:A_WIDTH].T, name=tag + "_dya")
    dyb = mm_nn(dz1b, w_out[A_WIDTH:].T, name=tag + "_dyb")
    dwo = jnp.concatenate([mm_tn(ya, dz1b, name=tag + "_dwo_a"), mm_tn(yb, dz1b, name=tag + "_dwo_b")], 0)
    dpqkv, dsinks = attn_bwd(pqkv, W["a_sinks"], dya, nb, name=tag + "_attn_bwd")
    dxc, dpbg, dwa_s, dwx_s, dba, dbx, dlam = rglru_bwd(xc, pbg, hh, dyb, wa_s, wx_s, W["b_ba"], W["b_bx"],
                                                       W["b_lam"], nb, name=tag + "_rglru_bwd")
    dpbx, dconv_w, dconv_b = conv_bwd(pbx, W["b_conv_w"][0], W["b_conv_b"], dxc, False, nb, name=tag + "_conv_bwd")
    dw_in = jnp.concatenate([mm_tn(x1b,dpqkv, name=tag + "_dwin_qkv"), mm_tn(x1b,dpbx, name=tag + "_dwin_bx"),
                             mm_tn(x1b,dpbg, name=tag + "_dwin_bg")], 1)
    dx1 = mm_nn(dpqkv, w_qkv.T, add=dz1, add_scale=DN_ALPHA, name=tag + "_dx_qkv")
    dx1 = mm_nn(dpbx, w_bx.T, add=dx1, name=tag + "_dx_bx")
    dx1 = mm_nn(dpbg, w_bg.T, add=dx1, name=tag + "_dx_bg")
    grads = {"ab_w_in": dw_in[None], "a_sinks": dsinks, "b_conv_w": dconv_w[None], "b_conv_b": dconv_b,
             "b_wa": _slab_blocks(dwa_s)[None], "b_ba": dba, "b_wx": _slab_blocks(dwx_s)[None], "b_bx": dbx,
             "b_lam": dlam, "ab_w_out": dwo[None]}
    return dx1, grads


def _mixer_c_fwd(x1, x1b, W, g, b, nb, tag):
    w_in = W["c_w_in"][0].astype(bf16)
    d = w_in.shape[0]
    o1, o2 = 3 * C_WIDTH, 4 * C_WIDTH
    w_qkv, w_z = w_in[:, :o1], w_in[:, o1:o2]
    w_ba = jnp.concatenate([w_in[:, o2:], jnp.zeros((d, LANES - 2 * C_HEADS), bf16)], 1)
    pqkv = mm_nn(x1b,w_qkv, name=tag + "_in_qkv")
    pz = mm_nn(x1b,w_z, name=tag + "_in_z")
    pba = mm_nn(x1b,w_ba, name=tag + "_in_ba")
    zero_b = jnp.zeros((1, o1), f32)
    qkvc = conv_fwd(pqkv, W["c_conv_w"][0], zero_b, True, nb, name=tag + "_conv_fwd")
    prep = gdn_prep_fwd(qkvc, pba, W["c_a_log"], W["c_dt_bias"], nb, name=tag + "_prep_fwd")
    yc, states = gdn_rec_fwd(*prep, pz, W["c_norm_g"], nb, name=tag + "_rec_fwd")
    w_out = W["c_w_out"][0].astype(bf16)
    x2, z1 = proj_ln([yc], [w_out], x1, g, b, name=tag + "_out_ln")
    saved = (pqkv, pz, pba, qkvc, prep, states, yc, w_qkv, w_z, w_ba, w_out, zero_b)
    return x2, z1, saved


def _mixer_c_bwd(x1b, dz1, dz1b, W, saved, nb, tag):
    pqkv, pz, pba, qkvc, prep, states, yc, w_qkv, w_z, w_ba, w_out, zero_b = saved
    dyc = mm_nn(dz1b, w_out.T, name=tag + "_dyc")
    dwo = mm_tn(yc, dz1b, name=tag + "_dwo")
    rec = gdn_rec_bwd(*prep, pz, W["c_norm_g"], states, dyc, nb, name=tag + "_rec_bwd")
    cts, dpz, dng = rec[:6], rec[6], rec[7]
    dq, dk, dv, dpba, dalog, ddtb = gdn_prep_bwd(qkvc, pba, W["c_a_log"], W["c_dt_bias"], cts, nb,
                                                 name=tag + "_prep_bwd")
    dqkvc = jnp.concatenate([dq, dk, dv], 1)
    dpqkv, dconv_w, _ = conv_bwd(pqkv, W["c_conv_w"][0], zero_b, dqkvc, True, nb, name=tag + "_conv_bwd")
    dw_in = jnp.concatenate([mm_tn(x1b,dpqkv, name=tag + "_dwin_qkv"), mm_tn(x1b,dpz, name=tag + "_dwin_z"),
                             mm_tn(x1b,dpba, name=tag + "_dwin_ba")[:, :2 * C_HEADS]], 1)
    dx1 = mm_nn(dpqkv, w_qkv.T, add=dz1, add_scale=DN_ALPHA, name=tag + "_dx_qkv")
    dx1 = mm_nn(dpz, w_z.T, add=dx1, name=tag + "_dx_z")
    dx1 = mm_nn(dpba, w_ba.T, add=dx1, name=tag + "_dx_ba")
    grads = {"c_w_in": dw_in[None], "c_conv_w": dconv_w[None], "c_a_log": dalog[:, :C_HEADS],
             "c_dt_bias": ddtb[:, :C_HEADS], "c_norm_g": dng, "c_w_out": dwo[None]}
    return dx1, grads


def _local_step(x, p, target, W, F, on_ffn_grads):
    nb, s, d = x.shape
    t = nb * s
    h = x.reshape(t, d)
    tape = []
    for i in range(DEPTH):
        tag = f"l{i}"
        f1 = [F[k][i] for k in ("ffn1_wg", "ffn1_wu", "ffn1_wd")]
        f2 = [F[k][i] for k in ("ffn2_wg", "ffn2_wu", "ffn2_wd")]
        lg = [W["ln_g"][i, k][None] for k in range(3)]
        lb = [W["ln_b"][i, k][None] for k in range(3)]
        x1, z0, x1b = ffn_fwd(h, *f1, lg[0], lb[0], name=tag + "_ffn1_fwd")
        mixer = _mixer_ab_fwd if i % 2 == 0 else _mixer_c_fwd
        x2, z1, msaved = mixer(x1, x1b, W, lg[1], lb[1], nb, tag + "_mix")
        x3, z2, _ = ffn_fwd(x2, *f2, lg[2], lb[2], name=tag + "_ffn2_fwd")
        pi = p[i].reshape(t, -1)
        pw = (W["ple_wg"][i].astype(bf16), W["ple_bg"][i][None], W["ple_wp"][i].astype(bf16))
        x4 = ple_fwd(x3, pi, *pw, name=tag + "_ple_fwd")
        tape.append((h, z0, x1b, msaved, z1, x2, z2, x3, pi, pw, lg))
        h = x4
    dh, sq = loss_head(h, target.reshape(t, d), name="loss_head")
    loss = 0.5 * jnp.sum(sq) / d
    per_layer = [None] * DEPTH
    grads = {}
    for i in reversed(range(DEPTH)):
        tag = f"l{i}"
        h_in, z0, x1b, msaved, z1, x2, z2, x3, pi, pw, lg = tape[i]
        dx3, dple_wg, dple_bg, dple_wp = ple_bwd(x3, pi, dh, pw[0], pw[0].T, pw[1], pw[2], name=tag + "_ple_bwd")
        dz2, dz2b, dg2, db2 = ln_bwd(z2, dx3, lg[2], name=tag + "_ln2_bwd")
        f1 = [F[k][i] for k in ("ffn1_wg", "ffn1_wu", "ffn1_wd")]
        f2 = [F[k][i] for k in ("ffn2_wg", "ffn2_wu", "ffn2_wd")]
        dgate, dup, *df2 = ffn_bwd_weights(x2.astype(bf16), dz2b, *f2, name=tag + "_ffn2_bwd_w")
        on_ffn_grads(i, 3, df2)
        dx2 = ffn_bwd_input(dgate, dup, f2[0], f2[1], dz2, name=tag + "_ffn2_bwd_x")
        dz1, dz1b, dg1, db1 = ln_bwd(z1, dx2, lg[1], name=tag + "_ln1_bwd")
        mixer_bwd = _mixer_ab_bwd if i % 2 == 0 else _mixer_c_bwd
        dx1, mgrads = mixer_bwd(x1b, dz1, dz1b, W, msaved, nb, tag + "_mix")
        grads.update(mgrads)
        dz0, dz0b, dg0, db0 = ln_bwd(z0, dx1, lg[0], name=tag + "_ln0_bwd")
        dgate, dup, *df1 = ffn_bwd_weights(h_in.astype(bf16), dz0b, *f1, name=tag + "_ffn1_bwd_w")
        on_ffn_grads(i, 0, df1)
        dh = ffn_bwd_input(dgate, dup, f1[0], f1[1], dz0, name=tag + "_ffn1_bwd_x")
        per_layer[i] = {"ln_g": jnp.concatenate([dg0, dg1, dg2], 0), "ln_b": jnp.concatenate([db0, db1, db2], 0),
                        "ple_wg": dple_wg, "ple_bg": dple_bg[0], "ple_wp": dple_wp}
    for k in per_layer[0]:
        grads[k] = jnp.stack([per_layer[i][k] for i in range(DEPTH)])
    return loss, dh.reshape(nb, s, d), grads


WEIGHT_NAMES = ("ffn1_wg", "ffn1_wu", "ffn1_wd", "ffn2_wg", "ffn2_wu", "ffn2_wd", "ln_g", "ln_b", "ple_wg", "ple_bg",
                "ple_wp", "ab_w_in", "a_sinks", "b_conv_w", "b_conv_b", "b_wa", "b_ba", "b_wx", "b_bx", "b_lam",
                "ab_w_out", "c_w_in", "c_conv_w", "c_a_log", "c_dt_bias", "c_norm_g", "c_w_out")
NATIVE_NAMES = WEIGHT_NAMES[:6]
PACKED_NAMES = WEIGHT_NAMES[6:]
PACK_MATRICES = ("ple_wg", "ple_wp", "ab_w_in", "ab_w_out", "c_w_in", "c_w_out")
PACK_GROUPS = (tuple(k for k in PACKED_NAMES if k not in PACK_MATRICES), PACK_MATRICES)
PACK_TRANSIT = (f32, bf16)
SHARD_AXIS = {"ffn1_wg": 2, "ffn1_wu": 2, "ffn1_wd": 1, "ffn2_wg": 2, "ffn2_wu": 2, "ffn2_wd": 1, "ln_g": 2, "ln_b": 2,
              "ple_wg": 1, "ple_wp": 2, "ab_w_in": 2, "b_conv_w": 2, "ab_w_out": 1, "c_w_in": 2, "c_conv_w": 2,
              "c_w_out": 1}
N_CHIPS = 4
PACK_COLS = LANES
PACK_TILE_MULTIPLE = 256
ELEMENTWISE_BLOCK_ELEMS = 128 * 1024


def _row_tile(r, cols):
    return _tile(r, max(2 * SUBLANES, ELEMENTWISE_BLOCK_ELEMS // cols), 2 * SUBLANES)
MESH = pl.DeviceIdType.MESH
ANY = pl.BlockSpec(memory_space=pl.ANY)


def _tiled_dims(shape):
    w = shape[-1]
    r = 1
    for dim in shape[:-1]:
        r *= dim
    return r, w, -(-r // SUBLANES) * SUBLANES, -(-w // LANES) * LANES


def _pack(pieces, lead=()):
    k = len(lead)
    tiles = []
    for a in pieces:
        r, w, rp, wp = _tiled_dims(a.shape[k:])
        a2 = jnp.pad(a.reshape(lead + (r, w)), [(0, 0)] * k + [(0, rp - r), (0, wp - w)])
        a2 = a2.reshape(lead + (rp // SUBLANES, SUBLANES, wp // LANES, LANES))
        a2 = jnp.swapaxes(a2, k + 1, k + 2)
        tiles.append(a2.reshape(lead + (-1, SUBLANES, LANES)))
    flat = jnp.concatenate(tiles, axis=k)
    n = flat.shape[k]
    n_pad = -(-n // PACK_TILE_MULTIPLE) * PACK_TILE_MULTIPLE
    flat = jnp.pad(flat, [(0, 0)] * k + [(0, n_pad - n), (0, 0), (0, 0)])
    return flat.reshape(lead + (n_pad * SUBLANES, PACK_COLS))


def _unpack(pack, shapes, lead=()):
    k = len(lead)
    flat = pack.reshape(lead + (-1, SUBLANES, LANES))
    out, o = [], 0
    for shp in shapes:
        r, w, rp, wp = _tiled_dims(shp)
        n = (rp // SUBLANES) * (wp // LANES)
        a2 = lax.slice_in_dim(flat, o, o + n, axis=k).reshape(lead + (rp // SUBLANES, wp // LANES, SUBLANES, LANES))
        a2 = jnp.swapaxes(a2, k + 1, k + 2).reshape(lead + (rp, wp))
        a2 = lax.slice_in_dim(lax.slice_in_dim(a2, 0, r, axis=k), 0, w, axis=k + 1)
        out.append(a2.reshape(lead + tuple(shp)))
        o += n
    return out


def _mesh_position():
    x, y, c = lax.axis_index("x"), lax.axis_index("y"), lax.axis_index("c")
    chips = [(1 - x, y), (x, 1 - y), (1 - x, 1 - y)]
    return x, y, c, chips


def _remote(src, dst, send_sems, recv_sems, k, to):
    return pltpu.make_async_remote_copy(src_ref=src, dst_ref=dst, send_sem=send_sems.at[k], recv_sem=recv_sems.at[k],
                                        device_id=to, device_id_type=MESH)


def _sems(n):
    return pltpu.SemaphoreType.DMA((n,))


def place_slot(parts, slots, n_slots, dtype, from_slot, *, name):
    n = len(parts)
    r, cols = parts[0].shape[-2:]
    tr = _row_tile(r, cols)

    def body(src_ref, dst_ref, *refs):
        for a in range(n):
            refs[n + a][...] = refs[a][...].astype(dtype)

    dst = pl.BlockSpec((None, tr, cols), lambda i, src_ref, dst_ref: (dst_ref[0], i, 0))
    src = (pl.BlockSpec((None, tr, cols), lambda i, src_ref, dst_ref: (src_ref[0], i, 0)) if from_slot
           else pl.BlockSpec((tr, cols), lambda i, src_ref, dst_ref: (i, 0)))
    return pl.pallas_call(
        body,
        grid_spec=pltpu.PrefetchScalarGridSpec(num_scalar_prefetch=2, grid=(r // tr,), in_specs=[src] * n,
                                               out_specs=[dst] * n),
        out_shape=[jax.ShapeDtypeStruct((n_slots, r, cols), dtype)] * n,
        compiler_params=_params(("parallel",)), name=name,
    )(*slots, *parts)


def gather_shards(bufs, *, name):
    n = len(bufs)

    def body(*refs):
        out_refs = refs[n:2 * n]
        send_sems, recv_sems = refs[2 * n:]
        x, y, c, chips = _mesh_position()
        me = 2 * x + y
        sibling = (x, y, 1 - c)
        waits = []
        for j, (cx, cy) in enumerate(chips):
            for a in range(n):
                own = out_refs[a].at[me, c]
                cp = _remote(own, own, send_sems, recv_sems, 6 * a + j, (cx, cy, c))
                cp.start()
                waits.append(cp.wait_send)
        for j, (cx, cy) in enumerate(chips):
            for a in range(n):
                got = out_refs[a].at[2 * cx + cy, c]
                _remote(got, got, send_sems, recv_sems, 6 * a + j, (cx, cy, c)).wait_recv()
                fw = _remote(got, got, send_sems, recv_sems, 6 * a + 3 + j, sibling)
                fw.start()
                waits.append(fw.wait_send)
        for j, (cx, cy) in enumerate(chips):
            for a in range(n):
                got = out_refs[a].at[2 * cx + cy, 1 - c]
                _remote(got, got, send_sems, recv_sems, 6 * a + 3 + j, sibling).wait_recv()
        for wait in waits:
            wait()

    return pl.pallas_call(
        body, out_shape=[jax.ShapeDtypeStruct(b.shape, b.dtype) for b in bufs],
        in_specs=[ANY] * n, out_specs=[ANY] * n, scratch_shapes=[_sems(6 * n), _sems(6 * n)],
        input_output_aliases={a: a for a in range(n)}, name=name,
    )(*bufs)


def chip_exchange(ps, qs, *, name):
    n = len(ps)

    def body(*refs):
        p_refs, q_refs = refs[:n], refs[2 * n:3 * n]
        send_sems, recv_sems = refs[3 * n:]
        x, y, c, chips = _mesh_position()
        me = 2 * x + y
        waits = []
        for j, (cx, cy) in enumerate(chips):
            for a in range(n):
                cp = _remote(p_refs[a].at[2 * cx + cy], q_refs[a].at[me], send_sems, recv_sems, 3 * a + j, (cx, cy, c))
                cp.start()
                waits.append(cp.wait_send)
        for j, (cx, cy) in enumerate(chips):
            for a in range(n):
                got = q_refs[a].at[2 * cx + cy]
                _remote(got, got, send_sems, recv_sems, 3 * a + j, (cx, cy, c)).wait_recv()
        for wait in waits:
            wait()

    return pl.pallas_call(
        body, out_shape=[jax.ShapeDtypeStruct(q_.shape, q_.dtype) for q_ in qs], in_specs=[ANY] * (2 * n),
        out_specs=[ANY] * n, scratch_shapes=[_sems(3 * n), _sems(3 * n)],
        input_output_aliases={n + a: a for a in range(n)}, name=name,
    )(*ps, *qs)


def gather_slots_async(bufs, collective_id, sources=None, *, name):
    n = len(bufs)
    refs = [jax.new_ref(b, memory_space=pltpu.MemorySpace.HBM) for b in bufs]
    src_refs = None if sources is None else [jax.new_ref(s_, memory_space=pltpu.MemorySpace.HBM) for s_ in sources]

    @pl.kernel(mesh=plsc.ScalarSubcoreMesh(axis_name="sequencer", num_cores=1), name=name,
               scratch_types=(_sems(3 * n), _sems(3 * n)),
               compiler_params=pltpu.CompilerParams(collective_id=collective_id))
    def launch(send_sems, recv_sems):
        x, y, c, chips = _mesh_position()
        me = 2 * x + y
        barrier = pltpu.get_barrier_semaphore()
        for cx, cy in chips:
            pl.semaphore_signal(barrier, inc=1, device_id=(cx, cy, c), device_id_type=MESH)
        pl.semaphore_wait(barrier, len(chips))
        sends = []
        for j, (cx, cy) in enumerate(chips):
            for a in range(n):
                own = refs[a].at[me]
                src = own if src_refs is None else src_refs[a].at[2 * cx + cy]
                cp = _remote(src, own, send_sems, recv_sems, 3 * a + j, (cx, cy, c))
                cp.start()
                sends.append(cp)
        for j, (cx, cy) in enumerate(chips):
            for a in range(n):
                got = refs[a].at[2 * cx + cy]
                _remote(got, got, send_sems, recv_sems, 3 * a + j, (cx, cy, c)).wait_recv()
        for cp in sends:
            cp.wait_send()

    launch()
    return [r[...] for r in refs]


N_DEVICES = 8
PEER_FLIPS = tuple((dx, dy, dc) for dx in (0, 1) for dy in (0, 1) for dc in (0, 1) if dx or dy or dc)


def exchange_partials_async(sends, recvs, collective_id, *, name):
    n = len(sends)
    s_refs = [jax.new_ref(a, memory_space=pltpu.MemorySpace.HBM) for a in sends]
    r_refs = [jax.new_ref(a, memory_space=pltpu.MemorySpace.HBM) for a in recvs]
    k = len(PEER_FLIPS)

    @pl.kernel(mesh=plsc.ScalarSubcoreMesh(axis_name="sequencer", num_cores=1), name=name,
               scratch_types=(_sems(k), _sems(k)), compiler_params=pltpu.CompilerParams(collective_id=collective_id))
    def launch(send_sems, recv_sems):
        x, y, c, _ = _mesh_position()
        me = 4 * x + 2 * y + c
        peers = [(1 - x if dx else x, 1 - y if dy else y, 1 - c if dc else c) for dx, dy, dc in PEER_FLIPS]
        barrier = pltpu.get_barrier_semaphore()
        for peer in peers:
            pl.semaphore_signal(barrier, inc=1, device_id=peer, device_id_type=MESH)
        pl.semaphore_wait(barrier, len(peers))
        sends_started = []
        for j, (px, py, pc) in enumerate(peers):
            for a in range(n):
                cp = _remote(s_refs[a].at[2 * px + py], r_refs[a].at[me], send_sems, recv_sems, j, (px, py, pc))
                cp.start()
                sends_started.append(cp)
        for j, (px, py, pc) in enumerate(peers):
            for a in range(n):
                got = r_refs[a].at[4 * px + 2 * py + pc]
                _remote(got, got, send_sems, recv_sems, j, (px, py, pc)).wait_recv()
        for cp in sends_started:
            cp.wait_send()

    launch()
    return [r[...] for r in r_refs]


def sibling_exchange(gs, *, name):
    n = len(gs)

    def body(*refs):
        g_refs, out_refs = refs[:n], refs[n:2 * n]
        send_sems, recv_sems = refs[2 * n:]
        x, y, c, _ = _mesh_position()
        cps = [_remote(g_refs[a].at[:, 1 - c], out_refs[a], send_sems, recv_sems, a, (x, y, 1 - c)) for a in range(n)]
        for cp in cps:
            cp.start()
        for cp in cps:
            cp.wait()

    return pl.pallas_call(
        body, out_shape=[jax.ShapeDtypeStruct(g.shape[:1] + g.shape[2:], g.dtype) for g in gs],
        in_specs=[ANY] * n, out_specs=[ANY] * n, scratch_shapes=[_sems(n), _sems(n)], name=name,
    )(*gs)


def add_own_half(gs, others, c_idx, dtype, *, name):
    n = len(gs)
    ns, _, r, cols = gs[0].shape
    tr = _row_tile(r, cols)

    def body(c_ref, *refs):
        for a in range(n):
            refs[2 * n + a][...] = (refs[a][...] + refs[n + a][...]).astype(dtype)

    own = pl.BlockSpec((None, None, tr, cols), lambda s, i, c_ref: (s, c_ref[0], i, 0))
    oth = pl.BlockSpec((None, tr, cols), lambda s, i, c_ref: (s, i, 0))
    return pl.pallas_call(
        body,
        grid_spec=pltpu.PrefetchScalarGridSpec(num_scalar_prefetch=1, grid=(ns, r // tr),
                                               in_specs=[own] * n + [oth] * n, out_specs=[oth] * n),
        out_shape=[jax.ShapeDtypeStruct((ns, r, cols), dtype)] * n,
        compiler_params=_params(("parallel", "parallel")), name=name,
    )(c_idx, *gs, *others)


def sum_slots(qs, *, name):
    n = len(qs)
    ns, r, cols = qs[0].shape
    tr = _row_tile(r, cols * ns)

    def body(*refs):
        for a in range(n):
            q_ref = refs[a]
            acc = q_ref[0].astype(f32) + q_ref[1].astype(f32)
            for i in range(2, ns):
                acc = acc + q_ref[i].astype(f32)
            refs[n + a][...] = acc

    return pl.pallas_call(
        body, grid=(r // tr,), in_specs=[pl.BlockSpec((ns, tr, cols), lambda i: (0, i, 0))] * n,
        out_specs=[pl.BlockSpec((tr, cols), lambda i: (i, 0))] * n,
        out_shape=[jax.ShapeDtypeStruct((r, cols), f32)] * n,
        compiler_params=_params(("parallel",)), name=name,
    )(*qs)


def sibling_share(bufs, *, name):
    n = len(bufs)

    def body(*refs):
        out_refs = refs[n:2 * n]
        send_sems, recv_sems = refs[2 * n:]
        x, y, c, _ = _mesh_position()
        sibling = (x, y, 1 - c)
        cps = []
        for a in range(n):
            own = out_refs[a].at[c]
            cp = _remote(own, own, send_sems, recv_sems, a, sibling)
            cp.start()
            cps.append(cp)
        for a in range(n):
            theirs = out_refs[a].at[1 - c]
            _remote(theirs, theirs, send_sems, recv_sems, a, sibling).wait_recv()
        for cp in cps:
            cp.wait_send()

    return pl.pallas_call(
        body, out_shape=[jax.ShapeDtypeStruct(b.shape, b.dtype) for b in bufs], in_specs=[ANY] * n,
        out_specs=[ANY] * n, scratch_shapes=[_sems(n), _sems(n)],
        input_output_aliases={a: a for a in range(n)}, name=name,
    )(*bufs)


def _adamw_update(w, g, m, v):
    m2 = ADAM_B1 * m + (1.0 - ADAM_B1) * g
    v2 = ADAM_B2 * v + (1.0 - ADAM_B2) * (g * g)
    m_hat = m2 / (1.0 - ADAM_B1 ** ADAM_STEP)
    v_hat = v2 / (1.0 - ADAM_B2 ** ADAM_STEP)
    return -ADAM_LR * (m_hat / (jnp.sqrt(v_hat) + ADAM_EPS) + ADAM_WD * w), m2, v2


def adamw_from_partials(ws, ms, vs, slots, layer, acc, *, name):
    n = len(ws)
    nl, r, cols = ws[0].shape
    ns = slots[0].shape[0]
    tr = _row_tile(r, cols * 2)

    def body(*refs):
        for a in range(n):
            w_ref, m_ref, v_ref, s_ref = (refs[k * n + a] for k in range(4))
            g_ref, d_ref, m2_ref, v2_ref = (refs[len(refs) - 4 * n + k * n + a] for k in range(4))
            g = s_ref[0].astype(f32) + s_ref[1].astype(f32)
            for i in range(2, ns):
                g = g + s_ref[i].astype(f32)
            g_ref[...] = g
            d_ref[...], m2_ref[...], v2_ref[...] = _adamw_update(w_ref[...], g, m_ref[...], v_ref[...])

    lay = pl.BlockSpec((None, tr, cols), lambda i: (layer, i, 0))
    in_specs = [lay] * (3 * n) + [pl.BlockSpec((ns, tr, cols), lambda i: (0, i, 0))] * n
    args = [*ws, *ms, *vs, *slots]
    aliases = {}
    if acc is not None:
        in_specs += [ANY] * (4 * n)
        args += [a for lst in acc for a in lst]
        aliases = {4 * n + k: k for k in range(4 * n)}
    out = pl.pallas_call(
        body, grid=(r // tr,), in_specs=in_specs, out_specs=[lay] * (4 * n),
        out_shape=[jax.ShapeDtypeStruct((nl, r, cols), f32)] * (4 * n), input_output_aliases=aliases,
        compiler_params=_params(("parallel",)), name=name,
    )(*args)
    return [list(out[k * n:(k + 1) * n]) for k in range(4)]


def adamw(ws, gs, ms, vs, *, name):
    n = len(ws)
    r, cols = ws[0].shape
    tr = _row_tile(r, cols)

    def body(*refs):
        for a in range(n):
            w_ref, g_ref, m_ref, v_ref = (refs[k * n + a] for k in range(4))
            d_ref, m2_ref, v2_ref = (refs[(4 + k) * n + a] for k in range(3))
            d_ref[...], m2_ref[...], v2_ref[...] = _adamw_update(w_ref[...], g_ref[...], m_ref[...], v_ref[...])

    row = pl.BlockSpec((tr, cols), lambda i: (i, 0))
    out = pl.pallas_call(
        body, grid=(r // tr,), in_specs=[row] * (4 * n), out_specs=[row] * (3 * n),
        out_shape=[jax.ShapeDtypeStruct((r, cols), f32)] * (3 * n),
        compiler_params=_params(("parallel",)), name=name,
    )(*ws, *gs, *ms, *vs)
    return out[:n], out[n:2 * n], out[2 * n:]


def _full_weights(gathered, names, weights):
    pieces = _unpack(gathered, [weights[k].shape for k in names], lead=(N_CHIPS,))
    full = {}
    for name, pc in zip(names, pieces):
        ax = SHARD_AXIS.get(name)
        if ax is None:
            full[name] = weights[name]
        else:
            shp = weights[name].shape
            full[name] = jnp.moveaxis(pc, 0, ax).reshape(shp[:ax] + (N_CHIPS * shp[ax],) + shp[ax + 1:])
    return full


def _grad_pack(grads, names, shapes):
    pieces = []
    for name, shp in zip(names, shapes):
        g = grads[name]
        ax = SHARD_AXIS.get(name)
        if ax is None:
            pieces.append(jnp.broadcast_to(g.reshape(shp)[None], (N_CHIPS,) + tuple(shp)))
        else:
            pieces.append(jnp.stack(jnp.split(g, N_CHIPS, axis=ax)))
    return _pack(pieces, lead=(N_CHIPS,))


def _by_shape(arrays):
    groups = {}
    for i, a in enumerate(arrays):
        groups.setdefault(a.shape, []).append(i)
    return list(groups.values())


def _grouped(fn, lists, n_out, tag):
    outs = [[None] * len(lists[0]) for _ in range(n_out)]
    for gi, idx in enumerate(_by_shape(lists[0])):
        res = fn(*[[lst[i] for i in idx] for lst in lists], name=f"{tag}_{gi}")
        res = res if n_out > 1 else (res,)
        for k in range(n_out):
            for i, r in zip(idx, res[k]):
                outs[k][i] = r
    return outs if n_out > 1 else outs[0]


def _train_step(x, p, loss_target, weights, m, v):
    shapes = [[weights[k].shape for k in names] for names in PACK_GROUPS]
    halves = lambda a: a.reshape((2, a.shape[0] // 2) + a.shape[1:])
    packs = lambda d_: [halves(_pack([d_[k] for k in names])) for names in PACK_GROUPS]
    nn_ = len(NATIVE_NAMES)
    local = [weights[k] for k in NATIVE_NAMES] + packs(weights)
    local_m = [m[k] for k in NATIVE_NAMES] + packs(m)
    local_v = [v[k] for k in NATIVE_NAMES] + packs(v)
    flat = lambda lst: [a.reshape((-1, a.shape[-1])) for a in lst]
    c_idx = lax.axis_index("c").astype(jnp.int32).reshape(1)
    chip_idx = (2 * lax.axis_index("x") + lax.axis_index("y")).astype(jnp.int32).reshape(1)
    c2 = (c_idx, c_idx)
    chip2 = (chip_idx, chip_idx)
    chip_dev = (chip_idx, 2 * chip_idx + c_idx)

    def placed(arrays, slot, n_slots, dtype, from_slot, tag):
        return _grouped(lambda a, name: place_slot(a, slot, n_slots, dtype, from_slot, name=name), [arrays], 1, tag)

    ffn_own = [weights[k][i] for i in range(DEPTH) for k in NATIVE_NAMES]
    ffn_bufs = placed(ffn_own, chip2, N_CHIPS, bf16, False, "place_ffn_weights")
    group = len(NATIVE_NAMES) // 2
    n_ffn_groups = len(ffn_bufs) // group
    ffn_gathered = []
    for gi in range(n_ffn_groups):
        ffn_gathered += gather_slots_async(ffn_bufs[gi * group:(gi + 1) * group], collective_id=1 + gi,
                                           name=f"comm_gather_ffn_{gi}")
    ffn_weights = {k: [ffn_gathered[i * len(NATIVE_NAMES) + j] for i in range(DEPTH)] for j, k in enumerate(NATIVE_NAMES)}
    pack_bufs = [placed(flat([a]), chip2, N_CHIPS, dt, False, f"place_packed_weights_{gi}")[0].reshape((N_CHIPS,) + a.shape)
                 for gi, (a, dt) in enumerate(zip(local[nn_:], PACK_TRANSIT))]
    full = {}
    for names, gathered in zip(PACK_GROUPS, gather_shards(pack_bufs, name="comm_gather_weights")):
        full.update(_full_weights(gathered, names, weights))
    first_grad_id = n_ffn_groups + 1
    in_flight = {}

    def on_ffn_grads(layer, first, partials):
        tag = f"ffn_grads_l{layer}_{first}"
        recvs = placed(partials, chip_dev, N_DEVICES, bf16, True, "place_" + tag)
        got = exchange_partials_async(partials, recvs, collective_id=first_grad_id + len(in_flight), name="comm_" + tag)
        in_flight[(layer, first)] = got

    loss, grad_x, grads = _local_step(x, p, loss_target, full, ffn_weights, on_ffn_grads)
    gs = [_grad_pack(grads, names, shp).reshape((N_CHIPS,) + a.shape)
          for names, shp, a in zip(PACK_GROUPS, shapes, local[nn_:])]
    others = sibling_exchange(gs, name="comm_grad_sibling")
    chip_sums = [add_own_half([g], [o], c_idx, dt, name=f"grad_add_sibling_{gi}")[0]
                 for gi, (g, o, dt) in enumerate(zip(gs, others, PACK_TRANSIT))]
    own = [placed([cs], chip2, N_CHIPS, dt, True, f"place_own_partial_{gi}")[0]
           for gi, (cs, dt) in enumerate(zip(chip_sums, PACK_TRANSIT))]
    slots = chip_exchange(chip_sums, own, name="comm_grad_chips")
    mine = _grouped(sum_slots, [list(slots)], 1, "grad_sum_chips")
    pack_sum = sibling_share(placed(mine, c2, 2, f32, False, "place_own_half"), name="comm_grad_share")
    ffn_out = [{} for _ in range(4)]
    for (layer, first), got in in_flight.items():
        names = NATIVE_NAMES[first:first + len(got)]
        for idx in _by_shape([weights[k] for k in names]):
            ks = [names[i] for i in idx]
            acc = [[out[k] for k in ks] for out in ffn_out] if ks[0] in ffn_out[0] else None
            res = adamw_from_partials([weights[k] for k in ks], [m[k] for k in ks], [v[k] for k in ks],
                                      [got[i] for i in idx], layer, acc, name=f"adamw_ffn_l{layer}_{first + idx[0]}")
            for out, arrays in zip(ffn_out, res):
                out.update(zip(ks, arrays))
    pack_out = [list(pack_sum)] + _grouped(adamw, [flat(local[nn_:]), flat(pack_sum), flat(local_m[nn_:]),
                                                    flat(local_v[nn_:])], 3, "adamw_packed")
    loss = lax.psum(loss, ("x", "y", "c"))
    outs = []
    for by_name, packs_ in zip(ffn_out, pack_out):
        by_name = dict(by_name)
        for names, shp, pk in zip(PACK_GROUPS, shapes, packs_):
            by_name.update(zip(names, _unpack(pk, shp)))
        outs += [by_name[k] for k in WEIGHT_NAMES]
    return (loss, grad_x, *outs)


def kernel(x, p, ffn1_wg, ffn1_wu, ffn1_wd, ffn2_wg, ffn2_wu, ffn2_wd, ln_g, ln_b, ple_wg, ple_bg, ple_wp, ab_w_in, a_sinks, b_conv_w, b_conv_b, b_wa, b_ba, b_wx, b_bx, b_lam, ab_w_out, c_w_in, c_conv_w, c_a_log, c_dt_bias, c_norm_g, c_w_out, loss_target, m_ffn1_wg, m_ffn1_wu, m_ffn1_wd, m_ffn2_wg, m_ffn2_wu, m_ffn2_wd, m_ln_g, m_ln_b, m_ple_wg, m_ple_bg, m_ple_wp, m_ab_w_in, m_a_sinks, m_b_conv_w, m_b_conv_b, m_b_wa, m_b_ba, m_b_wx, m_b_bx, m_b_lam, m_ab_w_out, m_c_w_in, m_c_conv_w, m_c_a_log, m_c_dt_bias, m_c_norm_g, m_c_w_out, v_ffn1_wg, v_ffn1_wu, v_ffn1_wd, v_ffn2_wg, v_ffn2_wu, v_ffn2_wd, v_ln_g, v_ln_b, v_ple_wg, v_ple_bg, v_ple_wp, v_ab_w_in, v_a_sinks, v_b_conv_w, v_b_conv_b, v_b_wa, v_b_ba, v_b_wx, v_b_bx, v_b_lam, v_ab_w_out, v_c_w_in, v_c_conv_w, v_c_a_log, v_c_dt_bias, v_c_norm_g, v_c_w_out):
    weights = [ffn1_wg, ffn1_wu, ffn1_wd, ffn2_wg, ffn2_wu, ffn2_wd, ln_g, ln_b, ple_wg, ple_bg, ple_wp, ab_w_in, a_sinks,
               b_conv_w, b_conv_b, b_wa, b_ba, b_wx, b_bx, b_lam, ab_w_out, c_w_in, c_conv_w, c_a_log, c_dt_bias, c_norm_g,
               c_w_out]
    m = [m_ffn1_wg, m_ffn1_wu, m_ffn1_wd, m_ffn2_wg, m_ffn2_wu, m_ffn2_wd, m_ln_g, m_ln_b, m_ple_wg, m_ple_bg, m_ple_wp,
         m_ab_w_in, m_a_sinks, m_b_conv_w, m_b_conv_b, m_b_wa, m_b_ba, m_b_wx, m_b_bx, m_b_lam, m_ab_w_out, m_c_w_in,
         m_c_conv_w, m_c_a_log, m_c_dt_bias, m_c_norm_g, m_c_w_out]
    v = [v_ffn1_wg, v_ffn1_wu, v_ffn1_wd, v_ffn2_wg, v_ffn2_wu, v_ffn2_wd, v_ln_g, v_ln_b, v_ple_wg, v_ple_bg, v_ple_wp,
         v_ab_w_in, v_a_sinks, v_b_conv_w, v_b_conv_b, v_b_wa, v_b_ba, v_b_wx, v_b_bx, v_b_lam, v_ab_w_out, v_c_w_in,
         v_c_conv_w, v_c_a_log, v_c_dt_bias, v_c_norm_g, v_c_w_out]
    return _train_step(x, p, loss_target, dict(zip(WEIGHT_NAMES, weights)), dict(zip(WEIGHT_NAMES, m)),
                       dict(zip(WEIGHT_NAMES, v)))
```

```python
import functools

import jax
import jax.numpy as jnp
from jax import lax
from jax.experimental import pallas as pl
from jax.experimental.pallas import tpu as pltpu
from jax.experimental.pallas import tpu_sc as plsc

f32 = jnp.float32
bf16 = jnp.bfloat16

DEPTH = 2
CHUNK = 64
A_HEADS, A_KV_HEADS, A_GROUP, A_HEAD_DIM = 8, 2, 4, 64
A_WIDTH, A_KV_WIDTH, A_WINDOW = 512, 128, 128
B_WIDTH, B_BLOCKS, B_BLOCK, B_CONV = 512, 8, 64, 4
RG_C = 8.0
C_HEADS, C_HEAD_DIM, C_WIDTH, C_CONV = 8, 128, 1024, 4
DN_ALPHA = (2.0 * DEPTH) ** 0.25
LN_EPS = 1e-5
NORM_EPS = 1e-6
NEG = -1e30
ADAM_LR, ADAM_B1, ADAM_B2, ADAM_EPS, ADAM_WD, ADAM_STEP = 0.001, 0.9, 0.999, 1e-08, 0.01, 10

VMEM_LIMIT_BYTES = 56 * 1024 * 1024
LANES = 128
SUBLANES = 8
GROUP_W = 128
PREP_FWD_UNROLL = 8
PREP_BWD_UNROLL = 8
C_HEADS_PER_STEP = 4
GDN_TIME_BLOCK = 512

NN = ((1,), (0,))
NT = ((1,), (1,))
TN = ((0,), (0,))


def _params(sem):
    return pltpu.CompilerParams(dimension_semantics=sem, vmem_limit_bytes=VMEM_LIMIT_BYTES)


def _tile(n, cap, mult):
    best = None
    t = mult
    while t <= min(n, cap):
        if n % t == 0:
            best = t
        t += mult
    return best if best is not None else n


def _bdot(a, b, dims):
    return lax.dot_general(a.astype(bf16), b.astype(bf16), (dims, ((), ())), preferred_element_type=f32)


def _running_sum(x, reverse):
    s = x.shape[0]
    t = lax.broadcasted_iota(jnp.int32, x.shape, 0)
    d = 1
    while d < s:
        if reverse:
            x = x + jnp.where(t < s - d, pltpu.roll(x, s - d, 0), 0.0)
        else:
            x = x + jnp.where(t >= d, pltpu.roll(x, d, 0), 0.0)
        d *= 2
    return x


@jax.custom_vjp
def _cumsum0(x):
    return _running_sum(x, False)


def _cumsum0_fwd(x):
    return _running_sum(x, False), None


def _cumsum0_bwd(_, g):
    return (_running_sum(g, True),)


_cumsum0.defvjp(_cumsum0_fwd, _cumsum0_bwd)


@jax.custom_vjp
def _bnn(a, b):
    return _bdot(a, b, NN)


def _bnn_fwd(a, b):
    return _bdot(a, b, NN), (a, b)


def _bnn_bwd(res, g):
    a, b = res
    return _bdot(g, b, NT), _bdot(a, g, TN)


_bnn.defvjp(_bnn_fwd, _bnn_bwd)


@jax.custom_vjp
def _bnt(a, b):
    return _bdot(a, b, NT)


def _bnt_fwd(a, b):
    return _bdot(a, b, NT), (a, b)


def _bnt_bwd(res, g):
    a, b = res
    return _bdot(g, b, NN), _bdot(g, a, TN)


_bnt.defvjp(_bnt_fwd, _bnt_bwd)


@jax.custom_vjp
def _btn(a, b):
    return _bdot(a, b, TN)


def _btn_fwd(a, b):
    return _bdot(a, b, TN), (a, b)


def _btn_bwd(res, g):
    a, b = res
    return _bdot(b, g, NT), _bdot(a, g, NN)


_btn.defvjp(_btn_fwd, _btn_bwd)

RAW_DOTS = (lambda a, b: _bdot(a, b, NN), lambda a, b: _bdot(a, b, NT), lambda a, b: _bdot(a, b, TN),
            lambda x: _running_sum(x, False))
VJP_DOTS = (_bnn, _bnt, _btn, _cumsum0)


def _layer_norm(z, g, b):
    mu = jnp.mean(z, -1, keepdims=True)
    d = z - mu
    var = jnp.mean(d * d, -1, keepdims=True)
    return d * lax.rsqrt(var + LN_EPS) * g + b


def _silu(x):
    return x * jax.nn.sigmoid(x)


def mm_nn(a, w, add=None, add_scale=1.0, *, name):
    m, k = a.shape
    n = w.shape[1]
    tm = _tile(m, 512, SUBLANES)
    tn = _tile(n, 1024, LANES)

    def body(*refs):
        if add is None:
            a_ref, w_ref, o_ref = refs
            o_ref[...] = _bdot(a_ref[...], w_ref[...], NN)
        else:
            a_ref, w_ref, add_ref, o_ref = refs
            o_ref[...] = _bdot(a_ref[...], w_ref[...], NN) + add_scale * add_ref[...]

    in_specs = [pl.BlockSpec((tm, k), lambda i, j: (i, 0)), pl.BlockSpec((k, tn), lambda i, j: (0, j))]
    args = [a, w]
    if add is not None:
        in_specs.append(pl.BlockSpec((tm, tn), lambda i, j: (i, j)))
        args.append(add)
    return pl.pallas_call(
        body, grid=(m // tm, n // tn), in_specs=in_specs,
        out_specs=pl.BlockSpec((tm, tn), lambda i, j: (i, j)),
        out_shape=jax.ShapeDtypeStruct((m, n), f32),
        compiler_params=_params(("parallel", "parallel")), name=name,
    )(*args)


def mm_tn(a, b, *, name):
    m, k = a.shape
    n = b.shape[1]
    tm = _tile(m, 1024, 2 * SUBLANES)
    tn = _tile(n, 1024, LANES)

    def body(a_ref, b_ref, o_ref):
        part = _bdot(a_ref[...], b_ref[...], TN)

        @pl.when(pl.program_id(1) == 0)
        def _():
            o_ref[...] = part

        @pl.when(pl.program_id(1) > 0)
        def _():
            o_ref[...] += part

    return pl.pallas_call(
        body, grid=(n // tn, m // tm),
        in_specs=[pl.BlockSpec((tm, k), lambda j, i: (i, 0)), pl.BlockSpec((tm, tn), lambda j, i: (i, j))],
        out_specs=pl.BlockSpec((k, tn), lambda j, i: (0, j)),
        out_shape=jax.ShapeDtypeStruct((k, n), f32),
        compiler_params=_params(("parallel", "arbitrary")), name=name,
    )(a, b)


def proj_ln(a_list, w_list, xres, g, b, *, name):
    t, d = xres.shape
    tm = _tile(t, 256, 2 * SUBLANES)
    na = len(a_list)

    def body(*refs):
        a_refs, w_refs = refs[:na], refs[na:2 * na]
        x_ref, g_ref, b_ref, y_ref, z_ref, yb_ref = refs[2 * na:]
        z = DN_ALPHA * x_ref[...]
        for a_ref, w_ref in zip(a_refs, w_refs):
            z = z + _bdot(a_ref[...], w_ref[...], NN)
        z_ref[...] = z
        y = _layer_norm(z, g_ref[...], b_ref[...])
        y_ref[...] = y
        yb_ref[...] = y.astype(bf16)

    in_specs = [pl.BlockSpec((tm, a.shape[1]), lambda i: (i, 0)) for a in a_list]
    in_specs += [pl.BlockSpec(w.shape, lambda i: (0, 0)) for w in w_list]
    in_specs += [pl.BlockSpec((tm, d), lambda i: (i, 0)), pl.BlockSpec((1, d), lambda i: (0, 0)),
                 pl.BlockSpec((1, d), lambda i: (0, 0))]
    return pl.pallas_call(
        body, grid=(t // tm,), in_specs=in_specs,
        out_specs=[pl.BlockSpec((tm, d), lambda i: (i, 0))] * 3,
        out_shape=[jax.ShapeDtypeStruct((t, d), f32)] * 2 + [jax.ShapeDtypeStruct((t, d), bf16)],
        compiler_params=_params(("parallel",)), name=name,
    )(*a_list, *w_list, xres, g, b)


def ln_bwd(z, dy, g, *, name):
    t, d = z.shape
    tm = _tile(t, 512, SUBLANES)

    def body(z_ref, dy_ref, g_ref, dz_ref, dzb_ref, dg_ref, db_ref):
        zz = z_ref[...]
        dy_ = dy_ref[...]
        mu = jnp.mean(zz, -1, keepdims=True)
        dd = zz - mu
        var = jnp.mean(dd * dd, -1, keepdims=True)
        rstd = lax.rsqrt(var + LN_EPS)
        xhat = dd * rstd
        dxh = dy_ * g_ref[...]
        dz = rstd * (dxh - jnp.mean(dxh, -1, keepdims=True) - xhat * jnp.mean(dxh * xhat, -1, keepdims=True))
        dz_ref[...] = dz
        dzb_ref[...] = dz.astype(bf16)
        pg = jnp.sum(dy_ * xhat, 0, keepdims=True)
        pb = jnp.sum(dy_, 0, keepdims=True)

        @pl.when(pl.program_id(0) == 0)
        def _():
            dg_ref[...] = pg
            db_ref[...] = pb

        @pl.when(pl.program_id(0) > 0)
        def _():
            dg_ref[...] += pg
            db_ref[...] += pb

    row = pl.BlockSpec((tm, d), lambda i: (i, 0))
    vec = pl.BlockSpec((1, d), lambda i: (0, 0))
    return pl.pallas_call(
        body, grid=(t // tm,), in_specs=[row, row, vec], out_specs=[row, row, vec, vec],
        out_shape=[jax.ShapeDtypeStruct((t, d), f32), jax.ShapeDtypeStruct((t, d), bf16),
                   jax.ShapeDtypeStruct((1, d), f32), jax.ShapeDtypeStruct((1, d), f32)],
        compiler_params=_params(("arbitrary",)), name=name,
    )(z, dy, g)


def loss_head(y, target, *, name):
    t, d = y.shape
    tm = _tile(t, 512, SUBLANES)

    def body(y_ref, t_ref, dy_ref, sq_ref):
        e = y_ref[...] - t_ref[...]
        dy_ref[...] = e * (1.0 / d)
        part = jnp.sum(e * e, 0, keepdims=True)

        @pl.when(pl.program_id(0) == 0)
        def _():
            sq_ref[...] = part

        @pl.when(pl.program_id(0) > 0)
        def _():
            sq_ref[...] += part

    row = pl.BlockSpec((tm, d), lambda i: (i, 0))
    vec = pl.BlockSpec((1, d), lambda i: (0, 0))
    return pl.pallas_call(
        body, grid=(t // tm,), in_specs=[row, row], out_specs=[row, vec],
        out_shape=[jax.ShapeDtypeStruct((t, d), f32), jax.ShapeDtypeStruct((1, d), f32)],
        compiler_params=_params(("arbitrary",)), name=name,
    )(y, target)


FFN_COL_BLOCK = 256
FFN_ROWS = 1024


def _lane_blocks(n):
    return [slice(s, min(s + FFN_COL_BLOCK, n)) for s in range(0, n, FFN_COL_BLOCK)]


def ffn_fwd(x, wg, wu, wd, g, b, *, name):
    t, d = x.shape
    nf, _, tf = wg.shape
    tm = _tile(t, FFN_ROWS, SUBLANES)

    def body(x_ref, wg_ref, wu_ref, wd_ref, g_ref, b_ref, y_ref, z_ref, yb_ref, acc_ref):
        f = pl.program_id(1)
        xb = x_ref[...].astype(bf16)
        part, pending = None, None
        for cols in _lane_blocks(tf):
            gate_up = (_bdot(xb, wg_ref[:, cols], NN), _bdot(xb, wu_ref[:, cols], NN), cols)
            if pending is not None:
                down = _bdot(_silu(pending[0]) * pending[1], wd_ref[pending[2], :], NN)
                part = down if part is None else part + down
            pending = gate_up
        down = _bdot(_silu(pending[0]) * pending[1], wd_ref[pending[2], :], NN)
        part = down if part is None else part + down

        @pl.when(f == 0)
        def _():
            acc_ref[...] = part

        @pl.when(f > 0)
        def _():
            acc_ref[...] += part

        @pl.when(f == nf - 1)
        def _():
            z = DN_ALPHA * x_ref[...] + 0.5 * acc_ref[...]
            z_ref[...] = z
            y = _layer_norm(z, g_ref[...], b_ref[...])
            y_ref[...] = y
            yb_ref[...] = y.astype(bf16)

    row = pl.BlockSpec((tm, d), lambda i, j: (i, 0))
    vec = pl.BlockSpec((1, d), lambda i, j: (0, 0))
    wcol = pl.BlockSpec((None, d, tf), lambda i, j: (j, 0, 0))
    wrow = pl.BlockSpec((None, tf, d), lambda i, j: (j, 0, 0))
    return pl.pallas_call(
        body, grid=(t // tm, nf),
        in_specs=[row, wcol, wcol, wrow, vec, vec],
        out_specs=[row, row, row],
        out_shape=[jax.ShapeDtypeStruct((t, d), f32)] * 2 + [jax.ShapeDtypeStruct((t, d), bf16)],
        scratch_shapes=[pltpu.VMEM((tm, d), f32)],
        compiler_params=_params(("parallel", "arbitrary")), name=name,
    )(x, wg, wu, wd, g, b)


def ffn_bwd_weights(xb, dzb, wg, wu, wd, *, name):
    t, d = xb.shape
    nf, _, tf = wg.shape
    tm = _tile(t, FFN_ROWS, SUBLANES)
    nt = t // tm

    def body(x_ref, dz_ref, wg_ref, wu_ref, wd_ref, dgate_ref, dup_ref, owg_ref, owu_ref, owd_ref,
             dwg_ref, dwu_ref, dwd_ref):
        x = x_ref[...]
        dzh = dz_ref[...] * 0.5

        def first_half(cols):
            return _bdot(x, wg_ref[:, cols], NN), _bdot(x, wu_ref[:, cols], NN), _bdot(dzh, wd_ref[cols, :], NT), cols

        def second_half(gate, up, dh, cols):
            sg = jax.nn.sigmoid(gate)
            s = gate * sg
            dup = (dh * s).astype(bf16)
            dgate = (dh * up * (sg * (1.0 + gate * (1.0 - sg)))).astype(bf16)
            dgate_ref[:, cols] = dgate
            dup_ref[:, cols] = dup
            return _bdot(x, dgate, TN), _bdot(x, dup, TN), _bdot(s * up, dzh, TN), cols

        parts, pending = [], None
        for cols in _lane_blocks(tf):
            nxt = first_half(cols)
            if pending is not None:
                parts.append(second_half(*pending))
            pending = nxt
        parts.append(second_half(*pending))

        @pl.when(pl.program_id(1) == 0)
        def _():
            for pwg, pwu, pwd, cols in parts:
                dwg_ref[:, cols] = pwg
                dwu_ref[:, cols] = pwu
                dwd_ref[cols, :] = pwd

        @pl.when(pl.program_id(1) > 0)
        def _():
            for pwg, pwu, pwd, cols in parts:
                dwg_ref[:, cols] += pwg
                dwu_ref[:, cols] += pwu
                dwd_ref[cols, :] += pwd

        @pl.when(pl.program_id(1) == nt - 1)
        def _():
            owg_ref[...] = dwg_ref[...].astype(bf16)
            owu_ref[...] = dwu_ref[...].astype(bf16)
            owd_ref[...] = dwd_ref[...].astype(bf16)

    row = pl.BlockSpec((tm, d), lambda j, i: (i, 0))
    wcol = pl.BlockSpec((None, d, tf), lambda j, i: (j, 0, 0))
    wrow = pl.BlockSpec((None, tf, d), lambda j, i: (j, 0, 0))
    act = pl.BlockSpec((None, tm, tf), lambda j, i: (j, i, 0))
    return pl.pallas_call(
        body, grid=(nf, nt), in_specs=[row, row, wcol, wcol, wrow], out_specs=[act, act, wcol, wcol, wrow],
        out_shape=[jax.ShapeDtypeStruct((nf, t, tf), bf16), jax.ShapeDtypeStruct((nf, t, tf), bf16),
                   jax.ShapeDtypeStruct((nf, d, tf), bf16), jax.ShapeDtypeStruct((nf, d, tf), bf16),
                   jax.ShapeDtypeStruct((nf, tf, d), bf16)],
        scratch_shapes=[pltpu.VMEM((d, tf), f32), pltpu.VMEM((d, tf), f32), pltpu.VMEM((tf, d), f32)],
        compiler_params=_params(("parallel", "arbitrary")), name=name,
    )(xb, dzb, wg, wu, wd)


def ffn_bwd_input(dgate, dup, wg, wu, dz, *, name):
    nf, t, tf = dgate.shape
    d = wg.shape[1]
    tm = _tile(t, FFN_ROWS // 2, SUBLANES)

    def body(dg_ref, du_ref, wg_ref, wu_ref, dz_ref, dx_ref):
        acc = DN_ALPHA * dz_ref[...]
        for j in range(nf):
            acc = acc + _bdot(dg_ref[j], wg_ref[j], NT) + _bdot(du_ref[j], wu_ref[j], NT)
        dx_ref[...] = acc

    act = pl.BlockSpec((nf, tm, tf), lambda i: (0, i, 0))
    wsp = pl.BlockSpec((nf, d, tf), lambda i: (0, 0, 0))
    row = pl.BlockSpec((tm, d), lambda i: (i, 0))
    return pl.pallas_call(
        body, grid=(t // tm,), in_specs=[act, act, wsp, wsp, row], out_specs=row,
        out_shape=jax.ShapeDtypeStruct((t, d), f32),
        compiler_params=_params(("parallel",)), name=name,
    )(dgate, dup, wg, wu, dz)


def ple_fwd(x, p, wg, bg, wp, *, name):
    t, d = x.shape
    dp = p.shape[1]
    tm = _tile(t, 512, 2 * SUBLANES)

    def body(x_ref, p_ref, wg_ref, bg_ref, wp_ref, o_ref, ob_ref):
        x_ = x_ref[...]
        gate = jax.nn.sigmoid(_bdot(x_, wg_ref[...], NN) + bg_ref[...])
        out = x_ + gate * _bdot(p_ref[...], wp_ref[...], NN)
        o_ref[...] = out
        ob_ref[...] = out.astype(bf16)

    row = pl.BlockSpec((tm, d), lambda i: (i, 0))
    return pl.pallas_call(
        body, grid=(t // tm,),
        in_specs=[row, pl.BlockSpec((tm, dp), lambda i: (i, 0)), pl.BlockSpec((d, d), lambda i: (0, 0)),
                  pl.BlockSpec((1, d), lambda i: (0, 0)), pl.BlockSpec((dp, d), lambda i: (0, 0))],
        out_specs=[row, row], out_shape=[jax.ShapeDtypeStruct((t, d), f32), jax.ShapeDtypeStruct((t, d), bf16)],
        compiler_params=_params(("parallel",)), name=name,
    )(x, p, wg, bg, wp)


def ple_bwd(x, p, dy, wg, wgt, bg, wp, *, name):
    t, d = x.shape
    dp = p.shape[1]
    tm = _tile(t, 512, SUBLANES)

    def body(x_ref, p_ref, dy_ref, wg_ref, wgt_ref, bg_ref, wp_ref, dx_ref, dwg_ref, dbg_ref, dwp_ref):
        x_ = x_ref[...]
        dy_ = dy_ref[...]
        s = jax.nn.sigmoid(_bdot(x_, wg_ref[...], NN) + bg_ref[...])
        e = _bdot(p_ref[...], wp_ref[...], NN)
        da = dy_ * e * s * (1.0 - s)
        de = dy_ * s
        dx_ref[...] = dy_ + _bdot(da, wgt_ref[...], NN)
        pwg = _bdot(x_, da, TN)
        pbg = jnp.sum(da, 0, keepdims=True)
        pwp = _bdot(p_ref[...], de, TN)

        @pl.when(pl.program_id(0) == 0)
        def _():
            dwg_ref[...] = pwg
            dbg_ref[...] = pbg
            dwp_ref[...] = pwp

        @pl.when(pl.program_id(0) > 0)
        def _():
            dwg_ref[...] += pwg
            dbg_ref[...] += pbg
            dwp_ref[...] += pwp

    row = pl.BlockSpec((tm, d), lambda i: (i, 0))
    full = lambda shape: pl.BlockSpec(shape, lambda i: (0, 0))
    return pl.pallas_call(
        body, grid=(t // tm,),
        in_specs=[row, pl.BlockSpec((tm, dp), lambda i: (i, 0)), row, full((d, d)), full((d, d)), full((1, d)),
                  full((dp, d))],
        out_specs=[row, full((d, d)), full((1, d)), full((dp, d))],
        out_shape=[jax.ShapeDtypeStruct((t, d), f32), jax.ShapeDtypeStruct((d, d), f32),
                   jax.ShapeDtypeStruct((1, d), f32), jax.ShapeDtypeStruct((dp, d), f32)],
        compiler_params=_params(("arbitrary",)), name=name,
    )(x, p, dy, wg, wgt, bg, wp)


def _conv_taps(xpad_ref, w_ref, s):
    acc = w_ref[0:1, :] * xpad_ref[SUBLANES - 3:SUBLANES - 3 + s, :]
    for j in range(1, 4):
        acc = acc + w_ref[j:j + 1, :] * xpad_ref[SUBLANES - 3 + j:SUBLANES - 3 + j + s, :]
    return acc


def conv_fwd(x, w, bias, act, nb, *, name):
    t, c = x.shape
    s = t // nb
    cw = GROUP_W

    def body(x_ref, w_ref, b_ref, y_ref, xpad):
        xpad[0:SUBLANES, :] = jnp.zeros((SUBLANES, cw), f32)
        xpad[SUBLANES:, :] = x_ref[...]
        acc = _conv_taps(xpad, w_ref, s) + b_ref[...]
        y_ref[...] = _silu(acc) if act else acc

    slab = pl.BlockSpec((s, cw), lambda b, g: (b, g))
    return pl.pallas_call(
        body, grid=(nb, c // cw),
        in_specs=[slab, pl.BlockSpec((4, cw), lambda b, g: (0, g)), pl.BlockSpec((1, cw), lambda b, g: (0, g))],
        out_specs=slab, out_shape=jax.ShapeDtypeStruct((t, c), f32),
        scratch_shapes=[pltpu.VMEM((s + SUBLANES, cw), f32)],
        compiler_params=_params(("parallel", "parallel")), name=name,
    )(x, w, bias)


def conv_bwd(x, w, bias, dy, act, nb, *, name):
    t, c = x.shape
    s = t // nb
    cw = GROUP_W

    def body(x_ref, w_ref, b_ref, dy_ref, dx_ref, dw_ref, db_ref, xpad, dpad):
        xpad[0:SUBLANES, :] = jnp.zeros((SUBLANES, cw), f32)
        xpad[SUBLANES:, :] = x_ref[...]
        dacc = dy_ref[...]
        if act:
            acc = _conv_taps(xpad, w_ref, s) + b_ref[...]
            sg = jax.nn.sigmoid(acc)
            dacc = dacc * (sg * (1.0 + acc * (1.0 - sg)))
        dpad[0:s, :] = dacc
        dpad[s:, :] = jnp.zeros((SUBLANES, cw), f32)
        dx = w_ref[0:1, :] * dpad[3:3 + s, :]
        for j in range(1, 4):
            dx = dx + w_ref[j:j + 1, :] * dpad[3 - j:3 - j + s, :]
        dx_ref[...] = dx
        first = pl.program_id(1) == 0
        for j in range(4):
            pw = jnp.sum(dacc * xpad[SUBLANES - 3 + j:SUBLANES - 3 + j + s, :], 0, keepdims=True)

            @pl.when(first)
            def _():
                dw_ref[j:j + 1, :] = pw

            @pl.when(jnp.logical_not(first))
            def _():
                dw_ref[j:j + 1, :] += pw

        pb = jnp.sum(dacc, 0, keepdims=True)

        @pl.when(first)
        def _():
            db_ref[...] = pb

        @pl.when(jnp.logical_not(first))
        def _():
            db_ref[...] += pb

    slab = pl.BlockSpec((s, cw), lambda g, b: (b, g))
    wsp = pl.BlockSpec((4, cw), lambda g, b: (0, g))
    bsp = pl.BlockSpec((1, cw), lambda g, b: (0, g))
    return pl.pallas_call(
        body, grid=(c // cw, nb), in_specs=[slab, wsp, bsp, slab], out_specs=[slab, wsp, bsp],
        out_shape=[jax.ShapeDtypeStruct((t, c), f32), jax.ShapeDtypeStruct((4, c), f32),
                   jax.ShapeDtypeStruct((1, c), f32)],
        scratch_shapes=[pltpu.VMEM((s + SUBLANES, cw), f32), pltpu.VMEM((s + SUBLANES, cw), f32)],
        compiler_params=_params(("parallel", "arbitrary")), name=name,
    )(x, w, bias, dy)


def _each(f, *lists):
    return [f(*a) for a in zip(*lists)]


def _attn_heads(qs, kbs, vbs, sinks, valid, dist, dots):
    nn, nt = dots[:2]
    kv = [h // A_GROUP for h in range(A_HEADS)]
    scs = [nt(qs[h], kbs[kv[h]]) for h in range(A_HEADS)]
    prs = []
    for h in range(A_HEADS):
        sc = scs[h] * (A_HEAD_DIM ** -0.5) - 2.0 ** -(h + 1) * dist
        sc = jnp.where(valid, sc, NEG)
        m = lax.stop_gradient(jnp.maximum(jnp.max(sc, -1, keepdims=True), sinks[h]))
        pr = jnp.exp(sc - m)
        den = jnp.sum(pr, -1, keepdims=True) + jnp.exp(sinks[h] - m)
        prs.append(pr / den)
    return [nn(prs[h], vbs[kv[h]]) for h in range(A_HEADS)]


A_Q_ROWS = 2 * CHUNK


def _attn_band_consts(r0):
    band = A_WINDOW + A_Q_ROWS
    qi = lax.broadcasted_iota(jnp.int32, (A_Q_ROWS, band), 0)
    kj = lax.broadcasted_iota(jnp.int32, (A_Q_ROWS, band), 1)
    dist = jnp.abs(qi + A_WINDOW - kj).astype(f32)
    qc, kc = qi // CHUNK, kj // CHUNK
    valid = ((kj + r0) >= A_WINDOW) & (kc >= qc) & (kc <= qc + A_WINDOW // CHUNK)
    return dist, valid


def attn_fwd(qkv, sinks, nb, *, name):
    t = qkv.shape[0]
    s = t // nb
    band = A_WINDOW + A_Q_ROWS
    hd = A_HEAD_DIM

    def body(qkv_ref, sink_ref, o_ref, kvpad):
        kvpad[0:A_WINDOW, :] = jnp.zeros((A_WINDOW, 2 * A_KV_WIDTH), f32)
        kvpad[A_WINDOW:, :] = qkv_ref[:, A_WIDTH:]

        def chunk(n, carry):
            r0 = pl.multiple_of(n * A_Q_ROWS, A_Q_ROWS)
            dist, valid = _attn_band_consts(r0)
            kbs = [kvpad[pl.ds(r0, band), kvh * hd:(kvh + 1) * hd] for kvh in range(A_KV_HEADS)]
            vbs = [kvpad[pl.ds(r0, band), A_KV_WIDTH + kvh * hd:A_KV_WIDTH + (kvh + 1) * hd]
                   for kvh in range(A_KV_HEADS)]
            qs = [qkv_ref[pl.ds(r0, A_Q_ROWS), h * hd:(h + 1) * hd] for h in range(A_HEADS)]
            outs = _attn_heads(qs, kbs, vbs, [sink_ref[:, h:h + 1] for h in range(A_HEADS)], valid, dist, RAW_DOTS)
            for h in range(A_HEADS):
                o_ref[pl.ds(r0, A_Q_ROWS), h * hd:(h + 1) * hd] = outs[h]
            return carry

        lax.fori_loop(0, s // A_Q_ROWS, chunk, 0)

    return pl.pallas_call(
        body, grid=(nb,),
        in_specs=[pl.BlockSpec((s, A_WIDTH + 2 * A_KV_WIDTH), lambda b: (b, 0)),
                  pl.BlockSpec((1, A_HEADS), lambda b: (0, 0))],
        out_specs=pl.BlockSpec((s, A_WIDTH), lambda b: (b, 0)),
        out_shape=jax.ShapeDtypeStruct((t, A_WIDTH), f32),
        scratch_shapes=[pltpu.VMEM((s + A_WINDOW, 2 * A_KV_WIDTH), f32)],
        compiler_params=_params(("parallel",)), name=name,
    )(qkv, sinks)


def attn_bwd(qkv, sinks, do, nb, *, name):
    t = qkv.shape[0]
    s = t // nb
    band = A_WINDOW + A_Q_ROWS
    hd = A_HEAD_DIM
    kvw = 2 * A_KV_WIDTH

    def body(qkv_ref, sink_ref, do_ref, dqkv_ref, dsink_ref, kvpad, dkvpad):
        kvpad[0:A_WINDOW, :] = jnp.zeros((A_WINDOW, kvw), f32)
        kvpad[A_WINDOW:, :] = qkv_ref[:, A_WIDTH:]
        dkvpad[...] = jnp.zeros((s + A_WINDOW, kvw), f32)

        def chunk(n, dsinks):
            r0 = pl.multiple_of(n * A_Q_ROWS, A_Q_ROWS)
            dist, valid = _attn_band_consts(r0)
            ksl = [slice(kvh * hd, (kvh + 1) * hd) for kvh in range(A_KV_HEADS)]
            vsl = [slice(A_KV_WIDTH + kvh * hd, A_KV_WIDTH + (kvh + 1) * hd) for kvh in range(A_KV_HEADS)]
            kbs = [kvpad[pl.ds(r0, band), sl] for sl in ksl]
            vbs = [kvpad[pl.ds(r0, band), sl] for sl in vsl]
            dkbs = [dkvpad[pl.ds(r0, band), sl] for sl in ksl]
            dvbs = [dkvpad[pl.ds(r0, band), sl] for sl in vsl]
            qs = [qkv_ref[pl.ds(r0, A_Q_ROWS), h * hd:(h + 1) * hd] for h in range(A_HEADS)]
            dos = [do_ref[pl.ds(r0, A_Q_ROWS), h * hd:(h + 1) * hd] for h in range(A_HEADS)]
            fn = functools.partial(_attn_heads, valid=valid, dist=dist, dots=VJP_DOTS)
            _, vjp = jax.vjp(fn, qs, kbs, vbs, [sink_ref[:, h:h + 1] for h in range(A_HEADS)])
            dqs, dks, dvs, dss = vjp(dos)
            for h in range(A_HEADS):
                dqkv_ref[pl.ds(r0, A_Q_ROWS), h * hd:(h + 1) * hd] = dqs[h]
            for kvh in range(A_KV_HEADS):
                dkvpad[pl.ds(r0, band), ksl[kvh]] = dkbs[kvh] + dks[kvh]
                dkvpad[pl.ds(r0, band), vsl[kvh]] = dvbs[kvh] + dvs[kvh]
            return tuple(dsinks[h] + dss[h] for h in range(A_HEADS))

        dsinks = lax.fori_loop(0, s // A_Q_ROWS, chunk, tuple(jnp.zeros((1, 1), f32) for _ in range(A_HEADS)))
        dqkv_ref[:, A_WIDTH:] = dkvpad[A_WINDOW:, :]
        first = pl.program_id(0) == 0
        for h in range(A_HEADS):
            @pl.when(first)
            def _():
                dsink_ref[:, h:h + 1] = dsinks[h]

            @pl.when(jnp.logical_not(first))
            def _():
                dsink_ref[:, h:h + 1] += dsinks[h]

    wq = A_WIDTH + kvw
    return pl.pallas_call(
        body, grid=(nb,),
        in_specs=[pl.BlockSpec((s, wq), lambda b: (b, 0)), pl.BlockSpec((1, A_HEADS), lambda b: (0, 0)),
                  pl.BlockSpec((s, A_WIDTH), lambda b: (b, 0))],
        out_specs=[pl.BlockSpec((s, wq), lambda b: (b, 0)), pl.BlockSpec((1, A_HEADS), lambda b: (0, 0))],
        out_shape=[jax.ShapeDtypeStruct((t, wq), f32), jax.ShapeDtypeStruct((1, A_HEADS), f32)],
        scratch_shapes=[pltpu.VMEM((s + A_WINDOW, kvw), f32), pltpu.VMEM((s + A_WINDOW, kvw), f32)],
        compiler_params=_params(("arbitrary",)), name=name,
    )(qkv, sinks, do)


def _rg_gates(xc, wa, wx, ba, bx, lam, nn):
    r = jax.nn.sigmoid(nn(xc, wa) + ba)
    i = jax.nn.sigmoid(nn(xc, wx) + bx)
    log_a = -RG_C * r * jax.nn.softplus(-lam)
    a = jnp.exp(log_a)
    mult = jnp.sqrt(-jnp.tanh(log_a) * (jnp.exp(2.0 * log_a) + 1.0))
    return a, mult * (i * xc)


def _linear_scan(a, u, reverse):
    s = a.shape[0]
    t = lax.broadcasted_iota(jnp.int32, a.shape, 0)
    d = 1
    while d < s:
        if reverse:
            keep = t < s - d
            shift = s - d
        else:
            keep = t >= d
            shift = d
        us = jnp.where(keep, pltpu.roll(u, shift, 0), 0.0)
        as_ = jnp.where(keep, pltpu.roll(a, shift, 0), 1.0)
        u = u + a * us
        a = a * as_
        d *= 2
    return u


def rglru_fwd(xc, bg, wa, wx, ba, bx, lam, nb, *, name):
    t, c = xc.shape
    s = t // nb
    cw = GROUP_W

    def body(xc_ref, bg_ref, wa_ref, wx_ref, ba_ref, bx_ref, lam_ref, y_ref, h_ref):
        a, u = _rg_gates(xc_ref[...], wa_ref[...], wx_ref[...], ba_ref[...], bx_ref[...], lam_ref[...], RAW_DOTS[0])
        h = _linear_scan(a, u, False)
        h_ref[...] = h
        y_ref[...] = h * jax.nn.gelu(bg_ref[...])

    slab = pl.BlockSpec((s, cw), lambda b, g: (b, g))
    wsp = pl.BlockSpec((None, cw, cw), lambda b, g: (g, 0, 0))
    vec = pl.BlockSpec((1, cw), lambda b, g: (0, g))
    return pl.pallas_call(
        body, grid=(nb, c // cw), in_specs=[slab, slab, wsp, wsp, vec, vec, vec], out_specs=[slab, slab],
        out_shape=[jax.ShapeDtypeStruct((t, c), f32)] * 2,
        compiler_params=_params(("parallel", "parallel")), name=name,
    )(xc, bg, wa, wx, ba, bx, lam)


def rglru_bwd(xc, bg, h, dy, wa, wx, ba, bx, lam, nb, *, name):
    t, c = xc.shape
    s = t // nb
    cw = GROUP_W

    def body(xc_ref, bg_ref, h_ref, dy_ref, wa_ref, wx_ref, ba_ref, bx_ref, lam_ref,
             dxc_ref, dbg_ref, dwa_ref, dwx_ref, dba_ref, dbx_ref, dlam_ref):
        h = h_ref[...]
        dy_ = dy_ref[...]
        gel, gel_vjp = jax.vjp(jax.nn.gelu, bg_ref[...])
        dbg_ref[...] = gel_vjp(dy_ * h)[0]
        dh = dy_ * gel
        gates = functools.partial(_rg_gates, nn=_bnn)
        (a, _), gates_vjp = jax.vjp(gates, xc_ref[...], wa_ref[...], wx_ref[...], ba_ref[...], bx_ref[...],
                                    lam_ref[...])
        ti = lax.broadcasted_iota(jnp.int32, a.shape, 0)
        a_next = jnp.where(ti < s - 1, pltpu.roll(a, s - 1, 0), 0.0)
        lam_t = _linear_scan(a_next, dh, True)
        h_prev = jnp.where(ti >= 1, pltpu.roll(h, 1, 0), 0.0)
        dxc, dwa, dwx, dba, dbx, dlam = gates_vjp((lam_t * h_prev, lam_t))
        dxc_ref[...] = dxc
        first = pl.program_id(1) == 0

        @pl.when(first)
        def _():
            dwa_ref[...] = dwa
            dwx_ref[...] = dwx
            dba_ref[...] = dba
            dbx_ref[...] = dbx
            dlam_ref[...] = dlam

        @pl.when(jnp.logical_not(first))
        def _():
            dwa_ref[...] += dwa
            dwx_ref[...] += dwx
            dba_ref[...] += dba
            dbx_ref[...] += dbx
            dlam_ref[...] += dlam

    slab = pl.BlockSpec((s, cw), lambda g, b: (b, g))
    wsp = pl.BlockSpec((None, cw, cw), lambda g, b: (g, 0, 0))
    vec = pl.BlockSpec((1, cw), lambda g, b: (0, g))
    ng = c // cw
    return pl.pallas_call(
        body, grid=(ng, nb), in_specs=[slab, slab, slab, slab, wsp, wsp, vec, vec, vec],
        out_specs=[slab, slab, wsp, wsp, vec, vec, vec],
        out_shape=[jax.ShapeDtypeStruct((t, c), f32), jax.ShapeDtypeStruct((t, c), f32),
                   jax.ShapeDtypeStruct((ng, cw, cw), f32), jax.ShapeDtypeStruct((ng, cw, cw), f32),
                   jax.ShapeDtypeStruct((1, c), f32), jax.ShapeDtypeStruct((1, c), f32),
                   jax.ShapeDtypeStruct((1, c), f32)],
        compiler_params=_params(("parallel", "arbitrary")), name=name,
    )(xc, bg, h, dy, wa, wx, ba, bx, lam)


def _gdn_chunks_prep(qs, ks, vs, bls, als, a_log, dt_b, dots):
    nn, nt, csum = dots[0], dots[1], dots[3]
    hd = C_HEAD_DIM
    ri = lax.broadcasted_iota(jnp.int32, (CHUNK, CHUNK), 0)
    ci = lax.broadcasted_iota(jnp.int32, (CHUNK, CHUNK), 1)
    tril = ri >= ci
    strict = ri > ci
    eye = (ri == ci).astype(f32)
    qn = [q * lax.rsqrt(jnp.sum(q * q, -1, keepdims=True) + NORM_EPS) * (hd ** -0.5) for q in qs]
    kn = [k * lax.rsqrt(jnp.sum(k * k, -1, keepdims=True) + NORM_EPS) for k in ks]
    beta = [jax.nn.sigmoid(bl) for bl in bls]
    g = [-jnp.exp(a_log) * jax.nn.softplus(al + dt_b) for al in als]
    gc_sq = [csum(jnp.broadcast_to(g_, (CHUNK, CHUNK))) for g_ in g]
    gc = [csum(jnp.broadcast_to(g_, (CHUNK, hd))) for g_ in g]
    decay = [jnp.where(tril, jnp.exp(jnp.where(tril, s - s.T, 0.0)), 0.0) for s in gc_sq]
    kb = _each(jnp.multiply, kn, beta)
    kk = _each(nt, kb, kn)
    pw = [-jnp.where(strict, a * d, 0.0) for a, d in zip(kk, decay)]
    inv = [eye + p_ for p_ in pw]
    for _ in range(5):
        pw = _each(nn, pw, pw)
        inv = _each(jnp.add, inv, _each(nn, inv, pw))
    egc = [jnp.exp(c_) for c_ in gc]
    u = _each(nn, inv, _each(jnp.multiply, vs, beta))
    w = _each(nn, inv, _each(jnp.multiply, kb, egc))
    attn = _each(jnp.multiply, _each(nt, qn, kn), decay)
    g_last = [jnp.sum(jnp.broadcast_to(g_, (CHUNK, hd)), 0, keepdims=True) for g_ in g]
    qg = _each(jnp.multiply, qn, egc)
    kdec = [k_ * jnp.exp(gl_ - c_) for k_, gl_, c_ in zip(kn, g_last, gc)]
    return [(qg[i], kdec[i], w[i], u[i], attn[i], jnp.exp(g_last[i])) for i in range(len(qs))]


def _gdn_heads_step(states, qgs, kdecs, ws, us, attns, gls, zs, ng, dots):
    nn, tn = dots[0], dots[2]
    v_new = _each(jnp.subtract, us, _each(nn, ws, states))
    o = _each(jnp.add, _each(nn, qgs, states), _each(nn, attns, v_new))
    new = [s * gl for s, gl in zip(states, gls)]
    new = _each(jnp.add, new, _each(tn, kdecs, v_new))
    y = [o_ * lax.rsqrt(jnp.mean(o_ * o_, -1, keepdims=True) + NORM_EPS) * ng * _silu(z) for o_, z in zip(o, zs)]
    return y, new


def _loop_unrolled(n, unroll, load, compute, store, init):
    u = unroll if n % unroll == 0 else 1

    def trip(i, carry):
        idx = [i * u + j for j in range(u)]
        loaded = [load(k) for k in idx]
        results = compute(loaded)
        for k, r in zip(idx, results):
            carry = store(k, r, carry)
        return carry

    return lax.fori_loop(0, n // u, trip, init)


def _pick_lane(x, lane):
    li = lax.broadcasted_iota(jnp.int32, x.shape, 1)
    return jnp.sum(jnp.where(li == lane, x, 0.0), 1, keepdims=True)


def _put_lane(col, lane, width):
    li = lax.broadcasted_iota(jnp.int32, (col.shape[0], width), 1)
    return jnp.where(li == lane, col, 0.0)


def _gdn_specs(s, nc):
    hd = C_HEAD_DIM
    head = lambda off: pl.BlockSpec((s, hd), lambda b, h, off=off: (b, off + h))
    attn = pl.BlockSpec((None, s, CHUNK), lambda b, h: (h, b, 0))
    gl = pl.BlockSpec((None, nc * SUBLANES, hd), lambda b, h: (h, b, 0))
    ba = pl.BlockSpec((s, LANES), lambda b, h: (b, 0))
    sc8 = pl.BlockSpec((1, C_HEADS), lambda b, h: (0, 0))
    return head, attn, gl, ba, sc8


def gdn_prep_fwd(qkv, ba, a_log, dt_b, nb, *, name):
    t = qkv.shape[0]
    s = t // nb
    nc = s // CHUNK
    hd = C_HEAD_DIM
    head, attn_sp, gl_sp, ba_sp, sc8 = _gdn_specs(s, nc)

    def body(q_ref, k_ref, v_ref, ba_ref, alog_ref, dtb_ref, qg_ref, kd_ref, w_ref, u_ref, at_ref, gl_ref):
        h = pl.program_id(1)
        a_log_h = _pick_lane(alog_ref[...], h)
        dt_b_h = _pick_lane(dtb_ref[...], h)

        def load(n):
            rows = pl.ds(pl.multiple_of(n * CHUNK, CHUNK), CHUNK)
            bav = ba_ref[rows, :]
            return q_ref[rows, :], k_ref[rows, :], v_ref[rows, :], _pick_lane(bav, h), _pick_lane(bav, C_HEADS + h)

        def compute(loaded):
            return _gdn_chunks_prep(*[list(x) for x in zip(*loaded)], a_log_h, dt_b_h, RAW_DOTS)

        def store(n, outs, carry):
            rows = pl.ds(pl.multiple_of(n * CHUNK, CHUNK), CHUNK)
            qg_ref[rows, :] = outs[0].astype(bf16)
            kd_ref[rows, :] = outs[1].astype(bf16)
            w_ref[rows, :] = outs[2].astype(bf16)
            u_ref[rows, :] = outs[3]
            at_ref[rows, :] = outs[4].astype(bf16)
            gl_ref[pl.ds(pl.multiple_of(n * SUBLANES, SUBLANES), SUBLANES), :] = jnp.broadcast_to(outs[5], (SUBLANES, hd))
            return carry

        _loop_unrolled(nc, PREP_FWD_UNROLL, load, compute, store, 0)

    big = jax.ShapeDtypeStruct((t, C_WIDTH), f32)
    bigb = jax.ShapeDtypeStruct((t, C_WIDTH), bf16)
    return pl.pallas_call(
        body, grid=(nb, C_HEADS),
        in_specs=[head(0), head(C_HEADS), head(2 * C_HEADS), ba_sp, sc8, sc8],
        out_specs=[head(0)] * 4 + [attn_sp, gl_sp],
        out_shape=[bigb, bigb, bigb, big, jax.ShapeDtypeStruct((C_HEADS, t, CHUNK), bf16),
                               jax.ShapeDtypeStruct((C_HEADS, nb * nc * SUBLANES, hd), f32)],
        compiler_params=_params(("parallel", "parallel")), name=name,
    )(qkv, qkv, qkv, ba, a_log, dt_b)


def gdn_prep_bwd(qkv, ba, a_log, dt_b, cts, nb, *, name):
    t = qkv.shape[0]
    s = t // nb
    nc = s // CHUNK
    hd = C_HEAD_DIM
    head, attn_sp, gl_sp, ba_sp, sc8 = _gdn_specs(s, nc)

    def body(q_ref, k_ref, v_ref, ba_ref, alog_ref, dtb_ref, cqg, ckd, cw_, cu, cat, cgl,
             dq_ref, dk_ref, dv_ref, dba_ref, dalog_ref, ddtb_ref):
        b = pl.program_id(0)
        h = pl.program_id(1)
        a_log_h = _pick_lane(alog_ref[...], h)
        dt_b_h = _pick_lane(dtb_ref[...], h)
        prep = functools.partial(_gdn_chunks_prep, dots=VJP_DOTS)

        @pl.when(h == 0)
        def _():
            dba_ref[...] = jnp.zeros((s, LANES), f32)

        def load(n):
            rows = pl.ds(pl.multiple_of(n * CHUNK, CHUNK), CHUNK)
            bav = ba_ref[rows, :]
            cgl_n = cgl[pl.ds(pl.multiple_of(n * SUBLANES, SUBLANES), SUBLANES), :][0:1, :]
            primals = (q_ref[rows, :], k_ref[rows, :], v_ref[rows, :], _pick_lane(bav, h), _pick_lane(bav, C_HEADS + h))
            return primals, (cqg[rows, :], ckd[rows, :], cw_[rows, :], cu[rows, :], cat[rows, :], cgl_n), dba_ref[rows, :]

        def compute(loaded):
            primals = [list(x) for x in zip(*[item[0] for item in loaded])]
            _, vjp = jax.vjp(prep, *primals, a_log_h, dt_b_h)
            dqs, dks, dvs, dbls, dals, dalog, ddtb = vjp([item[1] for item in loaded])
            zero = jnp.zeros((1, 1), f32)
            return [((dqs[i], dks[i], dvs[i], dbls[i], dals[i], dalog if i == 0 else zero, ddtb if i == 0 else zero),
                     loaded[i][2]) for i in range(len(loaded))]

        def store(n, res, carry):
            (dq, dk, dv, dbl, dal, dalog_n, ddtb_n), dba_old = res
            rows = pl.ds(pl.multiple_of(n * CHUNK, CHUNK), CHUNK)
            dq_ref[rows, :] = dq
            dk_ref[rows, :] = dk
            dv_ref[rows, :] = dv
            dba_ref[rows, :] = dba_old + _put_lane(dbl, h, LANES) + _put_lane(dal, C_HEADS + h, LANES)
            return carry[0] + dalog_n, carry[1] + ddtb_n

        da_log, ddt_b = _loop_unrolled(nc, PREP_BWD_UNROLL, load, compute, store,
                                       (jnp.zeros((1, 1), f32), jnp.zeros((1, 1), f32)))
        first = jnp.logical_and(b == 0, h == 0)

        @pl.when(first)
        def _():
            dalog_ref[...] = _put_lane(da_log, h, LANES)
            ddtb_ref[...] = _put_lane(ddt_b, h, LANES)

        @pl.when(jnp.logical_not(first))
        def _():
            dalog_ref[...] += _put_lane(da_log, h, LANES)
            ddtb_ref[...] += _put_lane(ddt_b, h, LANES)

    big = jax.ShapeDtypeStruct((t, C_WIDTH), f32)
    vec = pl.BlockSpec((1, LANES), lambda b, h: (0, 0))
    return pl.pallas_call(
        body, grid=(nb, C_HEADS),
        in_specs=[head(0), head(C_HEADS), head(2 * C_HEADS), ba_sp, sc8, sc8] + [head(0)] * 4 + [attn_sp, gl_sp],
        out_specs=[head(0)] * 3 + [ba_sp, vec, vec],
        out_shape=[big] * 3 + [jax.ShapeDtypeStruct((t, LANES), f32), jax.ShapeDtypeStruct((1, LANES), f32),
                               jax.ShapeDtypeStruct((1, LANES), f32)],
        compiler_params=_params(("arbitrary", "arbitrary")), name=name,
    )(qkv, qkv, qkv, ba, a_log, dt_b, *cts)


def _gdn_rec_specs(sb, nsb, hp, reverse):
    hd = C_HEAD_DIM
    ncb = sb // CHUNK
    blk = (lambda b, k: b * nsb + (nsb - 1 - k)) if reverse else (lambda b, k: b * nsb + k)
    wide = pl.BlockSpec((sb, hp * hd), lambda b, j, k: (blk(b, k), j))
    attn = pl.BlockSpec((hp, sb, CHUNK), lambda b, j, k: (j, blk(b, k), 0))
    gl = pl.BlockSpec((hp, ncb * SUBLANES, hd), lambda b, j, k: (j, blk(b, k), 0))
    ng = pl.BlockSpec((1, hd), lambda b, j, k: (0, 0))
    states = pl.BlockSpec((hp, ncb, hd, hd), lambda b, j, k: (j, blk(b, k), 0, 0))
    return wide, attn, gl, ng, states


def gdn_rec_fwd(qg, kdec, w, u, attn, gl, z, ng, nb, *, name):
    t = qg.shape[0]
    s = t // nb
    sb = min(s, GDN_TIME_BLOCK)
    nsb = s // sb
    hd = C_HEAD_DIM
    hp = C_HEADS_PER_STEP
    wide, attn_sp, gl_sp, ng_sp, st_sp = _gdn_rec_specs(sb, nsb, hp, False)

    def body(qg_ref, kd_ref, w_ref, u_ref, at_ref, gl_ref, z_ref, ng_ref, y_ref, st_ref, carry_ref):
        @pl.when(pl.program_id(2) == 0)
        def _():
            carry_ref[...] = jnp.zeros((hp, hd, hd), f32)

        def chunk(n, states):
            for j in range(hp):
                st_ref[j, n] = states[j]
            rows = pl.ds(pl.multiple_of(n * CHUNK, CHUNK), CHUNK)
            grow = pl.ds(pl.multiple_of(n * SUBLANES, SUBLANES), SUBLANES)
            cols = [slice(j * hd, (j + 1) * hd) for j in range(hp)]
            ins = [(qg_ref[rows, c], kd_ref[rows, c], w_ref[rows, c], u_ref[rows, c], at_ref[j, rows, :],
                    gl_ref[j, grow, :][0:1, :], z_ref[rows, c]) for j, c in enumerate(cols)]
            ys, new = _gdn_heads_step(list(states), *[list(x) for x in zip(*ins)], ng_ref[...], RAW_DOTS)
            for j in range(hp):
                y_ref[rows, cols[j]] = ys[j]
            return tuple(new)

        last = lax.fori_loop(0, sb // CHUNK, chunk, tuple(carry_ref[j] for j in range(hp)))
        for j in range(hp):
            carry_ref[j] = last[j]

    return pl.pallas_call(
        body, grid=(nb, C_HEADS // hp, nsb),
        in_specs=[wide] * 4 + [attn_sp, gl_sp, wide, ng_sp], out_specs=[wide, st_sp],
        out_shape=[jax.ShapeDtypeStruct((t, C_WIDTH), f32), jax.ShapeDtypeStruct((C_HEADS, t // CHUNK, hd, hd), f32)],
        scratch_shapes=[pltpu.VMEM((hp, hd, hd), f32)],
        compiler_params=_params(("parallel", "parallel", "arbitrary")), name=name,
    )(qg, kdec, w, u, attn, gl, z, ng)


def gdn_rec_bwd(qg, kdec, w, u, attn, gl, z, ng, states, dy, nb, *, name):
    t = qg.shape[0]
    s = t // nb
    sb = min(s, GDN_TIME_BLOCK)
    nsb = s // sb
    nc = sb // CHUNK
    hd = C_HEAD_DIM
    hp = C_HEADS_PER_STEP
    wide, attn_sp, gl_sp, ng_sp, st_sp = _gdn_rec_specs(sb, nsb, hp, True)

    def body(qg_ref, kd_ref, w_ref, u_ref, at_ref, gl_ref, z_ref, ng_ref, states, dy_ref,
             dqg_ref, dkd_ref, dw_ref, du_ref, dat_ref, dgl_ref, dz_ref, dng_ref, carry_ref):
        step = functools.partial(_gdn_heads_step, dots=VJP_DOTS)

        @pl.when(pl.program_id(2) == 0)
        def _():
            carry_ref[...] = jnp.zeros((hp, hd, hd), f32)

        def operands(n):
            rows = pl.ds(pl.multiple_of(n * CHUNK, CHUNK), CHUNK)
            grow = pl.ds(pl.multiple_of(n * SUBLANES, SUBLANES), SUBLANES)
            cols = [slice(j * hd, (j + 1) * hd) for j in range(hp)]
            return ([qg_ref[rows, c].astype(f32) for c in cols], [kd_ref[rows, c].astype(f32) for c in cols],
                    [w_ref[rows, c].astype(f32) for c in cols], [u_ref[rows, c] for c in cols],
                    [at_ref[j, rows, :].astype(f32) for j in range(hp)],
                    [gl_ref[j, grow, :][0:1, :] for j in range(hp)], [z_ref[rows, c] for c in cols])

        def bwd_chunk(i, carry):
            n = nc - 1 - i
            rows = pl.ds(pl.multiple_of(n * CHUNK, CHUNK), CHUNK)
            grow = pl.ds(pl.multiple_of(n * SUBLANES, SUBLANES), SUBLANES)
            dsts, dng = carry
            dys = [dy_ref[rows, j * hd:(j + 1) * hd] for j in range(hp)]
            _, vjp = jax.vjp(step, [states[j, n] for j in range(hp)], *operands(n), ng_ref[...])
            dst, dqg, dkd, dw, du, dat, dgl, dz, dng_n = vjp((dys, list(dsts)))
            for j in range(hp):
                cols = slice(j * hd, (j + 1) * hd)
                dqg_ref[rows, cols] = dqg[j]
                dkd_ref[rows, cols] = dkd[j]
                dw_ref[rows, cols] = dw[j]
                du_ref[rows, cols] = du[j]
                dat_ref[j, rows, :] = dat[j]
                dgl_ref[j, grow, :] = jnp.broadcast_to(dgl[j], (SUBLANES, hd))
                dz_ref[rows, cols] = dz[j]
            return tuple(dst), dng + dng_n

        dlast, dng = lax.fori_loop(0, nc, bwd_chunk,
                                   (tuple(carry_ref[j] for j in range(hp)), jnp.zeros((1, hd), f32)))
        for j in range(hp):
            carry_ref[j] = dlast[j]
        first = jnp.logical_and(jnp.logical_and(pl.program_id(0) == 0, pl.program_id(1) == 0), pl.program_id(2) == 0)

        @pl.when(first)
        def _():
            dng_ref[...] = dng

        @pl.when(jnp.logical_not(first))
        def _():
            dng_ref[...] += dng

    big = jax.ShapeDtypeStruct((t, C_WIDTH), f32)
    return pl.pallas_call(
        body, grid=(nb, C_HEADS // hp, nsb),
        in_specs=[wide] * 4 + [attn_sp, gl_sp, wide, ng_sp, st_sp, wide],
        out_specs=[wide] * 4 + [attn_sp, gl_sp, wide, ng_sp],
        out_shape=[big] * 4 + [jax.ShapeDtypeStruct(attn.shape, f32), jax.ShapeDtypeStruct(gl.shape, f32), big,
                               jax.ShapeDtypeStruct((1, hd), f32)],
        scratch_shapes=[pltpu.VMEM((hp, hd, hd), f32)],
        compiler_params=_params(("arbitrary", "arbitrary", "arbitrary")), name=name,
    )(qg, kdec, w, u, attn, gl, z, ng, states, dy)


def _blockdiag_slabs(w):
    per = GROUP_W // B_BLOCK
    slabs = jnp.zeros((B_BLOCKS // per, GROUP_W, GROUP_W), w.dtype)
    for h in range(B_BLOCKS):
        o = (h % per) * B_BLOCK
        slabs = slabs.at[h // per, o:o + B_BLOCK, o:o + B_BLOCK].set(w[h])
    return slabs


def _slab_blocks(slabs):
    per = GROUP_W // B_BLOCK
    return jnp.stack([slabs[h // per, (h % per) * B_BLOCK:(h % per + 1) * B_BLOCK,
                            (h % per) * B_BLOCK:(h % per + 1) * B_BLOCK] for h in range(B_BLOCKS)])


def _mixer_ab_fwd(x1, x1b, W, g, b, nb, tag):
    w_in = W["ab_w_in"][0].astype(bf16)
    o1, o2 = A_WIDTH + 2 * A_KV_WIDTH, A_WIDTH + 2 * A_KV_WIDTH + B_WIDTH
    w_qkv, w_bx, w_bg = w_in[:, :o1], w_in[:, o1:o2], w_in[:, o2:]
    pqkv = mm_nn(x1b,w_qkv, name=tag + "_in_qkv")
    pbx = mm_nn(x1b,w_bx, name=tag + "_in_bx")
    pbg = mm_nn(x1b,w_bg, name=tag + "_in_bg")
    ya = attn_fwd(pqkv, W["a_sinks"], nb, name=tag + "_attn_fwd")
    xc = conv_fwd(pbx, W["b_conv_w"][0], W["b_conv_b"], False, nb, name=tag + "_conv_fwd")
    wa_s, wx_s = _blockdiag_slabs(W["b_wa"][0]), _blockdiag_slabs(W["b_wx"][0])
    yb, hh = rglru_fwd(xc, pbg, wa_s, wx_s, W["b_ba"], W["b_bx"], W["b_lam"], nb, name=tag + "_rglru_fwd")
    w_out = W["ab_w_out"][0].astype(bf16)
    x2, z1, x2b = proj_ln([ya, yb], [w_out[:A_WIDTH], w_out[A_WIDTH:]], x1, g, b, name=tag + "_out_ln")
    saved = (pqkv, pbx, pbg, ya, xc, yb, hh, wa_s, wx_s, w_qkv, w_bx, w_bg, w_out)
    return x2, x2b, z1, saved


def _mixer_ab_bwd(x1b, dz1, dz1b, W, saved, nb, tag):
    pqkv, pbx, pbg, ya, xc, yb, hh, wa_s, wx_s, w_qkv, w_bx, w_bg, w_out = saved
    dya = mm_nn(dz1b, w_out[:A_WIDTH].T, name=tag + "_dya")
    dyb = mm_nn(dz1b, w_out[A_WIDTH:].T, name=tag + "_dyb")
    dwo = jnp.concatenate([mm_tn(ya, dz1b, name=tag + "_dwo_a"), mm_tn(yb, dz1b, name=tag + "_dwo_b")], 0)
    dpqkv, dsinks = attn_bwd(pqkv, W["a_sinks"], dya, nb, name=tag + "_attn_bwd")
    dxc, dpbg, dwa_s, dwx_s, dba, dbx, dlam = rglru_bwd(xc, pbg, hh, dyb, wa_s, wx_s, W["b_ba"], W["b_bx"],
                                                       W["b_lam"], nb, name=tag + "_rglru_bwd")
    dpbx, dconv_w, dconv_b = conv_bwd(pbx, W["b_conv_w"][0], W["b_conv_b"], dxc, False, nb, name=tag + "_conv_bwd")
    dw_in = jnp.concatenate([mm_tn(x1b,dpqkv, name=tag + "_dwin_qkv"), mm_tn(x1b,dpbx, name=tag + "_dwin_bx"),
                             mm_tn(x1b,dpbg, name=tag + "_dwin_bg")], 1)
    dx1 = mm_nn(dpqkv, w_qkv.T, add=dz1, add_scale=DN_ALPHA, name=tag + "_dx_qkv")
    dx1 = mm_nn(dpbx, w_bx.T, add=dx1, name=tag + "_dx_bx")
    dx1 = mm_nn(dpbg, w_bg.T, add=dx1, name=tag + "_dx_bg")
    grads = {"ab_w_in": dw_in[None], "a_sinks": dsinks, "b_conv_w": dconv_w[None], "b_conv_b": dconv_b,
             "b_wa": _slab_blocks(dwa_s)[None], "b_ba": dba, "b_wx": _slab_blocks(dwx_s)[None], "b_bx": dbx,
             "b_lam": dlam, "ab_w_out": dwo[None]}
    return dx1, grads


def _mixer_c_fwd(x1, x1b, W, g, b, nb, tag):
    w_in = W["c_w_in"][0].astype(bf16)
    d = w_in.shape[0]
    o1, o2 = 3 * C_WIDTH, 4 * C_WIDTH
    w_qkv, w_z = w_in[:, :o1], w_in[:, o1:o2]
    w_ba = jnp.concatenate([w_in[:, o2:], jnp.zeros((d, LANES - 2 * C_HEADS), bf16)], 1)
    pqkv = mm_nn(x1b,w_qkv, name=tag + "_in_qkv")
    pz = mm_nn(x1b,w_z, name=tag + "_in_z")
    pba = mm_nn(x1b,w_ba, name=tag + "_in_ba")
    zero_b = jnp.zeros((1, o1), f32)
    qkvc = conv_fwd(pqkv, W["c_conv_w"][0], zero_b, True, nb, name=tag + "_conv_fwd")
    prep = gdn_prep_fwd(qkvc, pba, W["c_a_log"], W["c_dt_bias"], nb, name=tag + "_prep_fwd")
    yc, states = gdn_rec_fwd(*prep, pz, W["c_norm_g"], nb, name=tag + "_rec_fwd")
    w_out = W["c_w_out"][0].astype(bf16)
    x2, z1, x2b = proj_ln([yc], [w_out], x1, g, b, name=tag + "_out_ln")
    saved = (pqkv, pz, pba, qkvc, prep, states, yc, w_qkv, w_z, w_ba, w_out, zero_b)
    return x2, x2b, z1, saved


def _mixer_c_bwd(x1b, dz1, dz1b, W, saved, nb, tag):
    pqkv, pz, pba, qkvc, prep, states, yc, w_qkv, w_z, w_ba, w_out, zero_b = saved
    dyc = mm_nn(dz1b, w_out.T, name=tag + "_dyc")
    dwo = mm_tn(yc, dz1b, name=tag + "_dwo")
    rec = gdn_rec_bwd(*prep, pz, W["c_norm_g"], states, dyc, nb, name=tag + "_rec_bwd")
    cts, dpz, dng = rec[:6], rec[6], rec[7]
    dq, dk, dv, dpba, dalog, ddtb = gdn_prep_bwd(qkvc, pba, W["c_a_log"], W["c_dt_bias"], cts, nb,
                                                 name=tag + "_prep_bwd")
    dqkvc = jnp.concatenate([dq, dk, dv], 1)
    dpqkv, dconv_w, _ = conv_bwd(pqkv, W["c_conv_w"][0], zero_b, dqkvc, True, nb, name=tag + "_conv_bwd")
    dw_in = jnp.concatenate([mm_tn(x1b,dpqkv, name=tag + "_dwin_qkv"), mm_tn(x1b,dpz, name=tag + "_dwin_z"),
                             mm_tn(x1b,dpba, name=tag + "_dwin_ba")[:, :2 * C_HEADS]], 1)
    dx1 = mm_nn(dpqkv, w_qkv.T, add=dz1, add_scale=DN_ALPHA, name=tag + "_dx_qkv")
    dx1 = mm_nn(dpz, w_z.T, add=dx1, name=tag + "_dx_z")
    dx1 = mm_nn(dpba, w_ba.T, add=dx1, name=tag + "_dx_ba")
    grads = {"c_w_in": dw_in[None], "c_conv_w": dconv_w[None], "c_a_log": dalog[:, :C_HEADS],
             "c_dt_bias": ddtb[:, :C_HEADS], "c_norm_g": dng, "c_w_out": dwo[None]}
    return dx1, grads


def _local_step(x, p, target, W, F, on_ffn_grads):
    nb, s, d = x.shape
    t = nb * s
    h = x.reshape(t, d)
    hb = h.astype(bf16)
    tape = []
    for i in range(DEPTH):
        tag = f"l{i}"
        f1 = [F[k][i] for k in ("ffn1_wg", "ffn1_wu", "ffn1_wd")]
        f2 = [F[k][i] for k in ("ffn2_wg", "ffn2_wu", "ffn2_wd")]
        lg = [W["ln_g"][i, k][None] for k in range(3)]
        lb = [W["ln_b"][i, k][None] for k in range(3)]
        x1, z0, x1b = ffn_fwd(h, *f1, lg[0], lb[0], name=tag + "_ffn1_fwd")
        mixer = _mixer_ab_fwd if i % 2 == 0 else _mixer_c_fwd
        x2, x2b, z1, msaved = mixer(x1, x1b, W, lg[1], lb[1], nb, tag + "_mix")
        x3, z2, _ = ffn_fwd(x2, *f2, lg[2], lb[2], name=tag + "_ffn2_fwd")
        pi = p[i].reshape(t, -1)
        pw = (W["ple_wg"][i].astype(bf16), W["ple_bg"][i][None], W["ple_wp"][i].astype(bf16))
        x4, x4b = ple_fwd(x3, pi, *pw, name=tag + "_ple_fwd")
        tape.append((hb, z0, x1b, msaved, z1, x2b, z2, x3, pi, pw, lg))
        h, hb = x4, x4b
    dh, sq = loss_head(h, target.reshape(t, d), name="loss_head")
    loss = 0.5 * jnp.sum(sq) / d
    per_layer = [None] * DEPTH
    grads = {}
    for i in reversed(range(DEPTH)):
        tag = f"l{i}"
        hb_in, z0, x1b, msaved, z1, x2b, z2, x3, pi, pw, lg = tape[i]
        dx3, dple_wg, dple_bg, dple_wp = ple_bwd(x3, pi, dh, pw[0], pw[0].T, pw[1], pw[2], name=tag + "_ple_bwd")
        dz2, dz2b, dg2, db2 = ln_bwd(z2, dx3, lg[2], name=tag + "_ln2_bwd")
        f1 = [F[k][i] for k in ("ffn1_wg", "ffn1_wu", "ffn1_wd")]
        f2 = [F[k][i] for k in ("ffn2_wg", "ffn2_wu", "ffn2_wd")]
        dgate, dup, *df2 = ffn_bwd_weights(x2b, dz2b, *f2, name=tag + "_ffn2_bwd_w")
        on_ffn_grads(i, 3, df2)
        dx2 = ffn_bwd_input(dgate, dup, f2[0], f2[1], dz2, name=tag + "_ffn2_bwd_x")
        dz1, dz1b, dg1, db1 = ln_bwd(z1, dx2, lg[1], name=tag + "_ln1_bwd")
        mixer_bwd = _mixer_ab_bwd if i % 2 == 0 else _mixer_c_bwd
        dx1, mgrads = mixer_bwd(x1b, dz1, dz1b, W, msaved, nb, tag + "_mix")
        grads.update(mgrads)
        dz0, dz0b, dg0, db0 = ln_bwd(z0, dx1, lg[0], name=tag + "_ln0_bwd")
        dgate, dup, *df1 = ffn_bwd_weights(hb_in, dz0b, *f1, name=tag + "_ffn1_bwd_w")
        on_ffn_grads(i, 0, df1)
        dh = ffn_bwd_input(dgate, dup, f1[0], f1[1], dz0, name=tag + "_ffn1_bwd_x")
        per_layer[i] = {"ln_g": jnp.concatenate([dg0, dg1, dg2], 0), "ln_b": jnp.concatenate([db0, db1, db2], 0),
                        "ple_wg": dple_wg, "ple_bg": dple_bg[0], "ple_wp": dple_wp}
    for k in per_layer[0]:
        grads[k] = jnp.stack([per_layer[i][k] for i in range(DEPTH)])
    return loss, dh.reshape(nb, s, d), grads


WEIGHT_NAMES = ("ffn1_wg", "ffn1_wu", "ffn1_wd", "ffn2_wg", "ffn2_wu", "ffn2_wd", "ln_g", "ln_b", "ple_wg", "ple_bg",
                "ple_wp", "ab_w_in", "a_sinks", "b_conv_w", "b_conv_b", "b_wa", "b_ba", "b_wx", "b_bx", "b_lam",
                "ab_w_out", "c_w_in", "c_conv_w", "c_a_log", "c_dt_bias", "c_norm_g", "c_w_out")
NATIVE_NAMES = WEIGHT_NAMES[:6]
PACKED_NAMES = WEIGHT_NAMES[6:]
PACK_MATRICES = ("ple_wg", "ple_wp", "ab_w_in", "ab_w_out", "c_w_in", "c_w_out")
PACK_GROUPS = (tuple(k for k in PACKED_NAMES if k not in PACK_MATRICES), PACK_MATRICES)
PACK_TRANSIT = (f32, bf16)
SHARD_AXIS = {"ffn1_wg": 2, "ffn1_wu": 2, "ffn1_wd": 1, "ffn2_wg": 2, "ffn2_wu": 2, "ffn2_wd": 1, "ln_g": 2, "ln_b": 2,
              "ple_wg": 1, "ple_wp": 2, "ab_w_in": 2, "b_conv_w": 2, "ab_w_out": 1, "c_w_in": 2, "c_conv_w": 2,
              "c_w_out": 1}
N_CHIPS = 4
PACK_COLS = LANES
PACK_TILE_MULTIPLE = 256
ELEMENTWISE_BLOCK_ELEMS = 128 * 1024


def _row_tile(r, cols):
    return _tile(r, max(2 * SUBLANES, ELEMENTWISE_BLOCK_ELEMS // cols), 2 * SUBLANES)
MESH = pl.DeviceIdType.MESH
ANY = pl.BlockSpec(memory_space=pl.ANY)


def _tiled_dims(shape):
    w = shape[-1]
    r = 1
    for dim in shape[:-1]:
        r *= dim
    return r, w, -(-r // SUBLANES) * SUBLANES, -(-w // LANES) * LANES


def _pack(pieces, lead=()):
    k = len(lead)
    tiles = []
    for a in pieces:
        r, w, rp, wp = _tiled_dims(a.shape[k:])
        a2 = jnp.pad(a.reshape(lead + (r, w)), [(0, 0)] * k + [(0, rp - r), (0, wp - w)])
        a2 = a2.reshape(lead + (rp // SUBLANES, SUBLANES, wp // LANES, LANES))
        a2 = jnp.swapaxes(a2, k + 1, k + 2)
        tiles.append(a2.reshape(lead + (-1, SUBLANES, LANES)))
    flat = jnp.concatenate(tiles, axis=k)
    n = flat.shape[k]
    n_pad = -(-n // PACK_TILE_MULTIPLE) * PACK_TILE_MULTIPLE
    flat = jnp.pad(flat, [(0, 0)] * k + [(0, n_pad - n), (0, 0), (0, 0)])
    return flat.reshape(lead + (n_pad * SUBLANES, PACK_COLS))


def _unpack(pack, shapes, lead=()):
    k = len(lead)
    flat = pack.reshape(lead + (-1, SUBLANES, LANES))
    out, o = [], 0
    for shp in shapes:
        r, w, rp, wp = _tiled_dims(shp)
        n = (rp // SUBLANES) * (wp // LANES)
        a2 = lax.slice_in_dim(flat, o, o + n, axis=k).reshape(lead + (rp // SUBLANES, wp // LANES, SUBLANES, LANES))
        a2 = jnp.swapaxes(a2, k + 1, k + 2).reshape(lead + (rp, wp))
        a2 = lax.slice_in_dim(lax.slice_in_dim(a2, 0, r, axis=k), 0, w, axis=k + 1)
        out.append(a2.reshape(lead + tuple(shp)))
        o += n
    return out


def _mesh_position():
    x, y, c = lax.axis_index("x"), lax.axis_index("y"), lax.axis_index("c")
    chips = [(1 - x, y), (x, 1 - y), (1 - x, 1 - y)]
    return x, y, c, chips


def _remote(src, dst, send_sems, recv_sems, k, to):
    return pltpu.make_async_remote_copy(src_ref=src, dst_ref=dst, send_sem=send_sems.at[k], recv_sem=recv_sems.at[k],
                                        device_id=to, device_id_type=MESH)


def _sems(n):
    return pltpu.SemaphoreType.DMA((n,))


def place_slot(parts, slots, n_slots, dtype, from_slot, *, name):
    n = len(parts)
    r, cols = parts[0].shape[-2:]
    tr = _row_tile(r, cols)

    def body(src_ref, dst_ref, *refs):
        for a in range(n):
            refs[n + a][...] = refs[a][...].astype(dtype)

    dst = pl.BlockSpec((None, tr, cols), lambda i, src_ref, dst_ref: (dst_ref[0], i, 0))
    src = (pl.BlockSpec((None, tr, cols), lambda i, src_ref, dst_ref: (src_ref[0], i, 0)) if from_slot
           else pl.BlockSpec((tr, cols), lambda i, src_ref, dst_ref: (i, 0)))
    return pl.pallas_call(
        body,
        grid_spec=pltpu.PrefetchScalarGridSpec(num_scalar_prefetch=2, grid=(r // tr,), in_specs=[src] * n,
                                               out_specs=[dst] * n),
        out_shape=[jax.ShapeDtypeStruct((n_slots, r, cols), dtype)] * n,
        compiler_params=_params(("parallel",)), name=name,
    )(*slots, *parts)


def gather_shards(bufs, *, name):
    n = len(bufs)

    def body(*refs):
        out_refs = refs[n:2 * n]
        send_sems, recv_sems = refs[2 * n:]
        x, y, c, chips = _mesh_position()
        me = 2 * x + y
        sibling = (x, y, 1 - c)
        waits = []
        for j, (cx, cy) in enumerate(chips):
            for a in range(n):
                own = out_refs[a].at[me, c]
                cp = _remote(own, own, send_sems, recv_sems, 6 * a + j, (cx, cy, c))
                cp.start()
                waits.append(cp.wait_send)
        for j, (cx, cy) in enumerate(chips):
            for a in range(n):
                got = out_refs[a].at[2 * cx + cy, c]
                _remote(got, got, send_sems, recv_sems, 6 * a + j, (cx, cy, c)).wait_recv()
                fw = _remote(got, got, send_sems, recv_sems, 6 * a + 3 + j, sibling)
                fw.start()
                waits.append(fw.wait_send)
        for j, (cx, cy) in enumerate(chips):
            for a in range(n):
                got = out_refs[a].at[2 * cx + cy, 1 - c]
                _remote(got, got, send_sems, recv_sems, 6 * a + 3 + j, sibling).wait_recv()
        for wait in waits:
            wait()

    return pl.pallas_call(
        body, out_shape=[jax.ShapeDtypeStruct(b.shape, b.dtype) for b in bufs],
        in_specs=[ANY] * n, out_specs=[ANY] * n, scratch_shapes=[_sems(6 * n), _sems(6 * n)],
        input_output_aliases={a: a for a in range(n)}, name=name,
    )(*bufs)


def chip_exchange(ps, qs, *, name):
    n = len(ps)

    def body(*refs):
        p_refs, q_refs = refs[:n], refs[2 * n:3 * n]
        send_sems, recv_sems = refs[3 * n:]
        x, y, c, chips = _mesh_position()
        me = 2 * x + y
        waits = []
        for j, (cx, cy) in enumerate(chips):
            for a in range(n):
                cp = _remote(p_refs[a].at[2 * cx + cy], q_refs[a].at[me], send_sems, recv_sems, 3 * a + j, (cx, cy, c))
                cp.start()
                waits.append(cp.wait_send)
        for j, (cx, cy) in enumerate(chips):
            for a in range(n):
                got = q_refs[a].at[2 * cx + cy]
                _remote(got, got, send_sems, recv_sems, 3 * a + j, (cx, cy, c)).wait_recv()
        for wait in waits:
            wait()

    return pl.pallas_call(
        body, out_shape=[jax.ShapeDtypeStruct(q_.shape, q_.dtype) for q_ in qs], in_specs=[ANY] * (2 * n),
        out_specs=[ANY] * n, scratch_shapes=[_sems(3 * n), _sems(3 * n)],
        input_output_aliases={n + a: a for a in range(n)}, name=name,
    )(*ps, *qs)


def gather_slots_async(bufs, collective_id, *, name):
    n = len(bufs)
    refs = [jax.new_ref(b, memory_space=pltpu.MemorySpace.HBM) for b in bufs]

    @pl.kernel(mesh=plsc.ScalarSubcoreMesh(axis_name="sequencer", num_cores=1), name=name,
               scratch_types=(_sems(3 * n), _sems(3 * n)),
               compiler_params=pltpu.CompilerParams(collective_id=collective_id))
    def launch(send_sems, recv_sems):
        x, y, c, chips = _mesh_position()
        me = 2 * x + y
        barrier = pltpu.get_barrier_semaphore()
        for cx, cy in chips:
            pl.semaphore_signal(barrier, inc=1, device_id=(cx, cy, c), device_id_type=MESH)
        pl.semaphore_wait(barrier, len(chips))
        sends = []
        for j, (cx, cy) in enumerate(chips):
            for a in range(n):
                own = refs[a].at[me]
                cp = _remote(own, own, send_sems, recv_sems, 3 * a + j, (cx, cy, c))
                cp.start()
                sends.append(cp)
        for j, (cx, cy) in enumerate(chips):
            for a in range(n):
                got = refs[a].at[2 * cx + cy]
                _remote(got, got, send_sems, recv_sems, 3 * a + j, (cx, cy, c)).wait_recv()
        for cp in sends:
            cp.wait_send()

    launch()
    return [r[...] for r in refs]


N_DEVICES = 8
PEER_FLIPS = tuple((dx, dy, dc) for dx in (0, 1) for dy in (0, 1) for dc in (0, 1) if dx or dy or dc)


def exchange_partials_async(sends, recvs, collective_id, *, name):
    n = len(sends)
    s_refs = [jax.new_ref(a, memory_space=pltpu.MemorySpace.HBM) for a in sends]
    r_refs = [jax.new_ref(a, memory_space=pltpu.MemorySpace.HBM) for a in recvs]
    k = len(PEER_FLIPS)

    @pl.kernel(mesh=plsc.ScalarSubcoreMesh(axis_name="sequencer", num_cores=1), name=name,
               scratch_types=(_sems(k), _sems(k)), compiler_params=pltpu.CompilerParams(collective_id=collective_id))
    def launch(send_sems, recv_sems):
        x, y, c, _ = _mesh_position()
        me = 4 * x + 2 * y + c
        peers = [(1 - x if dx else x, 1 - y if dy else y, 1 - c if dc else c) for dx, dy, dc in PEER_FLIPS]
        barrier = pltpu.get_barrier_semaphore()
        for peer in peers:
            pl.semaphore_signal(barrier, inc=1, device_id=peer, device_id_type=MESH)
        pl.semaphore_wait(barrier, len(peers))
        sends_started = []
        for j, (px, py, pc) in enumerate(peers):
            for a in range(n):
                cp = _remote(s_refs[a].at[2 * px + py], r_refs[a].at[me], send_sems, recv_sems, j, (px, py, pc))
                cp.start()
                sends_started.append(cp)
        for j, (px, py, pc) in enumerate(peers):
            for a in range(n):
                got = r_refs[a].at[4 * px + 2 * py + pc]
                _remote(got, got, send_sems, recv_sems, j, (px, py, pc)).wait_recv()
        for cp in sends_started:
            cp.wait_send()

    launch()
    return [r[...] for r in r_refs]


def sibling_exchange(gs, *, name):
    n = len(gs)

    def body(*refs):
        g_refs, out_refs = refs[:n], refs[n:2 * n]
        send_sems, recv_sems = refs[2 * n:]
        x, y, c, _ = _mesh_position()
        cps = [_remote(g_refs[a].at[:, 1 - c], out_refs[a], send_sems, recv_sems, a, (x, y, 1 - c)) for a in range(n)]
        for cp in cps:
            cp.start()
        for cp in cps:
            cp.wait()

    return pl.pallas_call(
        body, out_shape=[jax.ShapeDtypeStruct(g.shape[:1] + g.shape[2:], g.dtype) for g in gs],
        in_specs=[ANY] * n, out_specs=[ANY] * n, scratch_shapes=[_sems(n), _sems(n)], name=name,
    )(*gs)


def add_own_half(gs, others, c_idx, dtype, *, name):
    n = len(gs)
    ns, _, r, cols = gs[0].shape
    tr = _row_tile(r, cols)

    def body(c_ref, *refs):
        for a in range(n):
            refs[2 * n + a][...] = (refs[a][...] + refs[n + a][...]).astype(dtype)

    own = pl.BlockSpec((None, None, tr, cols), lambda s, i, c_ref: (s, c_ref[0], i, 0))
    oth = pl.BlockSpec((None, tr, cols), lambda s, i, c_ref: (s, i, 0))
    return pl.pallas_call(
        body,
        grid_spec=pltpu.PrefetchScalarGridSpec(num_scalar_prefetch=1, grid=(ns, r // tr),
                                               in_specs=[own] * n + [oth] * n, out_specs=[oth] * n),
        out_shape=[jax.ShapeDtypeStruct((ns, r, cols), dtype)] * n,
        compiler_params=_params(("parallel", "parallel")), name=name,
    )(c_idx, *gs, *others)


def sum_slots(qs, *, name):
    n = len(qs)
    ns, r, cols = qs[0].shape
    tr = _row_tile(r, cols * ns)

    def body(*refs):
        for a in range(n):
            q_ref = refs[a]
            acc = q_ref[0].astype(f32) + q_ref[1].astype(f32)
            for i in range(2, ns):
                acc = acc + q_ref[i].astype(f32)
            refs[n + a][...] = acc

    return pl.pallas_call(
        body, grid=(r // tr,), in_specs=[pl.BlockSpec((ns, tr, cols), lambda i: (0, i, 0))] * n,
        out_specs=[pl.BlockSpec((tr, cols), lambda i: (i, 0))] * n,
        out_shape=[jax.ShapeDtypeStruct((r, cols), f32)] * n,
        compiler_params=_params(("parallel",)), name=name,
    )(*qs)


def sibling_share(bufs, *, name):
    n = len(bufs)

    def body(*refs):
        out_refs = refs[n:2 * n]
        send_sems, recv_sems = refs[2 * n:]
        x, y, c, _ = _mesh_position()
        sibling = (x, y, 1 - c)
        cps = []
        for a in range(n):
            own = out_refs[a].at[c]
            cp = _remote(own, own, send_sems, recv_sems, a, sibling)
            cp.start()
            cps.append(cp)
        for a in range(n):
            theirs = out_refs[a].at[1 - c]
            _remote(theirs, theirs, send_sems, recv_sems, a, sibling).wait_recv()
        for cp in cps:
            cp.wait_send()

    return pl.pallas_call(
        body, out_shape=[jax.ShapeDtypeStruct(b.shape, b.dtype) for b in bufs], in_specs=[ANY] * n,
        out_specs=[ANY] * n, scratch_shapes=[_sems(n), _sems(n)],
        input_output_aliases={a: a for a in range(n)}, name=name,
    )(*bufs)


def _adamw_update(w, g, m, v):
    m2 = ADAM_B1 * m + (1.0 - ADAM_B1) * g
    v2 = ADAM_B2 * v + (1.0 - ADAM_B2) * (g * g)
    m_hat = m2 / (1.0 - ADAM_B1 ** ADAM_STEP)
    v_hat = v2 / (1.0 - ADAM_B2 ** ADAM_STEP)
    return -ADAM_LR * (m_hat / (jnp.sqrt(v_hat) + ADAM_EPS) + ADAM_WD * w), m2, v2


def adamw_from_partials(ws, ms, vs, slots, layer, acc, *, name):
    n = len(ws)
    nl, r, cols = ws[0].shape
    ns = slots[0].shape[0]
    tr = _row_tile(r, cols * 2)

    def body(*refs):
        for a in range(n):
            w_ref, m_ref, v_ref, s_ref = (refs[k * n + a] for k in range(4))
            g_ref, d_ref, m2_ref, v2_ref = (refs[len(refs) - 4 * n + k * n + a] for k in range(4))
            g = s_ref[0].astype(f32) + s_ref[1].astype(f32)
            for i in range(2, ns):
                g = g + s_ref[i].astype(f32)
            g_ref[...] = g
            d_ref[...], m2_ref[...], v2_ref[...] = _adamw_update(w_ref[...], g, m_ref[...], v_ref[...])

    lay = pl.BlockSpec((None, tr, cols), lambda i: (layer, i, 0))
    in_specs = [lay] * (3 * n) + [pl.BlockSpec((ns, tr, cols), lambda i: (0, i, 0))] * n
    args = [*ws, *ms, *vs, *slots]
    aliases = {}
    if acc is not None:
        in_specs += [ANY] * (4 * n)
        args += [a for lst in acc for a in lst]
        aliases = {4 * n + k: k for k in range(4 * n)}
    out = pl.pallas_call(
        body, grid=(r // tr,), in_specs=in_specs, out_specs=[lay] * (4 * n),
        out_shape=[jax.ShapeDtypeStruct((nl, r, cols), f32)] * (4 * n), input_output_aliases=aliases,
        compiler_params=_params(("parallel",)), name=name,
    )(*args)
    return [list(out[k * n:(k + 1) * n]) for k in range(4)]


def adamw(ws, gs, ms, vs, *, name):
    n = len(ws)
    r, cols = ws[0].shape
    tr = _row_tile(r, cols)

    def body(*refs):
        for a in range(n):
            w_ref, g_ref, m_ref, v_ref = (refs[k * n + a] for k in range(4))
            d_ref, m2_ref, v2_ref = (refs[(4 + k) * n + a] for k in range(3))
            d_ref[...], m2_ref[...], v2_ref[...] = _adamw_update(w_ref[...], g_ref[...], m_ref[...], v_ref[...])

    row = pl.BlockSpec((tr, cols), lambda i: (i, 0))
    out = pl.pallas_call(
        body, grid=(r // tr,), in_specs=[row] * (4 * n), out_specs=[row] * (3 * n),
        out_shape=[jax.ShapeDtypeStruct((r, cols), f32)] * (3 * n),
        compiler_params=_params(("parallel",)), name=name,
    )(*ws, *gs, *ms, *vs)
    return out[:n], out[n:2 * n], out[2 * n:]


def _full_weights(gathered, names, weights):
    pieces = _unpack(gathered, [weights[k].shape for k in names], lead=(N_CHIPS,))
    full = {}
    for name, pc in zip(names, pieces):
        ax = SHARD_AXIS.get(name)
        if ax is None:
            full[name] = weights[name]
        else:
            shp = weights[name].shape
            full[name] = jnp.moveaxis(pc, 0, ax).reshape(shp[:ax] + (N_CHIPS * shp[ax],) + shp[ax + 1:])
    return full


def _grad_pack(grads, names, shapes):
    pieces = []
    for name, shp in zip(names, shapes):
        g = grads[name]
        ax = SHARD_AXIS.get(name)
        if ax is None:
            pieces.append(jnp.broadcast_to(g.reshape(shp)[None], (N_CHIPS,) + tuple(shp)))
        else:
            pieces.append(jnp.stack(jnp.split(g, N_CHIPS, axis=ax)))
    return _pack(pieces, lead=(N_CHIPS,))


def _by_shape(arrays):
    groups = {}
    for i, a in enumerate(arrays):
        groups.setdefault(a.shape, []).append(i)
    return list(groups.values())


def _grouped(fn, lists, n_out, tag):
    outs = [[None] * len(lists[0]) for _ in range(n_out)]
    for gi, idx in enumerate(_by_shape(lists[0])):
        res = fn(*[[lst[i] for i in idx] for lst in lists], name=f"{tag}_{gi}")
        res = res if n_out > 1 else (res,)
        for k in range(n_out):
            for i, r in zip(idx, res[k]):
                outs[k][i] = r
    return outs if n_out > 1 else outs[0]


def _train_step(x, p, loss_target, weights, m, v):
    shapes = [[weights[k].shape for k in names] for names in PACK_GROUPS]
    halves = lambda a: a.reshape((2, a.shape[0] // 2) + a.shape[1:])
    packs = lambda d_: [halves(_pack([d_[k] for k in names])) for names in PACK_GROUPS]
    nn_ = len(NATIVE_NAMES)
    local = [weights[k] for k in NATIVE_NAMES] + packs(weights)
    local_m = [m[k] for k in NATIVE_NAMES] + packs(m)
    local_v = [v[k] for k in NATIVE_NAMES] + packs(v)
    flat = lambda lst: [a.reshape((-1, a.shape[-1])) for a in lst]
    c_idx = lax.axis_index("c").astype(jnp.int32).reshape(1)
    chip_idx = (2 * lax.axis_index("x") + lax.axis_index("y")).astype(jnp.int32).reshape(1)
    c2 = (c_idx, c_idx)
    chip2 = (chip_idx, chip_idx)
    chip_dev = (chip_idx, 2 * chip_idx + c_idx)

    def placed(arrays, slot, n_slots, dtype, from_slot, tag):
        return _grouped(lambda a, name: place_slot(a, slot, n_slots, dtype, from_slot, name=name), [arrays], 1, tag)

    ffn_own = [weights[k][i] for i in range(DEPTH) for k in NATIVE_NAMES]
    ffn_bufs = placed(ffn_own, chip2, N_CHIPS, bf16, False, "place_ffn_weights")
    group = len(NATIVE_NAMES) // 2
    n_ffn_groups = len(ffn_bufs) // group
    ffn_gathered = []
    for gi in range(n_ffn_groups):
        ffn_gathered += gather_slots_async(ffn_bufs[gi * group:(gi + 1) * group], collective_id=1 + gi,
                                           name=f"comm_gather_ffn_{gi}")
    ffn_weights = {k: [ffn_gathered[i * len(NATIVE_NAMES) + j] for i in range(DEPTH)] for j, k in enumerate(NATIVE_NAMES)}
    pack_bufs = [placed(flat([a]), chip2, N_CHIPS, dt, False, f"place_packed_weights_{gi}")[0].reshape((N_CHIPS,) + a.shape)
                 for gi, (a, dt) in enumerate(zip(local[nn_:], PACK_TRANSIT))]
    full = {}
    for names, gathered in zip(PACK_GROUPS, gather_shards(pack_bufs, name="comm_gather_weights")):
        full.update(_full_weights(gathered, names, weights))
    first_grad_id = n_ffn_groups + 1
    in_flight = {}

    def on_ffn_grads(layer, first, partials):
        tag = f"ffn_grads_l{layer}_{first}"
        recvs = placed(partials, chip_dev, N_DEVICES, bf16, True, "place_" + tag)
        got = exchange_partials_async(partials, recvs, collective_id=first_grad_id + len(in_flight), name="comm_" + tag)
        in_flight[(layer, first)] = got

    loss, grad_x, grads = _local_step(x, p, loss_target, full, ffn_weights, on_ffn_grads)
    gs = [_grad_pack(grads, names, shp).reshape((N_CHIPS,) + a.shape)
          for names, shp, a in zip(PACK_GROUPS, shapes, local[nn_:])]
    others = sibling_exchange(gs, name="comm_grad_sibling")
    chip_sums = [add_own_half([g], [o], c_idx, dt, name=f"grad_add_sibling_{gi}")[0]
                 for gi, (g, o, dt) in enumerate(zip(gs, others, PACK_TRANSIT))]
    own = [placed([cs], chip2, N_CHIPS, dt, True, f"place_own_partial_{gi}")[0]
           for gi, (cs, dt) in enumerate(zip(chip_sums, PACK_TRANSIT))]
    slots = chip_exchange(chip_sums, own, name="comm_grad_chips")
    mine = _grouped(sum_slots, [list(slots)], 1, "grad_sum_chips")
    pack_sum = sibling_share(placed(mine, c2, 2, f32, False, "place_own_half"), name="comm_grad_share")
    ffn_out = [{} for _ in range(4)]
    for (layer, first), got in in_flight.items():
        names = NATIVE_NAMES[first:first + len(got)]
        for idx in _by_shape([weights[k] for k in names]):
            ks = [names[i] for i in idx]
            acc = [[out[k] for k in ks] for out in ffn_out] if ks[0] in ffn_out[0] else None
            res = adamw_from_partials([weights[k] for k in ks], [m[k] for k in ks], [v[k] for k in ks],
                                      [got[i] for i in idx], layer, acc, name=f"adamw_ffn_l{layer}_{first + idx[0]}")
            for out, arrays in zip(ffn_out, res):
                out.update(zip(ks, arrays))
    pack_out = [list(pack_sum)] + _grouped(adamw, [flat(local[nn_:]), flat(pack_sum), flat(local_m[nn_:]),
                                                    flat(local_v[nn_:])], 3, "adamw_packed")
    loss = lax.psum(loss, ("x", "y", "c"))
    outs = []
    for by_name, packs_ in zip(ffn_out, pack_out):
        by_name = dict(by_name)
        for names, shp, pk in zip(PACK_GROUPS, shapes, packs_):
            by_name.update(zip(names, _unpack(pk, shp)))
        outs += [by_name[k] for k in WEIGHT_NAMES]
    return (loss, grad_x, *outs)


def kernel(x, p, ffn1_wg, ffn1_wu, ffn1_wd, ffn2_wg, ffn2_wu, ffn2_wd, ln_g, ln_b, ple_wg, ple_bg, ple_wp, ab_w_in, a_sinks, b_conv_w, b_conv_b, b_wa, b_ba, b_wx, b_bx, b_lam, ab_w_out, c_w_in, c_conv_w, c_a_log, c_dt_bias, c_norm_g, c_w_out, loss_target, m_ffn1_wg, m_ffn1_wu, m_ffn1_wd, m_ffn2_wg, m_ffn2_wu, m_ffn2_wd, m_ln_g, m_ln_b, m_ple_wg, m_ple_bg, m_ple_wp, m_ab_w_in, m_a_sinks, m_b_conv_w, m_b_conv_b, m_b_wa, m_b_ba, m_b_wx, m_b_bx, m_b_lam, m_ab_w_out, m_c_w_in, m_c_conv_w, m_c_a_log, m_c_dt_bias, m_c_norm_g, m_c_w_out, v_ffn1_wg, v_ffn1_wu, v_ffn1_wd, v_ffn2_wg, v_ffn2_wu, v_ffn2_wd, v_ln_g, v_ln_b, v_ple_wg, v_ple_bg, v_ple_wp, v_ab_w_in, v_a_sinks, v_b_conv_w, v_b_conv_b, v_b_wa, v_b_ba, v_b_wx, v_b_bx, v_b_lam, v_ab_w_out, v_c_w_in, v_c_conv_w, v_c_a_log, v_c_dt_bias, v_c_norm_g, v_c_w_out):
    weights = [ffn1_wg, ffn1_wu, ffn1_wd, ffn2_wg, ffn2_wu, ffn2_wd, ln_g, ln_b, ple_wg, ple_bg, ple_wp, ab_w_in, a_sinks,
               b_conv_w, b_conv_b, b_wa, b_ba, b_wx, b_bx, b_lam, ab_w_out, c_w_in, c_conv_w, c_a_log, c_dt_bias, c_norm_g,
               c_w_out]
    m = [m_ffn1_wg, m_ffn1_wu, m_ffn1_wd, m_ffn2_wg, m_ffn2_wu, m_ffn2_wd, m_ln_g, m_ln_b, m_ple_wg, m_ple_bg, m_ple_wp,
         m_ab_w_in, m_a_sinks, m_b_conv_w, m_b_conv_b, m_b_wa, m_b_ba, m_b_wx, m_b_bx, m_b_lam, m_ab_w_out, m_c_w_in,
         m_c_conv_w, m_c_a_log, m_c_dt_bias, m_c_norm_g, m_c_w_out]
    v = [v_ffn1_wg, v_ffn1_wu, v_ffn1_wd, v_ffn2_wg, v_ffn2_wu, v_ffn2_wd, v_ln_g, v_ln_b, v_ple_wg, v_ple_bg, v_ple_wp,
         v_ab_w_in, v_a_sinks, v_b_conv_w, v_b_conv_b, v_b_wa, v_b_ba, v_b_wx, v_b_bx, v_b_lam, v_ab_w_out, v_c_w_in,
         v_c_conv_w, v_c_a_log, v_c_dt_bias, v_c_norm_g, v_c_w_out]
    return _train_step(x, p, loss_target, dict(zip(WEIGHT_NAMES, weights)), dict(zip(WEIGHT_NAMES, m)),
                       dict(zip(WEIGHT_NAMES, v)))
```

```python
import functools

import jax
import jax.numpy as jnp
from jax import lax
from jax.experimental import pallas as pl
from jax.experimental.pallas import tpu as pltpu
from jax.experimental.pallas import tpu_sc as plsc

f32 = jnp.float32
bf16 = jnp.bfloat16

DEPTH = 2
CHUNK = 64
A_HEADS, A_KV_HEADS, A_GROUP, A_HEAD_DIM = 8, 2, 4, 64
A_WIDTH, A_KV_WIDTH, A_WINDOW = 512, 128, 128
B_WIDTH, B_BLOCKS, B_BLOCK, B_CONV = 512, 8, 64, 4
RG_C = 8.0
C_HEADS, C_HEAD_DIM, C_WIDTH, C_CONV = 8, 128, 1024, 4
DN_ALPHA = (2.0 * DEPTH) ** 0.25
LN_EPS = 1e-5
NORM_EPS = 1e-6
NEG = -1e30
ADAM_LR, ADAM_B1, ADAM_B2, ADAM_EPS, ADAM_WD, ADAM_STEP = 0.001, 0.9, 0.999, 1e-08, 0.01, 10

VMEM_LIMIT_BYTES = 56 * 1024 * 1024
LANES = 128
SUBLANES = 8
GROUP_W = 128
PREP_FWD_UNROLL = 16
PREP_BWD_UNROLL = 16
C_HEADS_PER_STEP = 8
GDN_TIME_BLOCK = 256

NN = ((1,), (0,))
NT = ((1,), (1,))
TN = ((0,), (0,))


def _params(sem):
    return pltpu.CompilerParams(dimension_semantics=sem, vmem_limit_bytes=VMEM_LIMIT_BYTES)


def _tile(n, cap, mult):
    best = None
    t = mult
    while t <= min(n, cap):
        if n % t == 0:
            best = t
        t += mult
    return best if best is not None else n


def _bdot(a, b, dims):
    return lax.dot_general(a.astype(bf16), b.astype(bf16), (dims, ((), ())), preferred_element_type=f32)


def _running_sum(x, reverse):
    s = x.shape[0]
    t = lax.broadcasted_iota(jnp.int32, x.shape, 0)
    d = 1
    while d < s:
        if reverse:
            x = x + jnp.where(t < s - d, pltpu.roll(x, s - d, 0), 0.0)
        else:
            x = x + jnp.where(t >= d, pltpu.roll(x, d, 0), 0.0)
        d *= 2
    return x


@jax.custom_vjp
def _cumsum0(x):
    return _running_sum(x, False)


def _cumsum0_fwd(x):
    return _running_sum(x, False), None


def _cumsum0_bwd(_, g):
    return (_running_sum(g, True),)


_cumsum0.defvjp(_cumsum0_fwd, _cumsum0_bwd)


@jax.custom_vjp
def _bnn(a, b):
    return _bdot(a, b, NN)


def _bnn_fwd(a, b):
    return _bdot(a, b, NN), (a, b)


def _bnn_bwd(res, g):
    a, b = res
    return _bdot(g, b, NT), _bdot(a, g, TN)


_bnn.defvjp(_bnn_fwd, _bnn_bwd)


@jax.custom_vjp
def _bnt(a, b):
    return _bdot(a, b, NT)


def _bnt_fwd(a, b):
    return _bdot(a, b, NT), (a, b)


def _bnt_bwd(res, g):
    a, b = res
    return _bdot(g, b, NN), _bdot(g, a, TN)


_bnt.defvjp(_bnt_fwd, _bnt_bwd)


@jax.custom_vjp
def _btn(a, b):
    return _bdot(a, b, TN)


def _btn_fwd(a, b):
    return _bdot(a, b, TN), (a, b)


def _btn_bwd(res, g):
    a, b = res
    return _bdot(b, g, NT), _bdot(a, g, NN)


_btn.defvjp(_btn_fwd, _btn_bwd)

RAW_DOTS = (lambda a, b: _bdot(a, b, NN), lambda a, b: _bdot(a, b, NT), lambda a, b: _bdot(a, b, TN),
            lambda x: _running_sum(x, False))
VJP_DOTS = (_bnn, _bnt, _btn, _cumsum0)


def _layer_norm(z, g, b):
    mu = jnp.mean(z, -1, keepdims=True)
    d = z - mu
    var = jnp.mean(d * d, -1, keepdims=True)
    return d * lax.rsqrt(var + LN_EPS) * g + b


def _silu(x):
    return x * jax.nn.sigmoid(x)


def mm_nn(a, w, add=None, add_scale=1.0, *, name):
    m, k = a.shape
    n = w.shape[1]
    tm = _tile(m, 512, SUBLANES)
    tn = _tile(n, 1024, LANES)

    def body(*refs):
        if add is None:
            a_ref, w_ref, o_ref = refs
            o_ref[...] = _bdot(a_ref[...], w_ref[...], NN)
        else:
            a_ref, w_ref, add_ref, o_ref = refs
            o_ref[...] = _bdot(a_ref[...], w_ref[...], NN) + add_scale * add_ref[...]

    in_specs = [pl.BlockSpec((tm, k), lambda i, j: (i, 0)), pl.BlockSpec((k, tn), lambda i, j: (0, j))]
    args = [a, w]
    if add is not None:
        in_specs.append(pl.BlockSpec((tm, tn), lambda i, j: (i, j)))
        args.append(add)
    return pl.pallas_call(
        body, grid=(m // tm, n // tn), in_specs=in_specs,
        out_specs=pl.BlockSpec((tm, tn), lambda i, j: (i, j)),
        out_shape=jax.ShapeDtypeStruct((m, n), f32),
        compiler_params=_params(("parallel", "parallel")), name=name,
    )(*args)


def mm_tn(a, b, *, name):
    m, k = a.shape
    n = b.shape[1]
    tm = _tile(m, 1024, 2 * SUBLANES)
    tn = _tile(n, 1024, LANES)

    def body(a_ref, b_ref, o_ref):
        part = _bdot(a_ref[...], b_ref[...], TN)

        @pl.when(pl.program_id(1) == 0)
        def _():
            o_ref[...] = part

        @pl.when(pl.program_id(1) > 0)
        def _():
            o_ref[...] += part

    return pl.pallas_call(
        body, grid=(n // tn, m // tm),
        in_specs=[pl.BlockSpec((tm, k), lambda j, i: (i, 0)), pl.BlockSpec((tm, tn), lambda j, i: (i, j))],
        out_specs=pl.BlockSpec((k, tn), lambda j, i: (0, j)),
        out_shape=jax.ShapeDtypeStruct((k, n), f32),
        compiler_params=_params(("parallel", "arbitrary")), name=name,
    )(a, b)


def proj_ln(a_list, w_list, xres, g, b, *, name):
    t, d = xres.shape
    tm = _tile(t, 256, 2 * SUBLANES)
    na = len(a_list)

    def body(*refs):
        a_refs, w_refs = refs[:na], refs[na:2 * na]
        x_ref, g_ref, b_ref, y_ref, z_ref, yb_ref = refs[2 * na:]
        z = DN_ALPHA * x_ref[...]
        for a_ref, w_ref in zip(a_refs, w_refs):
            z = z + _bdot(a_ref[...], w_ref[...], NN)
        z_ref[...] = z
        y = _layer_norm(z, g_ref[...], b_ref[...])
        y_ref[...] = y
        yb_ref[...] = y.astype(bf16)

    in_specs = [pl.BlockSpec((tm, a.shape[1]), lambda i: (i, 0)) for a in a_list]
    in_specs += [pl.BlockSpec(w.shape, lambda i: (0, 0)) for w in w_list]
    in_specs += [pl.BlockSpec((tm, d), lambda i: (i, 0)), pl.BlockSpec((1, d), lambda i: (0, 0)),
                 pl.BlockSpec((1, d), lambda i: (0, 0))]
    return pl.pallas_call(
        body, grid=(t // tm,), in_specs=in_specs,
        out_specs=[pl.BlockSpec((tm, d), lambda i: (i, 0))] * 3,
        out_shape=[jax.ShapeDtypeStruct((t, d), f32)] * 2 + [jax.ShapeDtypeStruct((t, d), bf16)],
        compiler_params=_params(("parallel",)), name=name,
    )(*a_list, *w_list, xres, g, b)


def ln_bwd(z, dy, g, *, name):
    t, d = z.shape
    tm = _tile(t, 512, SUBLANES)

    def body(z_ref, dy_ref, g_ref, dz_ref, dzb_ref, dg_ref, db_ref):
        zz = z_ref[...]
        dy_ = dy_ref[...]
        mu = jnp.mean(zz, -1, keepdims=True)
        dd = zz - mu
        var = jnp.mean(dd * dd, -1, keepdims=True)
        rstd = lax.rsqrt(var + LN_EPS)
        xhat = dd * rstd
        dxh = dy_ * g_ref[...]
        dz = rstd * (dxh - jnp.mean(dxh, -1, keepdims=True) - xhat * jnp.mean(dxh * xhat, -1, keepdims=True))
        dz_ref[...] = dz
        dzb_ref[...] = dz.astype(bf16)
        pg = jnp.sum(dy_ * xhat, 0, keepdims=True)
        pb = jnp.sum(dy_, 0, keepdims=True)

        @pl.when(pl.program_id(0) == 0)
        def _():
            dg_ref[...] = pg
            db_ref[...] = pb

        @pl.when(pl.program_id(0) > 0)
        def _():
            dg_ref[...] += pg
            db_ref[...] += pb

    row = pl.BlockSpec((tm, d), lambda i: (i, 0))
    vec = pl.BlockSpec((1, d), lambda i: (0, 0))
    return pl.pallas_call(
        body, grid=(t // tm,), in_specs=[row, row, vec], out_specs=[row, row, vec, vec],
        out_shape=[jax.ShapeDtypeStruct((t, d), f32), jax.ShapeDtypeStruct((t, d), bf16),
                   jax.ShapeDtypeStruct((1, d), f32), jax.ShapeDtypeStruct((1, d), f32)],
        compiler_params=_params(("arbitrary",)), name=name,
    )(z, dy, g)


def loss_head(y, target, *, name):
    t, d = y.shape
    tm = _tile(t, 512, SUBLANES)

    def body(y_ref, t_ref, dy_ref, sq_ref):
        e = y_ref[...] - t_ref[...]
        dy_ref[...] = e * (1.0 / d)
        part = jnp.sum(e * e, 0, keepdims=True)

        @pl.when(pl.program_id(0) == 0)
        def _():
            sq_ref[...] = part

        @pl.when(pl.program_id(0) > 0)
        def _():
            sq_ref[...] += part

    row = pl.BlockSpec((tm, d), lambda i: (i, 0))
    vec = pl.BlockSpec((1, d), lambda i: (0, 0))
    return pl.pallas_call(
        body, grid=(t // tm,), in_specs=[row, row], out_specs=[row, vec],
        out_shape=[jax.ShapeDtypeStruct((t, d), f32), jax.ShapeDtypeStruct((1, d), f32)],
        compiler_params=_params(("arbitrary",)), name=name,
    )(y, target)


FFN_COL_BLOCK = 256
FFN_ROWS = 1024


def _lane_blocks(n):
    return [slice(s, min(s + FFN_COL_BLOCK, n)) for s in range(0, n, FFN_COL_BLOCK)]


def ffn_fwd(x, wg, wu, wd, g, b, *, name):
    t, d = x.shape
    nf, _, tf = wg.shape
    tm = _tile(t, FFN_ROWS, SUBLANES)

    def body(x_ref, wg_ref, wu_ref, wd_ref, g_ref, b_ref, y_ref, z_ref, yb_ref, acc_ref):
        f = pl.program_id(1)
        xb = x_ref[...].astype(bf16)
        part, pending = None, None
        for cols in _lane_blocks(tf):
            gate_up = (_bdot(xb, wg_ref[:, cols], NN), _bdot(xb, wu_ref[:, cols], NN), cols)
            if pending is not None:
                down = _bdot(_silu(pending[0]) * pending[1], wd_ref[pending[2], :], NN)
                part = down if part is None else part + down
            pending = gate_up
        down = _bdot(_silu(pending[0]) * pending[1], wd_ref[pending[2], :], NN)
        part = down if part is None else part + down

        @pl.when(f == 0)
        def _():
            acc_ref[...] = part

        @pl.when(f > 0)
        def _():
            acc_ref[...] += part

        @pl.when(f == nf - 1)
        def _():
            z = DN_ALPHA * x_ref[...] + 0.5 * acc_ref[...]
            z_ref[...] = z
            y = _layer_norm(z, g_ref[...], b_ref[...])
            y_ref[...] = y
            yb_ref[...] = y.astype(bf16)

    row = pl.BlockSpec((tm, d), lambda i, j: (i, 0))
    vec = pl.BlockSpec((1, d), lambda i, j: (0, 0))
    wcol = pl.BlockSpec((None, d, tf), lambda i, j: (j, 0, 0))
    wrow = pl.BlockSpec((None, tf, d), lambda i, j: (j, 0, 0))
    return pl.pallas_call(
        body, grid=(t // tm, nf),
        in_specs=[row, wcol, wcol, wrow, vec, vec],
        out_specs=[row, row, row],
        out_shape=[jax.ShapeDtypeStruct((t, d), f32)] * 2 + [jax.ShapeDtypeStruct((t, d), bf16)],
        scratch_shapes=[pltpu.VMEM((tm, d), f32)],
        compiler_params=_params(("parallel", "arbitrary")), name=name,
    )(x, wg, wu, wd, g, b)


def ffn_bwd_weights(xb, dzb, wg, wu, wd, *, name):
    t, d = xb.shape
    nf, _, tf = wg.shape
    tm = _tile(t, FFN_ROWS, SUBLANES)
    nt = t // tm

    def body(x_ref, dz_ref, wg_ref, wu_ref, wd_ref, dgate_ref, dup_ref, owg_ref, owu_ref, owd_ref,
             dwg_ref, dwu_ref, dwd_ref):
        x = x_ref[...]
        dzh = dz_ref[...] * 0.5

        def first_half(cols):
            return _bdot(x, wg_ref[:, cols], NN), _bdot(x, wu_ref[:, cols], NN), _bdot(dzh, wd_ref[cols, :], NT), cols

        def second_half(gate, up, dh, cols):
            sg = jax.nn.sigmoid(gate)
            s = gate * sg
            dup = (dh * s).astype(bf16)
            dgate = (dh * up * (sg * (1.0 + gate * (1.0 - sg)))).astype(bf16)
            dgate_ref[:, cols] = dgate
            dup_ref[:, cols] = dup
            return _bdot(x, dgate, TN), _bdot(x, dup, TN), _bdot(s * up, dzh, TN), cols

        parts, pending = [], None
        for cols in _lane_blocks(tf):
            nxt = first_half(cols)
            if pending is not None:
                parts.append(second_half(*pending))
            pending = nxt
        parts.append(second_half(*pending))

        @pl.when(pl.program_id(1) == 0)
        def _():
            for pwg, pwu, pwd, cols in parts:
                dwg_ref[:, cols] = pwg
                dwu_ref[:, cols] = pwu
                dwd_ref[cols, :] = pwd

        @pl.when(pl.program_id(1) > 0)
        def _():
            for pwg, pwu, pwd, cols in parts:
                dwg_ref[:, cols] += pwg
                dwu_ref[:, cols] += pwu
                dwd_ref[cols, :] += pwd

        @pl.when(pl.program_id(1) == nt - 1)
        def _():
            owg_ref[...] = dwg_ref[...].astype(bf16)
            owu_ref[...] = dwu_ref[...].astype(bf16)
            owd_ref[...] = dwd_ref[...].astype(bf16)

    row = pl.BlockSpec((tm, d), lambda j, i: (i, 0))
    wcol = pl.BlockSpec((None, d, tf), lambda j, i: (j, 0, 0))
    wrow = pl.BlockSpec((None, tf, d), lambda j, i: (j, 0, 0))
    act = pl.BlockSpec((None, tm, tf), lambda j, i: (j, i, 0))
    return pl.pallas_call(
        body, grid=(nf, nt), in_specs=[row, row, wcol, wcol, wrow], out_specs=[act, act, wcol, wcol, wrow],
        out_shape=[jax.ShapeDtypeStruct((nf, t, tf), bf16), jax.ShapeDtypeStruct((nf, t, tf), bf16),
                   jax.ShapeDtypeStruct((nf, d, tf), bf16), jax.ShapeDtypeStruct((nf, d, tf), bf16),
                   jax.ShapeDtypeStruct((nf, tf, d), bf16)],
        scratch_shapes=[pltpu.VMEM((d, tf), f32), pltpu.VMEM((d, tf), f32), pltpu.VMEM((tf, d), f32)],
        compiler_params=_params(("parallel", "arbitrary")), name=name,
    )(xb, dzb, wg, wu, wd)


def ffn_bwd_input(dgate, dup, wg, wu, dz, *, name):
    nf, t, tf = dgate.shape
    d = wg.shape[1]
    tm = _tile(t, FFN_ROWS // 2, SUBLANES)

    def body(dg_ref, du_ref, wg_ref, wu_ref, dz_ref, dx_ref):
        acc = DN_ALPHA * dz_ref[...]
        for j in range(nf):
            acc = acc + _bdot(dg_ref[j], wg_ref[j], NT) + _bdot(du_ref[j], wu_ref[j], NT)
        dx_ref[...] = acc

    act = pl.BlockSpec((nf, tm, tf), lambda i: (0, i, 0))
    wsp = pl.BlockSpec((nf, d, tf), lambda i: (0, 0, 0))
    row = pl.BlockSpec((tm, d), lambda i: (i, 0))
    return pl.pallas_call(
        body, grid=(t // tm,), in_specs=[act, act, wsp, wsp, row], out_specs=row,
        out_shape=jax.ShapeDtypeStruct((t, d), f32),
        compiler_params=_params(("parallel",)), name=name,
    )(dgate, dup, wg, wu, dz)


def ple_fwd(x, p, wg, bg, wp, *, name):
    t, d = x.shape
    dp = p.shape[1]
    tm = _tile(t, 512, 2 * SUBLANES)

    def body(x_ref, p_ref, wg_ref, bg_ref, wp_ref, o_ref, ob_ref):
        x_ = x_ref[...]
        gate = jax.nn.sigmoid(_bdot(x_, wg_ref[...], NN) + bg_ref[...])
        out = x_ + gate * _bdot(p_ref[...], wp_ref[...], NN)
        o_ref[...] = out
        ob_ref[...] = out.astype(bf16)

    row = pl.BlockSpec((tm, d), lambda i: (i, 0))
    return pl.pallas_call(
        body, grid=(t // tm,),
        in_specs=[row, pl.BlockSpec((tm, dp), lambda i: (i, 0)), pl.BlockSpec((d, d), lambda i: (0, 0)),
                  pl.BlockSpec((1, d), lambda i: (0, 0)), pl.BlockSpec((dp, d), lambda i: (0, 0))],
        out_specs=[row, row], out_shape=[jax.ShapeDtypeStruct((t, d), f32), jax.ShapeDtypeStruct((t, d), bf16)],
        compiler_params=_params(("parallel",)), name=name,
    )(x, p, wg, bg, wp)


def ple_bwd(x, p, dy, wg, wgt, bg, wp, *, name):
    t, d = x.shape
    dp = p.shape[1]
    tm = _tile(t, 512, SUBLANES)

    def body(x_ref, p_ref, dy_ref, wg_ref, wgt_ref, bg_ref, wp_ref, dx_ref, dwg_ref, dbg_ref, dwp_ref):
        x_ = x_ref[...]
        dy_ = dy_ref[...]
        s = jax.nn.sigmoid(_bdot(x_, wg_ref[...], NN) + bg_ref[...])
        e = _bdot(p_ref[...], wp_ref[...], NN)
        da = dy_ * e * s * (1.0 - s)
        de = dy_ * s
        dx_ref[...] = dy_ + _bdot(da, wgt_ref[...], NN)
        pwg = _bdot(x_, da, TN)
        pbg = jnp.sum(da, 0, keepdims=True)
        pwp = _bdot(p_ref[...], de, TN)

        @pl.when(pl.program_id(0) == 0)
        def _():
            dwg_ref[...] = pwg
            dbg_ref[...] = pbg
            dwp_ref[...] = pwp

        @pl.when(pl.program_id(0) > 0)
        def _():
            dwg_ref[...] += pwg
            dbg_ref[...] += pbg
            dwp_ref[...] += pwp

    row = pl.BlockSpec((tm, d), lambda i: (i, 0))
    full = lambda shape: pl.BlockSpec(shape, lambda i: (0, 0))
    return pl.pallas_call(
        body, grid=(t // tm,),
        in_specs=[row, pl.BlockSpec((tm, dp), lambda i: (i, 0)), row, full((d, d)), full((d, d)), full((1, d)),
                  full((dp, d))],
        out_specs=[row, full((d, d)), full((1, d)), full((dp, d))],
        out_shape=[jax.ShapeDtypeStruct((t, d), f32), jax.ShapeDtypeStruct((d, d), f32),
                   jax.ShapeDtypeStruct((1, d), f32), jax.ShapeDtypeStruct((dp, d), f32)],
        compiler_params=_params(("arbitrary",)), name=name,
    )(x, p, dy, wg, wgt, bg, wp)


def _conv_taps(xpad_ref, w_ref, s):
    acc = w_ref[0:1, :] * xpad_ref[SUBLANES - 3:SUBLANES - 3 + s, :]
    for j in range(1, 4):
        acc = acc + w_ref[j:j + 1, :] * xpad_ref[SUBLANES - 3 + j:SUBLANES - 3 + j + s, :]
    return acc


def conv_fwd(x, w, bias, act, nb, *, name):
    t, c = x.shape
    s = t // nb
    cw = GROUP_W

    def body(x_ref, w_ref, b_ref, y_ref, xpad):
        xpad[0:SUBLANES, :] = jnp.zeros((SUBLANES, cw), f32)
        xpad[SUBLANES:, :] = x_ref[...]
        acc = _conv_taps(xpad, w_ref, s) + b_ref[...]
        y_ref[...] = _silu(acc) if act else acc

    slab = pl.BlockSpec((s, cw), lambda b, g: (b, g))
    return pl.pallas_call(
        body, grid=(nb, c // cw),
        in_specs=[slab, pl.BlockSpec((4, cw), lambda b, g: (0, g)), pl.BlockSpec((1, cw), lambda b, g: (0, g))],
        out_specs=slab, out_shape=jax.ShapeDtypeStruct((t, c), f32),
        scratch_shapes=[pltpu.VMEM((s + SUBLANES, cw), f32)],
        compiler_params=_params(("parallel", "parallel")), name=name,
    )(x, w, bias)


def conv_bwd(x, w, bias, dy, act, nb, *, name):
    t, c = x.shape
    s = t // nb
    cw = GROUP_W

    def body(x_ref, w_ref, b_ref, dy_ref, dx_ref, dw_ref, db_ref, xpad, dpad):
        xpad[0:SUBLANES, :] = jnp.zeros((SUBLANES, cw), f32)
        xpad[SUBLANES:, :] = x_ref[...]
        dacc = dy_ref[...]
        if act:
            acc = _conv_taps(xpad, w_ref, s) + b_ref[...]
            sg = jax.nn.sigmoid(acc)
            dacc = dacc * (sg * (1.0 + acc * (1.0 - sg)))
        dpad[0:s, :] = dacc
        dpad[s:, :] = jnp.zeros((SUBLANES, cw), f32)
        dx = w_ref[0:1, :] * dpad[3:3 + s, :]
        for j in range(1, 4):
            dx = dx + w_ref[j:j + 1, :] * dpad[3 - j:3 - j + s, :]
        dx_ref[...] = dx
        first = pl.program_id(1) == 0
        for j in range(4):
            pw = jnp.sum(dacc * xpad[SUBLANES - 3 + j:SUBLANES - 3 + j + s, :], 0, keepdims=True)

            @pl.when(first)
            def _():
                dw_ref[j:j + 1, :] = pw

            @pl.when(jnp.logical_not(first))
            def _():
                dw_ref[j:j + 1, :] += pw

        pb = jnp.sum(dacc, 0, keepdims=True)

        @pl.when(first)
        def _():
            db_ref[...] = pb

        @pl.when(jnp.logical_not(first))
        def _():
            db_ref[...] += pb

    slab = pl.BlockSpec((s, cw), lambda g, b: (b, g))
    wsp = pl.BlockSpec((4, cw), lambda g, b: (0, g))
    bsp = pl.BlockSpec((1, cw), lambda g, b: (0, g))
    return pl.pallas_call(
        body, grid=(c // cw, nb), in_specs=[slab, wsp, bsp, slab], out_specs=[slab, wsp, bsp],
        out_shape=[jax.ShapeDtypeStruct((t, c), f32), jax.ShapeDtypeStruct((4, c), f32),
                   jax.ShapeDtypeStruct((1, c), f32)],
        scratch_shapes=[pltpu.VMEM((s + SUBLANES, cw), f32), pltpu.VMEM((s + SUBLANES, cw), f32)],
        compiler_params=_params(("parallel", "arbitrary")), name=name,
    )(x, w, bias, dy)


def _each(f, *lists):
    return [f(*a) for a in zip(*lists)]


def _attn_heads(qs, kbs, vbs, sinks, valid, dist, dots):
    nn, nt = dots[:2]
    kv = [h // A_GROUP for h in range(A_HEADS)]
    scs = [nt(qs[h], kbs[kv[h]]) for h in range(A_HEADS)]
    prs = []
    for h in range(A_HEADS):
        sc = scs[h] * (A_HEAD_DIM ** -0.5) - 2.0 ** -(h + 1) * dist
        sc = jnp.where(valid, sc, NEG)
        m = lax.stop_gradient(jnp.maximum(jnp.max(sc, -1, keepdims=True), sinks[h]))
        pr = jnp.exp(sc - m)
        den = jnp.sum(pr, -1, keepdims=True) + jnp.exp(sinks[h] - m)
        prs.append(pr / den)
    return [nn(prs[h], vbs[kv[h]]) for h in range(A_HEADS)]


A_Q_ROWS = 2 * CHUNK


def _attn_band_consts(r0):
    band = A_WINDOW + A_Q_ROWS
    qi = lax.broadcasted_iota(jnp.int32, (A_Q_ROWS, band), 0)
    kj = lax.broadcasted_iota(jnp.int32, (A_Q_ROWS, band), 1)
    dist = jnp.abs(qi + A_WINDOW - kj).astype(f32)
    qc, kc = qi // CHUNK, kj // CHUNK
    valid = ((kj + r0) >= A_WINDOW) & (kc >= qc) & (kc <= qc + A_WINDOW // CHUNK)
    return dist, valid


def attn_fwd(qkv, sinks, nb, *, name):
    t = qkv.shape[0]
    s = t // nb
    band = A_WINDOW + A_Q_ROWS
    hd = A_HEAD_DIM

    def body(qkv_ref, sink_ref, o_ref, kvpad):
        kvpad[0:A_WINDOW, :] = jnp.zeros((A_WINDOW, 2 * A_KV_WIDTH), f32)
        kvpad[A_WINDOW:, :] = qkv_ref[:, A_WIDTH:]

        def chunk(n, carry):
            r0 = pl.multiple_of(n * A_Q_ROWS, A_Q_ROWS)
            dist, valid = _attn_band_consts(r0)
            kbs = [kvpad[pl.ds(r0, band), kvh * hd:(kvh + 1) * hd] for kvh in range(A_KV_HEADS)]
            vbs = [kvpad[pl.ds(r0, band), A_KV_WIDTH + kvh * hd:A_KV_WIDTH + (kvh + 1) * hd]
                   for kvh in range(A_KV_HEADS)]
            qs = [qkv_ref[pl.ds(r0, A_Q_ROWS), h * hd:(h + 1) * hd] for h in range(A_HEADS)]
            outs = _attn_heads(qs, kbs, vbs, [sink_ref[:, h:h + 1] for h in range(A_HEADS)], valid, dist, RAW_DOTS)
            for h in range(A_HEADS):
                o_ref[pl.ds(r0, A_Q_ROWS), h * hd:(h + 1) * hd] = outs[h]
            return carry

        lax.fori_loop(0, s // A_Q_ROWS, chunk, 0)

    return pl.pallas_call(
        body, grid=(nb,),
        in_specs=[pl.BlockSpec((s, A_WIDTH + 2 * A_KV_WIDTH), lambda b: (b, 0)),
                  pl.BlockSpec((1, A_HEADS), lambda b: (0, 0))],
        out_specs=pl.BlockSpec((s, A_WIDTH), lambda b: (b, 0)),
        out_shape=jax.ShapeDtypeStruct((t, A_WIDTH), f32),
        scratch_shapes=[pltpu.VMEM((s + A_WINDOW, 2 * A_KV_WIDTH), f32)],
        compiler_params=_params(("parallel",)), name=name,
    )(qkv, sinks)


def attn_bwd(qkv, sinks, do, nb, *, name):
    t = qkv.shape[0]
    s = t // nb
    band = A_WINDOW + A_Q_ROWS
    hd = A_HEAD_DIM
    kvw = 2 * A_KV_WIDTH

    def body(qkv_ref, sink_ref, do_ref, dqkv_ref, dsink_ref, kvpad, dkvpad):
        kvpad[0:A_WINDOW, :] = jnp.zeros((A_WINDOW, kvw), f32)
        kvpad[A_WINDOW:, :] = qkv_ref[:, A_WIDTH:]
        dkvpad[...] = jnp.zeros((s + A_WINDOW, kvw), f32)

        def chunk(n, dsinks):
            r0 = pl.multiple_of(n * A_Q_ROWS, A_Q_ROWS)
            dist, valid = _attn_band_consts(r0)
            ksl = [slice(kvh * hd, (kvh + 1) * hd) for kvh in range(A_KV_HEADS)]
            vsl = [slice(A_KV_WIDTH + kvh * hd, A_KV_WIDTH + (kvh + 1) * hd) for kvh in range(A_KV_HEADS)]
            kbs = [kvpad[pl.ds(r0, band), sl] for sl in ksl]
            vbs = [kvpad[pl.ds(r0, band), sl] for sl in vsl]
            dkbs = [dkvpad[pl.ds(r0, band), sl] for sl in ksl]
            dvbs = [dkvpad[pl.ds(r0, band), sl] for sl in vsl]
            qs = [qkv_ref[pl.ds(r0, A_Q_ROWS), h * hd:(h + 1) * hd] for h in range(A_HEADS)]
            dos = [do_ref[pl.ds(r0, A_Q_ROWS), h * hd:(h + 1) * hd] for h in range(A_HEADS)]
            fn = functools.partial(_attn_heads, valid=valid, dist=dist, dots=VJP_DOTS)
            _, vjp = jax.vjp(fn, qs, kbs, vbs, [sink_ref[:, h:h + 1] for h in range(A_HEADS)])
            dqs, dks, dvs, dss = vjp(dos)
            for h in range(A_HEADS):
                dqkv_ref[pl.ds(r0, A_Q_ROWS), h * hd:(h + 1) * hd] = dqs[h]
            for kvh in range(A_KV_HEADS):
                dkvpad[pl.ds(r0, band), ksl[kvh]] = dkbs[kvh] + dks[kvh]
                dkvpad[pl.ds(r0, band), vsl[kvh]] = dvbs[kvh] + dvs[kvh]
            return tuple(dsinks[h] + dss[h] for h in range(A_HEADS))

        dsinks = lax.fori_loop(0, s // A_Q_ROWS, chunk, tuple(jnp.zeros((1, 1), f32) for _ in range(A_HEADS)))
        dqkv_ref[:, A_WIDTH:] = dkvpad[A_WINDOW:, :]
        first = pl.program_id(0) == 0
        for h in range(A_HEADS):
            @pl.when(first)
            def _():
                dsink_ref[:, h:h + 1] = dsinks[h]

            @pl.when(jnp.logical_not(first))
            def _():
                dsink_ref[:, h:h + 1] += dsinks[h]

    wq = A_WIDTH + kvw
    return pl.pallas_call(
        body, grid=(nb,),
        in_specs=[pl.BlockSpec((s, wq), lambda b: (b, 0)), pl.BlockSpec((1, A_HEADS), lambda b: (0, 0)),
                  pl.BlockSpec((s, A_WIDTH), lambda b: (b, 0))],
        out_specs=[pl.BlockSpec((s, wq), lambda b: (b, 0)), pl.BlockSpec((1, A_HEADS), lambda b: (0, 0))],
        out_shape=[jax.ShapeDtypeStruct((t, wq), f32), jax.ShapeDtypeStruct((1, A_HEADS), f32)],
        scratch_shapes=[pltpu.VMEM((s + A_WINDOW, kvw), f32), pltpu.VMEM((s + A_WINDOW, kvw), f32)],
        compiler_params=_params(("arbitrary",)), name=name,
    )(qkv, sinks, do)


def _rg_gates(xc, wa, wx, ba, bx, lam, nn):
    r = jax.nn.sigmoid(nn(xc, wa) + ba)
    i = jax.nn.sigmoid(nn(xc, wx) + bx)
    log_a = -RG_C * r * jax.nn.softplus(-lam)
    a = jnp.exp(log_a)
    mult = jnp.sqrt(-jnp.tanh(log_a) * (jnp.exp(2.0 * log_a) + 1.0))
    return a, mult * (i * xc)


def _linear_scan(a, u, reverse):
    s = a.shape[0]
    t = lax.broadcasted_iota(jnp.int32, a.shape, 0)
    d = 1
    while d < s:
        if reverse:
            keep = t < s - d
            shift = s - d
        else:
            keep = t >= d
            shift = d
        us = jnp.where(keep, pltpu.roll(u, shift, 0), 0.0)
        as_ = jnp.where(keep, pltpu.roll(a, shift, 0), 1.0)
        u = u + a * us
        a = a * as_
        d *= 2
    return u


def rglru_fwd(xc, bg, wa, wx, ba, bx, lam, nb, *, name):
    t, c = xc.shape
    s = t // nb
    cw = GROUP_W

    def body(xc_ref, bg_ref, wa_ref, wx_ref, ba_ref, bx_ref, lam_ref, y_ref, h_ref):
        a, u = _rg_gates(xc_ref[...], wa_ref[...], wx_ref[...], ba_ref[...], bx_ref[...], lam_ref[...], RAW_DOTS[0])
        h = _linear_scan(a, u, False)
        h_ref[...] = h
        y_ref[...] = h * jax.nn.gelu(bg_ref[...])

    slab = pl.BlockSpec((s, cw), lambda b, g: (b, g))
    wsp = pl.BlockSpec((None, cw, cw), lambda b, g: (g, 0, 0))
    vec = pl.BlockSpec((1, cw), lambda b, g: (0, g))
    return pl.pallas_call(
        body, grid=(nb, c // cw), in_specs=[slab, slab, wsp, wsp, vec, vec, vec], out_specs=[slab, slab],
        out_shape=[jax.ShapeDtypeStruct((t, c), f32)] * 2,
        compiler_params=_params(("parallel", "parallel")), name=name,
    )(xc, bg, wa, wx, ba, bx, lam)


def rglru_bwd(xc, bg, h, dy, wa, wx, ba, bx, lam, nb, *, name):
    t, c = xc.shape
    s = t // nb
    cw = GROUP_W

    def body(xc_ref, bg_ref, h_ref, dy_ref, wa_ref, wx_ref, ba_ref, bx_ref, lam_ref,
             dxc_ref, dbg_ref, dwa_ref, dwx_ref, dba_ref, dbx_ref, dlam_ref):
        h = h_ref[...]
        dy_ = dy_ref[...]
        gel, gel_vjp = jax.vjp(jax.nn.gelu, bg_ref[...])
        dbg_ref[...] = gel_vjp(dy_ * h)[0]
        dh = dy_ * gel
        gates = functools.partial(_rg_gates, nn=_bnn)
        (a, _), gates_vjp = jax.vjp(gates, xc_ref[...], wa_ref[...], wx_ref[...], ba_ref[...], bx_ref[...],
                                    lam_ref[...])
        ti = lax.broadcasted_iota(jnp.int32, a.shape, 0)
        a_next = jnp.where(ti < s - 1, pltpu.roll(a, s - 1, 0), 0.0)
        lam_t = _linear_scan(a_next, dh, True)
        h_prev = jnp.where(ti >= 1, pltpu.roll(h, 1, 0), 0.0)
        dxc, dwa, dwx, dba, dbx, dlam = gates_vjp((lam_t * h_prev, lam_t))
        dxc_ref[...] = dxc
        first = pl.program_id(1) == 0

        @pl.when(first)
        def _():
            dwa_ref[...] = dwa
            dwx_ref[...] = dwx
            dba_ref[...] = dba
            dbx_ref[...] = dbx
            dlam_ref[...] = dlam

        @pl.when(jnp.logical_not(first))
        def _():
            dwa_ref[...] += dwa
            dwx_ref[...] += dwx
            dba_ref[...] += dba
            dbx_ref[...] += dbx
            dlam_ref[...] += dlam

    slab = pl.BlockSpec((s, cw), lambda g, b: (b, g))
    wsp = pl.BlockSpec((None, cw, cw), lambda g, b: (g, 0, 0))
    vec = pl.BlockSpec((1, cw), lambda g, b: (0, g))
    ng = c // cw
    return pl.pallas_call(
        body, grid=(ng, nb), in_specs=[slab, slab, slab, slab, wsp, wsp, vec, vec, vec],
        out_specs=[slab, slab, wsp, wsp, vec, vec, vec],
        out_shape=[jax.ShapeDtypeStruct((t, c), f32), jax.ShapeDtypeStruct((t, c), f32),
                   jax.ShapeDtypeStruct((ng, cw, cw), f32), jax.ShapeDtypeStruct((ng, cw, cw), f32),
                   jax.ShapeDtypeStruct((1, c), f32), jax.ShapeDtypeStruct((1, c), f32),
                   jax.ShapeDtypeStruct((1, c), f32)],
        compiler_params=_params(("parallel", "arbitrary")), name=name,
    )(xc, bg, h, dy, wa, wx, ba, bx, lam)


def _gdn_chunks_prep(qs, ks, vs, bls, als, a_log, dt_b, dots):
    nn, nt, csum = dots[0], dots[1], dots[3]
    hd = C_HEAD_DIM
    ri = lax.broadcasted_iota(jnp.int32, (CHUNK, CHUNK), 0)
    ci = lax.broadcasted_iota(jnp.int32, (CHUNK, CHUNK), 1)
    tril = ri >= ci
    strict = ri > ci
    eye = (ri == ci).astype(f32)
    qn = [q * lax.rsqrt(jnp.sum(q * q, -1, keepdims=True) + NORM_EPS) * (hd ** -0.5) for q in qs]
    kn = [k * lax.rsqrt(jnp.sum(k * k, -1, keepdims=True) + NORM_EPS) for k in ks]
    beta = [jax.nn.sigmoid(bl) for bl in bls]
    g = [-jnp.exp(a_log) * jax.nn.softplus(al + dt_b) for al in als]
    gc_sq = [csum(jnp.broadcast_to(g_, (CHUNK, CHUNK))) for g_ in g]
    gc = [csum(jnp.broadcast_to(g_, (CHUNK, hd))) for g_ in g]
    decay = [jnp.where(tril, jnp.exp(jnp.where(tril, s - s.T, 0.0)), 0.0) for s in gc_sq]
    kb = _each(jnp.multiply, kn, beta)
    kk = _each(nt, kb, kn)
    pw = [-jnp.where(strict, a * d, 0.0) for a, d in zip(kk, decay)]
    inv = [eye + p_ for p_ in pw]
    for _ in range(5):
        pw = _each(nn, pw, pw)
        inv = _each(jnp.add, inv, _each(nn, inv, pw))
    egc = [jnp.exp(c_) for c_ in gc]
    u = _each(nn, inv, _each(jnp.multiply, vs, beta))
    w = _each(nn, inv, _each(jnp.multiply, kb, egc))
    attn = _each(jnp.multiply, _each(nt, qn, kn), decay)
    g_last = [jnp.sum(jnp.broadcast_to(g_, (CHUNK, hd)), 0, keepdims=True) for g_ in g]
    qg = _each(jnp.multiply, qn, egc)
    kdec = [k_ * jnp.exp(gl_ - c_) for k_, gl_, c_ in zip(kn, g_last, gc)]
    return [(qg[i], kdec[i], w[i], u[i], attn[i], jnp.exp(g_last[i])) for i in range(len(qs))]


def _gdn_heads_step(states, qgs, kdecs, ws, us, attns, gls, zs, ng, dots):
    nn, tn = dots[0], dots[2]
    v_new = _each(jnp.subtract, us, _each(nn, ws, states))
    o = _each(jnp.add, _each(nn, qgs, states), _each(nn, attns, v_new))
    new = [s * gl for s, gl in zip(states, gls)]
    new = _each(jnp.add, new, _each(tn, kdecs, v_new))
    y = [o_ * lax.rsqrt(jnp.mean(o_ * o_, -1, keepdims=True) + NORM_EPS) * ng * _silu(z) for o_, z in zip(o, zs)]
    return y, new


def _loop_unrolled(n, unroll, load, compute, store, init):
    u = unroll if n % unroll == 0 else 1

    def trip(i, carry):
        idx = [i * u + j for j in range(u)]
        loaded = [load(k) for k in idx]
        results = compute(loaded)
        for k, r in zip(idx, results):
            carry = store(k, r, carry)
        return carry

    return lax.fori_loop(0, n // u, trip, init)


def _pick_lane(x, lane):
    li = lax.broadcasted_iota(jnp.int32, x.shape, 1)
    return jnp.sum(jnp.where(li == lane, x, 0.0), 1, keepdims=True)


def _put_lane(col, lane, width):
    li = lax.broadcasted_iota(jnp.int32, (col.shape[0], width), 1)
    return jnp.where(li == lane, col, 0.0)


def _gdn_specs(s, nc):
    hd = C_HEAD_DIM
    head = lambda off: pl.BlockSpec((s, hd), lambda b, h, off=off: (b, off + h))
    attn = pl.BlockSpec((None, s, CHUNK), lambda b, h: (h, b, 0))
    gl = pl.BlockSpec((None, nc * SUBLANES, hd), lambda b, h: (h, b, 0))
    ba = pl.BlockSpec((s, LANES), lambda b, h: (b, 0))
    sc8 = pl.BlockSpec((1, C_HEADS), lambda b, h: (0, 0))
    return head, attn, gl, ba, sc8


def gdn_prep_fwd(qkv, ba, a_log, dt_b, nb, *, name):
    t = qkv.shape[0]
    s = t // nb
    nc = s // CHUNK
    hd = C_HEAD_DIM
    head, attn_sp, gl_sp, ba_sp, sc8 = _gdn_specs(s, nc)

    def body(q_ref, k_ref, v_ref, ba_ref, alog_ref, dtb_ref, qg_ref, kd_ref, w_ref, u_ref, at_ref, gl_ref):
        h = pl.program_id(1)
        a_log_h = _pick_lane(alog_ref[...], h)
        dt_b_h = _pick_lane(dtb_ref[...], h)

        def load(n):
            rows = pl.ds(pl.multiple_of(n * CHUNK, CHUNK), CHUNK)
            bav = ba_ref[rows, :]
            return q_ref[rows, :], k_ref[rows, :], v_ref[rows, :], _pick_lane(bav, h), _pick_lane(bav, C_HEADS + h)

        def compute(loaded):
            return _gdn_chunks_prep(*[list(x) for x in zip(*loaded)], a_log_h, dt_b_h, RAW_DOTS)

        def store(n, outs, carry):
            rows = pl.ds(pl.multiple_of(n * CHUNK, CHUNK), CHUNK)
            qg_ref[rows, :] = outs[0].astype(bf16)
            kd_ref[rows, :] = outs[1].astype(bf16)
            w_ref[rows, :] = outs[2].astype(bf16)
            u_ref[rows, :] = outs[3]
            at_ref[rows, :] = outs[4].astype(bf16)
            gl_ref[pl.ds(pl.multiple_of(n * SUBLANES, SUBLANES), SUBLANES), :] = jnp.broadcast_to(outs[5], (SUBLANES, hd))
            return carry

        _loop_unrolled(nc, PREP_FWD_UNROLL, load, compute, store, 0)

    big = jax.ShapeDtypeStruct((t, C_WIDTH), f32)
    bigb = jax.ShapeDtypeStruct((t, C_WIDTH), bf16)
    return pl.pallas_call(
        body, grid=(nb, C_HEADS),
        in_specs=[head(0), head(C_HEADS), head(2 * C_HEADS), ba_sp, sc8, sc8],
        out_specs=[head(0)] * 4 + [attn_sp, gl_sp],
        out_shape=[bigb, bigb, bigb, big, jax.ShapeDtypeStruct((C_HEADS, t, CHUNK), bf16),
                               jax.ShapeDtypeStruct((C_HEADS, nb * nc * SUBLANES, hd), f32)],
        compiler_params=_params(("parallel", "parallel")), name=name,
    )(qkv, qkv, qkv, ba, a_log, dt_b)


def gdn_prep_bwd(qkv, ba, a_log, dt_b, cts, nb, *, name):
    t = qkv.shape[0]
    s = t // nb
    nc = s // CHUNK
    hd = C_HEAD_DIM
    head, attn_sp, gl_sp, ba_sp, sc8 = _gdn_specs(s, nc)

    def body(q_ref, k_ref, v_ref, ba_ref, alog_ref, dtb_ref, cqg, ckd, cw_, cu, cat, cgl,
             dq_ref, dk_ref, dv_ref, dba_ref, dalog_ref, ddtb_ref):
        b = pl.program_id(0)
        h = pl.program_id(1)
        a_log_h = _pick_lane(alog_ref[...], h)
        dt_b_h = _pick_lane(dtb_ref[...], h)
        prep = functools.partial(_gdn_chunks_prep, dots=VJP_DOTS)

        @pl.when(h == 0)
        def _():
            dba_ref[...] = jnp.zeros((s, LANES), f32)

        def load(n):
            rows = pl.ds(pl.multiple_of(n * CHUNK, CHUNK), CHUNK)
            bav = ba_ref[rows, :]
            cgl_n = cgl[pl.ds(pl.multiple_of(n * SUBLANES, SUBLANES), SUBLANES), :][0:1, :]
            primals = (q_ref[rows, :], k_ref[rows, :], v_ref[rows, :], _pick_lane(bav, h), _pick_lane(bav, C_HEADS + h))
            return primals, (cqg[rows, :], ckd[rows, :], cw_[rows, :], cu[rows, :], cat[rows, :], cgl_n), dba_ref[rows, :]

        def compute(loaded):
            primals = [list(x) for x in zip(*[item[0] for item in loaded])]
            _, vjp = jax.vjp(prep, *primals, a_log_h, dt_b_h)
            dqs, dks, dvs, dbls, dals, dalog, ddtb = vjp([item[1] for item in loaded])
            zero = jnp.zeros((1, 1), f32)
            return [((dqs[i], dks[i], dvs[i], dbls[i], dals[i], dalog if i == 0 else zero, ddtb if i == 0 else zero),
                     loaded[i][2]) for i in range(len(loaded))]

        def store(n, res, carry):
            (dq, dk, dv, dbl, dal, dalog_n, ddtb_n), dba_old = res
            rows = pl.ds(pl.multiple_of(n * CHUNK, CHUNK), CHUNK)
            dq_ref[rows, :] = dq
            dk_ref[rows, :] = dk
            dv_ref[rows, :] = dv
            dba_ref[rows, :] = dba_old + _put_lane(dbl, h, LANES) + _put_lane(dal, C_HEADS + h, LANES)
            return carry[0] + dalog_n, carry[1] + ddtb_n

        da_log, ddt_b = _loop_unrolled(nc, PREP_BWD_UNROLL, load, compute, store,
                                       (jnp.zeros((1, 1), f32), jnp.zeros((1, 1), f32)))
        first = jnp.logical_and(b == 0, h == 0)

        @pl.when(first)
        def _():
            dalog_ref[...] = _put_lane(da_log, h, LANES)
            ddtb_ref[...] = _put_lane(ddt_b, h, LANES)

        @pl.when(jnp.logical_not(first))
        def _():
            dalog_ref[...] += _put_lane(da_log, h, LANES)
            ddtb_ref[...] += _put_lane(ddt_b, h, LANES)

    big = jax.ShapeDtypeStruct((t, C_WIDTH), f32)
    vec = pl.BlockSpec((1, LANES), lambda b, h: (0, 0))
    return pl.pallas_call(
        body, grid=(nb, C_HEADS),
        in_specs=[head(0), head(C_HEADS), head(2 * C_HEADS), ba_sp, sc8, sc8] + [head(0)] * 4 + [attn_sp, gl_sp],
        out_specs=[head(0)] * 3 + [ba_sp, vec, vec],
        out_shape=[big] * 3 + [jax.ShapeDtypeStruct((t, LANES), f32), jax.ShapeDtypeStruct((1, LANES), f32),
                               jax.ShapeDtypeStruct((1, LANES), f32)],
        compiler_params=_params(("arbitrary", "arbitrary")), name=name,
    )(qkv, qkv, qkv, ba, a_log, dt_b, *cts)


def _gdn_rec_specs(sb, nsb, hp, reverse):
    hd = C_HEAD_DIM
    ncb = sb // CHUNK
    blk = (lambda b, k: b * nsb + (nsb - 1 - k)) if reverse else (lambda b, k: b * nsb + k)
    wide = pl.BlockSpec((sb, hp * hd), lambda b, j, k: (blk(b, k), j))
    attn = pl.BlockSpec((hp, sb, CHUNK), lambda b, j, k: (j, blk(b, k), 0))
    gl = pl.BlockSpec((hp, ncb * SUBLANES, hd), lambda b, j, k: (j, blk(b, k), 0))
    ng = pl.BlockSpec((1, hd), lambda b, j, k: (0, 0))
    states = pl.BlockSpec((hp, ncb, hd, hd), lambda b, j, k: (j, blk(b, k), 0, 0))
    return wide, attn, gl, ng, states


def gdn_rec_fwd(qg, kdec, w, u, attn, gl, z, ng, nb, *, name):
    t = qg.shape[0]
    s = t // nb
    sb = min(s, GDN_TIME_BLOCK)
    nsb = s // sb
    hd = C_HEAD_DIM
    hp = C_HEADS_PER_STEP
    wide, attn_sp, gl_sp, ng_sp, st_sp = _gdn_rec_specs(sb, nsb, hp, False)

    def body(qg_ref, kd_ref, w_ref, u_ref, at_ref, gl_ref, z_ref, ng_ref, y_ref, st_ref, carry_ref):
        @pl.when(pl.program_id(2) == 0)
        def _():
            carry_ref[...] = jnp.zeros((hp, hd, hd), f32)

        def chunk(n, states):
            for j in range(hp):
                st_ref[j, n] = states[j]
            rows = pl.ds(pl.multiple_of(n * CHUNK, CHUNK), CHUNK)
            grow = pl.ds(pl.multiple_of(n * SUBLANES, SUBLANES), SUBLANES)
            cols = [slice(j * hd, (j + 1) * hd) for j in range(hp)]
            ins = [(qg_ref[rows, c], kd_ref[rows, c], w_ref[rows, c], u_ref[rows, c], at_ref[j, rows, :],
                    gl_ref[j, grow, :][0:1, :], z_ref[rows, c]) for j, c in enumerate(cols)]
            ys, new = _gdn_heads_step(list(states), *[list(x) for x in zip(*ins)], ng_ref[...], RAW_DOTS)
            for j in range(hp):
                y_ref[rows, cols[j]] = ys[j]
            return tuple(new)

        last = lax.fori_loop(0, sb // CHUNK, chunk, tuple(carry_ref[j] for j in range(hp)))
        for j in range(hp):
            carry_ref[j] = last[j]

    return pl.pallas_call(
        body, grid=(nb, C_HEADS // hp, nsb),
        in_specs=[wide] * 4 + [attn_sp, gl_sp, wide, ng_sp], out_specs=[wide, st_sp],
        out_shape=[jax.ShapeDtypeStruct((t, C_WIDTH), f32), jax.ShapeDtypeStruct((C_HEADS, t // CHUNK, hd, hd), f32)],
        scratch_shapes=[pltpu.VMEM((hp, hd, hd), f32)],
        compiler_params=_params(("parallel", "parallel", "arbitrary")), name=name,
    )(qg, kdec, w, u, attn, gl, z, ng)


def gdn_rec_bwd(qg, kdec, w, u, attn, gl, z, ng, states, dy, nb, *, name):
    t = qg.shape[0]
    s = t // nb
    sb = min(s, GDN_TIME_BLOCK)
    nsb = s // sb
    nc = sb // CHUNK
    hd = C_HEAD_DIM
    hp = C_HEADS_PER_STEP
    wide, attn_sp, gl_sp, ng_sp, st_sp = _gdn_rec_specs(sb, nsb, hp, True)

    def body(qg_ref, kd_ref, w_ref, u_ref, at_ref, gl_ref, z_ref, ng_ref, states, dy_ref,
             dqg_ref, dkd_ref, dw_ref, du_ref, dat_ref, dgl_ref, dz_ref, dng_ref, carry_ref):
        step = functools.partial(_gdn_heads_step, dots=VJP_DOTS)

        @pl.when(pl.program_id(2) == 0)
        def _():
            carry_ref[...] = jnp.zeros((hp, hd, hd), f32)

        def operands(n):
            rows = pl.ds(pl.multiple_of(n * CHUNK, CHUNK), CHUNK)
            grow = pl.ds(pl.multiple_of(n * SUBLANES, SUBLANES), SUBLANES)
            cols = [slice(j * hd, (j + 1) * hd) for j in range(hp)]
            return ([qg_ref[rows, c].astype(f32) for c in cols], [kd_ref[rows, c].astype(f32) for c in cols],
                    [w_ref[rows, c].astype(f32) for c in cols], [u_ref[rows, c] for c in cols],
                    [at_ref[j, rows, :].astype(f32) for j in range(hp)],
                    [gl_ref[j, grow, :][0:1, :] for j in range(hp)], [z_ref[rows, c] for c in cols])

        def bwd_chunk(i, carry):
            n = nc - 1 - i
            rows = pl.ds(pl.multiple_of(n * CHUNK, CHUNK), CHUNK)
            grow = pl.ds(pl.multiple_of(n * SUBLANES, SUBLANES), SUBLANES)
            dsts, dng = carry
            dys = [dy_ref[rows, j * hd:(j + 1) * hd] for j in range(hp)]
            _, vjp = jax.vjp(step, [states[j, n] for j in range(hp)], *operands(n), ng_ref[...])
            dst, dqg, dkd, dw, du, dat, dgl, dz, dng_n = vjp((dys, list(dsts)))
            for j in range(hp):
                cols = slice(j * hd, (j + 1) * hd)
                dqg_ref[rows, cols] = dqg[j]
                dkd_ref[rows, cols] = dkd[j]
                dw_ref[rows, cols] = dw[j]
                du_ref[rows, cols] = du[j]
                dat_ref[j, rows, :] = dat[j]
                dgl_ref[j, grow, :] = jnp.broadcast_to(dgl[j], (SUBLANES, hd))
                dz_ref[rows, cols] = dz[j]
            return tuple(dst), dng + dng_n

        dlast, dng = lax.fori_loop(0, nc, bwd_chunk,
                                   (tuple(carry_ref[j] for j in range(hp)), jnp.zeros((1, hd), f32)))
        for j in range(hp):
            carry_ref[j] = dlast[j]
        first = jnp.logical_and(jnp.logical_and(pl.program_id(0) == 0, pl.program_id(1) == 0), pl.program_id(2) == 0)

        @pl.when(first)
        def _():
            dng_ref[...] = dng

        @pl.when(jnp.logical_not(first))
        def _():
            dng_ref[...] += dng

    big = jax.ShapeDtypeStruct((t, C_WIDTH), f32)
    return pl.pallas_call(
        body, grid=(nb, C_HEADS // hp, nsb),
        in_specs=[wide] * 4 + [attn_sp, gl_sp, wide, ng_sp, st_sp, wide],
        out_specs=[wide] * 4 + [attn_sp, gl_sp, wide, ng_sp],
        out_shape=[big] * 4 + [jax.ShapeDtypeStruct(attn.shape, f32), jax.ShapeDtypeStruct(gl.shape, f32), big,
                               jax.ShapeDtypeStruct((1, hd), f32)],
        scratch_shapes=[pltpu.VMEM((hp, hd, hd), f32)],
        compiler_params=_params(("arbitrary", "arbitrary", "arbitrary")), name=name,
    )(qg, kdec, w, u, attn, gl, z, ng, states, dy)


def _blockdiag_slabs(w):
    per = GROUP_W // B_BLOCK
    slabs = jnp.zeros((B_BLOCKS // per, GROUP_W, GROUP_W), w.dtype)
    for h in range(B_BLOCKS):
        o = (h % per) * B_BLOCK
        slabs = slabs.at[h // per, o:o + B_BLOCK, o:o + B_BLOCK].set(w[h])
    return slabs


def _slab_blocks(slabs):
    per = GROUP_W // B_BLOCK
    return jnp.stack([slabs[h // per, (h % per) * B_BLOCK:(h % per + 1) * B_BLOCK,
                            (h % per) * B_BLOCK:(h % per + 1) * B_BLOCK] for h in range(B_BLOCKS)])


def _mixer_ab_fwd(x1, x1b, W, g, b, nb, tag):
    w_in = W["ab_w_in"][0].astype(bf16)
    o1, o2 = A_WIDTH + 2 * A_KV_WIDTH, A_WIDTH + 2 * A_KV_WIDTH + B_WIDTH
    w_qkv, w_bx, w_bg = w_in[:, :o1], w_in[:, o1:o2], w_in[:, o2:]
    pqkv = mm_nn(x1b,w_qkv, name=tag + "_in_qkv")
    pbx = mm_nn(x1b,w_bx, name=tag + "_in_bx")
    pbg = mm_nn(x1b,w_bg, name=tag + "_in_bg")
    ya = attn_fwd(pqkv, W["a_sinks"], nb, name=tag + "_attn_fwd")
    xc = conv_fwd(pbx, W["b_conv_w"][0], W["b_conv_b"], False, nb, name=tag + "_conv_fwd")
    wa_s, wx_s = _blockdiag_slabs(W["b_wa"][0]), _blockdiag_slabs(W["b_wx"][0])
    yb, hh = rglru_fwd(xc, pbg, wa_s, wx_s, W["b_ba"], W["b_bx"], W["b_lam"], nb, name=tag + "_rglru_fwd")
    w_out = W["ab_w_out"][0].astype(bf16)
    x2, z1, x2b = proj_ln([ya, yb], [w_out[:A_WIDTH], w_out[A_WIDTH:]], x1, g, b, name=tag + "_out_ln")
    saved = (pqkv, pbx, pbg, ya, xc, yb, hh, wa_s, wx_s, w_qkv, w_bx, w_bg, w_out)
    return x2, x2b, z1, saved


def _mixer_ab_bwd(x1b, dz1, dz1b, W, saved, nb, tag):
    pqkv, pbx, pbg, ya, xc, yb, hh, wa_s, wx_s, w_qkv, w_bx, w_bg, w_out = saved
    dya = mm_nn(dz1b, w_out[:A_WIDTH].T, name=tag + "_dya")
    dyb = mm_nn(dz1b, w_out[A_WIDTH:].T, name=tag + "_dyb")
    dwo = jnp.concatenate([mm_tn(ya, dz1b, name=tag + "_dwo_a"), mm_tn(yb, dz1b, name=tag + "_dwo_b")], 0)
    dpqkv, dsinks = attn_bwd(pqkv, W["a_sinks"], dya, nb, name=tag + "_attn_bwd")
    dxc, dpbg, dwa_s, dwx_s, dba, dbx, dlam = rglru_bwd(xc, pbg, hh, dyb, wa_s, wx_s, W["b_ba"], W["b_bx"],
                                                       W["b_lam"], nb, name=tag + "_rglru_bwd")
    dpbx, dconv_w, dconv_b = conv_bwd(pbx, W["b_conv_w"][0], W["b_conv_b"], dxc, False, nb, name=tag + "_conv_bwd")
    dw_in = jnp.concatenate([mm_tn(x1b,dpqkv, name=tag + "_dwin_qkv"), mm_tn(x1b,dpbx, name=tag + "_dwin_bx"),
                             mm_tn(x1b,dpbg, name=tag + "_dwin_bg")], 1)
    dx1 = mm_nn(dpqkv, w_qkv.T, add=dz1, add_scale=DN_ALPHA, name=tag + "_dx_qkv")
    dx1 = mm_nn(dpbx, w_bx.T, add=dx1, name=tag + "_dx_bx")
    dx1 = mm_nn(dpbg, w_bg.T, add=dx1, name=tag + "_dx_bg")
    grads = {"ab_w_in": dw_in[None], "a_sinks": dsinks, "b_conv_w": dconv_w[None], "b_conv_b": dconv_b,
             "b_wa": _slab_blocks(dwa_s)[None], "b_ba": dba, "b_wx": _slab_blocks(dwx_s)[None], "b_bx": dbx,
             "b_lam": dlam, "ab_w_out": dwo[None]}
    return dx1, grads


def _mixer_c_fwd(x1, x1b, W, g, b, nb, tag):
    w_in = W["c_w_in"][0].astype(bf16)
    d = w_in.shape[0]
    o1, o2 = 3 * C_WIDTH, 4 * C_WIDTH
    w_qkv, w_z = w_in[:, :o1], w_in[:, o1:o2]
    w_ba = jnp.concatenate([w_in[:, o2:], jnp.zeros((d, LANES - 2 * C_HEADS), bf16)], 1)
    pqkv = mm_nn(x1b,w_qkv, name=tag + "_in_qkv")
    pz = mm_nn(x1b,w_z, name=tag + "_in_z")
    pba = mm_nn(x1b,w_ba, name=tag + "_in_ba")
    zero_b = jnp.zeros((1, o1), f32)
    qkvc = conv_fwd(pqkv, W["c_conv_w"][0], zero_b, True, nb, name=tag + "_conv_fwd")
    prep = gdn_prep_fwd(qkvc, pba, W["c_a_log"], W["c_dt_bias"], nb, name=tag + "_prep_fwd")
    yc, states = gdn_rec_fwd(*prep, pz, W["c_norm_g"], nb, name=tag + "_rec_fwd")
    w_out = W["c_w_out"][0].astype(bf16)
    x2, z1, x2b = proj_ln([yc], [w_out], x1, g, b, name=tag + "_out_ln")
    saved = (pqkv, pz, pba, qkvc, prep, states, yc, w_qkv, w_z, w_ba, w_out, zero_b)
    return x2, x2b, z1, saved


def _mixer_c_bwd(x1b, dz1, dz1b, W, saved, nb, tag):
    pqkv, pz, pba, qkvc, prep, states, yc, w_qkv, w_z, w_ba, w_out, zero_b = saved
    dyc = mm_nn(dz1b, w_out.T, name=tag + "_dyc")
    dwo = mm_tn(yc, dz1b, name=tag + "_dwo")
    rec = gdn_rec_bwd(*prep, pz, W["c_norm_g"], states, dyc, nb, name=tag + "_rec_bwd")
    cts, dpz, dng = rec[:6], rec[6], rec[7]
    dq, dk, dv, dpba, dalog, ddtb = gdn_prep_bwd(qkvc, pba, W["c_a_log"], W["c_dt_bias"], cts, nb,
                                                 name=tag + "_prep_bwd")
    dqkvc = jnp.concatenate([dq, dk, dv], 1)
    dpqkv, dconv_w, _ = conv_bwd(pqkv, W["c_conv_w"][0], zero_b, dqkvc, True, nb, name=tag + "_conv_bwd")
    dw_in = jnp.concatenate([mm_tn(x1b,dpqkv, name=tag + "_dwin_qkv"), mm_tn(x1b,dpz, name=tag + "_dwin_z"),
                             mm_tn(x1b,dpba, name=tag + "_dwin_ba")[:, :2 * C_HEADS]], 1)
    dx1 = mm_nn(dpqkv, w_qkv.T, add=dz1, add_scale=DN_ALPHA, name=tag + "_dx_qkv")
    dx1 = mm_nn(dpz, w_z.T, add=dx1, name=tag + "_dx_z")
    dx1 = mm_nn(dpba, w_ba.T, add=dx1, name=tag + "_dx_ba")
    grads = {"c_w_in": dw_in[None], "c_conv_w": dconv_w[None], "c_a_log": dalog[:, :C_HEADS],
             "c_dt_bias": ddtb[:, :C_HEADS], "c_norm_g": dng, "c_w_out": dwo[None]}
    return dx1, grads


def _local_step(x, p, target, W, F, on_ffn_grads):
    nb, s, d = x.shape
    t = nb * s
    h = x.reshape(t, d)
    hb = h.astype(bf16)
    tape = []
    for i in range(DEPTH):
        tag = f"l{i}"
        f1 = [F[k][i] for k in ("ffn1_wg", "ffn1_wu", "ffn1_wd")]
        f2 = [F[k][i] for k in ("ffn2_wg", "ffn2_wu", "ffn2_wd")]
        lg = [W["ln_g"][i, k][None] for k in range(3)]
        lb = [W["ln_b"][i, k][None] for k in range(3)]
        x1, z0, x1b = ffn_fwd(h, *f1, lg[0], lb[0], name=tag + "_ffn1_fwd")
        mixer = _mixer_ab_fwd if i % 2 == 0 else _mixer_c_fwd
        x2, x2b, z1, msaved = mixer(x1, x1b, W, lg[1], lb[1], nb, tag + "_mix")
        x3, z2, _ = ffn_fwd(x2, *f2, lg[2], lb[2], name=tag + "_ffn2_fwd")
        pi = p[i].reshape(t, -1)
        pw = (W["ple_wg"][i].astype(bf16), W["ple_bg"][i][None], W["ple_wp"][i].astype(bf16))
        x4, x4b = ple_fwd(x3, pi, *pw, name=tag + "_ple_fwd")
        tape.append((hb, z0, x1b, msaved, z1, x2b, z2, x3, pi, pw, lg))
        h, hb = x4, x4b
    dh, sq = loss_head(h, target.reshape(t, d), name="loss_head")
    loss = 0.5 * jnp.sum(sq) / d
    per_layer = [None] * DEPTH
    grads = {}
    for i in reversed(range(DEPTH)):
        tag = f"l{i}"
        hb_in, z0, x1b, msaved, z1, x2b, z2, x3, pi, pw, lg = tape[i]
        dx3, dple_wg, dple_bg, dple_wp = ple_bwd(x3, pi, dh, pw[0], pw[0].T, pw[1], pw[2], name=tag + "_ple_bwd")
        dz2, dz2b, dg2, db2 = ln_bwd(z2, dx3, lg[2], name=tag + "_ln2_bwd")
        f1 = [F[k][i] for k in ("ffn1_wg", "ffn1_wu", "ffn1_wd")]
        f2 = [F[k][i] for k in ("ffn2_wg", "ffn2_wu", "ffn2_wd")]
        dgate, dup, *df2 = ffn_bwd_weights(x2b, dz2b, *f2, name=tag + "_ffn2_bwd_w")
        on_ffn_grads(i, 3, df2)
        dx2 = ffn_bwd_input(dgate, dup, f2[0], f2[1], dz2, name=tag + "_ffn2_bwd_x")
        dz1, dz1b, dg1, db1 = ln_bwd(z1, dx2, lg[1], name=tag + "_ln1_bwd")
        mixer_bwd = _mixer_ab_bwd if i % 2 == 0 else _mixer_c_bwd
        dx1, mgrads = mixer_bwd(x1b, dz1, dz1b, W, msaved, nb, tag + "_mix")
        grads.update(mgrads)
        dz0, dz0b, dg0, db0 = ln_bwd(z0, dx1, lg[0], name=tag + "_ln0_bwd")
        dgate, dup, *df1 = ffn_bwd_weights(hb_in, dz0b, *f1, name=tag + "_ffn1_bwd_w")
        on_ffn_grads(i, 0, df1)
        dh = ffn_bwd_input(dgate, dup, f1[0], f1[1], dz0, name=tag + "_ffn1_bwd_x")
        per_layer[i] = {"ln_g": jnp.concatenate([dg0, dg1, dg2], 0), "ln_b": jnp.concatenate([db0, db1, db2], 0),
                        "ple_wg": dple_wg, "ple_bg": dple_bg[0], "ple_wp": dple_wp}
    for k in per_layer[0]:
        grads[k] = jnp.stack([per_layer[i][k] for i in range(DEPTH)])
    return loss, dh.reshape(nb, s, d), grads


WEIGHT_NAMES = ("ffn1_wg", "ffn1_wu", "ffn1_wd", "ffn2_wg", "ffn2_wu", "ffn2_wd", "ln_g", "ln_b", "ple_wg", "ple_bg",
                "ple_wp", "ab_w_in", "a_sinks", "b_conv_w", "b_conv_b", "b_wa", "b_ba", "b_wx", "b_bx", "b_lam",
                "ab_w_out", "c_w_in", "c_conv_w", "c_a_log", "c_dt_bias", "c_norm_g", "c_w_out")
NATIVE_NAMES = WEIGHT_NAMES[:6]
PACKED_NAMES = WEIGHT_NAMES[6:]
PACK_MATRICES = ("ple_wg", "ple_wp", "ab_w_in", "ab_w_out", "c_w_in", "c_w_out")
PACK_GROUPS = (tuple(k for k in PACKED_NAMES if k not in PACK_MATRICES), PACK_MATRICES)
PACK_TRANSIT = (f32, bf16)
SHARD_AXIS = {"ffn1_wg": 2, "ffn1_wu": 2, "ffn1_wd": 1, "ffn2_wg": 2, "ffn2_wu": 2, "ffn2_wd": 1, "ln_g": 2, "ln_b": 2,
              "ple_wg": 1, "ple_wp": 2, "ab_w_in": 2, "b_conv_w": 2, "ab_w_out": 1, "c_w_in": 2, "c_conv_w": 2,
              "c_w_out": 1}
N_CHIPS = 4
PACK_COLS = LANES
PACK_TILE_MULTIPLE = 256
ELEMENTWISE_BLOCK_ELEMS = 128 * 1024


def _row_tile(r, cols):
    return _tile(r, max(2 * SUBLANES, ELEMENTWISE_BLOCK_ELEMS // cols), 2 * SUBLANES)
MESH = pl.DeviceIdType.MESH
ANY = pl.BlockSpec(memory_space=pl.ANY)


def _tiled_dims(shape):
    w = shape[-1]
    r = 1
    for dim in shape[:-1]:
        r *= dim
    return r, w, -(-r // SUBLANES) * SUBLANES, -(-w // LANES) * LANES


def _pack(pieces, lead=()):
    k = len(lead)
    tiles = []
    for a in pieces:
        r, w, rp, wp = _tiled_dims(a.shape[k:])
        a2 = jnp.pad(a.reshape(lead + (r, w)), [(0, 0)] * k + [(0, rp - r), (0, wp - w)])
        a2 = a2.reshape(lead + (rp // SUBLANES, SUBLANES, wp // LANES, LANES))
        a2 = jnp.swapaxes(a2, k + 1, k + 2)
        tiles.append(a2.reshape(lead + (-1, SUBLANES, LANES)))
    flat = jnp.concatenate(tiles, axis=k)
    n = flat.shape[k]
    n_pad = -(-n // PACK_TILE_MULTIPLE) * PACK_TILE_MULTIPLE
    flat = jnp.pad(flat, [(0, 0)] * k + [(0, n_pad - n), (0, 0), (0, 0)])
    return flat.reshape(lead + (n_pad * SUBLANES, PACK_COLS))


def _unpack(pack, shapes, lead=()):
    k = len(lead)
    flat = pack.reshape(lead + (-1, SUBLANES, LANES))
    out, o = [], 0
    for shp in shapes:
        r, w, rp, wp = _tiled_dims(shp)
        n = (rp // SUBLANES) * (wp // LANES)
        a2 = lax.slice_in_dim(flat, o, o + n, axis=k).reshape(lead + (rp // SUBLANES, wp // LANES, SUBLANES, LANES))
        a2 = jnp.swapaxes(a2, k + 1, k + 2).reshape(lead + (rp, wp))
        a2 = lax.slice_in_dim(lax.slice_in_dim(a2, 0, r, axis=k), 0, w, axis=k + 1)
        out.append(a2.reshape(lead + tuple(shp)))
        o += n
    return out


def _mesh_position():
    x, y, c = lax.axis_index("x"), lax.axis_index("y"), lax.axis_index("c")
    chips = [(1 - x, y), (x, 1 - y), (1 - x, 1 - y)]
    return x, y, c, chips


def _remote(src, dst, send_sems, recv_sems, k, to):
    return pltpu.make_async_remote_copy(src_ref=src, dst_ref=dst, send_sem=send_sems.at[k], recv_sem=recv_sems.at[k],
                                        device_id=to, device_id_type=MESH)


def _sems(n):
    return pltpu.SemaphoreType.DMA((n,))


def place_slot(parts, slots, n_slots, dtype, from_slot, *, name):
    n = len(parts)
    r, cols = parts[0].shape[-2:]
    tr = _row_tile(r, cols)

    def body(src_ref, dst_ref, *refs):
        for a in range(n):
            refs[n + a][...] = refs[a][...].astype(dtype)

    dst = pl.BlockSpec((None, tr, cols), lambda i, src_ref, dst_ref: (dst_ref[0], i, 0))
    src = (pl.BlockSpec((None, tr, cols), lambda i, src_ref, dst_ref: (src_ref[0], i, 0)) if from_slot
           else pl.BlockSpec((tr, cols), lambda i, src_ref, dst_ref: (i, 0)))
    return pl.pallas_call(
        body,
        grid_spec=pltpu.PrefetchScalarGridSpec(num_scalar_prefetch=2, grid=(r // tr,), in_specs=[src] * n,
                                               out_specs=[dst] * n),
        out_shape=[jax.ShapeDtypeStruct((n_slots, r, cols), dtype)] * n,
        compiler_params=_params(("parallel",)), name=name,
    )(*slots, *parts)


def gather_shards(bufs, *, name):
    n = len(bufs)

    def body(*refs):
        out_refs = refs[n:2 * n]
        send_sems, recv_sems = refs[2 * n:]
        x, y, c, chips = _mesh_position()
        me = 2 * x + y
        sibling = (x, y, 1 - c)
        waits = []
        for j, (cx, cy) in enumerate(chips):
            for a in range(n):
                own = out_refs[a].at[me, c]
                cp = _remote(own, own, send_sems, recv_sems, 6 * a + j, (cx, cy, c))
                cp.start()
                waits.append(cp.wait_send)
        for j, (cx, cy) in enumerate(chips):
            for a in range(n):
                got = out_refs[a].at[2 * cx + cy, c]
                _remote(got, got, send_sems, recv_sems, 6 * a + j, (cx, cy, c)).wait_recv()
                fw = _remote(got, got, send_sems, recv_sems, 6 * a + 3 + j, sibling)
                fw.start()
                waits.append(fw.wait_send)
        for j, (cx, cy) in enumerate(chips):
            for a in range(n):
                got = out_refs[a].at[2 * cx + cy, 1 - c]
                _remote(got, got, send_sems, recv_sems, 6 * a + 3 + j, sibling).wait_recv()
        for wait in waits:
            wait()

    return pl.pallas_call(
        body, out_shape=[jax.ShapeDtypeStruct(b.shape, b.dtype) for b in bufs],
        in_specs=[ANY] * n, out_specs=[ANY] * n, scratch_shapes=[_sems(6 * n), _sems(6 * n)],
        input_output_aliases={a: a for a in range(n)}, name=name,
    )(*bufs)


def chip_exchange(ps, qs, *, name):
    n = len(ps)

    def body(*refs):
        p_refs, q_refs = refs[:n], refs[2 * n:3 * n]
        send_sems, recv_sems = refs[3 * n:]
        x, y, c, chips = _mesh_position()
        me = 2 * x + y
        waits = []
        for j, (cx, cy) in enumerate(chips):
            for a in range(n):
                cp = _remote(p_refs[a].at[2 * cx + cy], q_refs[a].at[me], send_sems, recv_sems, 3 * a + j, (cx, cy, c))
                cp.start()
                waits.append(cp.wait_send)
        for j, (cx, cy) in enumerate(chips):
            for a in range(n):
                got = q_refs[a].at[2 * cx + cy]
                _remote(got, got, send_sems, recv_sems, 3 * a + j, (cx, cy, c)).wait_recv()
        for wait in waits:
            wait()

    return pl.pallas_call(
        body, out_shape=[jax.ShapeDtypeStruct(q_.shape, q_.dtype) for q_ in qs], in_specs=[ANY] * (2 * n),
        out_specs=[ANY] * n, scratch_shapes=[_sems(3 * n), _sems(3 * n)],
        input_output_aliases={n + a: a for a in range(n)}, name=name,
    )(*ps, *qs)


def gather_slots_async(bufs, collective_id, *, name):
    n = len(bufs)
    refs = [jax.new_ref(b, memory_space=pltpu.MemorySpace.HBM) for b in bufs]

    @pl.kernel(mesh=plsc.ScalarSubcoreMesh(axis_name="sequencer", num_cores=1), name=name,
               scratch_types=(_sems(3 * n), _sems(3 * n)),
               compiler_params=pltpu.CompilerParams(collective_id=collective_id))
    def launch(send_sems, recv_sems):
        x, y, c, chips = _mesh_position()
        me = 2 * x + y
        barrier = pltpu.get_barrier_semaphore()
        for cx, cy in chips:
            pl.semaphore_signal(barrier, inc=1, device_id=(cx, cy, c), device_id_type=MESH)
        pl.semaphore_wait(barrier, len(chips))
        sends = []
        for j, (cx, cy) in enumerate(chips):
            for a in range(n):
                own = refs[a].at[me]
                cp = _remote(own, own, send_sems, recv_sems, 3 * a + j, (cx, cy, c))
                cp.start()
                sends.append(cp)
        for j, (cx, cy) in enumerate(chips):
            for a in range(n):
                got = refs[a].at[2 * cx + cy]
                _remote(got, got, send_sems, recv_sems, 3 * a + j, (cx, cy, c)).wait_recv()
        for cp in sends:
            cp.wait_send()

    launch()
    return [r[...] for r in refs]


N_DEVICES = 8
PEER_FLIPS = tuple((dx, dy, dc) for dx in (0, 1) for dy in (0, 1) for dc in (0, 1) if dx or dy or dc)


def exchange_partials_async(sends, recvs, collective_id, *, name):
    n = len(sends)
    s_refs = [jax.new_ref(a, memory_space=pltpu.MemorySpace.HBM) for a in sends]
    r_refs = [jax.new_ref(a, memory_space=pltpu.MemorySpace.HBM) for a in recvs]
    k = len(PEER_FLIPS)

    @pl.kernel(mesh=plsc.ScalarSubcoreMesh(axis_name="sequencer", num_cores=1), name=name,
               scratch_types=(_sems(k), _sems(k)), compiler_params=pltpu.CompilerParams(collective_id=collective_id))
    def launch(send_sems, recv_sems):
        x, y, c, _ = _mesh_position()
        me = 4 * x + 2 * y + c
        peers = [(1 - x if dx else x, 1 - y if dy else y, 1 - c if dc else c) for dx, dy, dc in PEER_FLIPS]
        barrier = pltpu.get_barrier_semaphore()
        for peer in peers:
            pl.semaphore_signal(barrier, inc=1, device_id=peer, device_id_type=MESH)
        pl.semaphore_wait(barrier, len(peers))
        sends_started = []
        for j, (px, py, pc) in enumerate(peers):
            for a in range(n):
                cp = _remote(s_refs[a].at[2 * px + py], r_refs[a].at[me], send_sems, recv_sems, j, (px, py, pc))
                cp.start()
                sends_started.append(cp)
        for j, (px, py, pc) in enumerate(peers):
            for a in range(n):
                got = r_refs[a].at[4 * px + 2 * py + pc]
                _remote(got, got, send_sems, recv_sems, j, (px, py, pc)).wait_recv()
        for cp in sends_started:
            cp.wait_send()

    launch()
    return [r[...] for r in r_refs]


def sibling_exchange(gs, *, name):
    n = len(gs)

    def body(*refs):
        g_refs, out_refs = refs[:n], refs[n:2 * n]
        send_sems, recv_sems = refs[2 * n:]
        x, y, c, _ = _mesh_position()
        cps = [_remote(g_refs[a].at[:, 1 - c], out_refs[a], send_sems, recv_sems, a, (x, y, 1 - c)) for a in range(n)]
        for cp in cps:
            cp.start()
        for cp in cps:
            cp.wait()

    return pl.pallas_call(
        body, out_shape=[jax.ShapeDtypeStruct(g.shape[:1] + g.shape[2:], g.dtype) for g in gs],
        in_specs=[ANY] * n, out_specs=[ANY] * n, scratch_shapes=[_sems(n), _sems(n)], name=name,
    )(*gs)


def add_own_half(gs, others, c_idx, dtype, *, name):
    n = len(gs)
    ns, _, r, cols = gs[0].shape
    tr = _row_tile(r, cols)

    def body(c_ref, *refs):
        for a in range(n):
            refs[2 * n + a][...] = (refs[a][...] + refs[n + a][...]).astype(dtype)

    own = pl.BlockSpec((None, None, tr, cols), lambda s, i, c_ref: (s, c_ref[0], i, 0))
    oth = pl.BlockSpec((None, tr, cols), lambda s, i, c_ref: (s, i, 0))
    return pl.pallas_call(
        body,
        grid_spec=pltpu.PrefetchScalarGridSpec(num_scalar_prefetch=1, grid=(ns, r // tr),
                                               in_specs=[own] * n + [oth] * n, out_specs=[oth] * n),
        out_shape=[jax.ShapeDtypeStruct((ns, r, cols), dtype)] * n,
        compiler_params=_params(("parallel", "parallel")), name=name,
    )(c_idx, *gs, *others)


def sum_slots(qs, *, name):
    n = len(qs)
    ns, r, cols = qs[0].shape
    tr = _row_tile(r, cols * ns)

    def body(*refs):
        for a in range(n):
            q_ref = refs[a]
            acc = q_ref[0].astype(f32) + q_ref[1].astype(f32)
            for i in range(2, ns):
                acc = acc + q_ref[i].astype(f32)
            refs[n + a][...] = acc

    return pl.pallas_call(
        body, grid=(r // tr,), in_specs=[pl.BlockSpec((ns, tr, cols), lambda i: (0, i, 0))] * n,
        out_specs=[pl.BlockSpec((tr, cols), lambda i: (i, 0))] * n,
        out_shape=[jax.ShapeDtypeStruct((r, cols), f32)] * n,
        compiler_params=_params(("parallel",)), name=name,
    )(*qs)


def sibling_share(bufs, *, name):
    n = len(bufs)

    def body(*refs):
        out_refs = refs[n:2 * n]
        send_sems, recv_sems = refs[2 * n:]
        x, y, c, _ = _mesh_position()
        sibling = (x, y, 1 - c)
        cps = []
        for a in range(n):
            own = out_refs[a].at[c]
            cp = _remote(own, own, send_sems, recv_sems, a, sibling)
            cp.start()
            cps.append(cp)
        for a in range(n):
            theirs = out_refs[a].at[1 - c]
            _remote(theirs, theirs, send_sems, recv_sems, a, sibling).wait_recv()
        for cp in cps:
            cp.wait_send()

    return pl.pallas_call(
        body, out_shape=[jax.ShapeDtypeStruct(b.shape, b.dtype) for b in bufs], in_specs=[ANY] * n,
        out_specs=[ANY] * n, scratch_shapes=[_sems(n), _sems(n)],
        input_output_aliases={a: a for a in range(n)}, name=name,
    )(*bufs)


def _adamw_update(w, g, m, v):
    m2 = ADAM_B1 * m + (1.0 - ADAM_B1) * g
    v2 = ADAM_B2 * v + (1.0 - ADAM_B2) * (g * g)
    m_hat = m2 / (1.0 - ADAM_B1 ** ADAM_STEP)
    v_hat = v2 / (1.0 - ADAM_B2 ** ADAM_STEP)
    return -ADAM_LR * (m_hat / (jnp.sqrt(v_hat) + ADAM_EPS) + ADAM_WD * w), m2, v2


def adamw_from_partials(ws, ms, vs, slots, layer, acc, *, name):
    n = len(ws)
    nl, r, cols = ws[0].shape
    ns = slots[0].shape[0]
    tr = _row_tile(r, cols * 2)

    def body(*refs):
        for a in range(n):
            w_ref, m_ref, v_ref, s_ref = (refs[k * n + a] for k in range(4))
            g_ref, d_ref, m2_ref, v2_ref = (refs[len(refs) - 4 * n + k * n + a] for k in range(4))
            g = s_ref[0].astype(f32) + s_ref[1].astype(f32)
            for i in range(2, ns):
                g = g + s_ref[i].astype(f32)
            g_ref[...] = g
            d_ref[...], m2_ref[...], v2_ref[...] = _adamw_update(w_ref[...], g, m_ref[...], v_ref[...])

    lay = pl.BlockSpec((None, tr, cols), lambda i: (layer, i, 0))
    in_specs = [lay] * (3 * n) + [pl.BlockSpec((ns, tr, cols), lambda i: (0, i, 0))] * n
    args = [*ws, *ms, *vs, *slots]
    aliases = {}
    if acc is not None:
        in_specs += [ANY] * (4 * n)
        args += [a for lst in acc for a in lst]
        aliases = {4 * n + k: k for k in range(4 * n)}
    out = pl.pallas_call(
        body, grid=(r // tr,), in_specs=in_specs, out_specs=[lay] * (4 * n),
        out_shape=[jax.ShapeDtypeStruct((nl, r, cols), f32)] * (4 * n), input_output_aliases=aliases,
        compiler_params=_params(("parallel",)), name=name,
    )(*args)
    return [list(out[k * n:(k + 1) * n]) for k in range(4)]


def adamw(ws, gs, ms, vs, *, name):
    n = len(ws)
    r, cols = ws[0].shape
    tr = _row_tile(r, cols)

    def body(*refs):
        for a in range(n):
            w_ref, g_ref, m_ref, v_ref = (refs[k * n + a] for k in range(4))
            d_ref, m2_ref, v2_ref = (refs[(4 + k) * n + a] for k in range(3))
            d_ref[...], m2_ref[...], v2_ref[...] = _adamw_update(w_ref[...], g_ref[...], m_ref[...], v_ref[...])

    row = pl.BlockSpec((tr, cols), lambda i: (i, 0))
    out = pl.pallas_call(
        body, grid=(r // tr,), in_specs=[row] * (4 * n), out_specs=[row] * (3 * n),
        out_shape=[jax.ShapeDtypeStruct((r, cols), f32)] * (3 * n),
        compiler_params=_params(("parallel",)), name=name,
    )(*ws, *gs, *ms, *vs)
    return out[:n], out[n:2 * n], out[2 * n:]


def _full_weights(gathered, names, weights):
    pieces = _unpack(gathered, [weights[k].shape for k in names], lead=(N_CHIPS,))
    full = {}
    for name, pc in zip(names, pieces):
        ax = SHARD_AXIS.get(name)
        if ax is None:
            full[name] = weights[name]
        else:
            shp = weights[name].shape
            full[name] = jnp.moveaxis(pc, 0, ax).reshape(shp[:ax] + (N_CHIPS * shp[ax],) + shp[ax + 1:])
    return full


def _grad_pack(grads, names, shapes):
    pieces = []
    for name, shp in zip(names, shapes):
        g = grads[name]
        ax = SHARD_AXIS.get(name)
        if ax is None:
            pieces.append(jnp.broadcast_to(g.reshape(shp)[None], (N_CHIPS,) + tuple(shp)))
        else:
            pieces.append(jnp.stack(jnp.split(g, N_CHIPS, axis=ax)))
    return _pack(pieces, lead=(N_CHIPS,))


def _by_shape(arrays):
    groups = {}
    for i, a in enumerate(arrays):
        groups.setdefault(a.shape, []).append(i)
    return list(groups.values())


def _grouped(fn, lists, n_out, tag):
    outs = [[None] * len(lists[0]) for _ in range(n_out)]
    for gi, idx in enumerate(_by_shape(lists[0])):
        res = fn(*[[lst[i] for i in idx] for lst in lists], name=f"{tag}_{gi}")
        res = res if n_out > 1 else (res,)
        for k in range(n_out):
            for i, r in zip(idx, res[k]):
                outs[k][i] = r
    return outs if n_out > 1 else outs[0]


def _train_step(x, p, loss_target, weights, m, v):
    shapes = [[weights[k].shape for k in names] for names in PACK_GROUPS]
    halves = lambda a: a.reshape((2, a.shape[0] // 2) + a.shape[1:])
    packs = lambda d_: [halves(_pack([d_[k] for k in names])) for names in PACK_GROUPS]
    nn_ = len(NATIVE_NAMES)
    local = [weights[k] for k in NATIVE_NAMES] + packs(weights)
    local_m = [m[k] for k in NATIVE_NAMES] + packs(m)
    local_v = [v[k] for k in NATIVE_NAMES] + packs(v)
    flat = lambda lst: [a.reshape((-1, a.shape[-1])) for a in lst]
    c_idx = lax.axis_index("c").astype(jnp.int32).reshape(1)
    chip_idx = (2 * lax.axis_index("x") + lax.axis_index("y")).astype(jnp.int32).reshape(1)
    c2 = (c_idx, c_idx)
    chip2 = (chip_idx, chip_idx)
    chip_dev = (chip_idx, 2 * chip_idx + c_idx)

    def placed(arrays, slot, n_slots, dtype, from_slot, tag):
        return _grouped(lambda a, name: place_slot(a, slot, n_slots, dtype, from_slot, name=name), [arrays], 1, tag)

    ffn_own = [weights[k][i] for i in range(DEPTH) for k in NATIVE_NAMES]
    ffn_bufs = placed(ffn_own, chip2, N_CHIPS, bf16, False, "place_ffn_weights")
    group = len(NATIVE_NAMES) // 2
    n_ffn_groups = len(ffn_bufs) // group
    ffn_gathered = []
    for gi in range(n_ffn_groups):
        ffn_gathered += gather_slots_async(ffn_bufs[gi * group:(gi + 1) * group], collective_id=1 + gi,
                                           name=f"comm_gather_ffn_{gi}")
    ffn_weights = {k: [ffn_gathered[i * len(NATIVE_NAMES) + j] for i in range(DEPTH)] for j, k in enumerate(NATIVE_NAMES)}
    pack_bufs = [placed(flat([a]), chip2, N_CHIPS, dt, False, f"place_packed_weights_{gi}")[0].reshape((N_CHIPS,) + a.shape)
                 for gi, (a, dt) in enumerate(zip(local[nn_:], PACK_TRANSIT))]
    full = {}
    for names, gathered in zip(PACK_GROUPS, gather_shards(pack_bufs, name="comm_gather_weights")):
        full.update(_full_weights(gathered, names, weights))
    first_grad_id = n_ffn_groups + 1
    in_flight = {}

    def on_ffn_grads(layer, first, partials):
        tag = f"ffn_grads_l{layer}_{first}"
        recvs = placed(partials, chip_dev, N_DEVICES, bf16, True, "place_" + tag)
        got = exchange_partials_async(partials, recvs, collective_id=first_grad_id + len(in_flight), name="comm_" + tag)
        in_flight[(layer, first)] = got

    loss, grad_x, grads = _local_step(x, p, loss_target, full, ffn_weights, on_ffn_grads)
    gs = [_grad_pack(grads, names, shp).reshape((N_CHIPS,) + a.shape)
          for names, shp, a in zip(PACK_GROUPS, shapes, local[nn_:])]
    others = sibling_exchange(gs, name="comm_grad_sibling")
    chip_sums = [add_own_half([g], [o], c_idx, dt, name=f"grad_add_sibling_{gi}")[0]
                 for gi, (g, o, dt) in enumerate(zip(gs, others, PACK_TRANSIT))]
    own = [placed([cs], chip2, N_CHIPS, dt, True, f"place_own_partial_{gi}")[0]
           for gi, (cs, dt) in enumerate(zip(chip_sums, PACK_TRANSIT))]
    slots = chip_exchange(chip_sums, own, name="comm_grad_chips")
    mine = _grouped(sum_slots, [list(slots)], 1, "grad_sum_chips")
    pack_sum = sibling_share(placed(mine, c2, 2, f32, False, "place_own_half"), name="comm_grad_share")
    ffn_out = [{} for _ in range(4)]
    for (layer, first), got in in_flight.items():
        names = NATIVE_NAMES[first:first + len(got)]
        for idx in _by_shape([weights[k] for k in names]):
            ks = [names[i] for i in idx]
            acc = [[out[k] for k in ks] for out in ffn_out] if ks[0] in ffn_out[0] else None
            res = adamw_from_partials([weights[k] for k in ks], [m[k] for k in ks], [v[k] for k in ks],
                                      [got[i] for i in idx], layer, acc, name=f"adamw_ffn_l{layer}_{first + idx[0]}")
            for out, arrays in zip(ffn_out, res):
                out.update(zip(ks, arrays))
    pack_out = [list(pack_sum)] + _grouped(adamw, [flat(local[nn_:]), flat(pack_sum), flat(local_m[nn_:]),
                                                    flat(local_v[nn_:])], 3, "adamw_packed")
    loss = lax.psum(loss, ("x", "y", "c"))
    outs = []
    for by_name, packs_ in zip(ffn_out, pack_out):
        by_name = dict(by_name)
        for names, shp, pk in zip(PACK_GROUPS, shapes, packs_):
            by_name.update(zip(names, _unpack(pk, shp)))
        outs += [by_name[k] for k in WEIGHT_NAMES]
    return (loss, grad_x, *outs)


def kernel(x, p, ffn1_wg, ffn1_wu, ffn1_wd, ffn2_wg, ffn2_wu, ffn2_wd, ln_g, ln_b, ple_wg, ple_bg, ple_wp, ab_w_in, a_sinks, b_conv_w, b_conv_b, b_wa, b_ba, b_wx, b_bx, b_lam, ab_w_out, c_w_in, c_conv_w, c_a_log, c_dt_bias, c_norm_g, c_w_out, loss_target, m_ffn1_wg, m_ffn1_wu, m_ffn1_wd, m_ffn2_wg, m_ffn2_wu, m_ffn2_wd, m_ln_g, m_ln_b, m_ple_wg, m_ple_bg, m_ple_wp, m_ab_w_in, m_a_sinks, m_b_conv_w, m_b_conv_b, m_b_wa, m_b_ba, m_b_wx, m_b_bx, m_b_lam, m_ab_w_out, m_c_w_in, m_c_conv_w, m_c_a_log, m_c_dt_bias, m_c_norm_g, m_c_w_out, v_ffn1_wg, v_ffn1_wu, v_ffn1_wd, v_ffn2_wg, v_ffn2_wu, v_ffn2_wd, v_ln_g, v_ln_b, v_ple_wg, v_ple_bg, v_ple_wp, v_ab_w_in, v_a_sinks, v_b_conv_w, v_b_conv_b, v_b_wa, v_b_ba, v_b_wx, v_b_bx, v_b_lam, v_ab_w_out, v_c_w_in, v_c_conv_w, v_c_a_log, v_c_dt_bias, v_c_norm_g, v_c_w_out):
    weights = [ffn1_wg, ffn1_wu, ffn1_wd, ffn2_wg, ffn2_wu, ffn2_wd, ln_g, ln_b, ple_wg, ple_bg, ple_wp, ab_w_in, a_sinks,
               b_conv_w, b_conv_b, b_wa, b_ba, b_wx, b_bx, b_lam, ab_w_out, c_w_in, c_conv_w, c_a_log, c_dt_bias, c_norm_g,
               c_w_out]
    m = [m_ffn1_wg, m_ffn1_wu, m_ffn1_wd, m_ffn2_wg, m_ffn2_wu, m_ffn2_wd, m_ln_g, m_ln_b, m_ple_wg, m_ple_bg, m_ple_wp,
         m_ab_w_in, m_a_sinks, m_b_conv_w, m_b_conv_b, m_b_wa, m_b_ba, m_b_wx, m_b_bx, m_b_lam, m_ab_w_out, m_c_w_in,
         m_c_conv_w, m_c_a_log, m_c_dt_bias, m_c_norm_g, m_c_w_out]
    v = [v_ffn1_wg, v_ffn1_wu, v_ffn1_wd, v_ffn2_wg, v_ffn2_wu, v_ffn2_wd, v_ln_g, v_ln_b, v_ple_wg, v_ple_bg, v_ple_wp,
         v_ab_w_in, v_a_sinks, v_b_conv_w, v_b_conv_b, v_b_wa, v_b_ba, v_b_wx, v_b_bx, v_b_lam, v_ab_w_out, v_c_w_in,
         v_c_conv_w, v_c_a_log, v_c_dt_bias, v_c_norm_g, v_c_w_out]
    return _train_step(x, p, loss_target, dict(zip(WEIGHT_NAMES, weights)), dict(zip(WEIGHT_NAMES, m)),
                       dict(zip(WEIGHT_NAMES, v)))
```

```python
import functools

import jax
import jax.numpy as jnp
from jax import lax
from jax.experimental import pallas as pl
from jax.experimental.pallas import tpu as pltpu
from jax.experimental.pallas import tpu_sc as plsc

f32 = jnp.float32
bf16 = jnp.bfloat16

DEPTH = 2
CHUNK = 64
A_HEADS, A_KV_HEADS, A_GROUP, A_HEAD_DIM = 8, 2, 4, 64
A_WIDTH, A_KV_WIDTH, A_WINDOW = 512, 128, 128
B_WIDTH, B_BLOCKS, B_BLOCK, B_CONV = 512, 8, 64, 4
RG_C = 8.0
C_HEADS, C_HEAD_DIM, C_WIDTH, C_CONV = 8, 128, 1024, 4
DN_ALPHA = (2.0 * DEPTH) ** 0.25
LN_EPS = 1e-5
NORM_EPS = 1e-6
NEG = -1e30
ADAM_LR, ADAM_B1, ADAM_B2, ADAM_EPS, ADAM_WD, ADAM_STEP = 0.001, 0.9, 0.999, 1e-08, 0.01, 10

VMEM_LIMIT_BYTES = 56 * 1024 * 1024
LANES = 128
SUBLANES = 8
GROUP_W = 128
PREP_FWD_UNROLL = 16
PREP_BWD_UNROLL = 16
C_HEADS_PER_STEP = 8
GDN_TIME_BLOCK = 256

NN = ((1,), (0,))
NT = ((1,), (1,))
TN = ((0,), (0,))


def _params(sem):
    return pltpu.CompilerParams(dimension_semantics=sem, vmem_limit_bytes=VMEM_LIMIT_BYTES)


def _tile(n, cap, mult):
    best = None
    t = mult
    while t <= min(n, cap):
        if n % t == 0:
            best = t
        t += mult
    return best if best is not None else n


def _bdot(a, b, dims):
    return lax.dot_general(a.astype(bf16), b.astype(bf16), (dims, ((), ())), preferred_element_type=f32)


def _running_sum(x, reverse):
    s = x.shape[0]
    t = lax.broadcasted_iota(jnp.int32, x.shape, 0)
    d = 1
    while d < s:
        if reverse:
            x = x + jnp.where(t < s - d, pltpu.roll(x, s - d, 0), 0.0)
        else:
            x = x + jnp.where(t >= d, pltpu.roll(x, d, 0), 0.0)
        d *= 2
    return x


@jax.custom_vjp
def _cumsum0(x):
    return _running_sum(x, False)


def _cumsum0_fwd(x):
    return _running_sum(x, False), None


def _cumsum0_bwd(_, g):
    return (_running_sum(g, True),)


_cumsum0.defvjp(_cumsum0_fwd, _cumsum0_bwd)


@jax.custom_vjp
def _bnn(a, b):
    return _bdot(a, b, NN)


def _bnn_fwd(a, b):
    return _bdot(a, b, NN), (a, b)


def _bnn_bwd(res, g):
    a, b = res
    return _bdot(g, b, NT), _bdot(a, g, TN)


_bnn.defvjp(_bnn_fwd, _bnn_bwd)


@jax.custom_vjp
def _bnt(a, b):
    return _bdot(a, b, NT)


def _bnt_fwd(a, b):
    return _bdot(a, b, NT), (a, b)


def _bnt_bwd(res, g):
    a, b = res
    return _bdot(g, b, NN), _bdot(g, a, TN)


_bnt.defvjp(_bnt_fwd, _bnt_bwd)


@jax.custom_vjp
def _btn(a, b):
    return _bdot(a, b, TN)


def _btn_fwd(a, b):
    return _bdot(a, b, TN), (a, b)


def _btn_bwd(res, g):
    a, b = res
    return _bdot(b, g, NT), _bdot(a, g, NN)


_btn.defvjp(_btn_fwd, _btn_bwd)

RAW_DOTS = (lambda a, b: _bdot(a, b, NN), lambda a, b: _bdot(a, b, NT), lambda a, b: _bdot(a, b, TN),
            lambda x: _running_sum(x, False))
VJP_DOTS = (_bnn, _bnt, _btn, _cumsum0)


def _layer_norm(z, g, b):
    mu = jnp.mean(z, -1, keepdims=True)
    d = z - mu
    var = jnp.mean(d * d, -1, keepdims=True)
    return d * lax.rsqrt(var + LN_EPS) * g + b


def _silu(x):
    return x * jax.nn.sigmoid(x)


def mm_nn(a, w, add=None, add_scale=1.0, *, name):
    m, k = a.shape
    n = w.shape[1]
    tm = _tile(m, 1024, 2 * SUBLANES)
    tn = _tile(n, 1024, LANES)

    def body(*refs):
        if add is None:
            a_ref, w_ref, o_ref = refs
            o_ref[...] = _bdot(a_ref[...], w_ref[...], NN)
        else:
            a_ref, w_ref, add_ref, o_ref = refs
            o_ref[...] = _bdot(a_ref[...], w_ref[...], NN) + add_scale * add_ref[...]

    in_specs = [pl.BlockSpec((tm, k), lambda i, j: (i, 0)), pl.BlockSpec((k, tn), lambda i, j: (0, j))]
    args = [a, w]
    if add is not None:
        in_specs.append(pl.BlockSpec((tm, tn), lambda i, j: (i, j)))
        args.append(add)
    return pl.pallas_call(
        body, grid=(m // tm, n // tn), in_specs=in_specs,
        out_specs=pl.BlockSpec((tm, tn), lambda i, j: (i, j)),
        out_shape=jax.ShapeDtypeStruct((m, n), f32),
        compiler_params=_params(("parallel", "parallel")), name=name,
    )(*args)


def mm_tn(a, b, *, name):
    m, k = a.shape
    n = b.shape[1]
    tm = _tile(m, 1024, 2 * SUBLANES)
    tn = _tile(n, 1024, LANES)

    def body(a_ref, b_ref, o_ref):
        part = _bdot(a_ref[...], b_ref[...], TN)

        @pl.when(pl.program_id(1) == 0)
        def _():
            o_ref[...] = part

        @pl.when(pl.program_id(1) > 0)
        def _():
            o_ref[...] += part

    return pl.pallas_call(
        body, grid=(n // tn, m // tm),
        in_specs=[pl.BlockSpec((tm, k), lambda j, i: (i, 0)), pl.BlockSpec((tm, tn), lambda j, i: (i, j))],
        out_specs=pl.BlockSpec((k, tn), lambda j, i: (0, j)),
        out_shape=jax.ShapeDtypeStruct((k, n), f32),
        compiler_params=_params(("parallel", "arbitrary")), name=name,
    )(a, b)


def proj_ln(a_list, w_list, xres, g, b, *, name):
    t, d = xres.shape
    tm = _tile(t, 256, 2 * SUBLANES)
    na = len(a_list)

    def body(*refs):
        a_refs, w_refs = refs[:na], refs[na:2 * na]
        x_ref, g_ref, b_ref, y_ref, z_ref, yb_ref = refs[2 * na:]
        z = DN_ALPHA * x_ref[...]
        for a_ref, w_ref in zip(a_refs, w_refs):
            z = z + _bdot(a_ref[...], w_ref[...], NN)
        z_ref[...] = z
        y = _layer_norm(z, g_ref[...], b_ref[...])
        y_ref[...] = y
        yb_ref[...] = y.astype(bf16)

    in_specs = [pl.BlockSpec((tm, a.shape[1]), lambda i: (i, 0)) for a in a_list]
    in_specs += [pl.BlockSpec(w.shape, lambda i: (0, 0)) for w in w_list]
    in_specs += [pl.BlockSpec((tm, d), lambda i: (i, 0)), pl.BlockSpec((1, d), lambda i: (0, 0)),
                 pl.BlockSpec((1, d), lambda i: (0, 0))]
    return pl.pallas_call(
        body, grid=(t // tm,), in_specs=in_specs,
        out_specs=[pl.BlockSpec((tm, d), lambda i: (i, 0))] * 3,
        out_shape=[jax.ShapeDtypeStruct((t, d), f32)] * 2 + [jax.ShapeDtypeStruct((t, d), bf16)],
        compiler_params=_params(("parallel",)), name=name,
    )(*a_list, *w_list, xres, g, b)


def ln_bwd(z, dy, g, *, name):
    t, d = z.shape
    tm = _tile(t, 512, SUBLANES)

    def body(z_ref, dy_ref, g_ref, dz_ref, dzb_ref, dg_ref, db_ref):
        zz = z_ref[...]
        dy_ = dy_ref[...]
        mu = jnp.mean(zz, -1, keepdims=True)
        dd = zz - mu
        var = jnp.mean(dd * dd, -1, keepdims=True)
        rstd = lax.rsqrt(var + LN_EPS)
        xhat = dd * rstd
        dxh = dy_ * g_ref[...]
        dz = rstd * (dxh - jnp.mean(dxh, -1, keepdims=True) - xhat * jnp.mean(dxh * xhat, -1, keepdims=True))
        dz_ref[...] = dz
        dzb_ref[...] = dz.astype(bf16)
        pg = jnp.sum(dy_ * xhat, 0, keepdims=True)
        pb = jnp.sum(dy_, 0, keepdims=True)

        @pl.when(pl.program_id(0) == 0)
        def _():
            dg_ref[...] = pg
            db_ref[...] = pb

        @pl.when(pl.program_id(0) > 0)
        def _():
            dg_ref[...] += pg
            db_ref[...] += pb

    row = pl.BlockSpec((tm, d), lambda i: (i, 0))
    vec = pl.BlockSpec((1, d), lambda i: (0, 0))
    return pl.pallas_call(
        body, grid=(t // tm,), in_specs=[row, row, vec], out_specs=[row, row, vec, vec],
        out_shape=[jax.ShapeDtypeStruct((t, d), f32), jax.ShapeDtypeStruct((t, d), bf16),
                   jax.ShapeDtypeStruct((1, d), f32), jax.ShapeDtypeStruct((1, d), f32)],
        compiler_params=_params(("arbitrary",)), name=name,
    )(z, dy, g)


def loss_head(y, target, *, name):
    t, d = y.shape
    tm = _tile(t, 512, SUBLANES)

    def body(y_ref, t_ref, dy_ref, sq_ref):
        e = y_ref[...] - t_ref[...]
        dy_ref[...] = e * (1.0 / d)
        part = jnp.sum(e * e, 0, keepdims=True)

        @pl.when(pl.program_id(0) == 0)
        def _():
            sq_ref[...] = part

        @pl.when(pl.program_id(0) > 0)
        def _():
            sq_ref[...] += part

    row = pl.BlockSpec((tm, d), lambda i: (i, 0))
    vec = pl.BlockSpec((1, d), lambda i: (0, 0))
    return pl.pallas_call(
        body, grid=(t // tm,), in_specs=[row, row], out_specs=[row, vec],
        out_shape=[jax.ShapeDtypeStruct((t, d), f32), jax.ShapeDtypeStruct((1, d), f32)],
        compiler_params=_params(("arbitrary",)), name=name,
    )(y, target)


FFN_COL_BLOCK = 256
FFN_ROWS = 1024


def _lane_blocks(n):
    return [slice(s, min(s + FFN_COL_BLOCK, n)) for s in range(0, n, FFN_COL_BLOCK)]


def ffn_fwd(x, wg, wu, wd, g, b, *, name):
    t, d = x.shape
    nf, _, tf = wg.shape
    tm = _tile(t, FFN_ROWS, SUBLANES)

    def body(x_ref, wg_ref, wu_ref, wd_ref, g_ref, b_ref, y_ref, z_ref, yb_ref, acc_ref):
        f = pl.program_id(1)
        xb = x_ref[...].astype(bf16)
        part, pending = None, None
        for cols in _lane_blocks(tf):
            gate_up = (_bdot(xb, wg_ref[:, cols], NN), _bdot(xb, wu_ref[:, cols], NN), cols)
            if pending is not None:
                down = _bdot(_silu(pending[0]) * pending[1], wd_ref[pending[2], :], NN)
                part = down if part is None else part + down
            pending = gate_up
        down = _bdot(_silu(pending[0]) * pending[1], wd_ref[pending[2], :], NN)
        part = down if part is None else part + down

        @pl.when(f == 0)
        def _():
            acc_ref[...] = part

        @pl.when(f > 0)
        def _():
            acc_ref[...] += part

        @pl.when(f == nf - 1)
        def _():
            z = DN_ALPHA * x_ref[...] + 0.5 * acc_ref[...]
            z_ref[...] = z
            y = _layer_norm(z, g_ref[...], b_ref[...])
            y_ref[...] = y
            yb_ref[...] = y.astype(bf16)

    row = pl.BlockSpec((tm, d), lambda i, j: (i, 0))
    vec = pl.BlockSpec((1, d), lambda i, j: (0, 0))
    wcol = pl.BlockSpec((None, d, tf), lambda i, j: (j, 0, 0))
    wrow = pl.BlockSpec((None, tf, d), lambda i, j: (j, 0, 0))
    return pl.pallas_call(
        body, grid=(t // tm, nf),
        in_specs=[row, wcol, wcol, wrow, vec, vec],
        out_specs=[row, row, row],
        out_shape=[jax.ShapeDtypeStruct((t, d), f32)] * 2 + [jax.ShapeDtypeStruct((t, d), bf16)],
        scratch_shapes=[pltpu.VMEM((tm, d), f32)],
        compiler_params=_params(("parallel", "arbitrary")), name=name,
    )(x, wg, wu, wd, g, b)


def ffn_bwd_weights(xb, dzb, wg, wu, wd, *, name):
    t, d = xb.shape
    nf, _, tf = wg.shape
    tm = _tile(t, FFN_ROWS, SUBLANES)
    nt = t // tm

    def body(x_ref, dz_ref, wg_ref, wu_ref, wd_ref, dgate_ref, dup_ref, owg_ref, owu_ref, owd_ref,
             dwg_ref, dwu_ref, dwd_ref):
        x = x_ref[...]
        dzh = dz_ref[...] * 0.5

        def first_half(cols):
            return _bdot(x, wg_ref[:, cols], NN), _bdot(x, wu_ref[:, cols], NN), _bdot(dzh, wd_ref[cols, :], NT), cols

        def second_half(gate, up, dh, cols):
            sg = jax.nn.sigmoid(gate)
            s = gate * sg
            dup = (dh * s).astype(bf16)
            dgate = (dh * up * (sg * (1.0 + gate * (1.0 - sg)))).astype(bf16)
            dgate_ref[:, cols] = dgate
            dup_ref[:, cols] = dup
            return _bdot(x, dgate, TN), _bdot(x, dup, TN), _bdot(s * up, dzh, TN), cols

        parts, pending = [], None
        for cols in _lane_blocks(tf):
            nxt = first_half(cols)
            if pending is not None:
                parts.append(second_half(*pending))
            pending = nxt
        parts.append(second_half(*pending))

        @pl.when(pl.program_id(1) == 0)
        def _():
            for pwg, pwu, pwd, cols in parts:
                dwg_ref[:, cols] = pwg
                dwu_ref[:, cols] = pwu
                dwd_ref[cols, :] = pwd

        @pl.when(pl.program_id(1) > 0)
        def _():
            for pwg, pwu, pwd, cols in parts:
                dwg_ref[:, cols] += pwg
                dwu_ref[:, cols] += pwu
                dwd_ref[cols, :] += pwd

        @pl.when(pl.program_id(1) == nt - 1)
        def _():
            owg_ref[...] = dwg_ref[...].astype(bf16)
            owu_ref[...] = dwu_ref[...].astype(bf16)
            owd_ref[...] = dwd_ref[...].astype(bf16)

    row = pl.BlockSpec((tm, d), lambda j, i: (i, 0))
    wcol = pl.BlockSpec((None, d, tf), lambda j, i: (j, 0, 0))
    wrow = pl.BlockSpec((None, tf, d), lambda j, i: (j, 0, 0))
    act = pl.BlockSpec((None, tm, tf), lambda j, i: (j, i, 0))
    return pl.pallas_call(
        body, grid=(nf, nt), in_specs=[row, row, wcol, wcol, wrow], out_specs=[act, act, wcol, wcol, wrow],
        out_shape=[jax.ShapeDtypeStruct((nf, t, tf), bf16), jax.ShapeDtypeStruct((nf, t, tf), bf16),
                   jax.ShapeDtypeStruct((nf, d, tf), bf16), jax.ShapeDtypeStruct((nf, d, tf), bf16),
                   jax.ShapeDtypeStruct((nf, tf, d), bf16)],
        scratch_shapes=[pltpu.VMEM((d, tf), f32), pltpu.VMEM((d, tf), f32), pltpu.VMEM((tf, d), f32)],
        compiler_params=_params(("parallel", "arbitrary")), name=name,
    )(xb, dzb, wg, wu, wd)


def ffn_bwd_input(dgate, dup, wg, wu, dz, *, name):
    nf, t, tf = dgate.shape
    d = wg.shape[1]
    tm = _tile(t, FFN_ROWS // 2, SUBLANES)

    def body(dg_ref, du_ref, wg_ref, wu_ref, dz_ref, dx_ref):
        acc = DN_ALPHA * dz_ref[...]
        for j in range(nf):
            acc = acc + _bdot(dg_ref[j], wg_ref[j], NT) + _bdot(du_ref[j], wu_ref[j], NT)
        dx_ref[...] = acc

    act = pl.BlockSpec((nf, tm, tf), lambda i: (0, i, 0))
    wsp = pl.BlockSpec((nf, d, tf), lambda i: (0, 0, 0))
    row = pl.BlockSpec((tm, d), lambda i: (i, 0))
    return pl.pallas_call(
        body, grid=(t // tm,), in_specs=[act, act, wsp, wsp, row], out_specs=row,
        out_shape=jax.ShapeDtypeStruct((t, d), f32),
        compiler_params=_params(("parallel",)), name=name,
    )(dgate, dup, wg, wu, dz)


def ple_fwd(x, p, wg, bg, wp, *, name):
    t, d = x.shape
    dp = p.shape[1]
    tm = _tile(t, 512, 2 * SUBLANES)

    def body(x_ref, p_ref, wg_ref, bg_ref, wp_ref, o_ref, ob_ref):
        x_ = x_ref[...]
        gate = jax.nn.sigmoid(_bdot(x_, wg_ref[...], NN) + bg_ref[...])
        out = x_ + gate * _bdot(p_ref[...], wp_ref[...], NN)
        o_ref[...] = out
        ob_ref[...] = out.astype(bf16)

    row = pl.BlockSpec((tm, d), lambda i: (i, 0))
    return pl.pallas_call(
        body, grid=(t // tm,),
        in_specs=[row, pl.BlockSpec((tm, dp), lambda i: (i, 0)), pl.BlockSpec((d, d), lambda i: (0, 0)),
                  pl.BlockSpec((1, d), lambda i: (0, 0)), pl.BlockSpec((dp, d), lambda i: (0, 0))],
        out_specs=[row, row], out_shape=[jax.ShapeDtypeStruct((t, d), f32), jax.ShapeDtypeStruct((t, d), bf16)],
        compiler_params=_params(("parallel",)), name=name,
    )(x, p, wg, bg, wp)


def ple_bwd(x, p, dy, wg, wgt, bg, wp, *, name):
    t, d = x.shape
    dp = p.shape[1]
    tm = _tile(t, 512, SUBLANES)

    def body(x_ref, p_ref, dy_ref, wg_ref, wgt_ref, bg_ref, wp_ref, dx_ref, dwg_ref, dbg_ref, dwp_ref):
        x_ = x_ref[...]
        dy_ = dy_ref[...]
        s = jax.nn.sigmoid(_bdot(x_, wg_ref[...], NN) + bg_ref[...])
        e = _bdot(p_ref[...], wp_ref[...], NN)
        da = dy_ * e * s * (1.0 - s)
        de = dy_ * s
        dx_ref[...] = dy_ + _bdot(da, wgt_ref[...], NN)
        pwg = _bdot(x_, da, TN)
        pbg = jnp.sum(da, 0, keepdims=True)
        pwp = _bdot(p_ref[...], de, TN)

        @pl.when(pl.program_id(0) == 0)
        def _():
            dwg_ref[...] = pwg
            dbg_ref[...] = pbg
            dwp_ref[...] = pwp

        @pl.when(pl.program_id(0) > 0)
        def _():
            dwg_ref[...] += pwg
            dbg_ref[...] += pbg
            dwp_ref[...] += pwp

    row = pl.BlockSpec((tm, d), lambda i: (i, 0))
    full = lambda shape: pl.BlockSpec(shape, lambda i: (0, 0))
    return pl.pallas_call(
        body, grid=(t // tm,),
        in_specs=[row, pl.BlockSpec((tm, dp), lambda i: (i, 0)), row, full((d, d)), full((d, d)), full((1, d)),
                  full((dp, d))],
        out_specs=[row, full((d, d)), full((1, d)), full((dp, d))],
        out_shape=[jax.ShapeDtypeStruct((t, d), f32), jax.ShapeDtypeStruct((d, d), f32),
                   jax.ShapeDtypeStruct((1, d), f32), jax.ShapeDtypeStruct((dp, d), f32)],
        compiler_params=_params(("arbitrary",)), name=name,
    )(x, p, dy, wg, wgt, bg, wp)


def _conv_taps(xpad_ref, w_ref, s):
    acc = w_ref[0:1, :] * xpad_ref[SUBLANES - 3:SUBLANES - 3 + s, :]
    for j in range(1, 4):
        acc = acc + w_ref[j:j + 1, :] * xpad_ref[SUBLANES - 3 + j:SUBLANES - 3 + j + s, :]
    return acc


def conv_fwd(x, w, bias, act, nb, *, name):
    t, c = x.shape
    s = t // nb
    cw = GROUP_W

    def body(x_ref, w_ref, b_ref, y_ref, xpad):
        xpad[0:SUBLANES, :] = jnp.zeros((SUBLANES, cw), f32)
        xpad[SUBLANES:, :] = x_ref[...]
        acc = _conv_taps(xpad, w_ref, s) + b_ref[...]
        y_ref[...] = _silu(acc) if act else acc

    slab = pl.BlockSpec((s, cw), lambda b, g: (b, g))
    return pl.pallas_call(
        body, grid=(nb, c // cw),
        in_specs=[slab, pl.BlockSpec((4, cw), lambda b, g: (0, g)), pl.BlockSpec((1, cw), lambda b, g: (0, g))],
        out_specs=slab, out_shape=jax.ShapeDtypeStruct((t, c), f32),
        scratch_shapes=[pltpu.VMEM((s + SUBLANES, cw), f32)],
        compiler_params=_params(("parallel", "parallel")), name=name,
    )(x, w, bias)


def conv_bwd(x, w, bias, dy, act, nb, *, name):
    t, c = x.shape
    s = t // nb
    cw = GROUP_W

    def body(x_ref, w_ref, b_ref, dy_ref, dx_ref, dw_ref, db_ref, xpad, dpad):
        xpad[0:SUBLANES, :] = jnp.zeros((SUBLANES, cw), f32)
        xpad[SUBLANES:, :] = x_ref[...]
        dacc = dy_ref[...]
        if act:
            acc = _conv_taps(xpad, w_ref, s) + b_ref[...]
            sg = jax.nn.sigmoid(acc)
            dacc = dacc * (sg * (1.0 + acc * (1.0 - sg)))
        dpad[0:s, :] = dacc
        dpad[s:, :] = jnp.zeros((SUBLANES, cw), f32)
        dx = w_ref[0:1, :] * dpad[3:3 + s, :]
        for j in range(1, 4):
            dx = dx + w_ref[j:j + 1, :] * dpad[3 - j:3 - j + s, :]
        dx_ref[...] = dx
        first = pl.program_id(1) == 0
        for j in range(4):
            pw = jnp.sum(dacc * xpad[SUBLANES - 3 + j:SUBLANES - 3 + j + s, :], 0, keepdims=True)

            @pl.when(first)
            def _():
                dw_ref[j:j + 1, :] = pw

            @pl.when(jnp.logical_not(first))
            def _():
                dw_ref[j:j + 1, :] += pw

        pb = jnp.sum(dacc, 0, keepdims=True)

        @pl.when(first)
        def _():
            db_ref[...] = pb

        @pl.when(jnp.logical_not(first))
        def _():
            db_ref[...] += pb

    slab = pl.BlockSpec((s, cw), lambda g, b: (b, g))
    wsp = pl.BlockSpec((4, cw), lambda g, b: (0, g))
    bsp = pl.BlockSpec((1, cw), lambda g, b: (0, g))
    return pl.pallas_call(
        body, grid=(c // cw, nb), in_specs=[slab, wsp, bsp, slab], out_specs=[slab, wsp, bsp],
        out_shape=[jax.ShapeDtypeStruct((t, c), f32), jax.ShapeDtypeStruct((4, c), f32),
                   jax.ShapeDtypeStruct((1, c), f32)],
        scratch_shapes=[pltpu.VMEM((s + SUBLANES, cw), f32), pltpu.VMEM((s + SUBLANES, cw), f32)],
        compiler_params=_params(("parallel", "arbitrary")), name=name,
    )(x, w, bias, dy)


def _each(f, *lists):
    return [f(*a) for a in zip(*lists)]


def _attn_heads(qs, kbs, vbs, sinks, valids, dist, dots):
    nn, nt = dots[:2]
    items = range(len(qs))
    kv = [(i // A_HEADS) * A_KV_HEADS + (i % A_HEADS) // A_GROUP for i in items]
    scs = [nt(qs[i], kbs[kv[i]]) for i in items]
    prs = []
    for i in items:
        h = i % A_HEADS
        sc = scs[i] * (A_HEAD_DIM ** -0.5) - 2.0 ** -(h + 1) * dist
        sc = jnp.where(valids[i // A_HEADS], sc, NEG)
        m = lax.stop_gradient(jnp.maximum(jnp.max(sc, -1, keepdims=True), sinks[h]))
        pr = jnp.exp(sc - m)
        den = jnp.sum(pr, -1, keepdims=True) + jnp.exp(sinks[h] - m)
        prs.append(pr / den)
    return [nn(prs[i], vbs[kv[i]]) for i in items]


A_Q_ROWS = 2 * CHUNK
A_STEPS_PER_TRIP = 2


def _attn_steps(s):
    return A_STEPS_PER_TRIP if s % (A_Q_ROWS * A_STEPS_PER_TRIP) == 0 else 1


def _attn_band_consts(r0):
    band = A_WINDOW + A_Q_ROWS
    qi = lax.broadcasted_iota(jnp.int32, (A_Q_ROWS, band), 0)
    kj = lax.broadcasted_iota(jnp.int32, (A_Q_ROWS, band), 1)
    dist = jnp.abs(qi + A_WINDOW - kj).astype(f32)
    qc, kc = qi // CHUNK, kj // CHUNK
    valid = ((kj + r0) >= A_WINDOW) & (kc >= qc) & (kc <= qc + A_WINDOW // CHUNK)
    return dist, valid


def attn_fwd(qkv, sinks, nb, *, name):
    t = qkv.shape[0]
    s = t // nb
    band = A_WINDOW + A_Q_ROWS
    hd = A_HEAD_DIM

    def body(qkv_ref, sink_ref, o_ref, kvpad):
        kvpad[0:A_WINDOW, :] = jnp.zeros((A_WINDOW, 2 * A_KV_WIDTH), f32)
        kvpad[A_WINDOW:, :] = qkv_ref[:, A_WIDTH:]

        def trip(n, carry):
            r0s = [pl.multiple_of((n * steps + j) * A_Q_ROWS, A_Q_ROWS) for j in range(steps)]
            consts = [_attn_band_consts(r0) for r0 in r0s]
            kbs = [kvpad[pl.ds(r0, band), kvh * hd:(kvh + 1) * hd] for r0 in r0s for kvh in range(A_KV_HEADS)]
            vbs = [kvpad[pl.ds(r0, band), A_KV_WIDTH + kvh * hd:A_KV_WIDTH + (kvh + 1) * hd]
                   for r0 in r0s for kvh in range(A_KV_HEADS)]
            qs = [qkv_ref[pl.ds(r0, A_Q_ROWS), h * hd:(h + 1) * hd] for r0 in r0s for h in range(A_HEADS)]
            outs = _attn_heads(qs, kbs, vbs, [sink_ref[:, h:h + 1] for h in range(A_HEADS)], [c_[1] for c_ in consts],
                               consts[0][0], RAW_DOTS)
            for j, r0 in enumerate(r0s):
                for h in range(A_HEADS):
                    o_ref[pl.ds(r0, A_Q_ROWS), h * hd:(h + 1) * hd] = outs[j * A_HEADS + h]
            return carry

        steps = _attn_steps(s)
        lax.fori_loop(0, s // (A_Q_ROWS * steps), trip, 0)

    return pl.pallas_call(
        body, grid=(nb,),
        in_specs=[pl.BlockSpec((s, A_WIDTH + 2 * A_KV_WIDTH), lambda b: (b, 0)),
                  pl.BlockSpec((1, A_HEADS), lambda b: (0, 0))],
        out_specs=pl.BlockSpec((s, A_WIDTH), lambda b: (b, 0)),
        out_shape=jax.ShapeDtypeStruct((t, A_WIDTH), f32),
        scratch_shapes=[pltpu.VMEM((s + A_WINDOW, 2 * A_KV_WIDTH), f32)],
        compiler_params=_params(("parallel",)), name=name,
    )(qkv, sinks)


def attn_bwd(qkv, sinks, do, nb, *, name):
    t = qkv.shape[0]
    s = t // nb
    band = A_WINDOW + A_Q_ROWS
    hd = A_HEAD_DIM
    kvw = 2 * A_KV_WIDTH

    def body(qkv_ref, sink_ref, do_ref, dqkv_ref, dsink_ref, kvpad, dkvpad):
        kvpad[0:A_WINDOW, :] = jnp.zeros((A_WINDOW, kvw), f32)
        kvpad[A_WINDOW:, :] = qkv_ref[:, A_WIDTH:]
        dkvpad[...] = jnp.zeros((s + A_WINDOW, kvw), f32)

        def trip(n, dsinks):
            r0s = [pl.multiple_of((n * steps + j) * A_Q_ROWS, A_Q_ROWS) for j in range(steps)]
            consts = [_attn_band_consts(r0) for r0 in r0s]
            ksl = [slice(kvh * hd, (kvh + 1) * hd) for kvh in range(A_KV_HEADS)]
            vsl = [slice(A_KV_WIDTH + kvh * hd, A_KV_WIDTH + (kvh + 1) * hd) for kvh in range(A_KV_HEADS)]
            kbs = [kvpad[pl.ds(r0, band), sl] for r0 in r0s for sl in ksl]
            vbs = [kvpad[pl.ds(r0, band), sl] for r0 in r0s for sl in vsl]
            qs = [qkv_ref[pl.ds(r0, A_Q_ROWS), h * hd:(h + 1) * hd] for r0 in r0s for h in range(A_HEADS)]
            dos = [do_ref[pl.ds(r0, A_Q_ROWS), h * hd:(h + 1) * hd] for r0 in r0s for h in range(A_HEADS)]
            fn = functools.partial(_attn_heads, valids=[c_[1] for c_ in consts], dist=consts[0][0], dots=VJP_DOTS)
            _, vjp = jax.vjp(fn, qs, kbs, vbs, [sink_ref[:, h:h + 1] for h in range(A_HEADS)])
            dqs, dks, dvs, dss = vjp(dos)
            for j, r0 in enumerate(r0s):
                for h in range(A_HEADS):
                    dqkv_ref[pl.ds(r0, A_Q_ROWS), h * hd:(h + 1) * hd] = dqs[j * A_HEADS + h]
            for j, r0 in enumerate(r0s):
                for kvh in range(A_KV_HEADS):
                    dkvpad[pl.ds(r0, band), ksl[kvh]] += dks[j * A_KV_HEADS + kvh]
                    dkvpad[pl.ds(r0, band), vsl[kvh]] += dvs[j * A_KV_HEADS + kvh]
            return tuple(dsinks[h] + dss[h] for h in range(A_HEADS))

        steps = _attn_steps(s)
        dsinks = lax.fori_loop(0, s // (A_Q_ROWS * steps), trip, tuple(jnp.zeros((1, 1), f32) for _ in range(A_HEADS)))
        dqkv_ref[:, A_WIDTH:] = dkvpad[A_WINDOW:, :]
        first = pl.program_id(0) == 0
        for h in range(A_HEADS):
            @pl.when(first)
            def _():
                dsink_ref[:, h:h + 1] = dsinks[h]

            @pl.when(jnp.logical_not(first))
            def _():
                dsink_ref[:, h:h + 1] += dsinks[h]

    wq = A_WIDTH + kvw
    return pl.pallas_call(
        body, grid=(nb,),
        in_specs=[pl.BlockSpec((s, wq), lambda b: (b, 0)), pl.BlockSpec((1, A_HEADS), lambda b: (0, 0)),
                  pl.BlockSpec((s, A_WIDTH), lambda b: (b, 0))],
        out_specs=[pl.BlockSpec((s, wq), lambda b: (b, 0)), pl.BlockSpec((1, A_HEADS), lambda b: (0, 0))],
        out_shape=[jax.ShapeDtypeStruct((t, wq), f32), jax.ShapeDtypeStruct((1, A_HEADS), f32)],
        scratch_shapes=[pltpu.VMEM((s + A_WINDOW, kvw), f32), pltpu.VMEM((s + A_WINDOW, kvw), f32)],
        compiler_params=_params(("arbitrary",)), name=name,
    )(qkv, sinks, do)


def _rg_gates(xc, wa, wx, ba, bx, lam, nn):
    r = jax.nn.sigmoid(nn(xc, wa) + ba)
    i = jax.nn.sigmoid(nn(xc, wx) + bx)
    log_a = -RG_C * r * jax.nn.softplus(-lam)
    a = jnp.exp(log_a)
    mult = jnp.sqrt(-jnp.tanh(log_a) * (jnp.exp(2.0 * log_a) + 1.0))
    return a, mult * (i * xc)


def _linear_scan(a, u, reverse):
    s = a.shape[0]
    t = lax.broadcasted_iota(jnp.int32, a.shape, 0)
    d = 1
    while d < s:
        if reverse:
            keep = t < s - d
            shift = s - d
        else:
            keep = t >= d
            shift = d
        us = jnp.where(keep, pltpu.roll(u, shift, 0), 0.0)
        as_ = jnp.where(keep, pltpu.roll(a, shift, 0), 1.0)
        u = u + a * us
        a = a * as_
        d *= 2
    return u


def rglru_fwd(xc, bg, wa, wx, ba, bx, lam, nb, *, name):
    t, c = xc.shape
    s = t // nb
    cw = GROUP_W

    def body(xc_ref, bg_ref, wa_ref, wx_ref, ba_ref, bx_ref, lam_ref, y_ref, h_ref):
        a, u = _rg_gates(xc_ref[...], wa_ref[...], wx_ref[...], ba_ref[...], bx_ref[...], lam_ref[...], RAW_DOTS[0])
        h = _linear_scan(a, u, False)
        h_ref[...] = h
        y_ref[...] = h * jax.nn.gelu(bg_ref[...])

    slab = pl.BlockSpec((s, cw), lambda b, g: (b, g))
    wsp = pl.BlockSpec((None, cw, cw), lambda b, g: (g, 0, 0))
    vec = pl.BlockSpec((1, cw), lambda b, g: (0, g))
    return pl.pallas_call(
        body, grid=(nb, c // cw), in_specs=[slab, slab, wsp, wsp, vec, vec, vec], out_specs=[slab, slab],
        out_shape=[jax.ShapeDtypeStruct((t, c), f32)] * 2,
        compiler_params=_params(("parallel", "parallel")), name=name,
    )(xc, bg, wa, wx, ba, bx, lam)


def rglru_bwd(xc, bg, h, dy, wa, wx, ba, bx, lam, nb, *, name):
    t, c = xc.shape
    s = t // nb
    cw = GROUP_W

    def body(xc_ref, bg_ref, h_ref, dy_ref, wa_ref, wx_ref, ba_ref, bx_ref, lam_ref,
             dxc_ref, dbg_ref, dwa_ref, dwx_ref, dba_ref, dbx_ref, dlam_ref):
        h = h_ref[...]
        dy_ = dy_ref[...]
        gel, gel_vjp = jax.vjp(jax.nn.gelu, bg_ref[...])
        dbg_ref[...] = gel_vjp(dy_ * h)[0]
        dh = dy_ * gel
        gates = functools.partial(_rg_gates, nn=_bnn)
        (a, _), gates_vjp = jax.vjp(gates, xc_ref[...], wa_ref[...], wx_ref[...], ba_ref[...], bx_ref[...],
                                    lam_ref[...])
        ti = lax.broadcasted_iota(jnp.int32, a.shape, 0)
        a_next = jnp.where(ti < s - 1, pltpu.roll(a, s - 1, 0), 0.0)
        lam_t = _linear_scan(a_next, dh, True)
        h_prev = jnp.where(ti >= 1, pltpu.roll(h, 1, 0), 0.0)
        dxc, dwa, dwx, dba, dbx, dlam = gates_vjp((lam_t * h_prev, lam_t))
        dxc_ref[...] = dxc
        first = pl.program_id(1) == 0

        @pl.when(first)
        def _():
            dwa_ref[...] = dwa
            dwx_ref[...] = dwx
            dba_ref[...] = dba
            dbx_ref[...] = dbx
            dlam_ref[...] = dlam

        @pl.when(jnp.logical_not(first))
        def _():
            dwa_ref[...] += dwa
            dwx_ref[...] += dwx
            dba_ref[...] += dba
            dbx_ref[...] += dbx
            dlam_ref[...] += dlam

    slab = pl.BlockSpec((s, cw), lambda g, b: (b, g))
    wsp = pl.BlockSpec((None, cw, cw), lambda g, b: (g, 0, 0))
    vec = pl.BlockSpec((1, cw), lambda g, b: (0, g))
    ng = c // cw
    return pl.pallas_call(
        body, grid=(ng, nb), in_specs=[slab, slab, slab, slab, wsp, wsp, vec, vec, vec],
        out_specs=[slab, slab, wsp, wsp, vec, vec, vec],
        out_shape=[jax.ShapeDtypeStruct((t, c), f32), jax.ShapeDtypeStruct((t, c), f32),
                   jax.ShapeDtypeStruct((ng, cw, cw), f32), jax.ShapeDtypeStruct((ng, cw, cw), f32),
                   jax.ShapeDtypeStruct((1, c), f32), jax.ShapeDtypeStruct((1, c), f32),
                   jax.ShapeDtypeStruct((1, c), f32)],
        compiler_params=_params(("parallel", "arbitrary")), name=name,
    )(xc, bg, h, dy, wa, wx, ba, bx, lam)


def _gdn_chunks_prep(qs, ks, vs, bls, als, a_log, dt_b, dots):
    nn, nt, csum = dots[0], dots[1], dots[3]
    hd = C_HEAD_DIM
    ri = lax.broadcasted_iota(jnp.int32, (CHUNK, CHUNK), 0)
    ci = lax.broadcasted_iota(jnp.int32, (CHUNK, CHUNK), 1)
    tril = ri >= ci
    strict = ri > ci
    eye = (ri == ci).astype(f32)
    qn = [q * lax.rsqrt(jnp.sum(q * q, -1, keepdims=True) + NORM_EPS) * (hd ** -0.5) for q in qs]
    kn = [k * lax.rsqrt(jnp.sum(k * k, -1, keepdims=True) + NORM_EPS) for k in ks]
    beta = [jax.nn.sigmoid(bl) for bl in bls]
    g = [-jnp.exp(a_log) * jax.nn.softplus(al + dt_b) for al in als]
    gc_sq = [csum(jnp.broadcast_to(g_, (CHUNK, CHUNK))) for g_ in g]
    gc = [csum(jnp.broadcast_to(g_, (CHUNK, hd))) for g_ in g]
    decay = [jnp.where(tril, jnp.exp(jnp.where(tril, s - s.T, 0.0)), 0.0) for s in gc_sq]
    kb = _each(jnp.multiply, kn, beta)
    kk = _each(nt, kb, kn)
    pw = [-jnp.where(strict, a * d, 0.0) for a, d in zip(kk, decay)]
    inv = [eye + p_ for p_ in pw]
    for _ in range(5):
        pw = _each(nn, pw, pw)
        inv = _each(jnp.add, inv, _each(nn, inv, pw))
    egc = [jnp.exp(c_) for c_ in gc]
    u = _each(nn, inv, _each(jnp.multiply, vs, beta))
    w = _each(nn, inv, _each(jnp.multiply, kb, egc))
    attn = _each(jnp.multiply, _each(nt, qn, kn), decay)
    g_last = [jnp.sum(jnp.broadcast_to(g_, (CHUNK, hd)), 0, keepdims=True) for g_ in g]
    qg = _each(jnp.multiply, qn, egc)
    kdec = [k_ * jnp.exp(gl_ - c_) for k_, gl_, c_ in zip(kn, g_last, gc)]
    return [(qg[i], kdec[i], w[i], u[i], attn[i], jnp.exp(g_last[i])) for i in range(len(qs))]


def _gdn_heads_step(states, qgs, kdecs, ws, us, attns, gls, zs, ng, dots):
    nn, tn = dots[0], dots[2]
    v_new = _each(jnp.subtract, us, _each(nn, ws, states))
    o = _each(jnp.add, _each(nn, qgs, states), _each(nn, attns, v_new))
    new = [s * gl for s, gl in zip(states, gls)]
    new = _each(jnp.add, new, _each(tn, kdecs, v_new))
    y = [o_ * lax.rsqrt(jnp.mean(o_ * o_, -1, keepdims=True) + NORM_EPS) * ng * _silu(z) for o_, z in zip(o, zs)]
    return y, new


def _loop_unrolled(n, unroll, load, compute, store, init):
    u = unroll if n % unroll == 0 else 1

    def trip(i, carry):
        idx = [i * u + j for j in range(u)]
        loaded = [load(k) for k in idx]
        results = compute(loaded)
        for k, r in zip(idx, results):
            carry = store(k, r, carry)
        return carry

    return lax.fori_loop(0, n // u, trip, init)


def _pick_lane(x, lane):
    li = lax.broadcasted_iota(jnp.int32, x.shape, 1)
    return jnp.sum(jnp.where(li == lane, x, 0.0), 1, keepdims=True)


def _put_lane(col, lane, width):
    li = lax.broadcasted_iota(jnp.int32, (col.shape[0], width), 1)
    return jnp.where(li == lane, col, 0.0)


def _gdn_specs(s, nc):
    hd = C_HEAD_DIM
    head = lambda off: pl.BlockSpec((s, hd), lambda b, h, off=off: (b, off + h))
    attn = pl.BlockSpec((None, s, CHUNK), lambda b, h: (h, b, 0))
    gl = pl.BlockSpec((None, nc * SUBLANES, hd), lambda b, h: (h, b, 0))
    ba = pl.BlockSpec((s, LANES), lambda b, h: (b, 0))
    sc8 = pl.BlockSpec((1, C_HEADS), lambda b, h: (0, 0))
    return head, attn, gl, ba, sc8


def gdn_prep_fwd(qkv, ba, a_log, dt_b, nb, *, name):
    t = qkv.shape[0]
    s = t // nb
    nc = s // CHUNK
    hd = C_HEAD_DIM
    head, attn_sp, gl_sp, ba_sp, sc8 = _gdn_specs(s, nc)

    def body(q_ref, k_ref, v_ref, ba_ref, alog_ref, dtb_ref, qg_ref, kd_ref, w_ref, u_ref, at_ref, gl_ref):
        h = pl.program_id(1)
        a_log_h = _pick_lane(alog_ref[...], h)
        dt_b_h = _pick_lane(dtb_ref[...], h)

        def load(n):
            rows = pl.ds(pl.multiple_of(n * CHUNK, CHUNK), CHUNK)
            bav = ba_ref[rows, :]
            return q_ref[rows, :], k_ref[rows, :], v_ref[rows, :], _pick_lane(bav, h), _pick_lane(bav, C_HEADS + h)

        def compute(loaded):
            return _gdn_chunks_prep(*[list(x) for x in zip(*loaded)], a_log_h, dt_b_h, RAW_DOTS)

        def store(n, outs, carry):
            rows = pl.ds(pl.multiple_of(n * CHUNK, CHUNK), CHUNK)
            qg_ref[rows, :] = outs[0].astype(bf16)
            kd_ref[rows, :] = outs[1].astype(bf16)
            w_ref[rows, :] = outs[2].astype(bf16)
            u_ref[rows, :] = outs[3]
            at_ref[rows, :] = outs[4].astype(bf16)
            gl_ref[pl.ds(pl.multiple_of(n * SUBLANES, SUBLANES), SUBLANES), :] = jnp.broadcast_to(outs[5], (SUBLANES, hd))
            return carry

        _loop_unrolled(nc, PREP_FWD_UNROLL, load, compute, store, 0)

    big = jax.ShapeDtypeStruct((t, C_WIDTH), f32)
    bigb = jax.ShapeDtypeStruct((t, C_WIDTH), bf16)
    return pl.pallas_call(
        body, grid=(nb, C_HEADS),
        in_specs=[head(0), head(C_HEADS), head(2 * C_HEADS), ba_sp, sc8, sc8],
        out_specs=[head(0)] * 4 + [attn_sp, gl_sp],
        out_shape=[bigb, bigb, bigb, big, jax.ShapeDtypeStruct((C_HEADS, t, CHUNK), bf16),
                               jax.ShapeDtypeStruct((C_HEADS, nb * nc * SUBLANES, hd), f32)],
        compiler_params=_params(("parallel", "parallel")), name=name,
    )(qkv, qkv, qkv, ba, a_log, dt_b)


def gdn_prep_bwd(qkv, ba, a_log, dt_b, cts, nb, *, name):
    t = qkv.shape[0]
    s = t // nb
    nc = s // CHUNK
    hd = C_HEAD_DIM
    head, attn_sp, gl_sp, ba_sp, sc8 = _gdn_specs(s, nc)

    def body(q_ref, k_ref, v_ref, ba_ref, alog_ref, dtb_ref, cqg, ckd, cw_, cu, cat, cgl,
             dq_ref, dk_ref, dv_ref, dba_ref, dalog_ref, ddtb_ref):
        b = pl.program_id(0)
        h = pl.program_id(1)
        a_log_h = _pick_lane(alog_ref[...], h)
        dt_b_h = _pick_lane(dtb_ref[...], h)
        prep = functools.partial(_gdn_chunks_prep, dots=VJP_DOTS)

        @pl.when(h == 0)
        def _():
            dba_ref[...] = jnp.zeros((s, LANES), f32)

        def load(n):
            rows = pl.ds(pl.multiple_of(n * CHUNK, CHUNK), CHUNK)
            bav = ba_ref[rows, :]
            cgl_n = cgl[pl.ds(pl.multiple_of(n * SUBLANES, SUBLANES), SUBLANES), :][0:1, :]
            primals = (q_ref[rows, :], k_ref[rows, :], v_ref[rows, :], _pick_lane(bav, h), _pick_lane(bav, C_HEADS + h))
            return primals, (cqg[rows, :], ckd[rows, :], cw_[rows, :], cu[rows, :], cat[rows, :], cgl_n), dba_ref[rows, :]

        def compute(loaded):
            primals = [list(x) for x in zip(*[item[0] for item in loaded])]
            _, vjp = jax.vjp(prep, *primals, a_log_h, dt_b_h)
            dqs, dks, dvs, dbls, dals, dalog, ddtb = vjp([item[1] for item in loaded])
            zero = jnp.zeros((1, 1), f32)
            return [((dqs[i], dks[i], dvs[i], dbls[i], dals[i], dalog if i == 0 else zero, ddtb if i == 0 else zero),
                     loaded[i][2]) for i in range(len(loaded))]

        def store(n, res, carry):
            (dq, dk, dv, dbl, dal, dalog_n, ddtb_n), dba_old = res
            rows = pl.ds(pl.multiple_of(n * CHUNK, CHUNK), CHUNK)
            dq_ref[rows, :] = dq
            dk_ref[rows, :] = dk
            dv_ref[rows, :] = dv
            dba_ref[rows, :] = dba_old + _put_lane(dbl, h, LANES) + _put_lane(dal, C_HEADS + h, LANES)
            return carry[0] + dalog_n, carry[1] + ddtb_n

        da_log, ddt_b = _loop_unrolled(nc, PREP_BWD_UNROLL, load, compute, store,
                                       (jnp.zeros((1, 1), f32), jnp.zeros((1, 1), f32)))
        first = jnp.logical_and(b == 0, h == 0)

        @pl.when(first)
        def _():
            dalog_ref[...] = _put_lane(da_log, h, LANES)
            ddtb_ref[...] = _put_lane(ddt_b, h, LANES)

        @pl.when(jnp.logical_not(first))
        def _():
            dalog_ref[...] += _put_lane(da_log, h, LANES)
            ddtb_ref[...] += _put_lane(ddt_b, h, LANES)

    big = jax.ShapeDtypeStruct((t, C_WIDTH), f32)
    vec = pl.BlockSpec((1, LANES), lambda b, h: (0, 0))
    return pl.pallas_call(
        body, grid=(nb, C_HEADS),
        in_specs=[head(0), head(C_HEADS), head(2 * C_HEADS), ba_sp, sc8, sc8] + [head(0)] * 4 + [attn_sp, gl_sp],
        out_specs=[head(0)] * 3 + [ba_sp, vec, vec],
        out_shape=[big] * 3 + [jax.ShapeDtypeStruct((t, LANES), f32), jax.ShapeDtypeStruct((1, LANES), f32),
                               jax.ShapeDtypeStruct((1, LANES), f32)],
        compiler_params=_params(("arbitrary", "arbitrary")), name=name,
    )(qkv, qkv, qkv, ba, a_log, dt_b, *cts)


def _gdn_rec_specs(sb, nsb, hp, reverse):
    hd = C_HEAD_DIM
    ncb = sb // CHUNK
    blk = (lambda b, k: b * nsb + (nsb - 1 - k)) if reverse else (lambda b, k: b * nsb + k)
    wide = pl.BlockSpec((sb, hp * hd), lambda b, j, k: (blk(b, k), j))
    attn = pl.BlockSpec((hp, sb, CHUNK), lambda b, j, k: (j, blk(b, k), 0))
    gl = pl.BlockSpec((hp, ncb * SUBLANES, hd), lambda b, j, k: (j, blk(b, k), 0))
    ng = pl.BlockSpec((1, hd), lambda b, j, k: (0, 0))
    states = pl.BlockSpec((hp, ncb, hd, hd), lambda b, j, k: (j, blk(b, k), 0, 0))
    return wide, attn, gl, ng, states


def gdn_rec_fwd(qg, kdec, w, u, attn, gl, z, ng, nb, *, name):
    t = qg.shape[0]
    s = t // nb
    sb = min(s, GDN_TIME_BLOCK)
    nsb = s // sb
    hd = C_HEAD_DIM
    hp = C_HEADS_PER_STEP
    wide, attn_sp, gl_sp, ng_sp, st_sp = _gdn_rec_specs(sb, nsb, hp, False)

    def body(qg_ref, kd_ref, w_ref, u_ref, at_ref, gl_ref, z_ref, ng_ref, y_ref, st_ref, carry_ref):
        @pl.when(pl.program_id(2) == 0)
        def _():
            carry_ref[...] = jnp.zeros((hp, hd, hd), f32)

        def chunk(n, states):
            for j in range(hp):
                st_ref[j, n] = states[j]
            rows = pl.ds(pl.multiple_of(n * CHUNK, CHUNK), CHUNK)
            grow = pl.ds(pl.multiple_of(n * SUBLANES, SUBLANES), SUBLANES)
            cols = [slice(j * hd, (j + 1) * hd) for j in range(hp)]
            ins = [(qg_ref[rows, c], kd_ref[rows, c], w_ref[rows, c], u_ref[rows, c], at_ref[j, rows, :],
                    gl_ref[j, grow, :][0:1, :], z_ref[rows, c]) for j, c in enumerate(cols)]
            ys, new = _gdn_heads_step(list(states), *[list(x) for x in zip(*ins)], ng_ref[...], RAW_DOTS)
            for j in range(hp):
                y_ref[rows, cols[j]] = ys[j]
            return tuple(new)

        last = lax.fori_loop(0, sb // CHUNK, chunk, tuple(carry_ref[j] for j in range(hp)))
        for j in range(hp):
            carry_ref[j] = last[j]

    return pl.pallas_call(
        body, grid=(nb, C_HEADS // hp, nsb),
        in_specs=[wide] * 4 + [attn_sp, gl_sp, wide, ng_sp], out_specs=[wide, st_sp],
        out_shape=[jax.ShapeDtypeStruct((t, C_WIDTH), f32), jax.ShapeDtypeStruct((C_HEADS, t // CHUNK, hd, hd), f32)],
        scratch_shapes=[pltpu.VMEM((hp, hd, hd), f32)],
        compiler_params=_params(("parallel", "parallel", "arbitrary")), name=name,
    )(qg, kdec, w, u, attn, gl, z, ng)


def gdn_rec_bwd(qg, kdec, w, u, attn, gl, z, ng, states, dy, nb, *, name):
    t = qg.shape[0]
    s = t // nb
    sb = min(s, GDN_TIME_BLOCK)
    nsb = s // sb
    nc = sb // CHUNK
    hd = C_HEAD_DIM
    hp = C_HEADS_PER_STEP
    wide, attn_sp, gl_sp, ng_sp, st_sp = _gdn_rec_specs(sb, nsb, hp, True)

    def body(qg_ref, kd_ref, w_ref, u_ref, at_ref, gl_ref, z_ref, ng_ref, states, dy_ref,
             dqg_ref, dkd_ref, dw_ref, du_ref, dat_ref, dgl_ref, dz_ref, dng_ref, carry_ref):
        step = functools.partial(_gdn_heads_step, dots=VJP_DOTS)

        @pl.when(pl.program_id(2) == 0)
        def _():
            carry_ref[...] = jnp.zeros((hp, hd, hd), f32)

        def operands(n):
            rows = pl.ds(pl.multiple_of(n * CHUNK, CHUNK), CHUNK)
            grow = pl.ds(pl.multiple_of(n * SUBLANES, SUBLANES), SUBLANES)
            cols = [slice(j * hd, (j + 1) * hd) for j in range(hp)]
            return ([qg_ref[rows, c].astype(f32) for c in cols], [kd_ref[rows, c].astype(f32) for c in cols],
                    [w_ref[rows, c].astype(f32) for c in cols], [u_ref[rows, c] for c in cols],
                    [at_ref[j, rows, :].astype(f32) for j in range(hp)],
                    [gl_ref[j, grow, :][0:1, :] for j in range(hp)], [z_ref[rows, c] for c in cols])

        def bwd_chunk(i, carry):
            n = nc - 1 - i
            rows = pl.ds(pl.multiple_of(n * CHUNK, CHUNK), CHUNK)
            grow = pl.ds(pl.multiple_of(n * SUBLANES, SUBLANES), SUBLANES)
            dsts, dng = carry
            dys = [dy_ref[rows, j * hd:(j + 1) * hd] for j in range(hp)]
            _, vjp = jax.vjp(step, [states[j, n] for j in range(hp)], *operands(n), ng_ref[...])
            dst, dqg, dkd, dw, du, dat, dgl, dz, dng_n = vjp((dys, list(dsts)))
            for j in range(hp):
                cols = slice(j * hd, (j + 1) * hd)
                dqg_ref[rows, cols] = dqg[j]
                dkd_ref[rows, cols] = dkd[j]
                dw_ref[rows, cols] = dw[j]
                du_ref[rows, cols] = du[j]
                dat_ref[j, rows, :] = dat[j]
                dgl_ref[j, grow, :] = jnp.broadcast_to(dgl[j], (SUBLANES, hd))
                dz_ref[rows, cols] = dz[j]
            return tuple(dst), dng + dng_n

        dlast, dng = lax.fori_loop(0, nc, bwd_chunk,
                                   (tuple(carry_ref[j] for j in range(hp)), jnp.zeros((1, hd), f32)))
        for j in range(hp):
            carry_ref[j] = dlast[j]
        first = jnp.logical_and(jnp.logical_and(pl.program_id(0) == 0, pl.program_id(1) == 0), pl.program_id(2) == 0)

        @pl.when(first)
        def _():
            dng_ref[...] = dng

        @pl.when(jnp.logical_not(first))
        def _():
            dng_ref[...] += dng

    big = jax.ShapeDtypeStruct((t, C_WIDTH), f32)
    return pl.pallas_call(
        body, grid=(nb, C_HEADS // hp, nsb),
        in_specs=[wide] * 4 + [attn_sp, gl_sp, wide, ng_sp, st_sp, wide],
        out_specs=[wide] * 4 + [attn_sp, gl_sp, wide, ng_sp],
        out_shape=[big] * 4 + [jax.ShapeDtypeStruct(attn.shape, f32), jax.ShapeDtypeStruct(gl.shape, f32), big,
                               jax.ShapeDtypeStruct((1, hd), f32)],
        scratch_shapes=[pltpu.VMEM((hp, hd, hd), f32)],
        compiler_params=_params(("arbitrary", "arbitrary", "arbitrary")), name=name,
    )(qg, kdec, w, u, attn, gl, z, ng, states, dy)


def _blockdiag_slabs(w):
    per = GROUP_W // B_BLOCK
    slabs = jnp.zeros((B_BLOCKS // per, GROUP_W, GROUP_W), w.dtype)
    for h in range(B_BLOCKS):
        o = (h % per) * B_BLOCK
        slabs = slabs.at[h // per, o:o + B_BLOCK, o:o + B_BLOCK].set(w[h])
    return slabs


def _slab_blocks(slabs):
    per = GROUP_W // B_BLOCK
    return jnp.stack([slabs[h // per, (h % per) * B_BLOCK:(h % per + 1) * B_BLOCK,
                            (h % per) * B_BLOCK:(h % per + 1) * B_BLOCK] for h in range(B_BLOCKS)])


def _mixer_ab_fwd(x1, x1b, W, g, b, nb, tag):
    w_in = W["ab_w_in"][0].astype(bf16)
    o1, o2 = A_WIDTH + 2 * A_KV_WIDTH, A_WIDTH + 2 * A_KV_WIDTH + B_WIDTH
    w_qkv, w_bx, w_bg = w_in[:, :o1], w_in[:, o1:o2], w_in[:, o2:]
    pqkv = mm_nn(x1b,w_qkv, name=tag + "_in_qkv")
    pbx = mm_nn(x1b,w_bx, name=tag + "_in_bx")
    pbg = mm_nn(x1b,w_bg, name=tag + "_in_bg")
    ya = attn_fwd(pqkv, W["a_sinks"], nb, name=tag + "_attn_fwd")
    xc = conv_fwd(pbx, W["b_conv_w"][0], W["b_conv_b"], False, nb, name=tag + "_conv_fwd")
    wa_s, wx_s = _blockdiag_slabs(W["b_wa"][0]), _blockdiag_slabs(W["b_wx"][0])
    yb, hh = rglru_fwd(xc, pbg, wa_s, wx_s, W["b_ba"], W["b_bx"], W["b_lam"], nb, name=tag + "_rglru_fwd")
    w_out = W["ab_w_out"][0].astype(bf16)
    x2, z1, x2b = proj_ln([ya, yb], [w_out[:A_WIDTH], w_out[A_WIDTH:]], x1, g, b, name=tag + "_out_ln")
    saved = (pqkv, pbx, pbg, ya, xc, yb, hh, wa_s, wx_s, w_qkv, w_bx, w_bg, w_out)
    return x2, x2b, z1, saved


def _mixer_ab_bwd(x1b, dz1, dz1b, W, saved, nb, tag):
    pqkv, pbx, pbg, ya, xc, yb, hh, wa_s, wx_s, w_qkv, w_bx, w_bg, w_out = saved
    dya = mm_nn(dz1b, w_out[:A_WIDTH].T, name=tag + "_dya")
    dyb = mm_nn(dz1b, w_out[A_WIDTH:].T, name=tag + "_dyb")
    dwo = jnp.concatenate([mm_tn(ya, dz1b, name=tag + "_dwo_a"), mm_tn(yb, dz1b, name=tag + "_dwo_b")], 0)
    dpqkv, dsinks = attn_bwd(pqkv, W["a_sinks"], dya, nb, name=tag + "_attn_bwd")
    dxc, dpbg, dwa_s, dwx_s, dba, dbx, dlam = rglru_bwd(xc, pbg, hh, dyb, wa_s, wx_s, W["b_ba"], W["b_bx"],
                                                       W["b_lam"], nb, name=tag + "_rglru_bwd")
    dpbx, dconv_w, dconv_b = conv_bwd(pbx, W["b_conv_w"][0], W["b_conv_b"], dxc, False, nb, name=tag + "_conv_bwd")
    dw_in = jnp.concatenate([mm_tn(x1b,dpqkv, name=tag + "_dwin_qkv"), mm_tn(x1b,dpbx, name=tag + "_dwin_bx"),
                             mm_tn(x1b,dpbg, name=tag + "_dwin_bg")], 1)
    dx1 = mm_nn(dpqkv, w_qkv.T, add=dz1, add_scale=DN_ALPHA, name=tag + "_dx_qkv")
    dx1 = mm_nn(dpbx, w_bx.T, add=dx1, name=tag + "_dx_bx")
    dx1 = mm_nn(dpbg, w_bg.T, add=dx1, name=tag + "_dx_bg")
    grads = {"ab_w_in": dw_in[None], "a_sinks": dsinks, "b_conv_w": dconv_w[None], "b_conv_b": dconv_b,
             "b_wa": _slab_blocks(dwa_s)[None], "b_ba": dba, "b_wx": _slab_blocks(dwx_s)[None], "b_bx": dbx,
             "b_lam": dlam, "ab_w_out": dwo[None]}
    return dx1, grads


def _mixer_c_fwd(x1, x1b, W, g, b, nb, tag):
    w_in = W["c_w_in"][0].astype(bf16)
    d = w_in.shape[0]
    o1, o2 = 3 * C_WIDTH, 4 * C_WIDTH
    w_qkv, w_z = w_in[:, :o1], w_in[:, o1:o2]
    w_ba = jnp.concatenate([w_in[:, o2:], jnp.zeros((d, LANES - 2 * C_HEADS), bf16)], 1)
    pqkv = mm_nn(x1b,w_qkv, name=tag + "_in_qkv")
    pz = mm_nn(x1b,w_z, name=tag + "_in_z")
    pba = mm_nn(x1b,w_ba, name=tag + "_in_ba")
    zero_b = jnp.zeros((1, o1), f32)
    qkvc = conv_fwd(pqkv, W["c_conv_w"][0], zero_b, True, nb, name=tag + "_conv_fwd")
    prep = gdn_prep_fwd(qkvc, pba, W["c_a_log"], W["c_dt_bias"], nb, name=tag + "_prep_fwd")
    yc, states = gdn_rec_fwd(*prep, pz, W["c_norm_g"], nb, name=tag + "_rec_fwd")
    w_out = W["c_w_out"][0].astype(bf16)
    x2, z1, x2b = proj_ln([yc], [w_out], x1, g, b, name=tag + "_out_ln")
    saved = (pqkv, pz, pba, qkvc, prep, states, yc, w_qkv, w_z, w_ba, w_out, zero_b)
    return x2, x2b, z1, saved


def _mixer_c_bwd(x1b, dz1, dz1b, W, saved, nb, tag):
    pqkv, pz, pba, qkvc, prep, states, yc, w_qkv, w_z, w_ba, w_out, zero_b = saved
    dyc = mm_nn(dz1b, w_out.T, name=tag + "_dyc")
    dwo = mm_tn(yc, dz1b, name=tag + "_dwo")
    rec = gdn_rec_bwd(*prep, pz, W["c_norm_g"], states, dyc, nb, name=tag + "_rec_bwd")
    cts, dpz, dng = rec[:6], rec[6], rec[7]
    dq, dk, dv, dpba, dalog, ddtb = gdn_prep_bwd(qkvc, pba, W["c_a_log"], W["c_dt_bias"], cts, nb,
                                                 name=tag + "_prep_bwd")
    dqkvc = jnp.concatenate([dq, dk, dv], 1)
    dpqkv, dconv_w, _ = conv_bwd(pqkv, W["c_conv_w"][0], zero_b, dqkvc, True, nb, name=tag + "_conv_bwd")
    dw_in = jnp.concatenate([mm_tn(x1b,dpqkv, name=tag + "_dwin_qkv"), mm_tn(x1b,dpz, name=tag + "_dwin_z"),
                             mm_tn(x1b,dpba, name=tag + "_dwin_ba")[:, :2 * C_HEADS]], 1)
    dx1 = mm_nn(dpqkv, w_qkv.T, add=dz1, add_scale=DN_ALPHA, name=tag + "_dx_qkv")
    dx1 = mm_nn(dpz, w_z.T, add=dx1, name=tag + "_dx_z")
    dx1 = mm_nn(dpba, w_ba.T, add=dx1, name=tag + "_dx_ba")
    grads = {"c_w_in": dw_in[None], "c_conv_w": dconv_w[None], "c_a_log": dalog[:, :C_HEADS],
             "c_dt_bias": ddtb[:, :C_HEADS], "c_norm_g": dng, "c_w_out": dwo[None]}
    return dx1, grads


def _local_step(x, p, target, W, F, on_ffn_grads):
    nb, s, d = x.shape
    t = nb * s
    h = x.reshape(t, d)
    hb = h.astype(bf16)
    tape = []
    for i in range(DEPTH):
        tag = f"l{i}"
        f1 = [F[k][i] for k in ("ffn1_wg", "ffn1_wu", "ffn1_wd")]
        f2 = [F[k][i] for k in ("ffn2_wg", "ffn2_wu", "ffn2_wd")]
        lg = [W["ln_g"][i, k][None] for k in range(3)]
        lb = [W["ln_b"][i, k][None] for k in range(3)]
        x1, z0, x1b = ffn_fwd(h, *f1, lg[0], lb[0], name=tag + "_ffn1_fwd")
        mixer = _mixer_ab_fwd if i % 2 == 0 else _mixer_c_fwd
        x2, x2b, z1, msaved = mixer(x1, x1b, W, lg[1], lb[1], nb, tag + "_mix")
        x3, z2, _ = ffn_fwd(x2, *f2, lg[2], lb[2], name=tag + "_ffn2_fwd")
        pi = p[i].reshape(t, -1)
        pw = (W["ple_wg"][i].astype(bf16), W["ple_bg"][i][None], W["ple_wp"][i].astype(bf16))
        x4, x4b = ple_fwd(x3, pi, *pw, name=tag + "_ple_fwd")
        tape.append((hb, z0, x1b, msaved, z1, x2b, z2, x3, pi, pw, lg))
        h, hb = x4, x4b
    dh, sq = loss_head(h, target.reshape(t, d), name="loss_head")
    loss = 0.5 * jnp.sum(sq) / d
    per_layer = [None] * DEPTH
    grads = {}
    for i in reversed(range(DEPTH)):
        tag = f"l{i}"
        hb_in, z0, x1b, msaved, z1, x2b, z2, x3, pi, pw, lg = tape[i]
        dx3, dple_wg, dple_bg, dple_wp = ple_bwd(x3, pi, dh, pw[0], pw[0].T, pw[1], pw[2], name=tag + "_ple_bwd")
        dz2, dz2b, dg2, db2 = ln_bwd(z2, dx3, lg[2], name=tag + "_ln2_bwd")
        f1 = [F[k][i] for k in ("ffn1_wg", "ffn1_wu", "ffn1_wd")]
        f2 = [F[k][i] for k in ("ffn2_wg", "ffn2_wu", "ffn2_wd")]
        dgate, dup, *df2 = ffn_bwd_weights(x2b, dz2b, *f2, name=tag + "_ffn2_bwd_w")
        on_ffn_grads(i, 3, df2)
        dx2 = ffn_bwd_input(dgate, dup, f2[0], f2[1], dz2, name=tag + "_ffn2_bwd_x")
        dz1, dz1b, dg1, db1 = ln_bwd(z1, dx2, lg[1], name=tag + "_ln1_bwd")
        mixer_bwd = _mixer_ab_bwd if i % 2 == 0 else _mixer_c_bwd
        dx1, mgrads = mixer_bwd(x1b, dz1, dz1b, W, msaved, nb, tag + "_mix")
        grads.update(mgrads)
        dz0, dz0b, dg0, db0 = ln_bwd(z0, dx1, lg[0], name=tag + "_ln0_bwd")
        dgate, dup, *df1 = ffn_bwd_weights(hb_in, dz0b, *f1, name=tag + "_ffn1_bwd_w")
        on_ffn_grads(i, 0, df1)
        dh = ffn_bwd_input(dgate, dup, f1[0], f1[1], dz0, name=tag + "_ffn1_bwd_x")
        per_layer[i] = {"ln_g": jnp.concatenate([dg0, dg1, dg2], 0), "ln_b": jnp.concatenate([db0, db1, db2], 0),
                        "ple_wg": dple_wg, "ple_bg": dple_bg[0], "ple_wp": dple_wp}
    for k in per_layer[0]:
        grads[k] = jnp.stack([per_layer[i][k] for i in range(DEPTH)])
    return loss, dh.reshape(nb, s, d), grads


WEIGHT_NAMES = ("ffn1_wg", "ffn1_wu", "ffn1_wd", "ffn2_wg", "ffn2_wu", "ffn2_wd", "ln_g", "ln_b", "ple_wg", "ple_bg",
                "ple_wp", "ab_w_in", "a_sinks", "b_conv_w", "b_conv_b", "b_wa", "b_ba", "b_wx", "b_bx", "b_lam",
                "ab_w_out", "c_w_in", "c_conv_w", "c_a_log", "c_dt_bias", "c_norm_g", "c_w_out")
NATIVE_NAMES = WEIGHT_NAMES[:6]
PACKED_NAMES = WEIGHT_NAMES[6:]
PACK_MATRICES = ("ple_wg", "ple_wp", "ab_w_in", "ab_w_out", "c_w_in", "c_w_out")
PACK_GROUPS = (tuple(k for k in PACKED_NAMES if k not in PACK_MATRICES), PACK_MATRICES)
PACK_TRANSIT = (f32, bf16)
SHARD_AXIS = {"ffn1_wg": 2, "ffn1_wu": 2, "ffn1_wd": 1, "ffn2_wg": 2, "ffn2_wu": 2, "ffn2_wd": 1, "ln_g": 2, "ln_b": 2,
              "ple_wg": 1, "ple_wp": 2, "ab_w_in": 2, "b_conv_w": 2, "ab_w_out": 1, "c_w_in": 2, "c_conv_w": 2,
              "c_w_out": 1}
N_CHIPS = 4
PACK_COLS = LANES
PACK_TILE_MULTIPLE = 256
ELEMENTWISE_BLOCK_ELEMS = 128 * 1024


def _row_tile(r, cols):
    return _tile(r, max(2 * SUBLANES, ELEMENTWISE_BLOCK_ELEMS // cols), 2 * SUBLANES)
MESH = pl.DeviceIdType.MESH
ANY = pl.BlockSpec(memory_space=pl.ANY)


def _tiled_dims(shape):
    w = shape[-1]
    r = 1
    for dim in shape[:-1]:
        r *= dim
    return r, w, -(-r // SUBLANES) * SUBLANES, -(-w // LANES) * LANES


def _pack(pieces, lead=()):
    k = len(lead)
    tiles = []
    for a in pieces:
        r, w, rp, wp = _tiled_dims(a.shape[k:])
        a2 = jnp.pad(a.reshape(lead + (r, w)), [(0, 0)] * k + [(0, rp - r), (0, wp - w)])
        a2 = a2.reshape(lead + (rp // SUBLANES, SUBLANES, wp // LANES, LANES))
        a2 = jnp.swapaxes(a2, k + 1, k + 2)
        tiles.append(a2.reshape(lead + (-1, SUBLANES, LANES)))
    flat = jnp.concatenate(tiles, axis=k)
    n = flat.shape[k]
    n_pad = -(-n // PACK_TILE_MULTIPLE) * PACK_TILE_MULTIPLE
    flat = jnp.pad(flat, [(0, 0)] * k + [(0, n_pad - n), (0, 0), (0, 0)])
    return flat.reshape(lead + (n_pad * SUBLANES, PACK_COLS))


def _unpack(pack, shapes, lead=()):
    k = len(lead)
    flat = pack.reshape(lead + (-1, SUBLANES, LANES))
    out, o = [], 0
    for shp in shapes:
        r, w, rp, wp = _tiled_dims(shp)
        n = (rp // SUBLANES) * (wp // LANES)
        a2 = lax.slice_in_dim(flat, o, o + n, axis=k).reshape(lead + (rp // SUBLANES, wp // LANES, SUBLANES, LANES))
        a2 = jnp.swapaxes(a2, k + 1, k + 2).reshape(lead + (rp, wp))
        a2 = lax.slice_in_dim(lax.slice_in_dim(a2, 0, r, axis=k), 0, w, axis=k + 1)
        out.append(a2.reshape(lead + tuple(shp)))
        o += n
    return out


def _mesh_position():
    x, y, c = lax.axis_index("x"), lax.axis_index("y"), lax.axis_index("c")
    chips = [(1 - x, y), (x, 1 - y), (1 - x, 1 - y)]
    return x, y, c, chips


def _remote(src, dst, send_sems, recv_sems, k, to):
    return pltpu.make_async_remote_copy(src_ref=src, dst_ref=dst, send_sem=send_sems.at[k], recv_sem=recv_sems.at[k],
                                        device_id=to, device_id_type=MESH)


def _sems(n):
    return pltpu.SemaphoreType.DMA((n,))


def place_slot(parts, slots, n_slots, dtype, from_slot, *, name):
    n = len(parts)
    r, cols = parts[0].shape[-2:]
    tr = _row_tile(r, cols)

    def body(src_ref, dst_ref, *refs):
        for a in range(n):
            refs[n + a][...] = refs[a][...].astype(dtype)

    dst = pl.BlockSpec((None, tr, cols), lambda i, src_ref, dst_ref: (dst_ref[0], i, 0))
    src = (pl.BlockSpec((None, tr, cols), lambda i, src_ref, dst_ref: (src_ref[0], i, 0)) if from_slot
           else pl.BlockSpec((tr, cols), lambda i, src_ref, dst_ref: (i, 0)))
    return pl.pallas_call(
        body,
        grid_spec=pltpu.PrefetchScalarGridSpec(num_scalar_prefetch=2, grid=(r // tr,), in_specs=[src] * n,
                                               out_specs=[dst] * n),
        out_shape=[jax.ShapeDtypeStruct((n_slots, r, cols), dtype)] * n,
        compiler_params=_params(("parallel",)), name=name,
    )(*slots, *parts)


def gather_shards(bufs, *, name):
    n = len(bufs)

    def body(*refs):
        out_refs = refs[n:2 * n]
        send_sems, recv_sems = refs[2 * n:]
        x, y, c, chips = _mesh_position()
        me = 2 * x + y
        sibling = (x, y, 1 - c)
        waits = []
        for j, (cx, cy) in enumerate(chips):
            for a in range(n):
                own = out_refs[a].at[me, c]
                cp = _remote(own, own, send_sems, recv_sems, 6 * a + j, (cx, cy, c))
                cp.start()
                waits.append(cp.wait_send)
        for j, (cx, cy) in enumerate(chips):
            for a in range(n):
                got = out_refs[a].at[2 * cx + cy, c]
                _remote(got, got, send_sems, recv_sems, 6 * a + j, (cx, cy, c)).wait_recv()
                fw = _remote(got, got, send_sems, recv_sems, 6 * a + 3 + j, sibling)
                fw.start()
                waits.append(fw.wait_send)
        for j, (cx, cy) in enumerate(chips):
            for a in range(n):
                got = out_refs[a].at[2 * cx + cy, 1 - c]
                _remote(got, got, send_sems, recv_sems, 6 * a + 3 + j, sibling).wait_recv()
        for wait in waits:
            wait()

    return pl.pallas_call(
        body, out_shape=[jax.ShapeDtypeStruct(b.shape, b.dtype) for b in bufs],
        in_specs=[ANY] * n, out_specs=[ANY] * n, scratch_shapes=[_sems(6 * n), _sems(6 * n)],
        input_output_aliases={a: a for a in range(n)}, name=name,
    )(*bufs)


def chip_exchange(ps, qs, *, name):
    n = len(ps)

    def body(*refs):
        p_refs, q_refs = refs[:n], refs[2 * n:3 * n]
        send_sems, recv_sems = refs[3 * n:]
        x, y, c, chips = _mesh_position()
        me = 2 * x + y
        waits = []
        for j, (cx, cy) in enumerate(chips):
            for a in range(n):
                cp = _remote(p_refs[a].at[2 * cx + cy], q_refs[a].at[me], send_sems, recv_sems, 3 * a + j, (cx, cy, c))
                cp.start()
                waits.append(cp.wait_send)
        for j, (cx, cy) in enumerate(chips):
            for a in range(n):
                got = q_refs[a].at[2 * cx + cy]
                _remote(got, got, send_sems, recv_sems, 3 * a + j, (cx, cy, c)).wait_recv()
        for wait in waits:
            wait()

    return pl.pallas_call(
        body, out_shape=[jax.ShapeDtypeStruct(q_.shape, q_.dtype) for q_ in qs], in_specs=[ANY] * (2 * n),
        out_specs=[ANY] * n, scratch_shapes=[_sems(3 * n), _sems(3 * n)],
        input_output_aliases={n + a: a for a in range(n)}, name=name,
    )(*ps, *qs)


def gather_slots_async(bufs, collective_id, *, name):
    n = len(bufs)
    refs = [jax.new_ref(b, memory_space=pltpu.MemorySpace.HBM) for b in bufs]

    @pl.kernel(mesh=plsc.ScalarSubcoreMesh(axis_name="sequencer", num_cores=1), name=name,
               scratch_types=(_sems(3 * n), _sems(3 * n)),
               compiler_params=pltpu.CompilerParams(collective_id=collective_id))
    def launch(send_sems, recv_sems):
        x, y, c, chips = _mesh_position()
        me = 2 * x + y
        barrier = pltpu.get_barrier_semaphore()
        for cx, cy in chips:
            pl.semaphore_signal(barrier, inc=1, device_id=(cx, cy, c), device_id_type=MESH)
        pl.semaphore_wait(barrier, len(chips))
        sends = []
        for j, (cx, cy) in enumerate(chips):
            for a in range(n):
                own = refs[a].at[me]
                cp = _remote(own, own, send_sems, recv_sems, 3 * a + j, (cx, cy, c))
                cp.start()
                sends.append(cp)
        for j, (cx, cy) in enumerate(chips):
            for a in range(n):
                got = refs[a].at[2 * cx + cy]
                _remote(got, got, send_sems, recv_sems, 3 * a + j, (cx, cy, c)).wait_recv()
        for cp in sends:
            cp.wait_send()

    launch()
    return [r[...] for r in refs]


N_DEVICES = 8
PEER_FLIPS = tuple((dx, dy, dc) for dx in (0, 1) for dy in (0, 1) for dc in (0, 1) if dx or dy or dc)


def exchange_partials_async(sends, recvs, collective_id, *, name):
    n = len(sends)
    s_refs = [jax.new_ref(a, memory_space=pltpu.MemorySpace.HBM) for a in sends]
    r_refs = [jax.new_ref(a, memory_space=pltpu.MemorySpace.HBM) for a in recvs]
    k = len(PEER_FLIPS)

    @pl.kernel(mesh=plsc.ScalarSubcoreMesh(axis_name="sequencer", num_cores=1), name=name,
               scratch_types=(_sems(k), _sems(k)), compiler_params=pltpu.CompilerParams(collective_id=collective_id))
    def launch(send_sems, recv_sems):
        x, y, c, _ = _mesh_position()
        me = 4 * x + 2 * y + c
        peers = [(1 - x if dx else x, 1 - y if dy else y, 1 - c if dc else c) for dx, dy, dc in PEER_FLIPS]
        barrier = pltpu.get_barrier_semaphore()
        for peer in peers:
            pl.semaphore_signal(barrier, inc=1, device_id=peer, device_id_type=MESH)
        pl.semaphore_wait(barrier, len(peers))
        sends_started = []
        for j, (px, py, pc) in enumerate(peers):
            for a in range(n):
                cp = _remote(s_refs[a].at[2 * px + py], r_refs[a].at[me], send_sems, recv_sems, j, (px, py, pc))
                cp.start()
                sends_started.append(cp)
        for j, (px, py, pc) in enumerate(peers):
            for a in range(n):
                got = r_refs[a].at[4 * px + 2 * py + pc]
                _remote(got, got, send_sems, recv_sems, j, (px, py, pc)).wait_recv()
        for cp in sends_started:
            cp.wait_send()

    launch()
    return [r[...] for r in r_refs]


def sibling_exchange(gs, *, name):
    n = len(gs)

    def body(*refs):
        g_refs, out_refs = refs[:n], refs[n:2 * n]
        send_sems, recv_sems = refs[2 * n:]
        x, y, c, _ = _mesh_position()
        cps = [_remote(g_refs[a].at[:, 1 - c], out_refs[a], send_sems, recv_sems, a, (x, y, 1 - c)) for a in range(n)]
        for cp in cps:
            cp.start()
        for cp in cps:
            cp.wait()

    return pl.pallas_call(
        body, out_shape=[jax.ShapeDtypeStruct(g.shape[:1] + g.shape[2:], g.dtype) for g in gs],
        in_specs=[ANY] * n, out_specs=[ANY] * n, scratch_shapes=[_sems(n), _sems(n)], name=name,
    )(*gs)


def add_own_half(gs, others, c_idx, dtype, *, name):
    n = len(gs)
    ns, _, r, cols = gs[0].shape
    tr = _row_tile(r, cols)

    def body(c_ref, *refs):
        for a in range(n):
            refs[2 * n + a][...] = (refs[a][...] + refs[n + a][...]).astype(dtype)

    own = pl.BlockSpec((None, None, tr, cols), lambda s, i, c_ref: (s, c_ref[0], i, 0))
    oth = pl.BlockSpec((None, tr, cols), lambda s, i, c_ref: (s, i, 0))
    return pl.pallas_call(
        body,
        grid_spec=pltpu.PrefetchScalarGridSpec(num_scalar_prefetch=1, grid=(ns, r // tr),
                                               in_specs=[own] * n + [oth] * n, out_specs=[oth] * n),
        out_shape=[jax.ShapeDtypeStruct((ns, r, cols), dtype)] * n,
        compiler_params=_params(("parallel", "parallel")), name=name,
    )(c_idx, *gs, *others)


def sum_slots(qs, *, name):
    n = len(qs)
    ns, r, cols = qs[0].shape
    tr = _row_tile(r, cols * ns)

    def body(*refs):
        for a in range(n):
            q_ref = refs[a]
            acc = q_ref[0].astype(f32) + q_ref[1].astype(f32)
            for i in range(2, ns):
                acc = acc + q_ref[i].astype(f32)
            refs[n + a][...] = acc

    return pl.pallas_call(
        body, grid=(r // tr,), in_specs=[pl.BlockSpec((ns, tr, cols), lambda i: (0, i, 0))] * n,
        out_specs=[pl.BlockSpec((tr, cols), lambda i: (i, 0))] * n,
        out_shape=[jax.ShapeDtypeStruct((r, cols), f32)] * n,
        compiler_params=_params(("parallel",)), name=name,
    )(*qs)


def sibling_share(bufs, *, name):
    n = len(bufs)

    def body(*refs):
        out_refs = refs[n:2 * n]
        send_sems, recv_sems = refs[2 * n:]
        x, y, c, _ = _mesh_position()
        sibling = (x, y, 1 - c)
        cps = []
        for a in range(n):
            own = out_refs[a].at[c]
            cp = _remote(own, own, send_sems, recv_sems, a, sibling)
            cp.start()
            cps.append(cp)
        for a in range(n):
            theirs = out_refs[a].at[1 - c]
            _remote(theirs, theirs, send_sems, recv_sems, a, sibling).wait_recv()
        for cp in cps:
            cp.wait_send()

    return pl.pallas_call(
        body, out_shape=[jax.ShapeDtypeStruct(b.shape, b.dtype) for b in bufs], in_specs=[ANY] * n,
        out_specs=[ANY] * n, scratch_shapes=[_sems(n), _sems(n)],
        input_output_aliases={a: a for a in range(n)}, name=name,
    )(*bufs)


def _adamw_update(w, g, m, v):
    m2 = ADAM_B1 * m + (1.0 - ADAM_B1) * g
    v2 = ADAM_B2 * v + (1.0 - ADAM_B2) * (g * g)
    m_hat = m2 / (1.0 - ADAM_B1 ** ADAM_STEP)
    v_hat = v2 / (1.0 - ADAM_B2 ** ADAM_STEP)
    return -ADAM_LR * (m_hat / (jnp.sqrt(v_hat) + ADAM_EPS) + ADAM_WD * w), m2, v2


def adamw_from_partials(ws, ms, vs, slots, layer, acc, *, name):
    n = len(ws)
    nl, r, cols = ws[0].shape
    ns = slots[0].shape[0]
    tr = _row_tile(r, cols * 2)

    def body(*refs):
        for a in range(n):
            w_ref, m_ref, v_ref, s_ref = (refs[k * n + a] for k in range(4))
            g_ref, d_ref, m2_ref, v2_ref = (refs[len(refs) - 4 * n + k * n + a] for k in range(4))
            g = s_ref[0].astype(f32) + s_ref[1].astype(f32)
            for i in range(2, ns):
                g = g + s_ref[i].astype(f32)
            g_ref[...] = g
            d_ref[...], m2_ref[...], v2_ref[...] = _adamw_update(w_ref[...], g, m_ref[...], v_ref[...])

    lay = pl.BlockSpec((None, tr, cols), lambda i: (layer, i, 0))
    in_specs = [lay] * (3 * n) + [pl.BlockSpec((ns, tr, cols), lambda i: (0, i, 0))] * n
    args = [*ws, *ms, *vs, *slots]
    aliases = {}
    if acc is not None:
        in_specs += [ANY] * (4 * n)
        args += [a for lst in acc for a in lst]
        aliases = {4 * n + k: k for k in range(4 * n)}
    out = pl.pallas_call(
        body, grid=(r // tr,), in_specs=in_specs, out_specs=[lay] * (4 * n),
        out_shape=[jax.ShapeDtypeStruct((nl, r, cols), f32)] * (4 * n), input_output_aliases=aliases,
        compiler_params=_params(("parallel",)), name=name,
    )(*args)
    return [list(out[k * n:(k + 1) * n]) for k in range(4)]


def adamw(ws, gs, ms, vs, *, name):
    n = len(ws)
    r, cols = ws[0].shape
    tr = _row_tile(r, cols)

    def body(*refs):
        for a in range(n):
            w_ref, g_ref, m_ref, v_ref = (refs[k * n + a] for k in range(4))
            d_ref, m2_ref, v2_ref = (refs[(4 + k) * n + a] for k in range(3))
            d_ref[...], m2_ref[...], v2_ref[...] = _adamw_update(w_ref[...], g_ref[...], m_ref[...], v_ref[...])

    row = pl.BlockSpec((tr, cols), lambda i: (i, 0))
    out = pl.pallas_call(
        body, grid=(r // tr,), in_specs=[row] * (4 * n), out_specs=[row] * (3 * n),
        out_shape=[jax.ShapeDtypeStruct((r, cols), f32)] * (3 * n),
        compiler_params=_params(("parallel",)), name=name,
    )(*ws, *gs, *ms, *vs)
    return out[:n], out[n:2 * n], out[2 * n:]


def _full_weights(gathered, names, weights):
    pieces = _unpack(gathered, [weights[k].shape for k in names], lead=(N_CHIPS,))
    full = {}
    for name, pc in zip(names, pieces):
        ax = SHARD_AXIS.get(name)
        if ax is None:
            full[name] = weights[name]
        else:
            shp = weights[name].shape
            full[name] = jnp.moveaxis(pc, 0, ax).reshape(shp[:ax] + (N_CHIPS * shp[ax],) + shp[ax + 1:])
    return full


def _grad_pack(grads, names, shapes):
    pieces = []
    for name, shp in zip(names, shapes):
        g = grads[name]
        ax = SHARD_AXIS.get(name)
        if ax is None:
            pieces.append(jnp.broadcast_to(g.reshape(shp)[None], (N_CHIPS,) + tuple(shp)))
        else:
            pieces.append(jnp.stack(jnp.split(g, N_CHIPS, axis=ax)))
    return _pack(pieces, lead=(N_CHIPS,))


def _by_shape(arrays):
    groups = {}
    for i, a in enumerate(arrays):
        groups.setdefault(a.shape, []).append(i)
    return list(groups.values())


def _grouped(fn, lists, n_out, tag):
    outs = [[None] * len(lists[0]) for _ in range(n_out)]
    for gi, idx in enumerate(_by_shape(lists[0])):
        res = fn(*[[lst[i] for i in idx] for lst in lists], name=f"{tag}_{gi}")
        res = res if n_out > 1 else (res,)
        for k in range(n_out):
            for i, r in zip(idx, res[k]):
                outs[k][i] = r
    return outs if n_out > 1 else outs[0]


def _train_step(x, p, loss_target, weights, m, v):
    shapes = [[weights[k].shape for k in names] for names in PACK_GROUPS]
    halves = lambda a: a.reshape((2, a.shape[0] // 2) + a.shape[1:])
    packs = lambda d_: [halves(_pack([d_[k] for k in names])) for names in PACK_GROUPS]
    nn_ = len(NATIVE_NAMES)
    local = [weights[k] for k in NATIVE_NAMES] + packs(weights)
    local_m = [m[k] for k in NATIVE_NAMES] + packs(m)
    local_v = [v[k] for k in NATIVE_NAMES] + packs(v)
    flat = lambda lst: [a.reshape((-1, a.shape[-1])) for a in lst]
    c_idx = lax.axis_index("c").astype(jnp.int32).reshape(1)
    chip_idx = (2 * lax.axis_index("x") + lax.axis_index("y")).astype(jnp.int32).reshape(1)
    c2 = (c_idx, c_idx)
    chip2 = (chip_idx, chip_idx)
    chip_dev = (chip_idx, 2 * chip_idx + c_idx)

    def placed(arrays, slot, n_slots, dtype, from_slot, tag):
        return _grouped(lambda a, name: place_slot(a, slot, n_slots, dtype, from_slot, name=name), [arrays], 1, tag)

    ffn_own = [weights[k][i] for i in range(DEPTH) for k in NATIVE_NAMES]
    ffn_bufs = placed(ffn_own, chip2, N_CHIPS, bf16, False, "place_ffn_weights")
    group = len(NATIVE_NAMES) // 2
    n_ffn_groups = len(ffn_bufs) // group
    ffn_gathered = []
    for gi in range(n_ffn_groups):
        ffn_gathered += gather_slots_async(ffn_bufs[gi * group:(gi + 1) * group], collective_id=1 + gi,
                                           name=f"comm_gather_ffn_{gi}")
    ffn_weights = {k: [ffn_gathered[i * len(NATIVE_NAMES) + j] for i in range(DEPTH)] for j, k in enumerate(NATIVE_NAMES)}
    pack_bufs = [placed(flat([a]), chip2, N_CHIPS, dt, False, f"place_packed_weights_{gi}")[0].reshape((N_CHIPS,) + a.shape)
                 for gi, (a, dt) in enumerate(zip(local[nn_:], PACK_TRANSIT))]
    full = {}
    for names, gathered in zip(PACK_GROUPS, gather_shards(pack_bufs, name="comm_gather_weights")):
        full.update(_full_weights(gathered, names, weights))
    first_grad_id = n_ffn_groups + 1
    in_flight = {}

    def on_ffn_grads(layer, first, partials):
        tag = f"ffn_grads_l{layer}_{first}"
        recvs = placed(partials, chip_dev, N_DEVICES, bf16, True, "place_" + tag)
        got = exchange_partials_async(partials, recvs, collective_id=first_grad_id + len(in_flight), name="comm_" + tag)
        in_flight[(layer, first)] = got

    loss, grad_x, grads = _local_step(x, p, loss_target, full, ffn_weights, on_ffn_grads)
    gs = [_grad_pack(grads, names, shp).reshape((N_CHIPS,) + a.shape)
          for names, shp, a in zip(PACK_GROUPS, shapes, local[nn_:])]
    others = sibling_exchange(gs, name="comm_grad_sibling")
    chip_sums = [add_own_half([g], [o], c_idx, dt, name=f"grad_add_sibling_{gi}")[0]
                 for gi, (g, o, dt) in enumerate(zip(gs, others, PACK_TRANSIT))]
    own = [placed([cs], chip2, N_CHIPS, dt, True, f"place_own_partial_{gi}")[0]
           for gi, (cs, dt) in enumerate(zip(chip_sums, PACK_TRANSIT))]
    slots = chip_exchange(chip_sums, own, name="comm_grad_chips")
    mine = _grouped(sum_slots, [list(slots)], 1, "grad_sum_chips")
    pack_sum = sibling_share(placed(mine, c2, 2, f32, False, "place_own_half"), name="comm_grad_share")
    ffn_out = [{} for _ in range(4)]
    for (layer, first), got in in_flight.items():
        names = NATIVE_NAMES[first:first + len(got)]
        for idx in _by_shape([weights[k] for k in names]):
            ks = [names[i] for i in idx]
            acc = [[out[k] for k in ks] for out in ffn_out] if ks[0] in ffn_out[0] else None
            res = adamw_from_partials([weights[k] for k in ks], [m[k] for k in ks], [v[k] for k in ks],
                                      [got[i] for i in idx], layer, acc, name=f"adamw_ffn_l{layer}_{first + idx[0]}")
            for out, arrays in zip(ffn_out, res):
                out.update(zip(ks, arrays))
    pack_out = [list(pack_sum)] + _grouped(adamw, [flat(local[nn_:]), flat(pack_sum), flat(local_m[nn_:]),
                                                    flat(local_v[nn_:])], 3, "adamw_packed")
    loss = lax.psum(loss, ("x", "y", "c"))
    outs = []
    for by_name, packs_ in zip(ffn_out, pack_out):
        by_name = dict(by_name)
        for names, shp, pk in zip(PACK_GROUPS, shapes, packs_):
            by_name.update(zip(names, _unpack(pk, shp)))
        outs += [by_name[k] for k in WEIGHT_NAMES]
    return (loss, grad_x, *outs)


def kernel(x, p, ffn1_wg, ffn1_wu, ffn1_wd, ffn2_wg, ffn2_wu, ffn2_wd, ln_g, ln_b, ple_wg, ple_bg, ple_wp, ab_w_in, a_sinks, b_conv_w, b_conv_b, b_wa, b_ba, b_wx, b_bx, b_lam, ab_w_out, c_w_in, c_conv_w, c_a_log, c_dt_bias, c_norm_g, c_w_out, loss_target, m_ffn1_wg, m_ffn1_wu, m_ffn1_wd, m_ffn2_wg, m_ffn2_wu, m_ffn2_wd, m_ln_g, m_ln_b, m_ple_wg, m_ple_bg, m_ple_wp, m_ab_w_in, m_a_sinks, m_b_conv_w, m_b_conv_b, m_b_wa, m_b_ba, m_b_wx, m_b_bx, m_b_lam, m_ab_w_out, m_c_w_in, m_c_conv_w, m_c_a_log, m_c_dt_bias, m_c_norm_g, m_c_w_out, v_ffn1_wg, v_ffn1_wu, v_ffn1_wd, v_ffn2_wg, v_ffn2_wu, v_ffn2_wd, v_ln_g, v_ln_b, v_ple_wg, v_ple_bg, v_ple_wp, v_ab_w_in, v_a_sinks, v_b_conv_w, v_b_conv_b, v_b_wa, v_b_ba, v_b_wx, v_b_bx, v_b_lam, v_ab_w_out, v_c_w_in, v_c_conv_w, v_c_a_log, v_c_dt_bias, v_c_norm_g, v_c_w_out):
    weights = [ffn1_wg, ffn1_wu, ffn1_wd, ffn2_wg, ffn2_wu, ffn2_wd, ln_g, ln_b, ple_wg, ple_bg, ple_wp, ab_w_in, a_sinks,
               b_conv_w, b_conv_b, b_wa, b_ba, b_wx, b_bx, b_lam, ab_w_out, c_w_in, c_conv_w, c_a_log, c_dt_bias, c_norm_g,
               c_w_out]
    m = [m_ffn1_wg, m_ffn1_wu, m_ffn1_wd, m_ffn2_wg, m_ffn2_wu, m_ffn2_wd, m_ln_g, m_ln_b, m_ple_wg, m_ple_bg, m_ple_wp,
         m_ab_w_in, m_a_sinks, m_b_conv_w, m_b_conv_b, m_b_wa, m_b_ba, m_b_wx, m_b_bx, m_b_lam, m_ab_w_out, m_c_w_in,
         m_c_conv_w, m_c_a_log, m_c_dt_bias, m_c_norm_g, m_c_w_out]
    v = [v_ffn1_wg, v_ffn1_wu, v_ffn1_wd, v_ffn2_wg, v_ffn2_wu, v_ffn2_wd, v_ln_g, v_ln_b, v_ple_wg, v_ple_bg, v_ple_wp,
         v_ab_w_in, v_a_sinks, v_b_conv_w, v_b_conv_b, v_b_wa, v_b_ba, v_b_wx, v_b_bx, v_b_lam, v_ab_w_out, v_c_w_in,
         v_c_conv_w, v_c_a_log, v_c_dt_bias, v_c_norm_g, v_c_w_out]
    return _train_step(x, p, loss_target, dict(zip(WEIGHT_NAMES, weights)), dict(zip(WEIGHT_NAMES, m)),
                       dict(zip(WEIGHT_NAMES, v)))
```

```python
import functools

import jax
import jax.numpy as jnp
from jax import lax
from jax.experimental import pallas as pl
from jax.experimental.pallas import tpu as pltpu
from jax.experimental.pallas import tpu_sc as plsc

f32 = jnp.float32
bf16 = jnp.bfloat16

DEPTH = 2
CHUNK = 64
A_HEADS, A_KV_HEADS, A_GROUP, A_HEAD_DIM = 8, 2, 4, 64
A_WIDTH, A_KV_WIDTH, A_WINDOW = 512, 128, 128
B_WIDTH, B_BLOCKS, B_BLOCK, B_CONV = 512, 8, 64, 4
RG_C = 8.0
C_HEADS, C_HEAD_DIM, C_WIDTH, C_CONV = 8, 128, 1024, 4
DN_ALPHA = (2.0 * DEPTH) ** 0.25
LN_EPS = 1e-5
NORM_EPS = 1e-6
NEG = -1e30
ADAM_LR, ADAM_B1, ADAM_B2, ADAM_EPS, ADAM_WD, ADAM_STEP = 0.001, 0.9, 0.999, 1e-08, 0.01, 10

VMEM_LIMIT_BYTES = 56 * 1024 * 1024
LANES = 128
SUBLANES = 8
GROUP_W = 128
PREP_FWD_UNROLL = 16
PREP_BWD_UNROLL = 16
C_HEADS_PER_STEP = 8
GDN_TIME_BLOCK = 256

NN = ((1,), (0,))
NT = ((1,), (1,))
TN = ((0,), (0,))


def _params(sem):
    return pltpu.CompilerParams(dimension_semantics=sem, vmem_limit_bytes=VMEM_LIMIT_BYTES)


def _tile(n, cap, mult):
    best = None
    t = mult
    while t <= min(n, cap):
        if n % t == 0:
            best = t
        t += mult
    return best if best is not None else n


def _bdot(a, b, dims):
    return lax.dot_general(a.astype(bf16), b.astype(bf16), (dims, ((), ())), preferred_element_type=f32)


def _running_sum(x, reverse):
    s = x.shape[0]
    t = lax.broadcasted_iota(jnp.int32, x.shape, 0)
    d = 1
    while d < s:
        if reverse:
            x = x + jnp.where(t < s - d, pltpu.roll(x, s - d, 0), 0.0)
        else:
            x = x + jnp.where(t >= d, pltpu.roll(x, d, 0), 0.0)
        d *= 2
    return x


@jax.custom_vjp
def _cumsum0(x):
    return _running_sum(x, False)


def _cumsum0_fwd(x):
    return _running_sum(x, False), None


def _cumsum0_bwd(_, g):
    return (_running_sum(g, True),)


_cumsum0.defvjp(_cumsum0_fwd, _cumsum0_bwd)


@jax.custom_vjp
def _bnn(a, b):
    return _bdot(a, b, NN)


def _bnn_fwd(a, b):
    return _bdot(a, b, NN), (a, b)


def _bnn_bwd(res, g):
    a, b = res
    return _bdot(g, b, NT), _bdot(a, g, TN)


_bnn.defvjp(_bnn_fwd, _bnn_bwd)


@jax.custom_vjp
def _bnt(a, b):
    return _bdot(a, b, NT)


def _bnt_fwd(a, b):
    return _bdot(a, b, NT), (a, b)


def _bnt_bwd(res, g):
    a, b = res
    return _bdot(g, b, NN), _bdot(g, a, TN)


_bnt.defvjp(_bnt_fwd, _bnt_bwd)


@jax.custom_vjp
def _btn(a, b):
    return _bdot(a, b, TN)


def _btn_fwd(a, b):
    return _bdot(a, b, TN), (a, b)


def _btn_bwd(res, g):
    a, b = res
    return _bdot(b, g, NT), _bdot(a, g, NN)


_btn.defvjp(_btn_fwd, _btn_bwd)

RAW_DOTS = (lambda a, b: _bdot(a, b, NN), lambda a, b: _bdot(a, b, NT), lambda a, b: _bdot(a, b, TN),
            lambda x: _running_sum(x, False))
VJP_DOTS = (_bnn, _bnt, _btn, _cumsum0)


def _layer_norm(z, g, b):
    mu = jnp.mean(z, -1, keepdims=True)
    d = z - mu
    var = jnp.mean(d * d, -1, keepdims=True)
    return d * lax.rsqrt(var + LN_EPS) * g + b


def _silu(x):
    return x * jax.nn.sigmoid(x)


def mm_nn(a, w, add=None, add_scale=1.0, *, name):
    m, k = a.shape
    n = w.shape[1]
    tm = _tile(m, 1024, 2 * SUBLANES)
    tn = _tile(n, 1024, LANES)

    def body(*refs):
        if add is None:
            a_ref, w_ref, o_ref = refs
            o_ref[...] = _bdot(a_ref[...], w_ref[...], NN)
        else:
            a_ref, w_ref, add_ref, o_ref = refs
            o_ref[...] = _bdot(a_ref[...], w_ref[...], NN) + add_scale * add_ref[...]

    in_specs = [pl.BlockSpec((tm, k), lambda i, j: (i, 0)), pl.BlockSpec((k, tn), lambda i, j: (0, j))]
    args = [a, w]
    if add is not None:
        in_specs.append(pl.BlockSpec((tm, tn), lambda i, j: (i, j)))
        args.append(add)
    return pl.pallas_call(
        body, grid=(m // tm, n // tn), in_specs=in_specs,
        out_specs=pl.BlockSpec((tm, tn), lambda i, j: (i, j)),
        out_shape=jax.ShapeDtypeStruct((m, n), f32),
        compiler_params=_params(("parallel", "parallel")), name=name,
    )(*args)


def mm_tn(a, b, *, name):
    m, k = a.shape
    n = b.shape[1]
    tm = _tile(m, 1024, 2 * SUBLANES)
    tn = _tile(n, 1024, LANES)

    def body(a_ref, b_ref, o_ref):
        part = _bdot(a_ref[...], b_ref[...], TN)

        @pl.when(pl.program_id(1) == 0)
        def _():
            o_ref[...] = part

        @pl.when(pl.program_id(1) > 0)
        def _():
            o_ref[...] += part

    return pl.pallas_call(
        body, grid=(n // tn, m // tm),
        in_specs=[pl.BlockSpec((tm, k), lambda j, i: (i, 0)), pl.BlockSpec((tm, tn), lambda j, i: (i, j))],
        out_specs=pl.BlockSpec((k, tn), lambda j, i: (0, j)),
        out_shape=jax.ShapeDtypeStruct((k, n), f32),
        compiler_params=_params(("parallel", "arbitrary")), name=name,
    )(a, b)


def proj_ln(a_list, w_list, xres, g, b, *, name):
    t, d = xres.shape
    tm = _tile(t, 256, 2 * SUBLANES)
    na = len(a_list)

    def body(*refs):
        a_refs, w_refs = refs[:na], refs[na:2 * na]
        x_ref, g_ref, b_ref, y_ref, z_ref, yb_ref = refs[2 * na:]
        z = DN_ALPHA * x_ref[...]
        for a_ref, w_ref in zip(a_refs, w_refs):
            z = z + _bdot(a_ref[...], w_ref[...], NN)
        z_ref[...] = z
        y = _layer_norm(z, g_ref[...], b_ref[...])
        y_ref[...] = y
        yb_ref[...] = y.astype(bf16)

    in_specs = [pl.BlockSpec((tm, a.shape[1]), lambda i: (i, 0)) for a in a_list]
    in_specs += [pl.BlockSpec(w.shape, lambda i: (0, 0)) for w in w_list]
    in_specs += [pl.BlockSpec((tm, d), lambda i: (i, 0)), pl.BlockSpec((1, d), lambda i: (0, 0)),
                 pl.BlockSpec((1, d), lambda i: (0, 0))]
    return pl.pallas_call(
        body, grid=(t // tm,), in_specs=in_specs,
        out_specs=[pl.BlockSpec((tm, d), lambda i: (i, 0))] * 3,
        out_shape=[jax.ShapeDtypeStruct((t, d), f32)] * 2 + [jax.ShapeDtypeStruct((t, d), bf16)],
        compiler_params=_params(("parallel",)), name=name,
    )(*a_list, *w_list, xres, g, b)


def ln_bwd(z, dy, g, *, name):
    t, d = z.shape
    tm = _tile(t, 512, SUBLANES)

    def body(z_ref, dy_ref, g_ref, dz_ref, dzb_ref, dg_ref, db_ref):
        zz = z_ref[...]
        dy_ = dy_ref[...]
        mu = jnp.mean(zz, -1, keepdims=True)
        dd = zz - mu
        var = jnp.mean(dd * dd, -1, keepdims=True)
        rstd = lax.rsqrt(var + LN_EPS)
        xhat = dd * rstd
        dxh = dy_ * g_ref[...]
        dz = rstd * (dxh - jnp.mean(dxh, -1, keepdims=True) - xhat * jnp.mean(dxh * xhat, -1, keepdims=True))
        dz_ref[...] = dz
        dzb_ref[...] = dz.astype(bf16)
        pg = jnp.sum(dy_ * xhat, 0, keepdims=True)
        pb = jnp.sum(dy_, 0, keepdims=True)

        @pl.when(pl.program_id(0) == 0)
        def _():
            dg_ref[...] = pg
            db_ref[...] = pb

        @pl.when(pl.program_id(0) > 0)
        def _():
            dg_ref[...] += pg
            db_ref[...] += pb

    row = pl.BlockSpec((tm, d), lambda i: (i, 0))
    vec = pl.BlockSpec((1, d), lambda i: (0, 0))
    return pl.pallas_call(
        body, grid=(t // tm,), in_specs=[row, row, vec], out_specs=[row, row, vec, vec],
        out_shape=[jax.ShapeDtypeStruct((t, d), f32), jax.ShapeDtypeStruct((t, d), bf16),
                   jax.ShapeDtypeStruct((1, d), f32), jax.ShapeDtypeStruct((1, d), f32)],
        compiler_params=_params(("arbitrary",)), name=name,
    )(z, dy, g)


def loss_head(y, target, *, name):
    t, d = y.shape
    tm = _tile(t, 512, SUBLANES)

    def body(y_ref, t_ref, dy_ref, sq_ref):
        e = y_ref[...] - t_ref[...]
        dy_ref[...] = e * (1.0 / d)
        part = jnp.sum(e * e, 0, keepdims=True)

        @pl.when(pl.program_id(0) == 0)
        def _():
            sq_ref[...] = part

        @pl.when(pl.program_id(0) > 0)
        def _():
            sq_ref[...] += part

    row = pl.BlockSpec((tm, d), lambda i: (i, 0))
    vec = pl.BlockSpec((1, d), lambda i: (0, 0))
    return pl.pallas_call(
        body, grid=(t // tm,), in_specs=[row, row], out_specs=[row, vec],
        out_shape=[jax.ShapeDtypeStruct((t, d), f32), jax.ShapeDtypeStruct((1, d), f32)],
        compiler_params=_params(("arbitrary",)), name=name,
    )(y, target)


FFN_COL_BLOCK = 256
FFN_ROWS = 1024


def _lane_blocks(n):
    return [slice(s, min(s + FFN_COL_BLOCK, n)) for s in range(0, n, FFN_COL_BLOCK)]


def ffn_fwd(x, wg, wu, wd, g, b, *, name):
    t, d = x.shape
    nf, _, tf = wg.shape
    tm = _tile(t, FFN_ROWS, SUBLANES)

    def body(x_ref, wg_ref, wu_ref, wd_ref, g_ref, b_ref, y_ref, z_ref, yb_ref, acc_ref):
        f = pl.program_id(1)
        xb = x_ref[...].astype(bf16)
        part, pending = None, None
        for cols in _lane_blocks(tf):
            gate_up = (_bdot(xb, wg_ref[:, cols], NN), _bdot(xb, wu_ref[:, cols], NN), cols)
            if pending is not None:
                down = _bdot(_silu(pending[0]) * pending[1], wd_ref[pending[2], :], NN)
                part = down if part is None else part + down
            pending = gate_up
        down = _bdot(_silu(pending[0]) * pending[1], wd_ref[pending[2], :], NN)
        part = down if part is None else part + down

        @pl.when(f == 0)
        def _():
            acc_ref[...] = part

        @pl.when(f > 0)
        def _():
            acc_ref[...] += part

        @pl.when(f == nf - 1)
        def _():
            z = DN_ALPHA * x_ref[...] + 0.5 * acc_ref[...]
            z_ref[...] = z
            y = _layer_norm(z, g_ref[...], b_ref[...])
            y_ref[...] = y
            yb_ref[...] = y.astype(bf16)

    row = pl.BlockSpec((tm, d), lambda i, j: (i, 0))
    vec = pl.BlockSpec((1, d), lambda i, j: (0, 0))
    wcol = pl.BlockSpec((None, d, tf), lambda i, j: (j, 0, 0))
    wrow = pl.BlockSpec((None, tf, d), lambda i, j: (j, 0, 0))
    return pl.pallas_call(
        body, grid=(t // tm, nf),
        in_specs=[row, wcol, wcol, wrow, vec, vec],
        out_specs=[row, row, row],
        out_shape=[jax.ShapeDtypeStruct((t, d), f32)] * 2 + [jax.ShapeDtypeStruct((t, d), bf16)],
        scratch_shapes=[pltpu.VMEM((tm, d), f32)],
        compiler_params=_params(("parallel", "arbitrary")), name=name,
    )(x, wg, wu, wd, g, b)


def ffn_bwd_weights(xb, dzb, wg, wu, wd, *, name):
    t, d = xb.shape
    nf, _, tf = wg.shape
    tm = _tile(t, FFN_ROWS, SUBLANES)
    nt = t // tm

    def body(x_ref, dz_ref, wg_ref, wu_ref, wd_ref, dgate_ref, dup_ref, owg_ref, owu_ref, owd_ref,
             dwg_ref, dwu_ref, dwd_ref):
        x = x_ref[...]
        dzh = dz_ref[...] * 0.5

        def first_half(cols):
            return _bdot(x, wg_ref[:, cols], NN), _bdot(x, wu_ref[:, cols], NN), _bdot(dzh, wd_ref[cols, :], NT), cols

        def second_half(gate, up, dh, cols):
            sg = jax.nn.sigmoid(gate)
            s = gate * sg
            dup = (dh * s).astype(bf16)
            dgate = (dh * up * (sg * (1.0 + gate * (1.0 - sg)))).astype(bf16)
            dgate_ref[:, cols] = dgate
            dup_ref[:, cols] = dup
            return _bdot(x, dgate, TN), _bdot(x, dup, TN), _bdot(s * up, dzh, TN), cols

        parts, pending = [], None
        for cols in _lane_blocks(tf):
            nxt = first_half(cols)
            if pending is not None:
                parts.append(second_half(*pending))
            pending = nxt
        parts.append(second_half(*pending))

        @pl.when(pl.program_id(1) == 0)
        def _():
            for pwg, pwu, pwd, cols in parts:
                dwg_ref[:, cols] = pwg
                dwu_ref[:, cols] = pwu
                dwd_ref[cols, :] = pwd

        @pl.when(pl.program_id(1) > 0)
        def _():
            for pwg, pwu, pwd, cols in parts:
                dwg_ref[:, cols] += pwg
                dwu_ref[:, cols] += pwu
                dwd_ref[cols, :] += pwd

        @pl.when(pl.program_id(1) == nt - 1)
        def _():
            owg_ref[...] = dwg_ref[...].astype(bf16)
            owu_ref[...] = dwu_ref[...].astype(bf16)
            owd_ref[...] = dwd_ref[...].astype(bf16)

    row = pl.BlockSpec((tm, d), lambda j, i: (i, 0))
    wcol = pl.BlockSpec((None, d, tf), lambda j, i: (j, 0, 0))
    wrow = pl.BlockSpec((None, tf, d), lambda j, i: (j, 0, 0))
    act = pl.BlockSpec((None, tm, tf), lambda j, i: (j, i, 0))
    return pl.pallas_call(
        body, grid=(nf, nt), in_specs=[row, row, wcol, wcol, wrow], out_specs=[act, act, wcol, wcol, wrow],
        out_shape=[jax.ShapeDtypeStruct((nf, t, tf), bf16), jax.ShapeDtypeStruct((nf, t, tf), bf16),
                   jax.ShapeDtypeStruct((nf, d, tf), bf16), jax.ShapeDtypeStruct((nf, d, tf), bf16),
                   jax.ShapeDtypeStruct((nf, tf, d), bf16)],
        scratch_shapes=[pltpu.VMEM((d, tf), f32), pltpu.VMEM((d, tf), f32), pltpu.VMEM((tf, d), f32)],
        compiler_params=_params(("parallel", "arbitrary")), name=name,
    )(xb, dzb, wg, wu, wd)


def ffn_bwd_input(dgate, dup, wg, wu, dz, *, name):
    nf, t, tf = dgate.shape
    d = wg.shape[1]
    tm = _tile(t, FFN_ROWS // 2, SUBLANES)

    def body(dg_ref, du_ref, wg_ref, wu_ref, dz_ref, dx_ref):
        acc = DN_ALPHA * dz_ref[...]
        for j in range(nf):
            acc = acc + _bdot(dg_ref[j], wg_ref[j], NT) + _bdot(du_ref[j], wu_ref[j], NT)
        dx_ref[...] = acc

    act = pl.BlockSpec((nf, tm, tf), lambda i: (0, i, 0))
    wsp = pl.BlockSpec((nf, d, tf), lambda i: (0, 0, 0))
    row = pl.BlockSpec((tm, d), lambda i: (i, 0))
    return pl.pallas_call(
        body, grid=(t // tm,), in_specs=[act, act, wsp, wsp, row], out_specs=row,
        out_shape=jax.ShapeDtypeStruct((t, d), f32),
        compiler_params=_params(("parallel",)), name=name,
    )(dgate, dup, wg, wu, dz)


def ple_fwd(x, p, wg, bg, wp, *, name):
    t, d = x.shape
    dp = p.shape[1]
    tm = _tile(t, 512, 2 * SUBLANES)

    def body(x_ref, p_ref, wg_ref, bg_ref, wp_ref, o_ref, ob_ref):
        x_ = x_ref[...]
        gate = jax.nn.sigmoid(_bdot(x_, wg_ref[...], NN) + bg_ref[...])
        out = x_ + gate * _bdot(p_ref[...], wp_ref[...], NN)
        o_ref[...] = out
        ob_ref[...] = out.astype(bf16)

    row = pl.BlockSpec((tm, d), lambda i: (i, 0))
    return pl.pallas_call(
        body, grid=(t // tm,),
        in_specs=[row, pl.BlockSpec((tm, dp), lambda i: (i, 0)), pl.BlockSpec((d, d), lambda i: (0, 0)),
                  pl.BlockSpec((1, d), lambda i: (0, 0)), pl.BlockSpec((dp, d), lambda i: (0, 0))],
        out_specs=[row, row], out_shape=[jax.ShapeDtypeStruct((t, d), f32), jax.ShapeDtypeStruct((t, d), bf16)],
        compiler_params=_params(("parallel",)), name=name,
    )(x, p, wg, bg, wp)


def ple_bwd(x, p, dy, wg, wgt, bg, wp, *, name):
    t, d = x.shape
    dp = p.shape[1]
    tm = _tile(t, 512, SUBLANES)

    def body(x_ref, p_ref, dy_ref, wg_ref, wgt_ref, bg_ref, wp_ref, dx_ref, dwg_ref, dbg_ref, dwp_ref):
        x_ = x_ref[...]
        dy_ = dy_ref[...]
        s = jax.nn.sigmoid(_bdot(x_, wg_ref[...], NN) + bg_ref[...])
        e = _bdot(p_ref[...], wp_ref[...], NN)
        da = dy_ * e * s * (1.0 - s)
        de = dy_ * s
        dx_ref[...] = dy_ + _bdot(da, wgt_ref[...], NN)
        pwg = _bdot(x_, da, TN)
        pbg = jnp.sum(da, 0, keepdims=True)
        pwp = _bdot(p_ref[...], de, TN)

        @pl.when(pl.program_id(0) == 0)
        def _():
            dwg_ref[...] = pwg
            dbg_ref[...] = pbg
            dwp_ref[...] = pwp

        @pl.when(pl.program_id(0) > 0)
        def _():
            dwg_ref[...] += pwg
            dbg_ref[...] += pbg
            dwp_ref[...] += pwp

    row = pl.BlockSpec((tm, d), lambda i: (i, 0))
    full = lambda shape: pl.BlockSpec(shape, lambda i: (0, 0))
    return pl.pallas_call(
        body, grid=(t // tm,),
        in_specs=[row, pl.BlockSpec((tm, dp), lambda i: (i, 0)), row, full((d, d)), full((d, d)), full((1, d)),
                  full((dp, d))],
        out_specs=[row, full((d, d)), full((1, d)), full((dp, d))],
        out_shape=[jax.ShapeDtypeStruct((t, d), f32), jax.ShapeDtypeStruct((d, d), f32),
                   jax.ShapeDtypeStruct((1, d), f32), jax.ShapeDtypeStruct((dp, d), f32)],
        compiler_params=_params(("arbitrary",)), name=name,
    )(x, p, dy, wg, wgt, bg, wp)


def _conv_taps(xpad_ref, w_ref, s):
    acc = w_ref[0:1, :] * xpad_ref[SUBLANES - 3:SUBLANES - 3 + s, :]
    for j in range(1, 4):
        acc = acc + w_ref[j:j + 1, :] * xpad_ref[SUBLANES - 3 + j:SUBLANES - 3 + j + s, :]
    return acc


def conv_fwd(x, w, bias, act, nb, *, name):
    t, c = x.shape
    s = t // nb
    cw = GROUP_W

    def body(x_ref, w_ref, b_ref, y_ref, xpad):
        xpad[0:SUBLANES, :] = jnp.zeros((SUBLANES, cw), f32)
        xpad[SUBLANES:, :] = x_ref[...]
        acc = _conv_taps(xpad, w_ref, s) + b_ref[...]
        y_ref[...] = _silu(acc) if act else acc

    slab = pl.BlockSpec((s, cw), lambda b, g: (b, g))
    return pl.pallas_call(
        body, grid=(nb, c // cw),
        in_specs=[slab, pl.BlockSpec((4, cw), lambda b, g: (0, g)), pl.BlockSpec((1, cw), lambda b, g: (0, g))],
        out_specs=slab, out_shape=jax.ShapeDtypeStruct((t, c), f32),
        scratch_shapes=[pltpu.VMEM((s + SUBLANES, cw), f32)],
        compiler_params=_params(("parallel", "parallel")), name=name,
    )(x, w, bias)


def conv_bwd(x, w, bias, dy, act, nb, *, name):
    t, c = x.shape
    s = t // nb
    cw = GROUP_W

    def body(x_ref, w_ref, b_ref, dy_ref, dx_ref, dw_ref, db_ref, xpad, dpad):
        xpad[0:SUBLANES, :] = jnp.zeros((SUBLANES, cw), f32)
        xpad[SUBLANES:, :] = x_ref[...]
        dacc = dy_ref[...]
        if act:
            acc = _conv_taps(xpad, w_ref, s) + b_ref[...]
            sg = jax.nn.sigmoid(acc)
            dacc = dacc * (sg * (1.0 + acc * (1.0 - sg)))
        dpad[0:s, :] = dacc
        dpad[s:, :] = jnp.zeros((SUBLANES, cw), f32)
        dx = w_ref[0:1, :] * dpad[3:3 + s, :]
        for j in range(1, 4):
            dx = dx + w_ref[j:j + 1, :] * dpad[3 - j:3 - j + s, :]
        dx_ref[...] = dx
        first = pl.program_id(1) == 0
        for j in range(4):
            pw = jnp.sum(dacc * xpad[SUBLANES - 3 + j:SUBLANES - 3 + j + s, :], 0, keepdims=True)

            @pl.when(first)
            def _():
                dw_ref[j:j + 1, :] = pw

            @pl.when(jnp.logical_not(first))
            def _():
                dw_ref[j:j + 1, :] += pw

        pb = jnp.sum(dacc, 0, keepdims=True)

        @pl.when(first)
        def _():
            db_ref[...] = pb

        @pl.when(jnp.logical_not(first))
        def _():
            db_ref[...] += pb

    slab = pl.BlockSpec((s, cw), lambda g, b: (b, g))
    wsp = pl.BlockSpec((4, cw), lambda g, b: (0, g))
    bsp = pl.BlockSpec((1, cw), lambda g, b: (0, g))
    return pl.pallas_call(
        body, grid=(c // cw, nb), in_specs=[slab, wsp, bsp, slab], out_specs=[slab, wsp, bsp],
        out_shape=[jax.ShapeDtypeStruct((t, c), f32), jax.ShapeDtypeStruct((4, c), f32),
                   jax.ShapeDtypeStruct((1, c), f32)],
        scratch_shapes=[pltpu.VMEM((s + SUBLANES, cw), f32), pltpu.VMEM((s + SUBLANES, cw), f32)],
        compiler_params=_params(("parallel", "arbitrary")), name=name,
    )(x, w, bias, dy)


def _each(f, *lists):
    return [f(*a) for a in zip(*lists)]


def _attn_heads(qs, kbs, vbs, sinks, valids, dist, dots):
    nn, nt = dots[:2]
    items = range(len(qs))
    kv = [(i // A_HEADS) * A_KV_HEADS + (i % A_HEADS) // A_GROUP for i in items]
    scs = [nt(qs[i], kbs[kv[i]]) for i in items]
    prs = []
    for i in items:
        h = i % A_HEADS
        sc = scs[i] * (A_HEAD_DIM ** -0.5) - 2.0 ** -(h + 1) * dist
        sc = jnp.where(valids[i // A_HEADS], sc, NEG)
        m = lax.stop_gradient(jnp.maximum(jnp.max(sc, -1, keepdims=True), sinks[h]))
        pr = jnp.exp(sc - m)
        den = jnp.sum(pr, -1, keepdims=True) + jnp.exp(sinks[h] - m)
        prs.append(pr / den)
    return [nn(prs[i], vbs[kv[i]]) for i in items]


A_Q_ROWS = 2 * CHUNK
A_STEPS_PER_TRIP = 2


def _attn_steps(s):
    return A_STEPS_PER_TRIP if s % (A_Q_ROWS * A_STEPS_PER_TRIP) == 0 else 1


def _attn_band_consts(r0):
    band = A_WINDOW + A_Q_ROWS
    qi = lax.broadcasted_iota(jnp.int32, (A_Q_ROWS, band), 0)
    kj = lax.broadcasted_iota(jnp.int32, (A_Q_ROWS, band), 1)
    dist = jnp.abs(qi + A_WINDOW - kj).astype(f32)
    qc, kc = qi // CHUNK, kj // CHUNK
    valid = ((kj + r0) >= A_WINDOW) & (kc >= qc) & (kc <= qc + A_WINDOW // CHUNK)
    return dist, valid


def attn_fwd(qkv, sinks, nb, *, name):
    t = qkv.shape[0]
    s = t // nb
    band = A_WINDOW + A_Q_ROWS
    hd = A_HEAD_DIM

    def body(qkv_ref, sink_ref, o_ref, kvpad):
        kvpad[0:A_WINDOW, :] = jnp.zeros((A_WINDOW, 2 * A_KV_WIDTH), f32)
        kvpad[A_WINDOW:, :] = qkv_ref[:, A_WIDTH:]

        def trip(n, carry):
            r0s = [pl.multiple_of((n * steps + j) * A_Q_ROWS, A_Q_ROWS) for j in range(steps)]
            consts = [_attn_band_consts(r0) for r0 in r0s]
            kbs = [kvpad[pl.ds(r0, band), kvh * hd:(kvh + 1) * hd] for r0 in r0s for kvh in range(A_KV_HEADS)]
            vbs = [kvpad[pl.ds(r0, band), A_KV_WIDTH + kvh * hd:A_KV_WIDTH + (kvh + 1) * hd]
                   for r0 in r0s for kvh in range(A_KV_HEADS)]
            qs = [qkv_ref[pl.ds(r0, A_Q_ROWS), h * hd:(h + 1) * hd] for r0 in r0s for h in range(A_HEADS)]
            outs = _attn_heads(qs, kbs, vbs, [sink_ref[:, h:h + 1] for h in range(A_HEADS)], [c_[1] for c_ in consts],
                               consts[0][0], RAW_DOTS)
            for j, r0 in enumerate(r0s):
                for h in range(A_HEADS):
                    o_ref[pl.ds(r0, A_Q_ROWS), h * hd:(h + 1) * hd] = outs[j * A_HEADS + h]
            return carry

        steps = _attn_steps(s)
        lax.fori_loop(0, s // (A_Q_ROWS * steps), trip, 0)

    return pl.pallas_call(
        body, grid=(nb,),
        in_specs=[pl.BlockSpec((s, A_WIDTH + 2 * A_KV_WIDTH), lambda b: (b, 0)),
                  pl.BlockSpec((1, A_HEADS), lambda b: (0, 0))],
        out_specs=pl.BlockSpec((s, A_WIDTH), lambda b: (b, 0)),
        out_shape=jax.ShapeDtypeStruct((t, A_WIDTH), f32),
        scratch_shapes=[pltpu.VMEM((s + A_WINDOW, 2 * A_KV_WIDTH), f32)],
        compiler_params=_params(("parallel",)), name=name,
    )(qkv, sinks)


def attn_bwd(qkv, sinks, do, nb, *, name):
    t = qkv.shape[0]
    s = t // nb
    band = A_WINDOW + A_Q_ROWS
    hd = A_HEAD_DIM
    kvw = 2 * A_KV_WIDTH

    def body(qkv_ref, sink_ref, do_ref, dqkv_ref, dsink_ref, kvpad, dkvpad):
        kvpad[0:A_WINDOW, :] = jnp.zeros((A_WINDOW, kvw), f32)
        kvpad[A_WINDOW:, :] = qkv_ref[:, A_WIDTH:]
        dkvpad[...] = jnp.zeros((s + A_WINDOW, kvw), f32)

        def trip(n, dsinks):
            r0s = [pl.multiple_of((n * steps + j) * A_Q_ROWS, A_Q_ROWS) for j in range(steps)]
            consts = [_attn_band_consts(r0) for r0 in r0s]
            ksl = [slice(kvh * hd, (kvh + 1) * hd) for kvh in range(A_KV_HEADS)]
            vsl = [slice(A_KV_WIDTH + kvh * hd, A_KV_WIDTH + (kvh + 1) * hd) for kvh in range(A_KV_HEADS)]
            kbs = [kvpad[pl.ds(r0, band), sl] for r0 in r0s for sl in ksl]
            vbs = [kvpad[pl.ds(r0, band), sl] for r0 in r0s for sl in vsl]
            qs = [qkv_ref[pl.ds(r0, A_Q_ROWS), h * hd:(h + 1) * hd] for r0 in r0s for h in range(A_HEADS)]
            dos = [do_ref[pl.ds(r0, A_Q_ROWS), h * hd:(h + 1) * hd] for r0 in r0s for h in range(A_HEADS)]
            fn = functools.partial(_attn_heads, valids=[c_[1] for c_ in consts], dist=consts[0][0], dots=VJP_DOTS)
            _, vjp = jax.vjp(fn, qs, kbs, vbs, [sink_ref[:, h:h + 1] for h in range(A_HEADS)])
            dqs, dks, dvs, dss = vjp(dos)
            for j, r0 in enumerate(r0s):
                for h in range(A_HEADS):
                    dqkv_ref[pl.ds(r0, A_Q_ROWS), h * hd:(h + 1) * hd] = dqs[j * A_HEADS + h]
            for j, r0 in enumerate(r0s):
                for kvh in range(A_KV_HEADS):
                    dkvpad[pl.ds(r0, band), ksl[kvh]] += dks[j * A_KV_HEADS + kvh]
                    dkvpad[pl.ds(r0, band), vsl[kvh]] += dvs[j * A_KV_HEADS + kvh]
            return tuple(dsinks[h] + dss[h] for h in range(A_HEADS))

        steps = _attn_steps(s)
        dsinks = lax.fori_loop(0, s // (A_Q_ROWS * steps), trip, tuple(jnp.zeros((1, 1), f32) for _ in range(A_HEADS)))
        dqkv_ref[:, A_WIDTH:] = dkvpad[A_WINDOW:, :]
        first = pl.program_id(0) == 0
        for h in range(A_HEADS):
            @pl.when(first)
            def _():
                dsink_ref[:, h:h + 1] = dsinks[h]

            @pl.when(jnp.logical_not(first))
            def _():
                dsink_ref[:, h:h + 1] += dsinks[h]

    wq = A_WIDTH + kvw
    return pl.pallas_call(
        body, grid=(nb,),
        in_specs=[pl.BlockSpec((s, wq), lambda b: (b, 0)), pl.BlockSpec((1, A_HEADS), lambda b: (0, 0)),
                  pl.BlockSpec((s, A_WIDTH), lambda b: (b, 0))],
        out_specs=[pl.BlockSpec((s, wq), lambda b: (b, 0)), pl.BlockSpec((1, A_HEADS), lambda b: (0, 0))],
        out_shape=[jax.ShapeDtypeStruct((t, wq), f32), jax.ShapeDtypeStruct((1, A_HEADS), f32)],
        scratch_shapes=[pltpu.VMEM((s + A_WINDOW, kvw), f32), pltpu.VMEM((s + A_WINDOW, kvw), f32)],
        compiler_params=_params(("arbitrary",)), name=name,
    )(qkv, sinks, do)


def _rg_gates(xc, wa, wx, ba, bx, lam, nn):
    r = jax.nn.sigmoid(nn(xc, wa) + ba)
    i = jax.nn.sigmoid(nn(xc, wx) + bx)
    log_a = -RG_C * r * jax.nn.softplus(-lam)
    a = jnp.exp(log_a)
    mult = jnp.sqrt(-jnp.tanh(log_a) * (jnp.exp(2.0 * log_a) + 1.0))
    return a, mult * (i * xc)


def _linear_scan(a, u, reverse):
    s = a.shape[0]
    t = lax.broadcasted_iota(jnp.int32, a.shape, 0)
    d = 1
    while d < s:
        if reverse:
            keep = t < s - d
            shift = s - d
        else:
            keep = t >= d
            shift = d
        us = jnp.where(keep, pltpu.roll(u, shift, 0), 0.0)
        as_ = jnp.where(keep, pltpu.roll(a, shift, 0), 1.0)
        u = u + a * us
        a = a * as_
        d *= 2
    return u


def rglru_fwd(xc, bg, wa, wx, ba, bx, lam, nb, *, name):
    t, c = xc.shape
    s = t // nb
    cw = GROUP_W

    def body(xc_ref, bg_ref, wa_ref, wx_ref, ba_ref, bx_ref, lam_ref, y_ref, h_ref):
        a, u = _rg_gates(xc_ref[...], wa_ref[...], wx_ref[...], ba_ref[...], bx_ref[...], lam_ref[...], RAW_DOTS[0])
        h = _linear_scan(a, u, False)
        h_ref[...] = h
        y_ref[...] = h * jax.nn.gelu(bg_ref[...])

    slab = pl.BlockSpec((s, cw), lambda b, g: (b, g))
    wsp = pl.BlockSpec((None, cw, cw), lambda b, g: (g, 0, 0))
    vec = pl.BlockSpec((1, cw), lambda b, g: (0, g))
    return pl.pallas_call(
        body, grid=(nb, c // cw), in_specs=[slab, slab, wsp, wsp, vec, vec, vec], out_specs=[slab, slab],
        out_shape=[jax.ShapeDtypeStruct((t, c), f32)] * 2,
        compiler_params=_params(("parallel", "parallel")), name=name,
    )(xc, bg, wa, wx, ba, bx, lam)


def rglru_bwd(xc, bg, h, dy, wa, wx, ba, bx, lam, nb, *, name):
    t, c = xc.shape
    s = t // nb
    cw = GROUP_W

    def body(xc_ref, bg_ref, h_ref, dy_ref, wa_ref, wx_ref, ba_ref, bx_ref, lam_ref,
             dxc_ref, dbg_ref, dwa_ref, dwx_ref, dba_ref, dbx_ref, dlam_ref):
        h = h_ref[...]
        dy_ = dy_ref[...]
        gel, gel_vjp = jax.vjp(jax.nn.gelu, bg_ref[...])
        dbg_ref[...] = gel_vjp(dy_ * h)[0]
        dh = dy_ * gel
        gates = functools.partial(_rg_gates, nn=_bnn)
        (a, _), gates_vjp = jax.vjp(gates, xc_ref[...], wa_ref[...], wx_ref[...], ba_ref[...], bx_ref[...],
                                    lam_ref[...])
        ti = lax.broadcasted_iota(jnp.int32, a.shape, 0)
        a_next = jnp.where(ti < s - 1, pltpu.roll(a, s - 1, 0), 0.0)
        lam_t = _linear_scan(a_next, dh, True)
        h_prev = jnp.where(ti >= 1, pltpu.roll(h, 1, 0), 0.0)
        dxc, dwa, dwx, dba, dbx, dlam = gates_vjp((lam_t * h_prev, lam_t))
        dxc_ref[...] = dxc
        first = pl.program_id(1) == 0

        @pl.when(first)
        def _():
            dwa_ref[...] = dwa
            dwx_ref[...] = dwx
            dba_ref[...] = dba
            dbx_ref[...] = dbx
            dlam_ref[...] = dlam

        @pl.when(jnp.logical_not(first))
        def _():
            dwa_ref[...] += dwa
            dwx_ref[...] += dwx
            dba_ref[...] += dba
            dbx_ref[...] += dbx
            dlam_ref[...] += dlam

    slab = pl.BlockSpec((s, cw), lambda g, b: (b, g))
    wsp = pl.BlockSpec((None, cw, cw), lambda g, b: (g, 0, 0))
    vec = pl.BlockSpec((1, cw), lambda g, b: (0, g))
    ng = c // cw
    return pl.pallas_call(
        body, grid=(ng, nb), in_specs=[slab, slab, slab, slab, wsp, wsp, vec, vec, vec],
        out_specs=[slab, slab, wsp, wsp, vec, vec, vec],
        out_shape=[jax.ShapeDtypeStruct((t, c), f32), jax.ShapeDtypeStruct((t, c), f32),
                   jax.ShapeDtypeStruct((ng, cw, cw), f32), jax.ShapeDtypeStruct((ng, cw, cw), f32),
                   jax.ShapeDtypeStruct((1, c), f32), jax.ShapeDtypeStruct((1, c), f32),
                   jax.ShapeDtypeStruct((1, c), f32)],
        compiler_params=_params(("parallel", "arbitrary")), name=name,
    )(xc, bg, h, dy, wa, wx, ba, bx, lam)


def _gdn_chunks_prep(qs, ks, vs, bls, als, a_log, dt_b, dots):
    nn, nt, csum = dots[0], dots[1], dots[3]
    hd = C_HEAD_DIM
    ri = lax.broadcasted_iota(jnp.int32, (CHUNK, CHUNK), 0)
    ci = lax.broadcasted_iota(jnp.int32, (CHUNK, CHUNK), 1)
    tril = ri >= ci
    strict = ri > ci
    eye = (ri == ci).astype(f32)
    qn = [q * lax.rsqrt(jnp.sum(q * q, -1, keepdims=True) + NORM_EPS) * (hd ** -0.5) for q in qs]
    kn = [k * lax.rsqrt(jnp.sum(k * k, -1, keepdims=True) + NORM_EPS) for k in ks]
    beta = [jax.nn.sigmoid(bl) for bl in bls]
    g = [-jnp.exp(a_log) * jax.nn.softplus(al + dt_b) for al in als]
    gc_sq = [csum(jnp.broadcast_to(g_, (CHUNK, CHUNK))) for g_ in g]
    gc = [csum(jnp.broadcast_to(g_, (CHUNK, hd))) for g_ in g]
    decay = [jnp.where(tril, jnp.exp(jnp.where(tril, s - s.T, 0.0)), 0.0) for s in gc_sq]
    kb = _each(jnp.multiply, kn, beta)
    kk = _each(nt, kb, kn)
    pw = [-jnp.where(strict, a * d, 0.0) for a, d in zip(kk, decay)]
    inv = [eye + p_ for p_ in pw]
    for _ in range(5):
        pw = _each(nn, pw, pw)
        inv = _each(jnp.add, inv, _each(nn, inv, pw))
    egc = [jnp.exp(c_) for c_ in gc]
    u = _each(nn, inv, _each(jnp.multiply, vs, beta))
    w = _each(nn, inv, _each(jnp.multiply, kb, egc))
    attn = _each(jnp.multiply, _each(nt, qn, kn), decay)
    g_last = [jnp.sum(jnp.broadcast_to(g_, (CHUNK, hd)), 0, keepdims=True) for g_ in g]
    qg = _each(jnp.multiply, qn, egc)
    kdec = [k_ * jnp.exp(gl_ - c_) for k_, gl_, c_ in zip(kn, g_last, gc)]
    return [(qg[i], kdec[i], w[i], u[i], attn[i], jnp.exp(g_last[i])) for i in range(len(qs))]


def _gdn_heads_step(states, qgs, kdecs, ws, us, attns, gls, zs, ng, dots):
    nn, tn = dots[0], dots[2]
    v_new = _each(jnp.subtract, us, _each(nn, ws, states))
    o = _each(jnp.add, _each(nn, qgs, states), _each(nn, attns, v_new))
    new = [s * gl for s, gl in zip(states, gls)]
    new = _each(jnp.add, new, _each(tn, kdecs, v_new))
    y = [o_ * lax.rsqrt(jnp.mean(o_ * o_, -1, keepdims=True) + NORM_EPS) * ng * _silu(z) for o_, z in zip(o, zs)]
    return y, new


def _loop_unrolled(n, unroll, load, compute, store, init):
    u = unroll if n % unroll == 0 else 1

    def trip(i, carry):
        idx = [i * u + j for j in range(u)]
        loaded = [load(k) for k in idx]
        results = compute(loaded)
        for k, r in zip(idx, results):
            carry = store(k, r, carry)
        return carry

    return lax.fori_loop(0, n // u, trip, init)


def _pick_lane(x, lane):
    li = lax.broadcasted_iota(jnp.int32, x.shape, 1)
    return jnp.sum(jnp.where(li == lane, x, 0.0), 1, keepdims=True)


def _put_lane(col, lane, width):
    li = lax.broadcasted_iota(jnp.int32, (col.shape[0], width), 1)
    return jnp.where(li == lane, col, 0.0)


def _gdn_specs(s, nc):
    hd = C_HEAD_DIM
    head = lambda off: pl.BlockSpec((s, hd), lambda b, h, off=off: (b, off + h))
    attn = pl.BlockSpec((None, s, CHUNK), lambda b, h: (h, b, 0))
    gl = pl.BlockSpec((None, nc * SUBLANES, hd), lambda b, h: (h, b, 0))
    ba = pl.BlockSpec((s, LANES), lambda b, h: (b, 0))
    sc8 = pl.BlockSpec((1, C_HEADS), lambda b, h: (0, 0))
    return head, attn, gl, ba, sc8


def gdn_prep_fwd(qkv, ba, a_log, dt_b, nb, *, name):
    t = qkv.shape[0]
    s = t // nb
    nc = s // CHUNK
    hd = C_HEAD_DIM
    head, attn_sp, gl_sp, ba_sp, sc8 = _gdn_specs(s, nc)

    def body(q_ref, k_ref, v_ref, ba_ref, alog_ref, dtb_ref, qg_ref, kd_ref, w_ref, u_ref, at_ref, gl_ref):
        h = pl.program_id(1)
        a_log_h = _pick_lane(alog_ref[...], h)
        dt_b_h = _pick_lane(dtb_ref[...], h)

        def load(n):
            rows = pl.ds(pl.multiple_of(n * CHUNK, CHUNK), CHUNK)
            bav = ba_ref[rows, :]
            return q_ref[rows, :], k_ref[rows, :], v_ref[rows, :], _pick_lane(bav, h), _pick_lane(bav, C_HEADS + h)

        def compute(loaded):
            return _gdn_chunks_prep(*[list(x) for x in zip(*loaded)], a_log_h, dt_b_h, RAW_DOTS)

        def store(n, outs, carry):
            rows = pl.ds(pl.multiple_of(n * CHUNK, CHUNK), CHUNK)
            qg_ref[rows, :] = outs[0].astype(bf16)
            kd_ref[rows, :] = outs[1].astype(bf16)
            w_ref[rows, :] = outs[2].astype(bf16)
            u_ref[rows, :] = outs[3]
            at_ref[rows, :] = outs[4].astype(bf16)
            gl_ref[pl.ds(pl.multiple_of(n * SUBLANES, SUBLANES), SUBLANES), :] = jnp.broadcast_to(outs[5], (SUBLANES, hd))
            return carry

        _loop_unrolled(nc, PREP_FWD_UNROLL, load, compute, store, 0)

    big = jax.ShapeDtypeStruct((t, C_WIDTH), f32)
    bigb = jax.ShapeDtypeStruct((t, C_WIDTH), bf16)
    return pl.pallas_call(
        body, grid=(nb, C_HEADS),
        in_specs=[head(0), head(C_HEADS), head(2 * C_HEADS), ba_sp, sc8, sc8],
        out_specs=[head(0)] * 4 + [attn_sp, gl_sp],
        out_shape=[bigb, bigb, bigb, big, jax.ShapeDtypeStruct((C_HEADS, t, CHUNK), bf16),
                               jax.ShapeDtypeStruct((C_HEADS, nb * nc * SUBLANES, hd), f32)],
        compiler_params=_params(("parallel", "parallel")), name=name,
    )(qkv, qkv, qkv, ba, a_log, dt_b)


def gdn_prep_bwd(qkv, ba, a_log, dt_b, cts, nb, *, name):
    t = qkv.shape[0]
    s = t // nb
    nc = s // CHUNK
    hd = C_HEAD_DIM
    head, attn_sp, gl_sp, ba_sp, sc8 = _gdn_specs(s, nc)

    def body(q_ref, k_ref, v_ref, ba_ref, alog_ref, dtb_ref, cqg, ckd, cw_, cu, cat, cgl,
             dq_ref, dk_ref, dv_ref, dba_ref, dalog_ref, ddtb_ref):
        b = pl.program_id(0)
        h = pl.program_id(1)
        a_log_h = _pick_lane(alog_ref[...], h)
        dt_b_h = _pick_lane(dtb_ref[...], h)
        prep = functools.partial(_gdn_chunks_prep, dots=VJP_DOTS)

        @pl.when(h == 0)
        def _():
            dba_ref[...] = jnp.zeros((s, LANES), f32)

        def load(n):
            rows = pl.ds(pl.multiple_of(n * CHUNK, CHUNK), CHUNK)
            bav = ba_ref[rows, :]
            cgl_n = cgl[pl.ds(pl.multiple_of(n * SUBLANES, SUBLANES), SUBLANES), :][0:1, :]
            primals = (q_ref[rows, :], k_ref[rows, :], v_ref[rows, :], _pick_lane(bav, h), _pick_lane(bav, C_HEADS + h))
            return primals, (cqg[rows, :], ckd[rows, :], cw_[rows, :], cu[rows, :], cat[rows, :], cgl_n), dba_ref[rows, :]

        def compute(loaded):
            primals = [list(x) for x in zip(*[item[0] for item in loaded])]
            _, vjp = jax.vjp(prep, *primals, a_log_h, dt_b_h)
            dqs, dks, dvs, dbls, dals, dalog, ddtb = vjp([item[1] for item in loaded])
            zero = jnp.zeros((1, 1), f32)
            return [((dqs[i], dks[i], dvs[i], dbls[i], dals[i], dalog if i == 0 else zero, ddtb if i == 0 else zero),
                     loaded[i][2]) for i in range(len(loaded))]

        def store(n, res, carry):
            (dq, dk, dv, dbl, dal, dalog_n, ddtb_n), dba_old = res
            rows = pl.ds(pl.multiple_of(n * CHUNK, CHUNK), CHUNK)
            dq_ref[rows, :] = dq
            dk_ref[rows, :] = dk
            dv_ref[rows, :] = dv
            dba_ref[rows, :] = dba_old + _put_lane(dbl, h, LANES) + _put_lane(dal, C_HEADS + h, LANES)
            return carry[0] + dalog_n, carry[1] + ddtb_n

        da_log, ddt_b = _loop_unrolled(nc, PREP_BWD_UNROLL, load, compute, store,
                                       (jnp.zeros((1, 1), f32), jnp.zeros((1, 1), f32)))
        first = jnp.logical_and(b == 0, h == 0)

        @pl.when(first)
        def _():
            dalog_ref[...] = _put_lane(da_log, h, LANES)
            ddtb_ref[...] = _put_lane(ddt_b, h, LANES)

        @pl.when(jnp.logical_not(first))
        def _():
            dalog_ref[...] += _put_lane(da_log, h, LANES)
            ddtb_ref[...] += _put_lane(ddt_b, h, LANES)

    big = jax.ShapeDtypeStruct((t, C_WIDTH), f32)
    vec = pl.BlockSpec((1, LANES), lambda b, h: (0, 0))
    return pl.pallas_call(
        body, grid=(nb, C_HEADS),
        in_specs=[head(0), head(C_HEADS), head(2 * C_HEADS), ba_sp, sc8, sc8] + [head(0)] * 4 + [attn_sp, gl_sp],
        out_specs=[head(0)] * 3 + [ba_sp, vec, vec],
        out_shape=[big] * 3 + [jax.ShapeDtypeStruct((t, LANES), f32), jax.ShapeDtypeStruct((1, LANES), f32),
                               jax.ShapeDtypeStruct((1, LANES), f32)],
        compiler_params=_params(("arbitrary", "arbitrary")), name=name,
    )(qkv, qkv, qkv, ba, a_log, dt_b, *cts)


def _gdn_rec_specs(sb, nsb, hp, reverse):
    hd = C_HEAD_DIM
    ncb = sb // CHUNK
    blk = (lambda b, k: b * nsb + (nsb - 1 - k)) if reverse else (lambda b, k: b * nsb + k)
    wide = pl.BlockSpec((sb, hp * hd), lambda b, j, k: (blk(b, k), j))
    attn = pl.BlockSpec((hp, sb, CHUNK), lambda b, j, k: (j, blk(b, k), 0))
    gl = pl.BlockSpec((hp, ncb * SUBLANES, hd), lambda b, j, k: (j, blk(b, k), 0))
    ng = pl.BlockSpec((1, hd), lambda b, j, k: (0, 0))
    states = pl.BlockSpec((hp, ncb, hd, hd), lambda b, j, k: (j, blk(b, k), 0, 0))
    return wide, attn, gl, ng, states


def gdn_rec_fwd(qg, kdec, w, u, attn, gl, z, ng, nb, *, name):
    t = qg.shape[0]
    s = t // nb
    sb = min(s, GDN_TIME_BLOCK)
    nsb = s // sb
    hd = C_HEAD_DIM
    hp = C_HEADS_PER_STEP
    wide, attn_sp, gl_sp, ng_sp, st_sp = _gdn_rec_specs(sb, nsb, hp, False)

    def body(qg_ref, kd_ref, w_ref, u_ref, at_ref, gl_ref, z_ref, ng_ref, y_ref, st_ref, carry_ref):
        @pl.when(pl.program_id(2) == 0)
        def _():
            carry_ref[...] = jnp.zeros((hp, hd, hd), f32)

        def chunk(n, states):
            for j in range(hp):
                st_ref[j, n] = states[j]
            rows = pl.ds(pl.multiple_of(n * CHUNK, CHUNK), CHUNK)
            grow = pl.ds(pl.multiple_of(n * SUBLANES, SUBLANES), SUBLANES)
            cols = [slice(j * hd, (j + 1) * hd) for j in range(hp)]
            ins = [(qg_ref[rows, c], kd_ref[rows, c], w_ref[rows, c], u_ref[rows, c], at_ref[j, rows, :],
                    gl_ref[j, grow, :][0:1, :], z_ref[rows, c]) for j, c in enumerate(cols)]
            ys, new = _gdn_heads_step(list(states), *[list(x) for x in zip(*ins)], ng_ref[...], RAW_DOTS)
            for j in range(hp):
                y_ref[rows, cols[j]] = ys[j]
            return tuple(new)

        last = lax.fori_loop(0, sb // CHUNK, chunk, tuple(carry_ref[j] for j in range(hp)))
        for j in range(hp):
            carry_ref[j] = last[j]

    return pl.pallas_call(
        body, grid=(nb, C_HEADS // hp, nsb),
        in_specs=[wide] * 4 + [attn_sp, gl_sp, wide, ng_sp], out_specs=[wide, st_sp],
        out_shape=[jax.ShapeDtypeStruct((t, C_WIDTH), f32), jax.ShapeDtypeStruct((C_HEADS, t // CHUNK, hd, hd), f32)],
        scratch_shapes=[pltpu.VMEM((hp, hd, hd), f32)],
        compiler_params=_params(("parallel", "parallel", "arbitrary")), name=name,
    )(qg, kdec, w, u, attn, gl, z, ng)


def gdn_rec_bwd(qg, kdec, w, u, attn, gl, z, ng, states, dy, nb, *, name):
    t = qg.shape[0]
    s = t // nb
    sb = min(s, GDN_TIME_BLOCK)
    nsb = s // sb
    nc = sb // CHUNK
    hd = C_HEAD_DIM
    hp = C_HEADS_PER_STEP
    wide, attn_sp, gl_sp, ng_sp, st_sp = _gdn_rec_specs(sb, nsb, hp, True)

    def body(qg_ref, kd_ref, w_ref, u_ref, at_ref, gl_ref, z_ref, ng_ref, states, dy_ref,
             dqg_ref, dkd_ref, dw_ref, du_ref, dat_ref, dgl_ref, dz_ref, dng_ref, carry_ref):
        step = functools.partial(_gdn_heads_step, dots=VJP_DOTS)

        @pl.when(pl.program_id(2) == 0)
        def _():
            carry_ref[...] = jnp.zeros((hp, hd, hd), f32)

        def operands(n):
            rows = pl.ds(pl.multiple_of(n * CHUNK, CHUNK), CHUNK)
            grow = pl.ds(pl.multiple_of(n * SUBLANES, SUBLANES), SUBLANES)
            cols = [slice(j * hd, (j + 1) * hd) for j in range(hp)]
            return ([qg_ref[rows, c].astype(f32) for c in cols], [kd_ref[rows, c].astype(f32) for c in cols],
                    [w_ref[rows, c].astype(f32) for c in cols], [u_ref[rows, c] for c in cols],
                    [at_ref[j, rows, :].astype(f32) for j in range(hp)],
                    [gl_ref[j, grow, :][0:1, :] for j in range(hp)], [z_ref[rows, c] for c in cols])

        def bwd_chunk(i, carry):
            n = nc - 1 - i
            rows = pl.ds(pl.multiple_of(n * CHUNK, CHUNK), CHUNK)
            grow = pl.ds(pl.multiple_of(n * SUBLANES, SUBLANES), SUBLANES)
            dsts, dng = carry
            dys = [dy_ref[rows, j * hd:(j + 1) * hd] for j in range(hp)]
            _, vjp = jax.vjp(step, [states[j, n] for j in range(hp)], *operands(n), ng_ref[...])
            dst, dqg, dkd, dw, du, dat, dgl, dz, dng_n = vjp((dys, list(dsts)))
            for j in range(hp):
                cols = slice(j * hd, (j + 1) * hd)
                dqg_ref[rows, cols] = dqg[j]
                dkd_ref[rows, cols] = dkd[j]
                dw_ref[rows, cols] = dw[j]
                du_ref[rows, cols] = du[j]
                dat_ref[j, rows, :] = dat[j]
                dgl_ref[j, grow, :] = jnp.broadcast_to(dgl[j], (SUBLANES, hd))
                dz_ref[rows, cols] = dz[j]
            return tuple(dst), dng + dng_n

        dlast, dng = lax.fori_loop(0, nc, bwd_chunk,
                                   (tuple(carry_ref[j] for j in range(hp)), jnp.zeros((1, hd), f32)))
        for j in range(hp):
            carry_ref[j] = dlast[j]
        first = jnp.logical_and(jnp.logical_and(pl.program_id(0) == 0, pl.program_id(1) == 0), pl.program_id(2) == 0)

        @pl.when(first)
        def _():
            dng_ref[...] = dng

        @pl.when(jnp.logical_not(first))
        def _():
            dng_ref[...] += dng

    big = jax.ShapeDtypeStruct((t, C_WIDTH), f32)
    return pl.pallas_call(
        body, grid=(nb, C_HEADS // hp, nsb),
        in_specs=[wide] * 4 + [attn_sp, gl_sp, wide, ng_sp, st_sp, wide],
        out_specs=[wide] * 4 + [attn_sp, gl_sp, wide, ng_sp],
        out_shape=[big] * 4 + [jax.ShapeDtypeStruct(attn.shape, f32), jax.ShapeDtypeStruct(gl.shape, f32), big,
                               jax.ShapeDtypeStruct((1, hd), f32)],
        scratch_shapes=[pltpu.VMEM((hp, hd, hd), f32)],
        compiler_params=_params(("arbitrary", "arbitrary", "arbitrary")), name=name,
    )(qg, kdec, w, u, attn, gl, z, ng, states, dy)


def _blockdiag_slabs(w):
    per = GROUP_W // B_BLOCK
    slabs = jnp.zeros((B_BLOCKS // per, GROUP_W, GROUP_W), w.dtype)
    for h in range(B_BLOCKS):
        o = (h % per) * B_BLOCK
        slabs = slabs.at[h // per, o:o + B_BLOCK, o:o + B_BLOCK].set(w[h])
    return slabs


def _slab_blocks(slabs):
    per = GROUP_W // B_BLOCK
    return jnp.stack([slabs[h // per, (h % per) * B_BLOCK:(h % per + 1) * B_BLOCK,
                            (h % per) * B_BLOCK:(h % per + 1) * B_BLOCK] for h in range(B_BLOCKS)])


def _mixer_ab_fwd(x1, x1b, W, g, b, nb, tag):
    w_in = W["ab_w_in"][0].astype(bf16)
    o1, o2 = A_WIDTH + 2 * A_KV_WIDTH, A_WIDTH + 2 * A_KV_WIDTH + B_WIDTH
    w_qkv, w_bx, w_bg = w_in[:, :o1], w_in[:, o1:o2], w_in[:, o2:]
    pqkv = mm_nn(x1b,w_qkv, name=tag + "_in_qkv")
    pbx = mm_nn(x1b,w_bx, name=tag + "_in_bx")
    pbg = mm_nn(x1b,w_bg, name=tag + "_in_bg")
    ya = attn_fwd(pqkv, W["a_sinks"], nb, name=tag + "_attn_fwd")
    xc = conv_fwd(pbx, W["b_conv_w"][0], W["b_conv_b"], False, nb, name=tag + "_conv_fwd")
    wa_s, wx_s = _blockdiag_slabs(W["b_wa"][0]), _blockdiag_slabs(W["b_wx"][0])
    yb, hh = rglru_fwd(xc, pbg, wa_s, wx_s, W["b_ba"], W["b_bx"], W["b_lam"], nb, name=tag + "_rglru_fwd")
    w_out = W["ab_w_out"][0].astype(bf16)
    x2, z1, x2b = proj_ln([ya, yb], [w_out[:A_WIDTH], w_out[A_WIDTH:]], x1, g, b, name=tag + "_out_ln")
    saved = (pqkv, pbx, pbg, ya, xc, yb, hh, wa_s, wx_s, w_qkv, w_bx, w_bg, w_out)
    return x2, x2b, z1, saved


def _mixer_ab_bwd(x1b, dz1, dz1b, W, saved, nb, tag):
    pqkv, pbx, pbg, ya, xc, yb, hh, wa_s, wx_s, w_qkv, w_bx, w_bg, w_out = saved
    dya = mm_nn(dz1b, w_out[:A_WIDTH].T, name=tag + "_dya")
    dyb = mm_nn(dz1b, w_out[A_WIDTH:].T, name=tag + "_dyb")
    dwo = jnp.concatenate([mm_tn(ya, dz1b, name=tag + "_dwo_a"), mm_tn(yb, dz1b, name=tag + "_dwo_b")], 0)
    dpqkv, dsinks = attn_bwd(pqkv, W["a_sinks"], dya, nb, name=tag + "_attn_bwd")
    dxc, dpbg, dwa_s, dwx_s, dba, dbx, dlam = rglru_bwd(xc, pbg, hh, dyb, wa_s, wx_s, W["b_ba"], W["b_bx"],
                                                       W["b_lam"], nb, name=tag + "_rglru_bwd")
    dpbx, dconv_w, dconv_b = conv_bwd(pbx, W["b_conv_w"][0], W["b_conv_b"], dxc, False, nb, name=tag + "_conv_bwd")
    dw_in = jnp.concatenate([mm_tn(x1b,dpqkv, name=tag + "_dwin_qkv"), mm_tn(x1b,dpbx, name=tag + "_dwin_bx"),
                             mm_tn(x1b,dpbg, name=tag + "_dwin_bg")], 1)
    dx1 = mm_nn(dpqkv, w_qkv.T, add=dz1, add_scale=DN_ALPHA, name=tag + "_dx_qkv")
    dx1 = mm_nn(dpbx, w_bx.T, add=dx1, name=tag + "_dx_bx")
    dx1 = mm_nn(dpbg, w_bg.T, add=dx1, name=tag + "_dx_bg")
    grads = {"ab_w_in": dw_in[None], "a_sinks": dsinks, "b_conv_w": dconv_w[None], "b_conv_b": dconv_b,
             "b_wa": _slab_blocks(dwa_s)[None], "b_ba": dba, "b_wx": _slab_blocks(dwx_s)[None], "b_bx": dbx,
             "b_lam": dlam, "ab_w_out": dwo[None]}
    return dx1, grads


def _mixer_c_fwd(x1, x1b, W, g, b, nb, tag):
    w_in = W["c_w_in"][0].astype(bf16)
    d = w_in.shape[0]
    o1, o2 = 3 * C_WIDTH, 4 * C_WIDTH
    w_qkv, w_z = w_in[:, :o1], w_in[:, o1:o2]
    w_ba = jnp.concatenate([w_in[:, o2:], jnp.zeros((d, LANES - 2 * C_HEADS), bf16)], 1)
    pqkv = mm_nn(x1b,w_qkv, name=tag + "_in_qkv")
    pz = mm_nn(x1b,w_z, name=tag + "_in_z")
    pba = mm_nn(x1b,w_ba, name=tag + "_in_ba")
    zero_b = jnp.zeros((1, o1), f32)
    qkvc = conv_fwd(pqkv, W["c_conv_w"][0], zero_b, True, nb, name=tag + "_conv_fwd")
    prep = gdn_prep_fwd(qkvc, pba, W["c_a_log"], W["c_dt_bias"], nb, name=tag + "_prep_fwd")
    yc, states = gdn_rec_fwd(*prep, pz, W["c_norm_g"], nb, name=tag + "_rec_fwd")
    w_out = W["c_w_out"][0].astype(bf16)
    x2, z1, x2b = proj_ln([yc], [w_out], x1, g, b, name=tag + "_out_ln")
    saved = (pqkv, pz, pba, qkvc, prep, states, yc, w_qkv, w_z, w_ba, w_out, zero_b)
    return x2, x2b, z1, saved


def _mixer_c_bwd(x1b, dz1, dz1b, W, saved, nb, tag):
    pqkv, pz, pba, qkvc, prep, states, yc, w_qkv, w_z, w_ba, w_out, zero_b = saved
    dyc = mm_nn(dz1b, w_out.T, name=tag + "_dyc")
    dwo = mm_tn(yc, dz1b, name=tag + "_dwo")
    rec = gdn_rec_bwd(*prep, pz, W["c_norm_g"], states, dyc, nb, name=tag + "_rec_bwd")
    cts, dpz, dng = rec[:6], rec[6], rec[7]
    dq, dk, dv, dpba, dalog, ddtb = gdn_prep_bwd(qkvc, pba, W["c_a_log"], W["c_dt_bias"], cts, nb,
                                                 name=tag + "_prep_bwd")
    dqkvc = jnp.concatenate([dq, dk, dv], 1)
    dpqkv, dconv_w, _ = conv_bwd(pqkv, W["c_conv_w"][0], zero_b, dqkvc, True, nb, name=tag + "_conv_bwd")
    dw_in = jnp.concatenate([mm_tn(x1b,dpqkv, name=tag + "_dwin_qkv"), mm_tn(x1b,dpz, name=tag + "_dwin_z"),
                             mm_tn(x1b,dpba, name=tag + "_dwin_ba")[:, :2 * C_HEADS]], 1)
    dx1 = mm_nn(dpqkv, w_qkv.T, add=dz1, add_scale=DN_ALPHA, name=tag + "_dx_qkv")
    dx1 = mm_nn(dpz, w_z.T, add=dx1, name=tag + "_dx_z")
    dx1 = mm_nn(dpba, w_ba.T, add=dx1, name=tag + "_dx_ba")
    grads = {"c_w_in": dw_in[None], "c_conv_w": dconv_w[None], "c_a_log": dalog[:, :C_HEADS],
             "c_dt_bias": ddtb[:, :C_HEADS], "c_norm_g": dng, "c_w_out": dwo[None]}
    return dx1, grads


def _local_step(x, p, target, W, F, on_ffn_grads):
    nb, s, d = x.shape
    t = nb * s
    h = x.reshape(t, d)
    hb = h.astype(bf16)
    tape = []
    for i in range(DEPTH):
        tag = f"l{i}"
        f1 = [F[k][i] for k in ("ffn1_wg", "ffn1_wu", "ffn1_wd")]
        f2 = [F[k][i] for k in ("ffn2_wg", "ffn2_wu", "ffn2_wd")]
        lg = [W["ln_g"][i, k][None] for k in range(3)]
        lb = [W["ln_b"][i, k][None] for k in range(3)]
        x1, z0, x1b = ffn_fwd(h, *f1, lg[0], lb[0], name=tag + "_ffn1_fwd")
        mixer = _mixer_ab_fwd if i % 2 == 0 else _mixer_c_fwd
        x2, x2b, z1, msaved = mixer(x1, x1b, W, lg[1], lb[1], nb, tag + "_mix")
        x3, z2, _ = ffn_fwd(x2, *f2, lg[2], lb[2], name=tag + "_ffn2_fwd")
        pi = p[i].reshape(t, -1)
        pw = (W["ple_wg"][i].astype(bf16), W["ple_bg"][i][None], W["ple_wp"][i].astype(bf16))
        x4, x4b = ple_fwd(x3, pi, *pw, name=tag + "_ple_fwd")
        tape.append((hb, z0, x1b, msaved, z1, x2b, z2, x3, pi, pw, lg))
        h, hb = x4, x4b
    dh, sq = loss_head(h, target.reshape(t, d), name="loss_head")
    loss = 0.5 * jnp.sum(sq) / d
    per_layer = [None] * DEPTH
    grads = {}
    for i in reversed(range(DEPTH)):
        tag = f"l{i}"
        hb_in, z0, x1b, msaved, z1, x2b, z2, x3, pi, pw, lg = tape[i]
        dx3, dple_wg, dple_bg, dple_wp = ple_bwd(x3, pi, dh, pw[0], pw[0].T, pw[1], pw[2], name=tag + "_ple_bwd")
        dz2, dz2b, dg2, db2 = ln_bwd(z2, dx3, lg[2], name=tag + "_ln2_bwd")
        f1 = [F[k][i] for k in ("ffn1_wg", "ffn1_wu", "ffn1_wd")]
        f2 = [F[k][i] for k in ("ffn2_wg", "ffn2_wu", "ffn2_wd")]
        dgate, dup, *df2 = ffn_bwd_weights(x2b, dz2b, *f2, name=tag + "_ffn2_bwd_w")
        on_ffn_grads(i, 3, df2)
        dx2 = ffn_bwd_input(dgate, dup, f2[0], f2[1], dz2, name=tag + "_ffn2_bwd_x")
        dz1, dz1b, dg1, db1 = ln_bwd(z1, dx2, lg[1], name=tag + "_ln1_bwd")
        mixer_bwd = _mixer_ab_bwd if i % 2 == 0 else _mixer_c_bwd
        dx1, mgrads = mixer_bwd(x1b, dz1, dz1b, W, msaved, nb, tag + "_mix")
        grads.update(mgrads)
        dz0, dz0b, dg0, db0 = ln_bwd(z0, dx1, lg[0], name=tag + "_ln0_bwd")
        dgate, dup, *df1 = ffn_bwd_weights(hb_in, dz0b, *f1, name=tag + "_ffn1_bwd_w")
        on_ffn_grads(i, 0, df1)
        dh = ffn_bwd_input(dgate, dup, f1[0], f1[1], dz0, name=tag + "_ffn1_bwd_x")
        per_layer[i] = {"ln_g": jnp.concatenate([dg0, dg1, dg2], 0), "ln_b": jnp.concatenate([db0, db1, db2], 0),
                        "ple_wg": dple_wg, "ple_bg": dple_bg[0], "ple_wp": dple_wp}
    for k in per_layer[0]:
        grads[k] = jnp.stack([per_layer[i][k] for i in range(DEPTH)])
    return loss, dh.reshape(nb, s, d), grads


WEIGHT_NAMES = ("ffn1_wg", "ffn1_wu", "ffn1_wd", "ffn2_wg", "ffn2_wu", "ffn2_wd", "ln_g", "ln_b", "ple_wg", "ple_bg",
                "ple_wp", "ab_w_in", "a_sinks", "b_conv_w", "b_conv_b", "b_wa", "b_ba", "b_wx", "b_bx", "b_lam",
                "ab_w_out", "c_w_in", "c_conv_w", "c_a_log", "c_dt_bias", "c_norm_g", "c_w_out")
NATIVE_NAMES = WEIGHT_NAMES[:6]
PACKED_NAMES = WEIGHT_NAMES[6:]
PACK_MATRICES = ("ple_wg", "ple_wp", "ab_w_in", "ab_w_out", "c_w_in", "c_w_out")
PACK_GROUPS = (tuple(k for k in PACKED_NAMES if k not in PACK_MATRICES), PACK_MATRICES)
PACK_TRANSIT = (f32, bf16)
SHARD_AXIS = {"ffn1_wg": 2, "ffn1_wu": 2, "ffn1_wd": 1, "ffn2_wg": 2, "ffn2_wu": 2, "ffn2_wd": 1, "ln_g": 2, "ln_b": 2,
              "ple_wg": 1, "ple_wp": 2, "ab_w_in": 2, "b_conv_w": 2, "ab_w_out": 1, "c_w_in": 2, "c_conv_w": 2,
              "c_w_out": 1}
N_CHIPS = 4
PACK_COLS = LANES
PACK_TILE_MULTIPLE = 256
ELEMENTWISE_BLOCK_ELEMS = 128 * 1024


def _row_tile(r, cols):
    return _tile(r, max(2 * SUBLANES, ELEMENTWISE_BLOCK_ELEMS // cols), 2 * SUBLANES)
MESH = pl.DeviceIdType.MESH
ANY = pl.BlockSpec(memory_space=pl.ANY)


def _tiled_dims(shape):
    w = shape[-1]
    r = 1
    for dim in shape[:-1]:
        r *= dim
    return r, w, -(-r // SUBLANES) * SUBLANES, -(-w // LANES) * LANES


def _pack(pieces, lead=()):
    k = len(lead)
    tiles = []
    for a in pieces:
        r, w, rp, wp = _tiled_dims(a.shape[k:])
        a2 = jnp.pad(a.reshape(lead + (r, w)), [(0, 0)] * k + [(0, rp - r), (0, wp - w)])
        a2 = a2.reshape(lead + (rp // SUBLANES, SUBLANES, wp // LANES, LANES))
        a2 = jnp.swapaxes(a2, k + 1, k + 2)
        tiles.append(a2.reshape(lead + (-1, SUBLANES, LANES)))
    flat = jnp.concatenate(tiles, axis=k)
    n = flat.shape[k]
    n_pad = -(-n // PACK_TILE_MULTIPLE) * PACK_TILE_MULTIPLE
    flat = jnp.pad(flat, [(0, 0)] * k + [(0, n_pad - n), (0, 0), (0, 0)])
    return flat.reshape(lead + (n_pad * SUBLANES, PACK_COLS))


def _unpack(pack, shapes, lead=()):
    k = len(lead)
    flat = pack.reshape(lead + (-1, SUBLANES, LANES))
    out, o = [], 0
    for shp in shapes:
        r, w, rp, wp = _tiled_dims(shp)
        n = (rp // SUBLANES) * (wp // LANES)
        a2 = lax.slice_in_dim(flat, o, o + n, axis=k).reshape(lead + (rp // SUBLANES, wp // LANES, SUBLANES, LANES))
        a2 = jnp.swapaxes(a2, k + 1, k + 2).reshape(lead + (rp, wp))
        a2 = lax.slice_in_dim(lax.slice_in_dim(a2, 0, r, axis=k), 0, w, axis=k + 1)
        out.append(a2.reshape(lead + tuple(shp)))
        o += n
    return out


def _mesh_position():
    x, y, c = lax.axis_index("x"), lax.axis_index("y"), lax.axis_index("c")
    chips = [(1 - x, y), (x, 1 - y), (1 - x, 1 - y)]
    return x, y, c, chips


def _remote(src, dst, send_sems, recv_sems, k, to):
    return pltpu.make_async_remote_copy(src_ref=src, dst_ref=dst, send_sem=send_sems.at[k], recv_sem=recv_sems.at[k],
                                        device_id=to, device_id_type=MESH)


def _sems(n):
    return pltpu.SemaphoreType.DMA((n,))


def place_slot(parts, slots, n_slots, dtype, from_slot, *, name):
    n = len(parts)
    r, cols = parts[0].shape[-2:]
    tr = _row_tile(r, cols)

    def body(src_ref, dst_ref, *refs):
        for a in range(n):
            refs[n + a][...] = refs[a][...].astype(dtype)

    dst = pl.BlockSpec((None, tr, cols), lambda i, src_ref, dst_ref: (dst_ref[0], i, 0))
    src = (pl.BlockSpec((None, tr, cols), lambda i, src_ref, dst_ref: (src_ref[0], i, 0)) if from_slot
           else pl.BlockSpec((tr, cols), lambda i, src_ref, dst_ref: (i, 0)))
    return pl.pallas_call(
        body,
        grid_spec=pltpu.PrefetchScalarGridSpec(num_scalar_prefetch=2, grid=(r // tr,), in_specs=[src] * n,
                                               out_specs=[dst] * n),
        out_shape=[jax.ShapeDtypeStruct((n_slots, r, cols), dtype)] * n,
        compiler_params=_params(("parallel",)), name=name,
    )(*slots, *parts)


def gather_shards(bufs, *, name):
    n = len(bufs)

    def body(*refs):
        out_refs = refs[n:2 * n]
        send_sems, recv_sems = refs[2 * n:]
        x, y, c, chips = _mesh_position()
        me = 2 * x + y
        sibling = (x, y, 1 - c)
        waits = []
        for j, (cx, cy) in enumerate(chips):
            for a in range(n):
                own = out_refs[a].at[me, c]
                cp = _remote(own, own, send_sems, recv_sems, 6 * a + j, (cx, cy, c))
                cp.start()
                waits.append(cp.wait_send)
        for j, (cx, cy) in enumerate(chips):
            for a in range(n):
                got = out_refs[a].at[2 * cx + cy, c]
                _remote(got, got, send_sems, recv_sems, 6 * a + j, (cx, cy, c)).wait_recv()
                fw = _remote(got, got, send_sems, recv_sems, 6 * a + 3 + j, sibling)
                fw.start()
                waits.append(fw.wait_send)
        for j, (cx, cy) in enumerate(chips):
            for a in range(n):
                got = out_refs[a].at[2 * cx + cy, 1 - c]
                _remote(got, got, send_sems, recv_sems, 6 * a + 3 + j, sibling).wait_recv()
        for wait in waits:
            wait()

    return pl.pallas_call(
        body, out_shape=[jax.ShapeDtypeStruct(b.shape, b.dtype) for b in bufs],
        in_specs=[ANY] * n, out_specs=[ANY] * n, scratch_shapes=[_sems(6 * n), _sems(6 * n)],
        input_output_aliases={a: a for a in range(n)}, name=name,
    )(*bufs)


def chip_exchange(ps, qs, *, name):
    n = len(ps)

    def body(*refs):
        p_refs, q_refs = refs[:n], refs[2 * n:3 * n]
        send_sems, recv_sems = refs[3 * n:]
        x, y, c, chips = _mesh_position()
        me = 2 * x + y
        waits = []
        for j, (cx, cy) in enumerate(chips):
            for a in range(n):
                cp = _remote(p_refs[a].at[2 * cx + cy], q_refs[a].at[me], send_sems, recv_sems, 3 * a + j, (cx, cy, c))
                cp.start()
                waits.append(cp.wait_send)
        for j, (cx, cy) in enumerate(chips):
            for a in range(n):
                got = q_refs[a].at[2 * cx + cy]
                _remote(got, got, send_sems, recv_sems, 3 * a + j, (cx, cy, c)).wait_recv()
        for wait in waits:
            wait()

    return pl.pallas_call(
        body, out_shape=[jax.ShapeDtypeStruct(q_.shape, q_.dtype) for q_ in qs], in_specs=[ANY] * (2 * n),
        out_specs=[ANY] * n, scratch_shapes=[_sems(3 * n), _sems(3 * n)],
        input_output_aliases={n + a: a for a in range(n)}, name=name,
    )(*ps, *qs)


def gather_slots_async(bufs, collective_id, *, name):
    n = len(bufs)
    refs = [jax.new_ref(b, memory_space=pltpu.MemorySpace.HBM) for b in bufs]

    @pl.kernel(mesh=plsc.ScalarSubcoreMesh(axis_name="sequencer", num_cores=1), name=name,
               scratch_types=(_sems(3 * n), _sems(3 * n)),
               compiler_params=pltpu.CompilerParams(collective_id=collective_id))
    def launch(send_sems, recv_sems):
        x, y, c, chips = _mesh_position()
        me = 2 * x + y
        barrier = pltpu.get_barrier_semaphore()
        for cx, cy in chips:
            pl.semaphore_signal(barrier, inc=1, device_id=(cx, cy, c), device_id_type=MESH)
        pl.semaphore_wait(barrier, len(chips))
        sends = []
        for j, (cx, cy) in enumerate(chips):
            for a in range(n):
                own = refs[a].at[me]
                cp = _remote(own, own, send_sems, recv_sems, 3 * a + j, (cx, cy, c))
                cp.start()
                sends.append(cp)
        for j, (cx, cy) in enumerate(chips):
            for a in range(n):
                got = refs[a].at[2 * cx + cy]
                _remote(got, got, send_sems, recv_sems, 3 * a + j, (cx, cy, c)).wait_recv()
        for cp in sends:
            cp.wait_send()

    launch()
    return [r[...] for r in refs]


N_DEVICES = 8
PEER_FLIPS = tuple((dx, dy, dc) for dx in (0, 1) for dy in (0, 1) for dc in (0, 1) if dx or dy or dc)


def exchange_partials_async(sends, collective_id, *, name):
    n = len(sends)
    k = len(PEER_FLIPS)

    def launch(*refs):
        s_refs, r_refs = refs[:n], refs[n:2 * n]
        send_sems, recv_sems, local_sems = refs[2 * n:]
        x, y, c, _ = _mesh_position()
        me = 4 * x + 2 * y + c
        peers = [(1 - x if dx else x, 1 - y if dy else y, 1 - c if dc else c) for dx, dy, dc in PEER_FLIPS]
        barrier = pltpu.get_barrier_semaphore()
        for peer in peers:
            pl.semaphore_signal(barrier, inc=1, device_id=peer, device_id_type=MESH)
        pl.semaphore_wait(barrier, len(peers))
        sends_started = []
        for a in range(n):
            own = pltpu.make_async_copy(s_refs[a].at[2 * x + y], r_refs[a].at[me], local_sems.at[a])
            own.start()
            sends_started.append(own)
        for j, (px, py, pc) in enumerate(peers):
            for a in range(n):
                cp = _remote(s_refs[a].at[2 * px + py], r_refs[a].at[me], send_sems, recv_sems, j, (px, py, pc))
                cp.start()
                sends_started.append(cp)
        for j, (px, py, pc) in enumerate(peers):
            for a in range(n):
                got = r_refs[a].at[4 * px + 2 * py + pc]
                _remote(got, got, send_sems, recv_sems, j, (px, py, pc)).wait_recv()
        for cp in sends_started[n:]:
            cp.wait_send()
        for cp in sends_started[:n]:
            cp.wait()

    return list(pl.kernel(
        launch, out_type=[jax.ShapeDtypeStruct((N_DEVICES,) + s_.shape[1:], s_.dtype) for s_ in sends],
        mesh=plsc.ScalarSubcoreMesh(axis_name="sequencer", num_cores=1), name=name,
        scratch_types=(_sems(k), _sems(k), _sems(n)),
        compiler_params=pltpu.CompilerParams(collective_id=collective_id))(*sends))


def sibling_exchange(gs, *, name):
    n = len(gs)

    def body(*refs):
        g_refs, out_refs = refs[:n], refs[n:2 * n]
        send_sems, recv_sems = refs[2 * n:]
        x, y, c, _ = _mesh_position()
        cps = [_remote(g_refs[a].at[:, 1 - c], out_refs[a], send_sems, recv_sems, a, (x, y, 1 - c)) for a in range(n)]
        for cp in cps:
            cp.start()
        for cp in cps:
            cp.wait()

    return pl.pallas_call(
        body, out_shape=[jax.ShapeDtypeStruct(g.shape[:1] + g.shape[2:], g.dtype) for g in gs],
        in_specs=[ANY] * n, out_specs=[ANY] * n, scratch_shapes=[_sems(n), _sems(n)], name=name,
    )(*gs)


def add_own_half(gs, others, c_idx, dtype, *, name):
    n = len(gs)
    ns, _, r, cols = gs[0].shape
    tr = _row_tile(r, cols)

    def body(c_ref, *refs):
        for a in range(n):
            refs[2 * n + a][...] = (refs[a][...] + refs[n + a][...]).astype(dtype)

    own = pl.BlockSpec((None, None, tr, cols), lambda s, i, c_ref: (s, c_ref[0], i, 0))
    oth = pl.BlockSpec((None, tr, cols), lambda s, i, c_ref: (s, i, 0))
    return pl.pallas_call(
        body,
        grid_spec=pltpu.PrefetchScalarGridSpec(num_scalar_prefetch=1, grid=(ns, r // tr),
                                               in_specs=[own] * n + [oth] * n, out_specs=[oth] * n),
        out_shape=[jax.ShapeDtypeStruct((ns, r, cols), dtype)] * n,
        compiler_params=_params(("parallel", "parallel")), name=name,
    )(c_idx, *gs, *others)


def sum_slots(qs, *, name):
    n = len(qs)
    ns, r, cols = qs[0].shape
    tr = _row_tile(r, cols * ns)

    def body(*refs):
        for a in range(n):
            q_ref = refs[a]
            acc = q_ref[0].astype(f32) + q_ref[1].astype(f32)
            for i in range(2, ns):
                acc = acc + q_ref[i].astype(f32)
            refs[n + a][...] = acc

    return pl.pallas_call(
        body, grid=(r // tr,), in_specs=[pl.BlockSpec((ns, tr, cols), lambda i: (0, i, 0))] * n,
        out_specs=[pl.BlockSpec((tr, cols), lambda i: (i, 0))] * n,
        out_shape=[jax.ShapeDtypeStruct((r, cols), f32)] * n,
        compiler_params=_params(("parallel",)), name=name,
    )(*qs)


def sibling_share(bufs, *, name):
    n = len(bufs)

    def body(*refs):
        out_refs = refs[n:2 * n]
        send_sems, recv_sems = refs[2 * n:]
        x, y, c, _ = _mesh_position()
        sibling = (x, y, 1 - c)
        cps = []
        for a in range(n):
            own = out_refs[a].at[c]
            cp = _remote(own, own, send_sems, recv_sems, a, sibling)
            cp.start()
            cps.append(cp)
        for a in range(n):
            theirs = out_refs[a].at[1 - c]
            _remote(theirs, theirs, send_sems, recv_sems, a, sibling).wait_recv()
        for cp in cps:
            cp.wait_send()

    return pl.pallas_call(
        body, out_shape=[jax.ShapeDtypeStruct(b.shape, b.dtype) for b in bufs], in_specs=[ANY] * n,
        out_specs=[ANY] * n, scratch_shapes=[_sems(n), _sems(n)],
        input_output_aliases={a: a for a in range(n)}, name=name,
    )(*bufs)


def _adamw_update(w, g, m, v):
    m2 = ADAM_B1 * m + (1.0 - ADAM_B1) * g
    v2 = ADAM_B2 * v + (1.0 - ADAM_B2) * (g * g)
    m_hat = m2 / (1.0 - ADAM_B1 ** ADAM_STEP)
    v_hat = v2 / (1.0 - ADAM_B2 ** ADAM_STEP)
    return -ADAM_LR * (m_hat / (jnp.sqrt(v_hat) + ADAM_EPS) + ADAM_WD * w), m2, v2


def adamw_from_partials(ws, ms, vs, slots, layer, acc, *, name):
    n = len(ws)
    nl, r, cols = ws[0].shape
    ns = slots[0].shape[0]
    tr = _row_tile(r, cols * 2)

    def body(*refs):
        for a in range(n):
            w_ref, m_ref, v_ref, s_ref = (refs[k * n + a] for k in range(4))
            g_ref, d_ref, m2_ref, v2_ref = (refs[len(refs) - 4 * n + k * n + a] for k in range(4))
            g = s_ref[0].astype(f32) + s_ref[1].astype(f32)
            for i in range(2, ns):
                g = g + s_ref[i].astype(f32)
            g_ref[...] = g
            d_ref[...], m2_ref[...], v2_ref[...] = _adamw_update(w_ref[...], g, m_ref[...], v_ref[...])

    lay = pl.BlockSpec((None, tr, cols), lambda i: (layer, i, 0))
    in_specs = [lay] * (3 * n) + [pl.BlockSpec((ns, tr, cols), lambda i: (0, i, 0))] * n
    args = [*ws, *ms, *vs, *slots]
    aliases = {}
    if acc is not None:
        in_specs += [ANY] * (4 * n)
        args += [a for lst in acc for a in lst]
        aliases = {4 * n + k: k for k in range(4 * n)}
    out = pl.pallas_call(
        body, grid=(r // tr,), in_specs=in_specs, out_specs=[lay] * (4 * n),
        out_shape=[jax.ShapeDtypeStruct((nl, r, cols), f32)] * (4 * n), input_output_aliases=aliases,
        compiler_params=_params(("parallel",)), name=name,
    )(*args)
    return [list(out[k * n:(k + 1) * n]) for k in range(4)]


def adamw(ws, gs, ms, vs, *, name):
    n = len(ws)
    r, cols = ws[0].shape
    tr = _row_tile(r, cols)

    def body(*refs):
        for a in range(n):
            w_ref, g_ref, m_ref, v_ref = (refs[k * n + a] for k in range(4))
            d_ref, m2_ref, v2_ref = (refs[(4 + k) * n + a] for k in range(3))
            d_ref[...], m2_ref[...], v2_ref[...] = _adamw_update(w_ref[...], g_ref[...], m_ref[...], v_ref[...])

    row = pl.BlockSpec((tr, cols), lambda i: (i, 0))
    out = pl.pallas_call(
        body, grid=(r // tr,), in_specs=[row] * (4 * n), out_specs=[row] * (3 * n),
        out_shape=[jax.ShapeDtypeStruct((r, cols), f32)] * (3 * n),
        compiler_params=_params(("parallel",)), name=name,
    )(*ws, *gs, *ms, *vs)
    return out[:n], out[n:2 * n], out[2 * n:]


def _full_weights(gathered, names, weights):
    pieces = _unpack(gathered, [weights[k].shape for k in names], lead=(N_CHIPS,))
    full = {}
    for name, pc in zip(names, pieces):
        ax = SHARD_AXIS.get(name)
        if ax is None:
            full[name] = weights[name]
        else:
            shp = weights[name].shape
            full[name] = jnp.moveaxis(pc, 0, ax).reshape(shp[:ax] + (N_CHIPS * shp[ax],) + shp[ax + 1:])
    return full


def _grad_pack(grads, names, shapes):
    pieces = []
    for name, shp in zip(names, shapes):
        g = grads[name]
        ax = SHARD_AXIS.get(name)
        if ax is None:
            pieces.append(jnp.broadcast_to(g.reshape(shp)[None], (N_CHIPS,) + tuple(shp)))
        else:
            pieces.append(jnp.stack(jnp.split(g, N_CHIPS, axis=ax)))
    return _pack(pieces, lead=(N_CHIPS,))


def _by_shape(arrays):
    groups = {}
    for i, a in enumerate(arrays):
        groups.setdefault(a.shape, []).append(i)
    return list(groups.values())


def _grouped(fn, lists, n_out, tag):
    outs = [[None] * len(lists[0]) for _ in range(n_out)]
    for gi, idx in enumerate(_by_shape(lists[0])):
        res = fn(*[[lst[i] for i in idx] for lst in lists], name=f"{tag}_{gi}")
        res = res if n_out > 1 else (res,)
        for k in range(n_out):
            for i, r in zip(idx, res[k]):
                outs[k][i] = r
    return outs if n_out > 1 else outs[0]


def _train_step(x, p, loss_target, weights, m, v):
    shapes = [[weights[k].shape for k in names] for names in PACK_GROUPS]
    halves = lambda a: a.reshape((2, a.shape[0] // 2) + a.shape[1:])
    packs = lambda d_: [halves(_pack([d_[k] for k in names])) for names in PACK_GROUPS]
    nn_ = len(NATIVE_NAMES)
    local = [weights[k] for k in NATIVE_NAMES] + packs(weights)
    local_m = [m[k] for k in NATIVE_NAMES] + packs(m)
    local_v = [v[k] for k in NATIVE_NAMES] + packs(v)
    flat = lambda lst: [a.reshape((-1, a.shape[-1])) for a in lst]
    c_idx = lax.axis_index("c").astype(jnp.int32).reshape(1)
    chip_idx = (2 * lax.axis_index("x") + lax.axis_index("y")).astype(jnp.int32).reshape(1)
    c2 = (c_idx, c_idx)
    chip2 = (chip_idx, chip_idx)

    def placed(arrays, slot, n_slots, dtype, from_slot, tag):
        return _grouped(lambda a, name: place_slot(a, slot, n_slots, dtype, from_slot, name=name), [arrays], 1, tag)

    ffn_own = [weights[k][i] for i in range(DEPTH) for k in NATIVE_NAMES]
    ffn_bufs = placed(ffn_own, chip2, N_CHIPS, bf16, False, "place_ffn_weights")
    group = len(NATIVE_NAMES) // 2
    n_ffn_groups = len(ffn_bufs) // group
    ffn_gathered = []
    for gi in range(n_ffn_groups):
        ffn_gathered += gather_slots_async(ffn_bufs[gi * group:(gi + 1) * group], collective_id=1 + gi,
                                           name=f"comm_gather_ffn_{gi}")
    ffn_weights = {k: [ffn_gathered[i * len(NATIVE_NAMES) + j] for i in range(DEPTH)] for j, k in enumerate(NATIVE_NAMES)}
    pack_bufs = [placed(flat([a]), chip2, N_CHIPS, dt, False, f"place_packed_weights_{gi}")[0].reshape((N_CHIPS,) + a.shape)
                 for gi, (a, dt) in enumerate(zip(local[nn_:], PACK_TRANSIT))]
    full = {}
    for names, gathered in zip(PACK_GROUPS, gather_shards(pack_bufs, name="comm_gather_weights")):
        full.update(_full_weights(gathered, names, weights))
    first_grad_id = n_ffn_groups + 1
    in_flight = {}

    def on_ffn_grads(layer, first, partials):
        tag = f"ffn_grads_l{layer}_{first}"
        got = exchange_partials_async(partials, collective_id=first_grad_id + len(in_flight), name="comm_" + tag)
        in_flight[(layer, first)] = got

    loss, grad_x, grads = _local_step(x, p, loss_target, full, ffn_weights, on_ffn_grads)
    gs = [_grad_pack(grads, names, shp).reshape((N_CHIPS,) + a.shape)
          for names, shp, a in zip(PACK_GROUPS, shapes, local[nn_:])]
    others = sibling_exchange(gs, name="comm_grad_sibling")
    chip_sums = [add_own_half([g], [o], c_idx, dt, name=f"grad_add_sibling_{gi}")[0]
                 for gi, (g, o, dt) in enumerate(zip(gs, others, PACK_TRANSIT))]
    own = [placed([cs], chip2, N_CHIPS, dt, True, f"place_own_partial_{gi}")[0]
           for gi, (cs, dt) in enumerate(zip(chip_sums, PACK_TRANSIT))]
    slots = chip_exchange(chip_sums, own, name="comm_grad_chips")
    mine = _grouped(sum_slots, [list(slots)], 1, "grad_sum_chips")
    pack_sum = sibling_share(placed(mine, c2, 2, f32, False, "place_own_half"), name="comm_grad_share")
    ffn_out = [{} for _ in range(4)]
    for (layer, first), got in in_flight.items():
        names = NATIVE_NAMES[first:first + len(got)]
        for idx in _by_shape([weights[k] for k in names]):
            ks = [names[i] for i in idx]
            acc = [[out[k] for k in ks] for out in ffn_out] if ks[0] in ffn_out[0] else None
            res = adamw_from_partials([weights[k] for k in ks], [m[k] for k in ks], [v[k] for k in ks],
                                      [got[i] for i in idx], layer, acc, name=f"adamw_ffn_l{layer}_{first + idx[0]}")
            for out, arrays in zip(ffn_out, res):
                out.update(zip(ks, arrays))
    pack_out = [list(pack_sum)] + _grouped(adamw, [flat(local[nn_:]), flat(pack_sum), flat(local_m[nn_:]),
                                                    flat(local_v[nn_:])], 3, "adamw_packed")
    loss = lax.psum(loss, ("x", "y", "c"))
    outs = []
    for by_name, packs_ in zip(ffn_out, pack_out):
        by_name = dict(by_name)
        for names, shp, pk in zip(PACK_GROUPS, shapes, packs_):
            by_name.update(zip(names, _unpack(pk, shp)))
        outs += [by_name[k] for k in WEIGHT_NAMES]
    return (loss, grad_x, *outs)


def kernel(x, p, ffn1_wg, ffn1_wu, ffn1_wd, ffn2_wg, ffn2_wu, ffn2_wd, ln_g, ln_b, ple_wg, ple_bg, ple_wp, ab_w_in, a_sinks, b_conv_w, b_conv_b, b_wa, b_ba, b_wx, b_bx, b_lam, ab_w_out, c_w_in, c_conv_w, c_a_log, c_dt_bias, c_norm_g, c_w_out, loss_target, m_ffn1_wg, m_ffn1_wu, m_ffn1_wd, m_ffn2_wg, m_ffn2_wu, m_ffn2_wd, m_ln_g, m_ln_b, m_ple_wg, m_ple_bg, m_ple_wp, m_ab_w_in, m_a_sinks, m_b_conv_w, m_b_conv_b, m_b_wa, m_b_ba, m_b_wx, m_b_bx, m_b_lam, m_ab_w_out, m_c_w_in, m_c_conv_w, m_c_a_log, m_c_dt_bias, m_c_norm_g, m_c_w_out, v_ffn1_wg, v_ffn1_wu, v_ffn1_wd, v_ffn2_wg, v_ffn2_wu, v_ffn2_wd, v_ln_g, v_ln_b, v_ple_wg, v_ple_bg, v_ple_wp, v_ab_w_in, v_a_sinks, v_b_conv_w, v_b_conv_b, v_b_wa, v_b_ba, v_b_wx, v_b_bx, v_b_lam, v_ab_w_out, v_c_w_in, v_c_conv_w, v_c_a_log, v_c_dt_bias, v_c_norm_g, v_c_w_out):
    weights = [ffn1_wg, ffn1_wu, ffn1_wd, ffn2_wg, ffn2_wu, ffn2_wd, ln_g, ln_b, ple_wg, ple_bg, ple_wp, ab_w_in, a_sinks,
               b_conv_w, b_conv_b, b_wa, b_ba, b_wx, b_bx, b_lam, ab_w_out, c_w_in, c_conv_w, c_a_log, c_dt_bias, c_norm_g,
               c_w_out]
    m = [m_ffn1_wg, m_ffn1_wu, m_ffn1_wd, m_ffn2_wg, m_ffn2_wu, m_ffn2_wd, m_ln_g, m_ln_b, m_ple_wg, m_ple_bg, m_ple_wp,
         m_ab_w_in, m_a_sinks, m_b_conv_w, m_b_conv_b, m_b_wa, m_b_ba, m_b_wx, m_b_bx, m_b_lam, m_ab_w_out, m_c_w_in,
         m_c_conv_w, m_c_a_log, m_c_dt_bias, m_c_norm_g, m_c_w_out]
    v = [v_ffn1_wg, v_ffn1_wu, v_ffn1_wd, v_ffn2_wg, v_ffn2_wu, v_ffn2_wd, v_ln_g, v_ln_b, v_ple_wg, v_ple_bg, v_ple_wp,
         v_ab_w_in, v_a_sinks, v_b_conv_w, v_b_conv_b, v_b_wa, v_b_ba, v_b_wx, v_b_bx, v_b_lam, v_ab_w_out, v_c_w_in,
         v_c_conv_w, v_c_a_log, v_c_dt_bias, v_c_norm_g, v_c_w_out]
    return _train_step(x, p, loss_target, dict(zip(WEIGHT_NAMES, weights)), dict(zip(WEIGHT_NAMES, m)),
                       dict(zip(WEIGHT_NAMES, v)))
```

```python
import functools

import jax
import jax.numpy as jnp
from jax import lax
from jax.experimental import pallas as pl
from jax.experimental.pallas import tpu as pltpu
from jax.experimental.pallas import tpu_sc as plsc

f32 = jnp.float32
bf16 = jnp.bfloat16

DEPTH = 2
CHUNK = 64
A_HEADS, A_KV_HEADS, A_GROUP, A_HEAD_DIM = 8, 2, 4, 64
A_WIDTH, A_KV_WIDTH, A_WINDOW = 512, 128, 128
B_WIDTH, B_BLOCKS, B_BLOCK, B_CONV = 512, 8, 64, 4
RG_C = 8.0
C_HEADS, C_HEAD_DIM, C_WIDTH, C_CONV = 8, 128, 1024, 4
DN_ALPHA = (2.0 * DEPTH) ** 0.25
LN_EPS = 1e-5
NORM_EPS = 1e-6
NEG = -1e30
ADAM_LR, ADAM_B1, ADAM_B2, ADAM_EPS, ADAM_WD, ADAM_STEP = 0.001, 0.9, 0.999, 1e-08, 0.01, 10

VMEM_LIMIT_BYTES = 56 * 1024 * 1024
LANES = 128
SUBLANES = 8
GROUP_W = 128
PREP_FWD_UNROLL = 16
PREP_BWD_UNROLL = 16
C_HEADS_PER_STEP = 8
GDN_TIME_BLOCK = 256

NN = ((1,), (0,))
NT = ((1,), (1,))
TN = ((0,), (0,))


def _params(sem):
    return pltpu.CompilerParams(dimension_semantics=sem, vmem_limit_bytes=VMEM_LIMIT_BYTES)


def _tile(n, cap, mult):
    best = None
    t = mult
    while t <= min(n, cap):
        if n % t == 0:
            best = t
        t += mult
    return best if best is not None else n


def _bdot(a, b, dims):
    return lax.dot_general(a.astype(bf16), b.astype(bf16), (dims, ((), ())), preferred_element_type=f32)


def _running_sum(x, reverse):
    s = x.shape[0]
    t = lax.broadcasted_iota(jnp.int32, x.shape, 0)
    d = 1
    while d < s:
        if reverse:
            x = x + jnp.where(t < s - d, pltpu.roll(x, s - d, 0), 0.0)
        else:
            x = x + jnp.where(t >= d, pltpu.roll(x, d, 0), 0.0)
        d *= 2
    return x


@jax.custom_vjp
def _cumsum0(x):
    return _running_sum(x, False)


def _cumsum0_fwd(x):
    return _running_sum(x, False), None


def _cumsum0_bwd(_, g):
    return (_running_sum(g, True),)


_cumsum0.defvjp(_cumsum0_fwd, _cumsum0_bwd)


@jax.custom_vjp
def _bnn(a, b):
    return _bdot(a, b, NN)


def _bnn_fwd(a, b):
    return _bdot(a, b, NN), (a, b)


def _bnn_bwd(res, g):
    a, b = res
    return _bdot(g, b, NT), _bdot(a, g, TN)


_bnn.defvjp(_bnn_fwd, _bnn_bwd)


@jax.custom_vjp
def _bnt(a, b):
    return _bdot(a, b, NT)


def _bnt_fwd(a, b):
    return _bdot(a, b, NT), (a, b)


def _bnt_bwd(res, g):
    a, b = res
    return _bdot(g, b, NN), _bdot(g, a, TN)


_bnt.defvjp(_bnt_fwd, _bnt_bwd)


@jax.custom_vjp
def _btn(a, b):
    return _bdot(a, b, TN)


def _btn_fwd(a, b):
    return _bdot(a, b, TN), (a, b)


def _btn_bwd(res, g):
    a, b = res
    return _bdot(b, g, NT), _bdot(a, g, NN)


_btn.defvjp(_btn_fwd, _btn_bwd)

RAW_DOTS = (lambda a, b: _bdot(a, b, NN), lambda a, b: _bdot(a, b, NT), lambda a, b: _bdot(a, b, TN),
            lambda x: _running_sum(x, False))
VJP_DOTS = (_bnn, _bnt, _btn, _cumsum0)


def _layer_norm(z, g, b):
    mu = jnp.mean(z, -1, keepdims=True)
    d = z - mu
    var = jnp.mean(d * d, -1, keepdims=True)
    return d * lax.rsqrt(var + LN_EPS) * g + b


def _silu(x):
    return x * jax.nn.sigmoid(x)


def mm_nn(a, w, add=None, add_scale=1.0, *, name):
    m, k = a.shape
    n = w.shape[1]
    tm = _tile(m, 1024, 2 * SUBLANES)
    tn = _tile(n, 1024, LANES)

    def body(*refs):
        if add is None:
            a_ref, w_ref, o_ref = refs
            o_ref[...] = _bdot(a_ref[...], w_ref[...], NN)
        else:
            a_ref, w_ref, add_ref, o_ref = refs
            o_ref[...] = _bdot(a_ref[...], w_ref[...], NN) + add_scale * add_ref[...]

    in_specs = [pl.BlockSpec((tm, k), lambda i, j: (i, 0)), pl.BlockSpec((k, tn), lambda i, j: (0, j))]
    args = [a, w]
    if add is not None:
        in_specs.append(pl.BlockSpec((tm, tn), lambda i, j: (i, j)))
        args.append(add)
    return pl.pallas_call(
        body, grid=(m // tm, n // tn), in_specs=in_specs,
        out_specs=pl.BlockSpec((tm, tn), lambda i, j: (i, j)),
        out_shape=jax.ShapeDtypeStruct((m, n), f32),
        compiler_params=_params(("parallel", "parallel")), name=name,
    )(*args)


def mm_tn(a, b, *, name):
    m, k = a.shape
    n = b.shape[1]
    tm = _tile(m, 1024, 2 * SUBLANES)
    tn = _tile(n, 1024, LANES)

    def body(a_ref, b_ref, o_ref):
        part = _bdot(a_ref[...], b_ref[...], TN)

        @pl.when(pl.program_id(1) == 0)
        def _():
            o_ref[...] = part

        @pl.when(pl.program_id(1) > 0)
        def _():
            o_ref[...] += part

    return pl.pallas_call(
        body, grid=(n // tn, m // tm),
        in_specs=[pl.BlockSpec((tm, k), lambda j, i: (i, 0)), pl.BlockSpec((tm, tn), lambda j, i: (i, j))],
        out_specs=pl.BlockSpec((k, tn), lambda j, i: (0, j)),
        out_shape=jax.ShapeDtypeStruct((k, n), f32),
        compiler_params=_params(("parallel", "arbitrary")), name=name,
    )(a, b)


def proj_ln(a_list, w_list, xres, g, b, *, name):
    t, d = xres.shape
    tm = _tile(t, 256, 2 * SUBLANES)
    na = len(a_list)

    def body(*refs):
        a_refs, w_refs = refs[:na], refs[na:2 * na]
        x_ref, g_ref, b_ref, y_ref, z_ref, yb_ref = refs[2 * na:]
        z = DN_ALPHA * x_ref[...]
        for a_ref, w_ref in zip(a_refs, w_refs):
            z = z + _bdot(a_ref[...], w_ref[...], NN)
        z_ref[...] = z
        y = _layer_norm(z, g_ref[...], b_ref[...])
        y_ref[...] = y
        yb_ref[...] = y.astype(bf16)

    in_specs = [pl.BlockSpec((tm, a.shape[1]), lambda i: (i, 0)) for a in a_list]
    in_specs += [pl.BlockSpec(w.shape, lambda i: (0, 0)) for w in w_list]
    in_specs += [pl.BlockSpec((tm, d), lambda i: (i, 0)), pl.BlockSpec((1, d), lambda i: (0, 0)),
                 pl.BlockSpec((1, d), lambda i: (0, 0))]
    return pl.pallas_call(
        body, grid=(t // tm,), in_specs=in_specs,
        out_specs=[pl.BlockSpec((tm, d), lambda i: (i, 0))] * 3,
        out_shape=[jax.ShapeDtypeStruct((t, d), f32)] * 2 + [jax.ShapeDtypeStruct((t, d), bf16)],
        compiler_params=_params(("parallel",)), name=name,
    )(*a_list, *w_list, xres, g, b)


def ln_bwd(z, dy, g, *, name):
    t, d = z.shape
    tm = _tile(t, 512, SUBLANES)

    def body(z_ref, dy_ref, g_ref, dz_ref, dzb_ref, dg_ref, db_ref):
        zz = z_ref[...]
        dy_ = dy_ref[...]
        mu = jnp.mean(zz, -1, keepdims=True)
        dd = zz - mu
        var = jnp.mean(dd * dd, -1, keepdims=True)
        rstd = lax.rsqrt(var + LN_EPS)
        xhat = dd * rstd
        dxh = dy_ * g_ref[...]
        dz = rstd * (dxh - jnp.mean(dxh, -1, keepdims=True) - xhat * jnp.mean(dxh * xhat, -1, keepdims=True))
        dz_ref[...] = dz
        dzb_ref[...] = dz.astype(bf16)
        pg = jnp.sum(dy_ * xhat, 0, keepdims=True)
        pb = jnp.sum(dy_, 0, keepdims=True)

        @pl.when(pl.program_id(0) == 0)
        def _():
            dg_ref[...] = pg
            db_ref[...] = pb

        @pl.when(pl.program_id(0) > 0)
        def _():
            dg_ref[...] += pg
            db_ref[...] += pb

    row = pl.BlockSpec((tm, d), lambda i: (i, 0))
    vec = pl.BlockSpec((1, d), lambda i: (0, 0))
    return pl.pallas_call(
        body, grid=(t // tm,), in_specs=[row, row, vec], out_specs=[row, row, vec, vec],
        out_shape=[jax.ShapeDtypeStruct((t, d), f32), jax.ShapeDtypeStruct((t, d), bf16),
                   jax.ShapeDtypeStruct((1, d), f32), jax.ShapeDtypeStruct((1, d), f32)],
        compiler_params=_params(("arbitrary",)), name=name,
    )(z, dy, g)


def loss_head(y, target, *, name):
    t, d = y.shape
    tm = _tile(t, 512, SUBLANES)

    def body(y_ref, t_ref, dy_ref, sq_ref):
        e = y_ref[...] - t_ref[...]
        dy_ref[...] = e * (1.0 / d)
        part = jnp.sum(e * e, 0, keepdims=True)

        @pl.when(pl.program_id(0) == 0)
        def _():
            sq_ref[...] = part

        @pl.when(pl.program_id(0) > 0)
        def _():
            sq_ref[...] += part

    row = pl.BlockSpec((tm, d), lambda i: (i, 0))
    vec = pl.BlockSpec((1, d), lambda i: (0, 0))
    return pl.pallas_call(
        body, grid=(t // tm,), in_specs=[row, row], out_specs=[row, vec],
        out_shape=[jax.ShapeDtypeStruct((t, d), f32), jax.ShapeDtypeStruct((1, d), f32)],
        compiler_params=_params(("arbitrary",)), name=name,
    )(y, target)


FFN_COL_BLOCK = 256
FFN_ROWS = 1024


def _lane_blocks(n):
    return [slice(s, min(s + FFN_COL_BLOCK, n)) for s in range(0, n, FFN_COL_BLOCK)]


def ffn_fwd(x, wg, wu, wd, g, b, *, name):
    t, d = x.shape
    nf, _, tf = wg.shape
    tm = _tile(t, FFN_ROWS, SUBLANES)

    def body(x_ref, wg_ref, wu_ref, wd_ref, g_ref, b_ref, y_ref, z_ref, yb_ref, acc_ref):
        f = pl.program_id(1)
        xb = x_ref[...].astype(bf16)
        part, pending = None, None
        for cols in _lane_blocks(tf):
            gate_up = (_bdot(xb, wg_ref[:, cols], NN), _bdot(xb, wu_ref[:, cols], NN), cols)
            if pending is not None:
                down = _bdot(_silu(pending[0]) * pending[1], wd_ref[pending[2], :], NN)
                part = down if part is None else part + down
            pending = gate_up
        down = _bdot(_silu(pending[0]) * pending[1], wd_ref[pending[2], :], NN)
        part = down if part is None else part + down

        @pl.when(f == 0)
        def _():
            acc_ref[...] = part

        @pl.when(f > 0)
        def _():
            acc_ref[...] += part

        @pl.when(f == nf - 1)
        def _():
            z = DN_ALPHA * x_ref[...] + 0.5 * acc_ref[...]
            z_ref[...] = z
            y = _layer_norm(z, g_ref[...], b_ref[...])
            y_ref[...] = y
            yb_ref[...] = y.astype(bf16)

    row = pl.BlockSpec((tm, d), lambda i, j: (i, 0))
    vec = pl.BlockSpec((1, d), lambda i, j: (0, 0))
    wcol = pl.BlockSpec((None, d, tf), lambda i, j: (j, 0, 0))
    wrow = pl.BlockSpec((None, tf, d), lambda i, j: (j, 0, 0))
    return pl.pallas_call(
        body, grid=(t // tm, nf),
        in_specs=[row, wcol, wcol, wrow, vec, vec],
        out_specs=[row, row, row],
        out_shape=[jax.ShapeDtypeStruct((t, d), f32)] * 2 + [jax.ShapeDtypeStruct((t, d), bf16)],
        scratch_shapes=[pltpu.VMEM((tm, d), f32)],
        compiler_params=_params(("parallel", "arbitrary")), name=name,
    )(x, wg, wu, wd, g, b)


def ffn_bwd_weights(xb, dzb, wg, wu, wd, *, name):
    t, d = xb.shape
    nf, _, tf = wg.shape
    tm = _tile(t, FFN_ROWS, SUBLANES)
    nt = t // tm

    def body(x_ref, dz_ref, wg_ref, wu_ref, wd_ref, dgate_ref, dup_ref, owg_ref, owu_ref, owd_ref,
             dwg_ref, dwu_ref, dwd_ref):
        x = x_ref[...]
        dzh = dz_ref[...] * 0.5

        def first_half(cols):
            return _bdot(x, wg_ref[:, cols], NN), _bdot(x, wu_ref[:, cols], NN), _bdot(dzh, wd_ref[cols, :], NT), cols

        def second_half(gate, up, dh, cols):
            sg = jax.nn.sigmoid(gate)
            s = gate * sg
            dup = (dh * s).astype(bf16)
            dgate = (dh * up * (sg * (1.0 + gate * (1.0 - sg)))).astype(bf16)
            dgate_ref[:, cols] = dgate
            dup_ref[:, cols] = dup
            return _bdot(x, dgate, TN), _bdot(x, dup, TN), _bdot(s * up, dzh, TN), cols

        parts, pending = [], None
        for cols in _lane_blocks(tf):
            nxt = first_half(cols)
            if pending is not None:
                parts.append(second_half(*pending))
            pending = nxt
        parts.append(second_half(*pending))

        @pl.when(pl.program_id(1) == 0)
        def _():
            for pwg, pwu, pwd, cols in parts:
                dwg_ref[:, cols] = pwg
                dwu_ref[:, cols] = pwu
                dwd_ref[cols, :] = pwd

        @pl.when(pl.program_id(1) > 0)
        def _():
            for pwg, pwu, pwd, cols in parts:
                dwg_ref[:, cols] += pwg
                dwu_ref[:, cols] += pwu
                dwd_ref[cols, :] += pwd

        @pl.when(pl.program_id(1) == nt - 1)
        def _():
            owg_ref[...] = dwg_ref[...].astype(bf16)
            owu_ref[...] = dwu_ref[...].astype(bf16)
            owd_ref[...] = dwd_ref[...].astype(bf16)

    row = pl.BlockSpec((tm, d), lambda j, i: (i, 0))
    wcol = pl.BlockSpec((None, d, tf), lambda j, i: (j, 0, 0))
    wrow = pl.BlockSpec((None, tf, d), lambda j, i: (j, 0, 0))
    act = pl.BlockSpec((None, tm, tf), lambda j, i: (j, i, 0))
    return pl.pallas_call(
        body, grid=(nf, nt), in_specs=[row, row, wcol, wcol, wrow], out_specs=[act, act, wcol, wcol, wrow],
        out_shape=[jax.ShapeDtypeStruct((nf, t, tf), bf16), jax.ShapeDtypeStruct((nf, t, tf), bf16),
                   jax.ShapeDtypeStruct((nf, d, tf), bf16), jax.ShapeDtypeStruct((nf, d, tf), bf16),
                   jax.ShapeDtypeStruct((nf, tf, d), bf16)],
        scratch_shapes=[pltpu.VMEM((d, tf), f32), pltpu.VMEM((d, tf), f32), pltpu.VMEM((tf, d), f32)],
        compiler_params=_params(("parallel", "arbitrary")), name=name,
    )(xb, dzb, wg, wu, wd)


def ffn_bwd_input(dgate, dup, wg, wu, dz, *, name):
    nf, t, tf = dgate.shape
    d = wg.shape[1]
    tm = _tile(t, FFN_ROWS // 2, SUBLANES)

    def body(dg_ref, du_ref, wg_ref, wu_ref, dz_ref, dx_ref):
        acc = DN_ALPHA * dz_ref[...]
        for j in range(nf):
            acc = acc + _bdot(dg_ref[j], wg_ref[j], NT) + _bdot(du_ref[j], wu_ref[j], NT)
        dx_ref[...] = acc

    act = pl.BlockSpec((nf, tm, tf), lambda i: (0, i, 0))
    wsp = pl.BlockSpec((nf, d, tf), lambda i: (0, 0, 0))
    row = pl.BlockSpec((tm, d), lambda i: (i, 0))
    return pl.pallas_call(
        body, grid=(t // tm,), in_specs=[act, act, wsp, wsp, row], out_specs=row,
        out_shape=jax.ShapeDtypeStruct((t, d), f32),
        compiler_params=_params(("parallel",)), name=name,
    )(dgate, dup, wg, wu, dz)


def ple_fwd(x, p, wg, bg, wp, *, name):
    t, d = x.shape
    dp = p.shape[1]
    tm = _tile(t, 512, 2 * SUBLANES)

    def body(x_ref, p_ref, wg_ref, bg_ref, wp_ref, o_ref, ob_ref):
        x_ = x_ref[...]
        gate = jax.nn.sigmoid(_bdot(x_, wg_ref[...], NN) + bg_ref[...])
        out = x_ + gate * _bdot(p_ref[...], wp_ref[...], NN)
        o_ref[...] = out
        ob_ref[...] = out.astype(bf16)

    row = pl.BlockSpec((tm, d), lambda i: (i, 0))
    return pl.pallas_call(
        body, grid=(t // tm,),
        in_specs=[row, pl.BlockSpec((tm, dp), lambda i: (i, 0)), pl.BlockSpec((d, d), lambda i: (0, 0)),
                  pl.BlockSpec((1, d), lambda i: (0, 0)), pl.BlockSpec((dp, d), lambda i: (0, 0))],
        out_specs=[row, row], out_shape=[jax.ShapeDtypeStruct((t, d), f32), jax.ShapeDtypeStruct((t, d), bf16)],
        compiler_params=_params(("parallel",)), name=name,
    )(x, p, wg, bg, wp)


def ple_bwd(x, p, dy, wg, wgt, bg, wp, *, name):
    t, d = x.shape
    dp = p.shape[1]
    tm = _tile(t, 512, SUBLANES)

    def body(x_ref, p_ref, dy_ref, wg_ref, wgt_ref, bg_ref, wp_ref, dx_ref, dwg_ref, dbg_ref, dwp_ref):
        x_ = x_ref[...]
        dy_ = dy_ref[...]
        s = jax.nn.sigmoid(_bdot(x_, wg_ref[...], NN) + bg_ref[...])
        e = _bdot(p_ref[...], wp_ref[...], NN)
        da = dy_ * e * s * (1.0 - s)
        de = dy_ * s
        dx_ref[...] = dy_ + _bdot(da, wgt_ref[...], NN)
        pwg = _bdot(x_, da, TN)
        pbg = jnp.sum(da, 0, keepdims=True)
        pwp = _bdot(p_ref[...], de, TN)

        @pl.when(pl.program_id(0) == 0)
        def _():
            dwg_ref[...] = pwg
            dbg_ref[...] = pbg
            dwp_ref[...] = pwp

        @pl.when(pl.program_id(0) > 0)
        def _():
            dwg_ref[...] += pwg
            dbg_ref[...] += pbg
            dwp_ref[...] += pwp

    row = pl.BlockSpec((tm, d), lambda i: (i, 0))
    full = lambda shape: pl.BlockSpec(shape, lambda i: (0, 0))
    return pl.pallas_call(
        body, grid=(t // tm,),
        in_specs=[row, pl.BlockSpec((tm, dp), lambda i: (i, 0)), row, full((d, d)), full((d, d)), full((1, d)),
                  full((dp, d))],
        out_specs=[row, full((d, d)), full((1, d)), full((dp, d))],
        out_shape=[jax.ShapeDtypeStruct((t, d), f32), jax.ShapeDtypeStruct((d, d), f32),
                   jax.ShapeDtypeStruct((1, d), f32), jax.ShapeDtypeStruct((dp, d), f32)],
        compiler_params=_params(("arbitrary",)), name=name,
    )(x, p, dy, wg, wgt, bg, wp)


def _conv_taps(xpad_ref, w_ref, s):
    acc = w_ref[0:1, :] * xpad_ref[SUBLANES - 3:SUBLANES - 3 + s, :]
    for j in range(1, 4):
        acc = acc + w_ref[j:j + 1, :] * xpad_ref[SUBLANES - 3 + j:SUBLANES - 3 + j + s, :]
    return acc


def conv_fwd(x, w, bias, act, nb, *, name):
    t, c = x.shape
    s = t // nb
    cw = GROUP_W

    def body(x_ref, w_ref, b_ref, y_ref, xpad):
        xpad[0:SUBLANES, :] = jnp.zeros((SUBLANES, cw), f32)
        xpad[SUBLANES:, :] = x_ref[...]
        acc = _conv_taps(xpad, w_ref, s) + b_ref[...]
        y_ref[...] = _silu(acc) if act else acc

    slab = pl.BlockSpec((s, cw), lambda b, g: (b, g))
    return pl.pallas_call(
        body, grid=(nb, c // cw),
        in_specs=[slab, pl.BlockSpec((4, cw), lambda b, g: (0, g)), pl.BlockSpec((1, cw), lambda b, g: (0, g))],
        out_specs=slab, out_shape=jax.ShapeDtypeStruct((t, c), f32),
        scratch_shapes=[pltpu.VMEM((s + SUBLANES, cw), f32)],
        compiler_params=_params(("parallel", "parallel")), name=name,
    )(x, w, bias)


def conv_bwd(x, w, bias, dy, act, nb, *, name):
    t, c = x.shape
    s = t // nb
    cw = GROUP_W

    def body(x_ref, w_ref, b_ref, dy_ref, dx_ref, dw_ref, db_ref, xpad, dpad):
        xpad[0:SUBLANES, :] = jnp.zeros((SUBLANES, cw), f32)
        xpad[SUBLANES:, :] = x_ref[...]
        dacc = dy_ref[...]
        if act:
            acc = _conv_taps(xpad, w_ref, s) + b_ref[...]
            sg = jax.nn.sigmoid(acc)
            dacc = dacc * (sg * (1.0 + acc * (1.0 - sg)))
        dpad[0:s, :] = dacc
        dpad[s:, :] = jnp.zeros((SUBLANES, cw), f32)
        dx = w_ref[0:1, :] * dpad[3:3 + s, :]
        for j in range(1, 4):
            dx = dx + w_ref[j:j + 1, :] * dpad[3 - j:3 - j + s, :]
        dx_ref[...] = dx
        first = pl.program_id(1) == 0
        for j in range(4):
            pw = jnp.sum(dacc * xpad[SUBLANES - 3 + j:SUBLANES - 3 + j + s, :], 0, keepdims=True)

            @pl.when(first)
            def _():
                dw_ref[j:j + 1, :] = pw

            @pl.when(jnp.logical_not(first))
            def _():
                dw_ref[j:j + 1, :] += pw

        pb = jnp.sum(dacc, 0, keepdims=True)

        @pl.when(first)
        def _():
            db_ref[...] = pb

        @pl.when(jnp.logical_not(first))
        def _():
            db_ref[...] += pb

    slab = pl.BlockSpec((s, cw), lambda g, b: (b, g))
    wsp = pl.BlockSpec((4, cw), lambda g, b: (0, g))
    bsp = pl.BlockSpec((1, cw), lambda g, b: (0, g))
    return pl.pallas_call(
        body, grid=(c // cw, nb), in_specs=[slab, wsp, bsp, slab], out_specs=[slab, wsp, bsp],
        out_shape=[jax.ShapeDtypeStruct((t, c), f32), jax.ShapeDtypeStruct((4, c), f32),
                   jax.ShapeDtypeStruct((1, c), f32)],
        scratch_shapes=[pltpu.VMEM((s + SUBLANES, cw), f32), pltpu.VMEM((s + SUBLANES, cw), f32)],
        compiler_params=_params(("parallel", "arbitrary")), name=name,
    )(x, w, bias, dy)


def _each(f, *lists):
    return [f(*a) for a in zip(*lists)]


def _attn_heads(qs, kbs, vbs, sinks, valids, dist, dots):
    nn, nt = dots[:2]
    items = range(len(qs))
    kv = [(i // A_HEADS) * A_KV_HEADS + (i % A_HEADS) // A_GROUP for i in items]
    scs = [nt(qs[i], kbs[kv[i]]) for i in items]
    prs = []
    for i in items:
        h = i % A_HEADS
        sc = scs[i] * (A_HEAD_DIM ** -0.5) - 2.0 ** -(h + 1) * dist
        sc = jnp.where(valids[i // A_HEADS], sc, NEG)
        m = lax.stop_gradient(jnp.maximum(jnp.max(sc, -1, keepdims=True), sinks[h]))
        pr = jnp.exp(sc - m)
        den = jnp.sum(pr, -1, keepdims=True) + jnp.exp(sinks[h] - m)
        prs.append(pr / den)
    return [nn(prs[i], vbs[kv[i]]) for i in items]


A_Q_ROWS = 2 * CHUNK
A_STEPS_PER_TRIP = 4


def _attn_steps(s):
    return A_STEPS_PER_TRIP if s % (A_Q_ROWS * A_STEPS_PER_TRIP) == 0 else 1


def _attn_band_consts(r0):
    band = A_WINDOW + A_Q_ROWS
    qi = lax.broadcasted_iota(jnp.int32, (A_Q_ROWS, band), 0)
    kj = lax.broadcasted_iota(jnp.int32, (A_Q_ROWS, band), 1)
    dist = jnp.abs(qi + A_WINDOW - kj).astype(f32)
    qc, kc = qi // CHUNK, kj // CHUNK
    valid = ((kj + r0) >= A_WINDOW) & (kc >= qc) & (kc <= qc + A_WINDOW // CHUNK)
    return dist, valid


def attn_fwd(qkv, sinks, nb, *, name):
    t = qkv.shape[0]
    s = t // nb
    band = A_WINDOW + A_Q_ROWS
    hd = A_HEAD_DIM

    def body(qkv_ref, sink_ref, o_ref, kvpad):
        kvpad[0:A_WINDOW, :] = jnp.zeros((A_WINDOW, 2 * A_KV_WIDTH), f32)
        kvpad[A_WINDOW:, :] = qkv_ref[:, A_WIDTH:]

        def trip(n, carry):
            r0s = [pl.multiple_of((n * steps + j) * A_Q_ROWS, A_Q_ROWS) for j in range(steps)]
            consts = [_attn_band_consts(r0) for r0 in r0s]
            kbs = [kvpad[pl.ds(r0, band), kvh * hd:(kvh + 1) * hd] for r0 in r0s for kvh in range(A_KV_HEADS)]
            vbs = [kvpad[pl.ds(r0, band), A_KV_WIDTH + kvh * hd:A_KV_WIDTH + (kvh + 1) * hd]
                   for r0 in r0s for kvh in range(A_KV_HEADS)]
            qs = [qkv_ref[pl.ds(r0, A_Q_ROWS), h * hd:(h + 1) * hd] for r0 in r0s for h in range(A_HEADS)]
            outs = _attn_heads(qs, kbs, vbs, [sink_ref[:, h:h + 1] for h in range(A_HEADS)], [c_[1] for c_ in consts],
                               consts[0][0], RAW_DOTS)
            for j, r0 in enumerate(r0s):
                for h in range(A_HEADS):
                    o_ref[pl.ds(r0, A_Q_ROWS), h * hd:(h + 1) * hd] = outs[j * A_HEADS + h]
            return carry

        steps = _attn_steps(s)
        lax.fori_loop(0, s // (A_Q_ROWS * steps), trip, 0)

    return pl.pallas_call(
        body, grid=(nb,),
        in_specs=[pl.BlockSpec((s, A_WIDTH + 2 * A_KV_WIDTH), lambda b: (b, 0)),
                  pl.BlockSpec((1, A_HEADS), lambda b: (0, 0))],
        out_specs=pl.BlockSpec((s, A_WIDTH), lambda b: (b, 0)),
        out_shape=jax.ShapeDtypeStruct((t, A_WIDTH), f32),
        scratch_shapes=[pltpu.VMEM((s + A_WINDOW, 2 * A_KV_WIDTH), f32)],
        compiler_params=_params(("parallel",)), name=name,
    )(qkv, sinks)


def attn_bwd(qkv, sinks, do, nb, *, name):
    t = qkv.shape[0]
    s = t // nb
    band = A_WINDOW + A_Q_ROWS
    hd = A_HEAD_DIM
    kvw = 2 * A_KV_WIDTH

    def body(qkv_ref, sink_ref, do_ref, dqkv_ref, dsink_ref, kvpad, dkvpad):
        kvpad[0:A_WINDOW, :] = jnp.zeros((A_WINDOW, kvw), f32)
        kvpad[A_WINDOW:, :] = qkv_ref[:, A_WIDTH:]
        dkvpad[...] = jnp.zeros((s + A_WINDOW, kvw), f32)

        def trip(n, dsinks):
            r0s = [pl.multiple_of((n * steps + j) * A_Q_ROWS, A_Q_ROWS) for j in range(steps)]
            consts = [_attn_band_consts(r0) for r0 in r0s]
            ksl = [slice(kvh * hd, (kvh + 1) * hd) for kvh in range(A_KV_HEADS)]
            vsl = [slice(A_KV_WIDTH + kvh * hd, A_KV_WIDTH + (kvh + 1) * hd) for kvh in range(A_KV_HEADS)]
            kbs = [kvpad[pl.ds(r0, band), sl] for r0 in r0s for sl in ksl]
            vbs = [kvpad[pl.ds(r0, band), sl] for r0 in r0s for sl in vsl]
            qs = [qkv_ref[pl.ds(r0, A_Q_ROWS), h * hd:(h + 1) * hd] for r0 in r0s for h in range(A_HEADS)]
            dos = [do_ref[pl.ds(r0, A_Q_ROWS), h * hd:(h + 1) * hd] for r0 in r0s for h in range(A_HEADS)]
            fn = functools.partial(_attn_heads, valids=[c_[1] for c_ in consts], dist=consts[0][0], dots=VJP_DOTS)
            _, vjp = jax.vjp(fn, qs, kbs, vbs, [sink_ref[:, h:h + 1] for h in range(A_HEADS)])
            dqs, dks, dvs, dss = vjp(dos)
            for j, r0 in enumerate(r0s):
                for h in range(A_HEADS):
                    dqkv_ref[pl.ds(r0, A_Q_ROWS), h * hd:(h + 1) * hd] = dqs[j * A_HEADS + h]
            for j, r0 in enumerate(r0s):
                for kvh in range(A_KV_HEADS):
                    dkvpad[pl.ds(r0, band), ksl[kvh]] += dks[j * A_KV_HEADS + kvh]
                    dkvpad[pl.ds(r0, band), vsl[kvh]] += dvs[j * A_KV_HEADS + kvh]
            return tuple(dsinks[h] + dss[h] for h in range(A_HEADS))

        steps = _attn_steps(s)
        dsinks = lax.fori_loop(0, s // (A_Q_ROWS * steps), trip, tuple(jnp.zeros((1, 1), f32) for _ in range(A_HEADS)))
        dqkv_ref[:, A_WIDTH:] = dkvpad[A_WINDOW:, :]
        first = pl.program_id(0) == 0
        for h in range(A_HEADS):
            @pl.when(first)
            def _():
                dsink_ref[:, h:h + 1] = dsinks[h]

            @pl.when(jnp.logical_not(first))
            def _():
                dsink_ref[:, h:h + 1] += dsinks[h]

    wq = A_WIDTH + kvw
    return pl.pallas_call(
        body, grid=(nb,),
        in_specs=[pl.BlockSpec((s, wq), lambda b: (b, 0)), pl.BlockSpec((1, A_HEADS), lambda b: (0, 0)),
                  pl.BlockSpec((s, A_WIDTH), lambda b: (b, 0))],
        out_specs=[pl.BlockSpec((s, wq), lambda b: (b, 0)), pl.BlockSpec((1, A_HEADS), lambda b: (0, 0))],
        out_shape=[jax.ShapeDtypeStruct((t, wq), f32), jax.ShapeDtypeStruct((1, A_HEADS), f32)],
        scratch_shapes=[pltpu.VMEM((s + A_WINDOW, kvw), f32), pltpu.VMEM((s + A_WINDOW, kvw), f32)],
        compiler_params=_params(("arbitrary",)), name=name,
    )(qkv, sinks, do)


def _rg_gates(xc, wa, wx, ba, bx, lam, nn):
    r = jax.nn.sigmoid(nn(xc, wa) + ba)
    i = jax.nn.sigmoid(nn(xc, wx) + bx)
    log_a = -RG_C * r * jax.nn.softplus(-lam)
    a = jnp.exp(log_a)
    mult = jnp.sqrt(-jnp.tanh(log_a) * (jnp.exp(2.0 * log_a) + 1.0))
    return a, mult * (i * xc)


def _linear_scan(a, u, reverse):
    s = a.shape[0]
    t = lax.broadcasted_iota(jnp.int32, a.shape, 0)
    d = 1
    while d < s:
        if reverse:
            keep = t < s - d
            shift = s - d
        else:
            keep = t >= d
            shift = d
        us = jnp.where(keep, pltpu.roll(u, shift, 0), 0.0)
        as_ = jnp.where(keep, pltpu.roll(a, shift, 0), 1.0)
        u = u + a * us
        a = a * as_
        d *= 2
    return u


def rglru_fwd(xc, bg, wa, wx, ba, bx, lam, nb, *, name):
    t, c = xc.shape
    s = t // nb
    cw = GROUP_W

    def body(xc_ref, bg_ref, wa_ref, wx_ref, ba_ref, bx_ref, lam_ref, y_ref, h_ref):
        a, u = _rg_gates(xc_ref[...], wa_ref[...], wx_ref[...], ba_ref[...], bx_ref[...], lam_ref[...], RAW_DOTS[0])
        h = _linear_scan(a, u, False)
        h_ref[...] = h
        y_ref[...] = h * jax.nn.gelu(bg_ref[...])

    slab = pl.BlockSpec((s, cw), lambda b, g: (b, g))
    wsp = pl.BlockSpec((None, cw, cw), lambda b, g: (g, 0, 0))
    vec = pl.BlockSpec((1, cw), lambda b, g: (0, g))
    return pl.pallas_call(
        body, grid=(nb, c // cw), in_specs=[slab, slab, wsp, wsp, vec, vec, vec], out_specs=[slab, slab],
        out_shape=[jax.ShapeDtypeStruct((t, c), f32)] * 2,
        compiler_params=_params(("parallel", "parallel")), name=name,
    )(xc, bg, wa, wx, ba, bx, lam)


def rglru_bwd(xc, bg, h, dy, wa, wx, ba, bx, lam, nb, *, name):
    t, c = xc.shape
    s = t // nb
    cw = GROUP_W

    def body(xc_ref, bg_ref, h_ref, dy_ref, wa_ref, wx_ref, ba_ref, bx_ref, lam_ref,
             dxc_ref, dbg_ref, dwa_ref, dwx_ref, dba_ref, dbx_ref, dlam_ref):
        h = h_ref[...]
        dy_ = dy_ref[...]
        gel, gel_vjp = jax.vjp(jax.nn.gelu, bg_ref[...])
        dbg_ref[...] = gel_vjp(dy_ * h)[0]
        dh = dy_ * gel
        gates = functools.partial(_rg_gates, nn=_bnn)
        (a, _), gates_vjp = jax.vjp(gates, xc_ref[...], wa_ref[...], wx_ref[...], ba_ref[...], bx_ref[...],
                                    lam_ref[...])
        ti = lax.broadcasted_iota(jnp.int32, a.shape, 0)
        a_next = jnp.where(ti < s - 1, pltpu.roll(a, s - 1, 0), 0.0)
        lam_t = _linear_scan(a_next, dh, True)
        h_prev = jnp.where(ti >= 1, pltpu.roll(h, 1, 0), 0.0)
        dxc, dwa, dwx, dba, dbx, dlam = gates_vjp((lam_t * h_prev, lam_t))
        dxc_ref[...] = dxc
        first = pl.program_id(1) == 0

        @pl.when(first)
        def _():
            dwa_ref[...] = dwa
            dwx_ref[...] = dwx
            dba_ref[...] = dba
            dbx_ref[...] = dbx
            dlam_ref[...] = dlam

        @pl.when(jnp.logical_not(first))
        def _():
            dwa_ref[...] += dwa
            dwx_ref[...] += dwx
            dba_ref[...] += dba
            dbx_ref[...] += dbx
            dlam_ref[...] += dlam

    slab = pl.BlockSpec((s, cw), lambda g, b: (b, g))
    wsp = pl.BlockSpec((None, cw, cw), lambda g, b: (g, 0, 0))
    vec = pl.BlockSpec((1, cw), lambda g, b: (0, g))
    ng = c // cw
    return pl.pallas_call(
        body, grid=(ng, nb), in_specs=[slab, slab, slab, slab, wsp, wsp, vec, vec, vec],
        out_specs=[slab, slab, wsp, wsp, vec, vec, vec],
        out_shape=[jax.ShapeDtypeStruct((t, c), f32), jax.ShapeDtypeStruct((t, c), f32),
                   jax.ShapeDtypeStruct((ng, cw, cw), f32), jax.ShapeDtypeStruct((ng, cw, cw), f32),
                   jax.ShapeDtypeStruct((1, c), f32), jax.ShapeDtypeStruct((1, c), f32),
                   jax.ShapeDtypeStruct((1, c), f32)],
        compiler_params=_params(("parallel", "arbitrary")), name=name,
    )(xc, bg, h, dy, wa, wx, ba, bx, lam)


def _gdn_chunks_prep(qs, ks, vs, bls, als, a_log, dt_b, dots):
    nn, nt, csum = dots[0], dots[1], dots[3]
    hd = C_HEAD_DIM
    ri = lax.broadcasted_iota(jnp.int32, (CHUNK, CHUNK), 0)
    ci = lax.broadcasted_iota(jnp.int32, (CHUNK, CHUNK), 1)
    tril = ri >= ci
    strict = ri > ci
    eye = (ri == ci).astype(f32)
    qn = [q * lax.rsqrt(jnp.sum(q * q, -1, keepdims=True) + NORM_EPS) * (hd ** -0.5) for q in qs]
    kn = [k * lax.rsqrt(jnp.sum(k * k, -1, keepdims=True) + NORM_EPS) for k in ks]
    beta = [jax.nn.sigmoid(bl) for bl in bls]
    g = [-jnp.exp(a_log) * jax.nn.softplus(al + dt_b) for al in als]
    gc_sq = [csum(jnp.broadcast_to(g_, (CHUNK, CHUNK))) for g_ in g]
    gc = [csum(jnp.broadcast_to(g_, (CHUNK, hd))) for g_ in g]
    decay = [jnp.where(tril, jnp.exp(jnp.where(tril, s - s.T, 0.0)), 0.0) for s in gc_sq]
    kb = _each(jnp.multiply, kn, beta)
    kk = _each(nt, kb, kn)
    pw = [-jnp.where(strict, a * d, 0.0) for a, d in zip(kk, decay)]
    inv = [eye + p_ for p_ in pw]
    for _ in range(5):
        pw = _each(nn, pw, pw)
        inv = _each(jnp.add, inv, _each(nn, inv, pw))
    egc = [jnp.exp(c_) for c_ in gc]
    u = _each(nn, inv, _each(jnp.multiply, vs, beta))
    w = _each(nn, inv, _each(jnp.multiply, kb, egc))
    attn = _each(jnp.multiply, _each(nt, qn, kn), decay)
    g_last = [jnp.sum(jnp.broadcast_to(g_, (CHUNK, hd)), 0, keepdims=True) for g_ in g]
    qg = _each(jnp.multiply, qn, egc)
    kdec = [k_ * jnp.exp(gl_ - c_) for k_, gl_, c_ in zip(kn, g_last, gc)]
    return [(qg[i], kdec[i], w[i], u[i], attn[i], jnp.exp(g_last[i])) for i in range(len(qs))]


def _gdn_heads_step(states, qgs, kdecs, ws, us, attns, gls, zs, ng, dots):
    nn, tn = dots[0], dots[2]
    v_new = _each(jnp.subtract, us, _each(nn, ws, states))
    o = _each(jnp.add, _each(nn, qgs, states), _each(nn, attns, v_new))
    new = [s * gl for s, gl in zip(states, gls)]
    new = _each(jnp.add, new, _each(tn, kdecs, v_new))
    y = [o_ * lax.rsqrt(jnp.mean(o_ * o_, -1, keepdims=True) + NORM_EPS) * ng * _silu(z) for o_, z in zip(o, zs)]
    return y, new


def _loop_unrolled(n, unroll, load, compute, store, init):
    u = unroll if n % unroll == 0 else 1

    def trip(i, carry):
        idx = [i * u + j for j in range(u)]
        loaded = [load(k) for k in idx]
        results = compute(loaded)
        for k, r in zip(idx, results):
            carry = store(k, r, carry)
        return carry

    return lax.fori_loop(0, n // u, trip, init)


def _pick_lane(x, lane):
    li = lax.broadcasted_iota(jnp.int32, x.shape, 1)
    return jnp.sum(jnp.where(li == lane, x, 0.0), 1, keepdims=True)


def _put_lane(col, lane, width):
    li = lax.broadcasted_iota(jnp.int32, (col.shape[0], width), 1)
    return jnp.where(li == lane, col, 0.0)


def _gdn_specs(s, nc):
    hd = C_HEAD_DIM
    head = lambda off: pl.BlockSpec((s, hd), lambda b, h, off=off: (b, off + h))
    attn = pl.BlockSpec((None, s, CHUNK), lambda b, h: (h, b, 0))
    gl = pl.BlockSpec((None, nc * SUBLANES, hd), lambda b, h: (h, b, 0))
    ba = pl.BlockSpec((s, LANES), lambda b, h: (b, 0))
    sc8 = pl.BlockSpec((1, C_HEADS), lambda b, h: (0, 0))
    return head, attn, gl, ba, sc8


def gdn_prep_fwd(qkv, ba, a_log, dt_b, nb, *, name):
    t = qkv.shape[0]
    s = t // nb
    nc = s // CHUNK
    hd = C_HEAD_DIM
    head, attn_sp, gl_sp, ba_sp, sc8 = _gdn_specs(s, nc)

    def body(q_ref, k_ref, v_ref, ba_ref, alog_ref, dtb_ref, qg_ref, kd_ref, w_ref, u_ref, at_ref, gl_ref):
        h = pl.program_id(1)
        a_log_h = _pick_lane(alog_ref[...], h)
        dt_b_h = _pick_lane(dtb_ref[...], h)

        def load(n):
            rows = pl.ds(pl.multiple_of(n * CHUNK, CHUNK), CHUNK)
            bav = ba_ref[rows, :]
            return q_ref[rows, :], k_ref[rows, :], v_ref[rows, :], _pick_lane(bav, h), _pick_lane(bav, C_HEADS + h)

        def compute(loaded):
            return _gdn_chunks_prep(*[list(x) for x in zip(*loaded)], a_log_h, dt_b_h, RAW_DOTS)

        def store(n, outs, carry):
            rows = pl.ds(pl.multiple_of(n * CHUNK, CHUNK), CHUNK)
            qg_ref[rows, :] = outs[0].astype(bf16)
            kd_ref[rows, :] = outs[1].astype(bf16)
            w_ref[rows, :] = outs[2].astype(bf16)
            u_ref[rows, :] = outs[3]
            at_ref[rows, :] = outs[4].astype(bf16)
            gl_ref[pl.ds(pl.multiple_of(n * SUBLANES, SUBLANES), SUBLANES), :] = jnp.broadcast_to(outs[5], (SUBLANES, hd))
            return carry

        _loop_unrolled(nc, PREP_FWD_UNROLL, load, compute, store, 0)

    big = jax.ShapeDtypeStruct((t, C_WIDTH), f32)
    bigb = jax.ShapeDtypeStruct((t, C_WIDTH), bf16)
    return pl.pallas_call(
        body, grid=(nb, C_HEADS),
        in_specs=[head(0), head(C_HEADS), head(2 * C_HEADS), ba_sp, sc8, sc8],
        out_specs=[head(0)] * 4 + [attn_sp, gl_sp],
        out_shape=[bigb, bigb, bigb, big, jax.ShapeDtypeStruct((C_HEADS, t, CHUNK), bf16),
                               jax.ShapeDtypeStruct((C_HEADS, nb * nc * SUBLANES, hd), f32)],
        compiler_params=_params(("parallel", "parallel")), name=name,
    )(qkv, qkv, qkv, ba, a_log, dt_b)


def gdn_prep_bwd(qkv, ba, a_log, dt_b, cts, nb, *, name):
    t = qkv.shape[0]
    s = t // nb
    nc = s // CHUNK
    hd = C_HEAD_DIM
    head, attn_sp, gl_sp, ba_sp, sc8 = _gdn_specs(s, nc)

    def body(q_ref, k_ref, v_ref, ba_ref, alog_ref, dtb_ref, cqg, ckd, cw_, cu, cat, cgl,
             dq_ref, dk_ref, dv_ref, dba_ref, dalog_ref, ddtb_ref):
        b = pl.program_id(0)
        h = pl.program_id(1)
        a_log_h = _pick_lane(alog_ref[...], h)
        dt_b_h = _pick_lane(dtb_ref[...], h)
        prep = functools.partial(_gdn_chunks_prep, dots=VJP_DOTS)

        @pl.when(h == 0)
        def _():
            dba_ref[...] = jnp.zeros((s, LANES), f32)

        def load(n):
            rows = pl.ds(pl.multiple_of(n * CHUNK, CHUNK), CHUNK)
            bav = ba_ref[rows, :]
            cgl_n = cgl[pl.ds(pl.multiple_of(n * SUBLANES, SUBLANES), SUBLANES), :][0:1, :]
            primals = (q_ref[rows, :], k_ref[rows, :], v_ref[rows, :], _pick_lane(bav, h), _pick_lane(bav, C_HEADS + h))
            return primals, (cqg[rows, :], ckd[rows, :], cw_[rows, :], cu[rows, :], cat[rows, :], cgl_n), dba_ref[rows, :]

        def compute(loaded):
            primals = [list(x) for x in zip(*[item[0] for item in loaded])]
            _, vjp = jax.vjp(prep, *primals, a_log_h, dt_b_h)
            dqs, dks, dvs, dbls, dals, dalog, ddtb = vjp([item[1] for item in loaded])
            zero = jnp.zeros((1, 1), f32)
            return [((dqs[i], dks[i], dvs[i], dbls[i], dals[i], dalog if i == 0 else zero, ddtb if i == 0 else zero),
                     loaded[i][2]) for i in range(len(loaded))]

        def store(n, res, carry):
            (dq, dk, dv, dbl, dal, dalog_n, ddtb_n), dba_old = res
            rows = pl.ds(pl.multiple_of(n * CHUNK, CHUNK), CHUNK)
            dq_ref[rows, :] = dq
            dk_ref[rows, :] = dk
            dv_ref[rows, :] = dv
            dba_ref[rows, :] = dba_old + _put_lane(dbl, h, LANES) + _put_lane(dal, C_HEADS + h, LANES)
            return carry[0] + dalog_n, carry[1] + ddtb_n

        da_log, ddt_b = _loop_unrolled(nc, PREP_BWD_UNROLL, load, compute, store,
                                       (jnp.zeros((1, 1), f32), jnp.zeros((1, 1), f32)))
        first = jnp.logical_and(b == 0, h == 0)

        @pl.when(first)
        def _():
            dalog_ref[...] = _put_lane(da_log, h, LANES)
            ddtb_ref[...] = _put_lane(ddt_b, h, LANES)

        @pl.when(jnp.logical_not(first))
        def _():
            dalog_ref[...] += _put_lane(da_log, h, LANES)
            ddtb_ref[...] += _put_lane(ddt_b, h, LANES)

    big = jax.ShapeDtypeStruct((t, C_WIDTH), f32)
    vec = pl.BlockSpec((1, LANES), lambda b, h: (0, 0))
    return pl.pallas_call(
        body, grid=(nb, C_HEADS),
        in_specs=[head(0), head(C_HEADS), head(2 * C_HEADS), ba_sp, sc8, sc8] + [head(0)] * 4 + [attn_sp, gl_sp],
        out_specs=[head(0)] * 3 + [ba_sp, vec, vec],
        out_shape=[big] * 3 + [jax.ShapeDtypeStruct((t, LANES), f32), jax.ShapeDtypeStruct((1, LANES), f32),
                               jax.ShapeDtypeStruct((1, LANES), f32)],
        compiler_params=_params(("arbitrary", "arbitrary")), name=name,
    )(qkv, qkv, qkv, ba, a_log, dt_b, *cts)


def _gdn_rec_specs(sb, nsb, hp, reverse):
    hd = C_HEAD_DIM
    ncb = sb // CHUNK
    blk = (lambda b, k: b * nsb + (nsb - 1 - k)) if reverse else (lambda b, k: b * nsb + k)
    wide = pl.BlockSpec((sb, hp * hd), lambda b, j, k: (blk(b, k), j))
    attn = pl.BlockSpec((hp, sb, CHUNK), lambda b, j, k: (j, blk(b, k), 0))
    gl = pl.BlockSpec((hp, ncb * SUBLANES, hd), lambda b, j, k: (j, blk(b, k), 0))
    ng = pl.BlockSpec((1, hd), lambda b, j, k: (0, 0))
    states = pl.BlockSpec((hp, ncb, hd, hd), lambda b, j, k: (j, blk(b, k), 0, 0))
    return wide, attn, gl, ng, states


def gdn_rec_fwd(qg, kdec, w, u, attn, gl, z, ng, nb, *, name):
    t = qg.shape[0]
    s = t // nb
    sb = min(s, GDN_TIME_BLOCK)
    nsb = s // sb
    hd = C_HEAD_DIM
    hp = C_HEADS_PER_STEP
    wide, attn_sp, gl_sp, ng_sp, st_sp = _gdn_rec_specs(sb, nsb, hp, False)

    def body(qg_ref, kd_ref, w_ref, u_ref, at_ref, gl_ref, z_ref, ng_ref, y_ref, st_ref, carry_ref):
        @pl.when(pl.program_id(2) == 0)
        def _():
            carry_ref[...] = jnp.zeros((hp, hd, hd), f32)

        def chunk(n, states):
            for j in range(hp):
                st_ref[j, n] = states[j]
            rows = pl.ds(pl.multiple_of(n * CHUNK, CHUNK), CHUNK)
            grow = pl.ds(pl.multiple_of(n * SUBLANES, SUBLANES), SUBLANES)
            cols = [slice(j * hd, (j + 1) * hd) for j in range(hp)]
            ins = [(qg_ref[rows, c], kd_ref[rows, c], w_ref[rows, c], u_ref[rows, c], at_ref[j, rows, :],
                    gl_ref[j, grow, :][0:1, :], z_ref[rows, c]) for j, c in enumerate(cols)]
            ys, new = _gdn_heads_step(list(states), *[list(x) for x in zip(*ins)], ng_ref[...], RAW_DOTS)
            for j in range(hp):
                y_ref[rows, cols[j]] = ys[j]
            return tuple(new)

        last = lax.fori_loop(0, sb // CHUNK, chunk, tuple(carry_ref[j] for j in range(hp)))
        for j in range(hp):
            carry_ref[j] = last[j]

    return pl.pallas_call(
        body, grid=(nb, C_HEADS // hp, nsb),
        in_specs=[wide] * 4 + [attn_sp, gl_sp, wide, ng_sp], out_specs=[wide, st_sp],
        out_shape=[jax.ShapeDtypeStruct((t, C_WIDTH), f32), jax.ShapeDtypeStruct((C_HEADS, t // CHUNK, hd, hd), f32)],
        scratch_shapes=[pltpu.VMEM((hp, hd, hd), f32)],
        compiler_params=_params(("parallel", "parallel", "arbitrary")), name=name,
    )(qg, kdec, w, u, attn, gl, z, ng)


def gdn_rec_bwd(qg, kdec, w, u, attn, gl, z, ng, states, dy, nb, *, name):
    t = qg.shape[0]
    s = t // nb
    sb = min(s, GDN_TIME_BLOCK)
    nsb = s // sb
    nc = sb // CHUNK
    hd = C_HEAD_DIM
    hp = C_HEADS_PER_STEP
    wide, attn_sp, gl_sp, ng_sp, st_sp = _gdn_rec_specs(sb, nsb, hp, True)

    def body(qg_ref, kd_ref, w_ref, u_ref, at_ref, gl_ref, z_ref, ng_ref, states, dy_ref,
             dqg_ref, dkd_ref, dw_ref, du_ref, dat_ref, dgl_ref, dz_ref, dng_ref, carry_ref):
        step = functools.partial(_gdn_heads_step, dots=VJP_DOTS)

        @pl.when(pl.program_id(2) == 0)
        def _():
            carry_ref[...] = jnp.zeros((hp, hd, hd), f32)

        def operands(n):
            rows = pl.ds(pl.multiple_of(n * CHUNK, CHUNK), CHUNK)
            grow = pl.ds(pl.multiple_of(n * SUBLANES, SUBLANES), SUBLANES)
            cols = [slice(j * hd, (j + 1) * hd) for j in range(hp)]
            return ([qg_ref[rows, c].astype(f32) for c in cols], [kd_ref[rows, c].astype(f32) for c in cols],
                    [w_ref[rows, c].astype(f32) for c in cols], [u_ref[rows, c] for c in cols],
                    [at_ref[j, rows, :].astype(f32) for j in range(hp)],
                    [gl_ref[j, grow, :][0:1, :] for j in range(hp)], [z_ref[rows, c] for c in cols])

        def bwd_chunk(i, carry):
            n = nc - 1 - i
            rows = pl.ds(pl.multiple_of(n * CHUNK, CHUNK), CHUNK)
            grow = pl.ds(pl.multiple_of(n * SUBLANES, SUBLANES), SUBLANES)
            dsts, dng = carry
            dys = [dy_ref[rows, j * hd:(j + 1) * hd] for j in range(hp)]
            _, vjp = jax.vjp(step, [states[j, n] for j in range(hp)], *operands(n), ng_ref[...])
            dst, dqg, dkd, dw, du, dat, dgl, dz, dng_n = vjp((dys, list(dsts)))
            for j in range(hp):
                cols = slice(j * hd, (j + 1) * hd)
                dqg_ref[rows, cols] = dqg[j]
                dkd_ref[rows, cols] = dkd[j]
                dw_ref[rows, cols] = dw[j]
                du_ref[rows, cols] = du[j]
                dat_ref[j, rows, :] = dat[j]
                dgl_ref[j, grow, :] = jnp.broadcast_to(dgl[j], (SUBLANES, hd))
                dz_ref[rows, cols] = dz[j]
            return tuple(dst), dng + dng_n

        dlast, dng = lax.fori_loop(0, nc, bwd_chunk,
                                   (tuple(carry_ref[j] for j in range(hp)), jnp.zeros((1, hd), f32)))
        for j in range(hp):
            carry_ref[j] = dlast[j]
        first = jnp.logical_and(jnp.logical_and(pl.program_id(0) == 0, pl.program_id(1) == 0), pl.program_id(2) == 0)

        @pl.when(first)
        def _():
            dng_ref[...] = dng

        @pl.when(jnp.logical_not(first))
        def _():
            dng_ref[...] += dng

    big = jax.ShapeDtypeStruct((t, C_WIDTH), f32)
    return pl.pallas_call(
        body, grid=(nb, C_HEADS // hp, nsb),
        in_specs=[wide] * 4 + [attn_sp, gl_sp, wide, ng_sp, st_sp, wide],
        out_specs=[wide] * 4 + [attn_sp, gl_sp, wide, ng_sp],
        out_shape=[big] * 4 + [jax.ShapeDtypeStruct(attn.shape, f32), jax.ShapeDtypeStruct(gl.shape, f32), big,
                               jax.ShapeDtypeStruct((1, hd), f32)],
        scratch_shapes=[pltpu.VMEM((hp, hd, hd), f32)],
        compiler_params=_params(("arbitrary", "arbitrary", "arbitrary")), name=name,
    )(qg, kdec, w, u, attn, gl, z, ng, states, dy)


def _blockdiag_slabs(w):
    per = GROUP_W // B_BLOCK
    slabs = jnp.zeros((B_BLOCKS // per, GROUP_W, GROUP_W), w.dtype)
    for h in range(B_BLOCKS):
        o = (h % per) * B_BLOCK
        slabs = slabs.at[h // per, o:o + B_BLOCK, o:o + B_BLOCK].set(w[h])
    return slabs


def _slab_blocks(slabs):
    per = GROUP_W // B_BLOCK
    return jnp.stack([slabs[h // per, (h % per) * B_BLOCK:(h % per + 1) * B_BLOCK,
                            (h % per) * B_BLOCK:(h % per + 1) * B_BLOCK] for h in range(B_BLOCKS)])


def _mixer_ab_fwd(x1, x1b, W, g, b, nb, tag):
    w_in = W["ab_w_in"][0].astype(bf16)
    o1, o2 = A_WIDTH + 2 * A_KV_WIDTH, A_WIDTH + 2 * A_KV_WIDTH + B_WIDTH
    w_qkv, w_bx, w_bg = w_in[:, :o1], w_in[:, o1:o2], w_in[:, o2:]
    pqkv = mm_nn(x1b,w_qkv, name=tag + "_in_qkv")
    pbx = mm_nn(x1b,w_bx, name=tag + "_in_bx")
    pbg = mm_nn(x1b,w_bg, name=tag + "_in_bg")
    ya = attn_fwd(pqkv, W["a_sinks"], nb, name=tag + "_attn_fwd")
    xc = conv_fwd(pbx, W["b_conv_w"][0], W["b_conv_b"], False, nb, name=tag + "_conv_fwd")
    wa_s, wx_s = _blockdiag_slabs(W["b_wa"][0]), _blockdiag_slabs(W["b_wx"][0])
    yb, hh = rglru_fwd(xc, pbg, wa_s, wx_s, W["b_ba"], W["b_bx"], W["b_lam"], nb, name=tag + "_rglru_fwd")
    w_out = W["ab_w_out"][0].astype(bf16)
    x2, z1, x2b = proj_ln([ya, yb], [w_out[:A_WIDTH], w_out[A_WIDTH:]], x1, g, b, name=tag + "_out_ln")
    saved = (pqkv, pbx, pbg, ya, xc, yb, hh, wa_s, wx_s, w_qkv, w_bx, w_bg, w_out)
    return x2, x2b, z1, saved


def _mixer_ab_bwd(x1b, dz1, dz1b, W, saved, nb, tag):
    pqkv, pbx, pbg, ya, xc, yb, hh, wa_s, wx_s, w_qkv, w_bx, w_bg, w_out = saved
    dya = mm_nn(dz1b, w_out[:A_WIDTH].T, name=tag + "_dya")
    dyb = mm_nn(dz1b, w_out[A_WIDTH:].T, name=tag + "_dyb")
    dwo = jnp.concatenate([mm_tn(ya, dz1b, name=tag + "_dwo_a"), mm_tn(yb, dz1b, name=tag + "_dwo_b")], 0)
    dpqkv, dsinks = attn_bwd(pqkv, W["a_sinks"], dya, nb, name=tag + "_attn_bwd")
    dxc, dpbg, dwa_s, dwx_s, dba, dbx, dlam = rglru_bwd(xc, pbg, hh, dyb, wa_s, wx_s, W["b_ba"], W["b_bx"],
                                                       W["b_lam"], nb, name=tag + "_rglru_bwd")
    dpbx, dconv_w, dconv_b = conv_bwd(pbx, W["b_conv_w"][0], W["b_conv_b"], dxc, False, nb, name=tag + "_conv_bwd")
    dw_in = jnp.concatenate([mm_tn(x1b,dpqkv, name=tag + "_dwin_qkv"), mm_tn(x1b,dpbx, name=tag + "_dwin_bx"),
                             mm_tn(x1b,dpbg, name=tag + "_dwin_bg")], 1)
    dx1 = mm_nn(dpqkv, w_qkv.T, add=dz1, add_scale=DN_ALPHA, name=tag + "_dx_qkv")
    dx1 = mm_nn(dpbx, w_bx.T, add=dx1, name=tag + "_dx_bx")
    dx1 = mm_nn(dpbg, w_bg.T, add=dx1, name=tag + "_dx_bg")
    grads = {"ab_w_in": dw_in[None], "a_sinks": dsinks, "b_conv_w": dconv_w[None], "b_conv_b": dconv_b,
             "b_wa": _slab_blocks(dwa_s)[None], "b_ba": dba, "b_wx": _slab_blocks(dwx_s)[None], "b_bx": dbx,
             "b_lam": dlam, "ab_w_out": dwo[None]}
    return dx1, grads


def _mixer_c_fwd(x1, x1b, W, g, b, nb, tag):
    w_in = W["c_w_in"][0].astype(bf16)
    d = w_in.shape[0]
    o1, o2 = 3 * C_WIDTH, 4 * C_WIDTH
    w_qkv, w_z = w_in[:, :o1], w_in[:, o1:o2]
    w_ba = jnp.concatenate([w_in[:, o2:], jnp.zeros((d, LANES - 2 * C_HEADS), bf16)], 1)
    pqkv = mm_nn(x1b,w_qkv, name=tag + "_in_qkv")
    pz = mm_nn(x1b,w_z, name=tag + "_in_z")
    pba = mm_nn(x1b,w_ba, name=tag + "_in_ba")
    zero_b = jnp.zeros((1, o1), f32)
    qkvc = conv_fwd(pqkv, W["c_conv_w"][0], zero_b, True, nb, name=tag + "_conv_fwd")
    prep = gdn_prep_fwd(qkvc, pba, W["c_a_log"], W["c_dt_bias"], nb, name=tag + "_prep_fwd")
    yc, states = gdn_rec_fwd(*prep, pz, W["c_norm_g"], nb, name=tag + "_rec_fwd")
    w_out = W["c_w_out"][0].astype(bf16)
    x2, z1, x2b = proj_ln([yc], [w_out], x1, g, b, name=tag + "_out_ln")
    saved = (pqkv, pz, pba, qkvc, prep, states, yc, w_qkv, w_z, w_ba, w_out, zero_b)
    return x2, x2b, z1, saved


def _mixer_c_bwd(x1b, dz1, dz1b, W, saved, nb, tag):
    pqkv, pz, pba, qkvc, prep, states, yc, w_qkv, w_z, w_ba, w_out, zero_b = saved
    dyc = mm_nn(dz1b, w_out.T, name=tag + "_dyc")
    dwo = mm_tn(yc, dz1b, name=tag + "_dwo")
    rec = gdn_rec_bwd(*prep, pz, W["c_norm_g"], states, dyc, nb, name=tag + "_rec_bwd")
    cts, dpz, dng = rec[:6], rec[6], rec[7]
    dq, dk, dv, dpba, dalog, ddtb = gdn_prep_bwd(qkvc, pba, W["c_a_log"], W["c_dt_bias"], cts, nb,
                                                 name=tag + "_prep_bwd")
    dqkvc = jnp.concatenate([dq, dk, dv], 1)
    dpqkv, dconv_w, _ = conv_bwd(pqkv, W["c_conv_w"][0], zero_b, dqkvc, True, nb, name=tag + "_conv_bwd")
    dw_in = jnp.concatenate([mm_tn(x1b,dpqkv, name=tag + "_dwin_qkv"), mm_tn(x1b,dpz, name=tag + "_dwin_z"),
                             mm_tn(x1b,dpba, name=tag + "_dwin_ba")[:, :2 * C_HEADS]], 1)
    dx1 = mm_nn(dpqkv, w_qkv.T, add=dz1, add_scale=DN_ALPHA, name=tag + "_dx_qkv")
    dx1 = mm_nn(dpz, w_z.T, add=dx1, name=tag + "_dx_z")
    dx1 = mm_nn(dpba, w_ba.T, add=dx1, name=tag + "_dx_ba")
    grads = {"c_w_in": dw_in[None], "c_conv_w": dconv_w[None], "c_a_log": dalog[:, :C_HEADS],
             "c_dt_bias": ddtb[:, :C_HEADS], "c_norm_g": dng, "c_w_out": dwo[None]}
    return dx1, grads


def _local_step(x, p, target, W, F, on_ffn_grads):
    nb, s, d = x.shape
    t = nb * s
    h = x.reshape(t, d)
    hb = h.astype(bf16)
    tape = []
    for i in range(DEPTH):
        tag = f"l{i}"
        f1 = [F[k][i] for k in ("ffn1_wg", "ffn1_wu", "ffn1_wd")]
        f2 = [F[k][i] for k in ("ffn2_wg", "ffn2_wu", "ffn2_wd")]
        lg = [W["ln_g"][i, k][None] for k in range(3)]
        lb = [W["ln_b"][i, k][None] for k in range(3)]
        x1, z0, x1b = ffn_fwd(h, *f1, lg[0], lb[0], name=tag + "_ffn1_fwd")
        mixer = _mixer_ab_fwd if i % 2 == 0 else _mixer_c_fwd
        x2, x2b, z1, msaved = mixer(x1, x1b, W, lg[1], lb[1], nb, tag + "_mix")
        x3, z2, _ = ffn_fwd(x2, *f2, lg[2], lb[2], name=tag + "_ffn2_fwd")
        pi = p[i].reshape(t, -1)
        pw = (W["ple_wg"][i].astype(bf16), W["ple_bg"][i][None], W["ple_wp"][i].astype(bf16))
        x4, x4b = ple_fwd(x3, pi, *pw, name=tag + "_ple_fwd")
        tape.append((hb, z0, x1b, msaved, z1, x2b, z2, x3, pi, pw, lg))
        h, hb = x4, x4b
    dh, sq = loss_head(h, target.reshape(t, d), name="loss_head")
    loss = 0.5 * jnp.sum(sq) / d
    per_layer = [None] * DEPTH
    grads = {}
    for i in reversed(range(DEPTH)):
        tag = f"l{i}"
        hb_in, z0, x1b, msaved, z1, x2b, z2, x3, pi, pw, lg = tape[i]
        dx3, dple_wg, dple_bg, dple_wp = ple_bwd(x3, pi, dh, pw[0], pw[0].T, pw[1], pw[2], name=tag + "_ple_bwd")
        dz2, dz2b, dg2, db2 = ln_bwd(z2, dx3, lg[2], name=tag + "_ln2_bwd")
        f1 = [F[k][i] for k in ("ffn1_wg", "ffn1_wu", "ffn1_wd")]
        f2 = [F[k][i] for k in ("ffn2_wg", "ffn2_wu", "ffn2_wd")]
        dgate, dup, *df2 = ffn_bwd_weights(x2b, dz2b, *f2, name=tag + "_ffn2_bwd_w")
        on_ffn_grads(i, 3, df2)
        dx2 = ffn_bwd_input(dgate, dup, f2[0], f2[1], dz2, name=tag + "_ffn2_bwd_x")
        dz1, dz1b, dg1, db1 = ln_bwd(z1, dx2, lg[1], name=tag + "_ln1_bwd")
        mixer_bwd = _mixer_ab_bwd if i % 2 == 0 else _mixer_c_bwd
        dx1, mgrads = mixer_bwd(x1b, dz1, dz1b, W, msaved, nb, tag + "_mix")
        grads.update(mgrads)
        dz0, dz0b, dg0, db0 = ln_bwd(z0, dx1, lg[0], name=tag + "_ln0_bwd")
        dgate, dup, *df1 = ffn_bwd_weights(hb_in, dz0b, *f1, name=tag + "_ffn1_bwd_w")
        on_ffn_grads(i, 0, df1)
        dh = ffn_bwd_input(dgate, dup, f1[0], f1[1], dz0, name=tag + "_ffn1_bwd_x")
        per_layer[i] = {"ln_g": jnp.concatenate([dg0, dg1, dg2], 0), "ln_b": jnp.concatenate([db0, db1, db2], 0),
                        "ple_wg": dple_wg, "ple_bg": dple_bg[0], "ple_wp": dple_wp}
    for k in per_layer[0]:
        grads[k] = jnp.stack([per_layer[i][k] for i in range(DEPTH)])
    return loss, dh.reshape(nb, s, d), grads


WEIGHT_NAMES = ("ffn1_wg", "ffn1_wu", "ffn1_wd", "ffn2_wg", "ffn2_wu", "ffn2_wd", "ln_g", "ln_b", "ple_wg", "ple_bg",
                "ple_wp", "ab_w_in", "a_sinks", "b_conv_w", "b_conv_b", "b_wa", "b_ba", "b_wx", "b_bx", "b_lam",
                "ab_w_out", "c_w_in", "c_conv_w", "c_a_log", "c_dt_bias", "c_norm_g", "c_w_out")
NATIVE_NAMES = WEIGHT_NAMES[:6]
PACKED_NAMES = WEIGHT_NAMES[6:]
PACK_MATRICES = ("ple_wg", "ple_wp", "ab_w_in", "ab_w_out", "c_w_in", "c_w_out")
PACK_GROUPS = (tuple(k for k in PACKED_NAMES if k not in PACK_MATRICES), PACK_MATRICES)
PACK_TRANSIT = (f32, bf16)
SHARD_AXIS = {"ffn1_wg": 2, "ffn1_wu": 2, "ffn1_wd": 1, "ffn2_wg": 2, "ffn2_wu": 2, "ffn2_wd": 1, "ln_g": 2, "ln_b": 2,
              "ple_wg": 1, "ple_wp": 2, "ab_w_in": 2, "b_conv_w": 2, "ab_w_out": 1, "c_w_in": 2, "c_conv_w": 2,
              "c_w_out": 1}
N_CHIPS = 4
PACK_COLS = LANES
PACK_TILE_MULTIPLE = 256
ELEMENTWISE_BLOCK_ELEMS = 128 * 1024


def _row_tile(r, cols):
    return _tile(r, max(2 * SUBLANES, ELEMENTWISE_BLOCK_ELEMS // cols), 2 * SUBLANES)
MESH = pl.DeviceIdType.MESH
ANY = pl.BlockSpec(memory_space=pl.ANY)


def _tiled_dims(shape):
    w = shape[-1]
    r = 1
    for dim in shape[:-1]:
        r *= dim
    return r, w, -(-r // SUBLANES) * SUBLANES, -(-w // LANES) * LANES


def _pack(pieces, lead=()):
    k = len(lead)
    tiles = []
    for a in pieces:
        r, w, rp, wp = _tiled_dims(a.shape[k:])
        a2 = jnp.pad(a.reshape(lead + (r, w)), [(0, 0)] * k + [(0, rp - r), (0, wp - w)])
        a2 = a2.reshape(lead + (rp // SUBLANES, SUBLANES, wp // LANES, LANES))
        a2 = jnp.swapaxes(a2, k + 1, k + 2)
        tiles.append(a2.reshape(lead + (-1, SUBLANES, LANES)))
    flat = jnp.concatenate(tiles, axis=k)
    n = flat.shape[k]
    n_pad = -(-n // PACK_TILE_MULTIPLE) * PACK_TILE_MULTIPLE
    flat = jnp.pad(flat, [(0, 0)] * k + [(0, n_pad - n), (0, 0), (0, 0)])
    return flat.reshape(lead + (n_pad * SUBLANES, PACK_COLS))


def _unpack(pack, shapes, lead=()):
    k = len(lead)
    flat = pack.reshape(lead + (-1, SUBLANES, LANES))
    out, o = [], 0
    for shp in shapes:
        r, w, rp, wp = _tiled_dims(shp)
        n = (rp // SUBLANES) * (wp // LANES)
        a2 = lax.slice_in_dim(flat, o, o + n, axis=k).reshape(lead + (rp // SUBLANES, wp // LANES, SUBLANES, LANES))
        a2 = jnp.swapaxes(a2, k + 1, k + 2).reshape(lead + (rp, wp))
        a2 = lax.slice_in_dim(lax.slice_in_dim(a2, 0, r, axis=k), 0, w, axis=k + 1)
        out.append(a2.reshape(lead + tuple(shp)))
        o += n
    return out


def _mesh_position():
    x, y, c = lax.axis_index("x"), lax.axis_index("y"), lax.axis_index("c")
    chips = [(1 - x, y), (x, 1 - y), (1 - x, 1 - y)]
    return x, y, c, chips


def _remote(src, dst, send_sems, recv_sems, k, to):
    return pltpu.make_async_remote_copy(src_ref=src, dst_ref=dst, send_sem=send_sems.at[k], recv_sem=recv_sems.at[k],
                                        device_id=to, device_id_type=MESH)


def _sems(n):
    return pltpu.SemaphoreType.DMA((n,))


def place_slot(parts, slots, n_slots, dtype, from_slot, *, name):
    n = len(parts)
    r, cols = parts[0].shape[-2:]
    tr = _row_tile(r, cols)

    def body(src_ref, dst_ref, *refs):
        for a in range(n):
            refs[n + a][...] = refs[a][...].astype(dtype)

    dst = pl.BlockSpec((None, tr, cols), lambda i, src_ref, dst_ref: (dst_ref[0], i, 0))
    src = (pl.BlockSpec((None, tr, cols), lambda i, src_ref, dst_ref: (src_ref[0], i, 0)) if from_slot
           else pl.BlockSpec((tr, cols), lambda i, src_ref, dst_ref: (i, 0)))
    return pl.pallas_call(
        body,
        grid_spec=pltpu.PrefetchScalarGridSpec(num_scalar_prefetch=2, grid=(r // tr,), in_specs=[src] * n,
                                               out_specs=[dst] * n),
        out_shape=[jax.ShapeDtypeStruct((n_slots, r, cols), dtype)] * n,
        compiler_params=_params(("parallel",)), name=name,
    )(*slots, *parts)


def gather_shards(bufs, *, name):
    n = len(bufs)

    def body(*refs):
        out_refs = refs[n:2 * n]
        send_sems, recv_sems = refs[2 * n:]
        x, y, c, chips = _mesh_position()
        me = 2 * x + y
        sibling = (x, y, 1 - c)
        waits = []
        for j, (cx, cy) in enumerate(chips):
            for a in range(n):
                own = out_refs[a].at[me, c]
                cp = _remote(own, own, send_sems, recv_sems, 6 * a + j, (cx, cy, c))
                cp.start()
                waits.append(cp.wait_send)
        for j, (cx, cy) in enumerate(chips):
            for a in range(n):
                got = out_refs[a].at[2 * cx + cy, c]
                _remote(got, got, send_sems, recv_sems, 6 * a + j, (cx, cy, c)).wait_recv()
                fw = _remote(got, got, send_sems, recv_sems, 6 * a + 3 + j, sibling)
                fw.start()
                waits.append(fw.wait_send)
        for j, (cx, cy) in enumerate(chips):
            for a in range(n):
                got = out_refs[a].at[2 * cx + cy, 1 - c]
                _remote(got, got, send_sems, recv_sems, 6 * a + 3 + j, sibling).wait_recv()
        for wait in waits:
            wait()

    return pl.pallas_call(
        body, out_shape=[jax.ShapeDtypeStruct(b.shape, b.dtype) for b in bufs],
        in_specs=[ANY] * n, out_specs=[ANY] * n, scratch_shapes=[_sems(6 * n), _sems(6 * n)],
        input_output_aliases={a: a for a in range(n)}, name=name,
    )(*bufs)


def chip_exchange(ps, qs, *, name):
    n = len(ps)

    def body(*refs):
        p_refs, q_refs = refs[:n], refs[2 * n:3 * n]
        send_sems, recv_sems = refs[3 * n:]
        x, y, c, chips = _mesh_position()
        me = 2 * x + y
        waits = []
        for j, (cx, cy) in enumerate(chips):
            for a in range(n):
                cp = _remote(p_refs[a].at[2 * cx + cy], q_refs[a].at[me], send_sems, recv_sems, 3 * a + j, (cx, cy, c))
                cp.start()
                waits.append(cp.wait_send)
        for j, (cx, cy) in enumerate(chips):
            for a in range(n):
                got = q_refs[a].at[2 * cx + cy]
                _remote(got, got, send_sems, recv_sems, 3 * a + j, (cx, cy, c)).wait_recv()
        for wait in waits:
            wait()

    return pl.pallas_call(
        body, out_shape=[jax.ShapeDtypeStruct(q_.shape, q_.dtype) for q_ in qs], in_specs=[ANY] * (2 * n),
        out_specs=[ANY] * n, scratch_shapes=[_sems(3 * n), _sems(3 * n)],
        input_output_aliases={n + a: a for a in range(n)}, name=name,
    )(*ps, *qs)


def gather_slots_async(bufs, collective_id, *, name):
    n = len(bufs)
    refs = [jax.new_ref(b, memory_space=pltpu.MemorySpace.HBM) for b in bufs]

    @pl.kernel(mesh=plsc.ScalarSubcoreMesh(axis_name="sequencer", num_cores=1), name=name,
               scratch_types=(_sems(3 * n), _sems(3 * n)),
               compiler_params=pltpu.CompilerParams(collective_id=collective_id))
    def launch(send_sems, recv_sems):
        x, y, c, chips = _mesh_position()
        me = 2 * x + y
        barrier = pltpu.get_barrier_semaphore()
        for cx, cy in chips:
            pl.semaphore_signal(barrier, inc=1, device_id=(cx, cy, c), device_id_type=MESH)
        pl.semaphore_wait(barrier, len(chips))
        sends = []
        for j, (cx, cy) in enumerate(chips):
            for a in range(n):
                own = refs[a].at[me]
                cp = _remote(own, own, send_sems, recv_sems, 3 * a + j, (cx, cy, c))
                cp.start()
                sends.append(cp)
        for j, (cx, cy) in enumerate(chips):
            for a in range(n):
                got = refs[a].at[2 * cx + cy]
                _remote(got, got, send_sems, recv_sems, 3 * a + j, (cx, cy, c)).wait_recv()
        for cp in sends:
            cp.wait_send()

    launch()
    return [r[...] for r in refs]


N_DEVICES = 8
PEER_FLIPS = tuple((dx, dy, dc) for dx in (0, 1) for dy in (0, 1) for dc in (0, 1) if dx or dy or dc)


def exchange_partials_async(sends, collective_id, *, name):
    n = len(sends)
    k = len(PEER_FLIPS)

    def launch(*refs):
        s_refs, r_refs = refs[:n], refs[n:2 * n]
        send_sems, recv_sems, local_sems = refs[2 * n:]
        x, y, c, _ = _mesh_position()
        me = 4 * x + 2 * y + c
        peers = [(1 - x if dx else x, 1 - y if dy else y, 1 - c if dc else c) for dx, dy, dc in PEER_FLIPS]
        barrier = pltpu.get_barrier_semaphore()
        for peer in peers:
            pl.semaphore_signal(barrier, inc=1, device_id=peer, device_id_type=MESH)
        pl.semaphore_wait(barrier, len(peers))
        sends_started = []
        for a in range(n):
            own = pltpu.make_async_copy(s_refs[a].at[2 * x + y], r_refs[a].at[me], local_sems.at[a])
            own.start()
            sends_started.append(own)
        for j, (px, py, pc) in enumerate(peers):
            for a in range(n):
                cp = _remote(s_refs[a].at[2 * px + py], r_refs[a].at[me], send_sems, recv_sems, j, (px, py, pc))
                cp.start()
                sends_started.append(cp)
        for j, (px, py, pc) in enumerate(peers):
            for a in range(n):
                got = r_refs[a].at[4 * px + 2 * py + pc]
                _remote(got, got, send_sems, recv_sems, j, (px, py, pc)).wait_recv()
        for cp in sends_started[n:]:
            cp.wait_send()
        for cp in sends_started[:n]:
            cp.wait()

    return list(pl.kernel(
        launch, out_type=[jax.ShapeDtypeStruct((N_DEVICES,) + s_.shape[1:], s_.dtype) for s_ in sends],
        mesh=plsc.ScalarSubcoreMesh(axis_name="sequencer", num_cores=1), name=name,
        scratch_types=(_sems(k), _sems(k), _sems(n)),
        compiler_params=pltpu.CompilerParams(collective_id=collective_id))(*sends))


def sibling_exchange(gs, *, name):
    n = len(gs)

    def body(*refs):
        g_refs, out_refs = refs[:n], refs[n:2 * n]
        send_sems, recv_sems = refs[2 * n:]
        x, y, c, _ = _mesh_position()
        cps = [_remote(g_refs[a].at[:, 1 - c], out_refs[a], send_sems, recv_sems, a, (x, y, 1 - c)) for a in range(n)]
        for cp in cps:
            cp.start()
        for cp in cps:
            cp.wait()

    return pl.pallas_call(
        body, out_shape=[jax.ShapeDtypeStruct(g.shape[:1] + g.shape[2:], g.dtype) for g in gs],
        in_specs=[ANY] * n, out_specs=[ANY] * n, scratch_shapes=[_sems(n), _sems(n)], name=name,
    )(*gs)


def add_own_half(gs, others, c_idx, dtype, *, name):
    n = len(gs)
    ns, _, r, cols = gs[0].shape
    tr = _row_tile(r, cols)

    def body(c_ref, *refs):
        for a in range(n):
            refs[2 * n + a][...] = (refs[a][...] + refs[n + a][...]).astype(dtype)

    own = pl.BlockSpec((None, None, tr, cols), lambda s, i, c_ref: (s, c_ref[0], i, 0))
    oth = pl.BlockSpec((None, tr, cols), lambda s, i, c_ref: (s, i, 0))
    return pl.pallas_call(
        body,
        grid_spec=pltpu.PrefetchScalarGridSpec(num_scalar_prefetch=1, grid=(ns, r // tr),
                                               in_specs=[own] * n + [oth] * n, out_specs=[oth] * n),
        out_shape=[jax.ShapeDtypeStruct((ns, r, cols), dtype)] * n,
        compiler_params=_params(("parallel", "parallel")), name=name,
    )(c_idx, *gs, *others)


def sum_slots(qs, *, name):
    n = len(qs)
    ns, r, cols = qs[0].shape
    tr = _row_tile(r, cols * ns)

    def body(*refs):
        for a in range(n):
            q_ref = refs[a]
            acc = q_ref[0].astype(f32) + q_ref[1].astype(f32)
            for i in range(2, ns):
                acc = acc + q_ref[i].astype(f32)
            refs[n + a][...] = acc

    return pl.pallas_call(
        body, grid=(r // tr,), in_specs=[pl.BlockSpec((ns, tr, cols), lambda i: (0, i, 0))] * n,
        out_specs=[pl.BlockSpec((tr, cols), lambda i: (i, 0))] * n,
        out_shape=[jax.ShapeDtypeStruct((r, cols), f32)] * n,
        compiler_params=_params(("parallel",)), name=name,
    )(*qs)


def sibling_share(bufs, *, name):
    n = len(bufs)

    def body(*refs):
        out_refs = refs[n:2 * n]
        send_sems, recv_sems = refs[2 * n:]
        x, y, c, _ = _mesh_position()
        sibling = (x, y, 1 - c)
        cps = []
        for a in range(n):
            own = out_refs[a].at[c]
            cp = _remote(own, own, send_sems, recv_sems, a, sibling)
            cp.start()
            cps.append(cp)
        for a in range(n):
            theirs = out_refs[a].at[1 - c]
            _remote(theirs, theirs, send_sems, recv_sems, a, sibling).wait_recv()
        for cp in cps:
            cp.wait_send()

    return pl.pallas_call(
        body, out_shape=[jax.ShapeDtypeStruct(b.shape, b.dtype) for b in bufs], in_specs=[ANY] * n,
        out_specs=[ANY] * n, scratch_shapes=[_sems(n), _sems(n)],
        input_output_aliases={a: a for a in range(n)}, name=name,
    )(*bufs)


def _adamw_update(w, g, m, v):
    m2 = ADAM_B1 * m + (1.0 - ADAM_B1) * g
    v2 = ADAM_B2 * v + (1.0 - ADAM_B2) * (g * g)
    m_hat = m2 / (1.0 - ADAM_B1 ** ADAM_STEP)
    v_hat = v2 / (1.0 - ADAM_B2 ** ADAM_STEP)
    return -ADAM_LR * (m_hat / (jnp.sqrt(v_hat) + ADAM_EPS) + ADAM_WD * w), m2, v2


def adamw_from_partials(ws, ms, vs, slots, layer, acc, *, name):
    n = len(ws)
    nl, r, cols = ws[0].shape
    ns = slots[0].shape[0]
    tr = _row_tile(r, cols * 2)

    def body(*refs):
        for a in range(n):
            w_ref, m_ref, v_ref, s_ref = (refs[k * n + a] for k in range(4))
            g_ref, d_ref, m2_ref, v2_ref = (refs[len(refs) - 4 * n + k * n + a] for k in range(4))
            g = s_ref[0].astype(f32) + s_ref[1].astype(f32)
            for i in range(2, ns):
                g = g + s_ref[i].astype(f32)
            g_ref[...] = g
            d_ref[...], m2_ref[...], v2_ref[...] = _adamw_update(w_ref[...], g, m_ref[...], v_ref[...])

    lay = pl.BlockSpec((None, tr, cols), lambda i: (layer, i, 0))
    in_specs = [lay] * (3 * n) + [pl.BlockSpec((ns, tr, cols), lambda i: (0, i, 0))] * n
    args = [*ws, *ms, *vs, *slots]
    aliases = {}
    if acc is not None:
        in_specs += [ANY] * (4 * n)
        args += [a for lst in acc for a in lst]
        aliases = {4 * n + k: k for k in range(4 * n)}
    out = pl.pallas_call(
        body, grid=(r // tr,), in_specs=in_specs, out_specs=[lay] * (4 * n),
        out_shape=[jax.ShapeDtypeStruct((nl, r, cols), f32)] * (4 * n), input_output_aliases=aliases,
        compiler_params=_params(("parallel",)), name=name,
    )(*args)
    return [list(out[k * n:(k + 1) * n]) for k in range(4)]


def adamw(ws, gs, ms, vs, *, name):
    n = len(ws)
    r, cols = ws[0].shape
    tr = _row_tile(r, cols)

    def body(*refs):
        for a in range(n):
            w_ref, g_ref, m_ref, v_ref = (refs[k * n + a] for k in range(4))
            d_ref, m2_ref, v2_ref = (refs[(4 + k) * n + a] for k in range(3))
            d_ref[...], m2_ref[...], v2_ref[...] = _adamw_update(w_ref[...], g_ref[...], m_ref[...], v_ref[...])

    row = pl.BlockSpec((tr, cols), lambda i: (i, 0))
    out = pl.pallas_call(
        body, grid=(r // tr,), in_specs=[row] * (4 * n), out_specs=[row] * (3 * n),
        out_shape=[jax.ShapeDtypeStruct((r, cols), f32)] * (3 * n),
        compiler_params=_params(("parallel",)), name=name,
    )(*ws, *gs, *ms, *vs)
    return out[:n], out[n:2 * n], out[2 * n:]


def _full_weights(gathered, names, weights):
    pieces = _unpack(gathered, [weights[k].shape for k in names], lead=(N_CHIPS,))
    full = {}
    for name, pc in zip(names, pieces):
        ax = SHARD_AXIS.get(name)
        if ax is None:
            full[name] = weights[name]
        else:
            shp = weights[name].shape
            full[name] = jnp.moveaxis(pc, 0, ax).reshape(shp[:ax] + (N_CHIPS * shp[ax],) + shp[ax + 1:])
    return full


def _grad_pack(grads, names, shapes):
    pieces = []
    for name, shp in zip(names, shapes):
        g = grads[name]
        ax = SHARD_AXIS.get(name)
        if ax is None:
            pieces.append(jnp.broadcast_to(g.reshape(shp)[None], (N_CHIPS,) + tuple(shp)))
        else:
            pieces.append(jnp.stack(jnp.split(g, N_CHIPS, axis=ax)))
    return _pack(pieces, lead=(N_CHIPS,))


def _by_shape(arrays):
    groups = {}
    for i, a in enumerate(arrays):
        groups.setdefault(a.shape, []).append(i)
    return list(groups.values())


def _grouped(fn, lists, n_out, tag):
    outs = [[None] * len(lists[0]) for _ in range(n_out)]
    for gi, idx in enumerate(_by_shape(lists[0])):
        res = fn(*[[lst[i] for i in idx] for lst in lists], name=f"{tag}_{gi}")
        res = res if n_out > 1 else (res,)
        for k in range(n_out):
            for i, r in zip(idx, res[k]):
                outs[k][i] = r
    return outs if n_out > 1 else outs[0]


def _train_step(x, p, loss_target, weights, m, v):
    shapes = [[weights[k].shape for k in names] for names in PACK_GROUPS]
    halves = lambda a: a.reshape((2, a.shape[0] // 2) + a.shape[1:])
    packs = lambda d_: [halves(_pack([d_[k] for k in names])) for names in PACK_GROUPS]
    nn_ = len(NATIVE_NAMES)
    local = [weights[k] for k in NATIVE_NAMES] + packs(weights)
    local_m = [m[k] for k in NATIVE_NAMES] + packs(m)
    local_v = [v[k] for k in NATIVE_NAMES] + packs(v)
    flat = lambda lst: [a.reshape((-1, a.shape[-1])) for a in lst]
    c_idx = lax.axis_index("c").astype(jnp.int32).reshape(1)
    chip_idx = (2 * lax.axis_index("x") + lax.axis_index("y")).astype(jnp.int32).reshape(1)
    c2 = (c_idx, c_idx)
    chip2 = (chip_idx, chip_idx)

    def placed(arrays, slot, n_slots, dtype, from_slot, tag):
        return _grouped(lambda a, name: place_slot(a, slot, n_slots, dtype, from_slot, name=name), [arrays], 1, tag)

    ffn_own = [weights[k][i] for i in range(DEPTH) for k in NATIVE_NAMES]
    ffn_bufs = placed(ffn_own, chip2, N_CHIPS, bf16, False, "place_ffn_weights")
    group = len(NATIVE_NAMES) // 2
    n_ffn_groups = len(ffn_bufs) // group
    ffn_gathered = []
    for gi in range(n_ffn_groups):
        ffn_gathered += gather_slots_async(ffn_bufs[gi * group:(gi + 1) * group], collective_id=1 + gi,
                                           name=f"comm_gather_ffn_{gi}")
    ffn_weights = {k: [ffn_gathered[i * len(NATIVE_NAMES) + j] for i in range(DEPTH)] for j, k in enumerate(NATIVE_NAMES)}
    pack_bufs = [placed(flat([a]), chip2, N_CHIPS, dt, False, f"place_packed_weights_{gi}")[0].reshape((N_CHIPS,) + a.shape)
                 for gi, (a, dt) in enumerate(zip(local[nn_:], PACK_TRANSIT))]
    full = {}
    for names, gathered in zip(PACK_GROUPS, gather_shards(pack_bufs, name="comm_gather_weights")):
        full.update(_full_weights(gathered, names, weights))
    first_grad_id = n_ffn_groups + 1
    in_flight = {}

    def on_ffn_grads(layer, first, partials):
        tag = f"ffn_grads_l{layer}_{first}"
        got = exchange_partials_async(partials, collective_id=first_grad_id + len(in_flight), name="comm_" + tag)
        in_flight[(layer, first)] = got

    loss, grad_x, grads = _local_step(x, p, loss_target, full, ffn_weights, on_ffn_grads)
    gs = [_grad_pack(grads, names, shp).reshape((N_CHIPS,) + a.shape)
          for names, shp, a in zip(PACK_GROUPS, shapes, local[nn_:])]
    others = sibling_exchange(gs, name="comm_grad_sibling")
    chip_sums = [add_own_half([g], [o], c_idx, dt, name=f"grad_add_sibling_{gi}")[0]
                 for gi, (g, o, dt) in enumerate(zip(gs, others, PACK_TRANSIT))]
    own = [placed([cs], chip2, N_CHIPS, dt, True, f"place_own_partial_{gi}")[0]
           for gi, (cs, dt) in enumerate(zip(chip_sums, PACK_TRANSIT))]
    slots = chip_exchange(chip_sums, own, name="comm_grad_chips")
    mine = _grouped(sum_slots, [list(slots)], 1, "grad_sum_chips")
    pack_sum = sibling_share(placed(mine, c2, 2, f32, False, "place_own_half"), name="comm_grad_share")
    ffn_out = [{} for _ in range(4)]
    for (layer, first), got in in_flight.items():
        names = NATIVE_NAMES[first:first + len(got)]
        for idx in _by_shape([weights[k] for k in names]):
            ks = [names[i] for i in idx]
            acc = [[out[k] for k in ks] for out in ffn_out] if ks[0] in ffn_out[0] else None
            res = adamw_from_partials([weights[k] for k in ks], [m[k] for k in ks], [v[k] for k in ks],
                                      [got[i] for i in idx], layer, acc, name=f"adamw_ffn_l{layer}_{first + idx[0]}")
            for out, arrays in zip(ffn_out, res):
                out.update(zip(ks, arrays))
    pack_out = [list(pack_sum)] + _grouped(adamw, [flat(local[nn_:]), flat(pack_sum), flat(local_m[nn_:]),
                                                    flat(local_v[nn_:])], 3, "adamw_packed")
    loss = lax.psum(loss, ("x", "y", "c"))
    outs = []
    for by_name, packs_ in zip(ffn_out, pack_out):
        by_name = dict(by_name)
        for names, shp, pk in zip(PACK_GROUPS, shapes, packs_):
            by_name.update(zip(names, _unpack(pk, shp)))
        outs += [by_name[k] for k in WEIGHT_NAMES]
    return (loss, grad_x, *outs)


def kernel(x, p, ffn1_wg, ffn1_wu, ffn1_wd, ffn2_wg, ffn2_wu, ffn2_wd, ln_g, ln_b, ple_wg, ple_bg, ple_wp, ab_w_in, a_sinks, b_conv_w, b_conv_b, b_wa, b_ba, b_wx, b_bx, b_lam, ab_w_out, c_w_in, c_conv_w, c_a_log, c_dt_bias, c_norm_g, c_w_out, loss_target, m_ffn1_wg, m_ffn1_wu, m_ffn1_wd, m_ffn2_wg, m_ffn2_wu, m_ffn2_wd, m_ln_g, m_ln_b, m_ple_wg, m_ple_bg, m_ple_wp, m_ab_w_in, m_a_sinks, m_b_conv_w, m_b_conv_b, m_b_wa, m_b_ba, m_b_wx, m_b_bx, m_b_lam, m_ab_w_out, m_c_w_in, m_c_conv_w, m_c_a_log, m_c_dt_bias, m_c_norm_g, m_c_w_out, v_ffn1_wg, v_ffn1_wu, v_ffn1_wd, v_ffn2_wg, v_ffn2_wu, v_ffn2_wd, v_ln_g, v_ln_b, v_ple_wg, v_ple_bg, v_ple_wp, v_ab_w_in, v_a_sinks, v_b_conv_w, v_b_conv_b, v_b_wa, v_b_ba, v_b_wx, v_b_bx, v_b_lam, v_ab_w_out, v_c_w_in, v_c_conv_w, v_c_a_log, v_c_dt_bias, v_c_norm_g, v_c_w_out):
    weights = [ffn1_wg, ffn1_wu, ffn1_wd, ffn2_wg, ffn2_wu, ffn2_wd, ln_g, ln_b, ple_wg, ple_bg, ple_wp, ab_w_in, a_sinks,
               b_conv_w, b_conv_b, b_wa, b_ba, b_wx, b_bx, b_lam, ab_w_out, c_w_in, c_conv_w, c_a_log, c_dt_bias, c_norm_g,
               c_w_out]
    m = [m_ffn1_wg, m_ffn1_wu, m_ffn1_wd, m_ffn2_wg, m_ffn2_wu, m_ffn2_wd, m_ln_g, m_ln_b, m_ple_wg, m_ple_bg, m_ple_wp,
         m_ab_w_in, m_a_sinks, m_b_conv_w, m_b_conv_b, m_b_wa, m_b_ba, m_b_wx, m_b_bx, m_b_lam, m_ab_w_out, m_c_w_in,
         m_c_conv_w, m_c_a_log, m_c_dt_bias, m_c_norm_g, m_c_w_out]
    v = [v_ffn1_wg, v_ffn1_wu, v_ffn1_wd, v_ffn2_wg, v_ffn2_wu, v_ffn2_wd, v_ln_g, v_ln_b, v_ple_wg, v_ple_bg, v_ple_wp,
         v_ab_w_in, v_a_sinks, v_b_conv_w, v_b_conv_b, v_b_wa, v_b_ba, v_b_wx, v_b_bx, v_b_lam, v_ab_w_out, v_c_w_in,
         v_c_conv_w, v_c_a_log, v_c_dt_bias, v_c_norm_g, v_c_w_out]
    return _train_step(x, p, loss_target, dict(zip(WEIGHT_NAMES, weights)), dict(zip(WEIGHT_NAMES, m)),
                       dict(zip(WEIGHT_NAMES, v)))
```

```python
import functools

import jax
import jax.numpy as jnp
from jax import lax
from jax.experimental import pallas as pl
from jax.experimental.pallas import tpu as pltpu
from jax.experimental.pallas import tpu_sc as plsc

f32 = jnp.float32
bf16 = jnp.bfloat16

DEPTH = 2
CHUNK = 64
A_HEADS, A_KV_HEADS, A_GROUP, A_HEAD_DIM = 8, 2, 4, 64
A_WIDTH, A_KV_WIDTH, A_WINDOW = 512, 128, 128
B_WIDTH, B_BLOCKS, B_BLOCK, B_CONV = 512, 8, 64, 4
RG_C = 8.0
C_HEADS, C_HEAD_DIM, C_WIDTH, C_CONV = 8, 128, 1024, 4
DN_ALPHA = (2.0 * DEPTH) ** 0.25
LN_EPS = 1e-5
NORM_EPS = 1e-6
NEG = -1e30
ADAM_LR, ADAM_B1, ADAM_B2, ADAM_EPS, ADAM_WD, ADAM_STEP = 0.001, 0.9, 0.999, 1e-08, 0.01, 10

VMEM_LIMIT_BYTES = 56 * 1024 * 1024
LANES = 128
SUBLANES = 8
GROUP_W = 128
PREP_FWD_UNROLL = 16
PREP_BWD_UNROLL = 16
C_HEADS_PER_STEP = 8
GDN_TIME_BLOCK = 256

NN = ((1,), (0,))
NT = ((1,), (1,))
TN = ((0,), (0,))


def _params(sem):
    return pltpu.CompilerParams(dimension_semantics=sem, vmem_limit_bytes=VMEM_LIMIT_BYTES)


def _tile(n, cap, mult):
    best = None
    t = mult
    while t <= min(n, cap):
        if n % t == 0:
            best = t
        t += mult
    return best if best is not None else n


def _bdot(a, b, dims):
    return lax.dot_general(a.astype(bf16), b.astype(bf16), (dims, ((), ())), preferred_element_type=f32)


def _running_sum(x, reverse):
    s = x.shape[0]
    t = lax.broadcasted_iota(jnp.int32, x.shape, 0)
    d = 1
    while d < s:
        if reverse:
            x = x + jnp.where(t < s - d, pltpu.roll(x, s - d, 0), 0.0)
        else:
            x = x + jnp.where(t >= d, pltpu.roll(x, d, 0), 0.0)
        d *= 2
    return x


@jax.custom_vjp
def _cumsum0(x):
    return _running_sum(x, False)


def _cumsum0_fwd(x):
    return _running_sum(x, False), None


def _cumsum0_bwd(_, g):
    return (_running_sum(g, True),)


_cumsum0.defvjp(_cumsum0_fwd, _cumsum0_bwd)


@jax.custom_vjp
def _bnn(a, b):
    return _bdot(a, b, NN)


def _bnn_fwd(a, b):
    return _bdot(a, b, NN), (a, b)


def _bnn_bwd(res, g):
    a, b = res
    return _bdot(g, b, NT), _bdot(a, g, TN)


_bnn.defvjp(_bnn_fwd, _bnn_bwd)


@jax.custom_vjp
def _bnt(a, b):
    return _bdot(a, b, NT)


def _bnt_fwd(a, b):
    return _bdot(a, b, NT), (a, b)


def _bnt_bwd(res, g):
    a, b = res
    return _bdot(g, b, NN), _bdot(g, a, TN)


_bnt.defvjp(_bnt_fwd, _bnt_bwd)


@jax.custom_vjp
def _btn(a, b):
    return _bdot(a, b, TN)


def _btn_fwd(a, b):
    return _bdot(a, b, TN), (a, b)


def _btn_bwd(res, g):
    a, b = res
    return _bdot(b, g, NT), _bdot(a, g, NN)


_btn.defvjp(_btn_fwd, _btn_bwd)

RAW_DOTS = (lambda a, b: _bdot(a, b, NN), lambda a, b: _bdot(a, b, NT), lambda a, b: _bdot(a, b, TN),
            lambda x: _running_sum(x, False))
VJP_DOTS = (_bnn, _bnt, _btn, _cumsum0)


def _layer_norm(z, g, b):
    mu = jnp.mean(z, -1, keepdims=True)
    d = z - mu
    var = jnp.mean(d * d, -1, keepdims=True)
    return d * lax.rsqrt(var + LN_EPS) * g + b


def _layer_norm_bwd(z, dy, g):
    mu = jnp.mean(z, -1, keepdims=True)
    dd = z - mu
    var = jnp.mean(dd * dd, -1, keepdims=True)
    rstd = lax.rsqrt(var + LN_EPS)
    xhat = dd * rstd
    dxh = dy * g
    dz = rstd * (dxh - jnp.mean(dxh, -1, keepdims=True) - xhat * jnp.mean(dxh * xhat, -1, keepdims=True))
    return dz, jnp.sum(dy * xhat, 0, keepdims=True), jnp.sum(dy, 0, keepdims=True)


def _silu(x):
    return x * jax.nn.sigmoid(x)


def mm_nn(a, w, add=None, add_scale=1.0, *, name):
    m, k = a.shape
    n = w.shape[1]
    tm = _tile(m, 1024, 2 * SUBLANES)
    tn = _tile(n, 1024, LANES)

    def body(*refs):
        if add is None:
            a_ref, w_ref, o_ref = refs
            o_ref[...] = _bdot(a_ref[...], w_ref[...], NN)
        else:
            a_ref, w_ref, add_ref, o_ref = refs
            o_ref[...] = _bdot(a_ref[...], w_ref[...], NN) + add_scale * add_ref[...]

    in_specs = [pl.BlockSpec((tm, k), lambda i, j: (i, 0)), pl.BlockSpec((k, tn), lambda i, j: (0, j))]
    args = [a, w]
    if add is not None:
        in_specs.append(pl.BlockSpec((tm, tn), lambda i, j: (i, j)))
        args.append(add)
    return pl.pallas_call(
        body, grid=(m // tm, n // tn), in_specs=in_specs,
        out_specs=pl.BlockSpec((tm, tn), lambda i, j: (i, j)),
        out_shape=jax.ShapeDtypeStruct((m, n), f32),
        compiler_params=_params(("parallel", "parallel")), name=name,
    )(*args)


def mm_tn(a, b, *, name):
    m, k = a.shape
    n = b.shape[1]
    tm = _tile(m, 1024, 2 * SUBLANES)
    tn = _tile(n, 1024, LANES)

    def body(a_ref, b_ref, o_ref):
        part = _bdot(a_ref[...], b_ref[...], TN)

        @pl.when(pl.program_id(1) == 0)
        def _():
            o_ref[...] = part

        @pl.when(pl.program_id(1) > 0)
        def _():
            o_ref[...] += part

    return pl.pallas_call(
        body, grid=(n // tn, m // tm),
        in_specs=[pl.BlockSpec((tm, k), lambda j, i: (i, 0)), pl.BlockSpec((tm, tn), lambda j, i: (i, j))],
        out_specs=pl.BlockSpec((k, tn), lambda j, i: (0, j)),
        out_shape=jax.ShapeDtypeStruct((k, n), f32),
        compiler_params=_params(("parallel", "arbitrary")), name=name,
    )(a, b)


def proj_ln(a_list, w_list, xres, g, b, *, name):
    t, d = xres.shape
    tm = _tile(t, 256, 2 * SUBLANES)
    na = len(a_list)

    def body(*refs):
        a_refs, w_refs = refs[:na], refs[na:2 * na]
        x_ref, g_ref, b_ref, y_ref, z_ref, yb_ref = refs[2 * na:]
        z = DN_ALPHA * x_ref[...]
        for a_ref, w_ref in zip(a_refs, w_refs):
            z = z + _bdot(a_ref[...], w_ref[...], NN)
        z_ref[...] = z
        y = _layer_norm(z, g_ref[...], b_ref[...])
        y_ref[...] = y
        yb_ref[...] = y.astype(bf16)

    in_specs = [pl.BlockSpec((tm, a.shape[1]), lambda i: (i, 0)) for a in a_list]
    in_specs += [pl.BlockSpec(w.shape, lambda i: (0, 0)) for w in w_list]
    in_specs += [pl.BlockSpec((tm, d), lambda i: (i, 0)), pl.BlockSpec((1, d), lambda i: (0, 0)),
                 pl.BlockSpec((1, d), lambda i: (0, 0))]
    return pl.pallas_call(
        body, grid=(t // tm,), in_specs=in_specs,
        out_specs=[pl.BlockSpec((tm, d), lambda i: (i, 0))] * 3,
        out_shape=[jax.ShapeDtypeStruct((t, d), f32)] * 2 + [jax.ShapeDtypeStruct((t, d), bf16)],
        compiler_params=_params(("parallel",)), name=name,
    )(*a_list, *w_list, xres, g, b)


def ln_bwd(z, dy, g, *, name):
    t, d = z.shape
    tm = _tile(t, 512, SUBLANES)

    def body(z_ref, dy_ref, g_ref, dz_ref, dzb_ref, dg_ref, db_ref):
        dz, pg, pb = _layer_norm_bwd(z_ref[...], dy_ref[...], g_ref[...])
        dz_ref[...] = dz
        dzb_ref[...] = dz.astype(bf16)

        @pl.when(pl.program_id(0) == 0)
        def _():
            dg_ref[...] = pg
            db_ref[...] = pb

        @pl.when(pl.program_id(0) > 0)
        def _():
            dg_ref[...] += pg
            db_ref[...] += pb

    row = pl.BlockSpec((tm, d), lambda i: (i, 0))
    vec = pl.BlockSpec((1, d), lambda i: (0, 0))
    return pl.pallas_call(
        body, grid=(t // tm,), in_specs=[row, row, vec], out_specs=[row, row, vec, vec],
        out_shape=[jax.ShapeDtypeStruct((t, d), f32), jax.ShapeDtypeStruct((t, d), bf16),
                   jax.ShapeDtypeStruct((1, d), f32), jax.ShapeDtypeStruct((1, d), f32)],
        compiler_params=_params(("arbitrary",)), name=name,
    )(z, dy, g)


def loss_head(y, target, *, name):
    t, d = y.shape
    tm = _tile(t, 512, SUBLANES)

    def body(y_ref, t_ref, dy_ref, sq_ref):
        e = y_ref[...] - t_ref[...]
        dy_ref[...] = e * (1.0 / d)
        part = jnp.sum(e * e, 0, keepdims=True)

        @pl.when(pl.program_id(0) == 0)
        def _():
            sq_ref[...] = part

        @pl.when(pl.program_id(0) > 0)
        def _():
            sq_ref[...] += part

    row = pl.BlockSpec((tm, d), lambda i: (i, 0))
    vec = pl.BlockSpec((1, d), lambda i: (0, 0))
    return pl.pallas_call(
        body, grid=(t // tm,), in_specs=[row, row], out_specs=[row, vec],
        out_shape=[jax.ShapeDtypeStruct((t, d), f32), jax.ShapeDtypeStruct((1, d), f32)],
        compiler_params=_params(("arbitrary",)), name=name,
    )(y, target)


FFN_COL_BLOCK = 256
FFN_ROWS = 1024


def _lane_blocks(n):
    return [slice(s, min(s + FFN_COL_BLOCK, n)) for s in range(0, n, FFN_COL_BLOCK)]


def ffn_fwd(x, wg, wu, wd, g, b, *, name):
    t, d = x.shape
    nf, _, tf = wg.shape
    tm = _tile(t, FFN_ROWS, SUBLANES)

    def body(x_ref, wg_ref, wu_ref, wd_ref, g_ref, b_ref, y_ref, z_ref, yb_ref, acc_ref):
        f = pl.program_id(1)
        xb = x_ref[...].astype(bf16)
        part, pending = None, None
        for cols in _lane_blocks(tf):
            gate_up = (_bdot(xb, wg_ref[:, cols], NN), _bdot(xb, wu_ref[:, cols], NN), cols)
            if pending is not None:
                down = _bdot(_silu(pending[0]) * pending[1], wd_ref[pending[2], :], NN)
                part = down if part is None else part + down
            pending = gate_up
        down = _bdot(_silu(pending[0]) * pending[1], wd_ref[pending[2], :], NN)
        part = down if part is None else part + down

        @pl.when(f == 0)
        def _():
            acc_ref[...] = part

        @pl.when(f > 0)
        def _():
            acc_ref[...] += part

        @pl.when(f == nf - 1)
        def _():
            z = DN_ALPHA * x_ref[...] + 0.5 * acc_ref[...]
            z_ref[...] = z
            y = _layer_norm(z, g_ref[...], b_ref[...])
            y_ref[...] = y
            yb_ref[...] = y.astype(bf16)

    row = pl.BlockSpec((tm, d), lambda i, j: (i, 0))
    vec = pl.BlockSpec((1, d), lambda i, j: (0, 0))
    wcol = pl.BlockSpec((None, d, tf), lambda i, j: (j, 0, 0))
    wrow = pl.BlockSpec((None, tf, d), lambda i, j: (j, 0, 0))
    return pl.pallas_call(
        body, grid=(t // tm, nf),
        in_specs=[row, wcol, wcol, wrow, vec, vec],
        out_specs=[row, row, row],
        out_shape=[jax.ShapeDtypeStruct((t, d), f32)] * 2 + [jax.ShapeDtypeStruct((t, d), bf16)],
        scratch_shapes=[pltpu.VMEM((tm, d), f32)],
        compiler_params=_params(("parallel", "arbitrary")), name=name,
    )(x, wg, wu, wd, g, b)


def ffn_bwd_weights(xb, dzb, wg, wu, wd, *, name):
    t, d = xb.shape
    nf, _, tf = wg.shape
    tm = _tile(t, FFN_ROWS, SUBLANES)
    nt = t // tm

    def body(x_ref, dz_ref, wg_ref, wu_ref, wd_ref, dgate_ref, dup_ref, owg_ref, owu_ref, owd_ref,
             dwg_ref, dwu_ref, dwd_ref):
        x = x_ref[...]
        dzh = dz_ref[...] * 0.5

        def first_half(cols):
            return _bdot(x, wg_ref[:, cols], NN), _bdot(x, wu_ref[:, cols], NN), _bdot(dzh, wd_ref[cols, :], NT), cols

        def second_half(gate, up, dh, cols):
            sg = jax.nn.sigmoid(gate)
            s = gate * sg
            dup = (dh * s).astype(bf16)
            dgate = (dh * up * (sg * (1.0 + gate * (1.0 - sg)))).astype(bf16)
            dgate_ref[:, cols] = dgate
            dup_ref[:, cols] = dup
            return _bdot(x, dgate, TN), _bdot(x, dup, TN), _bdot(s * up, dzh, TN), cols

        parts, pending = [], None
        for cols in _lane_blocks(tf):
            nxt = first_half(cols)
            if pending is not None:
                parts.append(second_half(*pending))
            pending = nxt
        parts.append(second_half(*pending))

        @pl.when(pl.program_id(1) == 0)
        def _():
            for pwg, pwu, pwd, cols in parts:
                dwg_ref[:, cols] = pwg
                dwu_ref[:, cols] = pwu
                dwd_ref[cols, :] = pwd

        @pl.when(pl.program_id(1) > 0)
        def _():
            for pwg, pwu, pwd, cols in parts:
                dwg_ref[:, cols] += pwg
                dwu_ref[:, cols] += pwu
                dwd_ref[cols, :] += pwd

        @pl.when(pl.program_id(1) == nt - 1)
        def _():
            owg_ref[...] = dwg_ref[...].astype(bf16)
            owu_ref[...] = dwu_ref[...].astype(bf16)
            owd_ref[...] = dwd_ref[...].astype(bf16)

    row = pl.BlockSpec((tm, d), lambda j, i: (i, 0))
    wcol = pl.BlockSpec((None, d, tf), lambda j, i: (j, 0, 0))
    wrow = pl.BlockSpec((None, tf, d), lambda j, i: (j, 0, 0))
    act = pl.BlockSpec((None, tm, tf), lambda j, i: (j, i, 0))
    return pl.pallas_call(
        body, grid=(nf, nt), in_specs=[row, row, wcol, wcol, wrow], out_specs=[act, act, wcol, wcol, wrow],
        out_shape=[jax.ShapeDtypeStruct((nf, t, tf), bf16), jax.ShapeDtypeStruct((nf, t, tf), bf16),
                   jax.ShapeDtypeStruct((nf, d, tf), bf16), jax.ShapeDtypeStruct((nf, d, tf), bf16),
                   jax.ShapeDtypeStruct((nf, tf, d), bf16)],
        scratch_shapes=[pltpu.VMEM((d, tf), f32), pltpu.VMEM((d, tf), f32), pltpu.VMEM((tf, d), f32)],
        compiler_params=_params(("parallel", "arbitrary")), name=name,
    )(xb, dzb, wg, wu, wd)


def ffn_bwd_input(dgate, dup, wg, wu, dz, *, name):
    nf, t, tf = dgate.shape
    d = wg.shape[1]
    tm = _tile(t, FFN_ROWS // 2, SUBLANES)

    def body(dg_ref, du_ref, wg_ref, wu_ref, dz_ref, dx_ref):
        acc = DN_ALPHA * dz_ref[...]
        for j in range(nf):
            acc = acc + _bdot(dg_ref[j], wg_ref[j], NT) + _bdot(du_ref[j], wu_ref[j], NT)
        dx_ref[...] = acc

    act = pl.BlockSpec((nf, tm, tf), lambda i: (0, i, 0))
    wsp = pl.BlockSpec((nf, d, tf), lambda i: (0, 0, 0))
    row = pl.BlockSpec((tm, d), lambda i: (i, 0))
    return pl.pallas_call(
        body, grid=(t // tm,), in_specs=[act, act, wsp, wsp, row], out_specs=row,
        out_shape=jax.ShapeDtypeStruct((t, d), f32),
        compiler_params=_params(("parallel",)), name=name,
    )(dgate, dup, wg, wu, dz)


def ple_fwd(x, p, wg, bg, wp, *, name):
    t, d = x.shape
    dp = p.shape[1]
    tm = _tile(t, 512, 2 * SUBLANES)

    def body(x_ref, p_ref, wg_ref, bg_ref, wp_ref, o_ref, ob_ref):
        x_ = x_ref[...]
        gate = jax.nn.sigmoid(_bdot(x_, wg_ref[...], NN) + bg_ref[...])
        out = x_ + gate * _bdot(p_ref[...], wp_ref[...], NN)
        o_ref[...] = out
        ob_ref[...] = out.astype(bf16)

    row = pl.BlockSpec((tm, d), lambda i: (i, 0))
    return pl.pallas_call(
        body, grid=(t // tm,),
        in_specs=[row, pl.BlockSpec((tm, dp), lambda i: (i, 0)), pl.BlockSpec((d, d), lambda i: (0, 0)),
                  pl.BlockSpec((1, d), lambda i: (0, 0)), pl.BlockSpec((dp, d), lambda i: (0, 0))],
        out_specs=[row, row], out_shape=[jax.ShapeDtypeStruct((t, d), f32), jax.ShapeDtypeStruct((t, d), bf16)],
        compiler_params=_params(("parallel",)), name=name,
    )(x, p, wg, bg, wp)


def ple_bwd(x, p, dy, wg, wgt, bg, wp, z, ln_g, *, name):
    t, d = x.shape
    dp = p.shape[1]
    tm = _tile(t, 512, 2 * SUBLANES)

    def body(x_ref, p_ref, dy_ref, wg_ref, wgt_ref, bg_ref, wp_ref, z_ref, g_ref,
             dz_ref, dzb_ref, dwg_ref, dbg_ref, dwp_ref, dg_ref, db_ref):
        x_ = x_ref[...]
        dy_ = dy_ref[...]
        s = jax.nn.sigmoid(_bdot(x_, wg_ref[...], NN) + bg_ref[...])
        e = _bdot(p_ref[...], wp_ref[...], NN)
        da = dy_ * e * s * (1.0 - s)
        de = dy_ * s
        dx = dy_ + _bdot(da, wgt_ref[...], NN)
        dz, pg, pb = _layer_norm_bwd(z_ref[...], dx, g_ref[...])
        dz_ref[...] = dz
        dzb_ref[...] = dz.astype(bf16)
        parts = ((dwg_ref, _bdot(x_, da, TN)), (dbg_ref, jnp.sum(da, 0, keepdims=True)),
                 (dwp_ref, _bdot(p_ref[...], de, TN)), (dg_ref, pg), (db_ref, pb))

        @pl.when(pl.program_id(0) == 0)
        def _():
            for ref, part in parts:
                ref[...] = part

        @pl.when(pl.program_id(0) > 0)
        def _():
            for ref, part in parts:
                ref[...] += part

    row = pl.BlockSpec((tm, d), lambda i: (i, 0))
    full = lambda shape: pl.BlockSpec(shape, lambda i: (0, 0))
    return pl.pallas_call(
        body, grid=(t // tm,),
        in_specs=[row, pl.BlockSpec((tm, dp), lambda i: (i, 0)), row, full((d, d)), full((d, d)), full((1, d)),
                  full((dp, d)), row, full((1, d))],
        out_specs=[row, row, full((d, d)), full((1, d)), full((dp, d)), full((1, d)), full((1, d))],
        out_shape=[jax.ShapeDtypeStruct((t, d), f32), jax.ShapeDtypeStruct((t, d), bf16),
                   jax.ShapeDtypeStruct((d, d), f32), jax.ShapeDtypeStruct((1, d), f32),
                   jax.ShapeDtypeStruct((dp, d), f32), jax.ShapeDtypeStruct((1, d), f32),
                   jax.ShapeDtypeStruct((1, d), f32)],
        compiler_params=_params(("arbitrary",)), name=name,
    )(x, p, dy, wg, wgt, bg, wp, z, ln_g)


def _conv_taps(xpad_ref, w_ref, s):
    acc = w_ref[0:1, :] * xpad_ref[SUBLANES - 3:SUBLANES - 3 + s, :]
    for j in range(1, 4):
        acc = acc + w_ref[j:j + 1, :] * xpad_ref[SUBLANES - 3 + j:SUBLANES - 3 + j + s, :]
    return acc


def conv_fwd(x, w, bias, act, nb, *, name):
    t, c = x.shape
    s = t // nb
    cw = GROUP_W

    def body(x_ref, w_ref, b_ref, y_ref, xpad):
        xpad[0:SUBLANES, :] = jnp.zeros((SUBLANES, cw), f32)
        xpad[SUBLANES:, :] = x_ref[...]
        acc = _conv_taps(xpad, w_ref, s) + b_ref[...]
        y_ref[...] = _silu(acc) if act else acc

    slab = pl.BlockSpec((s, cw), lambda b, g: (b, g))
    return pl.pallas_call(
        body, grid=(nb, c // cw),
        in_specs=[slab, pl.BlockSpec((4, cw), lambda b, g: (0, g)), pl.BlockSpec((1, cw), lambda b, g: (0, g))],
        out_specs=slab, out_shape=jax.ShapeDtypeStruct((t, c), f32),
        scratch_shapes=[pltpu.VMEM((s + SUBLANES, cw), f32)],
        compiler_params=_params(("parallel", "parallel")), name=name,
    )(x, w, bias)


def conv_bwd(x, w, bias, dy, act, nb, *, name):
    t, c = x.shape
    s = t // nb
    cw = GROUP_W

    def body(x_ref, w_ref, b_ref, dy_ref, dx_ref, dw_ref, db_ref, xpad, dpad):
        xpad[0:SUBLANES, :] = jnp.zeros((SUBLANES, cw), f32)
        xpad[SUBLANES:, :] = x_ref[...]
        dacc = dy_ref[...]
        if act:
            acc = _conv_taps(xpad, w_ref, s) + b_ref[...]
            sg = jax.nn.sigmoid(acc)
            dacc = dacc * (sg * (1.0 + acc * (1.0 - sg)))
        dpad[0:s, :] = dacc
        dpad[s:, :] = jnp.zeros((SUBLANES, cw), f32)
        dx = w_ref[0:1, :] * dpad[3:3 + s, :]
        for j in range(1, 4):
            dx = dx + w_ref[j:j + 1, :] * dpad[3 - j:3 - j + s, :]
        dx_ref[...] = dx
        first = pl.program_id(1) == 0
        for j in range(4):
            pw = jnp.sum(dacc * xpad[SUBLANES - 3 + j:SUBLANES - 3 + j + s, :], 0, keepdims=True)

            @pl.when(first)
            def _():
                dw_ref[j:j + 1, :] = pw

            @pl.when(jnp.logical_not(first))
            def _():
                dw_ref[j:j + 1, :] += pw

        pb = jnp.sum(dacc, 0, keepdims=True)

        @pl.when(first)
        def _():
            db_ref[...] = pb

        @pl.when(jnp.logical_not(first))
        def _():
            db_ref[...] += pb

    slab = pl.BlockSpec((s, cw), lambda g, b: (b, g))
    wsp = pl.BlockSpec((4, cw), lambda g, b: (0, g))
    bsp = pl.BlockSpec((1, cw), lambda g, b: (0, g))
    return pl.pallas_call(
        body, grid=(c // cw, nb), in_specs=[slab, wsp, bsp, slab], out_specs=[slab, wsp, bsp],
        out_shape=[jax.ShapeDtypeStruct((t, c), f32), jax.ShapeDtypeStruct((4, c), f32),
                   jax.ShapeDtypeStruct((1, c), f32)],
        scratch_shapes=[pltpu.VMEM((s + SUBLANES, cw), f32), pltpu.VMEM((s + SUBLANES, cw), f32)],
        compiler_params=_params(("parallel", "arbitrary")), name=name,
    )(x, w, bias, dy)


def _each(f, *lists):
    return [f(*a) for a in zip(*lists)]


def _attn_heads(qs, kbs, vbs, sinks, valids, dist, dots):
    nn, nt = dots[:2]
    items = range(len(qs))
    kv = [(i // A_HEADS) * A_KV_HEADS + (i % A_HEADS) // A_GROUP for i in items]
    scs = [nt(qs[i], kbs[kv[i]]) for i in items]
    prs = []
    for i in items:
        h = i % A_HEADS
        sc = scs[i] * (A_HEAD_DIM ** -0.5) - 2.0 ** -(h + 1) * dist
        sc = jnp.where(valids[i // A_HEADS], sc, NEG)
        m = lax.stop_gradient(jnp.maximum(jnp.max(sc, -1, keepdims=True), sinks[h]))
        pr = jnp.exp(sc - m)
        den = jnp.sum(pr, -1, keepdims=True) + jnp.exp(sinks[h] - m)
        prs.append(pr / den)
    return [nn(prs[i], vbs[kv[i]]) for i in items]


A_Q_ROWS = 2 * CHUNK
A_STEPS_PER_TRIP = 4


def _attn_steps(s):
    return A_STEPS_PER_TRIP if s % (A_Q_ROWS * A_STEPS_PER_TRIP) == 0 else 1


def _attn_band_consts(r0):
    band = A_WINDOW + A_Q_ROWS
    qi = lax.broadcasted_iota(jnp.int32, (A_Q_ROWS, band), 0)
    kj = lax.broadcasted_iota(jnp.int32, (A_Q_ROWS, band), 1)
    dist = jnp.abs(qi + A_WINDOW - kj).astype(f32)
    qc, kc = qi // CHUNK, kj // CHUNK
    valid = ((kj + r0) >= A_WINDOW) & (kc >= qc) & (kc <= qc + A_WINDOW // CHUNK)
    return dist, valid


def attn_fwd(qkv, sinks, nb, *, name):
    t = qkv.shape[0]
    s = t // nb
    band = A_WINDOW + A_Q_ROWS
    hd = A_HEAD_DIM

    def body(qkv_ref, sink_ref, o_ref, kvpad):
        kvpad[0:A_WINDOW, :] = jnp.zeros((A_WINDOW, 2 * A_KV_WIDTH), f32)
        kvpad[A_WINDOW:, :] = qkv_ref[:, A_WIDTH:]

        def trip(n, carry):
            r0s = [pl.multiple_of((n * steps + j) * A_Q_ROWS, A_Q_ROWS) for j in range(steps)]
            consts = [_attn_band_consts(r0) for r0 in r0s]
            kbs = [kvpad[pl.ds(r0, band), kvh * hd:(kvh + 1) * hd] for r0 in r0s for kvh in range(A_KV_HEADS)]
            vbs = [kvpad[pl.ds(r0, band), A_KV_WIDTH + kvh * hd:A_KV_WIDTH + (kvh + 1) * hd]
                   for r0 in r0s for kvh in range(A_KV_HEADS)]
            qs = [qkv_ref[pl.ds(r0, A_Q_ROWS), h * hd:(h + 1) * hd] for r0 in r0s for h in range(A_HEADS)]
            outs = _attn_heads(qs, kbs, vbs, [sink_ref[:, h:h + 1] for h in range(A_HEADS)], [c_[1] for c_ in consts],
                               consts[0][0], RAW_DOTS)
            for j, r0 in enumerate(r0s):
                for h in range(A_HEADS):
                    o_ref[pl.ds(r0, A_Q_ROWS), h * hd:(h + 1) * hd] = outs[j * A_HEADS + h]
            return carry

        steps = _attn_steps(s)
        lax.fori_loop(0, s // (A_Q_ROWS * steps), trip, 0)

    return pl.pallas_call(
        body, grid=(nb,),
        in_specs=[pl.BlockSpec((s, A_WIDTH + 2 * A_KV_WIDTH), lambda b: (b, 0)),
                  pl.BlockSpec((1, A_HEADS), lambda b: (0, 0))],
        out_specs=pl.BlockSpec((s, A_WIDTH), lambda b: (b, 0)),
        out_shape=jax.ShapeDtypeStruct((t, A_WIDTH), f32),
        scratch_shapes=[pltpu.VMEM((s + A_WINDOW, 2 * A_KV_WIDTH), f32)],
        compiler_params=_params(("parallel",)), name=name,
    )(qkv, sinks)


def attn_bwd(qkv, sinks, do, nb, *, name):
    t = qkv.shape[0]
    s = t // nb
    band = A_WINDOW + A_Q_ROWS
    hd = A_HEAD_DIM
    kvw = 2 * A_KV_WIDTH

    def body(qkv_ref, sink_ref, do_ref, dqkv_ref, dsink_ref, kvpad, dkvpad):
        kvpad[0:A_WINDOW, :] = jnp.zeros((A_WINDOW, kvw), f32)
        kvpad[A_WINDOW:, :] = qkv_ref[:, A_WIDTH:]
        dkvpad[...] = jnp.zeros((s + A_WINDOW, kvw), f32)

        def trip(n, dsinks):
            r0s = [pl.multiple_of((n * steps + j) * A_Q_ROWS, A_Q_ROWS) for j in range(steps)]
            consts = [_attn_band_consts(r0) for r0 in r0s]
            ksl = [slice(kvh * hd, (kvh + 1) * hd) for kvh in range(A_KV_HEADS)]
            vsl = [slice(A_KV_WIDTH + kvh * hd, A_KV_WIDTH + (kvh + 1) * hd) for kvh in range(A_KV_HEADS)]
            kbs = [kvpad[pl.ds(r0, band), sl] for r0 in r0s for sl in ksl]
            vbs = [kvpad[pl.ds(r0, band), sl] for r0 in r0s for sl in vsl]
            qs = [qkv_ref[pl.ds(r0, A_Q_ROWS), h * hd:(h + 1) * hd] for r0 in r0s for h in range(A_HEADS)]
            dos = [do_ref[pl.ds(r0, A_Q_ROWS), h * hd:(h + 1) * hd] for r0 in r0s for h in range(A_HEADS)]
            fn = functools.partial(_attn_heads, valids=[c_[1] for c_ in consts], dist=consts[0][0], dots=VJP_DOTS)
            _, vjp = jax.vjp(fn, qs, kbs, vbs, [sink_ref[:, h:h + 1] for h in range(A_HEADS)])
            dqs, dks, dvs, dss = vjp(dos)
            for j, r0 in enumerate(r0s):
                for h in range(A_HEADS):
                    dqkv_ref[pl.ds(r0, A_Q_ROWS), h * hd:(h + 1) * hd] = dqs[j * A_HEADS + h]
            for j, r0 in enumerate(r0s):
                for kvh in range(A_KV_HEADS):
                    dkvpad[pl.ds(r0, band), ksl[kvh]] += dks[j * A_KV_HEADS + kvh]
                    dkvpad[pl.ds(r0, band), vsl[kvh]] += dvs[j * A_KV_HEADS + kvh]
            return tuple(dsinks[h] + dss[h] for h in range(A_HEADS))

        steps = _attn_steps(s)
        dsinks = lax.fori_loop(0, s // (A_Q_ROWS * steps), trip, tuple(jnp.zeros((1, 1), f32) for _ in range(A_HEADS)))
        dqkv_ref[:, A_WIDTH:] = dkvpad[A_WINDOW:, :]
        first = pl.program_id(0) == 0
        for h in range(A_HEADS):
            @pl.when(first)
            def _():
                dsink_ref[:, h:h + 1] = dsinks[h]

            @pl.when(jnp.logical_not(first))
            def _():
                dsink_ref[:, h:h + 1] += dsinks[h]

    wq = A_WIDTH + kvw
    return pl.pallas_call(
        body, grid=(nb,),
        in_specs=[pl.BlockSpec((s, wq), lambda b: (b, 0)), pl.BlockSpec((1, A_HEADS), lambda b: (0, 0)),
                  pl.BlockSpec((s, A_WIDTH), lambda b: (b, 0))],
        out_specs=[pl.BlockSpec((s, wq), lambda b: (b, 0)), pl.BlockSpec((1, A_HEADS), lambda b: (0, 0))],
        out_shape=[jax.ShapeDtypeStruct((t, wq), f32), jax.ShapeDtypeStruct((1, A_HEADS), f32)],
        scratch_shapes=[pltpu.VMEM((s + A_WINDOW, kvw), f32), pltpu.VMEM((s + A_WINDOW, kvw), f32)],
        compiler_params=_params(("arbitrary",)), name=name,
    )(qkv, sinks, do)


def _rg_gates(xc, wa, wx, ba, bx, lam, nn):
    r = jax.nn.sigmoid(nn(xc, wa) + ba)
    i = jax.nn.sigmoid(nn(xc, wx) + bx)
    log_a = -RG_C * r * jax.nn.softplus(-lam)
    a = jnp.exp(log_a)
    mult = jnp.sqrt(-jnp.tanh(log_a) * (jnp.exp(2.0 * log_a) + 1.0))
    return a, mult * (i * xc)


def _linear_scan(a, u, reverse):
    s = a.shape[0]
    t = lax.broadcasted_iota(jnp.int32, a.shape, 0)
    d = 1
    while d < s:
        if reverse:
            keep = t < s - d
            shift = s - d
        else:
            keep = t >= d
            shift = d
        us = jnp.where(keep, pltpu.roll(u, shift, 0), 0.0)
        as_ = jnp.where(keep, pltpu.roll(a, shift, 0), 1.0)
        u = u + a * us
        a = a * as_
        d *= 2
    return u


def rglru_fwd(xc, bg, wa, wx, ba, bx, lam, nb, *, name):
    t, c = xc.shape
    s = t // nb
    cw = GROUP_W

    def body(xc_ref, bg_ref, wa_ref, wx_ref, ba_ref, bx_ref, lam_ref, y_ref, h_ref):
        a, u = _rg_gates(xc_ref[...], wa_ref[...], wx_ref[...], ba_ref[...], bx_ref[...], lam_ref[...], RAW_DOTS[0])
        h = _linear_scan(a, u, False)
        h_ref[...] = h
        y_ref[...] = h * jax.nn.gelu(bg_ref[...])

    slab = pl.BlockSpec((s, cw), lambda b, g: (b, g))
    wsp = pl.BlockSpec((None, cw, cw), lambda b, g: (g, 0, 0))
    vec = pl.BlockSpec((1, cw), lambda b, g: (0, g))
    return pl.pallas_call(
        body, grid=(nb, c // cw), in_specs=[slab, slab, wsp, wsp, vec, vec, vec], out_specs=[slab, slab],
        out_shape=[jax.ShapeDtypeStruct((t, c), f32)] * 2,
        compiler_params=_params(("parallel", "parallel")), name=name,
    )(xc, bg, wa, wx, ba, bx, lam)


def rglru_bwd(xc, bg, h, dy, wa, wx, ba, bx, lam, nb, *, name):
    t, c = xc.shape
    s = t // nb
    cw = GROUP_W

    def body(xc_ref, bg_ref, h_ref, dy_ref, wa_ref, wx_ref, ba_ref, bx_ref, lam_ref,
             dxc_ref, dbg_ref, dwa_ref, dwx_ref, dba_ref, dbx_ref, dlam_ref):
        h = h_ref[...]
        dy_ = dy_ref[...]
        gel, gel_vjp = jax.vjp(jax.nn.gelu, bg_ref[...])
        dbg_ref[...] = gel_vjp(dy_ * h)[0]
        dh = dy_ * gel
        gates = functools.partial(_rg_gates, nn=_bnn)
        (a, _), gates_vjp = jax.vjp(gates, xc_ref[...], wa_ref[...], wx_ref[...], ba_ref[...], bx_ref[...],
                                    lam_ref[...])
        ti = lax.broadcasted_iota(jnp.int32, a.shape, 0)
        a_next = jnp.where(ti < s - 1, pltpu.roll(a, s - 1, 0), 0.0)
        lam_t = _linear_scan(a_next, dh, True)
        h_prev = jnp.where(ti >= 1, pltpu.roll(h, 1, 0), 0.0)
        dxc, dwa, dwx, dba, dbx, dlam = gates_vjp((lam_t * h_prev, lam_t))
        dxc_ref[...] = dxc
        first = pl.program_id(1) == 0

        @pl.when(first)
        def _():
            dwa_ref[...] = dwa
            dwx_ref[...] = dwx
            dba_ref[...] = dba
            dbx_ref[...] = dbx
            dlam_ref[...] = dlam

        @pl.when(jnp.logical_not(first))
        def _():
            dwa_ref[...] += dwa
            dwx_ref[...] += dwx
            dba_ref[...] += dba
            dbx_ref[...] += dbx
            dlam_ref[...] += dlam

    slab = pl.BlockSpec((s, cw), lambda g, b: (b, g))
    wsp = pl.BlockSpec((None, cw, cw), lambda g, b: (g, 0, 0))
    vec = pl.BlockSpec((1, cw), lambda g, b: (0, g))
    ng = c // cw
    return pl.pallas_call(
        body, grid=(ng, nb), in_specs=[slab, slab, slab, slab, wsp, wsp, vec, vec, vec],
        out_specs=[slab, slab, wsp, wsp, vec, vec, vec],
        out_shape=[jax.ShapeDtypeStruct((t, c), f32), jax.ShapeDtypeStruct((t, c), f32),
                   jax.ShapeDtypeStruct((ng, cw, cw), f32), jax.ShapeDtypeStruct((ng, cw, cw), f32),
                   jax.ShapeDtypeStruct((1, c), f32), jax.ShapeDtypeStruct((1, c), f32),
                   jax.ShapeDtypeStruct((1, c), f32)],
        compiler_params=_params(("parallel", "arbitrary")), name=name,
    )(xc, bg, h, dy, wa, wx, ba, bx, lam)


def _gdn_chunks_prep(qs, ks, vs, bls, als, a_log, dt_b, dots):
    nn, nt, csum = dots[0], dots[1], dots[3]
    hd = C_HEAD_DIM
    ri = lax.broadcasted_iota(jnp.int32, (CHUNK, CHUNK), 0)
    ci = lax.broadcasted_iota(jnp.int32, (CHUNK, CHUNK), 1)
    tril = ri >= ci
    strict = ri > ci
    eye = (ri == ci).astype(f32)
    qn = [q * lax.rsqrt(jnp.sum(q * q, -1, keepdims=True) + NORM_EPS) * (hd ** -0.5) for q in qs]
    kn = [k * lax.rsqrt(jnp.sum(k * k, -1, keepdims=True) + NORM_EPS) for k in ks]
    beta = [jax.nn.sigmoid(bl) for bl in bls]
    g = [-jnp.exp(a_log) * jax.nn.softplus(al + dt_b) for al in als]
    gc_sq = [csum(jnp.broadcast_to(g_, (CHUNK, CHUNK))) for g_ in g]
    gc = [csum(jnp.broadcast_to(g_, (CHUNK, hd))) for g_ in g]
    decay = [jnp.where(tril, jnp.exp(jnp.where(tril, s - s.T, 0.0)), 0.0) for s in gc_sq]
    kb = _each(jnp.multiply, kn, beta)
    kk = _each(nt, kb, kn)
    pw = [-jnp.where(strict, a * d, 0.0) for a, d in zip(kk, decay)]
    inv = [eye + p_ for p_ in pw]
    for _ in range(5):
        pw = _each(nn, pw, pw)
        inv = _each(jnp.add, inv, _each(nn, inv, pw))
    egc = [jnp.exp(c_) for c_ in gc]
    u = _each(nn, inv, _each(jnp.multiply, vs, beta))
    w = _each(nn, inv, _each(jnp.multiply, kb, egc))
    attn = _each(jnp.multiply, _each(nt, qn, kn), decay)
    g_last = [jnp.sum(jnp.broadcast_to(g_, (CHUNK, hd)), 0, keepdims=True) for g_ in g]
    qg = _each(jnp.multiply, qn, egc)
    kdec = [k_ * jnp.exp(gl_ - c_) for k_, gl_, c_ in zip(kn, g_last, gc)]
    return [(qg[i], kdec[i], w[i], u[i], attn[i], jnp.exp(g_last[i])) for i in range(len(qs))]


def _gdn_heads_step(states, qgs, kdecs, ws, us, attns, gls, zs, ng, dots):
    nn, tn = dots[0], dots[2]
    v_new = _each(jnp.subtract, us, _each(nn, ws, states))
    o = _each(jnp.add, _each(nn, qgs, states), _each(nn, attns, v_new))
    new = [s * gl for s, gl in zip(states, gls)]
    new = _each(jnp.add, new, _each(tn, kdecs, v_new))
    y = [o_ * lax.rsqrt(jnp.mean(o_ * o_, -1, keepdims=True) + NORM_EPS) * ng * _silu(z) for o_, z in zip(o, zs)]
    return y, new


def _loop_unrolled(n, unroll, load, compute, store, init):
    u = unroll if n % unroll == 0 else 1

    def trip(i, carry):
        idx = [i * u + j for j in range(u)]
        loaded = [load(k) for k in idx]
        results = compute(loaded)
        for k, r in zip(idx, results):
            carry = store(k, r, carry)
        return carry

    return lax.fori_loop(0, n // u, trip, init)


def _pick_lane(x, lane):
    li = lax.broadcasted_iota(jnp.int32, x.shape, 1)
    return jnp.sum(jnp.where(li == lane, x, 0.0), 1, keepdims=True)


def _put_lane(col, lane, width):
    li = lax.broadcasted_iota(jnp.int32, (col.shape[0], width), 1)
    return jnp.where(li == lane, col, 0.0)


def _gdn_specs(s, nc):
    hd = C_HEAD_DIM
    head = lambda off: pl.BlockSpec((s, hd), lambda b, h, off=off: (b, off + h))
    attn = pl.BlockSpec((None, s, CHUNK), lambda b, h: (h, b, 0))
    gl = pl.BlockSpec((None, nc * SUBLANES, hd), lambda b, h: (h, b, 0))
    ba = pl.BlockSpec((s, LANES), lambda b, h: (b, 0))
    sc8 = pl.BlockSpec((1, C_HEADS), lambda b, h: (0, 0))
    return head, attn, gl, ba, sc8


def gdn_prep_fwd(qkv, ba, a_log, dt_b, nb, *, name):
    t = qkv.shape[0]
    s = t // nb
    nc = s // CHUNK
    hd = C_HEAD_DIM
    head, attn_sp, gl_sp, ba_sp, sc8 = _gdn_specs(s, nc)

    def body(q_ref, k_ref, v_ref, ba_ref, alog_ref, dtb_ref, qg_ref, kd_ref, w_ref, u_ref, at_ref, gl_ref):
        h = pl.program_id(1)
        a_log_h = _pick_lane(alog_ref[...], h)
        dt_b_h = _pick_lane(dtb_ref[...], h)

        def load(n):
            rows = pl.ds(pl.multiple_of(n * CHUNK, CHUNK), CHUNK)
            bav = ba_ref[rows, :]
            return q_ref[rows, :], k_ref[rows, :], v_ref[rows, :], _pick_lane(bav, h), _pick_lane(bav, C_HEADS + h)

        def compute(loaded):
            return _gdn_chunks_prep(*[list(x) for x in zip(*loaded)], a_log_h, dt_b_h, RAW_DOTS)

        def store(n, outs, carry):
            rows = pl.ds(pl.multiple_of(n * CHUNK, CHUNK), CHUNK)
            qg_ref[rows, :] = outs[0].astype(bf16)
            kd_ref[rows, :] = outs[1].astype(bf16)
            w_ref[rows, :] = outs[2].astype(bf16)
            u_ref[rows, :] = outs[3]
            at_ref[rows, :] = outs[4].astype(bf16)
            gl_ref[pl.ds(pl.multiple_of(n * SUBLANES, SUBLANES), SUBLANES), :] = jnp.broadcast_to(outs[5], (SUBLANES, hd))
            return carry

        _loop_unrolled(nc, PREP_FWD_UNROLL, load, compute, store, 0)

    big = jax.ShapeDtypeStruct((t, C_WIDTH), f32)
    bigb = jax.ShapeDtypeStruct((t, C_WIDTH), bf16)
    return pl.pallas_call(
        body, grid=(nb, C_HEADS),
        in_specs=[head(0), head(C_HEADS), head(2 * C_HEADS), ba_sp, sc8, sc8],
        out_specs=[head(0)] * 4 + [attn_sp, gl_sp],
        out_shape=[bigb, bigb, bigb, big, jax.ShapeDtypeStruct((C_HEADS, t, CHUNK), bf16),
                               jax.ShapeDtypeStruct((C_HEADS, nb * nc * SUBLANES, hd), f32)],
        compiler_params=_params(("parallel", "parallel")), name=name,
    )(qkv, qkv, qkv, ba, a_log, dt_b)


def gdn_prep_bwd(qkv, ba, a_log, dt_b, cts, nb, *, name):
    t = qkv.shape[0]
    s = t // nb
    nc = s // CHUNK
    hd = C_HEAD_DIM
    head, attn_sp, gl_sp, ba_sp, sc8 = _gdn_specs(s, nc)

    def body(q_ref, k_ref, v_ref, ba_ref, alog_ref, dtb_ref, cqg, ckd, cw_, cu, cat, cgl,
             dq_ref, dk_ref, dv_ref, dba_ref, dalog_ref, ddtb_ref):
        b = pl.program_id(0)
        h = pl.program_id(1)
        a_log_h = _pick_lane(alog_ref[...], h)
        dt_b_h = _pick_lane(dtb_ref[...], h)
        prep = functools.partial(_gdn_chunks_prep, dots=VJP_DOTS)

        @pl.when(h == 0)
        def _():
            dba_ref[...] = jnp.zeros((s, LANES), f32)

        def load(n):
            rows = pl.ds(pl.multiple_of(n * CHUNK, CHUNK), CHUNK)
            bav = ba_ref[rows, :]
            cgl_n = cgl[pl.ds(pl.multiple_of(n * SUBLANES, SUBLANES), SUBLANES), :][0:1, :]
            primals = (q_ref[rows, :], k_ref[rows, :], v_ref[rows, :], _pick_lane(bav, h), _pick_lane(bav, C_HEADS + h))
            return primals, (cqg[rows, :], ckd[rows, :], cw_[rows, :], cu[rows, :], cat[rows, :], cgl_n), dba_ref[rows, :]

        def compute(loaded):
            primals = [list(x) for x in zip(*[item[0] for item in loaded])]
            _, vjp = jax.vjp(prep, *primals, a_log_h, dt_b_h)
            dqs, dks, dvs, dbls, dals, dalog, ddtb = vjp([item[1] for item in loaded])
            zero = jnp.zeros((1, 1), f32)
            return [((dqs[i], dks[i], dvs[i], dbls[i], dals[i], dalog if i == 0 else zero, ddtb if i == 0 else zero),
                     loaded[i][2]) for i in range(len(loaded))]

        def store(n, res, carry):
            (dq, dk, dv, dbl, dal, dalog_n, ddtb_n), dba_old = res
            rows = pl.ds(pl.multiple_of(n * CHUNK, CHUNK), CHUNK)
            dq_ref[rows, :] = dq
            dk_ref[rows, :] = dk
            dv_ref[rows, :] = dv
            dba_ref[rows, :] = dba_old + _put_lane(dbl, h, LANES) + _put_lane(dal, C_HEADS + h, LANES)
            return carry[0] + dalog_n, carry[1] + ddtb_n

        da_log, ddt_b = _loop_unrolled(nc, PREP_BWD_UNROLL, load, compute, store,
                                       (jnp.zeros((1, 1), f32), jnp.zeros((1, 1), f32)))
        first = jnp.logical_and(b == 0, h == 0)

        @pl.when(first)
        def _():
            dalog_ref[...] = _put_lane(da_log, h, LANES)
            ddtb_ref[...] = _put_lane(ddt_b, h, LANES)

        @pl.when(jnp.logical_not(first))
        def _():
            dalog_ref[...] += _put_lane(da_log, h, LANES)
            ddtb_ref[...] += _put_lane(ddt_b, h, LANES)

    big = jax.ShapeDtypeStruct((t, C_WIDTH), f32)
    vec = pl.BlockSpec((1, LANES), lambda b, h: (0, 0))
    return pl.pallas_call(
        body, grid=(nb, C_HEADS),
        in_specs=[head(0), head(C_HEADS), head(2 * C_HEADS), ba_sp, sc8, sc8] + [head(0)] * 4 + [attn_sp, gl_sp],
        out_specs=[head(0)] * 3 + [ba_sp, vec, vec],
        out_shape=[big] * 3 + [jax.ShapeDtypeStruct((t, LANES), f32), jax.ShapeDtypeStruct((1, LANES), f32),
                               jax.ShapeDtypeStruct((1, LANES), f32)],
        compiler_params=_params(("arbitrary", "arbitrary")), name=name,
    )(qkv, qkv, qkv, ba, a_log, dt_b, *cts)


def _gdn_rec_specs(sb, nsb, hp, reverse):
    hd = C_HEAD_DIM
    ncb = sb // CHUNK
    blk = (lambda b, k: b * nsb + (nsb - 1 - k)) if reverse else (lambda b, k: b * nsb + k)
    wide = pl.BlockSpec((sb, hp * hd), lambda b, j, k: (blk(b, k), j))
    attn = pl.BlockSpec((hp, sb, CHUNK), lambda b, j, k: (j, blk(b, k), 0))
    gl = pl.BlockSpec((hp, ncb * SUBLANES, hd), lambda b, j, k: (j, blk(b, k), 0))
    ng = pl.BlockSpec((1, hd), lambda b, j, k: (0, 0))
    states = pl.BlockSpec((hp, ncb, hd, hd), lambda b, j, k: (j, blk(b, k), 0, 0))
    return wide, attn, gl, ng, states


def gdn_rec_fwd(qg, kdec, w, u, attn, gl, z, ng, nb, *, name):
    t = qg.shape[0]
    s = t // nb
    sb = min(s, GDN_TIME_BLOCK)
    nsb = s // sb
    hd = C_HEAD_DIM
    hp = C_HEADS_PER_STEP
    wide, attn_sp, gl_sp, ng_sp, st_sp = _gdn_rec_specs(sb, nsb, hp, False)

    def body(qg_ref, kd_ref, w_ref, u_ref, at_ref, gl_ref, z_ref, ng_ref, y_ref, st_ref, carry_ref):
        @pl.when(pl.program_id(2) == 0)
        def _():
            carry_ref[...] = jnp.zeros((hp, hd, hd), f32)

        def chunk(n, states):
            for j in range(hp):
                st_ref[j, n] = states[j]
            rows = pl.ds(pl.multiple_of(n * CHUNK, CHUNK), CHUNK)
            grow = pl.ds(pl.multiple_of(n * SUBLANES, SUBLANES), SUBLANES)
            cols = [slice(j * hd, (j + 1) * hd) for j in range(hp)]
            ins = [(qg_ref[rows, c], kd_ref[rows, c], w_ref[rows, c], u_ref[rows, c], at_ref[j, rows, :],
                    gl_ref[j, grow, :][0:1, :], z_ref[rows, c]) for j, c in enumerate(cols)]
            ys, new = _gdn_heads_step(list(states), *[list(x) for x in zip(*ins)], ng_ref[...], RAW_DOTS)
            for j in range(hp):
                y_ref[rows, cols[j]] = ys[j]
            return tuple(new)

        last = lax.fori_loop(0, sb // CHUNK, chunk, tuple(carry_ref[j] for j in range(hp)))
        for j in range(hp):
            carry_ref[j] = last[j]

    return pl.pallas_call(
        body, grid=(nb, C_HEADS // hp, nsb),
        in_specs=[wide] * 4 + [attn_sp, gl_sp, wide, ng_sp], out_specs=[wide, st_sp],
        out_shape=[jax.ShapeDtypeStruct((t, C_WIDTH), f32), jax.ShapeDtypeStruct((C_HEADS, t // CHUNK, hd, hd), f32)],
        scratch_shapes=[pltpu.VMEM((hp, hd, hd), f32)],
        compiler_params=_params(("parallel", "parallel", "arbitrary")), name=name,
    )(qg, kdec, w, u, attn, gl, z, ng)


def gdn_rec_bwd(qg, kdec, w, u, attn, gl, z, ng, states, dy, nb, *, name):
    t = qg.shape[0]
    s = t // nb
    sb = min(s, GDN_TIME_BLOCK)
    nsb = s // sb
    nc = sb // CHUNK
    hd = C_HEAD_DIM
    hp = C_HEADS_PER_STEP
    wide, attn_sp, gl_sp, ng_sp, st_sp = _gdn_rec_specs(sb, nsb, hp, True)

    def body(qg_ref, kd_ref, w_ref, u_ref, at_ref, gl_ref, z_ref, ng_ref, states, dy_ref,
             dqg_ref, dkd_ref, dw_ref, du_ref, dat_ref, dgl_ref, dz_ref, dng_ref, carry_ref):
        step = functools.partial(_gdn_heads_step, dots=VJP_DOTS)

        @pl.when(pl.program_id(2) == 0)
        def _():
            carry_ref[...] = jnp.zeros((hp, hd, hd), f32)

        def operands(n):
            rows = pl.ds(pl.multiple_of(n * CHUNK, CHUNK), CHUNK)
            grow = pl.ds(pl.multiple_of(n * SUBLANES, SUBLANES), SUBLANES)
            cols = [slice(j * hd, (j + 1) * hd) for j in range(hp)]
            return ([qg_ref[rows, c].astype(f32) for c in cols], [kd_ref[rows, c].astype(f32) for c in cols],
                    [w_ref[rows, c].astype(f32) for c in cols], [u_ref[rows, c] for c in cols],
                    [at_ref[j, rows, :].astype(f32) for j in range(hp)],
                    [gl_ref[j, grow, :][0:1, :] for j in range(hp)], [z_ref[rows, c] for c in cols])

        def bwd_chunk(i, carry):
            n = nc - 1 - i
            rows = pl.ds(pl.multiple_of(n * CHUNK, CHUNK), CHUNK)
            grow = pl.ds(pl.multiple_of(n * SUBLANES, SUBLANES), SUBLANES)
            dsts, dng = carry
            dys = [dy_ref[rows, j * hd:(j + 1) * hd] for j in range(hp)]
            _, vjp = jax.vjp(step, [states[j, n] for j in range(hp)], *operands(n), ng_ref[...])
            dst, dqg, dkd, dw, du, dat, dgl, dz, dng_n = vjp((dys, list(dsts)))
            for j in range(hp):
                cols = slice(j * hd, (j + 1) * hd)
                dqg_ref[rows, cols] = dqg[j]
                dkd_ref[rows, cols] = dkd[j]
                dw_ref[rows, cols] = dw[j]
                du_ref[rows, cols] = du[j]
                dat_ref[j, rows, :] = dat[j]
                dgl_ref[j, grow, :] = jnp.broadcast_to(dgl[j], (SUBLANES, hd))
                dz_ref[rows, cols] = dz[j]
            return tuple(dst), dng + dng_n

        dlast, dng = lax.fori_loop(0, nc, bwd_chunk,
                                   (tuple(carry_ref[j] for j in range(hp)), jnp.zeros((1, hd), f32)))
        for j in range(hp):
            carry_ref[j] = dlast[j]
        first = jnp.logical_and(jnp.logical_and(pl.program_id(0) == 0, pl.program_id(1) == 0), pl.program_id(2) == 0)

        @pl.when(first)
        def _():
            dng_ref[...] = dng

        @pl.when(jnp.logical_not(first))
        def _():
            dng_ref[...] += dng

    big = jax.ShapeDtypeStruct((t, C_WIDTH), f32)
    return pl.pallas_call(
        body, grid=(nb, C_HEADS // hp, nsb),
        in_specs=[wide] * 4 + [attn_sp, gl_sp, wide, ng_sp, st_sp, wide],
        out_specs=[wide] * 4 + [attn_sp, gl_sp, wide, ng_sp],
        out_shape=[big] * 4 + [jax.ShapeDtypeStruct(attn.shape, f32), jax.ShapeDtypeStruct(gl.shape, f32), big,
                               jax.ShapeDtypeStruct((1, hd), f32)],
        scratch_shapes=[pltpu.VMEM((hp, hd, hd), f32)],
        compiler_params=_params(("arbitrary", "arbitrary", "arbitrary")), name=name,
    )(qg, kdec, w, u, attn, gl, z, ng, states, dy)


def _blockdiag_slabs(w):
    per = GROUP_W // B_BLOCK
    slabs = jnp.zeros((B_BLOCKS // per, GROUP_W, GROUP_W), w.dtype)
    for h in range(B_BLOCKS):
        o = (h % per) * B_BLOCK
        slabs = slabs.at[h // per, o:o + B_BLOCK, o:o + B_BLOCK].set(w[h])
    return slabs


def _slab_blocks(slabs):
    per = GROUP_W // B_BLOCK
    return jnp.stack([slabs[h // per, (h % per) * B_BLOCK:(h % per + 1) * B_BLOCK,
                            (h % per) * B_BLOCK:(h % per + 1) * B_BLOCK] for h in range(B_BLOCKS)])


def _mixer_ab_fwd(x1, x1b, W, g, b, nb, tag):
    w_in = W["ab_w_in"][0].astype(bf16)
    o1, o2 = A_WIDTH + 2 * A_KV_WIDTH, A_WIDTH + 2 * A_KV_WIDTH + B_WIDTH
    w_qkv, w_bx, w_bg = w_in[:, :o1], w_in[:, o1:o2], w_in[:, o2:]
    pqkv = mm_nn(x1b,w_qkv, name=tag + "_in_qkv")
    pbx = mm_nn(x1b,w_bx, name=tag + "_in_bx")
    pbg = mm_nn(x1b,w_bg, name=tag + "_in_bg")
    ya = attn_fwd(pqkv, W["a_sinks"], nb, name=tag + "_attn_fwd")
    xc = conv_fwd(pbx, W["b_conv_w"][0], W["b_conv_b"], False, nb, name=tag + "_conv_fwd")
    wa_s, wx_s = _blockdiag_slabs(W["b_wa"][0]), _blockdiag_slabs(W["b_wx"][0])
    yb, hh = rglru_fwd(xc, pbg, wa_s, wx_s, W["b_ba"], W["b_bx"], W["b_lam"], nb, name=tag + "_rglru_fwd")
    w_out = W["ab_w_out"][0].astype(bf16)
    x2, z1, x2b = proj_ln([ya, yb], [w_out[:A_WIDTH], w_out[A_WIDTH:]], x1, g, b, name=tag + "_out_ln")
    saved = (pqkv, pbx, pbg, ya, xc, yb, hh, wa_s, wx_s, w_qkv, w_bx, w_bg, w_out)
    return x2, x2b, z1, saved


def _mixer_ab_bwd(x1b, dz1, dz1b, W, saved, nb, tag):
    pqkv, pbx, pbg, ya, xc, yb, hh, wa_s, wx_s, w_qkv, w_bx, w_bg, w_out = saved
    dya = mm_nn(dz1b, w_out[:A_WIDTH].T, name=tag + "_dya")
    dyb = mm_nn(dz1b, w_out[A_WIDTH:].T, name=tag + "_dyb")
    dwo = jnp.concatenate([mm_tn(ya, dz1b, name=tag + "_dwo_a"), mm_tn(yb, dz1b, name=tag + "_dwo_b")], 0)
    dpqkv, dsinks = attn_bwd(pqkv, W["a_sinks"], dya, nb, name=tag + "_attn_bwd")
    dxc, dpbg, dwa_s, dwx_s, dba, dbx, dlam = rglru_bwd(xc, pbg, hh, dyb, wa_s, wx_s, W["b_ba"], W["b_bx"],
                                                       W["b_lam"], nb, name=tag + "_rglru_bwd")
    dpbx, dconv_w, dconv_b = conv_bwd(pbx, W["b_conv_w"][0], W["b_conv_b"], dxc, False, nb, name=tag + "_conv_bwd")
    dw_in = jnp.concatenate([mm_tn(x1b,dpqkv, name=tag + "_dwin_qkv"), mm_tn(x1b,dpbx, name=tag + "_dwin_bx"),
                             mm_tn(x1b,dpbg, name=tag + "_dwin_bg")], 1)
    dx1 = mm_nn(dpqkv, w_qkv.T, add=dz1, add_scale=DN_ALPHA, name=tag + "_dx_qkv")
    dx1 = mm_nn(dpbx, w_bx.T, add=dx1, name=tag + "_dx_bx")
    dx1 = mm_nn(dpbg, w_bg.T, add=dx1, name=tag + "_dx_bg")
    grads = {"ab_w_in": dw_in[None], "a_sinks": dsinks, "b_conv_w": dconv_w[None], "b_conv_b": dconv_b,
             "b_wa": _slab_blocks(dwa_s)[None], "b_ba": dba, "b_wx": _slab_blocks(dwx_s)[None], "b_bx": dbx,
             "b_lam": dlam, "ab_w_out": dwo[None]}
    return dx1, grads


def _mixer_c_fwd(x1, x1b, W, g, b, nb, tag):
    w_in = W["c_w_in"][0].astype(bf16)
    d = w_in.shape[0]
    o1, o2 = 3 * C_WIDTH, 4 * C_WIDTH
    w_qkv, w_z = w_in[:, :o1], w_in[:, o1:o2]
    w_ba = jnp.concatenate([w_in[:, o2:], jnp.zeros((d, LANES - 2 * C_HEADS), bf16)], 1)
    pqkv = mm_nn(x1b,w_qkv, name=tag + "_in_qkv")
    pz = mm_nn(x1b,w_z, name=tag + "_in_z")
    pba = mm_nn(x1b,w_ba, name=tag + "_in_ba")
    zero_b = jnp.zeros((1, o1), f32)
    qkvc = conv_fwd(pqkv, W["c_conv_w"][0], zero_b, True, nb, name=tag + "_conv_fwd")
    prep = gdn_prep_fwd(qkvc, pba, W["c_a_log"], W["c_dt_bias"], nb, name=tag + "_prep_fwd")
    yc, states = gdn_rec_fwd(*prep, pz, W["c_norm_g"], nb, name=tag + "_rec_fwd")
    w_out = W["c_w_out"][0].astype(bf16)
    x2, z1, x2b = proj_ln([yc], [w_out], x1, g, b, name=tag + "_out_ln")
    saved = (pqkv, pz, pba, qkvc, prep, states, yc, w_qkv, w_z, w_ba, w_out, zero_b)
    return x2, x2b, z1, saved


def _mixer_c_bwd(x1b, dz1, dz1b, W, saved, nb, tag):
    pqkv, pz, pba, qkvc, prep, states, yc, w_qkv, w_z, w_ba, w_out, zero_b = saved
    dyc = mm_nn(dz1b, w_out.T, name=tag + "_dyc")
    dwo = mm_tn(yc, dz1b, name=tag + "_dwo")
    rec = gdn_rec_bwd(*prep, pz, W["c_norm_g"], states, dyc, nb, name=tag + "_rec_bwd")
    cts, dpz, dng = rec[:6], rec[6], rec[7]
    dq, dk, dv, dpba, dalog, ddtb = gdn_prep_bwd(qkvc, pba, W["c_a_log"], W["c_dt_bias"], cts, nb,
                                                 name=tag + "_prep_bwd")
    dqkvc = jnp.concatenate([dq, dk, dv], 1)
    dpqkv, dconv_w, _ = conv_bwd(pqkv, W["c_conv_w"][0], zero_b, dqkvc, True, nb, name=tag + "_conv_bwd")
    dw_in = jnp.concatenate([mm_tn(x1b,dpqkv, name=tag + "_dwin_qkv"), mm_tn(x1b,dpz, name=tag + "_dwin_z"),
                             mm_tn(x1b,dpba, name=tag + "_dwin_ba")[:, :2 * C_HEADS]], 1)
    dx1 = mm_nn(dpqkv, w_qkv.T, add=dz1, add_scale=DN_ALPHA, name=tag + "_dx_qkv")
    dx1 = mm_nn(dpz, w_z.T, add=dx1, name=tag + "_dx_z")
    dx1 = mm_nn(dpba, w_ba.T, add=dx1, name=tag + "_dx_ba")
    grads = {"c_w_in": dw_in[None], "c_conv_w": dconv_w[None], "c_a_log": dalog[:, :C_HEADS],
             "c_dt_bias": ddtb[:, :C_HEADS], "c_norm_g": dng, "c_w_out": dwo[None]}
    return dx1, grads


def _local_step(x, p, target, W, F, on_ffn_grads):
    nb, s, d = x.shape
    t = nb * s
    h = x.reshape(t, d)
    hb = h.astype(bf16)
    tape = []
    for i in range(DEPTH):
        tag = f"l{i}"
        f1 = [F[k][i] for k in ("ffn1_wg", "ffn1_wu", "ffn1_wd")]
        f2 = [F[k][i] for k in ("ffn2_wg", "ffn2_wu", "ffn2_wd")]
        lg = [W["ln_g"][i, k][None] for k in range(3)]
        lb = [W["ln_b"][i, k][None] for k in range(3)]
        x1, z0, x1b = ffn_fwd(h, *f1, lg[0], lb[0], name=tag + "_ffn1_fwd")
        mixer = _mixer_ab_fwd if i % 2 == 0 else _mixer_c_fwd
        x2, x2b, z1, msaved = mixer(x1, x1b, W, lg[1], lb[1], nb, tag + "_mix")
        x3, z2, _ = ffn_fwd(x2, *f2, lg[2], lb[2], name=tag + "_ffn2_fwd")
        pi = p[i].reshape(t, -1)
        pw = (W["ple_wg"][i].astype(bf16), W["ple_bg"][i][None], W["ple_wp"][i].astype(bf16))
        x4, x4b = ple_fwd(x3, pi, *pw, name=tag + "_ple_fwd")
        tape.append((hb, z0, x1b, msaved, z1, x2b, z2, x3, pi, pw, lg))
        h, hb = x4, x4b
    dh, sq = loss_head(h, target.reshape(t, d), name="loss_head")
    loss = 0.5 * jnp.sum(sq) / d
    per_layer = [None] * DEPTH
    grads = {}
    for i in reversed(range(DEPTH)):
        tag = f"l{i}"
        hb_in, z0, x1b, msaved, z1, x2b, z2, x3, pi, pw, lg = tape[i]
        dz2, dz2b, dple_wg, dple_bg, dple_wp, dg2, db2 = ple_bwd(x3, pi, dh, pw[0], pw[0].T, pw[1], pw[2], z2, lg[2],
                                                                 name=tag + "_ple_ln2_bwd")
        f1 = [F[k][i] for k in ("ffn1_wg", "ffn1_wu", "ffn1_wd")]
        f2 = [F[k][i] for k in ("ffn2_wg", "ffn2_wu", "ffn2_wd")]
        dgate, dup, *df2 = ffn_bwd_weights(x2b, dz2b, *f2, name=tag + "_ffn2_bwd_w")
        on_ffn_grads(i, 3, df2)
        dx2 = ffn_bwd_input(dgate, dup, f2[0], f2[1], dz2, name=tag + "_ffn2_bwd_x")
        dz1, dz1b, dg1, db1 = ln_bwd(z1, dx2, lg[1], name=tag + "_ln1_bwd")
        mixer_bwd = _mixer_ab_bwd if i % 2 == 0 else _mixer_c_bwd
        dx1, mgrads = mixer_bwd(x1b, dz1, dz1b, W, msaved, nb, tag + "_mix")
        grads.update(mgrads)
        dz0, dz0b, dg0, db0 = ln_bwd(z0, dx1, lg[0], name=tag + "_ln0_bwd")
        dgate, dup, *df1 = ffn_bwd_weights(hb_in, dz0b, *f1, name=tag + "_ffn1_bwd_w")
        on_ffn_grads(i, 0, df1)
        dh = ffn_bwd_input(dgate, dup, f1[0], f1[1], dz0, name=tag + "_ffn1_bwd_x")
        per_layer[i] = {"ln_g": jnp.concatenate([dg0, dg1, dg2], 0), "ln_b": jnp.concatenate([db0, db1, db2], 0),
                        "ple_wg": dple_wg, "ple_bg": dple_bg[0], "ple_wp": dple_wp}
    for k in per_layer[0]:
        grads[k] = jnp.stack([per_layer[i][k] for i in range(DEPTH)])
    return loss, dh.reshape(nb, s, d), grads


WEIGHT_NAMES = ("ffn1_wg", "ffn1_wu", "ffn1_wd", "ffn2_wg", "ffn2_wu", "ffn2_wd", "ln_g", "ln_b", "ple_wg", "ple_bg",
                "ple_wp", "ab_w_in", "a_sinks", "b_conv_w", "b_conv_b", "b_wa", "b_ba", "b_wx", "b_bx", "b_lam",
                "ab_w_out", "c_w_in", "c_conv_w", "c_a_log", "c_dt_bias", "c_norm_g", "c_w_out")
NATIVE_NAMES = WEIGHT_NAMES[:6]
PACKED_NAMES = WEIGHT_NAMES[6:]
PACK_MATRICES = ("ple_wg", "ple_wp", "ab_w_in", "ab_w_out", "c_w_in", "c_w_out")
PACK_GROUPS = (tuple(k for k in PACKED_NAMES if k not in PACK_MATRICES), PACK_MATRICES)
PACK_TRANSIT = (f32, bf16)
SHARD_AXIS = {"ffn1_wg": 2, "ffn1_wu": 2, "ffn1_wd": 1, "ffn2_wg": 2, "ffn2_wu": 2, "ffn2_wd": 1, "ln_g": 2, "ln_b": 2,
              "ple_wg": 1, "ple_wp": 2, "ab_w_in": 2, "b_conv_w": 2, "ab_w_out": 1, "c_w_in": 2, "c_conv_w": 2,
              "c_w_out": 1}
N_CHIPS = 4
PACK_COLS = LANES
PACK_TILE_MULTIPLE = 256
ELEMENTWISE_BLOCK_ELEMS = 128 * 1024


def _row_tile(r, cols):
    return _tile(r, max(2 * SUBLANES, ELEMENTWISE_BLOCK_ELEMS // cols), 2 * SUBLANES)
MESH = pl.DeviceIdType.MESH
ANY = pl.BlockSpec(memory_space=pl.ANY)


def _tiled_dims(shape):
    w = shape[-1]
    r = 1
    for dim in shape[:-1]:
        r *= dim
    return r, w, -(-r // SUBLANES) * SUBLANES, -(-w // LANES) * LANES


def _pack(pieces, lead=()):
    k = len(lead)
    tiles = []
    for a in pieces:
        r, w, rp, wp = _tiled_dims(a.shape[k:])
        a2 = jnp.pad(a.reshape(lead + (r, w)), [(0, 0)] * k + [(0, rp - r), (0, wp - w)])
        a2 = a2.reshape(lead + (rp // SUBLANES, SUBLANES, wp // LANES, LANES))
        a2 = jnp.swapaxes(a2, k + 1, k + 2)
        tiles.append(a2.reshape(lead + (-1, SUBLANES, LANES)))
    flat = jnp.concatenate(tiles, axis=k)
    n = flat.shape[k]
    n_pad = -(-n // PACK_TILE_MULTIPLE) * PACK_TILE_MULTIPLE
    flat = jnp.pad(flat, [(0, 0)] * k + [(0, n_pad - n), (0, 0), (0, 0)])
    return flat.reshape(lead + (n_pad * SUBLANES, PACK_COLS))


def _unpack(pack, shapes, lead=()):
    k = len(lead)
    flat = pack.reshape(lead + (-1, SUBLANES, LANES))
    out, o = [], 0
    for shp in shapes:
        r, w, rp, wp = _tiled_dims(shp)
        n = (rp // SUBLANES) * (wp // LANES)
        a2 = lax.slice_in_dim(flat, o, o + n, axis=k).reshape(lead + (rp // SUBLANES, wp // LANES, SUBLANES, LANES))
        a2 = jnp.swapaxes(a2, k + 1, k + 2).reshape(lead + (rp, wp))
        a2 = lax.slice_in_dim(lax.slice_in_dim(a2, 0, r, axis=k), 0, w, axis=k + 1)
        out.append(a2.reshape(lead + tuple(shp)))
        o += n
    return out


def _mesh_position():
    x, y, c = lax.axis_index("x"), lax.axis_index("y"), lax.axis_index("c")
    chips = [(1 - x, y), (x, 1 - y), (1 - x, 1 - y)]
    return x, y, c, chips


def _remote(src, dst, send_sems, recv_sems, k, to):
    return pltpu.make_async_remote_copy(src_ref=src, dst_ref=dst, send_sem=send_sems.at[k], recv_sem=recv_sems.at[k],
                                        device_id=to, device_id_type=MESH)


def _sems(n):
    return pltpu.SemaphoreType.DMA((n,))


def place_slot(parts, slots, n_slots, dtype, from_slot, *, name):
    n = len(parts)
    r, cols = parts[0].shape[-2:]
    tr = _row_tile(r, cols)

    def body(src_ref, dst_ref, *refs):
        for a in range(n):
            refs[n + a][...] = refs[a][...].astype(dtype)

    dst = pl.BlockSpec((None, tr, cols), lambda i, src_ref, dst_ref: (dst_ref[0], i, 0))
    src = (pl.BlockSpec((None, tr, cols), lambda i, src_ref, dst_ref: (src_ref[0], i, 0)) if from_slot
           else pl.BlockSpec((tr, cols), lambda i, src_ref, dst_ref: (i, 0)))
    return pl.pallas_call(
        body,
        grid_spec=pltpu.PrefetchScalarGridSpec(num_scalar_prefetch=2, grid=(r // tr,), in_specs=[src] * n,
                                               out_specs=[dst] * n),
        out_shape=[jax.ShapeDtypeStruct((n_slots, r, cols), dtype)] * n,
        compiler_params=_params(("parallel",)), name=name,
    )(*slots, *parts)


def gather_shards(bufs, *, name):
    n = len(bufs)

    def body(*refs):
        out_refs = refs[n:2 * n]
        send_sems, recv_sems = refs[2 * n:]
        x, y, c, chips = _mesh_position()
        me = 2 * x + y
        sibling = (x, y, 1 - c)
        waits = []
        for j, (cx, cy) in enumerate(chips):
            for a in range(n):
                own = out_refs[a].at[me, c]
                cp = _remote(own, own, send_sems, recv_sems, 6 * a + j, (cx, cy, c))
                cp.start()
                waits.append(cp.wait_send)
        for j, (cx, cy) in enumerate(chips):
            for a in range(n):
                got = out_refs[a].at[2 * cx + cy, c]
                _remote(got, got, send_sems, recv_sems, 6 * a + j, (cx, cy, c)).wait_recv()
                fw = _remote(got, got, send_sems, recv_sems, 6 * a + 3 + j, sibling)
                fw.start()
                waits.append(fw.wait_send)
        for j, (cx, cy) in enumerate(chips):
            for a in range(n):
                got = out_refs[a].at[2 * cx + cy, 1 - c]
                _remote(got, got, send_sems, recv_sems, 6 * a + 3 + j, sibling).wait_recv()
        for wait in waits:
            wait()

    return pl.pallas_call(
        body, out_shape=[jax.ShapeDtypeStruct(b.shape, b.dtype) for b in bufs],
        in_specs=[ANY] * n, out_specs=[ANY] * n, scratch_shapes=[_sems(6 * n), _sems(6 * n)],
        input_output_aliases={a: a for a in range(n)}, name=name,
    )(*bufs)


def chip_exchange(ps, qs, *, name):
    n = len(ps)

    def body(*refs):
        p_refs, q_refs = refs[:n], refs[2 * n:3 * n]
        send_sems, recv_sems = refs[3 * n:]
        x, y, c, chips = _mesh_position()
        me = 2 * x + y
        waits = []
        for j, (cx, cy) in enumerate(chips):
            for a in range(n):
                cp = _remote(p_refs[a].at[2 * cx + cy], q_refs[a].at[me], send_sems, recv_sems, 3 * a + j, (cx, cy, c))
                cp.start()
                waits.append(cp.wait_send)
        for j, (cx, cy) in enumerate(chips):
            for a in range(n):
                got = q_refs[a].at[2 * cx + cy]
                _remote(got, got, send_sems, recv_sems, 3 * a + j, (cx, cy, c)).wait_recv()
        for wait in waits:
            wait()

    return pl.pallas_call(
        body, out_shape=[jax.ShapeDtypeStruct(q_.shape, q_.dtype) for q_ in qs], in_specs=[ANY] * (2 * n),
        out_specs=[ANY] * n, scratch_shapes=[_sems(3 * n), _sems(3 * n)],
        input_output_aliases={n + a: a for a in range(n)}, name=name,
    )(*ps, *qs)


def gather_slots_async(bufs, collective_id, *, name):
    n = len(bufs)
    refs = [jax.new_ref(b, memory_space=pltpu.MemorySpace.HBM) for b in bufs]

    @pl.kernel(mesh=plsc.ScalarSubcoreMesh(axis_name="sequencer", num_cores=1), name=name,
               scratch_types=(_sems(3 * n), _sems(3 * n)),
               compiler_params=pltpu.CompilerParams(collective_id=collective_id))
    def launch(send_sems, recv_sems):
        x, y, c, chips = _mesh_position()
        me = 2 * x + y
        barrier = pltpu.get_barrier_semaphore()
        for cx, cy in chips:
            pl.semaphore_signal(barrier, inc=1, device_id=(cx, cy, c), device_id_type=MESH)
        pl.semaphore_wait(barrier, len(chips))
        sends = []
        for j, (cx, cy) in enumerate(chips):
            for a in range(n):
                own = refs[a].at[me]
                cp = _remote(own, own, send_sems, recv_sems, 3 * a + j, (cx, cy, c))
                cp.start()
                sends.append(cp)
        for j, (cx, cy) in enumerate(chips):
            for a in range(n):
                got = refs[a].at[2 * cx + cy]
                _remote(got, got, send_sems, recv_sems, 3 * a + j, (cx, cy, c)).wait_recv()
        for cp in sends:
            cp.wait_send()

    launch()
    return [r[...] for r in refs]


N_DEVICES = 8
PEER_FLIPS = tuple((dx, dy, dc) for dx in (0, 1) for dy in (0, 1) for dc in (0, 1) if dx or dy or dc)


def exchange_partials_async(sends, collective_id, *, name):
    n = len(sends)
    k = len(PEER_FLIPS)

    def launch(*refs):
        s_refs, r_refs = refs[:n], refs[n:2 * n]
        send_sems, recv_sems, local_sems = refs[2 * n:]
        x, y, c, _ = _mesh_position()
        me = 4 * x + 2 * y + c
        peers = [(1 - x if dx else x, 1 - y if dy else y, 1 - c if dc else c) for dx, dy, dc in PEER_FLIPS]
        barrier = pltpu.get_barrier_semaphore()
        for peer in peers:
            pl.semaphore_signal(barrier, inc=1, device_id=peer, device_id_type=MESH)
        pl.semaphore_wait(barrier, len(peers))
        sends_started = []
        for a in range(n):
            own = pltpu.make_async_copy(s_refs[a].at[2 * x + y], r_refs[a].at[me], local_sems.at[a])
            own.start()
            sends_started.append(own)
        for j, (px, py, pc) in enumerate(peers):
            for a in range(n):
                cp = _remote(s_refs[a].at[2 * px + py], r_refs[a].at[me], send_sems, recv_sems, j, (px, py, pc))
                cp.start()
                sends_started.append(cp)
        for j, (px, py, pc) in enumerate(peers):
            for a in range(n):
                got = r_refs[a].at[4 * px + 2 * py + pc]
                _remote(got, got, send_sems, recv_sems, j, (px, py, pc)).wait_recv()
        for cp in sends_started[n:]:
            cp.wait_send()
        for cp in sends_started[:n]:
            cp.wait()

    return list(pl.kernel(
        launch, out_type=[jax.ShapeDtypeStruct((N_DEVICES,) + s_.shape[1:], s_.dtype) for s_ in sends],
        mesh=plsc.ScalarSubcoreMesh(axis_name="sequencer", num_cores=1), name=name,
        scratch_types=(_sems(k), _sems(k), _sems(n)),
        compiler_params=pltpu.CompilerParams(collective_id=collective_id))(*sends))


def sibling_exchange(gs, *, name):
    n = len(gs)

    def body(*refs):
        g_refs, out_refs = refs[:n], refs[n:2 * n]
        send_sems, recv_sems = refs[2 * n:]
        x, y, c, _ = _mesh_position()
        cps = [_remote(g_refs[a].at[:, 1 - c], out_refs[a], send_sems, recv_sems, a, (x, y, 1 - c)) for a in range(n)]
        for cp in cps:
            cp.start()
        for cp in cps:
            cp.wait()

    return pl.pallas_call(
        body, out_shape=[jax.ShapeDtypeStruct(g.shape[:1] + g.shape[2:], g.dtype) for g in gs],
        in_specs=[ANY] * n, out_specs=[ANY] * n, scratch_shapes=[_sems(n), _sems(n)], name=name,
    )(*gs)


def add_own_half(gs, others, c_idx, dtype, *, name):
    n = len(gs)
    ns, _, r, cols = gs[0].shape
    tr = _row_tile(r, cols)

    def body(c_ref, *refs):
        for a in range(n):
            refs[2 * n + a][...] = (refs[a][...] + refs[n + a][...]).astype(dtype)

    own = pl.BlockSpec((None, None, tr, cols), lambda s, i, c_ref: (s, c_ref[0], i, 0))
    oth = pl.BlockSpec((None, tr, cols), lambda s, i, c_ref: (s, i, 0))
    return pl.pallas_call(
        body,
        grid_spec=pltpu.PrefetchScalarGridSpec(num_scalar_prefetch=1, grid=(ns, r // tr),
                                               in_specs=[own] * n + [oth] * n, out_specs=[oth] * n),
        out_shape=[jax.ShapeDtypeStruct((ns, r, cols), dtype)] * n,
        compiler_params=_params(("parallel", "parallel")), name=name,
    )(c_idx, *gs, *others)


def sum_slots(qs, *, name):
    n = len(qs)
    ns, r, cols = qs[0].shape
    tr = _row_tile(r, cols * ns)

    def body(*refs):
        for a in range(n):
            q_ref = refs[a]
            acc = q_ref[0].astype(f32) + q_ref[1].astype(f32)
            for i in range(2, ns):
                acc = acc + q_ref[i].astype(f32)
            refs[n + a][...] = acc

    return pl.pallas_call(
        body, grid=(r // tr,), in_specs=[pl.BlockSpec((ns, tr, cols), lambda i: (0, i, 0))] * n,
        out_specs=[pl.BlockSpec((tr, cols), lambda i: (i, 0))] * n,
        out_shape=[jax.ShapeDtypeStruct((r, cols), f32)] * n,
        compiler_params=_params(("parallel",)), name=name,
    )(*qs)


def sibling_share(bufs, *, name):
    n = len(bufs)

    def body(*refs):
        out_refs = refs[n:2 * n]
        send_sems, recv_sems = refs[2 * n:]
        x, y, c, _ = _mesh_position()
        sibling = (x, y, 1 - c)
        cps = []
        for a in range(n):
            own = out_refs[a].at[c]
            cp = _remote(own, own, send_sems, recv_sems, a, sibling)
            cp.start()
            cps.append(cp)
        for a in range(n):
            theirs = out_refs[a].at[1 - c]
            _remote(theirs, theirs, send_sems, recv_sems, a, sibling).wait_recv()
        for cp in cps:
            cp.wait_send()

    return pl.pallas_call(
        body, out_shape=[jax.ShapeDtypeStruct(b.shape, b.dtype) for b in bufs], in_specs=[ANY] * n,
        out_specs=[ANY] * n, scratch_shapes=[_sems(n), _sems(n)],
        input_output_aliases={a: a for a in range(n)}, name=name,
    )(*bufs)


def _adamw_update(w, g, m, v):
    m2 = ADAM_B1 * m + (1.0 - ADAM_B1) * g
    v2 = ADAM_B2 * v + (1.0 - ADAM_B2) * (g * g)
    m_hat = m2 / (1.0 - ADAM_B1 ** ADAM_STEP)
    v_hat = v2 / (1.0 - ADAM_B2 ** ADAM_STEP)
    return -ADAM_LR * (m_hat / (jnp.sqrt(v_hat) + ADAM_EPS) + ADAM_WD * w), m2, v2


def adamw_from_partials(ws, ms, vs, slots, layer, acc, *, name):
    n = len(ws)
    nl, r, cols = ws[0].shape
    ns = slots[0].shape[0]
    tr = _row_tile(r, cols * 2)

    def body(*refs):
        for a in range(n):
            w_ref, m_ref, v_ref, s_ref = (refs[k * n + a] for k in range(4))
            g_ref, d_ref, m2_ref, v2_ref = (refs[len(refs) - 4 * n + k * n + a] for k in range(4))
            g = s_ref[0].astype(f32) + s_ref[1].astype(f32)
            for i in range(2, ns):
                g = g + s_ref[i].astype(f32)
            g_ref[...] = g
            d_ref[...], m2_ref[...], v2_ref[...] = _adamw_update(w_ref[...], g, m_ref[...], v_ref[...])

    lay = pl.BlockSpec((None, tr, cols), lambda i: (layer, i, 0))
    in_specs = [lay] * (3 * n) + [pl.BlockSpec((ns, tr, cols), lambda i: (0, i, 0))] * n
    args = [*ws, *ms, *vs, *slots]
    aliases = {}
    if acc is not None:
        in_specs += [ANY] * (4 * n)
        args += [a for lst in acc for a in lst]
        aliases = {4 * n + k: k for k in range(4 * n)}
    out = pl.pallas_call(
        body, grid=(r // tr,), in_specs=in_specs, out_specs=[lay] * (4 * n),
        out_shape=[jax.ShapeDtypeStruct((nl, r, cols), f32)] * (4 * n), input_output_aliases=aliases,
        compiler_params=_params(("parallel",)), name=name,
    )(*args)
    return [list(out[k * n:(k + 1) * n]) for k in range(4)]


def adamw(ws, gs, ms, vs, *, name):
    n = len(ws)
    r, cols = ws[0].shape
    tr = _row_tile(r, cols)

    def body(*refs):
        for a in range(n):
            w_ref, g_ref, m_ref, v_ref = (refs[k * n + a] for k in range(4))
            d_ref, m2_ref, v2_ref = (refs[(4 + k) * n + a] for k in range(3))
            d_ref[...], m2_ref[...], v2_ref[...] = _adamw_update(w_ref[...], g_ref[...], m_ref[...], v_ref[...])

    row = pl.BlockSpec((tr, cols), lambda i: (i, 0))
    out = pl.pallas_call(
        body, grid=(r // tr,), in_specs=[row] * (4 * n), out_specs=[row] * (3 * n),
        out_shape=[jax.ShapeDtypeStruct((r, cols), f32)] * (3 * n),
        compiler_params=_params(("parallel",)), name=name,
    )(*ws, *gs, *ms, *vs)
    return out[:n], out[n:2 * n], out[2 * n:]


def _full_weights(gathered, names, weights):
    pieces = _unpack(gathered, [weights[k].shape for k in names], lead=(N_CHIPS,))
    full = {}
    for name, pc in zip(names, pieces):
        ax = SHARD_AXIS.get(name)
        if ax is None:
            full[name] = weights[name]
        else:
            shp = weights[name].shape
            full[name] = jnp.moveaxis(pc, 0, ax).reshape(shp[:ax] + (N_CHIPS * shp[ax],) + shp[ax + 1:])
    return full


def _grad_pack(grads, names, shapes):
    pieces = []
    for name, shp in zip(names, shapes):
        g = grads[name]
        ax = SHARD_AXIS.get(name)
        if ax is None:
            pieces.append(jnp.broadcast_to(g.reshape(shp)[None], (N_CHIPS,) + tuple(shp)))
        else:
            pieces.append(jnp.stack(jnp.split(g, N_CHIPS, axis=ax)))
    return _pack(pieces, lead=(N_CHIPS,))


def _by_shape(arrays):
    groups = {}
    for i, a in enumerate(arrays):
        groups.setdefault(a.shape, []).append(i)
    return list(groups.values())


def _grouped(fn, lists, n_out, tag):
    outs = [[None] * len(lists[0]) for _ in range(n_out)]
    for gi, idx in enumerate(_by_shape(lists[0])):
        res = fn(*[[lst[i] for i in idx] for lst in lists], name=f"{tag}_{gi}")
        res = res if n_out > 1 else (res,)
        for k in range(n_out):
            for i, r in zip(idx, res[k]):
                outs[k][i] = r
    return outs if n_out > 1 else outs[0]


def _train_step(x, p, loss_target, weights, m, v):
    shapes = [[weights[k].shape for k in names] for names in PACK_GROUPS]
    halves = lambda a: a.reshape((2, a.shape[0] // 2) + a.shape[1:])
    packs = lambda d_: [halves(_pack([d_[k] for k in names])) for names in PACK_GROUPS]
    nn_ = len(NATIVE_NAMES)
    local = [weights[k] for k in NATIVE_NAMES] + packs(weights)
    local_m = [m[k] for k in NATIVE_NAMES] + packs(m)
    local_v = [v[k] for k in NATIVE_NAMES] + packs(v)
    flat = lambda lst: [a.reshape((-1, a.shape[-1])) for a in lst]
    c_idx = lax.axis_index("c").astype(jnp.int32).reshape(1)
    chip_idx = (2 * lax.axis_index("x") + lax.axis_index("y")).astype(jnp.int32).reshape(1)
    c2 = (c_idx, c_idx)
    chip2 = (chip_idx, chip_idx)

    def placed(arrays, slot, n_slots, dtype, from_slot, tag):
        return _grouped(lambda a, name: place_slot(a, slot, n_slots, dtype, from_slot, name=name), [arrays], 1, tag)

    ffn_own = [weights[k][i] for i in range(DEPTH) for k in NATIVE_NAMES]
    ffn_bufs = placed(ffn_own, chip2, N_CHIPS, bf16, False, "place_ffn_weights")
    group = len(NATIVE_NAMES) // 2
    n_ffn_groups = len(ffn_bufs) // group
    ffn_gathered = []
    for gi in range(n_ffn_groups):
        ffn_gathered += gather_slots_async(ffn_bufs[gi * group:(gi + 1) * group], collective_id=1 + gi,
                                           name=f"comm_gather_ffn_{gi}")
    ffn_weights = {k: [ffn_gathered[i * len(NATIVE_NAMES) + j] for i in range(DEPTH)] for j, k in enumerate(NATIVE_NAMES)}
    pack_bufs = [placed(flat([a]), chip2, N_CHIPS, dt, False, f"place_packed_weights_{gi}")[0].reshape((N_CHIPS,) + a.shape)
                 for gi, (a, dt) in enumerate(zip(local[nn_:], PACK_TRANSIT))]
    full = {}
    for names, gathered in zip(PACK_GROUPS, gather_shards(pack_bufs, name="comm_gather_weights")):
        full.update(_full_weights(gathered, names, weights))
    first_grad_id = n_ffn_groups + 1
    in_flight = {}

    def on_ffn_grads(layer, first, partials):
        tag = f"ffn_grads_l{layer}_{first}"
        got = exchange_partials_async(partials, collective_id=first_grad_id + len(in_flight), name="comm_" + tag)
        in_flight[(layer, first)] = got

    loss, grad_x, grads = _local_step(x, p, loss_target, full, ffn_weights, on_ffn_grads)
    gs = [_grad_pack(grads, names, shp).reshape((N_CHIPS,) + a.shape)
          for names, shp, a in zip(PACK_GROUPS, shapes, local[nn_:])]
    others = sibling_exchange(gs, name="comm_grad_sibling")
    chip_sums = [add_own_half([g], [o], c_idx, dt, name=f"grad_add_sibling_{gi}")[0]
                 for gi, (g, o, dt) in enumerate(zip(gs, others, PACK_TRANSIT))]
    own = [placed([cs], chip2, N_CHIPS, dt, True, f"place_own_partial_{gi}")[0]
           for gi, (cs, dt) in enumerate(zip(chip_sums, PACK_TRANSIT))]
    slots = chip_exchange(chip_sums, own, name="comm_grad_chips")
    mine = _grouped(sum_slots, [list(slots)], 1, "grad_sum_chips")
    pack_sum = sibling_share(placed(mine, c2, 2, f32, False, "place_own_half"), name="comm_grad_share")
    ffn_out = [{} for _ in range(4)]
    for (layer, first), got in in_flight.items():
        names = NATIVE_NAMES[first:first + len(got)]
        for idx in _by_shape([weights[k] for k in names]):
            ks = [names[i] for i in idx]
            acc = [[out[k] for k in ks] for out in ffn_out] if ks[0] in ffn_out[0] else None
            res = adamw_from_partials([weights[k] for k in ks], [m[k] for k in ks], [v[k] for k in ks],
                                      [got[i] for i in idx], layer, acc, name=f"adamw_ffn_l{layer}_{first + idx[0]}")
            for out, arrays in zip(ffn_out, res):
                out.update(zip(ks, arrays))
    pack_out = [list(pack_sum)] + _grouped(adamw, [flat(local[nn_:]), flat(pack_sum), flat(local_m[nn_:]),
                                                    flat(local_v[nn_:])], 3, "adamw_packed")
    loss = lax.psum(loss, ("x", "y", "c"))
    outs = []
    for by_name, packs_ in zip(ffn_out, pack_out):
        by_name = dict(by_name)
        for names, shp, pk in zip(PACK_GROUPS, shapes, packs_):
            by_name.update(zip(names, _unpack(pk, shp)))
        outs += [by_name[k] for k in WEIGHT_NAMES]
    return (loss, grad_x, *outs)


def kernel(x, p, ffn1_wg, ffn1_wu, ffn1_wd, ffn2_wg, ffn2_wu, ffn2_wd, ln_g, ln_b, ple_wg, ple_bg, ple_wp, ab_w_in, a_sinks, b_conv_w, b_conv_b, b_wa, b_ba, b_wx, b_bx, b_lam, ab_w_out, c_w_in, c_conv_w, c_a_log, c_dt_bias, c_norm_g, c_w_out, loss_target, m_ffn1_wg, m_ffn1_wu, m_ffn1_wd, m_ffn2_wg, m_ffn2_wu, m_ffn2_wd, m_ln_g, m_ln_b, m_ple_wg, m_ple_bg, m_ple_wp, m_ab_w_in, m_a_sinks, m_b_conv_w, m_b_conv_b, m_b_wa, m_b_ba, m_b_wx, m_b_bx, m_b_lam, m_ab_w_out, m_c_w_in, m_c_conv_w, m_c_a_log, m_c_dt_bias, m_c_norm_g, m_c_w_out, v_ffn1_wg, v_ffn1_wu, v_ffn1_wd, v_ffn2_wg, v_ffn2_wu, v_ffn2_wd, v_ln_g, v_ln_b, v_ple_wg, v_ple_bg, v_ple_wp, v_ab_w_in, v_a_sinks, v_b_conv_w, v_b_conv_b, v_b_wa, v_b_ba, v_b_wx, v_b_bx, v_b_lam, v_ab_w_out, v_c_w_in, v_c_conv_w, v_c_a_log, v_c_dt_bias, v_c_norm_g, v_c_w_out):
    weights = [ffn1_wg, ffn1_wu, ffn1_wd, ffn2_wg, ffn2_wu, ffn2_wd, ln_g, ln_b, ple_wg, ple_bg, ple_wp, ab_w_in, a_sinks,
               b_conv_w, b_conv_b, b_wa, b_ba, b_wx, b_bx, b_lam, ab_w_out, c_w_in, c_conv_w, c_a_log, c_dt_bias, c_norm_g,
               c_w_out]
    m = [m_ffn1_wg, m_ffn1_wu, m_ffn1_wd, m_ffn2_wg, m_ffn2_wu, m_ffn2_wd, m_ln_g, m_ln_b, m_ple_wg, m_ple_bg, m_ple_wp,
         m_ab_w_in, m_a_sinks, m_b_conv_w, m_b_conv_b, m_b_wa, m_b_ba, m_b_wx, m_b_bx, m_b_lam, m_ab_w_out, m_c_w_in,
         m_c_conv_w, m_c_a_log, m_c_dt_bias, m_c_norm_g, m_c_w_out]
    v = [v_ffn1_wg, v_ffn1_wu, v_ffn1_wd, v_ffn2_wg, v_ffn2_wu, v_ffn2_wd, v_ln_g, v_ln_b, v_ple_wg, v_ple_bg, v_ple_wp,
         v_ab_w_in, v_a_sinks, v_b_conv_w, v_b_conv_b, v_b_wa, v_b_ba, v_b_wx, v_b_bx, v_b_lam, v_ab_w_out, v_c_w_in,
         v_c_conv_w, v_c_a_log, v_c_dt_bias, v_c_norm_g, v_c_w_out]
    return _train_step(x, p, loss_target, dict(zip(WEIGHT_NAMES, weights)), dict(zip(WEIGHT_NAMES, m)),
                       dict(zip(WEIGHT_NAMES, v)))
```

```python
import functools

import jax
import jax.numpy as jnp
from jax import lax
from jax.experimental import pallas as pl
from jax.experimental.pallas import tpu as pltpu
from jax.experimental.pallas import tpu_sc as plsc

f32 = jnp.float32
bf16 = jnp.bfloat16

DEPTH = 2
CHUNK = 64
A_HEADS, A_KV_HEADS, A_GROUP, A_HEAD_DIM = 8, 2, 4, 64
A_WIDTH, A_KV_WIDTH, A_WINDOW = 512, 128, 128
B_WIDTH, B_BLOCKS, B_BLOCK, B_CONV = 512, 8, 64, 4
RG_C = 8.0
C_HEADS, C_HEAD_DIM, C_WIDTH, C_CONV = 8, 128, 1024, 4
DN_ALPHA = (2.0 * DEPTH) ** 0.25
LN_EPS = 1e-5
NORM_EPS = 1e-6
NEG = -1e30
ADAM_LR, ADAM_B1, ADAM_B2, ADAM_EPS, ADAM_WD, ADAM_STEP = 0.001, 0.9, 0.999, 1e-08, 0.01, 10

VMEM_LIMIT_BYTES = 56 * 1024 * 1024
LANES = 128
SUBLANES = 8
GROUP_W = 128
PREP_FWD_UNROLL = 16
PREP_BWD_UNROLL = 16
C_HEADS_PER_STEP = 8
GDN_TIME_BLOCK = 256

NN = ((1,), (0,))
NT = ((1,), (1,))
TN = ((0,), (0,))


def _params(sem):
    return pltpu.CompilerParams(dimension_semantics=sem, vmem_limit_bytes=VMEM_LIMIT_BYTES)


def _tile(n, cap, mult):
    best = None
    t = mult
    while t <= min(n, cap):
        if n % t == 0:
            best = t
        t += mult
    return best if best is not None else n


def _bdot(a, b, dims):
    return lax.dot_general(a.astype(bf16), b.astype(bf16), (dims, ((), ())), preferred_element_type=f32)


def _running_sum(x, reverse):
    s = x.shape[0]
    t = lax.broadcasted_iota(jnp.int32, x.shape, 0)
    d = 1
    while d < s:
        if reverse:
            x = x + jnp.where(t < s - d, pltpu.roll(x, s - d, 0), 0.0)
        else:
            x = x + jnp.where(t >= d, pltpu.roll(x, d, 0), 0.0)
        d *= 2
    return x


@jax.custom_vjp
def _cumsum0(x):
    return _running_sum(x, False)


def _cumsum0_fwd(x):
    return _running_sum(x, False), None


def _cumsum0_bwd(_, g):
    return (_running_sum(g, True),)


_cumsum0.defvjp(_cumsum0_fwd, _cumsum0_bwd)


@jax.custom_vjp
def _bnn(a, b):
    return _bdot(a, b, NN)


def _bnn_fwd(a, b):
    return _bdot(a, b, NN), (a, b)


def _bnn_bwd(res, g):
    a, b = res
    return _bdot(g, b, NT), _bdot(a, g, TN)


_bnn.defvjp(_bnn_fwd, _bnn_bwd)


@jax.custom_vjp
def _bnt(a, b):
    return _bdot(a, b, NT)


def _bnt_fwd(a, b):
    return _bdot(a, b, NT), (a, b)


def _bnt_bwd(res, g):
    a, b = res
    return _bdot(g, b, NN), _bdot(g, a, TN)


_bnt.defvjp(_bnt_fwd, _bnt_bwd)


@jax.custom_vjp
def _btn(a, b):
    return _bdot(a, b, TN)


def _btn_fwd(a, b):
    return _bdot(a, b, TN), (a, b)


def _btn_bwd(res, g):
    a, b = res
    return _bdot(b, g, NT), _bdot(a, g, NN)


_btn.defvjp(_btn_fwd, _btn_bwd)

RAW_DOTS = (lambda a, b: _bdot(a, b, NN), lambda a, b: _bdot(a, b, NT), lambda a, b: _bdot(a, b, TN),
            lambda x: _running_sum(x, False))
VJP_DOTS = (_bnn, _bnt, _btn, _cumsum0)


def _layer_norm(z, g, b):
    mu = jnp.mean(z, -1, keepdims=True)
    d = z - mu
    var = jnp.mean(d * d, -1, keepdims=True)
    return d * lax.rsqrt(var + LN_EPS) * g + b


def _layer_norm_bwd(z, dy, g):
    mu = jnp.mean(z, -1, keepdims=True)
    dd = z - mu
    var = jnp.mean(dd * dd, -1, keepdims=True)
    rstd = lax.rsqrt(var + LN_EPS)
    xhat = dd * rstd
    dxh = dy * g
    dz = rstd * (dxh - jnp.mean(dxh, -1, keepdims=True) - xhat * jnp.mean(dxh * xhat, -1, keepdims=True))
    return dz, jnp.sum(dy * xhat, 0, keepdims=True), jnp.sum(dy, 0, keepdims=True)


def _silu(x):
    return x * jax.nn.sigmoid(x)


def mm_nn(a, w, add=None, add_scale=1.0, *, name):
    m, k = a.shape
    n = w.shape[1]
    tm = _tile(m, 1024, 2 * SUBLANES)
    tn = _tile(n, 1024, LANES)

    def body(*refs):
        if add is None:
            a_ref, w_ref, o_ref = refs
            o_ref[...] = _bdot(a_ref[...], w_ref[...], NN)
        else:
            a_ref, w_ref, add_ref, o_ref = refs
            o_ref[...] = _bdot(a_ref[...], w_ref[...], NN) + add_scale * add_ref[...]

    in_specs = [pl.BlockSpec((tm, k), lambda i, j: (i, 0)), pl.BlockSpec((k, tn), lambda i, j: (0, j))]
    args = [a, w]
    if add is not None:
        in_specs.append(pl.BlockSpec((tm, tn), lambda i, j: (i, j)))
        args.append(add)
    return pl.pallas_call(
        body, grid=(m // tm, n // tn), in_specs=in_specs,
        out_specs=pl.BlockSpec((tm, tn), lambda i, j: (i, j)),
        out_shape=jax.ShapeDtypeStruct((m, n), f32),
        compiler_params=_params(("parallel", "parallel")), name=name,
    )(*args)


def mm_tn(a, b, *, name):
    m, k = a.shape
    n = b.shape[1]
    tm = _tile(m, 1024, 2 * SUBLANES)
    tn = _tile(n, 1024, LANES)

    def body(a_ref, b_ref, o_ref):
        part = _bdot(a_ref[...], b_ref[...], TN)

        @pl.when(pl.program_id(1) == 0)
        def _():
            o_ref[...] = part

        @pl.when(pl.program_id(1) > 0)
        def _():
            o_ref[...] += part

    return pl.pallas_call(
        body, grid=(n // tn, m // tm),
        in_specs=[pl.BlockSpec((tm, k), lambda j, i: (i, 0)), pl.BlockSpec((tm, tn), lambda j, i: (i, j))],
        out_specs=pl.BlockSpec((k, tn), lambda j, i: (0, j)),
        out_shape=jax.ShapeDtypeStruct((k, n), f32),
        compiler_params=_params(("parallel", "arbitrary")), name=name,
    )(a, b)


def proj_ln(a_list, w_list, xres, g, b, *, name):
    t, d = xres.shape
    tm = _tile(t, 256, 2 * SUBLANES)
    na = len(a_list)

    def body(*refs):
        a_refs, w_refs = refs[:na], refs[na:2 * na]
        x_ref, g_ref, b_ref, y_ref, z_ref, yb_ref = refs[2 * na:]
        z = DN_ALPHA * x_ref[...]
        for a_ref, w_ref in zip(a_refs, w_refs):
            z = z + _bdot(a_ref[...], w_ref[...], NN)
        z_ref[...] = z
        y = _layer_norm(z, g_ref[...], b_ref[...])
        y_ref[...] = y
        yb_ref[...] = y.astype(bf16)

    in_specs = [pl.BlockSpec((tm, a.shape[1]), lambda i: (i, 0)) for a in a_list]
    in_specs += [pl.BlockSpec(w.shape, lambda i: (0, 0)) for w in w_list]
    in_specs += [pl.BlockSpec((tm, d), lambda i: (i, 0)), pl.BlockSpec((1, d), lambda i: (0, 0)),
                 pl.BlockSpec((1, d), lambda i: (0, 0))]
    return pl.pallas_call(
        body, grid=(t // tm,), in_specs=in_specs,
        out_specs=[pl.BlockSpec((tm, d), lambda i: (i, 0))] * 3,
        out_shape=[jax.ShapeDtypeStruct((t, d), f32)] * 2 + [jax.ShapeDtypeStruct((t, d), bf16)],
        compiler_params=_params(("parallel",)), name=name,
    )(*a_list, *w_list, xres, g, b)


def ln_bwd(z, dy, g, *, name):
    t, d = z.shape
    tm = _tile(t, 512, SUBLANES)

    def body(z_ref, dy_ref, g_ref, dz_ref, dzb_ref, dg_ref, db_ref):
        dz, pg, pb = _layer_norm_bwd(z_ref[...], dy_ref[...], g_ref[...])
        dz_ref[...] = dz
        dzb_ref[...] = dz.astype(bf16)

        @pl.when(pl.program_id(0) == 0)
        def _():
            dg_ref[...] = pg
            db_ref[...] = pb

        @pl.when(pl.program_id(0) > 0)
        def _():
            dg_ref[...] += pg
            db_ref[...] += pb

    row = pl.BlockSpec((tm, d), lambda i: (i, 0))
    vec = pl.BlockSpec((1, d), lambda i: (0, 0))
    return pl.pallas_call(
        body, grid=(t // tm,), in_specs=[row, row, vec], out_specs=[row, row, vec, vec],
        out_shape=[jax.ShapeDtypeStruct((t, d), f32), jax.ShapeDtypeStruct((t, d), bf16),
                   jax.ShapeDtypeStruct((1, d), f32), jax.ShapeDtypeStruct((1, d), f32)],
        compiler_params=_params(("arbitrary",)), name=name,
    )(z, dy, g)


def loss_head(y, target, *, name):
    t, d = y.shape
    tm = _tile(t, 512, SUBLANES)

    def body(y_ref, t_ref, dy_ref, sq_ref):
        e = y_ref[...] - t_ref[...]
        dy_ref[...] = e * (1.0 / d)
        part = jnp.sum(e * e, 0, keepdims=True)

        @pl.when(pl.program_id(0) == 0)
        def _():
            sq_ref[...] = part

        @pl.when(pl.program_id(0) > 0)
        def _():
            sq_ref[...] += part

    row = pl.BlockSpec((tm, d), lambda i: (i, 0))
    vec = pl.BlockSpec((1, d), lambda i: (0, 0))
    return pl.pallas_call(
        body, grid=(t // tm,), in_specs=[row, row], out_specs=[row, vec],
        out_shape=[jax.ShapeDtypeStruct((t, d), f32), jax.ShapeDtypeStruct((1, d), f32)],
        compiler_params=_params(("arbitrary",)), name=name,
    )(y, target)


FFN_COL_BLOCK = 256
FFN_ROWS = 1024


def _lane_blocks(n):
    return [slice(s, min(s + FFN_COL_BLOCK, n)) for s in range(0, n, FFN_COL_BLOCK)]


def ffn_fwd(x, wg, wu, wd, g, b, *, name):
    t, d = x.shape
    nf, _, tf = wg.shape
    tm = _tile(t, FFN_ROWS, SUBLANES)

    def body(x_ref, wg_ref, wu_ref, wd_ref, g_ref, b_ref, y_ref, z_ref, yb_ref, acc_ref):
        f = pl.program_id(1)
        xb = x_ref[...].astype(bf16)
        part, pending = None, None
        for cols in _lane_blocks(tf):
            gate_up = (_bdot(xb, wg_ref[:, cols], NN), _bdot(xb, wu_ref[:, cols], NN), cols)
            if pending is not None:
                down = _bdot(_silu(pending[0]) * pending[1], wd_ref[pending[2], :], NN)
                part = down if part is None else part + down
            pending = gate_up
        down = _bdot(_silu(pending[0]) * pending[1], wd_ref[pending[2], :], NN)
        part = down if part is None else part + down

        @pl.when(f == 0)
        def _():
            acc_ref[...] = part

        @pl.when(f > 0)
        def _():
            acc_ref[...] += part

        @pl.when(f == nf - 1)
        def _():
            z = DN_ALPHA * x_ref[...] + 0.5 * acc_ref[...]
            z_ref[...] = z
            y = _layer_norm(z, g_ref[...], b_ref[...])
            y_ref[...] = y
            yb_ref[...] = y.astype(bf16)

    row = pl.BlockSpec((tm, d), lambda i, j: (i, 0))
    vec = pl.BlockSpec((1, d), lambda i, j: (0, 0))
    wcol = pl.BlockSpec((None, d, tf), lambda i, j: (j, 0, 0))
    wrow = pl.BlockSpec((None, tf, d), lambda i, j: (j, 0, 0))
    return pl.pallas_call(
        body, grid=(t // tm, nf),
        in_specs=[row, wcol, wcol, wrow, vec, vec],
        out_specs=[row, row, row],
        out_shape=[jax.ShapeDtypeStruct((t, d), f32)] * 2 + [jax.ShapeDtypeStruct((t, d), bf16)],
        scratch_shapes=[pltpu.VMEM((tm, d), f32)],
        compiler_params=_params(("parallel", "arbitrary")), name=name,
    )(x, wg, wu, wd, g, b)


def ffn_bwd_weights(xb, dzb, wg, wu, wd, *, name):
    t, d = xb.shape
    nf, _, tf = wg.shape
    tm = _tile(t, FFN_ROWS, SUBLANES)
    nt = t // tm

    def body(x_ref, dz_ref, wg_ref, wu_ref, wd_ref, dgate_ref, dup_ref, owg_ref, owu_ref, owd_ref,
             dwg_ref, dwu_ref, dwd_ref):
        x = x_ref[...]
        dzh = dz_ref[...] * 0.5

        def first_half(cols):
            return _bdot(x, wg_ref[:, cols], NN), _bdot(x, wu_ref[:, cols], NN), _bdot(dzh, wd_ref[cols, :], NT), cols

        def second_half(gate, up, dh, cols):
            sg = jax.nn.sigmoid(gate)
            s = gate * sg
            dup = (dh * s).astype(bf16)
            dgate = (dh * up * (sg * (1.0 + gate * (1.0 - sg)))).astype(bf16)
            dgate_ref[:, cols] = dgate
            dup_ref[:, cols] = dup
            return _bdot(x, dgate, TN), _bdot(x, dup, TN), _bdot(s * up, dzh, TN), cols

        parts, pending = [], None
        for cols in _lane_blocks(tf):
            nxt = first_half(cols)
            if pending is not None:
                parts.append(second_half(*pending))
            pending = nxt
        parts.append(second_half(*pending))

        @pl.when(pl.program_id(1) == 0)
        def _():
            for pwg, pwu, pwd, cols in parts:
                dwg_ref[:, cols] = pwg
                dwu_ref[:, cols] = pwu
                dwd_ref[cols, :] = pwd

        @pl.when(pl.program_id(1) > 0)
        def _():
            for pwg, pwu, pwd, cols in parts:
                dwg_ref[:, cols] += pwg
                dwu_ref[:, cols] += pwu
                dwd_ref[cols, :] += pwd

        @pl.when(pl.program_id(1) == nt - 1)
        def _():
            owg_ref[...] = dwg_ref[...].astype(bf16)
            owu_ref[...] = dwu_ref[...].astype(bf16)
            owd_ref[...] = dwd_ref[...].astype(bf16)

    row = pl.BlockSpec((tm, d), lambda j, i: (i, 0))
    wcol = pl.BlockSpec((None, d, tf), lambda j, i: (j, 0, 0))
    wrow = pl.BlockSpec((None, tf, d), lambda j, i: (j, 0, 0))
    act = pl.BlockSpec((None, tm, tf), lambda j, i: (j, i, 0))
    return pl.pallas_call(
        body, grid=(nf, nt), in_specs=[row, row, wcol, wcol, wrow], out_specs=[act, act, wcol, wcol, wrow],
        out_shape=[jax.ShapeDtypeStruct((nf, t, tf), bf16), jax.ShapeDtypeStruct((nf, t, tf), bf16),
                   jax.ShapeDtypeStruct((nf, d, tf), bf16), jax.ShapeDtypeStruct((nf, d, tf), bf16),
                   jax.ShapeDtypeStruct((nf, tf, d), bf16)],
        scratch_shapes=[pltpu.VMEM((d, tf), f32), pltpu.VMEM((d, tf), f32), pltpu.VMEM((tf, d), f32)],
        compiler_params=_params(("parallel", "arbitrary")), name=name,
    )(xb, dzb, wg, wu, wd)


def ffn_bwd_input(dgate, dup, wg, wu, dz, *, name):
    nf, t, tf = dgate.shape
    d = wg.shape[1]
    tm = _tile(t, FFN_ROWS // 2, SUBLANES)

    def body(dg_ref, du_ref, wg_ref, wu_ref, dz_ref, dx_ref):
        acc = DN_ALPHA * dz_ref[...]
        for j in range(nf):
            acc = acc + _bdot(dg_ref[j], wg_ref[j], NT) + _bdot(du_ref[j], wu_ref[j], NT)
        dx_ref[...] = acc

    act = pl.BlockSpec((nf, tm, tf), lambda i: (0, i, 0))
    wsp = pl.BlockSpec((nf, d, tf), lambda i: (0, 0, 0))
    row = pl.BlockSpec((tm, d), lambda i: (i, 0))
    return pl.pallas_call(
        body, grid=(t // tm,), in_specs=[act, act, wsp, wsp, row], out_specs=row,
        out_shape=jax.ShapeDtypeStruct((t, d), f32),
        compiler_params=_params(("parallel",)), name=name,
    )(dgate, dup, wg, wu, dz)


def ffn_bwd_input_ln(dgate, dup, wg, wu, dz, z_in, ln_g, *, name):
    nf, t, tf = dgate.shape
    d = wg.shape[1]
    tm = _tile(t, FFN_ROWS // 2, 2 * SUBLANES)

    def body(dg_ref, du_ref, wg_ref, wu_ref, dz_ref, z_ref, g_ref, o_ref, ob_ref, dlg_ref, dlb_ref):
        dx = DN_ALPHA * dz_ref[...]
        for j in range(nf):
            dx = dx + _bdot(dg_ref[j], wg_ref[j], NT) + _bdot(du_ref[j], wu_ref[j], NT)
        dzi, pg, pb = _layer_norm_bwd(z_ref[...], dx, g_ref[...])
        o_ref[...] = dzi
        ob_ref[...] = dzi.astype(bf16)

        @pl.when(pl.program_id(0) == 0)
        def _():
            dlg_ref[...] = pg
            dlb_ref[...] = pb

        @pl.when(pl.program_id(0) > 0)
        def _():
            dlg_ref[...] += pg
            dlb_ref[...] += pb

    act = pl.BlockSpec((nf, tm, tf), lambda i: (0, i, 0))
    wsp = pl.BlockSpec((nf, d, tf), lambda i: (0, 0, 0))
    row = pl.BlockSpec((tm, d), lambda i: (i, 0))
    vec = pl.BlockSpec((1, d), lambda i: (0, 0))
    return pl.pallas_call(
        body, grid=(t // tm,), in_specs=[act, act, wsp, wsp, row, row, vec], out_specs=[row, row, vec, vec],
        out_shape=[jax.ShapeDtypeStruct((t, d), f32), jax.ShapeDtypeStruct((t, d), bf16),
                   jax.ShapeDtypeStruct((1, d), f32), jax.ShapeDtypeStruct((1, d), f32)],
        compiler_params=_params(("arbitrary",)), name=name,
    )(dgate, dup, wg, wu, dz, z_in, ln_g)


def ple_fwd(x, p, wg, bg, wp, *, name):
    t, d = x.shape
    dp = p.shape[1]
    tm = _tile(t, 512, 2 * SUBLANES)

    def body(x_ref, p_ref, wg_ref, bg_ref, wp_ref, o_ref, ob_ref):
        x_ = x_ref[...]
        gate = jax.nn.sigmoid(_bdot(x_, wg_ref[...], NN) + bg_ref[...])
        out = x_ + gate * _bdot(p_ref[...], wp_ref[...], NN)
        o_ref[...] = out
        ob_ref[...] = out.astype(bf16)

    row = pl.BlockSpec((tm, d), lambda i: (i, 0))
    return pl.pallas_call(
        body, grid=(t // tm,),
        in_specs=[row, pl.BlockSpec((tm, dp), lambda i: (i, 0)), pl.BlockSpec((d, d), lambda i: (0, 0)),
                  pl.BlockSpec((1, d), lambda i: (0, 0)), pl.BlockSpec((dp, d), lambda i: (0, 0))],
        out_specs=[row, row], out_shape=[jax.ShapeDtypeStruct((t, d), f32), jax.ShapeDtypeStruct((t, d), bf16)],
        compiler_params=_params(("parallel",)), name=name,
    )(x, p, wg, bg, wp)


def ple_bwd(x, p, dy, wg, wgt, bg, wp, z, ln_g, *, name):
    t, d = x.shape
    dp = p.shape[1]
    tm = _tile(t, 512, 2 * SUBLANES)

    def body(x_ref, p_ref, dy_ref, wg_ref, wgt_ref, bg_ref, wp_ref, z_ref, g_ref,
             dz_ref, dzb_ref, dwg_ref, dbg_ref, dwp_ref, dg_ref, db_ref):
        x_ = x_ref[...]
        dy_ = dy_ref[...]
        s = jax.nn.sigmoid(_bdot(x_, wg_ref[...], NN) + bg_ref[...])
        e = _bdot(p_ref[...], wp_ref[...], NN)
        da = dy_ * e * s * (1.0 - s)
        de = dy_ * s
        dx = dy_ + _bdot(da, wgt_ref[...], NN)
        dz, pg, pb = _layer_norm_bwd(z_ref[...], dx, g_ref[...])
        dz_ref[...] = dz
        dzb_ref[...] = dz.astype(bf16)
        parts = ((dwg_ref, _bdot(x_, da, TN)), (dbg_ref, jnp.sum(da, 0, keepdims=True)),
                 (dwp_ref, _bdot(p_ref[...], de, TN)), (dg_ref, pg), (db_ref, pb))

        @pl.when(pl.program_id(0) == 0)
        def _():
            for ref, part in parts:
                ref[...] = part

        @pl.when(pl.program_id(0) > 0)
        def _():
            for ref, part in parts:
                ref[...] += part

    row = pl.BlockSpec((tm, d), lambda i: (i, 0))
    full = lambda shape: pl.BlockSpec(shape, lambda i: (0, 0))
    return pl.pallas_call(
        body, grid=(t // tm,),
        in_specs=[row, pl.BlockSpec((tm, dp), lambda i: (i, 0)), row, full((d, d)), full((d, d)), full((1, d)),
                  full((dp, d)), row, full((1, d))],
        out_specs=[row, row, full((d, d)), full((1, d)), full((dp, d)), full((1, d)), full((1, d))],
        out_shape=[jax.ShapeDtypeStruct((t, d), f32), jax.ShapeDtypeStruct((t, d), bf16),
                   jax.ShapeDtypeStruct((d, d), f32), jax.ShapeDtypeStruct((1, d), f32),
                   jax.ShapeDtypeStruct((dp, d), f32), jax.ShapeDtypeStruct((1, d), f32),
                   jax.ShapeDtypeStruct((1, d), f32)],
        compiler_params=_params(("arbitrary",)), name=name,
    )(x, p, dy, wg, wgt, bg, wp, z, ln_g)


def _conv_taps(xpad_ref, w_ref, s):
    acc = w_ref[0:1, :] * xpad_ref[SUBLANES - 3:SUBLANES - 3 + s, :]
    for j in range(1, 4):
        acc = acc + w_ref[j:j + 1, :] * xpad_ref[SUBLANES - 3 + j:SUBLANES - 3 + j + s, :]
    return acc


def conv_fwd(x, w, bias, act, nb, *, name):
    t, c = x.shape
    s = t // nb
    cw = GROUP_W

    def body(x_ref, w_ref, b_ref, y_ref, xpad):
        xpad[0:SUBLANES, :] = jnp.zeros((SUBLANES, cw), f32)
        xpad[SUBLANES:, :] = x_ref[...]
        acc = _conv_taps(xpad, w_ref, s) + b_ref[...]
        y_ref[...] = _silu(acc) if act else acc

    slab = pl.BlockSpec((s, cw), lambda b, g: (b, g))
    return pl.pallas_call(
        body, grid=(nb, c // cw),
        in_specs=[slab, pl.BlockSpec((4, cw), lambda b, g: (0, g)), pl.BlockSpec((1, cw), lambda b, g: (0, g))],
        out_specs=slab, out_shape=jax.ShapeDtypeStruct((t, c), f32),
        scratch_shapes=[pltpu.VMEM((s + SUBLANES, cw), f32)],
        compiler_params=_params(("parallel", "parallel")), name=name,
    )(x, w, bias)


def conv_bwd(x, w, bias, dy, act, nb, *, name):
    t, c = x.shape
    s = t // nb
    cw = GROUP_W

    def body(x_ref, w_ref, b_ref, dy_ref, dx_ref, dw_ref, db_ref, xpad, dpad):
        xpad[0:SUBLANES, :] = jnp.zeros((SUBLANES, cw), f32)
        xpad[SUBLANES:, :] = x_ref[...]
        dacc = dy_ref[...]
        if act:
            acc = _conv_taps(xpad, w_ref, s) + b_ref[...]
            sg = jax.nn.sigmoid(acc)
            dacc = dacc * (sg * (1.0 + acc * (1.0 - sg)))
        dpad[0:s, :] = dacc
        dpad[s:, :] = jnp.zeros((SUBLANES, cw), f32)
        dx = w_ref[0:1, :] * dpad[3:3 + s, :]
        for j in range(1, 4):
            dx = dx + w_ref[j:j + 1, :] * dpad[3 - j:3 - j + s, :]
        dx_ref[...] = dx
        first = pl.program_id(1) == 0
        for j in range(4):
            pw = jnp.sum(dacc * xpad[SUBLANES - 3 + j:SUBLANES - 3 + j + s, :], 0, keepdims=True)

            @pl.when(first)
            def _():
                dw_ref[j:j + 1, :] = pw

            @pl.when(jnp.logical_not(first))
            def _():
                dw_ref[j:j + 1, :] += pw

        pb = jnp.sum(dacc, 0, keepdims=True)

        @pl.when(first)
        def _():
            db_ref[...] = pb

        @pl.when(jnp.logical_not(first))
        def _():
            db_ref[...] += pb

    slab = pl.BlockSpec((s, cw), lambda g, b: (b, g))
    wsp = pl.BlockSpec((4, cw), lambda g, b: (0, g))
    bsp = pl.BlockSpec((1, cw), lambda g, b: (0, g))
    return pl.pallas_call(
        body, grid=(c // cw, nb), in_specs=[slab, wsp, bsp, slab], out_specs=[slab, wsp, bsp],
        out_shape=[jax.ShapeDtypeStruct((t, c), f32), jax.ShapeDtypeStruct((4, c), f32),
                   jax.ShapeDtypeStruct((1, c), f32)],
        scratch_shapes=[pltpu.VMEM((s + SUBLANES, cw), f32), pltpu.VMEM((s + SUBLANES, cw), f32)],
        compiler_params=_params(("parallel", "arbitrary")), name=name,
    )(x, w, bias, dy)


def _each(f, *lists):
    return [f(*a) for a in zip(*lists)]


def _attn_heads(qs, kbs, vbs, sinks, valids, dist, dots):
    nn, nt = dots[:2]
    items = range(len(qs))
    kv = [(i // A_HEADS) * A_KV_HEADS + (i % A_HEADS) // A_GROUP for i in items]
    scs = [nt(qs[i], kbs[kv[i]]) for i in items]
    prs = []
    for i in items:
        h = i % A_HEADS
        sc = scs[i] * (A_HEAD_DIM ** -0.5) - 2.0 ** -(h + 1) * dist
        sc = jnp.where(valids[i // A_HEADS], sc, NEG)
        m = lax.stop_gradient(jnp.maximum(jnp.max(sc, -1, keepdims=True), sinks[h]))
        pr = jnp.exp(sc - m)
        den = jnp.sum(pr, -1, keepdims=True) + jnp.exp(sinks[h] - m)
        prs.append(pr / den)
    return [nn(prs[i], vbs[kv[i]]) for i in items]


A_Q_ROWS = 2 * CHUNK
A_STEPS_PER_TRIP = 4


def _attn_steps(s):
    return A_STEPS_PER_TRIP if s % (A_Q_ROWS * A_STEPS_PER_TRIP) == 0 else 1


def _attn_band_consts(r0):
    band = A_WINDOW + A_Q_ROWS
    qi = lax.broadcasted_iota(jnp.int32, (A_Q_ROWS, band), 0)
    kj = lax.broadcasted_iota(jnp.int32, (A_Q_ROWS, band), 1)
    dist = jnp.abs(qi + A_WINDOW - kj).astype(f32)
    qc, kc = qi // CHUNK, kj // CHUNK
    valid = ((kj + r0) >= A_WINDOW) & (kc >= qc) & (kc <= qc + A_WINDOW // CHUNK)
    return dist, valid


def attn_fwd(qkv, sinks, nb, *, name):
    t = qkv.shape[0]
    s = t // nb
    band = A_WINDOW + A_Q_ROWS
    hd = A_HEAD_DIM

    def body(qkv_ref, sink_ref, o_ref, kvpad):
        kvpad[0:A_WINDOW, :] = jnp.zeros((A_WINDOW, 2 * A_KV_WIDTH), f32)
        kvpad[A_WINDOW:, :] = qkv_ref[:, A_WIDTH:]

        def trip(n, carry):
            r0s = [pl.multiple_of((n * steps + j) * A_Q_ROWS, A_Q_ROWS) for j in range(steps)]
            consts = [_attn_band_consts(r0) for r0 in r0s]
            kbs = [kvpad[pl.ds(r0, band), kvh * hd:(kvh + 1) * hd] for r0 in r0s for kvh in range(A_KV_HEADS)]
            vbs = [kvpad[pl.ds(r0, band), A_KV_WIDTH + kvh * hd:A_KV_WIDTH + (kvh + 1) * hd]
                   for r0 in r0s for kvh in range(A_KV_HEADS)]
            qs = [qkv_ref[pl.ds(r0, A_Q_ROWS), h * hd:(h + 1) * hd] for r0 in r0s for h in range(A_HEADS)]
            outs = _attn_heads(qs, kbs, vbs, [sink_ref[:, h:h + 1] for h in range(A_HEADS)], [c_[1] for c_ in consts],
                               consts[0][0], RAW_DOTS)
            for j, r0 in enumerate(r0s):
                for h in range(A_HEADS):
                    o_ref[pl.ds(r0, A_Q_ROWS), h * hd:(h + 1) * hd] = outs[j * A_HEADS + h]
            return carry

        steps = _attn_steps(s)
        lax.fori_loop(0, s // (A_Q_ROWS * steps), trip, 0)

    return pl.pallas_call(
        body, grid=(nb,),
        in_specs=[pl.BlockSpec((s, A_WIDTH + 2 * A_KV_WIDTH), lambda b: (b, 0)),
                  pl.BlockSpec((1, A_HEADS), lambda b: (0, 0))],
        out_specs=pl.BlockSpec((s, A_WIDTH), lambda b: (b, 0)),
        out_shape=jax.ShapeDtypeStruct((t, A_WIDTH), f32),
        scratch_shapes=[pltpu.VMEM((s + A_WINDOW, 2 * A_KV_WIDTH), f32)],
        compiler_params=_params(("parallel",)), name=name,
    )(qkv, sinks)


def attn_bwd(qkv, sinks, do, nb, *, name):
    t = qkv.shape[0]
    s = t // nb
    band = A_WINDOW + A_Q_ROWS
    hd = A_HEAD_DIM
    kvw = 2 * A_KV_WIDTH

    def body(qkv_ref, sink_ref, do_ref, dqkv_ref, dsink_ref, kvpad, dkvpad):
        kvpad[0:A_WINDOW, :] = jnp.zeros((A_WINDOW, kvw), f32)
        kvpad[A_WINDOW:, :] = qkv_ref[:, A_WIDTH:]
        dkvpad[...] = jnp.zeros((s + A_WINDOW, kvw), f32)

        def trip(n, dsinks):
            r0s = [pl.multiple_of((n * steps + j) * A_Q_ROWS, A_Q_ROWS) for j in range(steps)]
            consts = [_attn_band_consts(r0) for r0 in r0s]
            ksl = [slice(kvh * hd, (kvh + 1) * hd) for kvh in range(A_KV_HEADS)]
            vsl = [slice(A_KV_WIDTH + kvh * hd, A_KV_WIDTH + (kvh + 1) * hd) for kvh in range(A_KV_HEADS)]
            kbs = [kvpad[pl.ds(r0, band), sl] for r0 in r0s for sl in ksl]
            vbs = [kvpad[pl.ds(r0, band), sl] for r0 in r0s for sl in vsl]
            qs = [qkv_ref[pl.ds(r0, A_Q_ROWS), h * hd:(h + 1) * hd] for r0 in r0s for h in range(A_HEADS)]
            dos = [do_ref[pl.ds(r0, A_Q_ROWS), h * hd:(h + 1) * hd] for r0 in r0s for h in range(A_HEADS)]
            fn = functools.partial(_attn_heads, valids=[c_[1] for c_ in consts], dist=consts[0][0], dots=VJP_DOTS)
            _, vjp = jax.vjp(fn, qs, kbs, vbs, [sink_ref[:, h:h + 1] for h in range(A_HEADS)])
            dqs, dks, dvs, dss = vjp(dos)
            for j, r0 in enumerate(r0s):
                for h in range(A_HEADS):
                    dqkv_ref[pl.ds(r0, A_Q_ROWS), h * hd:(h + 1) * hd] = dqs[j * A_HEADS + h]
            for j, r0 in enumerate(r0s):
                for kvh in range(A_KV_HEADS):
                    dkvpad[pl.ds(r0, band), ksl[kvh]] += dks[j * A_KV_HEADS + kvh]
                    dkvpad[pl.ds(r0, band), vsl[kvh]] += dvs[j * A_KV_HEADS + kvh]
            return tuple(dsinks[h] + dss[h] for h in range(A_HEADS))

        steps = _attn_steps(s)
        dsinks = lax.fori_loop(0, s // (A_Q_ROWS * steps), trip, tuple(jnp.zeros((1, 1), f32) for _ in range(A_HEADS)))
        dqkv_ref[:, A_WIDTH:] = dkvpad[A_WINDOW:, :]
        first = pl.program_id(0) == 0
        for h in range(A_HEADS):
            @pl.when(first)
            def _():
                dsink_ref[:, h:h + 1] = dsinks[h]

            @pl.when(jnp.logical_not(first))
            def _():
                dsink_ref[:, h:h + 1] += dsinks[h]

    wq = A_WIDTH + kvw
    return pl.pallas_call(
        body, grid=(nb,),
        in_specs=[pl.BlockSpec((s, wq), lambda b: (b, 0)), pl.BlockSpec((1, A_HEADS), lambda b: (0, 0)),
                  pl.BlockSpec((s, A_WIDTH), lambda b: (b, 0))],
        out_specs=[pl.BlockSpec((s, wq), lambda b: (b, 0)), pl.BlockSpec((1, A_HEADS), lambda b: (0, 0))],
        out_shape=[jax.ShapeDtypeStruct((t, wq), f32), jax.ShapeDtypeStruct((1, A_HEADS), f32)],
        scratch_shapes=[pltpu.VMEM((s + A_WINDOW, kvw), f32), pltpu.VMEM((s + A_WINDOW, kvw), f32)],
        compiler_params=_params(("arbitrary",)), name=name,
    )(qkv, sinks, do)


def _rg_gates(xc, wa, wx, ba, bx, lam, nn):
    r = jax.nn.sigmoid(nn(xc, wa) + ba)
    i = jax.nn.sigmoid(nn(xc, wx) + bx)
    log_a = -RG_C * r * jax.nn.softplus(-lam)
    a = jnp.exp(log_a)
    mult = jnp.sqrt(-jnp.tanh(log_a) * (jnp.exp(2.0 * log_a) + 1.0))
    return a, mult * (i * xc)


def _linear_scan(a, u, reverse):
    s = a.shape[0]
    t = lax.broadcasted_iota(jnp.int32, a.shape, 0)
    d = 1
    while d < s:
        if reverse:
            keep = t < s - d
            shift = s - d
        else:
            keep = t >= d
            shift = d
        us = jnp.where(keep, pltpu.roll(u, shift, 0), 0.0)
        as_ = jnp.where(keep, pltpu.roll(a, shift, 0), 1.0)
        u = u + a * us
        a = a * as_
        d *= 2
    return u


def rglru_fwd(xc, bg, wa, wx, ba, bx, lam, nb, *, name):
    t, c = xc.shape
    s = t // nb
    cw = GROUP_W

    def body(xc_ref, bg_ref, wa_ref, wx_ref, ba_ref, bx_ref, lam_ref, y_ref, h_ref):
        a, u = _rg_gates(xc_ref[...], wa_ref[...], wx_ref[...], ba_ref[...], bx_ref[...], lam_ref[...], RAW_DOTS[0])
        h = _linear_scan(a, u, False)
        h_ref[...] = h
        y_ref[...] = h * jax.nn.gelu(bg_ref[...])

    slab = pl.BlockSpec((s, cw), lambda b, g: (b, g))
    wsp = pl.BlockSpec((None, cw, cw), lambda b, g: (g, 0, 0))
    vec = pl.BlockSpec((1, cw), lambda b, g: (0, g))
    return pl.pallas_call(
        body, grid=(nb, c // cw), in_specs=[slab, slab, wsp, wsp, vec, vec, vec], out_specs=[slab, slab],
        out_shape=[jax.ShapeDtypeStruct((t, c), f32)] * 2,
        compiler_params=_params(("parallel", "parallel")), name=name,
    )(xc, bg, wa, wx, ba, bx, lam)


def rglru_bwd(xc, bg, h, dy, wa, wx, ba, bx, lam, nb, *, name):
    t, c = xc.shape
    s = t // nb
    cw = GROUP_W

    def body(xc_ref, bg_ref, h_ref, dy_ref, wa_ref, wx_ref, ba_ref, bx_ref, lam_ref,
             dxc_ref, dbg_ref, dwa_ref, dwx_ref, dba_ref, dbx_ref, dlam_ref):
        h = h_ref[...]
        dy_ = dy_ref[...]
        gel, gel_vjp = jax.vjp(jax.nn.gelu, bg_ref[...])
        dbg_ref[...] = gel_vjp(dy_ * h)[0]
        dh = dy_ * gel
        gates = functools.partial(_rg_gates, nn=_bnn)
        (a, _), gates_vjp = jax.vjp(gates, xc_ref[...], wa_ref[...], wx_ref[...], ba_ref[...], bx_ref[...],
                                    lam_ref[...])
        ti = lax.broadcasted_iota(jnp.int32, a.shape, 0)
        a_next = jnp.where(ti < s - 1, pltpu.roll(a, s - 1, 0), 0.0)
        lam_t = _linear_scan(a_next, dh, True)
        h_prev = jnp.where(ti >= 1, pltpu.roll(h, 1, 0), 0.0)
        dxc, dwa, dwx, dba, dbx, dlam = gates_vjp((lam_t * h_prev, lam_t))
        dxc_ref[...] = dxc
        first = pl.program_id(1) == 0

        @pl.when(first)
        def _():
            dwa_ref[...] = dwa
            dwx_ref[...] = dwx
            dba_ref[...] = dba
            dbx_ref[...] = dbx
            dlam_ref[...] = dlam

        @pl.when(jnp.logical_not(first))
        def _():
            dwa_ref[...] += dwa
            dwx_ref[...] += dwx
            dba_ref[...] += dba
            dbx_ref[...] += dbx
            dlam_ref[...] += dlam

    slab = pl.BlockSpec((s, cw), lambda g, b: (b, g))
    wsp = pl.BlockSpec((None, cw, cw), lambda g, b: (g, 0, 0))
    vec = pl.BlockSpec((1, cw), lambda g, b: (0, g))
    ng = c // cw
    return pl.pallas_call(
        body, grid=(ng, nb), in_specs=[slab, slab, slab, slab, wsp, wsp, vec, vec, vec],
        out_specs=[slab, slab, wsp, wsp, vec, vec, vec],
        out_shape=[jax.ShapeDtypeStruct((t, c), f32), jax.ShapeDtypeStruct((t, c), f32),
                   jax.ShapeDtypeStruct((ng, cw, cw), f32), jax.ShapeDtypeStruct((ng, cw, cw), f32),
                   jax.ShapeDtypeStruct((1, c), f32), jax.ShapeDtypeStruct((1, c), f32),
                   jax.ShapeDtypeStruct((1, c), f32)],
        compiler_params=_params(("parallel", "arbitrary")), name=name,
    )(xc, bg, h, dy, wa, wx, ba, bx, lam)


def _gdn_chunks_prep(qs, ks, vs, bls, als, a_log, dt_b, dots):
    nn, nt, csum = dots[0], dots[1], dots[3]
    hd = C_HEAD_DIM
    ri = lax.broadcasted_iota(jnp.int32, (CHUNK, CHUNK), 0)
    ci = lax.broadcasted_iota(jnp.int32, (CHUNK, CHUNK), 1)
    tril = ri >= ci
    strict = ri > ci
    eye = (ri == ci).astype(f32)
    qn = [q * lax.rsqrt(jnp.sum(q * q, -1, keepdims=True) + NORM_EPS) * (hd ** -0.5) for q in qs]
    kn = [k * lax.rsqrt(jnp.sum(k * k, -1, keepdims=True) + NORM_EPS) for k in ks]
    beta = [jax.nn.sigmoid(bl) for bl in bls]
    g = [-jnp.exp(a_log) * jax.nn.softplus(al + dt_b) for al in als]
    gc_sq = [csum(jnp.broadcast_to(g_, (CHUNK, CHUNK))) for g_ in g]
    gc = [csum(jnp.broadcast_to(g_, (CHUNK, hd))) for g_ in g]
    decay = [jnp.where(tril, jnp.exp(jnp.where(tril, s - s.T, 0.0)), 0.0) for s in gc_sq]
    kb = _each(jnp.multiply, kn, beta)
    kk = _each(nt, kb, kn)
    pw = [-jnp.where(strict, a * d, 0.0) for a, d in zip(kk, decay)]
    inv = [eye + p_ for p_ in pw]
    for _ in range(5):
        pw = _each(nn, pw, pw)
        inv = _each(jnp.add, inv, _each(nn, inv, pw))
    egc = [jnp.exp(c_) for c_ in gc]
    u = _each(nn, inv, _each(jnp.multiply, vs, beta))
    w = _each(nn, inv, _each(jnp.multiply, kb, egc))
    attn = _each(jnp.multiply, _each(nt, qn, kn), decay)
    g_last = [jnp.sum(jnp.broadcast_to(g_, (CHUNK, hd)), 0, keepdims=True) for g_ in g]
    qg = _each(jnp.multiply, qn, egc)
    kdec = [k_ * jnp.exp(gl_ - c_) for k_, gl_, c_ in zip(kn, g_last, gc)]
    return [(qg[i], kdec[i], w[i], u[i], attn[i], jnp.exp(g_last[i])) for i in range(len(qs))]


def _gdn_heads_step(states, qgs, kdecs, ws, us, attns, gls, zs, ng, dots):
    nn, tn = dots[0], dots[2]
    v_new = _each(jnp.subtract, us, _each(nn, ws, states))
    o = _each(jnp.add, _each(nn, qgs, states), _each(nn, attns, v_new))
    new = [s * gl for s, gl in zip(states, gls)]
    new = _each(jnp.add, new, _each(tn, kdecs, v_new))
    y = [o_ * lax.rsqrt(jnp.mean(o_ * o_, -1, keepdims=True) + NORM_EPS) * ng * _silu(z) for o_, z in zip(o, zs)]
    return y, new


def _loop_unrolled(n, unroll, load, compute, store, init):
    u = unroll if n % unroll == 0 else 1

    def trip(i, carry):
        idx = [i * u + j for j in range(u)]
        loaded = [load(k) for k in idx]
        results = compute(loaded)
        for k, r in zip(idx, results):
            carry = store(k, r, carry)
        return carry

    return lax.fori_loop(0, n // u, trip, init)


def _pick_lane(x, lane):
    li = lax.broadcasted_iota(jnp.int32, x.shape, 1)
    return jnp.sum(jnp.where(li == lane, x, 0.0), 1, keepdims=True)


def _put_lane(col, lane, width):
    li = lax.broadcasted_iota(jnp.int32, (col.shape[0], width), 1)
    return jnp.where(li == lane, col, 0.0)


def _gdn_specs(s, nc):
    hd = C_HEAD_DIM
    head = lambda off: pl.BlockSpec((s, hd), lambda b, h, off=off: (b, off + h))
    attn = pl.BlockSpec((None, s, CHUNK), lambda b, h: (h, b, 0))
    gl = pl.BlockSpec((None, nc * SUBLANES, hd), lambda b, h: (h, b, 0))
    ba = pl.BlockSpec((s, LANES), lambda b, h: (b, 0))
    sc8 = pl.BlockSpec((1, C_HEADS), lambda b, h: (0, 0))
    return head, attn, gl, ba, sc8


def gdn_prep_fwd(qkv, ba, a_log, dt_b, nb, *, name):
    t = qkv.shape[0]
    s = t // nb
    nc = s // CHUNK
    hd = C_HEAD_DIM
    head, attn_sp, gl_sp, ba_sp, sc8 = _gdn_specs(s, nc)

    def body(q_ref, k_ref, v_ref, ba_ref, alog_ref, dtb_ref, qg_ref, kd_ref, w_ref, u_ref, at_ref, gl_ref):
        h = pl.program_id(1)
        a_log_h = _pick_lane(alog_ref[...], h)
        dt_b_h = _pick_lane(dtb_ref[...], h)

        def load(n):
            rows = pl.ds(pl.multiple_of(n * CHUNK, CHUNK), CHUNK)
            bav = ba_ref[rows, :]
            return q_ref[rows, :], k_ref[rows, :], v_ref[rows, :], _pick_lane(bav, h), _pick_lane(bav, C_HEADS + h)

        def compute(loaded):
            return _gdn_chunks_prep(*[list(x) for x in zip(*loaded)], a_log_h, dt_b_h, RAW_DOTS)

        def store(n, outs, carry):
            rows = pl.ds(pl.multiple_of(n * CHUNK, CHUNK), CHUNK)
            qg_ref[rows, :] = outs[0].astype(bf16)
            kd_ref[rows, :] = outs[1].astype(bf16)
            w_ref[rows, :] = outs[2].astype(bf16)
            u_ref[rows, :] = outs[3]
            at_ref[rows, :] = outs[4].astype(bf16)
            gl_ref[pl.ds(pl.multiple_of(n * SUBLANES, SUBLANES), SUBLANES), :] = jnp.broadcast_to(outs[5], (SUBLANES, hd))
            return carry

        _loop_unrolled(nc, PREP_FWD_UNROLL, load, compute, store, 0)

    big = jax.ShapeDtypeStruct((t, C_WIDTH), f32)
    bigb = jax.ShapeDtypeStruct((t, C_WIDTH), bf16)
    return pl.pallas_call(
        body, grid=(nb, C_HEADS),
        in_specs=[head(0), head(C_HEADS), head(2 * C_HEADS), ba_sp, sc8, sc8],
        out_specs=[head(0)] * 4 + [attn_sp, gl_sp],
        out_shape=[bigb, bigb, bigb, big, jax.ShapeDtypeStruct((C_HEADS, t, CHUNK), bf16),
                               jax.ShapeDtypeStruct((C_HEADS, nb * nc * SUBLANES, hd), f32)],
        compiler_params=_params(("parallel", "parallel")), name=name,
    )(qkv, qkv, qkv, ba, a_log, dt_b)


def gdn_prep_bwd(qkv, ba, a_log, dt_b, cts, nb, *, name):
    t = qkv.shape[0]
    s = t // nb
    nc = s // CHUNK
    hd = C_HEAD_DIM
    head, attn_sp, gl_sp, ba_sp, sc8 = _gdn_specs(s, nc)

    def body(q_ref, k_ref, v_ref, ba_ref, alog_ref, dtb_ref, cqg, ckd, cw_, cu, cat, cgl,
             dq_ref, dk_ref, dv_ref, dba_ref, dalog_ref, ddtb_ref):
        b = pl.program_id(0)
        h = pl.program_id(1)
        a_log_h = _pick_lane(alog_ref[...], h)
        dt_b_h = _pick_lane(dtb_ref[...], h)
        prep = functools.partial(_gdn_chunks_prep, dots=VJP_DOTS)

        @pl.when(h == 0)
        def _():
            dba_ref[...] = jnp.zeros((s, LANES), f32)

        def load(n):
            rows = pl.ds(pl.multiple_of(n * CHUNK, CHUNK), CHUNK)
            bav = ba_ref[rows, :]
            cgl_n = cgl[pl.ds(pl.multiple_of(n * SUBLANES, SUBLANES), SUBLANES), :][0:1, :]
            primals = (q_ref[rows, :], k_ref[rows, :], v_ref[rows, :], _pick_lane(bav, h), _pick_lane(bav, C_HEADS + h))
            return primals, (cqg[rows, :], ckd[rows, :], cw_[rows, :], cu[rows, :], cat[rows, :], cgl_n), dba_ref[rows, :]

        def compute(loaded):
            primals = [list(x) for x in zip(*[item[0] for item in loaded])]
            _, vjp = jax.vjp(prep, *primals, a_log_h, dt_b_h)
            dqs, dks, dvs, dbls, dals, dalog, ddtb = vjp([item[1] for item in loaded])
            zero = jnp.zeros((1, 1), f32)
            return [((dqs[i], dks[i], dvs[i], dbls[i], dals[i], dalog if i == 0 else zero, ddtb if i == 0 else zero),
                     loaded[i][2]) for i in range(len(loaded))]

        def store(n, res, carry):
            (dq, dk, dv, dbl, dal, dalog_n, ddtb_n), dba_old = res
            rows = pl.ds(pl.multiple_of(n * CHUNK, CHUNK), CHUNK)
            dq_ref[rows, :] = dq
            dk_ref[rows, :] = dk
            dv_ref[rows, :] = dv
            dba_ref[rows, :] = dba_old + _put_lane(dbl, h, LANES) + _put_lane(dal, C_HEADS + h, LANES)
            return carry[0] + dalog_n, carry[1] + ddtb_n

        da_log, ddt_b = _loop_unrolled(nc, PREP_BWD_UNROLL, load, compute, store,
                                       (jnp.zeros((1, 1), f32), jnp.zeros((1, 1), f32)))
        first = jnp.logical_and(b == 0, h == 0)

        @pl.when(first)
        def _():
            dalog_ref[...] = _put_lane(da_log, h, LANES)
            ddtb_ref[...] = _put_lane(ddt_b, h, LANES)

        @pl.when(jnp.logical_not(first))
        def _():
            dalog_ref[...] += _put_lane(da_log, h, LANES)
            ddtb_ref[...] += _put_lane(ddt_b, h, LANES)

    big = jax.ShapeDtypeStruct((t, C_WIDTH), f32)
    vec = pl.BlockSpec((1, LANES), lambda b, h: (0, 0))
    return pl.pallas_call(
        body, grid=(nb, C_HEADS),
        in_specs=[head(0), head(C_HEADS), head(2 * C_HEADS), ba_sp, sc8, sc8] + [head(0)] * 4 + [attn_sp, gl_sp],
        out_specs=[head(0)] * 3 + [ba_sp, vec, vec],
        out_shape=[big] * 3 + [jax.ShapeDtypeStruct((t, LANES), f32), jax.ShapeDtypeStruct((1, LANES), f32),
                               jax.ShapeDtypeStruct((1, LANES), f32)],
        compiler_params=_params(("arbitrary", "arbitrary")), name=name,
    )(qkv, qkv, qkv, ba, a_log, dt_b, *cts)


def _gdn_rec_specs(sb, nsb, hp, reverse):
    hd = C_HEAD_DIM
    ncb = sb // CHUNK
    blk = (lambda b, k: b * nsb + (nsb - 1 - k)) if reverse else (lambda b, k: b * nsb + k)
    wide = pl.BlockSpec((sb, hp * hd), lambda b, j, k: (blk(b, k), j))
    attn = pl.BlockSpec((hp, sb, CHUNK), lambda b, j, k: (j, blk(b, k), 0))
    gl = pl.BlockSpec((hp, ncb * SUBLANES, hd), lambda b, j, k: (j, blk(b, k), 0))
    ng = pl.BlockSpec((1, hd), lambda b, j, k: (0, 0))
    states = pl.BlockSpec((hp, ncb, hd, hd), lambda b, j, k: (j, blk(b, k), 0, 0))
    return wide, attn, gl, ng, states


def gdn_rec_fwd(qg, kdec, w, u, attn, gl, z, ng, nb, *, name):
    t = qg.shape[0]
    s = t // nb
    sb = min(s, GDN_TIME_BLOCK)
    nsb = s // sb
    hd = C_HEAD_DIM
    hp = C_HEADS_PER_STEP
    wide, attn_sp, gl_sp, ng_sp, st_sp = _gdn_rec_specs(sb, nsb, hp, False)

    def body(qg_ref, kd_ref, w_ref, u_ref, at_ref, gl_ref, z_ref, ng_ref, y_ref, st_ref, carry_ref):
        @pl.when(pl.program_id(2) == 0)
        def _():
            carry_ref[...] = jnp.zeros((hp, hd, hd), f32)

        def chunk(n, states):
            for j in range(hp):
                st_ref[j, n] = states[j]
            rows = pl.ds(pl.multiple_of(n * CHUNK, CHUNK), CHUNK)
            grow = pl.ds(pl.multiple_of(n * SUBLANES, SUBLANES), SUBLANES)
            cols = [slice(j * hd, (j + 1) * hd) for j in range(hp)]
            ins = [(qg_ref[rows, c], kd_ref[rows, c], w_ref[rows, c], u_ref[rows, c], at_ref[j, rows, :],
                    gl_ref[j, grow, :][0:1, :], z_ref[rows, c]) for j, c in enumerate(cols)]
            ys, new = _gdn_heads_step(list(states), *[list(x) for x in zip(*ins)], ng_ref[...], RAW_DOTS)
            for j in range(hp):
                y_ref[rows, cols[j]] = ys[j]
            return tuple(new)

        last = lax.fori_loop(0, sb // CHUNK, chunk, tuple(carry_ref[j] for j in range(hp)))
        for j in range(hp):
            carry_ref[j] = last[j]

    return pl.pallas_call(
        body, grid=(nb, C_HEADS // hp, nsb),
        in_specs=[wide] * 4 + [attn_sp, gl_sp, wide, ng_sp], out_specs=[wide, st_sp],
        out_shape=[jax.ShapeDtypeStruct((t, C_WIDTH), f32), jax.ShapeDtypeStruct((C_HEADS, t // CHUNK, hd, hd), f32)],
        scratch_shapes=[pltpu.VMEM((hp, hd, hd), f32)],
        compiler_params=_params(("parallel", "parallel", "arbitrary")), name=name,
    )(qg, kdec, w, u, attn, gl, z, ng)


def gdn_rec_bwd(qg, kdec, w, u, attn, gl, z, ng, states, dy, nb, *, name):
    t = qg.shape[0]
    s = t // nb
    sb = min(s, GDN_TIME_BLOCK)
    nsb = s // sb
    nc = sb // CHUNK
    hd = C_HEAD_DIM
    hp = C_HEADS_PER_STEP
    wide, attn_sp, gl_sp, ng_sp, st_sp = _gdn_rec_specs(sb, nsb, hp, True)

    def body(qg_ref, kd_ref, w_ref, u_ref, at_ref, gl_ref, z_ref, ng_ref, states, dy_ref,
             dqg_ref, dkd_ref, dw_ref, du_ref, dat_ref, dgl_ref, dz_ref, dng_ref, carry_ref):
        step = functools.partial(_gdn_heads_step, dots=VJP_DOTS)

        @pl.when(pl.program_id(2) == 0)
        def _():
            carry_ref[...] = jnp.zeros((hp, hd, hd), f32)

        def operands(n):
            rows = pl.ds(pl.multiple_of(n * CHUNK, CHUNK), CHUNK)
            grow = pl.ds(pl.multiple_of(n * SUBLANES, SUBLANES), SUBLANES)
            cols = [slice(j * hd, (j + 1) * hd) for j in range(hp)]
            return ([qg_ref[rows, c].astype(f32) for c in cols], [kd_ref[rows, c].astype(f32) for c in cols],
                    [w_ref[rows, c].astype(f32) for c in cols], [u_ref[rows, c] for c in cols],
                    [at_ref[j, rows, :].astype(f32) for j in range(hp)],
                    [gl_ref[j, grow, :][0:1, :] for j in range(hp)], [z_ref[rows, c] for c in cols])

        def bwd_chunk(i, carry):
            n = nc - 1 - i
            rows = pl.ds(pl.multiple_of(n * CHUNK, CHUNK), CHUNK)
            grow = pl.ds(pl.multiple_of(n * SUBLANES, SUBLANES), SUBLANES)
            dsts, dng = carry
            dys = [dy_ref[rows, j * hd:(j + 1) * hd] for j in range(hp)]
            _, vjp = jax.vjp(step, [states[j, n] for j in range(hp)], *operands(n), ng_ref[...])
            dst, dqg, dkd, dw, du, dat, dgl, dz, dng_n = vjp((dys, list(dsts)))
            for j in range(hp):
                cols = slice(j * hd, (j + 1) * hd)
                dqg_ref[rows, cols] = dqg[j]
                dkd_ref[rows, cols] = dkd[j]
                dw_ref[rows, cols] = dw[j]
                du_ref[rows, cols] = du[j]
                dat_ref[j, rows, :] = dat[j]
                dgl_ref[j, grow, :] = jnp.broadcast_to(dgl[j], (SUBLANES, hd))
                dz_ref[rows, cols] = dz[j]
            return tuple(dst), dng + dng_n

        dlast, dng = lax.fori_loop(0, nc, bwd_chunk,
                                   (tuple(carry_ref[j] for j in range(hp)), jnp.zeros((1, hd), f32)))
        for j in range(hp):
            carry_ref[j] = dlast[j]
        first = jnp.logical_and(jnp.logical_and(pl.program_id(0) == 0, pl.program_id(1) == 0), pl.program_id(2) == 0)

        @pl.when(first)
        def _():
            dng_ref[...] = dng

        @pl.when(jnp.logical_not(first))
        def _():
            dng_ref[...] += dng

    big = jax.ShapeDtypeStruct((t, C_WIDTH), f32)
    return pl.pallas_call(
        body, grid=(nb, C_HEADS // hp, nsb),
        in_specs=[wide] * 4 + [attn_sp, gl_sp, wide, ng_sp, st_sp, wide],
        out_specs=[wide] * 4 + [attn_sp, gl_sp, wide, ng_sp],
        out_shape=[big] * 4 + [jax.ShapeDtypeStruct(attn.shape, f32), jax.ShapeDtypeStruct(gl.shape, f32), big,
                               jax.ShapeDtypeStruct((1, hd), f32)],
        scratch_shapes=[pltpu.VMEM((hp, hd, hd), f32)],
        compiler_params=_params(("arbitrary", "arbitrary", "arbitrary")), name=name,
    )(qg, kdec, w, u, attn, gl, z, ng, states, dy)


def _blockdiag_slabs(w):
    per = GROUP_W // B_BLOCK
    slabs = jnp.zeros((B_BLOCKS // per, GROUP_W, GROUP_W), w.dtype)
    for h in range(B_BLOCKS):
        o = (h % per) * B_BLOCK
        slabs = slabs.at[h // per, o:o + B_BLOCK, o:o + B_BLOCK].set(w[h])
    return slabs


def _slab_blocks(slabs):
    per = GROUP_W // B_BLOCK
    return jnp.stack([slabs[h // per, (h % per) * B_BLOCK:(h % per + 1) * B_BLOCK,
                            (h % per) * B_BLOCK:(h % per + 1) * B_BLOCK] for h in range(B_BLOCKS)])


def _mixer_ab_fwd(x1, x1b, W, g, b, nb, tag):
    w_in = W["ab_w_in"][0].astype(bf16)
    o1, o2 = A_WIDTH + 2 * A_KV_WIDTH, A_WIDTH + 2 * A_KV_WIDTH + B_WIDTH
    w_qkv, w_bx, w_bg = w_in[:, :o1], w_in[:, o1:o2], w_in[:, o2:]
    pqkv = mm_nn(x1b,w_qkv, name=tag + "_in_qkv")
    pbx = mm_nn(x1b,w_bx, name=tag + "_in_bx")
    pbg = mm_nn(x1b,w_bg, name=tag + "_in_bg")
    ya = attn_fwd(pqkv, W["a_sinks"], nb, name=tag + "_attn_fwd")
    xc = conv_fwd(pbx, W["b_conv_w"][0], W["b_conv_b"], False, nb, name=tag + "_conv_fwd")
    wa_s, wx_s = _blockdiag_slabs(W["b_wa"][0]), _blockdiag_slabs(W["b_wx"][0])
    yb, hh = rglru_fwd(xc, pbg, wa_s, wx_s, W["b_ba"], W["b_bx"], W["b_lam"], nb, name=tag + "_rglru_fwd")
    w_out = W["ab_w_out"][0].astype(bf16)
    x2, z1, x2b = proj_ln([ya, yb], [w_out[:A_WIDTH], w_out[A_WIDTH:]], x1, g, b, name=tag + "_out_ln")
    saved = (pqkv, pbx, pbg, ya, xc, yb, hh, wa_s, wx_s, w_qkv, w_bx, w_bg, w_out)
    return x2, x2b, z1, saved


def _mixer_ab_bwd(x1b, dz1, dz1b, W, saved, nb, tag):
    pqkv, pbx, pbg, ya, xc, yb, hh, wa_s, wx_s, w_qkv, w_bx, w_bg, w_out = saved
    dya = mm_nn(dz1b, w_out[:A_WIDTH].T, name=tag + "_dya")
    dyb = mm_nn(dz1b, w_out[A_WIDTH:].T, name=tag + "_dyb")
    dwo = jnp.concatenate([mm_tn(ya, dz1b, name=tag + "_dwo_a"), mm_tn(yb, dz1b, name=tag + "_dwo_b")], 0)
    dpqkv, dsinks = attn_bwd(pqkv, W["a_sinks"], dya, nb, name=tag + "_attn_bwd")
    dxc, dpbg, dwa_s, dwx_s, dba, dbx, dlam = rglru_bwd(xc, pbg, hh, dyb, wa_s, wx_s, W["b_ba"], W["b_bx"],
                                                       W["b_lam"], nb, name=tag + "_rglru_bwd")
    dpbx, dconv_w, dconv_b = conv_bwd(pbx, W["b_conv_w"][0], W["b_conv_b"], dxc, False, nb, name=tag + "_conv_bwd")
    dw_in = jnp.concatenate([mm_tn(x1b,dpqkv, name=tag + "_dwin_qkv"), mm_tn(x1b,dpbx, name=tag + "_dwin_bx"),
                             mm_tn(x1b,dpbg, name=tag + "_dwin_bg")], 1)
    dx1 = mm_nn(dpqkv, w_qkv.T, add=dz1, add_scale=DN_ALPHA, name=tag + "_dx_qkv")
    dx1 = mm_nn(dpbx, w_bx.T, add=dx1, name=tag + "_dx_bx")
    dx1 = mm_nn(dpbg, w_bg.T, add=dx1, name=tag + "_dx_bg")
    grads = {"ab_w_in": dw_in[None], "a_sinks": dsinks, "b_conv_w": dconv_w[None], "b_conv_b": dconv_b,
             "b_wa": _slab_blocks(dwa_s)[None], "b_ba": dba, "b_wx": _slab_blocks(dwx_s)[None], "b_bx": dbx,
             "b_lam": dlam, "ab_w_out": dwo[None]}
    return dx1, grads


def _mixer_c_fwd(x1, x1b, W, g, b, nb, tag):
    w_in = W["c_w_in"][0].astype(bf16)
    d = w_in.shape[0]
    o1, o2 = 3 * C_WIDTH, 4 * C_WIDTH
    w_qkv, w_z = w_in[:, :o1], w_in[:, o1:o2]
    w_ba = jnp.concatenate([w_in[:, o2:], jnp.zeros((d, LANES - 2 * C_HEADS), bf16)], 1)
    pqkv = mm_nn(x1b,w_qkv, name=tag + "_in_qkv")
    pz = mm_nn(x1b,w_z, name=tag + "_in_z")
    pba = mm_nn(x1b,w_ba, name=tag + "_in_ba")
    zero_b = jnp.zeros((1, o1), f32)
    qkvc = conv_fwd(pqkv, W["c_conv_w"][0], zero_b, True, nb, name=tag + "_conv_fwd")
    prep = gdn_prep_fwd(qkvc, pba, W["c_a_log"], W["c_dt_bias"], nb, name=tag + "_prep_fwd")
    yc, states = gdn_rec_fwd(*prep, pz, W["c_norm_g"], nb, name=tag + "_rec_fwd")
    w_out = W["c_w_out"][0].astype(bf16)
    x2, z1, x2b = proj_ln([yc], [w_out], x1, g, b, name=tag + "_out_ln")
    saved = (pqkv, pz, pba, qkvc, prep, states, yc, w_qkv, w_z, w_ba, w_out, zero_b)
    return x2, x2b, z1, saved


def _mixer_c_bwd(x1b, dz1, dz1b, W, saved, nb, tag):
    pqkv, pz, pba, qkvc, prep, states, yc, w_qkv, w_z, w_ba, w_out, zero_b = saved
    dyc = mm_nn(dz1b, w_out.T, name=tag + "_dyc")
    dwo = mm_tn(yc, dz1b, name=tag + "_dwo")
    rec = gdn_rec_bwd(*prep, pz, W["c_norm_g"], states, dyc, nb, name=tag + "_rec_bwd")
    cts, dpz, dng = rec[:6], rec[6], rec[7]
    dq, dk, dv, dpba, dalog, ddtb = gdn_prep_bwd(qkvc, pba, W["c_a_log"], W["c_dt_bias"], cts, nb,
                                                 name=tag + "_prep_bwd")
    dqkvc = jnp.concatenate([dq, dk, dv], 1)
    dpqkv, dconv_w, _ = conv_bwd(pqkv, W["c_conv_w"][0], zero_b, dqkvc, True, nb, name=tag + "_conv_bwd")
    dw_in = jnp.concatenate([mm_tn(x1b,dpqkv, name=tag + "_dwin_qkv"), mm_tn(x1b,dpz, name=tag + "_dwin_z"),
                             mm_tn(x1b,dpba, name=tag + "_dwin_ba")[:, :2 * C_HEADS]], 1)
    dx1 = mm_nn(dpqkv, w_qkv.T, add=dz1, add_scale=DN_ALPHA, name=tag + "_dx_qkv")
    dx1 = mm_nn(dpz, w_z.T, add=dx1, name=tag + "_dx_z")
    dx1 = mm_nn(dpba, w_ba.T, add=dx1, name=tag + "_dx_ba")
    grads = {"c_w_in": dw_in[None], "c_conv_w": dconv_w[None], "c_a_log": dalog[:, :C_HEADS],
             "c_dt_bias": ddtb[:, :C_HEADS], "c_norm_g": dng, "c_w_out": dwo[None]}
    return dx1, grads


def _local_step(x, p, target, W, F, on_ffn_grads):
    nb, s, d = x.shape
    t = nb * s
    h = x.reshape(t, d)
    hb = h.astype(bf16)
    tape = []
    for i in range(DEPTH):
        tag = f"l{i}"
        f1 = [F[k][i] for k in ("ffn1_wg", "ffn1_wu", "ffn1_wd")]
        f2 = [F[k][i] for k in ("ffn2_wg", "ffn2_wu", "ffn2_wd")]
        lg = [W["ln_g"][i, k][None] for k in range(3)]
        lb = [W["ln_b"][i, k][None] for k in range(3)]
        x1, z0, x1b = ffn_fwd(h, *f1, lg[0], lb[0], name=tag + "_ffn1_fwd")
        mixer = _mixer_ab_fwd if i % 2 == 0 else _mixer_c_fwd
        x2, x2b, z1, msaved = mixer(x1, x1b, W, lg[1], lb[1], nb, tag + "_mix")
        x3, z2, _ = ffn_fwd(x2, *f2, lg[2], lb[2], name=tag + "_ffn2_fwd")
        pi = p[i].reshape(t, -1)
        pw = (W["ple_wg"][i].astype(bf16), W["ple_bg"][i][None], W["ple_wp"][i].astype(bf16))
        x4, x4b = ple_fwd(x3, pi, *pw, name=tag + "_ple_fwd")
        tape.append((hb, z0, x1b, msaved, z1, x2b, z2, x3, pi, pw, lg))
        h, hb = x4, x4b
    dh, sq = loss_head(h, target.reshape(t, d), name="loss_head")
    loss = 0.5 * jnp.sum(sq) / d
    per_layer = [None] * DEPTH
    grads = {}
    for i in reversed(range(DEPTH)):
        tag = f"l{i}"
        hb_in, z0, x1b, msaved, z1, x2b, z2, x3, pi, pw, lg = tape[i]
        dz2, dz2b, dple_wg, dple_bg, dple_wp, dg2, db2 = ple_bwd(x3, pi, dh, pw[0], pw[0].T, pw[1], pw[2], z2, lg[2],
                                                                 name=tag + "_ple_ln2_bwd")
        f1 = [F[k][i] for k in ("ffn1_wg", "ffn1_wu", "ffn1_wd")]
        f2 = [F[k][i] for k in ("ffn2_wg", "ffn2_wu", "ffn2_wd")]
        dgate, dup, *df2 = ffn_bwd_weights(x2b, dz2b, *f2, name=tag + "_ffn2_bwd_w")
        on_ffn_grads(i, 3, df2)
        dz1, dz1b, dg1, db1 = ffn_bwd_input_ln(dgate, dup, f2[0], f2[1], dz2, z1, lg[1], name=tag + "_ffn2_bwd_x_ln1")
        mixer_bwd = _mixer_ab_bwd if i % 2 == 0 else _mixer_c_bwd
        dx1, mgrads = mixer_bwd(x1b, dz1, dz1b, W, msaved, nb, tag + "_mix")
        grads.update(mgrads)
        dz0, dz0b, dg0, db0 = ln_bwd(z0, dx1, lg[0], name=tag + "_ln0_bwd")
        dgate, dup, *df1 = ffn_bwd_weights(hb_in, dz0b, *f1, name=tag + "_ffn1_bwd_w")
        on_ffn_grads(i, 0, df1)
        dh = ffn_bwd_input(dgate, dup, f1[0], f1[1], dz0, name=tag + "_ffn1_bwd_x")
        per_layer[i] = {"ln_g": jnp.concatenate([dg0, dg1, dg2], 0), "ln_b": jnp.concatenate([db0, db1, db2], 0),
                        "ple_wg": dple_wg, "ple_bg": dple_bg[0], "ple_wp": dple_wp}
    for k in per_layer[0]:
        grads[k] = jnp.stack([per_layer[i][k] for i in range(DEPTH)])
    return loss, dh.reshape(nb, s, d), grads


WEIGHT_NAMES = ("ffn1_wg", "ffn1_wu", "ffn1_wd", "ffn2_wg", "ffn2_wu", "ffn2_wd", "ln_g", "ln_b", "ple_wg", "ple_bg",
                "ple_wp", "ab_w_in", "a_sinks", "b_conv_w", "b_conv_b", "b_wa", "b_ba", "b_wx", "b_bx", "b_lam",
                "ab_w_out", "c_w_in", "c_conv_w", "c_a_log", "c_dt_bias", "c_norm_g", "c_w_out")
NATIVE_NAMES = WEIGHT_NAMES[:6]
PACKED_NAMES = WEIGHT_NAMES[6:]
PACK_MATRICES = ("ple_wg", "ple_wp", "ab_w_in", "ab_w_out", "c_w_in", "c_w_out")
PACK_GROUPS = (tuple(k for k in PACKED_NAMES if k not in PACK_MATRICES), PACK_MATRICES)
PACK_TRANSIT = (f32, bf16)
SHARD_AXIS = {"ffn1_wg": 2, "ffn1_wu": 2, "ffn1_wd": 1, "ffn2_wg": 2, "ffn2_wu": 2, "ffn2_wd": 1, "ln_g": 2, "ln_b": 2,
              "ple_wg": 1, "ple_wp": 2, "ab_w_in": 2, "b_conv_w": 2, "ab_w_out": 1, "c_w_in": 2, "c_conv_w": 2,
              "c_w_out": 1}
N_CHIPS = 4
PACK_COLS = LANES
PACK_TILE_MULTIPLE = 256
ELEMENTWISE_BLOCK_ELEMS = 128 * 1024


def _row_tile(r, cols):
    return _tile(r, max(2 * SUBLANES, ELEMENTWISE_BLOCK_ELEMS // cols), 2 * SUBLANES)
MESH = pl.DeviceIdType.MESH
ANY = pl.BlockSpec(memory_space=pl.ANY)


def _tiled_dims(shape):
    w = shape[-1]
    r = 1
    for dim in shape[:-1]:
        r *= dim
    return r, w, -(-r // SUBLANES) * SUBLANES, -(-w // LANES) * LANES


def _pack(pieces, lead=()):
    k = len(lead)
    tiles = []
    for a in pieces:
        r, w, rp, wp = _tiled_dims(a.shape[k:])
        a2 = jnp.pad(a.reshape(lead + (r, w)), [(0, 0)] * k + [(0, rp - r), (0, wp - w)])
        a2 = a2.reshape(lead + (rp // SUBLANES, SUBLANES, wp // LANES, LANES))
        a2 = jnp.swapaxes(a2, k + 1, k + 2)
        tiles.append(a2.reshape(lead + (-1, SUBLANES, LANES)))
    flat = jnp.concatenate(tiles, axis=k)
    n = flat.shape[k]
    n_pad = -(-n // PACK_TILE_MULTIPLE) * PACK_TILE_MULTIPLE
    flat = jnp.pad(flat, [(0, 0)] * k + [(0, n_pad - n), (0, 0), (0, 0)])
    return flat.reshape(lead + (n_pad * SUBLANES, PACK_COLS))


def _unpack(pack, shapes, lead=()):
    k = len(lead)
    flat = pack.reshape(lead + (-1, SUBLANES, LANES))
    out, o = [], 0
    for shp in shapes:
        r, w, rp, wp = _tiled_dims(shp)
        n = (rp // SUBLANES) * (wp // LANES)
        a2 = lax.slice_in_dim(flat, o, o + n, axis=k).reshape(lead + (rp // SUBLANES, wp // LANES, SUBLANES, LANES))
        a2 = jnp.swapaxes(a2, k + 1, k + 2).reshape(lead + (rp, wp))
        a2 = lax.slice_in_dim(lax.slice_in_dim(a2, 0, r, axis=k), 0, w, axis=k + 1)
        out.append(a2.reshape(lead + tuple(shp)))
        o += n
    return out


def _mesh_position():
    x, y, c = lax.axis_index("x"), lax.axis_index("y"), lax.axis_index("c")
    chips = [(1 - x, y), (x, 1 - y), (1 - x, 1 - y)]
    return x, y, c, chips


def _remote(src, dst, send_sems, recv_sems, k, to):
    return pltpu.make_async_remote_copy(src_ref=src, dst_ref=dst, send_sem=send_sems.at[k], recv_sem=recv_sems.at[k],
                                        device_id=to, device_id_type=MESH)


def _sems(n):
    return pltpu.SemaphoreType.DMA((n,))


def place_slot(parts, slots, n_slots, dtype, from_slot, *, name):
    n = len(parts)
    r, cols = parts[0].shape[-2:]
    tr = _row_tile(r, cols)

    def body(src_ref, dst_ref, *refs):
        for a in range(n):
            refs[n + a][...] = refs[a][...].astype(dtype)

    dst = pl.BlockSpec((None, tr, cols), lambda i, src_ref, dst_ref: (dst_ref[0], i, 0))
    src = (pl.BlockSpec((None, tr, cols), lambda i, src_ref, dst_ref: (src_ref[0], i, 0)) if from_slot
           else pl.BlockSpec((tr, cols), lambda i, src_ref, dst_ref: (i, 0)))
    return pl.pallas_call(
        body,
        grid_spec=pltpu.PrefetchScalarGridSpec(num_scalar_prefetch=2, grid=(r // tr,), in_specs=[src] * n,
                                               out_specs=[dst] * n),
        out_shape=[jax.ShapeDtypeStruct((n_slots, r, cols), dtype)] * n,
        compiler_params=_params(("parallel",)), name=name,
    )(*slots, *parts)


def gather_shards(bufs, *, name):
    n = len(bufs)

    def body(*refs):
        out_refs = refs[n:2 * n]
        send_sems, recv_sems = refs[2 * n:]
        x, y, c, chips = _mesh_position()
        me = 2 * x + y
        sibling = (x, y, 1 - c)
        waits = []
        for j, (cx, cy) in enumerate(chips):
            for a in range(n):
                own = out_refs[a].at[me, c]
                cp = _remote(own, own, send_sems, recv_sems, 6 * a + j, (cx, cy, c))
                cp.start()
                waits.append(cp.wait_send)
        for j, (cx, cy) in enumerate(chips):
            for a in range(n):
                got = out_refs[a].at[2 * cx + cy, c]
                _remote(got, got, send_sems, recv_sems, 6 * a + j, (cx, cy, c)).wait_recv()
                fw = _remote(got, got, send_sems, recv_sems, 6 * a + 3 + j, sibling)
                fw.start()
                waits.append(fw.wait_send)
        for j, (cx, cy) in enumerate(chips):
            for a in range(n):
                got = out_refs[a].at[2 * cx + cy, 1 - c]
                _remote(got, got, send_sems, recv_sems, 6 * a + 3 + j, sibling).wait_recv()
        for wait in waits:
            wait()

    return pl.pallas_call(
        body, out_shape=[jax.ShapeDtypeStruct(b.shape, b.dtype) for b in bufs],
        in_specs=[ANY] * n, out_specs=[ANY] * n, scratch_shapes=[_sems(6 * n), _sems(6 * n)],
        input_output_aliases={a: a for a in range(n)}, name=name,
    )(*bufs)


def chip_exchange(ps, qs, *, name):
    n = len(ps)

    def body(*refs):
        p_refs, q_refs = refs[:n], refs[2 * n:3 * n]
        send_sems, recv_sems = refs[3 * n:]
        x, y, c, chips = _mesh_position()
        me = 2 * x + y
        waits = []
        for j, (cx, cy) in enumerate(chips):
            for a in range(n):
                cp = _remote(p_refs[a].at[2 * cx + cy], q_refs[a].at[me], send_sems, recv_sems, 3 * a + j, (cx, cy, c))
                cp.start()
                waits.append(cp.wait_send)
        for j, (cx, cy) in enumerate(chips):
            for a in range(n):
                got = q_refs[a].at[2 * cx + cy]
                _remote(got, got, send_sems, recv_sems, 3 * a + j, (cx, cy, c)).wait_recv()
        for wait in waits:
            wait()

    return pl.pallas_call(
        body, out_shape=[jax.ShapeDtypeStruct(q_.shape, q_.dtype) for q_ in qs], in_specs=[ANY] * (2 * n),
        out_specs=[ANY] * n, scratch_shapes=[_sems(3 * n), _sems(3 * n)],
        input_output_aliases={n + a: a for a in range(n)}, name=name,
    )(*ps, *qs)


def gather_slots_async(bufs, collective_id, *, name):
    n = len(bufs)
    refs = [jax.new_ref(b, memory_space=pltpu.MemorySpace.HBM) for b in bufs]

    @pl.kernel(mesh=plsc.ScalarSubcoreMesh(axis_name="sequencer", num_cores=1), name=name,
               scratch_types=(_sems(3 * n), _sems(3 * n)),
               compiler_params=pltpu.CompilerParams(collective_id=collective_id))
    def launch(send_sems, recv_sems):
        x, y, c, chips = _mesh_position()
        me = 2 * x + y
        barrier = pltpu.get_barrier_semaphore()
        for cx, cy in chips:
            pl.semaphore_signal(barrier, inc=1, device_id=(cx, cy, c), device_id_type=MESH)
        pl.semaphore_wait(barrier, len(chips))
        sends = []
        for j, (cx, cy) in enumerate(chips):
            for a in range(n):
                own = refs[a].at[me]
                cp = _remote(own, own, send_sems, recv_sems, 3 * a + j, (cx, cy, c))
                cp.start()
                sends.append(cp)
        for j, (cx, cy) in enumerate(chips):
            for a in range(n):
                got = refs[a].at[2 * cx + cy]
                _remote(got, got, send_sems, recv_sems, 3 * a + j, (cx, cy, c)).wait_recv()
        for cp in sends:
            cp.wait_send()

    launch()
    return [r[...] for r in refs]


N_DEVICES = 8
PEER_FLIPS = tuple((dx, dy, dc) for dx in (0, 1) for dy in (0, 1) for dc in (0, 1) if dx or dy or dc)


def exchange_partials_async(sends, collective_id, *, name):
    n = len(sends)
    k = len(PEER_FLIPS)

    def launch(*refs):
        s_refs, r_refs = refs[:n], refs[n:2 * n]
        send_sems, recv_sems, local_sems = refs[2 * n:]
        x, y, c, _ = _mesh_position()
        me = 4 * x + 2 * y + c
        peers = [(1 - x if dx else x, 1 - y if dy else y, 1 - c if dc else c) for dx, dy, dc in PEER_FLIPS]
        barrier = pltpu.get_barrier_semaphore()
        for peer in peers:
            pl.semaphore_signal(barrier, inc=1, device_id=peer, device_id_type=MESH)
        pl.semaphore_wait(barrier, len(peers))
        sends_started = []
        for a in range(n):
            own = pltpu.make_async_copy(s_refs[a].at[2 * x + y], r_refs[a].at[me], local_sems.at[a])
            own.start()
            sends_started.append(own)
        for j, (px, py, pc) in enumerate(peers):
            for a in range(n):
                cp = _remote(s_refs[a].at[2 * px + py], r_refs[a].at[me], send_sems, recv_sems, j, (px, py, pc))
                cp.start()
                sends_started.append(cp)
        for j, (px, py, pc) in enumerate(peers):
            for a in range(n):
                got = r_refs[a].at[4 * px + 2 * py + pc]
                _remote(got, got, send_sems, recv_sems, j, (px, py, pc)).wait_recv()
        for cp in sends_started[n:]:
            cp.wait_send()
        for cp in sends_started[:n]:
            cp.wait()

    return list(pl.kernel(
        launch, out_type=[jax.ShapeDtypeStruct((N_DEVICES,) + s_.shape[1:], s_.dtype) for s_ in sends],
        mesh=plsc.ScalarSubcoreMesh(axis_name="sequencer", num_cores=1), name=name,
        scratch_types=(_sems(k), _sems(k), _sems(n)),
        compiler_params=pltpu.CompilerParams(collective_id=collective_id))(*sends))


def sibling_exchange(gs, *, name):
    n = len(gs)

    def body(*refs):
        g_refs, out_refs = refs[:n], refs[n:2 * n]
        send_sems, recv_sems = refs[2 * n:]
        x, y, c, _ = _mesh_position()
        cps = [_remote(g_refs[a].at[:, 1 - c], out_refs[a], send_sems, recv_sems, a, (x, y, 1 - c)) for a in range(n)]
        for cp in cps:
            cp.start()
        for cp in cps:
            cp.wait()

    return pl.pallas_call(
        body, out_shape=[jax.ShapeDtypeStruct(g.shape[:1] + g.shape[2:], g.dtype) for g in gs],
        in_specs=[ANY] * n, out_specs=[ANY] * n, scratch_shapes=[_sems(n), _sems(n)], name=name,
    )(*gs)


def add_own_half(gs, others, c_idx, dtype, *, name):
    n = len(gs)
    ns, _, r, cols = gs[0].shape
    tr = _row_tile(r, cols)

    def body(c_ref, *refs):
        for a in range(n):
            refs[2 * n + a][...] = (refs[a][...] + refs[n + a][...]).astype(dtype)

    own = pl.BlockSpec((None, None, tr, cols), lambda s, i, c_ref: (s, c_ref[0], i, 0))
    oth = pl.BlockSpec((None, tr, cols), lambda s, i, c_ref: (s, i, 0))
    return pl.pallas_call(
        body,
        grid_spec=pltpu.PrefetchScalarGridSpec(num_scalar_prefetch=1, grid=(ns, r // tr),
                                               in_specs=[own] * n + [oth] * n, out_specs=[oth] * n),
        out_shape=[jax.ShapeDtypeStruct((ns, r, cols), dtype)] * n,
        compiler_params=_params(("parallel", "parallel")), name=name,
    )(c_idx, *gs, *others)


def sum_slots(qs, *, name):
    n = len(qs)
    ns, r, cols = qs[0].shape
    tr = _row_tile(r, cols * ns)

    def body(*refs):
        for a in range(n):
            q_ref = refs[a]
            acc = q_ref[0].astype(f32) + q_ref[1].astype(f32)
            for i in range(2, ns):
                acc = acc + q_ref[i].astype(f32)
            refs[n + a][...] = acc

    return pl.pallas_call(
        body, grid=(r // tr,), in_specs=[pl.BlockSpec((ns, tr, cols), lambda i: (0, i, 0))] * n,
        out_specs=[pl.BlockSpec((tr, cols), lambda i: (i, 0))] * n,
        out_shape=[jax.ShapeDtypeStruct((r, cols), f32)] * n,
        compiler_params=_params(("parallel",)), name=name,
    )(*qs)


def sibling_share(bufs, *, name):
    n = len(bufs)

    def body(*refs):
        out_refs = refs[n:2 * n]
        send_sems, recv_sems = refs[2 * n:]
        x, y, c, _ = _mesh_position()
        sibling = (x, y, 1 - c)
        cps = []
        for a in range(n):
            own = out_refs[a].at[c]
            cp = _remote(own, own, send_sems, recv_sems, a, sibling)
            cp.start()
            cps.append(cp)
        for a in range(n):
            theirs = out_refs[a].at[1 - c]
            _remote(theirs, theirs, send_sems, recv_sems, a, sibling).wait_recv()
        for cp in cps:
            cp.wait_send()

    return pl.pallas_call(
        body, out_shape=[jax.ShapeDtypeStruct(b.shape, b.dtype) for b in bufs], in_specs=[ANY] * n,
        out_specs=[ANY] * n, scratch_shapes=[_sems(n), _sems(n)],
        input_output_aliases={a: a for a in range(n)}, name=name,
    )(*bufs)


def _adamw_update(w, g, m, v):
    m2 = ADAM_B1 * m + (1.0 - ADAM_B1) * g
    v2 = ADAM_B2 * v + (1.0 - ADAM_B2) * (g * g)
    m_hat = m2 / (1.0 - ADAM_B1 ** ADAM_STEP)
    v_hat = v2 / (1.0 - ADAM_B2 ** ADAM_STEP)
    return -ADAM_LR * (m_hat / (jnp.sqrt(v_hat) + ADAM_EPS) + ADAM_WD * w), m2, v2


def adamw_from_partials(ws, ms, vs, slots, layer, acc, *, name):
    n = len(ws)
    nl, r, cols = ws[0].shape
    ns = slots[0].shape[0]
    tr = _row_tile(r, cols * 2)

    def body(*refs):
        for a in range(n):
            w_ref, m_ref, v_ref, s_ref = (refs[k * n + a] for k in range(4))
            g_ref, d_ref, m2_ref, v2_ref = (refs[len(refs) - 4 * n + k * n + a] for k in range(4))
            g = s_ref[0].astype(f32) + s_ref[1].astype(f32)
            for i in range(2, ns):
                g = g + s_ref[i].astype(f32)
            g_ref[...] = g
            d_ref[...], m2_ref[...], v2_ref[...] = _adamw_update(w_ref[...], g, m_ref[...], v_ref[...])

    lay = pl.BlockSpec((None, tr, cols), lambda i: (layer, i, 0))
    in_specs = [lay] * (3 * n) + [pl.BlockSpec((ns, tr, cols), lambda i: (0, i, 0))] * n
    args = [*ws, *ms, *vs, *slots]
    aliases = {}
    if acc is not None:
        in_specs += [ANY] * (4 * n)
        args += [a for lst in acc for a in lst]
        aliases = {4 * n + k: k for k in range(4 * n)}
    out = pl.pallas_call(
        body, grid=(r // tr,), in_specs=in_specs, out_specs=[lay] * (4 * n),
        out_shape=[jax.ShapeDtypeStruct((nl, r, cols), f32)] * (4 * n), input_output_aliases=aliases,
        compiler_params=_params(("parallel",)), name=name,
    )(*args)
    return [list(out[k * n:(k + 1) * n]) for k in range(4)]


def adamw(ws, gs, ms, vs, *, name):
    n = len(ws)
    r, cols = ws[0].shape
    tr = _row_tile(r, cols)

    def body(*refs):
        for a in range(n):
            w_ref, g_ref, m_ref, v_ref = (refs[k * n + a] for k in range(4))
            d_ref, m2_ref, v2_ref = (refs[(4 + k) * n + a] for k in range(3))
            d_ref[...], m2_ref[...], v2_ref[...] = _adamw_update(w_ref[...], g_ref[...], m_ref[...], v_ref[...])

    row = pl.BlockSpec((tr, cols), lambda i: (i, 0))
    out = pl.pallas_call(
        body, grid=(r // tr,), in_specs=[row] * (4 * n), out_specs=[row] * (3 * n),
        out_shape=[jax.ShapeDtypeStruct((r, cols), f32)] * (3 * n),
        compiler_params=_params(("parallel",)), name=name,
    )(*ws, *gs, *ms, *vs)
    return out[:n], out[n:2 * n], out[2 * n:]


def _full_weights(gathered, names, weights):
    pieces = _unpack(gathered, [weights[k].shape for k in names], lead=(N_CHIPS,))
    full = {}
    for name, pc in zip(names, pieces):
        ax = SHARD_AXIS.get(name)
        if ax is None:
            full[name] = weights[name]
        else:
            shp = weights[name].shape
            full[name] = jnp.moveaxis(pc, 0, ax).reshape(shp[:ax] + (N_CHIPS * shp[ax],) + shp[ax + 1:])
    return full


def _grad_pack(grads, names, shapes):
    pieces = []
    for name, shp in zip(names, shapes):
        g = grads[name]
        ax = SHARD_AXIS.get(name)
        if ax is None:
            pieces.append(jnp.broadcast_to(g.reshape(shp)[None], (N_CHIPS,) + tuple(shp)))
        else:
            pieces.append(jnp.stack(jnp.split(g, N_CHIPS, axis=ax)))
    return _pack(pieces, lead=(N_CHIPS,))


def _by_shape(arrays):
    groups = {}
    for i, a in enumerate(arrays):
        groups.setdefault(a.shape, []).append(i)
    return list(groups.values())


def _grouped(fn, lists, n_out, tag):
    outs = [[None] * len(lists[0]) for _ in range(n_out)]
    for gi, idx in enumerate(_by_shape(lists[0])):
        res = fn(*[[lst[i] for i in idx] for lst in lists], name=f"{tag}_{gi}")
        res = res if n_out > 1 else (res,)
        for k in range(n_out):
            for i, r in zip(idx, res[k]):
                outs[k][i] = r
    return outs if n_out > 1 else outs[0]


def _train_step(x, p, loss_target, weights, m, v):
    shapes = [[weights[k].shape for k in names] for names in PACK_GROUPS]
    halves = lambda a: a.reshape((2, a.shape[0] // 2) + a.shape[1:])
    packs = lambda d_: [halves(_pack([d_[k] for k in names])) for names in PACK_GROUPS]
    nn_ = len(NATIVE_NAMES)
    local = [weights[k] for k in NATIVE_NAMES] + packs(weights)
    local_m = [m[k] for k in NATIVE_NAMES] + packs(m)
    local_v = [v[k] for k in NATIVE_NAMES] + packs(v)
    flat = lambda lst: [a.reshape((-1, a.shape[-1])) for a in lst]
    c_idx = lax.axis_index("c").astype(jnp.int32).reshape(1)
    chip_idx = (2 * lax.axis_index("x") + lax.axis_index("y")).astype(jnp.int32).reshape(1)
    c2 = (c_idx, c_idx)
    chip2 = (chip_idx, chip_idx)

    def placed(arrays, slot, n_slots, dtype, from_slot, tag):
        return _grouped(lambda a, name: place_slot(a, slot, n_slots, dtype, from_slot, name=name), [arrays], 1, tag)

    ffn_own = [weights[k][i] for i in range(DEPTH) for k in NATIVE_NAMES]
    ffn_bufs = placed(ffn_own, chip2, N_CHIPS, bf16, False, "place_ffn_weights")
    group = len(NATIVE_NAMES) // 2
    n_ffn_groups = len(ffn_bufs) // group
    ffn_gathered = []
    for gi in range(n_ffn_groups):
        ffn_gathered += gather_slots_async(ffn_bufs[gi * group:(gi + 1) * group], collective_id=1 + gi,
                                           name=f"comm_gather_ffn_{gi}")
    ffn_weights = {k: [ffn_gathered[i * len(NATIVE_NAMES) + j] for i in range(DEPTH)] for j, k in enumerate(NATIVE_NAMES)}
    pack_bufs = [placed(flat([a]), chip2, N_CHIPS, dt, False, f"place_packed_weights_{gi}")[0].reshape((N_CHIPS,) + a.shape)
                 for gi, (a, dt) in enumerate(zip(local[nn_:], PACK_TRANSIT))]
    full = {}
    for names, gathered in zip(PACK_GROUPS, gather_shards(pack_bufs, name="comm_gather_weights")):
        full.update(_full_weights(gathered, names, weights))
    first_grad_id = n_ffn_groups + 1
    in_flight = {}

    def on_ffn_grads(layer, first, partials):
        tag = f"ffn_grads_l{layer}_{first}"
        got = exchange_partials_async(partials, collective_id=first_grad_id + len(in_flight), name="comm_" + tag)
        in_flight[(layer, first)] = got

    loss, grad_x, grads = _local_step(x, p, loss_target, full, ffn_weights, on_ffn_grads)
    gs = [_grad_pack(grads, names, shp).reshape((N_CHIPS,) + a.shape)
          for names, shp, a in zip(PACK_GROUPS, shapes, local[nn_:])]
    others = sibling_exchange(gs, name="comm_grad_sibling")
    chip_sums = [add_own_half([g], [o], c_idx, dt, name=f"grad_add_sibling_{gi}")[0]
                 for gi, (g, o, dt) in enumerate(zip(gs, others, PACK_TRANSIT))]
    own = [placed([cs], chip2, N_CHIPS, dt, True, f"place_own_partial_{gi}")[0]
           for gi, (cs, dt) in enumerate(zip(chip_sums, PACK_TRANSIT))]
    slots = chip_exchange(chip_sums, own, name="comm_grad_chips")
    mine = _grouped(sum_slots, [list(slots)], 1, "grad_sum_chips")
    pack_sum = sibling_share(placed(mine, c2, 2, f32, False, "place_own_half"), name="comm_grad_share")
    ffn_out = [{} for _ in range(4)]
    for (layer, first), got in in_flight.items():
        names = NATIVE_NAMES[first:first + len(got)]
        for idx in _by_shape([weights[k] for k in names]):
            ks = [names[i] for i in idx]
            acc = [[out[k] for k in ks] for out in ffn_out] if ks[0] in ffn_out[0] else None
            res = adamw_from_partials([weights[k] for k in ks], [m[k] for k in ks], [v[k] for k in ks],
                                      [got[i] for i in idx], layer, acc, name=f"adamw_ffn_l{layer}_{first + idx[0]}")
            for out, arrays in zip(ffn_out, res):
                out.update(zip(ks, arrays))
    pack_out = [list(pack_sum)] + _grouped(adamw, [flat(local[nn_:]), flat(pack_sum), flat(local_m[nn_:]),
                                                    flat(local_v[nn_:])], 3, "adamw_packed")
    loss = lax.psum(loss, ("x", "y", "c"))
    outs = []
    for by_name, packs_ in zip(ffn_out, pack_out):
        by_name = dict(by_name)
        for names, shp, pk in zip(PACK_GROUPS, shapes, packs_):
            by_name.update(zip(names, _unpack(pk, shp)))
        outs += [by_name[k] for k in WEIGHT_NAMES]
    return (loss, grad_x, *outs)


def kernel(x, p, ffn1_wg, ffn1_wu, ffn1_wd, ffn2_wg, ffn2_wu, ffn2_wd, ln_g, ln_b, ple_wg, ple_bg, ple_wp, ab_w_in, a_sinks, b_conv_w, b_conv_b, b_wa, b_ba, b_wx, b_bx, b_lam, ab_w_out, c_w_in, c_conv_w, c_a_log, c_dt_bias, c_norm_g, c_w_out, loss_target, m_ffn1_wg, m_ffn1_wu, m_ffn1_wd, m_ffn2_wg, m_ffn2_wu, m_ffn2_wd, m_ln_g, m_ln_b, m_ple_wg, m_ple_bg, m_ple_wp, m_ab_w_in, m_a_sinks, m_b_conv_w, m_b_conv_b, m_b_wa, m_b_ba, m_b_wx, m_b_bx, m_b_lam, m_ab_w_out, m_c_w_in, m_c_conv_w, m_c_a_log, m_c_dt_bias, m_c_norm_g, m_c_w_out, v_ffn1_wg, v_ffn1_wu, v_ffn1_wd, v_ffn2_wg, v_ffn2_wu, v_ffn2_wd, v_ln_g, v_ln_b, v_ple_wg, v_ple_bg, v_ple_wp, v_ab_w_in, v_a_sinks, v_b_conv_w, v_b_conv_b, v_b_wa, v_b_ba, v_b_wx, v_b_bx, v_b_lam, v_ab_w_out, v_c_w_in, v_c_conv_w, v_c_a_log, v_c_dt_bias, v_c_norm_g, v_c_w_out):
    weights = [ffn1_wg, ffn1_wu, ffn1_wd, ffn2_wg, ffn2_wu, ffn2_wd, ln_g, ln_b, ple_wg, ple_bg, ple_wp, ab_w_in, a_sinks,
               b_conv_w, b_conv_b, b_wa, b_ba, b_wx, b_bx, b_lam, ab_w_out, c_w_in, c_conv_w, c_a_log, c_dt_bias, c_norm_g,
               c_w_out]
    m = [m_ffn1_wg, m_ffn1_wu, m_ffn1_wd, m_ffn2_wg, m_ffn2_wu, m_ffn2_wd, m_ln_g, m_ln_b, m_ple_wg, m_ple_bg, m_ple_wp,
         m_ab_w_in, m_a_sinks, m_b_conv_w, m_b_conv_b, m_b_wa, m_b_ba, m_b_wx, m_b_bx, m_b_lam, m_ab_w_out, m_c_w_in,
         m_c_conv_w, m_c_a_log, m_c_dt_bias, m_c_norm_g, m_c_w_out]
    v = [v_ffn1_wg, v_ffn1_wu, v_ffn1_wd, v_ffn2_wg, v_ffn2_wu, v_ffn2_wd, v_ln_g, v_ln_b, v_ple_wg, v_ple_bg, v_ple_wp,
         v_ab_w_in, v_a_sinks, v_b_conv_w, v_b_conv_b, v_b_wa, v_b_ba, v_b_wx, v_b_bx, v_b_lam, v_ab_w_out, v_c_w_in,
         v_c_conv_w, v_c_a_log, v_c_dt_bias, v_c_norm_g, v_c_w_out]
    return _train_step(x, p, loss_target, dict(zip(WEIGHT_NAMES, weights)), dict(zip(WEIGHT_NAMES, m)),
                       dict(zip(WEIGHT_NAMES, v)))
```
